```python
import math
import jax, jax.numpy as jnp
from jax import lax
import numpy as np

D_MODEL = 1024
BATCH = 8
SEQ = 2048
DEPTH = 4

CONV_WIDTH = D_MODEL // 2
CONV_HEAD_DIM = 64
N_CONV_HEADS = CONV_WIDTH // CONV_HEAD_DIM
CONV_K = 3
SSM_WIDTH = D_MODEL - CONV_WIDTH
SSM_GROUP = 16
SSM_GROUPS = SSM_WIDTH // SSM_GROUP
SSM_STATE = 64
MIX_WIDTH = CONV_WIDTH + SSM_WIDTH
IN_COLS = 3 * CONV_WIDTH + SSM_WIDTH
D_FF = ((8 * D_MODEL // 3 + 127) // 128) * 128
PLE_DIM = 256
EPS = 1e-6
DT_MIN = 1e-3
DT_MAX = 1e-1

kernel_name = "hymba_style_conv_s5_macaron_trunk"


def rmsnorm(x, g):
    xf = x.astype(jnp.float32)
    xf = xf * lax.rsqrt(jnp.mean(xf * xf, axis=-1, keepdims=True) + EPS)
    return (xf * g.astype(jnp.float32)).astype(x.dtype)


def swiglu(u, w_gate, w_up, w_down):
    return (jax.nn.silu(u @ w_gate) * (u @ w_up)) @ w_down


def short_conv_group(z_b, z_c, z_v, conv_w, conv_b):
    v = z_c * z_v
    rhs = conv_w.astype(v.dtype)[:, None, :]
    y = lax.conv_general_dilated(
        v, rhs, window_strides=(1,), padding=[(CONV_K - 1, 0)],
        dimension_numbers=("NWC", "WIO", "NWC"), feature_group_count=CONV_WIDTH)
    return z_b * (y + conv_b.astype(v.dtype))


def s5_group(u_s, A_re, A_im, B_re, B_im, C_re, C_im, D, log_dt, glu_w, glu_b):
    f32 = jnp.float32
    bsz, L, _ = u_s.shape
    u = u_s.astype(f32).reshape(bsz, L, SSM_GROUPS, SSM_GROUP)
    ar, ai = A_re.astype(f32), A_im.astype(f32)
    dt = jnp.exp(log_dt.astype(f32))[:, None]
    mag = jnp.exp(ar * dt)
    ph = ai * dt
    lb_re, lb_im = mag * jnp.cos(ph), mag * jnp.sin(ph)
    nr, ni = lb_re - 1.0, lb_im
    den = ar * ar + ai * ai
    f_re = (nr * ar + ni * ai) / den
    f_im = (ni * ar - nr * ai) / den
    br, bi = B_re.astype(f32), B_im.astype(f32)
    bb_re = f_re[..., None] * br - f_im[..., None] * bi
    bb_im = f_re[..., None] * bi + f_im[..., None] * br
    bu_re = jnp.einsum("blgh,gph->blgp", u, bb_re)
    bu_im = jnp.einsum("blgh,gph->blgp", u, bb_im)
    a_re = jnp.broadcast_to(lb_re[None, None], (1, L, SSM_GROUPS, SSM_STATE))
    a_im = jnp.broadcast_to(lb_im[None, None], (1, L, SSM_GROUPS, SSM_STATE))

    def combine(e1, e2):
        a1r, a1i, b1r, b1i = e1
        a2r, a2i, b2r, b2i = e2
        return (a1r * a2r - a1i * a2i,
                a1r * a2i + a1i * a2r,
                a2r * b1r - a2i * b1i + b2r,
                a2r * b1i + a2i * b1r + b2i)

    _, _, h_re, h_im = lax.associative_scan(combine, (a_re, a_im, bu_re, bu_im), axis=1)
    y = (jnp.einsum("blgp,ghp->blgh", h_re, C_re.astype(f32))
         - jnp.einsum("blgp,ghp->blgh", h_im, C_im.astype(f32))
         + D.astype(f32) * u)
    y = y.reshape(bsz, L, SSM_WIDTH)
    zg = jax.nn.gelu(y)
    out = zg * jax.nn.sigmoid(zg @ glu_w.astype(f32) + glu_b.astype(f32))
    return out.astype(u_s.dtype)


def _fwd_setup_inputs(seed: int = 0) -> dict:
    key = jax.random.key(seed)
    ks = iter(jax.random.split(key, 40))
    nrm = lambda shape, s: jax.random.normal(next(ks), shape, jnp.float32) * s
    gain = lambda shape: 1.0 + nrm(shape, 0.02)
    n = jnp.arange(SSM_STATE, dtype=jnp.float32)
    log_dt = jax.random.uniform(next(ks), (DEPTH, SSM_GROUPS), jnp.float32,
                                math.log(DT_MIN), math.log(DT_MAX))
    return {
        "x": nrm((BATCH, SEQ, D_MODEL), 1.0),
        "p": nrm((DEPTH, BATCH, SEQ, PLE_DIM), 1.0),
        "ffn1_norm": gain((DEPTH, D_MODEL)),
        "ffn1_w_gate": nrm((DEPTH, D_MODEL, D_FF), D_MODEL ** -0.5),
        "ffn1_w_up": nrm((DEPTH, D_MODEL, D_FF), D_MODEL ** -0.5),
        "ffn1_w_down": nrm((DEPTH, D_FF, D_MODEL), D_FF ** -0.5),
        "mix_norm": gain((DEPTH, D_MODEL)),
        "w_in": nrm((DEPTH, D_MODEL, IN_COLS), D_MODEL ** -0.5),
        "conv_w": nrm((DEPTH, CONV_K, CONV_WIDTH), CONV_K ** -0.5),
        "conv_b": nrm((DEPTH, CONV_WIDTH), 0.02),
        "ssm_A_re": -0.5 + nrm((DEPTH, SSM_GROUPS, SSM_STATE), 0.01),
        "ssm_A_im": math.pi * n + nrm((DEPTH, SSM_GROUPS, SSM_STATE), 0.01),
        "ssm_B_re": nrm((DEPTH, SSM_GROUPS, SSM_STATE, SSM_GROUP), (2 * SSM_GROUP) ** -0.5),
        "ssm_B_im": nrm((DEPTH, SSM_GROUPS, SSM_STATE, SSM_GROUP), (2 * SSM_GROUP) ** -0.5),
        "ssm_C_re": nrm((DEPTH, SSM_GROUPS, SSM_GROUP, SSM_STATE), (2 * SSM_STATE) ** -0.5),
        "ssm_C_im": nrm((DEPTH, SSM_GROUPS, SSM_GROUP, SSM_STATE), (2 * SSM_STATE) ** -0.5),
        "ssm_D": nrm((DEPTH, SSM_GROUPS, SSM_GROUP), 1.0),
        "ssm_log_dt": log_dt,
        "glu_w": nrm((DEPTH, SSM_WIDTH, SSM_WIDTH), SSM_WIDTH ** -0.5),
        "glu_b": nrm((DEPTH, SSM_WIDTH), 0.02),
        "conv_out_norm": gain((DEPTH, CONV_WIDTH)),
        "ssm_out_norm": gain((DEPTH, SSM_WIDTH)),
        "w_out": nrm((DEPTH, MIX_WIDTH, D_MODEL), MIX_WIDTH ** -0.5),
        "ffn2_norm": gain((DEPTH, D_MODEL)),
        "ffn2_w_gate": nrm((DEPTH, D_MODEL, D_FF), D_MODEL ** -0.5),
        "ffn2_w_up": nrm((DEPTH, D_MODEL, D_FF), D_MODEL ** -0.5),
        "ffn2_w_down": nrm((DEPTH, D_FF, D_MODEL), D_FF ** -0.5),
        "ple_norm": gain((DEPTH, D_MODEL)),
        "ple_w_gate": nrm((DEPTH, D_MODEL, D_MODEL), D_MODEL ** -0.5),
        "ple_w_proj": nrm((DEPTH, PLE_DIM, D_MODEL), PLE_DIM ** -0.5),
        "final_norm": gain((D_MODEL,)),
    }


def _fwd_reference(x, p, ffn1_norm, ffn1_w_gate, ffn1_w_up, ffn1_w_down, mix_norm, w_in,
              conv_w, conv_b, ssm_A_re, ssm_A_im, ssm_B_re, ssm_B_im, ssm_C_re, ssm_C_im,
              ssm_D, ssm_log_dt, glu_w, glu_b, conv_out_norm, ssm_out_norm, w_out,
              ffn2_norm, ffn2_w_gate, ffn2_w_up, ffn2_w_down, ple_norm, ple_w_gate,
              ple_w_proj, final_norm):
    h = x
    for i in range(DEPTH):
        h = h + 0.5 * swiglu(rmsnorm(h, ffn1_norm[i]), ffn1_w_gate[i], ffn1_w_up[i], ffn1_w_down[i])
        z = rmsnorm(h, mix_norm[i]) @ w_in[i]
        z_b = z[..., :CONV_WIDTH]
        z_c = z[..., CONV_WIDTH:2 * CONV_WIDTH]
        z_v = z[..., 2 * CONV_WIDTH:3 * CONV_WIDTH]
        z_s = z[..., 3 * CONV_WIDTH:]
        y_a = short_conv_group(z_b, z_c, z_v, conv_w[i], conv_b[i])
        y_s = s5_group(z_s, ssm_A_re[i], ssm_A_im[i], ssm_B_re[i], ssm_B_im[i],
                       ssm_C_re[i], ssm_C_im[i], ssm_D[i], ssm_log_dt[i], glu_w[i], glu_b[i])
        y = jnp.concatenate([rmsnorm(y_a, conv_out_norm[i]), rmsnorm(y_s, ssm_out_norm[i])], axis=-1)
        h = h + y @ w_out[i]
        h = h + 0.5 * swiglu(rmsnorm(h, ffn2_norm[i]), ffn2_w_gate[i], ffn2_w_up[i], ffn2_w_down[i])
        gate = jax.nn.sigmoid(rmsnorm(h, ple_norm[i]) @ ple_w_gate[i])
        h = h + (p[i] @ ple_w_proj[i]) * gate
    return rmsnorm(h, final_norm)


import jax as _jax
import jax.numpy as _jnp

TWIN_FORMAT = 'train_step'
FWD_PARAMS = ['x', 'p', 'ffn1_norm', 'ffn1_w_gate', 'ffn1_w_up', 'ffn1_w_down', 'mix_norm', 'w_in', 'conv_w', 'conv_b', 'ssm_A_re', 'ssm_A_im', 'ssm_B_re', 'ssm_B_im', 'ssm_C_re', 'ssm_C_im', 'ssm_D', 'ssm_log_dt', 'glu_w', 'glu_b', 'conv_out_norm', 'ssm_out_norm', 'w_out', 'ffn2_norm', 'ffn2_w_gate', 'ffn2_w_up', 'ffn2_w_down', 'ple_norm', 'ple_w_gate', 'ple_w_proj', 'final_norm']
TWIN_WEIGHTS = ['ffn1_norm', 'ffn1_w_gate', 'ffn1_w_up', 'ffn1_w_down', 'mix_norm', 'w_in', 'conv_w', 'conv_b', 'ssm_A_re', 'ssm_A_im', 'ssm_B_re', 'ssm_B_im', 'ssm_C_re', 'ssm_C_im', 'ssm_D', 'ssm_log_dt', 'glu_w', 'glu_b', 'conv_out_norm', 'ssm_out_norm', 'w_out', 'ffn2_norm', 'ffn2_w_gate', 'ffn2_w_up', 'ffn2_w_down', 'ple_norm', 'ple_w_gate', 'ple_w_proj', 'final_norm']
TWIN_DIFF_INPUT = 'x'
TWIN_INPUTS = ['x', 'p', 'ffn1_norm', 'ffn1_w_gate', 'ffn1_w_up', 'ffn1_w_down', 'mix_norm', 'w_in', 'conv_w', 'conv_b', 'ssm_A_re', 'ssm_A_im', 'ssm_B_re', 'ssm_B_im', 'ssm_C_re', 'ssm_C_im', 'ssm_D', 'ssm_log_dt', 'glu_w', 'glu_b', 'conv_out_norm', 'ssm_out_norm', 'w_out', 'ffn2_norm', 'ffn2_w_gate', 'ffn2_w_up', 'ffn2_w_down', 'ple_norm', 'ple_w_gate', 'ple_w_proj', 'final_norm', 'loss_target', 'm_ffn1_norm', 'm_ffn1_w_gate', 'm_ffn1_w_up', 'm_ffn1_w_down', 'm_mix_norm', 'm_w_in', 'm_conv_w', 'm_conv_b', 'm_ssm_A_re', 'm_ssm_A_im', 'm_ssm_B_re', 'm_ssm_B_im', 'm_ssm_C_re', 'm_ssm_C_im', 'm_ssm_D', 'm_ssm_log_dt', 'm_glu_w', 'm_glu_b', 'm_conv_out_norm', 'm_ssm_out_norm', 'm_w_out', 'm_ffn2_norm', 'm_ffn2_w_gate', 'm_ffn2_w_up', 'm_ffn2_w_down', 'm_ple_norm', 'm_ple_w_gate', 'm_ple_w_proj', 'm_final_norm', 'v_ffn1_norm', 'v_ffn1_w_gate', 'v_ffn1_w_up', 'v_ffn1_w_down', 'v_mix_norm', 'v_w_in', 'v_conv_w', 'v_conv_b', 'v_ssm_A_re', 'v_ssm_A_im', 'v_ssm_B_re', 'v_ssm_B_im', 'v_ssm_C_re', 'v_ssm_C_im', 'v_ssm_D', 'v_ssm_log_dt', 'v_glu_w', 'v_glu_b', 'v_conv_out_norm', 'v_ssm_out_norm', 'v_w_out', 'v_ffn2_norm', 'v_ffn2_w_gate', 'v_ffn2_w_up', 'v_ffn2_w_down', 'v_ple_norm', 'v_ple_w_gate', 'v_ple_w_proj', 'v_final_norm']
TWIN_OUTPUTS = ['loss', 'grad_x', 'grad_ffn1_norm', 'grad_ffn1_w_gate', 'grad_ffn1_w_up', 'grad_ffn1_w_down', 'grad_mix_norm', 'grad_w_in', 'grad_conv_w', 'grad_conv_b', 'grad_ssm_A_re', 'grad_ssm_A_im', 'grad_ssm_B_re', 'grad_ssm_B_im', 'grad_ssm_C_re', 'grad_ssm_C_im', 'grad_ssm_D', 'grad_ssm_log_dt', 'grad_glu_w', 'grad_glu_b', 'grad_conv_out_norm', 'grad_ssm_out_norm', 'grad_w_out', 'grad_ffn2_norm', 'grad_ffn2_w_gate', 'grad_ffn2_w_up', 'grad_ffn2_w_down', 'grad_ple_norm', 'grad_ple_w_gate', 'grad_ple_w_proj', 'grad_final_norm', 'delta_ffn1_norm', 'delta_ffn1_w_gate', 'delta_ffn1_w_up', 'delta_ffn1_w_down', 'delta_mix_norm', 'delta_w_in', 'delta_conv_w', 'delta_conv_b', 'delta_ssm_A_re', 'delta_ssm_A_im', 'delta_ssm_B_re', 'delta_ssm_B_im', 'delta_ssm_C_re', 'delta_ssm_C_im', 'delta_ssm_D', 'delta_ssm_log_dt', 'delta_glu_w', 'delta_glu_b', 'delta_conv_out_norm', 'delta_ssm_out_norm', 'delta_w_out', 'delta_ffn2_norm', 'delta_ffn2_w_gate', 'delta_ffn2_w_up', 'delta_ffn2_w_down', 'delta_ple_norm', 'delta_ple_w_gate', 'delta_ple_w_proj', 'delta_final_norm', 'new_m_ffn1_norm', 'new_m_ffn1_w_gate', 'new_m_ffn1_w_up', 'new_m_ffn1_w_down', 'new_m_mix_norm', 'new_m_w_in', 'new_m_conv_w', 'new_m_conv_b', 'new_m_ssm_A_re', 'new_m_ssm_A_im', 'new_m_ssm_B_re', 'new_m_ssm_B_im', 'new_m_ssm_C_re', 'new_m_ssm_C_im', 'new_m_ssm_D', 'new_m_ssm_log_dt', 'new_m_glu_w', 'new_m_glu_b', 'new_m_conv_out_norm', 'new_m_ssm_out_norm', 'new_m_w_out', 'new_m_ffn2_norm', 'new_m_ffn2_w_gate', 'new_m_ffn2_w_up', 'new_m_ffn2_w_down', 'new_m_ple_norm', 'new_m_ple_w_gate', 'new_m_ple_w_proj', 'new_m_final_norm', 'new_v_ffn1_norm', 'new_v_ffn1_w_gate', 'new_v_ffn1_w_up', 'new_v_ffn1_w_down', 'new_v_mix_norm', 'new_v_w_in', 'new_v_conv_w', 'new_v_conv_b', 'new_v_ssm_A_re', 'new_v_ssm_A_im', 'new_v_ssm_B_re', 'new_v_ssm_B_im', 'new_v_ssm_C_re', 'new_v_ssm_C_im', 'new_v_ssm_D', 'new_v_ssm_log_dt', 'new_v_glu_w', 'new_v_glu_b', 'new_v_conv_out_norm', 'new_v_ssm_out_norm', 'new_v_w_out', 'new_v_ffn2_norm', 'new_v_ffn2_w_gate', 'new_v_ffn2_w_up', 'new_v_ffn2_w_down', 'new_v_ple_norm', 'new_v_ple_w_gate', 'new_v_ple_w_proj', 'new_v_final_norm']
TWIN_LEAF_KINDS = {'loss': 'loss', 'grad_x': 'grad_x', 'grad_ffn1_norm': 'grad_w', 'grad_ffn1_w_gate': 'grad_w', 'grad_ffn1_w_up': 'grad_w', 'grad_ffn1_w_down': 'grad_w', 'grad_mix_norm': 'grad_w', 'grad_w_in': 'grad_w', 'grad_conv_w': 'grad_w', 'grad_conv_b': 'grad_w', 'grad_ssm_A_re': 'grad_w', 'grad_ssm_A_im': 'grad_w', 'grad_ssm_B_re': 'grad_w', 'grad_ssm_B_im': 'grad_w', 'grad_ssm_C_re': 'grad_w', 'grad_ssm_C_im': 'grad_w', 'grad_ssm_D': 'grad_w', 'grad_ssm_log_dt': 'grad_w', 'grad_glu_w': 'grad_w', 'grad_glu_b': 'grad_w', 'grad_conv_out_norm': 'grad_w', 'grad_ssm_out_norm': 'grad_w', 'grad_w_out': 'grad_w', 'grad_ffn2_norm': 'grad_w', 'grad_ffn2_w_gate': 'grad_w', 'grad_ffn2_w_up': 'grad_w', 'grad_ffn2_w_down': 'grad_w', 'grad_ple_norm': 'grad_w', 'grad_ple_w_gate': 'grad_w', 'grad_ple_w_proj': 'grad_w', 'grad_final_norm': 'grad_w', 'delta_ffn1_norm': 'delta_w', 'delta_ffn1_w_gate': 'delta_w', 'delta_ffn1_w_up': 'delta_w', 'delta_ffn1_w_down': 'delta_w', 'delta_mix_norm': 'delta_w', 'delta_w_in': 'delta_w', 'delta_conv_w': 'delta_w', 'delta_conv_b': 'delta_w', 'delta_ssm_A_re': 'delta_w', 'delta_ssm_A_im': 'delta_w', 'delta_ssm_B_re': 'delta_w', 'delta_ssm_B_im': 'delta_w', 'delta_ssm_C_re': 'delta_w', 'delta_ssm_C_im': 'delta_w', 'delta_ssm_D': 'delta_w', 'delta_ssm_log_dt': 'delta_w', 'delta_glu_w': 'delta_w', 'delta_glu_b': 'delta_w', 'delta_conv_out_norm': 'delta_w', 'delta_ssm_out_norm': 'delta_w', 'delta_w_out': 'delta_w', 'delta_ffn2_norm': 'delta_w', 'delta_ffn2_w_gate': 'delta_w', 'delta_ffn2_w_up': 'delta_w', 'delta_ffn2_w_down': 'delta_w', 'delta_ple_norm': 'delta_w', 'delta_ple_w_gate': 'delta_w', 'delta_ple_w_proj': 'delta_w', 'delta_final_norm': 'delta_w', 'new_m_ffn1_norm': 'new_m', 'new_m_ffn1_w_gate': 'new_m', 'new_m_ffn1_w_up': 'new_m', 'new_m_ffn1_w_down': 'new_m', 'new_m_mix_norm': 'new_m', 'new_m_w_in': 'new_m', 'new_m_conv_w': 'new_m', 'new_m_conv_b': 'new_m', 'new_m_ssm_A_re': 'new_m', 'new_m_ssm_A_im': 'new_m', 'new_m_ssm_B_re': 'new_m', 'new_m_ssm_B_im': 'new_m', 'new_m_ssm_C_re': 'new_m', 'new_m_ssm_C_im': 'new_m', 'new_m_ssm_D': 'new_m', 'new_m_ssm_log_dt': 'new_m', 'new_m_glu_w': 'new_m', 'new_m_glu_b': 'new_m', 'new_m_conv_out_norm': 'new_m', 'new_m_ssm_out_norm': 'new_m', 'new_m_w_out': 'new_m', 'new_m_ffn2_norm': 'new_m', 'new_m_ffn2_w_gate': 'new_m', 'new_m_ffn2_w_up': 'new_m', 'new_m_ffn2_w_down': 'new_m', 'new_m_ple_norm': 'new_m', 'new_m_ple_w_gate': 'new_m', 'new_m_ple_w_proj': 'new_m', 'new_m_final_norm': 'new_m', 'new_v_ffn1_norm': 'new_v', 'new_v_ffn1_w_gate': 'new_v', 'new_v_ffn1_w_up': 'new_v', 'new_v_ffn1_w_down': 'new_v', 'new_v_mix_norm': 'new_v', 'new_v_w_in': 'new_v', 'new_v_conv_w': 'new_v', 'new_v_conv_b': 'new_v', 'new_v_ssm_A_re': 'new_v', 'new_v_ssm_A_im': 'new_v', 'new_v_ssm_B_re': 'new_v', 'new_v_ssm_B_im': 'new_v', 'new_v_ssm_C_re': 'new_v', 'new_v_ssm_C_im': 'new_v', 'new_v_ssm_D': 'new_v', 'new_v_ssm_log_dt': 'new_v', 'new_v_glu_w': 'new_v', 'new_v_glu_b': 'new_v', 'new_v_conv_out_norm': 'new_v', 'new_v_ssm_out_norm': 'new_v', 'new_v_w_out': 'new_v', 'new_v_ffn2_norm': 'new_v', 'new_v_ffn2_w_gate': 'new_v', 'new_v_ffn2_w_up': 'new_v', 'new_v_ffn2_w_down': 'new_v', 'new_v_ple_norm': 'new_v', 'new_v_ple_w_gate': 'new_v', 'new_v_ple_w_proj': 'new_v', 'new_v_final_norm': 'new_v'}


def _forward(args):
    return _fwd_reference(*[args[k] for k in FWD_PARAMS])


def _output_shape():
    out = _jax.eval_shape(lambda: _forward(_fwd_setup_inputs(0)))
    return out.shape, out.dtype

N_MICROBATCH = 1
ADAM_LR = 0.001
ADAM_B1 = 0.9
ADAM_B2 = 0.999
ADAM_EPS = 1e-08
ADAM_WD = 0.01
ADAM_STEP = 10
PER_EXAMPLE_BATCH_AXIS = {'x': 0, 'p': 1, 'loss_target': 0}
SHARED_INPUTS = []
_WEIGHT_DTYPES = {'ffn1_norm': _jnp.float32, 'ffn1_w_gate': _jnp.float32, 'ffn1_w_up': _jnp.float32, 'ffn1_w_down': _jnp.float32, 'mix_norm': _jnp.float32, 'w_in': _jnp.float32, 'conv_w': _jnp.float32, 'conv_b': _jnp.float32, 'ssm_A_re': _jnp.float32, 'ssm_A_im': _jnp.float32, 'ssm_B_re': _jnp.float32, 'ssm_B_im': _jnp.float32, 'ssm_C_re': _jnp.float32, 'ssm_C_im': _jnp.float32, 'ssm_D': _jnp.float32, 'ssm_log_dt': _jnp.float32, 'glu_w': _jnp.float32, 'glu_b': _jnp.float32, 'conv_out_norm': _jnp.float32, 'ssm_out_norm': _jnp.float32, 'w_out': _jnp.float32, 'ffn2_norm': _jnp.float32, 'ffn2_w_gate': _jnp.float32, 'ffn2_w_up': _jnp.float32, 'ffn2_w_down': _jnp.float32, 'ple_norm': _jnp.float32, 'ple_w_gate': _jnp.float32, 'ple_w_proj': _jnp.float32, 'final_norm': _jnp.float32}
MOMENT_SCALE = {'ffn1_norm': 5.341553e-02, 'ffn1_w_gate': 2.233660e-02, 'ffn1_w_up': 2.164371e-02, 'ffn1_w_down': 3.594695e-02, 'mix_norm': 1.156391e-01, 'w_in': 8.138173e-02, 'conv_w': 8.292888e-02, 'conv_b': 8.010270e-02, 'ssm_A_re': 4.132154e-03, 'ssm_A_im': 4.977589e-03, 'ssm_B_re': 3.037880e-03, 'ssm_B_im': 2.979699e-03, 'ssm_C_re': 6.009949e-03, 'ssm_C_im': 5.796994e-03, 'ssm_D': 9.036333e-02, 'ssm_log_dt': 3.240608e+00, 'glu_w': 2.254702e-02, 'glu_b': 3.461528e-02, 'conv_out_norm': 8.175854e-02, 'ssm_out_norm': 7.929542e-02, 'w_out': 8.047326e-02, 'ffn2_norm': 3.319895e-02, 'ffn2_w_gate': 1.420873e-02, 'ffn2_w_up': 1.380138e-02, 'ffn2_w_down': 2.281794e-02, 'ple_norm': 1.599623e-02, 'ple_w_gate': 1.614133e-02, 'ple_w_proj': 4.123174e-02, 'final_norm': 1.608359e+01}


def _to_microbatches(a, axis):
    t = _jnp.moveaxis(a, axis, 0)
    t = t.reshape((N_MICROBATCH, t.shape[0] // N_MICROBATCH) + t.shape[1:])
    return _jnp.moveaxis(t, 1, axis + 1)


def setup_inputs(seed: int = 0) -> dict:
    inp = _fwd_setup_inputs(seed)
    key = _jax.random.fold_in(_jax.random.key(seed), 7919)
    shape, _ = _output_shape()
    out = dict(inp)
    out["loss_target"] = _jax.random.normal(_jax.random.fold_in(key, 0), shape, _jnp.float32)
    for i, name in enumerate(TWIN_WEIGHTS):
        w = inp[name].astype(_jnp.float32)
        if MOMENT_SCALE is None:
            s = _jnp.sqrt(_jnp.mean(_jnp.square(w)) + 1e-30)
        else:
            s = MOMENT_SCALE[name]
        km, kv = _jax.random.split(_jax.random.fold_in(key, i + 1))
        out[name] = w
        out["m_" + name] = s * _jax.random.normal(km, w.shape, _jnp.float32)
        out["v_" + name] = (s * s) * _jax.random.uniform(kv, w.shape, _jnp.float32, 0.5, 1.5)
    if N_MICROBATCH > 1:
        for name, axis in PER_EXAMPLE_BATCH_AXIS.items():
            out[name] = _to_microbatches(out[name], axis)
    return {'x': out['x'], 'p': out['p'], 'ffn1_norm': out['ffn1_norm'], 'ffn1_w_gate': out['ffn1_w_gate'], 'ffn1_w_up': out['ffn1_w_up'], 'ffn1_w_down': out['ffn1_w_down'], 'mix_norm': out['mix_norm'], 'w_in': out['w_in'], 'conv_w': out['conv_w'], 'conv_b': out['conv_b'], 'ssm_A_re': out['ssm_A_re'], 'ssm_A_im': out['ssm_A_im'], 'ssm_B_re': out['ssm_B_re'], 'ssm_B_im': out['ssm_B_im'], 'ssm_C_re': out['ssm_C_re'], 'ssm_C_im': out['ssm_C_im'], 'ssm_D': out['ssm_D'], 'ssm_log_dt': out['ssm_log_dt'], 'glu_w': out['glu_w'], 'glu_b': out['glu_b'], 'conv_out_norm': out['conv_out_norm'], 'ssm_out_norm': out['ssm_out_norm'], 'w_out': out['w_out'], 'ffn2_norm': out['ffn2_norm'], 'ffn2_w_gate': out['ffn2_w_gate'], 'ffn2_w_up': out['ffn2_w_up'], 'ffn2_w_down': out['ffn2_w_down'], 'ple_norm': out['ple_norm'], 'ple_w_gate': out['ple_w_gate'], 'ple_w_proj': out['ple_w_proj'], 'final_norm': out['final_norm'], 'loss_target': out['loss_target'], 'm_ffn1_norm': out['m_ffn1_norm'], 'm_ffn1_w_gate': out['m_ffn1_w_gate'], 'm_ffn1_w_up': out['m_ffn1_w_up'], 'm_ffn1_w_down': out['m_ffn1_w_down'], 'm_mix_norm': out['m_mix_norm'], 'm_w_in': out['m_w_in'], 'm_conv_w': out['m_conv_w'], 'm_conv_b': out['m_conv_b'], 'm_ssm_A_re': out['m_ssm_A_re'], 'm_ssm_A_im': out['m_ssm_A_im'], 'm_ssm_B_re': out['m_ssm_B_re'], 'm_ssm_B_im': out['m_ssm_B_im'], 'm_ssm_C_re': out['m_ssm_C_re'], 'm_ssm_C_im': out['m_ssm_C_im'], 'm_ssm_D': out['m_ssm_D'], 'm_ssm_log_dt': out['m_ssm_log_dt'], 'm_glu_w': out['m_glu_w'], 'm_glu_b': out['m_glu_b'], 'm_conv_out_norm': out['m_conv_out_norm'], 'm_ssm_out_norm': out['m_ssm_out_norm'], 'm_w_out': out['m_w_out'], 'm_ffn2_norm': out['m_ffn2_norm'], 'm_ffn2_w_gate': out['m_ffn2_w_gate'], 'm_ffn2_w_up': out['m_ffn2_w_up'], 'm_ffn2_w_down': out['m_ffn2_w_down'], 'm_ple_norm': out['m_ple_norm'], 'm_ple_w_gate': out['m_ple_w_gate'], 'm_ple_w_proj': out['m_ple_w_proj'], 'm_final_norm': out['m_final_norm'], 'v_ffn1_norm': out['v_ffn1_norm'], 'v_ffn1_w_gate': out['v_ffn1_w_gate'], 'v_ffn1_w_up': out['v_ffn1_w_up'], 'v_ffn1_w_down': out['v_ffn1_w_down'], 'v_mix_norm': out['v_mix_norm'], 'v_w_in': out['v_w_in'], 'v_conv_w': out['v_conv_w'], 'v_conv_b': out['v_conv_b'], 'v_ssm_A_re': out['v_ssm_A_re'], 'v_ssm_A_im': out['v_ssm_A_im'], 'v_ssm_B_re': out['v_ssm_B_re'], 'v_ssm_B_im': out['v_ssm_B_im'], 'v_ssm_C_re': out['v_ssm_C_re'], 'v_ssm_C_im': out['v_ssm_C_im'], 'v_ssm_D': out['v_ssm_D'], 'v_ssm_log_dt': out['v_ssm_log_dt'], 'v_glu_w': out['v_glu_w'], 'v_glu_b': out['v_glu_b'], 'v_conv_out_norm': out['v_conv_out_norm'], 'v_ssm_out_norm': out['v_ssm_out_norm'], 'v_w_out': out['v_w_out'], 'v_ffn2_norm': out['v_ffn2_norm'], 'v_ffn2_w_gate': out['v_ffn2_w_gate'], 'v_ffn2_w_up': out['v_ffn2_w_up'], 'v_ffn2_w_down': out['v_ffn2_w_down'], 'v_ple_norm': out['v_ple_norm'], 'v_ple_w_gate': out['v_ple_w_gate'], 'v_ple_w_proj': out['v_ple_w_proj'], 'v_final_norm': out['v_final_norm']}


def _loss(weights, diff, rest, loss_target):
    with _jax.named_scope("forward"):
        args = {**rest, TWIN_DIFF_INPUT: diff, **{k: w.astype(_WEIGHT_DTYPES[k]) for k, w in weights.items()}}
        y = _forward(args)
    with _jax.named_scope("loss_head"):
        err = _jnp.square(y.astype(_jnp.float32) - loss_target)
        return 0.5 * _jnp.sum(_jnp.mean(err, axis=-1)) if err.ndim else 0.5 * err


def _adamw(w, g, m, v):
    m = ADAM_B1 * m + (1.0 - ADAM_B1) * g
    v = ADAM_B2 * v + (1.0 - ADAM_B2) * _jnp.square(g)
    m_hat = m / (1.0 - ADAM_B1 ** ADAM_STEP)
    v_hat = v / (1.0 - ADAM_B2 ** ADAM_STEP)
    delta = -ADAM_LR * (m_hat / (_jnp.sqrt(v_hat) + ADAM_EPS) + ADAM_WD * w)
    return delta, m, v


def reference(x, p, ffn1_norm, ffn1_w_gate, ffn1_w_up, ffn1_w_down, mix_norm, w_in, conv_w, conv_b, ssm_A_re, ssm_A_im, ssm_B_re, ssm_B_im, ssm_C_re, ssm_C_im, ssm_D, ssm_log_dt, glu_w, glu_b, conv_out_norm, ssm_out_norm, w_out, ffn2_norm, ffn2_w_gate, ffn2_w_up, ffn2_w_down, ple_norm, ple_w_gate, ple_w_proj, final_norm, loss_target, m_ffn1_norm, m_ffn1_w_gate, m_ffn1_w_up, m_ffn1_w_down, m_mix_norm, m_w_in, m_conv_w, m_conv_b, m_ssm_A_re, m_ssm_A_im, m_ssm_B_re, m_ssm_B_im, m_ssm_C_re, m_ssm_C_im, m_ssm_D, m_ssm_log_dt, m_glu_w, m_glu_b, m_conv_out_norm, m_ssm_out_norm, m_w_out, m_ffn2_norm, m_ffn2_w_gate, m_ffn2_w_up, m_ffn2_w_down, m_ple_norm, m_ple_w_gate, m_ple_w_proj, m_final_norm, v_ffn1_norm, v_ffn1_w_gate, v_ffn1_w_up, v_ffn1_w_down, v_mix_norm, v_w_in, v_conv_w, v_conv_b, v_ssm_A_re, v_ssm_A_im, v_ssm_B_re, v_ssm_B_im, v_ssm_C_re, v_ssm_C_im, v_ssm_D, v_ssm_log_dt, v_glu_w, v_glu_b, v_conv_out_norm, v_ssm_out_norm, v_w_out, v_ffn2_norm, v_ffn2_w_gate, v_ffn2_w_up, v_ffn2_w_down, v_ple_norm, v_ple_w_gate, v_ple_w_proj, v_final_norm):
    given = dict(x=x, p=p, ffn1_norm=ffn1_norm, ffn1_w_gate=ffn1_w_gate, ffn1_w_up=ffn1_w_up, ffn1_w_down=ffn1_w_down, mix_norm=mix_norm, w_in=w_in, conv_w=conv_w, conv_b=conv_b, ssm_A_re=ssm_A_re, ssm_A_im=ssm_A_im, ssm_B_re=ssm_B_re, ssm_B_im=ssm_B_im, ssm_C_re=ssm_C_re, ssm_C_im=ssm_C_im, ssm_D=ssm_D, ssm_log_dt=ssm_log_dt, glu_w=glu_w, glu_b=glu_b, conv_out_norm=conv_out_norm, ssm_out_norm=ssm_out_norm, w_out=w_out, ffn2_norm=ffn2_norm, ffn2_w_gate=ffn2_w_gate, ffn2_w_up=ffn2_w_up, ffn2_w_down=ffn2_w_down, ple_norm=ple_norm, ple_w_gate=ple_w_gate, ple_w_proj=ple_w_proj, final_norm=final_norm, loss_target=loss_target, m_ffn1_norm=m_ffn1_norm, m_ffn1_w_gate=m_ffn1_w_gate, m_ffn1_w_up=m_ffn1_w_up, m_ffn1_w_down=m_ffn1_w_down, m_mix_norm=m_mix_norm, m_w_in=m_w_in, m_conv_w=m_conv_w, m_conv_b=m_conv_b, m_ssm_A_re=m_ssm_A_re, m_ssm_A_im=m_ssm_A_im, m_ssm_B_re=m_ssm_B_re, m_ssm_B_im=m_ssm_B_im, m_ssm_C_re=m_ssm_C_re, m_ssm_C_im=m_ssm_C_im, m_ssm_D=m_ssm_D, m_ssm_log_dt=m_ssm_log_dt, m_glu_w=m_glu_w, m_glu_b=m_glu_b, m_conv_out_norm=m_conv_out_norm, m_ssm_out_norm=m_ssm_out_norm, m_w_out=m_w_out, m_ffn2_norm=m_ffn2_norm, m_ffn2_w_gate=m_ffn2_w_gate, m_ffn2_w_up=m_ffn2_w_up, m_ffn2_w_down=m_ffn2_w_down, m_ple_norm=m_ple_norm, m_ple_w_gate=m_ple_w_gate, m_ple_w_proj=m_ple_w_proj, m_final_norm=m_final_norm, v_ffn1_norm=v_ffn1_norm, v_ffn1_w_gate=v_ffn1_w_gate, v_ffn1_w_up=v_ffn1_w_up, v_ffn1_w_down=v_ffn1_w_down, v_mix_norm=v_mix_norm, v_w_in=v_w_in, v_conv_w=v_conv_w, v_conv_b=v_conv_b, v_ssm_A_re=v_ssm_A_re, v_ssm_A_im=v_ssm_A_im, v_ssm_B_re=v_ssm_B_re, v_ssm_B_im=v_ssm_B_im, v_ssm_C_re=v_ssm_C_re, v_ssm_C_im=v_ssm_C_im, v_ssm_D=v_ssm_D, v_ssm_log_dt=v_ssm_log_dt, v_glu_w=v_glu_w, v_glu_b=v_glu_b, v_conv_out_norm=v_conv_out_norm, v_ssm_out_norm=v_ssm_out_norm, v_w_out=v_w_out, v_ffn2_norm=v_ffn2_norm, v_ffn2_w_gate=v_ffn2_w_gate, v_ffn2_w_up=v_ffn2_w_up, v_ffn2_w_down=v_ffn2_w_down, v_ple_norm=v_ple_norm, v_ple_w_gate=v_ple_w_gate, v_ple_w_proj=v_ple_w_proj, v_final_norm=v_final_norm)
    weights = {n: given[n] for n in TWIN_WEIGHTS}
    shared = {n: given[n] for n in SHARED_INPUTS}
    per_example = {n: given[n] for n in ['x', 'p']}
    grad_fn = _jax.value_and_grad(_loss, argnums=(0, 1))

    def one_microbatch(ex, loss_target):
        ex = dict(ex)
        diff = ex.pop(TWIN_DIFF_INPUT)
        return grad_fn(weights, diff, {**shared, **ex}, loss_target)

    if N_MICROBATCH == 1:
        loss, (grad_w, grad_x) = one_microbatch(per_example, given["loss_target"])
    else:
        def body(carry, xs):
            loss_sum, grad_sum = carry
            l_k, (gw_k, gx_k) = one_microbatch(xs[0], xs[1])
            with _jax.named_scope("update"):
                return (loss_sum + l_k, _jax.tree.map(_jnp.add, grad_sum, gw_k)), gx_k

        init = (_jnp.zeros((), _jnp.float32), _jax.tree.map(_jnp.zeros_like, weights))
        (loss, grad_w), grad_x = _jax.lax.scan(body, init, (per_example, given["loss_target"]))
    with _jax.named_scope("update"):
        delta_w, new_m, new_v = {}, {}, {}
        for n in TWIN_WEIGHTS:
            delta_w[n], new_m[n], new_v[n] = _adamw(weights[n], grad_w[n], given["m_" + n], given["v_" + n])
    return (loss, grad_x, *[grad_w[n] for n in TWIN_WEIGHTS], *[delta_w[n] for n in TWIN_WEIGHTS],
            *[new_m[n] for n in TWIN_WEIGHTS], *[new_v[n] for n in TWIN_WEIGHTS])
```

```python
import math

import jax
import jax.numpy as jnp
from jax import lax
from jax.experimental import pallas as pl
from jax.experimental.pallas import tpu as pltpu

F32 = jnp.float32
BF16 = jnp.bfloat16

N_DEV = 8
DEPTH = 4
SEQ = 2048
D_MODEL = 1024
D_FF = 2816
CONV_W = 512
SSM_W = 512
SSM_GROUPS = 32
SSM_GROUP = 16
SSM_STATE = 64
N_STATE = SSM_GROUPS * SSM_STATE
IN_COLS = 2048
PLE_DIM = 256
EPS = 1e-6

ADAM_LR = 0.001
ADAM_B1 = 0.9
ADAM_B2 = 0.999
ADAM_EPS = 1e-08
ADAM_WD = 0.01
ADAM_STEP = 10

FF_BLOCK = 256
N_FF_BLOCKS = D_FF // FF_BLOCK
TOK_TILE_FFN = 1024
TOK_TILE = 512
CHUNK = 256
N_CHUNKS = SEQ // CHUNK
LANE_GROUP = 512
SUBLANES = 8
LANES = 128
MIB = 1024 * 1024

W_NAMES = ['ffn1_norm', 'ffn1_w_gate', 'ffn1_w_up', 'ffn1_w_down', 'mix_norm', 'w_in', 'conv_w', 'conv_b',
           'ssm_A_re', 'ssm_A_im', 'ssm_B_re', 'ssm_B_im', 'ssm_C_re', 'ssm_C_im', 'ssm_D', 'ssm_log_dt',
           'glu_w', 'glu_b', 'conv_out_norm', 'ssm_out_norm', 'w_out', 'ffn2_norm', 'ffn2_w_gate', 'ffn2_w_up',
           'ffn2_w_down', 'ple_norm', 'ple_w_gate', 'ple_w_proj', 'final_norm']
SMALL_NAMES = ['ffn1_norm', 'mix_norm', 'conv_b', 'ssm_A_re', 'ssm_A_im', 'ssm_B_re', 'ssm_B_im', 'ssm_C_re',
               'ssm_C_im', 'ssm_D', 'ssm_log_dt', 'glu_b', 'conv_out_norm', 'ssm_out_norm', 'ffn2_norm',
               'ple_norm', 'final_norm']

SEGS = ((3, 352), (3, 352), (1, 256), (1, 128), (1, 128), (1, 32), (1, 32))
PACK_ROWS = sum(n * r for n, r in SEGS)

MESH = pl.DeviceIdType.MESH
ANY = pl.BlockSpec(memory_space=pl.ANY)


def _cparams(sem=None, vmem_mib=48, **kw):
    return pltpu.CompilerParams(dimension_semantics=sem, vmem_limit_bytes=vmem_mib * MIB, **kw)


def _dot(a, b):
    return jnp.dot(a, b, preferred_element_type=F32)


def _dot_nt(a, b):
    return lax.dot_general(a, b, (((1,), (1,)), ((), ())), preferred_element_type=F32)


def _dot_tn(a, b):
    return lax.dot_general(a, b, (((0,), (0,)), ((), ())), preferred_element_type=F32)


def _rms_stats(x):
    r = lax.rsqrt(jnp.mean(x * x, axis=-1, keepdims=True) + EPS)
    return x * r, r


def _rms_bwd(dy, xh, r, g):
    dxh = dy * g
    dx = r * (dxh - xh * jnp.mean(dxh * xh, axis=-1, keepdims=True))
    dg = jnp.sum(dy * xh, axis=0, keepdims=True)
    return dx, dg


def _sigmoid(x):
    return 1.0 / (1.0 + jnp.exp(-x))


_GELU_C = math.sqrt(2.0 / math.pi)


def _gelu(x):
    t = jnp.tanh(_GELU_C * (x + 0.044715 * x * x * x))
    return 0.5 * x * (1.0 + t), t


def _gelu_grad(x, t):
    return 0.5 * (1.0 + t) + 0.5 * x * (1.0 - t * t) * _GELU_C * (1.0 + 3.0 * 0.044715 * x * x)


def _accumulate(ref, first, value):
    @pl.when(first)
    def _():
        ref[...] = value

    @pl.when(jnp.logical_not(first))
    def _():
        ref[...] += value


def _ffn_fwd(h, g, w3):
    tm = TOK_TILE_FFN

    def body(h_ref, g_ref, w_ref, out_ref, gu_ref, u_ref):
        k = pl.program_id(1)

        @pl.when(k == 0)
        def _():
            x = h_ref[...]
            xh, _ = _rms_stats(x)
            u_ref[...] = (xh * g_ref[...]).astype(BF16)
            out_ref[...] = x

        u = u_ref[...]
        gate = _dot_nt(u, w_ref[0])
        up = _dot_nt(u, w_ref[1])
        a = gate * _sigmoid(gate) * up
        gu_ref[0] = gate.astype(BF16)
        gu_ref[1] = up.astype(BF16)
        out_ref[...] += 0.5 * _dot(a.astype(BF16), w_ref[2])

    return pl.pallas_call(
        body, name="ffn_fwd",
        grid=(SEQ // tm, N_FF_BLOCKS),
        in_specs=[pl.BlockSpec((tm, D_MODEL), lambda m, k: (m, 0)),
                  pl.BlockSpec((1, D_MODEL), lambda m, k: (0, 0)),
                  pl.BlockSpec((3, FF_BLOCK, D_MODEL), lambda m, k: (0, k, 0))],
        out_specs=[pl.BlockSpec((tm, D_MODEL), lambda m, k: (m, 0)),
                   pl.BlockSpec((2, tm, FF_BLOCK), lambda m, k: (0, m, k))],
        out_shape=[jax.ShapeDtypeStruct((SEQ, D_MODEL), F32),
                   jax.ShapeDtypeStruct((2, SEQ, D_FF), BF16)],
        scratch_shapes=[pltpu.VMEM((tm, D_MODEL), BF16)],
        compiler_params=_cparams(("parallel", "arbitrary")),
    )(h, g, w3)


def _ffn_bwd_act(h, g, dout, gu, w3):
    tm = TOK_TILE

    def body(h_ref, g_ref, d_ref, gu_ref, w_ref, dh_ref, dga_ref, ud_ref, dg_ref, acc_ref):
        m = pl.program_id(0)
        k = pl.program_id(1)

        @pl.when(k == 0)
        def _():
            xh, _ = _rms_stats(h_ref[...])
            ud_ref[0] = (xh * g_ref[...]).astype(BF16)
            ud_ref[1] = (0.5 * d_ref[...]).astype(BF16)
            acc_ref[...] = jnp.zeros_like(acc_ref)

        gate = gu_ref[0].astype(F32)
        up = gu_ref[1].astype(F32)
        sg = _sigmoid(gate)
        silu = gate * sg
        da = _dot_nt(ud_ref[1], w_ref[2])
        dgate = (da * up * sg * (1.0 + gate * (1.0 - sg))).astype(BF16)
        dup = (da * silu).astype(BF16)
        dga_ref[0] = dgate
        dga_ref[1] = dup
        dga_ref[2] = (silu * up).astype(BF16)
        acc_ref[...] += _dot(dgate, w_ref[0]) + _dot(dup, w_ref[1])

        @pl.when(k == N_FF_BLOCKS - 1)
        def _():
            xh, r = _rms_stats(h_ref[...])
            dx, dg = _rms_bwd(acc_ref[...], xh, r, g_ref[...])
            dh_ref[...] = d_ref[...] + dx
            _accumulate(dg_ref, m == 0, dg)

    return pl.pallas_call(
        body, name="ffn_bwd_act",
        grid=(SEQ // tm, N_FF_BLOCKS),
        in_specs=[pl.BlockSpec((tm, D_MODEL), lambda m, k: (m, 0)),
                  pl.BlockSpec((1, D_MODEL), lambda m, k: (0, 0)),
                  pl.BlockSpec((tm, D_MODEL), lambda m, k: (m, 0)),
                  pl.BlockSpec((2, tm, FF_BLOCK), lambda m, k: (0, m, k)),
                  pl.BlockSpec((3, FF_BLOCK, D_MODEL), lambda m, k: (0, k, 0))],
        out_specs=[pl.BlockSpec((tm, D_MODEL), lambda m, k: (m, 0)),
                   pl.BlockSpec((3, tm, FF_BLOCK), lambda m, k: (0, m, k)),
                   pl.BlockSpec((2, tm, D_MODEL), lambda m, k: (0, m, 0)),
                   pl.BlockSpec((1, D_MODEL), lambda m, k: (0, 0))],
        out_shape=[jax.ShapeDtypeStruct((SEQ, D_MODEL), F32),
                   jax.ShapeDtypeStruct((3, SEQ, D_FF), BF16),
                   jax.ShapeDtypeStruct((2, SEQ, D_MODEL), BF16),
                   jax.ShapeDtypeStruct((1, D_MODEL), F32)],
        scratch_shapes=[pltpu.VMEM((tm, D_MODEL), F32)],
        compiler_params=_cparams(("arbitrary", "arbitrary")),
    )(h, g, dout, gu, w3)


def _matmul_tn(a, b, bm, out_dtype, name):
    na, t, m = a.shape
    nb, _, n = b.shape

    def body(a_ref, b_ref, o_ref):
        o_ref[0] = _dot_tn(a_ref[0], b_ref[0]).astype(out_dtype)

    return pl.pallas_call(
        body, name=name,
        grid=(na, m // bm),
        in_specs=[pl.BlockSpec((1, t, bm), lambda i, k: (i, 0, k)),
                  pl.BlockSpec((1, t, n), lambda i, k: (jnp.maximum(i - (na - nb), 0), 0, 0))],
        out_specs=pl.BlockSpec((1, bm, n), lambda i, k: (i, k, 0)),
        out_shape=jax.ShapeDtypeStruct((na, m, n), out_dtype),
        compiler_params=_cparams(("arbitrary", "parallel")),
    )(a, b)


def _inproj_fwd(h, g, wint):
    tm = TOK_TILE

    def body(h_ref, g_ref, w_ref, z_ref):
        xh, _ = _rms_stats(h_ref[...])
        z_ref[...] = _dot_nt((xh * g_ref[...]).astype(BF16), w_ref[...])

    return pl.pallas_call(
        body, name="inproj_fwd",
        grid=(SEQ // tm,),
        in_specs=[pl.BlockSpec((tm, D_MODEL), lambda m: (m, 0)),
                  pl.BlockSpec((1, D_MODEL), lambda m: (0, 0)),
                  pl.BlockSpec((IN_COLS, D_MODEL), lambda m: (0, 0))],
        out_specs=pl.BlockSpec((tm, IN_COLS), lambda m: (m, 0)),
        out_shape=jax.ShapeDtypeStruct((SEQ, IN_COLS), F32),
        compiler_params=_cparams(("parallel",)),
    )(h, g, wint)


def _inproj_bwd(h, g, dh, dz, wint):
    tm = TOK_TILE

    def body(h_ref, g_ref, dh_ref, dz_ref, w_ref, o_ref, u_ref, dg_ref):
        xh, r = _rms_stats(h_ref[...])
        u_ref[0] = (xh * g_ref[...]).astype(BF16)
        dx, dg = _rms_bwd(_dot(dz_ref[...], w_ref[...]), xh, r, g_ref[...])
        o_ref[...] = dh_ref[...] + dx
        _accumulate(dg_ref, pl.program_id(0) == 0, dg)

    return pl.pallas_call(
        body, name="inproj_bwd",
        grid=(SEQ // tm,),
        in_specs=[pl.BlockSpec((tm, D_MODEL), lambda m: (m, 0)),
                  pl.BlockSpec((1, D_MODEL), lambda m: (0, 0)),
                  pl.BlockSpec((tm, D_MODEL), lambda m: (m, 0)),
                  pl.BlockSpec((tm, IN_COLS), lambda m: (m, 0)),
                  pl.BlockSpec((IN_COLS, D_MODEL), lambda m: (0, 0))],
        out_specs=[pl.BlockSpec((tm, D_MODEL), lambda m: (m, 0)),
                   pl.BlockSpec((1, tm, D_MODEL), lambda m: (0, m, 0)),
                   pl.BlockSpec((1, D_MODEL), lambda m: (0, 0))],
        out_shape=[jax.ShapeDtypeStruct((SEQ, D_MODEL), F32),
                   jax.ShapeDtypeStruct((1, SEQ, D_MODEL), BF16),
                   jax.ShapeDtypeStruct((1, D_MODEL), F32)],
        compiler_params=_cparams(("arbitrary",)),
    )(h, g, dh, dz, wint)


def _row_ids(n, w):
    return lax.broadcasted_iota(jnp.int32, (n, w), 0)


def _bcast_row(x, i, n):
    return jnp.broadcast_to(x[i:i + 1, :], (n, x.shape[1]))


def _conv_taps(v, tail):
    n, w = v.shape
    rid = _row_ids(n, w)
    v1 = jnp.where(rid == 0, _bcast_row(tail, 7, n), pltpu.roll(v, 1, 0))
    v2 = jnp.where(rid == 0, _bcast_row(tail, 6, n),
                   jnp.where(rid == 1, _bcast_row(tail, 7, n), pltpu.roll(v, 2, 0)))
    return v1, v2


def _scan_chunk(work, ltab, carry, reverse):
    nblk = CHUNK // SUBLANES
    row = _row_ids(SUBLANES, LANE_GROUP)
    for gi in range(N_STATE // LANE_GROUP):
        cre = pl.ds(gi * LANE_GROUP, LANE_GROUP)
        cim = pl.ds(N_STATE + gi * LANE_GROUP, LANE_GROUP)
        pows = [(ltab[8 * k:8 * k + 8, cre], ltab[8 * k:8 * k + 8, cim]) for k in range(3)]
        pr = ltab[24:32, cre]
        pi = ltab[24:32, cim]

        def blk(i, c, cre=cre, cim=cim, pows=pows, pr=pr, pi=pi):
            cr, ci = c
            b = (nblk - 1 - i) if reverse else i
            r0 = pl.multiple_of(b * SUBLANES, SUBLANES)
            xr = work[pl.ds(r0, SUBLANES), cre]
            xi = work[pl.ds(r0, SUBLANES), cim]
            for k, s in enumerate((1, 2, 4)):
                lr, li = pows[k]
                if reverse:
                    keep = row < SUBLANES - s
                    sr = jnp.where(keep, pltpu.roll(xr, SUBLANES - s, 0), 0.0)
                    si = jnp.where(keep, pltpu.roll(xi, SUBLANES - s, 0), 0.0)
                else:
                    keep = row >= s
                    sr = jnp.where(keep, pltpu.roll(xr, s, 0), 0.0)
                    si = jnp.where(keep, pltpu.roll(xi, s, 0), 0.0)
                xr, xi = xr + lr * sr - li * si, xi + lr * si + li * sr
            xr, xi = xr + pr * cr - pi * ci, xi + pr * ci + pi * cr
            work[pl.ds(r0, SUBLANES), cre] = xr
            work[pl.ds(r0, SUBLANES), cim] = xi
            edge = 0 if reverse else SUBLANES - 1
            return _bcast_row(xr, edge, SUBLANES), _bcast_row(xi, edge, SUBLANES)

        cr, ci = lax.fori_loop(0, nblk, blk, (carry[:, cre], carry[:, cim]))
        carry[:, cre] = cr
        carry[:, cim] = ci


def _s5conv_fwd(z, convw, convb, bbmat, ccmat, dvec, ltab):
    def body(z_ref, cw_ref, cb_ref, bb_ref, cc_ref, d_ref, lt_ref, ya_ref, ys_ref, hs_ref,
             work, carry, tail):
        c = pl.program_id(0)

        @pl.when(c == 0)
        def _():
            carry[...] = jnp.zeros_like(carry)
            tail[...] = jnp.zeros_like(tail)

        zb = z_ref[:, 0:CONV_W]
        v = z_ref[:, CONV_W:2 * CONV_W] * z_ref[:, 2 * CONV_W:3 * CONV_W]
        us = z_ref[:, 3 * CONV_W:4 * CONV_W]
        v1, v2 = _conv_taps(v, tail[...])
        tail[...] = v[CHUNK - 8:CHUNK, :]
        y = cw_ref[0:1, :] * v2 + cw_ref[1:2, :] * v1 + cw_ref[2:3, :] * v
        ya_ref[...] = zb * (y + cb_ref[...])

        work[...] = _dot(us.astype(BF16), bb_ref[...])
        _scan_chunk(work, lt_ref, carry, reverse=False)
        hs = work[...].astype(BF16)
        hs_ref[...] = hs
        ys_ref[...] = _dot(hs, cc_ref[...]) + d_ref[...] * us

    return pl.pallas_call(
        body, name="s5conv_fwd",
        grid=(N_CHUNKS,),
        in_specs=[pl.BlockSpec((CHUNK, IN_COLS), lambda c: (c, 0)),
                  pl.BlockSpec((3, CONV_W), lambda c: (0, 0)),
                  pl.BlockSpec((1, CONV_W), lambda c: (0, 0)),
                  pl.BlockSpec((SSM_W, 2 * N_STATE), lambda c: (0, 0)),
                  pl.BlockSpec((2 * N_STATE, SSM_W), lambda c: (0, 0)),
                  pl.BlockSpec((1, SSM_W), lambda c: (0, 0)),
                  pl.BlockSpec((32, 2 * N_STATE), lambda c: (0, 0))],
        out_specs=[pl.BlockSpec((CHUNK, CONV_W), lambda c: (c, 0)),
                   pl.BlockSpec((CHUNK, SSM_W), lambda c: (c, 0)),
                   pl.BlockSpec((CHUNK, 2 * N_STATE), lambda c: (c, 0))],
        out_shape=[jax.ShapeDtypeStruct((SEQ, CONV_W), F32),
                   jax.ShapeDtypeStruct((SEQ, SSM_W), F32),
                   jax.ShapeDtypeStruct((SEQ, 2 * N_STATE), BF16)],
        scratch_shapes=[pltpu.VMEM((CHUNK, 2 * N_STATE), F32),
                        pltpu.VMEM((8, 2 * N_STATE), F32),
                        pltpu.VMEM((8, CONV_W), F32)],
        compiler_params=_cparams(("arbitrary",)),
    )(z, convw, convb, bbmat, ccmat, dvec, ltab)


def _s5conv_bwd(z, hs, dya, dys, convw, convb, bbmat, ccmat, dvec, ltab_rev):
    nc = N_CHUNKS
    hb = 16

    def body(z_ref, zp_ref, hs_ref, hp_ref, dya_ref, dys_ref, cw_ref, cb_ref, bb_ref, cc_ref, d_ref, lt_ref,
             dz_ref, g_ref, us_ref, dyb_ref, dl_ref, dcw_ref, work, carry, head):
        i = pl.program_id(0)
        first_chunk = i == nc - 1

        @pl.when(i == 0)
        def _():
            carry[...] = jnp.zeros_like(carry)
            head[...] = jnp.zeros_like(head)
            dl_ref[...] = jnp.zeros_like(dl_ref)
            dcw_ref[...] = jnp.zeros_like(dcw_ref)

        us = z_ref[:, 3 * CONV_W:4 * CONV_W]
        dy = dys_ref[...]
        dy_bf = dy.astype(BF16)
        us_ref[0] = us.astype(BF16)
        dyb_ref[0] = dy_bf

        work[...] = _dot_nt(dy_bf, cc_ref[...])
        _scan_chunk(work, lt_ref, carry, reverse=True)
        gg = work[...]
        gg_bf = gg.astype(BF16)
        g_ref[0] = gg_bf
        dus = d_ref[...] * dy + _dot_nt(gg_bf, bb_ref[...])

        hcur = hs_ref[...].astype(F32)
        hlast = hp_ref[...].astype(F32)[hb - 1:hb, :]
        hlast = jnp.where(first_chunk, 0.0, hlast)
        rid = _row_ids(CHUNK, 2 * N_STATE)
        hprev = jnp.where(rid == 0, jnp.broadcast_to(hlast, (CHUNK, 2 * N_STATE)), pltpu.roll(hcur, 1, 0))
        gr, gi = gg[:, :N_STATE], gg[:, N_STATE:]
        hr, hi = hprev[:, :N_STATE], hprev[:, N_STATE:]
        dl_ref[:, :N_STATE] += (gr * hr + gi * hi).reshape(CHUNK // 8, 8, N_STATE).sum(axis=0)
        dl_ref[:, N_STATE:] += (gi * hr - gr * hi).reshape(CHUNK // 8, 8, N_STATE).sum(axis=0)

        @pl.when(i == nc - 1)
        def _():
            dl_ref[0:1, :] = jnp.sum(dl_ref[...], axis=0, keepdims=True)

        zb = z_ref[:, 0:CONV_W]
        zc = z_ref[:, CONV_W:2 * CONV_W]
        zv = z_ref[:, 2 * CONV_W:3 * CONV_W]
        v = zc * zv
        vtail = jnp.where(first_chunk, 0.0, zp_ref[:, CONV_W:2 * CONV_W] * zp_ref[:, 2 * CONV_W:3 * CONV_W])
        v1, v2 = _conv_taps(v, vtail)
        w0, w1, w2 = cw_ref[0:1, :], cw_ref[1:2, :], cw_ref[2:3, :]
        y = w0 * v2 + w1 * v1 + w2 * v
        dya_v = dya_ref[...]
        dzb = dya_v * (y + cb_ref[...])
        dyc = dya_v * zb
        hd = head[...]
        rc = _row_ids(CHUNK, CONV_W)
        n1 = jnp.where(rc == CHUNK - 1, _bcast_row(hd, 0, CHUNK), pltpu.roll(dyc, CHUNK - 1, 0))
        n2 = jnp.where(rc == CHUNK - 1, _bcast_row(hd, 1, CHUNK),
                       jnp.where(rc == CHUNK - 2, _bcast_row(hd, 0, CHUNK), pltpu.roll(dyc, CHUNK - 2, 0)))
        head[...] = dyc[0:8, :]
        dv = w2 * dyc + w1 * n1 + w0 * n2
        dz_ref[:, 0:CONV_W] = dzb.astype(BF16)
        dz_ref[:, CONV_W:2 * CONV_W] = (dv * zv).astype(BF16)
        dz_ref[:, 2 * CONV_W:3 * CONV_W] = (dv * zc).astype(BF16)
        dz_ref[:, 3 * CONV_W:4 * CONV_W] = dus.astype(BF16)
        dcw_ref[0:1, :] += jnp.sum(dyc * v2, axis=0, keepdims=True)
        dcw_ref[1:2, :] += jnp.sum(dyc * v1, axis=0, keepdims=True)
        dcw_ref[2:3, :] += jnp.sum(dyc * v, axis=0, keepdims=True)
        dcw_ref[3:4, :] += jnp.sum(dyc, axis=0, keepdims=True)
        dcw_ref[4:5, :] += jnp.sum(dy * us, axis=0, keepdims=True)

    rev = lambda i: nc - 1 - i
    return pl.pallas_call(
        body, name="s5conv_bwd",
        grid=(nc,),
        in_specs=[pl.BlockSpec((CHUNK, IN_COLS), lambda i: (rev(i), 0)),
                  pl.BlockSpec((8, IN_COLS), lambda i: (jnp.maximum(rev(i) * (CHUNK // 8) - 1, 0), 0)),
                  pl.BlockSpec((CHUNK, 2 * N_STATE), lambda i: (rev(i), 0)),
                  pl.BlockSpec((hb, 2 * N_STATE), lambda i: (jnp.maximum(rev(i) * (CHUNK // hb) - 1, 0), 0)),
                  pl.BlockSpec((CHUNK, CONV_W), lambda i: (rev(i), 0)),
                  pl.BlockSpec((CHUNK, SSM_W), lambda i: (rev(i), 0)),
                  pl.BlockSpec((3, CONV_W), lambda i: (0, 0)),
                  pl.BlockSpec((1, CONV_W), lambda i: (0, 0)),
                  pl.BlockSpec((SSM_W, 2 * N_STATE), lambda i: (0, 0)),
                  pl.BlockSpec((2 * N_STATE, SSM_W), lambda i: (0, 0)),
                  pl.BlockSpec((1, SSM_W), lambda i: (0, 0)),
                  pl.BlockSpec((32, 2 * N_STATE), lambda i: (0, 0))],
        out_specs=[pl.BlockSpec((CHUNK, IN_COLS), lambda i: (rev(i), 0)),
                   pl.BlockSpec((1, CHUNK, 2 * N_STATE), lambda i: (0, rev(i), 0)),
                   pl.BlockSpec((1, CHUNK, SSM_W), lambda i: (0, rev(i), 0)),
                   pl.BlockSpec((1, CHUNK, SSM_W), lambda i: (0, rev(i), 0)),
                   pl.BlockSpec((8, 2 * N_STATE), lambda i: (0, 0)),
                   pl.BlockSpec((8, CONV_W), lambda i: (0, 0))],
        out_shape=[jax.ShapeDtypeStruct((SEQ, IN_COLS), BF16),
                   jax.ShapeDtypeStruct((1, SEQ, 2 * N_STATE), BF16),
                   jax.ShapeDtypeStruct((1, SEQ, SSM_W), BF16),
                   jax.ShapeDtypeStruct((1, SEQ, SSM_W), BF16),
                   jax.ShapeDtypeStruct((8, 2 * N_STATE), F32),
                   jax.ShapeDtypeStruct((8, CONV_W), F32)],
        scratch_shapes=[pltpu.VMEM((CHUNK, 2 * N_STATE), F32),
                        pltpu.VMEM((8, 2 * N_STATE), F32),
                        pltpu.VMEM((8, CONV_W), F32)],
        compiler_params=_cparams(("arbitrary",)),
    )(z, z, hs, hs, dya, dys, convw, convb, bbmat, ccmat, dvec, ltab_rev)


def _mix_out_fwd(h, ya, ys, gluw, glub, con, son, wout):
    tm = TOK_TILE

    def body(h_ref, ya_ref, ys_ref, gw_ref, gb_ref, con_ref, son_ref, wo_ref, o_ref):
        zg, _ = _gelu(ys_ref[...])
        q = _dot(zg.astype(BF16), gw_ref[...]) + gb_ref[...]
        out_s = zg * _sigmoid(q)
        na, _ = _rms_stats(ya_ref[...])
        ns, _ = _rms_stats(out_s)
        o_ref[...] = (h_ref[...]
                      + _dot((na * con_ref[...]).astype(BF16), wo_ref[0:CONV_W, :])
                      + _dot((ns * son_ref[...]).astype(BF16), wo_ref[CONV_W:2 * CONV_W, :]))

    row = lambda m: (m, 0)
    fixed = lambda m: (0, 0)
    return pl.pallas_call(
        body, name="mix_out_fwd",
        grid=(SEQ // tm,),
        in_specs=[pl.BlockSpec((tm, D_MODEL), row), pl.BlockSpec((tm, CONV_W), row), pl.BlockSpec((tm, SSM_W), row),
                  pl.BlockSpec((SSM_W, SSM_W), fixed), pl.BlockSpec((1, SSM_W), fixed),
                  pl.BlockSpec((1, CONV_W), fixed), pl.BlockSpec((1, SSM_W), fixed),
                  pl.BlockSpec((D_MODEL, D_MODEL), fixed)],
        out_specs=pl.BlockSpec((tm, D_MODEL), row),
        out_shape=jax.ShapeDtypeStruct((SEQ, D_MODEL), F32),
        compiler_params=_cparams(("parallel",)),
    )(h, ya, ys, gluw, glub, con, son, wout)


def _mix_out_bwd(dh, ya, ys, gluw, glub, con, son, wout):
    tm = TOK_TILE

    def body(dh_ref, ya_ref, ys_ref, gw_ref, gb_ref, con_ref, son_ref, wo_ref,
             dya_ref, dys_ref, yc_ref, dhb_ref, zg_ref, dq_ref, part_ref):
        ysv = ys_ref[...]
        zg, th = _gelu(ysv)
        zg_bf = zg.astype(BF16)
        s = _sigmoid(_dot(zg_bf, gw_ref[...]) + gb_ref[...])
        out_s = zg * s
        na, ra = _rms_stats(ya_ref[...])
        ns, rs = _rms_stats(out_s)
        dh_bf = dh_ref[...].astype(BF16)
        yc_ref[0, :, 0:CONV_W] = (na * con_ref[...]).astype(BF16)
        yc_ref[0, :, CONV_W:2 * CONV_W] = (ns * son_ref[...]).astype(BF16)
        dhb_ref[0] = dh_bf
        dca = _dot_nt(dh_bf, wo_ref[0:CONV_W, :])
        dcs = _dot_nt(dh_bf, wo_ref[CONV_W:2 * CONV_W, :])
        dya, dcon = _rms_bwd(dca, na, ra, con_ref[...])
        dos, dson = _rms_bwd(dcs, ns, rs, son_ref[...])
        dya_ref[...] = dya
        dq = dos * zg * s * (1.0 - s)
        dq_bf = dq.astype(BF16)
        dzg = dos * s + _dot_nt(dq_bf, gw_ref[...])
        dys_ref[...] = dzg * _gelu_grad(ysv, th)
        zg_ref[0] = zg_bf
        dq_ref[0] = dq_bf
        rid = _row_ids(SUBLANES, SSM_W)
        part = jnp.zeros((SUBLANES, SSM_W), F32)
        for i, rowv in enumerate((dcon, dson, jnp.sum(dq, axis=0, keepdims=True))):
            part = jnp.where(rid == i, jnp.broadcast_to(rowv, (SUBLANES, SSM_W)), part)
        _accumulate(part_ref, pl.program_id(0) == 0, part)

    row = lambda m: (m, 0)
    fixed = lambda m: (0, 0)
    lead = lambda m: (0, m, 0)
    return pl.pallas_call(
        body, name="mix_out_bwd",
        grid=(SEQ // tm,),
        in_specs=[pl.BlockSpec((tm, D_MODEL), row), pl.BlockSpec((tm, CONV_W), row), pl.BlockSpec((tm, SSM_W), row),
                  pl.BlockSpec((SSM_W, SSM_W), fixed), pl.BlockSpec((1, SSM_W), fixed),
                  pl.BlockSpec((1, CONV_W), fixed), pl.BlockSpec((1, SSM_W), fixed),
                  pl.BlockSpec((D_MODEL, D_MODEL), fixed)],
        out_specs=[pl.BlockSpec((tm, CONV_W), row), pl.BlockSpec((tm, SSM_W), row),
                   pl.BlockSpec((1, tm, D_MODEL), lead), pl.BlockSpec((1, tm, D_MODEL), lead),
                   pl.BlockSpec((1, tm, SSM_W), lead), pl.BlockSpec((1, tm, SSM_W), lead),
                   pl.BlockSpec((8, SSM_W), fixed)],
        out_shape=[jax.ShapeDtypeStruct((SEQ, CONV_W), F32), jax.ShapeDtypeStruct((SEQ, SSM_W), F32),
                   jax.ShapeDtypeStruct((1, SEQ, D_MODEL), BF16), jax.ShapeDtypeStruct((1, SEQ, D_MODEL), BF16),
                   jax.ShapeDtypeStruct((1, SEQ, SSM_W), BF16), jax.ShapeDtypeStruct((1, SEQ, SSM_W), BF16),
                   jax.ShapeDtypeStruct((8, SSM_W), F32)],
        compiler_params=_cparams(("arbitrary",)),
    )(dh, ya, ys, gluw, glub, con, son, wout)


def _ple_fwd(h, g, p, wgate, wprojt):
    tm = TOK_TILE

    def body(h_ref, g_ref, p_ref, wg_ref, wp_ref, o_ref):
        x = h_ref[...]
        xh, _ = _rms_stats(x)
        s = _sigmoid(_dot((xh * g_ref[...]).astype(BF16), wg_ref[...]))
        o_ref[...] = x + _dot_nt(p_ref[...].astype(BF16), wp_ref[...]) * s

    row = lambda m: (m, 0)
    fixed = lambda m: (0, 0)
    return pl.pallas_call(
        body, name="ple_fwd",
        grid=(SEQ // tm,),
        in_specs=[pl.BlockSpec((tm, D_MODEL), row), pl.BlockSpec((1, D_MODEL), fixed), pl.BlockSpec((tm, PLE_DIM), row),
                  pl.BlockSpec((D_MODEL, D_MODEL), fixed), pl.BlockSpec((D_MODEL, PLE_DIM), fixed)],
        out_specs=pl.BlockSpec((tm, D_MODEL), row),
        out_shape=jax.ShapeDtypeStruct((SEQ, D_MODEL), F32),
        compiler_params=_cparams(("parallel",)),
    )(h, g, p, wgate, wprojt)


def _ple_bwd(h, g, p, dh, wgate, wprojt):
    tm = TOK_TILE

    def body(h_ref, g_ref, p_ref, dh_ref, wg_ref, wp_ref, o_ref, u_ref, dq_ref, dpp_ref, pb_ref, dg_ref):
        xh, r = _rms_stats(h_ref[...])
        u = (xh * g_ref[...]).astype(BF16)
        s = _sigmoid(_dot(u, wg_ref[...]))
        p_bf = p_ref[...].astype(BF16)
        pp = _dot_nt(p_bf, wp_ref[...])
        dhv = dh_ref[...]
        dq = (dhv * pp * s * (1.0 - s)).astype(BF16)
        u_ref[0] = u
        dq_ref[0] = dq
        dpp_ref[0] = (dhv * s).astype(BF16)
        pb_ref[0] = p_bf
        dx, dg = _rms_bwd(_dot_nt(dq, wg_ref[...]), xh, r, g_ref[...])
        o_ref[...] = dhv + dx
        _accumulate(dg_ref, pl.program_id(0) == 0, dg)

    row = lambda m: (m, 0)
    fixed = lambda m: (0, 0)
    lead = lambda m: (0, m, 0)
    big = jax.ShapeDtypeStruct((1, SEQ, D_MODEL), BF16)
    return pl.pallas_call(
        body, name="ple_bwd",
        grid=(SEQ // tm,),
        in_specs=[pl.BlockSpec((tm, D_MODEL), row), pl.BlockSpec((1, D_MODEL), fixed), pl.BlockSpec((tm, PLE_DIM), row),
                  pl.BlockSpec((tm, D_MODEL), row),
                  pl.BlockSpec((D_MODEL, D_MODEL), fixed), pl.BlockSpec((D_MODEL, PLE_DIM), fixed)],
        out_specs=[pl.BlockSpec((tm, D_MODEL), row),
                   pl.BlockSpec((1, tm, D_MODEL), lead), pl.BlockSpec((1, tm, D_MODEL), lead),
                   pl.BlockSpec((1, tm, D_MODEL), lead), pl.BlockSpec((1, tm, PLE_DIM), lead),
                   pl.BlockSpec((1, D_MODEL), fixed)],
        out_shape=[jax.ShapeDtypeStruct((SEQ, D_MODEL), F32), big, big, big,
                   jax.ShapeDtypeStruct((1, SEQ, PLE_DIM), BF16),
                   jax.ShapeDtypeStruct((1, D_MODEL), F32)],
        compiler_params=_cparams(("arbitrary",)),
    )(h, g, p, dh, wgate, wprojt)


def _final_loss(h, g, target):
    tm = TOK_TILE

    def body(h_ref, g_ref, t_ref, loss_ref, dh_ref, dg_ref):
        first = pl.program_id(0) == 0
        xh, r = _rms_stats(h_ref[...])
        diff = xh * g_ref[...] - t_ref[...]
        part = 0.5 * jnp.sum(jnp.mean(diff * diff, axis=-1, keepdims=True), axis=0, keepdims=True)
        _accumulate(loss_ref, first, jnp.broadcast_to(part, (SUBLANES, LANES)))
        dx, dg = _rms_bwd(diff * (1.0 / D_MODEL), xh, r, g_ref[...])
        dh_ref[...] = dx
        _accumulate(dg_ref, first, dg)

    row = lambda m: (m, 0)
    fixed = lambda m: (0, 0)
    return pl.pallas_call(
        body, name="final_loss",
        grid=(SEQ // tm,),
        in_specs=[pl.BlockSpec((tm, D_MODEL), row), pl.BlockSpec((1, D_MODEL), fixed),
                  pl.BlockSpec((tm, D_MODEL), row)],
        out_specs=[pl.BlockSpec((SUBLANES, LANES), fixed),
                   pl.BlockSpec((tm, D_MODEL), row),
                   pl.BlockSpec((1, D_MODEL), fixed)],
        out_shape=[jax.ShapeDtypeStruct((SUBLANES, LANES), F32),
                   jax.ShapeDtypeStruct((SEQ, D_MODEL), F32),
                   jax.ShapeDtypeStruct((1, D_MODEL), F32)],
        compiler_params=_cparams(("arbitrary",)),
    )(h, g, target)


def _disc(ar, ai, ldt):
    dt = jnp.exp(ldt)
    mag = jnp.exp(ar * dt)
    ph = ai * dt
    lr, li = mag * jnp.cos(ph), mag * jnp.sin(ph)
    nr, ni = lr - 1.0, li
    den = ar * ar + ai * ai
    return lr, li, (nr * ar + ni * ai) / den, (ni * ar - nr * ai) / den


def _s5_disc(a, ldt, a_rep, ldt_rep, b):
    def body(a_ref, l_ref, ar_ref, lr_ref, b_ref, pw_ref, bb_ref):
        lr, li, _, _ = _disc(a_ref[0], a_ref[1], l_ref[...])
        pr, pi = lr, li
        for k in range(8):
            pw_ref[k] = pr
            pw_ref[8 + k] = pi
            pr, pi = pr * lr - pi * li, pr * li + pi * lr
        _, _, fr, fi = _disc(ar_ref[0], ar_ref[1], lr_ref[...])
        br, bi = b_ref[0], b_ref[1]
        bb_ref[0] = fr * br - fi * bi
        bb_ref[1] = fr * bi + fi * br

    return pl.pallas_call(
        body, name="s5_disc",
        out_shape=[jax.ShapeDtypeStruct((16, SSM_GROUPS, SSM_STATE), F32),
                   jax.ShapeDtypeStruct((2, SSM_GROUPS, SSM_STATE * SSM_GROUP), F32)],
    )(a, ldt, a_rep, ldt_rep, b)


def _dot_exact(x, sel):
    hi = x.astype(BF16)
    r1 = x - hi.astype(F32)
    mid = r1.astype(BF16)
    lo = (r1 - mid.astype(F32)).astype(BF16)
    return _dot(hi, sel) + _dot(mid, sel) + _dot(lo, sel)


def _s5_disc_bwd(a, ldt, a_rep, ldt_rep, b, dl, dbb, sel):
    def body(a_ref, l_ref, ar_ref, lr_ref, b_ref, dl_ref, dbb_ref, sel_ref, da_ref, dldt_ref, db_ref):
        _, _, fr, fi = _disc(ar_ref[0], ar_ref[1], lr_ref[...])
        br, bi = b_ref[0], b_ref[1]
        dr, di = dbb_ref[0], dbb_ref[1]
        db_ref[0] = fr * dr + fi * di
        db_ref[1] = fr * di - fi * dr
        dfr = _dot_exact(dr * br + di * bi, sel_ref[...])
        dfi = _dot_exact(di * br - dr * bi, sel_ref[...])
        _, vjp = jax.vjp(_disc, a_ref[0], a_ref[1], l_ref[...])
        dar, dai, dldt = vjp((dl_ref[0], dl_ref[1], dfr, dfi))
        da_ref[0] = dar
        da_ref[1] = dai
        dldt_ref[...] = jnp.sum(dldt, axis=1, keepdims=True)

    return pl.pallas_call(
        body, name="s5_disc_bwd",
        out_shape=[jax.ShapeDtypeStruct((2, SSM_GROUPS, SSM_STATE), F32),
                   jax.ShapeDtypeStruct((SSM_GROUPS, 1), F32),
                   jax.ShapeDtypeStruct((2, SSM_GROUPS, SSM_STATE * SSM_GROUP), F32)],
    )(a, ldt, a_rep, ldt_rep, b, dl, dbb, sel)


def _row_block(rows, cap=512):
    for bm in range(min(cap, rows), 0, -1):
        if rows % bm == 0 and (bm % 8 == 0 or bm == rows):
            return bm
    return rows


def _pair_sum(a, b):
    n, r, c = a.shape
    bm = _row_block(r)

    def body(a_ref, b_ref, s_ref, sb_ref):
        s = a_ref[...].astype(F32) + b_ref[...].astype(F32)
        s_ref[...] = s
        sb_ref[...] = s.astype(BF16)

    spec = pl.BlockSpec((1, bm, c), lambda i, k: (i, k, 0))
    return pl.pallas_call(
        body, name="pair_sum",
        grid=(n, r // bm),
        in_specs=[spec, spec], out_specs=[spec, spec],
        out_shape=[jax.ShapeDtypeStruct(a.shape, F32), jax.ShapeDtypeStruct(a.shape, BF16)],
        compiler_params=_cparams(("parallel", "parallel")),
    )(a, b)


def _chip_sum(own, rb):
    r, c = own.shape
    bm = _row_block(r)

    def body(o_ref, r_ref, s_ref):
        s_ref[...] = ((o_ref[...] + r_ref[0].astype(F32)) + r_ref[1].astype(F32)) + r_ref[2].astype(F32)

    return pl.pallas_call(
        body, name="chip_sum",
        grid=(r // bm,),
        in_specs=[pl.BlockSpec((bm, c), lambda k: (k, 0)), pl.BlockSpec((3, bm, c), lambda k: (0, k, 0))],
        out_specs=pl.BlockSpec((bm, c), lambda k: (k, 0)),
        out_shape=jax.ShapeDtypeStruct(own.shape, F32),
        compiler_params=_cparams(("parallel",)),
    )(own, rb)


def _sum8(x):
    _, r, c = x.shape
    bm = _row_block(r)

    def body(x_ref, s_ref):
        s = x_ref[0]
        for d in range(1, N_DEV):
            s = s + x_ref[d]
        s_ref[...] = s

    return pl.pallas_call(
        body, name="sum8",
        grid=(r // bm,),
        in_specs=[pl.BlockSpec((N_DEV, bm, c), lambda k: (0, k, 0))],
        out_specs=pl.BlockSpec((bm, c), lambda k: (k, 0)),
        out_shape=jax.ShapeDtypeStruct((r, c), F32),
        compiler_params=_cparams(("parallel",)),
    )(x)


def _adamw(w, g, m, v):
    r, c = w.shape
    bm = _row_block(r)
    bc1 = 1.0 - ADAM_B1 ** ADAM_STEP
    bc2 = 1.0 - ADAM_B2 ** ADAM_STEP

    def body(w_ref, g_ref, m_ref, v_ref, d_ref, nm_ref, nv_ref):
        gv = g_ref[...]
        nm = ADAM_B1 * m_ref[...] + (1.0 - ADAM_B1) * gv
        nv = ADAM_B2 * v_ref[...] + (1.0 - ADAM_B2) * (gv * gv)
        nm_ref[...] = nm
        nv_ref[...] = nv
        d_ref[...] = -ADAM_LR * ((nm / bc1) / (jnp.sqrt(nv / bc2) + ADAM_EPS) + ADAM_WD * w_ref[...])

    spec = pl.BlockSpec((bm, c), lambda k: (k, 0))
    shp = jax.ShapeDtypeStruct((r, c), F32)
    return pl.pallas_call(
        body, name="adamw",
        grid=(r // bm,),
        in_specs=[spec] * 4, out_specs=[spec] * 3, out_shape=[shp] * 3,
        compiler_params=_cparams(("parallel",)),
    )(w, g, m, v)


def _mesh_pos():
    return lax.axis_index("x"), lax.axis_index("y"), lax.axis_index("c")


def _dev_index(p):
    return 4 * p[0] + 2 * p[1] + p[2]


def _seg_offsets(segs):
    offs, o = [], 0
    for n, r in segs:
        offs.append(o)
        o += n * r
    return offs


def _remote(src, dst, send_sem, recv_sem, to):
    return pltpu.make_async_remote_copy(src_ref=src, dst_ref=dst, send_sem=send_sem, recv_sem=recv_sem,
                                        device_id=to, device_id_type=MESH)


def _allgather(pack, segs, name):
    rtot, c = pack.shape
    ns = len(segs)
    offs = _seg_offsets(segs)
    assert rtot == sum(n * r for n, r in segs)

    def body(pack_ref, *refs):
        outs = refs[:ns]
        send_sems, recv_sems, local_sem = refs[ns:]
        x, y, cc = _mesh_pos()
        me, sib = (x, y, cc), (x, y, 1 - cc)
        chips = [(1 - x, y), (x, 1 - y), (1 - x, 1 - y)]

        def pieces(dev, from_pack):
            res = []
            for a, (n, r) in enumerate(segs):
                for m in range(n):
                    dst = outs[a].at[m, pl.ds(pl.multiple_of(dev * r, r), r), :]
                    src = pack_ref.at[pl.ds(offs[a] + m * r, r), :] if from_pack else dst
                    res.append((src, dst))
            return res

        def push(k, dev, to, from_pack):
            for s, d in pieces(dev, from_pack):
                _remote(s, d, send_sems.at[k], recv_sems.at[k], to).start()

        def whole(k):
            return _remote(pack_ref, pack_ref, send_sems.at[k], recv_sems.at[k], me)

        my_dev = _dev_index(me)
        for s, d in pieces(my_dev, True):
            pltpu.make_async_copy(s, d, local_sem).start()
        push(0, my_dev, sib, True)
        for j, chip in enumerate(chips):
            push(1 + j, my_dev, (*chip, cc), True)
        for j, chip in enumerate(chips):
            whole(1 + j).wait_recv()
            push(4 + j, _dev_index((*chip, cc)), sib, False)
        whole(0).wait_recv()
        for j in range(3):
            whole(4 + j).wait_recv()
        for k in range(7):
            whole(k).wait_send()
        pltpu.make_async_copy(pack_ref, pack_ref, local_sem).wait()

    return pl.pallas_call(
        body, name=name,
        in_specs=[ANY], out_specs=[ANY] * ns,
        out_shape=[jax.ShapeDtypeStruct((n, N_DEV * r, c), pack.dtype) for n, r in segs],
        scratch_shapes=[pltpu.SemaphoreType.DMA((7,)), pltpu.SemaphoreType.DMA((7,)), pltpu.SemaphoreType.DMA],
    )(pack)


def _rs_sibling(fulls, segs):
    ns = len(segs)
    offs = _seg_offsets(segs)
    rtot = sum(n * r for n, r in segs)
    c = fulls[0].shape[-1]
    dt = fulls[0].dtype

    def body(*refs):
        srcs = refs[:ns]
        own_ref, got_ref, send_sem, recv_sem, local_sem = refs[ns:]
        x, y, cc = _mesh_pos()
        me, sib = (x, y, cc), (x, y, 1 - cc)
        for k in range(4):
            for a, (n, r) in enumerate(segs):
                for m in range(n):
                    rows = pl.ds(offs[a] + m * r, r)
                    mine = srcs[a].at[m, pl.ds(pl.multiple_of((2 * k + cc) * r, r), r), :]
                    theirs = srcs[a].at[m, pl.ds(pl.multiple_of((2 * k + 1 - cc) * r, r), r), :]
                    pltpu.make_async_copy(mine, own_ref.at[k, rows, :], local_sem).start()
                    _remote(theirs, got_ref.at[k, rows, :], send_sem, recv_sem, sib).start()
        _remote(own_ref, got_ref, send_sem, recv_sem, me).wait()
        pltpu.make_async_copy(own_ref, own_ref, local_sem).wait()

    shp = jax.ShapeDtypeStruct((4, rtot, c), dt)
    return pl.pallas_call(
        body, name="rs_sibling",
        in_specs=[ANY] * ns, out_specs=[ANY, ANY], out_shape=[shp, shp],
        scratch_shapes=[pltpu.SemaphoreType.DMA, pltpu.SemaphoreType.DMA, pltpu.SemaphoreType.DMA],
    )(*fulls)


def _rs_chips(pbf, p32):
    _, rtot, c = pbf.shape

    def body(pbf_ref, p32_ref, got_ref, own_ref, send_sems, recv_sems, local_sem):
        x, y, cc = _mesh_pos()
        chips = [(1 - x, y), (x, 1 - y), (1 - x, 1 - y)]
        keep = pltpu.make_async_copy(p32_ref.at[2 * x + y], own_ref, local_sem)
        keep.start()
        cps = [_remote(pbf_ref.at[2 * cx + cy], got_ref.at[j], send_sems.at[j], recv_sems.at[j], (cx, cy, cc))
               for j, (cx, cy) in enumerate(chips)]
        for cp in cps:
            cp.start()
        for cp in cps:
            cp.wait()
        keep.wait()

    return pl.pallas_call(
        body, name="rs_chips",
        in_specs=[ANY, ANY], out_specs=[ANY, ANY],
        out_shape=[jax.ShapeDtypeStruct((3, rtot, c), BF16), jax.ShapeDtypeStruct((rtot, c), F32)],
        scratch_shapes=[pltpu.SemaphoreType.DMA((3,)), pltpu.SemaphoreType.DMA((3,)), pltpu.SemaphoreType.DMA],
    )(pbf, p32)


def _tp(w):
    return jnp.swapaxes(w, -1, -2)


def _block_diag(blocks):
    g, r, c = blocks.shape
    eye = jnp.eye(g, dtype=blocks.dtype)
    return (blocks[:, :, None, :] * eye[:, None, :, None]).reshape(g * r, g * c)


def _diag_blocks(full, r, c):
    g = full.shape[0] // r
    return jnp.einsum('grgc->grc', full.reshape(g, r, g, c))


def _s5_prepare(a_re, a_im, log_dt, b_re, b_im, c_re, c_im):
    a = jnp.stack([a_re, a_im])
    ldt = jnp.broadcast_to(log_dt[:, None], (SSM_GROUPS, SSM_STATE))
    a_rep = jnp.repeat(a, SSM_GROUP, axis=-1)
    ldt_rep = jnp.broadcast_to(log_dt[:, None], (SSM_GROUPS, SSM_STATE * SSM_GROUP))
    b = jnp.stack([b_re.reshape(SSM_GROUPS, -1), b_im.reshape(SSM_GROUPS, -1)])
    disc_in = (a, ldt, a_rep, ldt_rep, b)
    pw, bb = _s5_disc(*disc_in)
    pr = pw[:8].reshape(8, N_STATE)
    pi = pw[8:].reshape(8, N_STATE)

    def table(pr, pi, edge):
        rows = [jnp.broadcast_to(jnp.concatenate([pr[k], pi[k]])[None], (8, 2 * N_STATE)) for k in (0, 1, 3)]
        return jnp.concatenate(rows + [edge], axis=0)

    ltab = table(pr, pi, jnp.concatenate([pr, pi], axis=1))
    ltab_rev = table(pr, -pi, jnp.concatenate([pr[::-1], -pi[::-1]], axis=1))
    bb4 = bb.reshape(2, SSM_GROUPS, SSM_STATE, SSM_GROUP)
    bbmat = jnp.concatenate([_block_diag(_tp(bb4[0])), _block_diag(_tp(bb4[1]))], axis=1).astype(BF16)
    ccmat = jnp.concatenate([_block_diag(_tp(c_re)), -_block_diag(_tp(c_im))], axis=0).astype(BF16)
    return disc_in, ltab, ltab_rev, bbmat, ccmat


def _layer_fwd(h, p_l, small, big):
    saved = {'h0': h}
    h, saved['gu1'] = _ffn_fwd(h, small['ffn1_norm'], big['ff1'])
    saved['h1'] = h
    z = _inproj_fwd(h, small['mix_norm'], big['wint'])
    ya, ys, hs = _s5conv_fwd(z, small['conv_w'], small['conv_b'], small['bbmat'], small['ccmat'], small['dvec'],
                             small['ltab'])
    saved.update(z=z, ya=ya, ys=ys, hs=hs)
    h = _mix_out_fwd(h, ya, ys, big['glu'], small['glu_b'], small['conv_out_norm'], small['ssm_out_norm'], big['wout'])
    saved['h2'] = h
    h, saved['gu2'] = _ffn_fwd(h, small['ffn2_norm'], big['ff2'])
    saved['h3'] = h
    h = _ple_fwd(h, small['ple_norm'], p_l, big['plg'], big['plpt'])
    return h, saved


def _ffn_bwd(h_in, g, dh, gu, w3):
    dh_in, dga, ud, dg = _ffn_bwd_act(h_in, g, dh, gu, w3)
    return dh_in, _matmul_tn(dga, ud, FF_BLOCK, BF16, "ffn_wgrad"), dg


def _layer_bwd(dh, p_l, small, big, saved):
    gs = {}
    dh, u, dq, dpp, pb, gs['ple_norm'] = _ple_bwd(saved['h3'], small['ple_norm'], p_l, dh, big['plg'], big['plpt'])
    d_plg = _matmul_tn(u, dq, 256, BF16, "ple_gate_wgrad")
    d_plpt = _matmul_tn(dpp, pb, 256, BF16, "ple_proj_wgrad")
    dh, d_ff2, gs['ffn2_norm'] = _ffn_bwd(saved['h2'], small['ffn2_norm'], dh, saved['gu2'], big['ff2'])

    dya, dys, ycat, dhb, zg, dq, part = _mix_out_bwd(dh, saved['ya'], saved['ys'], big['glu'], small['glu_b'],
                                                     small['conv_out_norm'], small['ssm_out_norm'], big['wout'])
    d_wout = _matmul_tn(ycat, dhb, 256, BF16, "w_out_wgrad")
    d_glu = _matmul_tn(zg, dq, 256, BF16, "glu_wgrad")
    dz, gadj, us, dyb, dl, dcw = _s5conv_bwd(saved['z'], saved['hs'], dya, dys, small['conv_w'], small['conv_b'],
                                             small['bbmat'], small['ccmat'], small['dvec'], small['ltab_rev'])
    d_bbt = _matmul_tn(gadj, us, 512, F32, "s5_b_wgrad")[0]
    d_cc = _matmul_tn(saved['hs'][None], dyb, 512, F32, "s5_c_wgrad")[0]
    dh, u, gs['mix_norm'] = _inproj_bwd(saved['h1'], small['mix_norm'], dh, dz, big['wint'])
    d_wint = _matmul_tn(dz[None], u, 256, BF16, "w_in_wgrad")
    dh, d_ff1, gs['ffn1_norm'] = _ffn_bwd(saved['h0'], small['ffn1_norm'], dh, saved['gu1'], big['ff1'])

    dbb = jnp.stack([_diag_blocks(d_bbt[:N_STATE], SSM_STATE, SSM_GROUP).reshape(SSM_GROUPS, -1),
                     _diag_blocks(d_bbt[N_STATE:], SSM_STATE, SSM_GROUP).reshape(SSM_GROUPS, -1)])
    dlb = dl[0].reshape(2, SSM_GROUPS, SSM_STATE)
    sel = jnp.repeat(jnp.eye(SSM_STATE, dtype=BF16), SSM_GROUP, axis=0)
    da, dldt, db = _s5_disc_bwd(*small['disc_in'], dlb, dbb, sel)
    gs['ssm_A_re'], gs['ssm_A_im'] = da[0], da[1]
    gs['ssm_log_dt'] = dldt[:, 0]
    gs['ssm_B_re'] = db[0].reshape(SSM_GROUPS, SSM_STATE, SSM_GROUP)
    gs['ssm_B_im'] = db[1].reshape(SSM_GROUPS, SSM_STATE, SSM_GROUP)
    gs['ssm_C_re'] = _tp(_diag_blocks(d_cc[:N_STATE], SSM_STATE, SSM_GROUP))
    gs['ssm_C_im'] = -_tp(_diag_blocks(d_cc[N_STATE:], SSM_STATE, SSM_GROUP))
    gs['conv_w'] = dcw[0:3]
    gs['conv_b'] = dcw[3]
    gs['ssm_D'] = dcw[4].reshape(SSM_GROUPS, SSM_GROUP)
    gs['conv_out_norm'], gs['ssm_out_norm'], gs['glu_b'] = part[0], part[1], part[2]
    for n in ('ple_norm', 'ffn2_norm', 'mix_norm', 'ffn1_norm'):
        gs[n] = gs[n][0]
    fulls = [d_ff1, d_ff2, d_wint, d_wout, d_plg,
             d_plpt.reshape(1, D_MODEL * PLE_DIM // D_MODEL, D_MODEL), d_glu.reshape(1, SSM_W * SSM_W // D_MODEL, D_MODEL)]
    return dh, fulls, gs


def _pad_rows(flat, mult):
    per = mult * LANES
    n = flat.shape[0]
    tot = -(-n // per) * per
    return jnp.pad(flat, (0, tot - n)).reshape(tot // LANES, LANES)


def _adamw_any(w, g, m, v):
    shp = w.shape
    two = (lambda t: t.reshape(-1, shp[-1]))
    d, nm, nv = _adamw(two(w), two(g), two(m), two(v))
    return d.reshape(shp), nm.reshape(shp), nv.reshape(shp)


def kernel(x, p, ffn1_norm, ffn1_w_gate, ffn1_w_up, ffn1_w_down, mix_norm, w_in, conv_w, conv_b, ssm_A_re, ssm_A_im, ssm_B_re, ssm_B_im, ssm_C_re, ssm_C_im, ssm_D, ssm_log_dt, glu_w, glu_b, conv_out_norm, ssm_out_norm, w_out, ffn2_norm, ffn2_w_gate, ffn2_w_up, ffn2_w_down, ple_norm, ple_w_gate, ple_w_proj, final_norm, loss_target, m_ffn1_norm, m_ffn1_w_gate, m_ffn1_w_up, m_ffn1_w_down, m_mix_norm, m_w_in, m_conv_w, m_conv_b, m_ssm_A_re, m_ssm_A_im, m_ssm_B_re, m_ssm_B_im, m_ssm_C_re, m_ssm_C_im, m_ssm_D, m_ssm_log_dt, m_glu_w, m_glu_b, m_conv_out_norm, m_ssm_out_norm, m_w_out, m_ffn2_norm, m_ffn2_w_gate, m_ffn2_w_up, m_ffn2_w_down, m_ple_norm, m_ple_w_gate, m_ple_w_proj, m_final_norm, v_ffn1_norm, v_ffn1_w_gate, v_ffn1_w_up, v_ffn1_w_down, v_mix_norm, v_w_in, v_conv_w, v_conv_b, v_ssm_A_re, v_ssm_A_im, v_ssm_B_re, v_ssm_B_im, v_ssm_C_re, v_ssm_C_im, v_ssm_D, v_ssm_log_dt, v_glu_w, v_glu_b, v_conv_out_norm, v_ssm_out_norm, v_w_out, v_ffn2_norm, v_ffn2_w_gate, v_ffn2_w_up, v_ffn2_w_down, v_ple_norm, v_ple_w_gate, v_ple_w_proj, v_final_norm):
    given = dict(locals())
    W = {n: given[n] for n in W_NAMES}
    M = {n: given['m_' + n] for n in W_NAMES}
    V = {n: given['v_' + n] for n in W_NAMES}
    my_dev = _dev_index(_mesh_pos())

    conv_shard = _pad_rows(W['conv_w'].reshape(-1), SUBLANES)
    conv_all = _allgather(conv_shard, ((1, SUBLANES),), "ag_conv_w")[0]
    conv_full = conv_all.reshape(N_DEV, -1)[:, :DEPTH * 3 * (CONV_W // N_DEV)]
    conv_full = conv_full.reshape(N_DEV, DEPTH, 3, CONV_W // N_DEV).transpose(1, 2, 0, 3).reshape(DEPTH, 3, CONV_W)

    bigs = []
    for l in range(DEPTH):
        pack = jnp.concatenate([
            _tp(W['ffn1_w_gate'][l]), _tp(W['ffn1_w_up'][l]), W['ffn1_w_down'][l],
            _tp(W['ffn2_w_gate'][l]), _tp(W['ffn2_w_up'][l]), W['ffn2_w_down'][l],
            _tp(W['w_in'][l]), W['w_out'][l], W['ple_w_gate'][l],
            _tp(W['ple_w_proj'][l]).reshape(-1, D_MODEL), W['glu_w'][l].reshape(-1, D_MODEL)], axis=0).astype(BF16)
        ff1, ff2, wint, wout, plg, plpt, glu = _allgather(pack, SEGS, "ag_weights")
        bigs.append(dict(ff1=ff1, ff2=ff2, wint=wint[0], wout=wout[0], plg=plg[0],
                         plpt=plpt.reshape(D_MODEL, PLE_DIM), glu=glu.reshape(SSM_W, SSM_W)))

    smalls, saves = [], []
    h = x[0]
    for l in range(DEPTH):
        small = {n: W[n][l][None] for n in ('ffn1_norm', 'mix_norm', 'conv_b', 'glu_b', 'conv_out_norm',
                                            'ssm_out_norm', 'ffn2_norm', 'ple_norm')}
        small['conv_w'] = conv_full[l]
        small['dvec'] = W['ssm_D'][l].reshape(1, SSM_W)
        (small['disc_in'], small['ltab'], small['ltab_rev'], small['bbmat'], small['ccmat']) = _s5_prepare(
            W['ssm_A_re'][l], W['ssm_A_im'][l], W['ssm_log_dt'][l], W['ssm_B_re'][l], W['ssm_B_im'][l],
            W['ssm_C_re'][l], W['ssm_C_im'][l])
        h, saved = _layer_fwd(h, p[l, 0], small, bigs[l])
        smalls.append(small)
        saves.append(saved)
    loss_tile, dh, d_final = _final_loss(h, W['final_norm'][None], loss_target[0])
    loss = lax.psum(loss_tile[0, 0], ("x", "y", "c"))

    layer_gs = [None] * DEPTH
    shard_grads = [None] * DEPTH
    for l in reversed(range(DEPTH)):
        dh, fulls, layer_gs[l] = _layer_bwd(dh, p[l, 0], smalls[l], bigs[l], saves[l])
        own, got = _rs_sibling(fulls, SEGS)
        p32, pbf = _pair_sum(own, got)
        got3, mine = _rs_chips(pbf, p32)
        shard_grads[l] = _chip_sum(mine, got3)
    grad_x = dh[None]

    gs = {n: jnp.stack([layer_gs[l][n] for l in range(DEPTH)]) for n in layer_gs[0]}
    gs['final_norm'] = d_final[0]
    flat = jnp.concatenate([gs[n].reshape(-1) for n in SMALL_NAMES] + [gs['conv_w'].reshape(-1)])
    n_flat = flat.shape[0]
    flat = _pad_rows(flat, SUBLANES)
    rows = flat.shape[0]
    gathered = _allgather(flat, ((1, rows),), "ag_small_grads")[0]
    red = _sum8(gathered.reshape(N_DEV, rows, LANES)).reshape(-1)[:n_flat]
    G = {}
    o = 0
    for n in SMALL_NAMES:
        G[n] = red[o:o + W[n].size].reshape(W[n].shape)
        o += W[n].size
    conv_g_full = red[o:].reshape(DEPTH, 3, CONV_W)
    G['conv_w'] = lax.dynamic_slice_in_dim(conv_g_full, my_dev * (CONV_W // N_DEV), CONV_W // N_DEV, axis=2)

    sg = jnp.stack(shard_grads)
    offs = _seg_offsets(SEGS)
    r = SEGS[0][1]
    for a, f in ((0, 'ffn1'), (1, 'ffn2')):
        G[f + '_w_gate'] = _tp(sg[:, offs[a]:offs[a] + r])
        G[f + '_w_up'] = _tp(sg[:, offs[a] + r:offs[a] + 2 * r])
        G[f + '_w_down'] = sg[:, offs[a] + 2 * r:offs[a] + 3 * r]
    G['w_in'] = _tp(sg[:, offs[2]:offs[2] + SEGS[2][1]])
    G['w_out'] = sg[:, offs[3]:offs[3] + SEGS[3][1]]
    G['ple_w_gate'] = sg[:, offs[4]:offs[4] + SEGS[4][1]]
    G['ple_w_proj'] = _tp(sg[:, offs[5]:offs[5] + SEGS[5][1]].reshape(DEPTH, D_MODEL // N_DEV, PLE_DIM))
    G['glu_w'] = sg[:, offs[6]:offs[6] + SEGS[6][1]].reshape(DEPTH, SSM_W // N_DEV, SSM_W)

    delta, new_m, new_v = {}, {}, {}
    cat = lambda src: _pad_rows(jnp.concatenate([src[n].reshape(-1) for n in SMALL_NAMES]), SUBLANES)
    d_s, m_s, v_s = _adamw(cat(W), cat(G), cat(M), cat(V))
    o = 0
    for n in SMALL_NAMES:
        for dst, src in ((delta, d_s), (new_m, m_s), (new_v, v_s)):
            dst[n] = src.reshape(-1)[o:o + W[n].size].reshape(W[n].shape)
        o += W[n].size
    for n in W_NAMES:
        if n not in delta:
            delta[n], new_m[n], new_v[n] = _adamw_any(W[n], G[n], M[n], V[n])

    return (loss, grad_x, *[G[n] for n in W_NAMES], *[delta[n] for n in W_NAMES],
            *[new_m[n] for n in W_NAMES], *[new_v[n] for n in W_NAMES])
```

```python
import math

import jax
import jax.numpy as jnp
from jax import lax
from jax.experimental import pallas as pl
from jax.experimental.pallas import tpu as pltpu

F32 = jnp.float32
BF16 = jnp.bfloat16

N_DEV = 8
DEPTH = 4
SEQ = 2048
D_MODEL = 1024
D_FF = 2816
CONV_W = 512
SSM_W = 512
SSM_GROUPS = 32
SSM_GROUP = 16
SSM_STATE = 64
N_STATE = SSM_GROUPS * SSM_STATE
IN_COLS = 2048
PLE_DIM = 256
EPS = 1e-6

ADAM_LR = 0.001
ADAM_B1 = 0.9
ADAM_B2 = 0.999
ADAM_EPS = 1e-08
ADAM_WD = 0.01
ADAM_STEP = 10

FF_BLOCK = 256
N_FF_BLOCKS = D_FF // FF_BLOCK
TOK_TILE_FFN = 1024
TOK_TILE = 512
CHUNK = 256
N_CHUNKS = SEQ // CHUNK
LANE_GROUP = 512
SUBLANES = 8
LANES = 128
MIB = 1024 * 1024

W_NAMES = ['ffn1_norm', 'ffn1_w_gate', 'ffn1_w_up', 'ffn1_w_down', 'mix_norm', 'w_in', 'conv_w', 'conv_b',
           'ssm_A_re', 'ssm_A_im', 'ssm_B_re', 'ssm_B_im', 'ssm_C_re', 'ssm_C_im', 'ssm_D', 'ssm_log_dt',
           'glu_w', 'glu_b', 'conv_out_norm', 'ssm_out_norm', 'w_out', 'ffn2_norm', 'ffn2_w_gate', 'ffn2_w_up',
           'ffn2_w_down', 'ple_norm', 'ple_w_gate', 'ple_w_proj', 'final_norm']
SMALL_NAMES = ['ffn1_norm', 'mix_norm', 'conv_b', 'ssm_A_re', 'ssm_A_im', 'ssm_B_re', 'ssm_B_im', 'ssm_C_re',
               'ssm_C_im', 'ssm_D', 'ssm_log_dt', 'glu_b', 'conv_out_norm', 'ssm_out_norm', 'ffn2_norm',
               'ple_norm', 'final_norm']

SEGS = ((3, 352), (3, 352), (1, 256), (1, 128), (1, 128), (1, 32), (1, 32))
PACK_ROWS = sum(n * r for n, r in SEGS)

MESH = pl.DeviceIdType.MESH
ANY = pl.BlockSpec(memory_space=pl.ANY)


def _cparams(sem=None, vmem_mib=48, **kw):
    return pltpu.CompilerParams(dimension_semantics=sem, vmem_limit_bytes=vmem_mib * MIB, **kw)


def _dot(a, b):
    return jnp.dot(a, b, preferred_element_type=F32)


def _dot_nt(a, b):
    return lax.dot_general(a, b, (((1,), (1,)), ((), ())), preferred_element_type=F32)


def _dot_tn(a, b):
    return lax.dot_general(a, b, (((0,), (0,)), ((), ())), preferred_element_type=F32)


def _rms_stats(x):
    r = lax.rsqrt(jnp.mean(x * x, axis=-1, keepdims=True) + EPS)
    return x * r, r


def _rms_bwd(dy, xh, r, g):
    dxh = dy * g
    dx = r * (dxh - xh * jnp.mean(dxh * xh, axis=-1, keepdims=True))
    dg = jnp.sum(dy * xh, axis=0, keepdims=True)
    return dx, dg


def _sigmoid(x):
    return 1.0 / (1.0 + jnp.exp(-x))


_GELU_C = math.sqrt(2.0 / math.pi)


def _gelu(x):
    t = jnp.tanh(_GELU_C * (x + 0.044715 * x * x * x))
    return 0.5 * x * (1.0 + t), t


def _gelu_grad(x, t):
    return 0.5 * (1.0 + t) + 0.5 * x * (1.0 - t * t) * _GELU_C * (1.0 + 3.0 * 0.044715 * x * x)


def _accumulate(ref, first, value):
    @pl.when(first)
    def _():
        ref[...] = value

    @pl.when(jnp.logical_not(first))
    def _():
        ref[...] += value


def _ffn_fwd(h, g, w3):
    tm = TOK_TILE_FFN

    def body(h_ref, g_ref, w_ref, out_ref, gu_ref, u_ref):
        k = pl.program_id(1)

        @pl.when(k == 0)
        def _():
            x = h_ref[...]
            xh, _ = _rms_stats(x)
            u_ref[...] = (xh * g_ref[...]).astype(BF16)
            out_ref[...] = x

        u = u_ref[...]
        gate = _dot_nt(u, w_ref[0])
        up = _dot_nt(u, w_ref[1])
        a = gate * _sigmoid(gate) * up
        gu_ref[0] = gate.astype(BF16)
        gu_ref[1] = up.astype(BF16)
        out_ref[...] += 0.5 * _dot(a.astype(BF16), w_ref[2])

    return pl.pallas_call(
        body, name="ffn_fwd",
        grid=(SEQ // tm, N_FF_BLOCKS),
        in_specs=[pl.BlockSpec((tm, D_MODEL), lambda m, k: (m, 0)),
                  pl.BlockSpec((1, D_MODEL), lambda m, k: (0, 0)),
                  pl.BlockSpec((3, FF_BLOCK, D_MODEL), lambda m, k: (0, k, 0))],
        out_specs=[pl.BlockSpec((tm, D_MODEL), lambda m, k: (m, 0)),
                   pl.BlockSpec((2, tm, FF_BLOCK), lambda m, k: (0, m, k))],
        out_shape=[jax.ShapeDtypeStruct((SEQ, D_MODEL), F32),
                   jax.ShapeDtypeStruct((2, SEQ, D_FF), BF16)],
        scratch_shapes=[pltpu.VMEM((tm, D_MODEL), BF16)],
        compiler_params=_cparams(("parallel", "arbitrary")),
    )(h, g, w3)


def _ffn_bwd_act(h, g, dout, gu, w3):
    tm = TOK_TILE

    def body(h_ref, g_ref, d_ref, gu_ref, w_ref, dh_ref, dga_ref, ud_ref, dg_ref, acc_ref):
        m = pl.program_id(0)
        k = pl.program_id(1)

        @pl.when(k == 0)
        def _():
            xh, _ = _rms_stats(h_ref[...])
            ud_ref[0] = (xh * g_ref[...]).astype(BF16)
            ud_ref[1] = (0.5 * d_ref[...]).astype(BF16)
            acc_ref[...] = jnp.zeros_like(acc_ref)

        gate = gu_ref[0].astype(F32)
        up = gu_ref[1].astype(F32)
        sg = _sigmoid(gate)
        silu = gate * sg
        da = _dot_nt(ud_ref[1], w_ref[2])
        dgate = (da * up * sg * (1.0 + gate * (1.0 - sg))).astype(BF16)
        dup = (da * silu).astype(BF16)
        dga_ref[0] = dgate
        dga_ref[1] = dup
        dga_ref[2] = (silu * up).astype(BF16)
        acc_ref[...] += _dot(dgate, w_ref[0]) + _dot(dup, w_ref[1])

        @pl.when(k == N_FF_BLOCKS - 1)
        def _():
            xh, r = _rms_stats(h_ref[...])
            dx, dg = _rms_bwd(acc_ref[...], xh, r, g_ref[...])
            dh_ref[...] = d_ref[...] + dx
            _accumulate(dg_ref, m == 0, dg)

    return pl.pallas_call(
        body, name="ffn_bwd_act",
        grid=(SEQ // tm, N_FF_BLOCKS),
        in_specs=[pl.BlockSpec((tm, D_MODEL), lambda m, k: (m, 0)),
                  pl.BlockSpec((1, D_MODEL), lambda m, k: (0, 0)),
                  pl.BlockSpec((tm, D_MODEL), lambda m, k: (m, 0)),
                  pl.BlockSpec((2, tm, FF_BLOCK), lambda m, k: (0, m, k)),
                  pl.BlockSpec((3, FF_BLOCK, D_MODEL), lambda m, k: (0, k, 0))],
        out_specs=[pl.BlockSpec((tm, D_MODEL), lambda m, k: (m, 0)),
                   pl.BlockSpec((3, tm, FF_BLOCK), lambda m, k: (0, m, k)),
                   pl.BlockSpec((2, tm, D_MODEL), lambda m, k: (0, m, 0)),
                   pl.BlockSpec((1, D_MODEL), lambda m, k: (0, 0))],
        out_shape=[jax.ShapeDtypeStruct((SEQ, D_MODEL), F32),
                   jax.ShapeDtypeStruct((3, SEQ, D_FF), BF16),
                   jax.ShapeDtypeStruct((2, SEQ, D_MODEL), BF16),
                   jax.ShapeDtypeStruct((1, D_MODEL), F32)],
        scratch_shapes=[pltpu.VMEM((tm, D_MODEL), F32)],
        compiler_params=_cparams(("arbitrary", "arbitrary")),
    )(h, g, dout, gu, w3)


def _matmul_tn(a, b, bm, out_dtype, name):
    na, t, m = a.shape
    nb, _, n = b.shape

    def body(a_ref, b_ref, o_ref):
        o_ref[0] = _dot_tn(a_ref[0], b_ref[0]).astype(out_dtype)

    return pl.pallas_call(
        body, name=name,
        grid=(na, m // bm),
        in_specs=[pl.BlockSpec((1, t, bm), lambda i, k: (i, 0, k)),
                  pl.BlockSpec((1, t, n), lambda i, k: (jnp.maximum(i - (na - nb), 0), 0, 0))],
        out_specs=pl.BlockSpec((1, bm, n), lambda i, k: (i, k, 0)),
        out_shape=jax.ShapeDtypeStruct((na, m, n), out_dtype),
        compiler_params=_cparams(("arbitrary", "parallel")),
    )(a, b)


def _inproj_fwd(h, g, wint):
    tm = TOK_TILE

    def body(h_ref, g_ref, w_ref, z_ref):
        xh, _ = _rms_stats(h_ref[...])
        z_ref[...] = _dot_nt((xh * g_ref[...]).astype(BF16), w_ref[...])

    return pl.pallas_call(
        body, name="inproj_fwd",
        grid=(SEQ // tm,),
        in_specs=[pl.BlockSpec((tm, D_MODEL), lambda m: (m, 0)),
                  pl.BlockSpec((1, D_MODEL), lambda m: (0, 0)),
                  pl.BlockSpec((IN_COLS, D_MODEL), lambda m: (0, 0))],
        out_specs=pl.BlockSpec((tm, IN_COLS), lambda m: (m, 0)),
        out_shape=jax.ShapeDtypeStruct((SEQ, IN_COLS), F32),
        compiler_params=_cparams(("parallel",)),
    )(h, g, wint)


def _inproj_bwd(h, g, dh, dz, wint):
    tm = TOK_TILE

    def body(h_ref, g_ref, dh_ref, dz_ref, w_ref, o_ref, u_ref, dg_ref):
        xh, r = _rms_stats(h_ref[...])
        u_ref[0] = (xh * g_ref[...]).astype(BF16)
        dx, dg = _rms_bwd(_dot(dz_ref[...], w_ref[...]), xh, r, g_ref[...])
        o_ref[...] = dh_ref[...] + dx
        _accumulate(dg_ref, pl.program_id(0) == 0, dg)

    return pl.pallas_call(
        body, name="inproj_bwd",
        grid=(SEQ // tm,),
        in_specs=[pl.BlockSpec((tm, D_MODEL), lambda m: (m, 0)),
                  pl.BlockSpec((1, D_MODEL), lambda m: (0, 0)),
                  pl.BlockSpec((tm, D_MODEL), lambda m: (m, 0)),
                  pl.BlockSpec((tm, IN_COLS), lambda m: (m, 0)),
                  pl.BlockSpec((IN_COLS, D_MODEL), lambda m: (0, 0))],
        out_specs=[pl.BlockSpec((tm, D_MODEL), lambda m: (m, 0)),
                   pl.BlockSpec((1, tm, D_MODEL), lambda m: (0, m, 0)),
                   pl.BlockSpec((1, D_MODEL), lambda m: (0, 0))],
        out_shape=[jax.ShapeDtypeStruct((SEQ, D_MODEL), F32),
                   jax.ShapeDtypeStruct((1, SEQ, D_MODEL), BF16),
                   jax.ShapeDtypeStruct((1, D_MODEL), F32)],
        compiler_params=_cparams(("arbitrary",)),
    )(h, g, dh, dz, wint)


def _row_ids(n, w):
    return lax.broadcasted_iota(jnp.int32, (n, w), 0)


def _bcast_row(x, i, n):
    return jnp.broadcast_to(x[i:i + 1, :], (n, x.shape[1]))


def _conv_taps(v, tail):
    n, w = v.shape
    rid = _row_ids(n, w)
    v1 = jnp.where(rid == 0, _bcast_row(tail, 7, n), pltpu.roll(v, 1, 0))
    v2 = jnp.where(rid == 0, _bcast_row(tail, 6, n),
                   jnp.where(rid == 1, _bcast_row(tail, 7, n), pltpu.roll(v, 2, 0)))
    return v1, v2


def _scan_chunk(work, ltab, carry, reverse):
    nblk = CHUNK // SUBLANES
    row = _row_ids(SUBLANES, LANE_GROUP)
    for gi in range(N_STATE // LANE_GROUP):
        cre = pl.ds(gi * LANE_GROUP, LANE_GROUP)
        cim = pl.ds(N_STATE + gi * LANE_GROUP, LANE_GROUP)
        pows = [(ltab[8 * k:8 * k + 8, cre], ltab[8 * k:8 * k + 8, cim]) for k in range(3)]
        pr = ltab[24:32, cre]
        pi = ltab[24:32, cim]

        def blk(i, c, cre=cre, cim=cim, pows=pows, pr=pr, pi=pi):
            cr, ci = c
            b = (nblk - 1 - i) if reverse else i
            r0 = pl.multiple_of(b * SUBLANES, SUBLANES)
            xr = work[pl.ds(r0, SUBLANES), cre]
            xi = work[pl.ds(r0, SUBLANES), cim]
            for k, s in enumerate((1, 2, 4)):
                lr, li = pows[k]
                if reverse:
                    keep = row < SUBLANES - s
                    sr = jnp.where(keep, pltpu.roll(xr, SUBLANES - s, 0), 0.0)
                    si = jnp.where(keep, pltpu.roll(xi, SUBLANES - s, 0), 0.0)
                else:
                    keep = row >= s
                    sr = jnp.where(keep, pltpu.roll(xr, s, 0), 0.0)
                    si = jnp.where(keep, pltpu.roll(xi, s, 0), 0.0)
                xr, xi = xr + lr * sr - li * si, xi + lr * si + li * sr
            xr, xi = xr + pr * cr - pi * ci, xi + pr * ci + pi * cr
            work[pl.ds(r0, SUBLANES), cre] = xr
            work[pl.ds(r0, SUBLANES), cim] = xi
            edge = 0 if reverse else SUBLANES - 1
            return _bcast_row(xr, edge, SUBLANES), _bcast_row(xi, edge, SUBLANES)

        cr, ci = lax.fori_loop(0, nblk, blk, (carry[:, cre], carry[:, cim]))
        carry[:, cre] = cr
        carry[:, cim] = ci


def _s5conv_fwd(z, convw, convb, bbmat, ccmat, dvec, ltab):
    def body(z_ref, cw_ref, cb_ref, bb_ref, cc_ref, d_ref, lt_ref, ya_ref, ys_ref, hs_ref,
             work, carry, tail):
        c = pl.program_id(0)

        @pl.when(c == 0)
        def _():
            carry[...] = jnp.zeros_like(carry)
            tail[...] = jnp.zeros_like(tail)

        zb = z_ref[:, 0:CONV_W]
        v = z_ref[:, CONV_W:2 * CONV_W] * z_ref[:, 2 * CONV_W:3 * CONV_W]
        us = z_ref[:, 3 * CONV_W:4 * CONV_W]
        v1, v2 = _conv_taps(v, tail[...])
        tail[...] = v[CHUNK - 8:CHUNK, :]
        y = cw_ref[0:1, :] * v2 + cw_ref[1:2, :] * v1 + cw_ref[2:3, :] * v
        ya_ref[...] = zb * (y + cb_ref[...])

        work[...] = _dot(us.astype(BF16), bb_ref[...])
        _scan_chunk(work, lt_ref, carry, reverse=False)
        hs = work[...].astype(BF16)
        hs_ref[...] = hs
        ys_ref[...] = _dot(hs, cc_ref[...]) + d_ref[...] * us

    return pl.pallas_call(
        body, name="s5conv_fwd",
        grid=(N_CHUNKS,),
        in_specs=[pl.BlockSpec((CHUNK, IN_COLS), lambda c: (c, 0)),
                  pl.BlockSpec((3, CONV_W), lambda c: (0, 0)),
                  pl.BlockSpec((1, CONV_W), lambda c: (0, 0)),
                  pl.BlockSpec((SSM_W, 2 * N_STATE), lambda c: (0, 0)),
                  pl.BlockSpec((2 * N_STATE, SSM_W), lambda c: (0, 0)),
                  pl.BlockSpec((1, SSM_W), lambda c: (0, 0)),
                  pl.BlockSpec((32, 2 * N_STATE), lambda c: (0, 0))],
        out_specs=[pl.BlockSpec((CHUNK, CONV_W), lambda c: (c, 0)),
                   pl.BlockSpec((CHUNK, SSM_W), lambda c: (c, 0)),
                   pl.BlockSpec((CHUNK, 2 * N_STATE), lambda c: (c, 0))],
        out_shape=[jax.ShapeDtypeStruct((SEQ, CONV_W), F32),
                   jax.ShapeDtypeStruct((SEQ, SSM_W), F32),
                   jax.ShapeDtypeStruct((SEQ, 2 * N_STATE), BF16)],
        scratch_shapes=[pltpu.VMEM((CHUNK, 2 * N_STATE), F32),
                        pltpu.VMEM((8, 2 * N_STATE), F32),
                        pltpu.VMEM((8, CONV_W), F32)],
        compiler_params=_cparams(("arbitrary",)),
    )(z, convw, convb, bbmat, ccmat, dvec, ltab)


def _s5conv_bwd(z, hs, dya, dys, convw, convb, bbmat, ccmat, dvec, ltab_rev):
    nc = N_CHUNKS
    hb = 16

    def body(z_ref, zp_ref, hs_ref, hp_ref, dya_ref, dys_ref, cw_ref, cb_ref, bb_ref, cc_ref, d_ref, lt_ref,
             dz_ref, g_ref, us_ref, dyb_ref, dl_ref, dcw_ref, work, carry, head):
        i = pl.program_id(0)
        first_chunk = i == nc - 1

        @pl.when(i == 0)
        def _():
            carry[...] = jnp.zeros_like(carry)
            head[...] = jnp.zeros_like(head)
            dl_ref[...] = jnp.zeros_like(dl_ref)
            dcw_ref[...] = jnp.zeros_like(dcw_ref)

        us = z_ref[:, 3 * CONV_W:4 * CONV_W]
        dy = dys_ref[...]
        dy_bf = dy.astype(BF16)
        us_ref[0] = us.astype(BF16)
        dyb_ref[0] = dy_bf

        work[...] = _dot_nt(dy_bf, cc_ref[...])
        _scan_chunk(work, lt_ref, carry, reverse=True)
        gg = work[...]
        gg_bf = gg.astype(BF16)
        g_ref[0] = gg_bf
        dus = d_ref[...] * dy + _dot_nt(gg_bf, bb_ref[...])

        hcur = hs_ref[...].astype(F32)
        hlast = hp_ref[...].astype(F32)[hb - 1:hb, :]
        hlast = jnp.where(first_chunk, 0.0, hlast)
        rid = _row_ids(CHUNK, 2 * N_STATE)
        hprev = jnp.where(rid == 0, jnp.broadcast_to(hlast, (CHUNK, 2 * N_STATE)), pltpu.roll(hcur, 1, 0))
        gr, gi = gg[:, :N_STATE], gg[:, N_STATE:]
        hr, hi = hprev[:, :N_STATE], hprev[:, N_STATE:]
        dl_ref[:, :N_STATE] += (gr * hr + gi * hi).reshape(CHUNK // 8, 8, N_STATE).sum(axis=0)
        dl_ref[:, N_STATE:] += (gi * hr - gr * hi).reshape(CHUNK // 8, 8, N_STATE).sum(axis=0)

        @pl.when(i == nc - 1)
        def _():
            dl_ref[0:1, :] = jnp.sum(dl_ref[...], axis=0, keepdims=True)

        zb = z_ref[:, 0:CONV_W]
        zc = z_ref[:, CONV_W:2 * CONV_W]
        zv = z_ref[:, 2 * CONV_W:3 * CONV_W]
        v = zc * zv
        vtail = jnp.where(first_chunk, 0.0, zp_ref[:, CONV_W:2 * CONV_W] * zp_ref[:, 2 * CONV_W:3 * CONV_W])
        v1, v2 = _conv_taps(v, vtail)
        w0, w1, w2 = cw_ref[0:1, :], cw_ref[1:2, :], cw_ref[2:3, :]
        y = w0 * v2 + w1 * v1 + w2 * v
        dya_v = dya_ref[...]
        dzb = dya_v * (y + cb_ref[...])
        dyc = dya_v * zb
        hd = head[...]
        rc = _row_ids(CHUNK, CONV_W)
        n1 = jnp.where(rc == CHUNK - 1, _bcast_row(hd, 0, CHUNK), pltpu.roll(dyc, CHUNK - 1, 0))
        n2 = jnp.where(rc == CHUNK - 1, _bcast_row(hd, 1, CHUNK),
                       jnp.where(rc == CHUNK - 2, _bcast_row(hd, 0, CHUNK), pltpu.roll(dyc, CHUNK - 2, 0)))
        head[...] = dyc[0:8, :]
        dv = w2 * dyc + w1 * n1 + w0 * n2
        dz_ref[:, 0:CONV_W] = dzb.astype(BF16)
        dz_ref[:, CONV_W:2 * CONV_W] = (dv * zv).astype(BF16)
        dz_ref[:, 2 * CONV_W:3 * CONV_W] = (dv * zc).astype(BF16)
        dz_ref[:, 3 * CONV_W:4 * CONV_W] = dus.astype(BF16)
        dcw_ref[0:1, :] += jnp.sum(dyc * v2, axis=0, keepdims=True)
        dcw_ref[1:2, :] += jnp.sum(dyc * v1, axis=0, keepdims=True)
        dcw_ref[2:3, :] += jnp.sum(dyc * v, axis=0, keepdims=True)
        dcw_ref[3:4, :] += jnp.sum(dyc, axis=0, keepdims=True)
        dcw_ref[4:5, :] += jnp.sum(dy * us, axis=0, keepdims=True)

    rev = lambda i: nc - 1 - i
    return pl.pallas_call(
        body, name="s5conv_bwd",
        grid=(nc,),
        in_specs=[pl.BlockSpec((CHUNK, IN_COLS), lambda i: (rev(i), 0)),
                  pl.BlockSpec((8, IN_COLS), lambda i: (jnp.maximum(rev(i) * (CHUNK // 8) - 1, 0), 0)),
                  pl.BlockSpec((CHUNK, 2 * N_STATE), lambda i: (rev(i), 0)),
                  pl.BlockSpec((hb, 2 * N_STATE), lambda i: (jnp.maximum(rev(i) * (CHUNK // hb) - 1, 0), 0)),
                  pl.BlockSpec((CHUNK, CONV_W), lambda i: (rev(i), 0)),
                  pl.BlockSpec((CHUNK, SSM_W), lambda i: (rev(i), 0)),
                  pl.BlockSpec((3, CONV_W), lambda i: (0, 0)),
                  pl.BlockSpec((1, CONV_W), lambda i: (0, 0)),
                  pl.BlockSpec((SSM_W, 2 * N_STATE), lambda i: (0, 0)),
                  pl.BlockSpec((2 * N_STATE, SSM_W), lambda i: (0, 0)),
                  pl.BlockSpec((1, SSM_W), lambda i: (0, 0)),
                  pl.BlockSpec((32, 2 * N_STATE), lambda i: (0, 0))],
        out_specs=[pl.BlockSpec((CHUNK, IN_COLS), lambda i: (rev(i), 0)),
                   pl.BlockSpec((1, CHUNK, 2 * N_STATE), lambda i: (0, rev(i), 0)),
                   pl.BlockSpec((1, CHUNK, SSM_W), lambda i: (0, rev(i), 0)),
                   pl.BlockSpec((1, CHUNK, SSM_W), lambda i: (0, rev(i), 0)),
                   pl.BlockSpec((8, 2 * N_STATE), lambda i: (0, 0)),
                   pl.BlockSpec((8, CONV_W), lambda i: (0, 0))],
        out_shape=[jax.ShapeDtypeStruct((SEQ, IN_COLS), BF16),
                   jax.ShapeDtypeStruct((1, SEQ, 2 * N_STATE), BF16),
                   jax.ShapeDtypeStruct((1, SEQ, SSM_W), BF16),
                   jax.ShapeDtypeStruct((1, SEQ, SSM_W), BF16),
                   jax.ShapeDtypeStruct((8, 2 * N_STATE), F32),
                   jax.ShapeDtypeStruct((8, CONV_W), F32)],
        scratch_shapes=[pltpu.VMEM((CHUNK, 2 * N_STATE), F32),
                        pltpu.VMEM((8, 2 * N_STATE), F32),
                        pltpu.VMEM((8, CONV_W), F32)],
        compiler_params=_cparams(("arbitrary",)),
    )(z, z, hs, hs, dya, dys, convw, convb, bbmat, ccmat, dvec, ltab_rev)


def _mix_out_fwd(h, ya, ys, gluw, glub, con, son, wout):
    tm = TOK_TILE

    def body(h_ref, ya_ref, ys_ref, gw_ref, gb_ref, con_ref, son_ref, wo_ref, o_ref):
        zg, _ = _gelu(ys_ref[...])
        q = _dot(zg.astype(BF16), gw_ref[...]) + gb_ref[...]
        out_s = zg * _sigmoid(q)
        na, _ = _rms_stats(ya_ref[...])
        ns, _ = _rms_stats(out_s)
        o_ref[...] = (h_ref[...]
                      + _dot((na * con_ref[...]).astype(BF16), wo_ref[0:CONV_W, :])
                      + _dot((ns * son_ref[...]).astype(BF16), wo_ref[CONV_W:2 * CONV_W, :]))

    row = lambda m: (m, 0)
    fixed = lambda m: (0, 0)
    return pl.pallas_call(
        body, name="mix_out_fwd",
        grid=(SEQ // tm,),
        in_specs=[pl.BlockSpec((tm, D_MODEL), row), pl.BlockSpec((tm, CONV_W), row), pl.BlockSpec((tm, SSM_W), row),
                  pl.BlockSpec((SSM_W, SSM_W), fixed), pl.BlockSpec((1, SSM_W), fixed),
                  pl.BlockSpec((1, CONV_W), fixed), pl.BlockSpec((1, SSM_W), fixed),
                  pl.BlockSpec((D_MODEL, D_MODEL), fixed)],
        out_specs=pl.BlockSpec((tm, D_MODEL), row),
        out_shape=jax.ShapeDtypeStruct((SEQ, D_MODEL), F32),
        compiler_params=_cparams(("parallel",)),
    )(h, ya, ys, gluw, glub, con, son, wout)


def _mix_out_bwd(dh, ya, ys, gluw, glub, con, son, wout):
    tm = TOK_TILE

    def body(dh_ref, ya_ref, ys_ref, gw_ref, gb_ref, con_ref, son_ref, wo_ref,
             dya_ref, dys_ref, yc_ref, dhb_ref, zg_ref, dq_ref, part_ref):
        ysv = ys_ref[...]
        zg, th = _gelu(ysv)
        zg_bf = zg.astype(BF16)
        s = _sigmoid(_dot(zg_bf, gw_ref[...]) + gb_ref[...])
        out_s = zg * s
        na, ra = _rms_stats(ya_ref[...])
        ns, rs = _rms_stats(out_s)
        dh_bf = dh_ref[...].astype(BF16)
        yc_ref[0, :, 0:CONV_W] = (na * con_ref[...]).astype(BF16)
        yc_ref[0, :, CONV_W:2 * CONV_W] = (ns * son_ref[...]).astype(BF16)
        dhb_ref[0] = dh_bf
        dca = _dot_nt(dh_bf, wo_ref[0:CONV_W, :])
        dcs = _dot_nt(dh_bf, wo_ref[CONV_W:2 * CONV_W, :])
        dya, dcon = _rms_bwd(dca, na, ra, con_ref[...])
        dos, dson = _rms_bwd(dcs, ns, rs, son_ref[...])
        dya_ref[...] = dya
        dq = dos * zg * s * (1.0 - s)
        dq_bf = dq.astype(BF16)
        dzg = dos * s + _dot_nt(dq_bf, gw_ref[...])
        dys_ref[...] = dzg * _gelu_grad(ysv, th)
        zg_ref[0] = zg_bf
        dq_ref[0] = dq_bf
        rid = _row_ids(SUBLANES, SSM_W)
        part = jnp.zeros((SUBLANES, SSM_W), F32)
        for i, rowv in enumerate((dcon, dson, jnp.sum(dq, axis=0, keepdims=True))):
            part = jnp.where(rid == i, jnp.broadcast_to(rowv, (SUBLANES, SSM_W)), part)
        _accumulate(part_ref, pl.program_id(0) == 0, part)

    row = lambda m: (m, 0)
    fixed = lambda m: (0, 0)
    lead = lambda m: (0, m, 0)
    return pl.pallas_call(
        body, name="mix_out_bwd",
        grid=(SEQ // tm,),
        in_specs=[pl.BlockSpec((tm, D_MODEL), row), pl.BlockSpec((tm, CONV_W), row), pl.BlockSpec((tm, SSM_W), row),
                  pl.BlockSpec((SSM_W, SSM_W), fixed), pl.BlockSpec((1, SSM_W), fixed),
                  pl.BlockSpec((1, CONV_W), fixed), pl.BlockSpec((1, SSM_W), fixed),
                  pl.BlockSpec((D_MODEL, D_MODEL), fixed)],
        out_specs=[pl.BlockSpec((tm, CONV_W), row), pl.BlockSpec((tm, SSM_W), row),
                   pl.BlockSpec((1, tm, D_MODEL), lead), pl.BlockSpec((1, tm, D_MODEL), lead),
                   pl.BlockSpec((1, tm, SSM_W), lead), pl.BlockSpec((1, tm, SSM_W), lead),
                   pl.BlockSpec((8, SSM_W), fixed)],
        out_shape=[jax.ShapeDtypeStruct((SEQ, CONV_W), F32), jax.ShapeDtypeStruct((SEQ, SSM_W), F32),
                   jax.ShapeDtypeStruct((1, SEQ, D_MODEL), BF16), jax.ShapeDtypeStruct((1, SEQ, D_MODEL), BF16),
                   jax.ShapeDtypeStruct((1, SEQ, SSM_W), BF16), jax.ShapeDtypeStruct((1, SEQ, SSM_W), BF16),
                   jax.ShapeDtypeStruct((8, SSM_W), F32)],
        compiler_params=_cparams(("arbitrary",)),
    )(dh, ya, ys, gluw, glub, con, son, wout)


def _ple_fwd(h, g, p, wgate, wprojt):
    tm = TOK_TILE

    def body(h_ref, g_ref, p_ref, wg_ref, wp_ref, o_ref):
        x = h_ref[...]
        xh, _ = _rms_stats(x)
        s = _sigmoid(_dot((xh * g_ref[...]).astype(BF16), wg_ref[...]))
        o_ref[...] = x + _dot_nt(p_ref[...].astype(BF16), wp_ref[...]) * s

    row = lambda m: (m, 0)
    fixed = lambda m: (0, 0)
    return pl.pallas_call(
        body, name="ple_fwd",
        grid=(SEQ // tm,),
        in_specs=[pl.BlockSpec((tm, D_MODEL), row), pl.BlockSpec((1, D_MODEL), fixed), pl.BlockSpec((tm, PLE_DIM), row),
                  pl.BlockSpec((D_MODEL, D_MODEL), fixed), pl.BlockSpec((D_MODEL, PLE_DIM), fixed)],
        out_specs=pl.BlockSpec((tm, D_MODEL), row),
        out_shape=jax.ShapeDtypeStruct((SEQ, D_MODEL), F32),
        compiler_params=_cparams(("parallel",)),
    )(h, g, p, wgate, wprojt)


def _ple_bwd(h, g, p, dh, wgate, wprojt):
    tm = TOK_TILE

    def body(h_ref, g_ref, p_ref, dh_ref, wg_ref, wp_ref, o_ref, u_ref, dq_ref, dpp_ref, pb_ref, dg_ref):
        xh, r = _rms_stats(h_ref[...])
        u = (xh * g_ref[...]).astype(BF16)
        s = _sigmoid(_dot(u, wg_ref[...]))
        p_bf = p_ref[...].astype(BF16)
        pp = _dot_nt(p_bf, wp_ref[...])
        dhv = dh_ref[...]
        dq = (dhv * pp * s * (1.0 - s)).astype(BF16)
        u_ref[0] = u
        dq_ref[0] = dq
        dpp_ref[0] = (dhv * s).astype(BF16)
        pb_ref[0] = p_bf
        dx, dg = _rms_bwd(_dot_nt(dq, wg_ref[...]), xh, r, g_ref[...])
        o_ref[...] = dhv + dx
        _accumulate(dg_ref, pl.program_id(0) == 0, dg)

    row = lambda m: (m, 0)
    fixed = lambda m: (0, 0)
    lead = lambda m: (0, m, 0)
    big = jax.ShapeDtypeStruct((1, SEQ, D_MODEL), BF16)
    return pl.pallas_call(
        body, name="ple_bwd",
        grid=(SEQ // tm,),
        in_specs=[pl.BlockSpec((tm, D_MODEL), row), pl.BlockSpec((1, D_MODEL), fixed), pl.BlockSpec((tm, PLE_DIM), row),
                  pl.BlockSpec((tm, D_MODEL), row),
                  pl.BlockSpec((D_MODEL, D_MODEL), fixed), pl.BlockSpec((D_MODEL, PLE_DIM), fixed)],
        out_specs=[pl.BlockSpec((tm, D_MODEL), row),
                   pl.BlockSpec((1, tm, D_MODEL), lead), pl.BlockSpec((1, tm, D_MODEL), lead),
                   pl.BlockSpec((1, tm, D_MODEL), lead), pl.BlockSpec((1, tm, PLE_DIM), lead),
                   pl.BlockSpec((1, D_MODEL), fixed)],
        out_shape=[jax.ShapeDtypeStruct((SEQ, D_MODEL), F32), big, big, big,
                   jax.ShapeDtypeStruct((1, SEQ, PLE_DIM), BF16),
                   jax.ShapeDtypeStruct((1, D_MODEL), F32)],
        compiler_params=_cparams(("arbitrary",)),
    )(h, g, p, dh, wgate, wprojt)


def _final_loss(h, g, target):
    tm = TOK_TILE

    def body(h_ref, g_ref, t_ref, loss_ref, dh_ref, dg_ref):
        first = pl.program_id(0) == 0
        xh, r = _rms_stats(h_ref[...])
        diff = xh * g_ref[...] - t_ref[...]
        part = 0.5 * jnp.sum(jnp.mean(diff * diff, axis=-1, keepdims=True), axis=0, keepdims=True)
        _accumulate(loss_ref, first, jnp.broadcast_to(part, (SUBLANES, LANES)))
        dx, dg = _rms_bwd(diff * (1.0 / D_MODEL), xh, r, g_ref[...])
        dh_ref[...] = dx
        _accumulate(dg_ref, first, dg)

    row = lambda m: (m, 0)
    fixed = lambda m: (0, 0)
    return pl.pallas_call(
        body, name="final_loss",
        grid=(SEQ // tm,),
        in_specs=[pl.BlockSpec((tm, D_MODEL), row), pl.BlockSpec((1, D_MODEL), fixed),
                  pl.BlockSpec((tm, D_MODEL), row)],
        out_specs=[pl.BlockSpec((SUBLANES, LANES), fixed),
                   pl.BlockSpec((tm, D_MODEL), row),
                   pl.BlockSpec((1, D_MODEL), fixed)],
        out_shape=[jax.ShapeDtypeStruct((SUBLANES, LANES), F32),
                   jax.ShapeDtypeStruct((SEQ, D_MODEL), F32),
                   jax.ShapeDtypeStruct((1, D_MODEL), F32)],
        compiler_params=_cparams(("arbitrary",)),
    )(h, g, target)


def _disc(ar, ai, ldt):
    dt = jnp.exp(ldt)
    mag = jnp.exp(ar * dt)
    ph = ai * dt
    lr, li = mag * jnp.cos(ph), mag * jnp.sin(ph)
    nr, ni = lr - 1.0, li
    den = ar * ar + ai * ai
    return lr, li, (nr * ar + ni * ai) / den, (ni * ar - nr * ai) / den


def _s5_disc(a, ldt, a_rep, ldt_rep, b):
    def body(a_ref, l_ref, ar_ref, lr_ref, b_ref, pw_ref, bb_ref):
        lr, li, _, _ = _disc(a_ref[0], a_ref[1], l_ref[...])
        pr, pi = lr, li
        for k in range(8):
            pw_ref[k] = pr
            pw_ref[8 + k] = pi
            pr, pi = pr * lr - pi * li, pr * li + pi * lr
        _, _, fr, fi = _disc(ar_ref[0], ar_ref[1], lr_ref[...])
        br, bi = b_ref[0], b_ref[1]
        bb_ref[0] = fr * br - fi * bi
        bb_ref[1] = fr * bi + fi * br

    return pl.pallas_call(
        body, name="s5_disc",
        out_shape=[jax.ShapeDtypeStruct((16, SSM_GROUPS, SSM_STATE), F32),
                   jax.ShapeDtypeStruct((2, SSM_GROUPS, SSM_STATE * SSM_GROUP), F32)],
    )(a, ldt, a_rep, ldt_rep, b)


def _dot_exact(x, sel):
    hi = x.astype(BF16)
    r1 = x - hi.astype(F32)
    mid = r1.astype(BF16)
    lo = (r1 - mid.astype(F32)).astype(BF16)
    return _dot(hi, sel) + _dot(mid, sel) + _dot(lo, sel)


def _s5_disc_bwd(a, ldt, a_rep, ldt_rep, b, dl, dbb, sel):
    def body(a_ref, l_ref, ar_ref, lr_ref, b_ref, dl_ref, dbb_ref, sel_ref, da_ref, dldt_ref, db_ref):
        _, _, fr, fi = _disc(ar_ref[0], ar_ref[1], lr_ref[...])
        br, bi = b_ref[0], b_ref[1]
        dr, di = dbb_ref[0], dbb_ref[1]
        db_ref[0] = fr * dr + fi * di
        db_ref[1] = fr * di - fi * dr
        dfr = _dot_exact(dr * br + di * bi, sel_ref[...])
        dfi = _dot_exact(di * br - dr * bi, sel_ref[...])
        _, vjp = jax.vjp(_disc, a_ref[0], a_ref[1], l_ref[...])
        dar, dai, dldt = vjp((dl_ref[0], dl_ref[1], dfr, dfi))
        da_ref[0] = dar
        da_ref[1] = dai
        dldt_ref[...] = jnp.sum(dldt, axis=1, keepdims=True)

    return pl.pallas_call(
        body, name="s5_disc_bwd",
        out_shape=[jax.ShapeDtypeStruct((2, SSM_GROUPS, SSM_STATE), F32),
                   jax.ShapeDtypeStruct((SSM_GROUPS, 1), F32),
                   jax.ShapeDtypeStruct((2, SSM_GROUPS, SSM_STATE * SSM_GROUP), F32)],
    )(a, ldt, a_rep, ldt_rep, b, dl, dbb, sel)


def _row_block(rows, cap=512):
    for bm in range(min(cap, rows), 0, -1):
        if rows % bm == 0 and (bm % 8 == 0 or bm == rows):
            return bm
    return rows


def _pair_sum(fulls, got, segs):
    ns = len(segs)
    offs = _seg_offsets(segs)
    _, rtot, c = got.shape
    parts = 2
    pr = rtot // parts
    assert pr * parts == rtot and pr % 16 == 0
    pieces = [[] for _ in range(parts)]
    for a, (n, r) in enumerate(segs):
        for m in range(n):
            lo = offs[a] + m * r
            for h in range(parts):
                clo, chi = max(lo, h * pr), min(lo + r, (h + 1) * pr)
                if chi > clo:
                    pieces[h].append((a, m, clo - lo, clo - h * pr, chi - clo))
    n_sems = max(len(ps) for ps in pieces)

    def body(*refs):
        srcs = refs[:ns]
        got_ref, p32_ref, pbf_ref, own_v, sems = refs[ns:]
        k = pl.program_id(0)
        h = pl.program_id(1)
        dev = 2 * k + lax.axis_index("c")
        for hh in range(parts):
            @pl.when(h == hh)
            def _(hh=hh):
                cps = []
                for i, (a, m, so, do, rows) in enumerate(pieces[hh]):
                    start = pl.multiple_of(dev * segs[a][1] + so, 16)
                    cps.append(pltpu.make_async_copy(srcs[a].at[m, pl.ds(start, rows), :],
                                                     own_v.at[pl.ds(do, rows), :], sems.at[i]))
                for cp in cps:
                    cp.start()
                for cp in cps:
                    cp.wait()
        s = own_v[...].astype(F32) + got_ref[0].astype(F32)
        p32_ref[0] = s
        pbf_ref[0] = s.astype(BF16)

    spec = pl.BlockSpec((1, pr, c), lambda k, h: (k, h, 0))
    return pl.pallas_call(
        body, name="pair_sum",
        grid=(4, parts),
        in_specs=[ANY] * ns + [spec], out_specs=[spec, spec],
        out_shape=[jax.ShapeDtypeStruct(got.shape, F32), jax.ShapeDtypeStruct(got.shape, BF16)],
        scratch_shapes=[pltpu.VMEM((pr, c), BF16), pltpu.SemaphoreType.DMA((n_sems,))],
        compiler_params=_cparams(("arbitrary", "arbitrary")),
    )(*fulls, got)


def _chip_sum(chip, p32, rb):
    _, r, c = p32.shape
    bm = _row_block(r)

    def body(chip_ref, o_ref, r_ref, s_ref):
        s_ref[...] = ((o_ref[0] + r_ref[0].astype(F32)) + r_ref[1].astype(F32)) + r_ref[2].astype(F32)

    return pl.pallas_call(
        body, name="chip_sum",
        grid_spec=pltpu.PrefetchScalarGridSpec(
            num_scalar_prefetch=1, grid=(r // bm,),
            in_specs=[pl.BlockSpec((1, bm, c), lambda k, chip_ref: (chip_ref[0], k, 0)),
                      pl.BlockSpec((3, bm, c), lambda k, chip_ref: (0, k, 0))],
            out_specs=pl.BlockSpec((bm, c), lambda k, chip_ref: (k, 0))),
        out_shape=jax.ShapeDtypeStruct((r, c), F32),
        compiler_params=_cparams(("parallel",)),
    )(chip, p32, rb)


def _sum8(x):
    _, r, c = x.shape
    bm = _row_block(r)

    def body(x_ref, s_ref):
        s = x_ref[0]
        for d in range(1, N_DEV):
            s = s + x_ref[d]
        s_ref[...] = s

    return pl.pallas_call(
        body, name="sum8",
        grid=(r // bm,),
        in_specs=[pl.BlockSpec((N_DEV, bm, c), lambda k: (0, k, 0))],
        out_specs=pl.BlockSpec((bm, c), lambda k: (k, 0)),
        out_shape=jax.ShapeDtypeStruct((r, c), F32),
        compiler_params=_cparams(("parallel",)),
    )(x)


def _adamw(w, g, m, v):
    r, c = w.shape
    bm = _row_block(r)
    bc1 = 1.0 - ADAM_B1 ** ADAM_STEP
    bc2 = 1.0 - ADAM_B2 ** ADAM_STEP

    def body(w_ref, g_ref, m_ref, v_ref, d_ref, nm_ref, nv_ref):
        gv = g_ref[...]
        nm = ADAM_B1 * m_ref[...] + (1.0 - ADAM_B1) * gv
        nv = ADAM_B2 * v_ref[...] + (1.0 - ADAM_B2) * (gv * gv)
        nm_ref[...] = nm
        nv_ref[...] = nv
        d_ref[...] = -ADAM_LR * ((nm / bc1) / (jnp.sqrt(nv / bc2) + ADAM_EPS) + ADAM_WD * w_ref[...])

    spec = pl.BlockSpec((bm, c), lambda k: (k, 0))
    shp = jax.ShapeDtypeStruct((r, c), F32)
    return pl.pallas_call(
        body, name="adamw",
        grid=(r // bm,),
        in_specs=[spec] * 4, out_specs=[spec] * 3, out_shape=[shp] * 3,
        compiler_params=_cparams(("parallel",)),
    )(w, g, m, v)


def _mesh_pos():
    return lax.axis_index("x"), lax.axis_index("y"), lax.axis_index("c")


def _dev_index(p):
    return 4 * p[0] + 2 * p[1] + p[2]


def _seg_offsets(segs):
    offs, o = [], 0
    for n, r in segs:
        offs.append(o)
        o += n * r
    return offs


def _remote(src, dst, send_sem, recv_sem, to):
    return pltpu.make_async_remote_copy(src_ref=src, dst_ref=dst, send_sem=send_sem, recv_sem=recv_sem,
                                        device_id=to, device_id_type=MESH)


def _allgather(pack, segs, name):
    rtot, c = pack.shape
    ns = len(segs)
    offs = _seg_offsets(segs)
    assert rtot == sum(n * r for n, r in segs)

    def body(pack_ref, *refs):
        outs = refs[:ns]
        send_sems, recv_sems, local_sem = refs[ns:]
        x, y, cc = _mesh_pos()
        me, sib = (x, y, cc), (x, y, 1 - cc)
        chips = [(1 - x, y), (x, 1 - y), (1 - x, 1 - y)]

        def pieces(dev, from_pack):
            res = []
            for a, (n, r) in enumerate(segs):
                for m in range(n):
                    dst = outs[a].at[m, pl.ds(pl.multiple_of(dev * r, r), r), :]
                    src = pack_ref.at[pl.ds(offs[a] + m * r, r), :] if from_pack else dst
                    res.append((src, dst))
            return res

        def push(k, dev, to, from_pack):
            for s, d in pieces(dev, from_pack):
                _remote(s, d, send_sems.at[k], recv_sems.at[k], to).start()

        def whole(k):
            return _remote(pack_ref, pack_ref, send_sems.at[k], recv_sems.at[k], me)

        my_dev = _dev_index(me)
        for s, d in pieces(my_dev, True):
            pltpu.make_async_copy(s, d, local_sem).start()
        push(0, my_dev, sib, True)
        for j, chip in enumerate(chips):
            push(1 + j, my_dev, (*chip, cc), True)
        for j, chip in enumerate(chips):
            whole(1 + j).wait_recv()
            push(4 + j, _dev_index((*chip, cc)), sib, False)
        whole(0).wait_recv()
        for j in range(3):
            whole(4 + j).wait_recv()
        for k in range(7):
            whole(k).wait_send()
        pltpu.make_async_copy(pack_ref, pack_ref, local_sem).wait()

    return pl.pallas_call(
        body, name=name,
        in_specs=[ANY], out_specs=[ANY] * ns,
        out_shape=[jax.ShapeDtypeStruct((n, N_DEV * r, c), pack.dtype) for n, r in segs],
        scratch_shapes=[pltpu.SemaphoreType.DMA((7,)), pltpu.SemaphoreType.DMA((7,)), pltpu.SemaphoreType.DMA],
    )(pack)


def _rs_sibling(fulls, segs):
    ns = len(segs)
    offs = _seg_offsets(segs)
    rtot = sum(n * r for n, r in segs)
    c = fulls[0].shape[-1]

    def body(*refs):
        srcs = refs[:ns]
        got_ref, send_sem, recv_sem = refs[ns:]
        x, y, cc = _mesh_pos()
        me, sib = (x, y, cc), (x, y, 1 - cc)
        for k in range(4):
            for a, (n, r) in enumerate(segs):
                for m in range(n):
                    theirs = srcs[a].at[m, pl.ds(pl.multiple_of((2 * k + 1 - cc) * r, r), r), :]
                    _remote(theirs, got_ref.at[k, pl.ds(offs[a] + m * r, r), :], send_sem, recv_sem, sib).start()
        _remote(got_ref, got_ref, send_sem, recv_sem, me).wait()

    return pl.pallas_call(
        body, name="rs_sibling",
        in_specs=[ANY] * ns, out_specs=ANY, out_shape=jax.ShapeDtypeStruct((4, rtot, c), fulls[0].dtype),
        scratch_shapes=[pltpu.SemaphoreType.DMA, pltpu.SemaphoreType.DMA],
    )(*fulls)


def _rs_chips(pbf):
    _, rtot, c = pbf.shape

    def body(pbf_ref, got_ref, send_sems, recv_sems):
        x, y, cc = _mesh_pos()
        chips = [(1 - x, y), (x, 1 - y), (1 - x, 1 - y)]
        cps = [_remote(pbf_ref.at[2 * cx + cy], got_ref.at[j], send_sems.at[j], recv_sems.at[j], (cx, cy, cc))
               for j, (cx, cy) in enumerate(chips)]
        for cp in cps:
            cp.start()
        for cp in cps:
            cp.wait()

    return pl.pallas_call(
        body, name="rs_chips",
        in_specs=[ANY], out_specs=ANY,
        out_shape=jax.ShapeDtypeStruct((3, rtot, c), BF16),
        scratch_shapes=[pltpu.SemaphoreType.DMA((3,)), pltpu.SemaphoreType.DMA((3,))],
    )(pbf)


def _tp(w):
    return jnp.swapaxes(w, -1, -2)


def _block_diag(blocks):
    g, r, c = blocks.shape
    eye = jnp.eye(g, dtype=blocks.dtype)
    return (blocks[:, :, None, :] * eye[:, None, :, None]).reshape(g * r, g * c)


def _diag_blocks(full, r, c):
    g = full.shape[0] // r
    return jnp.einsum('grgc->grc', full.reshape(g, r, g, c))


def _s5_prepare(a_re, a_im, log_dt, b_re, b_im, c_re, c_im):
    a = jnp.stack([a_re, a_im])
    ldt = jnp.broadcast_to(log_dt[:, None], (SSM_GROUPS, SSM_STATE))
    a_rep = jnp.repeat(a, SSM_GROUP, axis=-1)
    ldt_rep = jnp.broadcast_to(log_dt[:, None], (SSM_GROUPS, SSM_STATE * SSM_GROUP))
    b = jnp.stack([b_re.reshape(SSM_GROUPS, -1), b_im.reshape(SSM_GROUPS, -1)])
    disc_in = (a, ldt, a_rep, ldt_rep, b)
    pw, bb = _s5_disc(*disc_in)
    pr = pw[:8].reshape(8, N_STATE)
    pi = pw[8:].reshape(8, N_STATE)

    def table(pr, pi, edge):
        rows = [jnp.broadcast_to(jnp.concatenate([pr[k], pi[k]])[None], (8, 2 * N_STATE)) for k in (0, 1, 3)]
        return jnp.concatenate(rows + [edge], axis=0)

    ltab = table(pr, pi, jnp.concatenate([pr, pi], axis=1))
    ltab_rev = table(pr, -pi, jnp.concatenate([pr[::-1], -pi[::-1]], axis=1))
    bb4 = bb.reshape(2, SSM_GROUPS, SSM_STATE, SSM_GROUP)
    bbmat = jnp.concatenate([_block_diag(_tp(bb4[0])), _block_diag(_tp(bb4[1]))], axis=1).astype(BF16)
    ccmat = jnp.concatenate([_block_diag(_tp(c_re)), -_block_diag(_tp(c_im))], axis=0).astype(BF16)
    return disc_in, ltab, ltab_rev, bbmat, ccmat


def _layer_fwd(h, p_l, small, big):
    saved = {'h0': h}
    h, saved['gu1'] = _ffn_fwd(h, small['ffn1_norm'], big['ff1'])
    saved['h1'] = h
    z = _inproj_fwd(h, small['mix_norm'], big['wint'])
    ya, ys, hs = _s5conv_fwd(z, small['conv_w'], small['conv_b'], small['bbmat'], small['ccmat'], small['dvec'],
                             small['ltab'])
    saved.update(z=z, ya=ya, ys=ys, hs=hs)
    h = _mix_out_fwd(h, ya, ys, big['glu'], small['glu_b'], small['conv_out_norm'], small['ssm_out_norm'], big['wout'])
    saved['h2'] = h
    h, saved['gu2'] = _ffn_fwd(h, small['ffn2_norm'], big['ff2'])
    saved['h3'] = h
    h = _ple_fwd(h, small['ple_norm'], p_l, big['plg'], big['plpt'])
    return h, saved


def _ffn_bwd(h_in, g, dh, gu, w3):
    dh_in, dga, ud, dg = _ffn_bwd_act(h_in, g, dh, gu, w3)
    return dh_in, _matmul_tn(dga, ud, FF_BLOCK, BF16, "ffn_wgrad"), dg


def _layer_bwd(dh, p_l, small, big, saved):
    gs = {}
    dh, u, dq, dpp, pb, gs['ple_norm'] = _ple_bwd(saved['h3'], small['ple_norm'], p_l, dh, big['plg'], big['plpt'])
    d_plg = _matmul_tn(u, dq, 256, BF16, "ple_gate_wgrad")
    d_plpt = _matmul_tn(dpp, pb, 256, BF16, "ple_proj_wgrad")
    dh, d_ff2, gs['ffn2_norm'] = _ffn_bwd(saved['h2'], small['ffn2_norm'], dh, saved['gu2'], big['ff2'])

    dya, dys, ycat, dhb, zg, dq, part = _mix_out_bwd(dh, saved['ya'], saved['ys'], big['glu'], small['glu_b'],
                                                     small['conv_out_norm'], small['ssm_out_norm'], big['wout'])
    d_wout = _matmul_tn(ycat, dhb, 256, BF16, "w_out_wgrad")
    d_glu = _matmul_tn(zg, dq, 256, BF16, "glu_wgrad")
    dz, gadj, us, dyb, dl, dcw = _s5conv_bwd(saved['z'], saved['hs'], dya, dys, small['conv_w'], small['conv_b'],
                                             small['bbmat'], small['ccmat'], small['dvec'], small['ltab_rev'])
    d_bbt = _matmul_tn(gadj, us, 512, F32, "s5_b_wgrad")[0]
    d_cc = _matmul_tn(saved['hs'][None], dyb, 512, F32, "s5_c_wgrad")[0]
    dh, u, gs['mix_norm'] = _inproj_bwd(saved['h1'], small['mix_norm'], dh, dz, big['wint'])
    d_wint = _matmul_tn(dz[None], u, 256, BF16, "w_in_wgrad")
    dh, d_ff1, gs['ffn1_norm'] = _ffn_bwd(saved['h0'], small['ffn1_norm'], dh, saved['gu1'], big['ff1'])

    dbb = jnp.stack([_diag_blocks(d_bbt[:N_STATE], SSM_STATE, SSM_GROUP).reshape(SSM_GROUPS, -1),
                     _diag_blocks(d_bbt[N_STATE:], SSM_STATE, SSM_GROUP).reshape(SSM_GROUPS, -1)])
    dlb = dl[0].reshape(2, SSM_GROUPS, SSM_STATE)
    sel = jnp.repeat(jnp.eye(SSM_STATE, dtype=BF16), SSM_GROUP, axis=0)
    da, dldt, db = _s5_disc_bwd(*small['disc_in'], dlb, dbb, sel)
    gs['ssm_A_re'], gs['ssm_A_im'] = da[0], da[1]
    gs['ssm_log_dt'] = dldt[:, 0]
    gs['ssm_B_re'] = db[0].reshape(SSM_GROUPS, SSM_STATE, SSM_GROUP)
    gs['ssm_B_im'] = db[1].reshape(SSM_GROUPS, SSM_STATE, SSM_GROUP)
    gs['ssm_C_re'] = _tp(_diag_blocks(d_cc[:N_STATE], SSM_STATE, SSM_GROUP))
    gs['ssm_C_im'] = -_tp(_diag_blocks(d_cc[N_STATE:], SSM_STATE, SSM_GROUP))
    gs['conv_w'] = dcw[0:3]
    gs['conv_b'] = dcw[3]
    gs['ssm_D'] = dcw[4].reshape(SSM_GROUPS, SSM_GROUP)
    gs['conv_out_norm'], gs['ssm_out_norm'], gs['glu_b'] = part[0], part[1], part[2]
    for n in ('ple_norm', 'ffn2_norm', 'mix_norm', 'ffn1_norm'):
        gs[n] = gs[n][0]
    fulls = [d_ff1, d_ff2, d_wint, d_wout, d_plg,
             d_plpt.reshape(1, D_MODEL * PLE_DIM // D_MODEL, D_MODEL), d_glu.reshape(1, SSM_W * SSM_W // D_MODEL, D_MODEL)]
    return dh, fulls, gs


def _pad_rows(flat, mult):
    per = mult * LANES
    n = flat.shape[0]
    tot = -(-n // per) * per
    return jnp.pad(flat, (0, tot - n)).reshape(tot // LANES, LANES)


def _adamw_any(w, g, m, v):
    shp = w.shape
    two = (lambda t: t.reshape(-1, shp[-1]))
    d, nm, nv = _adamw(two(w), two(g), two(m), two(v))
    return d.reshape(shp), nm.reshape(shp), nv.reshape(shp)


def kernel(x, p, ffn1_norm, ffn1_w_gate, ffn1_w_up, ffn1_w_down, mix_norm, w_in, conv_w, conv_b, ssm_A_re, ssm_A_im, ssm_B_re, ssm_B_im, ssm_C_re, ssm_C_im, ssm_D, ssm_log_dt, glu_w, glu_b, conv_out_norm, ssm_out_norm, w_out, ffn2_norm, ffn2_w_gate, ffn2_w_up, ffn2_w_down, ple_norm, ple_w_gate, ple_w_proj, final_norm, loss_target, m_ffn1_norm, m_ffn1_w_gate, m_ffn1_w_up, m_ffn1_w_down, m_mix_norm, m_w_in, m_conv_w, m_conv_b, m_ssm_A_re, m_ssm_A_im, m_ssm_B_re, m_ssm_B_im, m_ssm_C_re, m_ssm_C_im, m_ssm_D, m_ssm_log_dt, m_glu_w, m_glu_b, m_conv_out_norm, m_ssm_out_norm, m_w_out, m_ffn2_norm, m_ffn2_w_gate, m_ffn2_w_up, m_ffn2_w_down, m_ple_norm, m_ple_w_gate, m_ple_w_proj, m_final_norm, v_ffn1_norm, v_ffn1_w_gate, v_ffn1_w_up, v_ffn1_w_down, v_mix_norm, v_w_in, v_conv_w, v_conv_b, v_ssm_A_re, v_ssm_A_im, v_ssm_B_re, v_ssm_B_im, v_ssm_C_re, v_ssm_C_im, v_ssm_D, v_ssm_log_dt, v_glu_w, v_glu_b, v_conv_out_norm, v_ssm_out_norm, v_w_out, v_ffn2_norm, v_ffn2_w_gate, v_ffn2_w_up, v_ffn2_w_down, v_ple_norm, v_ple_w_gate, v_ple_w_proj, v_final_norm):
    given = dict(locals())
    W = {n: given[n] for n in W_NAMES}
    M = {n: given['m_' + n] for n in W_NAMES}
    V = {n: given['v_' + n] for n in W_NAMES}
    my_dev = _dev_index(_mesh_pos())
    my_chip = (my_dev // 2).astype(jnp.int32).reshape(1)

    conv_shard = _pad_rows(W['conv_w'].reshape(-1), SUBLANES)
    conv_all = _allgather(conv_shard, ((1, SUBLANES),), "ag_conv_w")[0]
    conv_full = conv_all.reshape(N_DEV, -1)[:, :DEPTH * 3 * (CONV_W // N_DEV)]
    conv_full = conv_full.reshape(N_DEV, DEPTH, 3, CONV_W // N_DEV).transpose(1, 2, 0, 3).reshape(DEPTH, 3, CONV_W)

    bigs = []
    for l in range(DEPTH):
        pack = jnp.concatenate([
            _tp(W['ffn1_w_gate'][l]), _tp(W['ffn1_w_up'][l]), W['ffn1_w_down'][l],
            _tp(W['ffn2_w_gate'][l]), _tp(W['ffn2_w_up'][l]), W['ffn2_w_down'][l],
            _tp(W['w_in'][l]), W['w_out'][l], W['ple_w_gate'][l],
            _tp(W['ple_w_proj'][l]).reshape(-1, D_MODEL), W['glu_w'][l].reshape(-1, D_MODEL)], axis=0).astype(BF16)
        ff1, ff2, wint, wout, plg, plpt, glu = _allgather(pack, SEGS, "ag_weights")
        bigs.append(dict(ff1=ff1, ff2=ff2, wint=wint[0], wout=wout[0], plg=plg[0],
                         plpt=plpt.reshape(D_MODEL, PLE_DIM), glu=glu.reshape(SSM_W, SSM_W)))

    smalls, saves = [], []
    h = x[0]
    for l in range(DEPTH):
        small = {n: W[n][l][None] for n in ('ffn1_norm', 'mix_norm', 'conv_b', 'glu_b', 'conv_out_norm',
                                            'ssm_out_norm', 'ffn2_norm', 'ple_norm')}
        small['conv_w'] = conv_full[l]
        small['dvec'] = W['ssm_D'][l].reshape(1, SSM_W)
        (small['disc_in'], small['ltab'], small['ltab_rev'], small['bbmat'], small['ccmat']) = _s5_prepare(
            W['ssm_A_re'][l], W['ssm_A_im'][l], W['ssm_log_dt'][l], W['ssm_B_re'][l], W['ssm_B_im'][l],
            W['ssm_C_re'][l], W['ssm_C_im'][l])
        h, saved = _layer_fwd(h, p[l, 0], small, bigs[l])
        smalls.append(small)
        saves.append(saved)
    loss_tile, dh, d_final = _final_loss(h, W['final_norm'][None], loss_target[0])
    loss = lax.psum(loss_tile[0, 0], ("x", "y", "c"))

    layer_gs = [None] * DEPTH
    shard_grads = [None] * DEPTH
    for l in reversed(range(DEPTH)):
        dh, fulls, layer_gs[l] = _layer_bwd(dh, p[l, 0], smalls[l], bigs[l], saves[l])
        p32, pbf = _pair_sum(fulls, _rs_sibling(fulls, SEGS), SEGS)
        shard_grads[l] = _chip_sum(my_chip, p32, _rs_chips(pbf))
    grad_x = dh[None]

    gs = {n: jnp.stack([layer_gs[l][n] for l in range(DEPTH)]) for n in layer_gs[0]}
    gs['final_norm'] = d_final[0]
    flat = jnp.concatenate([gs[n].reshape(-1) for n in SMALL_NAMES] + [gs['conv_w'].reshape(-1)])
    n_flat = flat.shape[0]
    flat = _pad_rows(flat, SUBLANES)
    rows = flat.shape[0]
    gathered = _allgather(flat, ((1, rows),), "ag_small_grads")[0]
    red = _sum8(gathered.reshape(N_DEV, rows, LANES)).reshape(-1)[:n_flat]
    G = {}
    o = 0
    for n in SMALL_NAMES:
        G[n] = red[o:o + W[n].size].reshape(W[n].shape)
        o += W[n].size
    conv_g_full = red[o:].reshape(DEPTH, 3, CONV_W)
    G['conv_w'] = lax.dynamic_slice_in_dim(conv_g_full, my_dev * (CONV_W // N_DEV), CONV_W // N_DEV, axis=2)

    sg = jnp.stack(shard_grads)
    offs = _seg_offsets(SEGS)
    r = SEGS[0][1]
    for a, f in ((0, 'ffn1'), (1, 'ffn2')):
        G[f + '_w_gate'] = _tp(sg[:, offs[a]:offs[a] + r])
        G[f + '_w_up'] = _tp(sg[:, offs[a] + r:offs[a] + 2 * r])
        G[f + '_w_down'] = sg[:, offs[a] + 2 * r:offs[a] + 3 * r]
    G['w_in'] = _tp(sg[:, offs[2]:offs[2] + SEGS[2][1]])
    G['w_out'] = sg[:, offs[3]:offs[3] + SEGS[3][1]]
    G['ple_w_gate'] = sg[:, offs[4]:offs[4] + SEGS[4][1]]
    G['ple_w_proj'] = _tp(sg[:, offs[5]:offs[5] + SEGS[5][1]].reshape(DEPTH, D_MODEL // N_DEV, PLE_DIM))
    G['glu_w'] = sg[:, offs[6]:offs[6] + SEGS[6][1]].reshape(DEPTH, SSM_W // N_DEV, SSM_W)

    delta, new_m, new_v = {}, {}, {}
    cat = lambda src: _pad_rows(jnp.concatenate([src[n].reshape(-1) for n in SMALL_NAMES]), SUBLANES)
    d_s, m_s, v_s = _adamw(cat(W), cat(G), cat(M), cat(V))
    o = 0
    for n in SMALL_NAMES:
        for dst, src in ((delta, d_s), (new_m, m_s), (new_v, v_s)):
            dst[n] = src.reshape(-1)[o:o + W[n].size].reshape(W[n].shape)
        o += W[n].size
    for n in W_NAMES:
        if n not in delta:
            delta[n], new_m[n], new_v[n] = _adamw_any(W[n], G[n], M[n], V[n])

    return (loss, grad_x, *[G[n] for n in W_NAMES], *[delta[n] for n in W_NAMES],
            *[new_m[n] for n in W_NAMES], *[new_v[n] for n in W_NAMES])
```

```python
import math

import jax
import jax.numpy as jnp
from jax import lax
from jax.experimental import pallas as pl
from jax.experimental.pallas import tpu as pltpu

F32 = jnp.float32
BF16 = jnp.bfloat16

N_DEV = 8
DEPTH = 4
SEQ = 2048
D_MODEL = 1024
D_FF = 2816
CONV_W = 512
SSM_W = 512
SSM_GROUPS = 32
SSM_GROUP = 16
SSM_STATE = 64
N_STATE = SSM_GROUPS * SSM_STATE
IN_COLS = 2048
PLE_DIM = 256
EPS = 1e-6

ADAM_LR = 0.001
ADAM_B1 = 0.9
ADAM_B2 = 0.999
ADAM_EPS = 1e-08
ADAM_WD = 0.01
ADAM_STEP = 10

FF_BLOCK = 256
N_FF_BLOCKS = D_FF // FF_BLOCK
TOK_TILE_FFN = 1024
TOK_TILE = 512
CHUNK = 256
N_CHUNKS = SEQ // CHUNK
LANE_GROUP = 512
SUBLANES = 8
LANES = 128
MIB = 1024 * 1024

W_NAMES = ['ffn1_norm', 'ffn1_w_gate', 'ffn1_w_up', 'ffn1_w_down', 'mix_norm', 'w_in', 'conv_w', 'conv_b',
           'ssm_A_re', 'ssm_A_im', 'ssm_B_re', 'ssm_B_im', 'ssm_C_re', 'ssm_C_im', 'ssm_D', 'ssm_log_dt',
           'glu_w', 'glu_b', 'conv_out_norm', 'ssm_out_norm', 'w_out', 'ffn2_norm', 'ffn2_w_gate', 'ffn2_w_up',
           'ffn2_w_down', 'ple_norm', 'ple_w_gate', 'ple_w_proj', 'final_norm']
SMALL_NAMES = ['ffn1_norm', 'mix_norm', 'conv_b', 'ssm_A_re', 'ssm_A_im', 'ssm_B_re', 'ssm_B_im', 'ssm_C_re',
               'ssm_C_im', 'ssm_D', 'ssm_log_dt', 'glu_b', 'conv_out_norm', 'ssm_out_norm', 'ffn2_norm',
               'ple_norm', 'final_norm']

SEGS = ((3, 352), (3, 352), (1, 256), (1, 128), (1, 128), (1, 32), (1, 32))
PACK_ROWS = sum(n * r for n, r in SEGS)

MESH = pl.DeviceIdType.MESH
ANY = pl.BlockSpec(memory_space=pl.ANY)


def _cparams(sem=None, vmem_mib=48, **kw):
    return pltpu.CompilerParams(dimension_semantics=sem, vmem_limit_bytes=vmem_mib * MIB, **kw)


def _dot(a, b):
    return jnp.dot(a, b, preferred_element_type=F32)


def _dot_nt(a, b):
    return lax.dot_general(a, b, (((1,), (1,)), ((), ())), preferred_element_type=F32)


def _dot_tn(a, b):
    return lax.dot_general(a, b, (((0,), (0,)), ((), ())), preferred_element_type=F32)


def _rms_stats(x):
    r = lax.rsqrt(jnp.mean(x * x, axis=-1, keepdims=True) + EPS)
    return x * r, r


def _rms_bwd(dy, xh, r, g):
    dxh = dy * g
    dx = r * (dxh - xh * jnp.mean(dxh * xh, axis=-1, keepdims=True))
    dg = jnp.sum(dy * xh, axis=0, keepdims=True)
    return dx, dg


def _sigmoid(x):
    return 1.0 / (1.0 + jnp.exp(-x))


_GELU_C = math.sqrt(2.0 / math.pi)


def _gelu(x):
    t = jnp.tanh(_GELU_C * (x + 0.044715 * x * x * x))
    return 0.5 * x * (1.0 + t), t


def _gelu_grad(x, t):
    return 0.5 * (1.0 + t) + 0.5 * x * (1.0 - t * t) * _GELU_C * (1.0 + 3.0 * 0.044715 * x * x)


def _accumulate(ref, first, value):
    @pl.when(first)
    def _():
        ref[...] = value

    @pl.when(jnp.logical_not(first))
    def _():
        ref[...] += value


def _ffn_fwd(h, g, w3):
    tm = TOK_TILE_FFN

    def body(h_ref, g_ref, w_ref, out_ref, gu_ref, u_ref):
        k = pl.program_id(1)

        @pl.when(k == 0)
        def _():
            x = h_ref[...]
            xh, _ = _rms_stats(x)
            u_ref[...] = (xh * g_ref[...]).astype(BF16)
            out_ref[...] = x

        u = u_ref[...]
        gate = _dot_nt(u, w_ref[0])
        up = _dot_nt(u, w_ref[1])
        a = gate * _sigmoid(gate) * up
        gu_ref[0] = gate.astype(BF16)
        gu_ref[1] = up.astype(BF16)
        out_ref[...] += 0.5 * _dot(a.astype(BF16), w_ref[2])

    return pl.pallas_call(
        body, name="ffn_fwd",
        grid=(SEQ // tm, N_FF_BLOCKS),
        in_specs=[pl.BlockSpec((tm, D_MODEL), lambda m, k: (m, 0)),
                  pl.BlockSpec((1, D_MODEL), lambda m, k: (0, 0)),
                  pl.BlockSpec((3, FF_BLOCK, D_MODEL), lambda m, k: (0, k, 0))],
        out_specs=[pl.BlockSpec((tm, D_MODEL), lambda m, k: (m, 0)),
                   pl.BlockSpec((2, tm, FF_BLOCK), lambda m, k: (0, m, k))],
        out_shape=[jax.ShapeDtypeStruct((SEQ, D_MODEL), F32),
                   jax.ShapeDtypeStruct((2, SEQ, D_FF), BF16)],
        scratch_shapes=[pltpu.VMEM((tm, D_MODEL), BF16)],
        compiler_params=_cparams(("parallel", "arbitrary")),
    )(h, g, w3)


def _ffn_bwd_act(h, g, dout, gu, w3):
    tm = TOK_TILE

    def body(h_ref, g_ref, d_ref, gu_ref, w_ref, dh_ref, dga_ref, ud_ref, dg_ref, acc_ref):
        m = pl.program_id(0)
        k = pl.program_id(1)

        @pl.when(k == 0)
        def _():
            xh, _ = _rms_stats(h_ref[...])
            ud_ref[0] = (xh * g_ref[...]).astype(BF16)
            ud_ref[1] = (0.5 * d_ref[...]).astype(BF16)
            acc_ref[...] = jnp.zeros_like(acc_ref)

        gate = gu_ref[0].astype(F32)
        up = gu_ref[1].astype(F32)
        sg = _sigmoid(gate)
        silu = gate * sg
        da = _dot_nt(ud_ref[1], w_ref[2])
        dgate = (da * up * sg * (1.0 + gate * (1.0 - sg))).astype(BF16)
        dup = (da * silu).astype(BF16)
        dga_ref[0] = dgate
        dga_ref[1] = dup
        dga_ref[2] = (silu * up).astype(BF16)
        acc_ref[...] += _dot(dgate, w_ref[0]) + _dot(dup, w_ref[1])

        @pl.when(k == N_FF_BLOCKS - 1)
        def _():
            xh, r = _rms_stats(h_ref[...])
            dx, dg = _rms_bwd(acc_ref[...], xh, r, g_ref[...])
            dh_ref[...] = d_ref[...] + dx
            _accumulate(dg_ref, m == 0, dg)

    return pl.pallas_call(
        body, name="ffn_bwd_act",
        grid=(SEQ // tm, N_FF_BLOCKS),
        in_specs=[pl.BlockSpec((tm, D_MODEL), lambda m, k: (m, 0)),
                  pl.BlockSpec((1, D_MODEL), lambda m, k: (0, 0)),
                  pl.BlockSpec((tm, D_MODEL), lambda m, k: (m, 0)),
                  pl.BlockSpec((2, tm, FF_BLOCK), lambda m, k: (0, m, k)),
                  pl.BlockSpec((3, FF_BLOCK, D_MODEL), lambda m, k: (0, k, 0))],
        out_specs=[pl.BlockSpec((tm, D_MODEL), lambda m, k: (m, 0)),
                   pl.BlockSpec((3, tm, FF_BLOCK), lambda m, k: (0, m, k)),
                   pl.BlockSpec((2, tm, D_MODEL), lambda m, k: (0, m, 0)),
                   pl.BlockSpec((1, D_MODEL), lambda m, k: (0, 0))],
        out_shape=[jax.ShapeDtypeStruct((SEQ, D_MODEL), F32),
                   jax.ShapeDtypeStruct((3, SEQ, D_FF), BF16),
                   jax.ShapeDtypeStruct((2, SEQ, D_MODEL), BF16),
                   jax.ShapeDtypeStruct((1, D_MODEL), F32)],
        scratch_shapes=[pltpu.VMEM((tm, D_MODEL), F32)],
        compiler_params=_cparams(("arbitrary", "arbitrary")),
    )(h, g, dout, gu, w3)


def _matmul_tn(a, b, bm, out_dtype, name):
    na, t, m = a.shape
    nb, _, n = b.shape

    def body(a_ref, b_ref, o_ref):
        o_ref[0] = _dot_tn(a_ref[0], b_ref[0]).astype(out_dtype)

    return pl.pallas_call(
        body, name=name,
        grid=(na, m // bm),
        in_specs=[pl.BlockSpec((1, t, bm), lambda i, k: (i, 0, k)),
                  pl.BlockSpec((1, t, n), lambda i, k: (jnp.maximum(i - (na - nb), 0), 0, 0))],
        out_specs=pl.BlockSpec((1, bm, n), lambda i, k: (i, k, 0)),
        out_shape=jax.ShapeDtypeStruct((na, m, n), out_dtype),
        compiler_params=_cparams(("arbitrary", "parallel")),
    )(a, b)


def _inproj_fwd(h, g, wint):
    tm = TOK_TILE

    def body(h_ref, g_ref, w_ref, z_ref):
        xh, _ = _rms_stats(h_ref[...])
        z_ref[...] = _dot_nt((xh * g_ref[...]).astype(BF16), w_ref[...])

    return pl.pallas_call(
        body, name="inproj_fwd",
        grid=(SEQ // tm,),
        in_specs=[pl.BlockSpec((tm, D_MODEL), lambda m: (m, 0)),
                  pl.BlockSpec((1, D_MODEL), lambda m: (0, 0)),
                  pl.BlockSpec((IN_COLS, D_MODEL), lambda m: (0, 0))],
        out_specs=pl.BlockSpec((tm, IN_COLS), lambda m: (m, 0)),
        out_shape=jax.ShapeDtypeStruct((SEQ, IN_COLS), F32),
        compiler_params=_cparams(("parallel",)),
    )(h, g, wint)


def _inproj_bwd(h, g, dh, dz, wint):
    tm = TOK_TILE

    def body(h_ref, g_ref, dh_ref, dz_ref, w_ref, o_ref, u_ref, dg_ref):
        xh, r = _rms_stats(h_ref[...])
        u_ref[0] = (xh * g_ref[...]).astype(BF16)
        dx, dg = _rms_bwd(_dot(dz_ref[...], w_ref[...]), xh, r, g_ref[...])
        o_ref[...] = dh_ref[...] + dx
        _accumulate(dg_ref, pl.program_id(0) == 0, dg)

    return pl.pallas_call(
        body, name="inproj_bwd",
        grid=(SEQ // tm,),
        in_specs=[pl.BlockSpec((tm, D_MODEL), lambda m: (m, 0)),
                  pl.BlockSpec((1, D_MODEL), lambda m: (0, 0)),
                  pl.BlockSpec((tm, D_MODEL), lambda m: (m, 0)),
                  pl.BlockSpec((tm, IN_COLS), lambda m: (m, 0)),
                  pl.BlockSpec((IN_COLS, D_MODEL), lambda m: (0, 0))],
        out_specs=[pl.BlockSpec((tm, D_MODEL), lambda m: (m, 0)),
                   pl.BlockSpec((1, tm, D_MODEL), lambda m: (0, m, 0)),
                   pl.BlockSpec((1, D_MODEL), lambda m: (0, 0))],
        out_shape=[jax.ShapeDtypeStruct((SEQ, D_MODEL), F32),
                   jax.ShapeDtypeStruct((1, SEQ, D_MODEL), BF16),
                   jax.ShapeDtypeStruct((1, D_MODEL), F32)],
        compiler_params=_cparams(("arbitrary",)),
    )(h, g, dh, dz, wint)


def _row_ids(n, w):
    return lax.broadcasted_iota(jnp.int32, (n, w), 0)


def _bcast_row(x, i, n):
    return jnp.broadcast_to(x[i:i + 1, :], (n, x.shape[1]))


def _conv_taps(v, tail):
    n, w = v.shape
    rid = _row_ids(n, w)
    v1 = jnp.where(rid == 0, _bcast_row(tail, 7, n), pltpu.roll(v, 1, 0))
    v2 = jnp.where(rid == 0, _bcast_row(tail, 6, n),
                   jnp.where(rid == 1, _bcast_row(tail, 7, n), pltpu.roll(v, 2, 0)))
    return v1, v2


def _scan_chunk(work, ltab, carry, reverse):
    nblk = CHUNK // SUBLANES
    row = _row_ids(SUBLANES, LANE_GROUP)
    for gi in range(N_STATE // LANE_GROUP):
        cre = pl.ds(gi * LANE_GROUP, LANE_GROUP)
        cim = pl.ds(N_STATE + gi * LANE_GROUP, LANE_GROUP)
        pows = [(ltab[8 * k:8 * k + 8, cre], ltab[8 * k:8 * k + 8, cim]) for k in range(3)]
        pr = ltab[24:32, cre]
        pi = ltab[24:32, cim]

        def blk(i, c, cre=cre, cim=cim, pows=pows, pr=pr, pi=pi):
            cr, ci = c
            b = (nblk - 1 - i) if reverse else i
            r0 = pl.multiple_of(b * SUBLANES, SUBLANES)
            xr = work[pl.ds(r0, SUBLANES), cre]
            xi = work[pl.ds(r0, SUBLANES), cim]
            for k, s in enumerate((1, 2, 4)):
                lr, li = pows[k]
                if reverse:
                    keep = row < SUBLANES - s
                    sr = jnp.where(keep, pltpu.roll(xr, SUBLANES - s, 0), 0.0)
                    si = jnp.where(keep, pltpu.roll(xi, SUBLANES - s, 0), 0.0)
                else:
                    keep = row >= s
                    sr = jnp.where(keep, pltpu.roll(xr, s, 0), 0.0)
                    si = jnp.where(keep, pltpu.roll(xi, s, 0), 0.0)
                xr, xi = xr + lr * sr - li * si, xi + lr * si + li * sr
            xr, xi = xr + pr * cr - pi * ci, xi + pr * ci + pi * cr
            work[pl.ds(r0, SUBLANES), cre] = xr
            work[pl.ds(r0, SUBLANES), cim] = xi
            edge = 0 if reverse else SUBLANES - 1
            return _bcast_row(xr, edge, SUBLANES), _bcast_row(xi, edge, SUBLANES)

        cr, ci = lax.fori_loop(0, nblk, blk, (carry[:, cre], carry[:, cim]))
        carry[:, cre] = cr
        carry[:, cim] = ci


def _s5conv_fwd(z, convw, convb, bbmat, ccmat, dvec, ltab):
    def body(z_ref, cw_ref, cb_ref, bb_ref, cc_ref, d_ref, lt_ref, ya_ref, ys_ref, hs_ref,
             work, carry, tail):
        c = pl.program_id(0)

        @pl.when(c == 0)
        def _():
            carry[...] = jnp.zeros_like(carry)
            tail[...] = jnp.zeros_like(tail)

        zb = z_ref[:, 0:CONV_W]
        v = z_ref[:, CONV_W:2 * CONV_W] * z_ref[:, 2 * CONV_W:3 * CONV_W]
        us = z_ref[:, 3 * CONV_W:4 * CONV_W]
        v1, v2 = _conv_taps(v, tail[...])
        tail[...] = v[CHUNK - 8:CHUNK, :]
        y = cw_ref[0:1, :] * v2 + cw_ref[1:2, :] * v1 + cw_ref[2:3, :] * v
        ya_ref[...] = zb * (y + cb_ref[...])

        work[...] = _dot(us.astype(BF16), bb_ref[...])
        _scan_chunk(work, lt_ref, carry, reverse=False)
        hs = work[...].astype(BF16)
        hs_ref[...] = hs
        ys_ref[...] = _dot(hs, cc_ref[...]) + d_ref[...] * us

    return pl.pallas_call(
        body, name="s5conv_fwd",
        grid=(N_CHUNKS,),
        in_specs=[pl.BlockSpec((CHUNK, IN_COLS), lambda c: (c, 0)),
                  pl.BlockSpec((3, CONV_W), lambda c: (0, 0)),
                  pl.BlockSpec((1, CONV_W), lambda c: (0, 0)),
                  pl.BlockSpec((SSM_W, 2 * N_STATE), lambda c: (0, 0)),
                  pl.BlockSpec((2 * N_STATE, SSM_W), lambda c: (0, 0)),
                  pl.BlockSpec((1, SSM_W), lambda c: (0, 0)),
                  pl.BlockSpec((32, 2 * N_STATE), lambda c: (0, 0))],
        out_specs=[pl.BlockSpec((CHUNK, CONV_W), lambda c: (c, 0)),
                   pl.BlockSpec((CHUNK, SSM_W), lambda c: (c, 0)),
                   pl.BlockSpec((CHUNK, 2 * N_STATE), lambda c: (c, 0))],
        out_shape=[jax.ShapeDtypeStruct((SEQ, CONV_W), F32),
                   jax.ShapeDtypeStruct((SEQ, SSM_W), F32),
                   jax.ShapeDtypeStruct((SEQ, 2 * N_STATE), BF16)],
        scratch_shapes=[pltpu.VMEM((CHUNK, 2 * N_STATE), F32),
                        pltpu.VMEM((8, 2 * N_STATE), F32),
                        pltpu.VMEM((8, CONV_W), F32)],
        compiler_params=_cparams(("arbitrary",)),
    )(z, convw, convb, bbmat, ccmat, dvec, ltab)


def _s5conv_bwd(z, hs, dya, dys, convw, convb, bbmat, ccmat, dvec, ltab_rev):
    nc = N_CHUNKS
    hb = 16

    def body(z_ref, zp_ref, hs_ref, hp_ref, dya_ref, dys_ref, cw_ref, cb_ref, bb_ref, cc_ref, d_ref, lt_ref,
             dz_ref, g_ref, us_ref, dyb_ref, dl_ref, dcw_ref, work, carry, head):
        i = pl.program_id(0)
        first_chunk = i == nc - 1

        @pl.when(i == 0)
        def _():
            carry[...] = jnp.zeros_like(carry)
            head[...] = jnp.zeros_like(head)
            dl_ref[...] = jnp.zeros_like(dl_ref)
            dcw_ref[...] = jnp.zeros_like(dcw_ref)

        us = z_ref[:, 3 * CONV_W:4 * CONV_W]
        dy = dys_ref[...]
        dy_bf = dy.astype(BF16)
        us_ref[0] = us.astype(BF16)
        dyb_ref[0] = dy_bf

        work[...] = _dot_nt(dy_bf, cc_ref[...])
        _scan_chunk(work, lt_ref, carry, reverse=True)
        gg = work[...]
        gg_bf = gg.astype(BF16)
        g_ref[0] = gg_bf
        dus = d_ref[...] * dy + _dot_nt(gg_bf, bb_ref[...])

        hcur = hs_ref[...].astype(F32)
        hlast = hp_ref[...].astype(F32)[hb - 1:hb, :]
        hlast = jnp.where(first_chunk, 0.0, hlast)
        rid = _row_ids(CHUNK, 2 * N_STATE)
        hprev = jnp.where(rid == 0, jnp.broadcast_to(hlast, (CHUNK, 2 * N_STATE)), pltpu.roll(hcur, 1, 0))
        gr, gi = gg[:, :N_STATE], gg[:, N_STATE:]
        hr, hi = hprev[:, :N_STATE], hprev[:, N_STATE:]
        dl_ref[:, :N_STATE] += (gr * hr + gi * hi).reshape(CHUNK // 8, 8, N_STATE).sum(axis=0)
        dl_ref[:, N_STATE:] += (gi * hr - gr * hi).reshape(CHUNK // 8, 8, N_STATE).sum(axis=0)

        @pl.when(i == nc - 1)
        def _():
            dl_ref[0:1, :] = jnp.sum(dl_ref[...], axis=0, keepdims=True)

        zb = z_ref[:, 0:CONV_W]
        zc = z_ref[:, CONV_W:2 * CONV_W]
        zv = z_ref[:, 2 * CONV_W:3 * CONV_W]
        v = zc * zv
        vtail = jnp.where(first_chunk, 0.0, zp_ref[:, CONV_W:2 * CONV_W] * zp_ref[:, 2 * CONV_W:3 * CONV_W])
        v1, v2 = _conv_taps(v, vtail)
        w0, w1, w2 = cw_ref[0:1, :], cw_ref[1:2, :], cw_ref[2:3, :]
        y = w0 * v2 + w1 * v1 + w2 * v
        dya_v = dya_ref[...]
        dzb = dya_v * (y + cb_ref[...])
        dyc = dya_v * zb
        hd = head[...]
        rc = _row_ids(CHUNK, CONV_W)
        n1 = jnp.where(rc == CHUNK - 1, _bcast_row(hd, 0, CHUNK), pltpu.roll(dyc, CHUNK - 1, 0))
        n2 = jnp.where(rc == CHUNK - 1, _bcast_row(hd, 1, CHUNK),
                       jnp.where(rc == CHUNK - 2, _bcast_row(hd, 0, CHUNK), pltpu.roll(dyc, CHUNK - 2, 0)))
        head[...] = dyc[0:8, :]
        dv = w2 * dyc + w1 * n1 + w0 * n2
        dz_ref[:, 0:CONV_W] = dzb.astype(BF16)
        dz_ref[:, CONV_W:2 * CONV_W] = (dv * zv).astype(BF16)
        dz_ref[:, 2 * CONV_W:3 * CONV_W] = (dv * zc).astype(BF16)
        dz_ref[:, 3 * CONV_W:4 * CONV_W] = dus.astype(BF16)
        dcw_ref[0:1, :] += jnp.sum(dyc * v2, axis=0, keepdims=True)
        dcw_ref[1:2, :] += jnp.sum(dyc * v1, axis=0, keepdims=True)
        dcw_ref[2:3, :] += jnp.sum(dyc * v, axis=0, keepdims=True)
        dcw_ref[3:4, :] += jnp.sum(dyc, axis=0, keepdims=True)
        dcw_ref[4:5, :] += jnp.sum(dy * us, axis=0, keepdims=True)

    rev = lambda i: nc - 1 - i
    return pl.pallas_call(
        body, name="s5conv_bwd",
        grid=(nc,),
        in_specs=[pl.BlockSpec((CHUNK, IN_COLS), lambda i: (rev(i), 0)),
                  pl.BlockSpec((8, IN_COLS), lambda i: (jnp.maximum(rev(i) * (CHUNK // 8) - 1, 0), 0)),
                  pl.BlockSpec((CHUNK, 2 * N_STATE), lambda i: (rev(i), 0)),
                  pl.BlockSpec((hb, 2 * N_STATE), lambda i: (jnp.maximum(rev(i) * (CHUNK // hb) - 1, 0), 0)),
                  pl.BlockSpec((CHUNK, CONV_W), lambda i: (rev(i), 0)),
                  pl.BlockSpec((CHUNK, SSM_W), lambda i: (rev(i), 0)),
                  pl.BlockSpec((3, CONV_W), lambda i: (0, 0)),
                  pl.BlockSpec((1, CONV_W), lambda i: (0, 0)),
                  pl.BlockSpec((SSM_W, 2 * N_STATE), lambda i: (0, 0)),
                  pl.BlockSpec((2 * N_STATE, SSM_W), lambda i: (0, 0)),
                  pl.BlockSpec((1, SSM_W), lambda i: (0, 0)),
                  pl.BlockSpec((32, 2 * N_STATE), lambda i: (0, 0))],
        out_specs=[pl.BlockSpec((CHUNK, IN_COLS), lambda i: (rev(i), 0)),
                   pl.BlockSpec((1, CHUNK, 2 * N_STATE), lambda i: (0, rev(i), 0)),
                   pl.BlockSpec((1, CHUNK, SSM_W), lambda i: (0, rev(i), 0)),
                   pl.BlockSpec((1, CHUNK, SSM_W), lambda i: (0, rev(i), 0)),
                   pl.BlockSpec((8, 2 * N_STATE), lambda i: (0, 0)),
                   pl.BlockSpec((8, CONV_W), lambda i: (0, 0))],
        out_shape=[jax.ShapeDtypeStruct((SEQ, IN_COLS), BF16),
                   jax.ShapeDtypeStruct((1, SEQ, 2 * N_STATE), BF16),
                   jax.ShapeDtypeStruct((1, SEQ, SSM_W), BF16),
                   jax.ShapeDtypeStruct((1, SEQ, SSM_W), BF16),
                   jax.ShapeDtypeStruct((8, 2 * N_STATE), F32),
                   jax.ShapeDtypeStruct((8, CONV_W), F32)],
        scratch_shapes=[pltpu.VMEM((CHUNK, 2 * N_STATE), F32),
                        pltpu.VMEM((8, 2 * N_STATE), F32),
                        pltpu.VMEM((8, CONV_W), F32)],
        compiler_params=_cparams(("arbitrary",)),
    )(z, z, hs, hs, dya, dys, convw, convb, bbmat, ccmat, dvec, ltab_rev)


def _mix_out_fwd(h, ya, ys, gluw, glub, con, son, wout):
    tm = TOK_TILE

    def body(h_ref, ya_ref, ys_ref, gw_ref, gb_ref, con_ref, son_ref, wo_ref, o_ref):
        zg, _ = _gelu(ys_ref[...])
        q = _dot(zg.astype(BF16), gw_ref[...]) + gb_ref[...]
        out_s = zg * _sigmoid(q)
        na, _ = _rms_stats(ya_ref[...])
        ns, _ = _rms_stats(out_s)
        o_ref[...] = (h_ref[...]
                      + _dot((na * con_ref[...]).astype(BF16), wo_ref[0:CONV_W, :])
                      + _dot((ns * son_ref[...]).astype(BF16), wo_ref[CONV_W:2 * CONV_W, :]))

    row = lambda m: (m, 0)
    fixed = lambda m: (0, 0)
    return pl.pallas_call(
        body, name="mix_out_fwd",
        grid=(SEQ // tm,),
        in_specs=[pl.BlockSpec((tm, D_MODEL), row), pl.BlockSpec((tm, CONV_W), row), pl.BlockSpec((tm, SSM_W), row),
                  pl.BlockSpec((SSM_W, SSM_W), fixed), pl.BlockSpec((1, SSM_W), fixed),
                  pl.BlockSpec((1, CONV_W), fixed), pl.BlockSpec((1, SSM_W), fixed),
                  pl.BlockSpec((D_MODEL, D_MODEL), fixed)],
        out_specs=pl.BlockSpec((tm, D_MODEL), row),
        out_shape=jax.ShapeDtypeStruct((SEQ, D_MODEL), F32),
        compiler_params=_cparams(("parallel",)),
    )(h, ya, ys, gluw, glub, con, son, wout)


def _mix_out_bwd(dh, ya, ys, gluw, glub, con, son, wout):
    tm = TOK_TILE

    def body(dh_ref, ya_ref, ys_ref, gw_ref, gb_ref, con_ref, son_ref, wo_ref,
             dya_ref, dys_ref, yc_ref, dhb_ref, zg_ref, dq_ref, part_ref):
        ysv = ys_ref[...]
        zg, th = _gelu(ysv)
        zg_bf = zg.astype(BF16)
        s = _sigmoid(_dot(zg_bf, gw_ref[...]) + gb_ref[...])
        out_s = zg * s
        na, ra = _rms_stats(ya_ref[...])
        ns, rs = _rms_stats(out_s)
        dh_bf = dh_ref[...].astype(BF16)
        yc_ref[0, :, 0:CONV_W] = (na * con_ref[...]).astype(BF16)
        yc_ref[0, :, CONV_W:2 * CONV_W] = (ns * son_ref[...]).astype(BF16)
        dhb_ref[0] = dh_bf
        dca = _dot_nt(dh_bf, wo_ref[0:CONV_W, :])
        dcs = _dot_nt(dh_bf, wo_ref[CONV_W:2 * CONV_W, :])
        dya, dcon = _rms_bwd(dca, na, ra, con_ref[...])
        dos, dson = _rms_bwd(dcs, ns, rs, son_ref[...])
        dya_ref[...] = dya
        dq = dos * zg * s * (1.0 - s)
        dq_bf = dq.astype(BF16)
        dzg = dos * s + _dot_nt(dq_bf, gw_ref[...])
        dys_ref[...] = dzg * _gelu_grad(ysv, th)
        zg_ref[0] = zg_bf
        dq_ref[0] = dq_bf
        rid = _row_ids(SUBLANES, SSM_W)
        part = jnp.zeros((SUBLANES, SSM_W), F32)
        for i, rowv in enumerate((dcon, dson, jnp.sum(dq, axis=0, keepdims=True))):
            part = jnp.where(rid == i, jnp.broadcast_to(rowv, (SUBLANES, SSM_W)), part)
        _accumulate(part_ref, pl.program_id(0) == 0, part)

    row = lambda m: (m, 0)
    fixed = lambda m: (0, 0)
    lead = lambda m: (0, m, 0)
    return pl.pallas_call(
        body, name="mix_out_bwd",
        grid=(SEQ // tm,),
        in_specs=[pl.BlockSpec((tm, D_MODEL), row), pl.BlockSpec((tm, CONV_W), row), pl.BlockSpec((tm, SSM_W), row),
                  pl.BlockSpec((SSM_W, SSM_W), fixed), pl.BlockSpec((1, SSM_W), fixed),
                  pl.BlockSpec((1, CONV_W), fixed), pl.BlockSpec((1, SSM_W), fixed),
                  pl.BlockSpec((D_MODEL, D_MODEL), fixed)],
        out_specs=[pl.BlockSpec((tm, CONV_W), row), pl.BlockSpec((tm, SSM_W), row),
                   pl.BlockSpec((1, tm, D_MODEL), lead), pl.BlockSpec((1, tm, D_MODEL), lead),
                   pl.BlockSpec((1, tm, SSM_W), lead), pl.BlockSpec((1, tm, SSM_W), lead),
                   pl.BlockSpec((8, SSM_W), fixed)],
        out_shape=[jax.ShapeDtypeStruct((SEQ, CONV_W), F32), jax.ShapeDtypeStruct((SEQ, SSM_W), F32),
                   jax.ShapeDtypeStruct((1, SEQ, D_MODEL), BF16), jax.ShapeDtypeStruct((1, SEQ, D_MODEL), BF16),
                   jax.ShapeDtypeStruct((1, SEQ, SSM_W), BF16), jax.ShapeDtypeStruct((1, SEQ, SSM_W), BF16),
                   jax.ShapeDtypeStruct((8, SSM_W), F32)],
        compiler_params=_cparams(("arbitrary",)),
    )(dh, ya, ys, gluw, glub, con, son, wout)


def _ple_fwd(h, g, p, wgate, wprojt):
    tm = TOK_TILE

    def body(h_ref, g_ref, p_ref, wg_ref, wp_ref, o_ref):
        x = h_ref[...]
        xh, _ = _rms_stats(x)
        s = _sigmoid(_dot((xh * g_ref[...]).astype(BF16), wg_ref[...]))
        o_ref[...] = x + _dot_nt(p_ref[...].astype(BF16), wp_ref[...]) * s

    row = lambda m: (m, 0)
    fixed = lambda m: (0, 0)
    return pl.pallas_call(
        body, name="ple_fwd",
        grid=(SEQ // tm,),
        in_specs=[pl.BlockSpec((tm, D_MODEL), row), pl.BlockSpec((1, D_MODEL), fixed), pl.BlockSpec((tm, PLE_DIM), row),
                  pl.BlockSpec((D_MODEL, D_MODEL), fixed), pl.BlockSpec((D_MODEL, PLE_DIM), fixed)],
        out_specs=pl.BlockSpec((tm, D_MODEL), row),
        out_shape=jax.ShapeDtypeStruct((SEQ, D_MODEL), F32),
        compiler_params=_cparams(("parallel",)),
    )(h, g, p, wgate, wprojt)


def _ple_bwd(h, g, p, dh, wgate, wprojt):
    tm = TOK_TILE

    def body(h_ref, g_ref, p_ref, dh_ref, wg_ref, wp_ref, o_ref, u_ref, dq_ref, dpp_ref, pb_ref, dg_ref):
        xh, r = _rms_stats(h_ref[...])
        u = (xh * g_ref[...]).astype(BF16)
        s = _sigmoid(_dot(u, wg_ref[...]))
        p_bf = p_ref[...].astype(BF16)
        pp = _dot_nt(p_bf, wp_ref[...])
        dhv = dh_ref[...]
        dq = (dhv * pp * s * (1.0 - s)).astype(BF16)
        u_ref[0] = u
        dq_ref[0] = dq
        dpp_ref[0] = (dhv * s).astype(BF16)
        pb_ref[0] = p_bf
        dx, dg = _rms_bwd(_dot_nt(dq, wg_ref[...]), xh, r, g_ref[...])
        o_ref[...] = dhv + dx
        _accumulate(dg_ref, pl.program_id(0) == 0, dg)

    row = lambda m: (m, 0)
    fixed = lambda m: (0, 0)
    lead = lambda m: (0, m, 0)
    big = jax.ShapeDtypeStruct((1, SEQ, D_MODEL), BF16)
    return pl.pallas_call(
        body, name="ple_bwd",
        grid=(SEQ // tm,),
        in_specs=[pl.BlockSpec((tm, D_MODEL), row), pl.BlockSpec((1, D_MODEL), fixed), pl.BlockSpec((tm, PLE_DIM), row),
                  pl.BlockSpec((tm, D_MODEL), row),
                  pl.BlockSpec((D_MODEL, D_MODEL), fixed), pl.BlockSpec((D_MODEL, PLE_DIM), fixed)],
        out_specs=[pl.BlockSpec((tm, D_MODEL), row),
                   pl.BlockSpec((1, tm, D_MODEL), lead), pl.BlockSpec((1, tm, D_MODEL), lead),
                   pl.BlockSpec((1, tm, D_MODEL), lead), pl.BlockSpec((1, tm, PLE_DIM), lead),
                   pl.BlockSpec((1, D_MODEL), fixed)],
        out_shape=[jax.ShapeDtypeStruct((SEQ, D_MODEL), F32), big, big, big,
                   jax.ShapeDtypeStruct((1, SEQ, PLE_DIM), BF16),
                   jax.ShapeDtypeStruct((1, D_MODEL), F32)],
        compiler_params=_cparams(("arbitrary",)),
    )(h, g, p, dh, wgate, wprojt)


def _final_loss(h, g, target):
    tm = TOK_TILE

    def body(h_ref, g_ref, t_ref, loss_ref, dh_ref, dg_ref):
        first = pl.program_id(0) == 0
        xh, r = _rms_stats(h_ref[...])
        diff = xh * g_ref[...] - t_ref[...]
        part = 0.5 * jnp.sum(jnp.mean(diff * diff, axis=-1, keepdims=True), axis=0, keepdims=True)
        _accumulate(loss_ref, first, jnp.broadcast_to(part, (SUBLANES, LANES)))
        dx, dg = _rms_bwd(diff * (1.0 / D_MODEL), xh, r, g_ref[...])
        dh_ref[...] = dx
        _accumulate(dg_ref, first, dg)

    row = lambda m: (m, 0)
    fixed = lambda m: (0, 0)
    return pl.pallas_call(
        body, name="final_loss",
        grid=(SEQ // tm,),
        in_specs=[pl.BlockSpec((tm, D_MODEL), row), pl.BlockSpec((1, D_MODEL), fixed),
                  pl.BlockSpec((tm, D_MODEL), row)],
        out_specs=[pl.BlockSpec((SUBLANES, LANES), fixed),
                   pl.BlockSpec((tm, D_MODEL), row),
                   pl.BlockSpec((1, D_MODEL), fixed)],
        out_shape=[jax.ShapeDtypeStruct((SUBLANES, LANES), F32),
                   jax.ShapeDtypeStruct((SEQ, D_MODEL), F32),
                   jax.ShapeDtypeStruct((1, D_MODEL), F32)],
        compiler_params=_cparams(("arbitrary",)),
    )(h, g, target)


def _disc(ar, ai, ldt):
    dt = jnp.exp(ldt)
    mag = jnp.exp(ar * dt)
    ph = ai * dt
    lr, li = mag * jnp.cos(ph), mag * jnp.sin(ph)
    nr, ni = lr - 1.0, li
    den = ar * ar + ai * ai
    return lr, li, (nr * ar + ni * ai) / den, (ni * ar - nr * ai) / den


def _s5_disc(a, ldt, a_rep, ldt_rep, b):
    def body(a_ref, l_ref, ar_ref, lr_ref, b_ref, pw_ref, bb_ref):
        lr, li, _, _ = _disc(a_ref[0], a_ref[1], l_ref[...])
        pr, pi = lr, li
        for k in range(8):
            pw_ref[k] = pr
            pw_ref[8 + k] = pi
            pr, pi = pr * lr - pi * li, pr * li + pi * lr
        _, _, fr, fi = _disc(ar_ref[0], ar_ref[1], lr_ref[...])
        br, bi = b_ref[0], b_ref[1]
        bb_ref[0] = fr * br - fi * bi
        bb_ref[1] = fr * bi + fi * br

    return pl.pallas_call(
        body, name="s5_disc",
        out_shape=[jax.ShapeDtypeStruct((16, SSM_GROUPS, SSM_STATE), F32),
                   jax.ShapeDtypeStruct((2, SSM_GROUPS, SSM_STATE * SSM_GROUP), F32)],
    )(a, ldt, a_rep, ldt_rep, b)


def _dot_exact(x, sel):
    hi = x.astype(BF16)
    r1 = x - hi.astype(F32)
    mid = r1.astype(BF16)
    lo = (r1 - mid.astype(F32)).astype(BF16)
    return _dot(hi, sel) + _dot(mid, sel) + _dot(lo, sel)


def _s5_disc_bwd(a, ldt, a_rep, ldt_rep, b, dl, dbb, sel):
    def body(a_ref, l_ref, ar_ref, lr_ref, b_ref, dl_ref, dbb_ref, sel_ref, da_ref, dldt_ref, db_ref):
        _, _, fr, fi = _disc(ar_ref[0], ar_ref[1], lr_ref[...])
        br, bi = b_ref[0], b_ref[1]
        dr, di = dbb_ref[0], dbb_ref[1]
        db_ref[0] = fr * dr + fi * di
        db_ref[1] = fr * di - fi * dr
        dfr = _dot_exact(dr * br + di * bi, sel_ref[...])
        dfi = _dot_exact(di * br - dr * bi, sel_ref[...])
        _, vjp = jax.vjp(_disc, a_ref[0], a_ref[1], l_ref[...])
        dar, dai, dldt = vjp((dl_ref[0], dl_ref[1], dfr, dfi))
        da_ref[0] = dar
        da_ref[1] = dai
        dldt_ref[...] = jnp.sum(dldt, axis=1, keepdims=True)

    return pl.pallas_call(
        body, name="s5_disc_bwd",
        out_shape=[jax.ShapeDtypeStruct((2, SSM_GROUPS, SSM_STATE), F32),
                   jax.ShapeDtypeStruct((SSM_GROUPS, 1), F32),
                   jax.ShapeDtypeStruct((2, SSM_GROUPS, SSM_STATE * SSM_GROUP), F32)],
    )(a, ldt, a_rep, ldt_rep, b, dl, dbb, sel)


def _row_block(rows, cap=512):
    for bm in range(min(cap, rows), 0, -1):
        if rows % bm == 0 and (bm % 8 == 0 or bm == rows):
            return bm
    return rows


def _pair_sum(fulls, got, segs):
    ns = len(segs)
    offs = _seg_offsets(segs)
    _, rtot, c = got.shape
    parts = 2
    pr = rtot // parts
    assert pr * parts == rtot and pr % 16 == 0
    pieces = [[] for _ in range(parts)]
    for a, (n, r) in enumerate(segs):
        for m in range(n):
            lo = offs[a] + m * r
            for h in range(parts):
                clo, chi = max(lo, h * pr), min(lo + r, (h + 1) * pr)
                if chi > clo:
                    pieces[h].append((a, m, clo - lo, clo - h * pr, chi - clo))
    n_sems = max(len(ps) for ps in pieces)

    def body(*refs):
        srcs = refs[:ns]
        got_ref, p32_ref, pbf_ref, own_v, sems = refs[ns:]
        k = pl.program_id(0)
        h = pl.program_id(1)
        dev = 2 * k + lax.axis_index("c")
        for hh in range(parts):
            @pl.when(h == hh)
            def _(hh=hh):
                cps = []
                for i, (a, m, so, do, rows) in enumerate(pieces[hh]):
                    start = pl.multiple_of(dev * segs[a][1] + so, 16)
                    cps.append(pltpu.make_async_copy(srcs[a].at[m, pl.ds(start, rows), :],
                                                     own_v.at[pl.ds(do, rows), :], sems.at[i]))
                for cp in cps:
                    cp.start()
                for cp in cps:
                    cp.wait()
        s = own_v[...].astype(F32) + got_ref[0].astype(F32)
        p32_ref[0] = s
        pbf_ref[0] = s.astype(BF16)

    spec = pl.BlockSpec((1, pr, c), lambda k, h: (k, h, 0))
    return pl.pallas_call(
        body, name="pair_sum",
        grid=(4, parts),
        in_specs=[ANY] * ns + [spec], out_specs=[spec, spec],
        out_shape=[jax.ShapeDtypeStruct(got.shape, F32), jax.ShapeDtypeStruct(got.shape, BF16)],
        scratch_shapes=[pltpu.VMEM((pr, c), BF16), pltpu.SemaphoreType.DMA((n_sems,))],
        compiler_params=_cparams(("arbitrary", "arbitrary")),
    )(*fulls, got)


def _chip_sum(chip, p32, rb):
    _, r, c = p32.shape
    bm = _row_block(r)

    def body(chip_ref, o_ref, r_ref, s_ref):
        s_ref[...] = ((o_ref[0] + r_ref[0].astype(F32)) + r_ref[1].astype(F32)) + r_ref[2].astype(F32)

    return pl.pallas_call(
        body, name="chip_sum",
        grid_spec=pltpu.PrefetchScalarGridSpec(
            num_scalar_prefetch=1, grid=(r // bm,),
            in_specs=[pl.BlockSpec((1, bm, c), lambda k, chip_ref: (chip_ref[0], k, 0)),
                      pl.BlockSpec((3, bm, c), lambda k, chip_ref: (0, k, 0))],
            out_specs=pl.BlockSpec((bm, c), lambda k, chip_ref: (k, 0))),
        out_shape=jax.ShapeDtypeStruct((r, c), F32),
        compiler_params=_cparams(("parallel",)),
    )(chip, p32, rb)


def _sum8(x):
    _, r, c = x.shape
    bm = _row_block(r)

    def body(x_ref, s_ref):
        s = x_ref[0]
        for d in range(1, N_DEV):
            s = s + x_ref[d]
        s_ref[...] = s

    return pl.pallas_call(
        body, name="sum8",
        grid=(r // bm,),
        in_specs=[pl.BlockSpec((N_DEV, bm, c), lambda k: (0, k, 0))],
        out_specs=pl.BlockSpec((bm, c), lambda k: (k, 0)),
        out_shape=jax.ShapeDtypeStruct((r, c), F32),
        compiler_params=_cparams(("parallel",)),
    )(x)


def _adamw(w, g, m, v):
    r, c = w.shape
    bm = _row_block(r)
    bc1 = 1.0 - ADAM_B1 ** ADAM_STEP
    bc2 = 1.0 - ADAM_B2 ** ADAM_STEP

    def body(w_ref, g_ref, m_ref, v_ref, d_ref, nm_ref, nv_ref):
        gv = g_ref[...]
        nm = ADAM_B1 * m_ref[...] + (1.0 - ADAM_B1) * gv
        nv = ADAM_B2 * v_ref[...] + (1.0 - ADAM_B2) * (gv * gv)
        nm_ref[...] = nm
        nv_ref[...] = nv
        d_ref[...] = -ADAM_LR * ((nm / bc1) / (jnp.sqrt(nv / bc2) + ADAM_EPS) + ADAM_WD * w_ref[...])

    spec = pl.BlockSpec((bm, c), lambda k: (k, 0))
    shp = jax.ShapeDtypeStruct((r, c), F32)
    return pl.pallas_call(
        body, name="adamw",
        grid=(r // bm,),
        in_specs=[spec] * 4, out_specs=[spec] * 3, out_shape=[shp] * 3,
        compiler_params=_cparams(("parallel",)),
    )(w, g, m, v)


def _mesh_pos():
    return lax.axis_index("x"), lax.axis_index("y"), lax.axis_index("c")


def _dev_index(p):
    return 4 * p[0] + 2 * p[1] + p[2]


def _seg_offsets(segs):
    offs, o = [], 0
    for n, r in segs:
        offs.append(o)
        o += n * r
    return offs


def _remote(src, dst, send_sem, recv_sem, to):
    return pltpu.make_async_remote_copy(src_ref=src, dst_ref=dst, send_sem=send_sem, recv_sem=recv_sem,
                                        device_id=to, device_id_type=MESH)


def _allgather(pack, segs, name):
    rtot, c = pack.shape
    ns = len(segs)
    offs = _seg_offsets(segs)
    assert rtot == sum(n * r for n, r in segs)

    def body(pack_ref, *refs):
        outs = refs[:ns]
        send_sems, recv_sems, local_sem = refs[ns:]
        x, y, cc = _mesh_pos()
        me, sib = (x, y, cc), (x, y, 1 - cc)
        chips = [(1 - x, y), (x, 1 - y), (1 - x, 1 - y)]

        def pieces(dev, from_pack):
            res = []
            for a, (n, r) in enumerate(segs):
                for m in range(n):
                    dst = outs[a].at[m, pl.ds(pl.multiple_of(dev * r, r), r), :]
                    src = pack_ref.at[pl.ds(offs[a] + m * r, r), :] if from_pack else dst
                    res.append((src, dst))
            return res

        def push(k, dev, to, from_pack):
            for s, d in pieces(dev, from_pack):
                _remote(s, d, send_sems.at[k], recv_sems.at[k], to).start()

        def whole(k):
            return _remote(pack_ref, pack_ref, send_sems.at[k], recv_sems.at[k], me)

        my_dev = _dev_index(me)
        for s, d in pieces(my_dev, True):
            pltpu.make_async_copy(s, d, local_sem).start()
        push(0, my_dev, sib, True)
        for j, chip in enumerate(chips):
            push(1 + j, my_dev, (*chip, cc), True)
        for j, chip in enumerate(chips):
            whole(1 + j).wait_recv()
            push(4 + j, _dev_index((*chip, cc)), sib, False)
        whole(0).wait_recv()
        for j in range(3):
            whole(4 + j).wait_recv()
        for k in range(7):
            whole(k).wait_send()
        pltpu.make_async_copy(pack_ref, pack_ref, local_sem).wait()

    return pl.pallas_call(
        body, name=name,
        in_specs=[ANY], out_specs=[ANY] * ns,
        out_shape=[jax.ShapeDtypeStruct((n, N_DEV * r, c), pack.dtype) for n, r in segs],
        scratch_shapes=[pltpu.SemaphoreType.DMA((7,)), pltpu.SemaphoreType.DMA((7,)), pltpu.SemaphoreType.DMA],
    )(pack)


HBM = pl.BlockSpec(memory_space=pltpu.HBM)
SEM = pl.BlockSpec(memory_space=pltpu.SEMAPHORE)
VMEM_WHOLE = pl.BlockSpec(memory_space=pltpu.VMEM)
EFFECT = pltpu.SideEffectType.DATAFLOW_SIDE_EFFECTING


def _hbm(a):
    return pltpu.with_memory_space_constraint(a, pltpu.HBM)


def _ag_start(pack, segs, after, name):
    rtot, c = pack.shape
    ns = len(segs)
    offs = _seg_offsets(segs)

    def body(pack_ref, *refs):
        lands = refs[:ns]
        send_sems, recv_sems = refs[ns + 1], refs[ns + 2]
        token = refs[-1]
        x, y, cc = _mesh_pos()
        my_dev = _dev_index((x, y, cc))
        targets = [(x, y, 1 - cc), (1 - x, y, cc), (x, 1 - y, cc), (1 - x, 1 - y, cc)]
        for k, to in enumerate(targets):
            for a, (n, r) in enumerate(segs):
                for m in range(n):
                    _remote(pack_ref.at[pl.ds(offs[a] + m * r, r), :],
                            lands[a].at[m, pl.ds(pl.multiple_of(my_dev * r, r), r), :],
                            send_sems.at[k], recv_sems.at[k], to).start()
        token[...] = jnp.zeros_like(token)

    land_shapes = [(n, N_DEV * r, c) for n, r in segs]
    outs = pl.pallas_call(
        body, name=name,
        in_specs=[HBM] * (1 + ns) + [ANY],
        out_specs=[SEM, SEM, HBM] + [HBM] * ns + [VMEM_WHOLE],
        out_shape=[pltpu.SemaphoreType.DMA((4,)), pltpu.SemaphoreType.DMA((4,)), pltpu.HBM(pack.shape, pack.dtype)]
        + [pltpu.HBM(s, pack.dtype) for s in land_shapes] + [jax.ShapeDtypeStruct((SUBLANES, LANES), F32)],
        input_output_aliases={0: 2, **{1 + i: 3 + i for i in range(ns)}},
        compiler_params=pltpu.CompilerParams(has_side_effects=EFFECT),
    )(_hbm(pack), *[_hbm(lax.empty(s, pack.dtype)) for s in land_shapes], after)
    return outs[0], outs[1], outs[2], list(outs[3:3 + ns]), outs[-1]


def _ag_wait(send_sems, recv_sems, pack, lands, after, name):
    ns = len(lands)

    def body(pack_ref, *refs):
        send_ref, recv_ref = refs[ns], refs[ns + 1]
        me = _mesh_pos()
        for k in range(4):
            whole = _remote(pack_ref, pack_ref, send_ref.at[k], recv_ref.at[k], me)
            whole.wait_send()
            whole.wait_recv()

    outs = pl.pallas_call(
        body, name=name,
        in_specs=[HBM] * (1 + ns) + [SEM, SEM, ANY],
        out_specs=[HBM] * (1 + ns),
        out_shape=[pltpu.HBM(pack.shape, pack.dtype)] + [pltpu.HBM(a.shape, a.dtype) for a in lands],
        input_output_aliases={i: i for i in range(1 + ns)},
        compiler_params=pltpu.CompilerParams(has_side_effects=EFFECT),
    )(pack, *lands, send_sems, recv_sems, after)
    return outs[0], list(outs[1:])


def _ag_finish(pack, lands, segs):
    rtot, c = pack.shape
    ns = len(segs)
    offs = _seg_offsets(segs)

    def body(pack_ref, *refs):
        outs = refs[ns:2 * ns]
        stage, send_sems, recv_sems, local_sems = refs[2 * ns:]
        x, y, cc = _mesh_pos()
        me, sib = (x, y, cc), (x, y, 1 - cc)
        chips = [(1 - x, y), (x, 1 - y), (1 - x, 1 - y)]

        def rows(a, m, dev):
            return outs[a].at[m, pl.ds(pl.multiple_of(dev * segs[a][1], segs[a][1]), segs[a][1]), :]

        for j, chip in enumerate(chips):
            dev = _dev_index((*chip, cc))
            for a, (n, r) in enumerate(segs):
                for m in range(n):
                    _remote(rows(a, m, dev), rows(a, m, dev), send_sems.at[j], recv_sems.at[j], sib).start()
        load = pltpu.make_async_copy(pack_ref, stage, local_sems.at[0])
        load.start()
        load.wait()
        my_dev = _dev_index(me)
        for a, (n, r) in enumerate(segs):
            for m in range(n):
                pltpu.make_async_copy(stage.at[pl.ds(offs[a] + m * r, r), :], rows(a, m, my_dev), local_sems.at[1]).start()
        pltpu.make_async_copy(stage, pack_ref, local_sems.at[1]).wait()
        for j in range(3):
            _remote(pack_ref, pack_ref, send_sems.at[j], recv_sems.at[j], me).wait()

    outs = pl.pallas_call(
        body, name="ag_finish",
        in_specs=[ANY] * (1 + ns), out_specs=[ANY] * ns,
        out_shape=[jax.ShapeDtypeStruct(a.shape, a.dtype) for a in lands],
        input_output_aliases={1 + i: i for i in range(ns)},
        scratch_shapes=[pltpu.VMEM((rtot, c), pack.dtype), pltpu.SemaphoreType.DMA((3,)),
                        pltpu.SemaphoreType.DMA((3,)), pltpu.SemaphoreType.DMA((2,))],
        compiler_params=_cparams(None, 16),
    )(pack, *lands)
    return list(outs)


def _rs_chips_start(pbf, name):
    _, rtot, c = pbf.shape

    def body(pbf_ref, land_ref, send_sems, recv_sems, pbf_thru, land_thru, token):
        x, y, cc = _mesh_pos()
        for j, (cx, cy) in enumerate([(1 - x, y), (x, 1 - y), (1 - x, 1 - y)]):
            _remote(pbf_ref.at[2 * cx + cy], land_ref.at[j], send_sems.at[j], recv_sems.at[j], (cx, cy, cc)).start()
        token[...] = jnp.zeros_like(token)

    return pl.pallas_call(
        body, name=name,
        in_specs=[HBM, HBM],
        out_specs=[SEM, SEM, HBM, HBM, VMEM_WHOLE],
        out_shape=[pltpu.SemaphoreType.DMA((3,)), pltpu.SemaphoreType.DMA((3,)), pltpu.HBM(pbf.shape, pbf.dtype),
                   pltpu.HBM((3, rtot, c), pbf.dtype), jax.ShapeDtypeStruct((SUBLANES, LANES), F32)],
        input_output_aliases={0: 2, 1: 3},
        compiler_params=pltpu.CompilerParams(has_side_effects=EFFECT),
    )(_hbm(pbf), _hbm(lax.empty((3, rtot, c), pbf.dtype)))


def _rs_chips_wait(send_sems, recv_sems, pbf, land, after, name):
    def body(pbf_ref, land_ref, send_ref, recv_ref, after_ref, pbf_out, land_out):
        me = _mesh_pos()
        for j in range(3):
            cp = _remote(pbf_ref.at[0], land_ref.at[j], send_ref.at[j], recv_ref.at[j], me)
            cp.wait_send()
            cp.wait_recv()

    return pl.pallas_call(
        body, name=name,
        in_specs=[HBM, HBM, SEM, SEM, ANY], out_specs=[HBM, HBM],
        out_shape=[pltpu.HBM(pbf.shape, pbf.dtype), pltpu.HBM(land.shape, land.dtype)],
        input_output_aliases={0: 0, 1: 1},
        compiler_params=pltpu.CompilerParams(has_side_effects=EFFECT),
    )(pbf, land, send_sems, recv_sems, after)[1]


def _rs_sibling(fulls, segs):
    ns = len(segs)
    offs = _seg_offsets(segs)
    rtot = sum(n * r for n, r in segs)
    c = fulls[0].shape[-1]

    def body(*refs):
        srcs = refs[:ns]
        got_ref, send_sem, recv_sem = refs[ns:]
        x, y, cc = _mesh_pos()
        me, sib = (x, y, cc), (x, y, 1 - cc)
        for k in range(4):
            for a, (n, r) in enumerate(segs):
                for m in range(n):
                    theirs = srcs[a].at[m, pl.ds(pl.multiple_of((2 * k + 1 - cc) * r, r), r), :]
                    _remote(theirs, got_ref.at[k, pl.ds(offs[a] + m * r, r), :], send_sem, recv_sem, sib).start()
        _remote(got_ref, got_ref, send_sem, recv_sem, me).wait()

    return pl.pallas_call(
        body, name="rs_sibling",
        in_specs=[ANY] * ns, out_specs=ANY, out_shape=jax.ShapeDtypeStruct((4, rtot, c), fulls[0].dtype),
        scratch_shapes=[pltpu.SemaphoreType.DMA, pltpu.SemaphoreType.DMA],
    )(*fulls)


def _rs_chips(pbf):
    _, rtot, c = pbf.shape

    def body(pbf_ref, got_ref, send_sems, recv_sems):
        x, y, cc = _mesh_pos()
        chips = [(1 - x, y), (x, 1 - y), (1 - x, 1 - y)]
        cps = [_remote(pbf_ref.at[2 * cx + cy], got_ref.at[j], send_sems.at[j], recv_sems.at[j], (cx, cy, cc))
               for j, (cx, cy) in enumerate(chips)]
        for cp in cps:
            cp.start()
        for cp in cps:
            cp.wait()

    return pl.pallas_call(
        body, name="rs_chips",
        in_specs=[ANY], out_specs=ANY,
        out_shape=jax.ShapeDtypeStruct((3, rtot, c), BF16),
        scratch_shapes=[pltpu.SemaphoreType.DMA((3,)), pltpu.SemaphoreType.DMA((3,))],
    )(pbf)


def _tp(w):
    return jnp.swapaxes(w, -1, -2)


def _block_diag(blocks):
    g, r, c = blocks.shape
    eye = jnp.eye(g, dtype=blocks.dtype)
    return (blocks[:, :, None, :] * eye[:, None, :, None]).reshape(g * r, g * c)


def _diag_blocks(full, r, c):
    g = full.shape[0] // r
    return jnp.einsum('grgc->grc', full.reshape(g, r, g, c))


def _s5_prepare(a_re, a_im, log_dt, b_re, b_im, c_re, c_im):
    a = jnp.stack([a_re, a_im])
    ldt = jnp.broadcast_to(log_dt[:, None], (SSM_GROUPS, SSM_STATE))
    a_rep = jnp.repeat(a, SSM_GROUP, axis=-1)
    ldt_rep = jnp.broadcast_to(log_dt[:, None], (SSM_GROUPS, SSM_STATE * SSM_GROUP))
    b = jnp.stack([b_re.reshape(SSM_GROUPS, -1), b_im.reshape(SSM_GROUPS, -1)])
    disc_in = (a, ldt, a_rep, ldt_rep, b)
    pw, bb = _s5_disc(*disc_in)
    pr = pw[:8].reshape(8, N_STATE)
    pi = pw[8:].reshape(8, N_STATE)

    def table(pr, pi, edge):
        rows = [jnp.broadcast_to(jnp.concatenate([pr[k], pi[k]])[None], (8, 2 * N_STATE)) for k in (0, 1, 3)]
        return jnp.concatenate(rows + [edge], axis=0)

    ltab = table(pr, pi, jnp.concatenate([pr, pi], axis=1))
    ltab_rev = table(pr, -pi, jnp.concatenate([pr[::-1], -pi[::-1]], axis=1))
    bb4 = bb.reshape(2, SSM_GROUPS, SSM_STATE, SSM_GROUP)
    bbmat = jnp.concatenate([_block_diag(_tp(bb4[0])), _block_diag(_tp(bb4[1]))], axis=1).astype(BF16)
    ccmat = jnp.concatenate([_block_diag(_tp(c_re)), -_block_diag(_tp(c_im))], axis=0).astype(BF16)
    return disc_in, ltab, ltab_rev, bbmat, ccmat


def _layer_fwd(h, p_l, small, big):
    saved = {'h0': h}
    h, saved['gu1'] = _ffn_fwd(h, small['ffn1_norm'], big['ff1'])
    saved['h1'] = h
    z = _inproj_fwd(h, small['mix_norm'], big['wint'])
    ya, ys, hs = _s5conv_fwd(z, small['conv_w'], small['conv_b'], small['bbmat'], small['ccmat'], small['dvec'],
                             small['ltab'])
    saved.update(z=z, ya=ya, ys=ys, hs=hs)
    h = _mix_out_fwd(h, ya, ys, big['glu'], small['glu_b'], small['conv_out_norm'], small['ssm_out_norm'], big['wout'])
    saved['h2'] = h
    h, saved['gu2'] = _ffn_fwd(h, small['ffn2_norm'], big['ff2'])
    saved['h3'] = h
    h = _ple_fwd(h, small['ple_norm'], p_l, big['plg'], big['plpt'])
    return h, saved


def _ffn_bwd(h_in, g, dh, gu, w3):
    dh_in, dga, ud, dg = _ffn_bwd_act(h_in, g, dh, gu, w3)
    return dh_in, _matmul_tn(dga, ud, FF_BLOCK, BF16, "ffn_wgrad"), dg


def _layer_bwd(dh, p_l, small, big, saved):
    gs = {}
    dh, u, dq, dpp, pb, gs['ple_norm'] = _ple_bwd(saved['h3'], small['ple_norm'], p_l, dh, big['plg'], big['plpt'])
    d_plg = _matmul_tn(u, dq, 256, BF16, "ple_gate_wgrad")
    d_plpt = _matmul_tn(dpp, pb, 256, BF16, "ple_proj_wgrad")
    dh, d_ff2, gs['ffn2_norm'] = _ffn_bwd(saved['h2'], small['ffn2_norm'], dh, saved['gu2'], big['ff2'])

    dya, dys, ycat, dhb, zg, dq, part = _mix_out_bwd(dh, saved['ya'], saved['ys'], big['glu'], small['glu_b'],
                                                     small['conv_out_norm'], small['ssm_out_norm'], big['wout'])
    d_wout = _matmul_tn(ycat, dhb, 256, BF16, "w_out_wgrad")
    d_glu = _matmul_tn(zg, dq, 256, BF16, "glu_wgrad")
    dz, gadj, us, dyb, dl, dcw = _s5conv_bwd(saved['z'], saved['hs'], dya, dys, small['conv_w'], small['conv_b'],
                                             small['bbmat'], small['ccmat'], small['dvec'], small['ltab_rev'])
    d_bbt = _matmul_tn(gadj, us, 512, F32, "s5_b_wgrad")[0]
    d_cc = _matmul_tn(saved['hs'][None], dyb, 512, F32, "s5_c_wgrad")[0]
    dh, u, gs['mix_norm'] = _inproj_bwd(saved['h1'], small['mix_norm'], dh, dz, big['wint'])
    d_wint = _matmul_tn(dz[None], u, 256, BF16, "w_in_wgrad")
    dh, d_ff1, gs['ffn1_norm'] = _ffn_bwd(saved['h0'], small['ffn1_norm'], dh, saved['gu1'], big['ff1'])

    dbb = jnp.stack([_diag_blocks(d_bbt[:N_STATE], SSM_STATE, SSM_GROUP).reshape(SSM_GROUPS, -1),
                     _diag_blocks(d_bbt[N_STATE:], SSM_STATE, SSM_GROUP).reshape(SSM_GROUPS, -1)])
    dlb = dl[0].reshape(2, SSM_GROUPS, SSM_STATE)
    sel = jnp.repeat(jnp.eye(SSM_STATE, dtype=BF16), SSM_GROUP, axis=0)
    da, dldt, db = _s5_disc_bwd(*small['disc_in'], dlb, dbb, sel)
    gs['ssm_A_re'], gs['ssm_A_im'] = da[0], da[1]
    gs['ssm_log_dt'] = dldt[:, 0]
    gs['ssm_B_re'] = db[0].reshape(SSM_GROUPS, SSM_STATE, SSM_GROUP)
    gs['ssm_B_im'] = db[1].reshape(SSM_GROUPS, SSM_STATE, SSM_GROUP)
    gs['ssm_C_re'] = _tp(_diag_blocks(d_cc[:N_STATE], SSM_STATE, SSM_GROUP))
    gs['ssm_C_im'] = -_tp(_diag_blocks(d_cc[N_STATE:], SSM_STATE, SSM_GROUP))
    gs['conv_w'] = dcw[0:3]
    gs['conv_b'] = dcw[3]
    gs['ssm_D'] = dcw[4].reshape(SSM_GROUPS, SSM_GROUP)
    gs['conv_out_norm'], gs['ssm_out_norm'], gs['glu_b'] = part[0], part[1], part[2]
    for n in ('ple_norm', 'ffn2_norm', 'mix_norm', 'ffn1_norm'):
        gs[n] = gs[n][0]
    fulls = [d_ff1, d_ff2, d_wint, d_wout, d_plg,
             d_plpt.reshape(1, D_MODEL * PLE_DIM // D_MODEL, D_MODEL), d_glu.reshape(1, SSM_W * SSM_W // D_MODEL, D_MODEL)]
    return dh, fulls, gs


def _pad_rows(flat, mult):
    per = mult * LANES
    n = flat.shape[0]
    tot = -(-n // per) * per
    return jnp.pad(flat, (0, tot - n)).reshape(tot // LANES, LANES)


def _adamw_any(w, g, m, v):
    shp = w.shape
    two = (lambda t: t.reshape(-1, shp[-1]))
    d, nm, nv = _adamw(two(w), two(g), two(m), two(v))
    return d.reshape(shp), nm.reshape(shp), nv.reshape(shp)


def kernel(x, p, ffn1_norm, ffn1_w_gate, ffn1_w_up, ffn1_w_down, mix_norm, w_in, conv_w, conv_b, ssm_A_re, ssm_A_im, ssm_B_re, ssm_B_im, ssm_C_re, ssm_C_im, ssm_D, ssm_log_dt, glu_w, glu_b, conv_out_norm, ssm_out_norm, w_out, ffn2_norm, ffn2_w_gate, ffn2_w_up, ffn2_w_down, ple_norm, ple_w_gate, ple_w_proj, final_norm, loss_target, m_ffn1_norm, m_ffn1_w_gate, m_ffn1_w_up, m_ffn1_w_down, m_mix_norm, m_w_in, m_conv_w, m_conv_b, m_ssm_A_re, m_ssm_A_im, m_ssm_B_re, m_ssm_B_im, m_ssm_C_re, m_ssm_C_im, m_ssm_D, m_ssm_log_dt, m_glu_w, m_glu_b, m_conv_out_norm, m_ssm_out_norm, m_w_out, m_ffn2_norm, m_ffn2_w_gate, m_ffn2_w_up, m_ffn2_w_down, m_ple_norm, m_ple_w_gate, m_ple_w_proj, m_final_norm, v_ffn1_norm, v_ffn1_w_gate, v_ffn1_w_up, v_ffn1_w_down, v_mix_norm, v_w_in, v_conv_w, v_conv_b, v_ssm_A_re, v_ssm_A_im, v_ssm_B_re, v_ssm_B_im, v_ssm_C_re, v_ssm_C_im, v_ssm_D, v_ssm_log_dt, v_glu_w, v_glu_b, v_conv_out_norm, v_ssm_out_norm, v_w_out, v_ffn2_norm, v_ffn2_w_gate, v_ffn2_w_up, v_ffn2_w_down, v_ple_norm, v_ple_w_gate, v_ple_w_proj, v_final_norm):
    given = dict(locals())
    W = {n: given[n] for n in W_NAMES}
    M = {n: given['m_' + n] for n in W_NAMES}
    V = {n: given['v_' + n] for n in W_NAMES}
    my_dev = _dev_index(_mesh_pos())
    my_chip = (my_dev // 2).astype(jnp.int32).reshape(1)

    conv_shard = _pad_rows(W['conv_w'].reshape(-1), SUBLANES)
    conv_all = _allgather(conv_shard, ((1, SUBLANES),), "ag_conv_w")[0]
    conv_full = conv_all.reshape(N_DEV, -1)[:, :DEPTH * 3 * (CONV_W // N_DEV)]
    conv_full = conv_full.reshape(N_DEV, DEPTH, 3, CONV_W // N_DEV).transpose(1, 2, 0, 3).reshape(DEPTH, 3, CONV_W)

    packs = [jnp.concatenate([
        _tp(W['ffn1_w_gate'][l]), _tp(W['ffn1_w_up'][l]), W['ffn1_w_down'][l],
        _tp(W['ffn2_w_gate'][l]), _tp(W['ffn2_w_up'][l]), W['ffn2_w_down'][l],
        _tp(W['w_in'][l]), W['w_out'][l], W['ple_w_gate'][l],
        _tp(W['ple_w_proj'][l]).reshape(-1, D_MODEL), W['glu_w'][l].reshape(-1, D_MODEL)], axis=0).astype(BF16)
        for l in range(DEPTH)]

    smalls, saves, bigs = [], [], []
    h = x[0]
    flight = _ag_start(packs[0], SEGS, conv_full, "ag_start_0")
    for l in range(DEPTH):
        send_sems, recv_sems, pack_thru, lands, _ = flight
        pack_thru, lands = _ag_wait(send_sems, recv_sems, pack_thru, lands, h, "ag_wait_%d" % l)
        token = jnp.zeros((1, 1), F32)
        if l + 1 < DEPTH:
            flight = _ag_start(packs[l + 1], SEGS, lands[0], "ag_start_%d" % (l + 1))
            token = flight[4][0:1, 0:1]
        ff1, ff2, wint, wout, plg, plpt, glu = _ag_finish(pack_thru, lands, SEGS)
        bigs.append(dict(ff1=ff1, ff2=ff2, wint=wint[0], wout=wout[0], plg=plg[0],
                         plpt=plpt.reshape(D_MODEL, PLE_DIM), glu=glu.reshape(SSM_W, SSM_W)))
        small = {n: W[n][l][None] for n in ('ffn1_norm', 'mix_norm', 'conv_b', 'glu_b', 'conv_out_norm',
                                            'ssm_out_norm', 'ffn2_norm', 'ple_norm')}
        small['ffn1_norm'] = small['ffn1_norm'] + token
        small['conv_w'] = conv_full[l]
        small['dvec'] = W['ssm_D'][l].reshape(1, SSM_W)
        (small['disc_in'], small['ltab'], small['ltab_rev'], small['bbmat'], small['ccmat']) = _s5_prepare(
            W['ssm_A_re'][l], W['ssm_A_im'][l], W['ssm_log_dt'][l], W['ssm_B_re'][l], W['ssm_B_im'][l],
            W['ssm_C_re'][l], W['ssm_C_im'][l])
        h, saved = _layer_fwd(h, p[l, 0], small, bigs[l])
        smalls.append(small)
        saves.append(saved)
    loss_tile, dh, d_final = _final_loss(h, W['final_norm'][None], loss_target[0])
    loss = lax.psum(loss_tile[0, 0], ("x", "y", "c"))

    layer_gs = [None] * DEPTH
    shard_grads = [None] * DEPTH
    flight = None
    for l in reversed(range(DEPTH)):
        small = dict(smalls[l])
        if flight is not None:
            small['ple_norm'] = small['ple_norm'] + flight[1][4][0:1, 0:1]
        dh, fulls, layer_gs[l] = _layer_bwd(dh, p[l, 0], small, bigs[l], saves[l])
        p32, pbf = _pair_sum(fulls, _rs_sibling(fulls, SEGS), SEGS)
        if flight is not None:
            up, (send_sems, recv_sems, pbf_thru, land, _), p32_up = flight
            got3 = _rs_chips_wait(send_sems, recv_sems, pbf_thru, land, dh, "rs_wait_%d" % up)
            shard_grads[up] = _chip_sum(my_chip, p32_up, got3)
        flight = (l, _rs_chips_start(pbf, "rs_start_%d" % l), p32)
    up, (send_sems, recv_sems, pbf_thru, land, _), p32_up = flight
    shard_grads[up] = _chip_sum(my_chip, p32_up,
                                _rs_chips_wait(send_sems, recv_sems, pbf_thru, land, dh, "rs_wait_%d" % up))
    grad_x = dh[None]

    gs = {n: jnp.stack([layer_gs[l][n] for l in range(DEPTH)]) for n in layer_gs[0]}
    gs['final_norm'] = d_final[0]
    flat = jnp.concatenate([gs[n].reshape(-1) for n in SMALL_NAMES] + [gs['conv_w'].reshape(-1)])
    n_flat = flat.shape[0]
    flat = _pad_rows(flat, SUBLANES)
    rows = flat.shape[0]
    gathered = _allgather(flat, ((1, rows),), "ag_small_grads")[0]
    red = _sum8(gathered.reshape(N_DEV, rows, LANES)).reshape(-1)[:n_flat]
    G = {}
    o = 0
    for n in SMALL_NAMES:
        G[n] = red[o:o + W[n].size].reshape(W[n].shape)
        o += W[n].size
    conv_g_full = red[o:].reshape(DEPTH, 3, CONV_W)
    G['conv_w'] = lax.dynamic_slice_in_dim(conv_g_full, my_dev * (CONV_W // N_DEV), CONV_W // N_DEV, axis=2)

    sg = jnp.stack(shard_grads)
    offs = _seg_offsets(SEGS)
    r = SEGS[0][1]
    for a, f in ((0, 'ffn1'), (1, 'ffn2')):
        G[f + '_w_gate'] = _tp(sg[:, offs[a]:offs[a] + r])
        G[f + '_w_up'] = _tp(sg[:, offs[a] + r:offs[a] + 2 * r])
        G[f + '_w_down'] = sg[:, offs[a] + 2 * r:offs[a] + 3 * r]
    G['w_in'] = _tp(sg[:, offs[2]:offs[2] + SEGS[2][1]])
    G['w_out'] = sg[:, offs[3]:offs[3] + SEGS[3][1]]
    G['ple_w_gate'] = sg[:, offs[4]:offs[4] + SEGS[4][1]]
    G['ple_w_proj'] = _tp(sg[:, offs[5]:offs[5] + SEGS[5][1]].reshape(DEPTH, D_MODEL // N_DEV, PLE_DIM))
    G['glu_w'] = sg[:, offs[6]:offs[6] + SEGS[6][1]].reshape(DEPTH, SSM_W // N_DEV, SSM_W)

    delta, new_m, new_v = {}, {}, {}
    cat = lambda src: _pad_rows(jnp.concatenate([src[n].reshape(-1) for n in SMALL_NAMES]), SUBLANES)
    d_s, m_s, v_s = _adamw(cat(W), cat(G), cat(M), cat(V))
    o = 0
    for n in SMALL_NAMES:
        for dst, src in ((delta, d_s), (new_m, m_s), (new_v, v_s)):
            dst[n] = src.reshape(-1)[o:o + W[n].size].reshape(W[n].shape)
        o += W[n].size
    for n in W_NAMES:
        if n not in delta:
            delta[n], new_m[n], new_v[n] = _adamw_any(W[n], G[n], M[n], V[n])

    return (loss, grad_x, *[G[n] for n in W_NAMES], *[delta[n] for n in W_NAMES],
            *[new_m[n] for n in W_NAMES], *[new_v[n] for n in W_NAMES])
```

```python
import math

import jax
import jax.numpy as jnp
from jax import lax
from jax.experimental import pallas as pl
from jax.experimental.pallas import tpu as pltpu

F32 = jnp.float32
BF16 = jnp.bfloat16

N_DEV = 8
DEPTH = 4
SEQ = 2048
D_MODEL = 1024
D_FF = 2816
CONV_W = 512
SSM_W = 512
SSM_GROUPS = 32
SSM_GROUP = 16
SSM_STATE = 64
N_STATE = SSM_GROUPS * SSM_STATE
IN_COLS = 2048
PLE_DIM = 256
EPS = 1e-6

ADAM_LR = 0.001
ADAM_B1 = 0.9
ADAM_B2 = 0.999
ADAM_EPS = 1e-08
ADAM_WD = 0.01
ADAM_STEP = 10

FF_BLOCK = 256
N_FF_BLOCKS = D_FF // FF_BLOCK
TOK_TILE_FFN = 1024
TOK_TILE = 512
CHUNK = 256
N_CHUNKS = SEQ // CHUNK
LANE_GROUP = 512
SUBLANES = 8
LANES = 128
MIB = 1024 * 1024

W_NAMES = ['ffn1_norm', 'ffn1_w_gate', 'ffn1_w_up', 'ffn1_w_down', 'mix_norm', 'w_in', 'conv_w', 'conv_b',
           'ssm_A_re', 'ssm_A_im', 'ssm_B_re', 'ssm_B_im', 'ssm_C_re', 'ssm_C_im', 'ssm_D', 'ssm_log_dt',
           'glu_w', 'glu_b', 'conv_out_norm', 'ssm_out_norm', 'w_out', 'ffn2_norm', 'ffn2_w_gate', 'ffn2_w_up',
           'ffn2_w_down', 'ple_norm', 'ple_w_gate', 'ple_w_proj', 'final_norm']
SMALL_NAMES = ['ffn1_norm', 'mix_norm', 'conv_b', 'ssm_A_re', 'ssm_A_im', 'ssm_B_re', 'ssm_B_im', 'ssm_C_re',
               'ssm_C_im', 'ssm_D', 'ssm_log_dt', 'glu_b', 'conv_out_norm', 'ssm_out_norm', 'ffn2_norm',
               'ple_norm', 'final_norm']

SEGS = ((3, 352), (3, 352), (1, 256), (1, 128), (1, 128), (1, 32), (1, 32))
PACK_ROWS = sum(n * r for n, r in SEGS)

MESH = pl.DeviceIdType.MESH
ANY = pl.BlockSpec(memory_space=pl.ANY)


def _cparams(sem=None, vmem_mib=48, **kw):
    return pltpu.CompilerParams(dimension_semantics=sem, vmem_limit_bytes=vmem_mib * MIB, **kw)


def _dot(a, b):
    return jnp.dot(a, b, preferred_element_type=F32)


def _dot_nt(a, b):
    return lax.dot_general(a, b, (((1,), (1,)), ((), ())), preferred_element_type=F32)


def _dot_tn(a, b):
    return lax.dot_general(a, b, (((0,), (0,)), ((), ())), preferred_element_type=F32)


def _rms_stats(x):
    r = lax.rsqrt(jnp.mean(x * x, axis=-1, keepdims=True) + EPS)
    return x * r, r


def _rms_bwd(dy, xh, r, g):
    dxh = dy * g
    dx = r * (dxh - xh * jnp.mean(dxh * xh, axis=-1, keepdims=True))
    dg = jnp.sum(dy * xh, axis=0, keepdims=True)
    return dx, dg


def _sigmoid(x):
    return 0.5 * jnp.tanh(0.5 * x) + 0.5


_GELU_C = math.sqrt(2.0 / math.pi)


def _gelu(x):
    t = jnp.tanh(_GELU_C * (x + 0.044715 * x * x * x))
    return 0.5 * x * (1.0 + t), t


def _gelu_grad(x, t):
    return 0.5 * (1.0 + t) + 0.5 * x * (1.0 - t * t) * _GELU_C * (1.0 + 3.0 * 0.044715 * x * x)


def _accumulate(ref, first, value):
    @pl.when(first)
    def _():
        ref[...] = value

    @pl.when(jnp.logical_not(first))
    def _():
        ref[...] += value


def _ffn_fwd(h, g, w3):
    tm = TOK_TILE_FFN

    def body(h_ref, g_ref, w_ref, out_ref, gu_ref, u_ref):
        k = pl.program_id(1)

        @pl.when(k == 0)
        def _():
            x = h_ref[...]
            xh, _ = _rms_stats(x)
            u_ref[...] = (xh * g_ref[...]).astype(BF16)
            out_ref[...] = x

        u = u_ref[...]
        gate = _dot_nt(u, w_ref[0])
        up = _dot_nt(u, w_ref[1])
        a = gate * _sigmoid(gate) * up
        gu_ref[0] = gate.astype(BF16)
        gu_ref[1] = up.astype(BF16)
        out_ref[...] += 0.5 * _dot(a.astype(BF16), w_ref[2])

    return pl.pallas_call(
        body, name="ffn_fwd",
        grid=(SEQ // tm, N_FF_BLOCKS),
        in_specs=[pl.BlockSpec((tm, D_MODEL), lambda m, k: (m, 0)),
                  pl.BlockSpec((1, D_MODEL), lambda m, k: (0, 0)),
                  pl.BlockSpec((3, FF_BLOCK, D_MODEL), lambda m, k: (0, k, 0))],
        out_specs=[pl.BlockSpec((tm, D_MODEL), lambda m, k: (m, 0)),
                   pl.BlockSpec((2, tm, FF_BLOCK), lambda m, k: (0, m, k))],
        out_shape=[jax.ShapeDtypeStruct((SEQ, D_MODEL), F32),
                   jax.ShapeDtypeStruct((2, SEQ, D_FF), BF16)],
        scratch_shapes=[pltpu.VMEM((tm, D_MODEL), BF16)],
        compiler_params=_cparams(("parallel", "arbitrary")),
    )(h, g, w3)


def _ffn_bwd_act(h, g, dout, gu, w3):
    tm = TOK_TILE

    def body(h_ref, g_ref, d_ref, gu_ref, w_ref, dh_ref, dga_ref, ud_ref, dg_ref, acc_ref):
        m = pl.program_id(0)
        k = pl.program_id(1)

        @pl.when(k == 0)
        def _():
            xh, _ = _rms_stats(h_ref[...])
            ud_ref[0] = (xh * g_ref[...]).astype(BF16)
            ud_ref[1] = (0.5 * d_ref[...]).astype(BF16)
            acc_ref[...] = jnp.zeros_like(acc_ref)

        gate = gu_ref[0].astype(F32)
        up = gu_ref[1].astype(F32)
        sg = _sigmoid(gate)
        silu = gate * sg
        da = _dot_nt(ud_ref[1], w_ref[2])
        dgate = (da * up * (sg + silu * (1.0 - sg))).astype(BF16)
        dup = (da * silu).astype(BF16)
        dga_ref[0] = dgate
        dga_ref[1] = dup
        dga_ref[2] = (silu * up).astype(BF16)
        acc_ref[...] += _dot(jnp.concatenate([dgate, dup], axis=1), w_ref[0:2].reshape(2 * FF_BLOCK, D_MODEL))

        @pl.when(k == N_FF_BLOCKS - 1)
        def _():
            xh, r = _rms_stats(h_ref[...])
            dx, dg = _rms_bwd(acc_ref[...], xh, r, g_ref[...])
            dh_ref[...] = d_ref[...] + dx
            _accumulate(dg_ref, m == 0, dg)

    return pl.pallas_call(
        body, name="ffn_bwd_act",
        grid=(SEQ // tm, N_FF_BLOCKS),
        in_specs=[pl.BlockSpec((tm, D_MODEL), lambda m, k: (m, 0)),
                  pl.BlockSpec((1, D_MODEL), lambda m, k: (0, 0)),
                  pl.BlockSpec((tm, D_MODEL), lambda m, k: (m, 0)),
                  pl.BlockSpec((2, tm, FF_BLOCK), lambda m, k: (0, m, k)),
                  pl.BlockSpec((3, FF_BLOCK, D_MODEL), lambda m, k: (0, k, 0))],
        out_specs=[pl.BlockSpec((tm, D_MODEL), lambda m, k: (m, 0)),
                   pl.BlockSpec((3, tm, FF_BLOCK), lambda m, k: (0, m, k)),
                   pl.BlockSpec((2, tm, D_MODEL), lambda m, k: (0, m, 0)),
                   pl.BlockSpec((1, D_MODEL), lambda m, k: (0, 0))],
        out_shape=[jax.ShapeDtypeStruct((SEQ, D_MODEL), F32),
                   jax.ShapeDtypeStruct((3, SEQ, D_FF), BF16),
                   jax.ShapeDtypeStruct((2, SEQ, D_MODEL), BF16),
                   jax.ShapeDtypeStruct((1, D_MODEL), F32)],
        scratch_shapes=[pltpu.VMEM((tm, D_MODEL), F32)],
        compiler_params=_cparams(("arbitrary", "arbitrary")),
    )(h, g, dout, gu, w3)


def _matmul_tn(a, b, bm, out_dtype, name, bn=None):
    na, t, m = a.shape
    nb, _, n = b.shape
    bn = n if bn is None else bn

    def body(a_ref, b_ref, o_ref):
        o_ref[0] = _dot_tn(a_ref[0], b_ref[0]).astype(out_dtype)

    return pl.pallas_call(
        body, name=name,
        grid=(na, m // bm, n // bn),
        in_specs=[pl.BlockSpec((1, t, bm), lambda i, k, j: (i, 0, k)),
                  pl.BlockSpec((1, t, bn), lambda i, k, j: (jnp.maximum(i - (na - nb), 0), 0, j))],
        out_specs=pl.BlockSpec((1, bm, bn), lambda i, k, j: (i, k, j)),
        out_shape=jax.ShapeDtypeStruct((na, m, n), out_dtype),
        compiler_params=_cparams(("arbitrary", "parallel", "parallel")),
    )(a, b)


def _inproj_fwd(h, g, wint):
    tm = TOK_TILE

    def body(h_ref, g_ref, w_ref, z_ref):
        xh, _ = _rms_stats(h_ref[...])
        z_ref[...] = _dot_nt((xh * g_ref[...]).astype(BF16), w_ref[...])

    return pl.pallas_call(
        body, name="inproj_fwd",
        grid=(SEQ // tm,),
        in_specs=[pl.BlockSpec((tm, D_MODEL), lambda m: (m, 0)),
                  pl.BlockSpec((1, D_MODEL), lambda m: (0, 0)),
                  pl.BlockSpec((IN_COLS, D_MODEL), lambda m: (0, 0))],
        out_specs=pl.BlockSpec((tm, IN_COLS), lambda m: (m, 0)),
        out_shape=jax.ShapeDtypeStruct((SEQ, IN_COLS), F32),
        compiler_params=_cparams(("parallel",)),
    )(h, g, wint)


def _inproj_bwd(h, g, dh, dz, wint):
    tm = TOK_TILE

    def body(h_ref, g_ref, dh_ref, dz_ref, w_ref, o_ref, u_ref, dg_ref):
        xh, r = _rms_stats(h_ref[...])
        u_ref[0] = (xh * g_ref[...]).astype(BF16)
        dx, dg = _rms_bwd(_dot(dz_ref[...], w_ref[...]), xh, r, g_ref[...])
        o_ref[...] = dh_ref[...] + dx
        _accumulate(dg_ref, pl.program_id(0) == 0, dg)

    return pl.pallas_call(
        body, name="inproj_bwd",
        grid=(SEQ // tm,),
        in_specs=[pl.BlockSpec((tm, D_MODEL), lambda m: (m, 0)),
                  pl.BlockSpec((1, D_MODEL), lambda m: (0, 0)),
                  pl.BlockSpec((tm, D_MODEL), lambda m: (m, 0)),
                  pl.BlockSpec((tm, IN_COLS), lambda m: (m, 0)),
                  pl.BlockSpec((IN_COLS, D_MODEL), lambda m: (0, 0))],
        out_specs=[pl.BlockSpec((tm, D_MODEL), lambda m: (m, 0)),
                   pl.BlockSpec((1, tm, D_MODEL), lambda m: (0, m, 0)),
                   pl.BlockSpec((1, D_MODEL), lambda m: (0, 0))],
        out_shape=[jax.ShapeDtypeStruct((SEQ, D_MODEL), F32),
                   jax.ShapeDtypeStruct((1, SEQ, D_MODEL), BF16),
                   jax.ShapeDtypeStruct((1, D_MODEL), F32)],
        compiler_params=_cparams(("arbitrary",)),
    )(h, g, dh, dz, wint)


def _row_ids(n, w):
    return lax.broadcasted_iota(jnp.int32, (n, w), 0)


def _bcast_row(x, i, n):
    return jnp.broadcast_to(x[i:i + 1, :], (n, x.shape[1]))


def _conv_taps(v, tail):
    n, w = v.shape
    rid = _row_ids(n, w)
    v1 = jnp.where(rid == 0, _bcast_row(tail, 7, n), pltpu.roll(v, 1, 0))
    v2 = jnp.where(rid == 0, _bcast_row(tail, 6, n),
                   jnp.where(rid == 1, _bcast_row(tail, 7, n), pltpu.roll(v, 2, 0)))
    return v1, v2


def _scan_chunk(work, ltab, carry, reverse):
    nblk = CHUNK // SUBLANES
    row = _row_ids(SUBLANES, LANE_GROUP)
    for gi in range(N_STATE // LANE_GROUP):
        cre = pl.ds(gi * LANE_GROUP, LANE_GROUP)
        cim = pl.ds(N_STATE + gi * LANE_GROUP, LANE_GROUP)
        pows = [(ltab[8 * k:8 * k + 8, cre], ltab[8 * k:8 * k + 8, cim]) for k in range(3)]
        pr = ltab[24:32, cre]
        pi = ltab[24:32, cim]

        def blk(i, c, cre=cre, cim=cim, pows=pows, pr=pr, pi=pi):
            cr, ci = c
            b = (nblk - 1 - i) if reverse else i
            r0 = pl.multiple_of(b * SUBLANES, SUBLANES)
            xr = work[pl.ds(r0, SUBLANES), cre]
            xi = work[pl.ds(r0, SUBLANES), cim]
            for k, s in enumerate((1, 2, 4)):
                lr, li = pows[k]
                if reverse:
                    keep = row < SUBLANES - s
                    sr = jnp.where(keep, pltpu.roll(xr, SUBLANES - s, 0), 0.0)
                    si = jnp.where(keep, pltpu.roll(xi, SUBLANES - s, 0), 0.0)
                else:
                    keep = row >= s
                    sr = jnp.where(keep, pltpu.roll(xr, s, 0), 0.0)
                    si = jnp.where(keep, pltpu.roll(xi, s, 0), 0.0)
                xr, xi = xr + lr * sr - li * si, xi + lr * si + li * sr
            xr, xi = xr + pr * cr - pi * ci, xi + pr * ci + pi * cr
            work[pl.ds(r0, SUBLANES), cre] = xr
            work[pl.ds(r0, SUBLANES), cim] = xi
            edge = 0 if reverse else SUBLANES - 1
            return _bcast_row(xr, edge, SUBLANES), _bcast_row(xi, edge, SUBLANES)

        cr, ci = lax.fori_loop(0, nblk, blk, (carry[:, cre], carry[:, cim]))
        carry[:, cre] = cr
        carry[:, cim] = ci


def _s5conv_fwd(z, convw, convb, bbmat, ccmat, dvec, ltab):
    def body(z_ref, cw_ref, cb_ref, bb_ref, cc_ref, d_ref, lt_ref, ya_ref, ys_ref, hs_ref,
             work, carry, tail):
        c = pl.program_id(0)

        @pl.when(c == 0)
        def _():
            carry[...] = jnp.zeros_like(carry)
            tail[...] = jnp.zeros_like(tail)

        zb = z_ref[:, 0:CONV_W]
        v = z_ref[:, CONV_W:2 * CONV_W] * z_ref[:, 2 * CONV_W:3 * CONV_W]
        us = z_ref[:, 3 * CONV_W:4 * CONV_W]
        v1, v2 = _conv_taps(v, tail[...])
        tail[...] = v[CHUNK - 8:CHUNK, :]
        y = cw_ref[0:1, :] * v2 + cw_ref[1:2, :] * v1 + cw_ref[2:3, :] * v
        ya_ref[...] = zb * (y + cb_ref[...])

        work[...] = _dot(us.astype(BF16), bb_ref[...])
        _scan_chunk(work, lt_ref, carry, reverse=False)
        hs = work[...].astype(BF16)
        hs_ref[...] = hs
        ys_ref[...] = _dot_nt(hs, cc_ref[...]) + d_ref[...] * us

    return pl.pallas_call(
        body, name="s5conv_fwd",
        grid=(N_CHUNKS,),
        in_specs=[pl.BlockSpec((CHUNK, IN_COLS), lambda c: (c, 0)),
                  pl.BlockSpec((3, CONV_W), lambda c: (0, 0)),
                  pl.BlockSpec((1, CONV_W), lambda c: (0, 0)),
                  pl.BlockSpec((SSM_W, 2 * N_STATE), lambda c: (0, 0)),
                  pl.BlockSpec((SSM_W, 2 * N_STATE), lambda c: (0, 0)),
                  pl.BlockSpec((1, SSM_W), lambda c: (0, 0)),
                  pl.BlockSpec((32, 2 * N_STATE), lambda c: (0, 0))],
        out_specs=[pl.BlockSpec((CHUNK, CONV_W), lambda c: (c, 0)),
                   pl.BlockSpec((CHUNK, SSM_W), lambda c: (c, 0)),
                   pl.BlockSpec((CHUNK, 2 * N_STATE), lambda c: (c, 0))],
        out_shape=[jax.ShapeDtypeStruct((SEQ, CONV_W), F32),
                   jax.ShapeDtypeStruct((SEQ, SSM_W), F32),
                   jax.ShapeDtypeStruct((SEQ, 2 * N_STATE), BF16)],
        scratch_shapes=[pltpu.VMEM((CHUNK, 2 * N_STATE), F32),
                        pltpu.VMEM((8, 2 * N_STATE), F32),
                        pltpu.VMEM((8, CONV_W), F32)],
        compiler_params=_cparams(("arbitrary",)),
    )(z, convw, convb, bbmat, ccmat, dvec, ltab)


def _s5conv_bwd(z, hs, dya, dys, convw, convb, bbmat, ccmat, dvec, ltab_rev):
    nc = N_CHUNKS
    hb = 16

    def body(z_ref, zp_ref, hs_ref, hp_ref, dya_ref, dys_ref, cw_ref, cb_ref, bb_ref, cc_ref, d_ref, lt_ref,
             dz_ref, g_ref, us_ref, dyb_ref, dl_ref, dcw_ref, work, carry, head):
        i = pl.program_id(0)
        first_chunk = i == nc - 1

        @pl.when(i == 0)
        def _():
            carry[...] = jnp.zeros_like(carry)
            head[...] = jnp.zeros_like(head)
            dl_ref[...] = jnp.zeros_like(dl_ref)
            dcw_ref[...] = jnp.zeros_like(dcw_ref)

        us = z_ref[:, 3 * CONV_W:4 * CONV_W]
        dy = dys_ref[...]
        dy_bf = dy.astype(BF16)
        us_ref[0] = us.astype(BF16)
        dyb_ref[0] = dy_bf

        work[...] = _dot(dy_bf, cc_ref[...])
        _scan_chunk(work, lt_ref, carry, reverse=True)
        gg = work[...]
        gg_bf = gg.astype(BF16)
        g_ref[0] = gg_bf
        dus = d_ref[...] * dy + _dot_nt(gg_bf, bb_ref[...])

        hcur = hs_ref[...].astype(F32)
        hlast = hp_ref[...].astype(F32)[hb - 1:hb, :]
        hlast = jnp.where(first_chunk, 0.0, hlast)
        rid = _row_ids(CHUNK, 2 * N_STATE)
        hprev = jnp.where(rid == 0, jnp.broadcast_to(hlast, (CHUNK, 2 * N_STATE)), pltpu.roll(hcur, 1, 0))
        gr, gi = gg[:, :N_STATE], gg[:, N_STATE:]
        hr, hi = hprev[:, :N_STATE], hprev[:, N_STATE:]
        dl_ref[:, :N_STATE] += (gr * hr + gi * hi).reshape(CHUNK // 8, 8, N_STATE).sum(axis=0)
        dl_ref[:, N_STATE:] += (gi * hr - gr * hi).reshape(CHUNK // 8, 8, N_STATE).sum(axis=0)

        @pl.when(i == nc - 1)
        def _():
            dl_ref[0:1, :] = jnp.sum(dl_ref[...], axis=0, keepdims=True)

        zb = z_ref[:, 0:CONV_W]
        zc = z_ref[:, CONV_W:2 * CONV_W]
        zv = z_ref[:, 2 * CONV_W:3 * CONV_W]
        v = zc * zv
        vtail = jnp.where(first_chunk, 0.0, zp_ref[:, CONV_W:2 * CONV_W] * zp_ref[:, 2 * CONV_W:3 * CONV_W])
        v1, v2 = _conv_taps(v, vtail)
        w0, w1, w2 = cw_ref[0:1, :], cw_ref[1:2, :], cw_ref[2:3, :]
        y = w0 * v2 + w1 * v1 + w2 * v
        dya_v = dya_ref[...]
        dzb = dya_v * (y + cb_ref[...])
        dyc = dya_v * zb
        hd = head[...]
        rc = _row_ids(CHUNK, CONV_W)
        n1 = jnp.where(rc == CHUNK - 1, _bcast_row(hd, 0, CHUNK), pltpu.roll(dyc, CHUNK - 1, 0))
        n2 = jnp.where(rc == CHUNK - 1, _bcast_row(hd, 1, CHUNK),
                       jnp.where(rc == CHUNK - 2, _bcast_row(hd, 0, CHUNK), pltpu.roll(dyc, CHUNK - 2, 0)))
        head[...] = dyc[0:8, :]
        dv = w2 * dyc + w1 * n1 + w0 * n2
        dz_ref[:, 0:CONV_W] = dzb.astype(BF16)
        dz_ref[:, CONV_W:2 * CONV_W] = (dv * zv).astype(BF16)
        dz_ref[:, 2 * CONV_W:3 * CONV_W] = (dv * zc).astype(BF16)
        dz_ref[:, 3 * CONV_W:4 * CONV_W] = dus.astype(BF16)
        dcw_ref[0:1, :] += jnp.sum(dyc * v2, axis=0, keepdims=True)
        dcw_ref[1:2, :] += jnp.sum(dyc * v1, axis=0, keepdims=True)
        dcw_ref[2:3, :] += jnp.sum(dyc * v, axis=0, keepdims=True)
        dcw_ref[3:4, :] += jnp.sum(dyc, axis=0, keepdims=True)
        dcw_ref[4:5, :] += jnp.sum(dy * us, axis=0, keepdims=True)

    rev = lambda i: nc - 1 - i
    return pl.pallas_call(
        body, name="s5conv_bwd",
        grid=(nc,),
        in_specs=[pl.BlockSpec((CHUNK, IN_COLS), lambda i: (rev(i), 0)),
                  pl.BlockSpec((8, IN_COLS), lambda i: (jnp.maximum(rev(i) * (CHUNK // 8) - 1, 0), 0)),
                  pl.BlockSpec((CHUNK, 2 * N_STATE), lambda i: (rev(i), 0)),
                  pl.BlockSpec((hb, 2 * N_STATE), lambda i: (jnp.maximum(rev(i) * (CHUNK // hb) - 1, 0), 0)),
                  pl.BlockSpec((CHUNK, CONV_W), lambda i: (rev(i), 0)),
                  pl.BlockSpec((CHUNK, SSM_W), lambda i: (rev(i), 0)),
                  pl.BlockSpec((3, CONV_W), lambda i: (0, 0)),
                  pl.BlockSpec((1, CONV_W), lambda i: (0, 0)),
                  pl.BlockSpec((SSM_W, 2 * N_STATE), lambda i: (0, 0)),
                  pl.BlockSpec((SSM_W, 2 * N_STATE), lambda i: (0, 0)),
                  pl.BlockSpec((1, SSM_W), lambda i: (0, 0)),
                  pl.BlockSpec((32, 2 * N_STATE), lambda i: (0, 0))],
        out_specs=[pl.BlockSpec((CHUNK, IN_COLS), lambda i: (rev(i), 0)),
                   pl.BlockSpec((1, CHUNK, 2 * N_STATE), lambda i: (0, rev(i), 0)),
                   pl.BlockSpec((1, CHUNK, SSM_W), lambda i: (0, rev(i), 0)),
                   pl.BlockSpec((1, CHUNK, SSM_W), lambda i: (0, rev(i), 0)),
                   pl.BlockSpec((8, 2 * N_STATE), lambda i: (0, 0)),
                   pl.BlockSpec((8, CONV_W), lambda i: (0, 0))],
        out_shape=[jax.ShapeDtypeStruct((SEQ, IN_COLS), BF16),
                   jax.ShapeDtypeStruct((1, SEQ, 2 * N_STATE), BF16),
                   jax.ShapeDtypeStruct((1, SEQ, SSM_W), BF16),
                   jax.ShapeDtypeStruct((1, SEQ, SSM_W), BF16),
                   jax.ShapeDtypeStruct((8, 2 * N_STATE), F32),
                   jax.ShapeDtypeStruct((8, CONV_W), F32)],
        scratch_shapes=[pltpu.VMEM((CHUNK, 2 * N_STATE), F32),
                        pltpu.VMEM((8, 2 * N_STATE), F32),
                        pltpu.VMEM((8, CONV_W), F32)],
        compiler_params=_cparams(("arbitrary",)),
    )(z, z, hs, hs, dya, dys, convw, convb, bbmat, ccmat, dvec, ltab_rev)


def _mix_out_fwd(h, ya, ys, gluw, glub, con, son, wout):
    tm = TOK_TILE

    def body(h_ref, ya_ref, ys_ref, gw_ref, gb_ref, con_ref, son_ref, wo_ref, o_ref):
        zg, _ = _gelu(ys_ref[...])
        q = _dot(zg.astype(BF16), gw_ref[...]) + gb_ref[...]
        out_s = zg * _sigmoid(q)
        na, _ = _rms_stats(ya_ref[...])
        ns, _ = _rms_stats(out_s)
        o_ref[...] = (h_ref[...]
                      + _dot((na * con_ref[...]).astype(BF16), wo_ref[0:CONV_W, :])
                      + _dot((ns * son_ref[...]).astype(BF16), wo_ref[CONV_W:2 * CONV_W, :]))

    row = lambda m: (m, 0)
    fixed = lambda m: (0, 0)
    return pl.pallas_call(
        body, name="mix_out_fwd",
        grid=(SEQ // tm,),
        in_specs=[pl.BlockSpec((tm, D_MODEL), row), pl.BlockSpec((tm, CONV_W), row), pl.BlockSpec((tm, SSM_W), row),
                  pl.BlockSpec((SSM_W, SSM_W), fixed), pl.BlockSpec((1, SSM_W), fixed),
                  pl.BlockSpec((1, CONV_W), fixed), pl.BlockSpec((1, SSM_W), fixed),
                  pl.BlockSpec((D_MODEL, D_MODEL), fixed)],
        out_specs=pl.BlockSpec((tm, D_MODEL), row),
        out_shape=jax.ShapeDtypeStruct((SEQ, D_MODEL), F32),
        compiler_params=_cparams(("parallel",)),
    )(h, ya, ys, gluw, glub, con, son, wout)


def _mix_out_bwd(dh, ya, ys, gluw, glub, con, son, wout):
    tm = TOK_TILE

    def body(dh_ref, ya_ref, ys_ref, gw_ref, gb_ref, con_ref, son_ref, wo_ref,
             dya_ref, dys_ref, yc_ref, dhb_ref, zg_ref, dq_ref, part_ref):
        ysv = ys_ref[...]
        zg, th = _gelu(ysv)
        zg_bf = zg.astype(BF16)
        s = _sigmoid(_dot(zg_bf, gw_ref[...]) + gb_ref[...])
        out_s = zg * s
        na, ra = _rms_stats(ya_ref[...])
        ns, rs = _rms_stats(out_s)
        dh_bf = dh_ref[...].astype(BF16)
        yc_ref[0, :, 0:CONV_W] = (na * con_ref[...]).astype(BF16)
        yc_ref[0, :, CONV_W:2 * CONV_W] = (ns * son_ref[...]).astype(BF16)
        dhb_ref[0] = dh_bf
        dca = _dot_nt(dh_bf, wo_ref[0:CONV_W, :])
        dcs = _dot_nt(dh_bf, wo_ref[CONV_W:2 * CONV_W, :])
        dya, dcon = _rms_bwd(dca, na, ra, con_ref[...])
        dos, dson = _rms_bwd(dcs, ns, rs, son_ref[...])
        dya_ref[...] = dya
        dq = dos * zg * s * (1.0 - s)
        dq_bf = dq.astype(BF16)
        dzg = dos * s + _dot_nt(dq_bf, gw_ref[...])
        dys_ref[...] = dzg * _gelu_grad(ysv, th)
        zg_ref[0] = zg_bf
        dq_ref[0] = dq_bf
        rid = _row_ids(SUBLANES, SSM_W)
        part = jnp.zeros((SUBLANES, SSM_W), F32)
        for i, rowv in enumerate((dcon, dson, jnp.sum(dq, axis=0, keepdims=True))):
            part = jnp.where(rid == i, jnp.broadcast_to(rowv, (SUBLANES, SSM_W)), part)
        _accumulate(part_ref, pl.program_id(0) == 0, part)

    row = lambda m: (m, 0)
    fixed = lambda m: (0, 0)
    lead = lambda m: (0, m, 0)
    return pl.pallas_call(
        body, name="mix_out_bwd",
        grid=(SEQ // tm,),
        in_specs=[pl.BlockSpec((tm, D_MODEL), row), pl.BlockSpec((tm, CONV_W), row), pl.BlockSpec((tm, SSM_W), row),
                  pl.BlockSpec((SSM_W, SSM_W), fixed), pl.BlockSpec((1, SSM_W), fixed),
                  pl.BlockSpec((1, CONV_W), fixed), pl.BlockSpec((1, SSM_W), fixed),
                  pl.BlockSpec((D_MODEL, D_MODEL), fixed)],
        out_specs=[pl.BlockSpec((tm, CONV_W), row), pl.BlockSpec((tm, SSM_W), row),
                   pl.BlockSpec((1, tm, D_MODEL), lead), pl.BlockSpec((1, tm, D_MODEL), lead),
                   pl.BlockSpec((1, tm, SSM_W), lead), pl.BlockSpec((1, tm, SSM_W), lead),
                   pl.BlockSpec((8, SSM_W), fixed)],
        out_shape=[jax.ShapeDtypeStruct((SEQ, CONV_W), F32), jax.ShapeDtypeStruct((SEQ, SSM_W), F32),
                   jax.ShapeDtypeStruct((1, SEQ, D_MODEL), BF16), jax.ShapeDtypeStruct((1, SEQ, D_MODEL), BF16),
                   jax.ShapeDtypeStruct((1, SEQ, SSM_W), BF16), jax.ShapeDtypeStruct((1, SEQ, SSM_W), BF16),
                   jax.ShapeDtypeStruct((8, SSM_W), F32)],
        compiler_params=_cparams(("arbitrary",)),
    )(dh, ya, ys, gluw, glub, con, son, wout)


def _ple_fwd(h, g, p, wgate, wprojt):
    tm = TOK_TILE

    def body(h_ref, g_ref, p_ref, wg_ref, wp_ref, o_ref):
        x = h_ref[...]
        xh, _ = _rms_stats(x)
        s = _sigmoid(_dot((xh * g_ref[...]).astype(BF16), wg_ref[...]))
        o_ref[...] = x + _dot_nt(p_ref[...].astype(BF16), wp_ref[...]) * s

    row = lambda m: (m, 0)
    fixed = lambda m: (0, 0)
    return pl.pallas_call(
        body, name="ple_fwd",
        grid=(SEQ // tm,),
        in_specs=[pl.BlockSpec((tm, D_MODEL), row), pl.BlockSpec((1, D_MODEL), fixed), pl.BlockSpec((tm, PLE_DIM), row),
                  pl.BlockSpec((D_MODEL, D_MODEL), fixed), pl.BlockSpec((D_MODEL, PLE_DIM), fixed)],
        out_specs=pl.BlockSpec((tm, D_MODEL), row),
        out_shape=jax.ShapeDtypeStruct((SEQ, D_MODEL), F32),
        compiler_params=_cparams(("parallel",)),
    )(h, g, p, wgate, wprojt)


def _ple_bwd(h, g, p, dh, wgate, wprojt):
    tm = TOK_TILE

    def body(h_ref, g_ref, p_ref, dh_ref, wg_ref, wp_ref, o_ref, u_ref, dq_ref, dpp_ref, pb_ref, dg_ref):
        xh, r = _rms_stats(h_ref[...])
        u = (xh * g_ref[...]).astype(BF16)
        s = _sigmoid(_dot(u, wg_ref[...]))
        p_bf = p_ref[...].astype(BF16)
        pp = _dot_nt(p_bf, wp_ref[...])
        dhv = dh_ref[...]
        dq = (dhv * pp * s * (1.0 - s)).astype(BF16)
        u_ref[0] = u
        dq_ref[0] = dq
        dpp_ref[0] = (dhv * s).astype(BF16)
        pb_ref[0] = p_bf
        dx, dg = _rms_bwd(_dot_nt(dq, wg_ref[...]), xh, r, g_ref[...])
        o_ref[...] = dhv + dx
        _accumulate(dg_ref, pl.program_id(0) == 0, dg)

    row = lambda m: (m, 0)
    fixed = lambda m: (0, 0)
    lead = lambda m: (0, m, 0)
    big = jax.ShapeDtypeStruct((1, SEQ, D_MODEL), BF16)
    return pl.pallas_call(
        body, name="ple_bwd",
        grid=(SEQ // tm,),
        in_specs=[pl.BlockSpec((tm, D_MODEL), row), pl.BlockSpec((1, D_MODEL), fixed), pl.BlockSpec((tm, PLE_DIM), row),
                  pl.BlockSpec((tm, D_MODEL), row),
                  pl.BlockSpec((D_MODEL, D_MODEL), fixed), pl.BlockSpec((D_MODEL, PLE_DIM), fixed)],
        out_specs=[pl.BlockSpec((tm, D_MODEL), row),
                   pl.BlockSpec((1, tm, D_MODEL), lead), pl.BlockSpec((1, tm, D_MODEL), lead),
                   pl.BlockSpec((1, tm, D_MODEL), lead), pl.BlockSpec((1, tm, PLE_DIM), lead),
                   pl.BlockSpec((1, D_MODEL), fixed)],
        out_shape=[jax.ShapeDtypeStruct((SEQ, D_MODEL), F32), big, big, big,
                   jax.ShapeDtypeStruct((1, SEQ, PLE_DIM), BF16),
                   jax.ShapeDtypeStruct((1, D_MODEL), F32)],
        compiler_params=_cparams(("arbitrary",)),
    )(h, g, p, dh, wgate, wprojt)


def _final_loss(h, g, target):
    tm = TOK_TILE

    def body(h_ref, g_ref, t_ref, loss_ref, dh_ref, dg_ref):
        first = pl.program_id(0) == 0
        xh, r = _rms_stats(h_ref[...])
        diff = xh * g_ref[...] - t_ref[...]
        part = 0.5 * jnp.sum(jnp.mean(diff * diff, axis=-1, keepdims=True), axis=0, keepdims=True)
        _accumulate(loss_ref, first, jnp.broadcast_to(part, (SUBLANES, LANES)))
        dx, dg = _rms_bwd(diff * (1.0 / D_MODEL), xh, r, g_ref[...])
        dh_ref[...] = dx
        _accumulate(dg_ref, first, dg)

    row = lambda m: (m, 0)
    fixed = lambda m: (0, 0)
    return pl.pallas_call(
        body, name="final_loss",
        grid=(SEQ // tm,),
        in_specs=[pl.BlockSpec((tm, D_MODEL), row), pl.BlockSpec((1, D_MODEL), fixed),
                  pl.BlockSpec((tm, D_MODEL), row)],
        out_specs=[pl.BlockSpec((SUBLANES, LANES), fixed),
                   pl.BlockSpec((tm, D_MODEL), row),
                   pl.BlockSpec((1, D_MODEL), fixed)],
        out_shape=[jax.ShapeDtypeStruct((SUBLANES, LANES), F32),
                   jax.ShapeDtypeStruct((SEQ, D_MODEL), F32),
                   jax.ShapeDtypeStruct((1, D_MODEL), F32)],
        compiler_params=_cparams(("arbitrary",)),
    )(h, g, target)


def _disc(ar, ai, ldt):
    dt = jnp.exp(ldt)
    mag = jnp.exp(ar * dt)
    ph = ai * dt
    lr, li = mag * jnp.cos(ph), mag * jnp.sin(ph)
    nr, ni = lr - 1.0, li
    den = ar * ar + ai * ai
    return lr, li, (nr * ar + ni * ai) / den, (ni * ar - nr * ai) / den


def _s5_disc(a_row, ldt_row, a_rep, ldt_rep, bt, ct, tile_e, mask):
    n = N_STATE

    def body(ar_ref, lr_ref, ap_ref, lp_ref, b_ref, c_ref, e_ref, m_ref, lt_ref, ltr_ref, bb_ref, cc_ref):
        lr, li, _, _ = _disc(ar_ref[0], ar_ref[1], lr_ref[...])
        pr, pi = lr, li
        for k in range(1, 9):
            for ref, sgn, edge in ((lt_ref, 1.0, 24 + k - 1), (ltr_ref, -1.0, 24 + 8 - k)):
                if k in (1, 2, 4):
                    r0 = {1: 0, 2: 8, 4: 16}[k]
                    ref[r0:r0 + 8, 0:n] = jnp.broadcast_to(pr, (8, n))
                    ref[r0:r0 + 8, n:2 * n] = jnp.broadcast_to(sgn * pi, (8, n))
                ref[edge:edge + 1, 0:n] = pr
                ref[edge:edge + 1, n:2 * n] = sgn * pi
            pr, pi = pr * lr - pi * li, pr * li + pi * lr
        _, _, fr, fi = _disc(ap_ref[0], ap_ref[1], lp_ref[...])
        br, bi = b_ref[0], b_ref[1]
        e = e_ref[...]
        m = m_ref[...].astype(F32)
        bb_ref[:, 0:n] = (_dot((fr * br - fi * bi).astype(BF16), e) * m).astype(BF16)
        bb_ref[:, n:2 * n] = (_dot((fr * bi + fi * br).astype(BF16), e) * m).astype(BF16)
        cc_ref[:, 0:n] = (_dot(c_ref[0].astype(BF16), e) * m).astype(BF16)
        cc_ref[:, n:2 * n] = (-(_dot(c_ref[1].astype(BF16), e) * m)).astype(BF16)

    return pl.pallas_call(
        body, name="s5_disc",
        out_shape=[jax.ShapeDtypeStruct((32, 2 * n), F32), jax.ShapeDtypeStruct((32, 2 * n), F32),
                   jax.ShapeDtypeStruct((SSM_W, 2 * n), BF16), jax.ShapeDtypeStruct((SSM_W, 2 * n), BF16)],
        compiler_params=_cparams(None),
    )(a_row, ldt_row, a_rep, ldt_rep, bt, ct, tile_e, mask)


def _dot_exact(x, sel):
    hi = x.astype(BF16)
    r1 = x - hi.astype(F32)
    mid = r1.astype(BF16)
    lo = (r1 - mid.astype(F32)).astype(BF16)
    return _dot(hi, sel) + _dot(mid, sel) + _dot(lo, sel)


def _s5_disc_bwd(a, ldt, a_rep, ldt_rep, bt, mask, dl, d_bb, d_cc, fold):
    n = N_STATE

    def body(a_ref, l_ref, ap_ref, lp_ref, b_ref, m_ref, dl_ref, dbb_ref, dcc_ref, f_ref,
             da_ref, dldt_ref, db_ref, dc_ref):
        m = m_ref[...].astype(F32)
        fold_m = f_ref[...]
        diag = lambda x: _dot_exact(x * m, fold_m)
        dr, di = diag(dbb_ref[:, 0:n]), diag(dbb_ref[:, n:2 * n])
        dc_ref[0] = diag(dcc_ref[:, 0:n])
        dc_ref[1] = -diag(dcc_ref[:, n:2 * n])
        _, _, fr, fi = _disc(ap_ref[0], ap_ref[1], lp_ref[...])
        br, bi = b_ref[0], b_ref[1]
        db_ref[0] = fr * dr + fi * di
        db_ref[1] = fr * di - fi * dr
        per_state = lambda x: x.reshape(SSM_GROUPS, SSM_GROUP, SSM_STATE).sum(axis=1)
        dfr = per_state(dr * br + di * bi)
        dfi = per_state(di * br - dr * bi)
        _, vjp = jax.vjp(_disc, a_ref[0], a_ref[1], l_ref[...])
        dar, dai, dldt = vjp((dl_ref[0], dl_ref[1], dfr, dfi))
        da_ref[0] = dar
        da_ref[1] = dai
        dldt_ref[...] = jnp.sum(dldt, axis=1, keepdims=True)

    return pl.pallas_call(
        body, name="s5_disc_bwd",
        out_shape=[jax.ShapeDtypeStruct((2, SSM_GROUPS, SSM_STATE), F32),
                   jax.ShapeDtypeStruct((SSM_GROUPS, 1), F32),
                   jax.ShapeDtypeStruct((2, SSM_W, SSM_STATE), F32),
                   jax.ShapeDtypeStruct((2, SSM_W, SSM_STATE), F32)],
        compiler_params=_cparams(None),
    )(a, ldt, a_rep, ldt_rep, bt, mask, dl, d_bb, d_cc, fold)


def _row_block(rows, cap=512):
    for bm in range(min(cap, rows), 0, -1):
        if rows % bm == 0 and (bm % 8 == 0 or bm == rows):
            return bm
    return rows


def _pair_sum(fulls, got, segs):
    ns = len(segs)
    offs = _seg_offsets(segs)
    _, rtot, c = got.shape
    parts = 2
    pr = rtot // parts
    assert pr * parts == rtot and pr % 16 == 0
    pieces = [[] for _ in range(parts)]
    for a, (n, r) in enumerate(segs):
        for m in range(n):
            lo = offs[a] + m * r
            for h in range(parts):
                clo, chi = max(lo, h * pr), min(lo + r, (h + 1) * pr)
                if chi > clo:
                    pieces[h].append((a, m, clo - lo, clo - h * pr, chi - clo))
    n_sems = max(len(ps) for ps in pieces)

    def body(*refs):
        srcs = refs[:ns]
        got_ref, p32_ref, pbf_ref, own_v, sems = refs[ns:]
        k = pl.program_id(0)
        h = pl.program_id(1)
        dev = 2 * k + lax.axis_index("c")
        for hh in range(parts):
            @pl.when(h == hh)
            def _(hh=hh):
                cps = []
                for i, (a, m, so, do, rows) in enumerate(pieces[hh]):
                    start = pl.multiple_of(dev * segs[a][1] + so, 16)
                    cps.append(pltpu.make_async_copy(srcs[a].at[m, pl.ds(start, rows), :],
                                                     own_v.at[pl.ds(do, rows), :], sems.at[i]))
                for cp in cps:
                    cp.start()
                for cp in cps:
                    cp.wait()
        s = own_v[...].astype(F32) + got_ref[0].astype(F32)
        p32_ref[0] = s
        pbf_ref[0] = s.astype(BF16)

    spec = pl.BlockSpec((1, pr, c), lambda k, h: (k, h, 0))
    return pl.pallas_call(
        body, name="pair_sum",
        grid=(4, parts),
        in_specs=[ANY] * ns + [spec], out_specs=[spec, spec],
        out_shape=[jax.ShapeDtypeStruct(got.shape, F32), jax.ShapeDtypeStruct(got.shape, BF16)],
        scratch_shapes=[pltpu.VMEM((pr, c), BF16), pltpu.SemaphoreType.DMA((n_sems,))],
        compiler_params=_cparams(("arbitrary", "arbitrary")),
    )(*fulls, got)


def _chip_sum(chip, p32, rb):
    _, r, c = p32.shape
    bm = _row_block(r)

    def body(chip_ref, o_ref, r_ref, s_ref):
        s_ref[...] = ((o_ref[0] + r_ref[0].astype(F32)) + r_ref[1].astype(F32)) + r_ref[2].astype(F32)

    return pl.pallas_call(
        body, name="chip_sum",
        grid_spec=pltpu.PrefetchScalarGridSpec(
            num_scalar_prefetch=1, grid=(r // bm,),
            in_specs=[pl.BlockSpec((1, bm, c), lambda k, chip_ref: (chip_ref[0], k, 0)),
                      pl.BlockSpec((3, bm, c), lambda k, chip_ref: (0, k, 0))],
            out_specs=pl.BlockSpec((bm, c), lambda k, chip_ref: (k, 0))),
        out_shape=jax.ShapeDtypeStruct((r, c), F32),
        compiler_params=_cparams(("parallel",)),
    )(chip, p32, rb)


def _sum8(x):
    _, r, c = x.shape
    bm = _row_block(r)

    def body(x_ref, s_ref):
        s = x_ref[0]
        for d in range(1, N_DEV):
            s = s + x_ref[d]
        s_ref[...] = s

    return pl.pallas_call(
        body, name="sum8",
        grid=(r // bm,),
        in_specs=[pl.BlockSpec((N_DEV, bm, c), lambda k: (0, k, 0))],
        out_specs=pl.BlockSpec((bm, c), lambda k: (k, 0)),
        out_shape=jax.ShapeDtypeStruct((r, c), F32),
        compiler_params=_cparams(("parallel",)),
    )(x)


def _adamw(w, g, m, v):
    r, c = w.shape
    bm = _row_block(r)
    bc1 = 1.0 - ADAM_B1 ** ADAM_STEP
    bc2 = 1.0 - ADAM_B2 ** ADAM_STEP

    def body(w_ref, g_ref, m_ref, v_ref, d_ref, nm_ref, nv_ref):
        gv = g_ref[...]
        nm = ADAM_B1 * m_ref[...] + (1.0 - ADAM_B1) * gv
        nv = ADAM_B2 * v_ref[...] + (1.0 - ADAM_B2) * (gv * gv)
        nm_ref[...] = nm
        nv_ref[...] = nv
        d_ref[...] = -ADAM_LR * ((nm / bc1) / (jnp.sqrt(nv / bc2) + ADAM_EPS) + ADAM_WD * w_ref[...])

    spec = pl.BlockSpec((bm, c), lambda k: (k, 0))
    shp = jax.ShapeDtypeStruct((r, c), F32)
    return pl.pallas_call(
        body, name="adamw",
        grid=(r // bm,),
        in_specs=[spec] * 4, out_specs=[spec] * 3, out_shape=[shp] * 3,
        compiler_params=_cparams(("parallel",)),
    )(w, g, m, v)


def _mesh_pos():
    return lax.axis_index("x"), lax.axis_index("y"), lax.axis_index("c")


def _dev_index(p):
    return 4 * p[0] + 2 * p[1] + p[2]


def _seg_offsets(segs):
    offs, o = [], 0
    for n, r in segs:
        offs.append(o)
        o += n * r
    return offs


def _remote(src, dst, send_sem, recv_sem, to):
    return pltpu.make_async_remote_copy(src_ref=src, dst_ref=dst, send_sem=send_sem, recv_sem=recv_sem,
                                        device_id=to, device_id_type=MESH)


def _allgather(pack, segs, name):
    rtot, c = pack.shape
    ns = len(segs)
    offs = _seg_offsets(segs)
    assert rtot == sum(n * r for n, r in segs)

    def body(pack_ref, *refs):
        outs = refs[:ns]
        send_sems, recv_sems, local_sem = refs[ns:]
        x, y, cc = _mesh_pos()
        me, sib = (x, y, cc), (x, y, 1 - cc)
        chips = [(1 - x, y), (x, 1 - y), (1 - x, 1 - y)]

        def pieces(dev, from_pack):
            res = []
            for a, (n, r) in enumerate(segs):
                for m in range(n):
                    dst = outs[a].at[m, pl.ds(pl.multiple_of(dev * r, r), r), :]
                    src = pack_ref.at[pl.ds(offs[a] + m * r, r), :] if from_pack else dst
                    res.append((src, dst))
            return res

        def push(k, dev, to, from_pack):
            for s, d in pieces(dev, from_pack):
                _remote(s, d, send_sems.at[k], recv_sems.at[k], to).start()

        def whole(k):
            return _remote(pack_ref, pack_ref, send_sems.at[k], recv_sems.at[k], me)

        my_dev = _dev_index(me)
        for s, d in pieces(my_dev, True):
            pltpu.make_async_copy(s, d, local_sem).start()
        push(0, my_dev, sib, True)
        for j, chip in enumerate(chips):
            push(1 + j, my_dev, (*chip, cc), True)
        for j, chip in enumerate(chips):
            whole(1 + j).wait_recv()
            push(4 + j, _dev_index((*chip, cc)), sib, False)
        whole(0).wait_recv()
        for j in range(3):
            whole(4 + j).wait_recv()
        for k in range(7):
            whole(k).wait_send()
        pltpu.make_async_copy(pack_ref, pack_ref, local_sem).wait()

    return pl.pallas_call(
        body, name=name,
        in_specs=[ANY], out_specs=[ANY] * ns,
        out_shape=[jax.ShapeDtypeStruct((n, N_DEV * r, c), pack.dtype) for n, r in segs],
        scratch_shapes=[pltpu.SemaphoreType.DMA((7,)), pltpu.SemaphoreType.DMA((7,)), pltpu.SemaphoreType.DMA],
    )(pack)


HBM = pl.BlockSpec(memory_space=pltpu.HBM)
SEM = pl.BlockSpec(memory_space=pltpu.SEMAPHORE)
VMEM_WHOLE = pl.BlockSpec(memory_space=pltpu.VMEM)
EFFECT = pltpu.SideEffectType.DATAFLOW_SIDE_EFFECTING


def _hbm(a):
    return pltpu.with_memory_space_constraint(a, pltpu.HBM)


def _ag_start(pack, segs, after, name):
    rtot, c = pack.shape
    ns = len(segs)
    offs = _seg_offsets(segs)

    def body(pack_ref, *refs):
        lands = refs[:ns]
        send_sems, recv_sems = refs[ns + 1], refs[ns + 2]
        token = refs[-1]
        x, y, cc = _mesh_pos()
        my_dev = _dev_index((x, y, cc))
        targets = [(x, y, 1 - cc), (1 - x, y, cc), (x, 1 - y, cc), (1 - x, 1 - y, cc)]
        for k, to in enumerate(targets):
            for a, (n, r) in enumerate(segs):
                for m in range(n):
                    _remote(pack_ref.at[pl.ds(offs[a] + m * r, r), :],
                            lands[a].at[m, pl.ds(pl.multiple_of(my_dev * r, r), r), :],
                            send_sems.at[k], recv_sems.at[k], to).start()
        token[...] = jnp.zeros_like(token)

    land_shapes = [(n, N_DEV * r, c) for n, r in segs]
    outs = pl.pallas_call(
        body, name=name,
        in_specs=[HBM] * (1 + ns) + [ANY],
        out_specs=[SEM, SEM, HBM] + [HBM] * ns + [VMEM_WHOLE],
        out_shape=[pltpu.SemaphoreType.DMA((4,)), pltpu.SemaphoreType.DMA((4,)), pltpu.HBM(pack.shape, pack.dtype)]
        + [pltpu.HBM(s, pack.dtype) for s in land_shapes] + [jax.ShapeDtypeStruct((SUBLANES, LANES), F32)],
        input_output_aliases={0: 2, **{1 + i: 3 + i for i in range(ns)}},
        compiler_params=pltpu.CompilerParams(has_side_effects=EFFECT),
    )(_hbm(pack), *[_hbm(lax.empty(s, pack.dtype)) for s in land_shapes], after)
    return outs[0], outs[1], outs[2], list(outs[3:3 + ns]), outs[-1]


def _ag_wait(send_sems, recv_sems, pack, lands, after, name):
    ns = len(lands)

    def body(pack_ref, *refs):
        send_ref, recv_ref = refs[ns], refs[ns + 1]
        me = _mesh_pos()
        for k in range(4):
            whole = _remote(pack_ref, pack_ref, send_ref.at[k], recv_ref.at[k], me)
            whole.wait_send()
            whole.wait_recv()

    outs = pl.pallas_call(
        body, name=name,
        in_specs=[HBM] * (1 + ns) + [SEM, SEM, ANY],
        out_specs=[HBM] * (1 + ns),
        out_shape=[pltpu.HBM(pack.shape, pack.dtype)] + [pltpu.HBM(a.shape, a.dtype) for a in lands],
        input_output_aliases={i: i for i in range(1 + ns)},
        compiler_params=pltpu.CompilerParams(has_side_effects=EFFECT),
    )(pack, *lands, send_sems, recv_sems, after)
    return outs[0], list(outs[1:])


def _ag_finish(pack, lands, segs):
    rtot, c = pack.shape
    ns = len(segs)
    offs = _seg_offsets(segs)

    def body(pack_ref, *refs):
        outs = refs[ns:2 * ns]
        stage, send_sems, recv_sems, local_sems = refs[2 * ns:]
        x, y, cc = _mesh_pos()
        me, sib = (x, y, cc), (x, y, 1 - cc)
        chips = [(1 - x, y), (x, 1 - y), (1 - x, 1 - y)]

        def rows(a, m, dev):
            return outs[a].at[m, pl.ds(pl.multiple_of(dev * segs[a][1], segs[a][1]), segs[a][1]), :]

        for j, chip in enumerate(chips):
            dev = _dev_index((*chip, cc))
            for a, (n, r) in enumerate(segs):
                for m in range(n):
                    _remote(rows(a, m, dev), rows(a, m, dev), send_sems.at[j], recv_sems.at[j], sib).start()
        load = pltpu.make_async_copy(pack_ref, stage, local_sems.at[0])
        load.start()
        load.wait()
        my_dev = _dev_index(me)
        for a, (n, r) in enumerate(segs):
            for m in range(n):
                pltpu.make_async_copy(stage.at[pl.ds(offs[a] + m * r, r), :], rows(a, m, my_dev), local_sems.at[1]).start()
        pltpu.make_async_copy(stage, pack_ref, local_sems.at[1]).wait()
        for j in range(3):
            _remote(pack_ref, pack_ref, send_sems.at[j], recv_sems.at[j], me).wait()

    outs = pl.pallas_call(
        body, name="ag_finish",
        in_specs=[ANY] * (1 + ns), out_specs=[ANY] * ns,
        out_shape=[jax.ShapeDtypeStruct(a.shape, a.dtype) for a in lands],
        input_output_aliases={1 + i: i for i in range(ns)},
        scratch_shapes=[pltpu.VMEM((rtot, c), pack.dtype), pltpu.SemaphoreType.DMA((3,)),
                        pltpu.SemaphoreType.DMA((3,)), pltpu.SemaphoreType.DMA((2,))],
        compiler_params=_cparams(None, 16),
    )(pack, *lands)
    return list(outs)


def _rs_chips_start(pbf, name):
    _, rtot, c = pbf.shape

    def body(pbf_ref, land_ref, send_sems, recv_sems, pbf_thru, land_thru, token):
        x, y, cc = _mesh_pos()
        for j, (cx, cy) in enumerate([(1 - x, y), (x, 1 - y), (1 - x, 1 - y)]):
            _remote(pbf_ref.at[2 * cx + cy], land_ref.at[j], send_sems.at[j], recv_sems.at[j], (cx, cy, cc)).start()
        token[...] = jnp.zeros_like(token)

    return pl.pallas_call(
        body, name=name,
        in_specs=[HBM, HBM],
        out_specs=[SEM, SEM, HBM, HBM, VMEM_WHOLE],
        out_shape=[pltpu.SemaphoreType.DMA((3,)), pltpu.SemaphoreType.DMA((3,)), pltpu.HBM(pbf.shape, pbf.dtype),
                   pltpu.HBM((3, rtot, c), pbf.dtype), jax.ShapeDtypeStruct((SUBLANES, LANES), F32)],
        input_output_aliases={0: 2, 1: 3},
        compiler_params=pltpu.CompilerParams(has_side_effects=EFFECT),
    )(_hbm(pbf), _hbm(lax.empty((3, rtot, c), pbf.dtype)))


def _rs_chips_wait(send_sems, recv_sems, pbf, land, after, name):
    def body(pbf_ref, land_ref, send_ref, recv_ref, after_ref, pbf_out, land_out):
        me = _mesh_pos()
        for j in range(3):
            cp = _remote(pbf_ref.at[0], land_ref.at[j], send_ref.at[j], recv_ref.at[j], me)
            cp.wait_send()
            cp.wait_recv()

    return pl.pallas_call(
        body, name=name,
        in_specs=[HBM, HBM, SEM, SEM, ANY], out_specs=[HBM, HBM],
        out_shape=[pltpu.HBM(pbf.shape, pbf.dtype), pltpu.HBM(land.shape, land.dtype)],
        input_output_aliases={0: 0, 1: 1},
        compiler_params=pltpu.CompilerParams(has_side_effects=EFFECT),
    )(pbf, land, send_sems, recv_sems, after)[1]


def _rs_sibling(fulls, segs):
    ns = len(segs)
    offs = _seg_offsets(segs)
    rtot = sum(n * r for n, r in segs)
    c = fulls[0].shape[-1]

    def body(*refs):
        srcs = refs[:ns]
        got_ref, send_sem, recv_sem = refs[ns:]
        x, y, cc = _mesh_pos()
        me, sib = (x, y, cc), (x, y, 1 - cc)
        for k in range(4):
            for a, (n, r) in enumerate(segs):
                for m in range(n):
                    theirs = srcs[a].at[m, pl.ds(pl.multiple_of((2 * k + 1 - cc) * r, r), r), :]
                    _remote(theirs, got_ref.at[k, pl.ds(offs[a] + m * r, r), :], send_sem, recv_sem, sib).start()
        _remote(got_ref, got_ref, send_sem, recv_sem, me).wait()

    return pl.pallas_call(
        body, name="rs_sibling",
        in_specs=[ANY] * ns, out_specs=ANY, out_shape=jax.ShapeDtypeStruct((4, rtot, c), fulls[0].dtype),
        scratch_shapes=[pltpu.SemaphoreType.DMA, pltpu.SemaphoreType.DMA],
    )(*fulls)


def _rs_chips(pbf):
    _, rtot, c = pbf.shape

    def body(pbf_ref, got_ref, send_sems, recv_sems):
        x, y, cc = _mesh_pos()
        chips = [(1 - x, y), (x, 1 - y), (1 - x, 1 - y)]
        cps = [_remote(pbf_ref.at[2 * cx + cy], got_ref.at[j], send_sems.at[j], recv_sems.at[j], (cx, cy, cc))
               for j, (cx, cy) in enumerate(chips)]
        for cp in cps:
            cp.start()
        for cp in cps:
            cp.wait()

    return pl.pallas_call(
        body, name="rs_chips",
        in_specs=[ANY], out_specs=ANY,
        out_shape=jax.ShapeDtypeStruct((3, rtot, c), BF16),
        scratch_shapes=[pltpu.SemaphoreType.DMA((3,)), pltpu.SemaphoreType.DMA((3,))],
    )(pbf)


def _tp(w):
    return jnp.swapaxes(w, -1, -2)


def _s5_prepare(a_re, a_im, log_dt, b_re, b_im, c_re, c_im):
    a = jnp.stack([a_re, a_im], axis=1)
    ldt = jnp.broadcast_to(log_dt[:, :, None], (DEPTH, SSM_GROUPS, SSM_STATE))
    a_row = a.reshape(DEPTH, 2, 1, N_STATE)
    ldt_row = ldt.reshape(DEPTH, 1, N_STATE)
    a_rep = jnp.repeat(a, SSM_GROUP, axis=2)
    ldt_rep = jnp.repeat(ldt, SSM_GROUP, axis=1)
    bt = jnp.stack([_tp(b_re), _tp(b_im)], axis=1).reshape(DEPTH, 2, SSM_W, SSM_STATE)
    ct = jnp.stack([c_re, c_im], axis=1).reshape(DEPTH, 2, SSM_W, SSM_STATE)
    tile_e = jnp.tile(jnp.eye(SSM_STATE, dtype=BF16), (1, SSM_GROUPS))
    mask = jnp.repeat(jnp.repeat(jnp.eye(SSM_GROUPS, dtype=BF16), SSM_GROUP, axis=0), SSM_STATE, axis=1)
    out = []
    for l in range(DEPTH):
        tabs = _s5_disc(a_row[l], ldt_row[l], a_rep[l], ldt_rep[l], bt[l], ct[l], tile_e, mask)
        out.append(((a[l], ldt[l], a_rep[l], ldt_rep[l], bt[l], mask), *tabs))
    return out


def _layer_fwd(h, p_l, small, big):
    saved = {'h0': h}
    h, saved['gu1'] = _ffn_fwd(h, small['ffn1_norm'], big['ff1'])
    saved['h1'] = h
    z = _inproj_fwd(h, small['mix_norm'], big['wint'])
    ya, ys, hs = _s5conv_fwd(z, small['conv_w'], small['conv_b'], small['bbmat'], small['ccmat'], small['dvec'],
                             small['ltab'])
    saved.update(z=z, ya=ya, ys=ys, hs=hs)
    h = _mix_out_fwd(h, ya, ys, big['glu'], small['glu_b'], small['conv_out_norm'], small['ssm_out_norm'], big['wout'])
    saved['h2'] = h
    h, saved['gu2'] = _ffn_fwd(h, small['ffn2_norm'], big['ff2'])
    saved['h3'] = h
    h = _ple_fwd(h, small['ple_norm'], p_l, big['plg'], big['plpt'])
    return h, saved


def _ffn_bwd(h_in, g, dh, gu, w3):
    dh_in, dga, ud, dg = _ffn_bwd_act(h_in, g, dh, gu, w3)
    return dh_in, _matmul_tn(dga, ud, FF_BLOCK, BF16, "ffn_wgrad"), dg


def _layer_bwd(dh, p_l, small, big, saved):
    gs = {}
    dh, u, dq, dpp, pb, gs['ple_norm'] = _ple_bwd(saved['h3'], small['ple_norm'], p_l, dh, big['plg'], big['plpt'])
    d_plg = _matmul_tn(u, dq, 256, BF16, "ple_gate_wgrad")
    d_plpt = _matmul_tn(dpp, pb, 256, BF16, "ple_proj_wgrad")
    dh, d_ff2, gs['ffn2_norm'] = _ffn_bwd(saved['h2'], small['ffn2_norm'], dh, saved['gu2'], big['ff2'])

    dya, dys, ycat, dhb, zg, dq, part = _mix_out_bwd(dh, saved['ya'], saved['ys'], big['glu'], small['glu_b'],
                                                     small['conv_out_norm'], small['ssm_out_norm'], big['wout'])
    d_wout = _matmul_tn(ycat, dhb, 256, BF16, "w_out_wgrad")
    d_glu = _matmul_tn(zg, dq, 256, BF16, "glu_wgrad")
    dz, gadj, us, dyb, dl, dcw = _s5conv_bwd(saved['z'], saved['hs'], dya, dys, small['conv_w'], small['conv_b'],
                                             small['bbmat'], small['ccmat'], small['dvec'], small['ltab_rev'])
    d_bb = _matmul_tn(us, gadj, SSM_W, F32, "s5_b_wgrad", bn=1024)[0]
    d_cc = _matmul_tn(dyb, saved['hs'][None], SSM_W, F32, "s5_c_wgrad", bn=1024)[0]
    dh, u, gs['mix_norm'] = _inproj_bwd(saved['h1'], small['mix_norm'], dh, dz, big['wint'])
    d_wint = _matmul_tn(dz[None], u, 256, BF16, "w_in_wgrad")
    dh, d_ff1, gs['ffn1_norm'] = _ffn_bwd(saved['h0'], small['ffn1_norm'], dh, saved['gu1'], big['ff1'])

    dlb = dl[0].reshape(2, SSM_GROUPS, SSM_STATE)
    fold = jnp.tile(jnp.eye(SSM_STATE, dtype=BF16), (SSM_GROUPS, 1))
    da, dldt, dbt, dct = _s5_disc_bwd(*small['disc_in'], dlb, d_bb, d_cc, fold)
    gs['ssm_A_re'], gs['ssm_A_im'] = da[0], da[1]
    gs['ssm_log_dt'] = dldt[:, 0]
    ghp = (SSM_GROUPS, SSM_GROUP, SSM_STATE)
    gs['ssm_B_re'], gs['ssm_B_im'] = dbt[0].reshape(ghp), dbt[1].reshape(ghp)
    gs['ssm_C_re'], gs['ssm_C_im'] = dct[0].reshape(ghp), dct[1].reshape(ghp)
    gs['conv_w'] = dcw[0:3]
    gs['conv_b'] = dcw[3]
    gs['ssm_D'] = dcw[4].reshape(SSM_GROUPS, SSM_GROUP)
    gs['conv_out_norm'], gs['ssm_out_norm'], gs['glu_b'] = part[0], part[1], part[2]
    for n in ('ple_norm', 'ffn2_norm', 'mix_norm', 'ffn1_norm'):
        gs[n] = gs[n][0]
    fulls = [d_ff1, d_ff2, d_wint, d_wout, d_plg,
             d_plpt.reshape(1, D_MODEL * PLE_DIM // D_MODEL, D_MODEL), d_glu.reshape(1, SSM_W * SSM_W // D_MODEL, D_MODEL)]
    return dh, fulls, gs


VIEW_T = ('ffn1_w_gate', 'ffn1_w_up', 'ffn2_w_gate', 'ffn2_w_up', 'ssm_B_re', 'ssm_B_im')


def _view(name, a):
    return _tp(a) if name in VIEW_T else a


def _layer_pack(W, l):
    return jnp.concatenate([
        _tp(W['ffn1_w_gate'][l]), _tp(W['ffn1_w_up'][l]), W['ffn1_w_down'][l],
        _tp(W['ffn2_w_gate'][l]), _tp(W['ffn2_w_up'][l]), W['ffn2_w_down'][l],
        _tp(W['w_in'][l]), W['w_out'][l], W['ple_w_gate'][l],
        _tp(W['ple_w_proj'][l]).reshape(-1, D_MODEL), W['glu_w'][l].reshape(-1, D_MODEL)], axis=0).astype(BF16)


def _pad_rows(flat, mult):
    per = mult * LANES
    n = flat.shape[0]
    tot = -(-n // per) * per
    return jnp.pad(flat, (0, tot - n)).reshape(tot // LANES, LANES)


def _adamw_any(w, g, m, v):
    shp = w.shape
    two = (lambda t: t.reshape(-1, shp[-1]))
    d, nm, nv = _adamw(two(w), two(g), two(m), two(v))
    return d.reshape(shp), nm.reshape(shp), nv.reshape(shp)


def kernel(x, p, ffn1_norm, ffn1_w_gate, ffn1_w_up, ffn1_w_down, mix_norm, w_in, conv_w, conv_b, ssm_A_re, ssm_A_im, ssm_B_re, ssm_B_im, ssm_C_re, ssm_C_im, ssm_D, ssm_log_dt, glu_w, glu_b, conv_out_norm, ssm_out_norm, w_out, ffn2_norm, ffn2_w_gate, ffn2_w_up, ffn2_w_down, ple_norm, ple_w_gate, ple_w_proj, final_norm, loss_target, m_ffn1_norm, m_ffn1_w_gate, m_ffn1_w_up, m_ffn1_w_down, m_mix_norm, m_w_in, m_conv_w, m_conv_b, m_ssm_A_re, m_ssm_A_im, m_ssm_B_re, m_ssm_B_im, m_ssm_C_re, m_ssm_C_im, m_ssm_D, m_ssm_log_dt, m_glu_w, m_glu_b, m_conv_out_norm, m_ssm_out_norm, m_w_out, m_ffn2_norm, m_ffn2_w_gate, m_ffn2_w_up, m_ffn2_w_down, m_ple_norm, m_ple_w_gate, m_ple_w_proj, m_final_norm, v_ffn1_norm, v_ffn1_w_gate, v_ffn1_w_up, v_ffn1_w_down, v_mix_norm, v_w_in, v_conv_w, v_conv_b, v_ssm_A_re, v_ssm_A_im, v_ssm_B_re, v_ssm_B_im, v_ssm_C_re, v_ssm_C_im, v_ssm_D, v_ssm_log_dt, v_glu_w, v_glu_b, v_conv_out_norm, v_ssm_out_norm, v_w_out, v_ffn2_norm, v_ffn2_w_gate, v_ffn2_w_up, v_ffn2_w_down, v_ple_norm, v_ple_w_gate, v_ple_w_proj, v_final_norm):
    given = dict(locals())
    W = {n: given[n] for n in W_NAMES}
    M = {n: given['m_' + n] for n in W_NAMES}
    V = {n: given['v_' + n] for n in W_NAMES}
    Wv, Mv, Vv = [{n: _view(n, d[n]) for n in W_NAMES} for d in (W, M, V)]
    my_dev = _dev_index(_mesh_pos())
    my_chip = (my_dev // 2).astype(jnp.int32).reshape(1)

    packs = [_layer_pack(W, 0)]
    flight = _ag_start(packs[0], SEGS, W['final_norm'], "ag_start_0")
    s5 = _s5_prepare(*[W[n] for n in ('ssm_A_re', 'ssm_A_im', 'ssm_log_dt', 'ssm_B_re', 'ssm_B_im', 'ssm_C_re', 'ssm_C_im')])
    conv_shard = _pad_rows(W['conv_w'].reshape(-1), SUBLANES)
    conv_all = _allgather(conv_shard, ((1, SUBLANES),), "ag_conv_w")[0]
    conv_full = conv_all.reshape(N_DEV, -1)[:, :DEPTH * 3 * (CONV_W // N_DEV)]
    conv_full = conv_full.reshape(N_DEV, DEPTH, 3, CONV_W // N_DEV).transpose(1, 2, 0, 3).reshape(DEPTH, 3, CONV_W)

    packs += [_layer_pack(W, l) for l in range(1, DEPTH)]
    prepared = conv_full[0, 0:1, 0:1] + s5[DEPTH - 1][1][0:1, 0:1] + packs[DEPTH - 1][0:1, 0:1].astype(F32)

    smalls, saves, bigs = [], [], []
    h = x[0]
    for l in range(DEPTH):
        send_sems, recv_sems, pack_thru, lands, _ = flight
        pack_thru, lands = _ag_wait(send_sems, recv_sems, pack_thru, lands, prepared if l == 0 else h,
                                    "ag_wait_%d" % l)
        token = jnp.zeros((1, 1), F32)
        if l + 1 < DEPTH:
            flight = _ag_start(packs[l + 1], SEGS, lands[0], "ag_start_%d" % (l + 1))
            token = flight[4][0:1, 0:1]
        ff1, ff2, wint, wout, plg, plpt, glu = _ag_finish(pack_thru, lands, SEGS)
        bigs.append(dict(ff1=ff1, ff2=ff2, wint=wint[0], wout=wout[0], plg=plg[0],
                         plpt=plpt.reshape(D_MODEL, PLE_DIM), glu=glu.reshape(SSM_W, SSM_W)))
        small = {n: W[n][l][None] for n in ('ffn1_norm', 'mix_norm', 'conv_b', 'glu_b', 'conv_out_norm',
                                            'ssm_out_norm', 'ffn2_norm', 'ple_norm')}
        small['ffn1_norm'] = small['ffn1_norm'] + token
        small['conv_w'] = conv_full[l]
        small['dvec'] = W['ssm_D'][l].reshape(1, SSM_W)
        small['disc_in'], small['ltab'], small['ltab_rev'], small['bbmat'], small['ccmat'] = s5[l]
        h, saved = _layer_fwd(h, p[l, 0], small, bigs[l])
        smalls.append(small)
        saves.append(saved)
    loss_tile, dh, d_final = _final_loss(h, W['final_norm'][None], loss_target[0])
    loss = lax.psum(loss_tile[0, 0], ("x", "y", "c"))

    layer_gs = [None] * DEPTH
    shard_grads = [None] * DEPTH
    flight = None
    for l in reversed(range(DEPTH)):
        small = dict(smalls[l])
        if flight is not None:
            small['ple_norm'] = small['ple_norm'] + flight[1][4][0:1, 0:1]
        dh, fulls, layer_gs[l] = _layer_bwd(dh, p[l, 0], small, bigs[l], saves[l])
        p32, pbf = _pair_sum(fulls, _rs_sibling(fulls, SEGS), SEGS)
        if flight is not None:
            up, (send_sems, recv_sems, pbf_thru, land, _), p32_up = flight
            got3 = _rs_chips_wait(send_sems, recv_sems, pbf_thru, land, dh, "rs_wait_%d" % up)
            shard_grads[up] = _chip_sum(my_chip, p32_up, got3)
        flight = (l, _rs_chips_start(pbf, "rs_start_%d" % l), p32)
    grad_x = dh[None]

    gs = {n: jnp.stack([layer_gs[l][n] for l in range(DEPTH)]) for n in layer_gs[0]}
    gs['final_norm'] = d_final[0]
    flat = jnp.concatenate([gs[n].reshape(-1) for n in SMALL_NAMES] + [gs['conv_w'].reshape(-1)])
    n_flat = flat.shape[0]
    flat = _pad_rows(flat, SUBLANES)
    rows = flat.shape[0]
    gathered = _allgather(flat, ((1, rows),), "ag_small_grads")[0]
    red = _sum8(gathered.reshape(N_DEV, rows, LANES)).reshape(-1)[:n_flat]
    up, (send_sems, recv_sems, pbf_thru, land, _), p32_up = flight
    shard_grads[up] = _chip_sum(my_chip, p32_up,
                                _rs_chips_wait(send_sems, recv_sems, pbf_thru, land, red, "rs_wait_%d" % up))
    G = {}
    o = 0
    for n in SMALL_NAMES:
        G[n] = red[o:o + W[n].size].reshape(Wv[n].shape)
        o += W[n].size
    conv_g_full = red[o:].reshape(DEPTH, 3, CONV_W)
    G['conv_w'] = lax.dynamic_slice_in_dim(conv_g_full, my_dev * (CONV_W // N_DEV), CONV_W // N_DEV, axis=2)

    sg = jnp.stack(shard_grads)
    offs = _seg_offsets(SEGS)
    r = SEGS[0][1]
    for a, f in ((0, 'ffn1'), (1, 'ffn2')):
        G[f + '_w_gate'] = sg[:, offs[a]:offs[a] + r]
        G[f + '_w_up'] = sg[:, offs[a] + r:offs[a] + 2 * r]
        G[f + '_w_down'] = sg[:, offs[a] + 2 * r:offs[a] + 3 * r]
    G['w_in'] = _tp(sg[:, offs[2]:offs[2] + SEGS[2][1]])
    G['w_out'] = sg[:, offs[3]:offs[3] + SEGS[3][1]]
    G['ple_w_gate'] = sg[:, offs[4]:offs[4] + SEGS[4][1]]
    G['ple_w_proj'] = _tp(sg[:, offs[5]:offs[5] + SEGS[5][1]].reshape(DEPTH, D_MODEL // N_DEV, PLE_DIM))
    G['glu_w'] = sg[:, offs[6]:offs[6] + SEGS[6][1]].reshape(DEPTH, SSM_W // N_DEV, SSM_W)

    delta, new_m, new_v = {}, {}, {}
    cat = lambda src: _pad_rows(jnp.concatenate([src[n].reshape(-1) for n in SMALL_NAMES]), SUBLANES)
    d_s, m_s, v_s = _adamw(cat(Wv), cat(G), cat(Mv), cat(Vv))
    o = 0
    for n in SMALL_NAMES:
        for dst, src in ((delta, d_s), (new_m, m_s), (new_v, v_s)):
            dst[n] = src.reshape(-1)[o:o + W[n].size].reshape(Wv[n].shape)
        o += W[n].size
    for n in W_NAMES:
        if n not in delta:
            delta[n], new_m[n], new_v[n] = _adamw_any(Wv[n], G[n], Mv[n], Vv[n])

    outs = [[_view(n, d[n]) for n in W_NAMES] for d in (G, delta, new_m, new_v)]
    return (loss, grad_x, *outs[0], *outs[1], *outs[2], *outs[3])
```

```python
import math

import jax
import jax.numpy as jnp
from jax import lax
from jax.experimental import pallas as pl
from jax.experimental.pallas import tpu as pltpu

F32 = jnp.float32
BF16 = jnp.bfloat16

N_DEV = 8
DEPTH = 4
SEQ = 2048
D_MODEL = 1024
D_FF = 2816
CONV_W = 512
SSM_W = 512
SSM_GROUPS = 32
SSM_GROUP = 16
SSM_STATE = 64
N_STATE = SSM_GROUPS * SSM_STATE
IN_COLS = 2048
PLE_DIM = 256
EPS = 1e-6

ADAM_LR = 0.001
ADAM_B1 = 0.9
ADAM_B2 = 0.999
ADAM_EPS = 1e-08
ADAM_WD = 0.01
ADAM_STEP = 10

FF_BLOCK = 256
N_FF_BLOCKS = D_FF // FF_BLOCK
TOK_TILE_FFN = 1024
TOK_TILE = 512
CHUNK = 256
N_CHUNKS = SEQ // CHUNK
LANE_GROUP = 512
SUBLANES = 8
LANES = 128
MIB = 1024 * 1024

W_NAMES = ['ffn1_norm', 'ffn1_w_gate', 'ffn1_w_up', 'ffn1_w_down', 'mix_norm', 'w_in', 'conv_w', 'conv_b',
           'ssm_A_re', 'ssm_A_im', 'ssm_B_re', 'ssm_B_im', 'ssm_C_re', 'ssm_C_im', 'ssm_D', 'ssm_log_dt',
           'glu_w', 'glu_b', 'conv_out_norm', 'ssm_out_norm', 'w_out', 'ffn2_norm', 'ffn2_w_gate', 'ffn2_w_up',
           'ffn2_w_down', 'ple_norm', 'ple_w_gate', 'ple_w_proj', 'final_norm']
SMALL_NAMES = ['ffn1_norm', 'mix_norm', 'conv_b', 'ssm_A_re', 'ssm_A_im', 'ssm_B_re', 'ssm_B_im', 'ssm_C_re',
               'ssm_C_im', 'ssm_D', 'ssm_log_dt', 'glu_b', 'conv_out_norm', 'ssm_out_norm', 'ffn2_norm',
               'ple_norm', 'final_norm']

SEGS = ((3, 352), (3, 352), (1, 256), (1, 128), (1, 128), (1, 32), (1, 32))
PACK_ROWS = sum(n * r for n, r in SEGS)

MESH = pl.DeviceIdType.MESH
ANY = pl.BlockSpec(memory_space=pl.ANY)


def _cparams(sem=None, vmem_mib=48, **kw):
    return pltpu.CompilerParams(dimension_semantics=sem, vmem_limit_bytes=vmem_mib * MIB, **kw)


def _dot(a, b):
    return jnp.dot(a, b, preferred_element_type=F32)


def _dot_nt(a, b):
    return lax.dot_general(a, b, (((1,), (1,)), ((), ())), preferred_element_type=F32)


def _dot_tn(a, b):
    return lax.dot_general(a, b, (((0,), (0,)), ((), ())), preferred_element_type=F32)


def _rms_stats(x):
    r = lax.rsqrt(jnp.mean(x * x, axis=-1, keepdims=True) + EPS)
    return x * r, r


def _rms_bwd(dy, xh, r, g):
    dxh = dy * g
    dx = r * (dxh - xh * jnp.mean(dxh * xh, axis=-1, keepdims=True))
    dg = jnp.sum(dy * xh, axis=0, keepdims=True)
    return dx, dg


def _sigmoid(x):
    return 0.5 * jnp.tanh(0.5 * x) + 0.5


_GELU_C = math.sqrt(2.0 / math.pi)


def _gelu(x):
    t = jnp.tanh(_GELU_C * (x + 0.044715 * x * x * x))
    return 0.5 * x * (1.0 + t), t


def _gelu_grad(x, t):
    return 0.5 * (1.0 + t) + 0.5 * x * (1.0 - t * t) * _GELU_C * (1.0 + 3.0 * 0.044715 * x * x)


def _accumulate(ref, first, value):
    @pl.when(first)
    def _():
        ref[...] = value

    @pl.when(jnp.logical_not(first))
    def _():
        ref[...] += value


def _ffn_fwd(h, g, w3):
    tm = TOK_TILE_FFN
    last = N_FF_BLOCKS - 1

    def body(h_ref, g_ref, wgu_ref, wd_ref, wd_last_ref, out_ref, gu_ref, u_ref, a_ref):
        k = pl.program_id(1)

        @pl.when(k == 0)
        def _():
            x = h_ref[...]
            xh, _ = _rms_stats(x)
            u_ref[...] = (xh * g_ref[...]).astype(BF16)
            out_ref[...] = x
            a_ref[1] = jnp.zeros((tm, FF_BLOCK), BF16)

        out_ref[...] += 0.5 * _dot(a_ref[(k + 1) % 2], wd_ref[0])
        gu = _dot_nt(u_ref[...], wgu_ref[...].reshape(2 * FF_BLOCK, D_MODEL))
        gate, up = gu[:, :FF_BLOCK], gu[:, FF_BLOCK:]
        a_ref[k % 2] = (gate * _sigmoid(gate) * up).astype(BF16)
        gu_ref[0] = gate.astype(BF16)
        gu_ref[1] = up.astype(BF16)

        @pl.when(k == last)
        def _():
            out_ref[...] += 0.5 * _dot(a_ref[last % 2], wd_last_ref[0])

    return pl.pallas_call(
        body, name="ffn_fwd",
        grid=(SEQ // tm, N_FF_BLOCKS),
        in_specs=[pl.BlockSpec((tm, D_MODEL), lambda m, k: (m, 0)),
                  pl.BlockSpec((1, D_MODEL), lambda m, k: (0, 0)),
                  pl.BlockSpec((2, FF_BLOCK, D_MODEL), lambda m, k: (0, k, 0)),
                  pl.BlockSpec((1, FF_BLOCK, D_MODEL), lambda m, k: (2, jnp.maximum(k - 1, 0), 0)),
                  pl.BlockSpec((1, FF_BLOCK, D_MODEL), lambda m, k: (2, last, 0))],
        out_specs=[pl.BlockSpec((tm, D_MODEL), lambda m, k: (m, 0)),
                   pl.BlockSpec((2, tm, FF_BLOCK), lambda m, k: (0, m, k))],
        out_shape=[jax.ShapeDtypeStruct((SEQ, D_MODEL), F32),
                   jax.ShapeDtypeStruct((2, SEQ, D_FF), BF16)],
        scratch_shapes=[pltpu.VMEM((tm, D_MODEL), BF16), pltpu.VMEM((2, tm, FF_BLOCK), BF16)],
        compiler_params=_cparams(("parallel", "arbitrary")),
    )(h, g, w3, w3, w3)


def _ffn_bwd_act(h, g, dout, gu, w3):
    tm = TOK_TILE
    last = N_FF_BLOCKS - 1

    def body(h_ref, g_ref, d_ref, gu_ref, wd_ref, wgu_ref, wgu_last_ref, dh_ref, dga_ref, ud_ref, dg_ref,
             acc_ref, dgu_ref):
        m = pl.program_id(0)
        k = pl.program_id(1)

        @pl.when(k == 0)
        def _():
            xh, _ = _rms_stats(h_ref[...])
            ud_ref[0] = (xh * g_ref[...]).astype(BF16)
            ud_ref[1] = (0.5 * d_ref[...]).astype(BF16)
            acc_ref[...] = jnp.zeros_like(acc_ref)
            dgu_ref[1] = jnp.zeros((tm, 2 * FF_BLOCK), BF16)

        acc_ref[...] += _dot(dgu_ref[(k + 1) % 2], wgu_ref[...].reshape(2 * FF_BLOCK, D_MODEL))
        gate = gu_ref[0].astype(F32)
        up = gu_ref[1].astype(F32)
        sg = _sigmoid(gate)
        silu = gate * sg
        da = _dot_nt(ud_ref[1], wd_ref[0])
        dgate = (da * up * (sg + silu * (1.0 - sg))).astype(BF16)
        dup = (da * silu).astype(BF16)
        dga_ref[0] = dgate
        dga_ref[1] = dup
        dga_ref[2] = (silu * up).astype(BF16)
        dgu_ref[k % 2, :, 0:FF_BLOCK] = dgate
        dgu_ref[k % 2, :, FF_BLOCK:2 * FF_BLOCK] = dup

        @pl.when(k == last)
        def _():
            du = acc_ref[...] + _dot(dgu_ref[last % 2], wgu_last_ref[...].reshape(2 * FF_BLOCK, D_MODEL))
            xh, r = _rms_stats(h_ref[...])
            dx, dg = _rms_bwd(du, xh, r, g_ref[...])
            dh_ref[...] = d_ref[...] + dx
            _accumulate(dg_ref, m == 0, dg)

    return pl.pallas_call(
        body, name="ffn_bwd_act",
        grid=(SEQ // tm, N_FF_BLOCKS),
        in_specs=[pl.BlockSpec((tm, D_MODEL), lambda m, k: (m, 0)),
                  pl.BlockSpec((1, D_MODEL), lambda m, k: (0, 0)),
                  pl.BlockSpec((tm, D_MODEL), lambda m, k: (m, 0)),
                  pl.BlockSpec((2, tm, FF_BLOCK), lambda m, k: (0, m, k)),
                  pl.BlockSpec((1, FF_BLOCK, D_MODEL), lambda m, k: (2, k, 0)),
                  pl.BlockSpec((2, FF_BLOCK, D_MODEL), lambda m, k: (0, jnp.maximum(k - 1, 0), 0)),
                  pl.BlockSpec((2, FF_BLOCK, D_MODEL), lambda m, k: (0, last, 0))],
        out_specs=[pl.BlockSpec((tm, D_MODEL), lambda m, k: (m, 0)),
                   pl.BlockSpec((3, tm, FF_BLOCK), lambda m, k: (0, m, k)),
                   pl.BlockSpec((2, tm, D_MODEL), lambda m, k: (0, m, 0)),
                   pl.BlockSpec((1, D_MODEL), lambda m, k: (0, 0))],
        out_shape=[jax.ShapeDtypeStruct((SEQ, D_MODEL), F32),
                   jax.ShapeDtypeStruct((3, SEQ, D_FF), BF16),
                   jax.ShapeDtypeStruct((2, SEQ, D_MODEL), BF16),
                   jax.ShapeDtypeStruct((1, D_MODEL), F32)],
        scratch_shapes=[pltpu.VMEM((tm, D_MODEL), F32), pltpu.VMEM((2, tm, 2 * FF_BLOCK), BF16)],
        compiler_params=_cparams(("arbitrary", "arbitrary")),
    )(h, g, dout, gu, w3, w3, w3)


def _matmul_tn(a, b, bm, out_dtype, name, bn=None):
    na, t, m = a.shape
    nb, _, n = b.shape
    bn = n if bn is None else bn

    def body(a_ref, b_ref, o_ref):
        o_ref[0] = _dot_tn(a_ref[0], b_ref[0]).astype(out_dtype)

    return pl.pallas_call(
        body, name=name,
        grid=(na, m // bm, n // bn),
        in_specs=[pl.BlockSpec((1, t, bm), lambda i, k, j: (i, 0, k)),
                  pl.BlockSpec((1, t, bn), lambda i, k, j: (jnp.maximum(i - (na - nb), 0), 0, j))],
        out_specs=pl.BlockSpec((1, bm, bn), lambda i, k, j: (i, k, j)),
        out_shape=jax.ShapeDtypeStruct((na, m, n), out_dtype),
        compiler_params=_cparams(("arbitrary", "parallel", "parallel")),
    )(a, b)


def _inproj_fwd(h, g, wint):
    tm = TOK_TILE

    def body(h_ref, g_ref, w_ref, z_ref):
        xh, _ = _rms_stats(h_ref[...])
        z_ref[...] = _dot_nt((xh * g_ref[...]).astype(BF16), w_ref[...])

    return pl.pallas_call(
        body, name="inproj_fwd",
        grid=(SEQ // tm,),
        in_specs=[pl.BlockSpec((tm, D_MODEL), lambda m: (m, 0)),
                  pl.BlockSpec((1, D_MODEL), lambda m: (0, 0)),
                  pl.BlockSpec((IN_COLS, D_MODEL), lambda m: (0, 0))],
        out_specs=pl.BlockSpec((tm, IN_COLS), lambda m: (m, 0)),
        out_shape=jax.ShapeDtypeStruct((SEQ, IN_COLS), F32),
        compiler_params=_cparams(("parallel",)),
    )(h, g, wint)


def _inproj_bwd(h, g, dh, dz, wint):
    tm = TOK_TILE

    def body(h_ref, g_ref, dh_ref, dz_ref, w_ref, o_ref, u_ref, dg_ref):
        xh, r = _rms_stats(h_ref[...])
        u_ref[0] = (xh * g_ref[...]).astype(BF16)
        dx, dg = _rms_bwd(_dot(dz_ref[...], w_ref[...]), xh, r, g_ref[...])
        o_ref[...] = dh_ref[...] + dx
        _accumulate(dg_ref, pl.program_id(0) == 0, dg)

    return pl.pallas_call(
        body, name="inproj_bwd",
        grid=(SEQ // tm,),
        in_specs=[pl.BlockSpec((tm, D_MODEL), lambda m: (m, 0)),
                  pl.BlockSpec((1, D_MODEL), lambda m: (0, 0)),
                  pl.BlockSpec((tm, D_MODEL), lambda m: (m, 0)),
                  pl.BlockSpec((tm, IN_COLS), lambda m: (m, 0)),
                  pl.BlockSpec((IN_COLS, D_MODEL), lambda m: (0, 0))],
        out_specs=[pl.BlockSpec((tm, D_MODEL), lambda m: (m, 0)),
                   pl.BlockSpec((1, tm, D_MODEL), lambda m: (0, m, 0)),
                   pl.BlockSpec((1, D_MODEL), lambda m: (0, 0))],
        out_shape=[jax.ShapeDtypeStruct((SEQ, D_MODEL), F32),
                   jax.ShapeDtypeStruct((1, SEQ, D_MODEL), BF16),
                   jax.ShapeDtypeStruct((1, D_MODEL), F32)],
        compiler_params=_cparams(("arbitrary",)),
    )(h, g, dh, dz, wint)


def _row_ids(n, w):
    return lax.broadcasted_iota(jnp.int32, (n, w), 0)


def _bcast_row(x, i, n):
    return jnp.broadcast_to(x[i:i + 1, :], (n, x.shape[1]))


def _conv_taps(v, tail):
    n, w = v.shape
    rid = _row_ids(n, w)
    v1 = jnp.where(rid == 0, _bcast_row(tail, 7, n), pltpu.roll(v, 1, 0))
    v2 = jnp.where(rid == 0, _bcast_row(tail, 6, n),
                   jnp.where(rid == 1, _bcast_row(tail, 7, n), pltpu.roll(v, 2, 0)))
    return v1, v2


def _scan_chunk(work, ltab, carry, reverse):
    nblk = CHUNK // SUBLANES
    row = _row_ids(SUBLANES, LANE_GROUP)
    for gi in range(N_STATE // LANE_GROUP):
        cre = pl.ds(gi * LANE_GROUP, LANE_GROUP)
        cim = pl.ds(N_STATE + gi * LANE_GROUP, LANE_GROUP)
        pows = [(ltab[8 * k:8 * k + 8, cre], ltab[8 * k:8 * k + 8, cim]) for k in range(3)]
        pr = ltab[24:32, cre]
        pi = ltab[24:32, cim]

        def blk(i, c, cre=cre, cim=cim, pows=pows, pr=pr, pi=pi):
            cr, ci = c
            b = (nblk - 1 - i) if reverse else i
            r0 = pl.multiple_of(b * SUBLANES, SUBLANES)
            xr = work[pl.ds(r0, SUBLANES), cre]
            xi = work[pl.ds(r0, SUBLANES), cim]
            for k, s in enumerate((1, 2, 4)):
                lr, li = pows[k]
                if reverse:
                    keep = row < SUBLANES - s
                    sr = jnp.where(keep, pltpu.roll(xr, SUBLANES - s, 0), 0.0)
                    si = jnp.where(keep, pltpu.roll(xi, SUBLANES - s, 0), 0.0)
                else:
                    keep = row >= s
                    sr = jnp.where(keep, pltpu.roll(xr, s, 0), 0.0)
                    si = jnp.where(keep, pltpu.roll(xi, s, 0), 0.0)
                xr, xi = xr + lr * sr - li * si, xi + lr * si + li * sr
            xr, xi = xr + pr * cr - pi * ci, xi + pr * ci + pi * cr
            work[pl.ds(r0, SUBLANES), cre] = xr
            work[pl.ds(r0, SUBLANES), cim] = xi
            edge = 0 if reverse else SUBLANES - 1
            return _bcast_row(xr, edge, SUBLANES), _bcast_row(xi, edge, SUBLANES)

        cr, ci = lax.fori_loop(0, nblk, blk, (carry[:, cre], carry[:, cim]))
        carry[:, cre] = cr
        carry[:, cim] = ci


def _s5conv_fwd(z, convw, convb, bbmat, ccmat, dvec, ltab):
    def body(z_ref, cw_ref, cb_ref, bb_ref, cc_ref, d_ref, lt_ref, ya_ref, ys_ref, hs_ref,
             work, carry, tail):
        c = pl.program_id(0)

        @pl.when(c == 0)
        def _():
            carry[...] = jnp.zeros_like(carry)
            tail[...] = jnp.zeros_like(tail)

        zb = z_ref[:, 0:CONV_W]
        v = z_ref[:, CONV_W:2 * CONV_W] * z_ref[:, 2 * CONV_W:3 * CONV_W]
        us = z_ref[:, 3 * CONV_W:4 * CONV_W]
        v1, v2 = _conv_taps(v, tail[...])
        tail[...] = v[CHUNK - 8:CHUNK, :]
        y = cw_ref[0:1, :] * v2 + cw_ref[1:2, :] * v1 + cw_ref[2:3, :] * v
        ya_ref[...] = zb * (y + cb_ref[...])

        work[...] = _dot(us.astype(BF16), bb_ref[...])
        _scan_chunk(work, lt_ref, carry, reverse=False)
        hs = work[...].astype(BF16)
        hs_ref[...] = hs
        ys_ref[...] = _dot_nt(hs, cc_ref[...]) + d_ref[...] * us

    return pl.pallas_call(
        body, name="s5conv_fwd",
        grid=(N_CHUNKS,),
        in_specs=[pl.BlockSpec((CHUNK, IN_COLS), lambda c: (c, 0)),
                  pl.BlockSpec((3, CONV_W), lambda c: (0, 0)),
                  pl.BlockSpec((1, CONV_W), lambda c: (0, 0)),
                  pl.BlockSpec((SSM_W, 2 * N_STATE), lambda c: (0, 0)),
                  pl.BlockSpec((SSM_W, 2 * N_STATE), lambda c: (0, 0)),
                  pl.BlockSpec((1, SSM_W), lambda c: (0, 0)),
                  pl.BlockSpec((32, 2 * N_STATE), lambda c: (0, 0))],
        out_specs=[pl.BlockSpec((CHUNK, CONV_W), lambda c: (c, 0)),
                   pl.BlockSpec((CHUNK, SSM_W), lambda c: (c, 0)),
                   pl.BlockSpec((CHUNK, 2 * N_STATE), lambda c: (c, 0))],
        out_shape=[jax.ShapeDtypeStruct((SEQ, CONV_W), F32),
                   jax.ShapeDtypeStruct((SEQ, SSM_W), F32),
                   jax.ShapeDtypeStruct((SEQ, 2 * N_STATE), BF16)],
        scratch_shapes=[pltpu.VMEM((CHUNK, 2 * N_STATE), F32),
                        pltpu.VMEM((8, 2 * N_STATE), F32),
                        pltpu.VMEM((8, CONV_W), F32)],
        compiler_params=_cparams(("arbitrary",)),
    )(z, convw, convb, bbmat, ccmat, dvec, ltab)


def _s5conv_bwd(z, hs, dya, dys, convw, convb, bbmat, ccmat, dvec, ltab_rev):
    nc = N_CHUNKS
    hb = 16

    def body(z_ref, zp_ref, hs_ref, hp_ref, dya_ref, dys_ref, cw_ref, cb_ref, bb_ref, cc_ref, d_ref, lt_ref,
             dz_ref, g_ref, us_ref, dyb_ref, dl_ref, dcw_ref, work, carry, head):
        i = pl.program_id(0)
        first_chunk = i == nc - 1

        @pl.when(i == 0)
        def _():
            carry[...] = jnp.zeros_like(carry)
            head[...] = jnp.zeros_like(head)
            dl_ref[...] = jnp.zeros_like(dl_ref)
            dcw_ref[...] = jnp.zeros_like(dcw_ref)

        us = z_ref[:, 3 * CONV_W:4 * CONV_W]
        dy = dys_ref[...]
        dy_bf = dy.astype(BF16)
        us_ref[0] = us.astype(BF16)
        dyb_ref[0] = dy_bf

        work[...] = _dot(dy_bf, cc_ref[...])
        _scan_chunk(work, lt_ref, carry, reverse=True)
        gg = work[...]
        gg_bf = gg.astype(BF16)
        g_ref[0] = gg_bf
        dus = d_ref[...] * dy + _dot_nt(gg_bf, bb_ref[...])

        hcur = hs_ref[...].astype(F32)
        hlast = hp_ref[...].astype(F32)[hb - 1:hb, :]
        hlast = jnp.where(first_chunk, 0.0, hlast)
        rid = _row_ids(CHUNK, 2 * N_STATE)
        hprev = jnp.where(rid == 0, jnp.broadcast_to(hlast, (CHUNK, 2 * N_STATE)), pltpu.roll(hcur, 1, 0))
        gr, gi = gg[:, :N_STATE], gg[:, N_STATE:]
        hr, hi = hprev[:, :N_STATE], hprev[:, N_STATE:]
        dl_ref[:, :N_STATE] += (gr * hr + gi * hi).reshape(CHUNK // 8, 8, N_STATE).sum(axis=0)
        dl_ref[:, N_STATE:] += (gi * hr - gr * hi).reshape(CHUNK // 8, 8, N_STATE).sum(axis=0)

        @pl.when(i == nc - 1)
        def _():
            dl_ref[0:1, :] = jnp.sum(dl_ref[...], axis=0, keepdims=True)

        zb = z_ref[:, 0:CONV_W]
        zc = z_ref[:, CONV_W:2 * CONV_W]
        zv = z_ref[:, 2 * CONV_W:3 * CONV_W]
        v = zc * zv
        vtail = jnp.where(first_chunk, 0.0, zp_ref[:, CONV_W:2 * CONV_W] * zp_ref[:, 2 * CONV_W:3 * CONV_W])
        v1, v2 = _conv_taps(v, vtail)
        w0, w1, w2 = cw_ref[0:1, :], cw_ref[1:2, :], cw_ref[2:3, :]
        y = w0 * v2 + w1 * v1 + w2 * v
        dya_v = dya_ref[...]
        dzb = dya_v * (y + cb_ref[...])
        dyc = dya_v * zb
        hd = head[...]
        rc = _row_ids(CHUNK, CONV_W)
        n1 = jnp.where(rc == CHUNK - 1, _bcast_row(hd, 0, CHUNK), pltpu.roll(dyc, CHUNK - 1, 0))
        n2 = jnp.where(rc == CHUNK - 1, _bcast_row(hd, 1, CHUNK),
                       jnp.where(rc == CHUNK - 2, _bcast_row(hd, 0, CHUNK), pltpu.roll(dyc, CHUNK - 2, 0)))
        head[...] = dyc[0:8, :]
        dv = w2 * dyc + w1 * n1 + w0 * n2
        dz_ref[:, 0:CONV_W] = dzb.astype(BF16)
        dz_ref[:, CONV_W:2 * CONV_W] = (dv * zv).astype(BF16)
        dz_ref[:, 2 * CONV_W:3 * CONV_W] = (dv * zc).astype(BF16)
        dz_ref[:, 3 * CONV_W:4 * CONV_W] = dus.astype(BF16)
        dcw_ref[0:1, :] += jnp.sum(dyc * v2, axis=0, keepdims=True)
        dcw_ref[1:2, :] += jnp.sum(dyc * v1, axis=0, keepdims=True)
        dcw_ref[2:3, :] += jnp.sum(dyc * v, axis=0, keepdims=True)
        dcw_ref[3:4, :] += jnp.sum(dyc, axis=0, keepdims=True)
        dcw_ref[4:5, :] += jnp.sum(dy * us, axis=0, keepdims=True)

    rev = lambda i: nc - 1 - i
    return pl.pallas_call(
        body, name="s5conv_bwd",
        grid=(nc,),
        in_specs=[pl.BlockSpec((CHUNK, IN_COLS), lambda i: (rev(i), 0)),
                  pl.BlockSpec((8, IN_COLS), lambda i: (jnp.maximum(rev(i) * (CHUNK // 8) - 1, 0), 0)),
                  pl.BlockSpec((CHUNK, 2 * N_STATE), lambda i: (rev(i), 0)),
                  pl.BlockSpec((hb, 2 * N_STATE), lambda i: (jnp.maximum(rev(i) * (CHUNK // hb) - 1, 0), 0)),
                  pl.BlockSpec((CHUNK, CONV_W), lambda i: (rev(i), 0)),
                  pl.BlockSpec((CHUNK, SSM_W), lambda i: (rev(i), 0)),
                  pl.BlockSpec((3, CONV_W), lambda i: (0, 0)),
                  pl.BlockSpec((1, CONV_W), lambda i: (0, 0)),
                  pl.BlockSpec((SSM_W, 2 * N_STATE), lambda i: (0, 0)),
                  pl.BlockSpec((SSM_W, 2 * N_STATE), lambda i: (0, 0)),
                  pl.BlockSpec((1, SSM_W), lambda i: (0, 0)),
                  pl.BlockSpec((32, 2 * N_STATE), lambda i: (0, 0))],
        out_specs=[pl.BlockSpec((CHUNK, IN_COLS), lambda i: (rev(i), 0)),
                   pl.BlockSpec((1, CHUNK, 2 * N_STATE), lambda i: (0, rev(i), 0)),
                   pl.BlockSpec((1, CHUNK, SSM_W), lambda i: (0, rev(i), 0)),
                   pl.BlockSpec((1, CHUNK, SSM_W), lambda i: (0, rev(i), 0)),
                   pl.BlockSpec((8, 2 * N_STATE), lambda i: (0, 0)),
                   pl.BlockSpec((8, CONV_W), lambda i: (0, 0))],
        out_shape=[jax.ShapeDtypeStruct((SEQ, IN_COLS), BF16),
                   jax.ShapeDtypeStruct((1, SEQ, 2 * N_STATE), BF16),
                   jax.ShapeDtypeStruct((1, SEQ, SSM_W), BF16),
                   jax.ShapeDtypeStruct((1, SEQ, SSM_W), BF16),
                   jax.ShapeDtypeStruct((8, 2 * N_STATE), F32),
                   jax.ShapeDtypeStruct((8, CONV_W), F32)],
        scratch_shapes=[pltpu.VMEM((CHUNK, 2 * N_STATE), F32),
                        pltpu.VMEM((8, 2 * N_STATE), F32),
                        pltpu.VMEM((8, CONV_W), F32)],
        compiler_params=_cparams(("arbitrary",)),
    )(z, z, hs, hs, dya, dys, convw, convb, bbmat, ccmat, dvec, ltab_rev)


def _mix_out_fwd(h, ya, ys, gluw, glub, con, son, wout):
    tm = TOK_TILE

    def body(h_ref, ya_ref, ys_ref, gw_ref, gb_ref, con_ref, son_ref, wo_ref, o_ref):
        zg, _ = _gelu(ys_ref[...])
        q = _dot(zg.astype(BF16), gw_ref[...]) + gb_ref[...]
        out_s = zg * _sigmoid(q)
        na, _ = _rms_stats(ya_ref[...])
        ns, _ = _rms_stats(out_s)
        o_ref[...] = (h_ref[...]
                      + _dot((na * con_ref[...]).astype(BF16), wo_ref[0:CONV_W, :])
                      + _dot((ns * son_ref[...]).astype(BF16), wo_ref[CONV_W:2 * CONV_W, :]))

    row = lambda m: (m, 0)
    fixed = lambda m: (0, 0)
    return pl.pallas_call(
        body, name="mix_out_fwd",
        grid=(SEQ // tm,),
        in_specs=[pl.BlockSpec((tm, D_MODEL), row), pl.BlockSpec((tm, CONV_W), row), pl.BlockSpec((tm, SSM_W), row),
                  pl.BlockSpec((SSM_W, SSM_W), fixed), pl.BlockSpec((1, SSM_W), fixed),
                  pl.BlockSpec((1, CONV_W), fixed), pl.BlockSpec((1, SSM_W), fixed),
                  pl.BlockSpec((D_MODEL, D_MODEL), fixed)],
        out_specs=pl.BlockSpec((tm, D_MODEL), row),
        out_shape=jax.ShapeDtypeStruct((SEQ, D_MODEL), F32),
        compiler_params=_cparams(("parallel",)),
    )(h, ya, ys, gluw, glub, con, son, wout)


def _mix_out_bwd(dh, ya, ys, gluw, glub, con, son, wout):
    tm = TOK_TILE

    def body(dh_ref, ya_ref, ys_ref, gw_ref, gb_ref, con_ref, son_ref, wo_ref,
             dya_ref, dys_ref, yc_ref, dhb_ref, zg_ref, dq_ref, part_ref):
        ysv = ys_ref[...]
        zg, th = _gelu(ysv)
        zg_bf = zg.astype(BF16)
        s = _sigmoid(_dot(zg_bf, gw_ref[...]) + gb_ref[...])
        out_s = zg * s
        na, ra = _rms_stats(ya_ref[...])
        ns, rs = _rms_stats(out_s)
        dh_bf = dh_ref[...].astype(BF16)
        yc_ref[0, :, 0:CONV_W] = (na * con_ref[...]).astype(BF16)
        yc_ref[0, :, CONV_W:2 * CONV_W] = (ns * son_ref[...]).astype(BF16)
        dhb_ref[0] = dh_bf
        dca = _dot_nt(dh_bf, wo_ref[0:CONV_W, :])
        dcs = _dot_nt(dh_bf, wo_ref[CONV_W:2 * CONV_W, :])
        dya, dcon = _rms_bwd(dca, na, ra, con_ref[...])
        dos, dson = _rms_bwd(dcs, ns, rs, son_ref[...])
        dya_ref[...] = dya
        dq = dos * zg * s * (1.0 - s)
        dq_bf = dq.astype(BF16)
        dzg = dos * s + _dot_nt(dq_bf, gw_ref[...])
        dys_ref[...] = dzg * _gelu_grad(ysv, th)
        zg_ref[0] = zg_bf
        dq_ref[0] = dq_bf
        rid = _row_ids(SUBLANES, SSM_W)
        part = jnp.zeros((SUBLANES, SSM_W), F32)
        for i, rowv in enumerate((dcon, dson, jnp.sum(dq, axis=0, keepdims=True))):
            part = jnp.where(rid == i, jnp.broadcast_to(rowv, (SUBLANES, SSM_W)), part)
        _accumulate(part_ref, pl.program_id(0) == 0, part)

    row = lambda m: (m, 0)
    fixed = lambda m: (0, 0)
    lead = lambda m: (0, m, 0)
    return pl.pallas_call(
        body, name="mix_out_bwd",
        grid=(SEQ // tm,),
        in_specs=[pl.BlockSpec((tm, D_MODEL), row), pl.BlockSpec((tm, CONV_W), row), pl.BlockSpec((tm, SSM_W), row),
                  pl.BlockSpec((SSM_W, SSM_W), fixed), pl.BlockSpec((1, SSM_W), fixed),
                  pl.BlockSpec((1, CONV_W), fixed), pl.BlockSpec((1, SSM_W), fixed),
                  pl.BlockSpec((D_MODEL, D_MODEL), fixed)],
        out_specs=[pl.BlockSpec((tm, CONV_W), row), pl.BlockSpec((tm, SSM_W), row),
                   pl.BlockSpec((1, tm, D_MODEL), lead), pl.BlockSpec((1, tm, D_MODEL), lead),
                   pl.BlockSpec((1, tm, SSM_W), lead), pl.BlockSpec((1, tm, SSM_W), lead),
                   pl.BlockSpec((8, SSM_W), fixed)],
        out_shape=[jax.ShapeDtypeStruct((SEQ, CONV_W), F32), jax.ShapeDtypeStruct((SEQ, SSM_W), F32),
                   jax.ShapeDtypeStruct((1, SEQ, D_MODEL), BF16), jax.ShapeDtypeStruct((1, SEQ, D_MODEL), BF16),
                   jax.ShapeDtypeStruct((1, SEQ, SSM_W), BF16), jax.ShapeDtypeStruct((1, SEQ, SSM_W), BF16),
                   jax.ShapeDtypeStruct((8, SSM_W), F32)],
        compiler_params=_cparams(("arbitrary",)),
    )(dh, ya, ys, gluw, glub, con, son, wout)


def _ple_fwd(h, g, p, wgate, wprojt):
    tm = TOK_TILE

    def body(h_ref, g_ref, p_ref, wg_ref, wp_ref, o_ref):
        x = h_ref[...]
        xh, _ = _rms_stats(x)
        s = _sigmoid(_dot((xh * g_ref[...]).astype(BF16), wg_ref[...]))
        o_ref[...] = x + _dot_nt(p_ref[...].astype(BF16), wp_ref[...]) * s

    row = lambda m: (m, 0)
    fixed = lambda m: (0, 0)
    return pl.pallas_call(
        body, name="ple_fwd",
        grid=(SEQ // tm,),
        in_specs=[pl.BlockSpec((tm, D_MODEL), row), pl.BlockSpec((1, D_MODEL), fixed), pl.BlockSpec((tm, PLE_DIM), row),
                  pl.BlockSpec((D_MODEL, D_MODEL), fixed), pl.BlockSpec((D_MODEL, PLE_DIM), fixed)],
        out_specs=pl.BlockSpec((tm, D_MODEL), row),
        out_shape=jax.ShapeDtypeStruct((SEQ, D_MODEL), F32),
        compiler_params=_cparams(("parallel",)),
    )(h, g, p, wgate, wprojt)


def _ple_bwd(h, g, p, dh, wgate, wprojt):
    tm = TOK_TILE

    def body(h_ref, g_ref, p_ref, dh_ref, wg_ref, wp_ref, o_ref, u_ref, dq_ref, dpp_ref, pb_ref, dg_ref):
        xh, r = _rms_stats(h_ref[...])
        u = (xh * g_ref[...]).astype(BF16)
        s = _sigmoid(_dot(u, wg_ref[...]))
        p_bf = p_ref[...].astype(BF16)
        pp = _dot_nt(p_bf, wp_ref[...])
        dhv = dh_ref[...]
        dq = (dhv * pp * s * (1.0 - s)).astype(BF16)
        u_ref[0] = u
        dq_ref[0] = dq
        dpp_ref[0] = (dhv * s).astype(BF16)
        pb_ref[0] = p_bf
        dx, dg = _rms_bwd(_dot_nt(dq, wg_ref[...]), xh, r, g_ref[...])
        o_ref[...] = dhv + dx
        _accumulate(dg_ref, pl.program_id(0) == 0, dg)

    row = lambda m: (m, 0)
    fixed = lambda m: (0, 0)
    lead = lambda m: (0, m, 0)
    big = jax.ShapeDtypeStruct((1, SEQ, D_MODEL), BF16)
    return pl.pallas_call(
        body, name="ple_bwd",
        grid=(SEQ // tm,),
        in_specs=[pl.BlockSpec((tm, D_MODEL), row), pl.BlockSpec((1, D_MODEL), fixed), pl.BlockSpec((tm, PLE_DIM), row),
                  pl.BlockSpec((tm, D_MODEL), row),
                  pl.BlockSpec((D_MODEL, D_MODEL), fixed), pl.BlockSpec((D_MODEL, PLE_DIM), fixed)],
        out_specs=[pl.BlockSpec((tm, D_MODEL), row),
                   pl.BlockSpec((1, tm, D_MODEL), lead), pl.BlockSpec((1, tm, D_MODEL), lead),
                   pl.BlockSpec((1, tm, D_MODEL), lead), pl.BlockSpec((1, tm, PLE_DIM), lead),
                   pl.BlockSpec((1, D_MODEL), fixed)],
        out_shape=[jax.ShapeDtypeStruct((SEQ, D_MODEL), F32), big, big, big,
                   jax.ShapeDtypeStruct((1, SEQ, PLE_DIM), BF16),
                   jax.ShapeDtypeStruct((1, D_MODEL), F32)],
        compiler_params=_cparams(("arbitrary",)),
    )(h, g, p, dh, wgate, wprojt)


def _final_loss(h, g, target):
    tm = TOK_TILE

    def body(h_ref, g_ref, t_ref, loss_ref, dh_ref, dg_ref):
        first = pl.program_id(0) == 0
        xh, r = _rms_stats(h_ref[...])
        diff = xh * g_ref[...] - t_ref[...]
        part = 0.5 * jnp.sum(jnp.mean(diff * diff, axis=-1, keepdims=True), axis=0, keepdims=True)
        _accumulate(loss_ref, first, jnp.broadcast_to(part, (SUBLANES, LANES)))
        dx, dg = _rms_bwd(diff * (1.0 / D_MODEL), xh, r, g_ref[...])
        dh_ref[...] = dx
        _accumulate(dg_ref, first, dg)

    row = lambda m: (m, 0)
    fixed = lambda m: (0, 0)
    return pl.pallas_call(
        body, name="final_loss",
        grid=(SEQ // tm,),
        in_specs=[pl.BlockSpec((tm, D_MODEL), row), pl.BlockSpec((1, D_MODEL), fixed),
                  pl.BlockSpec((tm, D_MODEL), row)],
        out_specs=[pl.BlockSpec((SUBLANES, LANES), fixed),
                   pl.BlockSpec((tm, D_MODEL), row),
                   pl.BlockSpec((1, D_MODEL), fixed)],
        out_shape=[jax.ShapeDtypeStruct((SUBLANES, LANES), F32),
                   jax.ShapeDtypeStruct((SEQ, D_MODEL), F32),
                   jax.ShapeDtypeStruct((1, D_MODEL), F32)],
        compiler_params=_cparams(("arbitrary",)),
    )(h, g, target)


def _disc(ar, ai, ldt):
    dt = jnp.exp(ldt)
    mag = jnp.exp(ar * dt)
    ph = ai * dt
    lr, li = mag * jnp.cos(ph), mag * jnp.sin(ph)
    nr, ni = lr - 1.0, li
    den = ar * ar + ai * ai
    return lr, li, (nr * ar + ni * ai) / den, (ni * ar - nr * ai) / den


def _s5_disc(a_row, ldt_row, a_rep, ldt_rep, bt, ct, tile_e, mask):
    n = N_STATE

    def body(ar_ref, lr_ref, ap_ref, lp_ref, b_ref, c_ref, e_ref, m_ref, lt_ref, ltr_ref, bb_ref, cc_ref):
        lr, li, _, _ = _disc(ar_ref[0], ar_ref[1], lr_ref[...])
        pr, pi = lr, li
        for k in range(1, 9):
            for ref, sgn, edge in ((lt_ref, 1.0, 24 + k - 1), (ltr_ref, -1.0, 24 + 8 - k)):
                if k in (1, 2, 4):
                    r0 = {1: 0, 2: 8, 4: 16}[k]
                    ref[r0:r0 + 8, 0:n] = jnp.broadcast_to(pr, (8, n))
                    ref[r0:r0 + 8, n:2 * n] = jnp.broadcast_to(sgn * pi, (8, n))
                ref[edge:edge + 1, 0:n] = pr
                ref[edge:edge + 1, n:2 * n] = sgn * pi
            pr, pi = pr * lr - pi * li, pr * li + pi * lr
        _, _, fr, fi = _disc(ap_ref[0], ap_ref[1], lp_ref[...])
        br, bi = b_ref[0], b_ref[1]
        e = e_ref[...]
        m = m_ref[...].astype(F32)
        bb_ref[:, 0:n] = (_dot((fr * br - fi * bi).astype(BF16), e) * m).astype(BF16)
        bb_ref[:, n:2 * n] = (_dot((fr * bi + fi * br).astype(BF16), e) * m).astype(BF16)
        cc_ref[:, 0:n] = (_dot(c_ref[0].astype(BF16), e) * m).astype(BF16)
        cc_ref[:, n:2 * n] = (-(_dot(c_ref[1].astype(BF16), e) * m)).astype(BF16)

    return pl.pallas_call(
        body, name="s5_disc",
        out_shape=[jax.ShapeDtypeStruct((32, 2 * n), F32), jax.ShapeDtypeStruct((32, 2 * n), F32),
                   jax.ShapeDtypeStruct((SSM_W, 2 * n), BF16), jax.ShapeDtypeStruct((SSM_W, 2 * n), BF16)],
        compiler_params=_cparams(None),
    )(a_row, ldt_row, a_rep, ldt_rep, bt, ct, tile_e, mask)


def _dot_exact(x, sel):
    hi = x.astype(BF16)
    r1 = x - hi.astype(F32)
    mid = r1.astype(BF16)
    lo = (r1 - mid.astype(F32)).astype(BF16)
    return _dot(hi, sel) + _dot(mid, sel) + _dot(lo, sel)


def _s5_disc_bwd(a, ldt, a_rep, ldt_rep, bt, mask, dl, d_bb, d_cc, fold):
    n = N_STATE

    def body(a_ref, l_ref, ap_ref, lp_ref, b_ref, m_ref, dl_ref, dbb_ref, dcc_ref, f_ref,
             da_ref, dldt_ref, db_ref, dc_ref):
        m = m_ref[...].astype(F32)
        fold_m = f_ref[...]
        diag = lambda x: _dot_exact(x * m, fold_m)
        dr, di = diag(dbb_ref[:, 0:n]), diag(dbb_ref[:, n:2 * n])
        dc_ref[0] = diag(dcc_ref[:, 0:n])
        dc_ref[1] = -diag(dcc_ref[:, n:2 * n])
        _, _, fr, fi = _disc(ap_ref[0], ap_ref[1], lp_ref[...])
        br, bi = b_ref[0], b_ref[1]
        db_ref[0] = fr * dr + fi * di
        db_ref[1] = fr * di - fi * dr
        per_state = lambda x: x.reshape(SSM_GROUPS, SSM_GROUP, SSM_STATE).sum(axis=1)
        dfr = per_state(dr * br + di * bi)
        dfi = per_state(di * br - dr * bi)
        _, vjp = jax.vjp(_disc, a_ref[0], a_ref[1], l_ref[...])
        dar, dai, dldt = vjp((dl_ref[0], dl_ref[1], dfr, dfi))
        da_ref[0] = dar
        da_ref[1] = dai
        dldt_ref[...] = jnp.sum(dldt, axis=1, keepdims=True)

    return pl.pallas_call(
        body, name="s5_disc_bwd",
        out_shape=[jax.ShapeDtypeStruct((2, SSM_GROUPS, SSM_STATE), F32),
                   jax.ShapeDtypeStruct((SSM_GROUPS, 1), F32),
                   jax.ShapeDtypeStruct((2, SSM_W, SSM_STATE), F32),
                   jax.ShapeDtypeStruct((2, SSM_W, SSM_STATE), F32)],
        compiler_params=_cparams(None),
    )(a, ldt, a_rep, ldt_rep, bt, mask, dl, d_bb, d_cc, fold)


def _row_block(rows, cap=512):
    for bm in range(min(cap, rows), 0, -1):
        if rows % bm == 0 and (bm % 8 == 0 or bm == rows):
            return bm
    return rows


def _pair_sum(fulls, got, segs):
    ns = len(segs)
    offs = _seg_offsets(segs)
    _, rtot, c = got.shape
    parts = 2
    pr = rtot // parts
    assert pr * parts == rtot and pr % 16 == 0
    pieces = [[] for _ in range(parts)]
    for a, (n, r) in enumerate(segs):
        for m in range(n):
            lo = offs[a] + m * r
            for h in range(parts):
                clo, chi = max(lo, h * pr), min(lo + r, (h + 1) * pr)
                if chi > clo:
                    pieces[h].append((a, m, clo - lo, clo - h * pr, chi - clo))
    n_sems = max(len(ps) for ps in pieces)

    def body(*refs):
        srcs = refs[:ns]
        got_ref, p32_ref, pbf_ref, own_v, sems = refs[ns:]
        k = pl.program_id(0)
        h = pl.program_id(1)
        dev = 2 * k + lax.axis_index("c")
        for hh in range(parts):
            @pl.when(h == hh)
            def _(hh=hh):
                cps = []
                for i, (a, m, so, do, rows) in enumerate(pieces[hh]):
                    start = pl.multiple_of(dev * segs[a][1] + so, 16)
                    cps.append(pltpu.make_async_copy(srcs[a].at[m, pl.ds(start, rows), :],
                                                     own_v.at[pl.ds(do, rows), :], sems.at[i]))
                for cp in cps:
                    cp.start()
                for cp in cps:
                    cp.wait()
        s = own_v[...].astype(F32) + got_ref[0].astype(F32)
        p32_ref[0] = s
        pbf_ref[0] = s.astype(BF16)

    spec = pl.BlockSpec((1, pr, c), lambda k, h: (k, h, 0))
    return pl.pallas_call(
        body, name="pair_sum",
        grid=(4, parts),
        in_specs=[ANY] * ns + [spec], out_specs=[spec, spec],
        out_shape=[jax.ShapeDtypeStruct(got.shape, F32), jax.ShapeDtypeStruct(got.shape, BF16)],
        scratch_shapes=[pltpu.VMEM((pr, c), BF16), pltpu.SemaphoreType.DMA((n_sems,))],
        compiler_params=_cparams(("arbitrary", "arbitrary")),
    )(*fulls, got)


def _chip_sum(chip, p32, rb):
    _, r, c = p32.shape
    bm = _row_block(r)

    def body(chip_ref, o_ref, r_ref, s_ref):
        s_ref[...] = ((o_ref[0] + r_ref[0].astype(F32)) + r_ref[1].astype(F32)) + r_ref[2].astype(F32)

    return pl.pallas_call(
        body, name="chip_sum",
        grid_spec=pltpu.PrefetchScalarGridSpec(
            num_scalar_prefetch=1, grid=(r // bm,),
            in_specs=[pl.BlockSpec((1, bm, c), lambda k, chip_ref: (chip_ref[0], k, 0)),
                      pl.BlockSpec((3, bm, c), lambda k, chip_ref: (0, k, 0))],
            out_specs=pl.BlockSpec((bm, c), lambda k, chip_ref: (k, 0))),
        out_shape=jax.ShapeDtypeStruct((r, c), F32),
        compiler_params=_cparams(("parallel",)),
    )(chip, p32, rb)


def _sum8(x):
    _, r, c = x.shape
    bm = _row_block(r)

    def body(x_ref, s_ref):
        s = x_ref[0]
        for d in range(1, N_DEV):
            s = s + x_ref[d]
        s_ref[...] = s

    return pl.pallas_call(
        body, name="sum8",
        grid=(r // bm,),
        in_specs=[pl.BlockSpec((N_DEV, bm, c), lambda k: (0, k, 0))],
        out_specs=pl.BlockSpec((bm, c), lambda k: (k, 0)),
        out_shape=jax.ShapeDtypeStruct((r, c), F32),
        compiler_params=_cparams(("parallel",)),
    )(x)


def _adamw(w, g, m, v):
    r, c = w.shape
    bm = _row_block(r)
    bc1 = 1.0 - ADAM_B1 ** ADAM_STEP
    bc2 = 1.0 - ADAM_B2 ** ADAM_STEP

    def body(w_ref, g_ref, m_ref, v_ref, d_ref, nm_ref, nv_ref):
        gv = g_ref[...]
        nm = ADAM_B1 * m_ref[...] + (1.0 - ADAM_B1) * gv
        nv = ADAM_B2 * v_ref[...] + (1.0 - ADAM_B2) * (gv * gv)
        nm_ref[...] = nm
        nv_ref[...] = nv
        d_ref[...] = -ADAM_LR * ((nm / bc1) / (jnp.sqrt(nv / bc2) + ADAM_EPS) + ADAM_WD * w_ref[...])

    spec = pl.BlockSpec((bm, c), lambda k: (k, 0))
    shp = jax.ShapeDtypeStruct((r, c), F32)
    return pl.pallas_call(
        body, name="adamw",
        grid=(r // bm,),
        in_specs=[spec] * 4, out_specs=[spec] * 3, out_shape=[shp] * 3,
        compiler_params=_cparams(("parallel",)),
    )(w, g, m, v)


def _mesh_pos():
    return lax.axis_index("x"), lax.axis_index("y"), lax.axis_index("c")


def _dev_index(p):
    return 4 * p[0] + 2 * p[1] + p[2]


def _seg_offsets(segs):
    offs, o = [], 0
    for n, r in segs:
        offs.append(o)
        o += n * r
    return offs


def _remote(src, dst, send_sem, recv_sem, to):
    return pltpu.make_async_remote_copy(src_ref=src, dst_ref=dst, send_sem=send_sem, recv_sem=recv_sem,
                                        device_id=to, device_id_type=MESH)


def _allgather(pack, segs, name):
    rtot, c = pack.shape
    ns = len(segs)
    offs = _seg_offsets(segs)
    assert rtot == sum(n * r for n, r in segs)

    def body(pack_ref, *refs):
        outs = refs[:ns]
        send_sems, recv_sems, local_sem = refs[ns:]
        x, y, cc = _mesh_pos()
        me, sib = (x, y, cc), (x, y, 1 - cc)
        chips = [(1 - x, y), (x, 1 - y), (1 - x, 1 - y)]

        def pieces(dev, from_pack):
            res = []
            for a, (n, r) in enumerate(segs):
                for m in range(n):
                    dst = outs[a].at[m, pl.ds(pl.multiple_of(dev * r, r), r), :]
                    src = pack_ref.at[pl.ds(offs[a] + m * r, r), :] if from_pack else dst
                    res.append((src, dst))
            return res

        def push(k, dev, to, from_pack):
            for s, d in pieces(dev, from_pack):
                _remote(s, d, send_sems.at[k], recv_sems.at[k], to).start()

        def whole(k):
            return _remote(pack_ref, pack_ref, send_sems.at[k], recv_sems.at[k], me)

        my_dev = _dev_index(me)
        for s, d in pieces(my_dev, True):
            pltpu.make_async_copy(s, d, local_sem).start()
        push(0, my_dev, sib, True)
        for j, chip in enumerate(chips):
            push(1 + j, my_dev, (*chip, cc), True)
        for j, chip in enumerate(chips):
            whole(1 + j).wait_recv()
            push(4 + j, _dev_index((*chip, cc)), sib, False)
        whole(0).wait_recv()
        for j in range(3):
            whole(4 + j).wait_recv()
        for k in range(7):
            whole(k).wait_send()
        pltpu.make_async_copy(pack_ref, pack_ref, local_sem).wait()

    return pl.pallas_call(
        body, name=name,
        in_specs=[ANY], out_specs=[ANY] * ns,
        out_shape=[jax.ShapeDtypeStruct((n, N_DEV * r, c), pack.dtype) for n, r in segs],
        scratch_shapes=[pltpu.SemaphoreType.DMA((7,)), pltpu.SemaphoreType.DMA((7,)), pltpu.SemaphoreType.DMA],
    )(pack)


HBM = pl.BlockSpec(memory_space=pltpu.HBM)
SEM = pl.BlockSpec(memory_space=pltpu.SEMAPHORE)
VMEM_WHOLE = pl.BlockSpec(memory_space=pltpu.VMEM)
EFFECT = pltpu.SideEffectType.DATAFLOW_SIDE_EFFECTING


def _hbm(a):
    return pltpu.with_memory_space_constraint(a, pltpu.HBM)


def _ag_start(pack, segs, after, name):
    rtot, c = pack.shape
    ns = len(segs)
    offs = _seg_offsets(segs)

    def body(pack_ref, *refs):
        lands = refs[:ns]
        send_sems, recv_sems = refs[ns + 1], refs[ns + 2]
        token = refs[-1]
        x, y, cc = _mesh_pos()
        my_dev = _dev_index((x, y, cc))
        targets = [(x, y, 1 - cc), (1 - x, y, cc), (x, 1 - y, cc), (1 - x, 1 - y, cc)]
        for k, to in enumerate(targets):
            for a, (n, r) in enumerate(segs):
                for m in range(n):
                    _remote(pack_ref.at[pl.ds(offs[a] + m * r, r), :],
                            lands[a].at[m, pl.ds(pl.multiple_of(my_dev * r, r), r), :],
                            send_sems.at[k], recv_sems.at[k], to).start()
        token[...] = jnp.zeros_like(token)

    land_shapes = [(n, N_DEV * r, c) for n, r in segs]
    outs = pl.pallas_call(
        body, name=name,
        in_specs=[HBM] * (1 + ns) + [ANY],
        out_specs=[SEM, SEM, HBM] + [HBM] * ns + [VMEM_WHOLE],
        out_shape=[pltpu.SemaphoreType.DMA((4,)), pltpu.SemaphoreType.DMA((4,)), pltpu.HBM(pack.shape, pack.dtype)]
        + [pltpu.HBM(s, pack.dtype) for s in land_shapes] + [jax.ShapeDtypeStruct((SUBLANES, LANES), F32)],
        input_output_aliases={0: 2, **{1 + i: 3 + i for i in range(ns)}},
        compiler_params=pltpu.CompilerParams(has_side_effects=EFFECT),
    )(_hbm(pack), *[_hbm(lax.empty(s, pack.dtype)) for s in land_shapes], after)
    return outs[0], outs[1], outs[2], list(outs[3:3 + ns]), outs[-1]


def _ag_wait(send_sems, recv_sems, pack, lands, after, name):
    ns = len(lands)

    def body(pack_ref, *refs):
        send_ref, recv_ref = refs[ns], refs[ns + 1]
        me = _mesh_pos()
        for k in range(4):
            whole = _remote(pack_ref, pack_ref, send_ref.at[k], recv_ref.at[k], me)
            whole.wait_send()
            whole.wait_recv()

    outs = pl.pallas_call(
        body, name=name,
        in_specs=[HBM] * (1 + ns) + [SEM, SEM, ANY],
        out_specs=[HBM] * (1 + ns),
        out_shape=[pltpu.HBM(pack.shape, pack.dtype)] + [pltpu.HBM(a.shape, a.dtype) for a in lands],
        input_output_aliases={i: i for i in range(1 + ns)},
        compiler_params=pltpu.CompilerParams(has_side_effects=EFFECT),
    )(pack, *lands, send_sems, recv_sems, after)
    return outs[0], list(outs[1:])


def _ag_finish(pack, lands, segs):
    rtot, c = pack.shape
    ns = len(segs)
    offs = _seg_offsets(segs)

    def body(pack_ref, *refs):
        outs = refs[ns:2 * ns]
        stage, send_sems, recv_sems, local_sems = refs[2 * ns:]
        x, y, cc = _mesh_pos()
        me, sib = (x, y, cc), (x, y, 1 - cc)
        chips = [(1 - x, y), (x, 1 - y), (1 - x, 1 - y)]

        def rows(a, m, dev):
            return outs[a].at[m, pl.ds(pl.multiple_of(dev * segs[a][1], segs[a][1]), segs[a][1]), :]

        for j, chip in enumerate(chips):
            dev = _dev_index((*chip, cc))
            for a, (n, r) in enumerate(segs):
                for m in range(n):
                    _remote(rows(a, m, dev), rows(a, m, dev), send_sems.at[j], recv_sems.at[j], sib).start()
        load = pltpu.make_async_copy(pack_ref, stage, local_sems.at[0])
        load.start()
        load.wait()
        my_dev = _dev_index(me)
        for a, (n, r) in enumerate(segs):
            for m in range(n):
                pltpu.make_async_copy(stage.at[pl.ds(offs[a] + m * r, r), :], rows(a, m, my_dev), local_sems.at[1]).start()
        pltpu.make_async_copy(stage, pack_ref, local_sems.at[1]).wait()
        for j in range(3):
            _remote(pack_ref, pack_ref, send_sems.at[j], recv_sems.at[j], me).wait()

    outs = pl.pallas_call(
        body, name="ag_finish",
        in_specs=[ANY] * (1 + ns), out_specs=[ANY] * ns,
        out_shape=[jax.ShapeDtypeStruct(a.shape, a.dtype) for a in lands],
        input_output_aliases={1 + i: i for i in range(ns)},
        scratch_shapes=[pltpu.VMEM((rtot, c), pack.dtype), pltpu.SemaphoreType.DMA((3,)),
                        pltpu.SemaphoreType.DMA((3,)), pltpu.SemaphoreType.DMA((2,))],
        compiler_params=_cparams(None, 16),
    )(pack, *lands)
    return list(outs)


def _rs_chips_start(pbf, name):
    _, rtot, c = pbf.shape

    def body(pbf_ref, land_ref, send_sems, recv_sems, pbf_thru, land_thru, token):
        x, y, cc = _mesh_pos()
        for j, (cx, cy) in enumerate([(1 - x, y), (x, 1 - y), (1 - x, 1 - y)]):
            _remote(pbf_ref.at[2 * cx + cy], land_ref.at[j], send_sems.at[j], recv_sems.at[j], (cx, cy, cc)).start()
        token[...] = jnp.zeros_like(token)

    return pl.pallas_call(
        body, name=name,
        in_specs=[HBM, HBM],
        out_specs=[SEM, SEM, HBM, HBM, VMEM_WHOLE],
        out_shape=[pltpu.SemaphoreType.DMA((3,)), pltpu.SemaphoreType.DMA((3,)), pltpu.HBM(pbf.shape, pbf.dtype),
                   pltpu.HBM((3, rtot, c), pbf.dtype), jax.ShapeDtypeStruct((SUBLANES, LANES), F32)],
        input_output_aliases={0: 2, 1: 3},
        compiler_params=pltpu.CompilerParams(has_side_effects=EFFECT),
    )(_hbm(pbf), _hbm(lax.empty((3, rtot, c), pbf.dtype)))


def _rs_chips_wait(send_sems, recv_sems, pbf, land, after, name):
    def body(pbf_ref, land_ref, send_ref, recv_ref, after_ref, pbf_out, land_out):
        me = _mesh_pos()
        for j in range(3):
            cp = _remote(pbf_ref.at[0], land_ref.at[j], send_ref.at[j], recv_ref.at[j], me)
            cp.wait_send()
            cp.wait_recv()

    return pl.pallas_call(
        body, name=name,
        in_specs=[HBM, HBM, SEM, SEM, ANY], out_specs=[HBM, HBM],
        out_shape=[pltpu.HBM(pbf.shape, pbf.dtype), pltpu.HBM(land.shape, land.dtype)],
        input_output_aliases={0: 0, 1: 1},
        compiler_params=pltpu.CompilerParams(has_side_effects=EFFECT),
    )(pbf, land, send_sems, recv_sems, after)[1]


def _rs_sibling(fulls, segs):
    ns = len(segs)
    offs = _seg_offsets(segs)
    rtot = sum(n * r for n, r in segs)
    c = fulls[0].shape[-1]

    def body(*refs):
        srcs = refs[:ns]
        got_ref, send_sem, recv_sem = refs[ns:]
        x, y, cc = _mesh_pos()
        me, sib = (x, y, cc), (x, y, 1 - cc)
        for k in range(4):
            for a, (n, r) in enumerate(segs):
                for m in range(n):
                    theirs = srcs[a].at[m, pl.ds(pl.multiple_of((2 * k + 1 - cc) * r, r), r), :]
                    _remote(theirs, got_ref.at[k, pl.ds(offs[a] + m * r, r), :], send_sem, recv_sem, sib).start()
        _remote(got_ref, got_ref, send_sem, recv_sem, me).wait()

    return pl.pallas_call(
        body, name="rs_sibling",
        in_specs=[ANY] * ns, out_specs=ANY, out_shape=jax.ShapeDtypeStruct((4, rtot, c), fulls[0].dtype),
        scratch_shapes=[pltpu.SemaphoreType.DMA, pltpu.SemaphoreType.DMA],
    )(*fulls)


def _rs_chips(pbf):
    _, rtot, c = pbf.shape

    def body(pbf_ref, got_ref, send_sems, recv_sems):
        x, y, cc = _mesh_pos()
        chips = [(1 - x, y), (x, 1 - y), (1 - x, 1 - y)]
        cps = [_remote(pbf_ref.at[2 * cx + cy], got_ref.at[j], send_sems.at[j], recv_sems.at[j], (cx, cy, cc))
               for j, (cx, cy) in enumerate(chips)]
        for cp in cps:
            cp.start()
        for cp in cps:
            cp.wait()

    return pl.pallas_call(
        body, name="rs_chips",
        in_specs=[ANY], out_specs=ANY,
        out_shape=jax.ShapeDtypeStruct((3, rtot, c), BF16),
        scratch_shapes=[pltpu.SemaphoreType.DMA((3,)), pltpu.SemaphoreType.DMA((3,))],
    )(pbf)


def _tp(w):
    return jnp.swapaxes(w, -1, -2)


def _s5_prepare(a_re, a_im, log_dt, b_re, b_im, c_re, c_im):
    a = jnp.stack([a_re, a_im], axis=1)
    ldt = jnp.broadcast_to(log_dt[:, :, None], (DEPTH, SSM_GROUPS, SSM_STATE))
    a_row = a.reshape(DEPTH, 2, 1, N_STATE)
    ldt_row = ldt.reshape(DEPTH, 1, N_STATE)
    a_rep = jnp.repeat(a, SSM_GROUP, axis=2)
    ldt_rep = jnp.repeat(ldt, SSM_GROUP, axis=1)
    bt = jnp.stack([_tp(b_re), _tp(b_im)], axis=1).reshape(DEPTH, 2, SSM_W, SSM_STATE)
    ct = jnp.stack([c_re, c_im], axis=1).reshape(DEPTH, 2, SSM_W, SSM_STATE)
    tile_e = jnp.tile(jnp.eye(SSM_STATE, dtype=BF16), (1, SSM_GROUPS))
    mask = jnp.repeat(jnp.repeat(jnp.eye(SSM_GROUPS, dtype=BF16), SSM_GROUP, axis=0), SSM_STATE, axis=1)
    out = []
    for l in range(DEPTH):
        tabs = _s5_disc(a_row[l], ldt_row[l], a_rep[l], ldt_rep[l], bt[l], ct[l], tile_e, mask)
        out.append(((a[l], ldt[l], a_rep[l], ldt_rep[l], bt[l], mask), *tabs))
    return out


def _layer_fwd(h, p_l, small, big):
    saved = {'h0': h}
    h, saved['gu1'] = _ffn_fwd(h, small['ffn1_norm'], big['ff1'])
    saved['h1'] = h
    z = _inproj_fwd(h, small['mix_norm'], big['wint'])
    ya, ys, hs = _s5conv_fwd(z, small['conv_w'], small['conv_b'], small['bbmat'], small['ccmat'], small['dvec'],
                             small['ltab'])
    saved.update(z=z, ya=ya, ys=ys, hs=hs)
    h = _mix_out_fwd(h, ya, ys, big['glu'], small['glu_b'], small['conv_out_norm'], small['ssm_out_norm'], big['wout'])
    saved['h2'] = h
    h, saved['gu2'] = _ffn_fwd(h, small['ffn2_norm'], big['ff2'])
    saved['h3'] = h
    h = _ple_fwd(h, small['ple_norm'], p_l, big['plg'], big['plpt'])
    return h, saved


def _ffn_bwd(h_in, g, dh, gu, w3):
    dh_in, dga, ud, dg = _ffn_bwd_act(h_in, g, dh, gu, w3)
    return dh_in, _matmul_tn(dga, ud, FF_BLOCK, BF16, "ffn_wgrad"), dg


def _layer_bwd(dh, p_l, small, big, saved):
    gs = {}
    dh, u, dq, dpp, pb, gs['ple_norm'] = _ple_bwd(saved['h3'], small['ple_norm'], p_l, dh, big['plg'], big['plpt'])
    d_plg = _matmul_tn(u, dq, 256, BF16, "ple_gate_wgrad")
    d_plpt = _matmul_tn(dpp, pb, 256, BF16, "ple_proj_wgrad")
    dh, d_ff2, gs['ffn2_norm'] = _ffn_bwd(saved['h2'], small['ffn2_norm'], dh, saved['gu2'], big['ff2'])

    dya, dys, ycat, dhb, zg, dq, part = _mix_out_bwd(dh, saved['ya'], saved['ys'], big['glu'], small['glu_b'],
                                                     small['conv_out_norm'], small['ssm_out_norm'], big['wout'])
    d_wout = _matmul_tn(ycat, dhb, 256, BF16, "w_out_wgrad")
    d_glu = _matmul_tn(zg, dq, 256, BF16, "glu_wgrad")
    dz, gadj, us, dyb, dl, dcw = _s5conv_bwd(saved['z'], saved['hs'], dya, dys, small['conv_w'], small['conv_b'],
                                             small['bbmat'], small['ccmat'], small['dvec'], small['ltab_rev'])
    d_bb = _matmul_tn(us, gadj, SSM_W, F32, "s5_b_wgrad", bn=1024)[0]
    d_cc = _matmul_tn(dyb, saved['hs'][None], SSM_W, F32, "s5_c_wgrad", bn=1024)[0]
    dh, u, gs['mix_norm'] = _inproj_bwd(saved['h1'], small['mix_norm'], dh, dz, big['wint'])
    d_wint = _matmul_tn(dz[None], u, 256, BF16, "w_in_wgrad")
    dh, d_ff1, gs['ffn1_norm'] = _ffn_bwd(saved['h0'], small['ffn1_norm'], dh, saved['gu1'], big['ff1'])

    dlb = dl[0].reshape(2, SSM_GROUPS, SSM_STATE)
    fold = jnp.tile(jnp.eye(SSM_STATE, dtype=BF16), (SSM_GROUPS, 1))
    da, dldt, dbt, dct = _s5_disc_bwd(*small['disc_in'], dlb, d_bb, d_cc, fold)
    gs['ssm_A_re'], gs['ssm_A_im'] = da[0], da[1]
    gs['ssm_log_dt'] = dldt[:, 0]
    ghp = (SSM_GROUPS, SSM_GROUP, SSM_STATE)
    gs['ssm_B_re'], gs['ssm_B_im'] = dbt[0].reshape(ghp), dbt[1].reshape(ghp)
    gs['ssm_C_re'], gs['ssm_C_im'] = dct[0].reshape(ghp), dct[1].reshape(ghp)
    gs['conv_w'] = dcw[0:3]
    gs['conv_b'] = dcw[3]
    gs['ssm_D'] = dcw[4].reshape(SSM_GROUPS, SSM_GROUP)
    gs['conv_out_norm'], gs['ssm_out_norm'], gs['glu_b'] = part[0], part[1], part[2]
    for n in ('ple_norm', 'ffn2_norm', 'mix_norm', 'ffn1_norm'):
        gs[n] = gs[n][0]
    fulls = [d_ff1, d_ff2, d_wint, d_wout, d_plg,
             d_plpt.reshape(1, D_MODEL * PLE_DIM // D_MODEL, D_MODEL), d_glu.reshape(1, SSM_W * SSM_W // D_MODEL, D_MODEL)]
    return dh, fulls, gs


VIEW_T = ('ffn1_w_gate', 'ffn1_w_up', 'ffn2_w_gate', 'ffn2_w_up', 'ssm_B_re', 'ssm_B_im')


def _view(name, a):
    return _tp(a) if name in VIEW_T else a


def _layer_pack(W, l):
    return jnp.concatenate([
        _tp(W['ffn1_w_gate'][l]), _tp(W['ffn1_w_up'][l]), W['ffn1_w_down'][l],
        _tp(W['ffn2_w_gate'][l]), _tp(W['ffn2_w_up'][l]), W['ffn2_w_down'][l],
        _tp(W['w_in'][l]), W['w_out'][l], W['ple_w_gate'][l],
        _tp(W['ple_w_proj'][l]).reshape(-1, D_MODEL), W['glu_w'][l].reshape(-1, D_MODEL)], axis=0).astype(BF16)


def _pad_rows(flat, mult):
    per = mult * LANES
    n = flat.shape[0]
    tot = -(-n // per) * per
    return jnp.pad(flat, (0, tot - n)).reshape(tot // LANES, LANES)


def _adamw_any(w, g, m, v):
    shp = w.shape
    two = (lambda t: t.reshape(-1, shp[-1]))
    d, nm, nv = _adamw(two(w), two(g), two(m), two(v))
    return d.reshape(shp), nm.reshape(shp), nv.reshape(shp)


def kernel(x, p, ffn1_norm, ffn1_w_gate, ffn1_w_up, ffn1_w_down, mix_norm, w_in, conv_w, conv_b, ssm_A_re, ssm_A_im, ssm_B_re, ssm_B_im, ssm_C_re, ssm_C_im, ssm_D, ssm_log_dt, glu_w, glu_b, conv_out_norm, ssm_out_norm, w_out, ffn2_norm, ffn2_w_gate, ffn2_w_up, ffn2_w_down, ple_norm, ple_w_gate, ple_w_proj, final_norm, loss_target, m_ffn1_norm, m_ffn1_w_gate, m_ffn1_w_up, m_ffn1_w_down, m_mix_norm, m_w_in, m_conv_w, m_conv_b, m_ssm_A_re, m_ssm_A_im, m_ssm_B_re, m_ssm_B_im, m_ssm_C_re, m_ssm_C_im, m_ssm_D, m_ssm_log_dt, m_glu_w, m_glu_b, m_conv_out_norm, m_ssm_out_norm, m_w_out, m_ffn2_norm, m_ffn2_w_gate, m_ffn2_w_up, m_ffn2_w_down, m_ple_norm, m_ple_w_gate, m_ple_w_proj, m_final_norm, v_ffn1_norm, v_ffn1_w_gate, v_ffn1_w_up, v_ffn1_w_down, v_mix_norm, v_w_in, v_conv_w, v_conv_b, v_ssm_A_re, v_ssm_A_im, v_ssm_B_re, v_ssm_B_im, v_ssm_C_re, v_ssm_C_im, v_ssm_D, v_ssm_log_dt, v_glu_w, v_glu_b, v_conv_out_norm, v_ssm_out_norm, v_w_out, v_ffn2_norm, v_ffn2_w_gate, v_ffn2_w_up, v_ffn2_w_down, v_ple_norm, v_ple_w_gate, v_ple_w_proj, v_final_norm):
    given = dict(locals())
    W = {n: given[n] for n in W_NAMES}
    M = {n: given['m_' + n] for n in W_NAMES}
    V = {n: given['v_' + n] for n in W_NAMES}
    Wv, Mv, Vv = [{n: _view(n, d[n]) for n in W_NAMES} for d in (W, M, V)]
    my_dev = _dev_index(_mesh_pos())
    my_chip = (my_dev // 2).astype(jnp.int32).reshape(1)

    conv_shard = _pad_rows(W['conv_w'].reshape(-1), SUBLANES)
    conv_all = _allgather(conv_shard, ((1, SUBLANES),), "ag_conv_w")[0]
    conv_full = conv_all.reshape(N_DEV, -1)[:, :DEPTH * 3 * (CONV_W // N_DEV)]
    conv_full = conv_full.reshape(N_DEV, DEPTH, 3, CONV_W // N_DEV).transpose(1, 2, 0, 3).reshape(DEPTH, 3, CONV_W)
    packs = [_layer_pack(W, 0)]
    flight = _ag_start(packs[0], SEGS, conv_all, "ag_start_0")
    s5 = _s5_prepare(*[W[n] + flight[4][0, 0] for n in ('ssm_A_re', 'ssm_A_im', 'ssm_log_dt')],
                     *[W[n] for n in ('ssm_B_re', 'ssm_B_im', 'ssm_C_re', 'ssm_C_im')])

    packs += [_layer_pack(W, l) for l in range(1, DEPTH)]
    prepared = conv_full[0, 0:1, 0:1] + s5[DEPTH - 1][1][0:1, 0:1] + packs[DEPTH - 1][0:1, 0:1].astype(F32)

    smalls, saves, bigs = [], [], []
    h = x[0]
    for l in range(DEPTH):
        send_sems, recv_sems, pack_thru, lands, _ = flight
        pack_thru, lands = _ag_wait(send_sems, recv_sems, pack_thru, lands, prepared if l == 0 else h,
                                    "ag_wait_%d" % l)
        token = jnp.zeros((1, 1), F32)
        if l + 1 < DEPTH:
            flight = _ag_start(packs[l + 1], SEGS, lands[0], "ag_start_%d" % (l + 1))
            token = flight[4][0:1, 0:1]
        ff1, ff2, wint, wout, plg, plpt, glu = _ag_finish(pack_thru, lands, SEGS)
        bigs.append(dict(ff1=ff1, ff2=ff2, wint=wint[0], wout=wout[0], plg=plg[0],
                         plpt=plpt.reshape(D_MODEL, PLE_DIM), glu=glu.reshape(SSM_W, SSM_W)))
        small = {n: W[n][l][None] for n in ('ffn1_norm', 'mix_norm', 'conv_b', 'glu_b', 'conv_out_norm',
                                            'ssm_out_norm', 'ffn2_norm', 'ple_norm')}
        small['ffn1_norm'] = small['ffn1_norm'] + token
        small['conv_w'] = conv_full[l]
        small['dvec'] = W['ssm_D'][l].reshape(1, SSM_W)
        small['disc_in'], small['ltab'], small['ltab_rev'], small['bbmat'], small['ccmat'] = s5[l]
        h, saved = _layer_fwd(h, p[l, 0], small, bigs[l])
        smalls.append(small)
        saves.append(saved)
    loss_tile, dh, d_final = _final_loss(h, W['final_norm'][None], loss_target[0])
    loss = lax.psum(loss_tile[0, 0], ("x", "y", "c"))

    layer_gs = [None] * DEPTH
    shard_grads = [None] * DEPTH
    flight = None
    for l in reversed(range(DEPTH)):
        small = dict(smalls[l])
        if flight is not None:
            small['ple_norm'] = small['ple_norm'] + flight[1][4][0:1, 0:1]
        dh, fulls, layer_gs[l] = _layer_bwd(dh, p[l, 0], small, bigs[l], saves[l])
        p32, pbf = _pair_sum(fulls, _rs_sibling(fulls, SEGS), SEGS)
        if flight is not None:
            up, (send_sems, recv_sems, pbf_thru, land, _), p32_up = flight
            got3 = _rs_chips_wait(send_sems, recv_sems, pbf_thru, land, dh, "rs_wait_%d" % up)
            shard_grads[up] = _chip_sum(my_chip, p32_up, got3)
        flight = (l, _rs_chips_start(pbf, "rs_start_%d" % l), p32)
    grad_x = dh[None]

    gs = {n: jnp.stack([layer_gs[l][n] for l in range(DEPTH)]) for n in layer_gs[0]}
    gs['final_norm'] = d_final[0]
    flat = jnp.concatenate([gs[n].reshape(-1) for n in SMALL_NAMES] + [gs['conv_w'].reshape(-1)])
    n_flat = flat.shape[0]
    flat = _pad_rows(flat, SUBLANES) + flight[1][4][0:1, 0:1]
    rows = flat.shape[0]
    gathered = _allgather(flat, ((1, rows),), "ag_small_grads")[0]
    red = _sum8(gathered.reshape(N_DEV, rows, LANES)).reshape(-1)[:n_flat]
    up, (send_sems, recv_sems, pbf_thru, land, _), p32_up = flight
    shard_grads[up] = _chip_sum(my_chip, p32_up,
                                _rs_chips_wait(send_sems, recv_sems, pbf_thru, land, red, "rs_wait_%d" % up))
    G = {}
    o = 0
    for n in SMALL_NAMES:
        G[n] = red[o:o + W[n].size].reshape(Wv[n].shape)
        o += W[n].size
    conv_g_full = red[o:].reshape(DEPTH, 3, CONV_W)
    G['conv_w'] = lax.dynamic_slice_in_dim(conv_g_full, my_dev * (CONV_W // N_DEV), CONV_W // N_DEV, axis=2)

    sg = jnp.stack(shard_grads)
    offs = _seg_offsets(SEGS)
    r = SEGS[0][1]
    for a, f in ((0, 'ffn1'), (1, 'ffn2')):
        G[f + '_w_gate'] = sg[:, offs[a]:offs[a] + r]
        G[f + '_w_up'] = sg[:, offs[a] + r:offs[a] + 2 * r]
        G[f + '_w_down'] = sg[:, offs[a] + 2 * r:offs[a] + 3 * r]
    G['w_in'] = _tp(sg[:, offs[2]:offs[2] + SEGS[2][1]])
    G['w_out'] = sg[:, offs[3]:offs[3] + SEGS[3][1]]
    G['ple_w_gate'] = sg[:, offs[4]:offs[4] + SEGS[4][1]]
    G['ple_w_proj'] = _tp(sg[:, offs[5]:offs[5] + SEGS[5][1]].reshape(DEPTH, D_MODEL // N_DEV, PLE_DIM))
    G['glu_w'] = sg[:, offs[6]:offs[6] + SEGS[6][1]].reshape(DEPTH, SSM_W // N_DEV, SSM_W)

    delta, new_m, new_v = {}, {}, {}
    cat = lambda src: _pad_rows(jnp.concatenate([src[n].reshape(-1) for n in SMALL_NAMES]), SUBLANES)
    d_s, m_s, v_s = _adamw(cat(Wv), cat(G), cat(Mv), cat(Vv))
    o = 0
    for n in SMALL_NAMES:
        for dst, src in ((delta, d_s), (new_m, m_s), (new_v, v_s)):
            dst[n] = src.reshape(-1)[o:o + W[n].size].reshape(Wv[n].shape)
        o += W[n].size
    for n in W_NAMES:
        if n not in delta:
            delta[n], new_m[n], new_v[n] = _adamw_any(Wv[n], G[n], Mv[n], Vv[n])

    outs = [[_view(n, d[n]) for n in W_NAMES] for d in (G, delta, new_m, new_v)]
    return (loss, grad_x, *outs[0], *outs[1], *outs[2], *outs[3])
```

```python
import math

import jax
import jax.numpy as jnp
from jax import lax
from jax.experimental import pallas as pl
from jax.experimental.pallas import tpu as pltpu

F32 = jnp.float32
BF16 = jnp.bfloat16

N_DEV = 8
DEPTH = 4
SEQ = 2048
D_MODEL = 1024
D_FF = 2816
CONV_W = 512
SSM_W = 512
SSM_GROUPS = 32
SSM_GROUP = 16
SSM_STATE = 64
N_STATE = SSM_GROUPS * SSM_STATE
IN_COLS = 2048
PLE_DIM = 256
EPS = 1e-6

ADAM_LR = 0.001
ADAM_B1 = 0.9
ADAM_B2 = 0.999
ADAM_EPS = 1e-08
ADAM_WD = 0.01
ADAM_STEP = 10

FF_BLOCK = 256
N_FF_BLOCKS = D_FF // FF_BLOCK
TOK_TILE_FFN = 1024
TOK_TILE = 512
CHUNK = 256
N_CHUNKS = SEQ // CHUNK
LANE_GROUP = 512
SUBLANES = 8
LANES = 128
MIB = 1024 * 1024

W_NAMES = ['ffn1_norm', 'ffn1_w_gate', 'ffn1_w_up', 'ffn1_w_down', 'mix_norm', 'w_in', 'conv_w', 'conv_b',
           'ssm_A_re', 'ssm_A_im', 'ssm_B_re', 'ssm_B_im', 'ssm_C_re', 'ssm_C_im', 'ssm_D', 'ssm_log_dt',
           'glu_w', 'glu_b', 'conv_out_norm', 'ssm_out_norm', 'w_out', 'ffn2_norm', 'ffn2_w_gate', 'ffn2_w_up',
           'ffn2_w_down', 'ple_norm', 'ple_w_gate', 'ple_w_proj', 'final_norm']
SMALL_NAMES = ['ffn1_norm', 'mix_norm', 'conv_b', 'ssm_A_re', 'ssm_A_im', 'ssm_B_re', 'ssm_B_im', 'ssm_C_re',
               'ssm_C_im', 'ssm_D', 'ssm_log_dt', 'glu_b', 'conv_out_norm', 'ssm_out_norm', 'ffn2_norm',
               'ple_norm', 'final_norm']

SEGS = ((3, 352), (3, 352), (1, 256), (1, 128), (1, 128), (1, 32), (1, 32))
PACK_ROWS = sum(n * r for n, r in SEGS)

MESH = pl.DeviceIdType.MESH
ANY = pl.BlockSpec(memory_space=pl.ANY)


def _cparams(sem=None, vmem_mib=48, **kw):
    return pltpu.CompilerParams(dimension_semantics=sem, vmem_limit_bytes=vmem_mib * MIB, **kw)


def _dot(a, b):
    return jnp.dot(a, b, preferred_element_type=F32)


def _dot_nt(a, b):
    return lax.dot_general(a, b, (((1,), (1,)), ((), ())), preferred_element_type=F32)


def _dot_tn(a, b):
    return lax.dot_general(a, b, (((0,), (0,)), ((), ())), preferred_element_type=F32)


def _rms_stats(x):
    r = lax.rsqrt(jnp.mean(x * x, axis=-1, keepdims=True) + EPS)
    return x * r, r


def _rms_bwd(dy, xh, r, g):
    dxh = dy * g
    dx = r * (dxh - xh * jnp.mean(dxh * xh, axis=-1, keepdims=True))
    dg = jnp.sum(dy * xh, axis=0, keepdims=True)
    return dx, dg


def _sigmoid(x):
    return 0.5 * jnp.tanh(0.5 * x) + 0.5


_GELU_C = math.sqrt(2.0 / math.pi)


def _gelu(x):
    t = jnp.tanh(_GELU_C * (x + 0.044715 * x * x * x))
    return 0.5 * x * (1.0 + t), t


def _gelu_grad(x, t):
    return 0.5 * (1.0 + t) + 0.5 * x * (1.0 - t * t) * _GELU_C * (1.0 + 3.0 * 0.044715 * x * x)


def _accumulate(ref, first, value):
    @pl.when(first)
    def _():
        ref[...] = value

    @pl.when(jnp.logical_not(first))
    def _():
        ref[...] += value


def _ffn_fwd(h, g, w3):
    tm = TOK_TILE_FFN
    last = N_FF_BLOCKS - 1

    def body(h_ref, g_ref, wgu_ref, wd_ref, wd_last_ref, out_ref, gu_ref, u_ref, a_ref):
        k = pl.program_id(1)

        @pl.when(k == 0)
        def _():
            x = h_ref[...]
            xh, _ = _rms_stats(x)
            u_ref[...] = (xh * g_ref[...]).astype(BF16)
            out_ref[...] = x
            a_ref[1] = jnp.zeros((tm, FF_BLOCK), BF16)

        out_ref[...] += 0.5 * _dot(a_ref[(k + 1) % 2], wd_ref[0])
        gu = _dot_nt(u_ref[...], wgu_ref[...].reshape(2 * FF_BLOCK, D_MODEL))
        gate, up = gu[:, :FF_BLOCK], gu[:, FF_BLOCK:]
        a_ref[k % 2] = (gate * _sigmoid(gate) * up).astype(BF16)
        gu_ref[0] = gate.astype(BF16)
        gu_ref[1] = up.astype(BF16)

        @pl.when(k == last)
        def _():
            out_ref[...] += 0.5 * _dot(a_ref[last % 2], wd_last_ref[0])

    return pl.pallas_call(
        body, name="ffn_fwd",
        grid=(SEQ // tm, N_FF_BLOCKS),
        in_specs=[pl.BlockSpec((tm, D_MODEL), lambda m, k: (m, 0)),
                  pl.BlockSpec((1, D_MODEL), lambda m, k: (0, 0)),
                  pl.BlockSpec((2, FF_BLOCK, D_MODEL), lambda m, k: (0, k, 0)),
                  pl.BlockSpec((1, FF_BLOCK, D_MODEL), lambda m, k: (2, jnp.maximum(k - 1, 0), 0)),
                  pl.BlockSpec((1, FF_BLOCK, D_MODEL), lambda m, k: (2, last, 0))],
        out_specs=[pl.BlockSpec((tm, D_MODEL), lambda m, k: (m, 0)),
                   pl.BlockSpec((2, tm, FF_BLOCK), lambda m, k: (0, m, k))],
        out_shape=[jax.ShapeDtypeStruct((SEQ, D_MODEL), F32),
                   jax.ShapeDtypeStruct((2, SEQ, D_FF), BF16)],
        scratch_shapes=[pltpu.VMEM((tm, D_MODEL), BF16), pltpu.VMEM((2, tm, FF_BLOCK), BF16)],
        compiler_params=_cparams(("parallel", "arbitrary")),
    )(h, g, w3, w3, w3)


def _ffn_bwd_act(h, g, dout, gu, w3):
    tm = TOK_TILE
    last = N_FF_BLOCKS - 1

    def body(h_ref, g_ref, d_ref, gu_ref, wd_ref, wgu_ref, wgu_last_ref, dh_ref, dga_ref, ud_ref, dg_ref,
             acc_ref, dgu_ref):
        m = pl.program_id(0)
        k = pl.program_id(1)

        @pl.when(k == 0)
        def _():
            xh, _ = _rms_stats(h_ref[...])
            ud_ref[0] = (xh * g_ref[...]).astype(BF16)
            ud_ref[1] = (0.5 * d_ref[...]).astype(BF16)
            acc_ref[...] = jnp.zeros_like(acc_ref)
            dgu_ref[1] = jnp.zeros((tm, 2 * FF_BLOCK), BF16)

        acc_ref[...] += _dot(dgu_ref[(k + 1) % 2], wgu_ref[...].reshape(2 * FF_BLOCK, D_MODEL))
        gate = gu_ref[0].astype(F32)
        up = gu_ref[1].astype(F32)
        sg = _sigmoid(gate)
        silu = gate * sg
        da = _dot_nt(ud_ref[1], wd_ref[0])
        dgate = (da * up * (sg + silu * (1.0 - sg))).astype(BF16)
        dup = (da * silu).astype(BF16)
        dga_ref[0] = dgate
        dga_ref[1] = dup
        dga_ref[2] = (silu * up).astype(BF16)
        dgu_ref[k % 2, :, 0:FF_BLOCK] = dgate
        dgu_ref[k % 2, :, FF_BLOCK:2 * FF_BLOCK] = dup

        @pl.when(k == last)
        def _():
            du = acc_ref[...] + _dot(dgu_ref[last % 2], wgu_last_ref[...].reshape(2 * FF_BLOCK, D_MODEL))
            xh, r = _rms_stats(h_ref[...])
            dx, dg = _rms_bwd(du, xh, r, g_ref[...])
            dh_ref[...] = d_ref[...] + dx
            _accumulate(dg_ref, m == 0, dg)

    return pl.pallas_call(
        body, name="ffn_bwd_act",
        grid=(SEQ // tm, N_FF_BLOCKS),
        in_specs=[pl.BlockSpec((tm, D_MODEL), lambda m, k: (m, 0)),
                  pl.BlockSpec((1, D_MODEL), lambda m, k: (0, 0)),
                  pl.BlockSpec((tm, D_MODEL), lambda m, k: (m, 0)),
                  pl.BlockSpec((2, tm, FF_BLOCK), lambda m, k: (0, m, k)),
                  pl.BlockSpec((1, FF_BLOCK, D_MODEL), lambda m, k: (2, k, 0)),
                  pl.BlockSpec((2, FF_BLOCK, D_MODEL), lambda m, k: (0, jnp.maximum(k - 1, 0), 0)),
                  pl.BlockSpec((2, FF_BLOCK, D_MODEL), lambda m, k: (0, last, 0))],
        out_specs=[pl.BlockSpec((tm, D_MODEL), lambda m, k: (m, 0)),
                   pl.BlockSpec((3, tm, FF_BLOCK), lambda m, k: (0, m, k)),
                   pl.BlockSpec((2, tm, D_MODEL), lambda m, k: (0, m, 0)),
                   pl.BlockSpec((1, D_MODEL), lambda m, k: (0, 0))],
        out_shape=[jax.ShapeDtypeStruct((SEQ, D_MODEL), F32),
                   jax.ShapeDtypeStruct((3, SEQ, D_FF), BF16),
                   jax.ShapeDtypeStruct((2, SEQ, D_MODEL), BF16),
                   jax.ShapeDtypeStruct((1, D_MODEL), F32)],
        scratch_shapes=[pltpu.VMEM((tm, D_MODEL), F32), pltpu.VMEM((2, tm, 2 * FF_BLOCK), BF16)],
        compiler_params=_cparams(("arbitrary", "arbitrary")),
    )(h, g, dout, gu, w3, w3, w3)


def _matmul_tn(a, b, bm, out_dtype, name, bn=None):
    na, t, m = a.shape
    nb, _, n = b.shape
    bn = n if bn is None else bn

    def body(a_ref, b_ref, o_ref):
        o_ref[0] = _dot_tn(a_ref[0], b_ref[0]).astype(out_dtype)

    return pl.pallas_call(
        body, name=name,
        grid=(na, m // bm, n // bn),
        in_specs=[pl.BlockSpec((1, t, bm), lambda i, k, j: (i, 0, k)),
                  pl.BlockSpec((1, t, bn), lambda i, k, j: (jnp.maximum(i - (na - nb), 0), 0, j))],
        out_specs=pl.BlockSpec((1, bm, bn), lambda i, k, j: (i, k, j)),
        out_shape=jax.ShapeDtypeStruct((na, m, n), out_dtype),
        compiler_params=_cparams(("arbitrary", "parallel", "parallel")),
    )(a, b)


def _inproj_fwd(h, g, wint):
    tm = TOK_TILE

    def body(h_ref, g_ref, w_ref, z_ref):
        xh, _ = _rms_stats(h_ref[...])
        z_ref[...] = _dot_nt((xh * g_ref[...]).astype(BF16), w_ref[...])

    return pl.pallas_call(
        body, name="inproj_fwd",
        grid=(SEQ // tm,),
        in_specs=[pl.BlockSpec((tm, D_MODEL), lambda m: (m, 0)),
                  pl.BlockSpec((1, D_MODEL), lambda m: (0, 0)),
                  pl.BlockSpec((IN_COLS, D_MODEL), lambda m: (0, 0))],
        out_specs=pl.BlockSpec((tm, IN_COLS), lambda m: (m, 0)),
        out_shape=jax.ShapeDtypeStruct((SEQ, IN_COLS), F32),
        compiler_params=_cparams(("parallel",)),
    )(h, g, wint)


def _inproj_bwd(h, g, dh, dz, wint):
    tm = TOK_TILE

    def body(h_ref, g_ref, dh_ref, dz_ref, w_ref, o_ref, u_ref, dg_ref):
        xh, r = _rms_stats(h_ref[...])
        u_ref[0] = (xh * g_ref[...]).astype(BF16)
        dx, dg = _rms_bwd(_dot(dz_ref[...], w_ref[...]), xh, r, g_ref[...])
        o_ref[...] = dh_ref[...] + dx
        _accumulate(dg_ref, pl.program_id(0) == 0, dg)

    return pl.pallas_call(
        body, name="inproj_bwd",
        grid=(SEQ // tm,),
        in_specs=[pl.BlockSpec((tm, D_MODEL), lambda m: (m, 0)),
                  pl.BlockSpec((1, D_MODEL), lambda m: (0, 0)),
                  pl.BlockSpec((tm, D_MODEL), lambda m: (m, 0)),
                  pl.BlockSpec((tm, IN_COLS), lambda m: (m, 0)),
                  pl.BlockSpec((IN_COLS, D_MODEL), lambda m: (0, 0))],
        out_specs=[pl.BlockSpec((tm, D_MODEL), lambda m: (m, 0)),
                   pl.BlockSpec((1, tm, D_MODEL), lambda m: (0, m, 0)),
                   pl.BlockSpec((1, D_MODEL), lambda m: (0, 0))],
        out_shape=[jax.ShapeDtypeStruct((SEQ, D_MODEL), F32),
                   jax.ShapeDtypeStruct((1, SEQ, D_MODEL), BF16),
                   jax.ShapeDtypeStruct((1, D_MODEL), F32)],
        compiler_params=_cparams(("arbitrary",)),
    )(h, g, dh, dz, wint)


def _row_ids(n, w):
    return lax.broadcasted_iota(jnp.int32, (n, w), 0)


def _bcast_row(x, i, n):
    return jnp.broadcast_to(x[i:i + 1, :], (n, x.shape[1]))


def _conv_taps(v, tail):
    n, w = v.shape
    rid = _row_ids(n, w)
    v1 = jnp.where(rid == 0, _bcast_row(tail, 7, n), pltpu.roll(v, 1, 0))
    v2 = jnp.where(rid == 0, _bcast_row(tail, 6, n),
                   jnp.where(rid == 1, _bcast_row(tail, 7, n), pltpu.roll(v, 2, 0)))
    return v1, v2


def _scan_chunk(work, ltab, carry, reverse):
    nblk = CHUNK // SUBLANES
    row = _row_ids(SUBLANES, LANE_GROUP)
    for gi in range(N_STATE // LANE_GROUP):
        cre = pl.ds(gi * LANE_GROUP, LANE_GROUP)
        cim = pl.ds(N_STATE + gi * LANE_GROUP, LANE_GROUP)
        pows = [(ltab[8 * k:8 * k + 8, cre], ltab[8 * k:8 * k + 8, cim]) for k in range(3)]
        pr = ltab[24:32, cre]
        pi = ltab[24:32, cim]

        def blk(i, c, cre=cre, cim=cim, pows=pows, pr=pr, pi=pi):
            cr, ci = c
            b = (nblk - 1 - i) if reverse else i
            r0 = pl.multiple_of(b * SUBLANES, SUBLANES)
            xr = work[pl.ds(r0, SUBLANES), cre]
            xi = work[pl.ds(r0, SUBLANES), cim]
            for k, s in enumerate((1, 2, 4)):
                lr, li = pows[k]
                if reverse:
                    keep = row < SUBLANES - s
                    sr = jnp.where(keep, pltpu.roll(xr, SUBLANES - s, 0), 0.0)
                    si = jnp.where(keep, pltpu.roll(xi, SUBLANES - s, 0), 0.0)
                else:
                    keep = row >= s
                    sr = jnp.where(keep, pltpu.roll(xr, s, 0), 0.0)
                    si = jnp.where(keep, pltpu.roll(xi, s, 0), 0.0)
                xr, xi = xr + lr * sr - li * si, xi + lr * si + li * sr
            xr, xi = xr + pr * cr - pi * ci, xi + pr * ci + pi * cr
            work[pl.ds(r0, SUBLANES), cre] = xr
            work[pl.ds(r0, SUBLANES), cim] = xi
            edge = 0 if reverse else SUBLANES - 1
            return _bcast_row(xr, edge, SUBLANES), _bcast_row(xi, edge, SUBLANES)

        cr, ci = lax.fori_loop(0, nblk, blk, (carry[:, cre], carry[:, cim]))
        carry[:, cre] = cr
        carry[:, cim] = ci


def _s5conv_fwd(z, convw, convb, bbmat, ccmat, dvec, ltab):
    def body(z_ref, cw_ref, cb_ref, bb_ref, cc_ref, d_ref, lt_ref, ya_ref, ys_ref, hs_ref,
             work, carry, tail):
        c = pl.program_id(0)

        @pl.when(c == 0)
        def _():
            carry[...] = jnp.zeros_like(carry)
            tail[...] = jnp.zeros_like(tail)

        zb = z_ref[:, 0:CONV_W]
        v = z_ref[:, CONV_W:2 * CONV_W] * z_ref[:, 2 * CONV_W:3 * CONV_W]
        us = z_ref[:, 3 * CONV_W:4 * CONV_W]
        v1, v2 = _conv_taps(v, tail[...])
        tail[...] = v[CHUNK - 8:CHUNK, :]
        y = cw_ref[0:1, :] * v2 + cw_ref[1:2, :] * v1 + cw_ref[2:3, :] * v
        ya_ref[...] = zb * (y + cb_ref[...])

        work[...] = _dot(us.astype(BF16), bb_ref[...])
        _scan_chunk(work, lt_ref, carry, reverse=False)
        hs = work[...].astype(BF16)
        hs_ref[...] = hs
        ys_ref[...] = _dot_nt(hs, cc_ref[...]) + d_ref[...] * us

    return pl.pallas_call(
        body, name="s5conv_fwd",
        grid=(N_CHUNKS,),
        in_specs=[pl.BlockSpec((CHUNK, IN_COLS), lambda c: (c, 0)),
                  pl.BlockSpec((3, CONV_W), lambda c: (0, 0)),
                  pl.BlockSpec((1, CONV_W), lambda c: (0, 0)),
                  pl.BlockSpec((SSM_W, 2 * N_STATE), lambda c: (0, 0)),
                  pl.BlockSpec((SSM_W, 2 * N_STATE), lambda c: (0, 0)),
                  pl.BlockSpec((1, SSM_W), lambda c: (0, 0)),
                  pl.BlockSpec((32, 2 * N_STATE), lambda c: (0, 0))],
        out_specs=[pl.BlockSpec((CHUNK, CONV_W), lambda c: (c, 0)),
                   pl.BlockSpec((CHUNK, SSM_W), lambda c: (c, 0)),
                   pl.BlockSpec((CHUNK, 2 * N_STATE), lambda c: (c, 0))],
        out_shape=[jax.ShapeDtypeStruct((SEQ, CONV_W), F32),
                   jax.ShapeDtypeStruct((SEQ, SSM_W), F32),
                   jax.ShapeDtypeStruct((SEQ, 2 * N_STATE), BF16)],
        scratch_shapes=[pltpu.VMEM((CHUNK, 2 * N_STATE), F32),
                        pltpu.VMEM((8, 2 * N_STATE), F32),
                        pltpu.VMEM((8, CONV_W), F32)],
        compiler_params=_cparams(("arbitrary",)),
    )(z, convw, convb, bbmat, ccmat, dvec, ltab)


def _s5conv_bwd(z, hs, dya, dys, convw, convb, bbmat, ccmat, dvec, ltab_rev):
    nc = N_CHUNKS
    hb = 16

    def body(z_ref, zp_ref, hs_ref, hp_ref, dya_ref, dys_ref, cw_ref, cb_ref, bb_ref, cc_ref, d_ref, lt_ref,
             dz_ref, g_ref, us_ref, dyb_ref, dl_ref, dcw_ref, work, carry, head):
        i = pl.program_id(0)
        first_chunk = i == nc - 1

        @pl.when(i == 0)
        def _():
            carry[...] = jnp.zeros_like(carry)
            head[...] = jnp.zeros_like(head)
            dl_ref[...] = jnp.zeros_like(dl_ref)
            dcw_ref[...] = jnp.zeros_like(dcw_ref)

        us = z_ref[:, 3 * CONV_W:4 * CONV_W]
        dy = dys_ref[...]
        dy_bf = dy.astype(BF16)
        us_ref[0] = us.astype(BF16)
        dyb_ref[0] = dy_bf

        work[...] = _dot(dy_bf, cc_ref[...])
        _scan_chunk(work, lt_ref, carry, reverse=True)
        gg = work[...]
        gg_bf = gg.astype(BF16)
        g_ref[0] = gg_bf
        dus = d_ref[...] * dy + _dot_nt(gg_bf, bb_ref[...])

        hcur = hs_ref[...].astype(F32)
        hlast = hp_ref[...].astype(F32)[hb - 1:hb, :]
        hlast = jnp.where(first_chunk, 0.0, hlast)
        rid = _row_ids(CHUNK, 2 * N_STATE)
        hprev = jnp.where(rid == 0, jnp.broadcast_to(hlast, (CHUNK, 2 * N_STATE)), pltpu.roll(hcur, 1, 0))
        gr, gi = gg[:, :N_STATE], gg[:, N_STATE:]
        hr, hi = hprev[:, :N_STATE], hprev[:, N_STATE:]
        dl_ref[:, :N_STATE] += (gr * hr + gi * hi).reshape(CHUNK // 8, 8, N_STATE).sum(axis=0)
        dl_ref[:, N_STATE:] += (gi * hr - gr * hi).reshape(CHUNK // 8, 8, N_STATE).sum(axis=0)

        @pl.when(i == nc - 1)
        def _():
            dl_ref[0:1, :] = jnp.sum(dl_ref[...], axis=0, keepdims=True)

        zb = z_ref[:, 0:CONV_W]
        zc = z_ref[:, CONV_W:2 * CONV_W]
        zv = z_ref[:, 2 * CONV_W:3 * CONV_W]
        v = zc * zv
        vtail = jnp.where(first_chunk, 0.0, zp_ref[:, CONV_W:2 * CONV_W] * zp_ref[:, 2 * CONV_W:3 * CONV_W])
        v1, v2 = _conv_taps(v, vtail)
        w0, w1, w2 = cw_ref[0:1, :], cw_ref[1:2, :], cw_ref[2:3, :]
        y = w0 * v2 + w1 * v1 + w2 * v
        dya_v = dya_ref[...]
        dzb = dya_v * (y + cb_ref[...])
        dyc = dya_v * zb
        hd = head[...]
        rc = _row_ids(CHUNK, CONV_W)
        n1 = jnp.where(rc == CHUNK - 1, _bcast_row(hd, 0, CHUNK), pltpu.roll(dyc, CHUNK - 1, 0))
        n2 = jnp.where(rc == CHUNK - 1, _bcast_row(hd, 1, CHUNK),
                       jnp.where(rc == CHUNK - 2, _bcast_row(hd, 0, CHUNK), pltpu.roll(dyc, CHUNK - 2, 0)))
        head[...] = dyc[0:8, :]
        dv = w2 * dyc + w1 * n1 + w0 * n2
        dz_ref[:, 0:CONV_W] = dzb.astype(BF16)
        dz_ref[:, CONV_W:2 * CONV_W] = (dv * zv).astype(BF16)
        dz_ref[:, 2 * CONV_W:3 * CONV_W] = (dv * zc).astype(BF16)
        dz_ref[:, 3 * CONV_W:4 * CONV_W] = dus.astype(BF16)
        dcw_ref[0:1, :] += jnp.sum(dyc * v2, axis=0, keepdims=True)
        dcw_ref[1:2, :] += jnp.sum(dyc * v1, axis=0, keepdims=True)
        dcw_ref[2:3, :] += jnp.sum(dyc * v, axis=0, keepdims=True)
        dcw_ref[3:4, :] += jnp.sum(dyc, axis=0, keepdims=True)
        dcw_ref[4:5, :] += jnp.sum(dy * us, axis=0, keepdims=True)

    rev = lambda i: nc - 1 - i
    return pl.pallas_call(
        body, name="s5conv_bwd",
        grid=(nc,),
        in_specs=[pl.BlockSpec((CHUNK, IN_COLS), lambda i: (rev(i), 0)),
                  pl.BlockSpec((8, IN_COLS), lambda i: (jnp.maximum(rev(i) * (CHUNK // 8) - 1, 0), 0)),
                  pl.BlockSpec((CHUNK, 2 * N_STATE), lambda i: (rev(i), 0)),
                  pl.BlockSpec((hb, 2 * N_STATE), lambda i: (jnp.maximum(rev(i) * (CHUNK // hb) - 1, 0), 0)),
                  pl.BlockSpec((CHUNK, CONV_W), lambda i: (rev(i), 0)),
                  pl.BlockSpec((CHUNK, SSM_W), lambda i: (rev(i), 0)),
                  pl.BlockSpec((3, CONV_W), lambda i: (0, 0)),
                  pl.BlockSpec((1, CONV_W), lambda i: (0, 0)),
                  pl.BlockSpec((SSM_W, 2 * N_STATE), lambda i: (0, 0)),
                  pl.BlockSpec((SSM_W, 2 * N_STATE), lambda i: (0, 0)),
                  pl.BlockSpec((1, SSM_W), lambda i: (0, 0)),
                  pl.BlockSpec((32, 2 * N_STATE), lambda i: (0, 0))],
        out_specs=[pl.BlockSpec((CHUNK, IN_COLS), lambda i: (rev(i), 0)),
                   pl.BlockSpec((1, CHUNK, 2 * N_STATE), lambda i: (0, rev(i), 0)),
                   pl.BlockSpec((1, CHUNK, SSM_W), lambda i: (0, rev(i), 0)),
                   pl.BlockSpec((1, CHUNK, SSM_W), lambda i: (0, rev(i), 0)),
                   pl.BlockSpec((8, 2 * N_STATE), lambda i: (0, 0)),
                   pl.BlockSpec((8, CONV_W), lambda i: (0, 0))],
        out_shape=[jax.ShapeDtypeStruct((SEQ, IN_COLS), BF16),
                   jax.ShapeDtypeStruct((1, SEQ, 2 * N_STATE), BF16),
                   jax.ShapeDtypeStruct((1, SEQ, SSM_W), BF16),
                   jax.ShapeDtypeStruct((1, SEQ, SSM_W), BF16),
                   jax.ShapeDtypeStruct((8, 2 * N_STATE), F32),
                   jax.ShapeDtypeStruct((8, CONV_W), F32)],
        scratch_shapes=[pltpu.VMEM((CHUNK, 2 * N_STATE), F32),
                        pltpu.VMEM((8, 2 * N_STATE), F32),
                        pltpu.VMEM((8, CONV_W), F32)],
        compiler_params=_cparams(("arbitrary",)),
    )(z, z, hs, hs, dya, dys, convw, convb, bbmat, ccmat, dvec, ltab_rev)


def _mix_out_fwd(h, ya, ys, gluw, glub, con, son, wout):
    tm = TOK_TILE

    def body(h_ref, ya_ref, ys_ref, gw_ref, gb_ref, con_ref, son_ref, wo_ref, o_ref):
        zg, _ = _gelu(ys_ref[...])
        q = _dot(zg.astype(BF16), gw_ref[...]) + gb_ref[...]
        out_s = zg * _sigmoid(q)
        na, _ = _rms_stats(ya_ref[...])
        ns, _ = _rms_stats(out_s)
        o_ref[...] = (h_ref[...]
                      + _dot((na * con_ref[...]).astype(BF16), wo_ref[0:CONV_W, :])
                      + _dot((ns * son_ref[...]).astype(BF16), wo_ref[CONV_W:2 * CONV_W, :]))

    row = lambda m: (m, 0)
    fixed = lambda m: (0, 0)
    return pl.pallas_call(
        body, name="mix_out_fwd",
        grid=(SEQ // tm,),
        in_specs=[pl.BlockSpec((tm, D_MODEL), row), pl.BlockSpec((tm, CONV_W), row), pl.BlockSpec((tm, SSM_W), row),
                  pl.BlockSpec((SSM_W, SSM_W), fixed), pl.BlockSpec((1, SSM_W), fixed),
                  pl.BlockSpec((1, CONV_W), fixed), pl.BlockSpec((1, SSM_W), fixed),
                  pl.BlockSpec((D_MODEL, D_MODEL), fixed)],
        out_specs=pl.BlockSpec((tm, D_MODEL), row),
        out_shape=jax.ShapeDtypeStruct((SEQ, D_MODEL), F32),
        compiler_params=_cparams(("parallel",)),
    )(h, ya, ys, gluw, glub, con, son, wout)


def _mix_out_bwd(dh, ya, ys, gluw, glub, con, son, wout):
    tm = TOK_TILE

    def body(dh_ref, ya_ref, ys_ref, gw_ref, gb_ref, con_ref, son_ref, wo_ref,
             dya_ref, dys_ref, yc_ref, dhb_ref, zg_ref, dq_ref, part_ref):
        ysv = ys_ref[...]
        zg, th = _gelu(ysv)
        zg_bf = zg.astype(BF16)
        s = _sigmoid(_dot(zg_bf, gw_ref[...]) + gb_ref[...])
        out_s = zg * s
        na, ra = _rms_stats(ya_ref[...])
        ns, rs = _rms_stats(out_s)
        dh_bf = dh_ref[...].astype(BF16)
        yc_ref[0, :, 0:CONV_W] = (na * con_ref[...]).astype(BF16)
        yc_ref[0, :, CONV_W:2 * CONV_W] = (ns * son_ref[...]).astype(BF16)
        dhb_ref[0] = dh_bf
        dca = _dot_nt(dh_bf, wo_ref[0:CONV_W, :])
        dcs = _dot_nt(dh_bf, wo_ref[CONV_W:2 * CONV_W, :])
        dya, dcon = _rms_bwd(dca, na, ra, con_ref[...])
        dos, dson = _rms_bwd(dcs, ns, rs, son_ref[...])
        dya_ref[...] = dya
        dq = dos * zg * s * (1.0 - s)
        dq_bf = dq.astype(BF16)
        dzg = dos * s + _dot_nt(dq_bf, gw_ref[...])
        dys_ref[...] = dzg * _gelu_grad(ysv, th)
        zg_ref[0] = zg_bf
        dq_ref[0] = dq_bf
        rid = _row_ids(SUBLANES, SSM_W)
        part = jnp.zeros((SUBLANES, SSM_W), F32)
        for i, rowv in enumerate((dcon, dson, jnp.sum(dq, axis=0, keepdims=True))):
            part = jnp.where(rid == i, jnp.broadcast_to(rowv, (SUBLANES, SSM_W)), part)
        _accumulate(part_ref, pl.program_id(0) == 0, part)

    row = lambda m: (m, 0)
    fixed = lambda m: (0, 0)
    lead = lambda m: (0, m, 0)
    return pl.pallas_call(
        body, name="mix_out_bwd",
        grid=(SEQ // tm,),
        in_specs=[pl.BlockSpec((tm, D_MODEL), row), pl.BlockSpec((tm, CONV_W), row), pl.BlockSpec((tm, SSM_W), row),
                  pl.BlockSpec((SSM_W, SSM_W), fixed), pl.BlockSpec((1, SSM_W), fixed),
                  pl.BlockSpec((1, CONV_W), fixed), pl.BlockSpec((1, SSM_W), fixed),
                  pl.BlockSpec((D_MODEL, D_MODEL), fixed)],
        out_specs=[pl.BlockSpec((tm, CONV_W), row), pl.BlockSpec((tm, SSM_W), row),
                   pl.BlockSpec((1, tm, D_MODEL), lead), pl.BlockSpec((1, tm, D_MODEL), lead),
                   pl.BlockSpec((1, tm, SSM_W), lead), pl.BlockSpec((1, tm, SSM_W), lead),
                   pl.BlockSpec((8, SSM_W), fixed)],
        out_shape=[jax.ShapeDtypeStruct((SEQ, CONV_W), F32), jax.ShapeDtypeStruct((SEQ, SSM_W), F32),
                   jax.ShapeDtypeStruct((1, SEQ, D_MODEL), BF16), jax.ShapeDtypeStruct((1, SEQ, D_MODEL), BF16),
                   jax.ShapeDtypeStruct((1, SEQ, SSM_W), BF16), jax.ShapeDtypeStruct((1, SEQ, SSM_W), BF16),
                   jax.ShapeDtypeStruct((8, SSM_W), F32)],
        compiler_params=_cparams(("arbitrary",)),
    )(dh, ya, ys, gluw, glub, con, son, wout)


def _ple_fwd(h, g, p, wgate, wprojt):
    tm = TOK_TILE

    def body(h_ref, g_ref, p_ref, wg_ref, wp_ref, o_ref):
        x = h_ref[...]
        xh, _ = _rms_stats(x)
        s = _sigmoid(_dot((xh * g_ref[...]).astype(BF16), wg_ref[...]))
        o_ref[...] = x + _dot_nt(p_ref[...].astype(BF16), wp_ref[...]) * s

    row = lambda m: (m, 0)
    fixed = lambda m: (0, 0)
    return pl.pallas_call(
        body, name="ple_fwd",
        grid=(SEQ // tm,),
        in_specs=[pl.BlockSpec((tm, D_MODEL), row), pl.BlockSpec((1, D_MODEL), fixed), pl.BlockSpec((tm, PLE_DIM), row),
                  pl.BlockSpec((D_MODEL, D_MODEL), fixed), pl.BlockSpec((D_MODEL, PLE_DIM), fixed)],
        out_specs=pl.BlockSpec((tm, D_MODEL), row),
        out_shape=jax.ShapeDtypeStruct((SEQ, D_MODEL), F32),
        compiler_params=_cparams(("parallel",)),
    )(h, g, p, wgate, wprojt)


def _ple_bwd(h, g, p, dh, wgate, wprojt):
    tm = TOK_TILE

    def body(h_ref, g_ref, p_ref, dh_ref, wg_ref, wp_ref, o_ref, u_ref, dq_ref, dpp_ref, pb_ref, dg_ref):
        xh, r = _rms_stats(h_ref[...])
        u = (xh * g_ref[...]).astype(BF16)
        s = _sigmoid(_dot(u, wg_ref[...]))
        p_bf = p_ref[...].astype(BF16)
        pp = _dot_nt(p_bf, wp_ref[...])
        dhv = dh_ref[...]
        dq = (dhv * pp * s * (1.0 - s)).astype(BF16)
        u_ref[0] = u
        dq_ref[0] = dq
        dpp_ref[0] = (dhv * s).astype(BF16)
        pb_ref[0] = p_bf
        dx, dg = _rms_bwd(_dot_nt(dq, wg_ref[...]), xh, r, g_ref[...])
        o_ref[...] = dhv + dx
        _accumulate(dg_ref, pl.program_id(0) == 0, dg)

    row = lambda m: (m, 0)
    fixed = lambda m: (0, 0)
    lead = lambda m: (0, m, 0)
    big = jax.ShapeDtypeStruct((1, SEQ, D_MODEL), BF16)
    return pl.pallas_call(
        body, name="ple_bwd",
        grid=(SEQ // tm,),
        in_specs=[pl.BlockSpec((tm, D_MODEL), row), pl.BlockSpec((1, D_MODEL), fixed), pl.BlockSpec((tm, PLE_DIM), row),
                  pl.BlockSpec((tm, D_MODEL), row),
                  pl.BlockSpec((D_MODEL, D_MODEL), fixed), pl.BlockSpec((D_MODEL, PLE_DIM), fixed)],
        out_specs=[pl.BlockSpec((tm, D_MODEL), row),
                   pl.BlockSpec((1, tm, D_MODEL), lead), pl.BlockSpec((1, tm, D_MODEL), lead),
                   pl.BlockSpec((1, tm, D_MODEL), lead), pl.BlockSpec((1, tm, PLE_DIM), lead),
                   pl.BlockSpec((1, D_MODEL), fixed)],
        out_shape=[jax.ShapeDtypeStruct((SEQ, D_MODEL), F32), big, big, big,
                   jax.ShapeDtypeStruct((1, SEQ, PLE_DIM), BF16),
                   jax.ShapeDtypeStruct((1, D_MODEL), F32)],
        compiler_params=_cparams(("arbitrary",)),
    )(h, g, p, dh, wgate, wprojt)


def _final_loss(h, g, target):
    tm = TOK_TILE

    def body(h_ref, g_ref, t_ref, loss_ref, dh_ref, dg_ref):
        first = pl.program_id(0) == 0
        xh, r = _rms_stats(h_ref[...])
        diff = xh * g_ref[...] - t_ref[...]
        part = 0.5 * jnp.sum(jnp.mean(diff * diff, axis=-1, keepdims=True), axis=0, keepdims=True)
        _accumulate(loss_ref, first, jnp.broadcast_to(part, (SUBLANES, LANES)))
        dx, dg = _rms_bwd(diff * (1.0 / D_MODEL), xh, r, g_ref[...])
        dh_ref[...] = dx
        _accumulate(dg_ref, first, dg)

    row = lambda m: (m, 0)
    fixed = lambda m: (0, 0)
    return pl.pallas_call(
        body, name="final_loss",
        grid=(SEQ // tm,),
        in_specs=[pl.BlockSpec((tm, D_MODEL), row), pl.BlockSpec((1, D_MODEL), fixed),
                  pl.BlockSpec((tm, D_MODEL), row)],
        out_specs=[pl.BlockSpec((SUBLANES, LANES), fixed),
                   pl.BlockSpec((tm, D_MODEL), row),
                   pl.BlockSpec((1, D_MODEL), fixed)],
        out_shape=[jax.ShapeDtypeStruct((SUBLANES, LANES), F32),
                   jax.ShapeDtypeStruct((SEQ, D_MODEL), F32),
                   jax.ShapeDtypeStruct((1, D_MODEL), F32)],
        compiler_params=_cparams(("arbitrary",)),
    )(h, g, target)


def _disc(ar, ai, ldt):
    dt = jnp.exp(ldt)
    mag = jnp.exp(ar * dt)
    ph = ai * dt
    lr, li = mag * jnp.cos(ph), mag * jnp.sin(ph)
    nr, ni = lr - 1.0, li
    den = ar * ar + ai * ai
    return lr, li, (nr * ar + ni * ai) / den, (ni * ar - nr * ai) / den


def _s5_disc(a_row, ldt_row, a_rep, ldt_rep, bt, ct, tile_e, mask):
    n = N_STATE

    def body(ar_ref, lr_ref, ap_ref, lp_ref, b_ref, c_ref, e_ref, m_ref, lt_ref, ltr_ref, bb_ref, cc_ref):
        lr, li, _, _ = _disc(ar_ref[0], ar_ref[1], lr_ref[...])
        pr, pi = lr, li
        for k in range(1, 9):
            for ref, sgn, edge in ((lt_ref, 1.0, 24 + k - 1), (ltr_ref, -1.0, 24 + 8 - k)):
                if k in (1, 2, 4):
                    r0 = {1: 0, 2: 8, 4: 16}[k]
                    ref[r0:r0 + 8, 0:n] = jnp.broadcast_to(pr, (8, n))
                    ref[r0:r0 + 8, n:2 * n] = jnp.broadcast_to(sgn * pi, (8, n))
                ref[edge:edge + 1, 0:n] = pr
                ref[edge:edge + 1, n:2 * n] = sgn * pi
            pr, pi = pr * lr - pi * li, pr * li + pi * lr
        _, _, fr, fi = _disc(ap_ref[0], ap_ref[1], lp_ref[...])
        br, bi = b_ref[0], b_ref[1]
        e = e_ref[...]
        m = m_ref[...].astype(F32)
        bb_ref[:, 0:n] = (_dot((fr * br - fi * bi).astype(BF16), e) * m).astype(BF16)
        bb_ref[:, n:2 * n] = (_dot((fr * bi + fi * br).astype(BF16), e) * m).astype(BF16)
        cc_ref[:, 0:n] = (_dot(c_ref[0].astype(BF16), e) * m).astype(BF16)
        cc_ref[:, n:2 * n] = (-(_dot(c_ref[1].astype(BF16), e) * m)).astype(BF16)

    return pl.pallas_call(
        body, name="s5_disc",
        out_shape=[jax.ShapeDtypeStruct((32, 2 * n), F32), jax.ShapeDtypeStruct((32, 2 * n), F32),
                   jax.ShapeDtypeStruct((SSM_W, 2 * n), BF16), jax.ShapeDtypeStruct((SSM_W, 2 * n), BF16)],
        compiler_params=_cparams(None),
    )(a_row, ldt_row, a_rep, ldt_rep, bt, ct, tile_e, mask)


def _dot_exact(x, sel):
    hi = x.astype(BF16)
    r1 = x - hi.astype(F32)
    mid = r1.astype(BF16)
    lo = (r1 - mid.astype(F32)).astype(BF16)
    return _dot(hi, sel) + _dot(mid, sel) + _dot(lo, sel)


def _s5_disc_bwd(a, ldt, a_rep, ldt_rep, bt, mask, dl, d_bb, d_cc, fold):
    n = N_STATE

    def body(a_ref, l_ref, ap_ref, lp_ref, b_ref, m_ref, dl_ref, dbb_ref, dcc_ref, f_ref,
             da_ref, dldt_ref, db_ref, dc_ref):
        m = m_ref[...].astype(F32)
        fold_m = f_ref[...]
        diag = lambda x: _dot_exact(x * m, fold_m)
        dr, di = diag(dbb_ref[:, 0:n]), diag(dbb_ref[:, n:2 * n])
        dc_ref[0] = diag(dcc_ref[:, 0:n])
        dc_ref[1] = -diag(dcc_ref[:, n:2 * n])
        _, _, fr, fi = _disc(ap_ref[0], ap_ref[1], lp_ref[...])
        br, bi = b_ref[0], b_ref[1]
        db_ref[0] = fr * dr + fi * di
        db_ref[1] = fr * di - fi * dr
        per_state = lambda x: x.reshape(SSM_GROUPS, SSM_GROUP, SSM_STATE).sum(axis=1)
        dfr = per_state(dr * br + di * bi)
        dfi = per_state(di * br - dr * bi)
        _, vjp = jax.vjp(_disc, a_ref[0], a_ref[1], l_ref[...])
        dar, dai, dldt = vjp((dl_ref[0], dl_ref[1], dfr, dfi))
        da_ref[0] = dar
        da_ref[1] = dai
        dldt_ref[...] = jnp.sum(dldt, axis=1, keepdims=True)

    return pl.pallas_call(
        body, name="s5_disc_bwd",
        out_shape=[jax.ShapeDtypeStruct((2, SSM_GROUPS, SSM_STATE), F32),
                   jax.ShapeDtypeStruct((SSM_GROUPS, 1), F32),
                   jax.ShapeDtypeStruct((2, SSM_W, SSM_STATE), F32),
                   jax.ShapeDtypeStruct((2, SSM_W, SSM_STATE), F32)],
        compiler_params=_cparams(None),
    )(a, ldt, a_rep, ldt_rep, bt, mask, dl, d_bb, d_cc, fold)


def _row_block(rows, cap=512):
    for bm in range(min(cap, rows), 0, -1):
        if rows % bm == 0 and (bm % 8 == 0 or bm == rows):
            return bm
    return rows


def _pair_sum(fulls, got, segs):
    ns = len(segs)
    offs = _seg_offsets(segs)
    _, rtot, c = got.shape
    parts = 2
    pr = rtot // parts
    assert pr * parts == rtot and pr % 16 == 0
    pieces = [[] for _ in range(parts)]
    for a, (n, r) in enumerate(segs):
        for m in range(n):
            lo = offs[a] + m * r
            for h in range(parts):
                clo, chi = max(lo, h * pr), min(lo + r, (h + 1) * pr)
                if chi > clo:
                    pieces[h].append((a, m, clo - lo, clo - h * pr, chi - clo))
    n_sems = max(len(ps) for ps in pieces)

    def body(*refs):
        srcs = refs[:ns]
        got_ref, p32_ref, pbf_ref, own_v, sems = refs[ns:]
        h = pl.program_id(0)
        k = pl.program_id(1)
        dev = 2 * k + lax.axis_index("c")
        for hh in range(parts):
            @pl.when(h == hh)
            def _(hh=hh):
                cps = []
                for i, (a, m, so, do, rows) in enumerate(pieces[hh]):
                    start = pl.multiple_of(dev * segs[a][1] + so, 16)
                    cps.append(pltpu.make_async_copy(srcs[a].at[m, pl.ds(start, rows), :],
                                                     own_v.at[pl.ds(do, rows), :], sems.at[i]))
                for cp in cps:
                    cp.start()
                for cp in cps:
                    cp.wait()
        s = own_v[...].astype(F32) + got_ref[0].astype(F32)
        pbf_ref[0] = s.astype(BF16)

        @pl.when(k == 2 * lax.axis_index("x") + lax.axis_index("y"))
        def _():
            p32_ref[...] = s

    spec = pl.BlockSpec((1, pr, c), lambda h, k: (k, h, 0))
    return pl.pallas_call(
        body, name="pair_sum",
        grid=(parts, 4),
        in_specs=[HBM] * ns + [spec], out_specs=[pl.BlockSpec((pr, c), lambda h, k: (h, 0)), spec],
        out_shape=[jax.ShapeDtypeStruct((rtot, c), F32), jax.ShapeDtypeStruct(got.shape, BF16)],
        scratch_shapes=[pltpu.VMEM((pr, c), BF16), pltpu.SemaphoreType.DMA((n_sems,))],
        compiler_params=_cparams(("arbitrary", "arbitrary")),
    )(*fulls, got)


def _chip_sum(own, rb):
    r, c = own.shape
    bm = _row_block(r)

    def body(o_ref, r_ref, s_ref):
        s_ref[...] = ((o_ref[...] + r_ref[0].astype(F32)) + r_ref[1].astype(F32)) + r_ref[2].astype(F32)

    return pl.pallas_call(
        body, name="chip_sum",
        grid=(r // bm,),
        in_specs=[pl.BlockSpec((bm, c), lambda k: (k, 0)), pl.BlockSpec((3, bm, c), lambda k: (0, k, 0))],
        out_specs=pl.BlockSpec((bm, c), lambda k: (k, 0)),
        out_shape=jax.ShapeDtypeStruct((r, c), F32),
        compiler_params=_cparams(("parallel",)),
    )(own, rb)


def _sum8(x, after):
    _, r, c = x.shape
    bm = _row_block(r)

    def body(x_ref, after_ref, s_ref):
        s = x_ref[0]
        for d in range(1, N_DEV):
            s = s + x_ref[d]
        s_ref[...] = s

    return pl.pallas_call(
        body, name="sum8",
        grid=(r // bm,),
        in_specs=[pl.BlockSpec((N_DEV, bm, c), lambda k: (0, k, 0)), ANY],
        out_specs=pl.BlockSpec((bm, c), lambda k: (k, 0)),
        out_shape=jax.ShapeDtypeStruct((r, c), F32),
        compiler_params=_cparams(("parallel",)),
    )(x, after)


def _adamw(w, g, m, v):
    r, c = w.shape
    bm = _row_block(r)
    bc1 = 1.0 - ADAM_B1 ** ADAM_STEP
    bc2 = 1.0 - ADAM_B2 ** ADAM_STEP

    def body(w_ref, g_ref, m_ref, v_ref, d_ref, nm_ref, nv_ref):
        gv = g_ref[...]
        nm = ADAM_B1 * m_ref[...] + (1.0 - ADAM_B1) * gv
        nv = ADAM_B2 * v_ref[...] + (1.0 - ADAM_B2) * (gv * gv)
        nm_ref[...] = nm
        nv_ref[...] = nv
        d_ref[...] = -ADAM_LR * ((nm / bc1) / (jnp.sqrt(nv / bc2) + ADAM_EPS) + ADAM_WD * w_ref[...])

    spec = pl.BlockSpec((bm, c), lambda k: (k, 0))
    shp = jax.ShapeDtypeStruct((r, c), F32)
    return pl.pallas_call(
        body, name="adamw",
        grid=(r // bm,),
        in_specs=[spec] * 4, out_specs=[spec] * 3, out_shape=[shp] * 3,
        compiler_params=_cparams(("parallel",)),
    )(w, g, m, v)


def _mesh_pos():
    return lax.axis_index("x"), lax.axis_index("y"), lax.axis_index("c")


def _dev_index(p):
    return 4 * p[0] + 2 * p[1] + p[2]


def _seg_offsets(segs):
    offs, o = [], 0
    for n, r in segs:
        offs.append(o)
        o += n * r
    return offs


def _remote(src, dst, send_sem, recv_sem, to):
    return pltpu.make_async_remote_copy(src_ref=src, dst_ref=dst, send_sem=send_sem, recv_sem=recv_sem,
                                        device_id=to, device_id_type=MESH)


def _allgather(pack, segs, name):
    rtot, c = pack.shape
    ns = len(segs)
    offs = _seg_offsets(segs)
    assert rtot == sum(n * r for n, r in segs)

    def body(pack_ref, *refs):
        outs = refs[:ns]
        send_sems, recv_sems, local_sem = refs[ns:]
        x, y, cc = _mesh_pos()
        me, sib = (x, y, cc), (x, y, 1 - cc)
        chips = [(1 - x, y), (x, 1 - y), (1 - x, 1 - y)]

        def pieces(dev, from_pack):
            res = []
            for a, (n, r) in enumerate(segs):
                for m in range(n):
                    dst = outs[a].at[m, pl.ds(pl.multiple_of(dev * r, r), r), :]
                    src = pack_ref.at[pl.ds(offs[a] + m * r, r), :] if from_pack else dst
                    res.append((src, dst))
            return res

        def push(k, dev, to, from_pack):
            for s, d in pieces(dev, from_pack):
                _remote(s, d, send_sems.at[k], recv_sems.at[k], to).start()

        def whole(k):
            return _remote(pack_ref, pack_ref, send_sems.at[k], recv_sems.at[k], me)

        my_dev = _dev_index(me)
        for s, d in pieces(my_dev, True):
            pltpu.make_async_copy(s, d, local_sem).start()
        push(0, my_dev, sib, True)
        for j, chip in enumerate(chips):
            push(1 + j, my_dev, (*chip, cc), True)
        for j, chip in enumerate(chips):
            whole(1 + j).wait_recv()
            push(4 + j, _dev_index((*chip, cc)), sib, False)
        whole(0).wait_recv()
        for j in range(3):
            whole(4 + j).wait_recv()
        for k in range(7):
            whole(k).wait_send()
        pltpu.make_async_copy(pack_ref, pack_ref, local_sem).wait()

    return pl.pallas_call(
        body, name=name,
        in_specs=[HBM], out_specs=[HBM] * ns,
        out_shape=[jax.ShapeDtypeStruct((n, N_DEV * r, c), pack.dtype) for n, r in segs],
        scratch_shapes=[pltpu.SemaphoreType.DMA((7,)), pltpu.SemaphoreType.DMA((7,)), pltpu.SemaphoreType.DMA],
    )(pack)


HBM = pl.BlockSpec(memory_space=pltpu.HBM)
SEM = pl.BlockSpec(memory_space=pltpu.SEMAPHORE)
VMEM_WHOLE = pl.BlockSpec(memory_space=pltpu.VMEM)
EFFECT = pltpu.SideEffectType.DATAFLOW_SIDE_EFFECTING


def _hbm(a):
    return pltpu.with_memory_space_constraint(a, pltpu.HBM)


def _ag_start(pack, segs, after, name):
    rtot, c = pack.shape
    ns = len(segs)
    offs = _seg_offsets(segs)

    def body(pack_ref, *refs):
        lands = refs[:ns]
        send_sems, recv_sems = refs[ns + 1], refs[ns + 2]
        token = refs[-1]
        x, y, cc = _mesh_pos()
        my_dev = _dev_index((x, y, cc))
        targets = [(x, y, 1 - cc), (1 - x, y, cc), (x, 1 - y, cc), (1 - x, 1 - y, cc)]
        for k, to in enumerate(targets):
            for a, (n, r) in enumerate(segs):
                for m in range(n):
                    _remote(pack_ref.at[pl.ds(offs[a] + m * r, r), :],
                            lands[a].at[m, pl.ds(pl.multiple_of(my_dev * r, r), r), :],
                            send_sems.at[k], recv_sems.at[k], to).start()
        token[...] = jnp.zeros_like(token)

    land_shapes = [(n, N_DEV * r, c) for n, r in segs]
    outs = pl.pallas_call(
        body, name=name,
        in_specs=[HBM] * (1 + ns) + [ANY],
        out_specs=[SEM, SEM, HBM] + [HBM] * ns + [VMEM_WHOLE],
        out_shape=[pltpu.SemaphoreType.DMA((4,)), pltpu.SemaphoreType.DMA((4,)), pltpu.HBM(pack.shape, pack.dtype)]
        + [pltpu.HBM(s, pack.dtype) for s in land_shapes] + [jax.ShapeDtypeStruct((SUBLANES, LANES), F32)],
        input_output_aliases={0: 2, **{1 + i: 3 + i for i in range(ns)}},
        compiler_params=pltpu.CompilerParams(has_side_effects=EFFECT),
    )(_hbm(pack), *[_hbm(lax.empty(s, pack.dtype)) for s in land_shapes], after)
    return outs[0], outs[1], outs[2], list(outs[3:3 + ns]), outs[-1]


def _ag_wait(send_sems, recv_sems, pack, lands, after, name):
    ns = len(lands)

    def body(pack_ref, *refs):
        send_ref, recv_ref = refs[ns], refs[ns + 1]
        me = _mesh_pos()
        for k in range(4):
            whole = _remote(pack_ref, pack_ref, send_ref.at[k], recv_ref.at[k], me)
            whole.wait_send()
            whole.wait_recv()

    outs = pl.pallas_call(
        body, name=name,
        in_specs=[HBM] * (1 + ns) + [SEM, SEM, ANY],
        out_specs=[HBM] * (1 + ns),
        out_shape=[pltpu.HBM(pack.shape, pack.dtype)] + [pltpu.HBM(a.shape, a.dtype) for a in lands],
        input_output_aliases={i: i for i in range(1 + ns)},
        compiler_params=pltpu.CompilerParams(has_side_effects=EFFECT),
    )(pack, *lands, send_sems, recv_sems, after)
    return outs[0], list(outs[1:])


def _ag_finish(pack, lands, segs):
    rtot, c = pack.shape
    ns = len(segs)
    offs = _seg_offsets(segs)

    def body(pack_ref, *refs):
        outs = refs[ns:2 * ns]
        stage, send_sems, recv_sems, local_sems = refs[2 * ns:]
        x, y, cc = _mesh_pos()
        me, sib = (x, y, cc), (x, y, 1 - cc)
        chips = [(1 - x, y), (x, 1 - y), (1 - x, 1 - y)]

        def rows(a, m, dev):
            return outs[a].at[m, pl.ds(pl.multiple_of(dev * segs[a][1], segs[a][1]), segs[a][1]), :]

        for j, chip in enumerate(chips):
            dev = _dev_index((*chip, cc))
            for a, (n, r) in enumerate(segs):
                for m in range(n):
                    _remote(rows(a, m, dev), rows(a, m, dev), send_sems.at[j], recv_sems.at[j], sib).start()
        load = pltpu.make_async_copy(pack_ref, stage, local_sems.at[0])
        load.start()
        load.wait()
        my_dev = _dev_index(me)
        for a, (n, r) in enumerate(segs):
            for m in range(n):
                pltpu.make_async_copy(stage.at[pl.ds(offs[a] + m * r, r), :], rows(a, m, my_dev), local_sems.at[1]).start()
        pltpu.make_async_copy(stage, pack_ref, local_sems.at[1]).wait()
        for j in range(3):
            _remote(pack_ref, pack_ref, send_sems.at[j], recv_sems.at[j], me).wait()

    outs = pl.pallas_call(
        body, name="ag_finish",
        in_specs=[HBM] * (1 + ns), out_specs=[HBM] * ns,
        out_shape=[jax.ShapeDtypeStruct(a.shape, a.dtype) for a in lands],
        input_output_aliases={1 + i: i for i in range(ns)},
        scratch_shapes=[pltpu.VMEM((rtot, c), pack.dtype), pltpu.SemaphoreType.DMA((3,)),
                        pltpu.SemaphoreType.DMA((3,)), pltpu.SemaphoreType.DMA((2,))],
        compiler_params=_cparams(None, 16),
    )(pack, *lands)
    return list(outs)


def _rs_chips_start(pbf, after, name):
    _, rtot, c = pbf.shape

    def body(pbf_ref, land_ref, after_ref, send_sems, recv_sems, pbf_thru, land_thru, token):
        x, y, cc = _mesh_pos()
        for j, (cx, cy) in enumerate([(1 - x, y), (x, 1 - y), (1 - x, 1 - y)]):
            _remote(pbf_ref.at[2 * cx + cy], land_ref.at[j], send_sems.at[j], recv_sems.at[j], (cx, cy, cc)).start()
        token[...] = jnp.zeros_like(token)

    return pl.pallas_call(
        body, name=name,
        in_specs=[HBM, HBM, ANY],
        out_specs=[SEM, SEM, HBM, HBM, VMEM_WHOLE],
        out_shape=[pltpu.SemaphoreType.DMA((3,)), pltpu.SemaphoreType.DMA((3,)), pltpu.HBM(pbf.shape, pbf.dtype),
                   pltpu.HBM((3, rtot, c), pbf.dtype), jax.ShapeDtypeStruct((SUBLANES, LANES), F32)],
        input_output_aliases={0: 2, 1: 3},
        compiler_params=pltpu.CompilerParams(has_side_effects=EFFECT),
    )(_hbm(pbf), _hbm(lax.empty((3, rtot, c), pbf.dtype)), after)


def _rs_chips_wait(send_sems, recv_sems, pbf, land, after, name):
    def body(pbf_ref, land_ref, send_ref, recv_ref, after_ref, pbf_out, land_out):
        me = _mesh_pos()
        for j in range(3):
            cp = _remote(pbf_ref.at[0], land_ref.at[j], send_ref.at[j], recv_ref.at[j], me)
            cp.wait_send()
            cp.wait_recv()

    return pl.pallas_call(
        body, name=name,
        in_specs=[HBM, HBM, SEM, SEM, ANY], out_specs=[HBM, HBM],
        out_shape=[pltpu.HBM(pbf.shape, pbf.dtype), pltpu.HBM(land.shape, land.dtype)],
        input_output_aliases={0: 0, 1: 1},
        compiler_params=pltpu.CompilerParams(has_side_effects=EFFECT),
    )(pbf, land, send_sems, recv_sems, after)[1]


def _rs_sibling_start(fulls, segs, name):
    ns = len(segs)
    offs = _seg_offsets(segs)
    rtot = sum(n * r for n, r in segs)
    c = fulls[0].shape[-1]
    dt = fulls[0].dtype

    def body(*refs):
        srcs = refs[:ns]
        land_ref, send_sem, recv_sem = refs[ns], refs[ns + 1], refs[ns + 2]
        token = refs[-1]
        x, y, cc = _mesh_pos()
        for k in range(4):
            for a, (n, r) in enumerate(segs):
                for m in range(n):
                    theirs = srcs[a].at[m, pl.ds(pl.multiple_of((2 * k + 1 - cc) * r, r), r), :]
                    _remote(theirs, land_ref.at[k, pl.ds(offs[a] + m * r, r), :], send_sem, recv_sem,
                            (x, y, 1 - cc)).start()
        token[...] = jnp.zeros_like(token)

    outs = pl.pallas_call(
        body, name=name,
        in_specs=[HBM] * (ns + 1),
        out_specs=[SEM, SEM] + [HBM] * (ns + 1) + [VMEM_WHOLE],
        out_shape=[pltpu.SemaphoreType.DMA(()), pltpu.SemaphoreType.DMA(())]
        + [pltpu.HBM(a.shape, a.dtype) for a in fulls] + [pltpu.HBM((4, rtot, c), dt),
                                                           jax.ShapeDtypeStruct((SUBLANES, LANES), F32)],
        input_output_aliases={i: 2 + i for i in range(ns + 1)},
        compiler_params=pltpu.CompilerParams(has_side_effects=EFFECT),
    )(*[_hbm(a) for a in fulls], _hbm(lax.empty((4, rtot, c), dt)))
    return outs[0], outs[1], list(outs[2:2 + ns]), outs[2 + ns], outs[-1]


def _rs_sibling_wait(send_sem, recv_sem, fulls, land, after, name):
    ns = len(fulls)

    def body(*refs):
        land_ref, send_ref, recv_ref = refs[ns], refs[ns + 1], refs[ns + 2]
        whole = _remote(land_ref, land_ref, send_ref, recv_ref, _mesh_pos())
        whole.wait_send()
        whole.wait_recv()

    outs = pl.pallas_call(
        body, name=name,
        in_specs=[HBM] * (ns + 1) + [SEM, SEM, ANY], out_specs=[HBM] * (ns + 1),
        out_shape=[pltpu.HBM(a.shape, a.dtype) for a in fulls] + [pltpu.HBM(land.shape, land.dtype)],
        input_output_aliases={i: i for i in range(ns + 1)},
        compiler_params=pltpu.CompilerParams(has_side_effects=EFFECT),
    )(*fulls, land, send_sem, recv_sem, after)
    return list(outs[:ns]), outs[ns]


def _tp(w):
    return jnp.swapaxes(w, -1, -2)


def _s5_prepare(a_re, a_im, log_dt, b_re, b_im, c_re, c_im):
    a = jnp.stack([a_re, a_im], axis=1)
    ldt = jnp.broadcast_to(log_dt[:, :, None], (DEPTH, SSM_GROUPS, SSM_STATE))
    a_row = a.reshape(DEPTH, 2, 1, N_STATE)
    ldt_row = ldt.reshape(DEPTH, 1, N_STATE)
    a_rep = jnp.repeat(a, SSM_GROUP, axis=2)
    ldt_rep = jnp.repeat(ldt, SSM_GROUP, axis=1)
    bt = jnp.stack([_tp(b_re), _tp(b_im)], axis=1).reshape(DEPTH, 2, SSM_W, SSM_STATE)
    ct = jnp.stack([c_re, c_im], axis=1).reshape(DEPTH, 2, SSM_W, SSM_STATE)
    tile_e = jnp.tile(jnp.eye(SSM_STATE, dtype=BF16), (1, SSM_GROUPS))
    mask = jnp.repeat(jnp.repeat(jnp.eye(SSM_GROUPS, dtype=BF16), SSM_GROUP, axis=0), SSM_STATE, axis=1)
    out = []
    for l in range(DEPTH):
        tabs = _s5_disc(a_row[l], ldt_row[l], a_rep[l], ldt_rep[l], bt[l], ct[l], tile_e, mask)
        out.append(((a[l], ldt[l], a_rep[l], ldt_rep[l], bt[l], mask), *tabs))
    return out


def _layer_fwd(h, p_l, small, big):
    saved = {'h0': h}
    h, saved['gu1'] = _ffn_fwd(h, small['ffn1_norm'], big['ff1'])
    saved['h1'] = h
    z = _inproj_fwd(h, small['mix_norm'], big['wint'])
    ya, ys, hs = _s5conv_fwd(z, small['conv_w'], small['conv_b'], small['bbmat'], small['ccmat'], small['dvec'],
                             small['ltab'])
    saved.update(z=z, ya=ya, ys=ys, hs=hs)
    h = _mix_out_fwd(h, ya, ys, big['glu'], small['glu_b'], small['conv_out_norm'], small['ssm_out_norm'], big['wout'])
    saved['h2'] = h
    h, saved['gu2'] = _ffn_fwd(h, small['ffn2_norm'], big['ff2'])
    saved['h3'] = h
    h = _ple_fwd(h, small['ple_norm'], p_l, big['plg'], big['plpt'])
    return h, saved


def _ffn_bwd(h_in, g, dh, gu, w3):
    dh_in, dga, ud, dg = _ffn_bwd_act(h_in, g, dh, gu, w3)
    return dh_in, _matmul_tn(dga, ud, FF_BLOCK, BF16, "ffn_wgrad"), dg


def _layer_bwd_top(dh, p_l, small, big, saved):
    gs = {}
    dh, u, dq, dpp, pb, gs['ple_norm'] = _ple_bwd(saved['h3'], small['ple_norm'], p_l, dh, big['plg'], big['plpt'])
    d_plg = _matmul_tn(u, dq, 256, BF16, "ple_gate_wgrad")
    d_plpt = _matmul_tn(dpp, pb, 256, BF16, "ple_proj_wgrad")
    dh, d_ff2, gs['ffn2_norm'] = _ffn_bwd(saved['h2'], small['ffn2_norm'], dh, saved['gu2'], big['ff2'])
    return dh, (gs, d_plg, d_plpt, d_ff2)


def _layer_bwd_rest(dh, top, small, big, saved):
    gs, d_plg, d_plpt, d_ff2 = top
    dya, dys, ycat, dhb, zg, dq, part = _mix_out_bwd(dh, saved['ya'], saved['ys'], big['glu'], small['glu_b'],
                                                     small['conv_out_norm'], small['ssm_out_norm'], big['wout'])
    d_wout = _matmul_tn(ycat, dhb, 256, BF16, "w_out_wgrad")
    d_glu = _matmul_tn(zg, dq, 256, BF16, "glu_wgrad")
    dz, gadj, us, dyb, dl, dcw = _s5conv_bwd(saved['z'], saved['hs'], dya, dys, small['conv_w'], small['conv_b'],
                                             small['bbmat'], small['ccmat'], small['dvec'], small['ltab_rev'])
    d_bb = _matmul_tn(us, gadj, SSM_W, F32, "s5_b_wgrad", bn=1024)[0]
    d_cc = _matmul_tn(dyb, saved['hs'][None], SSM_W, F32, "s5_c_wgrad", bn=1024)[0]
    dh, u, gs['mix_norm'] = _inproj_bwd(saved['h1'], small['mix_norm'], dh, dz, big['wint'])
    d_wint = _matmul_tn(dz[None], u, 256, BF16, "w_in_wgrad")
    dh, d_ff1, gs['ffn1_norm'] = _ffn_bwd(saved['h0'], small['ffn1_norm'], dh, saved['gu1'], big['ff1'])

    dlb = dl[0].reshape(2, SSM_GROUPS, SSM_STATE)
    fold = jnp.tile(jnp.eye(SSM_STATE, dtype=BF16), (SSM_GROUPS, 1))
    da, dldt, dbt, dct = _s5_disc_bwd(*small['disc_in'], dlb, d_bb, d_cc, fold)
    gs['ssm_A_re'], gs['ssm_A_im'] = da[0], da[1]
    gs['ssm_log_dt'] = dldt[:, 0]
    ghp = (SSM_GROUPS, SSM_GROUP, SSM_STATE)
    gs['ssm_B_re'], gs['ssm_B_im'] = dbt[0].reshape(ghp), dbt[1].reshape(ghp)
    gs['ssm_C_re'], gs['ssm_C_im'] = dct[0].reshape(ghp), dct[1].reshape(ghp)
    gs['conv_w'] = dcw[0:3]
    gs['conv_b'] = dcw[3]
    gs['ssm_D'] = dcw[4].reshape(SSM_GROUPS, SSM_GROUP)
    gs['conv_out_norm'], gs['ssm_out_norm'], gs['glu_b'] = part[0], part[1], part[2]
    for n in ('ple_norm', 'ffn2_norm', 'mix_norm', 'ffn1_norm'):
        gs[n] = gs[n][0]
    fulls = [d_ff1, d_ff2, d_wint, d_wout, d_plg,
             d_plpt.reshape(1, D_MODEL * PLE_DIM // D_MODEL, D_MODEL), d_glu.reshape(1, SSM_W * SSM_W // D_MODEL, D_MODEL)]
    return dh, fulls, gs


VIEW_T = ('ffn1_w_gate', 'ffn1_w_up', 'ffn2_w_gate', 'ffn2_w_up', 'ssm_B_re', 'ssm_B_im')


def _view(name, a):
    return _tp(a) if name in VIEW_T else a


def _layer_pack(W, l):
    return jnp.concatenate([
        _tp(W['ffn1_w_gate'][l]), _tp(W['ffn1_w_up'][l]), W['ffn1_w_down'][l],
        _tp(W['ffn2_w_gate'][l]), _tp(W['ffn2_w_up'][l]), W['ffn2_w_down'][l],
        _tp(W['w_in'][l]), W['w_out'][l], W['ple_w_gate'][l],
        _tp(W['ple_w_proj'][l]).reshape(-1, D_MODEL), W['glu_w'][l].reshape(-1, D_MODEL)], axis=0).astype(BF16)


def _pad_rows(flat, mult, width=LANES):
    per = mult * width
    n = flat.shape[0]
    tot = -(-n // per) * per
    return jnp.pad(flat, (0, tot - n)).reshape(tot // width, width)


def _adamw_any(w, g, m, v):
    shp = w.shape
    two = (lambda t: t.reshape(-1, shp[-1]))
    d, nm, nv = _adamw(two(w), two(g), two(m), two(v))
    return d.reshape(shp), nm.reshape(shp), nv.reshape(shp)


def kernel(x, p, ffn1_norm, ffn1_w_gate, ffn1_w_up, ffn1_w_down, mix_norm, w_in, conv_w, conv_b, ssm_A_re, ssm_A_im, ssm_B_re, ssm_B_im, ssm_C_re, ssm_C_im, ssm_D, ssm_log_dt, glu_w, glu_b, conv_out_norm, ssm_out_norm, w_out, ffn2_norm, ffn2_w_gate, ffn2_w_up, ffn2_w_down, ple_norm, ple_w_gate, ple_w_proj, final_norm, loss_target, m_ffn1_norm, m_ffn1_w_gate, m_ffn1_w_up, m_ffn1_w_down, m_mix_norm, m_w_in, m_conv_w, m_conv_b, m_ssm_A_re, m_ssm_A_im, m_ssm_B_re, m_ssm_B_im, m_ssm_C_re, m_ssm_C_im, m_ssm_D, m_ssm_log_dt, m_glu_w, m_glu_b, m_conv_out_norm, m_ssm_out_norm, m_w_out, m_ffn2_norm, m_ffn2_w_gate, m_ffn2_w_up, m_ffn2_w_down, m_ple_norm, m_ple_w_gate, m_ple_w_proj, m_final_norm, v_ffn1_norm, v_ffn1_w_gate, v_ffn1_w_up, v_ffn1_w_down, v_mix_norm, v_w_in, v_conv_w, v_conv_b, v_ssm_A_re, v_ssm_A_im, v_ssm_B_re, v_ssm_B_im, v_ssm_C_re, v_ssm_C_im, v_ssm_D, v_ssm_log_dt, v_glu_w, v_glu_b, v_conv_out_norm, v_ssm_out_norm, v_w_out, v_ffn2_norm, v_ffn2_w_gate, v_ffn2_w_up, v_ffn2_w_down, v_ple_norm, v_ple_w_gate, v_ple_w_proj, v_final_norm):
    given = dict(locals())
    W = {n: given[n] for n in W_NAMES}
    M = {n: given['m_' + n] for n in W_NAMES}
    V = {n: given['v_' + n] for n in W_NAMES}
    Wv, Mv, Vv = [{n: _view(n, d[n]) for n in W_NAMES} for d in (W, M, V)]
    my_dev = _dev_index(_mesh_pos())

    conv_shard = _pad_rows(W['conv_w'].reshape(-1), SUBLANES)
    conv_all = _allgather(conv_shard, ((1, SUBLANES),), "ag_conv_w")[0]
    conv_full = conv_all.reshape(N_DEV, -1)[:, :DEPTH * 3 * (CONV_W // N_DEV)]
    conv_full = conv_full.reshape(N_DEV, DEPTH, 3, CONV_W // N_DEV).transpose(1, 2, 0, 3).reshape(DEPTH, 3, CONV_W)
    packs = [_layer_pack(W, 0)]
    flight = _ag_start(packs[0], SEGS, conv_all, "ag_start_0")
    s5 = _s5_prepare(*[W[n] + flight[4][0, 0] for n in ('ssm_A_re', 'ssm_A_im', 'ssm_log_dt')],
                     *[W[n] for n in ('ssm_B_re', 'ssm_B_im', 'ssm_C_re', 'ssm_C_im')])

    packs += [_layer_pack(W, l) for l in range(1, DEPTH)]
    prepared = conv_full[0, 0:1, 0:1] + s5[DEPTH - 1][1][0:1, 0:1] + packs[DEPTH - 1][0:1, 0:1].astype(F32)

    smalls, saves, bigs = [], [], []
    h = x[0]
    for l in range(DEPTH):
        send_sems, recv_sems, pack_thru, lands, _ = flight
        pack_thru, lands = _ag_wait(send_sems, recv_sems, pack_thru, lands, prepared if l == 0 else h,
                                    "ag_wait_%d" % l)
        token = jnp.zeros((1, 1), F32)
        if l + 1 < DEPTH:
            flight = _ag_start(packs[l + 1], SEGS, lands[0], "ag_start_%d" % (l + 1))
            token = flight[4][0:1, 0:1]
        ff1, ff2, wint, wout, plg, plpt, glu = _ag_finish(pack_thru, lands, SEGS)
        bigs.append(dict(ff1=ff1, ff2=ff2, wint=wint[0], wout=wout[0], plg=plg[0],
                         plpt=plpt.reshape(D_MODEL, PLE_DIM), glu=glu.reshape(SSM_W, SSM_W)))
        small = {n: W[n][l][None] for n in ('ffn1_norm', 'mix_norm', 'conv_b', 'glu_b', 'conv_out_norm',
                                            'ssm_out_norm', 'ffn2_norm', 'ple_norm')}
        small['ffn1_norm'] = small['ffn1_norm'] + token
        small['conv_w'] = conv_full[l]
        small['dvec'] = W['ssm_D'][l].reshape(1, SSM_W)
        small['disc_in'], small['ltab'], small['ltab_rev'], small['bbmat'], small['ccmat'] = s5[l]
        h, saved = _layer_fwd(h, p[l, 0], small, bigs[l])
        smalls.append(small)
        saves.append(saved)
    loss_tile, dh, d_final = _final_loss(h, W['final_norm'][None], loss_target[0])
    loss = lax.psum(loss_tile[0, 0], ("x", "y", "c"))

    layer_gs = [None] * DEPTH
    shard_grads = [None] * DEPTH
    zero = jnp.zeros((1, 1), F32)
    sib, ici = None, None

    def finish_sibling(after_sib, after_ici):
        nonlocal sib, ici
        up, (send_sem, recv_sem, fulls_thru, land, _) = sib
        fulls_thru, got = _rs_sibling_wait(send_sem, recv_sem, fulls_thru, land, after_sib, "sib_wait_%d" % up)
        own32, pbf = _pair_sum(fulls_thru, got, SEGS)
        finish_chips(pbf)
        ici = (up, _rs_chips_start(pbf, after_ici, "rs_start_%d" % up), own32)
        sib = None

    def finish_chips(after):
        nonlocal ici
        if ici is not None:
            up, (send_sems, recv_sems, pbf_thru, land, _), own32 = ici
            got3 = _rs_chips_wait(send_sems, recv_sems, pbf_thru, land, after, "rs_wait_%d" % up)
            shard_grads[up] = _chip_sum(own32, got3)
            ici = None

    for l in reversed(range(DEPTH)):
        small = dict(smalls[l])
        if sib is not None:
            small['ple_norm'] = small['ple_norm'] + sib[1][4][0:1, 0:1]
        dh, top = _layer_bwd_top(dh, p[l, 0], small, bigs[l], saves[l])
        if sib is not None:
            finish_sibling(dh, dh)
            small['glu_b'] = small['glu_b'] + ici[1][4][0:1, 0:1]
        dh, fulls, layer_gs[l] = _layer_bwd_rest(dh, top, small, bigs[l], saves[l])
        sib = (l, _rs_sibling_start(fulls, SEGS, "sib_start_%d" % l))
    grad_x = dh[None]

    gs = {n: jnp.stack([layer_gs[l][n] for l in range(DEPTH)]) for n in layer_gs[0]}
    gs['final_norm'] = d_final[0]
    flat = jnp.concatenate([gs[n].reshape(-1) for n in SMALL_NAMES] + [gs['conv_w'].reshape(-1)])
    n_flat = flat.shape[0]
    flat = _pad_rows(flat, 64, D_MODEL) + sib[1][4][0:1, 0:1]
    rows = flat.shape[0]
    gathered = _allgather(flat, ((1, rows),), "ag_small_grads")[0]
    finish_sibling(gathered, gathered)
    red = _sum8(gathered.reshape(N_DEV, rows, D_MODEL), ici[1][4]).reshape(-1)[:n_flat]
    finish_chips(red)
    G = {}
    o = 0
    for n in SMALL_NAMES:
        G[n] = red[o:o + W[n].size].reshape(Wv[n].shape)
        o += W[n].size
    conv_g_full = red[o:].reshape(DEPTH, 3, CONV_W)
    G['conv_w'] = lax.dynamic_slice_in_dim(conv_g_full, my_dev * (CONV_W // N_DEV), CONV_W // N_DEV, axis=2)

    sg = jnp.stack(shard_grads)
    offs = _seg_offsets(SEGS)
    r = SEGS[0][1]
    for a, f in ((0, 'ffn1'), (1, 'ffn2')):
        G[f + '_w_gate'] = sg[:, offs[a]:offs[a] + r]
        G[f + '_w_up'] = sg[:, offs[a] + r:offs[a] + 2 * r]
        G[f + '_w_down'] = sg[:, offs[a] + 2 * r:offs[a] + 3 * r]
    G['w_in'] = _tp(sg[:, offs[2]:offs[2] + SEGS[2][1]])
    G['w_out'] = sg[:, offs[3]:offs[3] + SEGS[3][1]]
    G['ple_w_gate'] = sg[:, offs[4]:offs[4] + SEGS[4][1]]
    G['ple_w_proj'] = _tp(sg[:, offs[5]:offs[5] + SEGS[5][1]].reshape(DEPTH, D_MODEL // N_DEV, PLE_DIM))
    G['glu_w'] = sg[:, offs[6]:offs[6] + SEGS[6][1]].reshape(DEPTH, SSM_W // N_DEV, SSM_W)

    delta, new_m, new_v = {}, {}, {}
    cat = lambda src: _pad_rows(jnp.concatenate([src[n].reshape(-1) for n in SMALL_NAMES]), SUBLANES, D_MODEL)
    d_s, m_s, v_s = _adamw(cat(Wv), cat(G), cat(Mv), cat(Vv))
    o = 0
    for n in SMALL_NAMES:
        for dst, src in ((delta, d_s), (new_m, m_s), (new_v, v_s)):
            dst[n] = src.reshape(-1)[o:o + W[n].size].reshape(Wv[n].shape)
        o += W[n].size
    for n in W_NAMES:
        if n not in delta:
            delta[n], new_m[n], new_v[n] = _adamw_any(Wv[n], G[n], Mv[n], Vv[n])

    outs = [[_view(n, d[n]) for n in W_NAMES] for d in (G, delta, new_m, new_v)]
    return (loss, grad_x, *outs[0], *outs[1], *outs[2], *outs[3])
```

```python
import math

import jax
import jax.numpy as jnp
from jax import lax
from jax.experimental import pallas as pl
from jax.experimental.pallas import tpu as pltpu

F32 = jnp.float32
BF16 = jnp.bfloat16

N_DEV = 8
DEPTH = 4
SEQ = 2048
D_MODEL = 1024
D_FF = 2816
CONV_W = 512
SSM_W = 512
SSM_GROUPS = 32
SSM_GROUP = 16
SSM_STATE = 64
N_STATE = SSM_GROUPS * SSM_STATE
IN_COLS = 2048
PLE_DIM = 256
EPS = 1e-6

ADAM_LR = 0.001
ADAM_B1 = 0.9
ADAM_B2 = 0.999
ADAM_EPS = 1e-08
ADAM_WD = 0.01
ADAM_STEP = 10

FF_BLOCK = 256
N_FF_BLOCKS = D_FF // FF_BLOCK
TOK_TILE_FFN_FWD = 2048
TOK_TILE_FFN_BWD = 1024
TOK_TILE = 512
CHUNK = 256
N_CHUNKS = SEQ // CHUNK
LANE_GROUP = 512
SUBLANES = 8
LANES = 128
MIB = 1024 * 1024

W_NAMES = ['ffn1_norm', 'ffn1_w_gate', 'ffn1_w_up', 'ffn1_w_down', 'mix_norm', 'w_in', 'conv_w', 'conv_b',
           'ssm_A_re', 'ssm_A_im', 'ssm_B_re', 'ssm_B_im', 'ssm_C_re', 'ssm_C_im', 'ssm_D', 'ssm_log_dt',
           'glu_w', 'glu_b', 'conv_out_norm', 'ssm_out_norm', 'w_out', 'ffn2_norm', 'ffn2_w_gate', 'ffn2_w_up',
           'ffn2_w_down', 'ple_norm', 'ple_w_gate', 'ple_w_proj', 'final_norm']
SMALL_NAMES = ['ffn1_norm', 'mix_norm', 'conv_b', 'ssm_A_re', 'ssm_A_im', 'ssm_B_re', 'ssm_B_im', 'ssm_C_re',
               'ssm_C_im', 'ssm_D', 'ssm_log_dt', 'glu_b', 'conv_out_norm', 'ssm_out_norm', 'ffn2_norm',
               'ple_norm', 'final_norm']

SEGS = ((3, 352), (3, 352), (1, 256), (1, 128), (1, 128), (1, 32), (1, 32))
PACK_ROWS = sum(n * r for n, r in SEGS)

MESH = pl.DeviceIdType.MESH
ANY = pl.BlockSpec(memory_space=pl.ANY)


def _cparams(sem=None, vmem_mib=48, **kw):
    return pltpu.CompilerParams(dimension_semantics=sem, vmem_limit_bytes=vmem_mib * MIB, **kw)


def _dot(a, b):
    return jnp.dot(a, b, preferred_element_type=F32)


def _dot_nt(a, b):
    return lax.dot_general(a, b, (((1,), (1,)), ((), ())), preferred_element_type=F32)


def _dot_tn(a, b):
    return lax.dot_general(a, b, (((0,), (0,)), ((), ())), preferred_element_type=F32)


def _rms_stats(x):
    r = lax.rsqrt(jnp.mean(x * x, axis=-1, keepdims=True) + EPS)
    return x * r, r


def _rms_bwd(dy, xh, r, g):
    dxh = dy * g
    dx = r * (dxh - xh * jnp.mean(dxh * xh, axis=-1, keepdims=True))
    dg = jnp.sum(dy * xh, axis=0, keepdims=True)
    return dx, dg


def _sigmoid(x):
    return 0.5 * jnp.tanh(0.5 * x) + 0.5


_GELU_C = math.sqrt(2.0 / math.pi)


def _gelu(x):
    t = jnp.tanh(_GELU_C * (x + 0.044715 * x * x * x))
    return 0.5 * x * (1.0 + t), t


def _gelu_grad(x, t):
    return 0.5 * (1.0 + t) + 0.5 * x * (1.0 - t * t) * _GELU_C * (1.0 + 3.0 * 0.044715 * x * x)


def _accumulate(ref, first, value):
    @pl.when(first)
    def _():
        ref[...] = value

    @pl.when(jnp.logical_not(first))
    def _():
        ref[...] += value


def _ffn_fwd(h, g, w3):
    tm = TOK_TILE_FFN_FWD
    last = N_FF_BLOCKS - 1

    def body(h_ref, g_ref, wgu_ref, wd_ref, wd_last_ref, out_ref, gu_ref, u_ref, a_ref):
        k = pl.program_id(1)

        @pl.when(k == 0)
        def _():
            x = h_ref[...]
            xh, _ = _rms_stats(x)
            u_ref[...] = (xh * g_ref[...]).astype(BF16)
            out_ref[...] = x
            a_ref[1] = jnp.zeros((tm, FF_BLOCK), BF16)

        out_ref[...] += 0.5 * _dot(a_ref[(k + 1) % 2], wd_ref[0])
        gu = _dot_nt(u_ref[...], wgu_ref[...].reshape(2 * FF_BLOCK, D_MODEL))
        gate, up = gu[:, :FF_BLOCK], gu[:, FF_BLOCK:]
        a_ref[k % 2] = (gate * _sigmoid(gate) * up).astype(BF16)
        gu_ref[0] = gate.astype(BF16)
        gu_ref[1] = up.astype(BF16)

        @pl.when(k == last)
        def _():
            out_ref[...] += 0.5 * _dot(a_ref[last % 2], wd_last_ref[0])

    return pl.pallas_call(
        body, name="ffn_fwd",
        grid=(SEQ // tm, N_FF_BLOCKS),
        in_specs=[pl.BlockSpec((tm, D_MODEL), lambda m, k: (m, 0), pipeline_mode=pl.Buffered(1)),
                  pl.BlockSpec((1, D_MODEL), lambda m, k: (0, 0)),
                  pl.BlockSpec((2, FF_BLOCK, D_MODEL), lambda m, k: (0, k, 0)),
                  pl.BlockSpec((1, FF_BLOCK, D_MODEL), lambda m, k: (2, jnp.maximum(k - 1, 0), 0)),
                  pl.BlockSpec((1, FF_BLOCK, D_MODEL), lambda m, k: (2, last, 0), pipeline_mode=pl.Buffered(1))],
        out_specs=[pl.BlockSpec((tm, D_MODEL), lambda m, k: (m, 0)),
                   pl.BlockSpec((2, tm, FF_BLOCK), lambda m, k: (0, m, k))],
        out_shape=[jax.ShapeDtypeStruct((SEQ, D_MODEL), F32),
                   jax.ShapeDtypeStruct((2, SEQ, D_FF), BF16)],
        scratch_shapes=[pltpu.VMEM((tm, D_MODEL), BF16), pltpu.VMEM((2, tm, FF_BLOCK), BF16)],
        compiler_params=_cparams(("parallel", "arbitrary"), 56),
    )(h, g, w3, w3, w3)


def _ffn_bwd_act(h, g, dout, gu, w3):
    tm = TOK_TILE_FFN_BWD
    last = N_FF_BLOCKS - 1

    def body(h_ref, g_ref, d_ref, gu_ref, wd_ref, wgu_ref, wgu_last_ref, dh_ref, dga_ref, ud_ref, dg_ref,
             acc_ref, dgu_ref):
        m = pl.program_id(0)
        k = pl.program_id(1)

        @pl.when(k == 0)
        def _():
            xh, _ = _rms_stats(h_ref[...])
            ud_ref[0] = (xh * g_ref[...]).astype(BF16)
            ud_ref[1] = (0.5 * d_ref[...]).astype(BF16)
            acc_ref[...] = jnp.zeros_like(acc_ref)
            dgu_ref[1] = jnp.zeros((tm, 2 * FF_BLOCK), BF16)

        acc_ref[...] += _dot(dgu_ref[(k + 1) % 2], wgu_ref[...].reshape(2 * FF_BLOCK, D_MODEL))
        gate = gu_ref[0].astype(F32)
        up = gu_ref[1].astype(F32)
        sg = _sigmoid(gate)
        silu = gate * sg
        da = _dot_nt(ud_ref[1], wd_ref[0])
        dgate = (da * up * (sg + silu * (1.0 - sg))).astype(BF16)
        dup = (da * silu).astype(BF16)
        dga_ref[0] = dgate
        dga_ref[1] = dup
        dga_ref[2] = (silu * up).astype(BF16)
        dgu_ref[k % 2, :, 0:FF_BLOCK] = dgate
        dgu_ref[k % 2, :, FF_BLOCK:2 * FF_BLOCK] = dup

        @pl.when(k == last)
        def _():
            du = acc_ref[...] + _dot(dgu_ref[last % 2], wgu_last_ref[...].reshape(2 * FF_BLOCK, D_MODEL))
            xh, r = _rms_stats(h_ref[...])
            dx, dg = _rms_bwd(du, xh, r, g_ref[...])
            dh_ref[...] = d_ref[...] + dx
            _accumulate(dg_ref, m == 0, dg)

    return pl.pallas_call(
        body, name="ffn_bwd_act",
        grid=(SEQ // tm, N_FF_BLOCKS),
        in_specs=[pl.BlockSpec((tm, D_MODEL), lambda m, k: (m, 0), pipeline_mode=pl.Buffered(1)),
                  pl.BlockSpec((1, D_MODEL), lambda m, k: (0, 0)),
                  pl.BlockSpec((tm, D_MODEL), lambda m, k: (m, 0), pipeline_mode=pl.Buffered(1)),
                  pl.BlockSpec((2, tm, FF_BLOCK), lambda m, k: (0, m, k)),
                  pl.BlockSpec((1, FF_BLOCK, D_MODEL), lambda m, k: (2, k, 0)),
                  pl.BlockSpec((2, FF_BLOCK, D_MODEL), lambda m, k: (0, jnp.maximum(k - 1, 0), 0)),
                  pl.BlockSpec((2, FF_BLOCK, D_MODEL), lambda m, k: (0, last, 0), pipeline_mode=pl.Buffered(1))],
        out_specs=[pl.BlockSpec((tm, D_MODEL), lambda m, k: (m, 0)),
                   pl.BlockSpec((3, tm, FF_BLOCK), lambda m, k: (0, m, k)),
                   pl.BlockSpec((2, tm, D_MODEL), lambda m, k: (0, m, 0)),
                   pl.BlockSpec((1, D_MODEL), lambda m, k: (0, 0))],
        out_shape=[jax.ShapeDtypeStruct((SEQ, D_MODEL), F32),
                   jax.ShapeDtypeStruct((3, SEQ, D_FF), BF16),
                   jax.ShapeDtypeStruct((2, SEQ, D_MODEL), BF16),
                   jax.ShapeDtypeStruct((1, D_MODEL), F32)],
        scratch_shapes=[pltpu.VMEM((tm, D_MODEL), F32), pltpu.VMEM((2, tm, 2 * FF_BLOCK), BF16)],
        compiler_params=_cparams(("arbitrary", "arbitrary"), 56),
    )(h, g, dout, gu, w3, w3, w3)


def _matmul_tn(a, b, bm, out_dtype, name, bn=None):
    na, t, m = a.shape
    nb, _, n = b.shape
    bn = n if bn is None else bn

    def body(a_ref, b_ref, o_ref):
        o_ref[0] = _dot_tn(a_ref[0], b_ref[0]).astype(out_dtype)

    return pl.pallas_call(
        body, name=name,
        grid=(na, m // bm, n // bn),
        in_specs=[pl.BlockSpec((1, t, bm), lambda i, k, j: (i, 0, k)),
                  pl.BlockSpec((1, t, bn), lambda i, k, j: (jnp.maximum(i - (na - nb), 0), 0, j))],
        out_specs=pl.BlockSpec((1, bm, bn), lambda i, k, j: (i, k, j)),
        out_shape=jax.ShapeDtypeStruct((na, m, n), out_dtype),
        compiler_params=_cparams(("arbitrary", "parallel", "parallel")),
    )(a, b)


def _inproj_fwd(h, g, wint):
    tm = TOK_TILE

    def body(h_ref, g_ref, w_ref, z_ref):
        xh, _ = _rms_stats(h_ref[...])
        z_ref[...] = _dot_nt((xh * g_ref[...]).astype(BF16), w_ref[...])

    return pl.pallas_call(
        body, name="inproj_fwd",
        grid=(SEQ // tm,),
        in_specs=[pl.BlockSpec((tm, D_MODEL), lambda m: (m, 0)),
                  pl.BlockSpec((1, D_MODEL), lambda m: (0, 0)),
                  pl.BlockSpec((IN_COLS, D_MODEL), lambda m: (0, 0))],
        out_specs=pl.BlockSpec((tm, IN_COLS), lambda m: (m, 0)),
        out_shape=jax.ShapeDtypeStruct((SEQ, IN_COLS), F32),
        compiler_params=_cparams(("parallel",)),
    )(h, g, wint)


def _inproj_bwd(h, g, dh, dz, wint):
    tm = TOK_TILE

    def body(h_ref, g_ref, dh_ref, dz_ref, w_ref, o_ref, u_ref, dg_ref):
        xh, r = _rms_stats(h_ref[...])
        u_ref[0] = (xh * g_ref[...]).astype(BF16)
        dx, dg = _rms_bwd(_dot(dz_ref[...], w_ref[...]), xh, r, g_ref[...])
        o_ref[...] = dh_ref[...] + dx
        _accumulate(dg_ref, pl.program_id(0) == 0, dg)

    return pl.pallas_call(
        body, name="inproj_bwd",
        grid=(SEQ // tm,),
        in_specs=[pl.BlockSpec((tm, D_MODEL), lambda m: (m, 0)),
                  pl.BlockSpec((1, D_MODEL), lambda m: (0, 0)),
                  pl.BlockSpec((tm, D_MODEL), lambda m: (m, 0)),
                  pl.BlockSpec((tm, IN_COLS), lambda m: (m, 0)),
                  pl.BlockSpec((IN_COLS, D_MODEL), lambda m: (0, 0))],
        out_specs=[pl.BlockSpec((tm, D_MODEL), lambda m: (m, 0)),
                   pl.BlockSpec((1, tm, D_MODEL), lambda m: (0, m, 0)),
                   pl.BlockSpec((1, D_MODEL), lambda m: (0, 0))],
        out_shape=[jax.ShapeDtypeStruct((SEQ, D_MODEL), F32),
                   jax.ShapeDtypeStruct((1, SEQ, D_MODEL), BF16),
                   jax.ShapeDtypeStruct((1, D_MODEL), F32)],
        compiler_params=_cparams(("arbitrary",)),
    )(h, g, dh, dz, wint)


def _row_ids(n, w):
    return lax.broadcasted_iota(jnp.int32, (n, w), 0)


def _bcast_row(x, i, n):
    return jnp.broadcast_to(x[i:i + 1, :], (n, x.shape[1]))


def _conv_taps(v, tail):
    n, w = v.shape
    rid = _row_ids(n, w)
    v1 = jnp.where(rid == 0, _bcast_row(tail, 7, n), pltpu.roll(v, 1, 0))
    v2 = jnp.where(rid == 0, _bcast_row(tail, 6, n),
                   jnp.where(rid == 1, _bcast_row(tail, 7, n), pltpu.roll(v, 2, 0)))
    return v1, v2


def _scan_chunk(work, ltab, carry, reverse):
    nblk = CHUNK // SUBLANES
    row = _row_ids(SUBLANES, LANE_GROUP)
    for gi in range(N_STATE // LANE_GROUP):
        cre = pl.ds(gi * LANE_GROUP, LANE_GROUP)
        cim = pl.ds(N_STATE + gi * LANE_GROUP, LANE_GROUP)
        pows = [(ltab[8 * k:8 * k + 8, cre], ltab[8 * k:8 * k + 8, cim]) for k in range(3)]
        pr = ltab[24:32, cre]
        pi = ltab[24:32, cim]

        def blk(i, c, cre=cre, cim=cim, pows=pows, pr=pr, pi=pi):
            cr, ci = c
            b = (nblk - 1 - i) if reverse else i
            r0 = pl.multiple_of(b * SUBLANES, SUBLANES)
            xr = work[pl.ds(r0, SUBLANES), cre]
            xi = work[pl.ds(r0, SUBLANES), cim]
            for k, s in enumerate((1, 2, 4)):
                lr, li = pows[k]
                if reverse:
                    keep = row < SUBLANES - s
                    sr = jnp.where(keep, pltpu.roll(xr, SUBLANES - s, 0), 0.0)
                    si = jnp.where(keep, pltpu.roll(xi, SUBLANES - s, 0), 0.0)
                else:
                    keep = row >= s
                    sr = jnp.where(keep, pltpu.roll(xr, s, 0), 0.0)
                    si = jnp.where(keep, pltpu.roll(xi, s, 0), 0.0)
                xr, xi = xr + lr * sr - li * si, xi + lr * si + li * sr
            xr, xi = xr + pr * cr - pi * ci, xi + pr * ci + pi * cr
            work[pl.ds(r0, SUBLANES), cre] = xr
            work[pl.ds(r0, SUBLANES), cim] = xi
            edge = 0 if reverse else SUBLANES - 1
            return _bcast_row(xr, edge, SUBLANES), _bcast_row(xi, edge, SUBLANES)

        cr, ci = lax.fori_loop(0, nblk, blk, (carry[:, cre], carry[:, cim]))
        carry[:, cre] = cr
        carry[:, cim] = ci


def _s5conv_fwd(z, convw, convb, bbmat, ccmat, dvec, ltab):
    def body(z_ref, cw_ref, cb_ref, bb_ref, cc_ref, d_ref, lt_ref, ya_ref, ys_ref, hs_ref,
             work, carry, tail):
        c = pl.program_id(0)

        @pl.when(c == 0)
        def _():
            carry[...] = jnp.zeros_like(carry)
            tail[...] = jnp.zeros_like(tail)

        zb = z_ref[:, 0:CONV_W]
        v = z_ref[:, CONV_W:2 * CONV_W] * z_ref[:, 2 * CONV_W:3 * CONV_W]
        us = z_ref[:, 3 * CONV_W:4 * CONV_W]
        v1, v2 = _conv_taps(v, tail[...])
        tail[...] = v[CHUNK - 8:CHUNK, :]
        y = cw_ref[0:1, :] * v2 + cw_ref[1:2, :] * v1 + cw_ref[2:3, :] * v
        ya_ref[...] = zb * (y + cb_ref[...])

        work[...] = _dot(us.astype(BF16), bb_ref[...])
        _scan_chunk(work, lt_ref, carry, reverse=False)
        hs = work[...].astype(BF16)
        hs_ref[...] = hs
        ys_ref[...] = _dot_nt(hs, cc_ref[...]) + d_ref[...] * us

    return pl.pallas_call(
        body, name="s5conv_fwd",
        grid=(N_CHUNKS,),
        in_specs=[pl.BlockSpec((CHUNK, IN_COLS), lambda c: (c, 0)),
                  pl.BlockSpec((3, CONV_W), lambda c: (0, 0)),
                  pl.BlockSpec((1, CONV_W), lambda c: (0, 0)),
                  pl.BlockSpec((SSM_W, 2 * N_STATE), lambda c: (0, 0)),
                  pl.BlockSpec((SSM_W, 2 * N_STATE), lambda c: (0, 0)),
                  pl.BlockSpec((1, SSM_W), lambda c: (0, 0)),
                  pl.BlockSpec((32, 2 * N_STATE), lambda c: (0, 0))],
        out_specs=[pl.BlockSpec((CHUNK, CONV_W), lambda c: (c, 0)),
                   pl.BlockSpec((CHUNK, SSM_W), lambda c: (c, 0)),
                   pl.BlockSpec((CHUNK, 2 * N_STATE), lambda c: (c, 0))],
        out_shape=[jax.ShapeDtypeStruct((SEQ, CONV_W), F32),
                   jax.ShapeDtypeStruct((SEQ, SSM_W), F32),
                   jax.ShapeDtypeStruct((SEQ, 2 * N_STATE), BF16)],
        scratch_shapes=[pltpu.VMEM((CHUNK, 2 * N_STATE), F32),
                        pltpu.VMEM((8, 2 * N_STATE), F32),
                        pltpu.VMEM((8, CONV_W), F32)],
        compiler_params=_cparams(("arbitrary",)),
    )(z, convw, convb, bbmat, ccmat, dvec, ltab)


def _s5conv_bwd(z, hs, dya, dys, convw, convb, bbmat, ccmat, dvec, ltab_rev):
    nc = N_CHUNKS
    hb = 16

    def body(z_ref, zp_ref, hs_ref, hp_ref, dya_ref, dys_ref, cw_ref, cb_ref, bb_ref, cc_ref, d_ref, lt_ref,
             dz_ref, g_ref, us_ref, dyb_ref, dl_ref, dcw_ref, work, carry, head):
        i = pl.program_id(0)
        first_chunk = i == nc - 1

        @pl.when(i == 0)
        def _():
            carry[...] = jnp.zeros_like(carry)
            head[...] = jnp.zeros_like(head)
            dl_ref[...] = jnp.zeros_like(dl_ref)
            dcw_ref[...] = jnp.zeros_like(dcw_ref)

        us = z_ref[:, 3 * CONV_W:4 * CONV_W]
        dy = dys_ref[...]
        dy_bf = dy.astype(BF16)
        us_ref[0] = us.astype(BF16)
        dyb_ref[0] = dy_bf

        work[...] = _dot(dy_bf, cc_ref[...])
        _scan_chunk(work, lt_ref, carry, reverse=True)
        gg = work[...]
        gg_bf = gg.astype(BF16)
        g_ref[0] = gg_bf
        dus = d_ref[...] * dy + _dot_nt(gg_bf, bb_ref[...])

        hcur = hs_ref[...].astype(F32)
        hlast = hp_ref[...].astype(F32)[hb - 1:hb, :]
        hlast = jnp.where(first_chunk, 0.0, hlast)
        rid = _row_ids(CHUNK, 2 * N_STATE)
        hprev = jnp.where(rid == 0, jnp.broadcast_to(hlast, (CHUNK, 2 * N_STATE)), pltpu.roll(hcur, 1, 0))
        gr, gi = gg[:, :N_STATE], gg[:, N_STATE:]
        hr, hi = hprev[:, :N_STATE], hprev[:, N_STATE:]
        dl_ref[:, :N_STATE] += (gr * hr + gi * hi).reshape(CHUNK // 8, 8, N_STATE).sum(axis=0)
        dl_ref[:, N_STATE:] += (gi * hr - gr * hi).reshape(CHUNK // 8, 8, N_STATE).sum(axis=0)

        @pl.when(i == nc - 1)
        def _():
            dl_ref[0:1, :] = jnp.sum(dl_ref[...], axis=0, keepdims=True)

        zb = z_ref[:, 0:CONV_W]
        zc = z_ref[:, CONV_W:2 * CONV_W]
        zv = z_ref[:, 2 * CONV_W:3 * CONV_W]
        v = zc * zv
        vtail = jnp.where(first_chunk, 0.0, zp_ref[:, CONV_W:2 * CONV_W] * zp_ref[:, 2 * CONV_W:3 * CONV_W])
        v1, v2 = _conv_taps(v, vtail)
        w0, w1, w2 = cw_ref[0:1, :], cw_ref[1:2, :], cw_ref[2:3, :]
        y = w0 * v2 + w1 * v1 + w2 * v
        dya_v = dya_ref[...]
        dzb = dya_v * (y + cb_ref[...])
        dyc = dya_v * zb
        hd = head[...]
        rc = _row_ids(CHUNK, CONV_W)
        n1 = jnp.where(rc == CHUNK - 1, _bcast_row(hd, 0, CHUNK), pltpu.roll(dyc, CHUNK - 1, 0))
        n2 = jnp.where(rc == CHUNK - 1, _bcast_row(hd, 1, CHUNK),
                       jnp.where(rc == CHUNK - 2, _bcast_row(hd, 0, CHUNK), pltpu.roll(dyc, CHUNK - 2, 0)))
        head[...] = dyc[0:8, :]
        dv = w2 * dyc + w1 * n1 + w0 * n2
        dz_ref[:, 0:CONV_W] = dzb.astype(BF16)
        dz_ref[:, CONV_W:2 * CONV_W] = (dv * zv).astype(BF16)
        dz_ref[:, 2 * CONV_W:3 * CONV_W] = (dv * zc).astype(BF16)
        dz_ref[:, 3 * CONV_W:4 * CONV_W] = dus.astype(BF16)
        dcw_ref[0:1, :] += jnp.sum(dyc * v2, axis=0, keepdims=True)
        dcw_ref[1:2, :] += jnp.sum(dyc * v1, axis=0, keepdims=True)
        dcw_ref[2:3, :] += jnp.sum(dyc * v, axis=0, keepdims=True)
        dcw_ref[3:4, :] += jnp.sum(dyc, axis=0, keepdims=True)
        dcw_ref[4:5, :] += jnp.sum(dy * us, axis=0, keepdims=True)

    rev = lambda i: nc - 1 - i
    return pl.pallas_call(
        body, name="s5conv_bwd",
        grid=(nc,),
        in_specs=[pl.BlockSpec((CHUNK, IN_COLS), lambda i: (rev(i), 0)),
                  pl.BlockSpec((8, IN_COLS), lambda i: (jnp.maximum(rev(i) * (CHUNK // 8) - 1, 0), 0)),
                  pl.BlockSpec((CHUNK, 2 * N_STATE), lambda i: (rev(i), 0)),
                  pl.BlockSpec((hb, 2 * N_STATE), lambda i: (jnp.maximum(rev(i) * (CHUNK // hb) - 1, 0), 0)),
                  pl.BlockSpec((CHUNK, CONV_W), lambda i: (rev(i), 0)),
                  pl.BlockSpec((CHUNK, SSM_W), lambda i: (rev(i), 0)),
                  pl.BlockSpec((3, CONV_W), lambda i: (0, 0)),
                  pl.BlockSpec((1, CONV_W), lambda i: (0, 0)),
                  pl.BlockSpec((SSM_W, 2 * N_STATE), lambda i: (0, 0)),
                  pl.BlockSpec((SSM_W, 2 * N_STATE), lambda i: (0, 0)),
                  pl.BlockSpec((1, SSM_W), lambda i: (0, 0)),
                  pl.BlockSpec((32, 2 * N_STATE), lambda i: (0, 0))],
        out_specs=[pl.BlockSpec((CHUNK, IN_COLS), lambda i: (rev(i), 0)),
                   pl.BlockSpec((1, CHUNK, 2 * N_STATE), lambda i: (0, rev(i), 0)),
                   pl.BlockSpec((1, CHUNK, SSM_W), lambda i: (0, rev(i), 0)),
                   pl.BlockSpec((1, CHUNK, SSM_W), lambda i: (0, rev(i), 0)),
                   pl.BlockSpec((8, 2 * N_STATE), lambda i: (0, 0)),
                   pl.BlockSpec((8, CONV_W), lambda i: (0, 0))],
        out_shape=[jax.ShapeDtypeStruct((SEQ, IN_COLS), BF16),
                   jax.ShapeDtypeStruct((1, SEQ, 2 * N_STATE), BF16),
                   jax.ShapeDtypeStruct((1, SEQ, SSM_W), BF16),
                   jax.ShapeDtypeStruct((1, SEQ, SSM_W), BF16),
                   jax.ShapeDtypeStruct((8, 2 * N_STATE), F32),
                   jax.ShapeDtypeStruct((8, CONV_W), F32)],
        scratch_shapes=[pltpu.VMEM((CHUNK, 2 * N_STATE), F32),
                        pltpu.VMEM((8, 2 * N_STATE), F32),
                        pltpu.VMEM((8, CONV_W), F32)],
        compiler_params=_cparams(("arbitrary",)),
    )(z, z, hs, hs, dya, dys, convw, convb, bbmat, ccmat, dvec, ltab_rev)


def _mix_out_fwd(h, ya, ys, gluw, glub, con, son, wout):
    tm = TOK_TILE

    def body(h_ref, ya_ref, ys_ref, gw_ref, gb_ref, con_ref, son_ref, wo_ref, o_ref):
        zg, _ = _gelu(ys_ref[...])
        q = _dot(zg.astype(BF16), gw_ref[...]) + gb_ref[...]
        out_s = zg * _sigmoid(q)
        na, _ = _rms_stats(ya_ref[...])
        ns, _ = _rms_stats(out_s)
        o_ref[...] = (h_ref[...]
                      + _dot((na * con_ref[...]).astype(BF16), wo_ref[0:CONV_W, :])
                      + _dot((ns * son_ref[...]).astype(BF16), wo_ref[CONV_W:2 * CONV_W, :]))

    row = lambda m: (m, 0)
    fixed = lambda m: (0, 0)
    return pl.pallas_call(
        body, name="mix_out_fwd",
        grid=(SEQ // tm,),
        in_specs=[pl.BlockSpec((tm, D_MODEL), row), pl.BlockSpec((tm, CONV_W), row), pl.BlockSpec((tm, SSM_W), row),
                  pl.BlockSpec((SSM_W, SSM_W), fixed), pl.BlockSpec((1, SSM_W), fixed),
                  pl.BlockSpec((1, CONV_W), fixed), pl.BlockSpec((1, SSM_W), fixed),
                  pl.BlockSpec((D_MODEL, D_MODEL), fixed)],
        out_specs=pl.BlockSpec((tm, D_MODEL), row),
        out_shape=jax.ShapeDtypeStruct((SEQ, D_MODEL), F32),
        compiler_params=_cparams(("parallel",)),
    )(h, ya, ys, gluw, glub, con, son, wout)


def _mix_out_bwd(dh, ya, ys, gluw, glub, con, son, wout):
    tm = TOK_TILE

    def body(dh_ref, ya_ref, ys_ref, gw_ref, gb_ref, con_ref, son_ref, wo_ref,
             dya_ref, dys_ref, yc_ref, dhb_ref, zg_ref, dq_ref, part_ref):
        ysv = ys_ref[...]
        zg, th = _gelu(ysv)
        zg_bf = zg.astype(BF16)
        s = _sigmoid(_dot(zg_bf, gw_ref[...]) + gb_ref[...])
        out_s = zg * s
        na, ra = _rms_stats(ya_ref[...])
        ns, rs = _rms_stats(out_s)
        dh_bf = dh_ref[...].astype(BF16)
        yc_ref[0, :, 0:CONV_W] = (na * con_ref[...]).astype(BF16)
        yc_ref[0, :, CONV_W:2 * CONV_W] = (ns * son_ref[...]).astype(BF16)
        dhb_ref[0] = dh_bf
        dca = _dot_nt(dh_bf, wo_ref[0:CONV_W, :])
        dcs = _dot_nt(dh_bf, wo_ref[CONV_W:2 * CONV_W, :])
        dya, dcon = _rms_bwd(dca, na, ra, con_ref[...])
        dos, dson = _rms_bwd(dcs, ns, rs, son_ref[...])
        dya_ref[...] = dya
        dq = dos * zg * s * (1.0 - s)
        dq_bf = dq.astype(BF16)
        dzg = dos * s + _dot_nt(dq_bf, gw_ref[...])
        dys_ref[...] = dzg * _gelu_grad(ysv, th)
        zg_ref[0] = zg_bf
        dq_ref[0] = dq_bf
        rid = _row_ids(SUBLANES, SSM_W)
        part = jnp.zeros((SUBLANES, SSM_W), F32)
        for i, rowv in enumerate((dcon, dson, jnp.sum(dq, axis=0, keepdims=True))):
            part = jnp.where(rid == i, jnp.broadcast_to(rowv, (SUBLANES, SSM_W)), part)
        _accumulate(part_ref, pl.program_id(0) == 0, part)

    row = lambda m: (m, 0)
    fixed = lambda m: (0, 0)
    lead = lambda m: (0, m, 0)
    return pl.pallas_call(
        body, name="mix_out_bwd",
        grid=(SEQ // tm,),
        in_specs=[pl.BlockSpec((tm, D_MODEL), row), pl.BlockSpec((tm, CONV_W), row), pl.BlockSpec((tm, SSM_W), row),
                  pl.BlockSpec((SSM_W, SSM_W), fixed), pl.BlockSpec((1, SSM_W), fixed),
                  pl.BlockSpec((1, CONV_W), fixed), pl.BlockSpec((1, SSM_W), fixed),
                  pl.BlockSpec((D_MODEL, D_MODEL), fixed)],
        out_specs=[pl.BlockSpec((tm, CONV_W), row), pl.BlockSpec((tm, SSM_W), row),
                   pl.BlockSpec((1, tm, D_MODEL), lead), pl.BlockSpec((1, tm, D_MODEL), lead),
                   pl.BlockSpec((1, tm, SSM_W), lead), pl.BlockSpec((1, tm, SSM_W), lead),
                   pl.BlockSpec((8, SSM_W), fixed)],
        out_shape=[jax.ShapeDtypeStruct((SEQ, CONV_W), F32), jax.ShapeDtypeStruct((SEQ, SSM_W), F32),
                   jax.ShapeDtypeStruct((1, SEQ, D_MODEL), BF16), jax.ShapeDtypeStruct((1, SEQ, D_MODEL), BF16),
                   jax.ShapeDtypeStruct((1, SEQ, SSM_W), BF16), jax.ShapeDtypeStruct((1, SEQ, SSM_W), BF16),
                   jax.ShapeDtypeStruct((8, SSM_W), F32)],
        compiler_params=_cparams(("arbitrary",)),
    )(dh, ya, ys, gluw, glub, con, son, wout)


def _ple_fwd(h, g, p, wgate, wprojt):
    tm = TOK_TILE

    def body(h_ref, g_ref, p_ref, wg_ref, wp_ref, o_ref):
        x = h_ref[...]
        xh, _ = _rms_stats(x)
        s = _sigmoid(_dot((xh * g_ref[...]).astype(BF16), wg_ref[...]))
        o_ref[...] = x + _dot_nt(p_ref[...].astype(BF16), wp_ref[...]) * s

    row = lambda m: (m, 0)
    fixed = lambda m: (0, 0)
    return pl.pallas_call(
        body, name="ple_fwd",
        grid=(SEQ // tm,),
        in_specs=[pl.BlockSpec((tm, D_MODEL), row), pl.BlockSpec((1, D_MODEL), fixed), pl.BlockSpec((tm, PLE_DIM), row),
                  pl.BlockSpec((D_MODEL, D_MODEL), fixed), pl.BlockSpec((D_MODEL, PLE_DIM), fixed)],
        out_specs=pl.BlockSpec((tm, D_MODEL), row),
        out_shape=jax.ShapeDtypeStruct((SEQ, D_MODEL), F32),
        compiler_params=_cparams(("parallel",)),
    )(h, g, p, wgate, wprojt)


def _ple_bwd(h, g, p, dh, wgate, wprojt):
    tm = TOK_TILE

    def body(h_ref, g_ref, p_ref, dh_ref, wg_ref, wp_ref, o_ref, u_ref, dq_ref, dpp_ref, pb_ref, dg_ref):
        xh, r = _rms_stats(h_ref[...])
        u = (xh * g_ref[...]).astype(BF16)
        s = _sigmoid(_dot(u, wg_ref[...]))
        p_bf = p_ref[...].astype(BF16)
        pp = _dot_nt(p_bf, wp_ref[...])
        dhv = dh_ref[...]
        dq = (dhv * pp * s * (1.0 - s)).astype(BF16)
        u_ref[0] = u
        dq_ref[0] = dq
        dpp_ref[0] = (dhv * s).astype(BF16)
        pb_ref[0] = p_bf
        dx, dg = _rms_bwd(_dot_nt(dq, wg_ref[...]), xh, r, g_ref[...])
        o_ref[...] = dhv + dx
        _accumulate(dg_ref, pl.program_id(0) == 0, dg)

    row = lambda m: (m, 0)
    fixed = lambda m: (0, 0)
    lead = lambda m: (0, m, 0)
    big = jax.ShapeDtypeStruct((1, SEQ, D_MODEL), BF16)
    return pl.pallas_call(
        body, name="ple_bwd",
        grid=(SEQ // tm,),
        in_specs=[pl.BlockSpec((tm, D_MODEL), row), pl.BlockSpec((1, D_MODEL), fixed), pl.BlockSpec((tm, PLE_DIM), row),
                  pl.BlockSpec((tm, D_MODEL), row),
                  pl.BlockSpec((D_MODEL, D_MODEL), fixed), pl.BlockSpec((D_MODEL, PLE_DIM), fixed)],
        out_specs=[pl.BlockSpec((tm, D_MODEL), row),
                   pl.BlockSpec((1, tm, D_MODEL), lead), pl.BlockSpec((1, tm, D_MODEL), lead),
                   pl.BlockSpec((1, tm, D_MODEL), lead), pl.BlockSpec((1, tm, PLE_DIM), lead),
                   pl.BlockSpec((1, D_MODEL), fixed)],
        out_shape=[jax.ShapeDtypeStruct((SEQ, D_MODEL), F32), big, big, big,
                   jax.ShapeDtypeStruct((1, SEQ, PLE_DIM), BF16),
                   jax.ShapeDtypeStruct((1, D_MODEL), F32)],
        compiler_params=_cparams(("arbitrary",)),
    )(h, g, p, dh, wgate, wprojt)


def _final_loss(h, g, target):
    tm = TOK_TILE

    def body(h_ref, g_ref, t_ref, loss_ref, dh_ref, dg_ref):
        first = pl.program_id(0) == 0
        xh, r = _rms_stats(h_ref[...])
        diff = xh * g_ref[...] - t_ref[...]
        part = 0.5 * jnp.sum(jnp.mean(diff * diff, axis=-1, keepdims=True), axis=0, keepdims=True)
        _accumulate(loss_ref, first, jnp.broadcast_to(part, (SUBLANES, LANES)))
        dx, dg = _rms_bwd(diff * (1.0 / D_MODEL), xh, r, g_ref[...])
        dh_ref[...] = dx
        _accumulate(dg_ref, first, dg)

    row = lambda m: (m, 0)
    fixed = lambda m: (0, 0)
    return pl.pallas_call(
        body, name="final_loss",
        grid=(SEQ // tm,),
        in_specs=[pl.BlockSpec((tm, D_MODEL), row), pl.BlockSpec((1, D_MODEL), fixed),
                  pl.BlockSpec((tm, D_MODEL), row)],
        out_specs=[pl.BlockSpec((SUBLANES, LANES), fixed),
                   pl.BlockSpec((tm, D_MODEL), row),
                   pl.BlockSpec((1, D_MODEL), fixed)],
        out_shape=[jax.ShapeDtypeStruct((SUBLANES, LANES), F32),
                   jax.ShapeDtypeStruct((SEQ, D_MODEL), F32),
                   jax.ShapeDtypeStruct((1, D_MODEL), F32)],
        compiler_params=_cparams(("arbitrary",)),
    )(h, g, target)


def _disc(ar, ai, ldt):
    dt = jnp.exp(ldt)
    mag = jnp.exp(ar * dt)
    ph = ai * dt
    lr, li = mag * jnp.cos(ph), mag * jnp.sin(ph)
    nr, ni = lr - 1.0, li
    den = ar * ar + ai * ai
    return lr, li, (nr * ar + ni * ai) / den, (ni * ar - nr * ai) / den


def _s5_disc(a_row, ldt_row, a_rep, ldt_rep, bt, ct, tile_e, mask):
    n = N_STATE

    def body(ar_ref, lr_ref, ap_ref, lp_ref, b_ref, c_ref, e_ref, m_ref, lt_ref, ltr_ref, bb_ref, cc_ref):
        lr, li, _, _ = _disc(ar_ref[0], ar_ref[1], lr_ref[...])
        pr, pi = lr, li
        for k in range(1, 9):
            for ref, sgn, edge in ((lt_ref, 1.0, 24 + k - 1), (ltr_ref, -1.0, 24 + 8 - k)):
                if k in (1, 2, 4):
                    r0 = {1: 0, 2: 8, 4: 16}[k]
                    ref[r0:r0 + 8, 0:n] = jnp.broadcast_to(pr, (8, n))
                    ref[r0:r0 + 8, n:2 * n] = jnp.broadcast_to(sgn * pi, (8, n))
                ref[edge:edge + 1, 0:n] = pr
                ref[edge:edge + 1, n:2 * n] = sgn * pi
            pr, pi = pr * lr - pi * li, pr * li + pi * lr
        _, _, fr, fi = _disc(ap_ref[0], ap_ref[1], lp_ref[...])
        br, bi = b_ref[0], b_ref[1]
        e = e_ref[...]
        m = m_ref[...].astype(F32)
        bb_ref[:, 0:n] = (_dot((fr * br - fi * bi).astype(BF16), e) * m).astype(BF16)
        bb_ref[:, n:2 * n] = (_dot((fr * bi + fi * br).astype(BF16), e) * m).astype(BF16)
        cc_ref[:, 0:n] = (_dot(c_ref[0].astype(BF16), e) * m).astype(BF16)
        cc_ref[:, n:2 * n] = (-(_dot(c_ref[1].astype(BF16), e) * m)).astype(BF16)

    return pl.pallas_call(
        body, name="s5_disc",
        out_shape=[jax.ShapeDtypeStruct((32, 2 * n), F32), jax.ShapeDtypeStruct((32, 2 * n), F32),
                   jax.ShapeDtypeStruct((SSM_W, 2 * n), BF16), jax.ShapeDtypeStruct((SSM_W, 2 * n), BF16)],
        compiler_params=_cparams(None),
    )(a_row, ldt_row, a_rep, ldt_rep, bt, ct, tile_e, mask)


def _dot_exact(x, sel):
    hi = x.astype(BF16)
    r1 = x - hi.astype(F32)
    mid = r1.astype(BF16)
    lo = (r1 - mid.astype(F32)).astype(BF16)
    return _dot(hi, sel) + _dot(mid, sel) + _dot(lo, sel)


def _s5_disc_bwd(a, ldt, a_rep, ldt_rep, bt, mask, dl, d_bb, d_cc, fold):
    n = N_STATE

    def body(a_ref, l_ref, ap_ref, lp_ref, b_ref, m_ref, dl_ref, dbb_ref, dcc_ref, f_ref,
             da_ref, dldt_ref, db_ref, dc_ref):
        m = m_ref[...].astype(F32)
        fold_m = f_ref[...]
        diag = lambda x: _dot_exact(x * m, fold_m)
        dr, di = diag(dbb_ref[:, 0:n]), diag(dbb_ref[:, n:2 * n])
        dc_ref[0] = diag(dcc_ref[:, 0:n])
        dc_ref[1] = -diag(dcc_ref[:, n:2 * n])
        _, _, fr, fi = _disc(ap_ref[0], ap_ref[1], lp_ref[...])
        br, bi = b_ref[0], b_ref[1]
        db_ref[0] = fr * dr + fi * di
        db_ref[1] = fr * di - fi * dr
        per_state = lambda x: x.reshape(SSM_GROUPS, SSM_GROUP, SSM_STATE).sum(axis=1)
        dfr = per_state(dr * br + di * bi)
        dfi = per_state(di * br - dr * bi)
        _, vjp = jax.vjp(_disc, a_ref[0], a_ref[1], l_ref[...])
        dar, dai, dldt = vjp((dl_ref[0], dl_ref[1], dfr, dfi))
        da_ref[0] = dar
        da_ref[1] = dai
        dldt_ref[...] = jnp.sum(dldt, axis=1, keepdims=True)

    return pl.pallas_call(
        body, name="s5_disc_bwd",
        out_shape=[jax.ShapeDtypeStruct((2, SSM_GROUPS, SSM_STATE), F32),
                   jax.ShapeDtypeStruct((SSM_GROUPS, 1), F32),
                   jax.ShapeDtypeStruct((2, SSM_W, SSM_STATE), F32),
                   jax.ShapeDtypeStruct((2, SSM_W, SSM_STATE), F32)],
        compiler_params=_cparams(None),
    )(a, ldt, a_rep, ldt_rep, bt, mask, dl, d_bb, d_cc, fold)


def _row_block(rows, cap=512):
    for bm in range(min(cap, rows), 0, -1):
        if rows % bm == 0 and (bm % 8 == 0 or bm == rows):
            return bm
    return rows


def _pair_sum(fulls, got, segs):
    ns = len(segs)
    offs = _seg_offsets(segs)
    _, rtot, c = got.shape
    parts = 2
    pr = rtot // parts
    assert pr * parts == rtot and pr % 16 == 0
    pieces = [[] for _ in range(parts)]
    for a, (n, r) in enumerate(segs):
        for m in range(n):
            lo = offs[a] + m * r
            for h in range(parts):
                clo, chi = max(lo, h * pr), min(lo + r, (h + 1) * pr)
                if chi > clo:
                    pieces[h].append((a, m, clo - lo, clo - h * pr, chi - clo))
    n_sems = max(len(ps) for ps in pieces)

    def body(*refs):
        srcs = refs[:ns]
        got_ref, p32_ref, pbf_ref, own_v, sems = refs[ns:]
        h = pl.program_id(0)
        k = pl.program_id(1)
        dev = 2 * k + lax.axis_index("c")
        for hh in range(parts):
            @pl.when(h == hh)
            def _(hh=hh):
                cps = []
                for i, (a, m, so, do, rows) in enumerate(pieces[hh]):
                    start = pl.multiple_of(dev * segs[a][1] + so, 16)
                    cps.append(pltpu.make_async_copy(srcs[a].at[m, pl.ds(start, rows), :],
                                                     own_v.at[pl.ds(do, rows), :], sems.at[i]))
                for cp in cps:
                    cp.start()
                for cp in cps:
                    cp.wait()
        s = own_v[...].astype(F32) + got_ref[0].astype(F32)
        pbf_ref[0] = s.astype(BF16)

        @pl.when(k == 2 * lax.axis_index("x") + lax.axis_index("y"))
        def _():
            p32_ref[...] = s

    spec = pl.BlockSpec((1, pr, c), lambda h, k: (k, h, 0))
    return pl.pallas_call(
        body, name="pair_sum",
        grid=(parts, 4),
        in_specs=[HBM] * ns + [spec], out_specs=[pl.BlockSpec((pr, c), lambda h, k: (h, 0)), spec],
        out_shape=[jax.ShapeDtypeStruct((rtot, c), F32), jax.ShapeDtypeStruct(got.shape, BF16)],
        scratch_shapes=[pltpu.VMEM((pr, c), BF16), pltpu.SemaphoreType.DMA((n_sems,))],
        compiler_params=_cparams(("arbitrary", "arbitrary")),
    )(*fulls, got)


def _chip_sum(own, rb):
    r, c = own.shape
    bm = _row_block(r)

    def body(o_ref, r_ref, s_ref):
        s_ref[...] = ((o_ref[...] + r_ref[0].astype(F32)) + r_ref[1].astype(F32)) + r_ref[2].astype(F32)

    return pl.pallas_call(
        body, name="chip_sum",
        grid=(r // bm,),
        in_specs=[pl.BlockSpec((bm, c), lambda k: (k, 0)), pl.BlockSpec((3, bm, c), lambda k: (0, k, 0))],
        out_specs=pl.BlockSpec((bm, c), lambda k: (k, 0)),
        out_shape=jax.ShapeDtypeStruct((r, c), F32),
        compiler_params=_cparams(("parallel",)),
    )(own, rb)


def _sum8(x, after):
    _, r, c = x.shape
    bm = _row_block(r)

    def body(x_ref, after_ref, s_ref):
        s = x_ref[0]
        for d in range(1, N_DEV):
            s = s + x_ref[d]
        s_ref[...] = s

    return pl.pallas_call(
        body, name="sum8",
        grid=(r // bm,),
        in_specs=[pl.BlockSpec((N_DEV, bm, c), lambda k: (0, k, 0)), ANY],
        out_specs=pl.BlockSpec((bm, c), lambda k: (k, 0)),
        out_shape=jax.ShapeDtypeStruct((r, c), F32),
        compiler_params=_cparams(("parallel",)),
    )(x, after)


def _adamw(w, g, m, v):
    r, c = w.shape
    bm = _row_block(r)
    bc1 = 1.0 - ADAM_B1 ** ADAM_STEP
    bc2 = 1.0 - ADAM_B2 ** ADAM_STEP

    def body(w_ref, g_ref, m_ref, v_ref, d_ref, nm_ref, nv_ref):
        gv = g_ref[...]
        nm = ADAM_B1 * m_ref[...] + (1.0 - ADAM_B1) * gv
        nv = ADAM_B2 * v_ref[...] + (1.0 - ADAM_B2) * (gv * gv)
        nm_ref[...] = nm
        nv_ref[...] = nv
        d_ref[...] = -ADAM_LR * ((nm / bc1) / (jnp.sqrt(nv / bc2) + ADAM_EPS) + ADAM_WD * w_ref[...])

    spec = pl.BlockSpec((bm, c), lambda k: (k, 0))
    shp = jax.ShapeDtypeStruct((r, c), F32)
    return pl.pallas_call(
        body, name="adamw",
        grid=(r // bm,),
        in_specs=[spec] * 4, out_specs=[spec] * 3, out_shape=[shp] * 3,
        compiler_params=_cparams(("parallel",)),
    )(w, g, m, v)


def _mesh_pos():
    return lax.axis_index("x"), lax.axis_index("y"), lax.axis_index("c")


def _dev_index(p):
    return 4 * p[0] + 2 * p[1] + p[2]


def _seg_offsets(segs):
    offs, o = [], 0
    for n, r in segs:
        offs.append(o)
        o += n * r
    return offs


def _remote(src, dst, send_sem, recv_sem, to):
    return pltpu.make_async_remote_copy(src_ref=src, dst_ref=dst, send_sem=send_sem, recv_sem=recv_sem,
                                        device_id=to, device_id_type=MESH)


def _allgather(pack, segs, name):
    rtot, c = pack.shape
    ns = len(segs)
    offs = _seg_offsets(segs)
    assert rtot == sum(n * r for n, r in segs)

    def body(pack_ref, *refs):
        outs = refs[:ns]
        send_sems, recv_sems, local_sem = refs[ns:]
        x, y, cc = _mesh_pos()
        me, sib = (x, y, cc), (x, y, 1 - cc)
        chips = [(1 - x, y), (x, 1 - y), (1 - x, 1 - y)]

        def pieces(dev, from_pack):
            res = []
            for a, (n, r) in enumerate(segs):
                for m in range(n):
                    dst = outs[a].at[m, pl.ds(pl.multiple_of(dev * r, r), r), :]
                    src = pack_ref.at[pl.ds(offs[a] + m * r, r), :] if from_pack else dst
                    res.append((src, dst))
            return res

        def push(k, dev, to, from_pack):
            for s, d in pieces(dev, from_pack):
                _remote(s, d, send_sems.at[k], recv_sems.at[k], to).start()

        def whole(k):
            return _remote(pack_ref, pack_ref, send_sems.at[k], recv_sems.at[k], me)

        my_dev = _dev_index(me)
        for s, d in pieces(my_dev, True):
            pltpu.make_async_copy(s, d, local_sem).start()
        push(0, my_dev, sib, True)
        for j, chip in enumerate(chips):
            push(1 + j, my_dev, (*chip, cc), True)
        for j, chip in enumerate(chips):
            whole(1 + j).wait_recv()
            push(4 + j, _dev_index((*chip, cc)), sib, False)
        whole(0).wait_recv()
        for j in range(3):
            whole(4 + j).wait_recv()
        for k in range(7):
            whole(k).wait_send()
        pltpu.make_async_copy(pack_ref, pack_ref, local_sem).wait()

    return pl.pallas_call(
        body, name=name,
        in_specs=[HBM], out_specs=[HBM] * ns,
        out_shape=[jax.ShapeDtypeStruct((n, N_DEV * r, c), pack.dtype) for n, r in segs],
        scratch_shapes=[pltpu.SemaphoreType.DMA((7,)), pltpu.SemaphoreType.DMA((7,)), pltpu.SemaphoreType.DMA],
    )(pack)


HBM = pl.BlockSpec(memory_space=pltpu.HBM)
SEM = pl.BlockSpec(memory_space=pltpu.SEMAPHORE)
VMEM_WHOLE = pl.BlockSpec(memory_space=pltpu.VMEM)
EFFECT = pltpu.SideEffectType.DATAFLOW_SIDE_EFFECTING


def _hbm(a):
    return pltpu.with_memory_space_constraint(a, pltpu.HBM)


def _ag_start(pack, segs, after, name):
    rtot, c = pack.shape
    ns = len(segs)
    offs = _seg_offsets(segs)

    def body(pack_ref, *refs):
        lands = refs[:ns]
        send_sems, recv_sems = refs[ns + 1], refs[ns + 2]
        token = refs[-1]
        x, y, cc = _mesh_pos()
        my_dev = _dev_index((x, y, cc))
        targets = [(x, y, 1 - cc), (1 - x, y, cc), (x, 1 - y, cc), (1 - x, 1 - y, cc)]
        for k, to in enumerate(targets):
            for a, (n, r) in enumerate(segs):
                for m in range(n):
                    _remote(pack_ref.at[pl.ds(offs[a] + m * r, r), :],
                            lands[a].at[m, pl.ds(pl.multiple_of(my_dev * r, r), r), :],
                            send_sems.at[k], recv_sems.at[k], to).start()
        token[...] = jnp.zeros_like(token)

    land_shapes = [(n, N_DEV * r, c) for n, r in segs]
    outs = pl.pallas_call(
        body, name=name,
        in_specs=[HBM] * (1 + ns) + [ANY],
        out_specs=[SEM, SEM, HBM] + [HBM] * ns + [VMEM_WHOLE],
        out_shape=[pltpu.SemaphoreType.DMA((4,)), pltpu.SemaphoreType.DMA((4,)), pltpu.HBM(pack.shape, pack.dtype)]
        + [pltpu.HBM(s, pack.dtype) for s in land_shapes] + [jax.ShapeDtypeStruct((SUBLANES, LANES), F32)],
        input_output_aliases={0: 2, **{1 + i: 3 + i for i in range(ns)}},
        compiler_params=pltpu.CompilerParams(has_side_effects=EFFECT),
    )(_hbm(pack), *[_hbm(lax.empty(s, pack.dtype)) for s in land_shapes], after)
    return outs[0], outs[1], outs[2], list(outs[3:3 + ns]), outs[-1]


def _ag_wait(send_sems, recv_sems, pack, lands, after, name):
    ns = len(lands)

    def body(pack_ref, *refs):
        send_ref, recv_ref = refs[ns], refs[ns + 1]
        me = _mesh_pos()
        for k in range(4):
            whole = _remote(pack_ref, pack_ref, send_ref.at[k], recv_ref.at[k], me)
            whole.wait_send()
            whole.wait_recv()

    outs = pl.pallas_call(
        body, name=name,
        in_specs=[HBM] * (1 + ns) + [SEM, SEM, ANY],
        out_specs=[HBM] * (1 + ns),
        out_shape=[pltpu.HBM(pack.shape, pack.dtype)] + [pltpu.HBM(a.shape, a.dtype) for a in lands],
        input_output_aliases={i: i for i in range(1 + ns)},
        compiler_params=pltpu.CompilerParams(has_side_effects=EFFECT),
    )(pack, *lands, send_sems, recv_sems, after)
    return outs[0], list(outs[1:])


def _ag_finish(pack, lands, segs):
    rtot, c = pack.shape
    ns = len(segs)
    offs = _seg_offsets(segs)

    def body(pack_ref, *refs):
        outs = refs[ns:2 * ns]
        stage, send_sems, recv_sems, local_sems = refs[2 * ns:]
        x, y, cc = _mesh_pos()
        me, sib = (x, y, cc), (x, y, 1 - cc)
        chips = [(1 - x, y), (x, 1 - y), (1 - x, 1 - y)]

        def rows(a, m, dev):
            return outs[a].at[m, pl.ds(pl.multiple_of(dev * segs[a][1], segs[a][1]), segs[a][1]), :]

        for j, chip in enumerate(chips):
            dev = _dev_index((*chip, cc))
            for a, (n, r) in enumerate(segs):
                for m in range(n):
                    _remote(rows(a, m, dev), rows(a, m, dev), send_sems.at[j], recv_sems.at[j], sib).start()
        load = pltpu.make_async_copy(pack_ref, stage, local_sems.at[0])
        load.start()
        load.wait()
        my_dev = _dev_index(me)
        for a, (n, r) in enumerate(segs):
            for m in range(n):
                pltpu.make_async_copy(stage.at[pl.ds(offs[a] + m * r, r), :], rows(a, m, my_dev), local_sems.at[1]).start()
        pltpu.make_async_copy(stage, pack_ref, local_sems.at[1]).wait()
        for j in range(3):
            _remote(pack_ref, pack_ref, send_sems.at[j], recv_sems.at[j], me).wait()

    outs = pl.pallas_call(
        body, name="ag_finish",
        in_specs=[HBM] * (1 + ns), out_specs=[HBM] * ns,
        out_shape=[jax.ShapeDtypeStruct(a.shape, a.dtype) for a in lands],
        input_output_aliases={1 + i: i for i in range(ns)},
        scratch_shapes=[pltpu.VMEM((rtot, c), pack.dtype), pltpu.SemaphoreType.DMA((3,)),
                        pltpu.SemaphoreType.DMA((3,)), pltpu.SemaphoreType.DMA((2,))],
        compiler_params=_cparams(None, 16),
    )(pack, *lands)
    return list(outs)


def _rs_chips_start(pbf, after, name):
    _, rtot, c = pbf.shape

    def body(pbf_ref, land_ref, after_ref, send_sems, recv_sems, pbf_thru, land_thru, token):
        x, y, cc = _mesh_pos()
        for j, (cx, cy) in enumerate([(1 - x, y), (x, 1 - y), (1 - x, 1 - y)]):
            _remote(pbf_ref.at[2 * cx + cy], land_ref.at[j], send_sems.at[j], recv_sems.at[j], (cx, cy, cc)).start()
        token[...] = jnp.zeros_like(token)

    return pl.pallas_call(
        body, name=name,
        in_specs=[HBM, HBM, ANY],
        out_specs=[SEM, SEM, HBM, HBM, VMEM_WHOLE],
        out_shape=[pltpu.SemaphoreType.DMA((3,)), pltpu.SemaphoreType.DMA((3,)), pltpu.HBM(pbf.shape, pbf.dtype),
                   pltpu.HBM((3, rtot, c), pbf.dtype), jax.ShapeDtypeStruct((SUBLANES, LANES), F32)],
        input_output_aliases={0: 2, 1: 3},
        compiler_params=pltpu.CompilerParams(has_side_effects=EFFECT),
    )(_hbm(pbf), _hbm(lax.empty((3, rtot, c), pbf.dtype)), after)


def _rs_chips_wait(send_sems, recv_sems, pbf, land, after, name):
    def body(pbf_ref, land_ref, send_ref, recv_ref, after_ref, pbf_out, land_out):
        me = _mesh_pos()
        for j in range(3):
            cp = _remote(pbf_ref.at[0], land_ref.at[j], send_ref.at[j], recv_ref.at[j], me)
            cp.wait_send()
            cp.wait_recv()

    return pl.pallas_call(
        body, name=name,
        in_specs=[HBM, HBM, SEM, SEM, ANY], out_specs=[HBM, HBM],
        out_shape=[pltpu.HBM(pbf.shape, pbf.dtype), pltpu.HBM(land.shape, land.dtype)],
        input_output_aliases={0: 0, 1: 1},
        compiler_params=pltpu.CompilerParams(has_side_effects=EFFECT),
    )(pbf, land, send_sems, recv_sems, after)[1]


def _rs_sibling_start(fulls, segs, name):
    ns = len(segs)
    offs = _seg_offsets(segs)
    rtot = sum(n * r for n, r in segs)
    c = fulls[0].shape[-1]
    dt = fulls[0].dtype

    def body(*refs):
        srcs = refs[:ns]
        land_ref, send_sem, recv_sem = refs[ns], refs[ns + 1], refs[ns + 2]
        token = refs[-1]
        x, y, cc = _mesh_pos()
        for k in range(4):
            for a, (n, r) in enumerate(segs):
                for m in range(n):
                    theirs = srcs[a].at[m, pl.ds(pl.multiple_of((2 * k + 1 - cc) * r, r), r), :]
                    _remote(theirs, land_ref.at[k, pl.ds(offs[a] + m * r, r), :], send_sem, recv_sem,
                            (x, y, 1 - cc)).start()
        token[...] = jnp.zeros_like(token)

    outs = pl.pallas_call(
        body, name=name,
        in_specs=[HBM] * (ns + 1),
        out_specs=[SEM, SEM] + [HBM] * (ns + 1) + [VMEM_WHOLE],
        out_shape=[pltpu.SemaphoreType.DMA(()), pltpu.SemaphoreType.DMA(())]
        + [pltpu.HBM(a.shape, a.dtype) for a in fulls] + [pltpu.HBM((4, rtot, c), dt),
                                                           jax.ShapeDtypeStruct((SUBLANES, LANES), F32)],
        input_output_aliases={i: 2 + i for i in range(ns + 1)},
        compiler_params=pltpu.CompilerParams(has_side_effects=EFFECT),
    )(*[_hbm(a) for a in fulls], _hbm(lax.empty((4, rtot, c), dt)))
    return outs[0], outs[1], list(outs[2:2 + ns]), outs[2 + ns], outs[-1]


def _rs_sibling_wait(send_sem, recv_sem, fulls, land, after, name):
    ns = len(fulls)

    def body(*refs):
        land_ref, send_ref, recv_ref = refs[ns], refs[ns + 1], refs[ns + 2]
        whole = _remote(land_ref, land_ref, send_ref, recv_ref, _mesh_pos())
        whole.wait_send()
        whole.wait_recv()

    outs = pl.pallas_call(
        body, name=name,
        in_specs=[HBM] * (ns + 1) + [SEM, SEM, ANY], out_specs=[HBM] * (ns + 1),
        out_shape=[pltpu.HBM(a.shape, a.dtype) for a in fulls] + [pltpu.HBM(land.shape, land.dtype)],
        input_output_aliases={i: i for i in range(ns + 1)},
        compiler_params=pltpu.CompilerParams(has_side_effects=EFFECT),
    )(*fulls, land, send_sem, recv_sem, after)
    return list(outs[:ns]), outs[ns]


def _tp(w):
    return jnp.swapaxes(w, -1, -2)


def _s5_prepare(a_re, a_im, log_dt, b_re, b_im, c_re, c_im):
    a = jnp.stack([a_re, a_im], axis=1)
    ldt = jnp.broadcast_to(log_dt[:, :, None], (DEPTH, SSM_GROUPS, SSM_STATE))
    a_row = a.reshape(DEPTH, 2, 1, N_STATE)
    ldt_row = ldt.reshape(DEPTH, 1, N_STATE)
    a_rep = jnp.repeat(a, SSM_GROUP, axis=2)
    ldt_rep = jnp.repeat(ldt, SSM_GROUP, axis=1)
    bt = jnp.stack([_tp(b_re), _tp(b_im)], axis=1).reshape(DEPTH, 2, SSM_W, SSM_STATE)
    ct = jnp.stack([c_re, c_im], axis=1).reshape(DEPTH, 2, SSM_W, SSM_STATE)
    tile_e = jnp.tile(jnp.eye(SSM_STATE, dtype=BF16), (1, SSM_GROUPS))
    mask = jnp.repeat(jnp.repeat(jnp.eye(SSM_GROUPS, dtype=BF16), SSM_GROUP, axis=0), SSM_STATE, axis=1)
    out = []
    for l in range(DEPTH):
        tabs = _s5_disc(a_row[l], ldt_row[l], a_rep[l], ldt_rep[l], bt[l], ct[l], tile_e, mask)
        out.append(((a[l], ldt[l], a_rep[l], ldt_rep[l], bt[l], mask), *tabs))
    return out


def _layer_fwd(h, p_l, small, big):
    saved = {'h0': h}
    h, saved['gu1'] = _ffn_fwd(h, small['ffn1_norm'], big['ff1'])
    saved['h1'] = h
    z = _inproj_fwd(h, small['mix_norm'], big['wint'])
    ya, ys, hs = _s5conv_fwd(z, small['conv_w'], small['conv_b'], small['bbmat'], small['ccmat'], small['dvec'],
                             small['ltab'])
    saved.update(z=z, ya=ya, ys=ys, hs=hs)
    h = _mix_out_fwd(h, ya, ys, big['glu'], small['glu_b'], small['conv_out_norm'], small['ssm_out_norm'], big['wout'])
    saved['h2'] = h
    h, saved['gu2'] = _ffn_fwd(h, small['ffn2_norm'], big['ff2'])
    saved['h3'] = h
    h = _ple_fwd(h, small['ple_norm'], p_l, big['plg'], big['plpt'])
    return h, saved


def _ffn_bwd(h_in, g, dh, gu, w3):
    dh_in, dga, ud, dg = _ffn_bwd_act(h_in, g, dh, gu, w3)
    return dh_in, _matmul_tn(dga, ud, FF_BLOCK, BF16, "ffn_wgrad"), dg


def _layer_bwd_top(dh, p_l, small, big, saved):
    gs = {}
    dh, u, dq, dpp, pb, gs['ple_norm'] = _ple_bwd(saved['h3'], small['ple_norm'], p_l, dh, big['plg'], big['plpt'])
    d_plg = _matmul_tn(u, dq, 256, BF16, "ple_gate_wgrad")
    d_plpt = _matmul_tn(dpp, pb, 256, BF16, "ple_proj_wgrad")
    dh, d_ff2, gs['ffn2_norm'] = _ffn_bwd(saved['h2'], small['ffn2_norm'], dh, saved['gu2'], big['ff2'])
    return dh, (gs, d_plg, d_plpt, d_ff2)


def _layer_bwd_rest(dh, top, small, big, saved):
    gs, d_plg, d_plpt, d_ff2 = top
    dya, dys, ycat, dhb, zg, dq, part = _mix_out_bwd(dh, saved['ya'], saved['ys'], big['glu'], small['glu_b'],
                                                     small['conv_out_norm'], small['ssm_out_norm'], big['wout'])
    d_wout = _matmul_tn(ycat, dhb, 256, BF16, "w_out_wgrad")
    d_glu = _matmul_tn(zg, dq, 256, BF16, "glu_wgrad")
    dz, gadj, us, dyb, dl, dcw = _s5conv_bwd(saved['z'], saved['hs'], dya, dys, small['conv_w'], small['conv_b'],
                                             small['bbmat'], small['ccmat'], small['dvec'], small['ltab_rev'])
    d_bb = _matmul_tn(us, gadj, SSM_W, F32, "s5_b_wgrad", bn=1024)[0]
    d_cc = _matmul_tn(dyb, saved['hs'][None], SSM_W, F32, "s5_c_wgrad", bn=1024)[0]
    dh, u, gs['mix_norm'] = _inproj_bwd(saved['h1'], small['mix_norm'], dh, dz, big['wint'])
    d_wint = _matmul_tn(dz[None], u, 256, BF16, "w_in_wgrad")
    dh, d_ff1, gs['ffn1_norm'] = _ffn_bwd(saved['h0'], small['ffn1_norm'], dh, saved['gu1'], big['ff1'])

    dlb = dl[0].reshape(2, SSM_GROUPS, SSM_STATE)
    fold = jnp.tile(jnp.eye(SSM_STATE, dtype=BF16), (SSM_GROUPS, 1))
    da, dldt, dbt, dct = _s5_disc_bwd(*small['disc_in'], dlb, d_bb, d_cc, fold)
    gs['ssm_A_re'], gs['ssm_A_im'] = da[0], da[1]
    gs['ssm_log_dt'] = dldt[:, 0]
    ghp = (SSM_GROUPS, SSM_GROUP, SSM_STATE)
    gs['ssm_B_re'], gs['ssm_B_im'] = dbt[0].reshape(ghp), dbt[1].reshape(ghp)
    gs['ssm_C_re'], gs['ssm_C_im'] = dct[0].reshape(ghp), dct[1].reshape(ghp)
    gs['conv_w'] = dcw[0:3]
    gs['conv_b'] = dcw[3]
    gs['ssm_D'] = dcw[4].reshape(SSM_GROUPS, SSM_GROUP)
    gs['conv_out_norm'], gs['ssm_out_norm'], gs['glu_b'] = part[0], part[1], part[2]
    for n in ('ple_norm', 'ffn2_norm', 'mix_norm', 'ffn1_norm'):
        gs[n] = gs[n][0]
    fulls = [d_ff1, d_ff2, d_wint, d_wout, d_plg,
             d_plpt.reshape(1, D_MODEL * PLE_DIM // D_MODEL, D_MODEL), d_glu.reshape(1, SSM_W * SSM_W // D_MODEL, D_MODEL)]
    return dh, fulls, gs


VIEW_T = ('ffn1_w_gate', 'ffn1_w_up', 'ffn2_w_gate', 'ffn2_w_up', 'ssm_B_re', 'ssm_B_im')


def _view(name, a):
    return _tp(a) if name in VIEW_T else a


def _layer_pack(W, l):
    return jnp.concatenate([
        _tp(W['ffn1_w_gate'][l]), _tp(W['ffn1_w_up'][l]), W['ffn1_w_down'][l],
        _tp(W['ffn2_w_gate'][l]), _tp(W['ffn2_w_up'][l]), W['ffn2_w_down'][l],
        _tp(W['w_in'][l]), W['w_out'][l], W['ple_w_gate'][l],
        _tp(W['ple_w_proj'][l]).reshape(-1, D_MODEL), W['glu_w'][l].reshape(-1, D_MODEL)], axis=0).astype(BF16)


def _pad_rows(flat, mult, width=LANES):
    per = mult * width
    n = flat.shape[0]
    tot = -(-n // per) * per
    return jnp.pad(flat, (0, tot - n)).reshape(tot // width, width)


def _adamw_any(w, g, m, v):
    shp = w.shape
    two = (lambda t: t.reshape(-1, shp[-1]))
    d, nm, nv = _adamw(two(w), two(g), two(m), two(v))
    return d.reshape(shp), nm.reshape(shp), nv.reshape(shp)


def kernel(x, p, ffn1_norm, ffn1_w_gate, ffn1_w_up, ffn1_w_down, mix_norm, w_in, conv_w, conv_b, ssm_A_re, ssm_A_im, ssm_B_re, ssm_B_im, ssm_C_re, ssm_C_im, ssm_D, ssm_log_dt, glu_w, glu_b, conv_out_norm, ssm_out_norm, w_out, ffn2_norm, ffn2_w_gate, ffn2_w_up, ffn2_w_down, ple_norm, ple_w_gate, ple_w_proj, final_norm, loss_target, m_ffn1_norm, m_ffn1_w_gate, m_ffn1_w_up, m_ffn1_w_down, m_mix_norm, m_w_in, m_conv_w, m_conv_b, m_ssm_A_re, m_ssm_A_im, m_ssm_B_re, m_ssm_B_im, m_ssm_C_re, m_ssm_C_im, m_ssm_D, m_ssm_log_dt, m_glu_w, m_glu_b, m_conv_out_norm, m_ssm_out_norm, m_w_out, m_ffn2_norm, m_ffn2_w_gate, m_ffn2_w_up, m_ffn2_w_down, m_ple_norm, m_ple_w_gate, m_ple_w_proj, m_final_norm, v_ffn1_norm, v_ffn1_w_gate, v_ffn1_w_up, v_ffn1_w_down, v_mix_norm, v_w_in, v_conv_w, v_conv_b, v_ssm_A_re, v_ssm_A_im, v_ssm_B_re, v_ssm_B_im, v_ssm_C_re, v_ssm_C_im, v_ssm_D, v_ssm_log_dt, v_glu_w, v_glu_b, v_conv_out_norm, v_ssm_out_norm, v_w_out, v_ffn2_norm, v_ffn2_w_gate, v_ffn2_w_up, v_ffn2_w_down, v_ple_norm, v_ple_w_gate, v_ple_w_proj, v_final_norm):
    given = dict(locals())
    W = {n: given[n] for n in W_NAMES}
    M = {n: given['m_' + n] for n in W_NAMES}
    V = {n: given['v_' + n] for n in W_NAMES}
    Wv, Mv, Vv = [{n: _view(n, d[n]) for n in W_NAMES} for d in (W, M, V)]
    my_dev = _dev_index(_mesh_pos())

    conv_shard = _pad_rows(W['conv_w'].reshape(-1), SUBLANES)
    conv_all = _allgather(conv_shard, ((1, SUBLANES),), "ag_conv_w")[0]
    conv_full = conv_all.reshape(N_DEV, -1)[:, :DEPTH * 3 * (CONV_W // N_DEV)]
    conv_full = conv_full.reshape(N_DEV, DEPTH, 3, CONV_W // N_DEV).transpose(1, 2, 0, 3).reshape(DEPTH, 3, CONV_W)
    packs = [_layer_pack(W, 0)]
    flight = _ag_start(packs[0], SEGS, conv_all, "ag_start_0")
    s5 = _s5_prepare(*[W[n] + flight[4][0, 0] for n in ('ssm_A_re', 'ssm_A_im', 'ssm_log_dt')],
                     *[W[n] for n in ('ssm_B_re', 'ssm_B_im', 'ssm_C_re', 'ssm_C_im')])

    packs += [_layer_pack(W, l) for l in range(1, DEPTH)]
    prepared = conv_full[0, 0:1, 0:1] + s5[DEPTH - 1][1][0:1, 0:1] + packs[DEPTH - 1][0:1, 0:1].astype(F32)

    smalls, saves, bigs = [], [], []
    h = x[0]
    for l in range(DEPTH):
        send_sems, recv_sems, pack_thru, lands, _ = flight
        pack_thru, lands = _ag_wait(send_sems, recv_sems, pack_thru, lands, prepared if l == 0 else h,
                                    "ag_wait_%d" % l)
        token = jnp.zeros((1, 1), F32)
        if l + 1 < DEPTH:
            flight = _ag_start(packs[l + 1], SEGS, lands[0], "ag_start_%d" % (l + 1))
            token = flight[4][0:1, 0:1]
        ff1, ff2, wint, wout, plg, plpt, glu = _ag_finish(pack_thru, lands, SEGS)
        bigs.append(dict(ff1=ff1, ff2=ff2, wint=wint[0], wout=wout[0], plg=plg[0],
                         plpt=plpt.reshape(D_MODEL, PLE_DIM), glu=glu.reshape(SSM_W, SSM_W)))
        small = {n: W[n][l][None] for n in ('ffn1_norm', 'mix_norm', 'conv_b', 'glu_b', 'conv_out_norm',
                                            'ssm_out_norm', 'ffn2_norm', 'ple_norm')}
        small['ffn1_norm'] = small['ffn1_norm'] + token
        small['conv_w'] = conv_full[l]
        small['dvec'] = W['ssm_D'][l].reshape(1, SSM_W)
        small['disc_in'], small['ltab'], small['ltab_rev'], small['bbmat'], small['ccmat'] = s5[l]
        h, saved = _layer_fwd(h, p[l, 0], small, bigs[l])
        smalls.append(small)
        saves.append(saved)
    loss_tile, dh, d_final = _final_loss(h, W['final_norm'][None], loss_target[0])
    loss = lax.psum(loss_tile[0, 0], ("x", "y", "c"))

    layer_gs = [None] * DEPTH
    shard_grads = [None] * DEPTH
    zero = jnp.zeros((1, 1), F32)
    sib, ici = None, None

    def finish_sibling(after_sib, after_ici):
        nonlocal sib, ici
        up, (send_sem, recv_sem, fulls_thru, land, _) = sib
        fulls_thru, got = _rs_sibling_wait(send_sem, recv_sem, fulls_thru, land, after_sib, "sib_wait_%d" % up)
        own32, pbf = _pair_sum(fulls_thru, got, SEGS)
        finish_chips(pbf)
        ici = (up, _rs_chips_start(pbf, after_ici, "rs_start_%d" % up), own32)
        sib = None

    def finish_chips(after):
        nonlocal ici
        if ici is not None:
            up, (send_sems, recv_sems, pbf_thru, land, _), own32 = ici
            got3 = _rs_chips_wait(send_sems, recv_sems, pbf_thru, land, after, "rs_wait_%d" % up)
            shard_grads[up] = _chip_sum(own32, got3)
            ici = None

    for l in reversed(range(DEPTH)):
        small = dict(smalls[l])
        if sib is not None:
            small['ple_norm'] = small['ple_norm'] + sib[1][4][0:1, 0:1]
        dh, top = _layer_bwd_top(dh, p[l, 0], small, bigs[l], saves[l])
        if sib is not None:
            finish_sibling(dh, dh)
            small['glu_b'] = small['glu_b'] + ici[1][4][0:1, 0:1]
        dh, fulls, layer_gs[l] = _layer_bwd_rest(dh, top, small, bigs[l], saves[l])
        sib = (l, _rs_sibling_start(fulls, SEGS, "sib_start_%d" % l))
    grad_x = dh[None]

    gs = {n: jnp.stack([layer_gs[l][n] for l in range(DEPTH)]) for n in layer_gs[0]}
    gs['final_norm'] = d_final[0]
    flat = jnp.concatenate([gs[n].reshape(-1) for n in SMALL_NAMES] + [gs['conv_w'].reshape(-1)])
    n_flat = flat.shape[0]
    flat = _pad_rows(flat, 64, D_MODEL) + sib[1][4][0:1, 0:1]
    rows = flat.shape[0]
    gathered = _allgather(flat, ((1, rows),), "ag_small_grads")[0]
    finish_sibling(gathered, gathered)
    red = _sum8(gathered.reshape(N_DEV, rows, D_MODEL), ici[1][4]).reshape(-1)[:n_flat]
    finish_chips(red)
    G = {}
    o = 0
    for n in SMALL_NAMES:
        G[n] = red[o:o + W[n].size].reshape(Wv[n].shape)
        o += W[n].size
    conv_g_full = red[o:].reshape(DEPTH, 3, CONV_W)
    G['conv_w'] = lax.dynamic_slice_in_dim(conv_g_full, my_dev * (CONV_W // N_DEV), CONV_W // N_DEV, axis=2)

    sg = jnp.stack(shard_grads)
    offs = _seg_offsets(SEGS)
    r = SEGS[0][1]
    for a, f in ((0, 'ffn1'), (1, 'ffn2')):
        G[f + '_w_gate'] = sg[:, offs[a]:offs[a] + r]
        G[f + '_w_up'] = sg[:, offs[a] + r:offs[a] + 2 * r]
        G[f + '_w_down'] = sg[:, offs[a] + 2 * r:offs[a] + 3 * r]
    G['w_in'] = _tp(sg[:, offs[2]:offs[2] + SEGS[2][1]])
    G['w_out'] = sg[:, offs[3]:offs[3] + SEGS[3][1]]
    G['ple_w_gate'] = sg[:, offs[4]:offs[4] + SEGS[4][1]]
    G['ple_w_proj'] = _tp(sg[:, offs[5]:offs[5] + SEGS[5][1]].reshape(DEPTH, D_MODEL // N_DEV, PLE_DIM))
    G['glu_w'] = sg[:, offs[6]:offs[6] + SEGS[6][1]].reshape(DEPTH, SSM_W // N_DEV, SSM_W)

    delta, new_m, new_v = {}, {}, {}
    cat = lambda src: _pad_rows(jnp.concatenate([src[n].reshape(-1) for n in SMALL_NAMES]), SUBLANES, D_MODEL)
    d_s, m_s, v_s = _adamw(cat(Wv), cat(G), cat(Mv), cat(Vv))
    o = 0
    for n in SMALL_NAMES:
        for dst, src in ((delta, d_s), (new_m, m_s), (new_v, v_s)):
            dst[n] = src.reshape(-1)[o:o + W[n].size].reshape(Wv[n].shape)
        o += W[n].size
    for n in W_NAMES:
        if n not in delta:
            delta[n], new_m[n], new_v[n] = _adamw_any(Wv[n], G[n], Mv[n], Vv[n])

    outs = [[_view(n, d[n]) for n in W_NAMES] for d in (G, delta, new_m, new_v)]
    return (loss, grad_x, *outs[0], *outs[1], *outs[2], *outs[3])
```

```python
import math

import jax
import jax.numpy as jnp
from jax import lax
from jax.experimental import pallas as pl
from jax.experimental.pallas import tpu as pltpu

F32 = jnp.float32
BF16 = jnp.bfloat16

N_DEV = 8
DEPTH = 4
SEQ = 2048
D_MODEL = 1024
D_FF = 2816
CONV_W = 512
SSM_W = 512
SSM_GROUPS = 32
SSM_GROUP = 16
SSM_STATE = 64
N_STATE = SSM_GROUPS * SSM_STATE
IN_COLS = 2048
PLE_DIM = 256
EPS = 1e-6

ADAM_LR = 0.001
ADAM_B1 = 0.9
ADAM_B2 = 0.999
ADAM_EPS = 1e-08
ADAM_WD = 0.01
ADAM_STEP = 10

FF_BLOCK = 256
N_FF_BLOCKS = D_FF // FF_BLOCK
TOK_TILE_FFN_FWD = 2048
TOK_TILE_FFN_BWD = 1024
TOK_TILE = 512
CHUNK = 256
N_CHUNKS = SEQ // CHUNK
LANE_GROUP = 512
SUBLANES = 8
LANES = 128
MIB = 1024 * 1024

W_NAMES = ['ffn1_norm', 'ffn1_w_gate', 'ffn1_w_up', 'ffn1_w_down', 'mix_norm', 'w_in', 'conv_w', 'conv_b',
           'ssm_A_re', 'ssm_A_im', 'ssm_B_re', 'ssm_B_im', 'ssm_C_re', 'ssm_C_im', 'ssm_D', 'ssm_log_dt',
           'glu_w', 'glu_b', 'conv_out_norm', 'ssm_out_norm', 'w_out', 'ffn2_norm', 'ffn2_w_gate', 'ffn2_w_up',
           'ffn2_w_down', 'ple_norm', 'ple_w_gate', 'ple_w_proj', 'final_norm']
SMALL_NAMES = ['ffn1_norm', 'mix_norm', 'conv_b', 'ssm_A_re', 'ssm_A_im', 'ssm_B_re', 'ssm_B_im', 'ssm_C_re',
               'ssm_C_im', 'ssm_D', 'ssm_log_dt', 'glu_b', 'conv_out_norm', 'ssm_out_norm', 'ffn2_norm',
               'ple_norm', 'final_norm']

SEGS = ((3, 352), (3, 352), (1, 256), (1, 128), (1, 128), (1, 32), (1, 32))
PACK_ROWS = sum(n * r for n, r in SEGS)

MESH = pl.DeviceIdType.MESH
ANY = pl.BlockSpec(memory_space=pl.ANY)


def _in_hbm(*arrays):
    return [pltpu.with_memory_space_constraint(a, pltpu.HBM) for a in arrays]


def _cparams(sem=None, vmem_mib=48, **kw):
    return pltpu.CompilerParams(dimension_semantics=sem, vmem_limit_bytes=vmem_mib * MIB, **kw)


def _dot(a, b):
    return jnp.dot(a, b, preferred_element_type=F32)


def _dot_nt(a, b):
    return lax.dot_general(a, b, (((1,), (1,)), ((), ())), preferred_element_type=F32)


def _dot_tn(a, b):
    return lax.dot_general(a, b, (((0,), (0,)), ((), ())), preferred_element_type=F32)


def _rms_stats(x):
    r = lax.rsqrt(jnp.mean(x * x, axis=-1, keepdims=True) + EPS)
    return x * r, r


def _rms_bwd(dy, xh, r, g):
    dxh = dy * g
    dx = r * (dxh - xh * jnp.mean(dxh * xh, axis=-1, keepdims=True))
    dg = jnp.sum(dy * xh, axis=0, keepdims=True)
    return dx, dg


def _sigmoid(x):
    return 0.5 * jnp.tanh(0.5 * x) + 0.5


_GELU_C = math.sqrt(2.0 / math.pi)


def _gelu(x):
    t = jnp.tanh(_GELU_C * (x + 0.044715 * x * x * x))
    return 0.5 * x * (1.0 + t), t


def _gelu_grad(x, t):
    return 0.5 * (1.0 + t) + 0.5 * x * (1.0 - t * t) * _GELU_C * (1.0 + 3.0 * 0.044715 * x * x)


def _accumulate(ref, first, value):
    @pl.when(first)
    def _():
        ref[...] = value

    @pl.when(jnp.logical_not(first))
    def _():
        ref[...] += value


def _ffn_fwd(h, g, w3):
    tm = TOK_TILE_FFN_FWD
    last = N_FF_BLOCKS - 1

    def body(h_ref, g_ref, wgu_ref, wd_ref, wd_last_ref, out_ref, gu_ref, u_ref, a_ref):
        k = pl.program_id(1)

        @pl.when(k == 0)
        def _():
            x = h_ref[...]
            xh, _ = _rms_stats(x)
            u_ref[...] = (xh * g_ref[...]).astype(BF16)
            out_ref[...] = x
            a_ref[1] = jnp.zeros((tm, FF_BLOCK), BF16)

        out_ref[...] += 0.5 * _dot(a_ref[(k + 1) % 2], wd_ref[0])
        gu = _dot_nt(u_ref[...], wgu_ref[...].reshape(2 * FF_BLOCK, D_MODEL))
        gate, up = gu[:, :FF_BLOCK], gu[:, FF_BLOCK:]
        a_ref[k % 2] = (gate * _sigmoid(gate) * up).astype(BF16)
        gu_ref[0] = gate.astype(BF16)
        gu_ref[1] = up.astype(BF16)

        @pl.when(k == last)
        def _():
            out_ref[...] += 0.5 * _dot(a_ref[last % 2], wd_last_ref[0])

    return pl.pallas_call(
        body, name="ffn_fwd",
        grid=(SEQ // tm, N_FF_BLOCKS),
        in_specs=[pl.BlockSpec((tm, D_MODEL), lambda m, k: (m, 0), pipeline_mode=pl.Buffered(1)),
                  pl.BlockSpec((1, D_MODEL), lambda m, k: (0, 0)),
                  pl.BlockSpec((2, FF_BLOCK, D_MODEL), lambda m, k: (0, k, 0)),
                  pl.BlockSpec((1, FF_BLOCK, D_MODEL), lambda m, k: (2, jnp.maximum(k - 1, 0), 0)),
                  pl.BlockSpec((1, FF_BLOCK, D_MODEL), lambda m, k: (2, last, 0), pipeline_mode=pl.Buffered(1))],
        out_specs=[pl.BlockSpec((tm, D_MODEL), lambda m, k: (m, 0)),
                   pl.BlockSpec((2, tm, FF_BLOCK), lambda m, k: (0, m, k))],
        out_shape=[jax.ShapeDtypeStruct((SEQ, D_MODEL), F32),
                   jax.ShapeDtypeStruct((2, SEQ, D_FF), BF16)],
        scratch_shapes=[pltpu.VMEM((tm, D_MODEL), BF16), pltpu.VMEM((2, tm, FF_BLOCK), BF16)],
        compiler_params=_cparams(("parallel", "arbitrary"), 56),
    )(*_in_hbm(h, g, w3, w3, w3))


def _ffn_bwd_act(h, g, dout, gu, w3):
    tm = TOK_TILE_FFN_BWD
    last = N_FF_BLOCKS - 1

    def body(h_ref, g_ref, d_ref, gu_ref, wd_ref, wgu_ref, wgu_last_ref, dh_ref, dga_ref, ud_ref, dg_ref,
             acc_ref, dgu_ref):
        m = pl.program_id(0)
        k = pl.program_id(1)

        @pl.when(k == 0)
        def _():
            xh, _ = _rms_stats(h_ref[...])
            ud_ref[0] = (xh * g_ref[...]).astype(BF16)
            ud_ref[1] = (0.5 * d_ref[...]).astype(BF16)
            acc_ref[...] = jnp.zeros_like(acc_ref)
            dgu_ref[1] = jnp.zeros((tm, 2 * FF_BLOCK), BF16)

        acc_ref[...] += _dot(dgu_ref[(k + 1) % 2], wgu_ref[...].reshape(2 * FF_BLOCK, D_MODEL))
        gate = gu_ref[0].astype(F32)
        up = gu_ref[1].astype(F32)
        sg = _sigmoid(gate)
        silu = gate * sg
        da = _dot_nt(ud_ref[1], wd_ref[0])
        dgate = (da * up * (sg + silu * (1.0 - sg))).astype(BF16)
        dup = (da * silu).astype(BF16)
        dga_ref[0] = dgate
        dga_ref[1] = dup
        dga_ref[2] = (silu * up).astype(BF16)
        dgu_ref[k % 2, :, 0:FF_BLOCK] = dgate
        dgu_ref[k % 2, :, FF_BLOCK:2 * FF_BLOCK] = dup

        @pl.when(k == last)
        def _():
            du = acc_ref[...] + _dot(dgu_ref[last % 2], wgu_last_ref[...].reshape(2 * FF_BLOCK, D_MODEL))
            xh, r = _rms_stats(h_ref[...])
            dx, dg = _rms_bwd(du, xh, r, g_ref[...])
            dh_ref[...] = d_ref[...] + dx
            _accumulate(dg_ref, m == 0, dg)

    return pl.pallas_call(
        body, name="ffn_bwd_act",
        grid=(SEQ // tm, N_FF_BLOCKS),
        in_specs=[pl.BlockSpec((tm, D_MODEL), lambda m, k: (m, 0), pipeline_mode=pl.Buffered(1)),
                  pl.BlockSpec((1, D_MODEL), lambda m, k: (0, 0)),
                  pl.BlockSpec((tm, D_MODEL), lambda m, k: (m, 0), pipeline_mode=pl.Buffered(1)),
                  pl.BlockSpec((2, tm, FF_BLOCK), lambda m, k: (0, m, k)),
                  pl.BlockSpec((1, FF_BLOCK, D_MODEL), lambda m, k: (2, k, 0)),
                  pl.BlockSpec((2, FF_BLOCK, D_MODEL), lambda m, k: (0, jnp.maximum(k - 1, 0), 0)),
                  pl.BlockSpec((2, FF_BLOCK, D_MODEL), lambda m, k: (0, last, 0), pipeline_mode=pl.Buffered(1))],
        out_specs=[pl.BlockSpec((tm, D_MODEL), lambda m, k: (m, 0)),
                   pl.BlockSpec((3, tm, FF_BLOCK), lambda m, k: (0, m, k)),
                   pl.BlockSpec((2, tm, D_MODEL), lambda m, k: (0, m, 0)),
                   pl.BlockSpec((1, D_MODEL), lambda m, k: (0, 0))],
        out_shape=[jax.ShapeDtypeStruct((SEQ, D_MODEL), F32),
                   jax.ShapeDtypeStruct((3, SEQ, D_FF), BF16),
                   jax.ShapeDtypeStruct((2, SEQ, D_MODEL), BF16),
                   jax.ShapeDtypeStruct((1, D_MODEL), F32)],
        scratch_shapes=[pltpu.VMEM((tm, D_MODEL), F32), pltpu.VMEM((2, tm, 2 * FF_BLOCK), BF16)],
        compiler_params=_cparams(("arbitrary", "arbitrary"), 56),
    )(*_in_hbm(h, g, dout, gu, w3, w3, w3))


def _matmul_tn(a, b, bm, out_dtype, name, bn=None):
    na, t, m = a.shape
    nb, _, n = b.shape
    bn = n if bn is None else bn

    def body(a_ref, b_ref, o_ref):
        o_ref[0] = _dot_tn(a_ref[0], b_ref[0]).astype(out_dtype)

    return pl.pallas_call(
        body, name=name,
        grid=(na, m // bm, n // bn),
        in_specs=[pl.BlockSpec((1, t, bm), lambda i, k, j: (i, 0, k)),
                  pl.BlockSpec((1, t, bn), lambda i, k, j: (jnp.maximum(i - (na - nb), 0), 0, j))],
        out_specs=pl.BlockSpec((1, bm, bn), lambda i, k, j: (i, k, j)),
        out_shape=jax.ShapeDtypeStruct((na, m, n), out_dtype),
        compiler_params=_cparams(("arbitrary", "parallel", "parallel")),
    )(*_in_hbm(a, b))


def _inproj_fwd(h, g, wint):
    tm = TOK_TILE

    def body(h_ref, g_ref, w_ref, z_ref):
        xh, _ = _rms_stats(h_ref[...])
        z_ref[...] = _dot_nt((xh * g_ref[...]).astype(BF16), w_ref[...])

    return pl.pallas_call(
        body, name="inproj_fwd",
        grid=(SEQ // tm,),
        in_specs=[pl.BlockSpec((tm, D_MODEL), lambda m: (m, 0)),
                  pl.BlockSpec((1, D_MODEL), lambda m: (0, 0)),
                  pl.BlockSpec((IN_COLS, D_MODEL), lambda m: (0, 0))],
        out_specs=pl.BlockSpec((tm, IN_COLS), lambda m: (m, 0)),
        out_shape=jax.ShapeDtypeStruct((SEQ, IN_COLS), F32),
        compiler_params=_cparams(("parallel",)),
    )(*_in_hbm(h, g, wint))


def _inproj_bwd(h, g, dh, dz, wint):
    tm = TOK_TILE

    def body(h_ref, g_ref, dh_ref, dz_ref, w_ref, o_ref, u_ref, dg_ref):
        xh, r = _rms_stats(h_ref[...])
        u_ref[0] = (xh * g_ref[...]).astype(BF16)
        dx, dg = _rms_bwd(_dot(dz_ref[...], w_ref[...]), xh, r, g_ref[...])
        o_ref[...] = dh_ref[...] + dx
        _accumulate(dg_ref, pl.program_id(0) == 0, dg)

    return pl.pallas_call(
        body, name="inproj_bwd",
        grid=(SEQ // tm,),
        in_specs=[pl.BlockSpec((tm, D_MODEL), lambda m: (m, 0)),
                  pl.BlockSpec((1, D_MODEL), lambda m: (0, 0)),
                  pl.BlockSpec((tm, D_MODEL), lambda m: (m, 0)),
                  pl.BlockSpec((tm, IN_COLS), lambda m: (m, 0)),
                  pl.BlockSpec((IN_COLS, D_MODEL), lambda m: (0, 0))],
        out_specs=[pl.BlockSpec((tm, D_MODEL), lambda m: (m, 0)),
                   pl.BlockSpec((1, tm, D_MODEL), lambda m: (0, m, 0)),
                   pl.BlockSpec((1, D_MODEL), lambda m: (0, 0))],
        out_shape=[jax.ShapeDtypeStruct((SEQ, D_MODEL), F32),
                   jax.ShapeDtypeStruct((1, SEQ, D_MODEL), BF16),
                   jax.ShapeDtypeStruct((1, D_MODEL), F32)],
        compiler_params=_cparams(("arbitrary",)),
    )(*_in_hbm(h, g, dh, dz, wint))


def _row_ids(n, w):
    return lax.broadcasted_iota(jnp.int32, (n, w), 0)


def _bcast_row(x, i, n):
    return jnp.broadcast_to(x[i:i + 1, :], (n, x.shape[1]))


def _conv_taps(v, tail):
    n, w = v.shape
    rid = _row_ids(n, w)
    v1 = jnp.where(rid == 0, _bcast_row(tail, 7, n), pltpu.roll(v, 1, 0))
    v2 = jnp.where(rid == 0, _bcast_row(tail, 6, n),
                   jnp.where(rid == 1, _bcast_row(tail, 7, n), pltpu.roll(v, 2, 0)))
    return v1, v2


def _scan_chunk(work, ltab, carry, reverse):
    nblk = CHUNK // SUBLANES
    for gi in range(N_STATE // LANE_GROUP):
        cre = pl.ds(gi * LANE_GROUP, LANE_GROUP)
        cim = pl.ds(N_STATE + gi * LANE_GROUP, LANE_GROUP)
        pows = [(ltab[8 * k:8 * k + 8, cre], ltab[8 * k:8 * k + 8, cim]) for k in range(3)]
        pr = ltab[24:32, cre]
        pi = ltab[24:32, cim]

        def blk(i, c, cre=cre, cim=cim, pows=pows, pr=pr, pi=pi):
            cr, ci = c
            b = (nblk - 1 - i) if reverse else i
            r0 = pl.multiple_of(b * SUBLANES, SUBLANES)
            xr = work[pl.ds(r0, SUBLANES), cre]
            xi = work[pl.ds(r0, SUBLANES), cim]
            for k, s in enumerate((1, 2, 4)):
                lr, li = pows[k]
                shift = SUBLANES - s if reverse else s
                sr = pltpu.roll(xr, shift, 0)
                si = pltpu.roll(xi, shift, 0)
                xr, xi = xr + lr * sr - li * si, xi + lr * si + li * sr
            xr, xi = xr + pr * cr - pi * ci, xi + pr * ci + pi * cr
            work[pl.ds(r0, SUBLANES), cre] = xr
            work[pl.ds(r0, SUBLANES), cim] = xi
            edge = 0 if reverse else SUBLANES - 1
            return _bcast_row(xr, edge, SUBLANES), _bcast_row(xi, edge, SUBLANES)

        cr, ci = lax.fori_loop(0, nblk, blk, (carry[:, cre], carry[:, cim]))
        carry[:, cre] = cr
        carry[:, cim] = ci


def _s5conv_fwd(z, convw, convb, bbmat, ccmat, dvec, ltab):
    def body(z_ref, cw_ref, cb_ref, bb_ref, cc_ref, d_ref, lt_ref, ya_ref, ys_ref, hs_ref,
             work, carry, tail):
        c = pl.program_id(0)

        @pl.when(c == 0)
        def _():
            carry[...] = jnp.zeros_like(carry)
            tail[...] = jnp.zeros_like(tail)

        zb = z_ref[:, 0:CONV_W]
        v = z_ref[:, CONV_W:2 * CONV_W] * z_ref[:, 2 * CONV_W:3 * CONV_W]
        us = z_ref[:, 3 * CONV_W:4 * CONV_W]
        v1, v2 = _conv_taps(v, tail[...])
        tail[...] = v[CHUNK - 8:CHUNK, :]
        y = cw_ref[0:1, :] * v2 + cw_ref[1:2, :] * v1 + cw_ref[2:3, :] * v
        ya_ref[...] = zb * (y + cb_ref[...])

        work[...] = _dot(us.astype(BF16), bb_ref[...])
        _scan_chunk(work, lt_ref, carry, reverse=False)
        hs = work[...].astype(BF16)
        hs_ref[...] = hs
        ys_ref[...] = _dot_nt(hs, cc_ref[...]) + d_ref[...] * us

    return pl.pallas_call(
        body, name="s5conv_fwd",
        grid=(N_CHUNKS,),
        in_specs=[pl.BlockSpec((CHUNK, IN_COLS), lambda c: (c, 0)),
                  pl.BlockSpec((3, CONV_W), lambda c: (0, 0)),
                  pl.BlockSpec((1, CONV_W), lambda c: (0, 0)),
                  pl.BlockSpec((SSM_W, 2 * N_STATE), lambda c: (0, 0)),
                  pl.BlockSpec((SSM_W, 2 * N_STATE), lambda c: (0, 0)),
                  pl.BlockSpec((1, SSM_W), lambda c: (0, 0)),
                  pl.BlockSpec((32, 2 * N_STATE), lambda c: (0, 0))],
        out_specs=[pl.BlockSpec((CHUNK, CONV_W), lambda c: (c, 0)),
                   pl.BlockSpec((CHUNK, SSM_W), lambda c: (c, 0)),
                   pl.BlockSpec((CHUNK, 2 * N_STATE), lambda c: (c, 0))],
        out_shape=[jax.ShapeDtypeStruct((SEQ, CONV_W), F32),
                   jax.ShapeDtypeStruct((SEQ, SSM_W), F32),
                   jax.ShapeDtypeStruct((SEQ, 2 * N_STATE), BF16)],
        scratch_shapes=[pltpu.VMEM((CHUNK, 2 * N_STATE), F32),
                        pltpu.VMEM((8, 2 * N_STATE), F32),
                        pltpu.VMEM((8, CONV_W), F32)],
        compiler_params=_cparams(("arbitrary",)),
    )(*_in_hbm(z, convw, convb, bbmat, ccmat, dvec, ltab))


def _s5conv_bwd(z, hs, dya, dys, convw, convb, bbmat, ccmat, dvec, ltab_rev):
    nc = N_CHUNKS
    hb = 16

    def body(z_ref, zp_ref, hs_ref, hp_ref, dya_ref, dys_ref, cw_ref, cb_ref, bb_ref, cc_ref, d_ref, lt_ref,
             dz_ref, g_ref, us_ref, dyb_ref, dl_ref, dcw_ref, work, carry, head):
        i = pl.program_id(0)
        first_chunk = i == nc - 1

        @pl.when(i == 0)
        def _():
            carry[...] = jnp.zeros_like(carry)
            head[...] = jnp.zeros_like(head)
            dl_ref[...] = jnp.zeros_like(dl_ref)
            dcw_ref[...] = jnp.zeros_like(dcw_ref)

        us = z_ref[:, 3 * CONV_W:4 * CONV_W]
        dy = dys_ref[...]
        dy_bf = dy.astype(BF16)
        us_ref[0] = us.astype(BF16)
        dyb_ref[0] = dy_bf

        work[...] = _dot(dy_bf, cc_ref[...])
        _scan_chunk(work, lt_ref, carry, reverse=True)
        gg = work[...]
        gg_bf = gg.astype(BF16)
        g_ref[0] = gg_bf
        dus = d_ref[...] * dy + _dot_nt(gg_bf, bb_ref[...])

        hcur = hs_ref[...].astype(F32)
        hlast = hp_ref[...].astype(F32)[hb - 1:hb, :]
        hlast = jnp.where(first_chunk, 0.0, hlast)
        rid = _row_ids(CHUNK, 2 * N_STATE)
        hprev = jnp.where(rid == 0, jnp.broadcast_to(hlast, (CHUNK, 2 * N_STATE)), pltpu.roll(hcur, 1, 0))
        gr, gi = gg[:, :N_STATE], gg[:, N_STATE:]
        hr, hi = hprev[:, :N_STATE], hprev[:, N_STATE:]
        dl_ref[:, :N_STATE] += (gr * hr + gi * hi).reshape(CHUNK // 8, 8, N_STATE).sum(axis=0)
        dl_ref[:, N_STATE:] += (gi * hr - gr * hi).reshape(CHUNK // 8, 8, N_STATE).sum(axis=0)

        @pl.when(i == nc - 1)
        def _():
            dl_ref[0:1, :] = jnp.sum(dl_ref[...], axis=0, keepdims=True)

        zb = z_ref[:, 0:CONV_W]
        zc = z_ref[:, CONV_W:2 * CONV_W]
        zv = z_ref[:, 2 * CONV_W:3 * CONV_W]
        v = zc * zv
        vtail = jnp.where(first_chunk, 0.0, zp_ref[:, CONV_W:2 * CONV_W] * zp_ref[:, 2 * CONV_W:3 * CONV_W])
        v1, v2 = _conv_taps(v, vtail)
        w0, w1, w2 = cw_ref[0:1, :], cw_ref[1:2, :], cw_ref[2:3, :]
        y = w0 * v2 + w1 * v1 + w2 * v
        dya_v = dya_ref[...]
        dzb = dya_v * (y + cb_ref[...])
        dyc = dya_v * zb
        hd = head[...]
        rc = _row_ids(CHUNK, CONV_W)
        n1 = jnp.where(rc == CHUNK - 1, _bcast_row(hd, 0, CHUNK), pltpu.roll(dyc, CHUNK - 1, 0))
        n2 = jnp.where(rc == CHUNK - 1, _bcast_row(hd, 1, CHUNK),
                       jnp.where(rc == CHUNK - 2, _bcast_row(hd, 0, CHUNK), pltpu.roll(dyc, CHUNK - 2, 0)))
        head[...] = dyc[0:8, :]
        dv = w2 * dyc + w1 * n1 + w0 * n2
        dz_ref[:, 0:CONV_W] = dzb.astype(BF16)
        dz_ref[:, CONV_W:2 * CONV_W] = (dv * zv).astype(BF16)
        dz_ref[:, 2 * CONV_W:3 * CONV_W] = (dv * zc).astype(BF16)
        dz_ref[:, 3 * CONV_W:4 * CONV_W] = dus.astype(BF16)
        dcw_ref[0:1, :] += jnp.sum(dyc * v2, axis=0, keepdims=True)
        dcw_ref[1:2, :] += jnp.sum(dyc * v1, axis=0, keepdims=True)
        dcw_ref[2:3, :] += jnp.sum(dyc * v, axis=0, keepdims=True)
        dcw_ref[3:4, :] += jnp.sum(dyc, axis=0, keepdims=True)
        dcw_ref[4:5, :] += jnp.sum(dy * us, axis=0, keepdims=True)

    rev = lambda i: nc - 1 - i
    return pl.pallas_call(
        body, name="s5conv_bwd",
        grid=(nc,),
        in_specs=[pl.BlockSpec((CHUNK, IN_COLS), lambda i: (rev(i), 0)),
                  pl.BlockSpec((8, IN_COLS), lambda i: (jnp.maximum(rev(i) * (CHUNK // 8) - 1, 0), 0)),
                  pl.BlockSpec((CHUNK, 2 * N_STATE), lambda i: (rev(i), 0)),
                  pl.BlockSpec((hb, 2 * N_STATE), lambda i: (jnp.maximum(rev(i) * (CHUNK // hb) - 1, 0), 0)),
                  pl.BlockSpec((CHUNK, CONV_W), lambda i: (rev(i), 0)),
                  pl.BlockSpec((CHUNK, SSM_W), lambda i: (rev(i), 0)),
                  pl.BlockSpec((3, CONV_W), lambda i: (0, 0)),
                  pl.BlockSpec((1, CONV_W), lambda i: (0, 0)),
                  pl.BlockSpec((SSM_W, 2 * N_STATE), lambda i: (0, 0)),
                  pl.BlockSpec((SSM_W, 2 * N_STATE), lambda i: (0, 0)),
                  pl.BlockSpec((1, SSM_W), lambda i: (0, 0)),
                  pl.BlockSpec((32, 2 * N_STATE), lambda i: (0, 0))],
        out_specs=[pl.BlockSpec((CHUNK, IN_COLS), lambda i: (rev(i), 0)),
                   pl.BlockSpec((1, CHUNK, 2 * N_STATE), lambda i: (0, rev(i), 0)),
                   pl.BlockSpec((1, CHUNK, SSM_W), lambda i: (0, rev(i), 0)),
                   pl.BlockSpec((1, CHUNK, SSM_W), lambda i: (0, rev(i), 0)),
                   pl.BlockSpec((8, 2 * N_STATE), lambda i: (0, 0)),
                   pl.BlockSpec((8, CONV_W), lambda i: (0, 0))],
        out_shape=[jax.ShapeDtypeStruct((SEQ, IN_COLS), BF16),
                   jax.ShapeDtypeStruct((1, SEQ, 2 * N_STATE), BF16),
                   jax.ShapeDtypeStruct((1, SEQ, SSM_W), BF16),
                   jax.ShapeDtypeStruct((1, SEQ, SSM_W), BF16),
                   jax.ShapeDtypeStruct((8, 2 * N_STATE), F32),
                   jax.ShapeDtypeStruct((8, CONV_W), F32)],
        scratch_shapes=[pltpu.VMEM((CHUNK, 2 * N_STATE), F32),
                        pltpu.VMEM((8, 2 * N_STATE), F32),
                        pltpu.VMEM((8, CONV_W), F32)],
        compiler_params=_cparams(("arbitrary",)),
    )(*_in_hbm(z, z, hs, hs, dya, dys, convw, convb, bbmat, ccmat, dvec, ltab_rev))


def _mix_out_fwd(h, ya, ys, gluw, glub, con, son, wout):
    tm = TOK_TILE

    def body(h_ref, ya_ref, ys_ref, gw_ref, gb_ref, con_ref, son_ref, wo_ref, o_ref):
        zg, _ = _gelu(ys_ref[...])
        q = _dot(zg.astype(BF16), gw_ref[...]) + gb_ref[...]
        out_s = zg * _sigmoid(q)
        na, _ = _rms_stats(ya_ref[...])
        ns, _ = _rms_stats(out_s)
        o_ref[...] = (h_ref[...]
                      + _dot((na * con_ref[...]).astype(BF16), wo_ref[0:CONV_W, :])
                      + _dot((ns * son_ref[...]).astype(BF16), wo_ref[CONV_W:2 * CONV_W, :]))

    row = lambda m: (m, 0)
    fixed = lambda m: (0, 0)
    return pl.pallas_call(
        body, name="mix_out_fwd",
        grid=(SEQ // tm,),
        in_specs=[pl.BlockSpec((tm, D_MODEL), row), pl.BlockSpec((tm, CONV_W), row), pl.BlockSpec((tm, SSM_W), row),
                  pl.BlockSpec((SSM_W, SSM_W), fixed), pl.BlockSpec((1, SSM_W), fixed),
                  pl.BlockSpec((1, CONV_W), fixed), pl.BlockSpec((1, SSM_W), fixed),
                  pl.BlockSpec((D_MODEL, D_MODEL), fixed)],
        out_specs=pl.BlockSpec((tm, D_MODEL), row),
        out_shape=jax.ShapeDtypeStruct((SEQ, D_MODEL), F32),
        compiler_params=_cparams(("parallel",)),
    )(*_in_hbm(h, ya, ys, gluw, glub, con, son, wout))


def _mix_out_bwd(dh, ya, ys, gluw, glub, con, son, wout):
    tm = TOK_TILE

    def body(dh_ref, ya_ref, ys_ref, gw_ref, gb_ref, con_ref, son_ref, wo_ref,
             dya_ref, dys_ref, yc_ref, dhb_ref, zg_ref, dq_ref, part_ref):
        ysv = ys_ref[...]
        zg, th = _gelu(ysv)
        zg_bf = zg.astype(BF16)
        s = _sigmoid(_dot(zg_bf, gw_ref[...]) + gb_ref[...])
        out_s = zg * s
        na, ra = _rms_stats(ya_ref[...])
        ns, rs = _rms_stats(out_s)
        dh_bf = dh_ref[...].astype(BF16)
        yc_ref[0, :, 0:CONV_W] = (na * con_ref[...]).astype(BF16)
        yc_ref[0, :, CONV_W:2 * CONV_W] = (ns * son_ref[...]).astype(BF16)
        dhb_ref[0] = dh_bf
        dca = _dot_nt(dh_bf, wo_ref[0:CONV_W, :])
        dcs = _dot_nt(dh_bf, wo_ref[CONV_W:2 * CONV_W, :])
        dya, dcon = _rms_bwd(dca, na, ra, con_ref[...])
        dos, dson = _rms_bwd(dcs, ns, rs, son_ref[...])
        dya_ref[...] = dya
        dq = dos * zg * s * (1.0 - s)
        dq_bf = dq.astype(BF16)
        dzg = dos * s + _dot_nt(dq_bf, gw_ref[...])
        dys_ref[...] = dzg * _gelu_grad(ysv, th)
        zg_ref[0] = zg_bf
        dq_ref[0] = dq_bf
        rid = _row_ids(SUBLANES, SSM_W)
        part = jnp.zeros((SUBLANES, SSM_W), F32)
        for i, rowv in enumerate((dcon, dson, jnp.sum(dq, axis=0, keepdims=True))):
            part = jnp.where(rid == i, jnp.broadcast_to(rowv, (SUBLANES, SSM_W)), part)
        _accumulate(part_ref, pl.program_id(0) == 0, part)

    row = lambda m: (m, 0)
    fixed = lambda m: (0, 0)
    lead = lambda m: (0, m, 0)
    return pl.pallas_call(
        body, name="mix_out_bwd",
        grid=(SEQ // tm,),
        in_specs=[pl.BlockSpec((tm, D_MODEL), row), pl.BlockSpec((tm, CONV_W), row), pl.BlockSpec((tm, SSM_W), row),
                  pl.BlockSpec((SSM_W, SSM_W), fixed), pl.BlockSpec((1, SSM_W), fixed),
                  pl.BlockSpec((1, CONV_W), fixed), pl.BlockSpec((1, SSM_W), fixed),
                  pl.BlockSpec((D_MODEL, D_MODEL), fixed)],
        out_specs=[pl.BlockSpec((tm, CONV_W), row), pl.BlockSpec((tm, SSM_W), row),
                   pl.BlockSpec((1, tm, D_MODEL), lead), pl.BlockSpec((1, tm, D_MODEL), lead),
                   pl.BlockSpec((1, tm, SSM_W), lead), pl.BlockSpec((1, tm, SSM_W), lead),
                   pl.BlockSpec((8, SSM_W), fixed)],
        out_shape=[jax.ShapeDtypeStruct((SEQ, CONV_W), F32), jax.ShapeDtypeStruct((SEQ, SSM_W), F32),
                   jax.ShapeDtypeStruct((1, SEQ, D_MODEL), BF16), jax.ShapeDtypeStruct((1, SEQ, D_MODEL), BF16),
                   jax.ShapeDtypeStruct((1, SEQ, SSM_W), BF16), jax.ShapeDtypeStruct((1, SEQ, SSM_W), BF16),
                   jax.ShapeDtypeStruct((8, SSM_W), F32)],
        compiler_params=_cparams(("arbitrary",)),
    )(*_in_hbm(dh, ya, ys, gluw, glub, con, son, wout))


def _ple_fwd(h, g, p, wgate, wprojt):
    tm = TOK_TILE

    def body(h_ref, g_ref, p_ref, wg_ref, wp_ref, o_ref):
        x = h_ref[...]
        xh, _ = _rms_stats(x)
        s = _sigmoid(_dot((xh * g_ref[...]).astype(BF16), wg_ref[...]))
        o_ref[...] = x + _dot_nt(p_ref[...].astype(BF16), wp_ref[...]) * s

    row = lambda m: (m, 0)
    fixed = lambda m: (0, 0)
    return pl.pallas_call(
        body, name="ple_fwd",
        grid=(SEQ // tm,),
        in_specs=[pl.BlockSpec((tm, D_MODEL), row), pl.BlockSpec((1, D_MODEL), fixed), pl.BlockSpec((tm, PLE_DIM), row),
                  pl.BlockSpec((D_MODEL, D_MODEL), fixed), pl.BlockSpec((D_MODEL, PLE_DIM), fixed)],
        out_specs=pl.BlockSpec((tm, D_MODEL), row),
        out_shape=jax.ShapeDtypeStruct((SEQ, D_MODEL), F32),
        compiler_params=_cparams(("parallel",)),
    )(*_in_hbm(h, g, p, wgate, wprojt))


def _ple_bwd(h, g, p, dh, wgate, wprojt):
    tm = TOK_TILE

    def body(h_ref, g_ref, p_ref, dh_ref, wg_ref, wp_ref, o_ref, u_ref, dq_ref, dpp_ref, pb_ref, dg_ref):
        xh, r = _rms_stats(h_ref[...])
        u = (xh * g_ref[...]).astype(BF16)
        s = _sigmoid(_dot(u, wg_ref[...]))
        p_bf = p_ref[...].astype(BF16)
        pp = _dot_nt(p_bf, wp_ref[...])
        dhv = dh_ref[...]
        dq = (dhv * pp * s * (1.0 - s)).astype(BF16)
        u_ref[0] = u
        dq_ref[0] = dq
        dpp_ref[0] = (dhv * s).astype(BF16)
        pb_ref[0] = p_bf
        dx, dg = _rms_bwd(_dot_nt(dq, wg_ref[...]), xh, r, g_ref[...])
        o_ref[...] = dhv + dx
        _accumulate(dg_ref, pl.program_id(0) == 0, dg)

    row = lambda m: (m, 0)
    fixed = lambda m: (0, 0)
    lead = lambda m: (0, m, 0)
    big = jax.ShapeDtypeStruct((1, SEQ, D_MODEL), BF16)
    return pl.pallas_call(
        body, name="ple_bwd",
        grid=(SEQ // tm,),
        in_specs=[pl.BlockSpec((tm, D_MODEL), row), pl.BlockSpec((1, D_MODEL), fixed), pl.BlockSpec((tm, PLE_DIM), row),
                  pl.BlockSpec((tm, D_MODEL), row),
                  pl.BlockSpec((D_MODEL, D_MODEL), fixed), pl.BlockSpec((D_MODEL, PLE_DIM), fixed)],
        out_specs=[pl.BlockSpec((tm, D_MODEL), row),
                   pl.BlockSpec((1, tm, D_MODEL), lead), pl.BlockSpec((1, tm, D_MODEL), lead),
                   pl.BlockSpec((1, tm, D_MODEL), lead), pl.BlockSpec((1, tm, PLE_DIM), lead),
                   pl.BlockSpec((1, D_MODEL), fixed)],
        out_shape=[jax.ShapeDtypeStruct((SEQ, D_MODEL), F32), big, big, big,
                   jax.ShapeDtypeStruct((1, SEQ, PLE_DIM), BF16),
                   jax.ShapeDtypeStruct((1, D_MODEL), F32)],
        compiler_params=_cparams(("arbitrary",)),
    )(*_in_hbm(h, g, p, dh, wgate, wprojt))


def _final_loss(h, g, target):
    tm = TOK_TILE

    def body(h_ref, g_ref, t_ref, loss_ref, dh_ref, dg_ref):
        first = pl.program_id(0) == 0
        xh, r = _rms_stats(h_ref[...])
        diff = xh * g_ref[...] - t_ref[...]
        part = 0.5 * jnp.sum(jnp.mean(diff * diff, axis=-1, keepdims=True), axis=0, keepdims=True)
        _accumulate(loss_ref, first, jnp.broadcast_to(part, (SUBLANES, LANES)))
        dx, dg = _rms_bwd(diff * (1.0 / D_MODEL), xh, r, g_ref[...])
        dh_ref[...] = dx
        _accumulate(dg_ref, first, dg)

    row = lambda m: (m, 0)
    fixed = lambda m: (0, 0)
    return pl.pallas_call(
        body, name="final_loss",
        grid=(SEQ // tm,),
        in_specs=[pl.BlockSpec((tm, D_MODEL), row), pl.BlockSpec((1, D_MODEL), fixed),
                  pl.BlockSpec((tm, D_MODEL), row)],
        out_specs=[pl.BlockSpec((SUBLANES, LANES), fixed),
                   pl.BlockSpec((tm, D_MODEL), row),
                   pl.BlockSpec((1, D_MODEL), fixed)],
        out_shape=[jax.ShapeDtypeStruct((SUBLANES, LANES), F32),
                   jax.ShapeDtypeStruct((SEQ, D_MODEL), F32),
                   jax.ShapeDtypeStruct((1, D_MODEL), F32)],
        compiler_params=_cparams(("arbitrary",)),
    )(*_in_hbm(h, g, target))


def _disc(ar, ai, ldt):
    dt = jnp.exp(ldt)
    mag = jnp.exp(ar * dt)
    ph = ai * dt
    lr, li = mag * jnp.cos(ph), mag * jnp.sin(ph)
    nr, ni = lr - 1.0, li
    den = ar * ar + ai * ai
    return lr, li, (nr * ar + ni * ai) / den, (ni * ar - nr * ai) / den


def _s5_disc(a_row, ldt_row, a_rep, ldt_rep, bt, ct, tile_e, mask):
    n = N_STATE

    def body(ar_ref, lr_ref, ap_ref, lp_ref, b_ref, c_ref, e_ref, m_ref, lt_ref, ltr_ref, bb_ref, cc_ref):
        lr, li, _, _ = _disc(ar_ref[0], ar_ref[1], lr_ref[...])
        pr, pi = lr, li
        rid = _row_ids(SUBLANES, n)
        for k in range(1, 9):
            for ref, sgn, edge in ((lt_ref, 1.0, 24 + k - 1), (ltr_ref, -1.0, 24 + 8 - k)):
                if k in (1, 2, 4):
                    r0 = {1: 0, 2: 8, 4: 16}[k]
                    keep = (rid >= k) if ref is lt_ref else (rid < SUBLANES - k)
                    ref[r0:r0 + 8, 0:n] = jnp.where(keep, jnp.broadcast_to(pr, (8, n)), 0.0)
                    ref[r0:r0 + 8, n:2 * n] = jnp.where(keep, jnp.broadcast_to(sgn * pi, (8, n)), 0.0)
                ref[edge:edge + 1, 0:n] = pr
                ref[edge:edge + 1, n:2 * n] = sgn * pi
            pr, pi = pr * lr - pi * li, pr * li + pi * lr
        _, _, fr, fi = _disc(ap_ref[0], ap_ref[1], lp_ref[...])
        br, bi = b_ref[0], b_ref[1]
        e = e_ref[...]
        m = m_ref[...].astype(F32)
        bb_ref[:, 0:n] = (_dot((fr * br - fi * bi).astype(BF16), e) * m).astype(BF16)
        bb_ref[:, n:2 * n] = (_dot((fr * bi + fi * br).astype(BF16), e) * m).astype(BF16)
        cc_ref[:, 0:n] = (_dot(c_ref[0].astype(BF16), e) * m).astype(BF16)
        cc_ref[:, n:2 * n] = (-(_dot(c_ref[1].astype(BF16), e) * m)).astype(BF16)

    return pl.pallas_call(
        body, name="s5_disc",
        out_shape=[jax.ShapeDtypeStruct((32, 2 * n), F32), jax.ShapeDtypeStruct((32, 2 * n), F32),
                   jax.ShapeDtypeStruct((SSM_W, 2 * n), BF16), jax.ShapeDtypeStruct((SSM_W, 2 * n), BF16)],
        compiler_params=_cparams(None),
    )(a_row, ldt_row, a_rep, ldt_rep, bt, ct, tile_e, mask)


def _dot_exact(x, sel):
    hi = x.astype(BF16)
    r1 = x - hi.astype(F32)
    mid = r1.astype(BF16)
    lo = (r1 - mid.astype(F32)).astype(BF16)
    return _dot(hi, sel) + _dot(mid, sel) + _dot(lo, sel)


def _s5_disc_bwd(a, ldt, a_rep, ldt_rep, bt, mask, dl, d_bb, d_cc, fold):
    n = N_STATE

    def body(a_ref, l_ref, ap_ref, lp_ref, b_ref, m_ref, dl_ref, dbb_ref, dcc_ref, f_ref,
             da_ref, dldt_ref, db_ref, dc_ref):
        m = m_ref[...].astype(F32)
        fold_m = f_ref[...]
        diag = lambda x: _dot_exact(x * m, fold_m)
        dr, di = diag(dbb_ref[:, 0:n]), diag(dbb_ref[:, n:2 * n])
        dc_ref[0] = diag(dcc_ref[:, 0:n])
        dc_ref[1] = -diag(dcc_ref[:, n:2 * n])
        _, _, fr, fi = _disc(ap_ref[0], ap_ref[1], lp_ref[...])
        br, bi = b_ref[0], b_ref[1]
        db_ref[0] = fr * dr + fi * di
        db_ref[1] = fr * di - fi * dr
        per_state = lambda x: x.reshape(SSM_GROUPS, SSM_GROUP, SSM_STATE).sum(axis=1)
        dfr = per_state(dr * br + di * bi)
        dfi = per_state(di * br - dr * bi)
        _, vjp = jax.vjp(_disc, a_ref[0], a_ref[1], l_ref[...])
        dar, dai, dldt = vjp((dl_ref[0], dl_ref[1], dfr, dfi))
        da_ref[0] = dar
        da_ref[1] = dai
        dldt_ref[...] = jnp.sum(dldt, axis=1, keepdims=True)

    return pl.pallas_call(
        body, name="s5_disc_bwd",
        out_shape=[jax.ShapeDtypeStruct((2, SSM_GROUPS, SSM_STATE), F32),
                   jax.ShapeDtypeStruct((SSM_GROUPS, 1), F32),
                   jax.ShapeDtypeStruct((2, SSM_W, SSM_STATE), F32),
                   jax.ShapeDtypeStruct((2, SSM_W, SSM_STATE), F32)],
        compiler_params=_cparams(None),
    )(a, ldt, a_rep, ldt_rep, bt, mask, dl, d_bb, d_cc, fold)


def _row_block(rows, cap=512):
    for bm in range(min(cap, rows), 0, -1):
        if rows % bm == 0 and (bm % 8 == 0 or bm == rows):
            return bm
    return rows


def _pair_sum(fulls, got, segs):
    ns = len(segs)
    offs = _seg_offsets(segs)
    _, rtot, c = got.shape
    parts = 2
    pr = rtot // parts
    assert pr * parts == rtot and pr % 16 == 0
    pieces = [[] for _ in range(parts)]
    for a, (n, r) in enumerate(segs):
        for m in range(n):
            lo = offs[a] + m * r
            for h in range(parts):
                clo, chi = max(lo, h * pr), min(lo + r, (h + 1) * pr)
                if chi > clo:
                    pieces[h].append((a, m, clo - lo, clo - h * pr, chi - clo))
    n_sems = max(len(ps) for ps in pieces)

    def body(*refs):
        srcs = refs[:ns]
        got_ref, p32_ref, pbf_ref, own_v, sems = refs[ns:]
        h = pl.program_id(0)
        k = pl.program_id(1)
        dev = 2 * k + lax.axis_index("c")
        for hh in range(parts):
            @pl.when(h == hh)
            def _(hh=hh):
                cps = []
                for i, (a, m, so, do, rows) in enumerate(pieces[hh]):
                    start = pl.multiple_of(dev * segs[a][1] + so, 16)
                    cps.append(pltpu.make_async_copy(srcs[a].at[m, pl.ds(start, rows), :],
                                                     own_v.at[pl.ds(do, rows), :], sems.at[i]))
                for cp in cps:
                    cp.start()
                for cp in cps:
                    cp.wait()
        s = own_v[...].astype(F32) + got_ref[0].astype(F32)
        pbf_ref[0] = s.astype(BF16)

        @pl.when(k == 2 * lax.axis_index("x") + lax.axis_index("y"))
        def _():
            p32_ref[...] = s

    spec = pl.BlockSpec((1, pr, c), lambda h, k: (k, h, 0))
    return pl.pallas_call(
        body, name="pair_sum",
        grid=(parts, 4),
        in_specs=[HBM] * ns + [spec], out_specs=[pl.BlockSpec((pr, c), lambda h, k: (h, 0)), spec],
        out_shape=[jax.ShapeDtypeStruct((rtot, c), F32), jax.ShapeDtypeStruct(got.shape, BF16)],
        scratch_shapes=[pltpu.VMEM((pr, c), BF16), pltpu.SemaphoreType.DMA((n_sems,))],
        compiler_params=_cparams(("arbitrary", "arbitrary")),
    )(*_in_hbm(*fulls, got))


def _chip_sum(own, rb):
    r, c = own.shape
    bm = _row_block(r)

    def body(o_ref, r_ref, s_ref):
        s_ref[...] = ((o_ref[...] + r_ref[0].astype(F32)) + r_ref[1].astype(F32)) + r_ref[2].astype(F32)

    return pl.pallas_call(
        body, name="chip_sum",
        grid=(r // bm,),
        in_specs=[pl.BlockSpec((bm, c), lambda k: (k, 0)), pl.BlockSpec((3, bm, c), lambda k: (0, k, 0))],
        out_specs=pl.BlockSpec((bm, c), lambda k: (k, 0)),
        out_shape=jax.ShapeDtypeStruct((r, c), F32),
        compiler_params=_cparams(("parallel",)),
    )(*_in_hbm(own, rb))


def _sum8(x, after):
    _, r, c = x.shape
    bm = _row_block(r)

    def body(x_ref, after_ref, s_ref):
        s = x_ref[0]
        for d in range(1, N_DEV):
            s = s + x_ref[d]
        s_ref[...] = s

    return pl.pallas_call(
        body, name="sum8",
        grid=(r // bm,),
        in_specs=[pl.BlockSpec((N_DEV, bm, c), lambda k: (0, k, 0)), ANY],
        out_specs=pl.BlockSpec((bm, c), lambda k: (k, 0)),
        out_shape=jax.ShapeDtypeStruct((r, c), F32),
        compiler_params=_cparams(("parallel",)),
    )(x, after)


def _adamw(w, g, m, v):
    r, c = w.shape
    bm = _row_block(r)
    bc1 = 1.0 - ADAM_B1 ** ADAM_STEP
    bc2 = 1.0 - ADAM_B2 ** ADAM_STEP

    def body(w_ref, g_ref, m_ref, v_ref, d_ref, nm_ref, nv_ref):
        gv = g_ref[...]
        nm = ADAM_B1 * m_ref[...] + (1.0 - ADAM_B1) * gv
        nv = ADAM_B2 * v_ref[...] + (1.0 - ADAM_B2) * (gv * gv)
        nm_ref[...] = nm
        nv_ref[...] = nv
        d_ref[...] = -ADAM_LR * ((nm / bc1) / (jnp.sqrt(nv / bc2) + ADAM_EPS) + ADAM_WD * w_ref[...])

    spec = pl.BlockSpec((bm, c), lambda k: (k, 0))
    shp = jax.ShapeDtypeStruct((r, c), F32)
    return pl.pallas_call(
        body, name="adamw",
        grid=(r // bm,),
        in_specs=[spec] * 4, out_specs=[spec] * 3, out_shape=[shp] * 3,
        compiler_params=_cparams(("parallel",)),
    )(*_in_hbm(w, g, m, v))


def _mesh_pos():
    return lax.axis_index("x"), lax.axis_index("y"), lax.axis_index("c")


def _dev_index(p):
    return 4 * p[0] + 2 * p[1] + p[2]


def _seg_offsets(segs):
    offs, o = [], 0
    for n, r in segs:
        offs.append(o)
        o += n * r
    return offs


def _remote(src, dst, send_sem, recv_sem, to):
    return pltpu.make_async_remote_copy(src_ref=src, dst_ref=dst, send_sem=send_sem, recv_sem=recv_sem,
                                        device_id=to, device_id_type=MESH)


def _allgather(pack, segs, name):
    rtot, c = pack.shape
    ns = len(segs)
    offs = _seg_offsets(segs)
    assert rtot == sum(n * r for n, r in segs)

    def body(pack_ref, *refs):
        outs = refs[:ns]
        send_sems, recv_sems, local_sem = refs[ns:]
        x, y, cc = _mesh_pos()
        me, sib = (x, y, cc), (x, y, 1 - cc)
        chips = [(1 - x, y), (x, 1 - y), (1 - x, 1 - y)]

        def pieces(dev, from_pack):
            res = []
            for a, (n, r) in enumerate(segs):
                for m in range(n):
                    dst = outs[a].at[m, pl.ds(pl.multiple_of(dev * r, r), r), :]
                    src = pack_ref.at[pl.ds(offs[a] + m * r, r), :] if from_pack else dst
                    res.append((src, dst))
            return res

        def push(k, dev, to, from_pack):
            for s, d in pieces(dev, from_pack):
                _remote(s, d, send_sems.at[k], recv_sems.at[k], to).start()

        def whole(k):
            return _remote(pack_ref, pack_ref, send_sems.at[k], recv_sems.at[k], me)

        my_dev = _dev_index(me)
        for s, d in pieces(my_dev, True):
            pltpu.make_async_copy(s, d, local_sem).start()
        push(0, my_dev, sib, True)
        for j, chip in enumerate(chips):
            push(1 + j, my_dev, (*chip, cc), True)
        for j, chip in enumerate(chips):
            whole(1 + j).wait_recv()
            push(4 + j, _dev_index((*chip, cc)), sib, False)
        whole(0).wait_recv()
        for j in range(3):
            whole(4 + j).wait_recv()
        for k in range(7):
            whole(k).wait_send()
        pltpu.make_async_copy(pack_ref, pack_ref, local_sem).wait()

    return pl.pallas_call(
        body, name=name,
        in_specs=[HBM], out_specs=[HBM] * ns,
        out_shape=[jax.ShapeDtypeStruct((n, N_DEV * r, c), pack.dtype) for n, r in segs],
        scratch_shapes=[pltpu.SemaphoreType.DMA((7,)), pltpu.SemaphoreType.DMA((7,)), pltpu.SemaphoreType.DMA],
    )(pack)


HBM = pl.BlockSpec(memory_space=pltpu.HBM)
SEM = pl.BlockSpec(memory_space=pltpu.SEMAPHORE)
VMEM_WHOLE = pl.BlockSpec(memory_space=pltpu.VMEM)
EFFECT = pltpu.SideEffectType.DATAFLOW_SIDE_EFFECTING


def _hbm(a):
    return pltpu.with_memory_space_constraint(a, pltpu.HBM)


def _ag_start(pack, segs, after, name):
    rtot, c = pack.shape
    ns = len(segs)
    offs = _seg_offsets(segs)

    def body(pack_ref, *refs):
        lands = refs[:ns]
        send_sems, recv_sems = refs[ns + 1], refs[ns + 2]
        token = refs[-1]
        x, y, cc = _mesh_pos()
        my_dev = _dev_index((x, y, cc))
        targets = [(x, y, 1 - cc), (1 - x, y, cc), (x, 1 - y, cc), (1 - x, 1 - y, cc)]
        for k, to in enumerate(targets):
            for a, (n, r) in enumerate(segs):
                for m in range(n):
                    _remote(pack_ref.at[pl.ds(offs[a] + m * r, r), :],
                            lands[a].at[m, pl.ds(pl.multiple_of(my_dev * r, r), r), :],
                            send_sems.at[k], recv_sems.at[k], to).start()
        token[...] = jnp.zeros_like(token)

    land_shapes = [(n, N_DEV * r, c) for n, r in segs]
    outs = pl.pallas_call(
        body, name=name,
        in_specs=[HBM] * (1 + ns) + [ANY],
        out_specs=[SEM, SEM, HBM] + [HBM] * ns + [VMEM_WHOLE],
        out_shape=[pltpu.SemaphoreType.DMA((4,)), pltpu.SemaphoreType.DMA((4,)), pltpu.HBM(pack.shape, pack.dtype)]
        + [pltpu.HBM(s, pack.dtype) for s in land_shapes] + [jax.ShapeDtypeStruct((SUBLANES, LANES), F32)],
        input_output_aliases={0: 2, **{1 + i: 3 + i for i in range(ns)}},
        compiler_params=pltpu.CompilerParams(has_side_effects=EFFECT),
    )(_hbm(pack), *[_hbm(lax.empty(s, pack.dtype)) for s in land_shapes], after)
    return outs[0], outs[1], outs[2], list(outs[3:3 + ns]), outs[-1]


def _ag_wait(send_sems, recv_sems, pack, lands, after, name):
    ns = len(lands)

    def body(pack_ref, *refs):
        send_ref, recv_ref = refs[ns], refs[ns + 1]
        me = _mesh_pos()
        for k in range(4):
            whole = _remote(pack_ref, pack_ref, send_ref.at[k], recv_ref.at[k], me)
            whole.wait_send()
            whole.wait_recv()

    outs = pl.pallas_call(
        body, name=name,
        in_specs=[HBM] * (1 + ns) + [SEM, SEM, ANY],
        out_specs=[HBM] * (1 + ns),
        out_shape=[pltpu.HBM(pack.shape, pack.dtype)] + [pltpu.HBM(a.shape, a.dtype) for a in lands],
        input_output_aliases={i: i for i in range(1 + ns)},
        compiler_params=pltpu.CompilerParams(has_side_effects=EFFECT),
    )(pack, *lands, send_sems, recv_sems, after)
    return outs[0], list(outs[1:])


def _ag_finish(pack, lands, segs):
    rtot, c = pack.shape
    ns = len(segs)
    offs = _seg_offsets(segs)

    def body(pack_ref, *refs):
        outs = refs[ns:2 * ns]
        stage, send_sems, recv_sems, local_sems = refs[2 * ns:]
        x, y, cc = _mesh_pos()
        me, sib = (x, y, cc), (x, y, 1 - cc)
        chips = [(1 - x, y), (x, 1 - y), (1 - x, 1 - y)]

        def rows(a, m, dev):
            return outs[a].at[m, pl.ds(pl.multiple_of(dev * segs[a][1], segs[a][1]), segs[a][1]), :]

        for j, chip in enumerate(chips):
            dev = _dev_index((*chip, cc))
            for a, (n, r) in enumerate(segs):
                for m in range(n):
                    _remote(rows(a, m, dev), rows(a, m, dev), send_sems.at[j], recv_sems.at[j], sib).start()
        load = pltpu.make_async_copy(pack_ref, stage, local_sems.at[0])
        load.start()
        load.wait()
        my_dev = _dev_index(me)
        for a, (n, r) in enumerate(segs):
            for m in range(n):
                pltpu.make_async_copy(stage.at[pl.ds(offs[a] + m * r, r), :], rows(a, m, my_dev), local_sems.at[1]).start()
        pltpu.make_async_copy(stage, pack_ref, local_sems.at[1]).wait()
        for j in range(3):
            _remote(pack_ref, pack_ref, send_sems.at[j], recv_sems.at[j], me).wait()

    outs = pl.pallas_call(
        body, name="ag_finish",
        in_specs=[HBM] * (1 + ns), out_specs=[HBM] * ns,
        out_shape=[jax.ShapeDtypeStruct(a.shape, a.dtype) for a in lands],
        input_output_aliases={1 + i: i for i in range(ns)},
        scratch_shapes=[pltpu.VMEM((rtot, c), pack.dtype), pltpu.SemaphoreType.DMA((3,)),
                        pltpu.SemaphoreType.DMA((3,)), pltpu.SemaphoreType.DMA((2,))],
        compiler_params=_cparams(None, 16),
    )(pack, *lands)
    return list(outs)


def _rs_chips_start(pbf, after, name):
    _, rtot, c = pbf.shape

    def body(pbf_ref, land_ref, after_ref, send_sems, recv_sems, pbf_thru, land_thru, token):
        x, y, cc = _mesh_pos()
        for j, (cx, cy) in enumerate([(1 - x, y), (x, 1 - y), (1 - x, 1 - y)]):
            _remote(pbf_ref.at[2 * cx + cy], land_ref.at[j], send_sems.at[j], recv_sems.at[j], (cx, cy, cc)).start()
        token[...] = jnp.zeros_like(token)

    return pl.pallas_call(
        body, name=name,
        in_specs=[HBM, HBM, ANY],
        out_specs=[SEM, SEM, HBM, HBM, VMEM_WHOLE],
        out_shape=[pltpu.SemaphoreType.DMA((3,)), pltpu.SemaphoreType.DMA((3,)), pltpu.HBM(pbf.shape, pbf.dtype),
                   pltpu.HBM((3, rtot, c), pbf.dtype), jax.ShapeDtypeStruct((SUBLANES, LANES), F32)],
        input_output_aliases={0: 2, 1: 3},
        compiler_params=pltpu.CompilerParams(has_side_effects=EFFECT),
    )(_hbm(pbf), _hbm(lax.empty((3, rtot, c), pbf.dtype)), after)


def _rs_chips_wait(send_sems, recv_sems, pbf, land, after, name):
    def body(pbf_ref, land_ref, send_ref, recv_ref, after_ref, pbf_out, land_out):
        me = _mesh_pos()
        for j in range(3):
            cp = _remote(pbf_ref.at[0], land_ref.at[j], send_ref.at[j], recv_ref.at[j], me)
            cp.wait_send()
            cp.wait_recv()

    return pl.pallas_call(
        body, name=name,
        in_specs=[HBM, HBM, SEM, SEM, ANY], out_specs=[HBM, HBM],
        out_shape=[pltpu.HBM(pbf.shape, pbf.dtype), pltpu.HBM(land.shape, land.dtype)],
        input_output_aliases={0: 0, 1: 1},
        compiler_params=pltpu.CompilerParams(has_side_effects=EFFECT),
    )(pbf, land, send_sems, recv_sems, after)[1]


def _rs_sibling_start(fulls, segs, name):
    ns = len(segs)
    offs = _seg_offsets(segs)
    rtot = sum(n * r for n, r in segs)
    c = fulls[0].shape[-1]
    dt = fulls[0].dtype

    def body(*refs):
        srcs = refs[:ns]
        land_ref, send_sem, recv_sem = refs[ns], refs[ns + 1], refs[ns + 2]
        token = refs[-1]
        x, y, cc = _mesh_pos()
        for k in range(4):
            for a, (n, r) in enumerate(segs):
                for m in range(n):
                    theirs = srcs[a].at[m, pl.ds(pl.multiple_of((2 * k + 1 - cc) * r, r), r), :]
                    _remote(theirs, land_ref.at[k, pl.ds(offs[a] + m * r, r), :], send_sem, recv_sem,
                            (x, y, 1 - cc)).start()
        token[...] = jnp.zeros_like(token)

    outs = pl.pallas_call(
        body, name=name,
        in_specs=[HBM] * (ns + 1),
        out_specs=[SEM, SEM] + [HBM] * (ns + 1) + [VMEM_WHOLE],
        out_shape=[pltpu.SemaphoreType.DMA(()), pltpu.SemaphoreType.DMA(())]
        + [pltpu.HBM(a.shape, a.dtype) for a in fulls] + [pltpu.HBM((4, rtot, c), dt),
                                                           jax.ShapeDtypeStruct((SUBLANES, LANES), F32)],
        input_output_aliases={i: 2 + i for i in range(ns + 1)},
        compiler_params=pltpu.CompilerParams(has_side_effects=EFFECT),
    )(*[_hbm(a) for a in fulls], _hbm(lax.empty((4, rtot, c), dt)))
    return outs[0], outs[1], list(outs[2:2 + ns]), outs[2 + ns], outs[-1]


def _rs_sibling_wait(send_sem, recv_sem, fulls, land, after, name):
    ns = len(fulls)

    def body(*refs):
        land_ref, send_ref, recv_ref = refs[ns], refs[ns + 1], refs[ns + 2]
        whole = _remote(land_ref, land_ref, send_ref, recv_ref, _mesh_pos())
        whole.wait_send()
        whole.wait_recv()

    outs = pl.pallas_call(
        body, name=name,
        in_specs=[HBM] * (ns + 1) + [SEM, SEM, ANY], out_specs=[HBM] * (ns + 1),
        out_shape=[pltpu.HBM(a.shape, a.dtype) for a in fulls] + [pltpu.HBM(land.shape, land.dtype)],
        input_output_aliases={i: i for i in range(ns + 1)},
        compiler_params=pltpu.CompilerParams(has_side_effects=EFFECT),
    )(*fulls, land, send_sem, recv_sem, after)
    return list(outs[:ns]), outs[ns]


def _tp(w):
    return jnp.swapaxes(w, -1, -2)


def _s5_prepare(a_re, a_im, log_dt, b_re, b_im, c_re, c_im):
    a = jnp.stack([a_re, a_im], axis=1)
    ldt = jnp.broadcast_to(log_dt[:, :, None], (DEPTH, SSM_GROUPS, SSM_STATE))
    a_row = a.reshape(DEPTH, 2, 1, N_STATE)
    ldt_row = ldt.reshape(DEPTH, 1, N_STATE)
    a_rep = jnp.repeat(a, SSM_GROUP, axis=2)
    ldt_rep = jnp.repeat(ldt, SSM_GROUP, axis=1)
    bt = jnp.stack([_tp(b_re), _tp(b_im)], axis=1).reshape(DEPTH, 2, SSM_W, SSM_STATE)
    ct = jnp.stack([c_re, c_im], axis=1).reshape(DEPTH, 2, SSM_W, SSM_STATE)
    tile_e = jnp.tile(jnp.eye(SSM_STATE, dtype=BF16), (1, SSM_GROUPS))
    mask = jnp.repeat(jnp.repeat(jnp.eye(SSM_GROUPS, dtype=BF16), SSM_GROUP, axis=0), SSM_STATE, axis=1)
    out = []
    for l in range(DEPTH):
        tabs = _s5_disc(a_row[l], ldt_row[l], a_rep[l], ldt_rep[l], bt[l], ct[l], tile_e, mask)
        out.append(((a[l], ldt[l], a_rep[l], ldt_rep[l], bt[l], mask), *tabs))
    return out


def _layer_fwd(h, p_l, small, big, arrive=None):
    saved = {'h0': h}
    if arrive is not None:
        arrive(0, h)
    h, saved['gu1'] = _ffn_fwd(h, small['ffn1_norm'], big['ff1'])
    saved['h1'] = h
    if arrive is not None:
        arrive(1, h)
    z = _inproj_fwd(h, small['mix_norm'], big['wint'])
    ya, ys, hs = _s5conv_fwd(z, small['conv_w'], small['conv_b'], small['bbmat'], small['ccmat'], small['dvec'],
                             small['ltab'])
    saved.update(z=z, ya=ya, ys=ys, hs=hs)
    h = _mix_out_fwd(h, ya, ys, big['glu'], small['glu_b'], small['conv_out_norm'], small['ssm_out_norm'], big['wout'])
    saved['h2'] = h
    if arrive is not None:
        arrive(2, h)
    h, saved['gu2'] = _ffn_fwd(h, small['ffn2_norm'], big['ff2'])
    saved['h3'] = h
    h = _ple_fwd(h, small['ple_norm'], p_l, big['plg'], big['plpt'])
    return h, saved


def _ffn_bwd(h_in, g, dh, gu, w3):
    dh_in, dga, ud, dg = _ffn_bwd_act(h_in, g, dh, gu, w3)
    return dh_in, _matmul_tn(dga, ud, FF_BLOCK, BF16, "ffn_wgrad"), dg


def _layer_bwd_top(dh, p_l, small, big, saved):
    gs = {}
    dh, u, dq, dpp, pb, gs['ple_norm'] = _ple_bwd(saved['h3'], small['ple_norm'], p_l, dh, big['plg'], big['plpt'])
    d_plg = _matmul_tn(u, dq, 256, BF16, "ple_gate_wgrad")
    d_plpt = _matmul_tn(dpp, pb, 256, BF16, "ple_proj_wgrad")
    dh, d_ff2, gs['ffn2_norm'] = _ffn_bwd(saved['h2'], small['ffn2_norm'], dh, saved['gu2'], big['ff2'])
    return dh, (gs, d_plg, d_plpt, d_ff2)


def _layer_bwd_rest(dh, top, small, big, saved):
    gs, d_plg, d_plpt, d_ff2 = top
    dya, dys, ycat, dhb, zg, dq, part = _mix_out_bwd(dh, saved['ya'], saved['ys'], big['glu'], small['glu_b'],
                                                     small['conv_out_norm'], small['ssm_out_norm'], big['wout'])
    d_wout = _matmul_tn(ycat, dhb, 256, BF16, "w_out_wgrad")
    d_glu = _matmul_tn(zg, dq, 256, BF16, "glu_wgrad")
    dz, gadj, us, dyb, dl, dcw = _s5conv_bwd(saved['z'], saved['hs'], dya, dys, small['conv_w'], small['conv_b'],
                                             small['bbmat'], small['ccmat'], small['dvec'], small['ltab_rev'])
    d_bb = _matmul_tn(us, gadj, SSM_W, F32, "s5_b_wgrad", bn=1024)[0]
    d_cc = _matmul_tn(dyb, saved['hs'][None], SSM_W, F32, "s5_c_wgrad", bn=1024)[0]
    dh, u, gs['mix_norm'] = _inproj_bwd(saved['h1'], small['mix_norm'], dh, dz, big['wint'])
    d_wint = _matmul_tn(dz[None], u, 256, BF16, "w_in_wgrad")
    dh, d_ff1, gs['ffn1_norm'] = _ffn_bwd(saved['h0'], small['ffn1_norm'], dh, saved['gu1'], big['ff1'])

    dlb = dl[0].reshape(2, SSM_GROUPS, SSM_STATE)
    fold = jnp.tile(jnp.eye(SSM_STATE, dtype=BF16), (SSM_GROUPS, 1))
    da, dldt, dbt, dct = _s5_disc_bwd(*small['disc_in'], dlb, d_bb, d_cc, fold)
    gs['ssm_A_re'], gs['ssm_A_im'] = da[0], da[1]
    gs['ssm_log_dt'] = dldt[:, 0]
    ghp = (SSM_GROUPS, SSM_GROUP, SSM_STATE)
    gs['ssm_B_re'], gs['ssm_B_im'] = dbt[0].reshape(ghp), dbt[1].reshape(ghp)
    gs['ssm_C_re'], gs['ssm_C_im'] = dct[0].reshape(ghp), dct[1].reshape(ghp)
    gs['conv_w'] = dcw[0:3]
    gs['conv_b'] = dcw[3]
    gs['ssm_D'] = dcw[4].reshape(SSM_GROUPS, SSM_GROUP)
    gs['conv_out_norm'], gs['ssm_out_norm'], gs['glu_b'] = part[0], part[1], part[2]
    for n in ('ple_norm', 'ffn2_norm', 'mix_norm', 'ffn1_norm'):
        gs[n] = gs[n][0]
    fulls = [d_ff1, d_ff2, d_wint, d_wout, d_plg,
             d_plpt.reshape(1, D_MODEL * PLE_DIM // D_MODEL, D_MODEL), d_glu.reshape(1, SSM_W * SSM_W // D_MODEL, D_MODEL)]
    return dh, fulls, gs


VIEW_T = ('ffn1_w_gate', 'ffn1_w_up', 'ffn2_w_gate', 'ffn2_w_up', 'ssm_B_re', 'ssm_B_im')


def _view(name, a):
    return _tp(a) if name in VIEW_T else a


SEG_NAMES = ('ff1', 'ff2', 'wint', 'wout', 'plg', 'plpt', 'glu')
FIRST_LAYER_GROUPS = ((0,), (2, 3, 6), (1, 4, 5))


def _layer_pack(W, l, segments=range(len(SEGS))):
    pieces = {
        0: lambda: [_tp(W['ffn1_w_gate'][l]), _tp(W['ffn1_w_up'][l]), W['ffn1_w_down'][l]],
        1: lambda: [_tp(W['ffn2_w_gate'][l]), _tp(W['ffn2_w_up'][l]), W['ffn2_w_down'][l]],
        2: lambda: [_tp(W['w_in'][l])],
        3: lambda: [W['w_out'][l]],
        4: lambda: [W['ple_w_gate'][l]],
        5: lambda: [_tp(W['ple_w_proj'][l]).reshape(-1, D_MODEL)],
        6: lambda: [W['glu_w'][l].reshape(-1, D_MODEL)],
    }
    return jnp.concatenate([a for s in segments for a in pieces[s]()], axis=0).astype(BF16)


def _as_big(named):
    shape = dict(wint=(IN_COLS, D_MODEL), wout=(D_MODEL, D_MODEL), plg=(D_MODEL, D_MODEL), plpt=(D_MODEL, PLE_DIM),
                 glu=(SSM_W, SSM_W))
    return {n: (a.reshape(shape[n]) if n in shape else a) for n, a in named.items()}


def _pad_rows(flat, mult, width=LANES):
    per = mult * width
    n = flat.shape[0]
    tot = -(-n // per) * per
    return jnp.pad(flat, (0, tot - n)).reshape(tot // width, width)


def _adamw_any(w, g, m, v):
    shp = w.shape
    two = (lambda t: t.reshape(-1, shp[-1]))
    d, nm, nv = _adamw(two(w), two(g), two(m), two(v))
    return d.reshape(shp), nm.reshape(shp), nv.reshape(shp)


def kernel(x, p, ffn1_norm, ffn1_w_gate, ffn1_w_up, ffn1_w_down, mix_norm, w_in, conv_w, conv_b, ssm_A_re, ssm_A_im, ssm_B_re, ssm_B_im, ssm_C_re, ssm_C_im, ssm_D, ssm_log_dt, glu_w, glu_b, conv_out_norm, ssm_out_norm, w_out, ffn2_norm, ffn2_w_gate, ffn2_w_up, ffn2_w_down, ple_norm, ple_w_gate, ple_w_proj, final_norm, loss_target, m_ffn1_norm, m_ffn1_w_gate, m_ffn1_w_up, m_ffn1_w_down, m_mix_norm, m_w_in, m_conv_w, m_conv_b, m_ssm_A_re, m_ssm_A_im, m_ssm_B_re, m_ssm_B_im, m_ssm_C_re, m_ssm_C_im, m_ssm_D, m_ssm_log_dt, m_glu_w, m_glu_b, m_conv_out_norm, m_ssm_out_norm, m_w_out, m_ffn2_norm, m_ffn2_w_gate, m_ffn2_w_up, m_ffn2_w_down, m_ple_norm, m_ple_w_gate, m_ple_w_proj, m_final_norm, v_ffn1_norm, v_ffn1_w_gate, v_ffn1_w_up, v_ffn1_w_down, v_mix_norm, v_w_in, v_conv_w, v_conv_b, v_ssm_A_re, v_ssm_A_im, v_ssm_B_re, v_ssm_B_im, v_ssm_C_re, v_ssm_C_im, v_ssm_D, v_ssm_log_dt, v_glu_w, v_glu_b, v_conv_out_norm, v_ssm_out_norm, v_w_out, v_ffn2_norm, v_ffn2_w_gate, v_ffn2_w_up, v_ffn2_w_down, v_ple_norm, v_ple_w_gate, v_ple_w_proj, v_final_norm):
    given = dict(locals())
    W = {n: given[n] for n in W_NAMES}
    M = {n: given['m_' + n] for n in W_NAMES}
    V = {n: given['v_' + n] for n in W_NAMES}
    Wv, Mv, Vv = [{n: _view(n, d[n]) for n in W_NAMES} for d in (W, M, V)]
    my_dev = _dev_index(_mesh_pos())

    conv_shard = _pad_rows(W['conv_w'].reshape(-1), SUBLANES)
    conv_all = _allgather(conv_shard, ((1, SUBLANES),), "ag_conv_w")[0]
    conv_full = conv_all.reshape(N_DEV, -1)[:, :DEPTH * 3 * (CONV_W // N_DEV)]
    conv_full = conv_full.reshape(N_DEV, DEPTH, 3, CONV_W // N_DEV).transpose(1, 2, 0, 3).reshape(DEPTH, 3, CONV_W)
    first, after = [], conv_all
    for gi, segments in enumerate(FIRST_LAYER_GROUPS):
        first.append(_ag_start(_layer_pack(W, 0, segments), tuple(SEGS[s] for s in segments), after,
                               "ag_start_0%s" % "abc"[gi]))
        after = first[-1][4]
    s5 = _s5_prepare(*[W[n] + after[0, 0] for n in ('ssm_A_re', 'ssm_A_im', 'ssm_log_dt')],
                     *[W[n] for n in ('ssm_B_re', 'ssm_B_im', 'ssm_C_re', 'ssm_C_im')])
    packs = [None] + [_layer_pack(W, l) for l in range(1, DEPTH)]
    prepared = conv_full[0, 0:1, 0:1] + s5[DEPTH - 1][1][0:1, 0:1] + packs[DEPTH - 1][0:1, 0:1].astype(F32)

    smalls, saves, bigs = [], [], []
    h = x[0]

    flight = None

    def gathered(handles, segments, after, name, next_layer=None, gate=None):
        nonlocal flight
        send_sems, recv_sems, pack_thru, lands, _ = handles
        pack_thru, lands = _ag_wait(send_sems, recv_sems, pack_thru, lands, after, "ag_wait_" + name)
        if next_layer is not None:
            flight = _ag_start(packs[next_layer], SEGS, lands[0], "ag_start_%d" % next_layer)
            gate[0][gate[1]] = gate[0][gate[1]] + flight[4][0:1, 0:1]
        outs = _ag_finish(pack_thru, lands, tuple(SEGS[s] for s in segments))
        return _as_big({SEG_NAMES[s]: a for s, a in zip(segments, outs)})

    for l in range(DEPTH):
        small = {n: W[n][l][None] for n in ('ffn1_norm', 'mix_norm', 'conv_b', 'glu_b', 'conv_out_norm',
                                            'ssm_out_norm', 'ffn2_norm', 'ple_norm')}
        small['conv_w'] = conv_full[l]
        small['dvec'] = W['ssm_D'][l].reshape(1, SSM_W)
        small['disc_in'], small['ltab'], small['ltab_rev'], small['bbmat'], small['ccmat'] = s5[l]
        big = {}
        bigs.append(big)
        if l == 0:
            def arrive(stage, h_now, big=big, small=small):
                big.update(gathered(first[stage], FIRST_LAYER_GROUPS[stage], prepared if stage == 0 else h_now,
                                    "0%s" % "abc"[stage], *((1, (small, 'ffn2_norm')) if stage == 2 else ())))
            h, saved = _layer_fwd(h, p[l, 0], small, big, arrive)
        else:
            nxt = (l + 1, (small, 'ffn1_norm')) if l + 1 < DEPTH else ()
            big.update(gathered(flight, range(len(SEGS)), h, "%d" % l, *nxt))
            h, saved = _layer_fwd(h, p[l, 0], small, big)
        smalls.append(small)
        saves.append(saved)
    loss_tile, dh, d_final = _final_loss(h, W['final_norm'][None], loss_target[0])
    loss = lax.psum(loss_tile[0, 0], ("x", "y", "c"))

    layer_gs = [None] * DEPTH
    shard_grads = [None] * DEPTH
    zero = jnp.zeros((1, 1), F32)
    sib, ici = None, None

    def finish_sibling(after_sib, after_ici):
        nonlocal sib, ici
        up, (send_sem, recv_sem, fulls_thru, land, _) = sib
        fulls_thru, got = _rs_sibling_wait(send_sem, recv_sem, fulls_thru, land, after_sib, "sib_wait_%d" % up)
        own32, pbf = _pair_sum(fulls_thru, got, SEGS)
        finish_chips(pbf)
        ici = (up, _rs_chips_start(pbf, after_ici, "rs_start_%d" % up), own32)
        sib = None

    def finish_chips(after):
        nonlocal ici
        if ici is not None:
            up, (send_sems, recv_sems, pbf_thru, land, _), own32 = ici
            got3 = _rs_chips_wait(send_sems, recv_sems, pbf_thru, land, after, "rs_wait_%d" % up)
            shard_grads[up] = _chip_sum(own32, got3)
            ici = None

    for l in reversed(range(DEPTH)):
        small = dict(smalls[l])
        if sib is not None:
            small['ple_norm'] = small['ple_norm'] + sib[1][4][0:1, 0:1]
        dh, top = _layer_bwd_top(dh, p[l, 0], small, bigs[l], saves[l])
        if sib is not None:
            finish_sibling(dh, dh)
            small['glu_b'] = small['glu_b'] + ici[1][4][0:1, 0:1]
        dh, fulls, layer_gs[l] = _layer_bwd_rest(dh, top, small, bigs[l], saves[l])
        sib = (l, _rs_sibling_start(fulls, SEGS, "sib_start_%d" % l))
    grad_x = dh[None]

    gs = {n: jnp.stack([layer_gs[l][n] for l in range(DEPTH)]) for n in layer_gs[0]}
    gs['final_norm'] = d_final[0]
    flat = jnp.concatenate([gs[n].reshape(-1) for n in SMALL_NAMES] + [gs['conv_w'].reshape(-1)])
    n_flat = flat.shape[0]
    flat = _pad_rows(flat, 64, D_MODEL) + sib[1][4][0:1, 0:1]
    rows = flat.shape[0]
    gathered = _allgather(flat, ((1, rows),), "ag_small_grads")[0]
    finish_sibling(gathered, gathered)
    red = _sum8(gathered.reshape(N_DEV, rows, D_MODEL), ici[1][4]).reshape(-1)[:n_flat]
    finish_chips(red)
    G = {}
    o = 0
    for n in SMALL_NAMES:
        G[n] = red[o:o + W[n].size].reshape(Wv[n].shape)
        o += W[n].size
    conv_g_full = red[o:].reshape(DEPTH, 3, CONV_W)
    G['conv_w'] = lax.dynamic_slice_in_dim(conv_g_full, my_dev * (CONV_W // N_DEV), CONV_W // N_DEV, axis=2)

    sg = jnp.stack(shard_grads)
    offs = _seg_offsets(SEGS)
    r = SEGS[0][1]
    for a, f in ((0, 'ffn1'), (1, 'ffn2')):
        G[f + '_w_gate'] = sg[:, offs[a]:offs[a] + r]
        G[f + '_w_up'] = sg[:, offs[a] + r:offs[a] + 2 * r]
        G[f + '_w_down'] = sg[:, offs[a] + 2 * r:offs[a] + 3 * r]
    G['w_in'] = _tp(sg[:, offs[2]:offs[2] + SEGS[2][1]])
    G['w_out'] = sg[:, offs[3]:offs[3] + SEGS[3][1]]
    G['ple_w_gate'] = sg[:, offs[4]:offs[4] + SEGS[4][1]]
    G['ple_w_proj'] = _tp(sg[:, offs[5]:offs[5] + SEGS[5][1]].reshape(DEPTH, D_MODEL // N_DEV, PLE_DIM))
    G['glu_w'] = sg[:, offs[6]:offs[6] + SEGS[6][1]].reshape(DEPTH, SSM_W // N_DEV, SSM_W)

    delta, new_m, new_v = {}, {}, {}
    cat = lambda src: _pad_rows(jnp.concatenate([src[n].reshape(-1) for n in SMALL_NAMES]), SUBLANES, D_MODEL)
    d_s, m_s, v_s = _adamw(cat(Wv), cat(G), cat(Mv), cat(Vv))
    o = 0
    for n in SMALL_NAMES:
        for dst, src in ((delta, d_s), (new_m, m_s), (new_v, v_s)):
            dst[n] = src.reshape(-1)[o:o + W[n].size].reshape(Wv[n].shape)
        o += W[n].size
    for n in W_NAMES:
        if n not in delta:
            delta[n], new_m[n], new_v[n] = _adamw_any(Wv[n], G[n], Mv[n], Vv[n])

    outs = [[_view(n, d[n]) for n in W_NAMES] for d in (G, delta, new_m, new_v)]
    return (loss, grad_x, *outs[0], *outs[1], *outs[2], *outs[3])
```

```python
import math

import jax
import jax.numpy as jnp
from jax import lax
from jax.experimental import pallas as pl
from jax.experimental.pallas import tpu as pltpu

F32 = jnp.float32
BF16 = jnp.bfloat16

N_DEV = 8
DEPTH = 4
SEQ = 2048
D_MODEL = 1024
D_FF = 2816
CONV_W = 512
SSM_W = 512
SSM_GROUPS = 32
SSM_GROUP = 16
SSM_STATE = 64
N_STATE = SSM_GROUPS * SSM_STATE
IN_COLS = 2048
PLE_DIM = 256
EPS = 1e-6

ADAM_LR = 0.001
ADAM_B1 = 0.9
ADAM_B2 = 0.999
ADAM_EPS = 1e-08
ADAM_WD = 0.01
ADAM_STEP = 10

FF_BLOCK = 256
N_FF_BLOCKS = D_FF // FF_BLOCK
TOK_TILE_FFN_FWD = 2048
TOK_TILE_FFN_BWD = 1024
TOK_TILE = 512
CHUNK = 256
N_CHUNKS = SEQ // CHUNK
LANE_GROUP = 512
SUBLANES = 8
LANES = 128
MIB = 1024 * 1024

W_NAMES = ['ffn1_norm', 'ffn1_w_gate', 'ffn1_w_up', 'ffn1_w_down', 'mix_norm', 'w_in', 'conv_w', 'conv_b',
           'ssm_A_re', 'ssm_A_im', 'ssm_B_re', 'ssm_B_im', 'ssm_C_re', 'ssm_C_im', 'ssm_D', 'ssm_log_dt',
           'glu_w', 'glu_b', 'conv_out_norm', 'ssm_out_norm', 'w_out', 'ffn2_norm', 'ffn2_w_gate', 'ffn2_w_up',
           'ffn2_w_down', 'ple_norm', 'ple_w_gate', 'ple_w_proj', 'final_norm']
SMALL_NAMES = ['ffn1_norm', 'mix_norm', 'conv_b', 'ssm_A_re', 'ssm_A_im', 'ssm_B_re', 'ssm_B_im', 'ssm_C_re',
               'ssm_C_im', 'ssm_D', 'ssm_log_dt', 'glu_b', 'conv_out_norm', 'ssm_out_norm', 'ffn2_norm',
               'ple_norm', 'final_norm']

SEGS = ((3, 352), (3, 352), (1, 256), (1, 128), (1, 128), (1, 32), (1, 32))
PACK_ROWS = sum(n * r for n, r in SEGS)

MESH = pl.DeviceIdType.MESH
UNREAD = pl.BlockSpec(memory_space=pltpu.HBM)


def _in_hbm(*arrays):
    return [pltpu.with_memory_space_constraint(a, pltpu.HBM) for a in arrays]


def _out_hbm(outs, which):
    if not isinstance(outs, (list, tuple)):
        return pltpu.with_memory_space_constraint(outs, pltpu.HBM) if which else outs
    return [pltpu.with_memory_space_constraint(a, pltpu.HBM) if i in which else a for i, a in enumerate(outs)]


def _cparams(sem=None, vmem_mib=48, **kw):
    return pltpu.CompilerParams(dimension_semantics=sem, vmem_limit_bytes=vmem_mib * MIB, **kw)


def _dot(a, b):
    return jnp.dot(a, b, preferred_element_type=F32)


def _dot_nt(a, b):
    return lax.dot_general(a, b, (((1,), (1,)), ((), ())), preferred_element_type=F32)


def _dot_tn(a, b):
    return lax.dot_general(a, b, (((0,), (0,)), ((), ())), preferred_element_type=F32)


def _rms_stats(x):
    r = lax.rsqrt(jnp.mean(x * x, axis=-1, keepdims=True) + EPS)
    return x * r, r


def _rms_bwd(dy, xh, r, g):
    dxh = dy * g
    dx = r * (dxh - xh * jnp.mean(dxh * xh, axis=-1, keepdims=True))
    dg = jnp.sum(dy * xh, axis=0, keepdims=True)
    return dx, dg


def _sigmoid(x):
    return 0.5 * jnp.tanh(0.5 * x) + 0.5


_GELU_C = math.sqrt(2.0 / math.pi)


def _gelu(x):
    t = jnp.tanh(_GELU_C * (x + 0.044715 * x * x * x))
    return 0.5 * x * (1.0 + t), t


def _gelu_grad(x, t):
    return 0.5 * (1.0 + t) + 0.5 * x * (1.0 - t * t) * _GELU_C * (1.0 + 3.0 * 0.044715 * x * x)


def _accumulate(ref, first, value):
    @pl.when(first)
    def _():
        ref[...] = value

    @pl.when(jnp.logical_not(first))
    def _():
        ref[...] += value


def _ffn_fwd(h, g, w3):
    tm = TOK_TILE_FFN_FWD
    last = N_FF_BLOCKS - 1

    def body(h_ref, g_ref, wgu_ref, wd_ref, wd_last_ref, out_ref, gu_ref, u_ref, a_ref):
        k = pl.program_id(1)

        @pl.when(k == 0)
        def _():
            x = h_ref[...]
            xh, _ = _rms_stats(x)
            u_ref[...] = (xh * g_ref[...]).astype(BF16)
            out_ref[...] = x
            a_ref[1] = jnp.zeros((tm, FF_BLOCK), BF16)

        out_ref[...] += 0.5 * _dot(a_ref[(k + 1) % 2], wd_ref[0])
        gu = _dot_nt(u_ref[...], wgu_ref[...].reshape(2 * FF_BLOCK, D_MODEL))
        gate, up = gu[:, :FF_BLOCK], gu[:, FF_BLOCK:]
        a_ref[k % 2] = (gate * _sigmoid(gate) * up).astype(BF16)
        gu_ref[0] = gate.astype(BF16)
        gu_ref[1] = up.astype(BF16)

        @pl.when(k == last)
        def _():
            out_ref[...] += 0.5 * _dot(a_ref[last % 2], wd_last_ref[0])

    return _out_hbm(pl.pallas_call(
        body, name="ffn_fwd",
        grid=(SEQ // tm, N_FF_BLOCKS),
        in_specs=[pl.BlockSpec((tm, D_MODEL), lambda m, k: (m, 0), pipeline_mode=pl.Buffered(1)),
                  pl.BlockSpec((1, D_MODEL), lambda m, k: (0, 0)),
                  pl.BlockSpec((2, FF_BLOCK, D_MODEL), lambda m, k: (0, k, 0)),
                  pl.BlockSpec((1, FF_BLOCK, D_MODEL), lambda m, k: (2, jnp.maximum(k - 1, 0), 0)),
                  pl.BlockSpec((1, FF_BLOCK, D_MODEL), lambda m, k: (2, last, 0), pipeline_mode=pl.Buffered(1))],
        out_specs=[pl.BlockSpec((tm, D_MODEL), lambda m, k: (m, 0)),
                   pl.BlockSpec((2, tm, FF_BLOCK), lambda m, k: (0, m, k))],
        out_shape=[jax.ShapeDtypeStruct((SEQ, D_MODEL), F32),
                   pltpu.HBM((2, SEQ, D_FF), BF16)],
        scratch_shapes=[pltpu.VMEM((tm, D_MODEL), BF16), pltpu.VMEM((2, tm, FF_BLOCK), BF16)],
        compiler_params=_cparams(("parallel", "arbitrary"), 56),
    )(*_in_hbm(h, g, w3, w3, w3)), (1,))


def _ffn_bwd_act(h, g, dout, gu, w3):
    tm = TOK_TILE_FFN_BWD
    last = N_FF_BLOCKS - 1

    def body(h_ref, g_ref, d_ref, gu_ref, wd_ref, wgu_ref, wgu_last_ref, dh_ref, dga_ref, ud_ref, dg_ref,
             acc_ref, dgu_ref):
        m = pl.program_id(0)
        k = pl.program_id(1)

        @pl.when(k == 0)
        def _():
            xh, _ = _rms_stats(h_ref[...])
            ud_ref[0] = (xh * g_ref[...]).astype(BF16)
            ud_ref[1] = (0.5 * d_ref[...]).astype(BF16)
            acc_ref[...] = jnp.zeros_like(acc_ref)
            dgu_ref[1] = jnp.zeros((tm, 2 * FF_BLOCK), BF16)

        acc_ref[...] += _dot(dgu_ref[(k + 1) % 2], wgu_ref[...].reshape(2 * FF_BLOCK, D_MODEL))
        gate = gu_ref[0].astype(F32)
        up = gu_ref[1].astype(F32)
        sg = _sigmoid(gate)
        silu = gate * sg
        da = _dot_nt(ud_ref[1], wd_ref[0])
        dgate = (da * up * (sg + silu * (1.0 - sg))).astype(BF16)
        dup = (da * silu).astype(BF16)
        dga_ref[0] = dgate
        dga_ref[1] = dup
        dga_ref[2] = (silu * up).astype(BF16)
        dgu_ref[k % 2, :, 0:FF_BLOCK] = dgate
        dgu_ref[k % 2, :, FF_BLOCK:2 * FF_BLOCK] = dup

        @pl.when(k == last)
        def _():
            du = acc_ref[...] + _dot(dgu_ref[last % 2], wgu_last_ref[...].reshape(2 * FF_BLOCK, D_MODEL))
            xh, r = _rms_stats(h_ref[...])
            dx, dg = _rms_bwd(du, xh, r, g_ref[...])
            dh_ref[...] = d_ref[...] + dx
            _accumulate(dg_ref, m == 0, dg)

    return _out_hbm(pl.pallas_call(
        body, name="ffn_bwd_act",
        grid=(SEQ // tm, N_FF_BLOCKS),
        in_specs=[pl.BlockSpec((tm, D_MODEL), lambda m, k: (m, 0), pipeline_mode=pl.Buffered(1)),
                  pl.BlockSpec((1, D_MODEL), lambda m, k: (0, 0)),
                  pl.BlockSpec((tm, D_MODEL), lambda m, k: (m, 0), pipeline_mode=pl.Buffered(1)),
                  pl.BlockSpec((2, tm, FF_BLOCK), lambda m, k: (0, m, k)),
                  pl.BlockSpec((1, FF_BLOCK, D_MODEL), lambda m, k: (2, k, 0)),
                  pl.BlockSpec((2, FF_BLOCK, D_MODEL), lambda m, k: (0, jnp.maximum(k - 1, 0), 0)),
                  pl.BlockSpec((2, FF_BLOCK, D_MODEL), lambda m, k: (0, last, 0), pipeline_mode=pl.Buffered(1))],
        out_specs=[pl.BlockSpec((tm, D_MODEL), lambda m, k: (m, 0)),
                   pl.BlockSpec((3, tm, FF_BLOCK), lambda m, k: (0, m, k)),
                   pl.BlockSpec((2, tm, D_MODEL), lambda m, k: (0, m, 0)),
                   pl.BlockSpec((1, D_MODEL), lambda m, k: (0, 0))],
        out_shape=[jax.ShapeDtypeStruct((SEQ, D_MODEL), F32),
                   pltpu.HBM((3, SEQ, D_FF), BF16),
                   pltpu.HBM((2, SEQ, D_MODEL), BF16),
                   jax.ShapeDtypeStruct((1, D_MODEL), F32)],
        scratch_shapes=[pltpu.VMEM((tm, D_MODEL), F32), pltpu.VMEM((2, tm, 2 * FF_BLOCK), BF16)],
        compiler_params=_cparams(("arbitrary", "arbitrary"), 56),
    )(*_in_hbm(h, g, dout, gu, w3, w3, w3)), (1, 2))


def _matmul_tn(a, b, bm, out_dtype, name, bn=None, to_kernel=True):
    na, t, m = a.shape
    nb, _, n = b.shape
    bn = n if bn is None else bn

    def body(a_ref, b_ref, o_ref):
        o_ref[0] = _dot_tn(a_ref[0], b_ref[0]).astype(out_dtype)

    return _out_hbm(pl.pallas_call(
        body, name=name,
        grid=(na, m // bm, n // bn),
        in_specs=[pl.BlockSpec((1, t, bm), lambda i, k, j: (i, 0, k)),
                  pl.BlockSpec((1, t, bn), lambda i, k, j: (jnp.maximum(i - (na - nb), 0), 0, j))],
        out_specs=pl.BlockSpec((1, bm, bn), lambda i, k, j: (i, k, j)),
        out_shape=pltpu.HBM((na, m, n), out_dtype) if to_kernel else jax.ShapeDtypeStruct((na, m, n), out_dtype),
        compiler_params=_cparams(("arbitrary", "parallel", "parallel")),
    )(*_in_hbm(a, b)), to_kernel)


def _inproj_fwd(h, g, wint):
    tm = TOK_TILE

    def body(h_ref, g_ref, w_ref, z_ref):
        xh, _ = _rms_stats(h_ref[...])
        z_ref[...] = _dot_nt((xh * g_ref[...]).astype(BF16), w_ref[...])

    return pl.pallas_call(
        body, name="inproj_fwd",
        grid=(SEQ // tm,),
        in_specs=[pl.BlockSpec((tm, D_MODEL), lambda m: (m, 0)),
                  pl.BlockSpec((1, D_MODEL), lambda m: (0, 0)),
                  pl.BlockSpec((None, IN_COLS, D_MODEL), lambda m: (0, 0, 0))],
        out_specs=pl.BlockSpec((tm, IN_COLS), lambda m: (m, 0)),
        out_shape=jax.ShapeDtypeStruct((SEQ, IN_COLS), F32),
        compiler_params=_cparams(("parallel",)),
    )(*_in_hbm(h, g, wint))


def _inproj_bwd(h, g, dh, dz, wint):
    tm = TOK_TILE

    def body(h_ref, g_ref, dh_ref, dz_ref, w_ref, o_ref, u_ref, dg_ref):
        xh, r = _rms_stats(h_ref[...])
        u_ref[0] = (xh * g_ref[...]).astype(BF16)
        dx, dg = _rms_bwd(_dot(dz_ref[...], w_ref[...]), xh, r, g_ref[...])
        o_ref[...] = dh_ref[...] + dx
        _accumulate(dg_ref, pl.program_id(0) == 0, dg)

    return _out_hbm(pl.pallas_call(
        body, name="inproj_bwd",
        grid=(SEQ // tm,),
        in_specs=[pl.BlockSpec((tm, D_MODEL), lambda m: (m, 0)),
                  pl.BlockSpec((1, D_MODEL), lambda m: (0, 0)),
                  pl.BlockSpec((tm, D_MODEL), lambda m: (m, 0)),
                  pl.BlockSpec((tm, IN_COLS), lambda m: (m, 0)),
                  pl.BlockSpec((None, IN_COLS, D_MODEL), lambda m: (0, 0, 0))],
        out_specs=[pl.BlockSpec((tm, D_MODEL), lambda m: (m, 0)),
                   pl.BlockSpec((1, tm, D_MODEL), lambda m: (0, m, 0)),
                   pl.BlockSpec((1, D_MODEL), lambda m: (0, 0))],
        out_shape=[jax.ShapeDtypeStruct((SEQ, D_MODEL), F32),
                   pltpu.HBM((1, SEQ, D_MODEL), BF16),
                   jax.ShapeDtypeStruct((1, D_MODEL), F32)],
        compiler_params=_cparams(("arbitrary",)),
    )(*_in_hbm(h, g, dh, dz, wint)), (1,))


def _row_ids(n, w):
    return lax.broadcasted_iota(jnp.int32, (n, w), 0)


def _bcast_row(x, i, n):
    return jnp.broadcast_to(x[i:i + 1, :], (n, x.shape[1]))


def _conv_taps(v, tail):
    n, w = v.shape
    rid = _row_ids(n, w)
    v1 = jnp.where(rid == 0, _bcast_row(tail, 7, n), pltpu.roll(v, 1, 0))
    v2 = jnp.where(rid == 0, _bcast_row(tail, 6, n),
                   jnp.where(rid == 1, _bcast_row(tail, 7, n), pltpu.roll(v, 2, 0)))
    return v1, v2


def _scan_chunk(work, ltab, carry, reverse):
    nblk = CHUNK // SUBLANES
    for gi in range(N_STATE // LANE_GROUP):
        cre = pl.ds(gi * LANE_GROUP, LANE_GROUP)
        cim = pl.ds(N_STATE + gi * LANE_GROUP, LANE_GROUP)
        pows = [(ltab[8 * k:8 * k + 8, cre], ltab[8 * k:8 * k + 8, cim]) for k in range(3)]
        pr = ltab[24:32, cre]
        pi = ltab[24:32, cim]

        def blk(i, c, cre=cre, cim=cim, pows=pows, pr=pr, pi=pi):
            cr, ci = c
            b = (nblk - 1 - i) if reverse else i
            r0 = pl.multiple_of(b * SUBLANES, SUBLANES)
            xr = work[pl.ds(r0, SUBLANES), cre]
            xi = work[pl.ds(r0, SUBLANES), cim]
            for k, s in enumerate((1, 2, 4)):
                lr, li = pows[k]
                shift = SUBLANES - s if reverse else s
                sr = pltpu.roll(xr, shift, 0)
                si = pltpu.roll(xi, shift, 0)
                xr, xi = xr + lr * sr - li * si, xi + lr * si + li * sr
            xr, xi = xr + pr * cr - pi * ci, xi + pr * ci + pi * cr
            work[pl.ds(r0, SUBLANES), cre] = xr
            work[pl.ds(r0, SUBLANES), cim] = xi
            edge = 0 if reverse else SUBLANES - 1
            return _bcast_row(xr, edge, SUBLANES), _bcast_row(xi, edge, SUBLANES)

        cr, ci = lax.fori_loop(0, nblk, blk, (carry[:, cre], carry[:, cim]))
        carry[:, cre] = cr
        carry[:, cim] = ci


def _s5conv_fwd(z, convw, convb, bbmat, ccmat, dvec, ltab):
    def body(z_ref, cw_ref, cb_ref, bb_ref, cc_ref, d_ref, lt_ref, ya_ref, ys_ref, hs_ref,
             work, carry, tail):
        c = pl.program_id(0)

        @pl.when(c == 0)
        def _():
            carry[...] = jnp.zeros_like(carry)
            tail[...] = jnp.zeros_like(tail)

        zb = z_ref[:, 0:CONV_W]
        v = z_ref[:, CONV_W:2 * CONV_W] * z_ref[:, 2 * CONV_W:3 * CONV_W]
        us = z_ref[:, 3 * CONV_W:4 * CONV_W]
        v1, v2 = _conv_taps(v, tail[...])
        tail[...] = v[CHUNK - 8:CHUNK, :]
        y = cw_ref[0:1, :] * v2 + cw_ref[1:2, :] * v1 + cw_ref[2:3, :] * v
        ya_ref[...] = zb * (y + cb_ref[...])

        work[...] = _dot(us.astype(BF16), bb_ref[...])
        _scan_chunk(work, lt_ref, carry, reverse=False)
        hs = work[...].astype(BF16)
        hs_ref[...] = hs
        ys_ref[...] = _dot_nt(hs, cc_ref[...]) + d_ref[...] * us

    return _out_hbm(pl.pallas_call(
        body, name="s5conv_fwd",
        grid=(N_CHUNKS,),
        in_specs=[pl.BlockSpec((CHUNK, IN_COLS), lambda c: (c, 0)),
                  pl.BlockSpec((3, CONV_W), lambda c: (0, 0)),
                  pl.BlockSpec((1, CONV_W), lambda c: (0, 0)),
                  pl.BlockSpec((SSM_W, 2 * N_STATE), lambda c: (0, 0)),
                  pl.BlockSpec((SSM_W, 2 * N_STATE), lambda c: (0, 0)),
                  pl.BlockSpec((1, SSM_W), lambda c: (0, 0)),
                  pl.BlockSpec((32, 2 * N_STATE), lambda c: (0, 0))],
        out_specs=[pl.BlockSpec((CHUNK, CONV_W), lambda c: (c, 0)),
                   pl.BlockSpec((CHUNK, SSM_W), lambda c: (c, 0)),
                   pl.BlockSpec((CHUNK, 2 * N_STATE), lambda c: (c, 0))],
        out_shape=[pltpu.HBM((SEQ, CONV_W), F32),
                   pltpu.HBM((SEQ, SSM_W), F32),
                   jax.ShapeDtypeStruct((SEQ, 2 * N_STATE), BF16)],
        scratch_shapes=[pltpu.VMEM((CHUNK, 2 * N_STATE), F32),
                        pltpu.VMEM((8, 2 * N_STATE), F32),
                        pltpu.VMEM((8, CONV_W), F32)],
        compiler_params=_cparams(("arbitrary",)),
    )(*_in_hbm(z, convw, convb, bbmat, ccmat, dvec, ltab)), (0, 1))


def _s5conv_bwd(z, hs, dya, dys, convw, convb, bbmat, ccmat, dvec, ltab_rev):
    nc = N_CHUNKS
    hb = 16

    def body(z_ref, zp_ref, hs_ref, hp_ref, dya_ref, dys_ref, cw_ref, cb_ref, bb_ref, cc_ref, d_ref, lt_ref,
             dz_ref, g_ref, us_ref, dyb_ref, dl_ref, dcw_ref, work, carry, head):
        i = pl.program_id(0)
        first_chunk = i == nc - 1

        @pl.when(i == 0)
        def _():
            carry[...] = jnp.zeros_like(carry)
            head[...] = jnp.zeros_like(head)
            dl_ref[...] = jnp.zeros_like(dl_ref)
            dcw_ref[...] = jnp.zeros_like(dcw_ref)

        us = z_ref[:, 3 * CONV_W:4 * CONV_W]
        dy = dys_ref[...]
        dy_bf = dy.astype(BF16)
        us_ref[0] = us.astype(BF16)
        dyb_ref[0] = dy_bf

        work[...] = _dot(dy_bf, cc_ref[...])
        _scan_chunk(work, lt_ref, carry, reverse=True)
        gg = work[...]
        gg_bf = gg.astype(BF16)
        g_ref[0] = gg_bf
        dus = d_ref[...] * dy + _dot_nt(gg_bf, bb_ref[...])

        hcur = hs_ref[...].astype(F32)
        hlast = hp_ref[...].astype(F32)[hb - 1:hb, :]
        hlast = jnp.where(first_chunk, 0.0, hlast)
        rid = _row_ids(CHUNK, 2 * N_STATE)
        hprev = jnp.where(rid == 0, jnp.broadcast_to(hlast, (CHUNK, 2 * N_STATE)), pltpu.roll(hcur, 1, 0))
        gr, gi = gg[:, :N_STATE], gg[:, N_STATE:]
        hr, hi = hprev[:, :N_STATE], hprev[:, N_STATE:]
        dl_ref[:, :N_STATE] += (gr * hr + gi * hi).reshape(CHUNK // 8, 8, N_STATE).sum(axis=0)
        dl_ref[:, N_STATE:] += (gi * hr - gr * hi).reshape(CHUNK // 8, 8, N_STATE).sum(axis=0)

        @pl.when(i == nc - 1)
        def _():
            dl_ref[0:1, :] = jnp.sum(dl_ref[...], axis=0, keepdims=True)

        zb = z_ref[:, 0:CONV_W]
        zc = z_ref[:, CONV_W:2 * CONV_W]
        zv = z_ref[:, 2 * CONV_W:3 * CONV_W]
        v = zc * zv
        vtail = jnp.where(first_chunk, 0.0, zp_ref[:, CONV_W:2 * CONV_W] * zp_ref[:, 2 * CONV_W:3 * CONV_W])
        v1, v2 = _conv_taps(v, vtail)
        w0, w1, w2 = cw_ref[0:1, :], cw_ref[1:2, :], cw_ref[2:3, :]
        y = w0 * v2 + w1 * v1 + w2 * v
        dya_v = dya_ref[...]
        dzb = dya_v * (y + cb_ref[...])
        dyc = dya_v * zb
        hd = head[...]
        rc = _row_ids(CHUNK, CONV_W)
        n1 = jnp.where(rc == CHUNK - 1, _bcast_row(hd, 0, CHUNK), pltpu.roll(dyc, CHUNK - 1, 0))
        n2 = jnp.where(rc == CHUNK - 1, _bcast_row(hd, 1, CHUNK),
                       jnp.where(rc == CHUNK - 2, _bcast_row(hd, 0, CHUNK), pltpu.roll(dyc, CHUNK - 2, 0)))
        head[...] = dyc[0:8, :]
        dv = w2 * dyc + w1 * n1 + w0 * n2
        dz_ref[:, 0:CONV_W] = dzb.astype(BF16)
        dz_ref[:, CONV_W:2 * CONV_W] = (dv * zv).astype(BF16)
        dz_ref[:, 2 * CONV_W:3 * CONV_W] = (dv * zc).astype(BF16)
        dz_ref[:, 3 * CONV_W:4 * CONV_W] = dus.astype(BF16)
        dcw_ref[0:1, :] += jnp.sum(dyc * v2, axis=0, keepdims=True)
        dcw_ref[1:2, :] += jnp.sum(dyc * v1, axis=0, keepdims=True)
        dcw_ref[2:3, :] += jnp.sum(dyc * v, axis=0, keepdims=True)
        dcw_ref[3:4, :] += jnp.sum(dyc, axis=0, keepdims=True)
        dcw_ref[4:5, :] += jnp.sum(dy * us, axis=0, keepdims=True)

    rev = lambda i: nc - 1 - i
    return _out_hbm(pl.pallas_call(
        body, name="s5conv_bwd",
        grid=(nc,),
        in_specs=[pl.BlockSpec((CHUNK, IN_COLS), lambda i: (rev(i), 0)),
                  pl.BlockSpec((8, IN_COLS), lambda i: (jnp.maximum(rev(i) * (CHUNK // 8) - 1, 0), 0)),
                  pl.BlockSpec((CHUNK, 2 * N_STATE), lambda i: (rev(i), 0)),
                  pl.BlockSpec((hb, 2 * N_STATE), lambda i: (jnp.maximum(rev(i) * (CHUNK // hb) - 1, 0), 0)),
                  pl.BlockSpec((CHUNK, CONV_W), lambda i: (rev(i), 0)),
                  pl.BlockSpec((CHUNK, SSM_W), lambda i: (rev(i), 0)),
                  pl.BlockSpec((3, CONV_W), lambda i: (0, 0)),
                  pl.BlockSpec((1, CONV_W), lambda i: (0, 0)),
                  pl.BlockSpec((SSM_W, 2 * N_STATE), lambda i: (0, 0)),
                  pl.BlockSpec((SSM_W, 2 * N_STATE), lambda i: (0, 0)),
                  pl.BlockSpec((1, SSM_W), lambda i: (0, 0)),
                  pl.BlockSpec((32, 2 * N_STATE), lambda i: (0, 0))],
        out_specs=[pl.BlockSpec((CHUNK, IN_COLS), lambda i: (rev(i), 0)),
                   pl.BlockSpec((1, CHUNK, 2 * N_STATE), lambda i: (0, rev(i), 0)),
                   pl.BlockSpec((1, CHUNK, SSM_W), lambda i: (0, rev(i), 0)),
                   pl.BlockSpec((1, CHUNK, SSM_W), lambda i: (0, rev(i), 0)),
                   pl.BlockSpec((8, 2 * N_STATE), lambda i: (0, 0)),
                   pl.BlockSpec((8, CONV_W), lambda i: (0, 0))],
        out_shape=[jax.ShapeDtypeStruct((SEQ, IN_COLS), BF16),
                   pltpu.HBM((1, SEQ, 2 * N_STATE), BF16),
                   pltpu.HBM((1, SEQ, SSM_W), BF16),
                   pltpu.HBM((1, SEQ, SSM_W), BF16),
                   jax.ShapeDtypeStruct((8, 2 * N_STATE), F32),
                   jax.ShapeDtypeStruct((8, CONV_W), F32)],
        scratch_shapes=[pltpu.VMEM((CHUNK, 2 * N_STATE), F32),
                        pltpu.VMEM((8, 2 * N_STATE), F32),
                        pltpu.VMEM((8, CONV_W), F32)],
        compiler_params=_cparams(("arbitrary",)),
    )(*_in_hbm(z, z, hs, hs, dya, dys, convw, convb, bbmat, ccmat, dvec, ltab_rev)), (1, 2, 3))


def _mix_out_fwd(h, ya, ys, gluw, glub, con, son, wout):
    tm = TOK_TILE

    def body(h_ref, ya_ref, ys_ref, gw_ref, gb_ref, con_ref, son_ref, wo_ref, o_ref):
        zg, _ = _gelu(ys_ref[...])
        q = _dot(zg.astype(BF16), gw_ref[...]) + gb_ref[...]
        out_s = zg * _sigmoid(q)
        na, _ = _rms_stats(ya_ref[...])
        ns, _ = _rms_stats(out_s)
        o_ref[...] = (h_ref[...]
                      + _dot((na * con_ref[...]).astype(BF16), wo_ref[0:CONV_W, :])
                      + _dot((ns * son_ref[...]).astype(BF16), wo_ref[CONV_W:2 * CONV_W, :]))

    row = lambda m: (m, 0)
    fixed = lambda m: (0, 0)
    return pl.pallas_call(
        body, name="mix_out_fwd",
        grid=(SEQ // tm,),
        in_specs=[pl.BlockSpec((tm, D_MODEL), row), pl.BlockSpec((tm, CONV_W), row), pl.BlockSpec((tm, SSM_W), row),
                  pl.BlockSpec((SSM_W, SSM_W), fixed), pl.BlockSpec((1, SSM_W), fixed),
                  pl.BlockSpec((1, CONV_W), fixed), pl.BlockSpec((1, SSM_W), fixed),
                  pl.BlockSpec((None, D_MODEL, D_MODEL), lambda m: (0, 0, 0))],
        out_specs=pl.BlockSpec((tm, D_MODEL), row),
        out_shape=jax.ShapeDtypeStruct((SEQ, D_MODEL), F32),
        compiler_params=_cparams(("parallel",)),
    )(*_in_hbm(h, ya, ys, gluw, glub, con, son, wout))


def _mix_out_bwd(dh, ya, ys, gluw, glub, con, son, wout):
    tm = TOK_TILE

    def body(dh_ref, ya_ref, ys_ref, gw_ref, gb_ref, con_ref, son_ref, wo_ref,
             dya_ref, dys_ref, yc_ref, dhb_ref, zg_ref, dq_ref, part_ref):
        ysv = ys_ref[...]
        zg, th = _gelu(ysv)
        zg_bf = zg.astype(BF16)
        s = _sigmoid(_dot(zg_bf, gw_ref[...]) + gb_ref[...])
        out_s = zg * s
        na, ra = _rms_stats(ya_ref[...])
        ns, rs = _rms_stats(out_s)
        dh_bf = dh_ref[...].astype(BF16)
        yc_ref[0, :, 0:CONV_W] = (na * con_ref[...]).astype(BF16)
        yc_ref[0, :, CONV_W:2 * CONV_W] = (ns * son_ref[...]).astype(BF16)
        dhb_ref[0] = dh_bf
        dca = _dot_nt(dh_bf, wo_ref[0:CONV_W, :])
        dcs = _dot_nt(dh_bf, wo_ref[CONV_W:2 * CONV_W, :])
        dya, dcon = _rms_bwd(dca, na, ra, con_ref[...])
        dos, dson = _rms_bwd(dcs, ns, rs, son_ref[...])
        dya_ref[...] = dya
        dq = dos * zg * s * (1.0 - s)
        dq_bf = dq.astype(BF16)
        dzg = dos * s + _dot_nt(dq_bf, gw_ref[...])
        dys_ref[...] = dzg * _gelu_grad(ysv, th)
        zg_ref[0] = zg_bf
        dq_ref[0] = dq_bf
        rid = _row_ids(SUBLANES, SSM_W)
        part = jnp.zeros((SUBLANES, SSM_W), F32)
        for i, rowv in enumerate((dcon, dson, jnp.sum(dq, axis=0, keepdims=True))):
            part = jnp.where(rid == i, jnp.broadcast_to(rowv, (SUBLANES, SSM_W)), part)
        _accumulate(part_ref, pl.program_id(0) == 0, part)

    row = lambda m: (m, 0)
    fixed = lambda m: (0, 0)
    lead = lambda m: (0, m, 0)
    return _out_hbm(pl.pallas_call(
        body, name="mix_out_bwd",
        grid=(SEQ // tm,),
        in_specs=[pl.BlockSpec((tm, D_MODEL), row), pl.BlockSpec((tm, CONV_W), row), pl.BlockSpec((tm, SSM_W), row),
                  pl.BlockSpec((SSM_W, SSM_W), fixed), pl.BlockSpec((1, SSM_W), fixed),
                  pl.BlockSpec((1, CONV_W), fixed), pl.BlockSpec((1, SSM_W), fixed),
                  pl.BlockSpec((None, D_MODEL, D_MODEL), lambda m: (0, 0, 0))],
        out_specs=[pl.BlockSpec((tm, CONV_W), row), pl.BlockSpec((tm, SSM_W), row),
                   pl.BlockSpec((1, tm, D_MODEL), lead), pl.BlockSpec((1, tm, D_MODEL), lead),
                   pl.BlockSpec((1, tm, SSM_W), lead), pl.BlockSpec((1, tm, SSM_W), lead),
                   pl.BlockSpec((8, SSM_W), fixed)],
        out_shape=[pltpu.HBM((SEQ, CONV_W), F32), pltpu.HBM((SEQ, SSM_W), F32),
                   pltpu.HBM((1, SEQ, D_MODEL), BF16), pltpu.HBM((1, SEQ, D_MODEL), BF16),
                   pltpu.HBM((1, SEQ, SSM_W), BF16), pltpu.HBM((1, SEQ, SSM_W), BF16),
                   jax.ShapeDtypeStruct((8, SSM_W), F32)],
        compiler_params=_cparams(("arbitrary",)),
    )(*_in_hbm(dh, ya, ys, gluw, glub, con, son, wout)), (0, 1, 2, 3, 4, 5))


def _ple_fwd(h, g, p, wgate, wprojt):
    tm = TOK_TILE

    def body(h_ref, g_ref, p_ref, wg_ref, wp_ref, o_ref):
        x = h_ref[...]
        xh, _ = _rms_stats(x)
        s = _sigmoid(_dot((xh * g_ref[...]).astype(BF16), wg_ref[...]))
        o_ref[...] = x + _dot_nt(p_ref[...].astype(BF16), wp_ref[...]) * s

    row = lambda m: (m, 0)
    fixed = lambda m: (0, 0)
    return pl.pallas_call(
        body, name="ple_fwd",
        grid=(SEQ // tm,),
        in_specs=[pl.BlockSpec((tm, D_MODEL), row), pl.BlockSpec((1, D_MODEL), fixed), pl.BlockSpec((tm, PLE_DIM), row),
                  pl.BlockSpec((None, D_MODEL, D_MODEL), lambda m: (0, 0, 0)), pl.BlockSpec((D_MODEL, PLE_DIM), fixed)],
        out_specs=pl.BlockSpec((tm, D_MODEL), row),
        out_shape=jax.ShapeDtypeStruct((SEQ, D_MODEL), F32),
        compiler_params=_cparams(("parallel",)),
    )(*_in_hbm(h, g, p, wgate, wprojt))


def _ple_bwd(h, g, p, dh, wgate, wprojt):
    tm = TOK_TILE

    def body(h_ref, g_ref, p_ref, dh_ref, wg_ref, wp_ref, o_ref, u_ref, dq_ref, dpp_ref, pb_ref, dg_ref):
        xh, r = _rms_stats(h_ref[...])
        u = (xh * g_ref[...]).astype(BF16)
        s = _sigmoid(_dot(u, wg_ref[...]))
        p_bf = p_ref[...].astype(BF16)
        pp = _dot_nt(p_bf, wp_ref[...])
        dhv = dh_ref[...]
        dq = (dhv * pp * s * (1.0 - s)).astype(BF16)
        u_ref[0] = u
        dq_ref[0] = dq
        dpp_ref[0] = (dhv * s).astype(BF16)
        pb_ref[0] = p_bf
        dx, dg = _rms_bwd(_dot_nt(dq, wg_ref[...]), xh, r, g_ref[...])
        o_ref[...] = dhv + dx
        _accumulate(dg_ref, pl.program_id(0) == 0, dg)

    row = lambda m: (m, 0)
    fixed = lambda m: (0, 0)
    lead = lambda m: (0, m, 0)
    big = pltpu.HBM((1, SEQ, D_MODEL), BF16)
    return _out_hbm(pl.pallas_call(
        body, name="ple_bwd",
        grid=(SEQ // tm,),
        in_specs=[pl.BlockSpec((tm, D_MODEL), row), pl.BlockSpec((1, D_MODEL), fixed), pl.BlockSpec((tm, PLE_DIM), row),
                  pl.BlockSpec((tm, D_MODEL), row),
                  pl.BlockSpec((None, D_MODEL, D_MODEL), lambda m: (0, 0, 0)), pl.BlockSpec((D_MODEL, PLE_DIM), fixed)],
        out_specs=[pl.BlockSpec((tm, D_MODEL), row),
                   pl.BlockSpec((1, tm, D_MODEL), lead), pl.BlockSpec((1, tm, D_MODEL), lead),
                   pl.BlockSpec((1, tm, D_MODEL), lead), pl.BlockSpec((1, tm, PLE_DIM), lead),
                   pl.BlockSpec((1, D_MODEL), fixed)],
        out_shape=[jax.ShapeDtypeStruct((SEQ, D_MODEL), F32), big, big, big,
                   pltpu.HBM((1, SEQ, PLE_DIM), BF16),
                   jax.ShapeDtypeStruct((1, D_MODEL), F32)],
        compiler_params=_cparams(("arbitrary",)),
    )(*_in_hbm(h, g, p, dh, wgate, wprojt)), (1, 2, 3, 4))


def _final_loss(h, g, target):
    tm = TOK_TILE

    def body(h_ref, g_ref, t_ref, loss_ref, dh_ref, dg_ref):
        first = pl.program_id(0) == 0
        xh, r = _rms_stats(h_ref[...])
        diff = xh * g_ref[...] - t_ref[...]
        part = 0.5 * jnp.sum(jnp.mean(diff * diff, axis=-1, keepdims=True), axis=0, keepdims=True)
        _accumulate(loss_ref, first, jnp.broadcast_to(part, (SUBLANES, LANES)))
        dx, dg = _rms_bwd(diff * (1.0 / D_MODEL), xh, r, g_ref[...])
        dh_ref[...] = dx
        _accumulate(dg_ref, first, dg)

    row = lambda m: (m, 0)
    fixed = lambda m: (0, 0)
    return pl.pallas_call(
        body, name="final_loss",
        grid=(SEQ // tm,),
        in_specs=[pl.BlockSpec((tm, D_MODEL), row), pl.BlockSpec((1, D_MODEL), fixed),
                  pl.BlockSpec((tm, D_MODEL), row)],
        out_specs=[pl.BlockSpec((SUBLANES, LANES), fixed),
                   pl.BlockSpec((tm, D_MODEL), row),
                   pl.BlockSpec((1, D_MODEL), fixed)],
        out_shape=[jax.ShapeDtypeStruct((SUBLANES, LANES), F32),
                   jax.ShapeDtypeStruct((SEQ, D_MODEL), F32),
                   jax.ShapeDtypeStruct((1, D_MODEL), F32)],
        compiler_params=_cparams(("arbitrary",)),
    )(*_in_hbm(h, g, target))


def _disc(ar, ai, ldt):
    dt = jnp.exp(ldt)
    mag = jnp.exp(ar * dt)
    ph = ai * dt
    lr, li = mag * jnp.cos(ph), mag * jnp.sin(ph)
    nr, ni = lr - 1.0, li
    den = ar * ar + ai * ai
    return lr, li, (nr * ar + ni * ai) / den, (ni * ar - nr * ai) / den


def _s5_disc(a_row, ldt_row, a_rep, ldt_rep, bt, ct, tile_e, mask):
    n = N_STATE

    def body(ar_ref, lr_ref, ap_ref, lp_ref, b_ref, c_ref, e_ref, m_ref, lt_ref, ltr_ref, bb_ref, cc_ref):
        lr, li, _, _ = _disc(ar_ref[0], ar_ref[1], lr_ref[...])
        pr, pi = lr, li
        rid = _row_ids(SUBLANES, n)
        for k in range(1, 9):
            for ref, sgn, edge in ((lt_ref, 1.0, 24 + k - 1), (ltr_ref, -1.0, 24 + 8 - k)):
                if k in (1, 2, 4):
                    r0 = {1: 0, 2: 8, 4: 16}[k]
                    keep = (rid >= k) if ref is lt_ref else (rid < SUBLANES - k)
                    ref[r0:r0 + 8, 0:n] = jnp.where(keep, jnp.broadcast_to(pr, (8, n)), 0.0)
                    ref[r0:r0 + 8, n:2 * n] = jnp.where(keep, jnp.broadcast_to(sgn * pi, (8, n)), 0.0)
                ref[edge:edge + 1, 0:n] = pr
                ref[edge:edge + 1, n:2 * n] = sgn * pi
            pr, pi = pr * lr - pi * li, pr * li + pi * lr
        _, _, fr, fi = _disc(ap_ref[0], ap_ref[1], lp_ref[...])
        br, bi = b_ref[0], b_ref[1]
        e = e_ref[...]
        m = m_ref[...].astype(F32)
        bb_ref[:, 0:n] = (_dot((fr * br - fi * bi).astype(BF16), e) * m).astype(BF16)
        bb_ref[:, n:2 * n] = (_dot((fr * bi + fi * br).astype(BF16), e) * m).astype(BF16)
        cc_ref[:, 0:n] = (_dot(c_ref[0].astype(BF16), e) * m).astype(BF16)
        cc_ref[:, n:2 * n] = (-(_dot(c_ref[1].astype(BF16), e) * m)).astype(BF16)

    return pl.pallas_call(
        body, name="s5_disc",
        out_shape=[jax.ShapeDtypeStruct((32, 2 * n), F32), jax.ShapeDtypeStruct((32, 2 * n), F32),
                   jax.ShapeDtypeStruct((SSM_W, 2 * n), BF16), jax.ShapeDtypeStruct((SSM_W, 2 * n), BF16)],
        compiler_params=_cparams(None),
    )(a_row, ldt_row, a_rep, ldt_rep, bt, ct, tile_e, mask)


def _dot_exact(x, sel):
    hi = x.astype(BF16)
    r1 = x - hi.astype(F32)
    mid = r1.astype(BF16)
    lo = (r1 - mid.astype(F32)).astype(BF16)
    return _dot(hi, sel) + _dot(mid, sel) + _dot(lo, sel)


def _s5_disc_bwd(a, ldt, a_rep, ldt_rep, bt, mask, dl, d_bb, d_cc, fold):
    n = N_STATE

    def body(a_ref, l_ref, ap_ref, lp_ref, b_ref, m_ref, dl_ref, dbb_ref, dcc_ref, f_ref,
             da_ref, dldt_ref, db_ref, dc_ref):
        m = m_ref[...].astype(F32)
        fold_m = f_ref[...]
        diag = lambda x: _dot_exact(x * m, fold_m)
        dr, di = diag(dbb_ref[:, 0:n]), diag(dbb_ref[:, n:2 * n])
        dc_ref[0] = diag(dcc_ref[:, 0:n])
        dc_ref[1] = -diag(dcc_ref[:, n:2 * n])
        _, _, fr, fi = _disc(ap_ref[0], ap_ref[1], lp_ref[...])
        br, bi = b_ref[0], b_ref[1]
        db_ref[0] = fr * dr + fi * di
        db_ref[1] = fr * di - fi * dr
        per_state = lambda x: x.reshape(SSM_GROUPS, SSM_GROUP, SSM_STATE).sum(axis=1)
        dfr = per_state(dr * br + di * bi)
        dfi = per_state(di * br - dr * bi)
        _, vjp = jax.vjp(_disc, a_ref[0], a_ref[1], l_ref[...])
        dar, dai, dldt = vjp((dl_ref[0], dl_ref[1], dfr, dfi))
        da_ref[0] = dar
        da_ref[1] = dai
        dldt_ref[...] = jnp.sum(dldt, axis=1, keepdims=True)

    return pl.pallas_call(
        body, name="s5_disc_bwd",
        out_shape=[jax.ShapeDtypeStruct((2, SSM_GROUPS, SSM_STATE), F32),
                   jax.ShapeDtypeStruct((SSM_GROUPS, 1), F32),
                   jax.ShapeDtypeStruct((2, SSM_W, SSM_STATE), F32),
                   jax.ShapeDtypeStruct((2, SSM_W, SSM_STATE), F32)],
        compiler_params=_cparams(None),
    )(a, ldt, a_rep, ldt_rep, bt, mask, dl, d_bb, d_cc, fold)


def _row_block(rows, cap=512):
    for bm in range(min(cap, rows), 0, -1):
        if rows % bm == 0 and (bm % 8 == 0 or bm == rows):
            return bm
    return rows


def _pair_sum(fulls, got, segs):
    ns = len(segs)
    offs = _seg_offsets(segs)
    _, rtot, c = got.shape
    parts = 2
    pr = rtot // parts
    assert pr * parts == rtot and pr % 16 == 0
    pieces = [[] for _ in range(parts)]
    for a, (n, r) in enumerate(segs):
        for m in range(n):
            lo = offs[a] + m * r
            for h in range(parts):
                clo, chi = max(lo, h * pr), min(lo + r, (h + 1) * pr)
                if chi > clo:
                    pieces[h].append((a, m, clo - lo, clo - h * pr, chi - clo))
    n_sems = max(len(ps) for ps in pieces)

    def body(*refs):
        srcs = refs[:ns]
        got_ref, p32_ref, pbf_ref, own_v, sems = refs[ns:]
        h = pl.program_id(0)
        k = pl.program_id(1)
        dev = 2 * k + lax.axis_index("c")
        for hh in range(parts):
            @pl.when(h == hh)
            def _(hh=hh):
                cps = []
                for i, (a, m, so, do, rows) in enumerate(pieces[hh]):
                    start = pl.multiple_of(dev * segs[a][1] + so, 16)
                    cps.append(pltpu.make_async_copy(srcs[a].at[m, pl.ds(start, rows), :],
                                                     own_v.at[pl.ds(do, rows), :], sems.at[i]))
                for cp in cps:
                    cp.start()
                for cp in cps:
                    cp.wait()
        s = own_v[...].astype(F32) + got_ref[0].astype(F32)
        pbf_ref[0] = s.astype(BF16)

        @pl.when(k == 2 * lax.axis_index("x") + lax.axis_index("y"))
        def _():
            p32_ref[...] = s

    spec = pl.BlockSpec((1, pr, c), lambda h, k: (k, h, 0))
    return pl.pallas_call(
        body, name="pair_sum",
        grid=(parts, 4),
        in_specs=[HBM] * ns + [spec], out_specs=[pl.BlockSpec((pr, c), lambda h, k: (h, 0)), spec],
        out_shape=[pltpu.HBM((rtot, c), F32), pltpu.HBM(got.shape, BF16)],
        scratch_shapes=[pltpu.VMEM((pr, c), BF16), pltpu.SemaphoreType.DMA((n_sems,))],
        compiler_params=_cparams(("arbitrary", "arbitrary")),
    )(*_in_hbm(*fulls, got))


def _chip_sum(own, rb):
    r, c = own.shape
    bm = _row_block(r)

    def body(o_ref, r_ref, s_ref):
        s_ref[...] = ((o_ref[...] + r_ref[0].astype(F32)) + r_ref[1].astype(F32)) + r_ref[2].astype(F32)

    return pl.pallas_call(
        body, name="chip_sum",
        grid=(r // bm,),
        in_specs=[pl.BlockSpec((bm, c), lambda k: (k, 0)), pl.BlockSpec((3, bm, c), lambda k: (0, k, 0))],
        out_specs=pl.BlockSpec((bm, c), lambda k: (k, 0)),
        out_shape=jax.ShapeDtypeStruct((r, c), F32),
        compiler_params=_cparams(("parallel",)),
    )(*_in_hbm(own, rb))


def _sum8(x, after):
    _, r, c = x.shape
    bm = _row_block(r)

    def body(x_ref, after_ref, s_ref):
        s = x_ref[0]
        for d in range(1, N_DEV):
            s = s + x_ref[d]
        s_ref[...] = s

    return pl.pallas_call(
        body, name="sum8",
        grid=(r // bm,),
        in_specs=[pl.BlockSpec((N_DEV, bm, c), lambda k: (0, k, 0)), UNREAD],
        out_specs=pl.BlockSpec((bm, c), lambda k: (k, 0)),
        out_shape=jax.ShapeDtypeStruct((r, c), F32),
        compiler_params=_cparams(("parallel",)),
    )(*_in_hbm(x, after))


def _adamw(w, g, m, v):
    r, c = w.shape
    bm = _row_block(r)
    bc1 = 1.0 - ADAM_B1 ** ADAM_STEP
    bc2 = 1.0 - ADAM_B2 ** ADAM_STEP

    def body(w_ref, g_ref, m_ref, v_ref, d_ref, nm_ref, nv_ref):
        gv = g_ref[...]
        nm = ADAM_B1 * m_ref[...] + (1.0 - ADAM_B1) * gv
        nv = ADAM_B2 * v_ref[...] + (1.0 - ADAM_B2) * (gv * gv)
        nm_ref[...] = nm
        nv_ref[...] = nv
        d_ref[...] = -ADAM_LR * ((nm / bc1) / (jnp.sqrt(nv / bc2) + ADAM_EPS) + ADAM_WD * w_ref[...])

    spec = pl.BlockSpec((bm, c), lambda k: (k, 0))
    shp = jax.ShapeDtypeStruct((r, c), F32)
    return pl.pallas_call(
        body, name="adamw",
        grid=(r // bm,),
        in_specs=[spec] * 4, out_specs=[spec] * 3, out_shape=[shp] * 3,
        compiler_params=_cparams(("parallel",)),
    )(*_in_hbm(w, g, m, v))


def _mesh_pos():
    return lax.axis_index("x"), lax.axis_index("y"), lax.axis_index("c")


def _dev_index(p):
    return 4 * p[0] + 2 * p[1] + p[2]


def _seg_offsets(segs):
    offs, o = [], 0
    for n, r in segs:
        offs.append(o)
        o += n * r
    return offs


def _remote(src, dst, send_sem, recv_sem, to):
    return pltpu.make_async_remote_copy(src_ref=src, dst_ref=dst, send_sem=send_sem, recv_sem=recv_sem,
                                        device_id=to, device_id_type=MESH)


def _allgather(pack, segs, name):
    rtot, c = pack.shape
    ns = len(segs)
    offs = _seg_offsets(segs)
    assert rtot == sum(n * r for n, r in segs)

    def body(pack_ref, *refs):
        outs = refs[:ns]
        send_sems, recv_sems, local_sem = refs[ns:]
        x, y, cc = _mesh_pos()
        me, sib = (x, y, cc), (x, y, 1 - cc)
        chips = [(1 - x, y), (x, 1 - y), (1 - x, 1 - y)]

        def pieces(dev, from_pack):
            res = []
            for a, (n, r) in enumerate(segs):
                for m in range(n):
                    dst = outs[a].at[m, pl.ds(pl.multiple_of(dev * r, r), r), :]
                    src = pack_ref.at[pl.ds(offs[a] + m * r, r), :] if from_pack else dst
                    res.append((src, dst))
            return res

        def push(k, dev, to, from_pack):
            for s, d in pieces(dev, from_pack):
                _remote(s, d, send_sems.at[k], recv_sems.at[k], to).start()

        def whole(k):
            return _remote(pack_ref, pack_ref, send_sems.at[k], recv_sems.at[k], me)

        my_dev = _dev_index(me)
        for s, d in pieces(my_dev, True):
            pltpu.make_async_copy(s, d, local_sem).start()
        push(0, my_dev, sib, True)
        for j, chip in enumerate(chips):
            push(1 + j, my_dev, (*chip, cc), True)
        for j, chip in enumerate(chips):
            whole(1 + j).wait_recv()
            push(4 + j, _dev_index((*chip, cc)), sib, False)
        whole(0).wait_recv()
        for j in range(3):
            whole(4 + j).wait_recv()
        for k in range(7):
            whole(k).wait_send()
        pltpu.make_async_copy(pack_ref, pack_ref, local_sem).wait()

    return pl.pallas_call(
        body, name=name,
        in_specs=[HBM], out_specs=[HBM] * ns,
        out_shape=[jax.ShapeDtypeStruct((n, N_DEV * r, c), pack.dtype) for n, r in segs],
        scratch_shapes=[pltpu.SemaphoreType.DMA((7,)), pltpu.SemaphoreType.DMA((7,)), pltpu.SemaphoreType.DMA],
    )(pack)


HBM = pl.BlockSpec(memory_space=pltpu.HBM)
SEM = pl.BlockSpec(memory_space=pltpu.SEMAPHORE)
VMEM_WHOLE = pl.BlockSpec(memory_space=pltpu.VMEM)
EFFECT = pltpu.SideEffectType.DATAFLOW_SIDE_EFFECTING


def _hbm(a):
    return pltpu.with_memory_space_constraint(a, pltpu.HBM)


def _ag_start(pack, segs, after, name):
    rtot, c = pack.shape
    ns = len(segs)
    offs = _seg_offsets(segs)

    def body(pack_ref, *refs):
        lands = refs[:ns]
        send_sems, recv_sems = refs[ns + 1], refs[ns + 2]
        token = refs[-1]
        x, y, cc = _mesh_pos()
        my_dev = _dev_index((x, y, cc))
        targets = [(x, y, 1 - cc), (1 - x, y, cc), (x, 1 - y, cc), (1 - x, 1 - y, cc)]
        for k, to in enumerate(targets):
            for a, (n, r) in enumerate(segs):
                for m in range(n):
                    _remote(pack_ref.at[pl.ds(offs[a] + m * r, r), :],
                            lands[a].at[m, pl.ds(pl.multiple_of(my_dev * r, r), r), :],
                            send_sems.at[k], recv_sems.at[k], to).start()
        token[...] = jnp.zeros_like(token)

    land_shapes = [(n, N_DEV * r, c) for n, r in segs]
    outs = pl.pallas_call(
        body, name=name,
        in_specs=[HBM] * (1 + ns) + [UNREAD],
        out_specs=[SEM, SEM, HBM] + [HBM] * ns + [VMEM_WHOLE],
        out_shape=[pltpu.SemaphoreType.DMA((4,)), pltpu.SemaphoreType.DMA((4,)), pltpu.HBM(pack.shape, pack.dtype)]
        + [pltpu.HBM(s, pack.dtype) for s in land_shapes] + [jax.ShapeDtypeStruct((SUBLANES, LANES), F32)],
        input_output_aliases={0: 2, **{1 + i: 3 + i for i in range(ns)}},
        compiler_params=pltpu.CompilerParams(has_side_effects=EFFECT),
    )(_hbm(pack), *[_hbm(lax.empty(s, pack.dtype)) for s in land_shapes], _hbm(after))
    return outs[0], outs[1], outs[2], list(outs[3:3 + ns]), outs[-1]


def _ag_wait(send_sems, recv_sems, pack, lands, after, name):
    ns = len(lands)

    def body(pack_ref, *refs):
        send_ref, recv_ref = refs[ns], refs[ns + 1]
        me = _mesh_pos()
        for k in range(4):
            whole = _remote(pack_ref, pack_ref, send_ref.at[k], recv_ref.at[k], me)
            whole.wait_send()
            whole.wait_recv()

    outs = pl.pallas_call(
        body, name=name,
        in_specs=[HBM] * (1 + ns) + [SEM, SEM, UNREAD],
        out_specs=[HBM] * (1 + ns),
        out_shape=[pltpu.HBM(pack.shape, pack.dtype)] + [pltpu.HBM(a.shape, a.dtype) for a in lands],
        input_output_aliases={i: i for i in range(1 + ns)},
        compiler_params=pltpu.CompilerParams(has_side_effects=EFFECT),
    )(pack, *lands, send_sems, recv_sems, _hbm(after))
    return outs[0], list(outs[1:])


def _ag_finish(pack, lands, segs):
    rtot, c = pack.shape
    ns = len(segs)
    offs = _seg_offsets(segs)

    def body(pack_ref, *refs):
        outs = refs[ns:2 * ns]
        stage, send_sems, recv_sems, local_sems = refs[2 * ns:]
        x, y, cc = _mesh_pos()
        me, sib = (x, y, cc), (x, y, 1 - cc)
        chips = [(1 - x, y), (x, 1 - y), (1 - x, 1 - y)]

        def rows(a, m, dev):
            return outs[a].at[m, pl.ds(pl.multiple_of(dev * segs[a][1], segs[a][1]), segs[a][1]), :]

        for j, chip in enumerate(chips):
            dev = _dev_index((*chip, cc))
            for a, (n, r) in enumerate(segs):
                for m in range(n):
                    _remote(rows(a, m, dev), rows(a, m, dev), send_sems.at[j], recv_sems.at[j], sib).start()
        load = pltpu.make_async_copy(pack_ref, stage, local_sems.at[0])
        load.start()
        load.wait()
        my_dev = _dev_index(me)
        for a, (n, r) in enumerate(segs):
            for m in range(n):
                pltpu.make_async_copy(stage.at[pl.ds(offs[a] + m * r, r), :], rows(a, m, my_dev), local_sems.at[1]).start()
        pltpu.make_async_copy(stage, pack_ref, local_sems.at[1]).wait()
        for j in range(3):
            _remote(pack_ref, pack_ref, send_sems.at[j], recv_sems.at[j], me).wait()

    outs = pl.pallas_call(
        body, name="ag_finish",
        in_specs=[HBM] * (1 + ns), out_specs=[HBM] * ns,
        out_shape=[pltpu.HBM(a.shape, a.dtype) if r >= 128 else jax.ShapeDtypeStruct(a.shape, a.dtype)
                   for a, (_, r) in zip(lands, segs)],
        input_output_aliases={1 + i: i for i in range(ns)},
        scratch_shapes=[pltpu.VMEM((rtot, c), pack.dtype), pltpu.SemaphoreType.DMA((3,)),
                        pltpu.SemaphoreType.DMA((3,)), pltpu.SemaphoreType.DMA((2,))],
        compiler_params=_cparams(None, 16),
    )(pack, *lands)
    return list(outs)


def _rs_chips_start(pbf, after, name):
    _, rtot, c = pbf.shape

    def body(pbf_ref, land_ref, after_ref, send_sems, recv_sems, pbf_thru, land_thru, token):
        x, y, cc = _mesh_pos()
        for j, (cx, cy) in enumerate([(1 - x, y), (x, 1 - y), (1 - x, 1 - y)]):
            _remote(pbf_ref.at[2 * cx + cy], land_ref.at[j], send_sems.at[j], recv_sems.at[j], (cx, cy, cc)).start()
        token[...] = jnp.zeros_like(token)

    return pl.pallas_call(
        body, name=name,
        in_specs=[HBM, HBM, UNREAD],
        out_specs=[SEM, SEM, HBM, HBM, VMEM_WHOLE],
        out_shape=[pltpu.SemaphoreType.DMA((3,)), pltpu.SemaphoreType.DMA((3,)), pltpu.HBM(pbf.shape, pbf.dtype),
                   pltpu.HBM((3, rtot, c), pbf.dtype), jax.ShapeDtypeStruct((SUBLANES, LANES), F32)],
        input_output_aliases={0: 2, 1: 3},
        compiler_params=pltpu.CompilerParams(has_side_effects=EFFECT),
    )(_hbm(pbf), _hbm(lax.empty((3, rtot, c), pbf.dtype)), _hbm(after))


def _rs_chips_wait(send_sems, recv_sems, pbf, land, after, name):
    def body(pbf_ref, land_ref, send_ref, recv_ref, after_ref, pbf_out, land_out):
        me = _mesh_pos()
        for j in range(3):
            cp = _remote(pbf_ref.at[0], land_ref.at[j], send_ref.at[j], recv_ref.at[j], me)
            cp.wait_send()
            cp.wait_recv()

    return pl.pallas_call(
        body, name=name,
        in_specs=[HBM, HBM, SEM, SEM, UNREAD], out_specs=[HBM, HBM],
        out_shape=[pltpu.HBM(pbf.shape, pbf.dtype), pltpu.HBM(land.shape, land.dtype)],
        input_output_aliases={0: 0, 1: 1},
        compiler_params=pltpu.CompilerParams(has_side_effects=EFFECT),
    )(pbf, land, send_sems, recv_sems, _hbm(after))[1]


def _rs_sibling_start(fulls, segs, name):
    ns = len(segs)
    offs = _seg_offsets(segs)
    rtot = sum(n * r for n, r in segs)
    c = fulls[0].shape[-1]
    dt = fulls[0].dtype

    def body(*refs):
        srcs = refs[:ns]
        land_ref, send_sem, recv_sem = refs[ns], refs[ns + 1], refs[ns + 2]
        token = refs[-1]
        x, y, cc = _mesh_pos()
        for k in range(4):
            for a, (n, r) in enumerate(segs):
                for m in range(n):
                    theirs = srcs[a].at[m, pl.ds(pl.multiple_of((2 * k + 1 - cc) * r, r), r), :]
                    _remote(theirs, land_ref.at[k, pl.ds(offs[a] + m * r, r), :], send_sem, recv_sem,
                            (x, y, 1 - cc)).start()
        token[...] = jnp.zeros_like(token)

    outs = pl.pallas_call(
        body, name=name,
        in_specs=[HBM] * (ns + 1),
        out_specs=[SEM, SEM] + [HBM] * (ns + 1) + [VMEM_WHOLE],
        out_shape=[pltpu.SemaphoreType.DMA(()), pltpu.SemaphoreType.DMA(())]
        + [pltpu.HBM(a.shape, a.dtype) for a in fulls] + [pltpu.HBM((4, rtot, c), dt),
                                                           jax.ShapeDtypeStruct((SUBLANES, LANES), F32)],
        input_output_aliases={i: 2 + i for i in range(ns + 1)},
        compiler_params=pltpu.CompilerParams(has_side_effects=EFFECT),
    )(*[_hbm(a) for a in fulls], _hbm(lax.empty((4, rtot, c), dt)))
    return outs[0], outs[1], list(outs[2:2 + ns]), outs[2 + ns], outs[-1]


def _rs_sibling_wait(send_sem, recv_sem, fulls, land, after, name):
    ns = len(fulls)

    def body(*refs):
        land_ref, send_ref, recv_ref = refs[ns], refs[ns + 1], refs[ns + 2]
        whole = _remote(land_ref, land_ref, send_ref, recv_ref, _mesh_pos())
        whole.wait_send()
        whole.wait_recv()

    outs = pl.pallas_call(
        body, name=name,
        in_specs=[HBM] * (ns + 1) + [SEM, SEM, UNREAD], out_specs=[HBM] * (ns + 1),
        out_shape=[pltpu.HBM(a.shape, a.dtype) for a in fulls] + [pltpu.HBM(land.shape, land.dtype)],
        input_output_aliases={i: i for i in range(ns + 1)},
        compiler_params=pltpu.CompilerParams(has_side_effects=EFFECT),
    )(*fulls, land, send_sem, recv_sem, _hbm(after))
    return list(outs[:ns]), outs[ns]


def _tp(w):
    return jnp.swapaxes(w, -1, -2)


def _s5_prepare(a_re, a_im, log_dt, b_re, b_im, c_re, c_im):
    a = jnp.stack([a_re, a_im], axis=1)
    ldt = jnp.broadcast_to(log_dt[:, :, None], (DEPTH, SSM_GROUPS, SSM_STATE))
    a_row = a.reshape(DEPTH, 2, 1, N_STATE)
    ldt_row = ldt.reshape(DEPTH, 1, N_STATE)
    a_rep = jnp.repeat(a, SSM_GROUP, axis=2)
    ldt_rep = jnp.repeat(ldt, SSM_GROUP, axis=1)
    bt = jnp.stack([_tp(b_re), _tp(b_im)], axis=1).reshape(DEPTH, 2, SSM_W, SSM_STATE)
    ct = jnp.stack([c_re, c_im], axis=1).reshape(DEPTH, 2, SSM_W, SSM_STATE)
    tile_e = jnp.tile(jnp.eye(SSM_STATE, dtype=BF16), (1, SSM_GROUPS))
    mask = jnp.repeat(jnp.repeat(jnp.eye(SSM_GROUPS, dtype=BF16), SSM_GROUP, axis=0), SSM_STATE, axis=1)
    out = []
    for l in range(DEPTH):
        tabs = _s5_disc(a_row[l], ldt_row[l], a_rep[l], ldt_rep[l], bt[l], ct[l], tile_e, mask)
        out.append(((a[l], ldt[l], a_rep[l], ldt_rep[l], bt[l], mask), *tabs))
    return out


def _layer_fwd(h, p_l, small, big, arrive=None):
    saved = {'h0': h}
    if arrive is not None:
        arrive(0, h)
    h, saved['gu1'] = _ffn_fwd(h, small['ffn1_norm'], big['ff1'])
    saved['h1'] = h
    if arrive is not None:
        arrive(1, h)
    z = _inproj_fwd(h, small['mix_norm'], big['wint'])
    ya, ys, hs = _s5conv_fwd(z, small['conv_w'], small['conv_b'], small['bbmat'], small['ccmat'], small['dvec'],
                             small['ltab'])
    saved.update(z=z, ya=ya, ys=ys, hs=hs)
    h = _mix_out_fwd(h, ya, ys, big['glu'], small['glu_b'], small['conv_out_norm'], small['ssm_out_norm'], big['wout'])
    saved['h2'] = h
    if arrive is not None:
        arrive(2, h)
    h, saved['gu2'] = _ffn_fwd(h, small['ffn2_norm'], big['ff2'])
    saved['h3'] = h
    h = _ple_fwd(h, small['ple_norm'], p_l, big['plg'], big['plpt'])
    return h, saved


def _ffn_bwd(h_in, g, dh, gu, w3):
    dh_in, dga, ud, dg = _ffn_bwd_act(h_in, g, dh, gu, w3)
    return dh_in, _matmul_tn(dga, ud, FF_BLOCK, BF16, "ffn_wgrad"), dg


def _layer_bwd_top(dh, p_l, small, big, saved):
    gs = {}
    dh, u, dq, dpp, pb, gs['ple_norm'] = _ple_bwd(saved['h3'], small['ple_norm'], p_l, dh, big['plg'], big['plpt'])
    d_plg = _matmul_tn(u, dq, 256, BF16, "ple_gate_wgrad")
    d_plpt = _matmul_tn(dpp, pb, 256, BF16, "ple_proj_wgrad", to_kernel=False)
    dh, d_ff2, gs['ffn2_norm'] = _ffn_bwd(saved['h2'], small['ffn2_norm'], dh, saved['gu2'], big['ff2'])
    return dh, (gs, d_plg, d_plpt, d_ff2)


def _layer_bwd_rest(dh, top, small, big, saved):
    gs, d_plg, d_plpt, d_ff2 = top
    dya, dys, ycat, dhb, zg, dq, part = _mix_out_bwd(dh, saved['ya'], saved['ys'], big['glu'], small['glu_b'],
                                                     small['conv_out_norm'], small['ssm_out_norm'], big['wout'])
    d_wout = _matmul_tn(ycat, dhb, 256, BF16, "w_out_wgrad")
    d_glu = _matmul_tn(zg, dq, 256, BF16, "glu_wgrad", to_kernel=False)
    dz, gadj, us, dyb, dl, dcw = _s5conv_bwd(saved['z'], saved['hs'], dya, dys, small['conv_w'], small['conv_b'],
                                             small['bbmat'], small['ccmat'], small['dvec'], small['ltab_rev'])
    d_bb = _matmul_tn(us, gadj, SSM_W, F32, "s5_b_wgrad", 1024, False)[0]
    d_cc = _matmul_tn(dyb, saved['hs'][None], SSM_W, F32, "s5_c_wgrad", 1024, False)[0]
    dh, u, gs['mix_norm'] = _inproj_bwd(saved['h1'], small['mix_norm'], dh, dz, big['wint'])
    d_wint = _matmul_tn(dz[None], u, 256, BF16, "w_in_wgrad")
    dh, d_ff1, gs['ffn1_norm'] = _ffn_bwd(saved['h0'], small['ffn1_norm'], dh, saved['gu1'], big['ff1'])

    dlb = dl[0].reshape(2, SSM_GROUPS, SSM_STATE)
    fold = jnp.tile(jnp.eye(SSM_STATE, dtype=BF16), (SSM_GROUPS, 1))
    da, dldt, dbt, dct = _s5_disc_bwd(*small['disc_in'], dlb, d_bb, d_cc, fold)
    gs['ssm_A_re'], gs['ssm_A_im'] = da[0], da[1]
    gs['ssm_log_dt'] = dldt[:, 0]
    ghp = (SSM_GROUPS, SSM_GROUP, SSM_STATE)
    gs['ssm_B_re'], gs['ssm_B_im'] = dbt[0].reshape(ghp), dbt[1].reshape(ghp)
    gs['ssm_C_re'], gs['ssm_C_im'] = dct[0].reshape(ghp), dct[1].reshape(ghp)
    gs['conv_w'] = dcw[0:3]
    gs['conv_b'] = dcw[3]
    gs['ssm_D'] = dcw[4].reshape(SSM_GROUPS, SSM_GROUP)
    gs['conv_out_norm'], gs['ssm_out_norm'], gs['glu_b'] = part[0], part[1], part[2]
    for n in ('ple_norm', 'ffn2_norm', 'mix_norm', 'ffn1_norm'):
        gs[n] = gs[n][0]
    fulls = [d_ff1, d_ff2, d_wint, d_wout, d_plg,
             d_plpt.reshape(1, D_MODEL * PLE_DIM // D_MODEL, D_MODEL), d_glu.reshape(1, SSM_W * SSM_W // D_MODEL, D_MODEL)]
    return dh, fulls, gs


VIEW_T = ('ffn1_w_gate', 'ffn1_w_up', 'ffn2_w_gate', 'ffn2_w_up', 'ssm_B_re', 'ssm_B_im')


def _view(name, a):
    return _tp(a) if name in VIEW_T else a


SEG_NAMES = ('ff1', 'ff2', 'wint', 'wout', 'plg', 'plpt', 'glu')
FIRST_LAYER_GROUPS = ((0,), (2, 3, 6), (1, 4, 5))


def _layer_pack(W, l, segments=range(len(SEGS))):
    pieces = {
        0: lambda: [_tp(W['ffn1_w_gate'][l]), _tp(W['ffn1_w_up'][l]), W['ffn1_w_down'][l]],
        1: lambda: [_tp(W['ffn2_w_gate'][l]), _tp(W['ffn2_w_up'][l]), W['ffn2_w_down'][l]],
        2: lambda: [_tp(W['w_in'][l])],
        3: lambda: [W['w_out'][l]],
        4: lambda: [W['ple_w_gate'][l]],
        5: lambda: [_tp(W['ple_w_proj'][l]).reshape(-1, D_MODEL)],
        6: lambda: [W['glu_w'][l].reshape(-1, D_MODEL)],
    }
    return jnp.concatenate([a for s in segments for a in pieces[s]()], axis=0).astype(BF16)


def _as_big(named):
    shape = dict(plpt=(D_MODEL, PLE_DIM), glu=(SSM_W, SSM_W))
    return {n: (a.reshape(shape[n]) if n in shape else a) for n, a in named.items()}


def _pad_rows(flat, mult, width=LANES):
    per = mult * width
    n = flat.shape[0]
    tot = -(-n // per) * per
    return jnp.pad(flat, (0, tot - n)).reshape(tot // width, width)


def _adamw_any(w, g, m, v):
    shp = w.shape
    two = (lambda t: t.reshape(-1, shp[-1]))
    d, nm, nv = _adamw(two(w), two(g), two(m), two(v))
    return d.reshape(shp), nm.reshape(shp), nv.reshape(shp)


def kernel(x, p, ffn1_norm, ffn1_w_gate, ffn1_w_up, ffn1_w_down, mix_norm, w_in, conv_w, conv_b, ssm_A_re, ssm_A_im, ssm_B_re, ssm_B_im, ssm_C_re, ssm_C_im, ssm_D, ssm_log_dt, glu_w, glu_b, conv_out_norm, ssm_out_norm, w_out, ffn2_norm, ffn2_w_gate, ffn2_w_up, ffn2_w_down, ple_norm, ple_w_gate, ple_w_proj, final_norm, loss_target, m_ffn1_norm, m_ffn1_w_gate, m_ffn1_w_up, m_ffn1_w_down, m_mix_norm, m_w_in, m_conv_w, m_conv_b, m_ssm_A_re, m_ssm_A_im, m_ssm_B_re, m_ssm_B_im, m_ssm_C_re, m_ssm_C_im, m_ssm_D, m_ssm_log_dt, m_glu_w, m_glu_b, m_conv_out_norm, m_ssm_out_norm, m_w_out, m_ffn2_norm, m_ffn2_w_gate, m_ffn2_w_up, m_ffn2_w_down, m_ple_norm, m_ple_w_gate, m_ple_w_proj, m_final_norm, v_ffn1_norm, v_ffn1_w_gate, v_ffn1_w_up, v_ffn1_w_down, v_mix_norm, v_w_in, v_conv_w, v_conv_b, v_ssm_A_re, v_ssm_A_im, v_ssm_B_re, v_ssm_B_im, v_ssm_C_re, v_ssm_C_im, v_ssm_D, v_ssm_log_dt, v_glu_w, v_glu_b, v_conv_out_norm, v_ssm_out_norm, v_w_out, v_ffn2_norm, v_ffn2_w_gate, v_ffn2_w_up, v_ffn2_w_down, v_ple_norm, v_ple_w_gate, v_ple_w_proj, v_final_norm):
    given = dict(locals())
    W = {n: given[n] for n in W_NAMES}
    M = {n: given['m_' + n] for n in W_NAMES}
    V = {n: given['v_' + n] for n in W_NAMES}
    Wv, Mv, Vv = [{n: _view(n, d[n]) for n in W_NAMES} for d in (W, M, V)]
    my_dev = _dev_index(_mesh_pos())

    conv_shard = _pad_rows(W['conv_w'].reshape(-1), SUBLANES)
    conv_all = _allgather(conv_shard, ((1, SUBLANES),), "ag_conv_w")[0]
    conv_full = conv_all.reshape(N_DEV, -1)[:, :DEPTH * 3 * (CONV_W // N_DEV)]
    conv_full = conv_full.reshape(N_DEV, DEPTH, 3, CONV_W // N_DEV).transpose(1, 2, 0, 3).reshape(DEPTH, 3, CONV_W)
    first, after = [], conv_all
    for gi, segments in enumerate(FIRST_LAYER_GROUPS):
        first.append(_ag_start(_layer_pack(W, 0, segments), tuple(SEGS[s] for s in segments), after,
                               "ag_start_0%s" % "abc"[gi]))
        after = first[-1][4]
    s5 = _s5_prepare(*[W[n] + after[0, 0] for n in ('ssm_A_re', 'ssm_A_im', 'ssm_log_dt')],
                     *[W[n] for n in ('ssm_B_re', 'ssm_B_im', 'ssm_C_re', 'ssm_C_im')])
    packs = [None] + [_layer_pack(W, l) for l in range(1, DEPTH)]
    prepared = conv_full[0, 0:1, 0:1] + s5[DEPTH - 1][1][0:1, 0:1] + packs[DEPTH - 1][0:1, 0:1].astype(F32)

    smalls, saves, bigs = [], [], []
    h = x[0]

    flight = None

    def gathered(handles, segments, after, name, next_layer=None, gate=None):
        nonlocal flight
        send_sems, recv_sems, pack_thru, lands, _ = handles
        pack_thru, lands = _ag_wait(send_sems, recv_sems, pack_thru, lands, after, "ag_wait_" + name)
        if next_layer is not None:
            flight = _ag_start(packs[next_layer], SEGS, pack_thru, "ag_start_%d" % next_layer)
            gate[0][gate[1]] = gate[0][gate[1]] + flight[4][0:1, 0:1]
        outs = _ag_finish(pack_thru, lands, tuple(SEGS[s] for s in segments))
        return _as_big({SEG_NAMES[s]: a for s, a in zip(segments, outs)})

    for l in range(DEPTH):
        small = {n: W[n][l][None] for n in ('ffn1_norm', 'mix_norm', 'conv_b', 'glu_b', 'conv_out_norm',
                                            'ssm_out_norm', 'ffn2_norm', 'ple_norm')}
        small['conv_w'] = conv_full[l]
        small['dvec'] = W['ssm_D'][l].reshape(1, SSM_W)
        small['disc_in'], small['ltab'], small['ltab_rev'], small['bbmat'], small['ccmat'] = s5[l]
        big = {}
        bigs.append(big)
        if l == 0:
            def arrive(stage, h_now, big=big, small=small):
                big.update(gathered(first[stage], FIRST_LAYER_GROUPS[stage], prepared if stage == 0 else h_now,
                                    "0%s" % "abc"[stage], *((1, (small, 'ffn2_norm')) if stage == 2 else ())))
            h, saved = _layer_fwd(h, p[l, 0], small, big, arrive)
        else:
            nxt = (l + 1, (small, 'ffn1_norm')) if l + 1 < DEPTH else ()
            big.update(gathered(flight, range(len(SEGS)), h, "%d" % l, *nxt))
            h, saved = _layer_fwd(h, p[l, 0], small, big)
        smalls.append(small)
        saves.append(saved)
    loss_tile, dh, d_final = _final_loss(h, W['final_norm'][None], loss_target[0])
    loss = lax.psum(loss_tile[0, 0], ("x", "y", "c"))

    layer_gs = [None] * DEPTH
    shard_grads = [None] * DEPTH
    zero = jnp.zeros((1, 1), F32)
    sib, ici = None, None

    def finish_sibling(after_sib, after_ici):
        nonlocal sib, ici
        up, (send_sem, recv_sem, fulls_thru, land, _) = sib
        fulls_thru, got = _rs_sibling_wait(send_sem, recv_sem, fulls_thru, land, after_sib, "sib_wait_%d" % up)
        own32, pbf = _pair_sum(fulls_thru, got, SEGS)
        finish_chips(own32)
        ici = (up, _rs_chips_start(pbf, after_ici, "rs_start_%d" % up), own32)
        sib = None

    def finish_chips(after):
        nonlocal ici
        if ici is not None:
            up, (send_sems, recv_sems, pbf_thru, land, _), own32 = ici
            got3 = _rs_chips_wait(send_sems, recv_sems, pbf_thru, land, after, "rs_wait_%d" % up)
            shard_grads[up] = _chip_sum(own32, got3)
            ici = None

    for l in reversed(range(DEPTH)):
        small = dict(smalls[l])
        if sib is not None:
            small['ple_norm'] = small['ple_norm'] + sib[1][4][0:1, 0:1]
        dh, top = _layer_bwd_top(dh, p[l, 0], small, bigs[l], saves[l])
        if sib is not None:
            finish_sibling(dh, dh)
            small['glu_b'] = small['glu_b'] + ici[1][4][0:1, 0:1]
        dh, fulls, layer_gs[l] = _layer_bwd_rest(dh, top, small, bigs[l], saves[l])
        sib = (l, _rs_sibling_start(fulls, SEGS, "sib_start_%d" % l))
    grad_x = dh[None]

    gs = {n: jnp.stack([layer_gs[l][n] for l in range(DEPTH)]) for n in layer_gs[0]}
    gs['final_norm'] = d_final[0]
    flat = jnp.concatenate([gs[n].reshape(-1) for n in SMALL_NAMES] + [gs['conv_w'].reshape(-1)])
    n_flat = flat.shape[0]
    flat = _pad_rows(flat, 64, D_MODEL) + sib[1][4][0:1, 0:1]
    rows = flat.shape[0]
    gathered = _allgather(flat, ((1, rows),), "ag_small_grads")[0]
    finish_sibling(gathered, gathered)
    red = _sum8(gathered.reshape(N_DEV, rows, D_MODEL), ici[1][4]).reshape(-1)[:n_flat]
    finish_chips(red)
    G = {}
    o = 0
    for n in SMALL_NAMES:
        G[n] = red[o:o + W[n].size].reshape(Wv[n].shape)
        o += W[n].size
    conv_g_full = red[o:].reshape(DEPTH, 3, CONV_W)
    G['conv_w'] = lax.dynamic_slice_in_dim(conv_g_full, my_dev * (CONV_W // N_DEV), CONV_W // N_DEV, axis=2)

    sg = jnp.stack(shard_grads)
    offs = _seg_offsets(SEGS)
    r = SEGS[0][1]
    for a, f in ((0, 'ffn1'), (1, 'ffn2')):
        G[f + '_w_gate'] = sg[:, offs[a]:offs[a] + r]
        G[f + '_w_up'] = sg[:, offs[a] + r:offs[a] + 2 * r]
        G[f + '_w_down'] = sg[:, offs[a] + 2 * r:offs[a] + 3 * r]
    G['w_in'] = _tp(sg[:, offs[2]:offs[2] + SEGS[2][1]])
    G['w_out'] = sg[:, offs[3]:offs[3] + SEGS[3][1]]
    G['ple_w_gate'] = sg[:, offs[4]:offs[4] + SEGS[4][1]]
    G['ple_w_proj'] = _tp(sg[:, offs[5]:offs[5] + SEGS[5][1]].reshape(DEPTH, D_MODEL // N_DEV, PLE_DIM))
    G['glu_w'] = sg[:, offs[6]:offs[6] + SEGS[6][1]].reshape(DEPTH, SSM_W // N_DEV, SSM_W)

    delta, new_m, new_v = {}, {}, {}
    cat = lambda src: _pad_rows(jnp.concatenate([src[n].reshape(-1) for n in SMALL_NAMES]), SUBLANES, D_MODEL)
    d_s, m_s, v_s = _adamw(cat(Wv), cat(G), cat(Mv), cat(Vv))
    o = 0
    for n in SMALL_NAMES:
        for dst, src in ((delta, d_s), (new_m, m_s), (new_v, v_s)):
            dst[n] = src.reshape(-1)[o:o + W[n].size].reshape(Wv[n].shape)
        o += W[n].size
    for n in W_NAMES:
        if n not in delta:
            delta[n], new_m[n], new_v[n] = _adamw_any(Wv[n], G[n], Mv[n], Vv[n])

    outs = [[_view(n, d[n]) for n in W_NAMES] for d in (G, delta, new_m, new_v)]
    return (loss, grad_x, *outs[0], *outs[1], *outs[2], *outs[3])
```

```python
import math

import jax
import jax.numpy as jnp
from jax import lax
from jax.experimental import pallas as pl
from jax.experimental.pallas import tpu as pltpu

F32 = jnp.float32
BF16 = jnp.bfloat16

N_DEV = 8
DEPTH = 4
SEQ = 2048
D_MODEL = 1024
D_FF = 2816
CONV_W = 512
SSM_W = 512
SSM_GROUPS = 32
SSM_GROUP = 16
SSM_STATE = 64
N_STATE = SSM_GROUPS * SSM_STATE
IN_COLS = 2048
PLE_DIM = 256
EPS = 1e-6

ADAM_LR = 0.001
ADAM_B1 = 0.9
ADAM_B2 = 0.999
ADAM_EPS = 1e-08
ADAM_WD = 0.01
ADAM_STEP = 10

FF_BLOCK = 256
N_FF_BLOCKS = D_FF // FF_BLOCK
TOK_TILE_FFN_FWD = 2048
TOK_TILE_FFN_BWD = 1024
TOK_TILE = 512
CHUNK = 256
N_CHUNKS = SEQ // CHUNK
LANE_GROUP = 512
SUBLANES = 8
LANES = 128
MIB = 1024 * 1024

W_NAMES = ['ffn1_norm', 'ffn1_w_gate', 'ffn1_w_up', 'ffn1_w_down', 'mix_norm', 'w_in', 'conv_w', 'conv_b',
           'ssm_A_re', 'ssm_A_im', 'ssm_B_re', 'ssm_B_im', 'ssm_C_re', 'ssm_C_im', 'ssm_D', 'ssm_log_dt',
           'glu_w', 'glu_b', 'conv_out_norm', 'ssm_out_norm', 'w_out', 'ffn2_norm', 'ffn2_w_gate', 'ffn2_w_up',
           'ffn2_w_down', 'ple_norm', 'ple_w_gate', 'ple_w_proj', 'final_norm']
SMALL_NAMES = ['ffn1_norm', 'mix_norm', 'conv_b', 'ssm_A_re', 'ssm_A_im', 'ssm_B_re', 'ssm_B_im', 'ssm_C_re',
               'ssm_C_im', 'ssm_D', 'ssm_log_dt', 'glu_b', 'conv_out_norm', 'ssm_out_norm', 'ffn2_norm',
               'ple_norm', 'final_norm']

SEGS = ((3, 352), (3, 352), (1, 256), (1, 128), (1, 128), (1, 32), (1, 32))
PACK_ROWS = sum(n * r for n, r in SEGS)

MESH = pl.DeviceIdType.MESH
UNREAD = pl.BlockSpec(memory_space=pltpu.HBM)


def _in_hbm(*arrays):
    return [pltpu.with_memory_space_constraint(a, pltpu.HBM) for a in arrays]


def _out_hbm(outs, which):
    if not isinstance(outs, (list, tuple)):
        return pltpu.with_memory_space_constraint(outs, pltpu.HBM) if which else outs
    return [pltpu.with_memory_space_constraint(a, pltpu.HBM) if i in which else a for i, a in enumerate(outs)]


def _cparams(sem=None, vmem_mib=48, **kw):
    return pltpu.CompilerParams(dimension_semantics=sem, vmem_limit_bytes=vmem_mib * MIB, **kw)


def _dot(a, b):
    return jnp.dot(a, b, preferred_element_type=F32)


def _dot_nt(a, b):
    return lax.dot_general(a, b, (((1,), (1,)), ((), ())), preferred_element_type=F32)


def _dot_tn(a, b):
    return lax.dot_general(a, b, (((0,), (0,)), ((), ())), preferred_element_type=F32)


def _rms_stats(x):
    r = lax.rsqrt(jnp.mean(x * x, axis=-1, keepdims=True) + EPS)
    return x * r, r


def _rms_bwd(dy, xh, r, g):
    dxh = dy * g
    dx = r * (dxh - xh * jnp.mean(dxh * xh, axis=-1, keepdims=True))
    dg = jnp.sum(dy * xh, axis=0, keepdims=True)
    return dx, dg


def _sigmoid(x):
    return 0.5 * jnp.tanh(0.5 * x) + 0.5


_GELU_C = math.sqrt(2.0 / math.pi)


def _gelu(x):
    t = jnp.tanh(_GELU_C * (x + 0.044715 * x * x * x))
    return 0.5 * x * (1.0 + t), t


def _gelu_grad(x, t):
    return 0.5 * (1.0 + t) + 0.5 * x * (1.0 - t * t) * _GELU_C * (1.0 + 3.0 * 0.044715 * x * x)


def _accumulate(ref, first, value):
    @pl.when(first)
    def _():
        ref[...] = value

    @pl.when(jnp.logical_not(first))
    def _():
        ref[...] += value


def _ffn_fwd(h, g, w3):
    tm = TOK_TILE_FFN_FWD
    last = N_FF_BLOCKS - 1

    def body(h_ref, g_ref, wgu_ref, wd_ref, wd_last_ref, out_ref, gu_ref, u_ref, a_ref):
        k = pl.program_id(1)

        @pl.when(k == 0)
        def _():
            x = h_ref[...]
            xh, _ = _rms_stats(x)
            u_ref[...] = (xh * g_ref[...]).astype(BF16)
            out_ref[...] = x
            a_ref[1] = jnp.zeros((tm, FF_BLOCK), BF16)

        out_ref[...] += 0.5 * _dot(a_ref[(k + 1) % 2], wd_ref[0])
        gu = _dot_nt(u_ref[...], wgu_ref[...].reshape(2 * FF_BLOCK, D_MODEL))
        gate, up = gu[:, :FF_BLOCK], gu[:, FF_BLOCK:]
        a_ref[k % 2] = (gate * _sigmoid(gate) * up).astype(BF16)
        gu_ref[0] = gate.astype(BF16)
        gu_ref[1] = up.astype(BF16)

        @pl.when(k == last)
        def _():
            out_ref[...] += 0.5 * _dot(a_ref[last % 2], wd_last_ref[0])

    return _out_hbm(pl.pallas_call(
        body, name="ffn_fwd",
        grid=(SEQ // tm, N_FF_BLOCKS),
        in_specs=[pl.BlockSpec((tm, D_MODEL), lambda m, k: (m, 0), pipeline_mode=pl.Buffered(1)),
                  pl.BlockSpec((1, D_MODEL), lambda m, k: (0, 0)),
                  pl.BlockSpec((2, FF_BLOCK, D_MODEL), lambda m, k: (0, k, 0)),
                  pl.BlockSpec((1, FF_BLOCK, D_MODEL), lambda m, k: (2, jnp.maximum(k - 1, 0), 0)),
                  pl.BlockSpec((1, FF_BLOCK, D_MODEL), lambda m, k: (2, last, 0), pipeline_mode=pl.Buffered(1))],
        out_specs=[pl.BlockSpec((tm, D_MODEL), lambda m, k: (m, 0)),
                   pl.BlockSpec((2, tm, FF_BLOCK), lambda m, k: (0, m, k))],
        out_shape=[jax.ShapeDtypeStruct((SEQ, D_MODEL), F32),
                   pltpu.HBM((2, SEQ, D_FF), BF16)],
        scratch_shapes=[pltpu.VMEM((tm, D_MODEL), BF16), pltpu.VMEM((2, tm, FF_BLOCK), BF16)],
        compiler_params=_cparams(("parallel", "arbitrary"), 56),
    )(*_in_hbm(h, g, w3, w3, w3)), (1,))


def _ffn_bwd_act(h, g, dout, gu, w3):
    tm = TOK_TILE_FFN_BWD
    last = N_FF_BLOCKS - 1

    def body(h_ref, g_ref, d_ref, gu_ref, wd_ref, wgu_ref, wgu_last_ref, dh_ref, dga_ref, ud_ref, dg_ref,
             acc_ref, dgu_ref):
        m = pl.program_id(0)
        k = pl.program_id(1)

        @pl.when(k == 0)
        def _():
            xh, _ = _rms_stats(h_ref[...])
            ud_ref[0] = (xh * g_ref[...]).astype(BF16)
            ud_ref[1] = (0.5 * d_ref[...]).astype(BF16)
            acc_ref[...] = jnp.zeros_like(acc_ref)
            dgu_ref[1] = jnp.zeros((tm, 2 * FF_BLOCK), BF16)

        acc_ref[...] += _dot(dgu_ref[(k + 1) % 2], wgu_ref[...].reshape(2 * FF_BLOCK, D_MODEL))
        gate = gu_ref[0].astype(F32)
        up = gu_ref[1].astype(F32)
        sg = _sigmoid(gate)
        silu = gate * sg
        da = _dot_nt(ud_ref[1], wd_ref[0])
        dgate = (da * up * (sg + silu * (1.0 - sg))).astype(BF16)
        dup = (da * silu).astype(BF16)
        dga_ref[0] = dgate
        dga_ref[1] = dup
        dga_ref[2] = (silu * up).astype(BF16)
        dgu_ref[k % 2, :, 0:FF_BLOCK] = dgate
        dgu_ref[k % 2, :, FF_BLOCK:2 * FF_BLOCK] = dup

        @pl.when(k == last)
        def _():
            du = acc_ref[...] + _dot(dgu_ref[last % 2], wgu_last_ref[...].reshape(2 * FF_BLOCK, D_MODEL))
            xh, r = _rms_stats(h_ref[...])
            dx, dg = _rms_bwd(du, xh, r, g_ref[...])
            dh_ref[...] = d_ref[...] + dx
            _accumulate(dg_ref, m == 0, dg)

    return _out_hbm(pl.pallas_call(
        body, name="ffn_bwd_act",
        grid=(SEQ // tm, N_FF_BLOCKS),
        in_specs=[pl.BlockSpec((tm, D_MODEL), lambda m, k: (m, 0), pipeline_mode=pl.Buffered(1)),
                  pl.BlockSpec((1, D_MODEL), lambda m, k: (0, 0)),
                  pl.BlockSpec((tm, D_MODEL), lambda m, k: (m, 0), pipeline_mode=pl.Buffered(1)),
                  pl.BlockSpec((2, tm, FF_BLOCK), lambda m, k: (0, m, k)),
                  pl.BlockSpec((1, FF_BLOCK, D_MODEL), lambda m, k: (2, k, 0)),
                  pl.BlockSpec((2, FF_BLOCK, D_MODEL), lambda m, k: (0, jnp.maximum(k - 1, 0), 0)),
                  pl.BlockSpec((2, FF_BLOCK, D_MODEL), lambda m, k: (0, last, 0), pipeline_mode=pl.Buffered(1))],
        out_specs=[pl.BlockSpec((tm, D_MODEL), lambda m, k: (m, 0)),
                   pl.BlockSpec((3, tm, FF_BLOCK), lambda m, k: (0, m, k)),
                   pl.BlockSpec((2, tm, D_MODEL), lambda m, k: (0, m, 0)),
                   pl.BlockSpec((1, D_MODEL), lambda m, k: (0, 0))],
        out_shape=[jax.ShapeDtypeStruct((SEQ, D_MODEL), F32),
                   pltpu.HBM((3, SEQ, D_FF), BF16),
                   pltpu.HBM((2, SEQ, D_MODEL), BF16),
                   jax.ShapeDtypeStruct((1, D_MODEL), F32)],
        scratch_shapes=[pltpu.VMEM((tm, D_MODEL), F32), pltpu.VMEM((2, tm, 2 * FF_BLOCK), BF16)],
        compiler_params=_cparams(("arbitrary", "arbitrary"), 56),
    )(*_in_hbm(h, g, dout, gu, w3, w3, w3)), (1, 2))


def _matmul_tn(a, b, bm, out_dtype, name, bn=None, to_kernel=True):
    na, t, m = a.shape
    nb, _, n = b.shape
    bn = n if bn is None else bn

    def body(a_ref, b_ref, o_ref):
        o_ref[0] = _dot_tn(a_ref[0], b_ref[0]).astype(out_dtype)

    return _out_hbm(pl.pallas_call(
        body, name=name,
        grid=(na, m // bm, n // bn),
        in_specs=[pl.BlockSpec((1, t, bm), lambda i, k, j: (i, 0, k)),
                  pl.BlockSpec((1, t, bn), lambda i, k, j: (jnp.maximum(i - (na - nb), 0), 0, j))],
        out_specs=pl.BlockSpec((1, bm, bn), lambda i, k, j: (i, k, j)),
        out_shape=pltpu.HBM((na, m, n), out_dtype) if to_kernel else jax.ShapeDtypeStruct((na, m, n), out_dtype),
        compiler_params=_cparams(("arbitrary", "parallel", "parallel")),
    )(*_in_hbm(a, b)), to_kernel)


def _inproj_fwd(h, g, wint):
    tm = TOK_TILE

    def body(h_ref, g_ref, w_ref, z_ref):
        xh, _ = _rms_stats(h_ref[...])
        z_ref[...] = _dot_nt((xh * g_ref[...]).astype(BF16), w_ref[...])

    return pl.pallas_call(
        body, name="inproj_fwd",
        grid=(SEQ // tm,),
        in_specs=[pl.BlockSpec((tm, D_MODEL), lambda m: (m, 0)),
                  pl.BlockSpec((1, D_MODEL), lambda m: (0, 0)),
                  pl.BlockSpec((None, IN_COLS, D_MODEL), lambda m: (0, 0, 0))],
        out_specs=pl.BlockSpec((tm, IN_COLS), lambda m: (m, 0)),
        out_shape=jax.ShapeDtypeStruct((SEQ, IN_COLS), F32),
        compiler_params=_cparams(("parallel",)),
    )(*_in_hbm(h, g, wint))


def _inproj_bwd(h, g, dh, dz, wint):
    tm = TOK_TILE

    def body(h_ref, g_ref, dh_ref, dz_ref, w_ref, o_ref, u_ref, dg_ref):
        xh, r = _rms_stats(h_ref[...])
        u_ref[0] = (xh * g_ref[...]).astype(BF16)
        dx, dg = _rms_bwd(_dot(dz_ref[...], w_ref[...]), xh, r, g_ref[...])
        o_ref[...] = dh_ref[...] + dx
        _accumulate(dg_ref, pl.program_id(0) == 0, dg)

    return _out_hbm(pl.pallas_call(
        body, name="inproj_bwd",
        grid=(SEQ // tm,),
        in_specs=[pl.BlockSpec((tm, D_MODEL), lambda m: (m, 0)),
                  pl.BlockSpec((1, D_MODEL), lambda m: (0, 0)),
                  pl.BlockSpec((tm, D_MODEL), lambda m: (m, 0)),
                  pl.BlockSpec((tm, IN_COLS), lambda m: (m, 0)),
                  pl.BlockSpec((None, IN_COLS, D_MODEL), lambda m: (0, 0, 0))],
        out_specs=[pl.BlockSpec((tm, D_MODEL), lambda m: (m, 0)),
                   pl.BlockSpec((1, tm, D_MODEL), lambda m: (0, m, 0)),
                   pl.BlockSpec((1, D_MODEL), lambda m: (0, 0))],
        out_shape=[jax.ShapeDtypeStruct((SEQ, D_MODEL), F32),
                   pltpu.HBM((1, SEQ, D_MODEL), BF16),
                   jax.ShapeDtypeStruct((1, D_MODEL), F32)],
        compiler_params=_cparams(("arbitrary",)),
    )(*_in_hbm(h, g, dh, dz, wint)), (1,))


def _row_ids(n, w):
    return lax.broadcasted_iota(jnp.int32, (n, w), 0)


def _bcast_row(x, i, n):
    return jnp.broadcast_to(x[i:i + 1, :], (n, x.shape[1]))


def _conv_taps(v, tail):
    n, w = v.shape
    rid = _row_ids(n, w)
    v1 = jnp.where(rid == 0, _bcast_row(tail, 7, n), pltpu.roll(v, 1, 0))
    v2 = jnp.where(rid == 0, _bcast_row(tail, 6, n),
                   jnp.where(rid == 1, _bcast_row(tail, 7, n), pltpu.roll(v, 2, 0)))
    return v1, v2


def _scan_chunk(work, ltab, carry, reverse):
    nblk = CHUNK // SUBLANES
    for gi in range(N_STATE // LANE_GROUP):
        cre = pl.ds(gi * LANE_GROUP, LANE_GROUP)
        cim = pl.ds(N_STATE + gi * LANE_GROUP, LANE_GROUP)
        pows = [(ltab[8 * k:8 * k + 8, cre], ltab[8 * k:8 * k + 8, cim]) for k in range(3)]
        pr = ltab[24:32, cre]
        pi = ltab[24:32, cim]

        def blk(i, c, cre=cre, cim=cim, pows=pows, pr=pr, pi=pi):
            cr, ci = c
            b = (nblk - 1 - i) if reverse else i
            r0 = pl.multiple_of(b * SUBLANES, SUBLANES)
            xr = work[pl.ds(r0, SUBLANES), cre]
            xi = work[pl.ds(r0, SUBLANES), cim]
            for k, s in enumerate((1, 2, 4)):
                lr, li = pows[k]
                shift = SUBLANES - s if reverse else s
                sr = pltpu.roll(xr, shift, 0)
                si = pltpu.roll(xi, shift, 0)
                xr, xi = xr + lr * sr - li * si, xi + lr * si + li * sr
            xr, xi = xr + pr * cr - pi * ci, xi + pr * ci + pi * cr
            work[pl.ds(r0, SUBLANES), cre] = xr
            work[pl.ds(r0, SUBLANES), cim] = xi
            edge = 0 if reverse else SUBLANES - 1
            return _bcast_row(xr, edge, SUBLANES), _bcast_row(xi, edge, SUBLANES)

        cr, ci = lax.fori_loop(0, nblk, blk, (carry[:, cre], carry[:, cim]))
        carry[:, cre] = cr
        carry[:, cim] = ci


def _s5conv_fwd(z, convw, convb, bbmat, ccmat, dvec, ltab):
    def body(z_ref, cw_ref, cb_ref, bb_ref, cc_ref, d_ref, lt_ref, ya_ref, ys_ref, hs_ref,
             work, carry, tail):
        c = pl.program_id(0)

        @pl.when(c == 0)
        def _():
            carry[...] = jnp.zeros_like(carry)
            tail[...] = jnp.zeros_like(tail)

        zb = z_ref[:, 0:CONV_W]
        v = z_ref[:, CONV_W:2 * CONV_W] * z_ref[:, 2 * CONV_W:3 * CONV_W]
        us = z_ref[:, 3 * CONV_W:4 * CONV_W]
        v1, v2 = _conv_taps(v, tail[...])
        tail[...] = v[CHUNK - 8:CHUNK, :]
        y = cw_ref[0:1, :] * v2 + cw_ref[1:2, :] * v1 + cw_ref[2:3, :] * v
        ya_ref[...] = zb * (y + cb_ref[...])

        work[...] = _dot(us.astype(BF16), bb_ref[...])
        _scan_chunk(work, lt_ref, carry, reverse=False)
        hs = work[...].astype(BF16)
        hs_ref[...] = hs
        ys_ref[...] = _dot_nt(hs, cc_ref[...]) + d_ref[...] * us

    return _out_hbm(pl.pallas_call(
        body, name="s5conv_fwd",
        grid=(N_CHUNKS,),
        in_specs=[pl.BlockSpec((CHUNK, IN_COLS), lambda c: (c, 0)),
                  pl.BlockSpec((3, CONV_W), lambda c: (0, 0)),
                  pl.BlockSpec((1, CONV_W), lambda c: (0, 0)),
                  pl.BlockSpec((SSM_W, 2 * N_STATE), lambda c: (0, 0)),
                  pl.BlockSpec((SSM_W, 2 * N_STATE), lambda c: (0, 0)),
                  pl.BlockSpec((1, SSM_W), lambda c: (0, 0)),
                  pl.BlockSpec((32, 2 * N_STATE), lambda c: (0, 0))],
        out_specs=[pl.BlockSpec((CHUNK, CONV_W), lambda c: (c, 0)),
                   pl.BlockSpec((CHUNK, SSM_W), lambda c: (c, 0)),
                   pl.BlockSpec((CHUNK, 2 * N_STATE), lambda c: (c, 0))],
        out_shape=[pltpu.HBM((SEQ, CONV_W), F32),
                   pltpu.HBM((SEQ, SSM_W), F32),
                   jax.ShapeDtypeStruct((SEQ, 2 * N_STATE), BF16)],
        scratch_shapes=[pltpu.VMEM((CHUNK, 2 * N_STATE), F32),
                        pltpu.VMEM((8, 2 * N_STATE), F32),
                        pltpu.VMEM((8, CONV_W), F32)],
        compiler_params=_cparams(("arbitrary",)),
    )(*_in_hbm(z, convw, convb, bbmat, ccmat, dvec, ltab)), (0, 1))


def _s5conv_bwd(z, hs, dya, dys, convw, convb, bbmat, ccmat, dvec, ltab_rev):
    nc = N_CHUNKS
    hb = 16

    def body(z_ref, zp_ref, hs_ref, hp_ref, dya_ref, dys_ref, cw_ref, cb_ref, bb_ref, cc_ref, d_ref, lt_ref,
             dz_ref, g_ref, us_ref, dyb_ref, dl_ref, dcw_ref, work, carry, head):
        i = pl.program_id(0)
        first_chunk = i == nc - 1

        @pl.when(i == 0)
        def _():
            carry[...] = jnp.zeros_like(carry)
            head[...] = jnp.zeros_like(head)
            dl_ref[...] = jnp.zeros_like(dl_ref)
            dcw_ref[...] = jnp.zeros_like(dcw_ref)

        us = z_ref[:, 3 * CONV_W:4 * CONV_W]
        dy = dys_ref[...]
        dy_bf = dy.astype(BF16)
        us_ref[0] = us.astype(BF16)
        dyb_ref[0] = dy_bf

        work[...] = _dot(dy_bf, cc_ref[...])
        _scan_chunk(work, lt_ref, carry, reverse=True)
        gg = work[...]
        gg_bf = gg.astype(BF16)
        g_ref[0] = gg_bf
        dus = d_ref[...] * dy + _dot_nt(gg_bf, bb_ref[...])

        hcur = hs_ref[...].astype(F32)
        hlast = hp_ref[...].astype(F32)[hb - 1:hb, :]
        hlast = jnp.where(first_chunk, 0.0, hlast)
        rid = _row_ids(CHUNK, 2 * N_STATE)
        hprev = jnp.where(rid == 0, jnp.broadcast_to(hlast, (CHUNK, 2 * N_STATE)), pltpu.roll(hcur, 1, 0))
        gr, gi = gg[:, :N_STATE], gg[:, N_STATE:]
        hr, hi = hprev[:, :N_STATE], hprev[:, N_STATE:]
        dl_ref[:, :N_STATE] += (gr * hr + gi * hi).reshape(CHUNK // 8, 8, N_STATE).sum(axis=0)
        dl_ref[:, N_STATE:] += (gi * hr - gr * hi).reshape(CHUNK // 8, 8, N_STATE).sum(axis=0)

        @pl.when(i == nc - 1)
        def _():
            dl_ref[0:1, :] = jnp.sum(dl_ref[...], axis=0, keepdims=True)

        zb = z_ref[:, 0:CONV_W]
        zc = z_ref[:, CONV_W:2 * CONV_W]
        zv = z_ref[:, 2 * CONV_W:3 * CONV_W]
        v = zc * zv
        vtail = jnp.where(first_chunk, 0.0, zp_ref[:, CONV_W:2 * CONV_W] * zp_ref[:, 2 * CONV_W:3 * CONV_W])
        v1, v2 = _conv_taps(v, vtail)
        w0, w1, w2 = cw_ref[0:1, :], cw_ref[1:2, :], cw_ref[2:3, :]
        y = w0 * v2 + w1 * v1 + w2 * v
        dya_v = dya_ref[...]
        dzb = dya_v * (y + cb_ref[...])
        dyc = dya_v * zb
        hd = head[...]
        rc = _row_ids(CHUNK, CONV_W)
        n1 = jnp.where(rc == CHUNK - 1, _bcast_row(hd, 0, CHUNK), pltpu.roll(dyc, CHUNK - 1, 0))
        n2 = jnp.where(rc == CHUNK - 1, _bcast_row(hd, 1, CHUNK),
                       jnp.where(rc == CHUNK - 2, _bcast_row(hd, 0, CHUNK), pltpu.roll(dyc, CHUNK - 2, 0)))
        head[...] = dyc[0:8, :]
        dv = w2 * dyc + w1 * n1 + w0 * n2
        dz_ref[:, 0:CONV_W] = dzb.astype(BF16)
        dz_ref[:, CONV_W:2 * CONV_W] = (dv * zv).astype(BF16)
        dz_ref[:, 2 * CONV_W:3 * CONV_W] = (dv * zc).astype(BF16)
        dz_ref[:, 3 * CONV_W:4 * CONV_W] = dus.astype(BF16)
        dcw_ref[0:1, :] += jnp.sum(dyc * v2, axis=0, keepdims=True)
        dcw_ref[1:2, :] += jnp.sum(dyc * v1, axis=0, keepdims=True)
        dcw_ref[2:3, :] += jnp.sum(dyc * v, axis=0, keepdims=True)
        dcw_ref[3:4, :] += jnp.sum(dyc, axis=0, keepdims=True)
        dcw_ref[4:5, :] += jnp.sum(dy * us, axis=0, keepdims=True)

    rev = lambda i: nc - 1 - i
    return _out_hbm(pl.pallas_call(
        body, name="s5conv_bwd",
        grid=(nc,),
        in_specs=[pl.BlockSpec((CHUNK, IN_COLS), lambda i: (rev(i), 0)),
                  pl.BlockSpec((8, IN_COLS), lambda i: (jnp.maximum(rev(i) * (CHUNK // 8) - 1, 0), 0)),
                  pl.BlockSpec((CHUNK, 2 * N_STATE), lambda i: (rev(i), 0)),
                  pl.BlockSpec((hb, 2 * N_STATE), lambda i: (jnp.maximum(rev(i) * (CHUNK // hb) - 1, 0), 0)),
                  pl.BlockSpec((CHUNK, CONV_W), lambda i: (rev(i), 0)),
                  pl.BlockSpec((CHUNK, SSM_W), lambda i: (rev(i), 0)),
                  pl.BlockSpec((3, CONV_W), lambda i: (0, 0)),
                  pl.BlockSpec((1, CONV_W), lambda i: (0, 0)),
                  pl.BlockSpec((SSM_W, 2 * N_STATE), lambda i: (0, 0)),
                  pl.BlockSpec((SSM_W, 2 * N_STATE), lambda i: (0, 0)),
                  pl.BlockSpec((1, SSM_W), lambda i: (0, 0)),
                  pl.BlockSpec((32, 2 * N_STATE), lambda i: (0, 0))],
        out_specs=[pl.BlockSpec((CHUNK, IN_COLS), lambda i: (rev(i), 0)),
                   pl.BlockSpec((1, CHUNK, 2 * N_STATE), lambda i: (0, rev(i), 0)),
                   pl.BlockSpec((1, CHUNK, SSM_W), lambda i: (0, rev(i), 0)),
                   pl.BlockSpec((1, CHUNK, SSM_W), lambda i: (0, rev(i), 0)),
                   pl.BlockSpec((8, 2 * N_STATE), lambda i: (0, 0)),
                   pl.BlockSpec((8, CONV_W), lambda i: (0, 0))],
        out_shape=[jax.ShapeDtypeStruct((SEQ, IN_COLS), BF16),
                   pltpu.HBM((1, SEQ, 2 * N_STATE), BF16),
                   pltpu.HBM((1, SEQ, SSM_W), BF16),
                   pltpu.HBM((1, SEQ, SSM_W), BF16),
                   jax.ShapeDtypeStruct((8, 2 * N_STATE), F32),
                   jax.ShapeDtypeStruct((8, CONV_W), F32)],
        scratch_shapes=[pltpu.VMEM((CHUNK, 2 * N_STATE), F32),
                        pltpu.VMEM((8, 2 * N_STATE), F32),
                        pltpu.VMEM((8, CONV_W), F32)],
        compiler_params=_cparams(("arbitrary",)),
    )(*_in_hbm(z, z, hs, hs, dya, dys, convw, convb, bbmat, ccmat, dvec, ltab_rev)), (1, 2, 3))


def _mix_out_fwd(h, ya, ys, gluw, glub, con, son, wout):
    tm = TOK_TILE

    def body(h_ref, ya_ref, ys_ref, gw_ref, gb_ref, con_ref, son_ref, wo_ref, o_ref):
        zg, _ = _gelu(ys_ref[...])
        q = _dot(zg.astype(BF16), gw_ref[...]) + gb_ref[...]
        out_s = zg * _sigmoid(q)
        na, _ = _rms_stats(ya_ref[...])
        ns, _ = _rms_stats(out_s)
        o_ref[...] = (h_ref[...]
                      + _dot((na * con_ref[...]).astype(BF16), wo_ref[0:CONV_W, :])
                      + _dot((ns * son_ref[...]).astype(BF16), wo_ref[CONV_W:2 * CONV_W, :]))

    row = lambda m: (m, 0)
    fixed = lambda m: (0, 0)
    return pl.pallas_call(
        body, name="mix_out_fwd",
        grid=(SEQ // tm,),
        in_specs=[pl.BlockSpec((tm, D_MODEL), row), pl.BlockSpec((tm, CONV_W), row), pl.BlockSpec((tm, SSM_W), row),
                  pl.BlockSpec((SSM_W, SSM_W), fixed), pl.BlockSpec((1, SSM_W), fixed),
                  pl.BlockSpec((1, CONV_W), fixed), pl.BlockSpec((1, SSM_W), fixed),
                  pl.BlockSpec((None, D_MODEL, D_MODEL), lambda m: (0, 0, 0))],
        out_specs=pl.BlockSpec((tm, D_MODEL), row),
        out_shape=jax.ShapeDtypeStruct((SEQ, D_MODEL), F32),
        compiler_params=_cparams(("parallel",)),
    )(*_in_hbm(h, ya, ys, gluw, glub, con, son, wout))


def _mix_out_bwd(dh, ya, ys, gluw, glub, con, son, wout):
    tm = TOK_TILE

    def body(dh_ref, ya_ref, ys_ref, gw_ref, gb_ref, con_ref, son_ref, wo_ref,
             dya_ref, dys_ref, yc_ref, dhb_ref, zg_ref, dq_ref, part_ref):
        ysv = ys_ref[...]
        zg, th = _gelu(ysv)
        zg_bf = zg.astype(BF16)
        s = _sigmoid(_dot(zg_bf, gw_ref[...]) + gb_ref[...])
        out_s = zg * s
        na, ra = _rms_stats(ya_ref[...])
        ns, rs = _rms_stats(out_s)
        dh_bf = dh_ref[...].astype(BF16)
        yc_ref[0, :, 0:CONV_W] = (na * con_ref[...]).astype(BF16)
        yc_ref[0, :, CONV_W:2 * CONV_W] = (ns * son_ref[...]).astype(BF16)
        dhb_ref[0] = dh_bf
        dca = _dot_nt(dh_bf, wo_ref[0:CONV_W, :])
        dcs = _dot_nt(dh_bf, wo_ref[CONV_W:2 * CONV_W, :])
        dya, dcon = _rms_bwd(dca, na, ra, con_ref[...])
        dos, dson = _rms_bwd(dcs, ns, rs, son_ref[...])
        dya_ref[...] = dya
        dq = dos * zg * s * (1.0 - s)
        dq_bf = dq.astype(BF16)
        dzg = dos * s + _dot_nt(dq_bf, gw_ref[...])
        dys_ref[...] = dzg * _gelu_grad(ysv, th)
        zg_ref[0] = zg_bf
        dq_ref[0] = dq_bf
        rid = _row_ids(SUBLANES, SSM_W)
        part = jnp.zeros((SUBLANES, SSM_W), F32)
        for i, rowv in enumerate((dcon, dson, jnp.sum(dq, axis=0, keepdims=True))):
            part = jnp.where(rid == i, jnp.broadcast_to(rowv, (SUBLANES, SSM_W)), part)
        _accumulate(part_ref, pl.program_id(0) == 0, part)

    row = lambda m: (m, 0)
    fixed = lambda m: (0, 0)
    lead = lambda m: (0, m, 0)
    return _out_hbm(pl.pallas_call(
        body, name="mix_out_bwd",
        grid=(SEQ // tm,),
        in_specs=[pl.BlockSpec((tm, D_MODEL), row), pl.BlockSpec((tm, CONV_W), row), pl.BlockSpec((tm, SSM_W), row),
                  pl.BlockSpec((SSM_W, SSM_W), fixed), pl.BlockSpec((1, SSM_W), fixed),
                  pl.BlockSpec((1, CONV_W), fixed), pl.BlockSpec((1, SSM_W), fixed),
                  pl.BlockSpec((None, D_MODEL, D_MODEL), lambda m: (0, 0, 0))],
        out_specs=[pl.BlockSpec((tm, CONV_W), row), pl.BlockSpec((tm, SSM_W), row),
                   pl.BlockSpec((1, tm, D_MODEL), lead), pl.BlockSpec((1, tm, D_MODEL), lead),
                   pl.BlockSpec((1, tm, SSM_W), lead), pl.BlockSpec((1, tm, SSM_W), lead),
                   pl.BlockSpec((8, SSM_W), fixed)],
        out_shape=[pltpu.HBM((SEQ, CONV_W), F32), pltpu.HBM((SEQ, SSM_W), F32),
                   pltpu.HBM((1, SEQ, D_MODEL), BF16), pltpu.HBM((1, SEQ, D_MODEL), BF16),
                   pltpu.HBM((1, SEQ, SSM_W), BF16), pltpu.HBM((1, SEQ, SSM_W), BF16),
                   jax.ShapeDtypeStruct((8, SSM_W), F32)],
        compiler_params=_cparams(("arbitrary",)),
    )(*_in_hbm(dh, ya, ys, gluw, glub, con, son, wout)), (0, 1, 2, 3, 4, 5))


def _ple_fwd(h, g, p, wgate, wprojt):
    tm = TOK_TILE

    def body(h_ref, g_ref, p_ref, wg_ref, wp_ref, o_ref):
        x = h_ref[...]
        xh, _ = _rms_stats(x)
        s = _sigmoid(_dot((xh * g_ref[...]).astype(BF16), wg_ref[...]))
        o_ref[...] = x + _dot_nt(p_ref[...].astype(BF16), wp_ref[...]) * s

    row = lambda m: (m, 0)
    fixed = lambda m: (0, 0)
    return pl.pallas_call(
        body, name="ple_fwd",
        grid=(SEQ // tm,),
        in_specs=[pl.BlockSpec((tm, D_MODEL), row), pl.BlockSpec((1, D_MODEL), fixed), pl.BlockSpec((tm, PLE_DIM), row),
                  pl.BlockSpec((None, D_MODEL, D_MODEL), lambda m: (0, 0, 0)), pl.BlockSpec((D_MODEL, PLE_DIM), fixed)],
        out_specs=pl.BlockSpec((tm, D_MODEL), row),
        out_shape=jax.ShapeDtypeStruct((SEQ, D_MODEL), F32),
        compiler_params=_cparams(("parallel",)),
    )(*_in_hbm(h, g, p, wgate, wprojt))


def _ple_bwd(h, g, p, dh, wgate, wprojt):
    tm = TOK_TILE

    def body(h_ref, g_ref, p_ref, dh_ref, wg_ref, wp_ref, o_ref, u_ref, dq_ref, dpp_ref, pb_ref, dg_ref):
        xh, r = _rms_stats(h_ref[...])
        u = (xh * g_ref[...]).astype(BF16)
        s = _sigmoid(_dot(u, wg_ref[...]))
        p_bf = p_ref[...].astype(BF16)
        pp = _dot_nt(p_bf, wp_ref[...])
        dhv = dh_ref[...]
        dq = (dhv * pp * s * (1.0 - s)).astype(BF16)
        u_ref[0] = u
        dq_ref[0] = dq
        dpp_ref[0] = (dhv * s).astype(BF16)
        pb_ref[0] = p_bf
        dx, dg = _rms_bwd(_dot_nt(dq, wg_ref[...]), xh, r, g_ref[...])
        o_ref[...] = dhv + dx
        _accumulate(dg_ref, pl.program_id(0) == 0, dg)

    row = lambda m: (m, 0)
    fixed = lambda m: (0, 0)
    lead = lambda m: (0, m, 0)
    big = pltpu.HBM((1, SEQ, D_MODEL), BF16)
    return _out_hbm(pl.pallas_call(
        body, name="ple_bwd",
        grid=(SEQ // tm,),
        in_specs=[pl.BlockSpec((tm, D_MODEL), row), pl.BlockSpec((1, D_MODEL), fixed), pl.BlockSpec((tm, PLE_DIM), row),
                  pl.BlockSpec((tm, D_MODEL), row),
                  pl.BlockSpec((None, D_MODEL, D_MODEL), lambda m: (0, 0, 0)), pl.BlockSpec((D_MODEL, PLE_DIM), fixed)],
        out_specs=[pl.BlockSpec((tm, D_MODEL), row),
                   pl.BlockSpec((1, tm, D_MODEL), lead), pl.BlockSpec((1, tm, D_MODEL), lead),
                   pl.BlockSpec((1, tm, D_MODEL), lead), pl.BlockSpec((1, tm, PLE_DIM), lead),
                   pl.BlockSpec((1, D_MODEL), fixed)],
        out_shape=[jax.ShapeDtypeStruct((SEQ, D_MODEL), F32), big, big, big,
                   pltpu.HBM((1, SEQ, PLE_DIM), BF16),
                   jax.ShapeDtypeStruct((1, D_MODEL), F32)],
        compiler_params=_cparams(("arbitrary",)),
    )(*_in_hbm(h, g, p, dh, wgate, wprojt)), (1, 2, 3, 4))


def _final_loss(h, g, target):
    tm = TOK_TILE

    def body(h_ref, g_ref, t_ref, loss_ref, dh_ref, dg_ref):
        first = pl.program_id(0) == 0
        xh, r = _rms_stats(h_ref[...])
        diff = xh * g_ref[...] - t_ref[...]
        part = 0.5 * jnp.sum(jnp.mean(diff * diff, axis=-1, keepdims=True), axis=0, keepdims=True)
        _accumulate(loss_ref, first, jnp.broadcast_to(part, (SUBLANES, LANES)))
        dx, dg = _rms_bwd(diff * (1.0 / D_MODEL), xh, r, g_ref[...])
        dh_ref[...] = dx
        _accumulate(dg_ref, first, dg)

    row = lambda m: (m, 0)
    fixed = lambda m: (0, 0)
    return pl.pallas_call(
        body, name="final_loss",
        grid=(SEQ // tm,),
        in_specs=[pl.BlockSpec((tm, D_MODEL), row), pl.BlockSpec((1, D_MODEL), fixed),
                  pl.BlockSpec((tm, D_MODEL), row)],
        out_specs=[pl.BlockSpec((SUBLANES, LANES), fixed),
                   pl.BlockSpec((tm, D_MODEL), row),
                   pl.BlockSpec((1, D_MODEL), fixed)],
        out_shape=[jax.ShapeDtypeStruct((SUBLANES, LANES), F32),
                   jax.ShapeDtypeStruct((SEQ, D_MODEL), F32),
                   jax.ShapeDtypeStruct((1, D_MODEL), F32)],
        compiler_params=_cparams(("arbitrary",)),
    )(*_in_hbm(h, g, target))


def _disc(ar, ai, ldt):
    dt = jnp.exp(ldt)
    mag = jnp.exp(ar * dt)
    ph = ai * dt
    lr, li = mag * jnp.cos(ph), mag * jnp.sin(ph)
    nr, ni = lr - 1.0, li
    den = ar * ar + ai * ai
    return lr, li, (nr * ar + ni * ai) / den, (ni * ar - nr * ai) / den


def _s5_disc(a_row, ldt_row, a_rep, ldt_rep, bt, ct, tile_e, mask):
    n = N_STATE

    def body(ar_ref, lr_ref, ap_ref, lp_ref, b_ref, c_ref, e_ref, m_ref, lt_ref, ltr_ref, bb_ref, cc_ref):
        lr, li, _, _ = _disc(ar_ref[0], ar_ref[1], lr_ref[...])
        pr, pi = lr, li
        rid = _row_ids(SUBLANES, n)
        for k in range(1, 9):
            for ref, sgn, edge in ((lt_ref, 1.0, 24 + k - 1), (ltr_ref, -1.0, 24 + 8 - k)):
                if k in (1, 2, 4):
                    r0 = {1: 0, 2: 8, 4: 16}[k]
                    keep = (rid >= k) if ref is lt_ref else (rid < SUBLANES - k)
                    ref[r0:r0 + 8, 0:n] = jnp.where(keep, jnp.broadcast_to(pr, (8, n)), 0.0)
                    ref[r0:r0 + 8, n:2 * n] = jnp.where(keep, jnp.broadcast_to(sgn * pi, (8, n)), 0.0)
                ref[edge:edge + 1, 0:n] = pr
                ref[edge:edge + 1, n:2 * n] = sgn * pi
            pr, pi = pr * lr - pi * li, pr * li + pi * lr
        _, _, fr, fi = _disc(ap_ref[0], ap_ref[1], lp_ref[...])
        br, bi = b_ref[0], b_ref[1]
        e = e_ref[...]
        m = m_ref[...].astype(F32)
        bb_ref[:, 0:n] = (_dot((fr * br - fi * bi).astype(BF16), e) * m).astype(BF16)
        bb_ref[:, n:2 * n] = (_dot((fr * bi + fi * br).astype(BF16), e) * m).astype(BF16)
        cc_ref[:, 0:n] = (_dot(c_ref[0].astype(BF16), e) * m).astype(BF16)
        cc_ref[:, n:2 * n] = (-(_dot(c_ref[1].astype(BF16), e) * m)).astype(BF16)

    return pl.pallas_call(
        body, name="s5_disc",
        out_shape=[jax.ShapeDtypeStruct((32, 2 * n), F32), jax.ShapeDtypeStruct((32, 2 * n), F32),
                   jax.ShapeDtypeStruct((SSM_W, 2 * n), BF16), jax.ShapeDtypeStruct((SSM_W, 2 * n), BF16)],
        compiler_params=_cparams(None),
    )(a_row, ldt_row, a_rep, ldt_rep, bt, ct, tile_e, mask)


def _dot_exact(x, sel):
    hi = x.astype(BF16)
    r1 = x - hi.astype(F32)
    mid = r1.astype(BF16)
    lo = (r1 - mid.astype(F32)).astype(BF16)
    return _dot(hi, sel) + _dot(mid, sel) + _dot(lo, sel)


def _s5_disc_bwd(a, ldt, a_rep, ldt_rep, bt, mask, dl, d_bb, d_cc, fold):
    n = N_STATE

    def body(a_ref, l_ref, ap_ref, lp_ref, b_ref, m_ref, dl_ref, dbb_ref, dcc_ref, f_ref,
             da_ref, dldt_ref, db_ref, dc_ref):
        m = m_ref[...].astype(F32)
        fold_m = f_ref[...]
        diag = lambda x: _dot_exact(x * m, fold_m)
        dr, di = diag(dbb_ref[:, 0:n]), diag(dbb_ref[:, n:2 * n])
        dc_ref[0] = diag(dcc_ref[:, 0:n])
        dc_ref[1] = -diag(dcc_ref[:, n:2 * n])
        _, _, fr, fi = _disc(ap_ref[0], ap_ref[1], lp_ref[...])
        br, bi = b_ref[0], b_ref[1]
        db_ref[0] = fr * dr + fi * di
        db_ref[1] = fr * di - fi * dr
        per_state = lambda x: x.reshape(SSM_GROUPS, SSM_GROUP, SSM_STATE).sum(axis=1)
        dfr = per_state(dr * br + di * bi)
        dfi = per_state(di * br - dr * bi)
        _, vjp = jax.vjp(_disc, a_ref[0], a_ref[1], l_ref[...])
        dar, dai, dldt = vjp((dl_ref[0], dl_ref[1], dfr, dfi))
        da_ref[0] = dar
        da_ref[1] = dai
        dldt_ref[...] = jnp.sum(dldt, axis=1, keepdims=True)

    return pl.pallas_call(
        body, name="s5_disc_bwd",
        out_shape=[jax.ShapeDtypeStruct((2, SSM_GROUPS, SSM_STATE), F32),
                   jax.ShapeDtypeStruct((SSM_GROUPS, 1), F32),
                   jax.ShapeDtypeStruct((2, SSM_W, SSM_STATE), F32),
                   jax.ShapeDtypeStruct((2, SSM_W, SSM_STATE), F32)],
        compiler_params=_cparams(None),
    )(a, ldt, a_rep, ldt_rep, bt, mask, dl, d_bb, d_cc, fold)


def _row_block(rows, cap=512):
    for bm in range(min(cap, rows), 0, -1):
        if rows % bm == 0 and (bm % 8 == 0 or bm == rows):
            return bm
    return rows


def _pair_sum(fulls, got, segs):
    ns = len(segs)
    offs = _seg_offsets(segs)
    _, rtot, c = got.shape
    parts = 2
    pr = rtot // parts
    assert pr * parts == rtot and pr % 16 == 0
    pieces = [[] for _ in range(parts)]
    for a, (n, r) in enumerate(segs):
        for m in range(n):
            lo = offs[a] + m * r
            for h in range(parts):
                clo, chi = max(lo, h * pr), min(lo + r, (h + 1) * pr)
                if chi > clo:
                    pieces[h].append((a, m, clo - lo, clo - h * pr, chi - clo))
    n_sems = max(len(ps) for ps in pieces)

    def body(*refs):
        srcs = refs[:ns]
        got_ref, p32_ref, pbf_ref, own_v, sems = refs[ns:]
        h = pl.program_id(0)
        k = pl.program_id(1)
        dev = 2 * k + lax.axis_index("c")
        for hh in range(parts):
            @pl.when(h == hh)
            def _(hh=hh):
                cps = []
                for i, (a, m, so, do, rows) in enumerate(pieces[hh]):
                    start = pl.multiple_of(dev * segs[a][1] + so, 16)
                    cps.append(pltpu.make_async_copy(srcs[a].at[m, pl.ds(start, rows), :],
                                                     own_v.at[pl.ds(do, rows), :], sems.at[i]))
                for cp in cps:
                    cp.start()
                for cp in cps:
                    cp.wait()
        s = own_v[...].astype(F32) + got_ref[0].astype(F32)
        pbf_ref[0] = s.astype(BF16)

        @pl.when(k == 2 * lax.axis_index("x") + lax.axis_index("y"))
        def _():
            p32_ref[...] = s

    spec = pl.BlockSpec((1, pr, c), lambda h, k: (k, h, 0))
    return pl.pallas_call(
        body, name="pair_sum",
        grid=(parts, 4),
        in_specs=[HBM] * ns + [spec], out_specs=[pl.BlockSpec((pr, c), lambda h, k: (h, 0)), spec],
        out_shape=[pltpu.HBM((rtot, c), F32), pltpu.HBM(got.shape, BF16)],
        scratch_shapes=[pltpu.VMEM((pr, c), BF16), pltpu.SemaphoreType.DMA((n_sems,))],
        compiler_params=_cparams(("arbitrary", "arbitrary")),
    )(*_in_hbm(*fulls, got))


def _chip_sum(own, rb):
    r, c = own.shape
    bm = _row_block(r)

    def body(o_ref, r_ref, s_ref):
        s_ref[...] = ((o_ref[...] + r_ref[0].astype(F32)) + r_ref[1].astype(F32)) + r_ref[2].astype(F32)

    return pl.pallas_call(
        body, name="chip_sum",
        grid=(r // bm,),
        in_specs=[pl.BlockSpec((bm, c), lambda k: (k, 0)), pl.BlockSpec((3, bm, c), lambda k: (0, k, 0))],
        out_specs=pl.BlockSpec((bm, c), lambda k: (k, 0)),
        out_shape=jax.ShapeDtypeStruct((r, c), F32),
        compiler_params=_cparams(("parallel",)),
    )(*_in_hbm(own, rb))


def _adamw(w, g, m, v):
    r, c = w.shape
    bm = _row_block(r)
    bc1 = 1.0 - ADAM_B1 ** ADAM_STEP
    bc2 = 1.0 - ADAM_B2 ** ADAM_STEP

    def body(w_ref, g_ref, m_ref, v_ref, d_ref, nm_ref, nv_ref):
        gv = g_ref[...]
        nm = ADAM_B1 * m_ref[...] + (1.0 - ADAM_B1) * gv
        nv = ADAM_B2 * v_ref[...] + (1.0 - ADAM_B2) * (gv * gv)
        nm_ref[...] = nm
        nv_ref[...] = nv
        d_ref[...] = -ADAM_LR * ((nm / bc1) / (jnp.sqrt(nv / bc2) + ADAM_EPS) + ADAM_WD * w_ref[...])

    spec = pl.BlockSpec((bm, c), lambda k: (k, 0))
    shp = jax.ShapeDtypeStruct((r, c), F32)
    return pl.pallas_call(
        body, name="adamw",
        grid=(r // bm,),
        in_specs=[spec] * 4, out_specs=[spec] * 3, out_shape=[shp] * 3,
        compiler_params=_cparams(("parallel",)),
    )(*_in_hbm(w, g, m, v))


def _mesh_pos():
    return lax.axis_index("x"), lax.axis_index("y"), lax.axis_index("c")


def _dev_index(p):
    return 4 * p[0] + 2 * p[1] + p[2]


def _seg_offsets(segs):
    offs, o = [], 0
    for n, r in segs:
        offs.append(o)
        o += n * r
    return offs


def _remote(src, dst, send_sem, recv_sem, to):
    return pltpu.make_async_remote_copy(src_ref=src, dst_ref=dst, send_sem=send_sem, recv_sem=recv_sem,
                                        device_id=to, device_id_type=MESH)


def _allgather(pack, segs, name):
    rtot, c = pack.shape
    ns = len(segs)
    offs = _seg_offsets(segs)
    assert rtot == sum(n * r for n, r in segs)

    def body(pack_ref, *refs):
        outs = refs[:ns]
        send_sems, recv_sems, local_sem = refs[ns:]
        x, y, cc = _mesh_pos()
        me, sib = (x, y, cc), (x, y, 1 - cc)
        chips = [(1 - x, y), (x, 1 - y), (1 - x, 1 - y)]

        def pieces(dev, from_pack):
            res = []
            for a, (n, r) in enumerate(segs):
                for m in range(n):
                    dst = outs[a].at[m, pl.ds(pl.multiple_of(dev * r, r), r), :]
                    src = pack_ref.at[pl.ds(offs[a] + m * r, r), :] if from_pack else dst
                    res.append((src, dst))
            return res

        def push(k, dev, to, from_pack):
            for s, d in pieces(dev, from_pack):
                _remote(s, d, send_sems.at[k], recv_sems.at[k], to).start()

        def whole(k):
            return _remote(pack_ref, pack_ref, send_sems.at[k], recv_sems.at[k], me)

        my_dev = _dev_index(me)
        for s, d in pieces(my_dev, True):
            pltpu.make_async_copy(s, d, local_sem).start()
        push(0, my_dev, sib, True)
        for j, chip in enumerate(chips):
            push(1 + j, my_dev, (*chip, cc), True)
        for j, chip in enumerate(chips):
            whole(1 + j).wait_recv()
            push(4 + j, _dev_index((*chip, cc)), sib, False)
        whole(0).wait_recv()
        for j in range(3):
            whole(4 + j).wait_recv()
        for k in range(7):
            whole(k).wait_send()
        pltpu.make_async_copy(pack_ref, pack_ref, local_sem).wait()

    return pl.pallas_call(
        body, name=name,
        in_specs=[HBM], out_specs=[HBM] * ns,
        out_shape=[jax.ShapeDtypeStruct((n, N_DEV * r, c), pack.dtype) for n, r in segs],
        scratch_shapes=[pltpu.SemaphoreType.DMA((7,)), pltpu.SemaphoreType.DMA((7,)), pltpu.SemaphoreType.DMA],
    )(pack)


HBM = pl.BlockSpec(memory_space=pltpu.HBM)
SEM = pl.BlockSpec(memory_space=pltpu.SEMAPHORE)
VMEM_WHOLE = pl.BlockSpec(memory_space=pltpu.VMEM)
EFFECT = pltpu.SideEffectType.DATAFLOW_SIDE_EFFECTING


def _hbm(a):
    return pltpu.with_memory_space_constraint(a, pltpu.HBM)


def _ag_start(pack, segs, after, name):
    rtot, c = pack.shape
    ns = len(segs)
    offs = _seg_offsets(segs)

    def body(pack_ref, *refs):
        lands = refs[:ns]
        send_sems, recv_sems = refs[ns + 1], refs[ns + 2]
        token = refs[-1]
        x, y, cc = _mesh_pos()
        my_dev = _dev_index((x, y, cc))
        targets = [(x, y, 1 - cc), (1 - x, y, cc), (x, 1 - y, cc), (1 - x, 1 - y, cc)]
        for k, to in enumerate(targets):
            for a, (n, r) in enumerate(segs):
                for m in range(n):
                    _remote(pack_ref.at[pl.ds(offs[a] + m * r, r), :],
                            lands[a].at[m, pl.ds(pl.multiple_of(my_dev * r, r), r), :],
                            send_sems.at[k], recv_sems.at[k], to).start()
        token[...] = jnp.zeros_like(token)

    land_shapes = [(n, N_DEV * r, c) for n, r in segs]
    outs = pl.pallas_call(
        body, name=name,
        in_specs=[HBM] * (1 + ns) + [UNREAD],
        out_specs=[SEM, SEM, HBM] + [HBM] * ns + [VMEM_WHOLE],
        out_shape=[pltpu.SemaphoreType.DMA((4,)), pltpu.SemaphoreType.DMA((4,)), pltpu.HBM(pack.shape, pack.dtype)]
        + [pltpu.HBM(s, pack.dtype) for s in land_shapes] + [jax.ShapeDtypeStruct((SUBLANES, LANES), F32)],
        input_output_aliases={0: 2, **{1 + i: 3 + i for i in range(ns)}},
        compiler_params=pltpu.CompilerParams(has_side_effects=EFFECT),
    )(_hbm(pack), *[_hbm(lax.empty(s, pack.dtype)) for s in land_shapes], _hbm(after))
    return outs[0], outs[1], outs[2], list(outs[3:3 + ns]), outs[-1]


def _ag_wait(send_sems, recv_sems, pack, lands, after, name):
    ns = len(lands)

    def body(pack_ref, *refs):
        send_ref, recv_ref = refs[ns], refs[ns + 1]
        me = _mesh_pos()
        for k in range(4):
            whole = _remote(pack_ref, pack_ref, send_ref.at[k], recv_ref.at[k], me)
            whole.wait_send()
            whole.wait_recv()

    outs = pl.pallas_call(
        body, name=name,
        in_specs=[HBM] * (1 + ns) + [SEM, SEM, UNREAD],
        out_specs=[HBM] * (1 + ns),
        out_shape=[pltpu.HBM(pack.shape, pack.dtype)] + [pltpu.HBM(a.shape, a.dtype) for a in lands],
        input_output_aliases={i: i for i in range(1 + ns)},
        compiler_params=pltpu.CompilerParams(has_side_effects=EFFECT),
    )(pack, *lands, send_sems, recv_sems, _hbm(after))
    return outs[0], list(outs[1:])


def _ag_finish(pack, lands, segs):
    rtot, c = pack.shape
    ns = len(segs)
    offs = _seg_offsets(segs)

    def body(pack_ref, *refs):
        outs = refs[ns:2 * ns]
        stage, send_sems, recv_sems, local_sems = refs[2 * ns:]
        x, y, cc = _mesh_pos()
        me, sib = (x, y, cc), (x, y, 1 - cc)
        chips = [(1 - x, y), (x, 1 - y), (1 - x, 1 - y)]

        def rows(a, m, dev):
            return outs[a].at[m, pl.ds(pl.multiple_of(dev * segs[a][1], segs[a][1]), segs[a][1]), :]

        for j, chip in enumerate(chips):
            dev = _dev_index((*chip, cc))
            for a, (n, r) in enumerate(segs):
                for m in range(n):
                    _remote(rows(a, m, dev), rows(a, m, dev), send_sems.at[j], recv_sems.at[j], sib).start()
        load = pltpu.make_async_copy(pack_ref, stage, local_sems.at[0])
        load.start()
        load.wait()
        my_dev = _dev_index(me)
        for a, (n, r) in enumerate(segs):
            for m in range(n):
                pltpu.make_async_copy(stage.at[pl.ds(offs[a] + m * r, r), :], rows(a, m, my_dev), local_sems.at[1]).start()
        pltpu.make_async_copy(stage, pack_ref, local_sems.at[1]).wait()
        for j in range(3):
            _remote(pack_ref, pack_ref, send_sems.at[j], recv_sems.at[j], me).wait()

    outs = pl.pallas_call(
        body, name="ag_finish",
        in_specs=[HBM] * (1 + ns), out_specs=[HBM] * ns,
        out_shape=[pltpu.HBM(a.shape, a.dtype) if r >= 128 else jax.ShapeDtypeStruct(a.shape, a.dtype)
                   for a, (_, r) in zip(lands, segs)],
        input_output_aliases={1 + i: i for i in range(ns)},
        scratch_shapes=[pltpu.VMEM((rtot, c), pack.dtype), pltpu.SemaphoreType.DMA((3,)),
                        pltpu.SemaphoreType.DMA((3,)), pltpu.SemaphoreType.DMA((2,))],
        compiler_params=_cparams(None, 16),
    )(pack, *lands)
    return list(outs)


def _rs_chips_start(pbf, after, name):
    _, rtot, c = pbf.shape

    def body(pbf_ref, land_ref, after_ref, send_sems, recv_sems, pbf_thru, land_thru, token):
        x, y, cc = _mesh_pos()
        for j, (cx, cy) in enumerate([(1 - x, y), (x, 1 - y), (1 - x, 1 - y)]):
            _remote(pbf_ref.at[2 * cx + cy], land_ref.at[j], send_sems.at[j], recv_sems.at[j], (cx, cy, cc)).start()
        token[...] = jnp.zeros_like(token)

    return pl.pallas_call(
        body, name=name,
        in_specs=[HBM, HBM, UNREAD],
        out_specs=[SEM, SEM, HBM, HBM, VMEM_WHOLE],
        out_shape=[pltpu.SemaphoreType.DMA((3,)), pltpu.SemaphoreType.DMA((3,)), pltpu.HBM(pbf.shape, pbf.dtype),
                   pltpu.HBM((3, rtot, c), pbf.dtype), jax.ShapeDtypeStruct((SUBLANES, LANES), F32)],
        input_output_aliases={0: 2, 1: 3},
        compiler_params=pltpu.CompilerParams(has_side_effects=EFFECT),
    )(_hbm(pbf), _hbm(lax.empty((3, rtot, c), pbf.dtype)), _hbm(after))


def _rs_chips_wait(send_sems, recv_sems, pbf, land, after, name):
    def body(pbf_ref, land_ref, send_ref, recv_ref, after_ref, pbf_out, land_out):
        me = _mesh_pos()
        for j in range(3):
            cp = _remote(pbf_ref.at[0], land_ref.at[j], send_ref.at[j], recv_ref.at[j], me)
            cp.wait_send()
            cp.wait_recv()

    return pl.pallas_call(
        body, name=name,
        in_specs=[HBM, HBM, SEM, SEM, UNREAD], out_specs=[HBM, HBM],
        out_shape=[pltpu.HBM(pbf.shape, pbf.dtype), pltpu.HBM(land.shape, land.dtype)],
        input_output_aliases={0: 0, 1: 1},
        compiler_params=pltpu.CompilerParams(has_side_effects=EFFECT),
    )(pbf, land, send_sems, recv_sems, _hbm(after))[1]


def _flips():
    return [(dx, dy, dc) for dx in (0, 1) for dy in (0, 1) for dc in (0, 1) if dx or dy or dc]


def _small_gather_start(flat, name):
    r, c = flat.shape

    def body(flat_ref, land_ref, send_sems, recv_sems, flat_thru, land_thru, token):
        x, y, cc = _mesh_pos()
        mine = land_ref.at[_dev_index((x, y, cc))]
        for k, (dx, dy, dc) in enumerate(_flips()):
            to = (1 - x if dx else x, 1 - y if dy else y, 1 - cc if dc else cc)
            _remote(flat_ref, mine, send_sems.at[k], recv_sems.at[k], to).start()
        token[...] = jnp.zeros_like(token)

    return pl.pallas_call(
        body, name=name,
        in_specs=[HBM, HBM],
        out_specs=[SEM, SEM, HBM, HBM, VMEM_WHOLE],
        out_shape=[pltpu.SemaphoreType.DMA((7,)), pltpu.SemaphoreType.DMA((7,)), pltpu.HBM(flat.shape, flat.dtype),
                   pltpu.HBM((N_DEV, r, c), flat.dtype), jax.ShapeDtypeStruct((SUBLANES, LANES), F32)],
        input_output_aliases={0: 2, 1: 3},
        compiler_params=pltpu.CompilerParams(has_side_effects=EFFECT),
    )(_hbm(flat), _hbm(lax.empty((N_DEV, r, c), flat.dtype)))


def _small_gather_wait(send_sems, recv_sems, flat, land, after, name):
    def body(flat_ref, land_ref, send_ref, recv_ref, after_ref, flat_out, land_out):
        me = _mesh_pos()
        for k in range(N_DEV - 1):
            cp = _remote(flat_ref, land_ref.at[0], send_ref.at[k], recv_ref.at[k], me)
            cp.wait_send()
            cp.wait_recv()

    return pl.pallas_call(
        body, name=name,
        in_specs=[HBM, HBM, SEM, SEM, UNREAD], out_specs=[HBM, HBM],
        out_shape=[pltpu.HBM(flat.shape, flat.dtype), pltpu.HBM(land.shape, land.dtype)],
        input_output_aliases={0: 0, 1: 1},
        compiler_params=pltpu.CompilerParams(has_side_effects=EFFECT),
    )(flat, land, send_sems, recv_sems, _hbm(after))


def _sum_devices(land, own):
    _, r, c = land.shape

    def body(land_ref, own_ref, out_ref):
        me = _dev_index(_mesh_pos())
        total = None
        for d in range(N_DEV):
            other = land_ref[jnp.where(d == me, (d + 1) % N_DEV, d)]
            block = jnp.where(d == me, own_ref[...], other)
            total = block if total is None else total + block
        out_ref[...] = total

    return pl.pallas_call(
        body, name="sum_devices",
        grid=(1,),
        in_specs=[pl.BlockSpec((N_DEV, r, c), lambda i: (0, 0, 0)), pl.BlockSpec((r, c), lambda i: (0, 0))],
        out_specs=pl.BlockSpec((r, c), lambda i: (0, 0)),
        out_shape=jax.ShapeDtypeStruct((r, c), F32),
        compiler_params=_cparams(("arbitrary",)),
    )(land, own)


def _rs_sibling_start(fulls, segs, name):
    ns = len(segs)
    offs = _seg_offsets(segs)
    rtot = sum(n * r for n, r in segs)
    c = fulls[0].shape[-1]
    dt = fulls[0].dtype

    def body(*refs):
        srcs = refs[:ns]
        land_ref, send_sem, recv_sem = refs[ns], refs[ns + 1], refs[ns + 2]
        token = refs[-1]
        x, y, cc = _mesh_pos()
        for k in range(4):
            for a, (n, r) in enumerate(segs):
                for m in range(n):
                    theirs = srcs[a].at[m, pl.ds(pl.multiple_of((2 * k + 1 - cc) * r, r), r), :]
                    _remote(theirs, land_ref.at[k, pl.ds(offs[a] + m * r, r), :], send_sem, recv_sem,
                            (x, y, 1 - cc)).start()
        token[...] = jnp.zeros_like(token)

    outs = pl.pallas_call(
        body, name=name,
        in_specs=[HBM] * (ns + 1),
        out_specs=[SEM, SEM] + [HBM] * (ns + 1) + [VMEM_WHOLE],
        out_shape=[pltpu.SemaphoreType.DMA(()), pltpu.SemaphoreType.DMA(())]
        + [pltpu.HBM(a.shape, a.dtype) for a in fulls] + [pltpu.HBM((4, rtot, c), dt),
                                                           jax.ShapeDtypeStruct((SUBLANES, LANES), F32)],
        input_output_aliases={i: 2 + i for i in range(ns + 1)},
        compiler_params=pltpu.CompilerParams(has_side_effects=EFFECT),
    )(*[_hbm(a) for a in fulls], _hbm(lax.empty((4, rtot, c), dt)))
    return outs[0], outs[1], list(outs[2:2 + ns]), outs[2 + ns], outs[-1]


def _rs_sibling_wait(send_sem, recv_sem, fulls, land, after, name):
    ns = len(fulls)

    def body(*refs):
        land_ref, send_ref, recv_ref = refs[ns], refs[ns + 1], refs[ns + 2]
        whole = _remote(land_ref, land_ref, send_ref, recv_ref, _mesh_pos())
        whole.wait_send()
        whole.wait_recv()

    outs = pl.pallas_call(
        body, name=name,
        in_specs=[HBM] * (ns + 1) + [SEM, SEM, UNREAD], out_specs=[HBM] * (ns + 1),
        out_shape=[pltpu.HBM(a.shape, a.dtype) for a in fulls] + [pltpu.HBM(land.shape, land.dtype)],
        input_output_aliases={i: i for i in range(ns + 1)},
        compiler_params=pltpu.CompilerParams(has_side_effects=EFFECT),
    )(*fulls, land, send_sem, recv_sem, _hbm(after))
    return list(outs[:ns]), outs[ns]


def _tp(w):
    return jnp.swapaxes(w, -1, -2)


def _s5_prepare(a_re, a_im, log_dt, b_re, b_im, c_re, c_im):
    a = jnp.stack([a_re, a_im], axis=1)
    ldt = jnp.broadcast_to(log_dt[:, :, None], (DEPTH, SSM_GROUPS, SSM_STATE))
    a_row = a.reshape(DEPTH, 2, 1, N_STATE)
    ldt_row = ldt.reshape(DEPTH, 1, N_STATE)
    a_rep = jnp.repeat(a, SSM_GROUP, axis=2)
    ldt_rep = jnp.repeat(ldt, SSM_GROUP, axis=1)
    bt = jnp.stack([_tp(b_re), _tp(b_im)], axis=1).reshape(DEPTH, 2, SSM_W, SSM_STATE)
    ct = jnp.stack([c_re, c_im], axis=1).reshape(DEPTH, 2, SSM_W, SSM_STATE)
    tile_e = jnp.tile(jnp.eye(SSM_STATE, dtype=BF16), (1, SSM_GROUPS))
    mask = jnp.repeat(jnp.repeat(jnp.eye(SSM_GROUPS, dtype=BF16), SSM_GROUP, axis=0), SSM_STATE, axis=1)
    out = []
    for l in range(DEPTH):
        tabs = _s5_disc(a_row[l], ldt_row[l], a_rep[l], ldt_rep[l], bt[l], ct[l], tile_e, mask)
        out.append(((a[l], ldt[l], a_rep[l], ldt_rep[l], bt[l], mask), *tabs))
    return out


def _layer_fwd(h, p_l, small, big, arrive=None):
    saved = {'h0': h}
    if arrive is not None:
        arrive(0, h)
    h, saved['gu1'] = _ffn_fwd(h, small['ffn1_norm'], big['ff1'])
    saved['h1'] = h
    if arrive is not None:
        arrive(1, h)
    z = _inproj_fwd(h, small['mix_norm'], big['wint'])
    ya, ys, hs = _s5conv_fwd(z, small['conv_w'], small['conv_b'], small['bbmat'], small['ccmat'], small['dvec'],
                             small['ltab'])
    saved.update(z=z, ya=ya, ys=ys, hs=hs)
    h = _mix_out_fwd(h, ya, ys, big['glu'], small['glu_b'], small['conv_out_norm'], small['ssm_out_norm'], big['wout'])
    saved['h2'] = h
    if arrive is not None:
        arrive(2, h)
    h, saved['gu2'] = _ffn_fwd(h, small['ffn2_norm'], big['ff2'])
    saved['h3'] = h
    h = _ple_fwd(h, small['ple_norm'], p_l, big['plg'], big['plpt'])
    return h, saved


def _ffn_bwd(h_in, g, dh, gu, w3):
    dh_in, dga, ud, dg = _ffn_bwd_act(h_in, g, dh, gu, w3)
    return dh_in, _matmul_tn(dga, ud, FF_BLOCK, BF16, "ffn_wgrad"), dg


def _layer_bwd_top(dh, p_l, small, big, saved):
    gs = {}
    dh, u, dq, dpp, pb, gs['ple_norm'] = _ple_bwd(saved['h3'], small['ple_norm'], p_l, dh, big['plg'], big['plpt'])
    d_plg = _matmul_tn(u, dq, 256, BF16, "ple_gate_wgrad")
    d_plpt = _matmul_tn(dpp, pb, 256, BF16, "ple_proj_wgrad", to_kernel=False)
    dh, d_ff2, gs['ffn2_norm'] = _ffn_bwd(saved['h2'], small['ffn2_norm'], dh, saved['gu2'], big['ff2'])
    return dh, (gs, d_plg, d_plpt, d_ff2)


def _layer_bwd_rest(dh, top, small, big, saved):
    gs, d_plg, d_plpt, d_ff2 = top
    dya, dys, ycat, dhb, zg, dq, part = _mix_out_bwd(dh, saved['ya'], saved['ys'], big['glu'], small['glu_b'],
                                                     small['conv_out_norm'], small['ssm_out_norm'], big['wout'])
    d_wout = _matmul_tn(ycat, dhb, 256, BF16, "w_out_wgrad")
    d_glu = _matmul_tn(zg, dq, 256, BF16, "glu_wgrad", to_kernel=False)
    dz, gadj, us, dyb, dl, dcw = _s5conv_bwd(saved['z'], saved['hs'], dya, dys, small['conv_w'], small['conv_b'],
                                             small['bbmat'], small['ccmat'], small['dvec'], small['ltab_rev'])
    d_bb = _matmul_tn(us, gadj, SSM_W, F32, "s5_b_wgrad", 1024, False)[0]
    d_cc = _matmul_tn(dyb, saved['hs'][None], SSM_W, F32, "s5_c_wgrad", 1024, False)[0]
    dh, u, gs['mix_norm'] = _inproj_bwd(saved['h1'], small['mix_norm'], dh, dz, big['wint'])
    d_wint = _matmul_tn(dz[None], u, 256, BF16, "w_in_wgrad")
    dh, d_ff1, gs['ffn1_norm'] = _ffn_bwd(saved['h0'], small['ffn1_norm'], dh, saved['gu1'], big['ff1'])

    dlb = dl[0].reshape(2, SSM_GROUPS, SSM_STATE)
    fold = jnp.tile(jnp.eye(SSM_STATE, dtype=BF16), (SSM_GROUPS, 1))
    da, dldt, dbt, dct = _s5_disc_bwd(*small['disc_in'], dlb, d_bb, d_cc, fold)
    gs['ssm_A_re'], gs['ssm_A_im'] = da[0], da[1]
    gs['ssm_log_dt'] = dldt[:, 0]
    ghp = (SSM_GROUPS, SSM_GROUP, SSM_STATE)
    gs['ssm_B_re'], gs['ssm_B_im'] = dbt[0].reshape(ghp), dbt[1].reshape(ghp)
    gs['ssm_C_re'], gs['ssm_C_im'] = dct[0].reshape(ghp), dct[1].reshape(ghp)
    gs['conv_w'] = dcw[0:3]
    gs['conv_b'] = dcw[3]
    gs['ssm_D'] = dcw[4].reshape(SSM_GROUPS, SSM_GROUP)
    gs['conv_out_norm'], gs['ssm_out_norm'], gs['glu_b'] = part[0], part[1], part[2]
    for n in ('ple_norm', 'ffn2_norm', 'mix_norm', 'ffn1_norm'):
        gs[n] = gs[n][0]
    fulls = [d_ff1, d_ff2, d_wint, d_wout, d_plg,
             d_plpt.reshape(1, D_MODEL * PLE_DIM // D_MODEL, D_MODEL), d_glu.reshape(1, SSM_W * SSM_W // D_MODEL, D_MODEL)]
    return dh, fulls, gs


VIEW_T = ('ffn1_w_gate', 'ffn1_w_up', 'ffn2_w_gate', 'ffn2_w_up', 'ssm_B_re', 'ssm_B_im')


def _view(name, a):
    return _tp(a) if name in VIEW_T else a


SEG_NAMES = ('ff1', 'ff2', 'wint', 'wout', 'plg', 'plpt', 'glu')
FIRST_LAYER_GROUPS = ((0,), (2, 3, 6), (1, 4, 5))


def _layer_pack(W, l, segments=range(len(SEGS))):
    pieces = {
        0: lambda: [_tp(W['ffn1_w_gate'][l]), _tp(W['ffn1_w_up'][l]), W['ffn1_w_down'][l]],
        1: lambda: [_tp(W['ffn2_w_gate'][l]), _tp(W['ffn2_w_up'][l]), W['ffn2_w_down'][l]],
        2: lambda: [_tp(W['w_in'][l])],
        3: lambda: [W['w_out'][l]],
        4: lambda: [W['ple_w_gate'][l]],
        5: lambda: [_tp(W['ple_w_proj'][l]).reshape(-1, D_MODEL)],
        6: lambda: [W['glu_w'][l].reshape(-1, D_MODEL)],
    }
    return jnp.concatenate([a for s in segments for a in pieces[s]()], axis=0).astype(BF16)


def _as_big(named):
    shape = dict(plpt=(D_MODEL, PLE_DIM), glu=(SSM_W, SSM_W))
    return {n: (a.reshape(shape[n]) if n in shape else a) for n, a in named.items()}


def _pad_rows(flat, mult, width=LANES):
    per = mult * width
    n = flat.shape[0]
    tot = -(-n // per) * per
    return jnp.pad(flat, (0, tot - n)).reshape(tot // width, width)


def _adamw_any(w, g, m, v):
    shp = w.shape
    two = (lambda t: t.reshape(-1, shp[-1]))
    d, nm, nv = _adamw(two(w), two(g), two(m), two(v))
    return d.reshape(shp), nm.reshape(shp), nv.reshape(shp)


def kernel(x, p, ffn1_norm, ffn1_w_gate, ffn1_w_up, ffn1_w_down, mix_norm, w_in, conv_w, conv_b, ssm_A_re, ssm_A_im, ssm_B_re, ssm_B_im, ssm_C_re, ssm_C_im, ssm_D, ssm_log_dt, glu_w, glu_b, conv_out_norm, ssm_out_norm, w_out, ffn2_norm, ffn2_w_gate, ffn2_w_up, ffn2_w_down, ple_norm, ple_w_gate, ple_w_proj, final_norm, loss_target, m_ffn1_norm, m_ffn1_w_gate, m_ffn1_w_up, m_ffn1_w_down, m_mix_norm, m_w_in, m_conv_w, m_conv_b, m_ssm_A_re, m_ssm_A_im, m_ssm_B_re, m_ssm_B_im, m_ssm_C_re, m_ssm_C_im, m_ssm_D, m_ssm_log_dt, m_glu_w, m_glu_b, m_conv_out_norm, m_ssm_out_norm, m_w_out, m_ffn2_norm, m_ffn2_w_gate, m_ffn2_w_up, m_ffn2_w_down, m_ple_norm, m_ple_w_gate, m_ple_w_proj, m_final_norm, v_ffn1_norm, v_ffn1_w_gate, v_ffn1_w_up, v_ffn1_w_down, v_mix_norm, v_w_in, v_conv_w, v_conv_b, v_ssm_A_re, v_ssm_A_im, v_ssm_B_re, v_ssm_B_im, v_ssm_C_re, v_ssm_C_im, v_ssm_D, v_ssm_log_dt, v_glu_w, v_glu_b, v_conv_out_norm, v_ssm_out_norm, v_w_out, v_ffn2_norm, v_ffn2_w_gate, v_ffn2_w_up, v_ffn2_w_down, v_ple_norm, v_ple_w_gate, v_ple_w_proj, v_final_norm):
    given = dict(locals())
    W = {n: given[n] for n in W_NAMES}
    M = {n: given['m_' + n] for n in W_NAMES}
    V = {n: given['v_' + n] for n in W_NAMES}
    Wv, Mv, Vv = [{n: _view(n, d[n]) for n in W_NAMES} for d in (W, M, V)]
    my_dev = _dev_index(_mesh_pos())

    conv_shard = _pad_rows(W['conv_w'].reshape(-1), SUBLANES)
    conv_all = _allgather(conv_shard, ((1, SUBLANES),), "ag_conv_w")[0]
    conv_full = conv_all.reshape(N_DEV, -1)[:, :DEPTH * 3 * (CONV_W // N_DEV)]
    conv_full = conv_full.reshape(N_DEV, DEPTH, 3, CONV_W // N_DEV).transpose(1, 2, 0, 3).reshape(DEPTH, 3, CONV_W)
    first, after = [], conv_all
    for gi, segments in enumerate(FIRST_LAYER_GROUPS):
        first.append(_ag_start(_layer_pack(W, 0, segments), tuple(SEGS[s] for s in segments), after,
                               "ag_start_0%s" % "abc"[gi]))
        after = first[-1][4]
    s5 = _s5_prepare(*[W[n] + after[0, 0] for n in ('ssm_A_re', 'ssm_A_im', 'ssm_log_dt')],
                     *[W[n] for n in ('ssm_B_re', 'ssm_B_im', 'ssm_C_re', 'ssm_C_im')])
    packs = [None] + [_layer_pack(W, l) for l in range(1, DEPTH)]
    prepared = conv_full[0, 0:1, 0:1] + s5[DEPTH - 1][1][0:1, 0:1] + packs[DEPTH - 1][0:1, 0:1].astype(F32)

    smalls, saves, bigs = [], [], []
    h = x[0]

    flight = None

    def gathered(handles, segments, after, name, next_layer=None, gate=None):
        nonlocal flight
        send_sems, recv_sems, pack_thru, lands, _ = handles
        pack_thru, lands = _ag_wait(send_sems, recv_sems, pack_thru, lands, after, "ag_wait_" + name)
        if next_layer is not None:
            flight = _ag_start(packs[next_layer], SEGS, pack_thru, "ag_start_%d" % next_layer)
            gate[0][gate[1]] = gate[0][gate[1]] + flight[4][0:1, 0:1]
        outs = _ag_finish(pack_thru, lands, tuple(SEGS[s] for s in segments))
        return _as_big({SEG_NAMES[s]: a for s, a in zip(segments, outs)})

    for l in range(DEPTH):
        small = {n: W[n][l][None] for n in ('ffn1_norm', 'mix_norm', 'conv_b', 'glu_b', 'conv_out_norm',
                                            'ssm_out_norm', 'ffn2_norm', 'ple_norm')}
        small['conv_w'] = conv_full[l]
        small['dvec'] = W['ssm_D'][l].reshape(1, SSM_W)
        small['disc_in'], small['ltab'], small['ltab_rev'], small['bbmat'], small['ccmat'] = s5[l]
        big = {}
        bigs.append(big)
        if l == 0:
            def arrive(stage, h_now, big=big, small=small):
                big.update(gathered(first[stage], FIRST_LAYER_GROUPS[stage], prepared if stage == 0 else h_now,
                                    "0%s" % "abc"[stage], *((1, (small, 'ffn2_norm')) if stage == 2 else ())))
            h, saved = _layer_fwd(h, p[l, 0], small, big, arrive)
        else:
            nxt = (l + 1, (small, 'ffn1_norm')) if l + 1 < DEPTH else ()
            big.update(gathered(flight, range(len(SEGS)), h, "%d" % l, *nxt))
            h, saved = _layer_fwd(h, p[l, 0], small, big)
        smalls.append(small)
        saves.append(saved)
    loss_tile, dh, d_final = _final_loss(h, W['final_norm'][None], loss_target[0])
    loss = lax.psum(loss_tile[0, 0], ("x", "y", "c"))

    layer_gs = [None] * DEPTH
    shard_grads = [None] * DEPTH
    zero = jnp.zeros((1, 1), F32)
    sib, ici = None, None

    def finish_sibling(after_sib, after_ici):
        nonlocal sib, ici
        up, (send_sem, recv_sem, fulls_thru, land, _) = sib
        fulls_thru, got = _rs_sibling_wait(send_sem, recv_sem, fulls_thru, land, after_sib, "sib_wait_%d" % up)
        own32, pbf = _pair_sum(fulls_thru, got, SEGS)
        done = finish_chips(own32)
        ici = (up, _rs_chips_start(pbf, after_ici if done is None else done, "rs_start_%d" % up), own32)
        sib = None

    def finish_chips(after):
        nonlocal ici
        if ici is None:
            return None
        up, (send_sems, recv_sems, pbf_thru, land, _), own32 = ici
        got3 = _rs_chips_wait(send_sems, recv_sems, pbf_thru, land, after, "rs_wait_%d" % up)
        shard_grads[up] = _chip_sum(own32, got3)
        ici = None
        return shard_grads[up]

    layer_names = [n for n in SMALL_NAMES if n != 'final_norm']
    small_flights = [None] * DEPTH
    for l in reversed(range(DEPTH)):
        small = dict(smalls[l])
        if sib is not None:
            small['ple_norm'] = small['ple_norm'] + sib[1][4][0:1, 0:1] + small_flights[l + 1][4][0:1, 0:1]
        dh, top = _layer_bwd_top(dh, p[l, 0], small, bigs[l], saves[l])
        if sib is not None:
            finish_sibling(dh, dh)
            small['glu_b'] = small['glu_b'] + ici[1][4][0:1, 0:1]
        dh, fulls, layer_gs[l] = _layer_bwd_rest(dh, top, small, bigs[l], saves[l])
        sib = (l, _rs_sibling_start(fulls, SEGS, "sib_start_%d" % l))
        last_slot = d_final[0] if l == DEPTH - 1 else jnp.zeros((D_MODEL,), F32)
        flat = jnp.concatenate([layer_gs[l][n].reshape(-1) for n in layer_names + ['conv_w']] + [last_slot])
        small_flights[l] = _small_gather_start(_pad_rows(flat, SUBLANES, D_MODEL), "small_start_%d" % l)
    grad_x = dh[None]
    finish_sibling(small_flights[0][4], small_flights[0][4])

    reduced = []
    for l in range(DEPTH):
        send_sems, recv_sems, flat_thru, land, _ = small_flights[l]
        flat_thru, land = _small_gather_wait(send_sems, recv_sems, flat_thru, land, ici[1][4], "small_wait_%d" % l)
        reduced.append(_sum_devices(land, flat_thru).reshape(-1))
    finish_chips(reduced[0])
    G = {}
    o = 0
    for n in layer_names + ['conv_w']:
        size = (W[n].size if n != 'conv_w' else DEPTH * 3 * CONV_W) // DEPTH
        shape = Wv[n].shape if n != 'conv_w' else (DEPTH, 3, CONV_W)
        G[n] = jnp.stack([red[o:o + size] for red in reduced]).reshape(shape)
        o += size
    G['final_norm'] = reduced[DEPTH - 1][o:o + D_MODEL]
    G['conv_w'] = lax.dynamic_slice_in_dim(G['conv_w'], my_dev * (CONV_W // N_DEV), CONV_W // N_DEV, axis=2)

    sg = jnp.stack(shard_grads)
    offs = _seg_offsets(SEGS)
    r = SEGS[0][1]
    for a, f in ((0, 'ffn1'), (1, 'ffn2')):
        G[f + '_w_gate'] = sg[:, offs[a]:offs[a] + r]
        G[f + '_w_up'] = sg[:, offs[a] + r:offs[a] + 2 * r]
        G[f + '_w_down'] = sg[:, offs[a] + 2 * r:offs[a] + 3 * r]
    G['w_in'] = _tp(sg[:, offs[2]:offs[2] + SEGS[2][1]])
    G['w_out'] = sg[:, offs[3]:offs[3] + SEGS[3][1]]
    G['ple_w_gate'] = sg[:, offs[4]:offs[4] + SEGS[4][1]]
    G['ple_w_proj'] = _tp(sg[:, offs[5]:offs[5] + SEGS[5][1]].reshape(DEPTH, D_MODEL // N_DEV, PLE_DIM))
    G['glu_w'] = sg[:, offs[6]:offs[6] + SEGS[6][1]].reshape(DEPTH, SSM_W // N_DEV, SSM_W)

    delta, new_m, new_v = {}, {}, {}
    cat = lambda src: _pad_rows(jnp.concatenate([src[n].reshape(-1) for n in SMALL_NAMES]), SUBLANES, D_MODEL)
    d_s, m_s, v_s = _adamw(cat(Wv), cat(G), cat(Mv), cat(Vv))
    o = 0
    for n in SMALL_NAMES:
        for dst, src in ((delta, d_s), (new_m, m_s), (new_v, v_s)):
            dst[n] = src.reshape(-1)[o:o + W[n].size].reshape(Wv[n].shape)
        o += W[n].size
    for n in W_NAMES:
        if n not in delta:
            delta[n], new_m[n], new_v[n] = _adamw_any(Wv[n], G[n], Mv[n], Vv[n])

    outs = [[_view(n, d[n]) for n in W_NAMES] for d in (G, delta, new_m, new_v)]
    return (loss, grad_x, *outs[0], *outs[1], *outs[2], *outs[3])
```

```python
import math

import jax
import jax.numpy as jnp
from jax import lax
from jax.experimental import pallas as pl
from jax.experimental.pallas import tpu as pltpu

F32 = jnp.float32
BF16 = jnp.bfloat16

N_DEV = 8
DEPTH = 4
SEQ = 2048
D_MODEL = 1024
D_FF = 2816
CONV_W = 512
SSM_W = 512
SSM_GROUPS = 32
SSM_GROUP = 16
SSM_STATE = 64
N_STATE = SSM_GROUPS * SSM_STATE
IN_COLS = 2048
PLE_DIM = 256
EPS = 1e-6

ADAM_LR = 0.001
ADAM_B1 = 0.9
ADAM_B2 = 0.999
ADAM_EPS = 1e-08
ADAM_WD = 0.01
ADAM_STEP = 10

FF_BLOCK = 256
N_FF_BLOCKS = D_FF // FF_BLOCK
TOK_TILE_FFN_FWD = 2048
TOK_TILE_FFN_BWD = 1024
TOK_TILE = 512
CHUNK = 256
N_CHUNKS = SEQ // CHUNK
LANE_GROUP = 512
SUBLANES = 8
LANES = 128
MIB = 1024 * 1024

W_NAMES = ['ffn1_norm', 'ffn1_w_gate', 'ffn1_w_up', 'ffn1_w_down', 'mix_norm', 'w_in', 'conv_w', 'conv_b',
           'ssm_A_re', 'ssm_A_im', 'ssm_B_re', 'ssm_B_im', 'ssm_C_re', 'ssm_C_im', 'ssm_D', 'ssm_log_dt',
           'glu_w', 'glu_b', 'conv_out_norm', 'ssm_out_norm', 'w_out', 'ffn2_norm', 'ffn2_w_gate', 'ffn2_w_up',
           'ffn2_w_down', 'ple_norm', 'ple_w_gate', 'ple_w_proj', 'final_norm']
SMALL_NAMES = ['ffn1_norm', 'mix_norm', 'conv_b', 'ssm_A_re', 'ssm_A_im', 'ssm_B_re', 'ssm_B_im', 'ssm_C_re',
               'ssm_C_im', 'ssm_D', 'ssm_log_dt', 'glu_b', 'conv_out_norm', 'ssm_out_norm', 'ffn2_norm',
               'ple_norm', 'final_norm']

SEGS = ((3, 352), (3, 352), (1, 256), (1, 128), (1, 128), (1, 32), (1, 32))
PACK_ROWS = sum(n * r for n, r in SEGS)

MESH = pl.DeviceIdType.MESH
UNREAD = pl.BlockSpec(memory_space=pltpu.HBM)


def _in_hbm(*arrays):
    return [pltpu.with_memory_space_constraint(a, pltpu.HBM) for a in arrays]


def _out_hbm(outs, which):
    if not isinstance(outs, (list, tuple)):
        return pltpu.with_memory_space_constraint(outs, pltpu.HBM) if which else outs
    return [pltpu.with_memory_space_constraint(a, pltpu.HBM) if i in which else a for i, a in enumerate(outs)]


def _cparams(sem=None, vmem_mib=48, **kw):
    return pltpu.CompilerParams(dimension_semantics=sem, vmem_limit_bytes=vmem_mib * MIB, **kw)


def _dot(a, b):
    return jnp.dot(a, b, preferred_element_type=F32)


def _dot_nt(a, b):
    return lax.dot_general(a, b, (((1,), (1,)), ((), ())), preferred_element_type=F32)


def _dot_tn(a, b):
    return lax.dot_general(a, b, (((0,), (0,)), ((), ())), preferred_element_type=F32)


def _rms_stats(x):
    r = lax.rsqrt(jnp.mean(x * x, axis=-1, keepdims=True) + EPS)
    return x * r, r


def _rms_bwd(dy, xh, r, g):
    dxh = dy * g
    dx = r * (dxh - xh * jnp.mean(dxh * xh, axis=-1, keepdims=True))
    dg = jnp.sum(dy * xh, axis=0, keepdims=True)
    return dx, dg


def _sigmoid(x):
    return 0.5 * jnp.tanh(0.5 * x) + 0.5


_GELU_C = math.sqrt(2.0 / math.pi)


def _gelu(x):
    t = jnp.tanh(_GELU_C * (x + 0.044715 * x * x * x))
    return 0.5 * x * (1.0 + t), t


def _gelu_grad(x, t):
    return 0.5 * (1.0 + t) + 0.5 * x * (1.0 - t * t) * _GELU_C * (1.0 + 3.0 * 0.044715 * x * x)


def _accumulate(ref, first, value):
    @pl.when(first)
    def _():
        ref[...] = value

    @pl.when(jnp.logical_not(first))
    def _():
        ref[...] += value


def _ffn_fwd(h, g, w3):
    tm = TOK_TILE_FFN_FWD
    last = N_FF_BLOCKS - 1

    def body(h_ref, g_ref, wgu_ref, wd_ref, wd_last_ref, out_ref, gu_ref, u_ref, a_ref):
        k = pl.program_id(1)

        @pl.when(k == 0)
        def _():
            x = h_ref[...]
            xh, _ = _rms_stats(x)
            u_ref[...] = (xh * g_ref[...]).astype(BF16)
            out_ref[...] = x
            a_ref[1] = jnp.zeros((tm, FF_BLOCK), BF16)

        out_ref[...] += 0.5 * _dot(a_ref[(k + 1) % 2], wd_ref[0])
        gu = _dot_nt(u_ref[...], wgu_ref[...].reshape(2 * FF_BLOCK, D_MODEL))
        gate, up = gu[:, :FF_BLOCK], gu[:, FF_BLOCK:]
        a_ref[k % 2] = (gate * _sigmoid(gate) * up).astype(BF16)
        gu_ref[0] = gate.astype(BF16)
        gu_ref[1] = up.astype(BF16)

        @pl.when(k == last)
        def _():
            out_ref[...] += 0.5 * _dot(a_ref[last % 2], wd_last_ref[0])

    return _out_hbm(pl.pallas_call(
        body, name="ffn_fwd",
        grid=(SEQ // tm, N_FF_BLOCKS),
        in_specs=[pl.BlockSpec((tm, D_MODEL), lambda m, k: (m, 0), pipeline_mode=pl.Buffered(1)),
                  pl.BlockSpec((1, D_MODEL), lambda m, k: (0, 0)),
                  pl.BlockSpec((2, FF_BLOCK, D_MODEL), lambda m, k: (0, k, 0)),
                  pl.BlockSpec((1, FF_BLOCK, D_MODEL), lambda m, k: (2, jnp.maximum(k - 1, 0), 0)),
                  pl.BlockSpec((1, FF_BLOCK, D_MODEL), lambda m, k: (2, last, 0), pipeline_mode=pl.Buffered(1))],
        out_specs=[pl.BlockSpec((tm, D_MODEL), lambda m, k: (m, 0)),
                   pl.BlockSpec((2, tm, FF_BLOCK), lambda m, k: (0, m, k))],
        out_shape=[jax.ShapeDtypeStruct((SEQ, D_MODEL), F32),
                   pltpu.HBM((2, SEQ, D_FF), BF16)],
        scratch_shapes=[pltpu.VMEM((tm, D_MODEL), BF16), pltpu.VMEM((2, tm, FF_BLOCK), BF16)],
        compiler_params=_cparams(("parallel", "arbitrary"), 56),
    )(*_in_hbm(h, g, w3, w3, w3)), (1,))


def _ffn_bwd_act(h, g, dout, gu, w3):
    tm = TOK_TILE_FFN_BWD
    last = N_FF_BLOCKS - 1

    def body(h_ref, g_ref, d_ref, gu_ref, wd_ref, wgu_ref, wgu_last_ref, dh_ref, dga_ref, ud_ref, dg_ref,
             acc_ref, dgu_ref):
        m = pl.program_id(0)
        k = pl.program_id(1)

        @pl.when(k == 0)
        def _():
            xh, _ = _rms_stats(h_ref[...])
            ud_ref[0] = (xh * g_ref[...]).astype(BF16)
            ud_ref[1] = (0.5 * d_ref[...]).astype(BF16)
            acc_ref[...] = jnp.zeros_like(acc_ref)
            dgu_ref[1] = jnp.zeros((tm, 2 * FF_BLOCK), BF16)

        acc_ref[...] += _dot(dgu_ref[(k + 1) % 2], wgu_ref[...].reshape(2 * FF_BLOCK, D_MODEL))
        gate = gu_ref[0].astype(F32)
        up = gu_ref[1].astype(F32)
        sg = _sigmoid(gate)
        silu = gate * sg
        da = _dot_nt(ud_ref[1], wd_ref[0])
        dgate = (da * up * (sg + silu * (1.0 - sg))).astype(BF16)
        dup = (da * silu).astype(BF16)
        dga_ref[0] = dgate
        dga_ref[1] = dup
        dga_ref[2] = (silu * up).astype(BF16)
        dgu_ref[k % 2, :, 0:FF_BLOCK] = dgate
        dgu_ref[k % 2, :, FF_BLOCK:2 * FF_BLOCK] = dup

        @pl.when(k == last)
        def _():
            du = acc_ref[...] + _dot(dgu_ref[last % 2], wgu_last_ref[...].reshape(2 * FF_BLOCK, D_MODEL))
            xh, r = _rms_stats(h_ref[...])
            dx, dg = _rms_bwd(du, xh, r, g_ref[...])
            dh_ref[...] = d_ref[...] + dx
            _accumulate(dg_ref, m == 0, dg)

    return _out_hbm(pl.pallas_call(
        body, name="ffn_bwd_act",
        grid=(SEQ // tm, N_FF_BLOCKS),
        in_specs=[pl.BlockSpec((tm, D_MODEL), lambda m, k: (m, 0), pipeline_mode=pl.Buffered(1)),
                  pl.BlockSpec((1, D_MODEL), lambda m, k: (0, 0)),
                  pl.BlockSpec((tm, D_MODEL), lambda m, k: (m, 0), pipeline_mode=pl.Buffered(1)),
                  pl.BlockSpec((2, tm, FF_BLOCK), lambda m, k: (0, m, k)),
                  pl.BlockSpec((1, FF_BLOCK, D_MODEL), lambda m, k: (2, k, 0)),
                  pl.BlockSpec((2, FF_BLOCK, D_MODEL), lambda m, k: (0, jnp.maximum(k - 1, 0), 0)),
                  pl.BlockSpec((2, FF_BLOCK, D_MODEL), lambda m, k: (0, last, 0), pipeline_mode=pl.Buffered(1))],
        out_specs=[pl.BlockSpec((tm, D_MODEL), lambda m, k: (m, 0)),
                   pl.BlockSpec((3, tm, FF_BLOCK), lambda m, k: (0, m, k)),
                   pl.BlockSpec((2, tm, D_MODEL), lambda m, k: (0, m, 0)),
                   pl.BlockSpec((1, D_MODEL), lambda m, k: (0, 0))],
        out_shape=[jax.ShapeDtypeStruct((SEQ, D_MODEL), F32),
                   pltpu.HBM((3, SEQ, D_FF), BF16),
                   pltpu.HBM((2, SEQ, D_MODEL), BF16),
                   jax.ShapeDtypeStruct((1, D_MODEL), F32)],
        scratch_shapes=[pltpu.VMEM((tm, D_MODEL), F32), pltpu.VMEM((2, tm, 2 * FF_BLOCK), BF16)],
        compiler_params=_cparams(("arbitrary", "arbitrary"), 56),
    )(*_in_hbm(h, g, dout, gu, w3, w3, w3)), (1, 2))


def _matmul_tn(a, b, bm, out_dtype, name, bn=None, to_kernel=True):
    na, t, m = a.shape
    nb, _, n = b.shape
    bn = n if bn is None else bn

    def body(a_ref, b_ref, o_ref):
        o_ref[0] = _dot_tn(a_ref[0], b_ref[0]).astype(out_dtype)

    return _out_hbm(pl.pallas_call(
        body, name=name,
        grid=(na, m // bm, n // bn),
        in_specs=[pl.BlockSpec((1, t, bm), lambda i, k, j: (i, 0, k)),
                  pl.BlockSpec((1, t, bn), lambda i, k, j: (jnp.maximum(i - (na - nb), 0), 0, j))],
        out_specs=pl.BlockSpec((1, bm, bn), lambda i, k, j: (i, k, j)),
        out_shape=pltpu.HBM((na, m, n), out_dtype) if to_kernel else jax.ShapeDtypeStruct((na, m, n), out_dtype),
        compiler_params=_cparams(("arbitrary", "parallel", "parallel")),
    )(*_in_hbm(a, b)), to_kernel)


def _inproj_fwd(h, g, wint):
    tm = TOK_TILE

    def body(h_ref, g_ref, w_ref, z_ref):
        xh, _ = _rms_stats(h_ref[...])
        z_ref[...] = _dot_nt((xh * g_ref[...]).astype(BF16), w_ref[...])

    return pl.pallas_call(
        body, name="inproj_fwd",
        grid=(SEQ // tm,),
        in_specs=[pl.BlockSpec((tm, D_MODEL), lambda m: (m, 0)),
                  pl.BlockSpec((1, D_MODEL), lambda m: (0, 0)),
                  pl.BlockSpec((None, IN_COLS, D_MODEL), lambda m: (0, 0, 0))],
        out_specs=pl.BlockSpec((tm, IN_COLS), lambda m: (m, 0)),
        out_shape=jax.ShapeDtypeStruct((SEQ, IN_COLS), F32),
        compiler_params=_cparams(("parallel",)),
    )(*_in_hbm(h, g, wint))


def _inproj_bwd(h, g, dh, dz, wint):
    tm = TOK_TILE

    def body(h_ref, g_ref, dh_ref, dz_ref, w_ref, o_ref, u_ref, dg_ref):
        xh, r = _rms_stats(h_ref[...])
        u_ref[0] = (xh * g_ref[...]).astype(BF16)
        dx, dg = _rms_bwd(_dot(dz_ref[...], w_ref[...]), xh, r, g_ref[...])
        o_ref[...] = dh_ref[...] + dx
        _accumulate(dg_ref, pl.program_id(0) == 0, dg)

    return _out_hbm(pl.pallas_call(
        body, name="inproj_bwd",
        grid=(SEQ // tm,),
        in_specs=[pl.BlockSpec((tm, D_MODEL), lambda m: (m, 0)),
                  pl.BlockSpec((1, D_MODEL), lambda m: (0, 0)),
                  pl.BlockSpec((tm, D_MODEL), lambda m: (m, 0)),
                  pl.BlockSpec((tm, IN_COLS), lambda m: (m, 0)),
                  pl.BlockSpec((None, IN_COLS, D_MODEL), lambda m: (0, 0, 0))],
        out_specs=[pl.BlockSpec((tm, D_MODEL), lambda m: (m, 0)),
                   pl.BlockSpec((1, tm, D_MODEL), lambda m: (0, m, 0)),
                   pl.BlockSpec((1, D_MODEL), lambda m: (0, 0))],
        out_shape=[jax.ShapeDtypeStruct((SEQ, D_MODEL), F32),
                   pltpu.HBM((1, SEQ, D_MODEL), BF16),
                   jax.ShapeDtypeStruct((1, D_MODEL), F32)],
        compiler_params=_cparams(("arbitrary",)),
    )(*_in_hbm(h, g, dh, dz, wint)), (1,))


def _row_ids(n, w):
    return lax.broadcasted_iota(jnp.int32, (n, w), 0)


def _bcast_row(x, i, n):
    return jnp.broadcast_to(x[i:i + 1, :], (n, x.shape[1]))


def _conv_taps(v, tail):
    n, w = v.shape
    rid = _row_ids(n, w)
    v1 = jnp.where(rid == 0, _bcast_row(tail, 7, n), pltpu.roll(v, 1, 0))
    v2 = jnp.where(rid == 0, _bcast_row(tail, 6, n),
                   jnp.where(rid == 1, _bcast_row(tail, 7, n), pltpu.roll(v, 2, 0)))
    return v1, v2


def _scan_chunk(work, ltab, carry, reverse):
    nblk = CHUNK // SUBLANES
    for gi in range(N_STATE // LANE_GROUP):
        cre = pl.ds(gi * LANE_GROUP, LANE_GROUP)
        cim = pl.ds(N_STATE + gi * LANE_GROUP, LANE_GROUP)
        pows = [(ltab[8 * k:8 * k + 8, cre], ltab[8 * k:8 * k + 8, cim]) for k in range(3)]
        pr = ltab[24:32, cre]
        pi = ltab[24:32, cim]

        def blk(i, c, cre=cre, cim=cim, pows=pows, pr=pr, pi=pi):
            cr, ci = c
            b = (nblk - 1 - i) if reverse else i
            r0 = pl.multiple_of(b * SUBLANES, SUBLANES)
            xr = work[pl.ds(r0, SUBLANES), cre]
            xi = work[pl.ds(r0, SUBLANES), cim]
            for k, s in enumerate((1, 2, 4)):
                lr, li = pows[k]
                shift = SUBLANES - s if reverse else s
                sr = pltpu.roll(xr, shift, 0)
                si = pltpu.roll(xi, shift, 0)
                xr, xi = xr + lr * sr - li * si, xi + lr * si + li * sr
            xr, xi = xr + pr * cr - pi * ci, xi + pr * ci + pi * cr
            work[pl.ds(r0, SUBLANES), cre] = xr
            work[pl.ds(r0, SUBLANES), cim] = xi
            edge = 0 if reverse else SUBLANES - 1
            return _bcast_row(xr, edge, SUBLANES), _bcast_row(xi, edge, SUBLANES)

        cr, ci = lax.fori_loop(0, nblk, blk, (carry[:, cre], carry[:, cim]))
        carry[:, cre] = cr
        carry[:, cim] = ci


def _s5conv_fwd(z, convw, convb, bbmat, ccmat, dvec, ltab):
    def body(z_ref, cw_ref, cb_ref, bb_ref, cc_ref, d_ref, lt_ref, ya_ref, ys_ref, hs_ref,
             work, carry, tail):
        c = pl.program_id(0)

        @pl.when(c == 0)
        def _():
            carry[...] = jnp.zeros_like(carry)
            tail[...] = jnp.zeros_like(tail)

        zb = z_ref[:, 0:CONV_W]
        v = z_ref[:, CONV_W:2 * CONV_W] * z_ref[:, 2 * CONV_W:3 * CONV_W]
        us = z_ref[:, 3 * CONV_W:4 * CONV_W]
        v1, v2 = _conv_taps(v, tail[...])
        tail[...] = v[CHUNK - 8:CHUNK, :]
        y = cw_ref[0:1, :] * v2 + cw_ref[1:2, :] * v1 + cw_ref[2:3, :] * v
        ya_ref[...] = zb * (y + cb_ref[...])

        work[...] = _dot(us.astype(BF16), bb_ref[...])
        _scan_chunk(work, lt_ref, carry, reverse=False)
        hs = work[...].astype(BF16)
        hs_ref[...] = hs
        ys_ref[...] = _dot_nt(hs, cc_ref[...]) + d_ref[...] * us

    return _out_hbm(pl.pallas_call(
        body, name="s5conv_fwd",
        grid=(N_CHUNKS,),
        in_specs=[pl.BlockSpec((CHUNK, IN_COLS), lambda c: (c, 0)),
                  pl.BlockSpec((3, CONV_W), lambda c: (0, 0)),
                  pl.BlockSpec((1, CONV_W), lambda c: (0, 0)),
                  pl.BlockSpec((SSM_W, 2 * N_STATE), lambda c: (0, 0)),
                  pl.BlockSpec((SSM_W, 2 * N_STATE), lambda c: (0, 0)),
                  pl.BlockSpec((1, SSM_W), lambda c: (0, 0)),
                  pl.BlockSpec((32, 2 * N_STATE), lambda c: (0, 0))],
        out_specs=[pl.BlockSpec((CHUNK, CONV_W), lambda c: (c, 0)),
                   pl.BlockSpec((CHUNK, SSM_W), lambda c: (c, 0)),
                   pl.BlockSpec((CHUNK, 2 * N_STATE), lambda c: (c, 0))],
        out_shape=[pltpu.HBM((SEQ, CONV_W), F32),
                   pltpu.HBM((SEQ, SSM_W), F32),
                   jax.ShapeDtypeStruct((SEQ, 2 * N_STATE), BF16)],
        scratch_shapes=[pltpu.VMEM((CHUNK, 2 * N_STATE), F32),
                        pltpu.VMEM((8, 2 * N_STATE), F32),
                        pltpu.VMEM((8, CONV_W), F32)],
        compiler_params=_cparams(("arbitrary",)),
    )(*_in_hbm(z, convw, convb, bbmat, ccmat, dvec, ltab)), (0, 1))


def _s5conv_bwd(z, hs, dya, dys, convw, convb, bbmat, ccmat, dvec, ltab_rev):
    nc = N_CHUNKS
    hb = 16

    def body(z_ref, zp_ref, hs_ref, hp_ref, dya_ref, dys_ref, cw_ref, cb_ref, bb_ref, cc_ref, d_ref, lt_ref,
             dz_ref, g_ref, us_ref, dyb_ref, dl_ref, dcw_ref, work, carry, head):
        i = pl.program_id(0)
        first_chunk = i == nc - 1

        @pl.when(i == 0)
        def _():
            carry[...] = jnp.zeros_like(carry)
            head[...] = jnp.zeros_like(head)
            dl_ref[...] = jnp.zeros_like(dl_ref)
            dcw_ref[...] = jnp.zeros_like(dcw_ref)

        us = z_ref[:, 3 * CONV_W:4 * CONV_W]
        dy = dys_ref[...]
        dy_bf = dy.astype(BF16)
        us_ref[0] = us.astype(BF16)
        dyb_ref[0] = dy_bf

        work[...] = _dot(dy_bf, cc_ref[...])
        _scan_chunk(work, lt_ref, carry, reverse=True)
        gg = work[...]
        gg_bf = gg.astype(BF16)
        g_ref[0] = gg_bf
        dus = d_ref[...] * dy + _dot_nt(gg_bf, bb_ref[...])

        hcur = hs_ref[...].astype(F32)
        hlast = hp_ref[...].astype(F32)[hb - 1:hb, :]
        hlast = jnp.where(first_chunk, 0.0, hlast)
        rid = _row_ids(CHUNK, 2 * N_STATE)
        hprev = jnp.where(rid == 0, jnp.broadcast_to(hlast, (CHUNK, 2 * N_STATE)), pltpu.roll(hcur, 1, 0))
        gr, gi = gg[:, :N_STATE], gg[:, N_STATE:]
        hr, hi = hprev[:, :N_STATE], hprev[:, N_STATE:]
        dl_ref[:, :N_STATE] += (gr * hr + gi * hi).reshape(CHUNK // 8, 8, N_STATE).sum(axis=0)
        dl_ref[:, N_STATE:] += (gi * hr - gr * hi).reshape(CHUNK // 8, 8, N_STATE).sum(axis=0)

        @pl.when(i == nc - 1)
        def _():
            dl_ref[0:1, :] = jnp.sum(dl_ref[...], axis=0, keepdims=True)

        zb = z_ref[:, 0:CONV_W]
        zc = z_ref[:, CONV_W:2 * CONV_W]
        zv = z_ref[:, 2 * CONV_W:3 * CONV_W]
        v = zc * zv
        vtail = jnp.where(first_chunk, 0.0, zp_ref[:, CONV_W:2 * CONV_W] * zp_ref[:, 2 * CONV_W:3 * CONV_W])
        v1, v2 = _conv_taps(v, vtail)
        w0, w1, w2 = cw_ref[0:1, :], cw_ref[1:2, :], cw_ref[2:3, :]
        y = w0 * v2 + w1 * v1 + w2 * v
        dya_v = dya_ref[...]
        dzb = dya_v * (y + cb_ref[...])
        dyc = dya_v * zb
        hd = head[...]
        rc = _row_ids(CHUNK, CONV_W)
        n1 = jnp.where(rc == CHUNK - 1, _bcast_row(hd, 0, CHUNK), pltpu.roll(dyc, CHUNK - 1, 0))
        n2 = jnp.where(rc == CHUNK - 1, _bcast_row(hd, 1, CHUNK),
                       jnp.where(rc == CHUNK - 2, _bcast_row(hd, 0, CHUNK), pltpu.roll(dyc, CHUNK - 2, 0)))
        head[...] = dyc[0:8, :]
        dv = w2 * dyc + w1 * n1 + w0 * n2
        dz_ref[:, 0:CONV_W] = dzb.astype(BF16)
        dz_ref[:, CONV_W:2 * CONV_W] = (dv * zv).astype(BF16)
        dz_ref[:, 2 * CONV_W:3 * CONV_W] = (dv * zc).astype(BF16)
        dz_ref[:, 3 * CONV_W:4 * CONV_W] = dus.astype(BF16)
        dcw_ref[0:1, :] += jnp.sum(dyc * v2, axis=0, keepdims=True)
        dcw_ref[1:2, :] += jnp.sum(dyc * v1, axis=0, keepdims=True)
        dcw_ref[2:3, :] += jnp.sum(dyc * v, axis=0, keepdims=True)
        dcw_ref[3:4, :] += jnp.sum(dyc, axis=0, keepdims=True)
        dcw_ref[4:5, :] += jnp.sum(dy * us, axis=0, keepdims=True)

    rev = lambda i: nc - 1 - i
    return _out_hbm(pl.pallas_call(
        body, name="s5conv_bwd",
        grid=(nc,),
        in_specs=[pl.BlockSpec((CHUNK, IN_COLS), lambda i: (rev(i), 0)),
                  pl.BlockSpec((8, IN_COLS), lambda i: (jnp.maximum(rev(i) * (CHUNK // 8) - 1, 0), 0)),
                  pl.BlockSpec((CHUNK, 2 * N_STATE), lambda i: (rev(i), 0)),
                  pl.BlockSpec((hb, 2 * N_STATE), lambda i: (jnp.maximum(rev(i) * (CHUNK // hb) - 1, 0), 0)),
                  pl.BlockSpec((CHUNK, CONV_W), lambda i: (rev(i), 0)),
                  pl.BlockSpec((CHUNK, SSM_W), lambda i: (rev(i), 0)),
                  pl.BlockSpec((3, CONV_W), lambda i: (0, 0)),
                  pl.BlockSpec((1, CONV_W), lambda i: (0, 0)),
                  pl.BlockSpec((SSM_W, 2 * N_STATE), lambda i: (0, 0)),
                  pl.BlockSpec((SSM_W, 2 * N_STATE), lambda i: (0, 0)),
                  pl.BlockSpec((1, SSM_W), lambda i: (0, 0)),
                  pl.BlockSpec((32, 2 * N_STATE), lambda i: (0, 0))],
        out_specs=[pl.BlockSpec((CHUNK, IN_COLS), lambda i: (rev(i), 0)),
                   pl.BlockSpec((1, CHUNK, 2 * N_STATE), lambda i: (0, rev(i), 0)),
                   pl.BlockSpec((1, CHUNK, SSM_W), lambda i: (0, rev(i), 0)),
                   pl.BlockSpec((1, CHUNK, SSM_W), lambda i: (0, rev(i), 0)),
                   pl.BlockSpec((8, 2 * N_STATE), lambda i: (0, 0)),
                   pl.BlockSpec((8, CONV_W), lambda i: (0, 0))],
        out_shape=[jax.ShapeDtypeStruct((SEQ, IN_COLS), BF16),
                   pltpu.HBM((1, SEQ, 2 * N_STATE), BF16),
                   pltpu.HBM((1, SEQ, SSM_W), BF16),
                   pltpu.HBM((1, SEQ, SSM_W), BF16),
                   jax.ShapeDtypeStruct((8, 2 * N_STATE), F32),
                   jax.ShapeDtypeStruct((8, CONV_W), F32)],
        scratch_shapes=[pltpu.VMEM((CHUNK, 2 * N_STATE), F32),
                        pltpu.VMEM((8, 2 * N_STATE), F32),
                        pltpu.VMEM((8, CONV_W), F32)],
        compiler_params=_cparams(("arbitrary",)),
    )(*_in_hbm(z, z, hs, hs, dya, dys, convw, convb, bbmat, ccmat, dvec, ltab_rev)), (1, 2, 3))


def _mix_out_fwd(h, ya, ys, gluw, glub, con, son, wout):
    tm = TOK_TILE

    def body(h_ref, ya_ref, ys_ref, gw_ref, gb_ref, con_ref, son_ref, wo_ref, o_ref):
        zg, _ = _gelu(ys_ref[...])
        q = _dot(zg.astype(BF16), gw_ref[...]) + gb_ref[...]
        out_s = zg * _sigmoid(q)
        na, _ = _rms_stats(ya_ref[...])
        ns, _ = _rms_stats(out_s)
        o_ref[...] = (h_ref[...]
                      + _dot((na * con_ref[...]).astype(BF16), wo_ref[0:CONV_W, :])
                      + _dot((ns * son_ref[...]).astype(BF16), wo_ref[CONV_W:2 * CONV_W, :]))

    row = lambda m: (m, 0)
    fixed = lambda m: (0, 0)
    return pl.pallas_call(
        body, name="mix_out_fwd",
        grid=(SEQ // tm,),
        in_specs=[pl.BlockSpec((tm, D_MODEL), row), pl.BlockSpec((tm, CONV_W), row), pl.BlockSpec((tm, SSM_W), row),
                  pl.BlockSpec((SSM_W, SSM_W), fixed), pl.BlockSpec((1, SSM_W), fixed),
                  pl.BlockSpec((1, CONV_W), fixed), pl.BlockSpec((1, SSM_W), fixed),
                  pl.BlockSpec((None, D_MODEL, D_MODEL), lambda m: (0, 0, 0))],
        out_specs=pl.BlockSpec((tm, D_MODEL), row),
        out_shape=jax.ShapeDtypeStruct((SEQ, D_MODEL), F32),
        compiler_params=_cparams(("parallel",)),
    )(*_in_hbm(h, ya, ys, gluw, glub, con, son, wout))


def _mix_out_bwd(dh, ya, ys, gluw, glub, con, son, wout):
    tm = TOK_TILE

    def body(dh_ref, ya_ref, ys_ref, gw_ref, gb_ref, con_ref, son_ref, wo_ref,
             dya_ref, dys_ref, yc_ref, dhb_ref, zg_ref, dq_ref, part_ref):
        ysv = ys_ref[...]
        zg, th = _gelu(ysv)
        zg_bf = zg.astype(BF16)
        s = _sigmoid(_dot(zg_bf, gw_ref[...]) + gb_ref[...])
        out_s = zg * s
        na, ra = _rms_stats(ya_ref[...])
        ns, rs = _rms_stats(out_s)
        dh_bf = dh_ref[...].astype(BF16)
        yc_ref[0, :, 0:CONV_W] = (na * con_ref[...]).astype(BF16)
        yc_ref[0, :, CONV_W:2 * CONV_W] = (ns * son_ref[...]).astype(BF16)
        dhb_ref[0] = dh_bf
        dca = _dot_nt(dh_bf, wo_ref[0:CONV_W, :])
        dcs = _dot_nt(dh_bf, wo_ref[CONV_W:2 * CONV_W, :])
        dya, dcon = _rms_bwd(dca, na, ra, con_ref[...])
        dos, dson = _rms_bwd(dcs, ns, rs, son_ref[...])
        dya_ref[...] = dya
        dq = dos * zg * s * (1.0 - s)
        dq_bf = dq.astype(BF16)
        dzg = dos * s + _dot_nt(dq_bf, gw_ref[...])
        dys_ref[...] = dzg * _gelu_grad(ysv, th)
        zg_ref[0] = zg_bf
        dq_ref[0] = dq_bf
        rid = _row_ids(SUBLANES, SSM_W)
        part = jnp.zeros((SUBLANES, SSM_W), F32)
        for i, rowv in enumerate((dcon, dson, jnp.sum(dq, axis=0, keepdims=True))):
            part = jnp.where(rid == i, jnp.broadcast_to(rowv, (SUBLANES, SSM_W)), part)
        _accumulate(part_ref, pl.program_id(0) == 0, part)

    row = lambda m: (m, 0)
    fixed = lambda m: (0, 0)
    lead = lambda m: (0, m, 0)
    return _out_hbm(pl.pallas_call(
        body, name="mix_out_bwd",
        grid=(SEQ // tm,),
        in_specs=[pl.BlockSpec((tm, D_MODEL), row), pl.BlockSpec((tm, CONV_W), row), pl.BlockSpec((tm, SSM_W), row),
                  pl.BlockSpec((SSM_W, SSM_W), fixed), pl.BlockSpec((1, SSM_W), fixed),
                  pl.BlockSpec((1, CONV_W), fixed), pl.BlockSpec((1, SSM_W), fixed),
                  pl.BlockSpec((None, D_MODEL, D_MODEL), lambda m: (0, 0, 0))],
        out_specs=[pl.BlockSpec((tm, CONV_W), row), pl.BlockSpec((tm, SSM_W), row),
                   pl.BlockSpec((1, tm, D_MODEL), lead), pl.BlockSpec((1, tm, D_MODEL), lead),
                   pl.BlockSpec((1, tm, SSM_W), lead), pl.BlockSpec((1, tm, SSM_W), lead),
                   pl.BlockSpec((8, SSM_W), fixed)],
        out_shape=[pltpu.HBM((SEQ, CONV_W), F32), pltpu.HBM((SEQ, SSM_W), F32),
                   pltpu.HBM((1, SEQ, D_MODEL), BF16), pltpu.HBM((1, SEQ, D_MODEL), BF16),
                   pltpu.HBM((1, SEQ, SSM_W), BF16), pltpu.HBM((1, SEQ, SSM_W), BF16),
                   jax.ShapeDtypeStruct((8, SSM_W), F32)],
        compiler_params=_cparams(("arbitrary",)),
    )(*_in_hbm(dh, ya, ys, gluw, glub, con, son, wout)), (0, 1, 2, 3, 4, 5))


def _ple_fwd(h, g, p, wgate, wprojt):
    tm = TOK_TILE

    def body(h_ref, g_ref, p_ref, wg_ref, wp_ref, o_ref):
        x = h_ref[...]
        xh, _ = _rms_stats(x)
        s = _sigmoid(_dot((xh * g_ref[...]).astype(BF16), wg_ref[...]))
        o_ref[...] = x + _dot_nt(p_ref[...].astype(BF16), wp_ref[...]) * s

    row = lambda m: (m, 0)
    fixed = lambda m: (0, 0)
    return pl.pallas_call(
        body, name="ple_fwd",
        grid=(SEQ // tm,),
        in_specs=[pl.BlockSpec((tm, D_MODEL), row), pl.BlockSpec((1, D_MODEL), fixed), pl.BlockSpec((tm, PLE_DIM), row),
                  pl.BlockSpec((None, D_MODEL, D_MODEL), lambda m: (0, 0, 0)), pl.BlockSpec((D_MODEL, PLE_DIM), fixed)],
        out_specs=pl.BlockSpec((tm, D_MODEL), row),
        out_shape=jax.ShapeDtypeStruct((SEQ, D_MODEL), F32),
        compiler_params=_cparams(("parallel",)),
    )(*_in_hbm(h, g, p, wgate, wprojt))


def _ple_bwd(h, g, p, dh, wgate, wprojt):
    tm = TOK_TILE

    def body(h_ref, g_ref, p_ref, dh_ref, wg_ref, wp_ref, o_ref, u_ref, dq_ref, dpp_ref, pb_ref, dg_ref):
        xh, r = _rms_stats(h_ref[...])
        u = (xh * g_ref[...]).astype(BF16)
        s = _sigmoid(_dot(u, wg_ref[...]))
        p_bf = p_ref[...].astype(BF16)
        pp = _dot_nt(p_bf, wp_ref[...])
        dhv = dh_ref[...]
        dq = (dhv * pp * s * (1.0 - s)).astype(BF16)
        u_ref[0] = u
        dq_ref[0] = dq
        dpp_ref[0] = (dhv * s).astype(BF16)
        pb_ref[0] = p_bf
        dx, dg = _rms_bwd(_dot_nt(dq, wg_ref[...]), xh, r, g_ref[...])
        o_ref[...] = dhv + dx
        _accumulate(dg_ref, pl.program_id(0) == 0, dg)

    row = lambda m: (m, 0)
    fixed = lambda m: (0, 0)
    lead = lambda m: (0, m, 0)
    big = pltpu.HBM((1, SEQ, D_MODEL), BF16)
    return _out_hbm(pl.pallas_call(
        body, name="ple_bwd",
        grid=(SEQ // tm,),
        in_specs=[pl.BlockSpec((tm, D_MODEL), row), pl.BlockSpec((1, D_MODEL), fixed), pl.BlockSpec((tm, PLE_DIM), row),
                  pl.BlockSpec((tm, D_MODEL), row),
                  pl.BlockSpec((None, D_MODEL, D_MODEL), lambda m: (0, 0, 0)), pl.BlockSpec((D_MODEL, PLE_DIM), fixed)],
        out_specs=[pl.BlockSpec((tm, D_MODEL), row),
                   pl.BlockSpec((1, tm, D_MODEL), lead), pl.BlockSpec((1, tm, D_MODEL), lead),
                   pl.BlockSpec((1, tm, D_MODEL), lead), pl.BlockSpec((1, tm, PLE_DIM), lead),
                   pl.BlockSpec((1, D_MODEL), fixed)],
        out_shape=[jax.ShapeDtypeStruct((SEQ, D_MODEL), F32), big, big, big,
                   pltpu.HBM((1, SEQ, PLE_DIM), BF16),
                   jax.ShapeDtypeStruct((1, D_MODEL), F32)],
        compiler_params=_cparams(("arbitrary",)),
    )(*_in_hbm(h, g, p, dh, wgate, wprojt)), (1, 2, 3, 4))


def _final_loss(h, g, target):
    tm = TOK_TILE

    def body(h_ref, g_ref, t_ref, loss_ref, dh_ref, dg_ref):
        first = pl.program_id(0) == 0
        xh, r = _rms_stats(h_ref[...])
        diff = xh * g_ref[...] - t_ref[...]
        part = 0.5 * jnp.sum(jnp.mean(diff * diff, axis=-1, keepdims=True), axis=0, keepdims=True)
        _accumulate(loss_ref, first, jnp.broadcast_to(part, (SUBLANES, LANES)))
        dx, dg = _rms_bwd(diff * (1.0 / D_MODEL), xh, r, g_ref[...])
        dh_ref[...] = dx
        _accumulate(dg_ref, first, dg)

    row = lambda m: (m, 0)
    fixed = lambda m: (0, 0)
    return pl.pallas_call(
        body, name="final_loss",
        grid=(SEQ // tm,),
        in_specs=[pl.BlockSpec((tm, D_MODEL), row), pl.BlockSpec((1, D_MODEL), fixed),
                  pl.BlockSpec((tm, D_MODEL), row)],
        out_specs=[pl.BlockSpec((SUBLANES, LANES), fixed),
                   pl.BlockSpec((tm, D_MODEL), row),
                   pl.BlockSpec((1, D_MODEL), fixed)],
        out_shape=[jax.ShapeDtypeStruct((SUBLANES, LANES), F32),
                   jax.ShapeDtypeStruct((SEQ, D_MODEL), F32),
                   jax.ShapeDtypeStruct((1, D_MODEL), F32)],
        compiler_params=_cparams(("arbitrary",)),
    )(*_in_hbm(h, g, target))


def _disc(ar, ai, ldt):
    dt = jnp.exp(ldt)
    mag = jnp.exp(ar * dt)
    ph = ai * dt
    lr, li = mag * jnp.cos(ph), mag * jnp.sin(ph)
    nr, ni = lr - 1.0, li
    den = ar * ar + ai * ai
    return lr, li, (nr * ar + ni * ai) / den, (ni * ar - nr * ai) / den


def _s5_disc(a_row, ldt_row, a_rep, ldt_rep, bt, ct, tile_e, mask):
    n = N_STATE

    def body(ar_ref, lr_ref, ap_ref, lp_ref, b_ref, c_ref, e_ref, m_ref, lt_ref, ltr_ref, bb_ref, cc_ref):
        lr, li, _, _ = _disc(ar_ref[0], ar_ref[1], lr_ref[...])
        pr, pi = lr, li
        rid = _row_ids(SUBLANES, n)
        for k in range(1, 9):
            for ref, sgn, edge in ((lt_ref, 1.0, 24 + k - 1), (ltr_ref, -1.0, 24 + 8 - k)):
                if k in (1, 2, 4):
                    r0 = {1: 0, 2: 8, 4: 16}[k]
                    keep = (rid >= k) if ref is lt_ref else (rid < SUBLANES - k)
                    ref[r0:r0 + 8, 0:n] = jnp.where(keep, jnp.broadcast_to(pr, (8, n)), 0.0)
                    ref[r0:r0 + 8, n:2 * n] = jnp.where(keep, jnp.broadcast_to(sgn * pi, (8, n)), 0.0)
                ref[edge:edge + 1, 0:n] = pr
                ref[edge:edge + 1, n:2 * n] = sgn * pi
            pr, pi = pr * lr - pi * li, pr * li + pi * lr
        _, _, fr, fi = _disc(ap_ref[0], ap_ref[1], lp_ref[...])
        br, bi = b_ref[0], b_ref[1]
        e = e_ref[...]
        m = m_ref[...].astype(F32)
        bb_ref[:, 0:n] = (_dot((fr * br - fi * bi).astype(BF16), e) * m).astype(BF16)
        bb_ref[:, n:2 * n] = (_dot((fr * bi + fi * br).astype(BF16), e) * m).astype(BF16)
        cc_ref[:, 0:n] = (_dot(c_ref[0].astype(BF16), e) * m).astype(BF16)
        cc_ref[:, n:2 * n] = (-(_dot(c_ref[1].astype(BF16), e) * m)).astype(BF16)

    return pl.pallas_call(
        body, name="s5_disc",
        out_shape=[jax.ShapeDtypeStruct((32, 2 * n), F32), jax.ShapeDtypeStruct((32, 2 * n), F32),
                   jax.ShapeDtypeStruct((SSM_W, 2 * n), BF16), jax.ShapeDtypeStruct((SSM_W, 2 * n), BF16)],
        compiler_params=_cparams(None),
    )(a_row, ldt_row, a_rep, ldt_rep, bt, ct, tile_e, mask)


def _dot_exact(x, sel):
    hi = x.astype(BF16)
    r1 = x - hi.astype(F32)
    mid = r1.astype(BF16)
    lo = (r1 - mid.astype(F32)).astype(BF16)
    return _dot(hi, sel) + _dot(mid, sel) + _dot(lo, sel)


def _s5_disc_bwd(a, ldt, a_rep, ldt_rep, bt, mask, dl, d_bb, d_cc, fold):
    n = N_STATE

    def body(a_ref, l_ref, ap_ref, lp_ref, b_ref, m_ref, dl_ref, dbb_ref, dcc_ref, f_ref,
             da_ref, dldt_ref, db_ref, dc_ref):
        m = m_ref[...].astype(F32)
        fold_m = f_ref[...]
        diag = lambda x: _dot_exact(x * m, fold_m)
        dr, di = diag(dbb_ref[:, 0:n]), diag(dbb_ref[:, n:2 * n])
        dc_ref[0] = diag(dcc_ref[:, 0:n])
        dc_ref[1] = -diag(dcc_ref[:, n:2 * n])
        _, _, fr, fi = _disc(ap_ref[0], ap_ref[1], lp_ref[...])
        br, bi = b_ref[0], b_ref[1]
        db_ref[0] = fr * dr + fi * di
        db_ref[1] = fr * di - fi * dr
        per_state = lambda x: x.reshape(SSM_GROUPS, SSM_GROUP, SSM_STATE).sum(axis=1)
        dfr = per_state(dr * br + di * bi)
        dfi = per_state(di * br - dr * bi)
        _, vjp = jax.vjp(_disc, a_ref[0], a_ref[1], l_ref[...])
        dar, dai, dldt = vjp((dl_ref[0], dl_ref[1], dfr, dfi))
        da_ref[0] = dar
        da_ref[1] = dai
        dldt_ref[...] = jnp.sum(dldt, axis=1, keepdims=True)

    return pl.pallas_call(
        body, name="s5_disc_bwd",
        out_shape=[jax.ShapeDtypeStruct((2, SSM_GROUPS, SSM_STATE), F32),
                   jax.ShapeDtypeStruct((SSM_GROUPS, 1), F32),
                   jax.ShapeDtypeStruct((2, SSM_W, SSM_STATE), F32),
                   jax.ShapeDtypeStruct((2, SSM_W, SSM_STATE), F32)],
        compiler_params=_cparams(None),
    )(a, ldt, a_rep, ldt_rep, bt, mask, dl, d_bb, d_cc, fold)


def _row_block(rows, cap=512):
    for bm in range(min(cap, rows), 0, -1):
        if rows % bm == 0 and (bm % 8 == 0 or bm == rows):
            return bm
    return rows


def _pair_sum(fulls, got, segs):
    ns = len(segs)
    offs = _seg_offsets(segs)
    _, rtot, c = got.shape
    parts = 2
    pr = rtot // parts
    assert pr * parts == rtot and pr % 16 == 0
    pieces = [[] for _ in range(parts)]
    for a, (n, r) in enumerate(segs):
        for m in range(n):
            lo = offs[a] + m * r
            for h in range(parts):
                clo, chi = max(lo, h * pr), min(lo + r, (h + 1) * pr)
                if chi > clo:
                    pieces[h].append((a, m, clo - lo, clo - h * pr, chi - clo))
    n_sems = max(len(ps) for ps in pieces)

    def body(*refs):
        srcs = refs[:ns]
        got_ref, p32_ref, pbf_ref, own_v, sems = refs[ns:]
        h = pl.program_id(0)
        k = pl.program_id(1)
        dev = 2 * k + lax.axis_index("c")
        for hh in range(parts):
            @pl.when(h == hh)
            def _(hh=hh):
                cps = []
                for i, (a, m, so, do, rows) in enumerate(pieces[hh]):
                    start = pl.multiple_of(dev * segs[a][1] + so, 16)
                    cps.append(pltpu.make_async_copy(srcs[a].at[m, pl.ds(start, rows), :],
                                                     own_v.at[pl.ds(do, rows), :], sems.at[i]))
                for cp in cps:
                    cp.start()
                for cp in cps:
                    cp.wait()
        s = own_v[...].astype(F32) + got_ref[0].astype(F32)
        pbf_ref[0] = s.astype(BF16)

        @pl.when(k == 2 * lax.axis_index("x") + lax.axis_index("y"))
        def _():
            p32_ref[...] = s

    spec = pl.BlockSpec((1, pr, c), lambda h, k: (k, h, 0))
    return pl.pallas_call(
        body, name="pair_sum",
        grid=(parts, 4),
        in_specs=[HBM] * ns + [spec], out_specs=[pl.BlockSpec((pr, c), lambda h, k: (h, 0)), spec],
        out_shape=[pltpu.HBM((rtot, c), F32), pltpu.HBM(got.shape, BF16)],
        scratch_shapes=[pltpu.VMEM((pr, c), BF16), pltpu.SemaphoreType.DMA((n_sems,))],
        compiler_params=_cparams(("arbitrary", "arbitrary")),
    )(*_in_hbm(*fulls, got))


def _chip_sum(own, rb):
    r, c = own.shape
    bm = _row_block(r)

    def body(o_ref, r_ref, s_ref):
        s_ref[...] = ((o_ref[...] + r_ref[0].astype(F32)) + r_ref[1].astype(F32)) + r_ref[2].astype(F32)

    return pl.pallas_call(
        body, name="chip_sum",
        grid=(r // bm,),
        in_specs=[pl.BlockSpec((bm, c), lambda k: (k, 0)), pl.BlockSpec((3, bm, c), lambda k: (0, k, 0))],
        out_specs=pl.BlockSpec((bm, c), lambda k: (k, 0)),
        out_shape=jax.ShapeDtypeStruct((r, c), F32),
        compiler_params=_cparams(("parallel",)),
    )(*_in_hbm(own, rb))


def _adamw(w, g, m, v):
    r, c = w.shape
    bm = _row_block(r)
    bc1 = 1.0 - ADAM_B1 ** ADAM_STEP
    bc2 = 1.0 - ADAM_B2 ** ADAM_STEP

    def body(w_ref, g_ref, m_ref, v_ref, d_ref, nm_ref, nv_ref):
        gv = g_ref[...]
        nm = ADAM_B1 * m_ref[...] + (1.0 - ADAM_B1) * gv
        nv = ADAM_B2 * v_ref[...] + (1.0 - ADAM_B2) * (gv * gv)
        nm_ref[...] = nm
        nv_ref[...] = nv
        d_ref[...] = -ADAM_LR * ((nm / bc1) / (jnp.sqrt(nv / bc2) + ADAM_EPS) + ADAM_WD * w_ref[...])

    spec = pl.BlockSpec((bm, c), lambda k: (k, 0))
    shp = jax.ShapeDtypeStruct((r, c), F32)
    return pl.pallas_call(
        body, name="adamw",
        grid=(r // bm,),
        in_specs=[spec] * 4, out_specs=[spec] * 3, out_shape=[shp] * 3,
        compiler_params=_cparams(("parallel",)),
    )(*_in_hbm(w, g, m, v))


def _adamw_layers(w, m, v, g, first, prev):
    depth, r, c = w.shape
    nl = g.shape[0]
    bm = _row_block(r)
    bc1 = 1.0 - ADAM_B1 ** ADAM_STEP
    bc2 = 1.0 - ADAM_B2 ** ADAM_STEP

    def body(w_ref, m_ref, v_ref, g_ref, *refs):
        go_ref, d_ref, nm_ref, nv_ref = refs[-4:]
        gv = g_ref[...]
        nm = ADAM_B1 * m_ref[...] + (1.0 - ADAM_B1) * gv
        nv = ADAM_B2 * v_ref[...] + (1.0 - ADAM_B2) * (gv * gv)
        go_ref[...] = gv
        nm_ref[...] = nm
        nv_ref[...] = nv
        d_ref[...] = -ADAM_LR * ((nm / bc1) / (jnp.sqrt(nv / bc2) + ADAM_EPS) + ADAM_WD * w_ref[...])

    at = pl.BlockSpec((1, bm, c), lambda i, k: (first + i, k, 0))
    shp = jax.ShapeDtypeStruct((depth, r, c), F32)
    old = [] if prev is None else list(prev)
    return pl.pallas_call(
        body, name="adamw_layers",
        grid=(nl, r // bm),
        in_specs=[at, at, at, pl.BlockSpec((1, bm, c), lambda i, k: (i, k, 0))] + [HBM] * len(old),
        out_specs=[at] * 4, out_shape=[shp] * 4,
        input_output_aliases={4 + i: i for i in range(len(old))},
        compiler_params=_cparams(("parallel", "parallel")),
    )(*_in_hbm(w, m, v, g), *old)


def _mesh_pos():
    return lax.axis_index("x"), lax.axis_index("y"), lax.axis_index("c")


def _dev_index(p):
    return 4 * p[0] + 2 * p[1] + p[2]


def _seg_offsets(segs):
    offs, o = [], 0
    for n, r in segs:
        offs.append(o)
        o += n * r
    return offs


def _remote(src, dst, send_sem, recv_sem, to):
    return pltpu.make_async_remote_copy(src_ref=src, dst_ref=dst, send_sem=send_sem, recv_sem=recv_sem,
                                        device_id=to, device_id_type=MESH)


def _allgather(pack, segs, name):
    rtot, c = pack.shape
    ns = len(segs)
    offs = _seg_offsets(segs)
    assert rtot == sum(n * r for n, r in segs)

    def body(pack_ref, *refs):
        outs = refs[:ns]
        send_sems, recv_sems, local_sem = refs[ns:]
        x, y, cc = _mesh_pos()
        me, sib = (x, y, cc), (x, y, 1 - cc)
        chips = [(1 - x, y), (x, 1 - y), (1 - x, 1 - y)]

        def pieces(dev, from_pack):
            res = []
            for a, (n, r) in enumerate(segs):
                for m in range(n):
                    dst = outs[a].at[m, pl.ds(pl.multiple_of(dev * r, r), r), :]
                    src = pack_ref.at[pl.ds(offs[a] + m * r, r), :] if from_pack else dst
                    res.append((src, dst))
            return res

        def push(k, dev, to, from_pack):
            for s, d in pieces(dev, from_pack):
                _remote(s, d, send_sems.at[k], recv_sems.at[k], to).start()

        def whole(k):
            return _remote(pack_ref, pack_ref, send_sems.at[k], recv_sems.at[k], me)

        my_dev = _dev_index(me)
        for s, d in pieces(my_dev, True):
            pltpu.make_async_copy(s, d, local_sem).start()
        push(0, my_dev, sib, True)
        for j, chip in enumerate(chips):
            push(1 + j, my_dev, (*chip, cc), True)
        for j, chip in enumerate(chips):
            whole(1 + j).wait_recv()
            push(4 + j, _dev_index((*chip, cc)), sib, False)
        whole(0).wait_recv()
        for j in range(3):
            whole(4 + j).wait_recv()
        for k in range(7):
            whole(k).wait_send()
        pltpu.make_async_copy(pack_ref, pack_ref, local_sem).wait()

    return pl.pallas_call(
        body, name=name,
        in_specs=[HBM], out_specs=[HBM] * ns,
        out_shape=[jax.ShapeDtypeStruct((n, N_DEV * r, c), pack.dtype) for n, r in segs],
        scratch_shapes=[pltpu.SemaphoreType.DMA((7,)), pltpu.SemaphoreType.DMA((7,)), pltpu.SemaphoreType.DMA],
    )(pack)


HBM = pl.BlockSpec(memory_space=pltpu.HBM)
SEM = pl.BlockSpec(memory_space=pltpu.SEMAPHORE)
VMEM_WHOLE = pl.BlockSpec(memory_space=pltpu.VMEM)
EFFECT = pltpu.SideEffectType.DATAFLOW_SIDE_EFFECTING


def _hbm(a):
    return pltpu.with_memory_space_constraint(a, pltpu.HBM)


def _ag_start(pack, segs, after, name):
    rtot, c = pack.shape
    ns = len(segs)
    offs = _seg_offsets(segs)

    def body(pack_ref, *refs):
        lands = refs[:ns]
        send_sems, recv_sems = refs[ns + 1], refs[ns + 2]
        token = refs[-1]
        x, y, cc = _mesh_pos()
        my_dev = _dev_index((x, y, cc))
        targets = [(x, y, 1 - cc), (1 - x, y, cc), (x, 1 - y, cc), (1 - x, 1 - y, cc)]
        for k, to in enumerate(targets):
            for a, (n, r) in enumerate(segs):
                for m in range(n):
                    _remote(pack_ref.at[pl.ds(offs[a] + m * r, r), :],
                            lands[a].at[m, pl.ds(pl.multiple_of(my_dev * r, r), r), :],
                            send_sems.at[k], recv_sems.at[k], to).start()
        token[...] = jnp.zeros_like(token)

    land_shapes = [(n, N_DEV * r, c) for n, r in segs]
    outs = pl.pallas_call(
        body, name=name,
        in_specs=[HBM] * (1 + ns) + [UNREAD],
        out_specs=[SEM, SEM, HBM] + [HBM] * ns + [VMEM_WHOLE],
        out_shape=[pltpu.SemaphoreType.DMA((4,)), pltpu.SemaphoreType.DMA((4,)), pltpu.HBM(pack.shape, pack.dtype)]
        + [pltpu.HBM(s, pack.dtype) for s in land_shapes] + [jax.ShapeDtypeStruct((SUBLANES, LANES), F32)],
        input_output_aliases={0: 2, **{1 + i: 3 + i for i in range(ns)}},
        compiler_params=pltpu.CompilerParams(has_side_effects=EFFECT),
    )(_hbm(pack), *[_hbm(lax.empty(s, pack.dtype)) for s in land_shapes], _hbm(after))
    return outs[0], outs[1], outs[2], list(outs[3:3 + ns]), outs[-1]


def _ag_wait(send_sems, recv_sems, pack, lands, after, name):
    ns = len(lands)

    def body(pack_ref, *refs):
        send_ref, recv_ref = refs[ns], refs[ns + 1]
        me = _mesh_pos()
        for k in range(4):
            whole = _remote(pack_ref, pack_ref, send_ref.at[k], recv_ref.at[k], me)
            whole.wait_send()
            whole.wait_recv()

    outs = pl.pallas_call(
        body, name=name,
        in_specs=[HBM] * (1 + ns) + [SEM, SEM, UNREAD],
        out_specs=[HBM] * (1 + ns),
        out_shape=[pltpu.HBM(pack.shape, pack.dtype)] + [pltpu.HBM(a.shape, a.dtype) for a in lands],
        input_output_aliases={i: i for i in range(1 + ns)},
        compiler_params=pltpu.CompilerParams(has_side_effects=EFFECT),
    )(pack, *lands, send_sems, recv_sems, _hbm(after))
    return outs[0], list(outs[1:])


def _ag_finish(pack, lands, segs):
    rtot, c = pack.shape
    ns = len(segs)
    offs = _seg_offsets(segs)

    def body(pack_ref, *refs):
        outs = refs[ns:2 * ns]
        stage, send_sems, recv_sems, local_sems = refs[2 * ns:]
        x, y, cc = _mesh_pos()
        me, sib = (x, y, cc), (x, y, 1 - cc)
        chips = [(1 - x, y), (x, 1 - y), (1 - x, 1 - y)]

        def rows(a, m, dev):
            return outs[a].at[m, pl.ds(pl.multiple_of(dev * segs[a][1], segs[a][1]), segs[a][1]), :]

        for j, chip in enumerate(chips):
            dev = _dev_index((*chip, cc))
            for a, (n, r) in enumerate(segs):
                for m in range(n):
                    _remote(rows(a, m, dev), rows(a, m, dev), send_sems.at[j], recv_sems.at[j], sib).start()
        load = pltpu.make_async_copy(pack_ref, stage, local_sems.at[0])
        load.start()
        load.wait()
        my_dev = _dev_index(me)
        for a, (n, r) in enumerate(segs):
            for m in range(n):
                pltpu.make_async_copy(stage.at[pl.ds(offs[a] + m * r, r), :], rows(a, m, my_dev), local_sems.at[1]).start()
        pltpu.make_async_copy(stage, pack_ref, local_sems.at[1]).wait()
        for j in range(3):
            _remote(pack_ref, pack_ref, send_sems.at[j], recv_sems.at[j], me).wait()

    outs = pl.pallas_call(
        body, name="ag_finish",
        in_specs=[HBM] * (1 + ns), out_specs=[HBM] * ns,
        out_shape=[pltpu.HBM(a.shape, a.dtype) if r >= 128 else jax.ShapeDtypeStruct(a.shape, a.dtype)
                   for a, (_, r) in zip(lands, segs)],
        input_output_aliases={1 + i: i for i in range(ns)},
        scratch_shapes=[pltpu.VMEM((rtot, c), pack.dtype), pltpu.SemaphoreType.DMA((3,)),
                        pltpu.SemaphoreType.DMA((3,)), pltpu.SemaphoreType.DMA((2,))],
        compiler_params=_cparams(None, 16),
    )(pack, *lands)
    return list(outs)


def _rs_chips_start(pbf, after, name):
    _, rtot, c = pbf.shape

    def body(pbf_ref, land_ref, after_ref, send_sems, recv_sems, pbf_thru, land_thru, token):
        x, y, cc = _mesh_pos()
        for j, (cx, cy) in enumerate([(1 - x, y), (x, 1 - y), (1 - x, 1 - y)]):
            _remote(pbf_ref.at[2 * cx + cy], land_ref.at[j], send_sems.at[j], recv_sems.at[j], (cx, cy, cc)).start()
        token[...] = jnp.zeros_like(token)

    return pl.pallas_call(
        body, name=name,
        in_specs=[HBM, HBM, UNREAD],
        out_specs=[SEM, SEM, HBM, HBM, VMEM_WHOLE],
        out_shape=[pltpu.SemaphoreType.DMA((3,)), pltpu.SemaphoreType.DMA((3,)), pltpu.HBM(pbf.shape, pbf.dtype),
                   pltpu.HBM((3, rtot, c), pbf.dtype), jax.ShapeDtypeStruct((SUBLANES, LANES), F32)],
        input_output_aliases={0: 2, 1: 3},
        compiler_params=pltpu.CompilerParams(has_side_effects=EFFECT),
    )(_hbm(pbf), _hbm(lax.empty((3, rtot, c), pbf.dtype)), _hbm(after))


def _rs_chips_wait(send_sems, recv_sems, pbf, land, after, name):
    def body(pbf_ref, land_ref, send_ref, recv_ref, after_ref, pbf_out, land_out):
        me = _mesh_pos()
        for j in range(3):
            cp = _remote(pbf_ref.at[0], land_ref.at[j], send_ref.at[j], recv_ref.at[j], me)
            cp.wait_send()
            cp.wait_recv()

    return pl.pallas_call(
        body, name=name,
        in_specs=[HBM, HBM, SEM, SEM, UNREAD], out_specs=[HBM, HBM],
        out_shape=[pltpu.HBM(pbf.shape, pbf.dtype), pltpu.HBM(land.shape, land.dtype)],
        input_output_aliases={0: 0, 1: 1},
        compiler_params=pltpu.CompilerParams(has_side_effects=EFFECT),
    )(pbf, land, send_sems, recv_sems, _hbm(after))[1]


def _flips():
    return [(dx, dy, dc) for dx in (0, 1) for dy in (0, 1) for dc in (0, 1) if dx or dy or dc]


def _small_gather_start(flat, name):
    r, c = flat.shape

    def body(flat_ref, land_ref, send_sems, recv_sems, flat_thru, land_thru, token):
        x, y, cc = _mesh_pos()
        mine = land_ref.at[_dev_index((x, y, cc))]
        for k, (dx, dy, dc) in enumerate(_flips()):
            to = (1 - x if dx else x, 1 - y if dy else y, 1 - cc if dc else cc)
            _remote(flat_ref, mine, send_sems.at[k], recv_sems.at[k], to).start()
        token[...] = jnp.zeros_like(token)

    return pl.pallas_call(
        body, name=name,
        in_specs=[HBM, HBM],
        out_specs=[SEM, SEM, HBM, HBM, VMEM_WHOLE],
        out_shape=[pltpu.SemaphoreType.DMA((7,)), pltpu.SemaphoreType.DMA((7,)), pltpu.HBM(flat.shape, flat.dtype),
                   pltpu.HBM((N_DEV, r, c), flat.dtype), jax.ShapeDtypeStruct((SUBLANES, LANES), F32)],
        input_output_aliases={0: 2, 1: 3},
        compiler_params=pltpu.CompilerParams(has_side_effects=EFFECT),
    )(_hbm(flat), _hbm(lax.empty((N_DEV, r, c), flat.dtype)))


def _small_gather_wait(send_sems, recv_sems, flat, land, after, name):
    def body(flat_ref, land_ref, send_ref, recv_ref, after_ref, flat_out, land_out):
        me = _mesh_pos()
        for k in range(N_DEV - 1):
            cp = _remote(flat_ref, land_ref.at[0], send_ref.at[k], recv_ref.at[k], me)
            cp.wait_send()
            cp.wait_recv()

    return pl.pallas_call(
        body, name=name,
        in_specs=[HBM, HBM, SEM, SEM, UNREAD], out_specs=[HBM, HBM],
        out_shape=[pltpu.HBM(flat.shape, flat.dtype), pltpu.HBM(land.shape, land.dtype)],
        input_output_aliases={0: 0, 1: 1},
        compiler_params=pltpu.CompilerParams(has_side_effects=EFFECT),
    )(flat, land, send_sems, recv_sems, _hbm(after))


def _sum_devices(land, own):
    _, r, c = land.shape

    def body(land_ref, own_ref, out_ref):
        me = _dev_index(_mesh_pos())
        total = None
        for d in range(N_DEV):
            other = land_ref[jnp.where(d == me, (d + 1) % N_DEV, d)]
            block = jnp.where(d == me, own_ref[...], other)
            total = block if total is None else total + block
        out_ref[...] = total

    return pl.pallas_call(
        body, name="sum_devices",
        grid=(1,),
        in_specs=[pl.BlockSpec((N_DEV, r, c), lambda i: (0, 0, 0)), pl.BlockSpec((r, c), lambda i: (0, 0))],
        out_specs=pl.BlockSpec((r, c), lambda i: (0, 0)),
        out_shape=jax.ShapeDtypeStruct((r, c), F32),
        compiler_params=_cparams(("arbitrary",)),
    )(land, own)


def _rs_sibling_start(fulls, segs, name):
    ns = len(segs)
    offs = _seg_offsets(segs)
    rtot = sum(n * r for n, r in segs)
    c = fulls[0].shape[-1]
    dt = fulls[0].dtype

    def body(*refs):
        srcs = refs[:ns]
        land_ref, send_sem, recv_sem = refs[ns], refs[ns + 1], refs[ns + 2]
        token = refs[-1]
        x, y, cc = _mesh_pos()
        for k in range(4):
            for a, (n, r) in enumerate(segs):
                for m in range(n):
                    theirs = srcs[a].at[m, pl.ds(pl.multiple_of((2 * k + 1 - cc) * r, r), r), :]
                    _remote(theirs, land_ref.at[k, pl.ds(offs[a] + m * r, r), :], send_sem, recv_sem,
                            (x, y, 1 - cc)).start()
        token[...] = jnp.zeros_like(token)

    outs = pl.pallas_call(
        body, name=name,
        in_specs=[HBM] * (ns + 1),
        out_specs=[SEM, SEM] + [HBM] * (ns + 1) + [VMEM_WHOLE],
        out_shape=[pltpu.SemaphoreType.DMA(()), pltpu.SemaphoreType.DMA(())]
        + [pltpu.HBM(a.shape, a.dtype) for a in fulls] + [pltpu.HBM((4, rtot, c), dt),
                                                           jax.ShapeDtypeStruct((SUBLANES, LANES), F32)],
        input_output_aliases={i: 2 + i for i in range(ns + 1)},
        compiler_params=pltpu.CompilerParams(has_side_effects=EFFECT),
    )(*[_hbm(a) for a in fulls], _hbm(lax.empty((4, rtot, c), dt)))
    return outs[0], outs[1], list(outs[2:2 + ns]), outs[2 + ns], outs[-1]


def _rs_sibling_wait(send_sem, recv_sem, fulls, land, after, name):
    ns = len(fulls)

    def body(*refs):
        land_ref, send_ref, recv_ref = refs[ns], refs[ns + 1], refs[ns + 2]
        whole = _remote(land_ref, land_ref, send_ref, recv_ref, _mesh_pos())
        whole.wait_send()
        whole.wait_recv()

    outs = pl.pallas_call(
        body, name=name,
        in_specs=[HBM] * (ns + 1) + [SEM, SEM, UNREAD], out_specs=[HBM] * (ns + 1),
        out_shape=[pltpu.HBM(a.shape, a.dtype) for a in fulls] + [pltpu.HBM(land.shape, land.dtype)],
        input_output_aliases={i: i for i in range(ns + 1)},
        compiler_params=pltpu.CompilerParams(has_side_effects=EFFECT),
    )(*fulls, land, send_sem, recv_sem, _hbm(after))
    return list(outs[:ns]), outs[ns]


def _tp(w):
    return jnp.swapaxes(w, -1, -2)


def _s5_prepare(a_re, a_im, log_dt, b_re, b_im, c_re, c_im):
    a = jnp.stack([a_re, a_im], axis=1)
    ldt = jnp.broadcast_to(log_dt[:, :, None], (DEPTH, SSM_GROUPS, SSM_STATE))
    a_row = a.reshape(DEPTH, 2, 1, N_STATE)
    ldt_row = ldt.reshape(DEPTH, 1, N_STATE)
    a_rep = jnp.repeat(a, SSM_GROUP, axis=2)
    ldt_rep = jnp.repeat(ldt, SSM_GROUP, axis=1)
    bt = jnp.stack([_tp(b_re), _tp(b_im)], axis=1).reshape(DEPTH, 2, SSM_W, SSM_STATE)
    ct = jnp.stack([c_re, c_im], axis=1).reshape(DEPTH, 2, SSM_W, SSM_STATE)
    tile_e = jnp.tile(jnp.eye(SSM_STATE, dtype=BF16), (1, SSM_GROUPS))
    mask = jnp.repeat(jnp.repeat(jnp.eye(SSM_GROUPS, dtype=BF16), SSM_GROUP, axis=0), SSM_STATE, axis=1)
    out = []
    for l in range(DEPTH):
        tabs = _s5_disc(a_row[l], ldt_row[l], a_rep[l], ldt_rep[l], bt[l], ct[l], tile_e, mask)
        out.append(((a[l], ldt[l], a_rep[l], ldt_rep[l], bt[l], mask), *tabs))
    return out


def _layer_fwd(h, p_l, small, big, arrive=None):
    saved = {'h0': h}
    if arrive is not None:
        arrive(0, h)
    h, saved['gu1'] = _ffn_fwd(h, small['ffn1_norm'], big['ff1'])
    saved['h1'] = h
    if arrive is not None:
        arrive(1, h)
    z = _inproj_fwd(h, small['mix_norm'], big['wint'])
    ya, ys, hs = _s5conv_fwd(z, small['conv_w'], small['conv_b'], small['bbmat'], small['ccmat'], small['dvec'],
                             small['ltab'])
    saved.update(z=z, ya=ya, ys=ys, hs=hs)
    h = _mix_out_fwd(h, ya, ys, big['glu'], small['glu_b'], small['conv_out_norm'], small['ssm_out_norm'], big['wout'])
    saved['h2'] = h
    if arrive is not None:
        arrive(2, h)
    h, saved['gu2'] = _ffn_fwd(h, small['ffn2_norm'], big['ff2'])
    saved['h3'] = h
    h = _ple_fwd(h, small['ple_norm'], p_l, big['plg'], big['plpt'])
    return h, saved


def _ffn_bwd(h_in, g, dh, gu, w3):
    dh_in, dga, ud, dg = _ffn_bwd_act(h_in, g, dh, gu, w3)
    return dh_in, _matmul_tn(dga, ud, FF_BLOCK, BF16, "ffn_wgrad"), dg


def _layer_bwd_top(dh, p_l, small, big, saved):
    gs = {}
    dh, u, dq, dpp, pb, gs['ple_norm'] = _ple_bwd(saved['h3'], small['ple_norm'], p_l, dh, big['plg'], big['plpt'])
    d_plg = _matmul_tn(u, dq, 256, BF16, "ple_gate_wgrad")
    d_plpt = _matmul_tn(dpp, pb, 256, BF16, "ple_proj_wgrad", to_kernel=False)
    dh, d_ff2, gs['ffn2_norm'] = _ffn_bwd(saved['h2'], small['ffn2_norm'], dh, saved['gu2'], big['ff2'])
    return dh, (gs, d_plg, d_plpt, d_ff2)


def _layer_bwd_rest(dh, top, small, big, saved):
    gs, d_plg, d_plpt, d_ff2 = top
    dya, dys, ycat, dhb, zg, dq, part = _mix_out_bwd(dh, saved['ya'], saved['ys'], big['glu'], small['glu_b'],
                                                     small['conv_out_norm'], small['ssm_out_norm'], big['wout'])
    d_wout = _matmul_tn(ycat, dhb, 256, BF16, "w_out_wgrad")
    d_glu = _matmul_tn(zg, dq, 256, BF16, "glu_wgrad", to_kernel=False)
    dz, gadj, us, dyb, dl, dcw = _s5conv_bwd(saved['z'], saved['hs'], dya, dys, small['conv_w'], small['conv_b'],
                                             small['bbmat'], small['ccmat'], small['dvec'], small['ltab_rev'])
    d_bb = _matmul_tn(us, gadj, SSM_W, F32, "s5_b_wgrad", 1024, False)[0]
    d_cc = _matmul_tn(dyb, saved['hs'][None], SSM_W, F32, "s5_c_wgrad", 1024, False)[0]
    dh, u, gs['mix_norm'] = _inproj_bwd(saved['h1'], small['mix_norm'], dh, dz, big['wint'])
    d_wint = _matmul_tn(dz[None], u, 256, BF16, "w_in_wgrad")
    dh, d_ff1, gs['ffn1_norm'] = _ffn_bwd(saved['h0'], small['ffn1_norm'], dh, saved['gu1'], big['ff1'])

    dlb = dl[0].reshape(2, SSM_GROUPS, SSM_STATE)
    fold = jnp.tile(jnp.eye(SSM_STATE, dtype=BF16), (SSM_GROUPS, 1))
    da, dldt, dbt, dct = _s5_disc_bwd(*small['disc_in'], dlb, d_bb, d_cc, fold)
    gs['ssm_A_re'], gs['ssm_A_im'] = da[0], da[1]
    gs['ssm_log_dt'] = dldt[:, 0]
    ghp = (SSM_GROUPS, SSM_GROUP, SSM_STATE)
    gs['ssm_B_re'], gs['ssm_B_im'] = dbt[0].reshape(ghp), dbt[1].reshape(ghp)
    gs['ssm_C_re'], gs['ssm_C_im'] = dct[0].reshape(ghp), dct[1].reshape(ghp)
    gs['conv_w'] = dcw[0:3]
    gs['conv_b'] = dcw[3]
    gs['ssm_D'] = dcw[4].reshape(SSM_GROUPS, SSM_GROUP)
    gs['conv_out_norm'], gs['ssm_out_norm'], gs['glu_b'] = part[0], part[1], part[2]
    for n in ('ple_norm', 'ffn2_norm', 'mix_norm', 'ffn1_norm'):
        gs[n] = gs[n][0]
    fulls = [d_ff1, d_ff2, d_wint, d_wout, d_plg,
             d_plpt.reshape(1, D_MODEL * PLE_DIM // D_MODEL, D_MODEL), d_glu.reshape(1, SSM_W * SSM_W // D_MODEL, D_MODEL)]
    return dh, fulls, gs


VIEW_T = ('ffn1_w_gate', 'ffn1_w_up', 'ffn2_w_gate', 'ffn2_w_up', 'ssm_B_re', 'ssm_B_im')


def _view(name, a):
    return _tp(a) if name in VIEW_T else a


SEG_NAMES = ('ff1', 'ff2', 'wint', 'wout', 'plg', 'plpt', 'glu')
FIRST_LAYER_GROUPS = ((0,), (2, 3, 6), (1, 4, 5))


def _layer_pack(W, l, segments=range(len(SEGS))):
    pieces = {
        0: lambda: [_tp(W['ffn1_w_gate'][l]), _tp(W['ffn1_w_up'][l]), W['ffn1_w_down'][l]],
        1: lambda: [_tp(W['ffn2_w_gate'][l]), _tp(W['ffn2_w_up'][l]), W['ffn2_w_down'][l]],
        2: lambda: [_tp(W['w_in'][l])],
        3: lambda: [W['w_out'][l]],
        4: lambda: [W['ple_w_gate'][l]],
        5: lambda: [_tp(W['ple_w_proj'][l]).reshape(-1, D_MODEL)],
        6: lambda: [W['glu_w'][l].reshape(-1, D_MODEL)],
    }
    return jnp.concatenate([a for s in segments for a in pieces[s]()], axis=0).astype(BF16)


def _as_big(named):
    shape = dict(plpt=(D_MODEL, PLE_DIM), glu=(SSM_W, SSM_W))
    return {n: (a.reshape(shape[n]) if n in shape else a) for n, a in named.items()}


def _pad_rows(flat, mult, width=LANES):
    per = mult * width
    n = flat.shape[0]
    tot = -(-n // per) * per
    return jnp.pad(flat, (0, tot - n)).reshape(tot // width, width)


def _adamw_any(w, g, m, v):
    shp = w.shape
    two = (lambda t: t.reshape(-1, shp[-1]))
    d, nm, nv = _adamw(two(w), two(g), two(m), two(v))
    return d.reshape(shp), nm.reshape(shp), nv.reshape(shp)


def kernel(x, p, ffn1_norm, ffn1_w_gate, ffn1_w_up, ffn1_w_down, mix_norm, w_in, conv_w, conv_b, ssm_A_re, ssm_A_im, ssm_B_re, ssm_B_im, ssm_C_re, ssm_C_im, ssm_D, ssm_log_dt, glu_w, glu_b, conv_out_norm, ssm_out_norm, w_out, ffn2_norm, ffn2_w_gate, ffn2_w_up, ffn2_w_down, ple_norm, ple_w_gate, ple_w_proj, final_norm, loss_target, m_ffn1_norm, m_ffn1_w_gate, m_ffn1_w_up, m_ffn1_w_down, m_mix_norm, m_w_in, m_conv_w, m_conv_b, m_ssm_A_re, m_ssm_A_im, m_ssm_B_re, m_ssm_B_im, m_ssm_C_re, m_ssm_C_im, m_ssm_D, m_ssm_log_dt, m_glu_w, m_glu_b, m_conv_out_norm, m_ssm_out_norm, m_w_out, m_ffn2_norm, m_ffn2_w_gate, m_ffn2_w_up, m_ffn2_w_down, m_ple_norm, m_ple_w_gate, m_ple_w_proj, m_final_norm, v_ffn1_norm, v_ffn1_w_gate, v_ffn1_w_up, v_ffn1_w_down, v_mix_norm, v_w_in, v_conv_w, v_conv_b, v_ssm_A_re, v_ssm_A_im, v_ssm_B_re, v_ssm_B_im, v_ssm_C_re, v_ssm_C_im, v_ssm_D, v_ssm_log_dt, v_glu_w, v_glu_b, v_conv_out_norm, v_ssm_out_norm, v_w_out, v_ffn2_norm, v_ffn2_w_gate, v_ffn2_w_up, v_ffn2_w_down, v_ple_norm, v_ple_w_gate, v_ple_w_proj, v_final_norm):
    given = dict(locals())
    W = {n: given[n] for n in W_NAMES}
    M = {n: given['m_' + n] for n in W_NAMES}
    V = {n: given['v_' + n] for n in W_NAMES}
    Wv, Mv, Vv = [{n: _view(n, d[n]) for n in W_NAMES} for d in (W, M, V)]
    my_dev = _dev_index(_mesh_pos())

    conv_shard = _pad_rows(W['conv_w'].reshape(-1), SUBLANES)
    conv_all = _allgather(conv_shard, ((1, SUBLANES),), "ag_conv_w")[0]
    conv_full = conv_all.reshape(N_DEV, -1)[:, :DEPTH * 3 * (CONV_W // N_DEV)]
    conv_full = conv_full.reshape(N_DEV, DEPTH, 3, CONV_W // N_DEV).transpose(1, 2, 0, 3).reshape(DEPTH, 3, CONV_W)
    first, after = [], conv_all
    for gi, segments in enumerate(FIRST_LAYER_GROUPS):
        first.append(_ag_start(_layer_pack(W, 0, segments), tuple(SEGS[s] for s in segments), after,
                               "ag_start_0%s" % "abc"[gi]))
        after = first[-1][4]
    s5 = _s5_prepare(*[W[n] + after[0, 0] for n in ('ssm_A_re', 'ssm_A_im', 'ssm_log_dt')],
                     *[W[n] for n in ('ssm_B_re', 'ssm_B_im', 'ssm_C_re', 'ssm_C_im')])
    packs = [None] + [_layer_pack(W, l) for l in range(1, DEPTH)]
    prepared = conv_full[0, 0:1, 0:1] + s5[DEPTH - 1][1][0:1, 0:1] + packs[DEPTH - 1][0:1, 0:1].astype(F32)

    smalls, saves, bigs = [], [], []
    h = x[0]

    flight = None

    def gathered(handles, segments, after, name, next_layer=None, gate=None):
        nonlocal flight
        send_sems, recv_sems, pack_thru, lands, _ = handles
        pack_thru, lands = _ag_wait(send_sems, recv_sems, pack_thru, lands, after, "ag_wait_" + name)
        if next_layer is not None:
            flight = _ag_start(packs[next_layer], SEGS, pack_thru, "ag_start_%d" % next_layer)
            gate[0][gate[1]] = gate[0][gate[1]] + flight[4][0:1, 0:1]
        outs = _ag_finish(pack_thru, lands, tuple(SEGS[s] for s in segments))
        return _as_big({SEG_NAMES[s]: a for s, a in zip(segments, outs)})

    for l in range(DEPTH):
        small = {n: W[n][l][None] for n in ('ffn1_norm', 'mix_norm', 'conv_b', 'glu_b', 'conv_out_norm',
                                            'ssm_out_norm', 'ffn2_norm', 'ple_norm')}
        small['conv_w'] = conv_full[l]
        small['dvec'] = W['ssm_D'][l].reshape(1, SSM_W)
        small['disc_in'], small['ltab'], small['ltab_rev'], small['bbmat'], small['ccmat'] = s5[l]
        big = {}
        bigs.append(big)
        if l == 0:
            def arrive(stage, h_now, big=big, small=small):
                big.update(gathered(first[stage], FIRST_LAYER_GROUPS[stage], prepared if stage == 0 else h_now,
                                    "0%s" % "abc"[stage], *((1, (small, 'ffn2_norm')) if stage == 2 else ())))
            h, saved = _layer_fwd(h, p[l, 0], small, big, arrive)
        else:
            nxt = (l + 1, (small, 'ffn1_norm')) if l + 1 < DEPTH else ()
            big.update(gathered(flight, range(len(SEGS)), h, "%d" % l, *nxt))
            h, saved = _layer_fwd(h, p[l, 0], small, big)
        smalls.append(small)
        saves.append(saved)
    loss_tile, dh, d_final = _final_loss(h, W['final_norm'][None], loss_target[0])
    loss = lax.psum(loss_tile[0, 0], ("x", "y", "c"))

    layer_gs = [None] * DEPTH
    shard_grads = [None] * DEPTH
    zero = jnp.zeros((1, 1), F32)
    sib, ici = None, None

    def finish_sibling(after_sib, after_ici):
        nonlocal sib, ici
        up, (send_sem, recv_sem, fulls_thru, land, _) = sib
        fulls_thru, got = _rs_sibling_wait(send_sem, recv_sem, fulls_thru, land, after_sib, "sib_wait_%d" % up)
        own32, pbf = _pair_sum(fulls_thru, got, SEGS)
        done = finish_chips(own32)
        ici = (up, _rs_chips_start(pbf, after_ici if done is None else done, "rs_start_%d" % up), own32)
        sib = None

    def finish_chips(after):
        nonlocal ici
        if ici is None:
            return None
        up, (send_sems, recv_sems, pbf_thru, land, _), own32 = ici
        got3 = _rs_chips_wait(send_sems, recv_sems, pbf_thru, land, after, "rs_wait_%d" % up)
        shard_grads[up] = _chip_sum(own32, got3)
        ici = None
        return shard_grads[up]

    layer_names = [n for n in SMALL_NAMES if n != 'final_norm']
    small_flights = [None] * DEPTH
    for l in reversed(range(DEPTH)):
        small = dict(smalls[l])
        if sib is not None:
            small['ple_norm'] = small['ple_norm'] + sib[1][4][0:1, 0:1] + small_flights[l + 1][4][0:1, 0:1]
        dh, top = _layer_bwd_top(dh, p[l, 0], small, bigs[l], saves[l])
        if sib is not None:
            finish_sibling(dh, dh)
            small['glu_b'] = small['glu_b'] + ici[1][4][0:1, 0:1]
        dh, fulls, layer_gs[l] = _layer_bwd_rest(dh, top, small, bigs[l], saves[l])
        sib = (l, _rs_sibling_start(fulls, SEGS, "sib_start_%d" % l))
        last_slot = d_final[0] if l == DEPTH - 1 else jnp.zeros((D_MODEL,), F32)
        flat = jnp.concatenate([layer_gs[l][n].reshape(-1) for n in layer_names + ['conv_w']] + [last_slot])
        small_flights[l] = _small_gather_start(_pad_rows(flat, SUBLANES, D_MODEL), "small_start_%d" % l)
    grad_x = dh[None]
    finish_sibling(small_flights[0][4], small_flights[0][4])

    reduced = []
    for l in range(DEPTH):
        send_sems, recv_sems, flat_thru, land, _ = small_flights[l]
        flat_thru, land = _small_gather_wait(send_sems, recv_sems, flat_thru, land, ici[1][4], "small_wait_%d" % l)
        reduced.append(_sum_devices(land, flat_thru).reshape(-1))
    G = {}
    o = 0
    for n in layer_names + ['conv_w']:
        size = (W[n].size if n != 'conv_w' else DEPTH * 3 * CONV_W) // DEPTH
        shape = Wv[n].shape if n != 'conv_w' else (DEPTH, 3, CONV_W)
        G[n] = jnp.stack([red[o:o + size] for red in reduced]).reshape(shape)
        o += size
    G['final_norm'] = reduced[DEPTH - 1][o:o + D_MODEL]
    G['conv_w'] = lax.dynamic_slice_in_dim(G['conv_w'], my_dev * (CONV_W // N_DEV), CONV_W // N_DEV, axis=2)

    delta, new_m, new_v = {}, {}, {}
    cat = lambda src: _pad_rows(jnp.concatenate([src[n].reshape(-1) for n in SMALL_NAMES]), SUBLANES, D_MODEL)
    d_s, m_s, v_s = _adamw(cat(Wv), cat(G), cat(Mv), cat(Vv))
    o = 0
    for n in SMALL_NAMES:
        for dst, src in ((delta, d_s), (new_m, m_s), (new_v, v_s)):
            dst[n] = src.reshape(-1)[o:o + W[n].size].reshape(Wv[n].shape)
        o += W[n].size
    delta['conv_w'], new_m['conv_w'], new_v['conv_w'] = _adamw_any(Wv['conv_w'], G['conv_w'], Mv['conv_w'], Vv['conv_w'])

    def unpack(sg):
        nl = sg.shape[0]
        offs = _seg_offsets(SEGS)
        r = SEGS[0][1]
        out = {}
        for a, f in ((0, 'ffn1'), (1, 'ffn2')):
            out[f + '_w_gate'] = sg[:, offs[a]:offs[a] + r]
            out[f + '_w_up'] = sg[:, offs[a] + r:offs[a] + 2 * r]
            out[f + '_w_down'] = sg[:, offs[a] + 2 * r:offs[a] + 3 * r]
        out['w_in'] = _tp(sg[:, offs[2]:offs[2] + SEGS[2][1]])
        out['w_out'] = sg[:, offs[3]:offs[3] + SEGS[3][1]]
        out['ple_w_gate'] = sg[:, offs[4]:offs[4] + SEGS[4][1]]
        out['ple_w_proj'] = _tp(sg[:, offs[5]:offs[5] + SEGS[5][1]].reshape(nl, D_MODEL // N_DEV, PLE_DIM))
        out['glu_w'] = sg[:, offs[6]:offs[6] + SEGS[6][1]].reshape(nl, SSM_W // N_DEV, SSM_W)
        return out

    upper = unpack(jnp.stack(shard_grads[1:]))
    part = {n: _adamw_layers(Wv[n], Mv[n], Vv[n], upper[n], 1, None) for n in upper}
    finish_chips(part['glu_w'][3])
    lower = unpack(shard_grads[0][None])
    for n in lower:
        G[n], delta[n], new_m[n], new_v[n] = _adamw_layers(Wv[n], Mv[n], Vv[n], lower[n], 0, part[n])

    outs = [[_view(n, d[n]) for n in W_NAMES] for d in (G, delta, new_m, new_v)]
    return (loss, grad_x, *outs[0], *outs[1], *outs[2], *outs[3])
```

```python
import math

import jax
import jax.numpy as jnp
from jax import lax
from jax.experimental import pallas as pl
from jax.experimental.pallas import tpu as pltpu

F32 = jnp.float32
BF16 = jnp.bfloat16

N_DEV = 8
DEPTH = 4
SEQ = 2048
D_MODEL = 1024
D_FF = 2816
CONV_W = 512
SSM_W = 512
SSM_GROUPS = 32
SSM_GROUP = 16
SSM_STATE = 64
N_STATE = SSM_GROUPS * SSM_STATE
IN_COLS = 2048
PLE_DIM = 256
EPS = 1e-6

ADAM_LR = 0.001
ADAM_B1 = 0.9
ADAM_B2 = 0.999
ADAM_EPS = 1e-08
ADAM_WD = 0.01
ADAM_STEP = 10

FF_BLOCK = 256
N_FF_BLOCKS = D_FF // FF_BLOCK
TOK_TILE_FFN_FWD = 2048
TOK_TILE_FFN_BWD = 1024
TOK_TILE = 512
CHUNK = 256
N_CHUNKS = SEQ // CHUNK
LANE_GROUP = 512
SUBLANES = 8
LANES = 128
MIB = 1024 * 1024

W_NAMES = ['ffn1_norm', 'ffn1_w_gate', 'ffn1_w_up', 'ffn1_w_down', 'mix_norm', 'w_in', 'conv_w', 'conv_b',
           'ssm_A_re', 'ssm_A_im', 'ssm_B_re', 'ssm_B_im', 'ssm_C_re', 'ssm_C_im', 'ssm_D', 'ssm_log_dt',
           'glu_w', 'glu_b', 'conv_out_norm', 'ssm_out_norm', 'w_out', 'ffn2_norm', 'ffn2_w_gate', 'ffn2_w_up',
           'ffn2_w_down', 'ple_norm', 'ple_w_gate', 'ple_w_proj', 'final_norm']
SMALL_NAMES = ['ffn1_norm', 'mix_norm', 'conv_b', 'ssm_A_re', 'ssm_A_im', 'ssm_B_re', 'ssm_B_im', 'ssm_C_re',
               'ssm_C_im', 'ssm_D', 'ssm_log_dt', 'glu_b', 'conv_out_norm', 'ssm_out_norm', 'ffn2_norm',
               'ple_norm', 'final_norm']

SEGS = ((3, 352), (3, 352), (1, 256), (1, 128), (1, 128), (1, 32), (1, 32))
PACK_ROWS = sum(n * r for n, r in SEGS)

MESH = pl.DeviceIdType.MESH
UNREAD = pl.BlockSpec(memory_space=pltpu.HBM)


def _in_hbm(*arrays):
    return [pltpu.with_memory_space_constraint(a, pltpu.HBM) for a in arrays]


def _out_hbm(outs, which):
    if not isinstance(outs, (list, tuple)):
        return pltpu.with_memory_space_constraint(outs, pltpu.HBM) if which else outs
    return [pltpu.with_memory_space_constraint(a, pltpu.HBM) if i in which else a for i, a in enumerate(outs)]


def _cparams(sem=None, vmem_mib=48, **kw):
    return pltpu.CompilerParams(dimension_semantics=sem, vmem_limit_bytes=vmem_mib * MIB, **kw)


def _dot(a, b):
    return jnp.dot(a, b, preferred_element_type=F32)


def _dot_nt(a, b):
    return lax.dot_general(a, b, (((1,), (1,)), ((), ())), preferred_element_type=F32)


def _dot_tn(a, b):
    return lax.dot_general(a, b, (((0,), (0,)), ((), ())), preferred_element_type=F32)


def _rms_stats(x):
    r = lax.rsqrt(jnp.mean(x * x, axis=-1, keepdims=True) + EPS)
    return x * r, r


def _rms_bwd(dy, xh, r, g):
    dxh = dy * g
    dx = r * (dxh - xh * jnp.mean(dxh * xh, axis=-1, keepdims=True))
    dg = jnp.sum(dy * xh, axis=0, keepdims=True)
    return dx, dg


def _sigmoid(x):
    return 0.5 * jnp.tanh(0.5 * x) + 0.5


_GELU_C = math.sqrt(2.0 / math.pi)


def _gelu(x):
    t = jnp.tanh(_GELU_C * (x + 0.044715 * x * x * x))
    return 0.5 * x * (1.0 + t), t


def _gelu_grad(x, t):
    return 0.5 * (1.0 + t) + 0.5 * x * (1.0 - t * t) * _GELU_C * (1.0 + 3.0 * 0.044715 * x * x)


def _accumulate(ref, first, value):
    @pl.when(first)
    def _():
        ref[...] = value

    @pl.when(jnp.logical_not(first))
    def _():
        ref[...] += value


def _ffn_fwd(h, g, w3):
    tm = TOK_TILE_FFN_FWD
    last = N_FF_BLOCKS - 1

    def body(h_ref, g_ref, wgu_ref, wd_ref, wd_last_ref, out_ref, gu_ref, u_ref, a_ref):
        k = pl.program_id(1)

        @pl.when(k == 0)
        def _():
            x = h_ref[...]
            xh, _ = _rms_stats(x)
            u_ref[...] = (xh * g_ref[...]).astype(BF16)
            out_ref[...] = x
            a_ref[1] = jnp.zeros((tm, FF_BLOCK), BF16)

        out_ref[...] += 0.5 * _dot(a_ref[(k + 1) % 2], wd_ref[0])
        gu = _dot_nt(u_ref[...], wgu_ref[...].reshape(2 * FF_BLOCK, D_MODEL))
        gate, up = gu[:, :FF_BLOCK], gu[:, FF_BLOCK:]
        a_ref[k % 2] = (gate * _sigmoid(gate) * up).astype(BF16)
        gu_ref[0] = gate.astype(BF16)
        gu_ref[1] = up.astype(BF16)

        @pl.when(k == last)
        def _():
            out_ref[...] += 0.5 * _dot(a_ref[last % 2], wd_last_ref[0])

    return _out_hbm(pl.pallas_call(
        body, name="ffn_fwd",
        grid=(SEQ // tm, N_FF_BLOCKS),
        in_specs=[pl.BlockSpec((tm, D_MODEL), lambda m, k: (m, 0), pipeline_mode=pl.Buffered(1)),
                  pl.BlockSpec((1, D_MODEL), lambda m, k: (0, 0)),
                  pl.BlockSpec((2, FF_BLOCK, D_MODEL), lambda m, k: (0, k, 0)),
                  pl.BlockSpec((1, FF_BLOCK, D_MODEL), lambda m, k: (2, jnp.maximum(k - 1, 0), 0)),
                  pl.BlockSpec((1, FF_BLOCK, D_MODEL), lambda m, k: (2, last, 0), pipeline_mode=pl.Buffered(1))],
        out_specs=[pl.BlockSpec((tm, D_MODEL), lambda m, k: (m, 0)),
                   pl.BlockSpec((2, tm, FF_BLOCK), lambda m, k: (0, m, k))],
        out_shape=[jax.ShapeDtypeStruct((SEQ, D_MODEL), F32),
                   pltpu.HBM((2, SEQ, D_FF), BF16)],
        scratch_shapes=[pltpu.VMEM((tm, D_MODEL), BF16), pltpu.VMEM((2, tm, FF_BLOCK), BF16)],
        compiler_params=_cparams(("parallel", "arbitrary"), 56),
    )(*_in_hbm(h, g, w3, w3, w3)), (1,))


def _ffn_bwd_act(h, g, dout, gu, w3):
    tm = TOK_TILE_FFN_BWD
    last = N_FF_BLOCKS - 1

    def body(h_ref, g_ref, d_ref, gu_ref, wd_ref, wgu_ref, wgu_last_ref, dh_ref, dga_ref, ud_ref, dg_ref,
             acc_ref, dgu_ref):
        m = pl.program_id(0)
        k = pl.program_id(1)

        @pl.when(k == 0)
        def _():
            xh, _ = _rms_stats(h_ref[...])
            ud_ref[0] = (xh * g_ref[...]).astype(BF16)
            ud_ref[1] = (0.5 * d_ref[...]).astype(BF16)
            acc_ref[...] = jnp.zeros_like(acc_ref)
            dgu_ref[1] = jnp.zeros((tm, 2 * FF_BLOCK), BF16)

        acc_ref[...] += _dot(dgu_ref[(k + 1) % 2], wgu_ref[...].reshape(2 * FF_BLOCK, D_MODEL))
        gate = gu_ref[0].astype(F32)
        up = gu_ref[1].astype(F32)
        sg = _sigmoid(gate)
        silu = gate * sg
        da = _dot_nt(ud_ref[1], wd_ref[0])
        dgate = (da * up * (sg + silu * (1.0 - sg))).astype(BF16)
        dup = (da * silu).astype(BF16)
        dga_ref[0] = dgate
        dga_ref[1] = dup
        dga_ref[2] = (silu * up).astype(BF16)
        dgu_ref[k % 2, :, 0:FF_BLOCK] = dgate
        dgu_ref[k % 2, :, FF_BLOCK:2 * FF_BLOCK] = dup

        @pl.when(k == last)
        def _():
            du = acc_ref[...] + _dot(dgu_ref[last % 2], wgu_last_ref[...].reshape(2 * FF_BLOCK, D_MODEL))
            xh, r = _rms_stats(h_ref[...])
            dx, dg = _rms_bwd(du, xh, r, g_ref[...])
            dh_ref[...] = d_ref[...] + dx
            _accumulate(dg_ref, m == 0, dg)

    return _out_hbm(pl.pallas_call(
        body, name="ffn_bwd_act",
        grid=(SEQ // tm, N_FF_BLOCKS),
        in_specs=[pl.BlockSpec((tm, D_MODEL), lambda m, k: (m, 0), pipeline_mode=pl.Buffered(1)),
                  pl.BlockSpec((1, D_MODEL), lambda m, k: (0, 0)),
                  pl.BlockSpec((tm, D_MODEL), lambda m, k: (m, 0), pipeline_mode=pl.Buffered(1)),
                  pl.BlockSpec((2, tm, FF_BLOCK), lambda m, k: (0, m, k)),
                  pl.BlockSpec((1, FF_BLOCK, D_MODEL), lambda m, k: (2, k, 0)),
                  pl.BlockSpec((2, FF_BLOCK, D_MODEL), lambda m, k: (0, jnp.maximum(k - 1, 0), 0)),
                  pl.BlockSpec((2, FF_BLOCK, D_MODEL), lambda m, k: (0, last, 0), pipeline_mode=pl.Buffered(1))],
        out_specs=[pl.BlockSpec((tm, D_MODEL), lambda m, k: (m, 0)),
                   pl.BlockSpec((3, tm, FF_BLOCK), lambda m, k: (0, m, k)),
                   pl.BlockSpec((2, tm, D_MODEL), lambda m, k: (0, m, 0)),
                   pl.BlockSpec((1, D_MODEL), lambda m, k: (0, 0))],
        out_shape=[jax.ShapeDtypeStruct((SEQ, D_MODEL), F32),
                   pltpu.HBM((3, SEQ, D_FF), BF16),
                   pltpu.HBM((2, SEQ, D_MODEL), BF16),
                   jax.ShapeDtypeStruct((1, D_MODEL), F32)],
        scratch_shapes=[pltpu.VMEM((tm, D_MODEL), F32), pltpu.VMEM((2, tm, 2 * FF_BLOCK), BF16)],
        compiler_params=_cparams(("arbitrary", "arbitrary"), 56),
    )(*_in_hbm(h, g, dout, gu, w3, w3, w3)), (1, 2))


def _matmul_tn(a, b, bm, out_dtype, name, bn=None, to_kernel=True):
    na, t, m = a.shape
    nb, _, n = b.shape
    bn = n if bn is None else bn

    def body(a_ref, b_ref, o_ref):
        o_ref[0] = _dot_tn(a_ref[0], b_ref[0]).astype(out_dtype)

    return _out_hbm(pl.pallas_call(
        body, name=name,
        grid=(na, m // bm, n // bn),
        in_specs=[pl.BlockSpec((1, t, bm), lambda i, k, j: (i, 0, k)),
                  pl.BlockSpec((1, t, bn), lambda i, k, j: (jnp.maximum(i - (na - nb), 0), 0, j))],
        out_specs=pl.BlockSpec((1, bm, bn), lambda i, k, j: (i, k, j)),
        out_shape=pltpu.HBM((na, m, n), out_dtype) if to_kernel else jax.ShapeDtypeStruct((na, m, n), out_dtype),
        compiler_params=_cparams(("arbitrary", "parallel", "parallel")),
    )(*_in_hbm(a, b)), to_kernel)


def _inproj_fwd(h, g, wint):
    tm = TOK_TILE

    def body(h_ref, g_ref, w_ref, z_ref):
        xh, _ = _rms_stats(h_ref[...])
        z_ref[...] = _dot_nt((xh * g_ref[...]).astype(BF16), w_ref[...])

    return pl.pallas_call(
        body, name="inproj_fwd",
        grid=(SEQ // tm,),
        in_specs=[pl.BlockSpec((tm, D_MODEL), lambda m: (m, 0)),
                  pl.BlockSpec((1, D_MODEL), lambda m: (0, 0)),
                  pl.BlockSpec((None, IN_COLS, D_MODEL), lambda m: (0, 0, 0))],
        out_specs=pl.BlockSpec((tm, IN_COLS), lambda m: (m, 0)),
        out_shape=jax.ShapeDtypeStruct((SEQ, IN_COLS), F32),
        compiler_params=_cparams(("parallel",)),
    )(*_in_hbm(h, g, wint))


def _inproj_bwd(h, g, dh, dz, wint):
    tm = TOK_TILE

    def body(h_ref, g_ref, dh_ref, dz_ref, w_ref, o_ref, u_ref, dg_ref):
        xh, r = _rms_stats(h_ref[...])
        u_ref[0] = (xh * g_ref[...]).astype(BF16)
        dx, dg = _rms_bwd(_dot(dz_ref[...], w_ref[...]), xh, r, g_ref[...])
        o_ref[...] = dh_ref[...] + dx
        _accumulate(dg_ref, pl.program_id(0) == 0, dg)

    return _out_hbm(pl.pallas_call(
        body, name="inproj_bwd",
        grid=(SEQ // tm,),
        in_specs=[pl.BlockSpec((tm, D_MODEL), lambda m: (m, 0)),
                  pl.BlockSpec((1, D_MODEL), lambda m: (0, 0)),
                  pl.BlockSpec((tm, D_MODEL), lambda m: (m, 0)),
                  pl.BlockSpec((tm, IN_COLS), lambda m: (m, 0)),
                  pl.BlockSpec((None, IN_COLS, D_MODEL), lambda m: (0, 0, 0))],
        out_specs=[pl.BlockSpec((tm, D_MODEL), lambda m: (m, 0)),
                   pl.BlockSpec((1, tm, D_MODEL), lambda m: (0, m, 0)),
                   pl.BlockSpec((1, D_MODEL), lambda m: (0, 0))],
        out_shape=[jax.ShapeDtypeStruct((SEQ, D_MODEL), F32),
                   pltpu.HBM((1, SEQ, D_MODEL), BF16),
                   jax.ShapeDtypeStruct((1, D_MODEL), F32)],
        compiler_params=_cparams(("arbitrary",)),
    )(*_in_hbm(h, g, dh, dz, wint)), (1,))


def _row_ids(n, w):
    return lax.broadcasted_iota(jnp.int32, (n, w), 0)


def _bcast_row(x, i, n):
    return jnp.broadcast_to(x[i:i + 1, :], (n, x.shape[1]))


def _conv_taps(v, tail):
    n, w = v.shape
    rid = _row_ids(n, w)
    v1 = jnp.where(rid == 0, _bcast_row(tail, 7, n), pltpu.roll(v, 1, 0))
    v2 = jnp.where(rid == 0, _bcast_row(tail, 6, n),
                   jnp.where(rid == 1, _bcast_row(tail, 7, n), pltpu.roll(v, 2, 0)))
    return v1, v2


def _scan_chunk(work, ltab, carry, reverse):
    nblk = CHUNK // SUBLANES
    for gi in range(N_STATE // LANE_GROUP):
        cre = pl.ds(gi * LANE_GROUP, LANE_GROUP)
        cim = pl.ds(N_STATE + gi * LANE_GROUP, LANE_GROUP)
        pows = [(ltab[8 * k:8 * k + 8, cre], ltab[8 * k:8 * k + 8, cim]) for k in range(3)]
        pr = ltab[24:32, cre]
        pi = ltab[24:32, cim]

        def blk(i, c, cre=cre, cim=cim, pows=pows, pr=pr, pi=pi):
            cr, ci = c
            b = (nblk - 1 - i) if reverse else i
            r0 = pl.multiple_of(b * SUBLANES, SUBLANES)
            xr = work[pl.ds(r0, SUBLANES), cre]
            xi = work[pl.ds(r0, SUBLANES), cim]
            for k, s in enumerate((1, 2, 4)):
                lr, li = pows[k]
                shift = SUBLANES - s if reverse else s
                sr = pltpu.roll(xr, shift, 0)
                si = pltpu.roll(xi, shift, 0)
                xr, xi = xr + lr * sr - li * si, xi + lr * si + li * sr
            xr, xi = xr + pr * cr - pi * ci, xi + pr * ci + pi * cr
            work[pl.ds(r0, SUBLANES), cre] = xr
            work[pl.ds(r0, SUBLANES), cim] = xi
            edge = 0 if reverse else SUBLANES - 1
            return _bcast_row(xr, edge, SUBLANES), _bcast_row(xi, edge, SUBLANES)

        cr, ci = lax.fori_loop(0, nblk, blk, (carry[:, cre], carry[:, cim]))
        carry[:, cre] = cr
        carry[:, cim] = ci


def _s5conv_fwd(z, convw, convb, bbmat, ccmat, dvec, ltab):
    def body(z_ref, cw_ref, cb_ref, bb_ref, cc_ref, d_ref, lt_ref, ya_ref, ys_ref, hs_ref,
             work, carry, tail):
        c = pl.program_id(0)

        @pl.when(c == 0)
        def _():
            carry[...] = jnp.zeros_like(carry)
            tail[...] = jnp.zeros_like(tail)

        zb = z_ref[:, 0:CONV_W]
        v = z_ref[:, CONV_W:2 * CONV_W] * z_ref[:, 2 * CONV_W:3 * CONV_W]
        us = z_ref[:, 3 * CONV_W:4 * CONV_W]
        v1, v2 = _conv_taps(v, tail[...])
        tail[...] = v[CHUNK - 8:CHUNK, :]
        y = cw_ref[0:1, :] * v2 + cw_ref[1:2, :] * v1 + cw_ref[2:3, :] * v
        ya_ref[...] = zb * (y + cb_ref[...])

        work[...] = _dot(us.astype(BF16), bb_ref[...])
        _scan_chunk(work, lt_ref, carry, reverse=False)
        hs = work[...].astype(BF16)
        hs_ref[...] = hs
        ys_ref[...] = _dot_nt(hs, cc_ref[...]) + d_ref[...] * us

    return _out_hbm(pl.pallas_call(
        body, name="s5conv_fwd",
        grid=(N_CHUNKS,),
        in_specs=[pl.BlockSpec((CHUNK, IN_COLS), lambda c: (c, 0)),
                  pl.BlockSpec((3, CONV_W), lambda c: (0, 0)),
                  pl.BlockSpec((1, CONV_W), lambda c: (0, 0)),
                  pl.BlockSpec((SSM_W, 2 * N_STATE), lambda c: (0, 0)),
                  pl.BlockSpec((SSM_W, 2 * N_STATE), lambda c: (0, 0)),
                  pl.BlockSpec((1, SSM_W), lambda c: (0, 0)),
                  pl.BlockSpec((32, 2 * N_STATE), lambda c: (0, 0))],
        out_specs=[pl.BlockSpec((CHUNK, CONV_W), lambda c: (c, 0)),
                   pl.BlockSpec((CHUNK, SSM_W), lambda c: (c, 0)),
                   pl.BlockSpec((CHUNK, 2 * N_STATE), lambda c: (c, 0))],
        out_shape=[pltpu.HBM((SEQ, CONV_W), F32),
                   pltpu.HBM((SEQ, SSM_W), F32),
                   jax.ShapeDtypeStruct((SEQ, 2 * N_STATE), BF16)],
        scratch_shapes=[pltpu.VMEM((CHUNK, 2 * N_STATE), F32),
                        pltpu.VMEM((8, 2 * N_STATE), F32),
                        pltpu.VMEM((8, CONV_W), F32)],
        compiler_params=_cparams(("arbitrary",)),
    )(*_in_hbm(z, convw, convb, bbmat, ccmat, dvec, ltab)), (0, 1))


def _s5conv_bwd(z, hs, dya, dys, convw, convb, bbmat, ccmat, dvec, ltab_rev):
    nc = N_CHUNKS
    hb = 16

    def body(z_ref, zp_ref, hs_ref, hp_ref, dya_ref, dys_ref, cw_ref, cb_ref, bb_ref, cc_ref, d_ref, lt_ref,
             dz_ref, g_ref, us_ref, dyb_ref, dl_ref, dcw_ref, work, carry, head):
        i = pl.program_id(0)
        first_chunk = i == nc - 1

        @pl.when(i == 0)
        def _():
            carry[...] = jnp.zeros_like(carry)
            head[...] = jnp.zeros_like(head)
            dl_ref[...] = jnp.zeros_like(dl_ref)
            dcw_ref[...] = jnp.zeros_like(dcw_ref)

        us = z_ref[:, 3 * CONV_W:4 * CONV_W]
        dy = dys_ref[...]
        dy_bf = dy.astype(BF16)
        us_ref[0] = us.astype(BF16)
        dyb_ref[0] = dy_bf

        work[...] = _dot(dy_bf, cc_ref[...])
        _scan_chunk(work, lt_ref, carry, reverse=True)
        gg = work[...]
        gg_bf = gg.astype(BF16)
        g_ref[0] = gg_bf
        dus = d_ref[...] * dy + _dot_nt(gg_bf, bb_ref[...])

        hcur = hs_ref[...].astype(F32)
        hlast = hp_ref[...].astype(F32)[hb - 1:hb, :]
        hlast = jnp.where(first_chunk, 0.0, hlast)
        rid = _row_ids(CHUNK, 2 * N_STATE)
        hprev = jnp.where(rid == 0, jnp.broadcast_to(hlast, (CHUNK, 2 * N_STATE)), pltpu.roll(hcur, 1, 0))
        gr, gi = gg[:, :N_STATE], gg[:, N_STATE:]
        hr, hi = hprev[:, :N_STATE], hprev[:, N_STATE:]
        dl_ref[:, :N_STATE] += (gr * hr + gi * hi).reshape(CHUNK // 8, 8, N_STATE).sum(axis=0)
        dl_ref[:, N_STATE:] += (gi * hr - gr * hi).reshape(CHUNK // 8, 8, N_STATE).sum(axis=0)

        @pl.when(i == nc - 1)
        def _():
            dl_ref[0:1, :] = jnp.sum(dl_ref[...], axis=0, keepdims=True)

        zb = z_ref[:, 0:CONV_W]
        zc = z_ref[:, CONV_W:2 * CONV_W]
        zv = z_ref[:, 2 * CONV_W:3 * CONV_W]
        v = zc * zv
        vtail = jnp.where(first_chunk, 0.0, zp_ref[:, CONV_W:2 * CONV_W] * zp_ref[:, 2 * CONV_W:3 * CONV_W])
        v1, v2 = _conv_taps(v, vtail)
        w0, w1, w2 = cw_ref[0:1, :], cw_ref[1:2, :], cw_ref[2:3, :]
        y = w0 * v2 + w1 * v1 + w2 * v
        dya_v = dya_ref[...]
        dzb = dya_v * (y + cb_ref[...])
        dyc = dya_v * zb
        hd = head[...]
        rc = _row_ids(CHUNK, CONV_W)
        n1 = jnp.where(rc == CHUNK - 1, _bcast_row(hd, 0, CHUNK), pltpu.roll(dyc, CHUNK - 1, 0))
        n2 = jnp.where(rc == CHUNK - 1, _bcast_row(hd, 1, CHUNK),
                       jnp.where(rc == CHUNK - 2, _bcast_row(hd, 0, CHUNK), pltpu.roll(dyc, CHUNK - 2, 0)))
        head[...] = dyc[0:8, :]
        dv = w2 * dyc + w1 * n1 + w0 * n2
        dz_ref[:, 0:CONV_W] = dzb.astype(BF16)
        dz_ref[:, CONV_W:2 * CONV_W] = (dv * zv).astype(BF16)
        dz_ref[:, 2 * CONV_W:3 * CONV_W] = (dv * zc).astype(BF16)
        dz_ref[:, 3 * CONV_W:4 * CONV_W] = dus.astype(BF16)
        dcw_ref[0:1, :] += jnp.sum(dyc * v2, axis=0, keepdims=True)
        dcw_ref[1:2, :] += jnp.sum(dyc * v1, axis=0, keepdims=True)
        dcw_ref[2:3, :] += jnp.sum(dyc * v, axis=0, keepdims=True)
        dcw_ref[3:4, :] += jnp.sum(dyc, axis=0, keepdims=True)
        dcw_ref[4:5, :] += jnp.sum(dy * us, axis=0, keepdims=True)

    rev = lambda i: nc - 1 - i
    return _out_hbm(pl.pallas_call(
        body, name="s5conv_bwd",
        grid=(nc,),
        in_specs=[pl.BlockSpec((CHUNK, IN_COLS), lambda i: (rev(i), 0)),
                  pl.BlockSpec((8, IN_COLS), lambda i: (jnp.maximum(rev(i) * (CHUNK // 8) - 1, 0), 0)),
                  pl.BlockSpec((CHUNK, 2 * N_STATE), lambda i: (rev(i), 0)),
                  pl.BlockSpec((hb, 2 * N_STATE), lambda i: (jnp.maximum(rev(i) * (CHUNK // hb) - 1, 0), 0)),
                  pl.BlockSpec((CHUNK, CONV_W), lambda i: (rev(i), 0)),
                  pl.BlockSpec((CHUNK, SSM_W), lambda i: (rev(i), 0)),
                  pl.BlockSpec((3, CONV_W), lambda i: (0, 0)),
                  pl.BlockSpec((1, CONV_W), lambda i: (0, 0)),
                  pl.BlockSpec((SSM_W, 2 * N_STATE), lambda i: (0, 0)),
                  pl.BlockSpec((SSM_W, 2 * N_STATE), lambda i: (0, 0)),
                  pl.BlockSpec((1, SSM_W), lambda i: (0, 0)),
                  pl.BlockSpec((32, 2 * N_STATE), lambda i: (0, 0))],
        out_specs=[pl.BlockSpec((CHUNK, IN_COLS), lambda i: (rev(i), 0)),
                   pl.BlockSpec((1, CHUNK, 2 * N_STATE), lambda i: (0, rev(i), 0)),
                   pl.BlockSpec((1, CHUNK, SSM_W), lambda i: (0, rev(i), 0)),
                   pl.BlockSpec((1, CHUNK, SSM_W), lambda i: (0, rev(i), 0)),
                   pl.BlockSpec((8, 2 * N_STATE), lambda i: (0, 0)),
                   pl.BlockSpec((8, CONV_W), lambda i: (0, 0))],
        out_shape=[jax.ShapeDtypeStruct((SEQ, IN_COLS), BF16),
                   pltpu.HBM((1, SEQ, 2 * N_STATE), BF16),
                   pltpu.HBM((1, SEQ, SSM_W), BF16),
                   pltpu.HBM((1, SEQ, SSM_W), BF16),
                   jax.ShapeDtypeStruct((8, 2 * N_STATE), F32),
                   jax.ShapeDtypeStruct((8, CONV_W), F32)],
        scratch_shapes=[pltpu.VMEM((CHUNK, 2 * N_STATE), F32),
                        pltpu.VMEM((8, 2 * N_STATE), F32),
                        pltpu.VMEM((8, CONV_W), F32)],
        compiler_params=_cparams(("arbitrary",)),
    )(*_in_hbm(z, z, hs, hs, dya, dys, convw, convb, bbmat, ccmat, dvec, ltab_rev)), (1, 2, 3))


def _mix_out_fwd(h, ya, ys, gluw, glub, con, son, wout):
    tm = TOK_TILE

    def body(h_ref, ya_ref, ys_ref, gw_ref, gb_ref, con_ref, son_ref, wo_ref, o_ref):
        zg, _ = _gelu(ys_ref[...])
        q = _dot(zg.astype(BF16), gw_ref[...]) + gb_ref[...]
        out_s = zg * _sigmoid(q)
        na, _ = _rms_stats(ya_ref[...])
        ns, _ = _rms_stats(out_s)
        o_ref[...] = (h_ref[...]
                      + _dot((na * con_ref[...]).astype(BF16), wo_ref[0:CONV_W, :])
                      + _dot((ns * son_ref[...]).astype(BF16), wo_ref[CONV_W:2 * CONV_W, :]))

    row = lambda m: (m, 0)
    fixed = lambda m: (0, 0)
    return pl.pallas_call(
        body, name="mix_out_fwd",
        grid=(SEQ // tm,),
        in_specs=[pl.BlockSpec((tm, D_MODEL), row), pl.BlockSpec((tm, CONV_W), row), pl.BlockSpec((tm, SSM_W), row),
                  pl.BlockSpec((SSM_W, SSM_W), fixed), pl.BlockSpec((1, SSM_W), fixed),
                  pl.BlockSpec((1, CONV_W), fixed), pl.BlockSpec((1, SSM_W), fixed),
                  pl.BlockSpec((None, D_MODEL, D_MODEL), lambda m: (0, 0, 0))],
        out_specs=pl.BlockSpec((tm, D_MODEL), row),
        out_shape=jax.ShapeDtypeStruct((SEQ, D_MODEL), F32),
        compiler_params=_cparams(("parallel",)),
    )(*_in_hbm(h, ya, ys, gluw, glub, con, son, wout))


def _mix_out_bwd(dh, ya, ys, gluw, glub, con, son, wout):
    tm = TOK_TILE

    def body(dh_ref, ya_ref, ys_ref, gw_ref, gb_ref, con_ref, son_ref, wo_ref,
             dya_ref, dys_ref, yc_ref, dhb_ref, zg_ref, dq_ref, part_ref):
        ysv = ys_ref[...]
        zg, th = _gelu(ysv)
        zg_bf = zg.astype(BF16)
        s = _sigmoid(_dot(zg_bf, gw_ref[...]) + gb_ref[...])
        out_s = zg * s
        na, ra = _rms_stats(ya_ref[...])
        ns, rs = _rms_stats(out_s)
        dh_bf = dh_ref[...].astype(BF16)
        yc_ref[0, :, 0:CONV_W] = (na * con_ref[...]).astype(BF16)
        yc_ref[0, :, CONV_W:2 * CONV_W] = (ns * son_ref[...]).astype(BF16)
        dhb_ref[0] = dh_bf
        dca = _dot_nt(dh_bf, wo_ref[0:CONV_W, :])
        dcs = _dot_nt(dh_bf, wo_ref[CONV_W:2 * CONV_W, :])
        dya, dcon = _rms_bwd(dca, na, ra, con_ref[...])
        dos, dson = _rms_bwd(dcs, ns, rs, son_ref[...])
        dya_ref[...] = dya
        dq = dos * zg * s * (1.0 - s)
        dq_bf = dq.astype(BF16)
        dzg = dos * s + _dot_nt(dq_bf, gw_ref[...])
        dys_ref[...] = dzg * _gelu_grad(ysv, th)
        zg_ref[0] = zg_bf
        dq_ref[0] = dq_bf
        rid = _row_ids(SUBLANES, SSM_W)
        part = jnp.zeros((SUBLANES, SSM_W), F32)
        for i, rowv in enumerate((dcon, dson, jnp.sum(dq, axis=0, keepdims=True))):
            part = jnp.where(rid == i, jnp.broadcast_to(rowv, (SUBLANES, SSM_W)), part)
        _accumulate(part_ref, pl.program_id(0) == 0, part)

    row = lambda m: (m, 0)
    fixed = lambda m: (0, 0)
    lead = lambda m: (0, m, 0)
    return _out_hbm(pl.pallas_call(
        body, name="mix_out_bwd",
        grid=(SEQ // tm,),
        in_specs=[pl.BlockSpec((tm, D_MODEL), row), pl.BlockSpec((tm, CONV_W), row), pl.BlockSpec((tm, SSM_W), row),
                  pl.BlockSpec((SSM_W, SSM_W), fixed), pl.BlockSpec((1, SSM_W), fixed),
                  pl.BlockSpec((1, CONV_W), fixed), pl.BlockSpec((1, SSM_W), fixed),
                  pl.BlockSpec((None, D_MODEL, D_MODEL), lambda m: (0, 0, 0))],
        out_specs=[pl.BlockSpec((tm, CONV_W), row), pl.BlockSpec((tm, SSM_W), row),
                   pl.BlockSpec((1, tm, D_MODEL), lead), pl.BlockSpec((1, tm, D_MODEL), lead),
                   pl.BlockSpec((1, tm, SSM_W), lead), pl.BlockSpec((1, tm, SSM_W), lead),
                   pl.BlockSpec((8, SSM_W), fixed)],
        out_shape=[pltpu.HBM((SEQ, CONV_W), F32), pltpu.HBM((SEQ, SSM_W), F32),
                   pltpu.HBM((1, SEQ, D_MODEL), BF16), pltpu.HBM((1, SEQ, D_MODEL), BF16),
                   pltpu.HBM((1, SEQ, SSM_W), BF16), pltpu.HBM((1, SEQ, SSM_W), BF16),
                   jax.ShapeDtypeStruct((8, SSM_W), F32)],
        compiler_params=_cparams(("arbitrary",)),
    )(*_in_hbm(dh, ya, ys, gluw, glub, con, son, wout)), (0, 1, 2, 3, 4, 5))


def _ple_fwd(h, g, p, wgate, wprojt):
    tm = TOK_TILE

    def body(h_ref, g_ref, p_ref, wg_ref, wp_ref, o_ref):
        x = h_ref[...]
        xh, _ = _rms_stats(x)
        s = _sigmoid(_dot((xh * g_ref[...]).astype(BF16), wg_ref[...]))
        o_ref[...] = x + _dot_nt(p_ref[...].astype(BF16), wp_ref[...]) * s

    row = lambda m: (m, 0)
    fixed = lambda m: (0, 0)
    return pl.pallas_call(
        body, name="ple_fwd",
        grid=(SEQ // tm,),
        in_specs=[pl.BlockSpec((tm, D_MODEL), row), pl.BlockSpec((1, D_MODEL), fixed), pl.BlockSpec((tm, PLE_DIM), row),
                  pl.BlockSpec((None, D_MODEL, D_MODEL), lambda m: (0, 0, 0)), pl.BlockSpec((D_MODEL, PLE_DIM), fixed)],
        out_specs=pl.BlockSpec((tm, D_MODEL), row),
        out_shape=jax.ShapeDtypeStruct((SEQ, D_MODEL), F32),
        compiler_params=_cparams(("parallel",)),
    )(*_in_hbm(h, g, p, wgate, wprojt))


def _ple_bwd(h, g, p, dh, wgate, wprojt):
    tm = TOK_TILE

    def body(h_ref, g_ref, p_ref, dh_ref, wg_ref, wp_ref, o_ref, u_ref, dq_ref, dpp_ref, pb_ref, dg_ref):
        xh, r = _rms_stats(h_ref[...])
        u = (xh * g_ref[...]).astype(BF16)
        s = _sigmoid(_dot(u, wg_ref[...]))
        p_bf = p_ref[...].astype(BF16)
        pp = _dot_nt(p_bf, wp_ref[...])
        dhv = dh_ref[...]
        dq = (dhv * pp * s * (1.0 - s)).astype(BF16)
        u_ref[0] = u
        dq_ref[0] = dq
        dpp_ref[0] = (dhv * s).astype(BF16)
        pb_ref[0] = p_bf
        dx, dg = _rms_bwd(_dot_nt(dq, wg_ref[...]), xh, r, g_ref[...])
        o_ref[...] = dhv + dx
        _accumulate(dg_ref, pl.program_id(0) == 0, dg)

    row = lambda m: (m, 0)
    fixed = lambda m: (0, 0)
    lead = lambda m: (0, m, 0)
    big = pltpu.HBM((1, SEQ, D_MODEL), BF16)
    return _out_hbm(pl.pallas_call(
        body, name="ple_bwd",
        grid=(SEQ // tm,),
        in_specs=[pl.BlockSpec((tm, D_MODEL), row), pl.BlockSpec((1, D_MODEL), fixed), pl.BlockSpec((tm, PLE_DIM), row),
                  pl.BlockSpec((tm, D_MODEL), row),
                  pl.BlockSpec((None, D_MODEL, D_MODEL), lambda m: (0, 0, 0)), pl.BlockSpec((D_MODEL, PLE_DIM), fixed)],
        out_specs=[pl.BlockSpec((tm, D_MODEL), row),
                   pl.BlockSpec((1, tm, D_MODEL), lead), pl.BlockSpec((1, tm, D_MODEL), lead),
                   pl.BlockSpec((1, tm, D_MODEL), lead), pl.BlockSpec((1, tm, PLE_DIM), lead),
                   pl.BlockSpec((1, D_MODEL), fixed)],
        out_shape=[jax.ShapeDtypeStruct((SEQ, D_MODEL), F32), big, big, big,
                   pltpu.HBM((1, SEQ, PLE_DIM), BF16),
                   jax.ShapeDtypeStruct((1, D_MODEL), F32)],
        compiler_params=_cparams(("arbitrary",)),
    )(*_in_hbm(h, g, p, dh, wgate, wprojt)), (1, 2, 3, 4))


def _final_loss(h, g, target):
    tm = TOK_TILE

    def body(h_ref, g_ref, t_ref, loss_ref, dh_ref, dg_ref):
        first = pl.program_id(0) == 0
        xh, r = _rms_stats(h_ref[...])
        diff = xh * g_ref[...] - t_ref[...]
        part = 0.5 * jnp.sum(jnp.mean(diff * diff, axis=-1, keepdims=True), axis=0, keepdims=True)
        _accumulate(loss_ref, first, jnp.broadcast_to(part, (SUBLANES, LANES)))
        dx, dg = _rms_bwd(diff * (1.0 / D_MODEL), xh, r, g_ref[...])
        dh_ref[...] = dx
        _accumulate(dg_ref, first, dg)

    row = lambda m: (m, 0)
    fixed = lambda m: (0, 0)
    return pl.pallas_call(
        body, name="final_loss",
        grid=(SEQ // tm,),
        in_specs=[pl.BlockSpec((tm, D_MODEL), row), pl.BlockSpec((1, D_MODEL), fixed),
                  pl.BlockSpec((tm, D_MODEL), row)],
        out_specs=[pl.BlockSpec((SUBLANES, LANES), fixed),
                   pl.BlockSpec((tm, D_MODEL), row),
                   pl.BlockSpec((1, D_MODEL), fixed)],
        out_shape=[jax.ShapeDtypeStruct((SUBLANES, LANES), F32),
                   jax.ShapeDtypeStruct((SEQ, D_MODEL), F32),
                   jax.ShapeDtypeStruct((1, D_MODEL), F32)],
        compiler_params=_cparams(("arbitrary",)),
    )(*_in_hbm(h, g, target))


def _disc(ar, ai, ldt):
    dt = jnp.exp(ldt)
    mag = jnp.exp(ar * dt)
    ph = ai * dt
    lr, li = mag * jnp.cos(ph), mag * jnp.sin(ph)
    nr, ni = lr - 1.0, li
    den = ar * ar + ai * ai
    return lr, li, (nr * ar + ni * ai) / den, (ni * ar - nr * ai) / den


def _s5_disc(a_row, ldt_row, a_rep, ldt_rep, bt, ct, tile_e, mask):
    n = N_STATE

    def body(ar_ref, lr_ref, ap_ref, lp_ref, b_ref, c_ref, e_ref, m_ref, lt_ref, ltr_ref, bb_ref, cc_ref):
        lr, li, _, _ = _disc(ar_ref[0], ar_ref[1], lr_ref[...])
        pr, pi = lr, li
        rid = _row_ids(SUBLANES, n)
        for k in range(1, 9):
            for ref, sgn, edge in ((lt_ref, 1.0, 24 + k - 1), (ltr_ref, -1.0, 24 + 8 - k)):
                if k in (1, 2, 4):
                    r0 = {1: 0, 2: 8, 4: 16}[k]
                    keep = (rid >= k) if ref is lt_ref else (rid < SUBLANES - k)
                    ref[r0:r0 + 8, 0:n] = jnp.where(keep, jnp.broadcast_to(pr, (8, n)), 0.0)
                    ref[r0:r0 + 8, n:2 * n] = jnp.where(keep, jnp.broadcast_to(sgn * pi, (8, n)), 0.0)
                ref[edge:edge + 1, 0:n] = pr
                ref[edge:edge + 1, n:2 * n] = sgn * pi
            pr, pi = pr * lr - pi * li, pr * li + pi * lr
        _, _, fr, fi = _disc(ap_ref[0], ap_ref[1], lp_ref[...])
        br, bi = b_ref[0], b_ref[1]
        e = e_ref[...]
        m = m_ref[...].astype(F32)
        bb_ref[:, 0:n] = (_dot((fr * br - fi * bi).astype(BF16), e) * m).astype(BF16)
        bb_ref[:, n:2 * n] = (_dot((fr * bi + fi * br).astype(BF16), e) * m).astype(BF16)
        cc_ref[:, 0:n] = (_dot(c_ref[0].astype(BF16), e) * m).astype(BF16)
        cc_ref[:, n:2 * n] = (-(_dot(c_ref[1].astype(BF16), e) * m)).astype(BF16)

    return pl.pallas_call(
        body, name="s5_disc",
        out_shape=[jax.ShapeDtypeStruct((32, 2 * n), F32), jax.ShapeDtypeStruct((32, 2 * n), F32),
                   jax.ShapeDtypeStruct((SSM_W, 2 * n), BF16), jax.ShapeDtypeStruct((SSM_W, 2 * n), BF16)],
        compiler_params=_cparams(None),
    )(a_row, ldt_row, a_rep, ldt_rep, bt, ct, tile_e, mask)


def _dot_exact(x, sel):
    hi = x.astype(BF16)
    r1 = x - hi.astype(F32)
    mid = r1.astype(BF16)
    lo = (r1 - mid.astype(F32)).astype(BF16)
    return _dot(hi, sel) + _dot(mid, sel) + _dot(lo, sel)


def _s5_disc_bwd(a, ldt, a_rep, ldt_rep, bt, mask, dl, d_bb, d_cc, fold):
    n = N_STATE

    def body(a_ref, l_ref, ap_ref, lp_ref, b_ref, m_ref, dl_ref, dbb_ref, dcc_ref, f_ref,
             da_ref, dldt_ref, db_ref, dc_ref):
        m = m_ref[...].astype(F32)
        fold_m = f_ref[...]
        diag = lambda x: _dot_exact(x * m, fold_m)
        dr, di = diag(dbb_ref[:, 0:n]), diag(dbb_ref[:, n:2 * n])
        dc_ref[0] = diag(dcc_ref[:, 0:n])
        dc_ref[1] = -diag(dcc_ref[:, n:2 * n])
        _, _, fr, fi = _disc(ap_ref[0], ap_ref[1], lp_ref[...])
        br, bi = b_ref[0], b_ref[1]
        db_ref[0] = fr * dr + fi * di
        db_ref[1] = fr * di - fi * dr
        per_state = lambda x: x.reshape(SSM_GROUPS, SSM_GROUP, SSM_STATE).sum(axis=1)
        dfr = per_state(dr * br + di * bi)
        dfi = per_state(di * br - dr * bi)
        _, vjp = jax.vjp(_disc, a_ref[0], a_ref[1], l_ref[...])
        dar, dai, dldt = vjp((dl_ref[0], dl_ref[1], dfr, dfi))
        da_ref[0] = dar
        da_ref[1] = dai
        dldt_ref[...] = jnp.sum(dldt, axis=1, keepdims=True)

    return pl.pallas_call(
        body, name="s5_disc_bwd",
        out_shape=[jax.ShapeDtypeStruct((2, SSM_GROUPS, SSM_STATE), F32),
                   jax.ShapeDtypeStruct((SSM_GROUPS, 1), F32),
                   jax.ShapeDtypeStruct((2, SSM_W, SSM_STATE), F32),
                   jax.ShapeDtypeStruct((2, SSM_W, SSM_STATE), F32)],
        compiler_params=_cparams(None),
    )(a, ldt, a_rep, ldt_rep, bt, mask, dl, d_bb, d_cc, fold)


def _row_block(rows, cap=512):
    for bm in range(min(cap, rows), 0, -1):
        if rows % bm == 0 and (bm % 8 == 0 or bm == rows):
            return bm
    return rows


def _pair_sum(fulls, got, segs):
    ns = len(segs)
    offs = _seg_offsets(segs)
    _, rtot, c = got.shape
    parts = 2
    pr = rtot // parts
    assert pr * parts == rtot and pr % 16 == 0
    pieces = [[] for _ in range(parts)]
    for a, (n, r) in enumerate(segs):
        for m in range(n):
            lo = offs[a] + m * r
            for h in range(parts):
                clo, chi = max(lo, h * pr), min(lo + r, (h + 1) * pr)
                if chi > clo:
                    pieces[h].append((a, m, clo - lo, clo - h * pr, chi - clo))
    n_sems = max(len(ps) for ps in pieces)

    def body(*refs):
        srcs = refs[:ns]
        got_ref, p32_ref, pbf_ref, own_v, sems = refs[ns:]
        h = pl.program_id(0)
        k = pl.program_id(1)
        dev = 2 * k + lax.axis_index("c")
        for hh in range(parts):
            @pl.when(h == hh)
            def _(hh=hh):
                cps = []
                for i, (a, m, so, do, rows) in enumerate(pieces[hh]):
                    start = pl.multiple_of(dev * segs[a][1] + so, 16)
                    cps.append(pltpu.make_async_copy(srcs[a].at[m, pl.ds(start, rows), :],
                                                     own_v.at[pl.ds(do, rows), :], sems.at[i]))
                for cp in cps:
                    cp.start()
                for cp in cps:
                    cp.wait()
        s = own_v[...].astype(F32) + got_ref[0].astype(F32)
        pbf_ref[0] = s.astype(BF16)

        @pl.when(k == 2 * lax.axis_index("x") + lax.axis_index("y"))
        def _():
            p32_ref[...] = s

    spec = pl.BlockSpec((1, pr, c), lambda h, k: (k, h, 0))
    return pl.pallas_call(
        body, name="pair_sum",
        grid=(parts, 4),
        in_specs=[HBM] * ns + [spec], out_specs=[pl.BlockSpec((pr, c), lambda h, k: (h, 0)), spec],
        out_shape=[pltpu.HBM((rtot, c), F32), pltpu.HBM(got.shape, BF16)],
        scratch_shapes=[pltpu.VMEM((pr, c), BF16), pltpu.SemaphoreType.DMA((n_sems,))],
        compiler_params=_cparams(("arbitrary", "arbitrary")),
    )(*_in_hbm(*fulls, got))


def _chip_sum(own, rb):
    r, c = own.shape
    bm = _row_block(r)

    def body(o_ref, r_ref, s_ref):
        s_ref[...] = ((o_ref[...] + r_ref[0].astype(F32)) + r_ref[1].astype(F32)) + r_ref[2].astype(F32)

    return pl.pallas_call(
        body, name="chip_sum",
        grid=(r // bm,),
        in_specs=[pl.BlockSpec((bm, c), lambda k: (k, 0)), pl.BlockSpec((3, bm, c), lambda k: (0, k, 0))],
        out_specs=pl.BlockSpec((bm, c), lambda k: (k, 0)),
        out_shape=jax.ShapeDtypeStruct((r, c), F32),
        compiler_params=_cparams(("parallel",)),
    )(*_in_hbm(own, rb))


def _adamw(w, g, m, v):
    r, c = w.shape
    bm = _row_block(r)
    bc1 = 1.0 - ADAM_B1 ** ADAM_STEP
    bc2 = 1.0 - ADAM_B2 ** ADAM_STEP

    def body(w_ref, g_ref, m_ref, v_ref, d_ref, nm_ref, nv_ref):
        gv = g_ref[...]
        nm = ADAM_B1 * m_ref[...] + (1.0 - ADAM_B1) * gv
        nv = ADAM_B2 * v_ref[...] + (1.0 - ADAM_B2) * (gv * gv)
        nm_ref[...] = nm
        nv_ref[...] = nv
        d_ref[...] = -ADAM_LR * ((nm / bc1) / (jnp.sqrt(nv / bc2) + ADAM_EPS) + ADAM_WD * w_ref[...])

    spec = pl.BlockSpec((bm, c), lambda k: (k, 0))
    shp = jax.ShapeDtypeStruct((r, c), F32)
    return pl.pallas_call(
        body, name="adamw",
        grid=(r // bm,),
        in_specs=[spec] * 4, out_specs=[spec] * 3, out_shape=[shp] * 3,
        compiler_params=_cparams(("parallel",)),
    )(*_in_hbm(w, g, m, v))


def _adamw_layers(w, m, v, g, first, prev):
    depth, r, c = w.shape
    nl = g.shape[0]
    bm = _row_block(r)
    bc1 = 1.0 - ADAM_B1 ** ADAM_STEP
    bc2 = 1.0 - ADAM_B2 ** ADAM_STEP

    def body(w_ref, m_ref, v_ref, g_ref, *refs):
        go_ref, d_ref, nm_ref, nv_ref = refs[-4:]
        gv = g_ref[...]
        nm = ADAM_B1 * m_ref[...] + (1.0 - ADAM_B1) * gv
        nv = ADAM_B2 * v_ref[...] + (1.0 - ADAM_B2) * (gv * gv)
        go_ref[...] = gv
        nm_ref[...] = nm
        nv_ref[...] = nv
        d_ref[...] = -ADAM_LR * ((nm / bc1) / (jnp.sqrt(nv / bc2) + ADAM_EPS) + ADAM_WD * w_ref[...])

    at = pl.BlockSpec((1, bm, c), lambda i, k: (first + i, k, 0))
    shp = jax.ShapeDtypeStruct((depth, r, c), F32)
    old = [] if prev is None else list(prev)
    return pl.pallas_call(
        body, name="adamw_layers",
        grid=(nl, r // bm),
        in_specs=[at, at, at, pl.BlockSpec((1, bm, c), lambda i, k: (i, k, 0))] + [HBM] * len(old),
        out_specs=[at] * 4, out_shape=[shp] * 4,
        input_output_aliases={4 + i: i for i in range(len(old))},
        compiler_params=_cparams(("parallel", "parallel")),
    )(*_in_hbm(w, m, v, g), *old)


def _mesh_pos():
    return lax.axis_index("x"), lax.axis_index("y"), lax.axis_index("c")


def _dev_index(p):
    return 4 * p[0] + 2 * p[1] + p[2]


def _seg_offsets(segs):
    offs, o = [], 0
    for n, r in segs:
        offs.append(o)
        o += n * r
    return offs


def _remote(src, dst, send_sem, recv_sem, to):
    return pltpu.make_async_remote_copy(src_ref=src, dst_ref=dst, send_sem=send_sem, recv_sem=recv_sem,
                                        device_id=to, device_id_type=MESH)


def _allgather(pack, segs, name):
    rtot, c = pack.shape
    ns = len(segs)
    offs = _seg_offsets(segs)
    assert rtot == sum(n * r for n, r in segs)

    def body(pack_ref, *refs):
        outs = refs[:ns]
        send_sems, recv_sems, local_sem = refs[ns:]
        x, y, cc = _mesh_pos()
        me, sib = (x, y, cc), (x, y, 1 - cc)
        chips = [(1 - x, y), (x, 1 - y), (1 - x, 1 - y)]

        def pieces(dev, from_pack):
            res = []
            for a, (n, r) in enumerate(segs):
                for m in range(n):
                    dst = outs[a].at[m, pl.ds(pl.multiple_of(dev * r, r), r), :]
                    src = pack_ref.at[pl.ds(offs[a] + m * r, r), :] if from_pack else dst
                    res.append((src, dst))
            return res

        def push(k, dev, to, from_pack):
            for s, d in pieces(dev, from_pack):
                _remote(s, d, send_sems.at[k], recv_sems.at[k], to).start()

        def whole(k):
            return _remote(pack_ref, pack_ref, send_sems.at[k], recv_sems.at[k], me)

        my_dev = _dev_index(me)
        for s, d in pieces(my_dev, True):
            pltpu.make_async_copy(s, d, local_sem).start()
        push(0, my_dev, sib, True)
        for j, chip in enumerate(chips):
            push(1 + j, my_dev, (*chip, cc), True)
        for j, chip in enumerate(chips):
            whole(1 + j).wait_recv()
            push(4 + j, _dev_index((*chip, cc)), sib, False)
        whole(0).wait_recv()
        for j in range(3):
            whole(4 + j).wait_recv()
        for k in range(7):
            whole(k).wait_send()
        pltpu.make_async_copy(pack_ref, pack_ref, local_sem).wait()

    return pl.pallas_call(
        body, name=name,
        in_specs=[HBM], out_specs=[HBM] * ns,
        out_shape=[jax.ShapeDtypeStruct((n, N_DEV * r, c), pack.dtype) for n, r in segs],
        scratch_shapes=[pltpu.SemaphoreType.DMA((7,)), pltpu.SemaphoreType.DMA((7,)), pltpu.SemaphoreType.DMA],
    )(pack)


HBM = pl.BlockSpec(memory_space=pltpu.HBM)
SEM = pl.BlockSpec(memory_space=pltpu.SEMAPHORE)
VMEM_WHOLE = pl.BlockSpec(memory_space=pltpu.VMEM)
EFFECT = pltpu.SideEffectType.DATAFLOW_SIDE_EFFECTING


def _hbm(a):
    return pltpu.with_memory_space_constraint(a, pltpu.HBM)


def _ag_start(pack, segs, after, name):
    rtot, c = pack.shape
    ns = len(segs)
    offs = _seg_offsets(segs)

    def body(pack_ref, *refs):
        lands = refs[:ns]
        send_sems, recv_sems = refs[ns + 1], refs[ns + 2]
        token = refs[-1]
        x, y, cc = _mesh_pos()
        my_dev = _dev_index((x, y, cc))
        targets = [(x, y, 1 - cc), (1 - x, y, cc), (x, 1 - y, cc), (1 - x, 1 - y, cc)]
        for k, to in enumerate(targets):
            for a, (n, r) in enumerate(segs):
                for m in range(n):
                    _remote(pack_ref.at[pl.ds(offs[a] + m * r, r), :],
                            lands[a].at[m, pl.ds(pl.multiple_of(my_dev * r, r), r), :],
                            send_sems.at[k], recv_sems.at[k], to).start()
        token[...] = jnp.zeros_like(token)

    land_shapes = [(n, N_DEV * r, c) for n, r in segs]
    outs = pl.pallas_call(
        body, name=name,
        in_specs=[HBM] * (1 + ns) + [UNREAD],
        out_specs=[SEM, SEM, HBM] + [HBM] * ns + [VMEM_WHOLE],
        out_shape=[pltpu.SemaphoreType.DMA((4,)), pltpu.SemaphoreType.DMA((4,)), pltpu.HBM(pack.shape, pack.dtype)]
        + [pltpu.HBM(s, pack.dtype) for s in land_shapes] + [jax.ShapeDtypeStruct((SUBLANES, LANES), F32)],
        input_output_aliases={0: 2, **{1 + i: 3 + i for i in range(ns)}},
        compiler_params=pltpu.CompilerParams(has_side_effects=EFFECT),
    )(_hbm(pack), *[_hbm(lax.empty(s, pack.dtype)) for s in land_shapes], _hbm(after))
    return outs[0], outs[1], outs[2], list(outs[3:3 + ns]), outs[-1]


def _ag_wait(send_sems, recv_sems, pack, lands, after, name):
    ns = len(lands)

    def body(pack_ref, *refs):
        send_ref, recv_ref = refs[ns], refs[ns + 1]
        me = _mesh_pos()
        for k in range(4):
            whole = _remote(pack_ref, pack_ref, send_ref.at[k], recv_ref.at[k], me)
            whole.wait_send()
            whole.wait_recv()

    outs = pl.pallas_call(
        body, name=name,
        in_specs=[HBM] * (1 + ns) + [SEM, SEM, UNREAD],
        out_specs=[HBM] * (1 + ns),
        out_shape=[pltpu.HBM(pack.shape, pack.dtype)] + [pltpu.HBM(a.shape, a.dtype) for a in lands],
        input_output_aliases={i: i for i in range(1 + ns)},
        compiler_params=pltpu.CompilerParams(has_side_effects=EFFECT),
    )(pack, *lands, send_sems, recv_sems, _hbm(after))
    return outs[0], list(outs[1:])


def _ag_finish(pack, lands, segs):
    rtot, c = pack.shape
    ns = len(segs)
    offs = _seg_offsets(segs)

    def body(pack_ref, *refs):
        outs = refs[ns:2 * ns]
        stage, send_sems, recv_sems, local_sems = refs[2 * ns:]
        x, y, cc = _mesh_pos()
        me, sib = (x, y, cc), (x, y, 1 - cc)
        chips = [(1 - x, y), (x, 1 - y), (1 - x, 1 - y)]

        def rows(a, m, dev):
            return outs[a].at[m, pl.ds(pl.multiple_of(dev * segs[a][1], segs[a][1]), segs[a][1]), :]

        for j, chip in enumerate(chips):
            dev = _dev_index((*chip, cc))
            for a, (n, r) in enumerate(segs):
                for m in range(n):
                    _remote(rows(a, m, dev), rows(a, m, dev), send_sems.at[j], recv_sems.at[j], sib).start()
        load = pltpu.make_async_copy(pack_ref, stage, local_sems.at[0])
        load.start()
        load.wait()
        my_dev = _dev_index(me)
        for a, (n, r) in enumerate(segs):
            for m in range(n):
                pltpu.make_async_copy(stage.at[pl.ds(offs[a] + m * r, r), :], rows(a, m, my_dev), local_sems.at[1]).start()
        pltpu.make_async_copy(stage, pack_ref, local_sems.at[1]).wait()
        for j in range(3):
            _remote(pack_ref, pack_ref, send_sems.at[j], recv_sems.at[j], me).wait()

    outs = pl.pallas_call(
        body, name="ag_finish",
        in_specs=[HBM] * (1 + ns), out_specs=[HBM] * ns,
        out_shape=[pltpu.HBM(a.shape, a.dtype) if r >= 128 else jax.ShapeDtypeStruct(a.shape, a.dtype)
                   for a, (_, r) in zip(lands, segs)],
        input_output_aliases={1 + i: i for i in range(ns)},
        scratch_shapes=[pltpu.VMEM((rtot, c), pack.dtype), pltpu.SemaphoreType.DMA((3,)),
                        pltpu.SemaphoreType.DMA((3,)), pltpu.SemaphoreType.DMA((2,))],
        compiler_params=_cparams(None, 16),
    )(pack, *lands)
    return list(outs)


def _rs_chips_start(pbf, after, name):
    _, rtot, c = pbf.shape

    def body(pbf_ref, land_ref, after_ref, send_sems, recv_sems, pbf_thru, land_thru, token):
        x, y, cc = _mesh_pos()
        for j, (cx, cy) in enumerate([(1 - x, y), (x, 1 - y), (1 - x, 1 - y)]):
            _remote(pbf_ref.at[2 * cx + cy], land_ref.at[j], send_sems.at[j], recv_sems.at[j], (cx, cy, cc)).start()
        token[...] = jnp.zeros_like(token)

    return pl.pallas_call(
        body, name=name,
        in_specs=[HBM, HBM, UNREAD],
        out_specs=[SEM, SEM, HBM, HBM, VMEM_WHOLE],
        out_shape=[pltpu.SemaphoreType.DMA((3,)), pltpu.SemaphoreType.DMA((3,)), pltpu.HBM(pbf.shape, pbf.dtype),
                   pltpu.HBM((3, rtot, c), pbf.dtype), jax.ShapeDtypeStruct((SUBLANES, LANES), F32)],
        input_output_aliases={0: 2, 1: 3},
        compiler_params=pltpu.CompilerParams(has_side_effects=EFFECT),
    )(_hbm(pbf), _hbm(lax.empty((3, rtot, c), pbf.dtype)), _hbm(after))


def _rs_chips_wait(send_sems, recv_sems, pbf, land, after, name):
    def body(pbf_ref, land_ref, send_ref, recv_ref, after_ref, pbf_out, land_out):
        me = _mesh_pos()
        for j in range(3):
            cp = _remote(pbf_ref.at[0], land_ref.at[j], send_ref.at[j], recv_ref.at[j], me)
            cp.wait_send()
            cp.wait_recv()

    return pl.pallas_call(
        body, name=name,
        in_specs=[HBM, HBM, SEM, SEM, UNREAD], out_specs=[HBM, HBM],
        out_shape=[pltpu.HBM(pbf.shape, pbf.dtype), pltpu.HBM(land.shape, land.dtype)],
        input_output_aliases={0: 0, 1: 1},
        compiler_params=pltpu.CompilerParams(has_side_effects=EFFECT),
    )(pbf, land, send_sems, recv_sems, _hbm(after))[1]


def _flips():
    return [(dx, dy, dc) for dx in (0, 1) for dy in (0, 1) for dc in (0, 1) if dx or dy or dc]


def _small_gather_start(flat, name):
    r, c = flat.shape

    def body(flat_ref, land_ref, send_sems, recv_sems, flat_thru, land_thru, token):
        x, y, cc = _mesh_pos()
        mine = land_ref.at[_dev_index((x, y, cc))]
        for k, (dx, dy, dc) in enumerate(_flips()):
            to = (1 - x if dx else x, 1 - y if dy else y, 1 - cc if dc else cc)
            _remote(flat_ref, mine, send_sems.at[k], recv_sems.at[k], to).start()
        token[...] = jnp.zeros_like(token)

    return pl.pallas_call(
        body, name=name,
        in_specs=[HBM, HBM],
        out_specs=[SEM, SEM, HBM, HBM, VMEM_WHOLE],
        out_shape=[pltpu.SemaphoreType.DMA((7,)), pltpu.SemaphoreType.DMA((7,)), pltpu.HBM(flat.shape, flat.dtype),
                   pltpu.HBM((N_DEV, r, c), flat.dtype), jax.ShapeDtypeStruct((SUBLANES, LANES), F32)],
        input_output_aliases={0: 2, 1: 3},
        compiler_params=pltpu.CompilerParams(has_side_effects=EFFECT),
    )(_hbm(flat), _hbm(lax.empty((N_DEV, r, c), flat.dtype)))


def _small_gather_wait(send_sems, recv_sems, flat, land, after, name):
    def body(flat_ref, land_ref, send_ref, recv_ref, after_ref, flat_out, land_out):
        me = _mesh_pos()
        for k in range(N_DEV - 1):
            cp = _remote(flat_ref, land_ref.at[0], send_ref.at[k], recv_ref.at[k], me)
            cp.wait_send()
            cp.wait_recv()

    return pl.pallas_call(
        body, name=name,
        in_specs=[HBM, HBM, SEM, SEM, UNREAD], out_specs=[HBM, HBM],
        out_shape=[pltpu.HBM(flat.shape, flat.dtype), pltpu.HBM(land.shape, land.dtype)],
        input_output_aliases={0: 0, 1: 1},
        compiler_params=pltpu.CompilerParams(has_side_effects=EFFECT),
    )(flat, land, send_sems, recv_sems, _hbm(after))


def _sum_devices(land, own):
    _, r, c = land.shape

    def body(land_ref, own_ref, out_ref):
        me = _dev_index(_mesh_pos())
        total = None
        for d in range(N_DEV):
            other = land_ref[jnp.where(d == me, (d + 1) % N_DEV, d)]
            block = jnp.where(d == me, own_ref[...], other)
            total = block if total is None else total + block
        out_ref[...] = total

    return pl.pallas_call(
        body, name="sum_devices",
        grid=(1,),
        in_specs=[pl.BlockSpec((N_DEV, r, c), lambda i: (0, 0, 0)), pl.BlockSpec((r, c), lambda i: (0, 0))],
        out_specs=pl.BlockSpec((r, c), lambda i: (0, 0)),
        out_shape=jax.ShapeDtypeStruct((r, c), F32),
        compiler_params=_cparams(("arbitrary",)),
    )(land, own)


def _rs_sibling_start(fulls, segs, name):
    ns = len(segs)
    offs = _seg_offsets(segs)
    rtot = sum(n * r for n, r in segs)
    c = fulls[0].shape[-1]
    dt = fulls[0].dtype

    def body(*refs):
        srcs = refs[:ns]
        land_ref, send_sem, recv_sem = refs[ns], refs[ns + 1], refs[ns + 2]
        token = refs[-1]
        x, y, cc = _mesh_pos()
        for k in range(4):
            for a, (n, r) in enumerate(segs):
                for m in range(n):
                    theirs = srcs[a].at[m, pl.ds(pl.multiple_of((2 * k + 1 - cc) * r, r), r), :]
                    _remote(theirs, land_ref.at[k, pl.ds(offs[a] + m * r, r), :], send_sem, recv_sem,
                            (x, y, 1 - cc)).start()
        token[...] = jnp.zeros_like(token)

    outs = pl.pallas_call(
        body, name=name,
        in_specs=[HBM] * (ns + 1),
        out_specs=[SEM, SEM] + [HBM] * (ns + 1) + [VMEM_WHOLE],
        out_shape=[pltpu.SemaphoreType.DMA(()), pltpu.SemaphoreType.DMA(())]
        + [pltpu.HBM(a.shape, a.dtype) for a in fulls] + [pltpu.HBM((4, rtot, c), dt),
                                                           jax.ShapeDtypeStruct((SUBLANES, LANES), F32)],
        input_output_aliases={i: 2 + i for i in range(ns + 1)},
        compiler_params=pltpu.CompilerParams(has_side_effects=EFFECT),
    )(*[_hbm(a) for a in fulls], _hbm(lax.empty((4, rtot, c), dt)))
    return outs[0], outs[1], list(outs[2:2 + ns]), outs[2 + ns], outs[-1]


def _rs_sibling_wait(send_sem, recv_sem, fulls, land, after, name):
    ns = len(fulls)

    def body(*refs):
        land_ref, send_ref, recv_ref = refs[ns], refs[ns + 1], refs[ns + 2]
        whole = _remote(land_ref, land_ref, send_ref, recv_ref, _mesh_pos())
        whole.wait_send()
        whole.wait_recv()

    outs = pl.pallas_call(
        body, name=name,
        in_specs=[HBM] * (ns + 1) + [SEM, SEM, UNREAD], out_specs=[HBM] * (ns + 1),
        out_shape=[pltpu.HBM(a.shape, a.dtype) for a in fulls] + [pltpu.HBM(land.shape, land.dtype)],
        input_output_aliases={i: i for i in range(ns + 1)},
        compiler_params=pltpu.CompilerParams(has_side_effects=EFFECT),
    )(*fulls, land, send_sem, recv_sem, _hbm(after))
    return list(outs[:ns]), outs[ns]


def _tp(w):
    return jnp.swapaxes(w, -1, -2)


def _s5_prepare(a_re, a_im, log_dt, b_re, b_im, c_re, c_im):
    a = jnp.stack([a_re, a_im], axis=1)
    ldt = jnp.broadcast_to(log_dt[:, :, None], (DEPTH, SSM_GROUPS, SSM_STATE))
    a_row = a.reshape(DEPTH, 2, 1, N_STATE)
    ldt_row = ldt.reshape(DEPTH, 1, N_STATE)
    a_rep = jnp.repeat(a, SSM_GROUP, axis=2)
    ldt_rep = jnp.repeat(ldt, SSM_GROUP, axis=1)
    bt = jnp.stack([_tp(b_re), _tp(b_im)], axis=1).reshape(DEPTH, 2, SSM_W, SSM_STATE)
    ct = jnp.stack([c_re, c_im], axis=1).reshape(DEPTH, 2, SSM_W, SSM_STATE)
    tile_e = jnp.tile(jnp.eye(SSM_STATE, dtype=BF16), (1, SSM_GROUPS))
    mask = jnp.repeat(jnp.repeat(jnp.eye(SSM_GROUPS, dtype=BF16), SSM_GROUP, axis=0), SSM_STATE, axis=1)
    out = []
    for l in range(DEPTH):
        tabs = _s5_disc(a_row[l], ldt_row[l], a_rep[l], ldt_rep[l], bt[l], ct[l], tile_e, mask)
        out.append(((a[l], ldt[l], a_rep[l], ldt_rep[l], bt[l], mask), *tabs))
    return out


def _layer_fwd(h, p_l, small, big, arrive=None):
    saved = {'h0': h}
    if arrive is not None:
        arrive(0, h)
    h, saved['gu1'] = _ffn_fwd(h, small['ffn1_norm'], big['ff1'])
    saved['h1'] = h
    if arrive is not None:
        arrive(1, h)
    z = _inproj_fwd(h, small['mix_norm'], big['wint'])
    ya, ys, hs = _s5conv_fwd(z, small['conv_w'], small['conv_b'], small['bbmat'], small['ccmat'], small['dvec'],
                             small['ltab'])
    saved.update(z=z, ya=ya, ys=ys, hs=hs)
    h = _mix_out_fwd(h, ya, ys, big['glu'], small['glu_b'], small['conv_out_norm'], small['ssm_out_norm'], big['wout'])
    saved['h2'] = h
    if arrive is not None:
        arrive(2, h)
    h, saved['gu2'] = _ffn_fwd(h, small['ffn2_norm'], big['ff2'])
    saved['h3'] = h
    h = _ple_fwd(h, small['ple_norm'], p_l, big['plg'], big['plpt'])
    return h, saved


def _ffn_bwd(h_in, g, dh, gu, w3):
    dh_in, dga, ud, dg = _ffn_bwd_act(h_in, g, dh, gu, w3)
    return dh_in, _matmul_tn(dga, ud, FF_BLOCK, BF16, "ffn_wgrad"), dg


def _layer_bwd_top(dh, p_l, small, big, saved):
    gs = {}
    dh, u, dq, dpp, pb, gs['ple_norm'] = _ple_bwd(saved['h3'], small['ple_norm'], p_l, dh, big['plg'], big['plpt'])
    d_plg = _matmul_tn(u, dq, 256, BF16, "ple_gate_wgrad")
    d_plpt = _matmul_tn(dpp, pb, 256, BF16, "ple_proj_wgrad", to_kernel=False)
    dh, d_ff2, gs['ffn2_norm'] = _ffn_bwd(saved['h2'], small['ffn2_norm'], dh, saved['gu2'], big['ff2'])
    return dh, (gs, d_plg, d_plpt, d_ff2)


def _layer_bwd_rest(dh, top, small, big, saved):
    gs, d_plg, d_plpt, d_ff2 = top
    dya, dys, ycat, dhb, zg, dq, part = _mix_out_bwd(dh, saved['ya'], saved['ys'], big['glu'], small['glu_b'],
                                                     small['conv_out_norm'], small['ssm_out_norm'], big['wout'])
    d_wout = _matmul_tn(ycat, dhb, 256, BF16, "w_out_wgrad")
    d_glu = _matmul_tn(zg, dq, 256, BF16, "glu_wgrad", to_kernel=False)
    dz, gadj, us, dyb, dl, dcw = _s5conv_bwd(saved['z'], saved['hs'], dya, dys, small['conv_w'], small['conv_b'],
                                             small['bbmat'], small['ccmat'], small['dvec'], small['ltab_rev'])
    d_bb = _matmul_tn(us, gadj, SSM_W, F32, "s5_b_wgrad", 1024, False)[0]
    d_cc = _matmul_tn(dyb, saved['hs'][None], SSM_W, F32, "s5_c_wgrad", 1024, False)[0]
    dh, u, gs['mix_norm'] = _inproj_bwd(saved['h1'], small['mix_norm'], dh, dz, big['wint'])
    d_wint = _matmul_tn(dz[None], u, 256, BF16, "w_in_wgrad")
    dh, d_ff1, gs['ffn1_norm'] = _ffn_bwd(saved['h0'], small['ffn1_norm'], dh, saved['gu1'], big['ff1'])

    dlb = dl[0].reshape(2, SSM_GROUPS, SSM_STATE)
    fold = jnp.tile(jnp.eye(SSM_STATE, dtype=BF16), (SSM_GROUPS, 1))
    da, dldt, dbt, dct = _s5_disc_bwd(*small['disc_in'], dlb, d_bb, d_cc, fold)
    gs['ssm_A_re'], gs['ssm_A_im'] = da[0], da[1]
    gs['ssm_log_dt'] = dldt[:, 0]
    ghp = (SSM_GROUPS, SSM_GROUP, SSM_STATE)
    gs['ssm_B_re'], gs['ssm_B_im'] = dbt[0].reshape(ghp), dbt[1].reshape(ghp)
    gs['ssm_C_re'], gs['ssm_C_im'] = dct[0].reshape(ghp), dct[1].reshape(ghp)
    gs['conv_w'] = dcw[0:3]
    gs['conv_b'] = dcw[3]
    gs['ssm_D'] = dcw[4].reshape(SSM_GROUPS, SSM_GROUP)
    gs['conv_out_norm'], gs['ssm_out_norm'], gs['glu_b'] = part[0], part[1], part[2]
    for n in ('ple_norm', 'ffn2_norm', 'mix_norm', 'ffn1_norm'):
        gs[n] = gs[n][0]
    fulls = [d_ff1, d_ff2, d_wint, d_wout, d_plg,
             d_plpt.reshape(1, D_MODEL * PLE_DIM // D_MODEL, D_MODEL), d_glu.reshape(1, SSM_W * SSM_W // D_MODEL, D_MODEL)]
    return dh, fulls, gs


VIEW_T = ('ffn1_w_gate', 'ffn1_w_up', 'ffn2_w_gate', 'ffn2_w_up', 'ssm_B_re', 'ssm_B_im')


def _view(name, a):
    return _tp(a) if name in VIEW_T else a


SEG_NAMES = ('ff1', 'ff2', 'wint', 'wout', 'plg', 'plpt', 'glu')
FIRST_LAYER_GROUPS = ((0,), (2, 3, 6), (1, 4, 5))


def _layer_pack(W, l, segments=range(len(SEGS))):
    pieces = {
        0: lambda: [_tp(W['ffn1_w_gate'][l]), _tp(W['ffn1_w_up'][l]), W['ffn1_w_down'][l]],
        1: lambda: [_tp(W['ffn2_w_gate'][l]), _tp(W['ffn2_w_up'][l]), W['ffn2_w_down'][l]],
        2: lambda: [_tp(W['w_in'][l])],
        3: lambda: [W['w_out'][l]],
        4: lambda: [W['ple_w_gate'][l]],
        5: lambda: [_tp(W['ple_w_proj'][l]).reshape(-1, D_MODEL)],
        6: lambda: [W['glu_w'][l].reshape(-1, D_MODEL)],
    }
    return jnp.concatenate([a for s in segments for a in pieces[s]()], axis=0).astype(BF16)


def _as_big(named):
    shape = dict(plpt=(D_MODEL, PLE_DIM), glu=(SSM_W, SSM_W))
    return {n: (a.reshape(shape[n]) if n in shape else a) for n, a in named.items()}


def _pad_rows(flat, mult, width=LANES):
    per = mult * width
    n = flat.shape[0]
    tot = -(-n // per) * per
    return jnp.pad(flat, (0, tot - n)).reshape(tot // width, width)


def _adamw_any(w, g, m, v):
    shp = w.shape
    two = (lambda t: t.reshape(-1, shp[-1]))
    d, nm, nv = _adamw(two(w), two(g), two(m), two(v))
    return d.reshape(shp), nm.reshape(shp), nv.reshape(shp)


def kernel(x, p, ffn1_norm, ffn1_w_gate, ffn1_w_up, ffn1_w_down, mix_norm, w_in, conv_w, conv_b, ssm_A_re, ssm_A_im, ssm_B_re, ssm_B_im, ssm_C_re, ssm_C_im, ssm_D, ssm_log_dt, glu_w, glu_b, conv_out_norm, ssm_out_norm, w_out, ffn2_norm, ffn2_w_gate, ffn2_w_up, ffn2_w_down, ple_norm, ple_w_gate, ple_w_proj, final_norm, loss_target, m_ffn1_norm, m_ffn1_w_gate, m_ffn1_w_up, m_ffn1_w_down, m_mix_norm, m_w_in, m_conv_w, m_conv_b, m_ssm_A_re, m_ssm_A_im, m_ssm_B_re, m_ssm_B_im, m_ssm_C_re, m_ssm_C_im, m_ssm_D, m_ssm_log_dt, m_glu_w, m_glu_b, m_conv_out_norm, m_ssm_out_norm, m_w_out, m_ffn2_norm, m_ffn2_w_gate, m_ffn2_w_up, m_ffn2_w_down, m_ple_norm, m_ple_w_gate, m_ple_w_proj, m_final_norm, v_ffn1_norm, v_ffn1_w_gate, v_ffn1_w_up, v_ffn1_w_down, v_mix_norm, v_w_in, v_conv_w, v_conv_b, v_ssm_A_re, v_ssm_A_im, v_ssm_B_re, v_ssm_B_im, v_ssm_C_re, v_ssm_C_im, v_ssm_D, v_ssm_log_dt, v_glu_w, v_glu_b, v_conv_out_norm, v_ssm_out_norm, v_w_out, v_ffn2_norm, v_ffn2_w_gate, v_ffn2_w_up, v_ffn2_w_down, v_ple_norm, v_ple_w_gate, v_ple_w_proj, v_final_norm):
    given = dict(locals())
    W = {n: given[n] for n in W_NAMES}
    M = {n: given['m_' + n] for n in W_NAMES}
    V = {n: given['v_' + n] for n in W_NAMES}
    Wv, Mv, Vv = [{n: _view(n, d[n]) for n in W_NAMES} for d in (W, M, V)]
    my_dev = _dev_index(_mesh_pos())

    conv_shard = _pad_rows(W['conv_w'].reshape(-1), SUBLANES)
    conv_all = _allgather(conv_shard, ((1, SUBLANES),), "ag_conv_w")[0]
    conv_full = conv_all.reshape(N_DEV, -1)[:, :DEPTH * 3 * (CONV_W // N_DEV)]
    conv_full = conv_full.reshape(N_DEV, DEPTH, 3, CONV_W // N_DEV).transpose(1, 2, 0, 3).reshape(DEPTH, 3, CONV_W)
    first, after = [], conv_all
    for gi, segments in enumerate(FIRST_LAYER_GROUPS):
        first.append(_ag_start(_layer_pack(W, 0, segments), tuple(SEGS[s] for s in segments), after,
                               "ag_start_0%s" % "abc"[gi]))
        after = first[-1][4]
    s5 = _s5_prepare(*[W[n] + after[0, 0] for n in ('ssm_A_re', 'ssm_A_im', 'ssm_log_dt')],
                     *[W[n] for n in ('ssm_B_re', 'ssm_B_im', 'ssm_C_re', 'ssm_C_im')])
    packs = [None] + [_layer_pack(W, l) for l in range(1, DEPTH)]
    prepared = conv_full[0, 0:1, 0:1] + s5[DEPTH - 1][1][0:1, 0:1] + packs[DEPTH - 1][0:1, 0:1].astype(F32)

    smalls, saves, bigs = [], [], []
    h = x[0]

    flight = None

    def gathered(handles, segments, after, name, next_layer=None, gate=None):
        nonlocal flight
        send_sems, recv_sems, pack_thru, lands, _ = handles
        pack_thru, lands = _ag_wait(send_sems, recv_sems, pack_thru, lands, after, "ag_wait_" + name)
        if next_layer is not None:
            flight = _ag_start(packs[next_layer], SEGS, pack_thru, "ag_start_%d" % next_layer)
            gate[0][gate[1]] = gate[0][gate[1]] + flight[4][0:1, 0:1]
        outs = _ag_finish(pack_thru, lands, tuple(SEGS[s] for s in segments))
        return _as_big({SEG_NAMES[s]: a for s, a in zip(segments, outs)})

    for l in range(DEPTH):
        small = {n: W[n][l][None] for n in ('ffn1_norm', 'mix_norm', 'conv_b', 'glu_b', 'conv_out_norm',
                                            'ssm_out_norm', 'ffn2_norm', 'ple_norm')}
        small['conv_w'] = conv_full[l]
        small['dvec'] = W['ssm_D'][l].reshape(1, SSM_W)
        small['disc_in'], small['ltab'], small['ltab_rev'], small['bbmat'], small['ccmat'] = s5[l]
        big = {}
        bigs.append(big)
        if l == 0:
            def arrive(stage, h_now, big=big, small=small):
                big.update(gathered(first[stage], FIRST_LAYER_GROUPS[stage], prepared if stage == 0 else h_now,
                                    "0%s" % "abc"[stage], *((1, (small, 'ffn2_norm')) if stage == 2 else ())))
            h, saved = _layer_fwd(h, p[l, 0], small, big, arrive)
        else:
            nxt = (l + 1, (small, 'ffn1_norm')) if l + 1 < DEPTH else ()
            big.update(gathered(flight, range(len(SEGS)), h, "%d" % l, *nxt))
            h, saved = _layer_fwd(h, p[l, 0], small, big)
        smalls.append(small)
        saves.append(saved)
    loss_tile, dh, d_final = _final_loss(h, W['final_norm'][None], loss_target[0])
    loss = lax.psum(loss_tile[0, 0], ("x", "y", "c"))

    layer_gs = [None] * DEPTH
    shard_grads = [None] * DEPTH
    zero = jnp.zeros((1, 1), F32)
    sib, ici = None, None

    def finish_sibling(after_sib, after_ici):
        nonlocal sib, ici
        up, (send_sem, recv_sem, fulls_thru, land, _) = sib
        fulls_thru, got = _rs_sibling_wait(send_sem, recv_sem, fulls_thru, land, after_sib, "sib_wait_%d" % up)
        own32, pbf = _pair_sum(fulls_thru, got, SEGS)
        done = finish_chips(own32)
        ici = (up, _rs_chips_start(pbf, after_ici if done is None else done, "rs_start_%d" % up), own32)
        sib = None

    def finish_chips(after):
        nonlocal ici
        if ici is None:
            return None
        up, (send_sems, recv_sems, pbf_thru, land, _), own32 = ici
        got3 = _rs_chips_wait(send_sems, recv_sems, pbf_thru, land, after, "rs_wait_%d" % up)
        shard_grads[up] = _chip_sum(own32, got3)
        ici = None
        return shard_grads[up]

    layer_names = [n for n in SMALL_NAMES if n != 'final_norm']
    small_flights = [None] * DEPTH
    for l in reversed(range(DEPTH)):
        small = dict(smalls[l])
        if sib is not None:
            small['ple_norm'] = small['ple_norm'] + sib[1][4][0:1, 0:1] + small_flights[l + 1][4][0:1, 0:1]
        dh, top = _layer_bwd_top(dh, p[l, 0], small, bigs[l], saves[l])
        if sib is not None:
            finish_sibling(dh, dh)
            small['glu_b'] = small['glu_b'] + ici[1][4][0:1, 0:1]
        dh, fulls, layer_gs[l] = _layer_bwd_rest(dh, top, small, bigs[l], saves[l])
        sib = (l, _rs_sibling_start(fulls, SEGS, "sib_start_%d" % l))
        last_slot = d_final[0] if l == DEPTH - 1 else jnp.zeros((D_MODEL,), F32)
        flat = jnp.concatenate([layer_gs[l][n].reshape(-1) for n in layer_names + ['conv_w']] + [last_slot])
        small_flights[l] = _small_gather_start(_pad_rows(flat, SUBLANES, D_MODEL), "small_start_%d" % l)
    grad_x = dh[None]
    finish_sibling(small_flights[0][4], small_flights[0][4])

    reduced = []
    for l in range(DEPTH):
        send_sems, recv_sems, flat_thru, land, _ = small_flights[l]
        flat_thru, land = _small_gather_wait(send_sems, recv_sems, flat_thru, land, ici[1][4], "small_wait_%d" % l)
        reduced.append(_sum_devices(land, flat_thru).reshape(-1))
    G = {}
    o = 0
    for n in layer_names + ['conv_w']:
        size = (W[n].size if n != 'conv_w' else DEPTH * 3 * CONV_W) // DEPTH
        shape = Wv[n].shape if n != 'conv_w' else (DEPTH, 3, CONV_W)
        G[n] = jnp.stack([red[o:o + size] for red in reduced]).reshape(shape)
        o += size
    G['final_norm'] = reduced[DEPTH - 1][o:o + D_MODEL]
    G['conv_w'] = lax.dynamic_slice_in_dim(G['conv_w'], my_dev * (CONV_W // N_DEV), CONV_W // N_DEV, axis=2)

    delta, new_m, new_v = {}, {}, {}
    cat = lambda src: _pad_rows(jnp.concatenate([src[n].reshape(-1) for n in SMALL_NAMES]), SUBLANES, D_MODEL)
    d_s, m_s, v_s = _adamw(cat(Wv), cat(G), cat(Mv), cat(Vv))
    o = 0
    for n in SMALL_NAMES:
        for dst, src in ((delta, d_s), (new_m, m_s), (new_v, v_s)):
            dst[n] = src.reshape(-1)[o:o + W[n].size].reshape(Wv[n].shape)
        o += W[n].size
    delta['conv_w'], new_m['conv_w'], new_v['conv_w'] = _adamw_any(Wv['conv_w'], G['conv_w'], Mv['conv_w'], Vv['conv_w'])

    def unpack(sg):
        nl = sg.shape[0]
        offs = _seg_offsets(SEGS)
        r = SEGS[0][1]
        out = {}
        for a, f in ((0, 'ffn1'), (1, 'ffn2')):
            out[f + '_w_gate'] = sg[:, offs[a]:offs[a] + r]
            out[f + '_w_up'] = sg[:, offs[a] + r:offs[a] + 2 * r]
            out[f + '_w_down'] = sg[:, offs[a] + 2 * r:offs[a] + 3 * r]
        out['w_in'] = _tp(sg[:, offs[2]:offs[2] + SEGS[2][1]])
        out['w_out'] = sg[:, offs[3]:offs[3] + SEGS[3][1]]
        out['ple_w_gate'] = sg[:, offs[4]:offs[4] + SEGS[4][1]]
        out['ple_w_proj'] = _tp(sg[:, offs[5]:offs[5] + SEGS[5][1]].reshape(nl, D_MODEL // N_DEV, PLE_DIM))
        out['glu_w'] = sg[:, offs[6]:offs[6] + SEGS[6][1]].reshape(nl, SSM_W // N_DEV, SSM_W)
        return out

    upper = unpack(jnp.stack(shard_grads[1:]))
    part = {n: _adamw_layers(Wv[n], Mv[n], Vv[n], upper[n], 1, None) for n in upper}
    finish_chips(sum(part[n][3][1, 0:1, 0:1] for n in part))
    lower = unpack(shard_grads[0][None])
    for n in lower:
        G[n], delta[n], new_m[n], new_v[n] = _adamw_layers(Wv[n], Mv[n], Vv[n], lower[n], 0, part[n])

    outs = [[_view(n, d[n]) for n in W_NAMES] for d in (G, delta, new_m, new_v)]
    return (loss, grad_x, *outs[0], *outs[1], *outs[2], *outs[3])
```

```python
import math

import jax
import jax.numpy as jnp
from jax import lax
from jax.experimental import pallas as pl
from jax.experimental.pallas import tpu as pltpu

F32 = jnp.float32
BF16 = jnp.bfloat16

N_DEV = 8
DEPTH = 4
SEQ = 2048
D_MODEL = 1024
D_FF = 2816
CONV_W = 512
SSM_W = 512
SSM_GROUPS = 32
SSM_GROUP = 16
SSM_STATE = 64
N_STATE = SSM_GROUPS * SSM_STATE
IN_COLS = 2048
PLE_DIM = 256
EPS = 1e-6

ADAM_LR = 0.001
ADAM_B1 = 0.9
ADAM_B2 = 0.999
ADAM_EPS = 1e-08
ADAM_WD = 0.01
ADAM_STEP = 10

FF_BLOCK = 256
N_FF_BLOCKS = D_FF // FF_BLOCK
TOK_TILE_FFN_FWD = 2048
TOK_TILE_FFN_BWD = 1024
TOK_TILE = 512
CHUNK = 256
N_CHUNKS = SEQ // CHUNK
LANE_GROUP = 512
SUBLANES = 8
LANES = 128
MIB = 1024 * 1024

W_NAMES = ['ffn1_norm', 'ffn1_w_gate', 'ffn1_w_up', 'ffn1_w_down', 'mix_norm', 'w_in', 'conv_w', 'conv_b',
           'ssm_A_re', 'ssm_A_im', 'ssm_B_re', 'ssm_B_im', 'ssm_C_re', 'ssm_C_im', 'ssm_D', 'ssm_log_dt',
           'glu_w', 'glu_b', 'conv_out_norm', 'ssm_out_norm', 'w_out', 'ffn2_norm', 'ffn2_w_gate', 'ffn2_w_up',
           'ffn2_w_down', 'ple_norm', 'ple_w_gate', 'ple_w_proj', 'final_norm']
SMALL_NAMES = ['ffn1_norm', 'mix_norm', 'conv_b', 'ssm_A_re', 'ssm_A_im', 'ssm_B_re', 'ssm_B_im', 'ssm_C_re',
               'ssm_C_im', 'ssm_D', 'ssm_log_dt', 'glu_b', 'conv_out_norm', 'ssm_out_norm', 'ffn2_norm',
               'ple_norm', 'final_norm']

SEGS = ((3, 352), (3, 352), (1, 256), (1, 128), (1, 128), (1, 32), (1, 32))
PACK_ROWS = sum(n * r for n, r in SEGS)

MESH = pl.DeviceIdType.MESH
UNREAD = pl.BlockSpec(memory_space=pltpu.HBM)


def _in_hbm(*arrays):
    return [pltpu.with_memory_space_constraint(a, pltpu.HBM) for a in arrays]


def _out_hbm(outs, which):
    if not isinstance(outs, (list, tuple)):
        return pltpu.with_memory_space_constraint(outs, pltpu.HBM) if which else outs
    return [pltpu.with_memory_space_constraint(a, pltpu.HBM) if i in which else a for i, a in enumerate(outs)]


def _cparams(sem=None, vmem_mib=48, **kw):
    return pltpu.CompilerParams(dimension_semantics=sem, vmem_limit_bytes=vmem_mib * MIB, **kw)


def _dot(a, b):
    return jnp.dot(a, b, preferred_element_type=F32)


def _dot_nt(a, b):
    return lax.dot_general(a, b, (((1,), (1,)), ((), ())), preferred_element_type=F32)


def _dot_tn(a, b):
    return lax.dot_general(a, b, (((0,), (0,)), ((), ())), preferred_element_type=F32)


def _rms_stats(x):
    r = lax.rsqrt(jnp.mean(x * x, axis=-1, keepdims=True) + EPS)
    return x * r, r


def _rms_bwd(dy, xh, r, g):
    dxh = dy * g
    dx = r * (dxh - xh * jnp.mean(dxh * xh, axis=-1, keepdims=True))
    dg = jnp.sum(dy * xh, axis=0, keepdims=True)
    return dx, dg


def _sigmoid(x):
    return 0.5 * jnp.tanh(0.5 * x) + 0.5


_GELU_C = math.sqrt(2.0 / math.pi)


def _gelu(x):
    t = jnp.tanh(_GELU_C * (x + 0.044715 * x * x * x))
    return 0.5 * x * (1.0 + t), t


def _gelu_grad(x, t):
    return 0.5 * (1.0 + t) + 0.5 * x * (1.0 - t * t) * _GELU_C * (1.0 + 3.0 * 0.044715 * x * x)


def _accumulate(ref, first, value):
    @pl.when(first)
    def _():
        ref[...] = value

    @pl.when(jnp.logical_not(first))
    def _():
        ref[...] += value


def _ffn_fwd(h, g, w3):
    tm = TOK_TILE_FFN_FWD
    last = N_FF_BLOCKS - 1

    def body(h_ref, g_ref, wgu_ref, wd_ref, wd_last_ref, out_ref, gu_ref, u_ref, a_ref):
        k = pl.program_id(1)

        @pl.when(k == 0)
        def _():
            x = h_ref[...]
            xh, _ = _rms_stats(x)
            u_ref[...] = (xh * g_ref[...]).astype(BF16)
            out_ref[...] = x
            a_ref[1] = jnp.zeros((tm, FF_BLOCK), BF16)

        out_ref[...] += 0.5 * _dot(a_ref[(k + 1) % 2], wd_ref[0])
        gu = _dot_nt(u_ref[...], wgu_ref[...].reshape(2 * FF_BLOCK, D_MODEL))
        gate, up = gu[:, :FF_BLOCK], gu[:, FF_BLOCK:]
        a_ref[k % 2] = (gate * _sigmoid(gate) * up).astype(BF16)
        gu_ref[0] = gate.astype(BF16)
        gu_ref[1] = up.astype(BF16)

        @pl.when(k == last)
        def _():
            out_ref[...] += 0.5 * _dot(a_ref[last % 2], wd_last_ref[0])

    return _out_hbm(pl.pallas_call(
        body, name="ffn_fwd",
        grid=(SEQ // tm, N_FF_BLOCKS),
        in_specs=[pl.BlockSpec((tm, D_MODEL), lambda m, k: (m, 0), pipeline_mode=pl.Buffered(1)),
                  pl.BlockSpec((1, D_MODEL), lambda m, k: (0, 0)),
                  pl.BlockSpec((2, FF_BLOCK, D_MODEL), lambda m, k: (0, k, 0)),
                  pl.BlockSpec((1, FF_BLOCK, D_MODEL), lambda m, k: (2, jnp.maximum(k - 1, 0), 0)),
                  pl.BlockSpec((1, FF_BLOCK, D_MODEL), lambda m, k: (2, last, 0), pipeline_mode=pl.Buffered(1))],
        out_specs=[pl.BlockSpec((tm, D_MODEL), lambda m, k: (m, 0)),
                   pl.BlockSpec((2, tm, FF_BLOCK), lambda m, k: (0, m, k))],
        out_shape=[jax.ShapeDtypeStruct((SEQ, D_MODEL), F32),
                   pltpu.HBM((2, SEQ, D_FF), BF16)],
        scratch_shapes=[pltpu.VMEM((tm, D_MODEL), BF16), pltpu.VMEM((2, tm, FF_BLOCK), BF16)],
        compiler_params=_cparams(("parallel", "arbitrary"), 56),
    )(*_in_hbm(h, g, w3, w3, w3)), (1,))


def _ffn_bwd_act(h, g, dout, gu, w3):
    tm = TOK_TILE_FFN_BWD
    last = N_FF_BLOCKS - 1

    def body(h_ref, g_ref, d_ref, gu_ref, wd_ref, wgu_ref, wgu_last_ref, dh_ref, dga_ref, ud_ref, dg_ref,
             acc_ref, dgu_ref):
        m = pl.program_id(0)
        k = pl.program_id(1)

        @pl.when(k == 0)
        def _():
            xh, _ = _rms_stats(h_ref[...])
            ud_ref[0] = (xh * g_ref[...]).astype(BF16)
            ud_ref[1] = (0.5 * d_ref[...]).astype(BF16)
            acc_ref[...] = jnp.zeros_like(acc_ref)
            dgu_ref[1] = jnp.zeros((tm, 2 * FF_BLOCK), BF16)

        acc_ref[...] += _dot(dgu_ref[(k + 1) % 2], wgu_ref[...].reshape(2 * FF_BLOCK, D_MODEL))
        gate = gu_ref[0].astype(F32)
        up = gu_ref[1].astype(F32)
        sg = _sigmoid(gate)
        silu = gate * sg
        da = _dot_nt(ud_ref[1], wd_ref[0])
        dgate = (da * up * (sg + silu * (1.0 - sg))).astype(BF16)
        dup = (da * silu).astype(BF16)
        dga_ref[0] = dgate
        dga_ref[1] = dup
        dga_ref[2] = (silu * up).astype(BF16)
        dgu_ref[k % 2, :, 0:FF_BLOCK] = dgate
        dgu_ref[k % 2, :, FF_BLOCK:2 * FF_BLOCK] = dup

        @pl.when(k == last)
        def _():
            du = acc_ref[...] + _dot(dgu_ref[last % 2], wgu_last_ref[...].reshape(2 * FF_BLOCK, D_MODEL))
            xh, r = _rms_stats(h_ref[...])
            dx, dg = _rms_bwd(du, xh, r, g_ref[...])
            dh_ref[...] = d_ref[...] + dx
            _accumulate(dg_ref, m == 0, dg)

    return _out_hbm(pl.pallas_call(
        body, name="ffn_bwd_act",
        grid=(SEQ // tm, N_FF_BLOCKS),
        in_specs=[pl.BlockSpec((tm, D_MODEL), lambda m, k: (m, 0), pipeline_mode=pl.Buffered(1)),
                  pl.BlockSpec((1, D_MODEL), lambda m, k: (0, 0)),
                  pl.BlockSpec((tm, D_MODEL), lambda m, k: (m, 0), pipeline_mode=pl.Buffered(1)),
                  pl.BlockSpec((2, tm, FF_BLOCK), lambda m, k: (0, m, k)),
                  pl.BlockSpec((1, FF_BLOCK, D_MODEL), lambda m, k: (2, k, 0)),
                  pl.BlockSpec((2, FF_BLOCK, D_MODEL), lambda m, k: (0, jnp.maximum(k - 1, 0), 0)),
                  pl.BlockSpec((2, FF_BLOCK, D_MODEL), lambda m, k: (0, last, 0), pipeline_mode=pl.Buffered(1))],
        out_specs=[pl.BlockSpec((tm, D_MODEL), lambda m, k: (m, 0)),
                   pl.BlockSpec((3, tm, FF_BLOCK), lambda m, k: (0, m, k)),
                   pl.BlockSpec((2, tm, D_MODEL), lambda m, k: (0, m, 0)),
                   pl.BlockSpec((1, D_MODEL), lambda m, k: (0, 0))],
        out_shape=[jax.ShapeDtypeStruct((SEQ, D_MODEL), F32),
                   pltpu.HBM((3, SEQ, D_FF), BF16),
                   pltpu.HBM((2, SEQ, D_MODEL), BF16),
                   jax.ShapeDtypeStruct((1, D_MODEL), F32)],
        scratch_shapes=[pltpu.VMEM((tm, D_MODEL), F32), pltpu.VMEM((2, tm, 2 * FF_BLOCK), BF16)],
        compiler_params=_cparams(("arbitrary", "arbitrary"), 56),
    )(*_in_hbm(h, g, dout, gu, w3, w3, w3)), (1, 2))


def _matmul_tn(a, b, bm, out_dtype, name, bn=None, to_kernel=True):
    na, t, m = a.shape
    nb, _, n = b.shape
    bn = n if bn is None else bn

    def body(a_ref, b_ref, o_ref):
        o_ref[0] = _dot_tn(a_ref[0], b_ref[0]).astype(out_dtype)

    return _out_hbm(pl.pallas_call(
        body, name=name,
        grid=(na, m // bm, n // bn),
        in_specs=[pl.BlockSpec((1, t, bm), lambda i, k, j: (i, 0, k)),
                  pl.BlockSpec((1, t, bn), lambda i, k, j: (jnp.maximum(i - (na - nb), 0), 0, j))],
        out_specs=pl.BlockSpec((1, bm, bn), lambda i, k, j: (i, k, j)),
        out_shape=pltpu.HBM((na, m, n), out_dtype) if to_kernel else jax.ShapeDtypeStruct((na, m, n), out_dtype),
        compiler_params=_cparams(("arbitrary", "parallel", "parallel")),
    )(*_in_hbm(a, b)), to_kernel)


def _inproj_fwd(h, g, wint):
    tm = TOK_TILE

    def body(h_ref, g_ref, w_ref, z_ref):
        xh, _ = _rms_stats(h_ref[...])
        z_ref[...] = _dot_nt((xh * g_ref[...]).astype(BF16), w_ref[...])

    return pl.pallas_call(
        body, name="inproj_fwd",
        grid=(SEQ // tm,),
        in_specs=[pl.BlockSpec((tm, D_MODEL), lambda m: (m, 0)),
                  pl.BlockSpec((1, D_MODEL), lambda m: (0, 0)),
                  pl.BlockSpec((None, IN_COLS, D_MODEL), lambda m: (0, 0, 0))],
        out_specs=pl.BlockSpec((tm, IN_COLS), lambda m: (m, 0)),
        out_shape=jax.ShapeDtypeStruct((SEQ, IN_COLS), F32),
        compiler_params=_cparams(("parallel",)),
    )(*_in_hbm(h, g, wint))


def _inproj_bwd(h, g, dh, dz, wint):
    tm = TOK_TILE

    def body(h_ref, g_ref, dh_ref, dz_ref, w_ref, o_ref, u_ref, dg_ref):
        xh, r = _rms_stats(h_ref[...])
        u_ref[0] = (xh * g_ref[...]).astype(BF16)
        dx, dg = _rms_bwd(_dot(dz_ref[...], w_ref[...]), xh, r, g_ref[...])
        o_ref[...] = dh_ref[...] + dx
        _accumulate(dg_ref, pl.program_id(0) == 0, dg)

    return _out_hbm(pl.pallas_call(
        body, name="inproj_bwd",
        grid=(SEQ // tm,),
        in_specs=[pl.BlockSpec((tm, D_MODEL), lambda m: (m, 0)),
                  pl.BlockSpec((1, D_MODEL), lambda m: (0, 0)),
                  pl.BlockSpec((tm, D_MODEL), lambda m: (m, 0)),
                  pl.BlockSpec((tm, IN_COLS), lambda m: (m, 0)),
                  pl.BlockSpec((None, IN_COLS, D_MODEL), lambda m: (0, 0, 0))],
        out_specs=[pl.BlockSpec((tm, D_MODEL), lambda m: (m, 0)),
                   pl.BlockSpec((1, tm, D_MODEL), lambda m: (0, m, 0)),
                   pl.BlockSpec((1, D_MODEL), lambda m: (0, 0))],
        out_shape=[jax.ShapeDtypeStruct((SEQ, D_MODEL), F32),
                   pltpu.HBM((1, SEQ, D_MODEL), BF16),
                   jax.ShapeDtypeStruct((1, D_MODEL), F32)],
        compiler_params=_cparams(("arbitrary",)),
    )(*_in_hbm(h, g, dh, dz, wint)), (1,))


def _row_ids(n, w):
    return lax.broadcasted_iota(jnp.int32, (n, w), 0)


def _bcast_row(x, i, n):
    return jnp.broadcast_to(x[i:i + 1, :], (n, x.shape[1]))


def _conv_taps(v, tail):
    n, w = v.shape
    rid = _row_ids(n, w)
    v1 = jnp.where(rid == 0, _bcast_row(tail, 7, n), pltpu.roll(v, 1, 0))
    v2 = jnp.where(rid == 0, _bcast_row(tail, 6, n),
                   jnp.where(rid == 1, _bcast_row(tail, 7, n), pltpu.roll(v, 2, 0)))
    return v1, v2


def _block_tiles():
    half_rows, half_cols = SSM_W // 2, N_STATE // 2
    for half in range(2):
        for part in range(2):
            yield (slice(half * half_rows, (half + 1) * half_rows),
                   slice(part * N_STATE + half * half_cols, part * N_STATE + (half + 1) * half_cols))


def _block_expand(x, mat_ref, out_ref):
    for rows, cols in _block_tiles():
        out_ref[:, cols] = _dot(x[:, rows], mat_ref[rows, cols])


def _block_contract(s, mat_ref):
    halves = {}
    for rows, cols in _block_tiles():
        part = _dot_nt(s[:, cols], mat_ref[rows, cols])
        halves[rows.start] = part if rows.start not in halves else halves[rows.start] + part
    return jnp.concatenate([halves[k] for k in sorted(halves)], axis=1)


def _block_wgrad(a, b, name):
    t = a.shape[1]
    half_rows, half_cols = SSM_W // 2, N_STATE // 2

    def body(a_ref, b_ref, o_ref):
        o_ref[...] = _dot_tn(a_ref[...], b_ref[...])

    return pl.pallas_call(
        body, name=name,
        grid=(2, 2),
        in_specs=[pl.BlockSpec((None, t, half_rows), lambda h, p: (0, 0, h)),
                  pl.BlockSpec((None, t, half_cols), lambda h, p: (0, 0, 2 * p + h))],
        out_specs=pl.BlockSpec((half_rows, half_cols), lambda h, p: (h, 2 * p + h)),
        out_shape=jax.ShapeDtypeStruct((SSM_W, 2 * N_STATE), F32),
        compiler_params=_cparams(("parallel", "parallel")),
    )(*_in_hbm(a, b))


def _scan_chunk(work, ltab, carry, reverse):
    nblk = CHUNK // SUBLANES
    for gi in range(N_STATE // LANE_GROUP):
        cre = pl.ds(gi * LANE_GROUP, LANE_GROUP)
        cim = pl.ds(N_STATE + gi * LANE_GROUP, LANE_GROUP)
        pows = [(ltab[8 * k:8 * k + 8, cre], ltab[8 * k:8 * k + 8, cim]) for k in range(3)]
        pr = ltab[24:32, cre]
        pi = ltab[24:32, cim]

        def blk(i, c, cre=cre, cim=cim, pows=pows, pr=pr, pi=pi):
            cr, ci = c
            b = (nblk - 1 - i) if reverse else i
            r0 = pl.multiple_of(b * SUBLANES, SUBLANES)
            xr = work[pl.ds(r0, SUBLANES), cre]
            xi = work[pl.ds(r0, SUBLANES), cim]
            for k, s in enumerate((1, 2, 4)):
                lr, li = pows[k]
                shift = SUBLANES - s if reverse else s
                sr = pltpu.roll(xr, shift, 0)
                si = pltpu.roll(xi, shift, 0)
                xr, xi = xr + lr * sr - li * si, xi + lr * si + li * sr
            xr, xi = xr + pr * cr - pi * ci, xi + pr * ci + pi * cr
            work[pl.ds(r0, SUBLANES), cre] = xr
            work[pl.ds(r0, SUBLANES), cim] = xi
            edge = 0 if reverse else SUBLANES - 1
            return _bcast_row(xr, edge, SUBLANES), _bcast_row(xi, edge, SUBLANES)

        cr, ci = lax.fori_loop(0, nblk, blk, (carry[:, cre], carry[:, cim]))
        carry[:, cre] = cr
        carry[:, cim] = ci


def _s5conv_fwd(z, convw, convb, bbmat, ccmat, dvec, ltab):
    def body(z_ref, cw_ref, cb_ref, bb_ref, cc_ref, d_ref, lt_ref, ya_ref, ys_ref, hs_ref,
             work, carry, tail):
        c = pl.program_id(0)

        @pl.when(c == 0)
        def _():
            carry[...] = jnp.zeros_like(carry)
            tail[...] = jnp.zeros_like(tail)

        zb = z_ref[:, 0:CONV_W]
        v = z_ref[:, CONV_W:2 * CONV_W] * z_ref[:, 2 * CONV_W:3 * CONV_W]
        us = z_ref[:, 3 * CONV_W:4 * CONV_W]
        v1, v2 = _conv_taps(v, tail[...])
        tail[...] = v[CHUNK - 8:CHUNK, :]
        y = cw_ref[0:1, :] * v2 + cw_ref[1:2, :] * v1 + cw_ref[2:3, :] * v
        ya_ref[...] = zb * (y + cb_ref[...])

        _block_expand(us.astype(BF16), bb_ref, work)
        _scan_chunk(work, lt_ref, carry, reverse=False)
        hs = work[...].astype(BF16)
        hs_ref[...] = hs
        ys_ref[...] = _block_contract(hs, cc_ref) + d_ref[...] * us

    return _out_hbm(pl.pallas_call(
        body, name="s5conv_fwd",
        grid=(N_CHUNKS,),
        in_specs=[pl.BlockSpec((CHUNK, IN_COLS), lambda c: (c, 0)),
                  pl.BlockSpec((3, CONV_W), lambda c: (0, 0)),
                  pl.BlockSpec((1, CONV_W), lambda c: (0, 0)),
                  pl.BlockSpec((SSM_W, 2 * N_STATE), lambda c: (0, 0)),
                  pl.BlockSpec((SSM_W, 2 * N_STATE), lambda c: (0, 0)),
                  pl.BlockSpec((1, SSM_W), lambda c: (0, 0)),
                  pl.BlockSpec((32, 2 * N_STATE), lambda c: (0, 0))],
        out_specs=[pl.BlockSpec((CHUNK, CONV_W), lambda c: (c, 0)),
                   pl.BlockSpec((CHUNK, SSM_W), lambda c: (c, 0)),
                   pl.BlockSpec((CHUNK, 2 * N_STATE), lambda c: (c, 0))],
        out_shape=[pltpu.HBM((SEQ, CONV_W), F32),
                   pltpu.HBM((SEQ, SSM_W), F32),
                   jax.ShapeDtypeStruct((SEQ, 2 * N_STATE), BF16)],
        scratch_shapes=[pltpu.VMEM((CHUNK, 2 * N_STATE), F32),
                        pltpu.VMEM((8, 2 * N_STATE), F32),
                        pltpu.VMEM((8, CONV_W), F32)],
        compiler_params=_cparams(("arbitrary",)),
    )(*_in_hbm(z, convw, convb, bbmat, ccmat, dvec, ltab)), (0, 1))


def _s5conv_bwd(z, hs, dya, dys, convw, convb, bbmat, ccmat, dvec, ltab_rev):
    nc = N_CHUNKS
    hb = 16

    def body(z_ref, zp_ref, hs_ref, hp_ref, dya_ref, dys_ref, cw_ref, cb_ref, bb_ref, cc_ref, d_ref, lt_ref,
             dz_ref, g_ref, us_ref, dyb_ref, dl_ref, dcw_ref, work, carry, head):
        i = pl.program_id(0)
        first_chunk = i == nc - 1

        @pl.when(i == 0)
        def _():
            carry[...] = jnp.zeros_like(carry)
            head[...] = jnp.zeros_like(head)
            dl_ref[...] = jnp.zeros_like(dl_ref)
            dcw_ref[...] = jnp.zeros_like(dcw_ref)

        us = z_ref[:, 3 * CONV_W:4 * CONV_W]
        dy = dys_ref[...]
        dy_bf = dy.astype(BF16)
        us_ref[0] = us.astype(BF16)
        dyb_ref[0] = dy_bf

        _block_expand(dy_bf, cc_ref, work)
        _scan_chunk(work, lt_ref, carry, reverse=True)
        gg = work[...]
        gg_bf = gg.astype(BF16)
        g_ref[0] = gg_bf
        dus = d_ref[...] * dy + _block_contract(gg_bf, bb_ref)

        hcur = hs_ref[...].astype(F32)
        hlast = hp_ref[...].astype(F32)[hb - 1:hb, :]
        hlast = jnp.where(first_chunk, 0.0, hlast)
        rid = _row_ids(CHUNK, 2 * N_STATE)
        hprev = jnp.where(rid == 0, jnp.broadcast_to(hlast, (CHUNK, 2 * N_STATE)), pltpu.roll(hcur, 1, 0))
        gr, gi = gg[:, :N_STATE], gg[:, N_STATE:]
        hr, hi = hprev[:, :N_STATE], hprev[:, N_STATE:]
        dl_ref[:, :N_STATE] += (gr * hr + gi * hi).reshape(CHUNK // 8, 8, N_STATE).sum(axis=0)
        dl_ref[:, N_STATE:] += (gi * hr - gr * hi).reshape(CHUNK // 8, 8, N_STATE).sum(axis=0)

        @pl.when(i == nc - 1)
        def _():
            dl_ref[0:1, :] = jnp.sum(dl_ref[...], axis=0, keepdims=True)

        zb = z_ref[:, 0:CONV_W]
        zc = z_ref[:, CONV_W:2 * CONV_W]
        zv = z_ref[:, 2 * CONV_W:3 * CONV_W]
        v = zc * zv
        vtail = jnp.where(first_chunk, 0.0, zp_ref[:, CONV_W:2 * CONV_W] * zp_ref[:, 2 * CONV_W:3 * CONV_W])
        v1, v2 = _conv_taps(v, vtail)
        w0, w1, w2 = cw_ref[0:1, :], cw_ref[1:2, :], cw_ref[2:3, :]
        y = w0 * v2 + w1 * v1 + w2 * v
        dya_v = dya_ref[...]
        dzb = dya_v * (y + cb_ref[...])
        dyc = dya_v * zb
        hd = head[...]
        rc = _row_ids(CHUNK, CONV_W)
        n1 = jnp.where(rc == CHUNK - 1, _bcast_row(hd, 0, CHUNK), pltpu.roll(dyc, CHUNK - 1, 0))
        n2 = jnp.where(rc == CHUNK - 1, _bcast_row(hd, 1, CHUNK),
                       jnp.where(rc == CHUNK - 2, _bcast_row(hd, 0, CHUNK), pltpu.roll(dyc, CHUNK - 2, 0)))
        head[...] = dyc[0:8, :]
        dv = w2 * dyc + w1 * n1 + w0 * n2
        dz_ref[:, 0:CONV_W] = dzb.astype(BF16)
        dz_ref[:, CONV_W:2 * CONV_W] = (dv * zv).astype(BF16)
        dz_ref[:, 2 * CONV_W:3 * CONV_W] = (dv * zc).astype(BF16)
        dz_ref[:, 3 * CONV_W:4 * CONV_W] = dus.astype(BF16)
        dcw_ref[0:1, :] += jnp.sum(dyc * v2, axis=0, keepdims=True)
        dcw_ref[1:2, :] += jnp.sum(dyc * v1, axis=0, keepdims=True)
        dcw_ref[2:3, :] += jnp.sum(dyc * v, axis=0, keepdims=True)
        dcw_ref[3:4, :] += jnp.sum(dyc, axis=0, keepdims=True)
        dcw_ref[4:5, :] += jnp.sum(dy * us, axis=0, keepdims=True)

    rev = lambda i: nc - 1 - i
    return _out_hbm(pl.pallas_call(
        body, name="s5conv_bwd",
        grid=(nc,),
        in_specs=[pl.BlockSpec((CHUNK, IN_COLS), lambda i: (rev(i), 0)),
                  pl.BlockSpec((8, IN_COLS), lambda i: (jnp.maximum(rev(i) * (CHUNK // 8) - 1, 0), 0)),
                  pl.BlockSpec((CHUNK, 2 * N_STATE), lambda i: (rev(i), 0)),
                  pl.BlockSpec((hb, 2 * N_STATE), lambda i: (jnp.maximum(rev(i) * (CHUNK // hb) - 1, 0), 0)),
                  pl.BlockSpec((CHUNK, CONV_W), lambda i: (rev(i), 0)),
                  pl.BlockSpec((CHUNK, SSM_W), lambda i: (rev(i), 0)),
                  pl.BlockSpec((3, CONV_W), lambda i: (0, 0)),
                  pl.BlockSpec((1, CONV_W), lambda i: (0, 0)),
                  pl.BlockSpec((SSM_W, 2 * N_STATE), lambda i: (0, 0)),
                  pl.BlockSpec((SSM_W, 2 * N_STATE), lambda i: (0, 0)),
                  pl.BlockSpec((1, SSM_W), lambda i: (0, 0)),
                  pl.BlockSpec((32, 2 * N_STATE), lambda i: (0, 0))],
        out_specs=[pl.BlockSpec((CHUNK, IN_COLS), lambda i: (rev(i), 0)),
                   pl.BlockSpec((1, CHUNK, 2 * N_STATE), lambda i: (0, rev(i), 0)),
                   pl.BlockSpec((1, CHUNK, SSM_W), lambda i: (0, rev(i), 0)),
                   pl.BlockSpec((1, CHUNK, SSM_W), lambda i: (0, rev(i), 0)),
                   pl.BlockSpec((8, 2 * N_STATE), lambda i: (0, 0)),
                   pl.BlockSpec((8, CONV_W), lambda i: (0, 0))],
        out_shape=[jax.ShapeDtypeStruct((SEQ, IN_COLS), BF16),
                   pltpu.HBM((1, SEQ, 2 * N_STATE), BF16),
                   pltpu.HBM((1, SEQ, SSM_W), BF16),
                   pltpu.HBM((1, SEQ, SSM_W), BF16),
                   jax.ShapeDtypeStruct((8, 2 * N_STATE), F32),
                   jax.ShapeDtypeStruct((8, CONV_W), F32)],
        scratch_shapes=[pltpu.VMEM((CHUNK, 2 * N_STATE), F32),
                        pltpu.VMEM((8, 2 * N_STATE), F32),
                        pltpu.VMEM((8, CONV_W), F32)],
        compiler_params=_cparams(("arbitrary",)),
    )(*_in_hbm(z, z, hs, hs, dya, dys, convw, convb, bbmat, ccmat, dvec, ltab_rev)), (1, 2, 3))


def _mix_out_fwd(h, ya, ys, gluw, glub, con, son, wout):
    tm = TOK_TILE

    def body(h_ref, ya_ref, ys_ref, gw_ref, gb_ref, con_ref, son_ref, wo_ref, o_ref):
        zg, _ = _gelu(ys_ref[...])
        q = _dot(zg.astype(BF16), gw_ref[...]) + gb_ref[...]
        out_s = zg * _sigmoid(q)
        na, _ = _rms_stats(ya_ref[...])
        ns, _ = _rms_stats(out_s)
        o_ref[...] = (h_ref[...]
                      + _dot((na * con_ref[...]).astype(BF16), wo_ref[0:CONV_W, :])
                      + _dot((ns * son_ref[...]).astype(BF16), wo_ref[CONV_W:2 * CONV_W, :]))

    row = lambda m: (m, 0)
    fixed = lambda m: (0, 0)
    return pl.pallas_call(
        body, name="mix_out_fwd",
        grid=(SEQ // tm,),
        in_specs=[pl.BlockSpec((tm, D_MODEL), row), pl.BlockSpec((tm, CONV_W), row), pl.BlockSpec((tm, SSM_W), row),
                  pl.BlockSpec((SSM_W, SSM_W), fixed), pl.BlockSpec((1, SSM_W), fixed),
                  pl.BlockSpec((1, CONV_W), fixed), pl.BlockSpec((1, SSM_W), fixed),
                  pl.BlockSpec((None, D_MODEL, D_MODEL), lambda m: (0, 0, 0))],
        out_specs=pl.BlockSpec((tm, D_MODEL), row),
        out_shape=jax.ShapeDtypeStruct((SEQ, D_MODEL), F32),
        compiler_params=_cparams(("parallel",)),
    )(*_in_hbm(h, ya, ys, gluw, glub, con, son, wout))


def _mix_out_bwd(dh, ya, ys, gluw, glub, con, son, wout):
    tm = TOK_TILE

    def body(dh_ref, ya_ref, ys_ref, gw_ref, gb_ref, con_ref, son_ref, wo_ref,
             dya_ref, dys_ref, yc_ref, dhb_ref, zg_ref, dq_ref, part_ref):
        ysv = ys_ref[...]
        zg, th = _gelu(ysv)
        zg_bf = zg.astype(BF16)
        s = _sigmoid(_dot(zg_bf, gw_ref[...]) + gb_ref[...])
        out_s = zg * s
        na, ra = _rms_stats(ya_ref[...])
        ns, rs = _rms_stats(out_s)
        dh_bf = dh_ref[...].astype(BF16)
        yc_ref[0, :, 0:CONV_W] = (na * con_ref[...]).astype(BF16)
        yc_ref[0, :, CONV_W:2 * CONV_W] = (ns * son_ref[...]).astype(BF16)
        dhb_ref[0] = dh_bf
        dca = _dot_nt(dh_bf, wo_ref[0:CONV_W, :])
        dcs = _dot_nt(dh_bf, wo_ref[CONV_W:2 * CONV_W, :])
        dya, dcon = _rms_bwd(dca, na, ra, con_ref[...])
        dos, dson = _rms_bwd(dcs, ns, rs, son_ref[...])
        dya_ref[...] = dya
        dq = dos * zg * s * (1.0 - s)
        dq_bf = dq.astype(BF16)
        dzg = dos * s + _dot_nt(dq_bf, gw_ref[...])
        dys_ref[...] = dzg * _gelu_grad(ysv, th)
        zg_ref[0] = zg_bf
        dq_ref[0] = dq_bf
        rid = _row_ids(SUBLANES, SSM_W)
        part = jnp.zeros((SUBLANES, SSM_W), F32)
        for i, rowv in enumerate((dcon, dson, jnp.sum(dq, axis=0, keepdims=True))):
            part = jnp.where(rid == i, jnp.broadcast_to(rowv, (SUBLANES, SSM_W)), part)
        _accumulate(part_ref, pl.program_id(0) == 0, part)

    row = lambda m: (m, 0)
    fixed = lambda m: (0, 0)
    lead = lambda m: (0, m, 0)
    return _out_hbm(pl.pallas_call(
        body, name="mix_out_bwd",
        grid=(SEQ // tm,),
        in_specs=[pl.BlockSpec((tm, D_MODEL), row), pl.BlockSpec((tm, CONV_W), row), pl.BlockSpec((tm, SSM_W), row),
                  pl.BlockSpec((SSM_W, SSM_W), fixed), pl.BlockSpec((1, SSM_W), fixed),
                  pl.BlockSpec((1, CONV_W), fixed), pl.BlockSpec((1, SSM_W), fixed),
                  pl.BlockSpec((None, D_MODEL, D_MODEL), lambda m: (0, 0, 0))],
        out_specs=[pl.BlockSpec((tm, CONV_W), row), pl.BlockSpec((tm, SSM_W), row),
                   pl.BlockSpec((1, tm, D_MODEL), lead), pl.BlockSpec((1, tm, D_MODEL), lead),
                   pl.BlockSpec((1, tm, SSM_W), lead), pl.BlockSpec((1, tm, SSM_W), lead),
                   pl.BlockSpec((8, SSM_W), fixed)],
        out_shape=[pltpu.HBM((SEQ, CONV_W), F32), pltpu.HBM((SEQ, SSM_W), F32),
                   pltpu.HBM((1, SEQ, D_MODEL), BF16), pltpu.HBM((1, SEQ, D_MODEL), BF16),
                   pltpu.HBM((1, SEQ, SSM_W), BF16), pltpu.HBM((1, SEQ, SSM_W), BF16),
                   jax.ShapeDtypeStruct((8, SSM_W), F32)],
        compiler_params=_cparams(("arbitrary",)),
    )(*_in_hbm(dh, ya, ys, gluw, glub, con, son, wout)), (0, 1, 2, 3, 4, 5))


def _ple_fwd(h, g, p, wgate, wprojt):
    tm = TOK_TILE

    def body(h_ref, g_ref, p_ref, wg_ref, wp_ref, o_ref):
        x = h_ref[...]
        xh, _ = _rms_stats(x)
        s = _sigmoid(_dot((xh * g_ref[...]).astype(BF16), wg_ref[...]))
        o_ref[...] = x + _dot_nt(p_ref[...].astype(BF16), wp_ref[...]) * s

    row = lambda m: (m, 0)
    fixed = lambda m: (0, 0)
    return pl.pallas_call(
        body, name="ple_fwd",
        grid=(SEQ // tm,),
        in_specs=[pl.BlockSpec((tm, D_MODEL), row), pl.BlockSpec((1, D_MODEL), fixed), pl.BlockSpec((tm, PLE_DIM), row),
                  pl.BlockSpec((None, D_MODEL, D_MODEL), lambda m: (0, 0, 0)), pl.BlockSpec((D_MODEL, PLE_DIM), fixed)],
        out_specs=pl.BlockSpec((tm, D_MODEL), row),
        out_shape=jax.ShapeDtypeStruct((SEQ, D_MODEL), F32),
        compiler_params=_cparams(("parallel",)),
    )(*_in_hbm(h, g, p, wgate, wprojt))


def _ple_bwd(h, g, p, dh, wgate, wprojt):
    tm = TOK_TILE

    def body(h_ref, g_ref, p_ref, dh_ref, wg_ref, wp_ref, o_ref, u_ref, dq_ref, dpp_ref, pb_ref, dg_ref):
        xh, r = _rms_stats(h_ref[...])
        u = (xh * g_ref[...]).astype(BF16)
        s = _sigmoid(_dot(u, wg_ref[...]))
        p_bf = p_ref[...].astype(BF16)
        pp = _dot_nt(p_bf, wp_ref[...])
        dhv = dh_ref[...]
        dq = (dhv * pp * s * (1.0 - s)).astype(BF16)
        u_ref[0] = u
        dq_ref[0] = dq
        dpp_ref[0] = (dhv * s).astype(BF16)
        pb_ref[0] = p_bf
        dx, dg = _rms_bwd(_dot_nt(dq, wg_ref[...]), xh, r, g_ref[...])
        o_ref[...] = dhv + dx
        _accumulate(dg_ref, pl.program_id(0) == 0, dg)

    row = lambda m: (m, 0)
    fixed = lambda m: (0, 0)
    lead = lambda m: (0, m, 0)
    big = pltpu.HBM((1, SEQ, D_MODEL), BF16)
    return _out_hbm(pl.pallas_call(
        body, name="ple_bwd",
        grid=(SEQ // tm,),
        in_specs=[pl.BlockSpec((tm, D_MODEL), row), pl.BlockSpec((1, D_MODEL), fixed), pl.BlockSpec((tm, PLE_DIM), row),
                  pl.BlockSpec((tm, D_MODEL), row),
                  pl.BlockSpec((None, D_MODEL, D_MODEL), lambda m: (0, 0, 0)), pl.BlockSpec((D_MODEL, PLE_DIM), fixed)],
        out_specs=[pl.BlockSpec((tm, D_MODEL), row),
                   pl.BlockSpec((1, tm, D_MODEL), lead), pl.BlockSpec((1, tm, D_MODEL), lead),
                   pl.BlockSpec((1, tm, D_MODEL), lead), pl.BlockSpec((1, tm, PLE_DIM), lead),
                   pl.BlockSpec((1, D_MODEL), fixed)],
        out_shape=[jax.ShapeDtypeStruct((SEQ, D_MODEL), F32), big, big, big,
                   pltpu.HBM((1, SEQ, PLE_DIM), BF16),
                   jax.ShapeDtypeStruct((1, D_MODEL), F32)],
        compiler_params=_cparams(("arbitrary",)),
    )(*_in_hbm(h, g, p, dh, wgate, wprojt)), (1, 2, 3, 4))


def _final_loss(h, g, target):
    tm = TOK_TILE

    def body(h_ref, g_ref, t_ref, loss_ref, dh_ref, dg_ref):
        first = pl.program_id(0) == 0
        xh, r = _rms_stats(h_ref[...])
        diff = xh * g_ref[...] - t_ref[...]
        part = 0.5 * jnp.sum(jnp.mean(diff * diff, axis=-1, keepdims=True), axis=0, keepdims=True)
        _accumulate(loss_ref, first, jnp.broadcast_to(part, (SUBLANES, LANES)))
        dx, dg = _rms_bwd(diff * (1.0 / D_MODEL), xh, r, g_ref[...])
        dh_ref[...] = dx
        _accumulate(dg_ref, first, dg)

    row = lambda m: (m, 0)
    fixed = lambda m: (0, 0)
    return pl.pallas_call(
        body, name="final_loss",
        grid=(SEQ // tm,),
        in_specs=[pl.BlockSpec((tm, D_MODEL), row), pl.BlockSpec((1, D_MODEL), fixed),
                  pl.BlockSpec((tm, D_MODEL), row)],
        out_specs=[pl.BlockSpec((SUBLANES, LANES), fixed),
                   pl.BlockSpec((tm, D_MODEL), row),
                   pl.BlockSpec((1, D_MODEL), fixed)],
        out_shape=[jax.ShapeDtypeStruct((SUBLANES, LANES), F32),
                   jax.ShapeDtypeStruct((SEQ, D_MODEL), F32),
                   jax.ShapeDtypeStruct((1, D_MODEL), F32)],
        compiler_params=_cparams(("arbitrary",)),
    )(*_in_hbm(h, g, target))


def _disc(ar, ai, ldt):
    dt = jnp.exp(ldt)
    mag = jnp.exp(ar * dt)
    ph = ai * dt
    lr, li = mag * jnp.cos(ph), mag * jnp.sin(ph)
    nr, ni = lr - 1.0, li
    den = ar * ar + ai * ai
    return lr, li, (nr * ar + ni * ai) / den, (ni * ar - nr * ai) / den


def _s5_disc(a_row, ldt_row, a_rep, ldt_rep, bt, ct, tile_e, mask):
    n = N_STATE

    def body(ar_ref, lr_ref, ap_ref, lp_ref, b_ref, c_ref, e_ref, m_ref, lt_ref, ltr_ref, bb_ref, cc_ref):
        lr, li, _, _ = _disc(ar_ref[0], ar_ref[1], lr_ref[...])
        pr, pi = lr, li
        rid = _row_ids(SUBLANES, n)
        for k in range(1, 9):
            for ref, sgn, edge in ((lt_ref, 1.0, 24 + k - 1), (ltr_ref, -1.0, 24 + 8 - k)):
                if k in (1, 2, 4):
                    r0 = {1: 0, 2: 8, 4: 16}[k]
                    keep = (rid >= k) if ref is lt_ref else (rid < SUBLANES - k)
                    ref[r0:r0 + 8, 0:n] = jnp.where(keep, jnp.broadcast_to(pr, (8, n)), 0.0)
                    ref[r0:r0 + 8, n:2 * n] = jnp.where(keep, jnp.broadcast_to(sgn * pi, (8, n)), 0.0)
                ref[edge:edge + 1, 0:n] = pr
                ref[edge:edge + 1, n:2 * n] = sgn * pi
            pr, pi = pr * lr - pi * li, pr * li + pi * lr
        _, _, fr, fi = _disc(ap_ref[0], ap_ref[1], lp_ref[...])
        br, bi = b_ref[0], b_ref[1]
        e = e_ref[...]
        m = m_ref[...].astype(F32)
        bb_ref[:, 0:n] = (_dot((fr * br - fi * bi).astype(BF16), e) * m).astype(BF16)
        bb_ref[:, n:2 * n] = (_dot((fr * bi + fi * br).astype(BF16), e) * m).astype(BF16)
        cc_ref[:, 0:n] = (_dot(c_ref[0].astype(BF16), e) * m).astype(BF16)
        cc_ref[:, n:2 * n] = (-(_dot(c_ref[1].astype(BF16), e) * m)).astype(BF16)

    return pl.pallas_call(
        body, name="s5_disc",
        out_shape=[jax.ShapeDtypeStruct((32, 2 * n), F32), jax.ShapeDtypeStruct((32, 2 * n), F32),
                   jax.ShapeDtypeStruct((SSM_W, 2 * n), BF16), jax.ShapeDtypeStruct((SSM_W, 2 * n), BF16)],
        compiler_params=_cparams(None),
    )(a_row, ldt_row, a_rep, ldt_rep, bt, ct, tile_e, mask)


def _dot_exact(x, sel):
    hi = x.astype(BF16)
    r1 = x - hi.astype(F32)
    mid = r1.astype(BF16)
    lo = (r1 - mid.astype(F32)).astype(BF16)
    return _dot(hi, sel) + _dot(mid, sel) + _dot(lo, sel)


def _s5_disc_bwd(a, ldt, a_rep, ldt_rep, bt, mask, dl, d_bb, d_cc, fold):
    n = N_STATE

    def body(a_ref, l_ref, ap_ref, lp_ref, b_ref, m_ref, dl_ref, dbb_ref, dcc_ref, f_ref,
             da_ref, dldt_ref, db_ref, dc_ref):
        m = m_ref[...].astype(F32)
        fold_m = f_ref[...]
        diag = lambda x: _dot_exact(jnp.where(m > 0.0, x, 0.0), fold_m)
        dr, di = diag(dbb_ref[:, 0:n]), diag(dbb_ref[:, n:2 * n])
        dc_ref[0] = diag(dcc_ref[:, 0:n])
        dc_ref[1] = -diag(dcc_ref[:, n:2 * n])
        _, _, fr, fi = _disc(ap_ref[0], ap_ref[1], lp_ref[...])
        br, bi = b_ref[0], b_ref[1]
        db_ref[0] = fr * dr + fi * di
        db_ref[1] = fr * di - fi * dr
        per_state = lambda x: x.reshape(SSM_GROUPS, SSM_GROUP, SSM_STATE).sum(axis=1)
        dfr = per_state(dr * br + di * bi)
        dfi = per_state(di * br - dr * bi)
        _, vjp = jax.vjp(_disc, a_ref[0], a_ref[1], l_ref[...])
        dar, dai, dldt = vjp((dl_ref[0], dl_ref[1], dfr, dfi))
        da_ref[0] = dar
        da_ref[1] = dai
        dldt_ref[...] = jnp.sum(dldt, axis=1, keepdims=True)

    return pl.pallas_call(
        body, name="s5_disc_bwd",
        out_shape=[jax.ShapeDtypeStruct((2, SSM_GROUPS, SSM_STATE), F32),
                   jax.ShapeDtypeStruct((SSM_GROUPS, 1), F32),
                   jax.ShapeDtypeStruct((2, SSM_W, SSM_STATE), F32),
                   jax.ShapeDtypeStruct((2, SSM_W, SSM_STATE), F32)],
        compiler_params=_cparams(None),
    )(a, ldt, a_rep, ldt_rep, bt, mask, dl, d_bb, d_cc, fold)


def _row_block(rows, cap=512):
    for bm in range(min(cap, rows), 0, -1):
        if rows % bm == 0 and (bm % 8 == 0 or bm == rows):
            return bm
    return rows


def _pair_sum(fulls, got, segs):
    ns = len(segs)
    offs = _seg_offsets(segs)
    _, rtot, c = got.shape
    parts = 2
    pr = rtot // parts
    assert pr * parts == rtot and pr % 16 == 0
    pieces = [[] for _ in range(parts)]
    for a, (n, r) in enumerate(segs):
        for m in range(n):
            lo = offs[a] + m * r
            for h in range(parts):
                clo, chi = max(lo, h * pr), min(lo + r, (h + 1) * pr)
                if chi > clo:
                    pieces[h].append((a, m, clo - lo, clo - h * pr, chi - clo))
    n_sems = max(len(ps) for ps in pieces)

    def body(*refs):
        srcs = refs[:ns]
        got_ref, p32_ref, pbf_ref, own_v, sems = refs[ns:]
        h = pl.program_id(0)
        k = pl.program_id(1)
        dev = 2 * k + lax.axis_index("c")
        for hh in range(parts):
            @pl.when(h == hh)
            def _(hh=hh):
                cps = []
                for i, (a, m, so, do, rows) in enumerate(pieces[hh]):
                    start = pl.multiple_of(dev * segs[a][1] + so, 16)
                    cps.append(pltpu.make_async_copy(srcs[a].at[m, pl.ds(start, rows), :],
                                                     own_v.at[pl.ds(do, rows), :], sems.at[i]))
                for cp in cps:
                    cp.start()
                for cp in cps:
                    cp.wait()
        s = own_v[...].astype(F32) + got_ref[0].astype(F32)
        pbf_ref[0] = s.astype(BF16)

        @pl.when(k == 2 * lax.axis_index("x") + lax.axis_index("y"))
        def _():
            p32_ref[...] = s

    spec = pl.BlockSpec((1, pr, c), lambda h, k: (k, h, 0))
    return pl.pallas_call(
        body, name="pair_sum",
        grid=(parts, 4),
        in_specs=[HBM] * ns + [spec], out_specs=[pl.BlockSpec((pr, c), lambda h, k: (h, 0)), spec],
        out_shape=[pltpu.HBM((rtot, c), F32), pltpu.HBM(got.shape, BF16)],
        scratch_shapes=[pltpu.VMEM((pr, c), BF16), pltpu.SemaphoreType.DMA((n_sems,))],
        compiler_params=_cparams(("arbitrary", "arbitrary")),
    )(*_in_hbm(*fulls, got))


def _chip_sum(own, rb):
    r, c = own.shape
    bm = _row_block(r)

    def body(o_ref, r_ref, s_ref):
        s_ref[...] = ((o_ref[...] + r_ref[0].astype(F32)) + r_ref[1].astype(F32)) + r_ref[2].astype(F32)

    return pl.pallas_call(
        body, name="chip_sum",
        grid=(r // bm,),
        in_specs=[pl.BlockSpec((bm, c), lambda k: (k, 0)), pl.BlockSpec((3, bm, c), lambda k: (0, k, 0))],
        out_specs=pl.BlockSpec((bm, c), lambda k: (k, 0)),
        out_shape=jax.ShapeDtypeStruct((r, c), F32),
        compiler_params=_cparams(("parallel",)),
    )(*_in_hbm(own, rb))


def _adamw(w, g, m, v):
    r, c = w.shape
    bm = _row_block(r)
    bc1 = 1.0 - ADAM_B1 ** ADAM_STEP
    bc2 = 1.0 - ADAM_B2 ** ADAM_STEP

    def body(w_ref, g_ref, m_ref, v_ref, d_ref, nm_ref, nv_ref):
        gv = g_ref[...]
        nm = ADAM_B1 * m_ref[...] + (1.0 - ADAM_B1) * gv
        nv = ADAM_B2 * v_ref[...] + (1.0 - ADAM_B2) * (gv * gv)
        nm_ref[...] = nm
        nv_ref[...] = nv
        d_ref[...] = -ADAM_LR * ((nm / bc1) / (jnp.sqrt(nv / bc2) + ADAM_EPS) + ADAM_WD * w_ref[...])

    spec = pl.BlockSpec((bm, c), lambda k: (k, 0))
    shp = jax.ShapeDtypeStruct((r, c), F32)
    return pl.pallas_call(
        body, name="adamw",
        grid=(r // bm,),
        in_specs=[spec] * 4, out_specs=[spec] * 3, out_shape=[shp] * 3,
        compiler_params=_cparams(("parallel",)),
    )(*_in_hbm(w, g, m, v))


def _adamw_layers(w, m, v, g, first, prev):
    depth, r, c = w.shape
    nl = g.shape[0]
    bm = _row_block(r)
    bc1 = 1.0 - ADAM_B1 ** ADAM_STEP
    bc2 = 1.0 - ADAM_B2 ** ADAM_STEP

    def body(w_ref, m_ref, v_ref, g_ref, *refs):
        go_ref, d_ref, nm_ref, nv_ref = refs[-4:]
        gv = g_ref[...]
        nm = ADAM_B1 * m_ref[...] + (1.0 - ADAM_B1) * gv
        nv = ADAM_B2 * v_ref[...] + (1.0 - ADAM_B2) * (gv * gv)
        go_ref[...] = gv
        nm_ref[...] = nm
        nv_ref[...] = nv
        d_ref[...] = -ADAM_LR * ((nm / bc1) / (jnp.sqrt(nv / bc2) + ADAM_EPS) + ADAM_WD * w_ref[...])

    at = pl.BlockSpec((1, bm, c), lambda i, k: (first + i, k, 0))
    shp = jax.ShapeDtypeStruct((depth, r, c), F32)
    old = [] if prev is None else list(prev)
    return pl.pallas_call(
        body, name="adamw_layers",
        grid=(nl, r // bm),
        in_specs=[at, at, at, pl.BlockSpec((1, bm, c), lambda i, k: (i, k, 0))] + [HBM] * len(old),
        out_specs=[at] * 4, out_shape=[shp] * 4,
        input_output_aliases={4 + i: i for i in range(len(old))},
        compiler_params=_cparams(("parallel", "parallel")),
    )(*_in_hbm(w, m, v, g), *old)


def _mesh_pos():
    return lax.axis_index("x"), lax.axis_index("y"), lax.axis_index("c")


def _dev_index(p):
    return 4 * p[0] + 2 * p[1] + p[2]


def _seg_offsets(segs):
    offs, o = [], 0
    for n, r in segs:
        offs.append(o)
        o += n * r
    return offs


def _remote(src, dst, send_sem, recv_sem, to):
    return pltpu.make_async_remote_copy(src_ref=src, dst_ref=dst, send_sem=send_sem, recv_sem=recv_sem,
                                        device_id=to, device_id_type=MESH)


def _allgather(pack, segs, name):
    rtot, c = pack.shape
    ns = len(segs)
    offs = _seg_offsets(segs)
    assert rtot == sum(n * r for n, r in segs)

    def body(pack_ref, *refs):
        outs = refs[:ns]
        send_sems, recv_sems, local_sem = refs[ns:]
        x, y, cc = _mesh_pos()
        me, sib = (x, y, cc), (x, y, 1 - cc)
        chips = [(1 - x, y), (x, 1 - y), (1 - x, 1 - y)]

        def pieces(dev, from_pack):
            res = []
            for a, (n, r) in enumerate(segs):
                for m in range(n):
                    dst = outs[a].at[m, pl.ds(pl.multiple_of(dev * r, r), r), :]
                    src = pack_ref.at[pl.ds(offs[a] + m * r, r), :] if from_pack else dst
                    res.append((src, dst))
            return res

        def push(k, dev, to, from_pack):
            for s, d in pieces(dev, from_pack):
                _remote(s, d, send_sems.at[k], recv_sems.at[k], to).start()

        def whole(k):
            return _remote(pack_ref, pack_ref, send_sems.at[k], recv_sems.at[k], me)

        my_dev = _dev_index(me)
        for s, d in pieces(my_dev, True):
            pltpu.make_async_copy(s, d, local_sem).start()
        push(0, my_dev, sib, True)
        for j, chip in enumerate(chips):
            push(1 + j, my_dev, (*chip, cc), True)
        for j, chip in enumerate(chips):
            whole(1 + j).wait_recv()
            push(4 + j, _dev_index((*chip, cc)), sib, False)
        whole(0).wait_recv()
        for j in range(3):
            whole(4 + j).wait_recv()
        for k in range(7):
            whole(k).wait_send()
        pltpu.make_async_copy(pack_ref, pack_ref, local_sem).wait()

    return pl.pallas_call(
        body, name=name,
        in_specs=[HBM], out_specs=[HBM] * ns,
        out_shape=[jax.ShapeDtypeStruct((n, N_DEV * r, c), pack.dtype) for n, r in segs],
        scratch_shapes=[pltpu.SemaphoreType.DMA((7,)), pltpu.SemaphoreType.DMA((7,)), pltpu.SemaphoreType.DMA],
    )(pack)


HBM = pl.BlockSpec(memory_space=pltpu.HBM)
SEM = pl.BlockSpec(memory_space=pltpu.SEMAPHORE)
VMEM_WHOLE = pl.BlockSpec(memory_space=pltpu.VMEM)
EFFECT = pltpu.SideEffectType.DATAFLOW_SIDE_EFFECTING


def _hbm(a):
    return pltpu.with_memory_space_constraint(a, pltpu.HBM)


def _ag_start(pack, segs, after, name):
    rtot, c = pack.shape
    ns = len(segs)
    offs = _seg_offsets(segs)

    def body(pack_ref, *refs):
        lands = refs[:ns]
        send_sems, recv_sems = refs[ns + 1], refs[ns + 2]
        token = refs[-1]
        x, y, cc = _mesh_pos()
        my_dev = _dev_index((x, y, cc))
        targets = [(x, y, 1 - cc), (1 - x, y, cc), (x, 1 - y, cc), (1 - x, 1 - y, cc)]
        for k, to in enumerate(targets):
            for a, (n, r) in enumerate(segs):
                for m in range(n):
                    _remote(pack_ref.at[pl.ds(offs[a] + m * r, r), :],
                            lands[a].at[m, pl.ds(pl.multiple_of(my_dev * r, r), r), :],
                            send_sems.at[k], recv_sems.at[k], to).start()
        token[...] = jnp.zeros_like(token)

    land_shapes = [(n, N_DEV * r, c) for n, r in segs]
    outs = pl.pallas_call(
        body, name=name,
        in_specs=[HBM] * (1 + ns) + [UNREAD],
        out_specs=[SEM, SEM, HBM] + [HBM] * ns + [VMEM_WHOLE],
        out_shape=[pltpu.SemaphoreType.DMA((4,)), pltpu.SemaphoreType.DMA((4,)), pltpu.HBM(pack.shape, pack.dtype)]
        + [pltpu.HBM(s, pack.dtype) for s in land_shapes] + [jax.ShapeDtypeStruct((SUBLANES, LANES), F32)],
        input_output_aliases={0: 2, **{1 + i: 3 + i for i in range(ns)}},
        compiler_params=pltpu.CompilerParams(has_side_effects=EFFECT),
    )(_hbm(pack), *[_hbm(lax.empty(s, pack.dtype)) for s in land_shapes], _hbm(after))
    return outs[0], outs[1], outs[2], list(outs[3:3 + ns]), outs[-1]


def _ag_wait(send_sems, recv_sems, pack, lands, after, name):
    ns = len(lands)

    def body(pack_ref, *refs):
        send_ref, recv_ref = refs[ns], refs[ns + 1]
        me = _mesh_pos()
        for k in range(4):
            whole = _remote(pack_ref, pack_ref, send_ref.at[k], recv_ref.at[k], me)
            whole.wait_send()
            whole.wait_recv()

    outs = pl.pallas_call(
        body, name=name,
        in_specs=[HBM] * (1 + ns) + [SEM, SEM, UNREAD],
        out_specs=[HBM] * (1 + ns),
        out_shape=[pltpu.HBM(pack.shape, pack.dtype)] + [pltpu.HBM(a.shape, a.dtype) for a in lands],
        input_output_aliases={i: i for i in range(1 + ns)},
        compiler_params=pltpu.CompilerParams(has_side_effects=EFFECT),
    )(pack, *lands, send_sems, recv_sems, _hbm(after))
    return outs[0], list(outs[1:])


def _ag_finish(pack, lands, segs):
    rtot, c = pack.shape
    ns = len(segs)
    offs = _seg_offsets(segs)

    def body(pack_ref, *refs):
        outs = refs[ns:2 * ns]
        stage, send_sems, recv_sems, local_sems = refs[2 * ns:]
        x, y, cc = _mesh_pos()
        me, sib = (x, y, cc), (x, y, 1 - cc)
        chips = [(1 - x, y), (x, 1 - y), (1 - x, 1 - y)]

        def rows(a, m, dev):
            return outs[a].at[m, pl.ds(pl.multiple_of(dev * segs[a][1], segs[a][1]), segs[a][1]), :]

        for j, chip in enumerate(chips):
            dev = _dev_index((*chip, cc))
            for a, (n, r) in enumerate(segs):
                for m in range(n):
                    _remote(rows(a, m, dev), rows(a, m, dev), send_sems.at[j], recv_sems.at[j], sib).start()
        load = pltpu.make_async_copy(pack_ref, stage, local_sems.at[0])
        load.start()
        load.wait()
        my_dev = _dev_index(me)
        for a, (n, r) in enumerate(segs):
            for m in range(n):
                pltpu.make_async_copy(stage.at[pl.ds(offs[a] + m * r, r), :], rows(a, m, my_dev), local_sems.at[1]).start()
        pltpu.make_async_copy(stage, pack_ref, local_sems.at[1]).wait()
        for j in range(3):
            _remote(pack_ref, pack_ref, send_sems.at[j], recv_sems.at[j], me).wait()

    outs = pl.pallas_call(
        body, name="ag_finish",
        in_specs=[HBM] * (1 + ns), out_specs=[HBM] * ns,
        out_shape=[pltpu.HBM(a.shape, a.dtype) if r >= 128 else jax.ShapeDtypeStruct(a.shape, a.dtype)
                   for a, (_, r) in zip(lands, segs)],
        input_output_aliases={1 + i: i for i in range(ns)},
        scratch_shapes=[pltpu.VMEM((rtot, c), pack.dtype), pltpu.SemaphoreType.DMA((3,)),
                        pltpu.SemaphoreType.DMA((3,)), pltpu.SemaphoreType.DMA((2,))],
        compiler_params=_cparams(None, 16),
    )(pack, *lands)
    return list(outs)


def _rs_chips_start(pbf, after, name):
    _, rtot, c = pbf.shape

    def body(pbf_ref, land_ref, after_ref, send_sems, recv_sems, pbf_thru, land_thru, token):
        x, y, cc = _mesh_pos()
        for j, (cx, cy) in enumerate([(1 - x, y), (x, 1 - y), (1 - x, 1 - y)]):
            _remote(pbf_ref.at[2 * cx + cy], land_ref.at[j], send_sems.at[j], recv_sems.at[j], (cx, cy, cc)).start()
        token[...] = jnp.zeros_like(token)

    return pl.pallas_call(
        body, name=name,
        in_specs=[HBM, HBM, UNREAD],
        out_specs=[SEM, SEM, HBM, HBM, VMEM_WHOLE],
        out_shape=[pltpu.SemaphoreType.DMA((3,)), pltpu.SemaphoreType.DMA((3,)), pltpu.HBM(pbf.shape, pbf.dtype),
                   pltpu.HBM((3, rtot, c), pbf.dtype), jax.ShapeDtypeStruct((SUBLANES, LANES), F32)],
        input_output_aliases={0: 2, 1: 3},
        compiler_params=pltpu.CompilerParams(has_side_effects=EFFECT),
    )(_hbm(pbf), _hbm(lax.empty((3, rtot, c), pbf.dtype)), _hbm(after))


def _rs_chips_wait(send_sems, recv_sems, pbf, land, after, name):
    def body(pbf_ref, land_ref, send_ref, recv_ref, after_ref, pbf_out, land_out):
        me = _mesh_pos()
        for j in range(3):
            cp = _remote(pbf_ref.at[0], land_ref.at[j], send_ref.at[j], recv_ref.at[j], me)
            cp.wait_send()
            cp.wait_recv()

    return pl.pallas_call(
        body, name=name,
        in_specs=[HBM, HBM, SEM, SEM, UNREAD], out_specs=[HBM, HBM],
        out_shape=[pltpu.HBM(pbf.shape, pbf.dtype), pltpu.HBM(land.shape, land.dtype)],
        input_output_aliases={0: 0, 1: 1},
        compiler_params=pltpu.CompilerParams(has_side_effects=EFFECT),
    )(pbf, land, send_sems, recv_sems, _hbm(after))[1]


def _flips():
    return [(dx, dy, dc) for dx in (0, 1) for dy in (0, 1) for dc in (0, 1) if dx or dy or dc]


def _small_gather_start(flat, name):
    r, c = flat.shape

    def body(flat_ref, land_ref, send_sems, recv_sems, flat_thru, land_thru, token):
        x, y, cc = _mesh_pos()
        mine = land_ref.at[_dev_index((x, y, cc))]
        for k, (dx, dy, dc) in enumerate(_flips()):
            to = (1 - x if dx else x, 1 - y if dy else y, 1 - cc if dc else cc)
            _remote(flat_ref, mine, send_sems.at[k], recv_sems.at[k], to).start()
        token[...] = jnp.zeros_like(token)

    return pl.pallas_call(
        body, name=name,
        in_specs=[HBM, HBM],
        out_specs=[SEM, SEM, HBM, HBM, VMEM_WHOLE],
        out_shape=[pltpu.SemaphoreType.DMA((7,)), pltpu.SemaphoreType.DMA((7,)), pltpu.HBM(flat.shape, flat.dtype),
                   pltpu.HBM((N_DEV, r, c), flat.dtype), jax.ShapeDtypeStruct((SUBLANES, LANES), F32)],
        input_output_aliases={0: 2, 1: 3},
        compiler_params=pltpu.CompilerParams(has_side_effects=EFFECT),
    )(_hbm(flat), _hbm(lax.empty((N_DEV, r, c), flat.dtype)))


def _small_gather_wait(send_sems, recv_sems, flat, land, after, name):
    def body(flat_ref, land_ref, send_ref, recv_ref, after_ref, flat_out, land_out):
        me = _mesh_pos()
        for k in range(N_DEV - 1):
            cp = _remote(flat_ref, land_ref.at[0], send_ref.at[k], recv_ref.at[k], me)
            cp.wait_send()
            cp.wait_recv()

    return pl.pallas_call(
        body, name=name,
        in_specs=[HBM, HBM, SEM, SEM, UNREAD], out_specs=[HBM, HBM],
        out_shape=[pltpu.HBM(flat.shape, flat.dtype), pltpu.HBM(land.shape, land.dtype)],
        input_output_aliases={0: 0, 1: 1},
        compiler_params=pltpu.CompilerParams(has_side_effects=EFFECT),
    )(flat, land, send_sems, recv_sems, _hbm(after))


def _sum_devices(land, own):
    _, r, c = land.shape

    def body(land_ref, own_ref, out_ref):
        me = _dev_index(_mesh_pos())
        total = None
        for d in range(N_DEV):
            other = land_ref[jnp.where(d == me, (d + 1) % N_DEV, d)]
            block = jnp.where(d == me, own_ref[...], other)
            total = block if total is None else total + block
        out_ref[...] = total

    return pl.pallas_call(
        body, name="sum_devices",
        grid=(1,),
        in_specs=[pl.BlockSpec((N_DEV, r, c), lambda i: (0, 0, 0)), pl.BlockSpec((r, c), lambda i: (0, 0))],
        out_specs=pl.BlockSpec((r, c), lambda i: (0, 0)),
        out_shape=jax.ShapeDtypeStruct((r, c), F32),
        compiler_params=_cparams(("arbitrary",)),
    )(land, own)


def _rs_sibling_start(fulls, segs, name):
    ns = len(segs)
    offs = _seg_offsets(segs)
    rtot = sum(n * r for n, r in segs)
    c = fulls[0].shape[-1]
    dt = fulls[0].dtype

    def body(*refs):
        srcs = refs[:ns]
        land_ref, send_sem, recv_sem = refs[ns], refs[ns + 1], refs[ns + 2]
        token = refs[-1]
        x, y, cc = _mesh_pos()
        for k in range(4):
            for a, (n, r) in enumerate(segs):
                for m in range(n):
                    theirs = srcs[a].at[m, pl.ds(pl.multiple_of((2 * k + 1 - cc) * r, r), r), :]
                    _remote(theirs, land_ref.at[k, pl.ds(offs[a] + m * r, r), :], send_sem, recv_sem,
                            (x, y, 1 - cc)).start()
        token[...] = jnp.zeros_like(token)

    outs = pl.pallas_call(
        body, name=name,
        in_specs=[HBM] * (ns + 1),
        out_specs=[SEM, SEM] + [HBM] * (ns + 1) + [VMEM_WHOLE],
        out_shape=[pltpu.SemaphoreType.DMA(()), pltpu.SemaphoreType.DMA(())]
        + [pltpu.HBM(a.shape, a.dtype) for a in fulls] + [pltpu.HBM((4, rtot, c), dt),
                                                           jax.ShapeDtypeStruct((SUBLANES, LANES), F32)],
        input_output_aliases={i: 2 + i for i in range(ns + 1)},
        compiler_params=pltpu.CompilerParams(has_side_effects=EFFECT),
    )(*[_hbm(a) for a in fulls], _hbm(lax.empty((4, rtot, c), dt)))
    return outs[0], outs[1], list(outs[2:2 + ns]), outs[2 + ns], outs[-1]


def _rs_sibling_wait(send_sem, recv_sem, fulls, land, after, name):
    ns = len(fulls)

    def body(*refs):
        land_ref, send_ref, recv_ref = refs[ns], refs[ns + 1], refs[ns + 2]
        whole = _remote(land_ref, land_ref, send_ref, recv_ref, _mesh_pos())
        whole.wait_send()
        whole.wait_recv()

    outs = pl.pallas_call(
        body, name=name,
        in_specs=[HBM] * (ns + 1) + [SEM, SEM, UNREAD], out_specs=[HBM] * (ns + 1),
        out_shape=[pltpu.HBM(a.shape, a.dtype) for a in fulls] + [pltpu.HBM(land.shape, land.dtype)],
        input_output_aliases={i: i for i in range(ns + 1)},
        compiler_params=pltpu.CompilerParams(has_side_effects=EFFECT),
    )(*fulls, land, send_sem, recv_sem, _hbm(after))
    return list(outs[:ns]), outs[ns]


def _tp(w):
    return jnp.swapaxes(w, -1, -2)


def _s5_prepare(a_re, a_im, log_dt, b_re, b_im, c_re, c_im):
    a = jnp.stack([a_re, a_im], axis=1)
    ldt = jnp.broadcast_to(log_dt[:, :, None], (DEPTH, SSM_GROUPS, SSM_STATE))
    a_row = a.reshape(DEPTH, 2, 1, N_STATE)
    ldt_row = ldt.reshape(DEPTH, 1, N_STATE)
    a_rep = jnp.repeat(a, SSM_GROUP, axis=2)
    ldt_rep = jnp.repeat(ldt, SSM_GROUP, axis=1)
    bt = jnp.stack([_tp(b_re), _tp(b_im)], axis=1).reshape(DEPTH, 2, SSM_W, SSM_STATE)
    ct = jnp.stack([c_re, c_im], axis=1).reshape(DEPTH, 2, SSM_W, SSM_STATE)
    tile_e = jnp.tile(jnp.eye(SSM_STATE, dtype=BF16), (1, SSM_GROUPS))
    mask = jnp.repeat(jnp.repeat(jnp.eye(SSM_GROUPS, dtype=BF16), SSM_GROUP, axis=0), SSM_STATE, axis=1)
    out = []
    for l in range(DEPTH):
        tabs = _s5_disc(a_row[l], ldt_row[l], a_rep[l], ldt_rep[l], bt[l], ct[l], tile_e, mask)
        out.append(((a[l], ldt[l], a_rep[l], ldt_rep[l], bt[l], mask), *tabs))
    return out


def _layer_fwd(h, p_l, small, big, arrive=None):
    saved = {'h0': h}
    if arrive is not None:
        arrive(0, h)
    h, saved['gu1'] = _ffn_fwd(h, small['ffn1_norm'], big['ff1'])
    saved['h1'] = h
    if arrive is not None:
        arrive(1, h)
    z = _inproj_fwd(h, small['mix_norm'], big['wint'])
    ya, ys, hs = _s5conv_fwd(z, small['conv_w'], small['conv_b'], small['bbmat'], small['ccmat'], small['dvec'],
                             small['ltab'])
    saved.update(z=z, ya=ya, ys=ys, hs=hs)
    h = _mix_out_fwd(h, ya, ys, big['glu'], small['glu_b'], small['conv_out_norm'], small['ssm_out_norm'], big['wout'])
    saved['h2'] = h
    if arrive is not None:
        arrive(2, h)
    h, saved['gu2'] = _ffn_fwd(h, small['ffn2_norm'], big['ff2'])
    saved['h3'] = h
    h = _ple_fwd(h, small['ple_norm'], p_l, big['plg'], big['plpt'])
    return h, saved


def _ffn_bwd(h_in, g, dh, gu, w3):
    dh_in, dga, ud, dg = _ffn_bwd_act(h_in, g, dh, gu, w3)
    return dh_in, _matmul_tn(dga, ud, FF_BLOCK, BF16, "ffn_wgrad"), dg


def _layer_bwd_top(dh, p_l, small, big, saved):
    gs = {}
    dh, u, dq, dpp, pb, gs['ple_norm'] = _ple_bwd(saved['h3'], small['ple_norm'], p_l, dh, big['plg'], big['plpt'])
    d_plg = _matmul_tn(u, dq, 256, BF16, "ple_gate_wgrad")
    d_plpt = _matmul_tn(dpp, pb, 256, BF16, "ple_proj_wgrad", to_kernel=False)
    dh, d_ff2, gs['ffn2_norm'] = _ffn_bwd(saved['h2'], small['ffn2_norm'], dh, saved['gu2'], big['ff2'])
    return dh, (gs, d_plg, d_plpt, d_ff2)


def _layer_bwd_rest(dh, top, small, big, saved):
    gs, d_plg, d_plpt, d_ff2 = top
    dya, dys, ycat, dhb, zg, dq, part = _mix_out_bwd(dh, saved['ya'], saved['ys'], big['glu'], small['glu_b'],
                                                     small['conv_out_norm'], small['ssm_out_norm'], big['wout'])
    d_wout = _matmul_tn(ycat, dhb, 256, BF16, "w_out_wgrad")
    d_glu = _matmul_tn(zg, dq, 256, BF16, "glu_wgrad", to_kernel=False)
    dz, gadj, us, dyb, dl, dcw = _s5conv_bwd(saved['z'], saved['hs'], dya, dys, small['conv_w'], small['conv_b'],
                                             small['bbmat'], small['ccmat'], small['dvec'], small['ltab_rev'])
    d_bb = _block_wgrad(us, gadj, "s5_b_wgrad")
    d_cc = _block_wgrad(dyb, saved['hs'][None], "s5_c_wgrad")
    dh, u, gs['mix_norm'] = _inproj_bwd(saved['h1'], small['mix_norm'], dh, dz, big['wint'])
    d_wint = _matmul_tn(dz[None], u, 256, BF16, "w_in_wgrad")
    dh, d_ff1, gs['ffn1_norm'] = _ffn_bwd(saved['h0'], small['ffn1_norm'], dh, saved['gu1'], big['ff1'])

    dlb = dl[0].reshape(2, SSM_GROUPS, SSM_STATE)
    fold = jnp.tile(jnp.eye(SSM_STATE, dtype=BF16), (SSM_GROUPS, 1))
    da, dldt, dbt, dct = _s5_disc_bwd(*small['disc_in'], dlb, d_bb, d_cc, fold)
    gs['ssm_A_re'], gs['ssm_A_im'] = da[0], da[1]
    gs['ssm_log_dt'] = dldt[:, 0]
    ghp = (SSM_GROUPS, SSM_GROUP, SSM_STATE)
    gs['ssm_B_re'], gs['ssm_B_im'] = dbt[0].reshape(ghp), dbt[1].reshape(ghp)
    gs['ssm_C_re'], gs['ssm_C_im'] = dct[0].reshape(ghp), dct[1].reshape(ghp)
    gs['conv_w'] = dcw[0:3]
    gs['conv_b'] = dcw[3]
    gs['ssm_D'] = dcw[4].reshape(SSM_GROUPS, SSM_GROUP)
    gs['conv_out_norm'], gs['ssm_out_norm'], gs['glu_b'] = part[0], part[1], part[2]
    for n in ('ple_norm', 'ffn2_norm', 'mix_norm', 'ffn1_norm'):
        gs[n] = gs[n][0]
    fulls = [d_ff1, d_ff2, d_wint, d_wout, d_plg,
             d_plpt.reshape(1, D_MODEL * PLE_DIM // D_MODEL, D_MODEL), d_glu.reshape(1, SSM_W * SSM_W // D_MODEL, D_MODEL)]
    return dh, fulls, gs


VIEW_T = ('ffn1_w_gate', 'ffn1_w_up', 'ffn2_w_gate', 'ffn2_w_up', 'ssm_B_re', 'ssm_B_im')


def _view(name, a):
    return _tp(a) if name in VIEW_T else a


SEG_NAMES = ('ff1', 'ff2', 'wint', 'wout', 'plg', 'plpt', 'glu')
FIRST_LAYER_GROUPS = ((0,), (2, 3, 6), (1, 4, 5))


def _layer_pack(W, l, segments=range(len(SEGS))):
    pieces = {
        0: lambda: [_tp(W['ffn1_w_gate'][l]), _tp(W['ffn1_w_up'][l]), W['ffn1_w_down'][l]],
        1: lambda: [_tp(W['ffn2_w_gate'][l]), _tp(W['ffn2_w_up'][l]), W['ffn2_w_down'][l]],
        2: lambda: [_tp(W['w_in'][l])],
        3: lambda: [W['w_out'][l]],
        4: lambda: [W['ple_w_gate'][l]],
        5: lambda: [_tp(W['ple_w_proj'][l]).reshape(-1, D_MODEL)],
        6: lambda: [W['glu_w'][l].reshape(-1, D_MODEL)],
    }
    return jnp.concatenate([a for s in segments for a in pieces[s]()], axis=0).astype(BF16)


def _as_big(named):
    shape = dict(plpt=(D_MODEL, PLE_DIM), glu=(SSM_W, SSM_W))
    return {n: (a.reshape(shape[n]) if n in shape else a) for n, a in named.items()}


def _pad_rows(flat, mult, width=LANES):
    per = mult * width
    n = flat.shape[0]
    tot = -(-n // per) * per
    return jnp.pad(flat, (0, tot - n)).reshape(tot // width, width)


def _adamw_any(w, g, m, v):
    shp = w.shape
    two = (lambda t: t.reshape(-1, shp[-1]))
    d, nm, nv = _adamw(two(w), two(g), two(m), two(v))
    return d.reshape(shp), nm.reshape(shp), nv.reshape(shp)


def kernel(x, p, ffn1_norm, ffn1_w_gate, ffn1_w_up, ffn1_w_down, mix_norm, w_in, conv_w, conv_b, ssm_A_re, ssm_A_im, ssm_B_re, ssm_B_im, ssm_C_re, ssm_C_im, ssm_D, ssm_log_dt, glu_w, glu_b, conv_out_norm, ssm_out_norm, w_out, ffn2_norm, ffn2_w_gate, ffn2_w_up, ffn2_w_down, ple_norm, ple_w_gate, ple_w_proj, final_norm, loss_target, m_ffn1_norm, m_ffn1_w_gate, m_ffn1_w_up, m_ffn1_w_down, m_mix_norm, m_w_in, m_conv_w, m_conv_b, m_ssm_A_re, m_ssm_A_im, m_ssm_B_re, m_ssm_B_im, m_ssm_C_re, m_ssm_C_im, m_ssm_D, m_ssm_log_dt, m_glu_w, m_glu_b, m_conv_out_norm, m_ssm_out_norm, m_w_out, m_ffn2_norm, m_ffn2_w_gate, m_ffn2_w_up, m_ffn2_w_down, m_ple_norm, m_ple_w_gate, m_ple_w_proj, m_final_norm, v_ffn1_norm, v_ffn1_w_gate, v_ffn1_w_up, v_ffn1_w_down, v_mix_norm, v_w_in, v_conv_w, v_conv_b, v_ssm_A_re, v_ssm_A_im, v_ssm_B_re, v_ssm_B_im, v_ssm_C_re, v_ssm_C_im, v_ssm_D, v_ssm_log_dt, v_glu_w, v_glu_b, v_conv_out_norm, v_ssm_out_norm, v_w_out, v_ffn2_norm, v_ffn2_w_gate, v_ffn2_w_up, v_ffn2_w_down, v_ple_norm, v_ple_w_gate, v_ple_w_proj, v_final_norm):
    given = dict(locals())
    W = {n: given[n] for n in W_NAMES}
    M = {n: given['m_' + n] for n in W_NAMES}
    V = {n: given['v_' + n] for n in W_NAMES}
    Wv, Mv, Vv = [{n: _view(n, d[n]) for n in W_NAMES} for d in (W, M, V)]
    my_dev = _dev_index(_mesh_pos())

    conv_shard = _pad_rows(W['conv_w'].reshape(-1), SUBLANES)
    conv_all = _allgather(conv_shard, ((1, SUBLANES),), "ag_conv_w")[0]
    conv_full = conv_all.reshape(N_DEV, -1)[:, :DEPTH * 3 * (CONV_W // N_DEV)]
    conv_full = conv_full.reshape(N_DEV, DEPTH, 3, CONV_W // N_DEV).transpose(1, 2, 0, 3).reshape(DEPTH, 3, CONV_W)
    first, after = [], conv_all
    for gi, segments in enumerate(FIRST_LAYER_GROUPS):
        first.append(_ag_start(_layer_pack(W, 0, segments), tuple(SEGS[s] for s in segments), after,
                               "ag_start_0%s" % "abc"[gi]))
        after = first[-1][4]
    s5 = _s5_prepare(*[W[n] + after[0, 0] for n in ('ssm_A_re', 'ssm_A_im', 'ssm_log_dt')],
                     *[W[n] for n in ('ssm_B_re', 'ssm_B_im', 'ssm_C_re', 'ssm_C_im')])
    packs = [None] + [_layer_pack(W, l) for l in range(1, DEPTH)]
    prepared = conv_full[0, 0:1, 0:1] + s5[DEPTH - 1][1][0:1, 0:1] + packs[DEPTH - 1][0:1, 0:1].astype(F32)

    smalls, saves, bigs = [], [], []
    h = x[0]

    flight = None

    def gathered(handles, segments, after, name, next_layer=None, gate=None):
        nonlocal flight
        send_sems, recv_sems, pack_thru, lands, _ = handles
        pack_thru, lands = _ag_wait(send_sems, recv_sems, pack_thru, lands, after, "ag_wait_" + name)
        if next_layer is not None:
            flight = _ag_start(packs[next_layer], SEGS, pack_thru, "ag_start_%d" % next_layer)
            gate[0][gate[1]] = gate[0][gate[1]] + flight[4][0:1, 0:1]
        outs = _ag_finish(pack_thru, lands, tuple(SEGS[s] for s in segments))
        return _as_big({SEG_NAMES[s]: a for s, a in zip(segments, outs)})

    for l in range(DEPTH):
        small = {n: W[n][l][None] for n in ('ffn1_norm', 'mix_norm', 'conv_b', 'glu_b', 'conv_out_norm',
                                            'ssm_out_norm', 'ffn2_norm', 'ple_norm')}
        small['conv_w'] = conv_full[l]
        small['dvec'] = W['ssm_D'][l].reshape(1, SSM_W)
        small['disc_in'], small['ltab'], small['ltab_rev'], small['bbmat'], small['ccmat'] = s5[l]
        big = {}
        bigs.append(big)
        if l == 0:
            def arrive(stage, h_now, big=big, small=small):
                big.update(gathered(first[stage], FIRST_LAYER_GROUPS[stage], prepared if stage == 0 else h_now,
                                    "0%s" % "abc"[stage], *((1, (small, 'ffn2_norm')) if stage == 2 else ())))
            h, saved = _layer_fwd(h, p[l, 0], small, big, arrive)
        else:
            nxt = (l + 1, (small, 'ffn1_norm')) if l + 1 < DEPTH else ()
            big.update(gathered(flight, range(len(SEGS)), h, "%d" % l, *nxt))
            h, saved = _layer_fwd(h, p[l, 0], small, big)
        smalls.append(small)
        saves.append(saved)
    loss_tile, dh, d_final = _final_loss(h, W['final_norm'][None], loss_target[0])
    loss = lax.psum(loss_tile[0, 0], ("x", "y", "c"))

    layer_gs = [None] * DEPTH
    shard_grads = [None] * DEPTH
    zero = jnp.zeros((1, 1), F32)
    sib, ici = None, None

    def finish_sibling(after_sib, after_ici):
        nonlocal sib, ici
        up, (send_sem, recv_sem, fulls_thru, land, _) = sib
        fulls_thru, got = _rs_sibling_wait(send_sem, recv_sem, fulls_thru, land, after_sib, "sib_wait_%d" % up)
        own32, pbf = _pair_sum(fulls_thru, got, SEGS)
        done = finish_chips(own32)
        ici = (up, _rs_chips_start(pbf, after_ici if done is None else done, "rs_start_%d" % up), own32)
        sib = None

    def finish_chips(after):
        nonlocal ici
        if ici is None:
            return None
        up, (send_sems, recv_sems, pbf_thru, land, _), own32 = ici
        got3 = _rs_chips_wait(send_sems, recv_sems, pbf_thru, land, after, "rs_wait_%d" % up)
        shard_grads[up] = _chip_sum(own32, got3)
        ici = None
        return shard_grads[up]

    layer_names = [n for n in SMALL_NAMES if n != 'final_norm']
    small_flights = [None] * DEPTH
    for l in reversed(range(DEPTH)):
        small = dict(smalls[l])
        if sib is not None:
            small['ple_norm'] = small['ple_norm'] + sib[1][4][0:1, 0:1] + small_flights[l + 1][4][0:1, 0:1]
        dh, top = _layer_bwd_top(dh, p[l, 0], small, bigs[l], saves[l])
        if sib is not None:
            finish_sibling(dh, dh)
            small['glu_b'] = small['glu_b'] + ici[1][4][0:1, 0:1]
        dh, fulls, layer_gs[l] = _layer_bwd_rest(dh, top, small, bigs[l], saves[l])
        sib = (l, _rs_sibling_start(fulls, SEGS, "sib_start_%d" % l))
        last_slot = d_final[0] if l == DEPTH - 1 else jnp.zeros((D_MODEL,), F32)
        flat = jnp.concatenate([layer_gs[l][n].reshape(-1) for n in layer_names + ['conv_w']] + [last_slot])
        small_flights[l] = _small_gather_start(_pad_rows(flat, SUBLANES, D_MODEL), "small_start_%d" % l)
    grad_x = dh[None]
    finish_sibling(small_flights[0][4], small_flights[0][4])

    reduced = []
    for l in range(DEPTH):
        send_sems, recv_sems, flat_thru, land, _ = small_flights[l]
        flat_thru, land = _small_gather_wait(send_sems, recv_sems, flat_thru, land, ici[1][4], "small_wait_%d" % l)
        reduced.append(_sum_devices(land, flat_thru).reshape(-1))
    reduced = jnp.stack(reduced)
    G = {}
    o = 0
    for n in layer_names + ['conv_w']:
        size = (W[n].size if n != 'conv_w' else DEPTH * 3 * CONV_W) // DEPTH
        shape = Wv[n].shape if n != 'conv_w' else (DEPTH, 3, CONV_W)
        G[n] = reduced[:, o:o + size].reshape(shape)
        o += size
    G['final_norm'] = reduced[DEPTH - 1, o:o + D_MODEL]
    G['conv_w'] = lax.dynamic_slice_in_dim(G['conv_w'], my_dev * (CONV_W // N_DEV), CONV_W // N_DEV, axis=2)

    delta, new_m, new_v = {}, {}, {}
    for n in SMALL_NAMES + ['conv_w']:
        two = (lambda t: t.reshape(1, -1) if t.ndim == 1 else t)
        delta[n], new_m[n], new_v[n] = [t.reshape(Wv[n].shape) for t in
                                        _adamw_any(two(Wv[n]), two(G[n]), two(Mv[n]), two(Vv[n]))]

    def unpack(sg):
        nl = sg.shape[0]
        offs = _seg_offsets(SEGS)
        r = SEGS[0][1]
        out = {}
        for a, f in ((0, 'ffn1'), (1, 'ffn2')):
            out[f + '_w_gate'] = sg[:, offs[a]:offs[a] + r]
            out[f + '_w_up'] = sg[:, offs[a] + r:offs[a] + 2 * r]
            out[f + '_w_down'] = sg[:, offs[a] + 2 * r:offs[a] + 3 * r]
        out['w_in'] = _tp(sg[:, offs[2]:offs[2] + SEGS[2][1]])
        out['w_out'] = sg[:, offs[3]:offs[3] + SEGS[3][1]]
        out['ple_w_gate'] = sg[:, offs[4]:offs[4] + SEGS[4][1]]
        out['ple_w_proj'] = _tp(sg[:, offs[5]:offs[5] + SEGS[5][1]].reshape(nl, D_MODEL // N_DEV, PLE_DIM))
        out['glu_w'] = sg[:, offs[6]:offs[6] + SEGS[6][1]].reshape(nl, SSM_W // N_DEV, SSM_W)
        return out

    upper = unpack(jnp.stack(shard_grads[1:]))
    part = {n: _adamw_layers(Wv[n], Mv[n], Vv[n], upper[n], 1, None) for n in upper}
    finish_chips(sum(part[n][3][1, 0:1, 0:1] for n in part))
    lower = unpack(shard_grads[0][None])
    for n in lower:
        G[n], delta[n], new_m[n], new_v[n] = _adamw_layers(Wv[n], Mv[n], Vv[n], lower[n], 0, part[n])

    outs = [[_view(n, d[n]) for n in W_NAMES] for d in (G, delta, new_m, new_v)]
    return (loss, grad_x, *outs[0], *outs[1], *outs[2], *outs[3])
```

```python
import math

import jax
import jax.numpy as jnp
from jax import lax
from jax.experimental import pallas as pl
from jax.experimental.pallas import tpu as pltpu

F32 = jnp.float32
BF16 = jnp.bfloat16

N_DEV = 8
DEPTH = 4
SEQ = 2048
D_MODEL = 1024
D_FF = 2816
CONV_W = 512
SSM_W = 512
SSM_GROUPS = 32
SSM_GROUP = 16
SSM_STATE = 64
N_STATE = SSM_GROUPS * SSM_STATE
IN_COLS = 2048
PLE_DIM = 256
EPS = 1e-6

ADAM_LR = 0.001
ADAM_B1 = 0.9
ADAM_B2 = 0.999
ADAM_EPS = 1e-08
ADAM_WD = 0.01
ADAM_STEP = 10

FF_BLOCK = 256
N_FF_BLOCKS = D_FF // FF_BLOCK
TOK_TILE_FFN_FWD = 2048
TOK_TILE_FFN_BWD = 1024
TOK_TILE = 256
CHUNK = 256
N_CHUNKS = SEQ // CHUNK
LANE_GROUP = 512
SUBLANES = 8
LANES = 128
MIB = 1024 * 1024

W_NAMES = ['ffn1_norm', 'ffn1_w_gate', 'ffn1_w_up', 'ffn1_w_down', 'mix_norm', 'w_in', 'conv_w', 'conv_b',
           'ssm_A_re', 'ssm_A_im', 'ssm_B_re', 'ssm_B_im', 'ssm_C_re', 'ssm_C_im', 'ssm_D', 'ssm_log_dt',
           'glu_w', 'glu_b', 'conv_out_norm', 'ssm_out_norm', 'w_out', 'ffn2_norm', 'ffn2_w_gate', 'ffn2_w_up',
           'ffn2_w_down', 'ple_norm', 'ple_w_gate', 'ple_w_proj', 'final_norm']
SMALL_NAMES = ['ffn1_norm', 'mix_norm', 'conv_b', 'ssm_A_re', 'ssm_A_im', 'ssm_B_re', 'ssm_B_im', 'ssm_C_re',
               'ssm_C_im', 'ssm_D', 'ssm_log_dt', 'glu_b', 'conv_out_norm', 'ssm_out_norm', 'ffn2_norm',
               'ple_norm', 'final_norm']

SEGS = ((3, 352), (3, 352), (1, 256), (1, 128), (1, 128), (1, 32), (1, 32))
PACK_ROWS = sum(n * r for n, r in SEGS)

MESH = pl.DeviceIdType.MESH
UNREAD = pl.BlockSpec(memory_space=pltpu.HBM)


def _in_hbm(*arrays):
    return [pltpu.with_memory_space_constraint(a, pltpu.HBM) for a in arrays]


def _out_hbm(outs, which):
    if not isinstance(outs, (list, tuple)):
        return pltpu.with_memory_space_constraint(outs, pltpu.HBM) if which else outs
    return [pltpu.with_memory_space_constraint(a, pltpu.HBM) if i in which else a for i, a in enumerate(outs)]


def _cparams(sem=None, vmem_mib=48, **kw):
    return pltpu.CompilerParams(dimension_semantics=sem, vmem_limit_bytes=vmem_mib * MIB, **kw)


def _dot(a, b):
    return jnp.dot(a, b, preferred_element_type=F32)


def _dot_nt(a, b):
    return lax.dot_general(a, b, (((1,), (1,)), ((), ())), preferred_element_type=F32)


def _dot_tn(a, b):
    return lax.dot_general(a, b, (((0,), (0,)), ((), ())), preferred_element_type=F32)


def _rms_stats(x):
    r = lax.rsqrt(jnp.mean(x * x, axis=-1, keepdims=True) + EPS)
    return x * r, r


def _rms_bwd(dy, xh, r, g):
    dxh = dy * g
    dx = r * (dxh - xh * jnp.mean(dxh * xh, axis=-1, keepdims=True))
    dg = jnp.sum(dy * xh, axis=0, keepdims=True)
    return dx, dg


def _sigmoid(x):
    return 0.5 * jnp.tanh(0.5 * x) + 0.5


_GELU_C = math.sqrt(2.0 / math.pi)


def _gelu(x):
    t = jnp.tanh(_GELU_C * (x + 0.044715 * x * x * x))
    return 0.5 * x * (1.0 + t), t


def _gelu_grad(x, t):
    return 0.5 * (1.0 + t) + 0.5 * x * (1.0 - t * t) * _GELU_C * (1.0 + 3.0 * 0.044715 * x * x)


def _accumulate(ref, first, value):
    @pl.when(first)
    def _():
        ref[...] = value

    @pl.when(jnp.logical_not(first))
    def _():
        ref[...] += value


def _ffn_fwd(h, g, w3):
    tm = TOK_TILE_FFN_FWD
    last = N_FF_BLOCKS - 1

    def body(h_ref, g_ref, wgu_ref, wd_ref, wd_last_ref, out_ref, gu_ref, u_ref, a_ref):
        k = pl.program_id(1)

        @pl.when(k == 0)
        def _():
            x = h_ref[...]
            xh, _ = _rms_stats(x)
            u_ref[...] = (xh * g_ref[...]).astype(BF16)
            out_ref[...] = x
            a_ref[1] = jnp.zeros((tm, FF_BLOCK), BF16)

        out_ref[...] += 0.5 * _dot(a_ref[(k + 1) % 2], wd_ref[0])
        gu = _dot_nt(u_ref[...], wgu_ref[...].reshape(2 * FF_BLOCK, D_MODEL))
        gate, up = gu[:, :FF_BLOCK], gu[:, FF_BLOCK:]
        a_ref[k % 2] = (gate * _sigmoid(gate) * up).astype(BF16)
        gu_ref[0] = gate.astype(BF16)
        gu_ref[1] = up.astype(BF16)

        @pl.when(k == last)
        def _():
            out_ref[...] += 0.5 * _dot(a_ref[last % 2], wd_last_ref[0])

    return _out_hbm(pl.pallas_call(
        body, name="ffn_fwd",
        grid=(SEQ // tm, N_FF_BLOCKS),
        in_specs=[pl.BlockSpec((tm, D_MODEL), lambda m, k: (m, 0), pipeline_mode=pl.Buffered(1)),
                  pl.BlockSpec((1, D_MODEL), lambda m, k: (0, 0)),
                  pl.BlockSpec((2, FF_BLOCK, D_MODEL), lambda m, k: (0, k, 0)),
                  pl.BlockSpec((1, FF_BLOCK, D_MODEL), lambda m, k: (2, jnp.maximum(k - 1, 0), 0)),
                  pl.BlockSpec((1, FF_BLOCK, D_MODEL), lambda m, k: (2, last, 0), pipeline_mode=pl.Buffered(1))],
        out_specs=[pl.BlockSpec((tm, D_MODEL), lambda m, k: (m, 0)),
                   pl.BlockSpec((2, tm, FF_BLOCK), lambda m, k: (0, m, k))],
        out_shape=[jax.ShapeDtypeStruct((SEQ, D_MODEL), F32),
                   pltpu.HBM((2, SEQ, D_FF), BF16)],
        scratch_shapes=[pltpu.VMEM((tm, D_MODEL), BF16), pltpu.VMEM((2, tm, FF_BLOCK), BF16)],
        compiler_params=_cparams(("parallel", "arbitrary"), 56),
    )(*_in_hbm(h, g, w3, w3, w3)), (1,))


def _ffn_bwd_act(h, g, dout, gu, w3):
    tm = TOK_TILE_FFN_BWD
    last = N_FF_BLOCKS - 1

    def body(h_ref, g_ref, d_ref, gu_ref, wd_ref, wgu_ref, wgu_last_ref, dh_ref, dga_ref, ud_ref, dg_ref,
             acc_ref, dgu_ref):
        m = pl.program_id(0)
        k = pl.program_id(1)

        @pl.when(k == 0)
        def _():
            xh, _ = _rms_stats(h_ref[...])
            ud_ref[0] = (xh * g_ref[...]).astype(BF16)
            ud_ref[1] = (0.5 * d_ref[...]).astype(BF16)
            acc_ref[...] = jnp.zeros_like(acc_ref)
            dgu_ref[1] = jnp.zeros((tm, 2 * FF_BLOCK), BF16)

        acc_ref[...] += _dot(dgu_ref[(k + 1) % 2], wgu_ref[...].reshape(2 * FF_BLOCK, D_MODEL))
        gate = gu_ref[0].astype(F32)
        up = gu_ref[1].astype(F32)
        sg = _sigmoid(gate)
        silu = gate * sg
        da = _dot_nt(ud_ref[1], wd_ref[0])
        dgate = (da * up * (sg + silu * (1.0 - sg))).astype(BF16)
        dup = (da * silu).astype(BF16)
        dga_ref[0] = dgate
        dga_ref[1] = dup
        dga_ref[2] = (silu * up).astype(BF16)
        dgu_ref[k % 2, :, 0:FF_BLOCK] = dgate
        dgu_ref[k % 2, :, FF_BLOCK:2 * FF_BLOCK] = dup

        @pl.when(k == last)
        def _():
            du = acc_ref[...] + _dot(dgu_ref[last % 2], wgu_last_ref[...].reshape(2 * FF_BLOCK, D_MODEL))
            xh, r = _rms_stats(h_ref[...])
            dx, dg = _rms_bwd(du, xh, r, g_ref[...])
            dh_ref[...] = d_ref[...] + dx
            _accumulate(dg_ref, m == 0, dg)

    return _out_hbm(pl.pallas_call(
        body, name="ffn_bwd_act",
        grid=(SEQ // tm, N_FF_BLOCKS),
        in_specs=[pl.BlockSpec((tm, D_MODEL), lambda m, k: (m, 0), pipeline_mode=pl.Buffered(1)),
                  pl.BlockSpec((1, D_MODEL), lambda m, k: (0, 0)),
                  pl.BlockSpec((tm, D_MODEL), lambda m, k: (m, 0), pipeline_mode=pl.Buffered(1)),
                  pl.BlockSpec((2, tm, FF_BLOCK), lambda m, k: (0, m, k)),
                  pl.BlockSpec((1, FF_BLOCK, D_MODEL), lambda m, k: (2, k, 0)),
                  pl.BlockSpec((2, FF_BLOCK, D_MODEL), lambda m, k: (0, jnp.maximum(k - 1, 0), 0)),
                  pl.BlockSpec((2, FF_BLOCK, D_MODEL), lambda m, k: (0, last, 0), pipeline_mode=pl.Buffered(1))],
        out_specs=[pl.BlockSpec((tm, D_MODEL), lambda m, k: (m, 0)),
                   pl.BlockSpec((3, tm, FF_BLOCK), lambda m, k: (0, m, k)),
                   pl.BlockSpec((2, tm, D_MODEL), lambda m, k: (0, m, 0)),
                   pl.BlockSpec((1, D_MODEL), lambda m, k: (0, 0))],
        out_shape=[jax.ShapeDtypeStruct((SEQ, D_MODEL), F32),
                   pltpu.HBM((3, SEQ, D_FF), BF16),
                   pltpu.HBM((2, SEQ, D_MODEL), BF16),
                   jax.ShapeDtypeStruct((1, D_MODEL), F32)],
        scratch_shapes=[pltpu.VMEM((tm, D_MODEL), F32), pltpu.VMEM((2, tm, 2 * FF_BLOCK), BF16)],
        compiler_params=_cparams(("arbitrary", "arbitrary"), 56),
    )(*_in_hbm(h, g, dout, gu, w3, w3, w3)), (1, 2))


def _matmul_tn(a, b, bm, out_dtype, name, bn=None, to_kernel=True):
    na, t, m = a.shape
    nb, _, n = b.shape
    bn = n if bn is None else bn

    def body(a_ref, b_ref, o_ref):
        o_ref[0] = _dot_tn(a_ref[0], b_ref[0]).astype(out_dtype)

    return _out_hbm(pl.pallas_call(
        body, name=name,
        grid=(na, m // bm, n // bn),
        in_specs=[pl.BlockSpec((1, t, bm), lambda i, k, j: (i, 0, k)),
                  pl.BlockSpec((1, t, bn), lambda i, k, j: (jnp.maximum(i - (na - nb), 0), 0, j))],
        out_specs=pl.BlockSpec((1, bm, bn), lambda i, k, j: (i, k, j)),
        out_shape=pltpu.HBM((na, m, n), out_dtype) if to_kernel else jax.ShapeDtypeStruct((na, m, n), out_dtype),
        compiler_params=_cparams(("arbitrary", "parallel", "parallel")),
    )(*_in_hbm(a, b)), to_kernel)


def _inproj_fwd(h, g, wint):
    tm = TOK_TILE

    def body(h_ref, g_ref, w_ref, z_ref):
        xh, _ = _rms_stats(h_ref[...])
        z_ref[...] = _dot_nt((xh * g_ref[...]).astype(BF16), w_ref[...])

    return pl.pallas_call(
        body, name="inproj_fwd",
        grid=(SEQ // tm,),
        in_specs=[pl.BlockSpec((tm, D_MODEL), lambda m: (m, 0)),
                  pl.BlockSpec((1, D_MODEL), lambda m: (0, 0)),
                  pl.BlockSpec((None, IN_COLS, D_MODEL), lambda m: (0, 0, 0))],
        out_specs=pl.BlockSpec((tm, IN_COLS), lambda m: (m, 0)),
        out_shape=jax.ShapeDtypeStruct((SEQ, IN_COLS), F32),
        compiler_params=_cparams(("parallel",)),
    )(*_in_hbm(h, g, wint))


def _inproj_bwd(h, g, dh, dz, wint):
    tm = TOK_TILE

    def body(h_ref, g_ref, dh_ref, dz_ref, w_ref, o_ref, u_ref, dg_ref):
        xh, r = _rms_stats(h_ref[...])
        u_ref[0] = (xh * g_ref[...]).astype(BF16)
        dx, dg = _rms_bwd(_dot(dz_ref[...], w_ref[...]), xh, r, g_ref[...])
        o_ref[...] = dh_ref[...] + dx
        _accumulate(dg_ref, pl.program_id(0) == 0, dg)

    return _out_hbm(pl.pallas_call(
        body, name="inproj_bwd",
        grid=(SEQ // tm,),
        in_specs=[pl.BlockSpec((tm, D_MODEL), lambda m: (m, 0)),
                  pl.BlockSpec((1, D_MODEL), lambda m: (0, 0)),
                  pl.BlockSpec((tm, D_MODEL), lambda m: (m, 0)),
                  pl.BlockSpec((tm, IN_COLS), lambda m: (m, 0)),
                  pl.BlockSpec((None, IN_COLS, D_MODEL), lambda m: (0, 0, 0))],
        out_specs=[pl.BlockSpec((tm, D_MODEL), lambda m: (m, 0)),
                   pl.BlockSpec((1, tm, D_MODEL), lambda m: (0, m, 0)),
                   pl.BlockSpec((1, D_MODEL), lambda m: (0, 0))],
        out_shape=[jax.ShapeDtypeStruct((SEQ, D_MODEL), F32),
                   pltpu.HBM((1, SEQ, D_MODEL), BF16),
                   jax.ShapeDtypeStruct((1, D_MODEL), F32)],
        compiler_params=_cparams(("arbitrary",)),
    )(*_in_hbm(h, g, dh, dz, wint)), (1,))


def _row_ids(n, w):
    return lax.broadcasted_iota(jnp.int32, (n, w), 0)


def _bcast_row(x, i, n):
    return jnp.broadcast_to(x[i:i + 1, :], (n, x.shape[1]))


def _conv_taps(v, tail):
    n, w = v.shape
    rid = _row_ids(n, w)
    v1 = jnp.where(rid == 0, _bcast_row(tail, 7, n), pltpu.roll(v, 1, 0))
    v2 = jnp.where(rid == 0, _bcast_row(tail, 6, n),
                   jnp.where(rid == 1, _bcast_row(tail, 7, n), pltpu.roll(v, 2, 0)))
    return v1, v2


def _block_tiles():
    half_rows, half_cols = SSM_W // 2, N_STATE // 2
    for half in range(2):
        for part in range(2):
            yield (slice(half * half_rows, (half + 1) * half_rows),
                   slice(part * N_STATE + half * half_cols, part * N_STATE + (half + 1) * half_cols))


def _block_expand(x, mat_ref, out_ref):
    for rows, cols in _block_tiles():
        out_ref[:, cols] = _dot(x[:, rows], mat_ref[rows, cols])


def _block_contract(s, mat_ref):
    halves = {}
    for rows, cols in _block_tiles():
        part = _dot_nt(s[:, cols], mat_ref[rows, cols])
        halves[rows.start] = part if rows.start not in halves else halves[rows.start] + part
    return jnp.concatenate([halves[k] for k in sorted(halves)], axis=1)


def _block_wgrad(a, b, name):
    t = a.shape[1]
    half_rows, half_cols = SSM_W // 2, N_STATE // 2

    def body(a_ref, b_ref, o_ref):
        o_ref[...] = _dot_tn(a_ref[...], b_ref[...])

    return pl.pallas_call(
        body, name=name,
        grid=(2, 2),
        in_specs=[pl.BlockSpec((None, t, half_rows), lambda h, p: (0, 0, h)),
                  pl.BlockSpec((None, t, half_cols), lambda h, p: (0, 0, 2 * p + h))],
        out_specs=pl.BlockSpec((half_rows, half_cols), lambda h, p: (h, 2 * p + h)),
        out_shape=jax.ShapeDtypeStruct((SSM_W, 2 * N_STATE), F32),
        compiler_params=_cparams(("parallel", "parallel")),
    )(*_in_hbm(a, b))


def _scan_chunk(work, ltab, carry, reverse):
    nblk = CHUNK // SUBLANES
    for gi in range(N_STATE // LANE_GROUP):
        cre = pl.ds(gi * LANE_GROUP, LANE_GROUP)
        cim = pl.ds(N_STATE + gi * LANE_GROUP, LANE_GROUP)
        pows = [(ltab[8 * k:8 * k + 8, cre], ltab[8 * k:8 * k + 8, cim]) for k in range(3)]
        pr = ltab[24:32, cre]
        pi = ltab[24:32, cim]

        def blk(i, c, cre=cre, cim=cim, pows=pows, pr=pr, pi=pi):
            cr, ci = c
            b = (nblk - 1 - i) if reverse else i
            r0 = pl.multiple_of(b * SUBLANES, SUBLANES)
            xr = work[pl.ds(r0, SUBLANES), cre]
            xi = work[pl.ds(r0, SUBLANES), cim]
            for k, s in enumerate((1, 2, 4)):
                lr, li = pows[k]
                shift = SUBLANES - s if reverse else s
                sr = pltpu.roll(xr, shift, 0)
                si = pltpu.roll(xi, shift, 0)
                xr, xi = xr + lr * sr - li * si, xi + lr * si + li * sr
            xr, xi = xr + pr * cr - pi * ci, xi + pr * ci + pi * cr
            work[pl.ds(r0, SUBLANES), cre] = xr
            work[pl.ds(r0, SUBLANES), cim] = xi
            edge = 0 if reverse else SUBLANES - 1
            return _bcast_row(xr, edge, SUBLANES), _bcast_row(xi, edge, SUBLANES)

        cr, ci = lax.fori_loop(0, nblk, blk, (carry[:, cre], carry[:, cim]))
        carry[:, cre] = cr
        carry[:, cim] = ci


def _s5conv_fwd(z, convw, convb, bbmat, ccmat, dvec, ltab):
    def body(z_ref, cw_ref, cb_ref, bb_ref, cc_ref, d_ref, lt_ref, ya_ref, ys_ref, hs_ref,
             work, carry, tail):
        c = pl.program_id(0)

        @pl.when(c == 0)
        def _():
            carry[...] = jnp.zeros_like(carry)
            tail[...] = jnp.zeros_like(tail)

        zb = z_ref[:, 0:CONV_W]
        v = z_ref[:, CONV_W:2 * CONV_W] * z_ref[:, 2 * CONV_W:3 * CONV_W]
        us = z_ref[:, 3 * CONV_W:4 * CONV_W]
        v1, v2 = _conv_taps(v, tail[...])
        tail[...] = v[CHUNK - 8:CHUNK, :]
        y = cw_ref[0:1, :] * v2 + cw_ref[1:2, :] * v1 + cw_ref[2:3, :] * v
        ya_ref[...] = zb * (y + cb_ref[...])

        _block_expand(us.astype(BF16), bb_ref, work)
        _scan_chunk(work, lt_ref, carry, reverse=False)
        hs = work[...].astype(BF16)
        hs_ref[...] = hs
        ys_ref[...] = _block_contract(hs, cc_ref) + d_ref[...] * us

    return _out_hbm(pl.pallas_call(
        body, name="s5conv_fwd",
        grid=(N_CHUNKS,),
        in_specs=[pl.BlockSpec((CHUNK, IN_COLS), lambda c: (c, 0)),
                  pl.BlockSpec((3, CONV_W), lambda c: (0, 0)),
                  pl.BlockSpec((1, CONV_W), lambda c: (0, 0)),
                  pl.BlockSpec((SSM_W, 2 * N_STATE), lambda c: (0, 0)),
                  pl.BlockSpec((SSM_W, 2 * N_STATE), lambda c: (0, 0)),
                  pl.BlockSpec((1, SSM_W), lambda c: (0, 0)),
                  pl.BlockSpec((32, 2 * N_STATE), lambda c: (0, 0))],
        out_specs=[pl.BlockSpec((CHUNK, CONV_W), lambda c: (c, 0)),
                   pl.BlockSpec((CHUNK, SSM_W), lambda c: (c, 0)),
                   pl.BlockSpec((CHUNK, 2 * N_STATE), lambda c: (c, 0))],
        out_shape=[pltpu.HBM((SEQ, CONV_W), F32),
                   pltpu.HBM((SEQ, SSM_W), F32),
                   jax.ShapeDtypeStruct((SEQ, 2 * N_STATE), BF16)],
        scratch_shapes=[pltpu.VMEM((CHUNK, 2 * N_STATE), F32),
                        pltpu.VMEM((8, 2 * N_STATE), F32),
                        pltpu.VMEM((8, CONV_W), F32)],
        compiler_params=_cparams(("arbitrary",)),
    )(*_in_hbm(z, convw, convb, bbmat, ccmat, dvec, ltab)), (0, 1))


def _s5conv_bwd(z, hs, dya, dys, convw, convb, bbmat, ccmat, dvec, ltab_rev):
    nc = N_CHUNKS
    hb = 16

    def body(z_ref, zp_ref, hs_ref, hp_ref, dya_ref, dys_ref, cw_ref, cb_ref, bb_ref, cc_ref, d_ref, lt_ref,
             dz_ref, g_ref, us_ref, dyb_ref, dl_ref, dcw_ref, work, carry, head):
        i = pl.program_id(0)
        first_chunk = i == nc - 1

        @pl.when(i == 0)
        def _():
            carry[...] = jnp.zeros_like(carry)
            head[...] = jnp.zeros_like(head)
            dl_ref[...] = jnp.zeros_like(dl_ref)
            dcw_ref[...] = jnp.zeros_like(dcw_ref)

        us = z_ref[:, 3 * CONV_W:4 * CONV_W]
        dy = dys_ref[...]
        dy_bf = dy.astype(BF16)
        us_ref[0] = us.astype(BF16)
        dyb_ref[0] = dy_bf

        _block_expand(dy_bf, cc_ref, work)
        _scan_chunk(work, lt_ref, carry, reverse=True)
        gg = work[...]
        gg_bf = gg.astype(BF16)
        g_ref[0] = gg_bf
        dus = d_ref[...] * dy + _block_contract(gg_bf, bb_ref)

        hcur = hs_ref[...].astype(F32)
        hlast = hp_ref[...].astype(F32)[hb - 1:hb, :]
        hlast = jnp.where(first_chunk, 0.0, hlast)
        rid = _row_ids(CHUNK, 2 * N_STATE)
        hprev = jnp.where(rid == 0, jnp.broadcast_to(hlast, (CHUNK, 2 * N_STATE)), pltpu.roll(hcur, 1, 0))
        gr, gi = gg[:, :N_STATE], gg[:, N_STATE:]
        hr, hi = hprev[:, :N_STATE], hprev[:, N_STATE:]
        dl_ref[:, :N_STATE] += (gr * hr + gi * hi).reshape(CHUNK // 8, 8, N_STATE).sum(axis=0)
        dl_ref[:, N_STATE:] += (gi * hr - gr * hi).reshape(CHUNK // 8, 8, N_STATE).sum(axis=0)

        @pl.when(i == nc - 1)
        def _():
            dl_ref[0:1, :] = jnp.sum(dl_ref[...], axis=0, keepdims=True)

        zb = z_ref[:, 0:CONV_W]
        zc = z_ref[:, CONV_W:2 * CONV_W]
        zv = z_ref[:, 2 * CONV_W:3 * CONV_W]
        v = zc * zv
        vtail = jnp.where(first_chunk, 0.0, zp_ref[:, CONV_W:2 * CONV_W] * zp_ref[:, 2 * CONV_W:3 * CONV_W])
        v1, v2 = _conv_taps(v, vtail)
        w0, w1, w2 = cw_ref[0:1, :], cw_ref[1:2, :], cw_ref[2:3, :]
        y = w0 * v2 + w1 * v1 + w2 * v
        dya_v = dya_ref[...]
        dzb = dya_v * (y + cb_ref[...])
        dyc = dya_v * zb
        hd = head[...]
        rc = _row_ids(CHUNK, CONV_W)
        n1 = jnp.where(rc == CHUNK - 1, _bcast_row(hd, 0, CHUNK), pltpu.roll(dyc, CHUNK - 1, 0))
        n2 = jnp.where(rc == CHUNK - 1, _bcast_row(hd, 1, CHUNK),
                       jnp.where(rc == CHUNK - 2, _bcast_row(hd, 0, CHUNK), pltpu.roll(dyc, CHUNK - 2, 0)))
        head[...] = dyc[0:8, :]
        dv = w2 * dyc + w1 * n1 + w0 * n2
        dz_ref[:, 0:CONV_W] = dzb.astype(BF16)
        dz_ref[:, CONV_W:2 * CONV_W] = (dv * zv).astype(BF16)
        dz_ref[:, 2 * CONV_W:3 * CONV_W] = (dv * zc).astype(BF16)
        dz_ref[:, 3 * CONV_W:4 * CONV_W] = dus.astype(BF16)
        dcw_ref[0:1, :] += jnp.sum(dyc * v2, axis=0, keepdims=True)
        dcw_ref[1:2, :] += jnp.sum(dyc * v1, axis=0, keepdims=True)
        dcw_ref[2:3, :] += jnp.sum(dyc * v, axis=0, keepdims=True)
        dcw_ref[3:4, :] += jnp.sum(dyc, axis=0, keepdims=True)
        dcw_ref[4:5, :] += jnp.sum(dy * us, axis=0, keepdims=True)

    rev = lambda i: nc - 1 - i
    return _out_hbm(pl.pallas_call(
        body, name="s5conv_bwd",
        grid=(nc,),
        in_specs=[pl.BlockSpec((CHUNK, IN_COLS), lambda i: (rev(i), 0)),
                  pl.BlockSpec((8, IN_COLS), lambda i: (jnp.maximum(rev(i) * (CHUNK // 8) - 1, 0), 0)),
                  pl.BlockSpec((CHUNK, 2 * N_STATE), lambda i: (rev(i), 0)),
                  pl.BlockSpec((hb, 2 * N_STATE), lambda i: (jnp.maximum(rev(i) * (CHUNK // hb) - 1, 0), 0)),
                  pl.BlockSpec((CHUNK, CONV_W), lambda i: (rev(i), 0)),
                  pl.BlockSpec((CHUNK, SSM_W), lambda i: (rev(i), 0)),
                  pl.BlockSpec((3, CONV_W), lambda i: (0, 0)),
                  pl.BlockSpec((1, CONV_W), lambda i: (0, 0)),
                  pl.BlockSpec((SSM_W, 2 * N_STATE), lambda i: (0, 0)),
                  pl.BlockSpec((SSM_W, 2 * N_STATE), lambda i: (0, 0)),
                  pl.BlockSpec((1, SSM_W), lambda i: (0, 0)),
                  pl.BlockSpec((32, 2 * N_STATE), lambda i: (0, 0))],
        out_specs=[pl.BlockSpec((CHUNK, IN_COLS), lambda i: (rev(i), 0)),
                   pl.BlockSpec((1, CHUNK, 2 * N_STATE), lambda i: (0, rev(i), 0)),
                   pl.BlockSpec((1, CHUNK, SSM_W), lambda i: (0, rev(i), 0)),
                   pl.BlockSpec((1, CHUNK, SSM_W), lambda i: (0, rev(i), 0)),
                   pl.BlockSpec((8, 2 * N_STATE), lambda i: (0, 0)),
                   pl.BlockSpec((8, CONV_W), lambda i: (0, 0))],
        out_shape=[jax.ShapeDtypeStruct((SEQ, IN_COLS), BF16),
                   pltpu.HBM((1, SEQ, 2 * N_STATE), BF16),
                   pltpu.HBM((1, SEQ, SSM_W), BF16),
                   pltpu.HBM((1, SEQ, SSM_W), BF16),
                   jax.ShapeDtypeStruct((8, 2 * N_STATE), F32),
                   jax.ShapeDtypeStruct((8, CONV_W), F32)],
        scratch_shapes=[pltpu.VMEM((CHUNK, 2 * N_STATE), F32),
                        pltpu.VMEM((8, 2 * N_STATE), F32),
                        pltpu.VMEM((8, CONV_W), F32)],
        compiler_params=_cparams(("arbitrary",)),
    )(*_in_hbm(z, z, hs, hs, dya, dys, convw, convb, bbmat, ccmat, dvec, ltab_rev)), (1, 2, 3))


def _mix_out_fwd(h, ya, ys, gluw, glub, con, son, wout):
    tm = TOK_TILE

    def body(h_ref, ya_ref, ys_ref, gw_ref, gb_ref, con_ref, son_ref, wo_ref, o_ref):
        zg, _ = _gelu(ys_ref[...])
        q = _dot(zg.astype(BF16), gw_ref[...]) + gb_ref[...]
        out_s = zg * _sigmoid(q)
        na, _ = _rms_stats(ya_ref[...])
        ns, _ = _rms_stats(out_s)
        o_ref[...] = (h_ref[...]
                      + _dot((na * con_ref[...]).astype(BF16), wo_ref[0:CONV_W, :])
                      + _dot((ns * son_ref[...]).astype(BF16), wo_ref[CONV_W:2 * CONV_W, :]))

    row = lambda m: (m, 0)
    fixed = lambda m: (0, 0)
    return pl.pallas_call(
        body, name="mix_out_fwd",
        grid=(SEQ // tm,),
        in_specs=[pl.BlockSpec((tm, D_MODEL), row), pl.BlockSpec((tm, CONV_W), row), pl.BlockSpec((tm, SSM_W), row),
                  pl.BlockSpec((SSM_W, SSM_W), fixed), pl.BlockSpec((1, SSM_W), fixed),
                  pl.BlockSpec((1, CONV_W), fixed), pl.BlockSpec((1, SSM_W), fixed),
                  pl.BlockSpec((None, D_MODEL, D_MODEL), lambda m: (0, 0, 0))],
        out_specs=pl.BlockSpec((tm, D_MODEL), row),
        out_shape=jax.ShapeDtypeStruct((SEQ, D_MODEL), F32),
        compiler_params=_cparams(("parallel",)),
    )(*_in_hbm(h, ya, ys, gluw, glub, con, son, wout))


def _mix_out_bwd(dh, ya, ys, gluw, glub, con, son, wout):
    tm = TOK_TILE

    def body(dh_ref, ya_ref, ys_ref, gw_ref, gb_ref, con_ref, son_ref, wo_ref,
             dya_ref, dys_ref, yc_ref, dhb_ref, zg_ref, dq_ref, part_ref):
        ysv = ys_ref[...]
        zg, th = _gelu(ysv)
        zg_bf = zg.astype(BF16)
        s = _sigmoid(_dot(zg_bf, gw_ref[...]) + gb_ref[...])
        out_s = zg * s
        na, ra = _rms_stats(ya_ref[...])
        ns, rs = _rms_stats(out_s)
        dh_bf = dh_ref[...].astype(BF16)
        yc_ref[0, :, 0:CONV_W] = (na * con_ref[...]).astype(BF16)
        yc_ref[0, :, CONV_W:2 * CONV_W] = (ns * son_ref[...]).astype(BF16)
        dhb_ref[0] = dh_bf
        dca = _dot_nt(dh_bf, wo_ref[0:CONV_W, :])
        dcs = _dot_nt(dh_bf, wo_ref[CONV_W:2 * CONV_W, :])
        dya, dcon = _rms_bwd(dca, na, ra, con_ref[...])
        dos, dson = _rms_bwd(dcs, ns, rs, son_ref[...])
        dya_ref[...] = dya
        dq = dos * zg * s * (1.0 - s)
        dq_bf = dq.astype(BF16)
        dzg = dos * s + _dot_nt(dq_bf, gw_ref[...])
        dys_ref[...] = dzg * _gelu_grad(ysv, th)
        zg_ref[0] = zg_bf
        dq_ref[0] = dq_bf
        rid = _row_ids(SUBLANES, SSM_W)
        part = jnp.zeros((SUBLANES, SSM_W), F32)
        for i, rowv in enumerate((dcon, dson, jnp.sum(dq, axis=0, keepdims=True))):
            part = jnp.where(rid == i, jnp.broadcast_to(rowv, (SUBLANES, SSM_W)), part)
        _accumulate(part_ref, pl.program_id(0) == 0, part)

    row = lambda m: (m, 0)
    fixed = lambda m: (0, 0)
    lead = lambda m: (0, m, 0)
    return _out_hbm(pl.pallas_call(
        body, name="mix_out_bwd",
        grid=(SEQ // tm,),
        in_specs=[pl.BlockSpec((tm, D_MODEL), row), pl.BlockSpec((tm, CONV_W), row), pl.BlockSpec((tm, SSM_W), row),
                  pl.BlockSpec((SSM_W, SSM_W), fixed), pl.BlockSpec((1, SSM_W), fixed),
                  pl.BlockSpec((1, CONV_W), fixed), pl.BlockSpec((1, SSM_W), fixed),
                  pl.BlockSpec((None, D_MODEL, D_MODEL), lambda m: (0, 0, 0))],
        out_specs=[pl.BlockSpec((tm, CONV_W), row), pl.BlockSpec((tm, SSM_W), row),
                   pl.BlockSpec((1, tm, D_MODEL), lead), pl.BlockSpec((1, tm, D_MODEL), lead),
                   pl.BlockSpec((1, tm, SSM_W), lead), pl.BlockSpec((1, tm, SSM_W), lead),
                   pl.BlockSpec((8, SSM_W), fixed)],
        out_shape=[pltpu.HBM((SEQ, CONV_W), F32), pltpu.HBM((SEQ, SSM_W), F32),
                   pltpu.HBM((1, SEQ, D_MODEL), BF16), pltpu.HBM((1, SEQ, D_MODEL), BF16),
                   pltpu.HBM((1, SEQ, SSM_W), BF16), pltpu.HBM((1, SEQ, SSM_W), BF16),
                   jax.ShapeDtypeStruct((8, SSM_W), F32)],
        compiler_params=_cparams(("arbitrary",)),
    )(*_in_hbm(dh, ya, ys, gluw, glub, con, son, wout)), (0, 1, 2, 3, 4, 5))


def _ple_fwd(h, g, p, wgate, wprojt):
    tm = TOK_TILE

    def body(h_ref, g_ref, p_ref, wg_ref, wp_ref, o_ref):
        x = h_ref[...]
        xh, _ = _rms_stats(x)
        s = _sigmoid(_dot((xh * g_ref[...]).astype(BF16), wg_ref[...]))
        o_ref[...] = x + _dot_nt(p_ref[...].astype(BF16), wp_ref[...]) * s

    row = lambda m: (m, 0)
    fixed = lambda m: (0, 0)
    return pl.pallas_call(
        body, name="ple_fwd",
        grid=(SEQ // tm,),
        in_specs=[pl.BlockSpec((tm, D_MODEL), row), pl.BlockSpec((1, D_MODEL), fixed), pl.BlockSpec((tm, PLE_DIM), row),
                  pl.BlockSpec((None, D_MODEL, D_MODEL), lambda m: (0, 0, 0)), pl.BlockSpec((D_MODEL, PLE_DIM), fixed)],
        out_specs=pl.BlockSpec((tm, D_MODEL), row),
        out_shape=jax.ShapeDtypeStruct((SEQ, D_MODEL), F32),
        compiler_params=_cparams(("parallel",)),
    )(*_in_hbm(h, g, p, wgate, wprojt))


def _ple_bwd(h, g, p, dh, wgate, wprojt):
    tm = TOK_TILE

    def body(h_ref, g_ref, p_ref, dh_ref, wg_ref, wp_ref, o_ref, u_ref, dq_ref, dpp_ref, pb_ref, dg_ref):
        xh, r = _rms_stats(h_ref[...])
        u = (xh * g_ref[...]).astype(BF16)
        s = _sigmoid(_dot(u, wg_ref[...]))
        p_bf = p_ref[...].astype(BF16)
        pp = _dot_nt(p_bf, wp_ref[...])
        dhv = dh_ref[...]
        dq = (dhv * pp * s * (1.0 - s)).astype(BF16)
        u_ref[0] = u
        dq_ref[0] = dq
        dpp_ref[0] = (dhv * s).astype(BF16)
        pb_ref[0] = p_bf
        dx, dg = _rms_bwd(_dot_nt(dq, wg_ref[...]), xh, r, g_ref[...])
        o_ref[...] = dhv + dx
        _accumulate(dg_ref, pl.program_id(0) == 0, dg)

    row = lambda m: (m, 0)
    fixed = lambda m: (0, 0)
    lead = lambda m: (0, m, 0)
    big = pltpu.HBM((1, SEQ, D_MODEL), BF16)
    return _out_hbm(pl.pallas_call(
        body, name="ple_bwd",
        grid=(SEQ // tm,),
        in_specs=[pl.BlockSpec((tm, D_MODEL), row), pl.BlockSpec((1, D_MODEL), fixed), pl.BlockSpec((tm, PLE_DIM), row),
                  pl.BlockSpec((tm, D_MODEL), row),
                  pl.BlockSpec((None, D_MODEL, D_MODEL), lambda m: (0, 0, 0)), pl.BlockSpec((D_MODEL, PLE_DIM), fixed)],
        out_specs=[pl.BlockSpec((tm, D_MODEL), row),
                   pl.BlockSpec((1, tm, D_MODEL), lead), pl.BlockSpec((1, tm, D_MODEL), lead),
                   pl.BlockSpec((1, tm, D_MODEL), lead), pl.BlockSpec((1, tm, PLE_DIM), lead),
                   pl.BlockSpec((1, D_MODEL), fixed)],
        out_shape=[jax.ShapeDtypeStruct((SEQ, D_MODEL), F32), big, big, big,
                   pltpu.HBM((1, SEQ, PLE_DIM), BF16),
                   jax.ShapeDtypeStruct((1, D_MODEL), F32)],
        compiler_params=_cparams(("arbitrary",)),
    )(*_in_hbm(h, g, p, dh, wgate, wprojt)), (1, 2, 3, 4))


def _final_loss(h, g, target):
    tm = TOK_TILE

    def body(h_ref, g_ref, t_ref, loss_ref, dh_ref, dg_ref):
        first = pl.program_id(0) == 0
        xh, r = _rms_stats(h_ref[...])
        diff = xh * g_ref[...] - t_ref[...]
        part = 0.5 * jnp.sum(jnp.mean(diff * diff, axis=-1, keepdims=True), axis=0, keepdims=True)
        _accumulate(loss_ref, first, jnp.broadcast_to(part, (SUBLANES, LANES)))
        dx, dg = _rms_bwd(diff * (1.0 / D_MODEL), xh, r, g_ref[...])
        dh_ref[...] = dx
        _accumulate(dg_ref, first, dg)

    row = lambda m: (m, 0)
    fixed = lambda m: (0, 0)
    return pl.pallas_call(
        body, name="final_loss",
        grid=(SEQ // tm,),
        in_specs=[pl.BlockSpec((tm, D_MODEL), row), pl.BlockSpec((1, D_MODEL), fixed),
                  pl.BlockSpec((tm, D_MODEL), row)],
        out_specs=[pl.BlockSpec((SUBLANES, LANES), fixed),
                   pl.BlockSpec((tm, D_MODEL), row),
                   pl.BlockSpec((1, D_MODEL), fixed)],
        out_shape=[jax.ShapeDtypeStruct((SUBLANES, LANES), F32),
                   jax.ShapeDtypeStruct((SEQ, D_MODEL), F32),
                   jax.ShapeDtypeStruct((1, D_MODEL), F32)],
        compiler_params=_cparams(("arbitrary",)),
    )(*_in_hbm(h, g, target))


def _disc(ar, ai, ldt):
    dt = jnp.exp(ldt)
    mag = jnp.exp(ar * dt)
    ph = ai * dt
    lr, li = mag * jnp.cos(ph), mag * jnp.sin(ph)
    nr, ni = lr - 1.0, li
    den = ar * ar + ai * ai
    return lr, li, (nr * ar + ni * ai) / den, (ni * ar - nr * ai) / den


def _s5_disc(a_row, ldt_row, a_rep, ldt_rep, bt, ct, tile_e, mask):
    n = N_STATE

    def body(ar_ref, lr_ref, ap_ref, lp_ref, b_ref, c_ref, e_ref, m_ref, lt_ref, ltr_ref, bb_ref, cc_ref):
        lr, li, _, _ = _disc(ar_ref[0], ar_ref[1], lr_ref[...])
        pr, pi = lr, li
        rid = _row_ids(SUBLANES, n)
        for k in range(1, 9):
            for ref, sgn, edge in ((lt_ref, 1.0, 24 + k - 1), (ltr_ref, -1.0, 24 + 8 - k)):
                if k in (1, 2, 4):
                    r0 = {1: 0, 2: 8, 4: 16}[k]
                    keep = (rid >= k) if ref is lt_ref else (rid < SUBLANES - k)
                    ref[r0:r0 + 8, 0:n] = jnp.where(keep, jnp.broadcast_to(pr, (8, n)), 0.0)
                    ref[r0:r0 + 8, n:2 * n] = jnp.where(keep, jnp.broadcast_to(sgn * pi, (8, n)), 0.0)
                ref[edge:edge + 1, 0:n] = pr
                ref[edge:edge + 1, n:2 * n] = sgn * pi
            pr, pi = pr * lr - pi * li, pr * li + pi * lr
        _, _, fr, fi = _disc(ap_ref[0], ap_ref[1], lp_ref[...])
        br, bi = b_ref[0], b_ref[1]
        e = e_ref[...]
        m = m_ref[...].astype(F32)
        bb_ref[:, 0:n] = (_dot((fr * br - fi * bi).astype(BF16), e) * m).astype(BF16)
        bb_ref[:, n:2 * n] = (_dot((fr * bi + fi * br).astype(BF16), e) * m).astype(BF16)
        cc_ref[:, 0:n] = (_dot(c_ref[0].astype(BF16), e) * m).astype(BF16)
        cc_ref[:, n:2 * n] = (-(_dot(c_ref[1].astype(BF16), e) * m)).astype(BF16)

    return pl.pallas_call(
        body, name="s5_disc",
        out_shape=[jax.ShapeDtypeStruct((32, 2 * n), F32), jax.ShapeDtypeStruct((32, 2 * n), F32),
                   jax.ShapeDtypeStruct((SSM_W, 2 * n), BF16), jax.ShapeDtypeStruct((SSM_W, 2 * n), BF16)],
        compiler_params=_cparams(None),
    )(a_row, ldt_row, a_rep, ldt_rep, bt, ct, tile_e, mask)


def _dot_exact(x, sel):
    hi = x.astype(BF16)
    r1 = x - hi.astype(F32)
    mid = r1.astype(BF16)
    lo = (r1 - mid.astype(F32)).astype(BF16)
    return _dot(hi, sel) + _dot(mid, sel) + _dot(lo, sel)


def _s5_disc_bwd(a, ldt, a_rep, ldt_rep, bt, mask, dl, d_bb, d_cc, fold):
    n = N_STATE

    def body(a_ref, l_ref, ap_ref, lp_ref, b_ref, m_ref, dl_ref, dbb_ref, dcc_ref, f_ref,
             da_ref, dldt_ref, db_ref, dc_ref):
        m = m_ref[...].astype(F32)
        fold_m = f_ref[...]
        diag = lambda x: _dot_exact(jnp.where(m > 0.0, x, 0.0), fold_m)
        dr, di = diag(dbb_ref[:, 0:n]), diag(dbb_ref[:, n:2 * n])
        dc_ref[0] = diag(dcc_ref[:, 0:n])
        dc_ref[1] = -diag(dcc_ref[:, n:2 * n])
        _, _, fr, fi = _disc(ap_ref[0], ap_ref[1], lp_ref[...])
        br, bi = b_ref[0], b_ref[1]
        db_ref[0] = fr * dr + fi * di
        db_ref[1] = fr * di - fi * dr
        per_state = lambda x: x.reshape(SSM_GROUPS, SSM_GROUP, SSM_STATE).sum(axis=1)
        dfr = per_state(dr * br + di * bi)
        dfi = per_state(di * br - dr * bi)
        _, vjp = jax.vjp(_disc, a_ref[0], a_ref[1], l_ref[...])
        dar, dai, dldt = vjp((dl_ref[0], dl_ref[1], dfr, dfi))
        da_ref[0] = dar
        da_ref[1] = dai
        dldt_ref[...] = jnp.sum(dldt, axis=1, keepdims=True)

    return pl.pallas_call(
        body, name="s5_disc_bwd",
        out_shape=[jax.ShapeDtypeStruct((2, SSM_GROUPS, SSM_STATE), F32),
                   jax.ShapeDtypeStruct((SSM_GROUPS, 1), F32),
                   jax.ShapeDtypeStruct((2, SSM_W, SSM_STATE), F32),
                   jax.ShapeDtypeStruct((2, SSM_W, SSM_STATE), F32)],
        compiler_params=_cparams(None),
    )(a, ldt, a_rep, ldt_rep, bt, mask, dl, d_bb, d_cc, fold)


def _row_block(rows, cap=512):
    for bm in range(min(cap, rows), 0, -1):
        if rows % bm == 0 and (bm % 8 == 0 or bm == rows):
            return bm
    return rows


def _pair_sum(fulls, got, segs):
    ns = len(segs)
    offs = _seg_offsets(segs)
    _, rtot, c = got.shape
    parts = 2
    pr = rtot // parts
    assert pr * parts == rtot and pr % 16 == 0
    pieces = [[] for _ in range(parts)]
    for a, (n, r) in enumerate(segs):
        for m in range(n):
            lo = offs[a] + m * r
            for h in range(parts):
                clo, chi = max(lo, h * pr), min(lo + r, (h + 1) * pr)
                if chi > clo:
                    pieces[h].append((a, m, clo - lo, clo - h * pr, chi - clo))
    n_sems = max(len(ps) for ps in pieces)

    def body(*refs):
        srcs = refs[:ns]
        got_ref, p32_ref, pbf_ref, own_v, sems = refs[ns:]
        h = pl.program_id(0)
        k = pl.program_id(1)
        dev = 2 * k + lax.axis_index("c")
        for hh in range(parts):
            @pl.when(h == hh)
            def _(hh=hh):
                cps = []
                for i, (a, m, so, do, rows) in enumerate(pieces[hh]):
                    start = pl.multiple_of(dev * segs[a][1] + so, 16)
                    cps.append(pltpu.make_async_copy(srcs[a].at[m, pl.ds(start, rows), :],
                                                     own_v.at[pl.ds(do, rows), :], sems.at[i]))
                for cp in cps:
                    cp.start()
                for cp in cps:
                    cp.wait()
        s = own_v[...].astype(F32) + got_ref[0].astype(F32)
        pbf_ref[0] = s.astype(BF16)

        @pl.when(k == 2 * lax.axis_index("x") + lax.axis_index("y"))
        def _():
            p32_ref[...] = s

    spec = pl.BlockSpec((1, pr, c), lambda h, k: (k, h, 0))
    return pl.pallas_call(
        body, name="pair_sum",
        grid=(parts, 4),
        in_specs=[HBM] * ns + [spec], out_specs=[pl.BlockSpec((pr, c), lambda h, k: (h, 0)), spec],
        out_shape=[pltpu.HBM((rtot, c), F32), pltpu.HBM(got.shape, BF16)],
        scratch_shapes=[pltpu.VMEM((pr, c), BF16), pltpu.SemaphoreType.DMA((n_sems,))],
        compiler_params=_cparams(("arbitrary", "arbitrary")),
    )(*_in_hbm(*fulls, got))


def _chip_sum(own, rb):
    r, c = own.shape
    bm = _row_block(r)

    def body(o_ref, r_ref, s_ref):
        s_ref[...] = ((o_ref[...] + r_ref[0].astype(F32)) + r_ref[1].astype(F32)) + r_ref[2].astype(F32)

    return pl.pallas_call(
        body, name="chip_sum",
        grid=(r // bm,),
        in_specs=[pl.BlockSpec((bm, c), lambda k: (k, 0)), pl.BlockSpec((3, bm, c), lambda k: (0, k, 0))],
        out_specs=pl.BlockSpec((bm, c), lambda k: (k, 0)),
        out_shape=jax.ShapeDtypeStruct((r, c), F32),
        compiler_params=_cparams(("parallel",)),
    )(*_in_hbm(own, rb))


def _adamw(w, g, m, v):
    r, c = w.shape
    bm = _row_block(r)
    bc1 = 1.0 - ADAM_B1 ** ADAM_STEP
    bc2 = 1.0 - ADAM_B2 ** ADAM_STEP

    def body(w_ref, g_ref, m_ref, v_ref, d_ref, nm_ref, nv_ref):
        gv = g_ref[...]
        nm = ADAM_B1 * m_ref[...] + (1.0 - ADAM_B1) * gv
        nv = ADAM_B2 * v_ref[...] + (1.0 - ADAM_B2) * (gv * gv)
        nm_ref[...] = nm
        nv_ref[...] = nv
        d_ref[...] = -ADAM_LR * ((nm / bc1) / (jnp.sqrt(nv / bc2) + ADAM_EPS) + ADAM_WD * w_ref[...])

    spec = pl.BlockSpec((bm, c), lambda k: (k, 0))
    shp = jax.ShapeDtypeStruct((r, c), F32)
    return pl.pallas_call(
        body, name="adamw",
        grid=(r // bm,),
        in_specs=[spec] * 4, out_specs=[spec] * 3, out_shape=[shp] * 3,
        compiler_params=_cparams(("parallel",)),
    )(*_in_hbm(w, g, m, v))


def _adamw_layers(sets, first, prev):
    ns = len(sets)
    depth, r, c = sets[0][0].shape
    nl = sets[0][3].shape[0]
    bm = _row_block(r, min(512, max(SUBLANES, (24 * MIB) // (ns * 8 * 2 * c * 4))))
    bc1 = 1.0 - ADAM_B1 ** ADAM_STEP
    bc2 = 1.0 - ADAM_B2 ** ADAM_STEP

    def body(*refs):
        outs = refs[len(refs) - 4 * ns:]
        for s in range(ns):
            w_ref, m_ref, v_ref, g_ref = refs[4 * s:4 * s + 4]
            go_ref, d_ref, nm_ref, nv_ref = outs[4 * s:4 * s + 4]
            gv = g_ref[...]
            nm = ADAM_B1 * m_ref[...] + (1.0 - ADAM_B1) * gv
            nv = ADAM_B2 * v_ref[...] + (1.0 - ADAM_B2) * (gv * gv)
            go_ref[...] = gv
            nm_ref[...] = nm
            nv_ref[...] = nv
            d_ref[...] = -ADAM_LR * ((nm / bc1) / (jnp.sqrt(nv / bc2) + ADAM_EPS) + ADAM_WD * w_ref[...])

    at = pl.BlockSpec((1, bm, c), lambda i, k: (first + i, k, 0))
    shp = jax.ShapeDtypeStruct((depth, r, c), F32)
    old = [] if prev is None else [a for four in prev for a in four]
    flat = pl.pallas_call(
        body, name="adamw_layers",
        grid=(nl, r // bm),
        in_specs=[at, at, at, pl.BlockSpec((1, bm, c), lambda i, k: (i, k, 0))] * ns + [HBM] * len(old),
        out_specs=[at] * (4 * ns), out_shape=[shp] * (4 * ns),
        input_output_aliases={4 * ns + i: i for i in range(len(old))},
        compiler_params=_cparams(("parallel", "parallel")),
    )(*_in_hbm(*[a for four in sets for a in four]), *old)
    return [flat[4 * s:4 * s + 4] for s in range(ns)]


def _mesh_pos():
    return lax.axis_index("x"), lax.axis_index("y"), lax.axis_index("c")


def _dev_index(p):
    return 4 * p[0] + 2 * p[1] + p[2]


def _seg_offsets(segs):
    offs, o = [], 0
    for n, r in segs:
        offs.append(o)
        o += n * r
    return offs


def _remote(src, dst, send_sem, recv_sem, to):
    return pltpu.make_async_remote_copy(src_ref=src, dst_ref=dst, send_sem=send_sem, recv_sem=recv_sem,
                                        device_id=to, device_id_type=MESH)


def _allgather(pack, segs, name):
    rtot, c = pack.shape
    ns = len(segs)
    offs = _seg_offsets(segs)
    assert rtot == sum(n * r for n, r in segs)

    def body(pack_ref, *refs):
        outs = refs[:ns]
        send_sems, recv_sems, local_sem = refs[ns:]
        x, y, cc = _mesh_pos()
        me, sib = (x, y, cc), (x, y, 1 - cc)
        chips = [(1 - x, y), (x, 1 - y), (1 - x, 1 - y)]

        def pieces(dev, from_pack):
            res = []
            for a, (n, r) in enumerate(segs):
                for m in range(n):
                    dst = outs[a].at[m, pl.ds(pl.multiple_of(dev * r, r), r), :]
                    src = pack_ref.at[pl.ds(offs[a] + m * r, r), :] if from_pack else dst
                    res.append((src, dst))
            return res

        def push(k, dev, to, from_pack):
            for s, d in pieces(dev, from_pack):
                _remote(s, d, send_sems.at[k], recv_sems.at[k], to).start()

        def whole(k):
            return _remote(pack_ref, pack_ref, send_sems.at[k], recv_sems.at[k], me)

        my_dev = _dev_index(me)
        for s, d in pieces(my_dev, True):
            pltpu.make_async_copy(s, d, local_sem).start()
        push(0, my_dev, sib, True)
        for j, chip in enumerate(chips):
            push(1 + j, my_dev, (*chip, cc), True)
        for j, chip in enumerate(chips):
            whole(1 + j).wait_recv()
            push(4 + j, _dev_index((*chip, cc)), sib, False)
        whole(0).wait_recv()
        for j in range(3):
            whole(4 + j).wait_recv()
        for k in range(7):
            whole(k).wait_send()
        pltpu.make_async_copy(pack_ref, pack_ref, local_sem).wait()

    return pl.pallas_call(
        body, name=name,
        in_specs=[HBM], out_specs=[HBM] * ns,
        out_shape=[jax.ShapeDtypeStruct((n, N_DEV * r, c), pack.dtype) for n, r in segs],
        scratch_shapes=[pltpu.SemaphoreType.DMA((7,)), pltpu.SemaphoreType.DMA((7,)), pltpu.SemaphoreType.DMA],
    )(pack)


HBM = pl.BlockSpec(memory_space=pltpu.HBM)
SEM = pl.BlockSpec(memory_space=pltpu.SEMAPHORE)
VMEM_WHOLE = pl.BlockSpec(memory_space=pltpu.VMEM)
EFFECT = pltpu.SideEffectType.DATAFLOW_SIDE_EFFECTING


def _hbm(a):
    return pltpu.with_memory_space_constraint(a, pltpu.HBM)


def _ag_start(pack, segs, after, name):
    rtot, c = pack.shape
    ns = len(segs)
    offs = _seg_offsets(segs)

    def body(pack_ref, *refs):
        lands = refs[:ns]
        send_sems, recv_sems = refs[ns + 1], refs[ns + 2]
        token = refs[-1]
        x, y, cc = _mesh_pos()
        my_dev = _dev_index((x, y, cc))
        targets = [(x, y, 1 - cc), (1 - x, y, cc), (x, 1 - y, cc), (1 - x, 1 - y, cc)]
        for k, to in enumerate(targets):
            for a, (n, r) in enumerate(segs):
                for m in range(n):
                    _remote(pack_ref.at[pl.ds(offs[a] + m * r, r), :],
                            lands[a].at[m, pl.ds(pl.multiple_of(my_dev * r, r), r), :],
                            send_sems.at[k], recv_sems.at[k], to).start()
        token[...] = jnp.zeros_like(token)

    land_shapes = [(n, N_DEV * r, c) for n, r in segs]
    outs = pl.pallas_call(
        body, name=name,
        in_specs=[HBM] * (1 + ns) + [UNREAD],
        out_specs=[SEM, SEM, HBM] + [HBM] * ns + [VMEM_WHOLE],
        out_shape=[pltpu.SemaphoreType.DMA((4,)), pltpu.SemaphoreType.DMA((4,)), pltpu.HBM(pack.shape, pack.dtype)]
        + [pltpu.HBM(s, pack.dtype) for s in land_shapes] + [jax.ShapeDtypeStruct((SUBLANES, LANES), F32)],
        input_output_aliases={0: 2, **{1 + i: 3 + i for i in range(ns)}},
        compiler_params=pltpu.CompilerParams(has_side_effects=EFFECT),
    )(_hbm(pack), *[_hbm(lax.empty(s, pack.dtype)) for s in land_shapes], _hbm(after))
    return outs[0], outs[1], outs[2], list(outs[3:3 + ns]), outs[-1]


def _ag_wait(send_sems, recv_sems, pack, lands, after, name):
    ns = len(lands)

    def body(pack_ref, *refs):
        send_ref, recv_ref = refs[ns], refs[ns + 1]
        me = _mesh_pos()
        for k in range(4):
            whole = _remote(pack_ref, pack_ref, send_ref.at[k], recv_ref.at[k], me)
            whole.wait_send()
            whole.wait_recv()

    outs = pl.pallas_call(
        body, name=name,
        in_specs=[HBM] * (1 + ns) + [SEM, SEM, UNREAD],
        out_specs=[HBM] * (1 + ns),
        out_shape=[pltpu.HBM(pack.shape, pack.dtype)] + [pltpu.HBM(a.shape, a.dtype) for a in lands],
        input_output_aliases={i: i for i in range(1 + ns)},
        compiler_params=pltpu.CompilerParams(has_side_effects=EFFECT),
    )(pack, *lands, send_sems, recv_sems, _hbm(after))
    return outs[0], list(outs[1:])


def _ag_finish(pack, lands, segs):
    rtot, c = pack.shape
    ns = len(segs)
    offs = _seg_offsets(segs)

    def body(pack_ref, *refs):
        outs = refs[ns:2 * ns]
        stage, send_sems, recv_sems, local_sems = refs[2 * ns:]
        x, y, cc = _mesh_pos()
        me, sib = (x, y, cc), (x, y, 1 - cc)
        chips = [(1 - x, y), (x, 1 - y), (1 - x, 1 - y)]

        def rows(a, m, dev):
            return outs[a].at[m, pl.ds(pl.multiple_of(dev * segs[a][1], segs[a][1]), segs[a][1]), :]

        for j, chip in enumerate(chips):
            dev = _dev_index((*chip, cc))
            for a, (n, r) in enumerate(segs):
                for m in range(n):
                    _remote(rows(a, m, dev), rows(a, m, dev), send_sems.at[j], recv_sems.at[j], sib).start()
        load = pltpu.make_async_copy(pack_ref, stage, local_sems.at[0])
        load.start()
        load.wait()
        my_dev = _dev_index(me)
        for a, (n, r) in enumerate(segs):
            for m in range(n):
                pltpu.make_async_copy(stage.at[pl.ds(offs[a] + m * r, r), :], rows(a, m, my_dev), local_sems.at[1]).start()
        pltpu.make_async_copy(stage, pack_ref, local_sems.at[1]).wait()
        for j in range(3):
            _remote(pack_ref, pack_ref, send_sems.at[j], recv_sems.at[j], me).wait()

    outs = pl.pallas_call(
        body, name="ag_finish",
        in_specs=[HBM] * (1 + ns), out_specs=[HBM] * ns,
        out_shape=[pltpu.HBM(a.shape, a.dtype) if r >= 128 else jax.ShapeDtypeStruct(a.shape, a.dtype)
                   for a, (_, r) in zip(lands, segs)],
        input_output_aliases={1 + i: i for i in range(ns)},
        scratch_shapes=[pltpu.VMEM((rtot, c), pack.dtype), pltpu.SemaphoreType.DMA((3,)),
                        pltpu.SemaphoreType.DMA((3,)), pltpu.SemaphoreType.DMA((2,))],
        compiler_params=_cparams(None, 16),
    )(pack, *lands)
    return list(outs)


def _rs_chips_start(pbf, after, name):
    _, rtot, c = pbf.shape

    def body(pbf_ref, land_ref, after_ref, send_sems, recv_sems, pbf_thru, land_thru, token):
        x, y, cc = _mesh_pos()
        for j, (cx, cy) in enumerate([(1 - x, y), (x, 1 - y), (1 - x, 1 - y)]):
            _remote(pbf_ref.at[2 * cx + cy], land_ref.at[j], send_sems.at[j], recv_sems.at[j], (cx, cy, cc)).start()
        token[...] = jnp.zeros_like(token)

    return pl.pallas_call(
        body, name=name,
        in_specs=[HBM, HBM, UNREAD],
        out_specs=[SEM, SEM, HBM, HBM, VMEM_WHOLE],
        out_shape=[pltpu.SemaphoreType.DMA((3,)), pltpu.SemaphoreType.DMA((3,)), pltpu.HBM(pbf.shape, pbf.dtype),
                   pltpu.HBM((3, rtot, c), pbf.dtype), jax.ShapeDtypeStruct((SUBLANES, LANES), F32)],
        input_output_aliases={0: 2, 1: 3},
        compiler_params=pltpu.CompilerParams(has_side_effects=EFFECT),
    )(_hbm(pbf), _hbm(lax.empty((3, rtot, c), pbf.dtype)), _hbm(after))


def _rs_chips_wait(send_sems, recv_sems, pbf, land, after, name):
    def body(pbf_ref, land_ref, send_ref, recv_ref, after_ref, pbf_out, land_out):
        me = _mesh_pos()
        for j in range(3):
            cp = _remote(pbf_ref.at[0], land_ref.at[j], send_ref.at[j], recv_ref.at[j], me)
            cp.wait_send()
            cp.wait_recv()

    return pl.pallas_call(
        body, name=name,
        in_specs=[HBM, HBM, SEM, SEM, UNREAD], out_specs=[HBM, HBM],
        out_shape=[pltpu.HBM(pbf.shape, pbf.dtype), pltpu.HBM(land.shape, land.dtype)],
        input_output_aliases={0: 0, 1: 1},
        compiler_params=pltpu.CompilerParams(has_side_effects=EFFECT),
    )(pbf, land, send_sems, recv_sems, _hbm(after))[1]


def _flips():
    return [(dx, dy, dc) for dx in (0, 1) for dy in (0, 1) for dc in (0, 1) if dx or dy or dc]


def _small_gather_start(flat, name):
    r, c = flat.shape

    def body(flat_ref, land_ref, send_sems, recv_sems, flat_thru, land_thru, token):
        x, y, cc = _mesh_pos()
        mine = land_ref.at[_dev_index((x, y, cc))]
        for k, (dx, dy, dc) in enumerate(_flips()):
            to = (1 - x if dx else x, 1 - y if dy else y, 1 - cc if dc else cc)
            _remote(flat_ref, mine, send_sems.at[k], recv_sems.at[k], to).start()
        token[...] = jnp.zeros_like(token)

    return pl.pallas_call(
        body, name=name,
        in_specs=[HBM, HBM],
        out_specs=[SEM, SEM, HBM, HBM, VMEM_WHOLE],
        out_shape=[pltpu.SemaphoreType.DMA((7,)), pltpu.SemaphoreType.DMA((7,)), pltpu.HBM(flat.shape, flat.dtype),
                   pltpu.HBM((N_DEV, r, c), flat.dtype), jax.ShapeDtypeStruct((SUBLANES, LANES), F32)],
        input_output_aliases={0: 2, 1: 3},
        compiler_params=pltpu.CompilerParams(has_side_effects=EFFECT),
    )(_hbm(flat), _hbm(lax.empty((N_DEV, r, c), flat.dtype)))


def _small_gather_wait(send_sems, recv_sems, flat, land, after, name):
    def body(flat_ref, land_ref, send_ref, recv_ref, after_ref, flat_out, land_out):
        me = _mesh_pos()
        for k in range(N_DEV - 1):
            cp = _remote(flat_ref, land_ref.at[0], send_ref.at[k], recv_ref.at[k], me)
            cp.wait_send()
            cp.wait_recv()

    return pl.pallas_call(
        body, name=name,
        in_specs=[HBM, HBM, SEM, SEM, UNREAD], out_specs=[HBM, HBM],
        out_shape=[pltpu.HBM(flat.shape, flat.dtype), pltpu.HBM(land.shape, land.dtype)],
        input_output_aliases={0: 0, 1: 1},
        compiler_params=pltpu.CompilerParams(has_side_effects=EFFECT),
    )(flat, land, send_sems, recv_sems, _hbm(after))


def _sum_devices(land, own):
    _, r, c = land.shape

    def body(land_ref, own_ref, out_ref):
        me = _dev_index(_mesh_pos())
        total = None
        for d in range(N_DEV):
            other = land_ref[jnp.where(d == me, (d + 1) % N_DEV, d)]
            block = jnp.where(d == me, own_ref[...], other)
            total = block if total is None else total + block
        out_ref[...] = total

    return pl.pallas_call(
        body, name="sum_devices",
        grid=(1,),
        in_specs=[pl.BlockSpec((N_DEV, r, c), lambda i: (0, 0, 0)), pl.BlockSpec((r, c), lambda i: (0, 0))],
        out_specs=pl.BlockSpec((r, c), lambda i: (0, 0)),
        out_shape=jax.ShapeDtypeStruct((r, c), F32),
        compiler_params=_cparams(("arbitrary",)),
    )(land, own)


def _rs_sibling_start(fulls, segs, name):
    ns = len(segs)
    offs = _seg_offsets(segs)
    rtot = sum(n * r for n, r in segs)
    c = fulls[0].shape[-1]
    dt = fulls[0].dtype

    def body(*refs):
        srcs = refs[:ns]
        land_ref, send_sem, recv_sem = refs[ns], refs[ns + 1], refs[ns + 2]
        token = refs[-1]
        x, y, cc = _mesh_pos()
        for k in range(4):
            for a, (n, r) in enumerate(segs):
                for m in range(n):
                    theirs = srcs[a].at[m, pl.ds(pl.multiple_of((2 * k + 1 - cc) * r, r), r), :]
                    _remote(theirs, land_ref.at[k, pl.ds(offs[a] + m * r, r), :], send_sem, recv_sem,
                            (x, y, 1 - cc)).start()
        token[...] = jnp.zeros_like(token)

    outs = pl.pallas_call(
        body, name=name,
        in_specs=[HBM] * (ns + 1),
        out_specs=[SEM, SEM] + [HBM] * (ns + 1) + [VMEM_WHOLE],
        out_shape=[pltpu.SemaphoreType.DMA(()), pltpu.SemaphoreType.DMA(())]
        + [pltpu.HBM(a.shape, a.dtype) for a in fulls] + [pltpu.HBM((4, rtot, c), dt),
                                                           jax.ShapeDtypeStruct((SUBLANES, LANES), F32)],
        input_output_aliases={i: 2 + i for i in range(ns + 1)},
        compiler_params=pltpu.CompilerParams(has_side_effects=EFFECT),
    )(*[_hbm(a) for a in fulls], _hbm(lax.empty((4, rtot, c), dt)))
    return outs[0], outs[1], list(outs[2:2 + ns]), outs[2 + ns], outs[-1]


def _rs_sibling_wait(send_sem, recv_sem, fulls, land, after, name):
    ns = len(fulls)

    def body(*refs):
        land_ref, send_ref, recv_ref = refs[ns], refs[ns + 1], refs[ns + 2]
        whole = _remote(land_ref, land_ref, send_ref, recv_ref, _mesh_pos())
        whole.wait_send()
        whole.wait_recv()

    outs = pl.pallas_call(
        body, name=name,
        in_specs=[HBM] * (ns + 1) + [SEM, SEM, UNREAD], out_specs=[HBM] * (ns + 1),
        out_shape=[pltpu.HBM(a.shape, a.dtype) for a in fulls] + [pltpu.HBM(land.shape, land.dtype)],
        input_output_aliases={i: i for i in range(ns + 1)},
        compiler_params=pltpu.CompilerParams(has_side_effects=EFFECT),
    )(*fulls, land, send_sem, recv_sem, _hbm(after))
    return list(outs[:ns]), outs[ns]


def _tp(w):
    return jnp.swapaxes(w, -1, -2)


def _s5_prepare(a_re, a_im, log_dt, b_re, b_im, c_re, c_im):
    a = jnp.stack([a_re, a_im], axis=1)
    ldt = jnp.broadcast_to(log_dt[:, :, None], (DEPTH, SSM_GROUPS, SSM_STATE))
    a_row = a.reshape(DEPTH, 2, 1, N_STATE)
    ldt_row = ldt.reshape(DEPTH, 1, N_STATE)
    a_rep = jnp.repeat(a, SSM_GROUP, axis=2)
    ldt_rep = jnp.repeat(ldt, SSM_GROUP, axis=1)
    bt = jnp.stack([_tp(b_re), _tp(b_im)], axis=1).reshape(DEPTH, 2, SSM_W, SSM_STATE)
    ct = jnp.stack([c_re, c_im], axis=1).reshape(DEPTH, 2, SSM_W, SSM_STATE)
    tile_e = jnp.tile(jnp.eye(SSM_STATE, dtype=BF16), (1, SSM_GROUPS))
    mask = jnp.repeat(jnp.repeat(jnp.eye(SSM_GROUPS, dtype=BF16), SSM_GROUP, axis=0), SSM_STATE, axis=1)
    out = []
    for l in range(DEPTH):
        tabs = _s5_disc(a_row[l], ldt_row[l], a_rep[l], ldt_rep[l], bt[l], ct[l], tile_e, mask)
        out.append(((a[l], ldt[l], a_rep[l], ldt_rep[l], bt[l], mask), *tabs))
    return out


def _layer_fwd(h, p_l, small, big, arrive=None):
    saved = {'h0': h}
    if arrive is not None:
        arrive(0, h)
    h, saved['gu1'] = _ffn_fwd(h, small['ffn1_norm'], big['ff1'])
    saved['h1'] = h
    if arrive is not None:
        arrive(1, h)
    z = _inproj_fwd(h, small['mix_norm'], big['wint'])
    ya, ys, hs = _s5conv_fwd(z, small['conv_w'], small['conv_b'], small['bbmat'], small['ccmat'], small['dvec'],
                             small['ltab'])
    saved.update(z=z, ya=ya, ys=ys, hs=hs)
    h = _mix_out_fwd(h, ya, ys, big['glu'], small['glu_b'], small['conv_out_norm'], small['ssm_out_norm'], big['wout'])
    saved['h2'] = h
    if arrive is not None:
        arrive(2, h)
    h, saved['gu2'] = _ffn_fwd(h, small['ffn2_norm'], big['ff2'])
    saved['h3'] = h
    h = _ple_fwd(h, small['ple_norm'], p_l, big['plg'], big['plpt'])
    return h, saved


def _ffn_bwd(h_in, g, dh, gu, w3):
    dh_in, dga, ud, dg = _ffn_bwd_act(h_in, g, dh, gu, w3)
    return dh_in, _matmul_tn(dga, ud, FF_BLOCK, BF16, "ffn_wgrad"), dg


def _layer_bwd_top(dh, p_l, small, big, saved):
    gs = {}
    dh, u, dq, dpp, pb, gs['ple_norm'] = _ple_bwd(saved['h3'], small['ple_norm'], p_l, dh, big['plg'], big['plpt'])
    d_plg = _matmul_tn(u, dq, 256, BF16, "ple_gate_wgrad")
    d_plpt = _matmul_tn(dpp, pb, 256, BF16, "ple_proj_wgrad", to_kernel=False)
    dh, d_ff2, gs['ffn2_norm'] = _ffn_bwd(saved['h2'], small['ffn2_norm'], dh, saved['gu2'], big['ff2'])
    return dh, (gs, d_plg, d_plpt, d_ff2)


def _layer_bwd_rest(dh, top, small, big, saved):
    gs, d_plg, d_plpt, d_ff2 = top
    dya, dys, ycat, dhb, zg, dq, part = _mix_out_bwd(dh, saved['ya'], saved['ys'], big['glu'], small['glu_b'],
                                                     small['conv_out_norm'], small['ssm_out_norm'], big['wout'])
    d_wout = _matmul_tn(ycat, dhb, 256, BF16, "w_out_wgrad")
    d_glu = _matmul_tn(zg, dq, 256, BF16, "glu_wgrad", to_kernel=False)
    dz, gadj, us, dyb, dl, dcw = _s5conv_bwd(saved['z'], saved['hs'], dya, dys, small['conv_w'], small['conv_b'],
                                             small['bbmat'], small['ccmat'], small['dvec'], small['ltab_rev'])
    d_bb = _block_wgrad(us, gadj, "s5_b_wgrad")
    d_cc = _block_wgrad(dyb, saved['hs'][None], "s5_c_wgrad")
    dh, u, gs['mix_norm'] = _inproj_bwd(saved['h1'], small['mix_norm'], dh, dz, big['wint'])
    d_wint = _matmul_tn(dz[None], u, 256, BF16, "w_in_wgrad")
    dh, d_ff1, gs['ffn1_norm'] = _ffn_bwd(saved['h0'], small['ffn1_norm'], dh, saved['gu1'], big['ff1'])

    dlb = dl[0].reshape(2, SSM_GROUPS, SSM_STATE)
    fold = jnp.tile(jnp.eye(SSM_STATE, dtype=BF16), (SSM_GROUPS, 1))
    da, dldt, dbt, dct = _s5_disc_bwd(*small['disc_in'], dlb, d_bb, d_cc, fold)
    gs['ssm_A_re'], gs['ssm_A_im'] = da[0], da[1]
    gs['ssm_log_dt'] = dldt[:, 0]
    ghp = (SSM_GROUPS, SSM_GROUP, SSM_STATE)
    gs['ssm_B_re'], gs['ssm_B_im'] = dbt[0].reshape(ghp), dbt[1].reshape(ghp)
    gs['ssm_C_re'], gs['ssm_C_im'] = dct[0].reshape(ghp), dct[1].reshape(ghp)
    gs['conv_w'] = dcw[0:3]
    gs['conv_b'] = dcw[3]
    gs['ssm_D'] = dcw[4].reshape(SSM_GROUPS, SSM_GROUP)
    gs['conv_out_norm'], gs['ssm_out_norm'], gs['glu_b'] = part[0], part[1], part[2]
    for n in ('ple_norm', 'ffn2_norm', 'mix_norm', 'ffn1_norm'):
        gs[n] = gs[n][0]
    fulls = [d_ff1, d_ff2, d_wint, d_wout, d_plg,
             d_plpt.reshape(1, D_MODEL * PLE_DIM // D_MODEL, D_MODEL), d_glu.reshape(1, SSM_W * SSM_W // D_MODEL, D_MODEL)]
    return dh, fulls, gs


VIEW_T = ('ffn1_w_gate', 'ffn1_w_up', 'ffn2_w_gate', 'ffn2_w_up', 'ssm_B_re', 'ssm_B_im')


def _view(name, a):
    return _tp(a) if name in VIEW_T else a


SEG_NAMES = ('ff1', 'ff2', 'wint', 'wout', 'plg', 'plpt', 'glu')
FIRST_LAYER_GROUPS = ((0,), (2, 3, 6), (1, 4, 5))


def _layer_pack(W, l, segments=range(len(SEGS))):
    pieces = {
        0: lambda: [_tp(W['ffn1_w_gate'][l]), _tp(W['ffn1_w_up'][l]), W['ffn1_w_down'][l]],
        1: lambda: [_tp(W['ffn2_w_gate'][l]), _tp(W['ffn2_w_up'][l]), W['ffn2_w_down'][l]],
        2: lambda: [_tp(W['w_in'][l])],
        3: lambda: [W['w_out'][l]],
        4: lambda: [W['ple_w_gate'][l]],
        5: lambda: [_tp(W['ple_w_proj'][l]).reshape(-1, D_MODEL)],
        6: lambda: [W['glu_w'][l].reshape(-1, D_MODEL)],
    }
    return jnp.concatenate([a for s in segments for a in pieces[s]()], axis=0).astype(BF16)


def _as_big(named):
    shape = dict(plpt=(D_MODEL, PLE_DIM), glu=(SSM_W, SSM_W))
    return {n: (a.reshape(shape[n]) if n in shape else a) for n, a in named.items()}


def _pad_rows(flat, mult, width=LANES):
    per = mult * width
    n = flat.shape[0]
    tot = -(-n // per) * per
    return jnp.pad(flat, (0, tot - n)).reshape(tot // width, width)


def _adamw_any(w, g, m, v):
    shp = w.shape
    two = (lambda t: t.reshape(-1, shp[-1]))
    d, nm, nv = _adamw(two(w), two(g), two(m), two(v))
    return d.reshape(shp), nm.reshape(shp), nv.reshape(shp)


def kernel(x, p, ffn1_norm, ffn1_w_gate, ffn1_w_up, ffn1_w_down, mix_norm, w_in, conv_w, conv_b, ssm_A_re, ssm_A_im, ssm_B_re, ssm_B_im, ssm_C_re, ssm_C_im, ssm_D, ssm_log_dt, glu_w, glu_b, conv_out_norm, ssm_out_norm, w_out, ffn2_norm, ffn2_w_gate, ffn2_w_up, ffn2_w_down, ple_norm, ple_w_gate, ple_w_proj, final_norm, loss_target, m_ffn1_norm, m_ffn1_w_gate, m_ffn1_w_up, m_ffn1_w_down, m_mix_norm, m_w_in, m_conv_w, m_conv_b, m_ssm_A_re, m_ssm_A_im, m_ssm_B_re, m_ssm_B_im, m_ssm_C_re, m_ssm_C_im, m_ssm_D, m_ssm_log_dt, m_glu_w, m_glu_b, m_conv_out_norm, m_ssm_out_norm, m_w_out, m_ffn2_norm, m_ffn2_w_gate, m_ffn2_w_up, m_ffn2_w_down, m_ple_norm, m_ple_w_gate, m_ple_w_proj, m_final_norm, v_ffn1_norm, v_ffn1_w_gate, v_ffn1_w_up, v_ffn1_w_down, v_mix_norm, v_w_in, v_conv_w, v_conv_b, v_ssm_A_re, v_ssm_A_im, v_ssm_B_re, v_ssm_B_im, v_ssm_C_re, v_ssm_C_im, v_ssm_D, v_ssm_log_dt, v_glu_w, v_glu_b, v_conv_out_norm, v_ssm_out_norm, v_w_out, v_ffn2_norm, v_ffn2_w_gate, v_ffn2_w_up, v_ffn2_w_down, v_ple_norm, v_ple_w_gate, v_ple_w_proj, v_final_norm):
    given = dict(locals())
    W = {n: given[n] for n in W_NAMES}
    M = {n: given['m_' + n] for n in W_NAMES}
    V = {n: given['v_' + n] for n in W_NAMES}
    Wv, Mv, Vv = [{n: _view(n, d[n]) for n in W_NAMES} for d in (W, M, V)]
    my_dev = _dev_index(_mesh_pos())

    conv_shard = _pad_rows(W['conv_w'].reshape(-1), SUBLANES)
    conv_all = _allgather(conv_shard, ((1, SUBLANES),), "ag_conv_w")[0]
    conv_full = conv_all.reshape(N_DEV, -1)[:, :DEPTH * 3 * (CONV_W // N_DEV)]
    conv_full = conv_full.reshape(N_DEV, DEPTH, 3, CONV_W // N_DEV).transpose(1, 2, 0, 3).reshape(DEPTH, 3, CONV_W)
    first, after = [], conv_all
    for gi, segments in enumerate(FIRST_LAYER_GROUPS):
        first.append(_ag_start(_layer_pack(W, 0, segments), tuple(SEGS[s] for s in segments), after,
                               "ag_start_0%s" % "abc"[gi]))
        after = first[-1][4]
    s5 = _s5_prepare(*[W[n] + after[0, 0] for n in ('ssm_A_re', 'ssm_A_im', 'ssm_log_dt')],
                     *[W[n] for n in ('ssm_B_re', 'ssm_B_im', 'ssm_C_re', 'ssm_C_im')])
    packs = [None] + [_layer_pack(W, l) for l in range(1, DEPTH)]
    prepared = conv_full[0, 0:1, 0:1] + s5[DEPTH - 1][1][0:1, 0:1] + packs[DEPTH - 1][0:1, 0:1].astype(F32)

    smalls, saves, bigs = [], [], []
    h = x[0]

    flight = None

    def gathered(handles, segments, after, name, next_layer=None, gate=None):
        nonlocal flight
        send_sems, recv_sems, pack_thru, lands, _ = handles
        pack_thru, lands = _ag_wait(send_sems, recv_sems, pack_thru, lands, after, "ag_wait_" + name)
        if next_layer is not None:
            flight = _ag_start(packs[next_layer], SEGS, pack_thru, "ag_start_%d" % next_layer)
            gate[0][gate[1]] = gate[0][gate[1]] + flight[4][0:1, 0:1]
        outs = _ag_finish(pack_thru, lands, tuple(SEGS[s] for s in segments))
        return _as_big({SEG_NAMES[s]: a for s, a in zip(segments, outs)})

    for l in range(DEPTH):
        small = {n: W[n][l][None] for n in ('ffn1_norm', 'mix_norm', 'conv_b', 'glu_b', 'conv_out_norm',
                                            'ssm_out_norm', 'ffn2_norm', 'ple_norm')}
        small['conv_w'] = conv_full[l]
        small['dvec'] = W['ssm_D'][l].reshape(1, SSM_W)
        small['disc_in'], small['ltab'], small['ltab_rev'], small['bbmat'], small['ccmat'] = s5[l]
        big = {}
        bigs.append(big)
        if l == 0:
            def arrive(stage, h_now, big=big, small=small):
                big.update(gathered(first[stage], FIRST_LAYER_GROUPS[stage], prepared if stage == 0 else h_now,
                                    "0%s" % "abc"[stage], *((1, (small, 'ffn2_norm')) if stage == 2 else ())))
            h, saved = _layer_fwd(h, p[l, 0], small, big, arrive)
        else:
            nxt = (l + 1, (small, 'ffn1_norm')) if l + 1 < DEPTH else ()
            big.update(gathered(flight, range(len(SEGS)), h, "%d" % l, *nxt))
            h, saved = _layer_fwd(h, p[l, 0], small, big)
        smalls.append(small)
        saves.append(saved)
    loss_tile, dh, d_final = _final_loss(h, W['final_norm'][None], loss_target[0])
    loss = lax.psum(loss_tile[0, 0], ("x", "y", "c"))

    layer_gs = [None] * DEPTH
    shard_grads = [None] * DEPTH
    zero = jnp.zeros((1, 1), F32)
    sib, ici = None, None

    def finish_sibling(after_sib, after_ici):
        nonlocal sib, ici
        up, (send_sem, recv_sem, fulls_thru, land, _) = sib
        fulls_thru, got = _rs_sibling_wait(send_sem, recv_sem, fulls_thru, land, after_sib, "sib_wait_%d" % up)
        own32, pbf = _pair_sum(fulls_thru, got, SEGS)
        done = finish_chips(own32)
        ici = (up, _rs_chips_start(pbf, after_ici if done is None else done, "rs_start_%d" % up), own32)
        sib = None

    def finish_chips(after):
        nonlocal ici
        if ici is None:
            return None
        up, (send_sems, recv_sems, pbf_thru, land, _), own32 = ici
        got3 = _rs_chips_wait(send_sems, recv_sems, pbf_thru, land, after, "rs_wait_%d" % up)
        shard_grads[up] = _chip_sum(own32, got3)
        ici = None
        return shard_grads[up]

    layer_names = [n for n in SMALL_NAMES if n != 'final_norm']
    small_flights = [None] * DEPTH
    for l in reversed(range(DEPTH)):
        small = dict(smalls[l])
        if sib is not None:
            small['ple_norm'] = small['ple_norm'] + sib[1][4][0:1, 0:1] + small_flights[l + 1][4][0:1, 0:1]
        dh, top = _layer_bwd_top(dh, p[l, 0], small, bigs[l], saves[l])
        if sib is not None:
            finish_sibling(dh, dh)
            small['glu_b'] = small['glu_b'] + ici[1][4][0:1, 0:1]
        dh, fulls, layer_gs[l] = _layer_bwd_rest(dh, top, small, bigs[l], saves[l])
        sib = (l, _rs_sibling_start(fulls, SEGS, "sib_start_%d" % l))
        last_slot = d_final[0] if l == DEPTH - 1 else jnp.zeros((D_MODEL,), F32)
        flat = jnp.concatenate([layer_gs[l][n].reshape(-1) for n in layer_names + ['conv_w']] + [last_slot])
        small_flights[l] = _small_gather_start(_pad_rows(flat, SUBLANES, D_MODEL), "small_start_%d" % l)
    grad_x = dh[None]
    finish_sibling(small_flights[0][4], small_flights[0][4])

    reduced = []
    for l in range(DEPTH):
        send_sems, recv_sems, flat_thru, land, _ = small_flights[l]
        flat_thru, land = _small_gather_wait(send_sems, recv_sems, flat_thru, land, ici[1][4], "small_wait_%d" % l)
        reduced.append(_sum_devices(land, flat_thru).reshape(-1))
    reduced = jnp.stack(reduced)
    G = {}
    o = 0
    for n in layer_names + ['conv_w']:
        size = (W[n].size if n != 'conv_w' else DEPTH * 3 * CONV_W) // DEPTH
        shape = Wv[n].shape if n != 'conv_w' else (DEPTH, 3, CONV_W)
        G[n] = reduced[:, o:o + size].reshape(shape)
        o += size
    G['final_norm'] = reduced[DEPTH - 1, o:o + D_MODEL]
    G['conv_w'] = lax.dynamic_slice_in_dim(G['conv_w'], my_dev * (CONV_W // N_DEV), CONV_W // N_DEV, axis=2)

    delta, new_m, new_v = {}, {}, {}
    for n in SMALL_NAMES + ['conv_w']:
        two = (lambda t: t.reshape(1, -1) if t.ndim == 1 else t)
        delta[n], new_m[n], new_v[n] = [t.reshape(Wv[n].shape) for t in
                                        _adamw_any(two(Wv[n]), two(G[n]), two(Mv[n]), two(Vv[n]))]

    def unpack(sg):
        nl = sg.shape[0]
        offs = _seg_offsets(SEGS)
        r = SEGS[0][1]
        out = {}
        for a, f in ((0, 'ffn1'), (1, 'ffn2')):
            out[f + '_w_gate'] = sg[:, offs[a]:offs[a] + r]
            out[f + '_w_up'] = sg[:, offs[a] + r:offs[a] + 2 * r]
            out[f + '_w_down'] = sg[:, offs[a] + 2 * r:offs[a] + 3 * r]
        out['w_in'] = _tp(sg[:, offs[2]:offs[2] + SEGS[2][1]])
        out['w_out'] = sg[:, offs[3]:offs[3] + SEGS[3][1]]
        out['ple_w_gate'] = sg[:, offs[4]:offs[4] + SEGS[4][1]]
        out['ple_w_proj'] = _tp(sg[:, offs[5]:offs[5] + SEGS[5][1]].reshape(nl, D_MODEL // N_DEV, PLE_DIM))
        out['glu_w'] = sg[:, offs[6]:offs[6] + SEGS[6][1]].reshape(nl, SSM_W // N_DEV, SSM_W)
        return out

    upper = unpack(jnp.stack(shard_grads[1:]))
    groups = {}
    for n in upper:
        groups.setdefault(Wv[n].shape, []).append(n)
    part = {shape: _adamw_layers([(Wv[n], Mv[n], Vv[n], upper[n]) for n in ns], 1, None)
            for shape, ns in groups.items()}
    finish_chips(sum(four[3][1, 0:1, 0:1] for fours in part.values() for four in fours))
    lower = unpack(shard_grads[0][None])
    for shape, ns in groups.items():
        done = _adamw_layers([(Wv[n], Mv[n], Vv[n], lower[n]) for n in ns], 0, part[shape])
        for n, four in zip(ns, done):
            G[n], delta[n], new_m[n], new_v[n] = four

    outs = [[_view(n, d[n]) for n in W_NAMES] for d in (G, delta, new_m, new_v)]
    return (loss, grad_x, *outs[0], *outs[1], *outs[2], *outs[3])
```

```python
import math

import jax
import jax.numpy as jnp
from jax import lax
from jax.experimental import pallas as pl
from jax.experimental.pallas import tpu as pltpu

F32 = jnp.float32
BF16 = jnp.bfloat16

N_DEV = 8
DEPTH = 4
SEQ = 2048
D_MODEL = 1024
D_FF = 2816
CONV_W = 512
SSM_W = 512
SSM_GROUPS = 32
SSM_GROUP = 16
SSM_STATE = 64
N_STATE = SSM_GROUPS * SSM_STATE
IN_COLS = 2048
PLE_DIM = 256
EPS = 1e-6

ADAM_LR = 0.001
ADAM_B1 = 0.9
ADAM_B2 = 0.999
ADAM_EPS = 1e-08
ADAM_WD = 0.01
ADAM_STEP = 10

FF_BLOCK = 256
N_FF_BLOCKS = D_FF // FF_BLOCK
TOK_TILE_FFN_FWD = 2048
TOK_TILE_FFN_BWD = 1024
TOK_TILE = 512
CHUNK = 256
N_CHUNKS = SEQ // CHUNK
LANE_GROUP = 512
SUBLANES = 8
LANES = 128
MIB = 1024 * 1024

W_NAMES = ['ffn1_norm', 'ffn1_w_gate', 'ffn1_w_up', 'ffn1_w_down', 'mix_norm', 'w_in', 'conv_w', 'conv_b',
           'ssm_A_re', 'ssm_A_im', 'ssm_B_re', 'ssm_B_im', 'ssm_C_re', 'ssm_C_im', 'ssm_D', 'ssm_log_dt',
           'glu_w', 'glu_b', 'conv_out_norm', 'ssm_out_norm', 'w_out', 'ffn2_norm', 'ffn2_w_gate', 'ffn2_w_up',
           'ffn2_w_down', 'ple_norm', 'ple_w_gate', 'ple_w_proj', 'final_norm']
SMALL_NAMES = ['ffn1_norm', 'mix_norm', 'conv_b', 'ssm_A_re', 'ssm_A_im', 'ssm_B_re', 'ssm_B_im', 'ssm_C_re',
               'ssm_C_im', 'ssm_D', 'ssm_log_dt', 'glu_b', 'conv_out_norm', 'ssm_out_norm', 'ffn2_norm',
               'ple_norm', 'final_norm']

SEGS = ((3, 352), (3, 352), (1, 256), (1, 128), (1, 128), (1, 32), (1, 32))
PACK_ROWS = sum(n * r for n, r in SEGS)

MESH = pl.DeviceIdType.MESH
UNREAD = pl.BlockSpec(memory_space=pltpu.HBM)


def _in_hbm(*arrays):
    return [pltpu.with_memory_space_constraint(a, pltpu.HBM) for a in arrays]


def _out_hbm(outs, which):
    if not isinstance(outs, (list, tuple)):
        return pltpu.with_memory_space_constraint(outs, pltpu.HBM) if which else outs
    return [pltpu.with_memory_space_constraint(a, pltpu.HBM) if i in which else a for i, a in enumerate(outs)]


def _cparams(sem=None, vmem_mib=48, **kw):
    return pltpu.CompilerParams(dimension_semantics=sem, vmem_limit_bytes=vmem_mib * MIB, **kw)


def _dot(a, b):
    return jnp.dot(a, b, preferred_element_type=F32)


def _dot_nt(a, b):
    return lax.dot_general(a, b, (((1,), (1,)), ((), ())), preferred_element_type=F32)


def _dot_tn(a, b):
    return lax.dot_general(a, b, (((0,), (0,)), ((), ())), preferred_element_type=F32)


def _rms_stats(x):
    r = lax.rsqrt(jnp.mean(x * x, axis=-1, keepdims=True) + EPS)
    return x * r, r


def _rms_bwd(dy, xh, r, g):
    dxh = dy * g
    dx = r * (dxh - xh * jnp.mean(dxh * xh, axis=-1, keepdims=True))
    dg = jnp.sum(dy * xh, axis=0, keepdims=True)
    return dx, dg


def _sigmoid(x):
    return 0.5 * jnp.tanh(0.5 * x) + 0.5


_GELU_C = math.sqrt(2.0 / math.pi)


def _gelu(x):
    t = jnp.tanh(_GELU_C * (x + 0.044715 * x * x * x))
    return 0.5 * x * (1.0 + t), t


def _gelu_grad(x, t):
    return 0.5 * (1.0 + t) + 0.5 * x * (1.0 - t * t) * _GELU_C * (1.0 + 3.0 * 0.044715 * x * x)


def _accumulate(ref, first, value):
    @pl.when(first)
    def _():
        ref[...] = value

    @pl.when(jnp.logical_not(first))
    def _():
        ref[...] += value


def _ffn_fwd(h, g, w3):
    tm = TOK_TILE_FFN_FWD
    last = N_FF_BLOCKS - 1

    def body(h_ref, g_ref, wgu_ref, wd_ref, wd_last_ref, out_ref, gu_ref, u_ref, a_ref):
        k = pl.program_id(1)

        @pl.when(k == 0)
        def _():
            x = h_ref[...]
            xh, _ = _rms_stats(x)
            u_ref[...] = (xh * g_ref[...]).astype(BF16)
            out_ref[...] = x
            a_ref[1] = jnp.zeros((tm, FF_BLOCK), BF16)

        out_ref[...] += 0.5 * _dot(a_ref[(k + 1) % 2], wd_ref[0])
        gu = _dot_nt(u_ref[...], wgu_ref[...].reshape(2 * FF_BLOCK, D_MODEL))
        gate, up = gu[:, :FF_BLOCK], gu[:, FF_BLOCK:]
        a_ref[k % 2] = (gate * _sigmoid(gate) * up).astype(BF16)
        gu_ref[0] = gate.astype(BF16)
        gu_ref[1] = up.astype(BF16)

        @pl.when(k == last)
        def _():
            out_ref[...] += 0.5 * _dot(a_ref[last % 2], wd_last_ref[0])

    return _out_hbm(pl.pallas_call(
        body, name="ffn_fwd",
        grid=(SEQ // tm, N_FF_BLOCKS),
        in_specs=[pl.BlockSpec((tm, D_MODEL), lambda m, k: (m, 0), pipeline_mode=pl.Buffered(1)),
                  pl.BlockSpec((1, D_MODEL), lambda m, k: (0, 0)),
                  pl.BlockSpec((2, FF_BLOCK, D_MODEL), lambda m, k: (0, k, 0)),
                  pl.BlockSpec((1, FF_BLOCK, D_MODEL), lambda m, k: (2, jnp.maximum(k - 1, 0), 0)),
                  pl.BlockSpec((1, FF_BLOCK, D_MODEL), lambda m, k: (2, last, 0), pipeline_mode=pl.Buffered(1))],
        out_specs=[pl.BlockSpec((tm, D_MODEL), lambda m, k: (m, 0)),
                   pl.BlockSpec((2, tm, FF_BLOCK), lambda m, k: (0, m, k))],
        out_shape=[jax.ShapeDtypeStruct((SEQ, D_MODEL), F32),
                   pltpu.HBM((2, SEQ, D_FF), BF16)],
        scratch_shapes=[pltpu.VMEM((tm, D_MODEL), BF16), pltpu.VMEM((2, tm, FF_BLOCK), BF16)],
        compiler_params=_cparams(("parallel", "arbitrary"), 56),
    )(*_in_hbm(h, g, w3, w3, w3)), (1,))


def _ffn_bwd_act(h, g, dout, gu, w3):
    tm = TOK_TILE_FFN_BWD
    last = N_FF_BLOCKS - 1

    def body(h_ref, g_ref, d_ref, gu_ref, wd_ref, wgu_ref, wgu_last_ref, dh_ref, dga_ref, ud_ref, dg_ref,
             acc_ref, dgu_ref):
        m = pl.program_id(0)
        k = pl.program_id(1)

        @pl.when(k == 0)
        def _():
            xh, _ = _rms_stats(h_ref[...])
            ud_ref[0] = (xh * g_ref[...]).astype(BF16)
            ud_ref[1] = (0.5 * d_ref[...]).astype(BF16)
            acc_ref[...] = jnp.zeros_like(acc_ref)
            dgu_ref[1] = jnp.zeros((tm, 2 * FF_BLOCK), BF16)

        acc_ref[...] += _dot(dgu_ref[(k + 1) % 2], wgu_ref[...].reshape(2 * FF_BLOCK, D_MODEL))
        gate = gu_ref[0].astype(F32)
        up = gu_ref[1].astype(F32)
        sg = _sigmoid(gate)
        silu = gate * sg
        da = _dot_nt(ud_ref[1], wd_ref[0])
        dgate = (da * up * (sg + silu * (1.0 - sg))).astype(BF16)
        dup = (da * silu).astype(BF16)
        dga_ref[0] = dgate
        dga_ref[1] = dup
        dga_ref[2] = (silu * up).astype(BF16)
        dgu_ref[k % 2, :, 0:FF_BLOCK] = dgate
        dgu_ref[k % 2, :, FF_BLOCK:2 * FF_BLOCK] = dup

        @pl.when(k == last)
        def _():
            du = acc_ref[...] + _dot(dgu_ref[last % 2], wgu_last_ref[...].reshape(2 * FF_BLOCK, D_MODEL))
            xh, r = _rms_stats(h_ref[...])
            dx, dg = _rms_bwd(du, xh, r, g_ref[...])
            dh_ref[...] = d_ref[...] + dx
            _accumulate(dg_ref, m == 0, dg)

    return _out_hbm(pl.pallas_call(
        body, name="ffn_bwd_act",
        grid=(SEQ // tm, N_FF_BLOCKS),
        in_specs=[pl.BlockSpec((tm, D_MODEL), lambda m, k: (m, 0), pipeline_mode=pl.Buffered(1)),
                  pl.BlockSpec((1, D_MODEL), lambda m, k: (0, 0)),
                  pl.BlockSpec((tm, D_MODEL), lambda m, k: (m, 0), pipeline_mode=pl.Buffered(1)),
                  pl.BlockSpec((2, tm, FF_BLOCK), lambda m, k: (0, m, k)),
                  pl.BlockSpec((1, FF_BLOCK, D_MODEL), lambda m, k: (2, k, 0)),
                  pl.BlockSpec((2, FF_BLOCK, D_MODEL), lambda m, k: (0, jnp.maximum(k - 1, 0), 0)),
                  pl.BlockSpec((2, FF_BLOCK, D_MODEL), lambda m, k: (0, last, 0), pipeline_mode=pl.Buffered(1))],
        out_specs=[pl.BlockSpec((tm, D_MODEL), lambda m, k: (m, 0)),
                   pl.BlockSpec((3, tm, FF_BLOCK), lambda m, k: (0, m, k)),
                   pl.BlockSpec((2, tm, D_MODEL), lambda m, k: (0, m, 0)),
                   pl.BlockSpec((1, D_MODEL), lambda m, k: (0, 0))],
        out_shape=[jax.ShapeDtypeStruct((SEQ, D_MODEL), F32),
                   pltpu.HBM((3, SEQ, D_FF), BF16),
                   pltpu.HBM((2, SEQ, D_MODEL), BF16),
                   jax.ShapeDtypeStruct((1, D_MODEL), F32)],
        scratch_shapes=[pltpu.VMEM((tm, D_MODEL), F32), pltpu.VMEM((2, tm, 2 * FF_BLOCK), BF16)],
        compiler_params=_cparams(("arbitrary", "arbitrary"), 56),
    )(*_in_hbm(h, g, dout, gu, w3, w3, w3)), (1, 2))


def _matmul_tn(a, b, bm, out_dtype, name, bn=None, to_kernel=True):
    na, t, m = a.shape
    nb, _, n = b.shape
    bn = n if bn is None else bn

    def body(a_ref, b_ref, o_ref):
        o_ref[0] = _dot_tn(a_ref[0], b_ref[0]).astype(out_dtype)

    return _out_hbm(pl.pallas_call(
        body, name=name,
        grid=(na, m // bm, n // bn),
        in_specs=[pl.BlockSpec((1, t, bm), lambda i, k, j: (i, 0, k)),
                  pl.BlockSpec((1, t, bn), lambda i, k, j: (jnp.maximum(i - (na - nb), 0), 0, j))],
        out_specs=pl.BlockSpec((1, bm, bn), lambda i, k, j: (i, k, j)),
        out_shape=pltpu.HBM((na, m, n), out_dtype) if to_kernel else jax.ShapeDtypeStruct((na, m, n), out_dtype),
        compiler_params=_cparams(("arbitrary", "parallel", "parallel")),
    )(*_in_hbm(a, b)), to_kernel)


def _inproj_fwd(h, g, wint):
    tm = TOK_TILE

    def body(h_ref, g_ref, w_ref, z_ref):
        xh, _ = _rms_stats(h_ref[...])
        z_ref[...] = _dot_nt((xh * g_ref[...]).astype(BF16), w_ref[...])

    return pl.pallas_call(
        body, name="inproj_fwd",
        grid=(SEQ // tm,),
        in_specs=[pl.BlockSpec((tm, D_MODEL), lambda m: (m, 0)),
                  pl.BlockSpec((1, D_MODEL), lambda m: (0, 0)),
                  pl.BlockSpec((None, IN_COLS, D_MODEL), lambda m: (0, 0, 0))],
        out_specs=pl.BlockSpec((tm, IN_COLS), lambda m: (m, 0)),
        out_shape=jax.ShapeDtypeStruct((SEQ, IN_COLS), F32),
        compiler_params=_cparams(("parallel",)),
    )(*_in_hbm(h, g, wint))


def _inproj_bwd(h, g, dh, dz, wint):
    tm = TOK_TILE

    def body(h_ref, g_ref, dh_ref, dz_ref, w_ref, o_ref, u_ref, dg_ref):
        xh, r = _rms_stats(h_ref[...])
        u_ref[0] = (xh * g_ref[...]).astype(BF16)
        dx, dg = _rms_bwd(_dot(dz_ref[...], w_ref[...]), xh, r, g_ref[...])
        o_ref[...] = dh_ref[...] + dx
        _accumulate(dg_ref, pl.program_id(0) == 0, dg)

    return _out_hbm(pl.pallas_call(
        body, name="inproj_bwd",
        grid=(SEQ // tm,),
        in_specs=[pl.BlockSpec((tm, D_MODEL), lambda m: (m, 0)),
                  pl.BlockSpec((1, D_MODEL), lambda m: (0, 0)),
                  pl.BlockSpec((tm, D_MODEL), lambda m: (m, 0)),
                  pl.BlockSpec((tm, IN_COLS), lambda m: (m, 0)),
                  pl.BlockSpec((None, IN_COLS, D_MODEL), lambda m: (0, 0, 0))],
        out_specs=[pl.BlockSpec((tm, D_MODEL), lambda m: (m, 0)),
                   pl.BlockSpec((1, tm, D_MODEL), lambda m: (0, m, 0)),
                   pl.BlockSpec((1, D_MODEL), lambda m: (0, 0))],
        out_shape=[jax.ShapeDtypeStruct((SEQ, D_MODEL), F32),
                   pltpu.HBM((1, SEQ, D_MODEL), BF16),
                   jax.ShapeDtypeStruct((1, D_MODEL), F32)],
        compiler_params=_cparams(("arbitrary",)),
    )(*_in_hbm(h, g, dh, dz, wint)), (1,))


def _row_ids(n, w):
    return lax.broadcasted_iota(jnp.int32, (n, w), 0)


def _bcast_row(x, i, n):
    return jnp.broadcast_to(x[i:i + 1, :], (n, x.shape[1]))


def _conv_taps(v, tail):
    n, w = v.shape
    rid = _row_ids(n, w)
    v1 = jnp.where(rid == 0, _bcast_row(tail, 7, n), pltpu.roll(v, 1, 0))
    v2 = jnp.where(rid == 0, _bcast_row(tail, 6, n),
                   jnp.where(rid == 1, _bcast_row(tail, 7, n), pltpu.roll(v, 2, 0)))
    return v1, v2


def _block_tiles():
    half_rows, half_cols = SSM_W // 2, N_STATE // 2
    for half in range(2):
        for part in range(2):
            yield (slice(half * half_rows, (half + 1) * half_rows),
                   slice(part * N_STATE + half * half_cols, part * N_STATE + (half + 1) * half_cols))


def _block_expand(x, mat_ref, out_ref):
    for rows, cols in _block_tiles():
        out_ref[:, cols] = _dot(x[:, rows], mat_ref[rows, cols])


def _block_contract(s, mat_ref):
    halves = {}
    for rows, cols in _block_tiles():
        part = _dot_nt(s[:, cols], mat_ref[rows, cols])
        halves[rows.start] = part if rows.start not in halves else halves[rows.start] + part
    return jnp.concatenate([halves[k] for k in sorted(halves)], axis=1)


def _block_wgrad(a, b, name):
    t = a.shape[1]
    half_rows, half_cols = SSM_W // 2, N_STATE // 2

    def body(a_ref, b_ref, o_ref):
        o_ref[...] = _dot_tn(a_ref[...], b_ref[...])

    return pl.pallas_call(
        body, name=name,
        grid=(2, 2),
        in_specs=[pl.BlockSpec((None, t, half_rows), lambda h, p: (0, 0, h)),
                  pl.BlockSpec((None, t, half_cols), lambda h, p: (0, 0, 2 * p + h))],
        out_specs=pl.BlockSpec((half_rows, half_cols), lambda h, p: (h, 2 * p + h)),
        out_shape=jax.ShapeDtypeStruct((SSM_W, 2 * N_STATE), F32),
        compiler_params=_cparams(("parallel", "parallel")),
    )(*_in_hbm(a, b))


def _scan_chunk(work, ltab, carry, reverse):
    nblk = CHUNK // SUBLANES
    for gi in range(N_STATE // LANE_GROUP):
        cre = pl.ds(gi * LANE_GROUP, LANE_GROUP)
        cim = pl.ds(N_STATE + gi * LANE_GROUP, LANE_GROUP)
        pows = [(ltab[8 * k:8 * k + 8, cre], ltab[8 * k:8 * k + 8, cim]) for k in range(3)]
        pr = ltab[24:32, cre]
        pi = ltab[24:32, cim]

        def blk(i, c, cre=cre, cim=cim, pows=pows, pr=pr, pi=pi):
            cr, ci = c
            b = (nblk - 1 - i) if reverse else i
            r0 = pl.multiple_of(b * SUBLANES, SUBLANES)
            xr = work[pl.ds(r0, SUBLANES), cre]
            xi = work[pl.ds(r0, SUBLANES), cim]
            for k, s in enumerate((1, 2, 4)):
                lr, li = pows[k]
                shift = SUBLANES - s if reverse else s
                sr = pltpu.roll(xr, shift, 0)
                si = pltpu.roll(xi, shift, 0)
                xr, xi = xr + lr * sr - li * si, xi + lr * si + li * sr
            xr, xi = xr + pr * cr - pi * ci, xi + pr * ci + pi * cr
            work[pl.ds(r0, SUBLANES), cre] = xr
            work[pl.ds(r0, SUBLANES), cim] = xi
            edge = 0 if reverse else SUBLANES - 1
            return _bcast_row(xr, edge, SUBLANES), _bcast_row(xi, edge, SUBLANES)

        cr, ci = lax.fori_loop(0, nblk, blk, (carry[:, cre], carry[:, cim]))
        carry[:, cre] = cr
        carry[:, cim] = ci


def _s5conv_fwd(z, convw, convb, bbmat, ccmat, dvec, ltab):
    def body(z_ref, cw_ref, cb_ref, bb_ref, cc_ref, d_ref, lt_ref, ya_ref, ys_ref, hs_ref,
             work, carry, tail):
        c = pl.program_id(0)

        @pl.when(c == 0)
        def _():
            carry[...] = jnp.zeros_like(carry)
            tail[...] = jnp.zeros_like(tail)

        zb = z_ref[:, 0:CONV_W]
        v = z_ref[:, CONV_W:2 * CONV_W] * z_ref[:, 2 * CONV_W:3 * CONV_W]
        us = z_ref[:, 3 * CONV_W:4 * CONV_W]
        v1, v2 = _conv_taps(v, tail[...])
        tail[...] = v[CHUNK - 8:CHUNK, :]
        y = cw_ref[0:1, :] * v2 + cw_ref[1:2, :] * v1 + cw_ref[2:3, :] * v
        ya_ref[...] = zb * (y + cb_ref[...])

        _block_expand(us.astype(BF16), bb_ref, work)
        _scan_chunk(work, lt_ref, carry, reverse=False)
        hs = work[...].astype(BF16)
        hs_ref[...] = hs
        ys_ref[...] = _block_contract(hs, cc_ref) + d_ref[...] * us

    return _out_hbm(pl.pallas_call(
        body, name="s5conv_fwd",
        grid=(N_CHUNKS,),
        in_specs=[pl.BlockSpec((CHUNK, IN_COLS), lambda c: (c, 0)),
                  pl.BlockSpec((3, CONV_W), lambda c: (0, 0)),
                  pl.BlockSpec((1, CONV_W), lambda c: (0, 0)),
                  pl.BlockSpec((SSM_W, 2 * N_STATE), lambda c: (0, 0)),
                  pl.BlockSpec((SSM_W, 2 * N_STATE), lambda c: (0, 0)),
                  pl.BlockSpec((1, SSM_W), lambda c: (0, 0)),
                  pl.BlockSpec((32, 2 * N_STATE), lambda c: (0, 0))],
        out_specs=[pl.BlockSpec((CHUNK, CONV_W), lambda c: (c, 0)),
                   pl.BlockSpec((CHUNK, SSM_W), lambda c: (c, 0)),
                   pl.BlockSpec((CHUNK, 2 * N_STATE), lambda c: (c, 0))],
        out_shape=[pltpu.HBM((SEQ, CONV_W), F32),
                   pltpu.HBM((SEQ, SSM_W), F32),
                   jax.ShapeDtypeStruct((SEQ, 2 * N_STATE), BF16)],
        scratch_shapes=[pltpu.VMEM((CHUNK, 2 * N_STATE), F32),
                        pltpu.VMEM((8, 2 * N_STATE), F32),
                        pltpu.VMEM((8, CONV_W), F32)],
        compiler_params=_cparams(("arbitrary",)),
    )(*_in_hbm(z, convw, convb, bbmat, ccmat, dvec, ltab)), (0, 1))


def _s5conv_bwd(z, hs, dya, dys, convw, convb, bbmat, ccmat, dvec, ltab_rev):
    nc = N_CHUNKS
    hb = 16

    def body(z_ref, zp_ref, hs_ref, hp_ref, dya_ref, dys_ref, cw_ref, cb_ref, bb_ref, cc_ref, d_ref, lt_ref,
             dz_ref, g_ref, us_ref, dyb_ref, dl_ref, dcw_ref, work, carry, head):
        i = pl.program_id(0)
        first_chunk = i == nc - 1

        @pl.when(i == 0)
        def _():
            carry[...] = jnp.zeros_like(carry)
            head[...] = jnp.zeros_like(head)
            dl_ref[...] = jnp.zeros_like(dl_ref)
            dcw_ref[...] = jnp.zeros_like(dcw_ref)

        us = z_ref[:, 3 * CONV_W:4 * CONV_W]
        dy = dys_ref[...]
        dy_bf = dy.astype(BF16)
        us_ref[0] = us.astype(BF16)
        dyb_ref[0] = dy_bf

        _block_expand(dy_bf, cc_ref, work)
        _scan_chunk(work, lt_ref, carry, reverse=True)
        gg = work[...]
        gg_bf = gg.astype(BF16)
        g_ref[0] = gg_bf
        dus = d_ref[...] * dy + _block_contract(gg_bf, bb_ref)

        hcur = hs_ref[...].astype(F32)
        hlast = hp_ref[...].astype(F32)[hb - 1:hb, :]
        hlast = jnp.where(first_chunk, 0.0, hlast)
        rid = _row_ids(CHUNK, 2 * N_STATE)
        hprev = jnp.where(rid == 0, jnp.broadcast_to(hlast, (CHUNK, 2 * N_STATE)), pltpu.roll(hcur, 1, 0))
        gr, gi = gg[:, :N_STATE], gg[:, N_STATE:]
        hr, hi = hprev[:, :N_STATE], hprev[:, N_STATE:]
        dl_ref[:, :N_STATE] += (gr * hr + gi * hi).reshape(CHUNK // 8, 8, N_STATE).sum(axis=0)
        dl_ref[:, N_STATE:] += (gi * hr - gr * hi).reshape(CHUNK // 8, 8, N_STATE).sum(axis=0)

        @pl.when(i == nc - 1)
        def _():
            dl_ref[0:1, :] = jnp.sum(dl_ref[...], axis=0, keepdims=True)

        zb = z_ref[:, 0:CONV_W]
        zc = z_ref[:, CONV_W:2 * CONV_W]
        zv = z_ref[:, 2 * CONV_W:3 * CONV_W]
        v = zc * zv
        vtail = jnp.where(first_chunk, 0.0, zp_ref[:, CONV_W:2 * CONV_W] * zp_ref[:, 2 * CONV_W:3 * CONV_W])
        v1, v2 = _conv_taps(v, vtail)
        w0, w1, w2 = cw_ref[0:1, :], cw_ref[1:2, :], cw_ref[2:3, :]
        y = w0 * v2 + w1 * v1 + w2 * v
        dya_v = dya_ref[...]
        dzb = dya_v * (y + cb_ref[...])
        dyc = dya_v * zb
        hd = head[...]
        rc = _row_ids(CHUNK, CONV_W)
        n1 = jnp.where(rc == CHUNK - 1, _bcast_row(hd, 0, CHUNK), pltpu.roll(dyc, CHUNK - 1, 0))
        n2 = jnp.where(rc == CHUNK - 1, _bcast_row(hd, 1, CHUNK),
                       jnp.where(rc == CHUNK - 2, _bcast_row(hd, 0, CHUNK), pltpu.roll(dyc, CHUNK - 2, 0)))
        head[...] = dyc[0:8, :]
        dv = w2 * dyc + w1 * n1 + w0 * n2
        dz_ref[:, 0:CONV_W] = dzb.astype(BF16)
        dz_ref[:, CONV_W:2 * CONV_W] = (dv * zv).astype(BF16)
        dz_ref[:, 2 * CONV_W:3 * CONV_W] = (dv * zc).astype(BF16)
        dz_ref[:, 3 * CONV_W:4 * CONV_W] = dus.astype(BF16)
        dcw_ref[0:1, :] += jnp.sum(dyc * v2, axis=0, keepdims=True)
        dcw_ref[1:2, :] += jnp.sum(dyc * v1, axis=0, keepdims=True)
        dcw_ref[2:3, :] += jnp.sum(dyc * v, axis=0, keepdims=True)
        dcw_ref[3:4, :] += jnp.sum(dyc, axis=0, keepdims=True)
        dcw_ref[4:5, :] += jnp.sum(dy * us, axis=0, keepdims=True)

    rev = lambda i: nc - 1 - i
    return _out_hbm(pl.pallas_call(
        body, name="s5conv_bwd",
        grid=(nc,),
        in_specs=[pl.BlockSpec((CHUNK, IN_COLS), lambda i: (rev(i), 0)),
                  pl.BlockSpec((8, IN_COLS), lambda i: (jnp.maximum(rev(i) * (CHUNK // 8) - 1, 0), 0)),
                  pl.BlockSpec((CHUNK, 2 * N_STATE), lambda i: (rev(i), 0)),
                  pl.BlockSpec((hb, 2 * N_STATE), lambda i: (jnp.maximum(rev(i) * (CHUNK // hb) - 1, 0), 0)),
                  pl.BlockSpec((CHUNK, CONV_W), lambda i: (rev(i), 0)),
                  pl.BlockSpec((CHUNK, SSM_W), lambda i: (rev(i), 0)),
                  pl.BlockSpec((3, CONV_W), lambda i: (0, 0)),
                  pl.BlockSpec((1, CONV_W), lambda i: (0, 0)),
                  pl.BlockSpec((SSM_W, 2 * N_STATE), lambda i: (0, 0)),
                  pl.BlockSpec((SSM_W, 2 * N_STATE), lambda i: (0, 0)),
                  pl.BlockSpec((1, SSM_W), lambda i: (0, 0)),
                  pl.BlockSpec((32, 2 * N_STATE), lambda i: (0, 0))],
        out_specs=[pl.BlockSpec((CHUNK, IN_COLS), lambda i: (rev(i), 0)),
                   pl.BlockSpec((1, CHUNK, 2 * N_STATE), lambda i: (0, rev(i), 0)),
                   pl.BlockSpec((1, CHUNK, SSM_W), lambda i: (0, rev(i), 0)),
                   pl.BlockSpec((1, CHUNK, SSM_W), lambda i: (0, rev(i), 0)),
                   pl.BlockSpec((8, 2 * N_STATE), lambda i: (0, 0)),
                   pl.BlockSpec((8, CONV_W), lambda i: (0, 0))],
        out_shape=[jax.ShapeDtypeStruct((SEQ, IN_COLS), BF16),
                   pltpu.HBM((1, SEQ, 2 * N_STATE), BF16),
                   pltpu.HBM((1, SEQ, SSM_W), BF16),
                   pltpu.HBM((1, SEQ, SSM_W), BF16),
                   jax.ShapeDtypeStruct((8, 2 * N_STATE), F32),
                   jax.ShapeDtypeStruct((8, CONV_W), F32)],
        scratch_shapes=[pltpu.VMEM((CHUNK, 2 * N_STATE), F32),
                        pltpu.VMEM((8, 2 * N_STATE), F32),
                        pltpu.VMEM((8, CONV_W), F32)],
        compiler_params=_cparams(("arbitrary",)),
    )(*_in_hbm(z, z, hs, hs, dya, dys, convw, convb, bbmat, ccmat, dvec, ltab_rev)), (1, 2, 3))


def _mix_out_fwd(h, ya, ys, gluw, glub, con, son, wout):
    tm = TOK_TILE

    def body(h_ref, ya_ref, ys_ref, gw_ref, gb_ref, con_ref, son_ref, wo_ref, o_ref):
        zg, _ = _gelu(ys_ref[...])
        q = _dot(zg.astype(BF16), gw_ref[...]) + gb_ref[...]
        out_s = zg * _sigmoid(q)
        na, _ = _rms_stats(ya_ref[...])
        ns, _ = _rms_stats(out_s)
        o_ref[...] = (h_ref[...]
                      + _dot((na * con_ref[...]).astype(BF16), wo_ref[0:CONV_W, :])
                      + _dot((ns * son_ref[...]).astype(BF16), wo_ref[CONV_W:2 * CONV_W, :]))

    row = lambda m: (m, 0)
    fixed = lambda m: (0, 0)
    return pl.pallas_call(
        body, name="mix_out_fwd",
        grid=(SEQ // tm,),
        in_specs=[pl.BlockSpec((tm, D_MODEL), row), pl.BlockSpec((tm, CONV_W), row), pl.BlockSpec((tm, SSM_W), row),
                  pl.BlockSpec((SSM_W, SSM_W), fixed), pl.BlockSpec((1, SSM_W), fixed),
                  pl.BlockSpec((1, CONV_W), fixed), pl.BlockSpec((1, SSM_W), fixed),
                  pl.BlockSpec((None, D_MODEL, D_MODEL), lambda m: (0, 0, 0))],
        out_specs=pl.BlockSpec((tm, D_MODEL), row),
        out_shape=jax.ShapeDtypeStruct((SEQ, D_MODEL), F32),
        compiler_params=_cparams(("parallel",)),
    )(*_in_hbm(h, ya, ys, gluw, glub, con, son, wout))


def _mix_out_bwd(dh, ya, ys, gluw, glub, con, son, wout):
    tm = TOK_TILE

    def body(dh_ref, ya_ref, ys_ref, gw_ref, gb_ref, con_ref, son_ref, wo_ref,
             dya_ref, dys_ref, yc_ref, dhb_ref, zg_ref, dq_ref, part_ref):
        ysv = ys_ref[...]
        zg, th = _gelu(ysv)
        zg_bf = zg.astype(BF16)
        s = _sigmoid(_dot(zg_bf, gw_ref[...]) + gb_ref[...])
        out_s = zg * s
        na, ra = _rms_stats(ya_ref[...])
        ns, rs = _rms_stats(out_s)
        dh_bf = dh_ref[...].astype(BF16)
        yc_ref[0, :, 0:CONV_W] = (na * con_ref[...]).astype(BF16)
        yc_ref[0, :, CONV_W:2 * CONV_W] = (ns * son_ref[...]).astype(BF16)
        dhb_ref[0] = dh_bf
        dca = _dot_nt(dh_bf, wo_ref[0:CONV_W, :])
        dcs = _dot_nt(dh_bf, wo_ref[CONV_W:2 * CONV_W, :])
        dya, dcon = _rms_bwd(dca, na, ra, con_ref[...])
        dos, dson = _rms_bwd(dcs, ns, rs, son_ref[...])
        dya_ref[...] = dya
        dq = dos * zg * s * (1.0 - s)
        dq_bf = dq.astype(BF16)
        dzg = dos * s + _dot_nt(dq_bf, gw_ref[...])
        dys_ref[...] = dzg * _gelu_grad(ysv, th)
        zg_ref[0] = zg_bf
        dq_ref[0] = dq_bf
        rid = _row_ids(SUBLANES, SSM_W)
        part = jnp.zeros((SUBLANES, SSM_W), F32)
        for i, rowv in enumerate((dcon, dson, jnp.sum(dq, axis=0, keepdims=True))):
            part = jnp.where(rid == i, jnp.broadcast_to(rowv, (SUBLANES, SSM_W)), part)
        _accumulate(part_ref, pl.program_id(0) == 0, part)

    row = lambda m: (m, 0)
    fixed = lambda m: (0, 0)
    lead = lambda m: (0, m, 0)
    return _out_hbm(pl.pallas_call(
        body, name="mix_out_bwd",
        grid=(SEQ // tm,),
        in_specs=[pl.BlockSpec((tm, D_MODEL), row), pl.BlockSpec((tm, CONV_W), row), pl.BlockSpec((tm, SSM_W), row),
                  pl.BlockSpec((SSM_W, SSM_W), fixed), pl.BlockSpec((1, SSM_W), fixed),
                  pl.BlockSpec((1, CONV_W), fixed), pl.BlockSpec((1, SSM_W), fixed),
                  pl.BlockSpec((None, D_MODEL, D_MODEL), lambda m: (0, 0, 0))],
        out_specs=[pl.BlockSpec((tm, CONV_W), row), pl.BlockSpec((tm, SSM_W), row),
                   pl.BlockSpec((1, tm, D_MODEL), lead), pl.BlockSpec((1, tm, D_MODEL), lead),
                   pl.BlockSpec((1, tm, SSM_W), lead), pl.BlockSpec((1, tm, SSM_W), lead),
                   pl.BlockSpec((8, SSM_W), fixed)],
        out_shape=[pltpu.HBM((SEQ, CONV_W), F32), pltpu.HBM((SEQ, SSM_W), F32),
                   pltpu.HBM((1, SEQ, D_MODEL), BF16), pltpu.HBM((1, SEQ, D_MODEL), BF16),
                   pltpu.HBM((1, SEQ, SSM_W), BF16), pltpu.HBM((1, SEQ, SSM_W), BF16),
                   jax.ShapeDtypeStruct((8, SSM_W), F32)],
        compiler_params=_cparams(("arbitrary",)),
    )(*_in_hbm(dh, ya, ys, gluw, glub, con, son, wout)), (0, 1, 2, 3, 4, 5))


def _ple_fwd(h, g, p, wgate, wprojt):
    tm = TOK_TILE

    def body(h_ref, g_ref, p_ref, wg_ref, wp_ref, o_ref):
        x = h_ref[...]
        xh, _ = _rms_stats(x)
        s = _sigmoid(_dot((xh * g_ref[...]).astype(BF16), wg_ref[...]))
        o_ref[...] = x + _dot_nt(p_ref[...].astype(BF16), wp_ref[...]) * s

    row = lambda m: (m, 0)
    fixed = lambda m: (0, 0)
    return pl.pallas_call(
        body, name="ple_fwd",
        grid=(SEQ // tm,),
        in_specs=[pl.BlockSpec((tm, D_MODEL), row), pl.BlockSpec((1, D_MODEL), fixed), pl.BlockSpec((tm, PLE_DIM), row),
                  pl.BlockSpec((None, D_MODEL, D_MODEL), lambda m: (0, 0, 0)), pl.BlockSpec((D_MODEL, PLE_DIM), fixed)],
        out_specs=pl.BlockSpec((tm, D_MODEL), row),
        out_shape=jax.ShapeDtypeStruct((SEQ, D_MODEL), F32),
        compiler_params=_cparams(("parallel",)),
    )(*_in_hbm(h, g, p, wgate, wprojt))


def _ple_bwd(h, g, p, dh, wgate, wprojt):
    tm = TOK_TILE

    def body(h_ref, g_ref, p_ref, dh_ref, wg_ref, wp_ref, o_ref, u_ref, dq_ref, dpp_ref, pb_ref, dg_ref):
        xh, r = _rms_stats(h_ref[...])
        u = (xh * g_ref[...]).astype(BF16)
        s = _sigmoid(_dot(u, wg_ref[...]))
        p_bf = p_ref[...].astype(BF16)
        pp = _dot_nt(p_bf, wp_ref[...])
        dhv = dh_ref[...]
        dq = (dhv * pp * s * (1.0 - s)).astype(BF16)
        u_ref[0] = u
        dq_ref[0] = dq
        dpp_ref[0] = (dhv * s).astype(BF16)
        pb_ref[0] = p_bf
        dx, dg = _rms_bwd(_dot_nt(dq, wg_ref[...]), xh, r, g_ref[...])
        o_ref[...] = dhv + dx
        _accumulate(dg_ref, pl.program_id(0) == 0, dg)

    row = lambda m: (m, 0)
    fixed = lambda m: (0, 0)
    lead = lambda m: (0, m, 0)
    big = pltpu.HBM((1, SEQ, D_MODEL), BF16)
    return _out_hbm(pl.pallas_call(
        body, name="ple_bwd",
        grid=(SEQ // tm,),
        in_specs=[pl.BlockSpec((tm, D_MODEL), row), pl.BlockSpec((1, D_MODEL), fixed), pl.BlockSpec((tm, PLE_DIM), row),
                  pl.BlockSpec((tm, D_MODEL), row),
                  pl.BlockSpec((None, D_MODEL, D_MODEL), lambda m: (0, 0, 0)), pl.BlockSpec((D_MODEL, PLE_DIM), fixed)],
        out_specs=[pl.BlockSpec((tm, D_MODEL), row),
                   pl.BlockSpec((1, tm, D_MODEL), lead), pl.BlockSpec((1, tm, D_MODEL), lead),
                   pl.BlockSpec((1, tm, D_MODEL), lead), pl.BlockSpec((1, tm, PLE_DIM), lead),
                   pl.BlockSpec((1, D_MODEL), fixed)],
        out_shape=[jax.ShapeDtypeStruct((SEQ, D_MODEL), F32), big, big, big,
                   pltpu.HBM((1, SEQ, PLE_DIM), BF16),
                   jax.ShapeDtypeStruct((1, D_MODEL), F32)],
        compiler_params=_cparams(("arbitrary",)),
    )(*_in_hbm(h, g, p, dh, wgate, wprojt)), (1, 2, 3, 4))


def _final_loss(h, g, target):
    tm = TOK_TILE

    def body(h_ref, g_ref, t_ref, loss_ref, dh_ref, dg_ref):
        first = pl.program_id(0) == 0
        xh, r = _rms_stats(h_ref[...])
        diff = xh * g_ref[...] - t_ref[...]
        part = 0.5 * jnp.sum(jnp.mean(diff * diff, axis=-1, keepdims=True), axis=0, keepdims=True)
        _accumulate(loss_ref, first, jnp.broadcast_to(part, (SUBLANES, LANES)))
        dx, dg = _rms_bwd(diff * (1.0 / D_MODEL), xh, r, g_ref[...])
        dh_ref[...] = dx
        _accumulate(dg_ref, first, dg)

    row = lambda m: (m, 0)
    fixed = lambda m: (0, 0)
    return pl.pallas_call(
        body, name="final_loss",
        grid=(SEQ // tm,),
        in_specs=[pl.BlockSpec((tm, D_MODEL), row), pl.BlockSpec((1, D_MODEL), fixed),
                  pl.BlockSpec((tm, D_MODEL), row)],
        out_specs=[pl.BlockSpec((SUBLANES, LANES), fixed),
                   pl.BlockSpec((tm, D_MODEL), row),
                   pl.BlockSpec((1, D_MODEL), fixed)],
        out_shape=[jax.ShapeDtypeStruct((SUBLANES, LANES), F32),
                   jax.ShapeDtypeStruct((SEQ, D_MODEL), F32),
                   jax.ShapeDtypeStruct((1, D_MODEL), F32)],
        compiler_params=_cparams(("arbitrary",)),
    )(*_in_hbm(h, g, target))


def _disc(ar, ai, ldt):
    dt = jnp.exp(ldt)
    mag = jnp.exp(ar * dt)
    ph = ai * dt
    lr, li = mag * jnp.cos(ph), mag * jnp.sin(ph)
    nr, ni = lr - 1.0, li
    den = ar * ar + ai * ai
    return lr, li, (nr * ar + ni * ai) / den, (ni * ar - nr * ai) / den


def _s5_disc(a_row, ldt_row, a_rep, ldt_rep, bt, ct, tile_e, mask):
    n = N_STATE

    def body(ar_ref, lr_ref, ap_ref, lp_ref, b_ref, c_ref, e_ref, m_ref, lt_ref, ltr_ref, bb_ref, cc_ref):
        lr, li, _, _ = _disc(ar_ref[0], ar_ref[1], lr_ref[...])
        pr, pi = lr, li
        rid = _row_ids(SUBLANES, n)
        for k in range(1, 9):
            for ref, sgn, edge in ((lt_ref, 1.0, 24 + k - 1), (ltr_ref, -1.0, 24 + 8 - k)):
                if k in (1, 2, 4):
                    r0 = {1: 0, 2: 8, 4: 16}[k]
                    keep = (rid >= k) if ref is lt_ref else (rid < SUBLANES - k)
                    ref[r0:r0 + 8, 0:n] = jnp.where(keep, jnp.broadcast_to(pr, (8, n)), 0.0)
                    ref[r0:r0 + 8, n:2 * n] = jnp.where(keep, jnp.broadcast_to(sgn * pi, (8, n)), 0.0)
                ref[edge:edge + 1, 0:n] = pr
                ref[edge:edge + 1, n:2 * n] = sgn * pi
            pr, pi = pr * lr - pi * li, pr * li + pi * lr
        _, _, fr, fi = _disc(ap_ref[0], ap_ref[1], lp_ref[...])
        br, bi = b_ref[0], b_ref[1]
        e = e_ref[...]
        m = m_ref[...].astype(F32)
        bb_ref[:, 0:n] = (_dot((fr * br - fi * bi).astype(BF16), e) * m).astype(BF16)
        bb_ref[:, n:2 * n] = (_dot((fr * bi + fi * br).astype(BF16), e) * m).astype(BF16)
        cc_ref[:, 0:n] = (_dot(c_ref[0].astype(BF16), e) * m).astype(BF16)
        cc_ref[:, n:2 * n] = (-(_dot(c_ref[1].astype(BF16), e) * m)).astype(BF16)

    return pl.pallas_call(
        body, name="s5_disc",
        out_shape=[jax.ShapeDtypeStruct((32, 2 * n), F32), jax.ShapeDtypeStruct((32, 2 * n), F32),
                   jax.ShapeDtypeStruct((SSM_W, 2 * n), BF16), jax.ShapeDtypeStruct((SSM_W, 2 * n), BF16)],
        compiler_params=_cparams(None),
    )(a_row, ldt_row, a_rep, ldt_rep, bt, ct, tile_e, mask)


def _dot_exact(x, sel):
    hi = x.astype(BF16)
    r1 = x - hi.astype(F32)
    mid = r1.astype(BF16)
    lo = (r1 - mid.astype(F32)).astype(BF16)
    return _dot(hi, sel) + _dot(mid, sel) + _dot(lo, sel)


def _s5_disc_bwd(a, ldt, a_rep, ldt_rep, bt, mask, dl, d_bb, d_cc, fold):
    n = N_STATE

    def body(a_ref, l_ref, ap_ref, lp_ref, b_ref, m_ref, dl_ref, dbb_ref, dcc_ref, f_ref,
             da_ref, dldt_ref, db_ref, dc_ref):
        m = m_ref[...].astype(F32)
        fold_m = f_ref[...]
        diag = lambda x: _dot_exact(jnp.where(m > 0.0, x, 0.0), fold_m)
        dr, di = diag(dbb_ref[:, 0:n]), diag(dbb_ref[:, n:2 * n])
        dc_ref[0] = diag(dcc_ref[:, 0:n])
        dc_ref[1] = -diag(dcc_ref[:, n:2 * n])
        _, _, fr, fi = _disc(ap_ref[0], ap_ref[1], lp_ref[...])
        br, bi = b_ref[0], b_ref[1]
        db_ref[0] = fr * dr + fi * di
        db_ref[1] = fr * di - fi * dr
        per_state = lambda x: x.reshape(SSM_GROUPS, SSM_GROUP, SSM_STATE).sum(axis=1)
        dfr = per_state(dr * br + di * bi)
        dfi = per_state(di * br - dr * bi)
        _, vjp = jax.vjp(_disc, a_ref[0], a_ref[1], l_ref[...])
        dar, dai, dldt = vjp((dl_ref[0], dl_ref[1], dfr, dfi))
        da_ref[0] = dar
        da_ref[1] = dai
        dldt_ref[...] = jnp.sum(dldt, axis=1, keepdims=True)

    return pl.pallas_call(
        body, name="s5_disc_bwd",
        out_shape=[jax.ShapeDtypeStruct((2, SSM_GROUPS, SSM_STATE), F32),
                   jax.ShapeDtypeStruct((SSM_GROUPS, 1), F32),
                   jax.ShapeDtypeStruct((2, SSM_W, SSM_STATE), F32),
                   jax.ShapeDtypeStruct((2, SSM_W, SSM_STATE), F32)],
        compiler_params=_cparams(None),
    )(a, ldt, a_rep, ldt_rep, bt, mask, dl, d_bb, d_cc, fold)


def _row_block(rows, cap=512):
    for bm in range(min(cap, rows), 0, -1):
        if rows % bm == 0 and (bm % 8 == 0 or bm == rows):
            return bm
    return rows


SUM_PARTS = 2


def _own_pieces(segs, rtot):
    pr = rtot // SUM_PARTS
    assert pr * SUM_PARTS == rtot and pr % 16 == 0
    offs = _seg_offsets(segs)
    pieces = [[] for _ in range(SUM_PARTS)]
    for a, (n, r) in enumerate(segs):
        for m in range(n):
            lo = offs[a] + m * r
            for h in range(SUM_PARTS):
                clo, chi = max(lo, h * pr), min(lo + r, (h + 1) * pr)
                if chi > clo:
                    pieces[h].append((a, m, clo - lo, clo - h * pr, chi - clo))
    return pieces


def _pair_rows(srcs, got_ref, segs, pieces, h, chip, own_v, got_v, sems):
    pr = own_v.shape[0]
    dev = 2 * chip + lax.axis_index("c")
    for hh in range(SUM_PARTS):
        @pl.when(h == hh)
        def _(hh=hh):
            cps = [pltpu.make_async_copy(got_ref.at[chip, pl.ds(hh * pr, pr), :], got_v, sems.at[0])]
            for i, (a, m, so, do, rows) in enumerate(pieces[hh]):
                start = pl.multiple_of(dev * segs[a][1] + so, 16)
                cps.append(pltpu.make_async_copy(srcs[a].at[m, pl.ds(start, rows), :],
                                                 own_v.at[pl.ds(do, rows), :], sems.at[1 + i]))
            for cp in cps:
                cp.start()
            for cp in cps:
                cp.wait()
    return own_v[...].astype(F32) + got_v[...].astype(F32)


def _pair_sum(fulls, got, segs):
    ns = len(segs)
    _, rtot, c = got.shape
    pieces = _own_pieces(segs, rtot)
    pr = rtot // SUM_PARTS

    def body(*refs):
        srcs = refs[:ns]
        got_ref, pbf_ref, own_v, got_v, sems = refs[ns:]
        x, y, _ = _mesh_pos()
        j = pl.program_id(1)
        chip = jnp.where(j == 0, 2 * (1 - x) + y, jnp.where(j == 1, 2 * x + 1 - y, 2 * (1 - x) + 1 - y))
        pbf_ref[0] = _pair_rows(srcs, got_ref, segs, pieces, pl.program_id(0), chip, own_v, got_v, sems).astype(BF16)

    return pl.pallas_call(
        body, name="pair_sum",
        grid=(SUM_PARTS, 3),
        in_specs=[HBM] * (ns + 1), out_specs=pl.BlockSpec((1, pr, c), lambda h, j: (j, h, 0)),
        out_shape=pltpu.HBM((3, rtot, c), BF16),
        scratch_shapes=[pltpu.VMEM((pr, c), BF16), pltpu.VMEM((pr, c), BF16),
                        pltpu.SemaphoreType.DMA((1 + max(len(p) for p in pieces),))],
        compiler_params=_cparams(("arbitrary", "arbitrary")),
    )(*_in_hbm(*fulls, got))


def _chip_sum(fulls, got, rb, segs):
    ns = len(segs)
    _, rtot, c = got.shape
    pieces = _own_pieces(segs, rtot)
    pr = rtot // SUM_PARTS

    def body(*refs):
        srcs = refs[:ns]
        got_ref, r_ref, s_ref, own_v, got_v, sems = refs[ns:]
        x, y, _ = _mesh_pos()
        own = _pair_rows(srcs, got_ref, segs, pieces, pl.program_id(0), 2 * x + y, own_v, got_v, sems)
        s_ref[...] = ((own + r_ref[0].astype(F32)) + r_ref[1].astype(F32)) + r_ref[2].astype(F32)

    return pl.pallas_call(
        body, name="chip_sum",
        grid=(SUM_PARTS,),
        in_specs=[HBM] * (ns + 1) + [pl.BlockSpec((3, pr, c), lambda h: (0, h, 0))],
        out_specs=pl.BlockSpec((pr, c), lambda h: (h, 0)),
        out_shape=jax.ShapeDtypeStruct((rtot, c), F32),
        scratch_shapes=[pltpu.VMEM((pr, c), BF16), pltpu.VMEM((pr, c), BF16),
                        pltpu.SemaphoreType.DMA((1 + max(len(p) for p in pieces),))],
        compiler_params=_cparams(("arbitrary",)),
    )(*_in_hbm(*fulls, got, rb))


def _adamw(w, g, m, v):
    r, c = w.shape
    bm = _row_block(r)
    bc1 = 1.0 - ADAM_B1 ** ADAM_STEP
    bc2 = 1.0 - ADAM_B2 ** ADAM_STEP

    def body(w_ref, g_ref, m_ref, v_ref, d_ref, nm_ref, nv_ref):
        gv = g_ref[...]
        nm = ADAM_B1 * m_ref[...] + (1.0 - ADAM_B1) * gv
        nv = ADAM_B2 * v_ref[...] + (1.0 - ADAM_B2) * (gv * gv)
        nm_ref[...] = nm
        nv_ref[...] = nv
        d_ref[...] = -ADAM_LR * ((nm / bc1) / (jnp.sqrt(nv / bc2) + ADAM_EPS) + ADAM_WD * w_ref[...])

    spec = pl.BlockSpec((bm, c), lambda k: (k, 0))
    shp = jax.ShapeDtypeStruct((r, c), F32)
    return pl.pallas_call(
        body, name="adamw",
        grid=(r // bm,),
        in_specs=[spec] * 4, out_specs=[spec] * 3, out_shape=[shp] * 3,
        compiler_params=_cparams(("parallel",)),
    )(*_in_hbm(w, g, m, v))


def _adamw_layers(sets, first, prev):
    ns = len(sets)
    depth, r, c = sets[0][0].shape
    nl = sets[0][3].shape[0]
    bm = _row_block(r, min(512, max(SUBLANES, (24 * MIB) // (ns * 8 * 2 * c * 4))))
    bc1 = 1.0 - ADAM_B1 ** ADAM_STEP
    bc2 = 1.0 - ADAM_B2 ** ADAM_STEP

    def body(*refs):
        outs = refs[len(refs) - 4 * ns:]
        for s in range(ns):
            w_ref, m_ref, v_ref, g_ref = refs[4 * s:4 * s + 4]
            go_ref, d_ref, nm_ref, nv_ref = outs[4 * s:4 * s + 4]
            gv = g_ref[...]
            nm = ADAM_B1 * m_ref[...] + (1.0 - ADAM_B1) * gv
            nv = ADAM_B2 * v_ref[...] + (1.0 - ADAM_B2) * (gv * gv)
            go_ref[...] = gv
            nm_ref[...] = nm
            nv_ref[...] = nv
            d_ref[...] = -ADAM_LR * ((nm / bc1) / (jnp.sqrt(nv / bc2) + ADAM_EPS) + ADAM_WD * w_ref[...])

    at = pl.BlockSpec((1, bm, c), lambda i, k: (first + i, k, 0))
    shp = jax.ShapeDtypeStruct((depth, r, c), F32)
    old = [] if prev is None else [a for four in prev for a in four]
    flat = pl.pallas_call(
        body, name="adamw_layers",
        grid=(nl, r // bm),
        in_specs=[at, at, at, pl.BlockSpec((1, bm, c), lambda i, k: (i, k, 0))] * ns + [HBM] * len(old),
        out_specs=[at] * (4 * ns), out_shape=[shp] * (4 * ns),
        input_output_aliases={4 * ns + i: i for i in range(len(old))},
        compiler_params=_cparams(("parallel", "parallel")),
    )(*_in_hbm(*[a for four in sets for a in four]), *old)
    return [flat[4 * s:4 * s + 4] for s in range(ns)]


def _mesh_pos():
    return lax.axis_index("x"), lax.axis_index("y"), lax.axis_index("c")


def _dev_index(p):
    return 4 * p[0] + 2 * p[1] + p[2]


def _seg_offsets(segs):
    offs, o = [], 0
    for n, r in segs:
        offs.append(o)
        o += n * r
    return offs


def _remote(src, dst, send_sem, recv_sem, to):
    return pltpu.make_async_remote_copy(src_ref=src, dst_ref=dst, send_sem=send_sem, recv_sem=recv_sem,
                                        device_id=to, device_id_type=MESH)


def _allgather(pack, segs, name):
    rtot, c = pack.shape
    ns = len(segs)
    offs = _seg_offsets(segs)
    assert rtot == sum(n * r for n, r in segs)

    def body(pack_ref, *refs):
        outs = refs[:ns]
        send_sems, recv_sems, local_sem = refs[ns:]
        x, y, cc = _mesh_pos()
        me, sib = (x, y, cc), (x, y, 1 - cc)
        chips = [(1 - x, y), (x, 1 - y), (1 - x, 1 - y)]

        def pieces(dev, from_pack):
            res = []
            for a, (n, r) in enumerate(segs):
                for m in range(n):
                    dst = outs[a].at[m, pl.ds(pl.multiple_of(dev * r, r), r), :]
                    src = pack_ref.at[pl.ds(offs[a] + m * r, r), :] if from_pack else dst
                    res.append((src, dst))
            return res

        def push(k, dev, to, from_pack):
            for s, d in pieces(dev, from_pack):
                _remote(s, d, send_sems.at[k], recv_sems.at[k], to).start()

        def whole(k):
            return _remote(pack_ref, pack_ref, send_sems.at[k], recv_sems.at[k], me)

        my_dev = _dev_index(me)
        for s, d in pieces(my_dev, True):
            pltpu.make_async_copy(s, d, local_sem).start()
        push(0, my_dev, sib, True)
        for j, chip in enumerate(chips):
            push(1 + j, my_dev, (*chip, cc), True)
        for j, chip in enumerate(chips):
            whole(1 + j).wait_recv()
            push(4 + j, _dev_index((*chip, cc)), sib, False)
        whole(0).wait_recv()
        for j in range(3):
            whole(4 + j).wait_recv()
        for k in range(7):
            whole(k).wait_send()
        pltpu.make_async_copy(pack_ref, pack_ref, local_sem).wait()

    return pl.pallas_call(
        body, name=name,
        in_specs=[HBM], out_specs=[HBM] * ns,
        out_shape=[jax.ShapeDtypeStruct((n, N_DEV * r, c), pack.dtype) for n, r in segs],
        scratch_shapes=[pltpu.SemaphoreType.DMA((7,)), pltpu.SemaphoreType.DMA((7,)), pltpu.SemaphoreType.DMA],
    )(pack)


HBM = pl.BlockSpec(memory_space=pltpu.HBM)
SEM = pl.BlockSpec(memory_space=pltpu.SEMAPHORE)
VMEM_WHOLE = pl.BlockSpec(memory_space=pltpu.VMEM)
EFFECT = pltpu.SideEffectType.DATAFLOW_SIDE_EFFECTING


def _hbm(a):
    return pltpu.with_memory_space_constraint(a, pltpu.HBM)


def _ag_start(pack, segs, after, name):
    rtot, c = pack.shape
    ns = len(segs)
    offs = _seg_offsets(segs)

    def body(pack_ref, *refs):
        lands = refs[:ns]
        send_sems, recv_sems = refs[ns + 1], refs[ns + 2]
        token = refs[-1]
        x, y, cc = _mesh_pos()
        my_dev = _dev_index((x, y, cc))
        targets = [(x, y, 1 - cc), (1 - x, y, cc), (x, 1 - y, cc), (1 - x, 1 - y, cc)]
        for k, to in enumerate(targets):
            for a, (n, r) in enumerate(segs):
                for m in range(n):
                    _remote(pack_ref.at[pl.ds(offs[a] + m * r, r), :],
                            lands[a].at[m, pl.ds(pl.multiple_of(my_dev * r, r), r), :],
                            send_sems.at[k], recv_sems.at[k], to).start()
        token[...] = jnp.zeros_like(token)

    land_shapes = [(n, N_DEV * r, c) for n, r in segs]
    outs = pl.pallas_call(
        body, name=name,
        in_specs=[HBM] * (1 + ns) + [UNREAD],
        out_specs=[SEM, SEM, HBM] + [HBM] * ns + [VMEM_WHOLE],
        out_shape=[pltpu.SemaphoreType.DMA((4,)), pltpu.SemaphoreType.DMA((4,)), pltpu.HBM(pack.shape, pack.dtype)]
        + [pltpu.HBM(s, pack.dtype) for s in land_shapes] + [jax.ShapeDtypeStruct((SUBLANES, LANES), F32)],
        input_output_aliases={0: 2, **{1 + i: 3 + i for i in range(ns)}},
        compiler_params=pltpu.CompilerParams(has_side_effects=EFFECT),
    )(_hbm(pack), *[_hbm(lax.empty(s, pack.dtype)) for s in land_shapes], _hbm(after))
    return outs[0], outs[1], outs[2], list(outs[3:3 + ns]), outs[-1]


def _ag_wait(send_sems, recv_sems, pack, lands, after, name):
    ns = len(lands)

    def body(pack_ref, *refs):
        send_ref, recv_ref = refs[ns], refs[ns + 1]
        me = _mesh_pos()
        for k in range(4):
            whole = _remote(pack_ref, pack_ref, send_ref.at[k], recv_ref.at[k], me)
            whole.wait_send()
            whole.wait_recv()

    outs = pl.pallas_call(
        body, name=name,
        in_specs=[HBM] * (1 + ns) + [SEM, SEM, UNREAD],
        out_specs=[HBM] * (1 + ns),
        out_shape=[pltpu.HBM(pack.shape, pack.dtype)] + [pltpu.HBM(a.shape, a.dtype) for a in lands],
        input_output_aliases={i: i for i in range(1 + ns)},
        compiler_params=pltpu.CompilerParams(has_side_effects=EFFECT),
    )(pack, *lands, send_sems, recv_sems, _hbm(after))
    return outs[0], list(outs[1:])


def _ag_finish(pack, lands, segs):
    rtot, c = pack.shape
    ns = len(segs)
    offs = _seg_offsets(segs)

    def body(pack_ref, *refs):
        outs = refs[ns:2 * ns]
        stage, send_sems, recv_sems, local_sems = refs[2 * ns:]
        x, y, cc = _mesh_pos()
        me, sib = (x, y, cc), (x, y, 1 - cc)
        chips = [(1 - x, y), (x, 1 - y), (1 - x, 1 - y)]

        def rows(a, m, dev):
            return outs[a].at[m, pl.ds(pl.multiple_of(dev * segs[a][1], segs[a][1]), segs[a][1]), :]

        for j, chip in enumerate(chips):
            dev = _dev_index((*chip, cc))
            for a, (n, r) in enumerate(segs):
                for m in range(n):
                    _remote(rows(a, m, dev), rows(a, m, dev), send_sems.at[j], recv_sems.at[j], sib).start()
        load = pltpu.make_async_copy(pack_ref, stage, local_sems.at[0])
        load.start()
        load.wait()
        my_dev = _dev_index(me)
        for a, (n, r) in enumerate(segs):
            for m in range(n):
                pltpu.make_async_copy(stage.at[pl.ds(offs[a] + m * r, r), :], rows(a, m, my_dev), local_sems.at[1]).start()
        pltpu.make_async_copy(stage, pack_ref, local_sems.at[1]).wait()
        for j in range(3):
            _remote(pack_ref, pack_ref, send_sems.at[j], recv_sems.at[j], me).wait()

    outs = pl.pallas_call(
        body, name="ag_finish",
        in_specs=[HBM] * (1 + ns), out_specs=[HBM] * ns,
        out_shape=[pltpu.HBM(a.shape, a.dtype) if r >= 128 else jax.ShapeDtypeStruct(a.shape, a.dtype)
                   for a, (_, r) in zip(lands, segs)],
        input_output_aliases={1 + i: i for i in range(ns)},
        scratch_shapes=[pltpu.VMEM((rtot, c), pack.dtype), pltpu.SemaphoreType.DMA((3,)),
                        pltpu.SemaphoreType.DMA((3,)), pltpu.SemaphoreType.DMA((2,))],
        compiler_params=_cparams(None, 16),
    )(pack, *lands)
    return list(outs)


def _rs_chips_start(pbf, after, name):
    _, rtot, c = pbf.shape

    def body(pbf_ref, land_ref, after_ref, send_sems, recv_sems, pbf_thru, land_thru, token):
        x, y, cc = _mesh_pos()
        for j, (cx, cy) in enumerate([(1 - x, y), (x, 1 - y), (1 - x, 1 - y)]):
            _remote(pbf_ref.at[j], land_ref.at[j], send_sems.at[j], recv_sems.at[j], (cx, cy, cc)).start()
        token[...] = jnp.zeros_like(token)

    return pl.pallas_call(
        body, name=name,
        in_specs=[HBM, HBM, UNREAD],
        out_specs=[SEM, SEM, HBM, HBM, VMEM_WHOLE],
        out_shape=[pltpu.SemaphoreType.DMA((3,)), pltpu.SemaphoreType.DMA((3,)), pltpu.HBM(pbf.shape, pbf.dtype),
                   pltpu.HBM((3, rtot, c), pbf.dtype), jax.ShapeDtypeStruct((SUBLANES, LANES), F32)],
        input_output_aliases={0: 2, 1: 3},
        compiler_params=pltpu.CompilerParams(has_side_effects=EFFECT),
    )(_hbm(pbf), _hbm(lax.empty((3, rtot, c), pbf.dtype)), _hbm(after))


def _rs_chips_wait(send_sems, recv_sems, pbf, land, after, name):
    def body(pbf_ref, land_ref, send_ref, recv_ref, after_ref, pbf_out, land_out):
        me = _mesh_pos()
        for j in range(3):
            cp = _remote(pbf_ref.at[0], land_ref.at[j], send_ref.at[j], recv_ref.at[j], me)
            cp.wait_send()
            cp.wait_recv()

    return pl.pallas_call(
        body, name=name,
        in_specs=[HBM, HBM, SEM, SEM, UNREAD], out_specs=[HBM, HBM],
        out_shape=[pltpu.HBM(pbf.shape, pbf.dtype), pltpu.HBM(land.shape, land.dtype)],
        input_output_aliases={0: 0, 1: 1},
        compiler_params=pltpu.CompilerParams(has_side_effects=EFFECT),
    )(pbf, land, send_sems, recv_sems, _hbm(after))[1]


def _flips():
    return [(dx, dy, dc) for dx in (0, 1) for dy in (0, 1) for dc in (0, 1) if dx or dy or dc]


def _small_gather_start(flat, name):
    r, c = flat.shape

    def body(flat_ref, land_ref, send_sems, recv_sems, flat_thru, land_thru, token):
        x, y, cc = _mesh_pos()
        mine = land_ref.at[_dev_index((x, y, cc))]
        for k, (dx, dy, dc) in enumerate(_flips()):
            to = (1 - x if dx else x, 1 - y if dy else y, 1 - cc if dc else cc)
            _remote(flat_ref, mine, send_sems.at[k], recv_sems.at[k], to).start()
        token[...] = jnp.zeros_like(token)

    return pl.pallas_call(
        body, name=name,
        in_specs=[HBM, HBM],
        out_specs=[SEM, SEM, HBM, HBM, VMEM_WHOLE],
        out_shape=[pltpu.SemaphoreType.DMA((7,)), pltpu.SemaphoreType.DMA((7,)), pltpu.HBM(flat.shape, flat.dtype),
                   pltpu.HBM((N_DEV, r, c), flat.dtype), jax.ShapeDtypeStruct((SUBLANES, LANES), F32)],
        input_output_aliases={0: 2, 1: 3},
        compiler_params=pltpu.CompilerParams(has_side_effects=EFFECT),
    )(_hbm(flat), _hbm(lax.empty((N_DEV, r, c), flat.dtype)))


def _small_gather_wait(send_sems, recv_sems, flat, land, after, name):
    def body(flat_ref, land_ref, send_ref, recv_ref, after_ref, flat_out, land_out):
        me = _mesh_pos()
        for k in range(N_DEV - 1):
            cp = _remote(flat_ref, land_ref.at[0], send_ref.at[k], recv_ref.at[k], me)
            cp.wait_send()
            cp.wait_recv()

    return pl.pallas_call(
        body, name=name,
        in_specs=[HBM, HBM, SEM, SEM, UNREAD], out_specs=[HBM, HBM],
        out_shape=[pltpu.HBM(flat.shape, flat.dtype), pltpu.HBM(land.shape, land.dtype)],
        input_output_aliases={0: 0, 1: 1},
        compiler_params=pltpu.CompilerParams(has_side_effects=EFFECT),
    )(flat, land, send_sems, recv_sems, _hbm(after))


def _sum_devices(land, own):
    _, r, c = land.shape

    def body(land_ref, own_ref, out_ref):
        me = _dev_index(_mesh_pos())
        total = None
        for d in range(N_DEV):
            other = land_ref[jnp.where(d == me, (d + 1) % N_DEV, d)]
            block = jnp.where(d == me, own_ref[...], other)
            total = block if total is None else total + block
        out_ref[...] = total

    return pl.pallas_call(
        body, name="sum_devices",
        grid=(1,),
        in_specs=[pl.BlockSpec((N_DEV, r, c), lambda i: (0, 0, 0)), pl.BlockSpec((r, c), lambda i: (0, 0))],
        out_specs=pl.BlockSpec((r, c), lambda i: (0, 0)),
        out_shape=jax.ShapeDtypeStruct((r, c), F32),
        compiler_params=_cparams(("arbitrary",)),
    )(land, own)


def _rs_sibling_start(fulls, segs, name):
    ns = len(segs)
    offs = _seg_offsets(segs)
    rtot = sum(n * r for n, r in segs)
    c = fulls[0].shape[-1]
    dt = fulls[0].dtype

    def body(*refs):
        srcs = refs[:ns]
        land_ref, send_sem, recv_sem = refs[ns], refs[ns + 1], refs[ns + 2]
        token = refs[-1]
        x, y, cc = _mesh_pos()
        for k in range(4):
            for a, (n, r) in enumerate(segs):
                for m in range(n):
                    theirs = srcs[a].at[m, pl.ds(pl.multiple_of((2 * k + 1 - cc) * r, r), r), :]
                    _remote(theirs, land_ref.at[k, pl.ds(offs[a] + m * r, r), :], send_sem, recv_sem,
                            (x, y, 1 - cc)).start()
        token[...] = jnp.zeros_like(token)

    outs = pl.pallas_call(
        body, name=name,
        in_specs=[HBM] * (ns + 1),
        out_specs=[SEM, SEM] + [HBM] * (ns + 1) + [VMEM_WHOLE],
        out_shape=[pltpu.SemaphoreType.DMA(()), pltpu.SemaphoreType.DMA(())]
        + [pltpu.HBM(a.shape, a.dtype) for a in fulls] + [pltpu.HBM((4, rtot, c), dt),
                                                           jax.ShapeDtypeStruct((SUBLANES, LANES), F32)],
        input_output_aliases={i: 2 + i for i in range(ns + 1)},
        compiler_params=pltpu.CompilerParams(has_side_effects=EFFECT),
    )(*[_hbm(a) for a in fulls], _hbm(lax.empty((4, rtot, c), dt)))
    return outs[0], outs[1], list(outs[2:2 + ns]), outs[2 + ns], outs[-1]


def _rs_sibling_wait(send_sem, recv_sem, fulls, land, after, name):
    ns = len(fulls)

    def body(*refs):
        land_ref, send_ref, recv_ref = refs[ns], refs[ns + 1], refs[ns + 2]
        whole = _remote(land_ref, land_ref, send_ref, recv_ref, _mesh_pos())
        whole.wait_send()
        whole.wait_recv()

    outs = pl.pallas_call(
        body, name=name,
        in_specs=[HBM] * (ns + 1) + [SEM, SEM, UNREAD], out_specs=[HBM] * (ns + 1),
        out_shape=[pltpu.HBM(a.shape, a.dtype) for a in fulls] + [pltpu.HBM(land.shape, land.dtype)],
        input_output_aliases={i: i for i in range(ns + 1)},
        compiler_params=pltpu.CompilerParams(has_side_effects=EFFECT),
    )(*fulls, land, send_sem, recv_sem, _hbm(after))
    return list(outs[:ns]), outs[ns]


def _tp(w):
    return jnp.swapaxes(w, -1, -2)


def _s5_prepare(a_re, a_im, log_dt, b_re, b_im, c_re, c_im):
    a = jnp.stack([a_re, a_im], axis=1)
    ldt = jnp.broadcast_to(log_dt[:, :, None], (DEPTH, SSM_GROUPS, SSM_STATE))
    a_row = a.reshape(DEPTH, 2, 1, N_STATE)
    ldt_row = ldt.reshape(DEPTH, 1, N_STATE)
    a_rep = jnp.repeat(a, SSM_GROUP, axis=2)
    ldt_rep = jnp.repeat(ldt, SSM_GROUP, axis=1)
    bt = jnp.stack([_tp(b_re), _tp(b_im)], axis=1).reshape(DEPTH, 2, SSM_W, SSM_STATE)
    ct = jnp.stack([c_re, c_im], axis=1).reshape(DEPTH, 2, SSM_W, SSM_STATE)
    tile_e = jnp.tile(jnp.eye(SSM_STATE, dtype=BF16), (1, SSM_GROUPS))
    mask = jnp.repeat(jnp.repeat(jnp.eye(SSM_GROUPS, dtype=BF16), SSM_GROUP, axis=0), SSM_STATE, axis=1)
    out = []
    for l in range(DEPTH):
        tabs = _s5_disc(a_row[l], ldt_row[l], a_rep[l], ldt_rep[l], bt[l], ct[l], tile_e, mask)
        out.append(((a[l], ldt[l], a_rep[l], ldt_rep[l], bt[l], mask), *tabs))
    return out


def _layer_fwd(h, p_l, small, big, arrive=None):
    saved = {'h0': h}
    if arrive is not None:
        arrive(0, h)
    h, saved['gu1'] = _ffn_fwd(h, small['ffn1_norm'], big['ff1'])
    saved['h1'] = h
    if arrive is not None:
        arrive(1, h)
    z = _inproj_fwd(h, small['mix_norm'], big['wint'])
    ya, ys, hs = _s5conv_fwd(z, small['conv_w'], small['conv_b'], small['bbmat'], small['ccmat'], small['dvec'],
                             small['ltab'])
    saved.update(z=z, ya=ya, ys=ys, hs=hs)
    h = _mix_out_fwd(h, ya, ys, big['glu'], small['glu_b'], small['conv_out_norm'], small['ssm_out_norm'], big['wout'])
    saved['h2'] = h
    if arrive is not None:
        arrive(2, h)
    h, saved['gu2'] = _ffn_fwd(h, small['ffn2_norm'], big['ff2'])
    saved['h3'] = h
    h = _ple_fwd(h, small['ple_norm'], p_l, big['plg'], big['plpt'])
    return h, saved


def _ffn_bwd(h_in, g, dh, gu, w3):
    dh_in, dga, ud, dg = _ffn_bwd_act(h_in, g, dh, gu, w3)
    return dh_in, _matmul_tn(dga, ud, FF_BLOCK, BF16, "ffn_wgrad"), dg


def _layer_bwd_top(dh, p_l, small, big, saved):
    gs = {}
    dh, u, dq, dpp, pb, gs['ple_norm'] = _ple_bwd(saved['h3'], small['ple_norm'], p_l, dh, big['plg'], big['plpt'])
    d_plg = _matmul_tn(u, dq, 256, BF16, "ple_gate_wgrad")
    d_plpt = _matmul_tn(dpp, pb, 256, BF16, "ple_proj_wgrad", to_kernel=False)
    dh, d_ff2, gs['ffn2_norm'] = _ffn_bwd(saved['h2'], small['ffn2_norm'], dh, saved['gu2'], big['ff2'])
    return dh, (gs, d_plg, d_plpt, d_ff2)


def _layer_bwd_rest(dh, top, small, big, saved):
    gs, d_plg, d_plpt, d_ff2 = top
    dya, dys, ycat, dhb, zg, dq, part = _mix_out_bwd(dh, saved['ya'], saved['ys'], big['glu'], small['glu_b'],
                                                     small['conv_out_norm'], small['ssm_out_norm'], big['wout'])
    d_wout = _matmul_tn(ycat, dhb, 256, BF16, "w_out_wgrad")
    d_glu = _matmul_tn(zg, dq, 256, BF16, "glu_wgrad", to_kernel=False)
    dz, gadj, us, dyb, dl, dcw = _s5conv_bwd(saved['z'], saved['hs'], dya, dys, small['conv_w'], small['conv_b'],
                                             small['bbmat'], small['ccmat'], small['dvec'], small['ltab_rev'])
    d_bb = _block_wgrad(us, gadj, "s5_b_wgrad")
    d_cc = _block_wgrad(dyb, saved['hs'][None], "s5_c_wgrad")
    dh, u, gs['mix_norm'] = _inproj_bwd(saved['h1'], small['mix_norm'], dh, dz, big['wint'])
    d_wint = _matmul_tn(dz[None], u, 256, BF16, "w_in_wgrad")
    dh, d_ff1, gs['ffn1_norm'] = _ffn_bwd(saved['h0'], small['ffn1_norm'], dh, saved['gu1'], big['ff1'])

    dlb = dl[0].reshape(2, SSM_GROUPS, SSM_STATE)
    fold = jnp.tile(jnp.eye(SSM_STATE, dtype=BF16), (SSM_GROUPS, 1))
    da, dldt, dbt, dct = _s5_disc_bwd(*small['disc_in'], dlb, d_bb, d_cc, fold)
    gs['ssm_A_re'], gs['ssm_A_im'] = da[0], da[1]
    gs['ssm_log_dt'] = dldt[:, 0]
    ghp = (SSM_GROUPS, SSM_GROUP, SSM_STATE)
    gs['ssm_B_re'], gs['ssm_B_im'] = dbt[0].reshape(ghp), dbt[1].reshape(ghp)
    gs['ssm_C_re'], gs['ssm_C_im'] = dct[0].reshape(ghp), dct[1].reshape(ghp)
    gs['conv_w'] = dcw[0:3]
    gs['conv_b'] = dcw[3]
    gs['ssm_D'] = dcw[4].reshape(SSM_GROUPS, SSM_GROUP)
    gs['conv_out_norm'], gs['ssm_out_norm'], gs['glu_b'] = part[0], part[1], part[2]
    for n in ('ple_norm', 'ffn2_norm', 'mix_norm', 'ffn1_norm'):
        gs[n] = gs[n][0]
    fulls = [d_ff1, d_ff2, d_wint, d_wout, d_plg,
             d_plpt.reshape(1, D_MODEL * PLE_DIM // D_MODEL, D_MODEL), d_glu.reshape(1, SSM_W * SSM_W // D_MODEL, D_MODEL)]
    return dh, fulls, gs


VIEW_T = ('ffn1_w_gate', 'ffn1_w_up', 'ffn2_w_gate', 'ffn2_w_up', 'ssm_B_re', 'ssm_B_im')


def _view(name, a):
    return _tp(a) if name in VIEW_T else a


SEG_NAMES = ('ff1', 'ff2', 'wint', 'wout', 'plg', 'plpt', 'glu')
FIRST_LAYER_GROUPS = ((0,), (2, 3, 6), (1, 4, 5))


def _layer_pack(W, l, segments=range(len(SEGS))):
    pieces = {
        0: lambda: [_tp(W['ffn1_w_gate'][l]), _tp(W['ffn1_w_up'][l]), W['ffn1_w_down'][l]],
        1: lambda: [_tp(W['ffn2_w_gate'][l]), _tp(W['ffn2_w_up'][l]), W['ffn2_w_down'][l]],
        2: lambda: [_tp(W['w_in'][l])],
        3: lambda: [W['w_out'][l]],
        4: lambda: [W['ple_w_gate'][l]],
        5: lambda: [_tp(W['ple_w_proj'][l]).reshape(-1, D_MODEL)],
        6: lambda: [W['glu_w'][l].reshape(-1, D_MODEL)],
    }
    return jnp.concatenate([a for s in segments for a in pieces[s]()], axis=0).astype(BF16)


def _as_big(named):
    shape = dict(plpt=(D_MODEL, PLE_DIM), glu=(SSM_W, SSM_W))
    return {n: (a.reshape(shape[n]) if n in shape else a) for n, a in named.items()}


def _pad_rows(flat, mult, width=LANES):
    per = mult * width
    n = flat.shape[0]
    tot = -(-n // per) * per
    return jnp.pad(flat, (0, tot - n)).reshape(tot // width, width)


def _adamw_any(w, g, m, v):
    shp = w.shape
    two = (lambda t: t.reshape(-1, shp[-1]))
    d, nm, nv = _adamw(two(w), two(g), two(m), two(v))
    return d.reshape(shp), nm.reshape(shp), nv.reshape(shp)


def kernel(x, p, ffn1_norm, ffn1_w_gate, ffn1_w_up, ffn1_w_down, mix_norm, w_in, conv_w, conv_b, ssm_A_re, ssm_A_im, ssm_B_re, ssm_B_im, ssm_C_re, ssm_C_im, ssm_D, ssm_log_dt, glu_w, glu_b, conv_out_norm, ssm_out_norm, w_out, ffn2_norm, ffn2_w_gate, ffn2_w_up, ffn2_w_down, ple_norm, ple_w_gate, ple_w_proj, final_norm, loss_target, m_ffn1_norm, m_ffn1_w_gate, m_ffn1_w_up, m_ffn1_w_down, m_mix_norm, m_w_in, m_conv_w, m_conv_b, m_ssm_A_re, m_ssm_A_im, m_ssm_B_re, m_ssm_B_im, m_ssm_C_re, m_ssm_C_im, m_ssm_D, m_ssm_log_dt, m_glu_w, m_glu_b, m_conv_out_norm, m_ssm_out_norm, m_w_out, m_ffn2_norm, m_ffn2_w_gate, m_ffn2_w_up, m_ffn2_w_down, m_ple_norm, m_ple_w_gate, m_ple_w_proj, m_final_norm, v_ffn1_norm, v_ffn1_w_gate, v_ffn1_w_up, v_ffn1_w_down, v_mix_norm, v_w_in, v_conv_w, v_conv_b, v_ssm_A_re, v_ssm_A_im, v_ssm_B_re, v_ssm_B_im, v_ssm_C_re, v_ssm_C_im, v_ssm_D, v_ssm_log_dt, v_glu_w, v_glu_b, v_conv_out_norm, v_ssm_out_norm, v_w_out, v_ffn2_norm, v_ffn2_w_gate, v_ffn2_w_up, v_ffn2_w_down, v_ple_norm, v_ple_w_gate, v_ple_w_proj, v_final_norm):
    given = dict(locals())
    W = {n: given[n] for n in W_NAMES}
    M = {n: given['m_' + n] for n in W_NAMES}
    V = {n: given['v_' + n] for n in W_NAMES}
    Wv, Mv, Vv = [{n: _view(n, d[n]) for n in W_NAMES} for d in (W, M, V)]
    my_dev = _dev_index(_mesh_pos())

    conv_shard = _pad_rows(W['conv_w'].reshape(-1), SUBLANES)
    conv_all = _allgather(conv_shard, ((1, SUBLANES),), "ag_conv_w")[0]
    conv_full = conv_all.reshape(N_DEV, -1)[:, :DEPTH * 3 * (CONV_W // N_DEV)]
    conv_full = conv_full.reshape(N_DEV, DEPTH, 3, CONV_W // N_DEV).transpose(1, 2, 0, 3).reshape(DEPTH, 3, CONV_W)
    first, after = [], conv_all
    for gi, segments in enumerate(FIRST_LAYER_GROUPS):
        first.append(_ag_start(_layer_pack(W, 0, segments), tuple(SEGS[s] for s in segments), after,
                               "ag_start_0%s" % "abc"[gi]))
        after = first[-1][4]
    s5 = _s5_prepare(*[W[n] + after[0, 0] for n in ('ssm_A_re', 'ssm_A_im', 'ssm_log_dt')],
                     *[W[n] for n in ('ssm_B_re', 'ssm_B_im', 'ssm_C_re', 'ssm_C_im')])
    packs = [None] + [_layer_pack(W, l) for l in range(1, DEPTH)]
    prepared = conv_full[0, 0:1, 0:1] + s5[DEPTH - 1][1][0:1, 0:1] + packs[DEPTH - 1][0:1, 0:1].astype(F32)

    smalls, saves, bigs = [], [], []
    h = x[0]

    flight = None

    def gathered(handles, segments, after, name, next_layer=None, gate=None):
        nonlocal flight
        send_sems, recv_sems, pack_thru, lands, _ = handles
        pack_thru, lands = _ag_wait(send_sems, recv_sems, pack_thru, lands, after, "ag_wait_" + name)
        if next_layer is not None:
            flight = _ag_start(packs[next_layer], SEGS, pack_thru, "ag_start_%d" % next_layer)
            gate[0][gate[1]] = gate[0][gate[1]] + flight[4][0:1, 0:1]
        outs = _ag_finish(pack_thru, lands, tuple(SEGS[s] for s in segments))
        return _as_big({SEG_NAMES[s]: a for s, a in zip(segments, outs)})

    for l in range(DEPTH):
        small = {n: W[n][l][None] for n in ('ffn1_norm', 'mix_norm', 'conv_b', 'glu_b', 'conv_out_norm',
                                            'ssm_out_norm', 'ffn2_norm', 'ple_norm')}
        small['conv_w'] = conv_full[l]
        small['dvec'] = W['ssm_D'][l].reshape(1, SSM_W)
        small['disc_in'], small['ltab'], small['ltab_rev'], small['bbmat'], small['ccmat'] = s5[l]
        big = {}
        bigs.append(big)
        if l == 0:
            def arrive(stage, h_now, big=big, small=small):
                big.update(gathered(first[stage], FIRST_LAYER_GROUPS[stage], prepared if stage == 0 else h_now,
                                    "0%s" % "abc"[stage], *((1, (small, 'ffn2_norm')) if stage == 2 else ())))
            h, saved = _layer_fwd(h, p[l, 0], small, big, arrive)
        else:
            nxt = (l + 1, (small, 'ffn1_norm')) if l + 1 < DEPTH else ()
            big.update(gathered(flight, range(len(SEGS)), h, "%d" % l, *nxt))
            h, saved = _layer_fwd(h, p[l, 0], small, big)
        smalls.append(small)
        saves.append(saved)
    loss_tile, dh, d_final = _final_loss(h, W['final_norm'][None], loss_target[0])
    loss = lax.psum(loss_tile[0, 0], ("x", "y", "c"))

    layer_gs = [None] * DEPTH
    shard_grads = [None] * DEPTH
    zero = jnp.zeros((1, 1), F32)
    sib, ici = None, None

    def finish_sibling(after_sib, after_ici):
        nonlocal sib, ici
        up, (send_sem, recv_sem, fulls_thru, land, _) = sib
        fulls_thru, got = _rs_sibling_wait(send_sem, recv_sem, fulls_thru, land, after_sib, "sib_wait_%d" % up)
        pbf = _pair_sum(fulls_thru, got, SEGS)
        done = finish_chips(after_ici)
        ici = (up, _rs_chips_start(pbf, after_ici if done is None else done, "rs_start_%d" % up), fulls_thru, got)
        sib = None

    def finish_chips(after):
        nonlocal ici
        if ici is None:
            return None
        up, (send_sems, recv_sems, pbf_thru, land, _), fulls_up, got_up = ici
        got3 = _rs_chips_wait(send_sems, recv_sems, pbf_thru, land, after, "rs_wait_%d" % up)
        shard_grads[up] = _chip_sum(fulls_up, got_up, got3, SEGS)
        ici = None
        return shard_grads[up]

    layer_names = [n for n in SMALL_NAMES if n != 'final_norm']
    small_flights = [None] * DEPTH
    for l in reversed(range(DEPTH)):
        small = dict(smalls[l])
        if sib is not None:
            small['ple_norm'] = small['ple_norm'] + sib[1][4][0:1, 0:1] + small_flights[l + 1][4][0:1, 0:1]
        dh, top = _layer_bwd_top(dh, p[l, 0], small, bigs[l], saves[l])
        if sib is not None:
            finish_sibling(dh, dh)
            small['glu_b'] = small['glu_b'] + ici[1][4][0:1, 0:1]
        dh, fulls, layer_gs[l] = _layer_bwd_rest(dh, top, small, bigs[l], saves[l])
        sib = (l, _rs_sibling_start(fulls, SEGS, "sib_start_%d" % l))
        last_slot = d_final[0] if l == DEPTH - 1 else jnp.zeros((D_MODEL,), F32)
        flat = jnp.concatenate([layer_gs[l][n].reshape(-1) for n in layer_names + ['conv_w']] + [last_slot])
        small_flights[l] = _small_gather_start(_pad_rows(flat, SUBLANES, D_MODEL), "small_start_%d" % l)
    grad_x = dh[None]
    finish_sibling(small_flights[0][4], small_flights[0][4])

    reduced = []
    for l in range(DEPTH):
        send_sems, recv_sems, flat_thru, land, _ = small_flights[l]
        flat_thru, land = _small_gather_wait(send_sems, recv_sems, flat_thru, land, ici[1][4], "small_wait_%d" % l)
        reduced.append(_sum_devices(land, flat_thru).reshape(-1))
    reduced = jnp.stack(reduced)
    G = {}
    o = 0
    for n in layer_names + ['conv_w']:
        size = (W[n].size if n != 'conv_w' else DEPTH * 3 * CONV_W) // DEPTH
        shape = Wv[n].shape if n != 'conv_w' else (DEPTH, 3, CONV_W)
        G[n] = reduced[:, o:o + size].reshape(shape)
        o += size
    G['final_norm'] = reduced[DEPTH - 1, o:o + D_MODEL]
    G['conv_w'] = lax.dynamic_slice_in_dim(G['conv_w'], my_dev * (CONV_W // N_DEV), CONV_W // N_DEV, axis=2)

    delta, new_m, new_v = {}, {}, {}
    for n in SMALL_NAMES + ['conv_w']:
        two = (lambda t: t.reshape(1, -1) if t.ndim == 1 else t)
        delta[n], new_m[n], new_v[n] = [t.reshape(Wv[n].shape) for t in
                                        _adamw_any(two(Wv[n]), two(G[n]), two(Mv[n]), two(Vv[n]))]

    def unpack(sg):
        nl = sg.shape[0]
        offs = _seg_offsets(SEGS)
        r = SEGS[0][1]
        out = {}
        for a, f in ((0, 'ffn1'), (1, 'ffn2')):
            out[f + '_w_gate'] = sg[:, offs[a]:offs[a] + r]
            out[f + '_w_up'] = sg[:, offs[a] + r:offs[a] + 2 * r]
            out[f + '_w_down'] = sg[:, offs[a] + 2 * r:offs[a] + 3 * r]
        out['w_in'] = _tp(sg[:, offs[2]:offs[2] + SEGS[2][1]])
        out['w_out'] = sg[:, offs[3]:offs[3] + SEGS[3][1]]
        out['ple_w_gate'] = sg[:, offs[4]:offs[4] + SEGS[4][1]]
        out['ple_w_proj'] = _tp(sg[:, offs[5]:offs[5] + SEGS[5][1]].reshape(nl, D_MODEL // N_DEV, PLE_DIM))
        out['glu_w'] = sg[:, offs[6]:offs[6] + SEGS[6][1]].reshape(nl, SSM_W // N_DEV, SSM_W)
        return out

    upper = unpack(jnp.stack(shard_grads[1:]))
    groups = {}
    for n in upper:
        groups.setdefault(Wv[n].shape, []).append(n)
    part = {shape: _adamw_layers([(Wv[n], Mv[n], Vv[n], upper[n]) for n in ns], 1, None)
            for shape, ns in groups.items()}
    finish_chips(sum(four[3][1, 0:1, 0:1] for fours in part.values() for four in fours))
    lower = unpack(shard_grads[0][None])
    for shape, ns in groups.items():
        done = _adamw_layers([(Wv[n], Mv[n], Vv[n], lower[n]) for n in ns], 0, part[shape])
        for n, four in zip(ns, done):
            G[n], delta[n], new_m[n], new_v[n] = four

    outs = [[_view(n, d[n]) for n in W_NAMES] for d in (G, delta, new_m, new_v)]
    return (loss, grad_x, *outs[0], *outs[1], *outs[2], *outs[3])
```

```python
import math

import jax
import jax.numpy as jnp
from jax import lax
from jax.experimental import pallas as pl
from jax.experimental.pallas import tpu as pltpu

F32 = jnp.float32
BF16 = jnp.bfloat16

N_DEV = 8
DEPTH = 4
SEQ = 2048
D_MODEL = 1024
D_FF = 2816
CONV_W = 512
SSM_W = 512
SSM_GROUPS = 32
SSM_GROUP = 16
SSM_STATE = 64
N_STATE = SSM_GROUPS * SSM_STATE
IN_COLS = 2048
PLE_DIM = 256
EPS = 1e-6

ADAM_LR = 0.001
ADAM_B1 = 0.9
ADAM_B2 = 0.999
ADAM_EPS = 1e-08
ADAM_WD = 0.01
ADAM_STEP = 10

FF_BLOCK = 256
N_FF_BLOCKS = D_FF // FF_BLOCK
TOK_TILE_FFN_FWD = 2048
TOK_TILE_FFN_BWD = 1024
TOK_TILE = 512
CHUNK = 256
N_CHUNKS = SEQ // CHUNK
LANE_GROUP = 512
SUBLANES = 8
LANES = 128
MIB = 1024 * 1024

W_NAMES = ['ffn1_norm', 'ffn1_w_gate', 'ffn1_w_up', 'ffn1_w_down', 'mix_norm', 'w_in', 'conv_w', 'conv_b',
           'ssm_A_re', 'ssm_A_im', 'ssm_B_re', 'ssm_B_im', 'ssm_C_re', 'ssm_C_im', 'ssm_D', 'ssm_log_dt',
           'glu_w', 'glu_b', 'conv_out_norm', 'ssm_out_norm', 'w_out', 'ffn2_norm', 'ffn2_w_gate', 'ffn2_w_up',
           'ffn2_w_down', 'ple_norm', 'ple_w_gate', 'ple_w_proj', 'final_norm']
SMALL_NAMES = ['ffn1_norm', 'mix_norm', 'conv_b', 'ssm_A_re', 'ssm_A_im', 'ssm_B_re', 'ssm_B_im', 'ssm_C_re',
               'ssm_C_im', 'ssm_D', 'ssm_log_dt', 'glu_b', 'conv_out_norm', 'ssm_out_norm', 'ffn2_norm',
               'ple_norm', 'final_norm']

SEGS = ((3, 352), (3, 352), (1, 256), (1, 128), (1, 128), (1, 32), (1, 32))
PACK_ROWS = sum(n * r for n, r in SEGS)

MESH = pl.DeviceIdType.MESH
UNREAD = pl.BlockSpec(memory_space=pltpu.HBM)


def _in_hbm(*arrays):
    return [pltpu.with_memory_space_constraint(a, pltpu.HBM) for a in arrays]


def _out_hbm(outs, which):
    if not isinstance(outs, (list, tuple)):
        return pltpu.with_memory_space_constraint(outs, pltpu.HBM) if which else outs
    return [pltpu.with_memory_space_constraint(a, pltpu.HBM) if i in which else a for i, a in enumerate(outs)]


def _cparams(sem=None, vmem_mib=48, **kw):
    return pltpu.CompilerParams(dimension_semantics=sem, vmem_limit_bytes=vmem_mib * MIB, **kw)


def _dot(a, b):
    return jnp.dot(a, b, preferred_element_type=F32)


def _dot_nt(a, b):
    return lax.dot_general(a, b, (((1,), (1,)), ((), ())), preferred_element_type=F32)


def _dot_tn(a, b):
    return lax.dot_general(a, b, (((0,), (0,)), ((), ())), preferred_element_type=F32)


def _rms_stats(x):
    r = lax.rsqrt(jnp.mean(x * x, axis=-1, keepdims=True) + EPS)
    return x * r, r


def _rms_bwd(dy, xh, r, g):
    dxh = dy * g
    dx = r * (dxh - xh * jnp.mean(dxh * xh, axis=-1, keepdims=True))
    dg = jnp.sum(dy * xh, axis=0, keepdims=True)
    return dx, dg


def _sigmoid(x):
    return 0.5 * jnp.tanh(0.5 * x) + 0.5


_GELU_C = math.sqrt(2.0 / math.pi)


def _gelu(x):
    t = jnp.tanh(_GELU_C * (x + 0.044715 * x * x * x))
    return 0.5 * x * (1.0 + t), t


def _gelu_grad(x, t):
    return 0.5 * (1.0 + t) + 0.5 * x * (1.0 - t * t) * _GELU_C * (1.0 + 3.0 * 0.044715 * x * x)


def _accumulate(ref, first, value):
    @pl.when(first)
    def _():
        ref[...] = value

    @pl.when(jnp.logical_not(first))
    def _():
        ref[...] += value


def _ffn_fwd(h, g, w3):
    tm = TOK_TILE_FFN_FWD
    last = N_FF_BLOCKS - 1

    def body(h_ref, g_ref, wgu_ref, wd_ref, wd_last_ref, out_ref, gu_ref, u_ref, a_ref):
        k = pl.program_id(1)

        @pl.when(k == 0)
        def _():
            x = h_ref[...]
            xh, _ = _rms_stats(x)
            u_ref[...] = (xh * g_ref[...]).astype(BF16)
            out_ref[...] = x
            a_ref[1] = jnp.zeros((tm, FF_BLOCK), BF16)

        out_ref[...] += 0.5 * _dot(a_ref[(k + 1) % 2], wd_ref[0])
        gu = _dot_nt(u_ref[...], wgu_ref[...].reshape(2 * FF_BLOCK, D_MODEL))
        gate, up = gu[:, :FF_BLOCK], gu[:, FF_BLOCK:]
        a_ref[k % 2] = (gate * _sigmoid(gate) * up).astype(BF16)
        gu_ref[0] = gate.astype(BF16)
        gu_ref[1] = up.astype(BF16)

        @pl.when(k == last)
        def _():
            out_ref[...] += 0.5 * _dot(a_ref[last % 2], wd_last_ref[0])

    return _out_hbm(pl.pallas_call(
        body, name="ffn_fwd",
        grid=(SEQ // tm, N_FF_BLOCKS),
        in_specs=[pl.BlockSpec((tm, D_MODEL), lambda m, k: (m, 0), pipeline_mode=pl.Buffered(1)),
                  pl.BlockSpec((1, D_MODEL), lambda m, k: (0, 0)),
                  pl.BlockSpec((2, FF_BLOCK, D_MODEL), lambda m, k: (0, k, 0)),
                  pl.BlockSpec((1, FF_BLOCK, D_MODEL), lambda m, k: (2, jnp.maximum(k - 1, 0), 0)),
                  pl.BlockSpec((1, FF_BLOCK, D_MODEL), lambda m, k: (2, last, 0), pipeline_mode=pl.Buffered(1))],
        out_specs=[pl.BlockSpec((tm, D_MODEL), lambda m, k: (m, 0)),
                   pl.BlockSpec((2, tm, FF_BLOCK), lambda m, k: (0, m, k))],
        out_shape=[jax.ShapeDtypeStruct((SEQ, D_MODEL), F32),
                   pltpu.HBM((2, SEQ, D_FF), BF16)],
        scratch_shapes=[pltpu.VMEM((tm, D_MODEL), BF16), pltpu.VMEM((2, tm, FF_BLOCK), BF16)],
        compiler_params=_cparams(("parallel", "arbitrary"), 56),
    )(*_in_hbm(h, g, w3, w3, w3)), (1,))


def _ffn_bwd_act(h, g, dout, gu, w3):
    tm = TOK_TILE_FFN_BWD
    last = N_FF_BLOCKS - 1

    def body(h_ref, g_ref, d_ref, gu_ref, wd_ref, wgu_ref, wgu_last_ref, dh_ref, dga_ref, ud_ref, dg_ref,
             acc_ref, dgu_ref):
        m = pl.program_id(0)
        k = pl.program_id(1)

        @pl.when(k == 0)
        def _():
            xh, _ = _rms_stats(h_ref[...])
            ud_ref[0] = (xh * g_ref[...]).astype(BF16)
            ud_ref[1] = (0.5 * d_ref[...]).astype(BF16)
            acc_ref[...] = jnp.zeros_like(acc_ref)
            dgu_ref[1] = jnp.zeros((tm, 2 * FF_BLOCK), BF16)

        acc_ref[...] += _dot(dgu_ref[(k + 1) % 2], wgu_ref[...].reshape(2 * FF_BLOCK, D_MODEL))
        gate = gu_ref[0].astype(F32)
        up = gu_ref[1].astype(F32)
        sg = _sigmoid(gate)
        silu = gate * sg
        da = _dot_nt(ud_ref[1], wd_ref[0])
        dgate = (da * up * (sg + silu * (1.0 - sg))).astype(BF16)
        dup = (da * silu).astype(BF16)
        dga_ref[0] = dgate
        dga_ref[1] = dup
        dga_ref[2] = (silu * up).astype(BF16)
        dgu_ref[k % 2, :, 0:FF_BLOCK] = dgate
        dgu_ref[k % 2, :, FF_BLOCK:2 * FF_BLOCK] = dup

        @pl.when(k == last)
        def _():
            du = acc_ref[...] + _dot(dgu_ref[last % 2], wgu_last_ref[...].reshape(2 * FF_BLOCK, D_MODEL))
            xh, r = _rms_stats(h_ref[...])
            dx, dg = _rms_bwd(du, xh, r, g_ref[...])
            dh_ref[...] = d_ref[...] + dx
            _accumulate(dg_ref, m == 0, dg)

    return _out_hbm(pl.pallas_call(
        body, name="ffn_bwd_act",
        grid=(SEQ // tm, N_FF_BLOCKS),
        in_specs=[pl.BlockSpec((tm, D_MODEL), lambda m, k: (m, 0), pipeline_mode=pl.Buffered(1)),
                  pl.BlockSpec((1, D_MODEL), lambda m, k: (0, 0)),
                  pl.BlockSpec((tm, D_MODEL), lambda m, k: (m, 0), pipeline_mode=pl.Buffered(1)),
                  pl.BlockSpec((2, tm, FF_BLOCK), lambda m, k: (0, m, k)),
                  pl.BlockSpec((1, FF_BLOCK, D_MODEL), lambda m, k: (2, k, 0)),
                  pl.BlockSpec((2, FF_BLOCK, D_MODEL), lambda m, k: (0, jnp.maximum(k - 1, 0), 0)),
                  pl.BlockSpec((2, FF_BLOCK, D_MODEL), lambda m, k: (0, last, 0), pipeline_mode=pl.Buffered(1))],
        out_specs=[pl.BlockSpec((tm, D_MODEL), lambda m, k: (m, 0)),
                   pl.BlockSpec((3, tm, FF_BLOCK), lambda m, k: (0, m, k)),
                   pl.BlockSpec((2, tm, D_MODEL), lambda m, k: (0, m, 0)),
                   pl.BlockSpec((1, D_MODEL), lambda m, k: (0, 0))],
        out_shape=[jax.ShapeDtypeStruct((SEQ, D_MODEL), F32),
                   pltpu.HBM((3, SEQ, D_FF), BF16),
                   pltpu.HBM((2, SEQ, D_MODEL), BF16),
                   jax.ShapeDtypeStruct((1, D_MODEL), F32)],
        scratch_shapes=[pltpu.VMEM((tm, D_MODEL), F32), pltpu.VMEM((2, tm, 2 * FF_BLOCK), BF16)],
        compiler_params=_cparams(("arbitrary", "arbitrary"), 56),
    )(*_in_hbm(h, g, dout, gu, w3, w3, w3)), (1, 2))


def _matmul_tn(a, b, bm, out_dtype, name, bn=None, to_kernel=True):
    na, t, m = a.shape
    nb, _, n = b.shape
    bn = n if bn is None else bn

    def body(a_ref, b_ref, o_ref):
        o_ref[0] = _dot_tn(a_ref[0], b_ref[0]).astype(out_dtype)

    return _out_hbm(pl.pallas_call(
        body, name=name,
        grid=(na, m // bm, n // bn),
        in_specs=[pl.BlockSpec((1, t, bm), lambda i, k, j: (i, 0, k)),
                  pl.BlockSpec((1, t, bn), lambda i, k, j: (jnp.maximum(i - (na - nb), 0), 0, j))],
        out_specs=pl.BlockSpec((1, bm, bn), lambda i, k, j: (i, k, j)),
        out_shape=pltpu.HBM((na, m, n), out_dtype) if to_kernel else jax.ShapeDtypeStruct((na, m, n), out_dtype),
        compiler_params=_cparams(("arbitrary", "parallel", "parallel")),
    )(*_in_hbm(a, b)), to_kernel)


def _inproj_fwd(h, g, wint):
    tm = TOK_TILE

    def body(h_ref, g_ref, w_ref, z_ref):
        xh, _ = _rms_stats(h_ref[...])
        z_ref[...] = _dot_nt((xh * g_ref[...]).astype(BF16), w_ref[...])

    return pl.pallas_call(
        body, name="inproj_fwd",
        grid=(SEQ // tm,),
        in_specs=[pl.BlockSpec((tm, D_MODEL), lambda m: (m, 0)),
                  pl.BlockSpec((1, D_MODEL), lambda m: (0, 0)),
                  pl.BlockSpec((None, IN_COLS, D_MODEL), lambda m: (0, 0, 0))],
        out_specs=pl.BlockSpec((tm, IN_COLS), lambda m: (m, 0)),
        out_shape=jax.ShapeDtypeStruct((SEQ, IN_COLS), F32),
        compiler_params=_cparams(("parallel",)),
    )(*_in_hbm(h, g, wint))


def _inproj_bwd(h, g, dh, dz, wint):
    tm = TOK_TILE

    def body(h_ref, g_ref, dh_ref, dz_ref, w_ref, o_ref, u_ref, dg_ref):
        xh, r = _rms_stats(h_ref[...])
        u_ref[0] = (xh * g_ref[...]).astype(BF16)
        dx, dg = _rms_bwd(_dot(dz_ref[...], w_ref[...]), xh, r, g_ref[...])
        o_ref[...] = dh_ref[...] + dx
        _accumulate(dg_ref, pl.program_id(0) == 0, dg)

    return _out_hbm(pl.pallas_call(
        body, name="inproj_bwd",
        grid=(SEQ // tm,),
        in_specs=[pl.BlockSpec((tm, D_MODEL), lambda m: (m, 0)),
                  pl.BlockSpec((1, D_MODEL), lambda m: (0, 0)),
                  pl.BlockSpec((tm, D_MODEL), lambda m: (m, 0)),
                  pl.BlockSpec((tm, IN_COLS), lambda m: (m, 0)),
                  pl.BlockSpec((None, IN_COLS, D_MODEL), lambda m: (0, 0, 0))],
        out_specs=[pl.BlockSpec((tm, D_MODEL), lambda m: (m, 0)),
                   pl.BlockSpec((1, tm, D_MODEL), lambda m: (0, m, 0)),
                   pl.BlockSpec((1, D_MODEL), lambda m: (0, 0))],
        out_shape=[jax.ShapeDtypeStruct((SEQ, D_MODEL), F32),
                   pltpu.HBM((1, SEQ, D_MODEL), BF16),
                   jax.ShapeDtypeStruct((1, D_MODEL), F32)],
        compiler_params=_cparams(("arbitrary",)),
    )(*_in_hbm(h, g, dh, dz, wint)), (1,))


def _row_ids(n, w):
    return lax.broadcasted_iota(jnp.int32, (n, w), 0)


def _bcast_row(x, i, n):
    return jnp.broadcast_to(x[i:i + 1, :], (n, x.shape[1]))


def _conv_taps(v, tail):
    n, w = v.shape
    rid = _row_ids(n, w)
    v1 = jnp.where(rid == 0, _bcast_row(tail, 7, n), pltpu.roll(v, 1, 0))
    v2 = jnp.where(rid == 0, _bcast_row(tail, 6, n),
                   jnp.where(rid == 1, _bcast_row(tail, 7, n), pltpu.roll(v, 2, 0)))
    return v1, v2


def _block_tiles():
    half_rows, half_cols = SSM_W // 2, N_STATE // 2
    for half in range(2):
        for part in range(2):
            yield (slice(half * half_rows, (half + 1) * half_rows),
                   slice(part * N_STATE + half * half_cols, part * N_STATE + (half + 1) * half_cols))


def _block_expand(x, mat_ref, out_ref):
    for rows, cols in _block_tiles():
        out_ref[:, cols] = _dot(x[:, rows], mat_ref[rows, cols])


def _block_contract(s, mat_ref):
    halves = {}
    for rows, cols in _block_tiles():
        part = _dot_nt(s[:, cols], mat_ref[rows, cols])
        halves[rows.start] = part if rows.start not in halves else halves[rows.start] + part
    return jnp.concatenate([halves[k] for k in sorted(halves)], axis=1)


def _block_wgrad(a, b, name):
    t = a.shape[1]
    half_rows, half_cols = SSM_W // 2, N_STATE // 2

    def body(a_ref, b_ref, o_ref):
        o_ref[...] = _dot_tn(a_ref[...], b_ref[...])

    return pl.pallas_call(
        body, name=name,
        grid=(2, 2),
        in_specs=[pl.BlockSpec((None, t, half_rows), lambda h, p: (0, 0, h)),
                  pl.BlockSpec((None, t, half_cols), lambda h, p: (0, 0, 2 * p + h))],
        out_specs=pl.BlockSpec((half_rows, half_cols), lambda h, p: (h, 2 * p + h)),
        out_shape=jax.ShapeDtypeStruct((SSM_W, 2 * N_STATE), F32),
        compiler_params=_cparams(("parallel", "parallel")),
    )(*_in_hbm(a, b))


def _scan_chunk(work, ltab, carry, reverse):
    nblk = CHUNK // SUBLANES
    for gi in range(N_STATE // LANE_GROUP):
        cre = pl.ds(gi * LANE_GROUP, LANE_GROUP)
        cim = pl.ds(N_STATE + gi * LANE_GROUP, LANE_GROUP)
        pows = [(ltab[8 * k:8 * k + 8, cre], ltab[8 * k:8 * k + 8, cim]) for k in range(3)]
        pr = ltab[24:32, cre]
        pi = ltab[24:32, cim]

        def blk(i, c, cre=cre, cim=cim, pows=pows, pr=pr, pi=pi):
            cr, ci = c
            b = (nblk - 1 - i) if reverse else i
            r0 = pl.multiple_of(b * SUBLANES, SUBLANES)
            xr = work[pl.ds(r0, SUBLANES), cre]
            xi = work[pl.ds(r0, SUBLANES), cim]
            for k, s in enumerate((1, 2, 4)):
                lr, li = pows[k]
                shift = SUBLANES - s if reverse else s
                sr = pltpu.roll(xr, shift, 0)
                si = pltpu.roll(xi, shift, 0)
                xr, xi = xr + lr * sr - li * si, xi + lr * si + li * sr
            xr, xi = xr + pr * cr - pi * ci, xi + pr * ci + pi * cr
            work[pl.ds(r0, SUBLANES), cre] = xr
            work[pl.ds(r0, SUBLANES), cim] = xi
            edge = 0 if reverse else SUBLANES - 1
            return _bcast_row(xr, edge, SUBLANES), _bcast_row(xi, edge, SUBLANES)

        cr, ci = lax.fori_loop(0, nblk, blk, (carry[:, cre], carry[:, cim]))
        carry[:, cre] = cr
        carry[:, cim] = ci


def _s5conv_fwd(z, convw, convb, bbmat, ccmat, dvec, ltab):
    def body(z_ref, cw_ref, cb_ref, bb_ref, cc_ref, d_ref, lt_ref, ya_ref, ys_ref, hs_ref,
             work, carry, tail):
        c = pl.program_id(0)

        @pl.when(c == 0)
        def _():
            carry[...] = jnp.zeros_like(carry)
            tail[...] = jnp.zeros_like(tail)

        zb = z_ref[:, 0:CONV_W]
        v = z_ref[:, CONV_W:2 * CONV_W] * z_ref[:, 2 * CONV_W:3 * CONV_W]
        us = z_ref[:, 3 * CONV_W:4 * CONV_W]
        v1, v2 = _conv_taps(v, tail[...])
        tail[...] = v[CHUNK - 8:CHUNK, :]
        y = cw_ref[0:1, :] * v2 + cw_ref[1:2, :] * v1 + cw_ref[2:3, :] * v
        ya_ref[...] = zb * (y + cb_ref[...])

        _block_expand(us.astype(BF16), bb_ref, work)
        _scan_chunk(work, lt_ref, carry, reverse=False)
        hs = work[...].astype(BF16)
        hs_ref[...] = hs
        ys_ref[...] = _block_contract(hs, cc_ref) + d_ref[...] * us

    return _out_hbm(pl.pallas_call(
        body, name="s5conv_fwd",
        grid=(N_CHUNKS,),
        in_specs=[pl.BlockSpec((CHUNK, IN_COLS), lambda c: (c, 0)),
                  pl.BlockSpec((3, CONV_W), lambda c: (0, 0)),
                  pl.BlockSpec((1, CONV_W), lambda c: (0, 0)),
                  pl.BlockSpec((SSM_W, 2 * N_STATE), lambda c: (0, 0)),
                  pl.BlockSpec((SSM_W, 2 * N_STATE), lambda c: (0, 0)),
                  pl.BlockSpec((1, SSM_W), lambda c: (0, 0)),
                  pl.BlockSpec((32, 2 * N_STATE), lambda c: (0, 0))],
        out_specs=[pl.BlockSpec((CHUNK, CONV_W), lambda c: (c, 0)),
                   pl.BlockSpec((CHUNK, SSM_W), lambda c: (c, 0)),
                   pl.BlockSpec((CHUNK, 2 * N_STATE), lambda c: (c, 0))],
        out_shape=[pltpu.HBM((SEQ, CONV_W), F32),
                   pltpu.HBM((SEQ, SSM_W), F32),
                   jax.ShapeDtypeStruct((SEQ, 2 * N_STATE), BF16)],
        scratch_shapes=[pltpu.VMEM((CHUNK, 2 * N_STATE), F32),
                        pltpu.VMEM((8, 2 * N_STATE), F32),
                        pltpu.VMEM((8, CONV_W), F32)],
        compiler_params=_cparams(("arbitrary",)),
    )(*_in_hbm(z, convw, convb, bbmat, ccmat, dvec, ltab)), (0, 1))


def _s5conv_bwd(z, hs, dya, dys, convw, convb, bbmat, ccmat, dvec, ltab_rev):
    nc = N_CHUNKS
    hb = 16

    def body(z_ref, zp_ref, hs_ref, hp_ref, dya_ref, dys_ref, cw_ref, cb_ref, bb_ref, cc_ref, d_ref, lt_ref,
             dz_ref, g_ref, us_ref, dyb_ref, dl_ref, dcw_ref, work, carry, head):
        i = pl.program_id(0)
        first_chunk = i == nc - 1

        @pl.when(i == 0)
        def _():
            carry[...] = jnp.zeros_like(carry)
            head[...] = jnp.zeros_like(head)
            dl_ref[...] = jnp.zeros_like(dl_ref)
            dcw_ref[...] = jnp.zeros_like(dcw_ref)

        us = z_ref[:, 3 * CONV_W:4 * CONV_W]
        dy = dys_ref[...]
        dy_bf = dy.astype(BF16)
        us_ref[0] = us.astype(BF16)
        dyb_ref[0] = dy_bf

        _block_expand(dy_bf, cc_ref, work)
        _scan_chunk(work, lt_ref, carry, reverse=True)
        gg = work[...]
        gg_bf = gg.astype(BF16)
        g_ref[0] = gg_bf
        dus = d_ref[...] * dy + _block_contract(gg_bf, bb_ref)

        hcur = hs_ref[...].astype(F32)
        hlast = hp_ref[...].astype(F32)[hb - 1:hb, :]
        hlast = jnp.where(first_chunk, 0.0, hlast)
        rid = _row_ids(CHUNK, 2 * N_STATE)
        hprev = jnp.where(rid == 0, jnp.broadcast_to(hlast, (CHUNK, 2 * N_STATE)), pltpu.roll(hcur, 1, 0))
        gr, gi = gg[:, :N_STATE], gg[:, N_STATE:]
        hr, hi = hprev[:, :N_STATE], hprev[:, N_STATE:]
        dl_ref[:, :N_STATE] += (gr * hr + gi * hi).reshape(CHUNK // 8, 8, N_STATE).sum(axis=0)
        dl_ref[:, N_STATE:] += (gi * hr - gr * hi).reshape(CHUNK // 8, 8, N_STATE).sum(axis=0)

        @pl.when(i == nc - 1)
        def _():
            dl_ref[0:1, :] = jnp.sum(dl_ref[...], axis=0, keepdims=True)

        zb = z_ref[:, 0:CONV_W]
        zc = z_ref[:, CONV_W:2 * CONV_W]
        zv = z_ref[:, 2 * CONV_W:3 * CONV_W]
        v = zc * zv
        vtail = jnp.where(first_chunk, 0.0, zp_ref[:, CONV_W:2 * CONV_W] * zp_ref[:, 2 * CONV_W:3 * CONV_W])
        v1, v2 = _conv_taps(v, vtail)
        w0, w1, w2 = cw_ref[0:1, :], cw_ref[1:2, :], cw_ref[2:3, :]
        y = w0 * v2 + w1 * v1 + w2 * v
        dya_v = dya_ref[...]
        dzb = dya_v * (y + cb_ref[...])
        dyc = dya_v * zb
        hd = head[...]
        rc = _row_ids(CHUNK, CONV_W)
        n1 = jnp.where(rc == CHUNK - 1, _bcast_row(hd, 0, CHUNK), pltpu.roll(dyc, CHUNK - 1, 0))
        n2 = jnp.where(rc == CHUNK - 1, _bcast_row(hd, 1, CHUNK),
                       jnp.where(rc == CHUNK - 2, _bcast_row(hd, 0, CHUNK), pltpu.roll(dyc, CHUNK - 2, 0)))
        head[...] = dyc[0:8, :]
        dv = w2 * dyc + w1 * n1 + w0 * n2
        dz_ref[:, 0:CONV_W] = dzb.astype(BF16)
        dz_ref[:, CONV_W:2 * CONV_W] = (dv * zv).astype(BF16)
        dz_ref[:, 2 * CONV_W:3 * CONV_W] = (dv * zc).astype(BF16)
        dz_ref[:, 3 * CONV_W:4 * CONV_W] = dus.astype(BF16)
        dcw_ref[0:1, :] += jnp.sum(dyc * v2, axis=0, keepdims=True)
        dcw_ref[1:2, :] += jnp.sum(dyc * v1, axis=0, keepdims=True)
        dcw_ref[2:3, :] += jnp.sum(dyc * v, axis=0, keepdims=True)
        dcw_ref[3:4, :] += jnp.sum(dyc, axis=0, keepdims=True)
        dcw_ref[4:5, :] += jnp.sum(dy * us, axis=0, keepdims=True)

    rev = lambda i: nc - 1 - i
    return _out_hbm(pl.pallas_call(
        body, name="s5conv_bwd",
        grid=(nc,),
        in_specs=[pl.BlockSpec((CHUNK, IN_COLS), lambda i: (rev(i), 0)),
                  pl.BlockSpec((8, IN_COLS), lambda i: (jnp.maximum(rev(i) * (CHUNK // 8) - 1, 0), 0)),
                  pl.BlockSpec((CHUNK, 2 * N_STATE), lambda i: (rev(i), 0)),
                  pl.BlockSpec((hb, 2 * N_STATE), lambda i: (jnp.maximum(rev(i) * (CHUNK // hb) - 1, 0), 0)),
                  pl.BlockSpec((CHUNK, CONV_W), lambda i: (rev(i), 0)),
                  pl.BlockSpec((CHUNK, SSM_W), lambda i: (rev(i), 0)),
                  pl.BlockSpec((3, CONV_W), lambda i: (0, 0)),
                  pl.BlockSpec((1, CONV_W), lambda i: (0, 0)),
                  pl.BlockSpec((SSM_W, 2 * N_STATE), lambda i: (0, 0)),
                  pl.BlockSpec((SSM_W, 2 * N_STATE), lambda i: (0, 0)),
                  pl.BlockSpec((1, SSM_W), lambda i: (0, 0)),
                  pl.BlockSpec((32, 2 * N_STATE), lambda i: (0, 0))],
        out_specs=[pl.BlockSpec((CHUNK, IN_COLS), lambda i: (rev(i), 0)),
                   pl.BlockSpec((1, CHUNK, 2 * N_STATE), lambda i: (0, rev(i), 0)),
                   pl.BlockSpec((1, CHUNK, SSM_W), lambda i: (0, rev(i), 0)),
                   pl.BlockSpec((1, CHUNK, SSM_W), lambda i: (0, rev(i), 0)),
                   pl.BlockSpec((8, 2 * N_STATE), lambda i: (0, 0)),
                   pl.BlockSpec((8, CONV_W), lambda i: (0, 0))],
        out_shape=[jax.ShapeDtypeStruct((SEQ, IN_COLS), BF16),
                   pltpu.HBM((1, SEQ, 2 * N_STATE), BF16),
                   pltpu.HBM((1, SEQ, SSM_W), BF16),
                   pltpu.HBM((1, SEQ, SSM_W), BF16),
                   jax.ShapeDtypeStruct((8, 2 * N_STATE), F32),
                   jax.ShapeDtypeStruct((8, CONV_W), F32)],
        scratch_shapes=[pltpu.VMEM((CHUNK, 2 * N_STATE), F32),
                        pltpu.VMEM((8, 2 * N_STATE), F32),
                        pltpu.VMEM((8, CONV_W), F32)],
        compiler_params=_cparams(("arbitrary",)),
    )(*_in_hbm(z, z, hs, hs, dya, dys, convw, convb, bbmat, ccmat, dvec, ltab_rev)), (1, 2, 3))


def _mix_out_fwd(h, ya, ys, gluw, glub, con, son, wout):
    tm = TOK_TILE

    def body(h_ref, ya_ref, ys_ref, gw_ref, gb_ref, con_ref, son_ref, wo_ref, o_ref):
        zg, _ = _gelu(ys_ref[...])
        q = _dot(zg.astype(BF16), gw_ref[...]) + gb_ref[...]
        out_s = zg * _sigmoid(q)
        na, _ = _rms_stats(ya_ref[...])
        ns, _ = _rms_stats(out_s)
        o_ref[...] = (h_ref[...]
                      + _dot((na * con_ref[...]).astype(BF16), wo_ref[0:CONV_W, :])
                      + _dot((ns * son_ref[...]).astype(BF16), wo_ref[CONV_W:2 * CONV_W, :]))

    row = lambda m: (m, 0)
    fixed = lambda m: (0, 0)
    return pl.pallas_call(
        body, name="mix_out_fwd",
        grid=(SEQ // tm,),
        in_specs=[pl.BlockSpec((tm, D_MODEL), row), pl.BlockSpec((tm, CONV_W), row), pl.BlockSpec((tm, SSM_W), row),
                  pl.BlockSpec((SSM_W, SSM_W), fixed), pl.BlockSpec((1, SSM_W), fixed),
                  pl.BlockSpec((1, CONV_W), fixed), pl.BlockSpec((1, SSM_W), fixed),
                  pl.BlockSpec((None, D_MODEL, D_MODEL), lambda m: (0, 0, 0))],
        out_specs=pl.BlockSpec((tm, D_MODEL), row),
        out_shape=jax.ShapeDtypeStruct((SEQ, D_MODEL), F32),
        compiler_params=_cparams(("parallel",)),
    )(*_in_hbm(h, ya, ys, gluw, glub, con, son, wout))


def _mix_out_bwd(dh, ya, ys, gluw, glub, con, son, wout):
    tm = TOK_TILE

    def body(dh_ref, ya_ref, ys_ref, gw_ref, gb_ref, con_ref, son_ref, wo_ref,
             dya_ref, dys_ref, yc_ref, dhb_ref, zg_ref, dq_ref, part_ref):
        ysv = ys_ref[...]
        zg, th = _gelu(ysv)
        zg_bf = zg.astype(BF16)
        s = _sigmoid(_dot(zg_bf, gw_ref[...]) + gb_ref[...])
        out_s = zg * s
        na, ra = _rms_stats(ya_ref[...])
        ns, rs = _rms_stats(out_s)
        dh_bf = dh_ref[...].astype(BF16)
        yc_ref[0, :, 0:CONV_W] = (na * con_ref[...]).astype(BF16)
        yc_ref[0, :, CONV_W:2 * CONV_W] = (ns * son_ref[...]).astype(BF16)
        dhb_ref[0] = dh_bf
        dca = _dot_nt(dh_bf, wo_ref[0:CONV_W, :])
        dcs = _dot_nt(dh_bf, wo_ref[CONV_W:2 * CONV_W, :])
        dya, dcon = _rms_bwd(dca, na, ra, con_ref[...])
        dos, dson = _rms_bwd(dcs, ns, rs, son_ref[...])
        dya_ref[...] = dya
        dq = dos * zg * s * (1.0 - s)
        dq_bf = dq.astype(BF16)
        dzg = dos * s + _dot_nt(dq_bf, gw_ref[...])
        dys_ref[...] = dzg * _gelu_grad(ysv, th)
        zg_ref[0] = zg_bf
        dq_ref[0] = dq_bf
        rid = _row_ids(SUBLANES, SSM_W)
        part = jnp.zeros((SUBLANES, SSM_W), F32)
        for i, rowv in enumerate((dcon, dson, jnp.sum(dq, axis=0, keepdims=True))):
            part = jnp.where(rid == i, jnp.broadcast_to(rowv, (SUBLANES, SSM_W)), part)
        _accumulate(part_ref, pl.program_id(0) == 0, part)

    row = lambda m: (m, 0)
    fixed = lambda m: (0, 0)
    lead = lambda m: (0, m, 0)
    return _out_hbm(pl.pallas_call(
        body, name="mix_out_bwd",
        grid=(SEQ // tm,),
        in_specs=[pl.BlockSpec((tm, D_MODEL), row), pl.BlockSpec((tm, CONV_W), row), pl.BlockSpec((tm, SSM_W), row),
                  pl.BlockSpec((SSM_W, SSM_W), fixed), pl.BlockSpec((1, SSM_W), fixed),
                  pl.BlockSpec((1, CONV_W), fixed), pl.BlockSpec((1, SSM_W), fixed),
                  pl.BlockSpec((None, D_MODEL, D_MODEL), lambda m: (0, 0, 0))],
        out_specs=[pl.BlockSpec((tm, CONV_W), row), pl.BlockSpec((tm, SSM_W), row),
                   pl.BlockSpec((1, tm, D_MODEL), lead), pl.BlockSpec((1, tm, D_MODEL), lead),
                   pl.BlockSpec((1, tm, SSM_W), lead), pl.BlockSpec((1, tm, SSM_W), lead),
                   pl.BlockSpec((8, SSM_W), fixed)],
        out_shape=[pltpu.HBM((SEQ, CONV_W), F32), pltpu.HBM((SEQ, SSM_W), F32),
                   pltpu.HBM((1, SEQ, D_MODEL), BF16), pltpu.HBM((1, SEQ, D_MODEL), BF16),
                   pltpu.HBM((1, SEQ, SSM_W), BF16), pltpu.HBM((1, SEQ, SSM_W), BF16),
                   jax.ShapeDtypeStruct((8, SSM_W), F32)],
        compiler_params=_cparams(("arbitrary",)),
    )(*_in_hbm(dh, ya, ys, gluw, glub, con, son, wout)), (0, 1, 2, 3, 4, 5))


def _ple_fwd(h, g, p, wgate, wprojt):
    tm = TOK_TILE

    def body(h_ref, g_ref, p_ref, wg_ref, wp_ref, o_ref):
        x = h_ref[...]
        xh, _ = _rms_stats(x)
        s = _sigmoid(_dot((xh * g_ref[...]).astype(BF16), wg_ref[...]))
        o_ref[...] = x + _dot_nt(p_ref[...].astype(BF16), wp_ref[...]) * s

    row = lambda m: (m, 0)
    fixed = lambda m: (0, 0)
    return pl.pallas_call(
        body, name="ple_fwd",
        grid=(SEQ // tm,),
        in_specs=[pl.BlockSpec((tm, D_MODEL), row), pl.BlockSpec((1, D_MODEL), fixed), pl.BlockSpec((tm, PLE_DIM), row),
                  pl.BlockSpec((None, D_MODEL, D_MODEL), lambda m: (0, 0, 0)), pl.BlockSpec((D_MODEL, PLE_DIM), fixed)],
        out_specs=pl.BlockSpec((tm, D_MODEL), row),
        out_shape=jax.ShapeDtypeStruct((SEQ, D_MODEL), F32),
        compiler_params=_cparams(("parallel",)),
    )(*_in_hbm(h, g, p, wgate, wprojt))


def _ple_bwd(h, g, p, dh, wgate, wprojt):
    tm = TOK_TILE

    def body(h_ref, g_ref, p_ref, dh_ref, wg_ref, wp_ref, o_ref, u_ref, dq_ref, dpp_ref, pb_ref, dg_ref):
        xh, r = _rms_stats(h_ref[...])
        u = (xh * g_ref[...]).astype(BF16)
        s = _sigmoid(_dot(u, wg_ref[...]))
        p_bf = p_ref[...].astype(BF16)
        pp = _dot_nt(p_bf, wp_ref[...])
        dhv = dh_ref[...]
        dq = (dhv * pp * s * (1.0 - s)).astype(BF16)
        u_ref[0] = u
        dq_ref[0] = dq
        dpp_ref[0] = (dhv * s).astype(BF16)
        pb_ref[0] = p_bf
        dx, dg = _rms_bwd(_dot_nt(dq, wg_ref[...]), xh, r, g_ref[...])
        o_ref[...] = dhv + dx
        _accumulate(dg_ref, pl.program_id(0) == 0, dg)

    row = lambda m: (m, 0)
    fixed = lambda m: (0, 0)
    lead = lambda m: (0, m, 0)
    big = pltpu.HBM((1, SEQ, D_MODEL), BF16)
    return _out_hbm(pl.pallas_call(
        body, name="ple_bwd",
        grid=(SEQ // tm,),
        in_specs=[pl.BlockSpec((tm, D_MODEL), row), pl.BlockSpec((1, D_MODEL), fixed), pl.BlockSpec((tm, PLE_DIM), row),
                  pl.BlockSpec((tm, D_MODEL), row),
                  pl.BlockSpec((None, D_MODEL, D_MODEL), lambda m: (0, 0, 0)), pl.BlockSpec((D_MODEL, PLE_DIM), fixed)],
        out_specs=[pl.BlockSpec((tm, D_MODEL), row),
                   pl.BlockSpec((1, tm, D_MODEL), lead), pl.BlockSpec((1, tm, D_MODEL), lead),
                   pl.BlockSpec((1, tm, D_MODEL), lead), pl.BlockSpec((1, tm, PLE_DIM), lead),
                   pl.BlockSpec((1, D_MODEL), fixed)],
        out_shape=[jax.ShapeDtypeStruct((SEQ, D_MODEL), F32), big, big, big,
                   pltpu.HBM((1, SEQ, PLE_DIM), BF16),
                   jax.ShapeDtypeStruct((1, D_MODEL), F32)],
        compiler_params=_cparams(("arbitrary",)),
    )(*_in_hbm(h, g, p, dh, wgate, wprojt)), (1, 2, 3, 4))


def _final_loss(h, g, target):
    tm = TOK_TILE

    def body(h_ref, g_ref, t_ref, loss_ref, dh_ref, dg_ref):
        first = pl.program_id(0) == 0
        xh, r = _rms_stats(h_ref[...])
        diff = xh * g_ref[...] - t_ref[...]
        part = 0.5 * jnp.sum(jnp.mean(diff * diff, axis=-1, keepdims=True), axis=0, keepdims=True)
        _accumulate(loss_ref, first, jnp.broadcast_to(part, (SUBLANES, LANES)))
        dx, dg = _rms_bwd(diff * (1.0 / D_MODEL), xh, r, g_ref[...])
        dh_ref[...] = dx
        _accumulate(dg_ref, first, dg)

    row = lambda m: (m, 0)
    fixed = lambda m: (0, 0)
    return pl.pallas_call(
        body, name="final_loss",
        grid=(SEQ // tm,),
        in_specs=[pl.BlockSpec((tm, D_MODEL), row), pl.BlockSpec((1, D_MODEL), fixed),
                  pl.BlockSpec((tm, D_MODEL), row)],
        out_specs=[pl.BlockSpec((SUBLANES, LANES), fixed),
                   pl.BlockSpec((tm, D_MODEL), row),
                   pl.BlockSpec((1, D_MODEL), fixed)],
        out_shape=[jax.ShapeDtypeStruct((SUBLANES, LANES), F32),
                   jax.ShapeDtypeStruct((SEQ, D_MODEL), F32),
                   jax.ShapeDtypeStruct((1, D_MODEL), F32)],
        compiler_params=_cparams(("arbitrary",)),
    )(*_in_hbm(h, g, target))


def _disc(ar, ai, ldt):
    dt = jnp.exp(ldt)
    mag = jnp.exp(ar * dt)
    ph = ai * dt
    lr, li = mag * jnp.cos(ph), mag * jnp.sin(ph)
    nr, ni = lr - 1.0, li
    den = ar * ar + ai * ai
    return lr, li, (nr * ar + ni * ai) / den, (ni * ar - nr * ai) / den


def _s5_disc(a_row, ldt_row, a_rep, ldt_rep, bt, ct, tile_e, mask):
    n = N_STATE

    def body(ar_ref, lr_ref, ap_ref, lp_ref, b_ref, c_ref, e_ref, m_ref, lt_ref, ltr_ref, bb_ref, cc_ref):
        lr, li, _, _ = _disc(ar_ref[0], ar_ref[1], lr_ref[...])
        pr, pi = lr, li
        rid = _row_ids(SUBLANES, n)
        for k in range(1, 9):
            for ref, sgn, edge in ((lt_ref, 1.0, 24 + k - 1), (ltr_ref, -1.0, 24 + 8 - k)):
                if k in (1, 2, 4):
                    r0 = {1: 0, 2: 8, 4: 16}[k]
                    keep = (rid >= k) if ref is lt_ref else (rid < SUBLANES - k)
                    ref[r0:r0 + 8, 0:n] = jnp.where(keep, jnp.broadcast_to(pr, (8, n)), 0.0)
                    ref[r0:r0 + 8, n:2 * n] = jnp.where(keep, jnp.broadcast_to(sgn * pi, (8, n)), 0.0)
                ref[edge:edge + 1, 0:n] = pr
                ref[edge:edge + 1, n:2 * n] = sgn * pi
            pr, pi = pr * lr - pi * li, pr * li + pi * lr
        _, _, fr, fi = _disc(ap_ref[0], ap_ref[1], lp_ref[...])
        br, bi = b_ref[0], b_ref[1]
        e = e_ref[...]
        m = m_ref[...].astype(F32)
        bb_ref[:, 0:n] = (_dot((fr * br - fi * bi).astype(BF16), e) * m).astype(BF16)
        bb_ref[:, n:2 * n] = (_dot((fr * bi + fi * br).astype(BF16), e) * m).astype(BF16)
        cc_ref[:, 0:n] = (_dot(c_ref[0].astype(BF16), e) * m).astype(BF16)
        cc_ref[:, n:2 * n] = (-(_dot(c_ref[1].astype(BF16), e) * m)).astype(BF16)

    return pl.pallas_call(
        body, name="s5_disc",
        out_shape=[jax.ShapeDtypeStruct((32, 2 * n), F32), jax.ShapeDtypeStruct((32, 2 * n), F32),
                   jax.ShapeDtypeStruct((SSM_W, 2 * n), BF16), jax.ShapeDtypeStruct((SSM_W, 2 * n), BF16)],
        compiler_params=_cparams(None),
    )(a_row, ldt_row, a_rep, ldt_rep, bt, ct, tile_e, mask)


def _dot_exact(x, sel):
    hi = x.astype(BF16)
    r1 = x - hi.astype(F32)
    mid = r1.astype(BF16)
    lo = (r1 - mid.astype(F32)).astype(BF16)
    return _dot(hi, sel) + _dot(mid, sel) + _dot(lo, sel)


def _s5_disc_bwd(a, ldt, a_rep, ldt_rep, bt, mask, dl, d_bb, d_cc, fold):
    n = N_STATE

    def body(a_ref, l_ref, ap_ref, lp_ref, b_ref, m_ref, dl_ref, dbb_ref, dcc_ref, f_ref,
             da_ref, dldt_ref, db_ref, dc_ref):
        m = m_ref[...].astype(F32)
        fold_m = f_ref[...]
        diag = lambda x: _dot_exact(jnp.where(m > 0.0, x, 0.0), fold_m)
        dr, di = diag(dbb_ref[:, 0:n]), diag(dbb_ref[:, n:2 * n])
        dc_ref[0] = diag(dcc_ref[:, 0:n])
        dc_ref[1] = -diag(dcc_ref[:, n:2 * n])
        _, _, fr, fi = _disc(ap_ref[0], ap_ref[1], lp_ref[...])
        br, bi = b_ref[0], b_ref[1]
        db_ref[0] = fr * dr + fi * di
        db_ref[1] = fr * di - fi * dr
        per_state = lambda x: x.reshape(SSM_GROUPS, SSM_GROUP, SSM_STATE).sum(axis=1)
        dfr = per_state(dr * br + di * bi)
        dfi = per_state(di * br - dr * bi)
        _, vjp = jax.vjp(_disc, a_ref[0], a_ref[1], l_ref[...])
        dar, dai, dldt = vjp((dl_ref[0], dl_ref[1], dfr, dfi))
        da_ref[0] = dar
        da_ref[1] = dai
        dldt_ref[...] = jnp.sum(dldt, axis=1, keepdims=True)

    return pl.pallas_call(
        body, name="s5_disc_bwd",
        out_shape=[jax.ShapeDtypeStruct((2, SSM_GROUPS, SSM_STATE), F32),
                   jax.ShapeDtypeStruct((SSM_GROUPS, 1), F32),
                   jax.ShapeDtypeStruct((2, SSM_W, SSM_STATE), F32),
                   jax.ShapeDtypeStruct((2, SSM_W, SSM_STATE), F32)],
        compiler_params=_cparams(None),
    )(a, ldt, a_rep, ldt_rep, bt, mask, dl, d_bb, d_cc, fold)


def _row_block(rows, cap=512):
    for bm in range(min(cap, rows), 0, -1):
        if rows % bm == 0 and (bm % 8 == 0 or bm == rows):
            return bm
    return rows


SUM_PARTS = 2


def _own_pieces(segs, rtot):
    pr = rtot // SUM_PARTS
    assert pr * SUM_PARTS == rtot and pr % 16 == 0
    offs = _seg_offsets(segs)
    pieces = [[] for _ in range(SUM_PARTS)]
    for a, (n, r) in enumerate(segs):
        for m in range(n):
            lo = offs[a] + m * r
            for h in range(SUM_PARTS):
                clo, chi = max(lo, h * pr), min(lo + r, (h + 1) * pr)
                if chi > clo:
                    pieces[h].append((a, m, clo - lo, clo - h * pr, chi - clo))
    return pieces


def _pair_rows(srcs, got_ref, segs, pieces, h, chip, own_v, got_v, sems):
    pr = own_v.shape[0]
    dev = 2 * chip + lax.axis_index("c")
    for hh in range(SUM_PARTS):
        @pl.when(h == hh)
        def _(hh=hh):
            cps = [pltpu.make_async_copy(got_ref.at[chip, pl.ds(hh * pr, pr), :], got_v, sems.at[0])]
            for i, (a, m, so, do, rows) in enumerate(pieces[hh]):
                start = pl.multiple_of(dev * segs[a][1] + so, 16)
                cps.append(pltpu.make_async_copy(srcs[a].at[m, pl.ds(start, rows), :],
                                                 own_v.at[pl.ds(do, rows), :], sems.at[1 + i]))
            for cp in cps:
                cp.start()
            for cp in cps:
                cp.wait()
    return own_v[...].astype(F32) + got_v[...].astype(F32)


def _pair_sum(fulls, got, segs):
    ns = len(segs)
    _, rtot, c = got.shape
    pieces = _own_pieces(segs, rtot)
    pr = rtot // SUM_PARTS

    def body(*refs):
        srcs = refs[:ns]
        got_ref, pbf_ref, own_v, got_v, sems = refs[ns:]
        x, y, _ = _mesh_pos()
        j = pl.program_id(1)
        chip = jnp.where(j == 0, 2 * (1 - x) + y, jnp.where(j == 1, 2 * x + 1 - y, 2 * (1 - x) + 1 - y))
        pbf_ref[0] = _pair_rows(srcs, got_ref, segs, pieces, pl.program_id(0), chip, own_v, got_v, sems).astype(BF16)

    return pl.pallas_call(
        body, name="pair_sum",
        grid=(SUM_PARTS, 3),
        in_specs=[HBM] * (ns + 1), out_specs=pl.BlockSpec((1, pr, c), lambda h, j: (j, h, 0)),
        out_shape=pltpu.HBM((3, rtot, c), BF16),
        scratch_shapes=[pltpu.VMEM((pr, c), BF16), pltpu.VMEM((pr, c), BF16),
                        pltpu.SemaphoreType.DMA((1 + max(len(p) for p in pieces),))],
        compiler_params=_cparams(("arbitrary", "arbitrary")),
    )(*_in_hbm(*fulls, got))


def _chip_sum(fulls, got, rb, segs, layer, into):
    ns = len(segs)
    _, rtot, c = got.shape
    pieces = _own_pieces(segs, rtot)
    pr = rtot // SUM_PARTS

    def body(*refs):
        srcs = refs[:ns]
        got_ref, r_ref = refs[ns], refs[ns + 1]
        s_ref, own_v, got_v, sems = refs[-4:]
        x, y, _ = _mesh_pos()
        own = _pair_rows(srcs, got_ref, segs, pieces, pl.program_id(0), 2 * x + y, own_v, got_v, sems)
        s_ref[0] = ((own + r_ref[0].astype(F32)) + r_ref[1].astype(F32)) + r_ref[2].astype(F32)

    old = [] if into is None else [into]
    return pl.pallas_call(
        body, name="chip_sum",
        grid=(SUM_PARTS,),
        in_specs=[HBM] * (ns + 1) + [pl.BlockSpec((3, pr, c), lambda h: (0, h, 0))] + [HBM] * len(old),
        out_specs=pl.BlockSpec((1, pr, c), lambda h: (layer, h, 0)),
        out_shape=jax.ShapeDtypeStruct((DEPTH, rtot, c), F32),
        input_output_aliases={ns + 2: 0} if old else {},
        scratch_shapes=[pltpu.VMEM((pr, c), BF16), pltpu.VMEM((pr, c), BF16),
                        pltpu.SemaphoreType.DMA((1 + max(len(p) for p in pieces),))],
        compiler_params=_cparams(("arbitrary",)),
    )(*_in_hbm(*fulls, got, rb), *old)


def _adamw(w, g, m, v):
    r, c = w.shape
    bm = _row_block(r)
    bc1 = 1.0 - ADAM_B1 ** ADAM_STEP
    bc2 = 1.0 - ADAM_B2 ** ADAM_STEP

    def body(w_ref, g_ref, m_ref, v_ref, d_ref, nm_ref, nv_ref):
        gv = g_ref[...]
        nm = ADAM_B1 * m_ref[...] + (1.0 - ADAM_B1) * gv
        nv = ADAM_B2 * v_ref[...] + (1.0 - ADAM_B2) * (gv * gv)
        nm_ref[...] = nm
        nv_ref[...] = nv
        d_ref[...] = -ADAM_LR * ((nm / bc1) / (jnp.sqrt(nv / bc2) + ADAM_EPS) + ADAM_WD * w_ref[...])

    spec = pl.BlockSpec((bm, c), lambda k: (k, 0))
    shp = jax.ShapeDtypeStruct((r, c), F32)
    return pl.pallas_call(
        body, name="adamw",
        grid=(r // bm,),
        in_specs=[spec] * 4, out_specs=[spec] * 3, out_shape=[shp] * 3,
        compiler_params=_cparams(("parallel",)),
    )(*_in_hbm(w, g, m, v))


def _adamw_layers(sets, first, nl, prev):
    ns = len(sets)
    depth, r, c = sets[0][0].shape
    bm = _row_block(r, min(512, max(SUBLANES, (24 * MIB) // (ns * 8 * 2 * c * 4))))
    while any(four[4] is not None and four[4] % bm for four in sets):
        bm //= 2
    assert bm % SUBLANES == 0 and r % bm == 0
    bc1 = 1.0 - ADAM_B1 ** ADAM_STEP
    bc2 = 1.0 - ADAM_B2 ** ADAM_STEP

    def body(*refs):
        outs = refs[len(refs) - 4 * ns:]
        for s in range(ns):
            w_ref, m_ref, v_ref, g_ref = refs[4 * s:4 * s + 4]
            go_ref, d_ref, nm_ref, nv_ref = outs[4 * s:4 * s + 4]
            gv = g_ref[...]
            nm = ADAM_B1 * m_ref[...] + (1.0 - ADAM_B1) * gv
            nv = ADAM_B2 * v_ref[...] + (1.0 - ADAM_B2) * (gv * gv)
            go_ref[...] = gv
            nm_ref[...] = nm
            nv_ref[...] = nv
            d_ref[...] = -ADAM_LR * ((nm / bc1) / (jnp.sqrt(nv / bc2) + ADAM_EPS) + ADAM_WD * w_ref[...])

    at = pl.BlockSpec((1, bm, c), lambda i, k: (first + i, k, 0))

    def grad_spec(g_rows):
        if g_rows is None:
            return pl.BlockSpec((1, bm, c), lambda i, k: (i, k, 0))
        return pl.BlockSpec((1, bm, c), lambda i, k: (first + i, g_rows // bm + k, 0))

    shp = jax.ShapeDtypeStruct((depth, r, c), F32)
    old = [] if prev is None else [a for four in prev for a in four]
    flat = pl.pallas_call(
        body, name="adamw_layers",
        grid=(nl, r // bm),
        in_specs=[spec for four in sets for spec in (at, at, at, grad_spec(four[4]))] + [HBM] * len(old),
        out_specs=[at] * (4 * ns), out_shape=[shp] * (4 * ns),
        input_output_aliases={4 * ns + i: i for i in range(len(old))},
        compiler_params=_cparams(("parallel", "parallel")),
    )(*_in_hbm(*[a for four in sets for a in four[:4]]), *old)
    return [flat[4 * s:4 * s + 4] for s in range(ns)]


def _mesh_pos():
    return lax.axis_index("x"), lax.axis_index("y"), lax.axis_index("c")


def _dev_index(p):
    return 4 * p[0] + 2 * p[1] + p[2]


def _seg_offsets(segs):
    offs, o = [], 0
    for n, r in segs:
        offs.append(o)
        o += n * r
    return offs


def _remote(src, dst, send_sem, recv_sem, to):
    return pltpu.make_async_remote_copy(src_ref=src, dst_ref=dst, send_sem=send_sem, recv_sem=recv_sem,
                                        device_id=to, device_id_type=MESH)


def _allgather(pack, segs, name):
    rtot, c = pack.shape
    ns = len(segs)
    offs = _seg_offsets(segs)
    assert rtot == sum(n * r for n, r in segs)

    def body(pack_ref, *refs):
        outs = refs[:ns]
        send_sems, recv_sems, local_sem = refs[ns:]
        x, y, cc = _mesh_pos()
        me, sib = (x, y, cc), (x, y, 1 - cc)
        chips = [(1 - x, y), (x, 1 - y), (1 - x, 1 - y)]

        def pieces(dev, from_pack):
            res = []
            for a, (n, r) in enumerate(segs):
                for m in range(n):
                    dst = outs[a].at[m, pl.ds(pl.multiple_of(dev * r, r), r), :]
                    src = pack_ref.at[pl.ds(offs[a] + m * r, r), :] if from_pack else dst
                    res.append((src, dst))
            return res

        def push(k, dev, to, from_pack):
            for s, d in pieces(dev, from_pack):
                _remote(s, d, send_sems.at[k], recv_sems.at[k], to).start()

        def whole(k):
            return _remote(pack_ref, pack_ref, send_sems.at[k], recv_sems.at[k], me)

        my_dev = _dev_index(me)
        for s, d in pieces(my_dev, True):
            pltpu.make_async_copy(s, d, local_sem).start()
        push(0, my_dev, sib, True)
        for j, chip in enumerate(chips):
            push(1 + j, my_dev, (*chip, cc), True)
        for j, chip in enumerate(chips):
            whole(1 + j).wait_recv()
            push(4 + j, _dev_index((*chip, cc)), sib, False)
        whole(0).wait_recv()
        for j in range(3):
            whole(4 + j).wait_recv()
        for k in range(7):
            whole(k).wait_send()
        pltpu.make_async_copy(pack_ref, pack_ref, local_sem).wait()

    return pl.pallas_call(
        body, name=name,
        in_specs=[HBM], out_specs=[HBM] * ns,
        out_shape=[jax.ShapeDtypeStruct((n, N_DEV * r, c), pack.dtype) for n, r in segs],
        scratch_shapes=[pltpu.SemaphoreType.DMA((7,)), pltpu.SemaphoreType.DMA((7,)), pltpu.SemaphoreType.DMA],
    )(pack)


HBM = pl.BlockSpec(memory_space=pltpu.HBM)
SEM = pl.BlockSpec(memory_space=pltpu.SEMAPHORE)
VMEM_WHOLE = pl.BlockSpec(memory_space=pltpu.VMEM)
EFFECT = pltpu.SideEffectType.DATAFLOW_SIDE_EFFECTING


def _hbm(a):
    return pltpu.with_memory_space_constraint(a, pltpu.HBM)


def _ag_start(pack, segs, after, name):
    rtot, c = pack.shape
    ns = len(segs)
    offs = _seg_offsets(segs)

    def body(pack_ref, *refs):
        lands = refs[:ns]
        send_sems, recv_sems = refs[ns + 1], refs[ns + 2]
        token = refs[-1]
        x, y, cc = _mesh_pos()
        my_dev = _dev_index((x, y, cc))
        targets = [(x, y, 1 - cc), (1 - x, y, cc), (x, 1 - y, cc), (1 - x, 1 - y, cc)]
        for k, to in enumerate(targets):
            for a, (n, r) in enumerate(segs):
                for m in range(n):
                    _remote(pack_ref.at[pl.ds(offs[a] + m * r, r), :],
                            lands[a].at[m, pl.ds(pl.multiple_of(my_dev * r, r), r), :],
                            send_sems.at[k], recv_sems.at[k], to).start()
        token[...] = jnp.zeros_like(token)

    land_shapes = [(n, N_DEV * r, c) for n, r in segs]
    outs = pl.pallas_call(
        body, name=name,
        in_specs=[HBM] * (1 + ns) + [UNREAD],
        out_specs=[SEM, SEM, HBM] + [HBM] * ns + [VMEM_WHOLE],
        out_shape=[pltpu.SemaphoreType.DMA((4,)), pltpu.SemaphoreType.DMA((4,)), pltpu.HBM(pack.shape, pack.dtype)]
        + [pltpu.HBM(s, pack.dtype) for s in land_shapes] + [jax.ShapeDtypeStruct((SUBLANES, LANES), F32)],
        input_output_aliases={0: 2, **{1 + i: 3 + i for i in range(ns)}},
        compiler_params=pltpu.CompilerParams(has_side_effects=EFFECT),
    )(_hbm(pack), *[_hbm(lax.empty(s, pack.dtype)) for s in land_shapes], _hbm(after))
    return outs[0], outs[1], outs[2], list(outs[3:3 + ns]), outs[-1]


def _ag_wait(send_sems, recv_sems, pack, lands, after, name):
    ns = len(lands)

    def body(pack_ref, *refs):
        send_ref, recv_ref = refs[ns], refs[ns + 1]
        me = _mesh_pos()
        for k in range(4):
            whole = _remote(pack_ref, pack_ref, send_ref.at[k], recv_ref.at[k], me)
            whole.wait_send()
            whole.wait_recv()

    outs = pl.pallas_call(
        body, name=name,
        in_specs=[HBM] * (1 + ns) + [SEM, SEM, UNREAD],
        out_specs=[HBM] * (1 + ns),
        out_shape=[pltpu.HBM(pack.shape, pack.dtype)] + [pltpu.HBM(a.shape, a.dtype) for a in lands],
        input_output_aliases={i: i for i in range(1 + ns)},
        compiler_params=pltpu.CompilerParams(has_side_effects=EFFECT),
    )(pack, *lands, send_sems, recv_sems, _hbm(after))
    return outs[0], list(outs[1:])


def _ag_finish(pack, lands, segs):
    rtot, c = pack.shape
    ns = len(segs)
    offs = _seg_offsets(segs)

    def body(pack_ref, *refs):
        outs = refs[ns:2 * ns]
        stage, send_sems, recv_sems, local_sems = refs[2 * ns:]
        x, y, cc = _mesh_pos()
        me, sib = (x, y, cc), (x, y, 1 - cc)
        chips = [(1 - x, y), (x, 1 - y), (1 - x, 1 - y)]

        def rows(a, m, dev):
            return outs[a].at[m, pl.ds(pl.multiple_of(dev * segs[a][1], segs[a][1]), segs[a][1]), :]

        for j, chip in enumerate(chips):
            dev = _dev_index((*chip, cc))
            for a, (n, r) in enumerate(segs):
                for m in range(n):
                    _remote(rows(a, m, dev), rows(a, m, dev), send_sems.at[j], recv_sems.at[j], sib).start()
        load = pltpu.make_async_copy(pack_ref, stage, local_sems.at[0])
        load.start()
        load.wait()
        my_dev = _dev_index(me)
        for a, (n, r) in enumerate(segs):
            for m in range(n):
                pltpu.make_async_copy(stage.at[pl.ds(offs[a] + m * r, r), :], rows(a, m, my_dev), local_sems.at[1]).start()
        pltpu.make_async_copy(stage, pack_ref, local_sems.at[1]).wait()
        for j in range(3):
            _remote(pack_ref, pack_ref, send_sems.at[j], recv_sems.at[j], me).wait()

    outs = pl.pallas_call(
        body, name="ag_finish",
        in_specs=[HBM] * (1 + ns), out_specs=[HBM] * ns,
        out_shape=[pltpu.HBM(a.shape, a.dtype) if r >= 128 else jax.ShapeDtypeStruct(a.shape, a.dtype)
                   for a, (_, r) in zip(lands, segs)],
        input_output_aliases={1 + i: i for i in range(ns)},
        scratch_shapes=[pltpu.VMEM((rtot, c), pack.dtype), pltpu.SemaphoreType.DMA((3,)),
                        pltpu.SemaphoreType.DMA((3,)), pltpu.SemaphoreType.DMA((2,))],
        compiler_params=_cparams(None, 16),
    )(pack, *lands)
    return list(outs)


def _rs_chips_start(pbf, after, name):
    _, rtot, c = pbf.shape

    def body(pbf_ref, land_ref, after_ref, send_sems, recv_sems, pbf_thru, land_thru, token):
        x, y, cc = _mesh_pos()
        for j, (cx, cy) in enumerate([(1 - x, y), (x, 1 - y), (1 - x, 1 - y)]):
            _remote(pbf_ref.at[j], land_ref.at[j], send_sems.at[j], recv_sems.at[j], (cx, cy, cc)).start()
        token[...] = jnp.zeros_like(token)

    return pl.pallas_call(
        body, name=name,
        in_specs=[HBM, HBM, UNREAD],
        out_specs=[SEM, SEM, HBM, HBM, VMEM_WHOLE],
        out_shape=[pltpu.SemaphoreType.DMA((3,)), pltpu.SemaphoreType.DMA((3,)), pltpu.HBM(pbf.shape, pbf.dtype),
                   pltpu.HBM((3, rtot, c), pbf.dtype), jax.ShapeDtypeStruct((SUBLANES, LANES), F32)],
        input_output_aliases={0: 2, 1: 3},
        compiler_params=pltpu.CompilerParams(has_side_effects=EFFECT),
    )(_hbm(pbf), _hbm(lax.empty((3, rtot, c), pbf.dtype)), _hbm(after))


def _rs_chips_wait(send_sems, recv_sems, pbf, land, after, name):
    def body(pbf_ref, land_ref, send_ref, recv_ref, after_ref, pbf_out, land_out):
        me = _mesh_pos()
        for j in range(3):
            cp = _remote(pbf_ref.at[0], land_ref.at[j], send_ref.at[j], recv_ref.at[j], me)
            cp.wait_send()
            cp.wait_recv()

    return pl.pallas_call(
        body, name=name,
        in_specs=[HBM, HBM, SEM, SEM, UNREAD], out_specs=[HBM, HBM],
        out_shape=[pltpu.HBM(pbf.shape, pbf.dtype), pltpu.HBM(land.shape, land.dtype)],
        input_output_aliases={0: 0, 1: 1},
        compiler_params=pltpu.CompilerParams(has_side_effects=EFFECT),
    )(pbf, land, send_sems, recv_sems, _hbm(after))[1]


def _flips():
    return [(dx, dy, dc) for dx in (0, 1) for dy in (0, 1) for dc in (0, 1) if dx or dy or dc]


def _small_gather_start(flat, name):
    r, c = flat.shape

    def body(flat_ref, land_ref, send_sems, recv_sems, flat_thru, land_thru, token):
        x, y, cc = _mesh_pos()
        mine = land_ref.at[_dev_index((x, y, cc))]
        for k, (dx, dy, dc) in enumerate(_flips()):
            to = (1 - x if dx else x, 1 - y if dy else y, 1 - cc if dc else cc)
            _remote(flat_ref, mine, send_sems.at[k], recv_sems.at[k], to).start()
        token[...] = jnp.zeros_like(token)

    return pl.pallas_call(
        body, name=name,
        in_specs=[HBM, HBM],
        out_specs=[SEM, SEM, HBM, HBM, VMEM_WHOLE],
        out_shape=[pltpu.SemaphoreType.DMA((7,)), pltpu.SemaphoreType.DMA((7,)), pltpu.HBM(flat.shape, flat.dtype),
                   pltpu.HBM((N_DEV, r, c), flat.dtype), jax.ShapeDtypeStruct((SUBLANES, LANES), F32)],
        input_output_aliases={0: 2, 1: 3},
        compiler_params=pltpu.CompilerParams(has_side_effects=EFFECT),
    )(_hbm(flat), _hbm(lax.empty((N_DEV, r, c), flat.dtype)))


def _small_gather_wait(send_sems, recv_sems, flat, land, after, name):
    def body(flat_ref, land_ref, send_ref, recv_ref, after_ref, flat_out, land_out):
        me = _mesh_pos()
        for k in range(N_DEV - 1):
            cp = _remote(flat_ref, land_ref.at[0], send_ref.at[k], recv_ref.at[k], me)
            cp.wait_send()
            cp.wait_recv()

    return pl.pallas_call(
        body, name=name,
        in_specs=[HBM, HBM, SEM, SEM, UNREAD], out_specs=[HBM, HBM],
        out_shape=[pltpu.HBM(flat.shape, flat.dtype), pltpu.HBM(land.shape, land.dtype)],
        input_output_aliases={0: 0, 1: 1},
        compiler_params=pltpu.CompilerParams(has_side_effects=EFFECT),
    )(flat, land, send_sems, recv_sems, _hbm(after))


def _sum_devices(land, own):
    _, r, c = land.shape

    def body(land_ref, own_ref, out_ref):
        me = _dev_index(_mesh_pos())
        total = None
        for d in range(N_DEV):
            other = land_ref[jnp.where(d == me, (d + 1) % N_DEV, d)]
            block = jnp.where(d == me, own_ref[...], other)
            total = block if total is None else total + block
        out_ref[...] = total

    return pl.pallas_call(
        body, name="sum_devices",
        grid=(1,),
        in_specs=[pl.BlockSpec((N_DEV, r, c), lambda i: (0, 0, 0)), pl.BlockSpec((r, c), lambda i: (0, 0))],
        out_specs=pl.BlockSpec((r, c), lambda i: (0, 0)),
        out_shape=jax.ShapeDtypeStruct((r, c), F32),
        compiler_params=_cparams(("arbitrary",)),
    )(land, own)


def _rs_sibling_start(fulls, segs, name):
    ns = len(segs)
    offs = _seg_offsets(segs)
    rtot = sum(n * r for n, r in segs)
    c = fulls[0].shape[-1]
    dt = fulls[0].dtype

    def body(*refs):
        srcs = refs[:ns]
        land_ref, send_sem, recv_sem = refs[ns], refs[ns + 1], refs[ns + 2]
        token = refs[-1]
        x, y, cc = _mesh_pos()
        for k in range(4):
            for a, (n, r) in enumerate(segs):
                for m in range(n):
                    theirs = srcs[a].at[m, pl.ds(pl.multiple_of((2 * k + 1 - cc) * r, r), r), :]
                    _remote(theirs, land_ref.at[k, pl.ds(offs[a] + m * r, r), :], send_sem, recv_sem,
                            (x, y, 1 - cc)).start()
        token[...] = jnp.zeros_like(token)

    outs = pl.pallas_call(
        body, name=name,
        in_specs=[HBM] * (ns + 1),
        out_specs=[SEM, SEM] + [HBM] * (ns + 1) + [VMEM_WHOLE],
        out_shape=[pltpu.SemaphoreType.DMA(()), pltpu.SemaphoreType.DMA(())]
        + [pltpu.HBM(a.shape, a.dtype) for a in fulls] + [pltpu.HBM((4, rtot, c), dt),
                                                           jax.ShapeDtypeStruct((SUBLANES, LANES), F32)],
        input_output_aliases={i: 2 + i for i in range(ns + 1)},
        compiler_params=pltpu.CompilerParams(has_side_effects=EFFECT),
    )(*[_hbm(a) for a in fulls], _hbm(lax.empty((4, rtot, c), dt)))
    return outs[0], outs[1], list(outs[2:2 + ns]), outs[2 + ns], outs[-1]


def _rs_sibling_wait(send_sem, recv_sem, fulls, land, after, name):
    ns = len(fulls)

    def body(*refs):
        land_ref, send_ref, recv_ref = refs[ns], refs[ns + 1], refs[ns + 2]
        whole = _remote(land_ref, land_ref, send_ref, recv_ref, _mesh_pos())
        whole.wait_send()
        whole.wait_recv()

    outs = pl.pallas_call(
        body, name=name,
        in_specs=[HBM] * (ns + 1) + [SEM, SEM, UNREAD], out_specs=[HBM] * (ns + 1),
        out_shape=[pltpu.HBM(a.shape, a.dtype) for a in fulls] + [pltpu.HBM(land.shape, land.dtype)],
        input_output_aliases={i: i for i in range(ns + 1)},
        compiler_params=pltpu.CompilerParams(has_side_effects=EFFECT),
    )(*fulls, land, send_sem, recv_sem, _hbm(after))
    return list(outs[:ns]), outs[ns]


def _tp(w):
    return jnp.swapaxes(w, -1, -2)


def _s5_prepare(a_re, a_im, log_dt, b_re, b_im, c_re, c_im):
    a = jnp.stack([a_re, a_im], axis=1)
    ldt = jnp.broadcast_to(log_dt[:, :, None], (DEPTH, SSM_GROUPS, SSM_STATE))
    a_row = a.reshape(DEPTH, 2, 1, N_STATE)
    ldt_row = ldt.reshape(DEPTH, 1, N_STATE)
    a_rep = jnp.repeat(a, SSM_GROUP, axis=2)
    ldt_rep = jnp.repeat(ldt, SSM_GROUP, axis=1)
    bt = jnp.stack([_tp(b_re), _tp(b_im)], axis=1).reshape(DEPTH, 2, SSM_W, SSM_STATE)
    ct = jnp.stack([c_re, c_im], axis=1).reshape(DEPTH, 2, SSM_W, SSM_STATE)
    tile_e = jnp.tile(jnp.eye(SSM_STATE, dtype=BF16), (1, SSM_GROUPS))
    mask = jnp.repeat(jnp.repeat(jnp.eye(SSM_GROUPS, dtype=BF16), SSM_GROUP, axis=0), SSM_STATE, axis=1)
    out = []
    for l in range(DEPTH):
        tabs = _s5_disc(a_row[l], ldt_row[l], a_rep[l], ldt_rep[l], bt[l], ct[l], tile_e, mask)
        out.append(((a[l], ldt[l], a_rep[l], ldt_rep[l], bt[l], mask), *tabs))
    return out


def _layer_fwd(h, p_l, small, big, arrive=None):
    saved = {'h0': h}
    if arrive is not None:
        arrive(0, h)
    h, saved['gu1'] = _ffn_fwd(h, small['ffn1_norm'], big['ff1'])
    saved['h1'] = h
    if arrive is not None:
        arrive(1, h)
    z = _inproj_fwd(h, small['mix_norm'], big['wint'])
    ya, ys, hs = _s5conv_fwd(z, small['conv_w'], small['conv_b'], small['bbmat'], small['ccmat'], small['dvec'],
                             small['ltab'])
    saved.update(z=z, ya=ya, ys=ys, hs=hs)
    h = _mix_out_fwd(h, ya, ys, big['glu'], small['glu_b'], small['conv_out_norm'], small['ssm_out_norm'], big['wout'])
    saved['h2'] = h
    if arrive is not None:
        arrive(2, h)
    h, saved['gu2'] = _ffn_fwd(h, small['ffn2_norm'], big['ff2'])
    saved['h3'] = h
    h = _ple_fwd(h, small['ple_norm'], p_l, big['plg'], big['plpt'])
    return h, saved


def _ffn_bwd(h_in, g, dh, gu, w3):
    dh_in, dga, ud, dg = _ffn_bwd_act(h_in, g, dh, gu, w3)
    return dh_in, _matmul_tn(dga, ud, FF_BLOCK, BF16, "ffn_wgrad"), dg


def _layer_bwd_top(dh, p_l, small, big, saved):
    gs = {}
    dh, u, dq, dpp, pb, gs['ple_norm'] = _ple_bwd(saved['h3'], small['ple_norm'], p_l, dh, big['plg'], big['plpt'])
    d_plg = _matmul_tn(u, dq, 256, BF16, "ple_gate_wgrad")
    d_plpt = _matmul_tn(dpp, pb, 256, BF16, "ple_proj_wgrad", to_kernel=False)
    dh, d_ff2, gs['ffn2_norm'] = _ffn_bwd(saved['h2'], small['ffn2_norm'], dh, saved['gu2'], big['ff2'])
    return dh, (gs, d_plg, d_plpt, d_ff2)


def _layer_bwd_rest(dh, top, small, big, saved):
    gs, d_plg, d_plpt, d_ff2 = top
    dya, dys, ycat, dhb, zg, dq, part = _mix_out_bwd(dh, saved['ya'], saved['ys'], big['glu'], small['glu_b'],
                                                     small['conv_out_norm'], small['ssm_out_norm'], big['wout'])
    d_wout = _matmul_tn(ycat, dhb, 256, BF16, "w_out_wgrad")
    d_glu = _matmul_tn(zg, dq, 256, BF16, "glu_wgrad", to_kernel=False)
    dz, gadj, us, dyb, dl, dcw = _s5conv_bwd(saved['z'], saved['hs'], dya, dys, small['conv_w'], small['conv_b'],
                                             small['bbmat'], small['ccmat'], small['dvec'], small['ltab_rev'])
    d_bb = _block_wgrad(us, gadj, "s5_b_wgrad")
    d_cc = _block_wgrad(dyb, saved['hs'][None], "s5_c_wgrad")
    dh, u, gs['mix_norm'] = _inproj_bwd(saved['h1'], small['mix_norm'], dh, dz, big['wint'])
    d_wint = _matmul_tn(dz[None], u, 256, BF16, "w_in_wgrad")
    dh, d_ff1, gs['ffn1_norm'] = _ffn_bwd(saved['h0'], small['ffn1_norm'], dh, saved['gu1'], big['ff1'])

    dlb = dl[0].reshape(2, SSM_GROUPS, SSM_STATE)
    fold = jnp.tile(jnp.eye(SSM_STATE, dtype=BF16), (SSM_GROUPS, 1))
    da, dldt, dbt, dct = _s5_disc_bwd(*small['disc_in'], dlb, d_bb, d_cc, fold)
    gs['ssm_A_re'], gs['ssm_A_im'] = da[0], da[1]
    gs['ssm_log_dt'] = dldt[:, 0]
    ghp = (SSM_GROUPS, SSM_GROUP, SSM_STATE)
    gs['ssm_B_re'], gs['ssm_B_im'] = dbt[0].reshape(ghp), dbt[1].reshape(ghp)
    gs['ssm_C_re'], gs['ssm_C_im'] = dct[0].reshape(ghp), dct[1].reshape(ghp)
    gs['conv_w'] = dcw[0:3]
    gs['conv_b'] = dcw[3]
    gs['ssm_D'] = dcw[4].reshape(SSM_GROUPS, SSM_GROUP)
    gs['conv_out_norm'], gs['ssm_out_norm'], gs['glu_b'] = part[0], part[1], part[2]
    for n in ('ple_norm', 'ffn2_norm', 'mix_norm', 'ffn1_norm'):
        gs[n] = gs[n][0]
    fulls = [d_ff1, d_ff2, d_wint, d_wout, d_plg,
             d_plpt.reshape(1, D_MODEL * PLE_DIM // D_MODEL, D_MODEL), d_glu.reshape(1, SSM_W * SSM_W // D_MODEL, D_MODEL)]
    return dh, fulls, gs


VIEW_T = ('ffn1_w_gate', 'ffn1_w_up', 'ffn2_w_gate', 'ffn2_w_up', 'ssm_B_re', 'ssm_B_im')


def _view(name, a):
    return _tp(a) if name in VIEW_T else a


SEG_NAMES = ('ff1', 'ff2', 'wint', 'wout', 'plg', 'plpt', 'glu')
FIRST_LAYER_GROUPS = ((0,), (2, 3, 6), (1, 4, 5))


def _layer_pack(W, l, segments=range(len(SEGS))):
    pieces = {
        0: lambda: [_tp(W['ffn1_w_gate'][l]), _tp(W['ffn1_w_up'][l]), W['ffn1_w_down'][l]],
        1: lambda: [_tp(W['ffn2_w_gate'][l]), _tp(W['ffn2_w_up'][l]), W['ffn2_w_down'][l]],
        2: lambda: [_tp(W['w_in'][l])],
        3: lambda: [W['w_out'][l]],
        4: lambda: [W['ple_w_gate'][l]],
        5: lambda: [_tp(W['ple_w_proj'][l]).reshape(-1, D_MODEL)],
        6: lambda: [W['glu_w'][l].reshape(-1, D_MODEL)],
    }
    return jnp.concatenate([a for s in segments for a in pieces[s]()], axis=0).astype(BF16)


def _as_big(named):
    shape = dict(plpt=(D_MODEL, PLE_DIM), glu=(SSM_W, SSM_W))
    return {n: (a.reshape(shape[n]) if n in shape else a) for n, a in named.items()}


def _pad_rows(flat, mult, width=LANES):
    per = mult * width
    n = flat.shape[0]
    tot = -(-n // per) * per
    return jnp.pad(flat, (0, tot - n)).reshape(tot // width, width)


def _adamw_any(w, g, m, v):
    shp = w.shape
    two = (lambda t: t.reshape(-1, shp[-1]))
    d, nm, nv = _adamw(two(w), two(g), two(m), two(v))
    return d.reshape(shp), nm.reshape(shp), nv.reshape(shp)


def kernel(x, p, ffn1_norm, ffn1_w_gate, ffn1_w_up, ffn1_w_down, mix_norm, w_in, conv_w, conv_b, ssm_A_re, ssm_A_im, ssm_B_re, ssm_B_im, ssm_C_re, ssm_C_im, ssm_D, ssm_log_dt, glu_w, glu_b, conv_out_norm, ssm_out_norm, w_out, ffn2_norm, ffn2_w_gate, ffn2_w_up, ffn2_w_down, ple_norm, ple_w_gate, ple_w_proj, final_norm, loss_target, m_ffn1_norm, m_ffn1_w_gate, m_ffn1_w_up, m_ffn1_w_down, m_mix_norm, m_w_in, m_conv_w, m_conv_b, m_ssm_A_re, m_ssm_A_im, m_ssm_B_re, m_ssm_B_im, m_ssm_C_re, m_ssm_C_im, m_ssm_D, m_ssm_log_dt, m_glu_w, m_glu_b, m_conv_out_norm, m_ssm_out_norm, m_w_out, m_ffn2_norm, m_ffn2_w_gate, m_ffn2_w_up, m_ffn2_w_down, m_ple_norm, m_ple_w_gate, m_ple_w_proj, m_final_norm, v_ffn1_norm, v_ffn1_w_gate, v_ffn1_w_up, v_ffn1_w_down, v_mix_norm, v_w_in, v_conv_w, v_conv_b, v_ssm_A_re, v_ssm_A_im, v_ssm_B_re, v_ssm_B_im, v_ssm_C_re, v_ssm_C_im, v_ssm_D, v_ssm_log_dt, v_glu_w, v_glu_b, v_conv_out_norm, v_ssm_out_norm, v_w_out, v_ffn2_norm, v_ffn2_w_gate, v_ffn2_w_up, v_ffn2_w_down, v_ple_norm, v_ple_w_gate, v_ple_w_proj, v_final_norm):
    given = dict(locals())
    W = {n: given[n] for n in W_NAMES}
    M = {n: given['m_' + n] for n in W_NAMES}
    V = {n: given['v_' + n] for n in W_NAMES}
    Wv, Mv, Vv = [{n: _view(n, d[n]) for n in W_NAMES} for d in (W, M, V)]
    my_dev = _dev_index(_mesh_pos())

    conv_shard = _pad_rows(W['conv_w'].reshape(-1), SUBLANES)
    conv_all = _allgather(conv_shard, ((1, SUBLANES),), "ag_conv_w")[0]
    conv_full = conv_all.reshape(N_DEV, -1)[:, :DEPTH * 3 * (CONV_W // N_DEV)]
    conv_full = conv_full.reshape(N_DEV, DEPTH, 3, CONV_W // N_DEV).transpose(1, 2, 0, 3).reshape(DEPTH, 3, CONV_W)
    first, after = [], conv_all
    for gi, segments in enumerate(FIRST_LAYER_GROUPS):
        first.append(_ag_start(_layer_pack(W, 0, segments), tuple(SEGS[s] for s in segments), after,
                               "ag_start_0%s" % "abc"[gi]))
        after = first[-1][4]
    s5 = _s5_prepare(*[W[n] + after[0, 0] for n in ('ssm_A_re', 'ssm_A_im', 'ssm_log_dt')],
                     *[W[n] for n in ('ssm_B_re', 'ssm_B_im', 'ssm_C_re', 'ssm_C_im')])
    packs = [None] + [_layer_pack(W, l) for l in range(1, DEPTH)]
    prepared = conv_full[0, 0:1, 0:1] + s5[DEPTH - 1][1][0:1, 0:1] + packs[DEPTH - 1][0:1, 0:1].astype(F32)

    smalls, saves, bigs = [], [], []
    h = x[0]

    flight = None

    def gathered(handles, segments, after, name, next_layer=None, gate=None):
        nonlocal flight
        send_sems, recv_sems, pack_thru, lands, _ = handles
        pack_thru, lands = _ag_wait(send_sems, recv_sems, pack_thru, lands, after, "ag_wait_" + name)
        if next_layer is not None:
            flight = _ag_start(packs[next_layer], SEGS, pack_thru, "ag_start_%d" % next_layer)
            gate[0][gate[1]] = gate[0][gate[1]] + flight[4][0:1, 0:1]
        outs = _ag_finish(pack_thru, lands, tuple(SEGS[s] for s in segments))
        return _as_big({SEG_NAMES[s]: a for s, a in zip(segments, outs)})

    for l in range(DEPTH):
        small = {n: W[n][l][None] for n in ('ffn1_norm', 'mix_norm', 'conv_b', 'glu_b', 'conv_out_norm',
                                            'ssm_out_norm', 'ffn2_norm', 'ple_norm')}
        small['conv_w'] = conv_full[l]
        small['dvec'] = W['ssm_D'][l].reshape(1, SSM_W)
        small['disc_in'], small['ltab'], small['ltab_rev'], small['bbmat'], small['ccmat'] = s5[l]
        big = {}
        bigs.append(big)
        if l == 0:
            def arrive(stage, h_now, big=big, small=small):
                big.update(gathered(first[stage], FIRST_LAYER_GROUPS[stage], prepared if stage == 0 else h_now,
                                    "0%s" % "abc"[stage], *((1, (small, 'ffn2_norm')) if stage == 2 else ())))
            h, saved = _layer_fwd(h, p[l, 0], small, big, arrive)
        else:
            nxt = (l + 1, (small, 'ffn1_norm')) if l + 1 < DEPTH else ()
            big.update(gathered(flight, range(len(SEGS)), h, "%d" % l, *nxt))
            h, saved = _layer_fwd(h, p[l, 0], small, big)
        smalls.append(small)
        saves.append(saved)
    loss_tile, dh, d_final = _final_loss(h, W['final_norm'][None], loss_target[0])
    loss = lax.psum(loss_tile[0, 0], ("x", "y", "c"))

    layer_gs = [None] * DEPTH
    shard_grads = None
    sib, ici = None, None

    def finish_sibling(after_sib, after_ici):
        nonlocal sib, ici
        up, (send_sem, recv_sem, fulls_thru, land, _) = sib
        fulls_thru, got = _rs_sibling_wait(send_sem, recv_sem, fulls_thru, land, after_sib, "sib_wait_%d" % up)
        pbf = _pair_sum(fulls_thru, got, SEGS)
        done = finish_chips(after_ici)
        ici = (up, _rs_chips_start(pbf, after_ici if done is None else done, "rs_start_%d" % up), fulls_thru, got)
        sib = None

    def finish_chips(after):
        nonlocal ici, shard_grads
        if ici is None:
            return None
        up, (send_sems, recv_sems, pbf_thru, land, _), fulls_up, got_up = ici
        got3 = _rs_chips_wait(send_sems, recv_sems, pbf_thru, land, after, "rs_wait_%d" % up)
        shard_grads = _chip_sum(fulls_up, got_up, got3, SEGS, up, shard_grads)
        ici = None
        return shard_grads

    layer_names = [n for n in SMALL_NAMES if n != 'final_norm']
    small_flights = [None] * DEPTH
    for l in reversed(range(DEPTH)):
        small = dict(smalls[l])
        if sib is not None:
            small['ple_norm'] = small['ple_norm'] + sib[1][4][0:1, 0:1] + small_flights[l + 1][4][0:1, 0:1]
        dh, top = _layer_bwd_top(dh, p[l, 0], small, bigs[l], saves[l])
        if sib is not None:
            finish_sibling(dh, dh)
            small['glu_b'] = small['glu_b'] + ici[1][4][0:1, 0:1]
        dh, fulls, layer_gs[l] = _layer_bwd_rest(dh, top, small, bigs[l], saves[l])
        sib = (l, _rs_sibling_start(fulls, SEGS, "sib_start_%d" % l))
        last_slot = d_final[0] if l == DEPTH - 1 else jnp.zeros((D_MODEL,), F32)
        flat = jnp.concatenate([layer_gs[l][n].reshape(-1) for n in layer_names + ['conv_w']] + [last_slot])
        small_flights[l] = _small_gather_start(_pad_rows(flat, SUBLANES, D_MODEL), "small_start_%d" % l)
    grad_x = dh[None]
    finish_sibling(small_flights[0][4], small_flights[0][4])

    reduced = []
    for l in range(DEPTH):
        send_sems, recv_sems, flat_thru, land, _ = small_flights[l]
        flat_thru, land = _small_gather_wait(send_sems, recv_sems, flat_thru, land, ici[1][4], "small_wait_%d" % l)
        reduced.append(_sum_devices(land, flat_thru).reshape(-1))
    reduced = jnp.stack(reduced)
    G = {}
    o = 0
    for n in layer_names + ['conv_w']:
        size = (W[n].size if n != 'conv_w' else DEPTH * 3 * CONV_W) // DEPTH
        shape = Wv[n].shape if n != 'conv_w' else (DEPTH, 3, CONV_W)
        G[n] = reduced[:, o:o + size].reshape(shape)
        o += size
    G['final_norm'] = reduced[DEPTH - 1, o:o + D_MODEL]
    G['conv_w'] = lax.dynamic_slice_in_dim(G['conv_w'], my_dev * (CONV_W // N_DEV), CONV_W // N_DEV, axis=2)

    delta, new_m, new_v = {}, {}, {}
    for n in SMALL_NAMES + ['conv_w']:
        two = (lambda t: t.reshape(1, -1) if t.ndim == 1 else t)
        delta[n], new_m[n], new_v[n] = [t.reshape(Wv[n].shape) for t in
                                        _adamw_any(two(Wv[n]), two(G[n]), two(Mv[n]), two(Vv[n]))]

    offs = _seg_offsets(SEGS)
    r = SEGS[0][1]
    packed_rows = {'w_out': offs[3], 'ple_w_gate': offs[4]}
    for a, f in ((0, 'ffn1'), (1, 'ffn2')):
        packed_rows.update({f + '_w_gate': offs[a], f + '_w_up': offs[a] + r, f + '_w_down': offs[a] + 2 * r})

    def relaid(sg):
        nl = sg.shape[0]
        return {'w_in': _tp(sg[:, offs[2]:offs[2] + SEGS[2][1]]),
                'ple_w_proj': _tp(sg[:, offs[5]:offs[5] + SEGS[5][1]].reshape(nl, D_MODEL // N_DEV, PLE_DIM)),
                'glu_w': sg[:, offs[6]:offs[6] + SEGS[6][1]].reshape(nl, SSM_W // N_DEV, SSM_W)}

    groups = {}
    for n in list(packed_rows) + ['w_in', 'ple_w_proj', 'glu_w']:
        groups.setdefault(Wv[n].shape, []).append(n)

    def update(first, nl, prev):
        other = relaid(shard_grads[first:first + nl])
        sets = lambda ns: [(Wv[n], Mv[n], Vv[n], shard_grads, packed_rows[n]) if n in packed_rows
                           else (Wv[n], Mv[n], Vv[n], other[n], None) for n in ns]
        return {shape: _adamw_layers(sets(ns), first, nl, None if prev is None else prev[shape])
                for shape, ns in groups.items()}

    part = update(1, DEPTH - 1, None)
    finish_chips(sum(four[3][1, 0:1, 0:1] for fours in part.values() for four in fours))
    for shape, fours in update(0, 1, part).items():
        for n, four in zip(groups[shape], fours):
            G[n], delta[n], new_m[n], new_v[n] = four

    outs = [[_view(n, d[n]) for n in W_NAMES] for d in (G, delta, new_m, new_v)]
    return (loss, grad_x, *outs[0], *outs[1], *outs[2], *outs[3])
```

```python
import math

import jax
import jax.numpy as jnp
from jax import lax
from jax.experimental import pallas as pl
from jax.experimental.pallas import tpu as pltpu

F32 = jnp.float32
BF16 = jnp.bfloat16

N_DEV = 8
DEPTH = 4
SEQ = 2048
D_MODEL = 1024
D_FF = 2816
CONV_W = 512
SSM_W = 512
SSM_GROUPS = 32
SSM_GROUP = 16
SSM_STATE = 64
N_STATE = SSM_GROUPS * SSM_STATE
IN_COLS = 2048
PLE_DIM = 256
EPS = 1e-6

ADAM_LR = 0.001
ADAM_B1 = 0.9
ADAM_B2 = 0.999
ADAM_EPS = 1e-08
ADAM_WD = 0.01
ADAM_STEP = 10

FF_BLOCK = 256
N_FF_BLOCKS = D_FF // FF_BLOCK
TOK_TILE_FFN_FWD = 2048
TOK_TILE_FFN_BWD = 1024
TOK_TILE = 512
CHUNK = 256
N_CHUNKS = SEQ // CHUNK
LANE_GROUP = 512
SUBLANES = 8
LANES = 128
MIB = 1024 * 1024

W_NAMES = ['ffn1_norm', 'ffn1_w_gate', 'ffn1_w_up', 'ffn1_w_down', 'mix_norm', 'w_in', 'conv_w', 'conv_b',
           'ssm_A_re', 'ssm_A_im', 'ssm_B_re', 'ssm_B_im', 'ssm_C_re', 'ssm_C_im', 'ssm_D', 'ssm_log_dt',
           'glu_w', 'glu_b', 'conv_out_norm', 'ssm_out_norm', 'w_out', 'ffn2_norm', 'ffn2_w_gate', 'ffn2_w_up',
           'ffn2_w_down', 'ple_norm', 'ple_w_gate', 'ple_w_proj', 'final_norm']
SMALL_NAMES = ['ffn1_norm', 'mix_norm', 'conv_b', 'ssm_A_re', 'ssm_A_im', 'ssm_B_re', 'ssm_B_im', 'ssm_C_re',
               'ssm_C_im', 'ssm_D', 'ssm_log_dt', 'glu_b', 'conv_out_norm', 'ssm_out_norm', 'ffn2_norm',
               'ple_norm', 'final_norm']

SEGS = ((3, 352), (3, 352), (1, 256), (1, 128), (1, 128), (1, 32), (1, 32))
PACK_ROWS = sum(n * r for n, r in SEGS)

MESH = pl.DeviceIdType.MESH
UNREAD = pl.BlockSpec(memory_space=pltpu.HBM)


def _in_hbm(*arrays):
    return [pltpu.with_memory_space_constraint(a, pltpu.HBM) for a in arrays]


def _out_hbm(outs, which):
    if not isinstance(outs, (list, tuple)):
        return pltpu.with_memory_space_constraint(outs, pltpu.HBM) if which else outs
    return [pltpu.with_memory_space_constraint(a, pltpu.HBM) if i in which else a for i, a in enumerate(outs)]


def _cparams(sem=None, vmem_mib=48, **kw):
    return pltpu.CompilerParams(dimension_semantics=sem, vmem_limit_bytes=vmem_mib * MIB, **kw)


def _dot(a, b):
    return jnp.dot(a, b, preferred_element_type=F32)


def _dot_nt(a, b):
    return lax.dot_general(a, b, (((1,), (1,)), ((), ())), preferred_element_type=F32)


def _dot_tn(a, b):
    return lax.dot_general(a, b, (((0,), (0,)), ((), ())), preferred_element_type=F32)


def _rms_stats(x):
    r = lax.rsqrt(jnp.mean(x * x, axis=-1, keepdims=True) + EPS)
    return x * r, r


def _rms_bwd(dy, xh, r, g):
    dxh = dy * g
    dx = r * (dxh - xh * jnp.mean(dxh * xh, axis=-1, keepdims=True))
    dg = jnp.sum(dy * xh, axis=0, keepdims=True)
    return dx, dg


def _sigmoid(x):
    return 0.5 * jnp.tanh(0.5 * x) + 0.5


_GELU_C = math.sqrt(2.0 / math.pi)


def _gelu(x):
    t = jnp.tanh(_GELU_C * (x + 0.044715 * x * x * x))
    return 0.5 * x * (1.0 + t), t


def _gelu_grad(x, t):
    return 0.5 * (1.0 + t) + 0.5 * x * (1.0 - t * t) * _GELU_C * (1.0 + 3.0 * 0.044715 * x * x)


def _accumulate(ref, first, value):
    @pl.when(first)
    def _():
        ref[...] = value

    @pl.when(jnp.logical_not(first))
    def _():
        ref[...] += value


def _ffn_fwd(h, g, w3):
    tm = TOK_TILE_FFN_FWD
    last = N_FF_BLOCKS - 1

    def body(h_ref, g_ref, wgu_ref, wd_ref, wd_last_ref, out_ref, gu_ref, u_ref, a_ref):
        k = pl.program_id(1)

        @pl.when(k == 0)
        def _():
            x = h_ref[...]
            xh, _ = _rms_stats(x)
            u_ref[...] = (xh * g_ref[...]).astype(BF16)
            out_ref[...] = x
            a_ref[1] = jnp.zeros((tm, FF_BLOCK), BF16)

        out_ref[...] += 0.5 * _dot(a_ref[(k + 1) % 2], wd_ref[0])
        gu = _dot_nt(u_ref[...], wgu_ref[...].reshape(2 * FF_BLOCK, D_MODEL))
        gate, up = gu[:, :FF_BLOCK], gu[:, FF_BLOCK:]
        a_ref[k % 2] = (gate * _sigmoid(gate) * up).astype(BF16)
        gu_ref[0] = gate.astype(BF16)
        gu_ref[1] = up.astype(BF16)

        @pl.when(k == last)
        def _():
            out_ref[...] += 0.5 * _dot(a_ref[last % 2], wd_last_ref[0])

    return _out_hbm(pl.pallas_call(
        body, name="ffn_fwd",
        grid=(SEQ // tm, N_FF_BLOCKS),
        in_specs=[pl.BlockSpec((tm, D_MODEL), lambda m, k: (m, 0), pipeline_mode=pl.Buffered(1)),
                  pl.BlockSpec((1, D_MODEL), lambda m, k: (0, 0)),
                  pl.BlockSpec((2, FF_BLOCK, D_MODEL), lambda m, k: (0, k, 0)),
                  pl.BlockSpec((1, FF_BLOCK, D_MODEL), lambda m, k: (2, jnp.maximum(k - 1, 0), 0)),
                  pl.BlockSpec((1, FF_BLOCK, D_MODEL), lambda m, k: (2, last, 0), pipeline_mode=pl.Buffered(1))],
        out_specs=[pl.BlockSpec((tm, D_MODEL), lambda m, k: (m, 0)),
                   pl.BlockSpec((2, tm, FF_BLOCK), lambda m, k: (0, m, k))],
        out_shape=[jax.ShapeDtypeStruct((SEQ, D_MODEL), F32),
                   pltpu.HBM((2, SEQ, D_FF), BF16)],
        scratch_shapes=[pltpu.VMEM((tm, D_MODEL), BF16), pltpu.VMEM((2, tm, FF_BLOCK), BF16)],
        compiler_params=_cparams(("parallel", "arbitrary"), 56),
    )(*_in_hbm(h, g, w3, w3, w3)), (1,))


def _ffn_bwd_act(h, g, dout, gu, w3):
    tm = TOK_TILE_FFN_BWD
    last = N_FF_BLOCKS - 1

    def body(h_ref, g_ref, d_ref, gu_ref, wd_ref, wgu_ref, wgu_last_ref, dh_ref, dga_ref, ud_ref, dg_ref,
             acc_ref, dgu_ref):
        m = pl.program_id(0)
        k = pl.program_id(1)

        @pl.when(k == 0)
        def _():
            xh, _ = _rms_stats(h_ref[...])
            ud_ref[0] = (xh * g_ref[...]).astype(BF16)
            ud_ref[1] = (0.5 * d_ref[...]).astype(BF16)
            acc_ref[...] = jnp.zeros_like(acc_ref)
            dgu_ref[1] = jnp.zeros((tm, 2 * FF_BLOCK), BF16)

        acc_ref[...] += _dot(dgu_ref[(k + 1) % 2], wgu_ref[...].reshape(2 * FF_BLOCK, D_MODEL))
        gate = gu_ref[0].astype(F32)
        up = gu_ref[1].astype(F32)
        sg = _sigmoid(gate)
        silu = gate * sg
        da = _dot_nt(ud_ref[1], wd_ref[0])
        dgate = (da * up * (sg + silu * (1.0 - sg))).astype(BF16)
        dup = (da * silu).astype(BF16)
        dga_ref[0] = dgate
        dga_ref[1] = dup
        dga_ref[2] = (silu * up).astype(BF16)
        dgu_ref[k % 2, :, 0:FF_BLOCK] = dgate
        dgu_ref[k % 2, :, FF_BLOCK:2 * FF_BLOCK] = dup

        @pl.when(k == last)
        def _():
            du = acc_ref[...] + _dot(dgu_ref[last % 2], wgu_last_ref[...].reshape(2 * FF_BLOCK, D_MODEL))
            xh, r = _rms_stats(h_ref[...])
            dx, dg = _rms_bwd(du, xh, r, g_ref[...])
            dh_ref[...] = d_ref[...] + dx
            _accumulate(dg_ref, m == 0, dg)

    return _out_hbm(pl.pallas_call(
        body, name="ffn_bwd_act",
        grid=(SEQ // tm, N_FF_BLOCKS),
        in_specs=[pl.BlockSpec((tm, D_MODEL), lambda m, k: (m, 0), pipeline_mode=pl.Buffered(1)),
                  pl.BlockSpec((1, D_MODEL), lambda m, k: (0, 0)),
                  pl.BlockSpec((tm, D_MODEL), lambda m, k: (m, 0), pipeline_mode=pl.Buffered(1)),
                  pl.BlockSpec((2, tm, FF_BLOCK), lambda m, k: (0, m, k)),
                  pl.BlockSpec((1, FF_BLOCK, D_MODEL), lambda m, k: (2, k, 0)),
                  pl.BlockSpec((2, FF_BLOCK, D_MODEL), lambda m, k: (0, jnp.maximum(k - 1, 0), 0)),
                  pl.BlockSpec((2, FF_BLOCK, D_MODEL), lambda m, k: (0, last, 0), pipeline_mode=pl.Buffered(1))],
        out_specs=[pl.BlockSpec((tm, D_MODEL), lambda m, k: (m, 0)),
                   pl.BlockSpec((3, tm, FF_BLOCK), lambda m, k: (0, m, k)),
                   pl.BlockSpec((2, tm, D_MODEL), lambda m, k: (0, m, 0)),
                   pl.BlockSpec((1, D_MODEL), lambda m, k: (0, 0))],
        out_shape=[jax.ShapeDtypeStruct((SEQ, D_MODEL), F32),
                   pltpu.HBM((3, SEQ, D_FF), BF16),
                   pltpu.HBM((2, SEQ, D_MODEL), BF16),
                   jax.ShapeDtypeStruct((1, D_MODEL), F32)],
        scratch_shapes=[pltpu.VMEM((tm, D_MODEL), F32), pltpu.VMEM((2, tm, 2 * FF_BLOCK), BF16)],
        compiler_params=_cparams(("arbitrary", "arbitrary"), 56),
    )(*_in_hbm(h, g, dout, gu, w3, w3, w3)), (1, 2))


def _matmul_tn(a, b, bm, out_dtype, name, bn=None, to_kernel=True):
    na, t, m = a.shape
    nb, _, n = b.shape
    bn = n if bn is None else bn

    def body(a_ref, b_ref, o_ref):
        o_ref[0] = _dot_tn(a_ref[0], b_ref[0]).astype(out_dtype)

    return _out_hbm(pl.pallas_call(
        body, name=name,
        grid=(na, m // bm, n // bn),
        in_specs=[pl.BlockSpec((1, t, bm), lambda i, k, j: (i, 0, k)),
                  pl.BlockSpec((1, t, bn), lambda i, k, j: (jnp.maximum(i - (na - nb), 0), 0, j))],
        out_specs=pl.BlockSpec((1, bm, bn), lambda i, k, j: (i, k, j)),
        out_shape=pltpu.HBM((na, m, n), out_dtype) if to_kernel else jax.ShapeDtypeStruct((na, m, n), out_dtype),
        compiler_params=_cparams(("arbitrary", "parallel", "parallel")),
    )(*_in_hbm(a, b)), to_kernel)


def _inproj_fwd(h, g, wint):
    tm = TOK_TILE

    def body(h_ref, g_ref, w_ref, z_ref):
        xh, _ = _rms_stats(h_ref[...])
        z_ref[...] = _dot_nt((xh * g_ref[...]).astype(BF16), w_ref[...])

    return pl.pallas_call(
        body, name="inproj_fwd",
        grid=(SEQ // tm,),
        in_specs=[pl.BlockSpec((tm, D_MODEL), lambda m: (m, 0)),
                  pl.BlockSpec((1, D_MODEL), lambda m: (0, 0)),
                  pl.BlockSpec((None, IN_COLS, D_MODEL), lambda m: (0, 0, 0))],
        out_specs=pl.BlockSpec((tm, IN_COLS), lambda m: (m, 0)),
        out_shape=jax.ShapeDtypeStruct((SEQ, IN_COLS), F32),
        compiler_params=_cparams(("parallel",)),
    )(*_in_hbm(h, g, wint))


def _inproj_bwd(h, g, dh, dz, wint):
    tm = TOK_TILE

    def body(h_ref, g_ref, dh_ref, dz_ref, w_ref, o_ref, u_ref, dg_ref):
        xh, r = _rms_stats(h_ref[...])
        u_ref[0] = (xh * g_ref[...]).astype(BF16)
        dx, dg = _rms_bwd(_dot(dz_ref[...], w_ref[...]), xh, r, g_ref[...])
        o_ref[...] = dh_ref[...] + dx
        _accumulate(dg_ref, pl.program_id(0) == 0, dg)

    return _out_hbm(pl.pallas_call(
        body, name="inproj_bwd",
        grid=(SEQ // tm,),
        in_specs=[pl.BlockSpec((tm, D_MODEL), lambda m: (m, 0)),
                  pl.BlockSpec((1, D_MODEL), lambda m: (0, 0)),
                  pl.BlockSpec((tm, D_MODEL), lambda m: (m, 0)),
                  pl.BlockSpec((tm, IN_COLS), lambda m: (m, 0)),
                  pl.BlockSpec((None, IN_COLS, D_MODEL), lambda m: (0, 0, 0))],
        out_specs=[pl.BlockSpec((tm, D_MODEL), lambda m: (m, 0)),
                   pl.BlockSpec((1, tm, D_MODEL), lambda m: (0, m, 0)),
                   pl.BlockSpec((1, D_MODEL), lambda m: (0, 0))],
        out_shape=[jax.ShapeDtypeStruct((SEQ, D_MODEL), F32),
                   pltpu.HBM((1, SEQ, D_MODEL), BF16),
                   jax.ShapeDtypeStruct((1, D_MODEL), F32)],
        compiler_params=_cparams(("arbitrary",)),
    )(*_in_hbm(h, g, dh, dz, wint)), (1,))


def _row_ids(n, w):
    return lax.broadcasted_iota(jnp.int32, (n, w), 0)


def _bcast_row(x, i, n):
    return jnp.broadcast_to(x[i:i + 1, :], (n, x.shape[1]))


def _conv_taps(v, tail):
    n, w = v.shape
    rid = _row_ids(n, w)
    v1 = jnp.where(rid == 0, _bcast_row(tail, 7, n), pltpu.roll(v, 1, 0))
    v2 = jnp.where(rid == 0, _bcast_row(tail, 6, n),
                   jnp.where(rid == 1, _bcast_row(tail, 7, n), pltpu.roll(v, 2, 0)))
    return v1, v2


def _block_tiles():
    half_rows, half_cols = SSM_W // 2, N_STATE // 2
    for half in range(2):
        for part in range(2):
            yield (slice(half * half_rows, (half + 1) * half_rows),
                   slice(part * N_STATE + half * half_cols, part * N_STATE + (half + 1) * half_cols))


def _block_expand(x, mat_ref, out_ref):
    for rows, cols in _block_tiles():
        out_ref[:, cols] = _dot(x[:, rows], mat_ref[rows, cols])


def _block_contract(s, mat_ref):
    halves = {}
    for rows, cols in _block_tiles():
        part = _dot_nt(s[:, cols], mat_ref[rows, cols])
        halves[rows.start] = part if rows.start not in halves else halves[rows.start] + part
    return jnp.concatenate([halves[k] for k in sorted(halves)], axis=1)


def _block_wgrad(a, b, name):
    t = a.shape[1]
    half_rows, half_cols = SSM_W // 2, N_STATE // 2

    def body(a_ref, b_ref, o_ref):
        o_ref[...] = _dot_tn(a_ref[...], b_ref[...])

    return pl.pallas_call(
        body, name=name,
        grid=(2, 2),
        in_specs=[pl.BlockSpec((None, t, half_rows), lambda h, p: (0, 0, h)),
                  pl.BlockSpec((None, t, half_cols), lambda h, p: (0, 0, 2 * p + h))],
        out_specs=pl.BlockSpec((half_rows, half_cols), lambda h, p: (h, 2 * p + h)),
        out_shape=jax.ShapeDtypeStruct((SSM_W, 2 * N_STATE), F32),
        compiler_params=_cparams(("parallel", "parallel")),
    )(*_in_hbm(a, b))


def _scan_chunk(work, ltab, carry, reverse):
    nblk = CHUNK // SUBLANES
    for gi in range(N_STATE // LANE_GROUP):
        cre = pl.ds(gi * LANE_GROUP, LANE_GROUP)
        cim = pl.ds(N_STATE + gi * LANE_GROUP, LANE_GROUP)
        pows = [(ltab[8 * k:8 * k + 8, cre], ltab[8 * k:8 * k + 8, cim]) for k in range(3)]
        pr = ltab[24:32, cre]
        pi = ltab[24:32, cim]

        def blk(i, c, cre=cre, cim=cim, pows=pows, pr=pr, pi=pi):
            cr, ci = c
            b = (nblk - 1 - i) if reverse else i
            r0 = pl.multiple_of(b * SUBLANES, SUBLANES)
            xr = work[pl.ds(r0, SUBLANES), cre]
            xi = work[pl.ds(r0, SUBLANES), cim]
            for k, s in enumerate((1, 2, 4)):
                lr, li = pows[k]
                shift = SUBLANES - s if reverse else s
                sr = pltpu.roll(xr, shift, 0)
                si = pltpu.roll(xi, shift, 0)
                xr, xi = xr + lr * sr - li * si, xi + lr * si + li * sr
            xr, xi = xr + pr * cr - pi * ci, xi + pr * ci + pi * cr
            work[pl.ds(r0, SUBLANES), cre] = xr
            work[pl.ds(r0, SUBLANES), cim] = xi
            edge = 0 if reverse else SUBLANES - 1
            return _bcast_row(xr, edge, SUBLANES), _bcast_row(xi, edge, SUBLANES)

        cr, ci = lax.fori_loop(0, nblk, blk, (carry[:, cre], carry[:, cim]))
        carry[:, cre] = cr
        carry[:, cim] = ci


def _s5conv_fwd(z, convw, convb, bbmat, ccmat, dvec, ltab):
    def body(z_ref, cw_ref, cb_ref, bb_ref, cc_ref, d_ref, lt_ref, ya_ref, ys_ref, hs_ref,
             work, carry, tail):
        c = pl.program_id(0)

        @pl.when(c == 0)
        def _():
            carry[...] = jnp.zeros_like(carry)
            tail[...] = jnp.zeros_like(tail)

        zb = z_ref[:, 0:CONV_W]
        v = z_ref[:, CONV_W:2 * CONV_W] * z_ref[:, 2 * CONV_W:3 * CONV_W]
        us = z_ref[:, 3 * CONV_W:4 * CONV_W]
        v1, v2 = _conv_taps(v, tail[...])
        tail[...] = v[CHUNK - 8:CHUNK, :]
        y = cw_ref[0:1, :] * v2 + cw_ref[1:2, :] * v1 + cw_ref[2:3, :] * v
        ya_ref[...] = zb * (y + cb_ref[...])

        _block_expand(us.astype(BF16), bb_ref, work)
        _scan_chunk(work, lt_ref, carry, reverse=False)
        hs = work[...].astype(BF16)
        hs_ref[...] = hs
        ys_ref[...] = _block_contract(hs, cc_ref) + d_ref[...] * us

    return _out_hbm(pl.pallas_call(
        body, name="s5conv_fwd",
        grid=(N_CHUNKS,),
        in_specs=[pl.BlockSpec((CHUNK, IN_COLS), lambda c: (c, 0)),
                  pl.BlockSpec((3, CONV_W), lambda c: (0, 0)),
                  pl.BlockSpec((1, CONV_W), lambda c: (0, 0)),
                  pl.BlockSpec((SSM_W, 2 * N_STATE), lambda c: (0, 0)),
                  pl.BlockSpec((SSM_W, 2 * N_STATE), lambda c: (0, 0)),
                  pl.BlockSpec((1, SSM_W), lambda c: (0, 0)),
                  pl.BlockSpec((32, 2 * N_STATE), lambda c: (0, 0))],
        out_specs=[pl.BlockSpec((CHUNK, CONV_W), lambda c: (c, 0)),
                   pl.BlockSpec((CHUNK, SSM_W), lambda c: (c, 0)),
                   pl.BlockSpec((CHUNK, 2 * N_STATE), lambda c: (c, 0))],
        out_shape=[pltpu.HBM((SEQ, CONV_W), F32),
                   pltpu.HBM((SEQ, SSM_W), F32),
                   jax.ShapeDtypeStruct((SEQ, 2 * N_STATE), BF16)],
        scratch_shapes=[pltpu.VMEM((CHUNK, 2 * N_STATE), F32),
                        pltpu.VMEM((8, 2 * N_STATE), F32),
                        pltpu.VMEM((8, CONV_W), F32)],
        compiler_params=_cparams(("arbitrary",)),
    )(*_in_hbm(z, convw, convb, bbmat, ccmat, dvec, ltab)), (0, 1))


def _s5conv_bwd(z, hs, dya, dys, convw, convb, bbmat, ccmat, dvec, ltab_rev):
    nc = N_CHUNKS
    hb = 16

    def body(z_ref, zp_ref, hs_ref, hp_ref, dya_ref, dys_ref, cw_ref, cb_ref, bb_ref, cc_ref, d_ref, lt_ref,
             dz_ref, g_ref, us_ref, dyb_ref, dl_ref, dcw_ref, work, carry, head):
        i = pl.program_id(0)
        first_chunk = i == nc - 1

        @pl.when(i == 0)
        def _():
            carry[...] = jnp.zeros_like(carry)
            head[...] = jnp.zeros_like(head)
            dl_ref[...] = jnp.zeros_like(dl_ref)
            dcw_ref[...] = jnp.zeros_like(dcw_ref)

        us = z_ref[:, 3 * CONV_W:4 * CONV_W]
        dy = dys_ref[...]
        dy_bf = dy.astype(BF16)
        us_ref[0] = us.astype(BF16)
        dyb_ref[0] = dy_bf

        _block_expand(dy_bf, cc_ref, work)
        _scan_chunk(work, lt_ref, carry, reverse=True)
        gg = work[...]
        gg_bf = gg.astype(BF16)
        g_ref[0] = gg_bf
        dus = d_ref[...] * dy + _block_contract(gg_bf, bb_ref)

        hcur = hs_ref[...].astype(F32)
        hlast = hp_ref[...].astype(F32)[hb - 1:hb, :]
        hlast = jnp.where(first_chunk, 0.0, hlast)
        rid = _row_ids(CHUNK, 2 * N_STATE)
        hprev = jnp.where(rid == 0, jnp.broadcast_to(hlast, (CHUNK, 2 * N_STATE)), pltpu.roll(hcur, 1, 0))
        gr, gi = gg[:, :N_STATE], gg[:, N_STATE:]
        hr, hi = hprev[:, :N_STATE], hprev[:, N_STATE:]
        dl_ref[:, :N_STATE] += (gr * hr + gi * hi).reshape(CHUNK // 8, 8, N_STATE).sum(axis=0)
        dl_ref[:, N_STATE:] += (gi * hr - gr * hi).reshape(CHUNK // 8, 8, N_STATE).sum(axis=0)

        @pl.when(i == nc - 1)
        def _():
            dl_ref[0:1, :] = jnp.sum(dl_ref[...], axis=0, keepdims=True)

        zb = z_ref[:, 0:CONV_W]
        zc = z_ref[:, CONV_W:2 * CONV_W]
        zv = z_ref[:, 2 * CONV_W:3 * CONV_W]
        v = zc * zv
        vtail = jnp.where(first_chunk, 0.0, zp_ref[:, CONV_W:2 * CONV_W] * zp_ref[:, 2 * CONV_W:3 * CONV_W])
        v1, v2 = _conv_taps(v, vtail)
        w0, w1, w2 = cw_ref[0:1, :], cw_ref[1:2, :], cw_ref[2:3, :]
        y = w0 * v2 + w1 * v1 + w2 * v
        dya_v = dya_ref[...]
        dzb = dya_v * (y + cb_ref[...])
        dyc = dya_v * zb
        hd = head[...]
        rc = _row_ids(CHUNK, CONV_W)
        n1 = jnp.where(rc == CHUNK - 1, _bcast_row(hd, 0, CHUNK), pltpu.roll(dyc, CHUNK - 1, 0))
        n2 = jnp.where(rc == CHUNK - 1, _bcast_row(hd, 1, CHUNK),
                       jnp.where(rc == CHUNK - 2, _bcast_row(hd, 0, CHUNK), pltpu.roll(dyc, CHUNK - 2, 0)))
        head[...] = dyc[0:8, :]
        dv = w2 * dyc + w1 * n1 + w0 * n2
        dz_ref[:, 0:CONV_W] = dzb.astype(BF16)
        dz_ref[:, CONV_W:2 * CONV_W] = (dv * zv).astype(BF16)
        dz_ref[:, 2 * CONV_W:3 * CONV_W] = (dv * zc).astype(BF16)
        dz_ref[:, 3 * CONV_W:4 * CONV_W] = dus.astype(BF16)
        dcw_ref[0:1, :] += jnp.sum(dyc * v2, axis=0, keepdims=True)
        dcw_ref[1:2, :] += jnp.sum(dyc * v1, axis=0, keepdims=True)
        dcw_ref[2:3, :] += jnp.sum(dyc * v, axis=0, keepdims=True)
        dcw_ref[3:4, :] += jnp.sum(dyc, axis=0, keepdims=True)
        dcw_ref[4:5, :] += jnp.sum(dy * us, axis=0, keepdims=True)

    rev = lambda i: nc - 1 - i
    return _out_hbm(pl.pallas_call(
        body, name="s5conv_bwd",
        grid=(nc,),
        in_specs=[pl.BlockSpec((CHUNK, IN_COLS), lambda i: (rev(i), 0)),
                  pl.BlockSpec((8, IN_COLS), lambda i: (jnp.maximum(rev(i) * (CHUNK // 8) - 1, 0), 0)),
                  pl.BlockSpec((CHUNK, 2 * N_STATE), lambda i: (rev(i), 0)),
                  pl.BlockSpec((hb, 2 * N_STATE), lambda i: (jnp.maximum(rev(i) * (CHUNK // hb) - 1, 0), 0)),
                  pl.BlockSpec((CHUNK, CONV_W), lambda i: (rev(i), 0)),
                  pl.BlockSpec((CHUNK, SSM_W), lambda i: (rev(i), 0)),
                  pl.BlockSpec((3, CONV_W), lambda i: (0, 0)),
                  pl.BlockSpec((1, CONV_W), lambda i: (0, 0)),
                  pl.BlockSpec((SSM_W, 2 * N_STATE), lambda i: (0, 0)),
                  pl.BlockSpec((SSM_W, 2 * N_STATE), lambda i: (0, 0)),
                  pl.BlockSpec((1, SSM_W), lambda i: (0, 0)),
                  pl.BlockSpec((32, 2 * N_STATE), lambda i: (0, 0))],
        out_specs=[pl.BlockSpec((CHUNK, IN_COLS), lambda i: (rev(i), 0)),
                   pl.BlockSpec((1, CHUNK, 2 * N_STATE), lambda i: (0, rev(i), 0)),
                   pl.BlockSpec((1, CHUNK, SSM_W), lambda i: (0, rev(i), 0)),
                   pl.BlockSpec((1, CHUNK, SSM_W), lambda i: (0, rev(i), 0)),
                   pl.BlockSpec((8, 2 * N_STATE), lambda i: (0, 0)),
                   pl.BlockSpec((8, CONV_W), lambda i: (0, 0))],
        out_shape=[jax.ShapeDtypeStruct((SEQ, IN_COLS), BF16),
                   pltpu.HBM((1, SEQ, 2 * N_STATE), BF16),
                   pltpu.HBM((1, SEQ, SSM_W), BF16),
                   pltpu.HBM((1, SEQ, SSM_W), BF16),
                   jax.ShapeDtypeStruct((8, 2 * N_STATE), F32),
                   jax.ShapeDtypeStruct((8, CONV_W), F32)],
        scratch_shapes=[pltpu.VMEM((CHUNK, 2 * N_STATE), F32),
                        pltpu.VMEM((8, 2 * N_STATE), F32),
                        pltpu.VMEM((8, CONV_W), F32)],
        compiler_params=_cparams(("arbitrary",)),
    )(*_in_hbm(z, z, hs, hs, dya, dys, convw, convb, bbmat, ccmat, dvec, ltab_rev)), (1, 2, 3))


def _mix_out_fwd(h, ya, ys, gluw, glub, con, son, wout):
    tm = TOK_TILE

    def body(h_ref, ya_ref, ys_ref, gw_ref, gb_ref, con_ref, son_ref, wo_ref, o_ref):
        zg, _ = _gelu(ys_ref[...])
        q = _dot(zg.astype(BF16), gw_ref[...]) + gb_ref[...]
        out_s = zg * _sigmoid(q)
        na, _ = _rms_stats(ya_ref[...])
        ns, _ = _rms_stats(out_s)
        o_ref[...] = (h_ref[...]
                      + _dot((na * con_ref[...]).astype(BF16), wo_ref[0:CONV_W, :])
                      + _dot((ns * son_ref[...]).astype(BF16), wo_ref[CONV_W:2 * CONV_W, :]))

    row = lambda m: (m, 0)
    fixed = lambda m: (0, 0)
    return pl.pallas_call(
        body, name="mix_out_fwd",
        grid=(SEQ // tm,),
        in_specs=[pl.BlockSpec((tm, D_MODEL), row), pl.BlockSpec((tm, CONV_W), row), pl.BlockSpec((tm, SSM_W), row),
                  pl.BlockSpec((SSM_W, SSM_W), fixed), pl.BlockSpec((1, SSM_W), fixed),
                  pl.BlockSpec((1, CONV_W), fixed), pl.BlockSpec((1, SSM_W), fixed),
                  pl.BlockSpec((None, D_MODEL, D_MODEL), lambda m: (0, 0, 0))],
        out_specs=pl.BlockSpec((tm, D_MODEL), row),
        out_shape=jax.ShapeDtypeStruct((SEQ, D_MODEL), F32),
        compiler_params=_cparams(("parallel",)),
    )(*_in_hbm(h, ya, ys, gluw, glub, con, son, wout))


def _mix_out_bwd(dh, ya, ys, gluw, glub, con, son, wout):
    tm = TOK_TILE

    def body(dh_ref, ya_ref, ys_ref, gw_ref, gb_ref, con_ref, son_ref, wo_ref,
             dya_ref, dys_ref, yc_ref, dhb_ref, zg_ref, dq_ref, part_ref):
        ysv = ys_ref[...]
        zg, th = _gelu(ysv)
        zg_bf = zg.astype(BF16)
        s = _sigmoid(_dot(zg_bf, gw_ref[...]) + gb_ref[...])
        out_s = zg * s
        na, ra = _rms_stats(ya_ref[...])
        ns, rs = _rms_stats(out_s)
        dh_bf = dh_ref[...].astype(BF16)
        yc_ref[0, :, 0:CONV_W] = (na * con_ref[...]).astype(BF16)
        yc_ref[0, :, CONV_W:2 * CONV_W] = (ns * son_ref[...]).astype(BF16)
        dhb_ref[0] = dh_bf
        dca = _dot_nt(dh_bf, wo_ref[0:CONV_W, :])
        dcs = _dot_nt(dh_bf, wo_ref[CONV_W:2 * CONV_W, :])
        dya, dcon = _rms_bwd(dca, na, ra, con_ref[...])
        dos, dson = _rms_bwd(dcs, ns, rs, son_ref[...])
        dya_ref[...] = dya
        dq = dos * zg * s * (1.0 - s)
        dq_bf = dq.astype(BF16)
        dzg = dos * s + _dot_nt(dq_bf, gw_ref[...])
        dys_ref[...] = dzg * _gelu_grad(ysv, th)
        zg_ref[0] = zg_bf
        dq_ref[0] = dq_bf
        rid = _row_ids(SUBLANES, SSM_W)
        part = jnp.zeros((SUBLANES, SSM_W), F32)
        for i, rowv in enumerate((dcon, dson, jnp.sum(dq, axis=0, keepdims=True))):
            part = jnp.where(rid == i, jnp.broadcast_to(rowv, (SUBLANES, SSM_W)), part)
        _accumulate(part_ref, pl.program_id(0) == 0, part)

    row = lambda m: (m, 0)
    fixed = lambda m: (0, 0)
    lead = lambda m: (0, m, 0)
    return _out_hbm(pl.pallas_call(
        body, name="mix_out_bwd",
        grid=(SEQ // tm,),
        in_specs=[pl.BlockSpec((tm, D_MODEL), row), pl.BlockSpec((tm, CONV_W), row), pl.BlockSpec((tm, SSM_W), row),
                  pl.BlockSpec((SSM_W, SSM_W), fixed), pl.BlockSpec((1, SSM_W), fixed),
                  pl.BlockSpec((1, CONV_W), fixed), pl.BlockSpec((1, SSM_W), fixed),
                  pl.BlockSpec((None, D_MODEL, D_MODEL), lambda m: (0, 0, 0))],
        out_specs=[pl.BlockSpec((tm, CONV_W), row), pl.BlockSpec((tm, SSM_W), row),
                   pl.BlockSpec((1, tm, D_MODEL), lead), pl.BlockSpec((1, tm, D_MODEL), lead),
                   pl.BlockSpec((1, tm, SSM_W), lead), pl.BlockSpec((1, tm, SSM_W), lead),
                   pl.BlockSpec((8, SSM_W), fixed)],
        out_shape=[pltpu.HBM((SEQ, CONV_W), F32), pltpu.HBM((SEQ, SSM_W), F32),
                   pltpu.HBM((1, SEQ, D_MODEL), BF16), pltpu.HBM((1, SEQ, D_MODEL), BF16),
                   pltpu.HBM((1, SEQ, SSM_W), BF16), pltpu.HBM((1, SEQ, SSM_W), BF16),
                   jax.ShapeDtypeStruct((8, SSM_W), F32)],
        compiler_params=_cparams(("arbitrary",)),
    )(*_in_hbm(dh, ya, ys, gluw, glub, con, son, wout)), (0, 1, 2, 3, 4, 5))


def _ple_fwd(h, g, p, wgate, wprojt):
    tm = TOK_TILE

    def body(h_ref, g_ref, p_ref, wg_ref, wp_ref, o_ref):
        x = h_ref[...]
        xh, _ = _rms_stats(x)
        s = _sigmoid(_dot((xh * g_ref[...]).astype(BF16), wg_ref[...]))
        o_ref[...] = x + _dot_nt(p_ref[...].astype(BF16), wp_ref[...]) * s

    row = lambda m: (m, 0)
    fixed = lambda m: (0, 0)
    return pl.pallas_call(
        body, name="ple_fwd",
        grid=(SEQ // tm,),
        in_specs=[pl.BlockSpec((tm, D_MODEL), row), pl.BlockSpec((1, D_MODEL), fixed), pl.BlockSpec((tm, PLE_DIM), row),
                  pl.BlockSpec((None, D_MODEL, D_MODEL), lambda m: (0, 0, 0)), pl.BlockSpec((D_MODEL, PLE_DIM), fixed)],
        out_specs=pl.BlockSpec((tm, D_MODEL), row),
        out_shape=jax.ShapeDtypeStruct((SEQ, D_MODEL), F32),
        compiler_params=_cparams(("parallel",)),
    )(*_in_hbm(h, g, p, wgate, wprojt))


def _ple_bwd(h, g, p, dh, wgate, wprojt):
    tm = TOK_TILE

    def body(h_ref, g_ref, p_ref, dh_ref, wg_ref, wp_ref, o_ref, u_ref, dq_ref, dpp_ref, pb_ref, dg_ref):
        xh, r = _rms_stats(h_ref[...])
        u = (xh * g_ref[...]).astype(BF16)
        s = _sigmoid(_dot(u, wg_ref[...]))
        p_bf = p_ref[...].astype(BF16)
        pp = _dot_nt(p_bf, wp_ref[...])
        dhv = dh_ref[...]
        dq = (dhv * pp * s * (1.0 - s)).astype(BF16)
        u_ref[0] = u
        dq_ref[0] = dq
        dpp_ref[0] = (dhv * s).astype(BF16)
        pb_ref[0] = p_bf
        dx, dg = _rms_bwd(_dot_nt(dq, wg_ref[...]), xh, r, g_ref[...])
        o_ref[...] = dhv + dx
        _accumulate(dg_ref, pl.program_id(0) == 0, dg)

    row = lambda m: (m, 0)
    fixed = lambda m: (0, 0)
    lead = lambda m: (0, m, 0)
    big = pltpu.HBM((1, SEQ, D_MODEL), BF16)
    return _out_hbm(pl.pallas_call(
        body, name="ple_bwd",
        grid=(SEQ // tm,),
        in_specs=[pl.BlockSpec((tm, D_MODEL), row), pl.BlockSpec((1, D_MODEL), fixed), pl.BlockSpec((tm, PLE_DIM), row),
                  pl.BlockSpec((tm, D_MODEL), row),
                  pl.BlockSpec((None, D_MODEL, D_MODEL), lambda m: (0, 0, 0)), pl.BlockSpec((D_MODEL, PLE_DIM), fixed)],
        out_specs=[pl.BlockSpec((tm, D_MODEL), row),
                   pl.BlockSpec((1, tm, D_MODEL), lead), pl.BlockSpec((1, tm, D_MODEL), lead),
                   pl.BlockSpec((1, tm, D_MODEL), lead), pl.BlockSpec((1, tm, PLE_DIM), lead),
                   pl.BlockSpec((1, D_MODEL), fixed)],
        out_shape=[jax.ShapeDtypeStruct((SEQ, D_MODEL), F32), big, big, big,
                   pltpu.HBM((1, SEQ, PLE_DIM), BF16),
                   jax.ShapeDtypeStruct((1, D_MODEL), F32)],
        compiler_params=_cparams(("arbitrary",)),
    )(*_in_hbm(h, g, p, dh, wgate, wprojt)), (1, 2, 3, 4))


def _final_loss(h, g, target):
    tm = TOK_TILE

    def body(h_ref, g_ref, t_ref, loss_ref, dh_ref, dg_ref):
        first = pl.program_id(0) == 0
        xh, r = _rms_stats(h_ref[...])
        diff = xh * g_ref[...] - t_ref[...]
        part = 0.5 * jnp.sum(jnp.mean(diff * diff, axis=-1, keepdims=True), axis=0, keepdims=True)
        _accumulate(loss_ref, first, jnp.broadcast_to(part, (SUBLANES, LANES)))
        dx, dg = _rms_bwd(diff * (1.0 / D_MODEL), xh, r, g_ref[...])
        dh_ref[...] = dx
        _accumulate(dg_ref, first, dg)

    row = lambda m: (m, 0)
    fixed = lambda m: (0, 0)
    return pl.pallas_call(
        body, name="final_loss",
        grid=(SEQ // tm,),
        in_specs=[pl.BlockSpec((tm, D_MODEL), row), pl.BlockSpec((1, D_MODEL), fixed),
                  pl.BlockSpec((tm, D_MODEL), row)],
        out_specs=[pl.BlockSpec((SUBLANES, LANES), fixed),
                   pl.BlockSpec((tm, D_MODEL), row),
                   pl.BlockSpec((1, D_MODEL), fixed)],
        out_shape=[jax.ShapeDtypeStruct((SUBLANES, LANES), F32),
                   jax.ShapeDtypeStruct((SEQ, D_MODEL), F32),
                   jax.ShapeDtypeStruct((1, D_MODEL), F32)],
        compiler_params=_cparams(("arbitrary",)),
    )(*_in_hbm(h, g, target))


def _disc(ar, ai, ldt):
    dt = jnp.exp(ldt)
    mag = jnp.exp(ar * dt)
    ph = ai * dt
    lr, li = mag * jnp.cos(ph), mag * jnp.sin(ph)
    nr, ni = lr - 1.0, li
    den = ar * ar + ai * ai
    return lr, li, (nr * ar + ni * ai) / den, (ni * ar - nr * ai) / den


def _s5_disc(a_row, ldt_row, a_rep, ldt_rep, bt, ct, tile_e, mask):
    n = N_STATE

    def body(ar_ref, lr_ref, ap_ref, lp_ref, b_ref, c_ref, e_ref, m_ref, lt_ref, ltr_ref, bb_ref, cc_ref):
        lr, li, _, _ = _disc(ar_ref[0], ar_ref[1], lr_ref[...])
        pr, pi = lr, li
        rid = _row_ids(SUBLANES, n)
        for k in range(1, 9):
            for ref, sgn, edge in ((lt_ref, 1.0, 24 + k - 1), (ltr_ref, -1.0, 24 + 8 - k)):
                if k in (1, 2, 4):
                    r0 = {1: 0, 2: 8, 4: 16}[k]
                    keep = (rid >= k) if ref is lt_ref else (rid < SUBLANES - k)
                    ref[r0:r0 + 8, 0:n] = jnp.where(keep, jnp.broadcast_to(pr, (8, n)), 0.0)
                    ref[r0:r0 + 8, n:2 * n] = jnp.where(keep, jnp.broadcast_to(sgn * pi, (8, n)), 0.0)
                ref[edge:edge + 1, 0:n] = pr
                ref[edge:edge + 1, n:2 * n] = sgn * pi
            pr, pi = pr * lr - pi * li, pr * li + pi * lr
        _, _, fr, fi = _disc(ap_ref[0], ap_ref[1], lp_ref[...])
        br, bi = b_ref[0], b_ref[1]
        e = e_ref[...]
        m = m_ref[...].astype(F32)
        bb_ref[:, 0:n] = (_dot((fr * br - fi * bi).astype(BF16), e) * m).astype(BF16)
        bb_ref[:, n:2 * n] = (_dot((fr * bi + fi * br).astype(BF16), e) * m).astype(BF16)
        cc_ref[:, 0:n] = (_dot(c_ref[0].astype(BF16), e) * m).astype(BF16)
        cc_ref[:, n:2 * n] = (-(_dot(c_ref[1].astype(BF16), e) * m)).astype(BF16)

    return pl.pallas_call(
        body, name="s5_disc",
        out_shape=[jax.ShapeDtypeStruct((32, 2 * n), F32), jax.ShapeDtypeStruct((32, 2 * n), F32),
                   jax.ShapeDtypeStruct((SSM_W, 2 * n), BF16), jax.ShapeDtypeStruct((SSM_W, 2 * n), BF16)],
        compiler_params=_cparams(None),
    )(a_row, ldt_row, a_rep, ldt_rep, bt, ct, tile_e, mask)


def _dot_exact(x, sel):
    hi = x.astype(BF16)
    r1 = x - hi.astype(F32)
    mid = r1.astype(BF16)
    lo = (r1 - mid.astype(F32)).astype(BF16)
    return _dot(hi, sel) + _dot(mid, sel) + _dot(lo, sel)


def _s5_disc_bwd(a, ldt, a_rep, ldt_rep, bt, mask, dl, d_bb, d_cc, fold):
    n = N_STATE

    def body(a_ref, l_ref, ap_ref, lp_ref, b_ref, m_ref, dl_ref, dbb_ref, dcc_ref, f_ref,
             da_ref, dldt_ref, db_ref, dc_ref):
        m = m_ref[...].astype(F32)
        fold_m = f_ref[...]
        diag = lambda x: _dot_exact(jnp.where(m > 0.0, x, 0.0), fold_m)
        dr, di = diag(dbb_ref[:, 0:n]), diag(dbb_ref[:, n:2 * n])
        dc_ref[0] = diag(dcc_ref[:, 0:n])
        dc_ref[1] = -diag(dcc_ref[:, n:2 * n])
        _, _, fr, fi = _disc(ap_ref[0], ap_ref[1], lp_ref[...])
        br, bi = b_ref[0], b_ref[1]
        db_ref[0] = fr * dr + fi * di
        db_ref[1] = fr * di - fi * dr
        per_state = lambda x: x.reshape(SSM_GROUPS, SSM_GROUP, SSM_STATE).sum(axis=1)
        dfr = per_state(dr * br + di * bi)
        dfi = per_state(di * br - dr * bi)
        _, vjp = jax.vjp(_disc, a_ref[0], a_ref[1], l_ref[...])
        dar, dai, dldt = vjp((dl_ref[0], dl_ref[1], dfr, dfi))
        da_ref[0] = dar
        da_ref[1] = dai
        dldt_ref[...] = jnp.sum(dldt, axis=1, keepdims=True)

    return pl.pallas_call(
        body, name="s5_disc_bwd",
        out_shape=[jax.ShapeDtypeStruct((2, SSM_GROUPS, SSM_STATE), F32),
                   jax.ShapeDtypeStruct((SSM_GROUPS, 1), F32),
                   jax.ShapeDtypeStruct((2, SSM_W, SSM_STATE), F32),
                   jax.ShapeDtypeStruct((2, SSM_W, SSM_STATE), F32)],
        compiler_params=_cparams(None),
    )(a, ldt, a_rep, ldt_rep, bt, mask, dl, d_bb, d_cc, fold)


def _row_block(rows, cap=512):
    for bm in range(min(cap, rows), 0, -1):
        if rows % bm == 0 and (bm % 8 == 0 or bm == rows):
            return bm
    return rows


SUM_PARTS = 2


def _own_pieces(segs, rtot):
    pr = rtot // SUM_PARTS
    assert pr * SUM_PARTS == rtot and pr % 16 == 0
    offs = _seg_offsets(segs)
    pieces = [[] for _ in range(SUM_PARTS)]
    for a, (n, r) in enumerate(segs):
        for m in range(n):
            lo = offs[a] + m * r
            for h in range(SUM_PARTS):
                clo, chi = max(lo, h * pr), min(lo + r, (h + 1) * pr)
                if chi > clo:
                    pieces[h].append((a, m, clo - lo, clo - h * pr, chi - clo))
    return pieces


def _pair_rows(srcs, got_ref, segs, pieces, h, chip, own_v, got_v, sems):
    pr = own_v.shape[0]
    dev = 2 * chip + lax.axis_index("c")
    for hh in range(SUM_PARTS):
        @pl.when(h == hh)
        def _(hh=hh):
            cps = [pltpu.make_async_copy(got_ref.at[chip, pl.ds(hh * pr, pr), :], got_v, sems.at[0])]
            for i, (a, m, so, do, rows) in enumerate(pieces[hh]):
                start = pl.multiple_of(dev * segs[a][1] + so, 16)
                cps.append(pltpu.make_async_copy(srcs[a].at[m, pl.ds(start, rows), :],
                                                 own_v.at[pl.ds(do, rows), :], sems.at[1 + i]))
            for cp in cps:
                cp.start()
            for cp in cps:
                cp.wait()
    return own_v[...].astype(F32) + got_v[...].astype(F32)


def _pair_sum(fulls, got, segs):
    ns = len(segs)
    _, rtot, c = got.shape
    pieces = _own_pieces(segs, rtot)
    pr = rtot // SUM_PARTS

    def body(*refs):
        srcs = refs[:ns]
        got_ref, pbf_ref, own_v, got_v, sems = refs[ns:]
        x, y, _ = _mesh_pos()
        j = pl.program_id(1)
        chip = jnp.where(j == 0, 2 * (1 - x) + y, jnp.where(j == 1, 2 * x + 1 - y, 2 * (1 - x) + 1 - y))
        pbf_ref[0] = _pair_rows(srcs, got_ref, segs, pieces, pl.program_id(0), chip, own_v, got_v, sems).astype(BF16)

    return pl.pallas_call(
        body, name="pair_sum",
        grid=(SUM_PARTS, 3),
        in_specs=[HBM] * (ns + 1), out_specs=pl.BlockSpec((1, pr, c), lambda h, j: (j, h, 0)),
        out_shape=pltpu.HBM((3, rtot, c), BF16),
        scratch_shapes=[pltpu.VMEM((pr, c), BF16), pltpu.VMEM((pr, c), BF16),
                        pltpu.SemaphoreType.DMA((1 + max(len(p) for p in pieces),))],
        compiler_params=_cparams(("arbitrary", "arbitrary")),
    )(*_in_hbm(*fulls, got))


def _chip_sum(fulls, got, rb, segs, layer, into):
    ns = len(segs)
    _, rtot, c = got.shape
    pieces = _own_pieces(segs, rtot)
    pr = rtot // SUM_PARTS

    def body(*refs):
        srcs = refs[:ns]
        got_ref, r_ref = refs[ns], refs[ns + 1]
        s_ref, own_v, got_v, sems = refs[-4:]
        x, y, _ = _mesh_pos()
        own = _pair_rows(srcs, got_ref, segs, pieces, pl.program_id(0), 2 * x + y, own_v, got_v, sems)
        s_ref[0] = ((own + r_ref[0].astype(F32)) + r_ref[1].astype(F32)) + r_ref[2].astype(F32)

    old = [] if into is None else [into]
    return pl.pallas_call(
        body, name="chip_sum",
        grid=(SUM_PARTS,),
        in_specs=[HBM] * (ns + 1) + [pl.BlockSpec((3, pr, c), lambda h: (0, h, 0))] + [HBM] * len(old),
        out_specs=pl.BlockSpec((1, pr, c), lambda h: (layer, h, 0)),
        out_shape=jax.ShapeDtypeStruct((DEPTH, rtot, c), F32),
        input_output_aliases={ns + 2: 0} if old else {},
        scratch_shapes=[pltpu.VMEM((pr, c), BF16), pltpu.VMEM((pr, c), BF16),
                        pltpu.SemaphoreType.DMA((1 + max(len(p) for p in pieces),))],
        compiler_params=_cparams(("arbitrary",)),
    )(*_in_hbm(*fulls, got, rb), *old)


def _adamw(w, g, m, v):
    r, c = w.shape
    bm = _row_block(r)
    bc1 = 1.0 - ADAM_B1 ** ADAM_STEP
    bc2 = 1.0 - ADAM_B2 ** ADAM_STEP

    def body(w_ref, g_ref, m_ref, v_ref, d_ref, nm_ref, nv_ref):
        gv = g_ref[...]
        nm = ADAM_B1 * m_ref[...] + (1.0 - ADAM_B1) * gv
        nv = ADAM_B2 * v_ref[...] + (1.0 - ADAM_B2) * (gv * gv)
        nm_ref[...] = nm
        nv_ref[...] = nv
        d_ref[...] = -ADAM_LR * ((nm / bc1) / (jnp.sqrt(nv / bc2) + ADAM_EPS) + ADAM_WD * w_ref[...])

    spec = pl.BlockSpec((bm, c), lambda k: (k, 0))
    shp = jax.ShapeDtypeStruct((r, c), F32)
    return pl.pallas_call(
        body, name="adamw",
        grid=(r // bm,),
        in_specs=[spec] * 4, out_specs=[spec] * 3, out_shape=[shp] * 3,
        compiler_params=_cparams(("parallel",)),
    )(*_in_hbm(w, g, m, v))


def _adamw_layers(sets, first, nl, prev):
    ns = len(sets)
    depth, r, c = sets[0][0].shape
    bm = _row_block(r, min(512, max(SUBLANES, (24 * MIB) // (ns * 8 * 2 * c * 4))))
    while any(four[4] is not None and four[4] % bm for four in sets):
        bm //= 2
    assert bm % SUBLANES == 0 and r % bm == 0
    bc1 = 1.0 - ADAM_B1 ** ADAM_STEP
    bc2 = 1.0 - ADAM_B2 ** ADAM_STEP

    def body(*refs):
        outs = refs[len(refs) - 4 * ns:]
        for s in range(ns):
            w_ref, m_ref, v_ref, g_ref = refs[4 * s:4 * s + 4]
            go_ref, d_ref, nm_ref, nv_ref = outs[4 * s:4 * s + 4]
            gv = g_ref[...]
            nm = ADAM_B1 * m_ref[...] + (1.0 - ADAM_B1) * gv
            nv = ADAM_B2 * v_ref[...] + (1.0 - ADAM_B2) * (gv * gv)
            go_ref[...] = gv
            nm_ref[...] = nm
            nv_ref[...] = nv
            d_ref[...] = -ADAM_LR * ((nm / bc1) / (jnp.sqrt(nv / bc2) + ADAM_EPS) + ADAM_WD * w_ref[...])

    at = pl.BlockSpec((1, bm, c), lambda i, k: (first + i, k, 0))

    def grad_spec(g_rows):
        if g_rows is None:
            return pl.BlockSpec((1, bm, c), lambda i, k: (i, k, 0))
        return pl.BlockSpec((1, bm, c), lambda i, k: (first + i, g_rows // bm + k, 0))

    shp = jax.ShapeDtypeStruct((depth, r, c), F32)
    old = [] if prev is None else [a for four in prev for a in four]
    flat = pl.pallas_call(
        body, name="adamw_layers",
        grid=(nl, r // bm),
        in_specs=[spec for four in sets for spec in (at, at, at, grad_spec(four[4]))] + [HBM] * len(old),
        out_specs=[at] * (4 * ns), out_shape=[shp] * (4 * ns),
        input_output_aliases={4 * ns + i: i for i in range(len(old))},
        compiler_params=_cparams(("parallel", "parallel")),
    )(*_in_hbm(*[a for four in sets for a in four[:4]]), *old)
    return [flat[4 * s:4 * s + 4] for s in range(ns)]


def _mesh_pos():
    return lax.axis_index("x"), lax.axis_index("y"), lax.axis_index("c")


def _dev_index(p):
    return 4 * p[0] + 2 * p[1] + p[2]


def _seg_offsets(segs):
    offs, o = [], 0
    for n, r in segs:
        offs.append(o)
        o += n * r
    return offs


def _remote(src, dst, send_sem, recv_sem, to):
    return pltpu.make_async_remote_copy(src_ref=src, dst_ref=dst, send_sem=send_sem, recv_sem=recv_sem,
                                        device_id=to, device_id_type=MESH)


def _allgather(pack, segs, name):
    rtot, c = pack.shape
    ns = len(segs)
    offs = _seg_offsets(segs)
    assert rtot == sum(n * r for n, r in segs)

    def body(pack_ref, *refs):
        outs = refs[:ns]
        send_sems, recv_sems, local_sem = refs[ns:]
        x, y, cc = _mesh_pos()
        me, sib = (x, y, cc), (x, y, 1 - cc)
        chips = [(1 - x, y), (x, 1 - y), (1 - x, 1 - y)]

        def pieces(dev, from_pack):
            res = []
            for a, (n, r) in enumerate(segs):
                for m in range(n):
                    dst = outs[a].at[m, pl.ds(pl.multiple_of(dev * r, r), r), :]
                    src = pack_ref.at[pl.ds(offs[a] + m * r, r), :] if from_pack else dst
                    res.append((src, dst))
            return res

        def push(k, dev, to, from_pack):
            for s, d in pieces(dev, from_pack):
                _remote(s, d, send_sems.at[k], recv_sems.at[k], to).start()

        def whole(k):
            return _remote(pack_ref, pack_ref, send_sems.at[k], recv_sems.at[k], me)

        my_dev = _dev_index(me)
        for s, d in pieces(my_dev, True):
            pltpu.make_async_copy(s, d, local_sem).start()
        push(0, my_dev, sib, True)
        for j, chip in enumerate(chips):
            push(1 + j, my_dev, (*chip, cc), True)
        for j, chip in enumerate(chips):
            whole(1 + j).wait_recv()
            push(4 + j, _dev_index((*chip, cc)), sib, False)
        whole(0).wait_recv()
        for j in range(3):
            whole(4 + j).wait_recv()
        for k in range(7):
            whole(k).wait_send()
        pltpu.make_async_copy(pack_ref, pack_ref, local_sem).wait()

    return pl.pallas_call(
        body, name=name,
        in_specs=[HBM], out_specs=[HBM] * ns,
        out_shape=[jax.ShapeDtypeStruct((n, N_DEV * r, c), pack.dtype) for n, r in segs],
        scratch_shapes=[pltpu.SemaphoreType.DMA((7,)), pltpu.SemaphoreType.DMA((7,)), pltpu.SemaphoreType.DMA],
    )(pack)


HBM = pl.BlockSpec(memory_space=pltpu.HBM)
SEM = pl.BlockSpec(memory_space=pltpu.SEMAPHORE)
VMEM_WHOLE = pl.BlockSpec(memory_space=pltpu.VMEM)
EFFECT = pltpu.SideEffectType.DATAFLOW_SIDE_EFFECTING


def _hbm(a):
    return pltpu.with_memory_space_constraint(a, pltpu.HBM)


def _ag_start(pack, segs, after, name):
    rtot, c = pack.shape
    ns = len(segs)
    offs = _seg_offsets(segs)

    def body(pack_ref, *refs):
        lands = refs[:ns]
        send_sems, recv_sems = refs[ns + 1], refs[ns + 2]
        token = refs[-1]
        x, y, cc = _mesh_pos()
        my_dev = _dev_index((x, y, cc))
        targets = [(x, y, 1 - cc), (1 - x, y, cc), (x, 1 - y, cc), (1 - x, 1 - y, cc)]
        for k, to in enumerate(targets):
            for a, (n, r) in enumerate(segs):
                for m in range(n):
                    _remote(pack_ref.at[pl.ds(offs[a] + m * r, r), :],
                            lands[a].at[m, pl.ds(pl.multiple_of(my_dev * r, r), r), :],
                            send_sems.at[k], recv_sems.at[k], to).start()
        token[...] = jnp.zeros_like(token)

    land_shapes = [(n, N_DEV * r, c) for n, r in segs]
    outs = pl.pallas_call(
        body, name=name,
        in_specs=[HBM] * (1 + ns) + [UNREAD],
        out_specs=[SEM, SEM, HBM] + [HBM] * ns + [VMEM_WHOLE],
        out_shape=[pltpu.SemaphoreType.DMA((4,)), pltpu.SemaphoreType.DMA((4,)), pltpu.HBM(pack.shape, pack.dtype)]
        + [pltpu.HBM(s, pack.dtype) for s in land_shapes] + [jax.ShapeDtypeStruct((SUBLANES, LANES), F32)],
        input_output_aliases={0: 2, **{1 + i: 3 + i for i in range(ns)}},
        compiler_params=pltpu.CompilerParams(has_side_effects=EFFECT),
    )(_hbm(pack), *[_hbm(lax.empty(s, pack.dtype)) for s in land_shapes], _hbm(after))
    return outs[0], outs[1], outs[2], list(outs[3:3 + ns]), outs[-1]


def _ag_wait(send_sems, recv_sems, pack, lands, after, name):
    ns = len(lands)

    def body(pack_ref, *refs):
        send_ref, recv_ref = refs[ns], refs[ns + 1]
        me = _mesh_pos()
        for k in range(4):
            whole = _remote(pack_ref, pack_ref, send_ref.at[k], recv_ref.at[k], me)
            whole.wait_send()
            whole.wait_recv()

    outs = pl.pallas_call(
        body, name=name,
        in_specs=[HBM] * (1 + ns) + [SEM, SEM, UNREAD],
        out_specs=[HBM] * (1 + ns),
        out_shape=[pltpu.HBM(pack.shape, pack.dtype)] + [pltpu.HBM(a.shape, a.dtype) for a in lands],
        input_output_aliases={i: i for i in range(1 + ns)},
        compiler_params=pltpu.CompilerParams(has_side_effects=EFFECT),
    )(pack, *lands, send_sems, recv_sems, _hbm(after))
    return outs[0], list(outs[1:])


def _ag_finish(pack, lands, segs):
    rtot, c = pack.shape
    ns = len(segs)
    offs = _seg_offsets(segs)

    def body(pack_ref, *refs):
        outs = refs[ns:2 * ns]
        stage, send_sems, recv_sems, local_sems = refs[2 * ns:]
        x, y, cc = _mesh_pos()
        me, sib = (x, y, cc), (x, y, 1 - cc)
        chips = [(1 - x, y), (x, 1 - y), (1 - x, 1 - y)]

        def rows(a, m, dev):
            return outs[a].at[m, pl.ds(pl.multiple_of(dev * segs[a][1], segs[a][1]), segs[a][1]), :]

        for j, chip in enumerate(chips):
            dev = _dev_index((*chip, cc))
            for a, (n, r) in enumerate(segs):
                for m in range(n):
                    _remote(rows(a, m, dev), rows(a, m, dev), send_sems.at[j], recv_sems.at[j], sib).start()
        load = pltpu.make_async_copy(pack_ref, stage, local_sems.at[0])
        load.start()
        load.wait()
        my_dev = _dev_index(me)
        for a, (n, r) in enumerate(segs):
            for m in range(n):
                pltpu.make_async_copy(stage.at[pl.ds(offs[a] + m * r, r), :], rows(a, m, my_dev), local_sems.at[1]).start()
        pltpu.make_async_copy(stage, pack_ref, local_sems.at[1]).wait()
        for j in range(3):
            _remote(pack_ref, pack_ref, send_sems.at[j], recv_sems.at[j], me).wait()

    outs = pl.pallas_call(
        body, name="ag_finish",
        in_specs=[HBM] * (1 + ns), out_specs=[HBM] * ns,
        out_shape=[pltpu.HBM(a.shape, a.dtype) if r >= 128 else jax.ShapeDtypeStruct(a.shape, a.dtype)
                   for a, (_, r) in zip(lands, segs)],
        input_output_aliases={1 + i: i for i in range(ns)},
        scratch_shapes=[pltpu.VMEM((rtot, c), pack.dtype), pltpu.SemaphoreType.DMA((3,)),
                        pltpu.SemaphoreType.DMA((3,)), pltpu.SemaphoreType.DMA((2,))],
        compiler_params=_cparams(None, 16),
    )(pack, *lands)
    return list(outs)


def _rs_chips_start(pbf, after, name):
    _, rtot, c = pbf.shape

    def body(pbf_ref, land_ref, after_ref, send_sems, recv_sems, pbf_thru, land_thru, token):
        x, y, cc = _mesh_pos()
        for j, (cx, cy) in enumerate([(1 - x, y), (x, 1 - y), (1 - x, 1 - y)]):
            _remote(pbf_ref.at[j], land_ref.at[j], send_sems.at[j], recv_sems.at[j], (cx, cy, cc)).start()
        token[...] = jnp.zeros_like(token)

    return pl.pallas_call(
        body, name=name,
        in_specs=[HBM, HBM, UNREAD],
        out_specs=[SEM, SEM, HBM, HBM, VMEM_WHOLE],
        out_shape=[pltpu.SemaphoreType.DMA((3,)), pltpu.SemaphoreType.DMA((3,)), pltpu.HBM(pbf.shape, pbf.dtype),
                   pltpu.HBM((3, rtot, c), pbf.dtype), jax.ShapeDtypeStruct((SUBLANES, LANES), F32)],
        input_output_aliases={0: 2, 1: 3},
        compiler_params=pltpu.CompilerParams(has_side_effects=EFFECT),
    )(_hbm(pbf), _hbm(lax.empty((3, rtot, c), pbf.dtype)), _hbm(after))


def _rs_chips_wait(send_sems, recv_sems, pbf, land, after, name):
    def body(pbf_ref, land_ref, send_ref, recv_ref, after_ref, pbf_out, land_out):
        me = _mesh_pos()
        for j in range(3):
            cp = _remote(pbf_ref.at[0], land_ref.at[j], send_ref.at[j], recv_ref.at[j], me)
            cp.wait_send()
            cp.wait_recv()

    return pl.pallas_call(
        body, name=name,
        in_specs=[HBM, HBM, SEM, SEM, UNREAD], out_specs=[HBM, HBM],
        out_shape=[pltpu.HBM(pbf.shape, pbf.dtype), pltpu.HBM(land.shape, land.dtype)],
        input_output_aliases={0: 0, 1: 1},
        compiler_params=pltpu.CompilerParams(has_side_effects=EFFECT),
    )(pbf, land, send_sems, recv_sems, _hbm(after))[1]


def _flips():
    return [(dx, dy, dc) for dx in (0, 1) for dy in (0, 1) for dc in (0, 1) if dx or dy or dc]


def _small_gather_start(flat, name):
    r, c = flat.shape

    def body(flat_ref, land_ref, send_sems, recv_sems, flat_thru, land_thru, token):
        x, y, cc = _mesh_pos()
        mine = land_ref.at[_dev_index((x, y, cc))]
        for k, (dx, dy, dc) in enumerate(_flips()):
            to = (1 - x if dx else x, 1 - y if dy else y, 1 - cc if dc else cc)
            _remote(flat_ref, mine, send_sems.at[k], recv_sems.at[k], to).start()
        token[...] = jnp.zeros_like(token)

    return pl.pallas_call(
        body, name=name,
        in_specs=[HBM, HBM],
        out_specs=[SEM, SEM, HBM, HBM, VMEM_WHOLE],
        out_shape=[pltpu.SemaphoreType.DMA((7,)), pltpu.SemaphoreType.DMA((7,)), pltpu.HBM(flat.shape, flat.dtype),
                   pltpu.HBM((N_DEV, r, c), flat.dtype), jax.ShapeDtypeStruct((SUBLANES, LANES), F32)],
        input_output_aliases={0: 2, 1: 3},
        compiler_params=pltpu.CompilerParams(has_side_effects=EFFECT),
    )(_hbm(flat), _hbm(lax.empty((N_DEV, r, c), flat.dtype)))


def _small_gather_wait(send_sems, recv_sems, flat, land, after, name):
    def body(flat_ref, land_ref, send_ref, recv_ref, after_ref, flat_out, land_out):
        me = _mesh_pos()
        for k in range(N_DEV - 1):
            cp = _remote(flat_ref, land_ref.at[0], send_ref.at[k], recv_ref.at[k], me)
            cp.wait_send()
            cp.wait_recv()

    return pl.pallas_call(
        body, name=name,
        in_specs=[HBM, HBM, SEM, SEM, UNREAD], out_specs=[HBM, HBM],
        out_shape=[pltpu.HBM(flat.shape, flat.dtype), pltpu.HBM(land.shape, land.dtype)],
        input_output_aliases={0: 0, 1: 1},
        compiler_params=pltpu.CompilerParams(has_side_effects=EFFECT),
    )(flat, land, send_sems, recv_sems, _hbm(after))


def _sum_devices(land, own):
    _, r, c = land.shape

    def body(land_ref, own_ref, out_ref):
        me = _dev_index(_mesh_pos())
        total = None
        for d in range(N_DEV):
            other = land_ref[jnp.where(d == me, (d + 1) % N_DEV, d)]
            block = jnp.where(d == me, own_ref[...], other)
            total = block if total is None else total + block
        out_ref[...] = total

    return pl.pallas_call(
        body, name="sum_devices",
        grid=(1,),
        in_specs=[pl.BlockSpec((N_DEV, r, c), lambda i: (0, 0, 0)), pl.BlockSpec((r, c), lambda i: (0, 0))],
        out_specs=pl.BlockSpec((r, c), lambda i: (0, 0)),
        out_shape=jax.ShapeDtypeStruct((r, c), F32),
        compiler_params=_cparams(("arbitrary",)),
    )(land, own)


def _rs_sibling_start(fulls, segs, name):
    ns = len(segs)
    offs = _seg_offsets(segs)
    rtot = sum(n * r for n, r in segs)
    c = fulls[0].shape[-1]
    dt = fulls[0].dtype

    def body(*refs):
        srcs = refs[:ns]
        land_ref, send_sem, recv_sem = refs[ns], refs[ns + 1], refs[ns + 2]
        token = refs[-1]
        x, y, cc = _mesh_pos()
        for k in range(4):
            for a, (n, r) in enumerate(segs):
                for m in range(n):
                    theirs = srcs[a].at[m, pl.ds(pl.multiple_of((2 * k + 1 - cc) * r, r), r), :]
                    _remote(theirs, land_ref.at[k, pl.ds(offs[a] + m * r, r), :], send_sem, recv_sem,
                            (x, y, 1 - cc)).start()
        token[...] = jnp.zeros_like(token)

    outs = pl.pallas_call(
        body, name=name,
        in_specs=[HBM] * (ns + 1),
        out_specs=[SEM, SEM] + [HBM] * (ns + 1) + [VMEM_WHOLE],
        out_shape=[pltpu.SemaphoreType.DMA(()), pltpu.SemaphoreType.DMA(())]
        + [pltpu.HBM(a.shape, a.dtype) for a in fulls] + [pltpu.HBM((4, rtot, c), dt),
                                                           jax.ShapeDtypeStruct((SUBLANES, LANES), F32)],
        input_output_aliases={i: 2 + i for i in range(ns + 1)},
        compiler_params=pltpu.CompilerParams(has_side_effects=EFFECT),
    )(*[_hbm(a) for a in fulls], _hbm(lax.empty((4, rtot, c), dt)))
    return outs[0], outs[1], list(outs[2:2 + ns]), outs[2 + ns], outs[-1]


def _rs_sibling_wait(send_sem, recv_sem, fulls, land, after, name):
    ns = len(fulls)

    def body(*refs):
        land_ref, send_ref, recv_ref = refs[ns], refs[ns + 1], refs[ns + 2]
        whole = _remote(land_ref, land_ref, send_ref, recv_ref, _mesh_pos())
        whole.wait_send()
        whole.wait_recv()

    outs = pl.pallas_call(
        body, name=name,
        in_specs=[HBM] * (ns + 1) + [SEM, SEM, UNREAD], out_specs=[HBM] * (ns + 1),
        out_shape=[pltpu.HBM(a.shape, a.dtype) for a in fulls] + [pltpu.HBM(land.shape, land.dtype)],
        input_output_aliases={i: i for i in range(ns + 1)},
        compiler_params=pltpu.CompilerParams(has_side_effects=EFFECT),
    )(*fulls, land, send_sem, recv_sem, _hbm(after))
    return list(outs[:ns]), outs[ns]


def _tp(w):
    return jnp.swapaxes(w, -1, -2)


def _s5_prepare(a_re, a_im, log_dt, b_re, b_im, c_re, c_im):
    a = jnp.stack([a_re, a_im], axis=1)
    ldt = jnp.broadcast_to(log_dt[:, :, None], (DEPTH, SSM_GROUPS, SSM_STATE))
    a_row = a.reshape(DEPTH, 2, 1, N_STATE)
    ldt_row = ldt.reshape(DEPTH, 1, N_STATE)
    a_rep = jnp.repeat(a, SSM_GROUP, axis=2)
    ldt_rep = jnp.repeat(ldt, SSM_GROUP, axis=1)
    bt = jnp.stack([_tp(b_re), _tp(b_im)], axis=1).reshape(DEPTH, 2, SSM_W, SSM_STATE)
    ct = jnp.stack([c_re, c_im], axis=1).reshape(DEPTH, 2, SSM_W, SSM_STATE)
    tile_e = jnp.tile(jnp.eye(SSM_STATE, dtype=BF16), (1, SSM_GROUPS))
    mask = jnp.repeat(jnp.repeat(jnp.eye(SSM_GROUPS, dtype=BF16), SSM_GROUP, axis=0), SSM_STATE, axis=1)
    out = []
    for l in range(DEPTH):
        tabs = _s5_disc(a_row[l], ldt_row[l], a_rep[l], ldt_rep[l], bt[l], ct[l], tile_e, mask)
        out.append(((a[l], ldt[l], a_rep[l], ldt_rep[l], bt[l], mask), *tabs))
    return out


def _layer_fwd(h, p_l, small, big, arrive=None):
    saved = {'h0': h}
    if arrive is not None:
        arrive(0, h)
    h, saved['gu1'] = _ffn_fwd(h, small['ffn1_norm'], big['ff1'])
    saved['h1'] = h
    if arrive is not None:
        arrive(1, h)
    z = _inproj_fwd(h, small['mix_norm'], big['wint'])
    ya, ys, hs = _s5conv_fwd(z, small['conv_w'], small['conv_b'], small['bbmat'], small['ccmat'], small['dvec'],
                             small['ltab'])
    saved.update(z=z, ya=ya, ys=ys, hs=hs)
    h = _mix_out_fwd(h, ya, ys, big['glu'], small['glu_b'], small['conv_out_norm'], small['ssm_out_norm'], big['wout'])
    saved['h2'] = h
    if arrive is not None:
        arrive(2, h)
    h, saved['gu2'] = _ffn_fwd(h, small['ffn2_norm'], big['ff2'])
    saved['h3'] = h
    h = _ple_fwd(h, small['ple_norm'], p_l, big['plg'], big['plpt'])
    return h, saved


def _ffn_bwd(h_in, g, dh, gu, w3):
    dh_in, dga, ud, dg = _ffn_bwd_act(h_in, g, dh, gu, w3)
    return dh_in, _matmul_tn(dga, ud, FF_BLOCK, BF16, "ffn_wgrad"), dg


def _layer_bwd_top(dh, p_l, small, big, saved):
    gs = {}
    dh, u, dq, dpp, pb, gs['ple_norm'] = _ple_bwd(saved['h3'], small['ple_norm'], p_l, dh, big['plg'], big['plpt'])
    d_plg = _matmul_tn(u, dq, 256, BF16, "ple_gate_wgrad")
    d_plpt = _matmul_tn(dpp, pb, 256, BF16, "ple_proj_wgrad", to_kernel=False)
    dh, d_ff2, gs['ffn2_norm'] = _ffn_bwd(saved['h2'], small['ffn2_norm'], dh, saved['gu2'], big['ff2'])
    return dh, (gs, d_plg, d_plpt, d_ff2)


def _layer_bwd_rest(dh, top, small, big, saved):
    gs, d_plg, d_plpt, d_ff2 = top
    dya, dys, ycat, dhb, zg, dq, part = _mix_out_bwd(dh, saved['ya'], saved['ys'], big['glu'], small['glu_b'],
                                                     small['conv_out_norm'], small['ssm_out_norm'], big['wout'])
    d_wout = _matmul_tn(ycat, dhb, 256, BF16, "w_out_wgrad")
    d_glu = _matmul_tn(zg, dq, 256, BF16, "glu_wgrad", to_kernel=False)
    dz, gadj, us, dyb, dl, dcw = _s5conv_bwd(saved['z'], saved['hs'], dya, dys, small['conv_w'], small['conv_b'],
                                             small['bbmat'], small['ccmat'], small['dvec'], small['ltab_rev'])
    d_bb = _block_wgrad(us, gadj, "s5_b_wgrad")
    d_cc = _block_wgrad(dyb, saved['hs'][None], "s5_c_wgrad")
    dh, u, gs['mix_norm'] = _inproj_bwd(saved['h1'], small['mix_norm'], dh, dz, big['wint'])
    d_wint = _matmul_tn(dz[None], u, 256, BF16, "w_in_wgrad")
    dh, d_ff1, gs['ffn1_norm'] = _ffn_bwd(saved['h0'], small['ffn1_norm'], dh, saved['gu1'], big['ff1'])

    dlb = dl[0].reshape(2, SSM_GROUPS, SSM_STATE)
    fold = jnp.tile(jnp.eye(SSM_STATE, dtype=BF16), (SSM_GROUPS, 1))
    da, dldt, dbt, dct = _s5_disc_bwd(*small['disc_in'], dlb, d_bb, d_cc, fold)
    gs['ssm_A_re'], gs['ssm_A_im'] = da[0], da[1]
    gs['ssm_log_dt'] = dldt[:, 0]
    ghp = (SSM_GROUPS, SSM_GROUP, SSM_STATE)
    gs['ssm_B_re'], gs['ssm_B_im'] = dbt[0].reshape(ghp), dbt[1].reshape(ghp)
    gs['ssm_C_re'], gs['ssm_C_im'] = dct[0].reshape(ghp), dct[1].reshape(ghp)
    gs['conv_w'] = dcw[0:3]
    gs['conv_b'] = dcw[3]
    gs['ssm_D'] = dcw[4].reshape(SSM_GROUPS, SSM_GROUP)
    gs['conv_out_norm'], gs['ssm_out_norm'], gs['glu_b'] = part[0], part[1], part[2]
    for n in ('ple_norm', 'ffn2_norm', 'mix_norm', 'ffn1_norm'):
        gs[n] = gs[n][0]
    fulls = [d_ff1, d_ff2, d_wint, d_wout, d_plg,
             d_plpt.reshape(1, D_MODEL * PLE_DIM // D_MODEL, D_MODEL), d_glu.reshape(1, SSM_W * SSM_W // D_MODEL, D_MODEL)]
    return dh, fulls, gs


VIEW_T = ('ffn1_w_gate', 'ffn1_w_up', 'ffn2_w_gate', 'ffn2_w_up', 'ssm_B_re', 'ssm_B_im')


def _view(name, a):
    return _tp(a) if name in VIEW_T else a


SEG_NAMES = ('ff1', 'ff2', 'wint', 'wout', 'plg', 'plpt', 'glu')
FIRST_LAYER_GROUPS = ((0,), (2, 3, 6), (1, 4, 5))


def _layer_pack(W, l, segments=range(len(SEGS))):
    pieces = {
        0: lambda: [_tp(W['ffn1_w_gate'][l]), _tp(W['ffn1_w_up'][l]), W['ffn1_w_down'][l]],
        1: lambda: [_tp(W['ffn2_w_gate'][l]), _tp(W['ffn2_w_up'][l]), W['ffn2_w_down'][l]],
        2: lambda: [_tp(W['w_in'][l])],
        3: lambda: [W['w_out'][l]],
        4: lambda: [W['ple_w_gate'][l]],
        5: lambda: [_tp(W['ple_w_proj'][l]).reshape(-1, D_MODEL)],
        6: lambda: [W['glu_w'][l].reshape(-1, D_MODEL)],
    }
    return jnp.concatenate([a for s in segments for a in pieces[s]()], axis=0).astype(BF16)


def _as_big(named):
    shape = dict(plpt=(D_MODEL, PLE_DIM), glu=(SSM_W, SSM_W))
    return {n: (a.reshape(shape[n]) if n in shape else a) for n, a in named.items()}


def _pad_rows(flat, mult, width=LANES):
    per = mult * width
    n = flat.shape[0]
    tot = -(-n // per) * per
    return jnp.pad(flat, (0, tot - n)).reshape(tot // width, width)


def _adamw_any(w, g, m, v):
    shp = w.shape
    two = (lambda t: t.reshape(-1, shp[-1]))
    d, nm, nv = _adamw(two(w), two(g), two(m), two(v))
    return d.reshape(shp), nm.reshape(shp), nv.reshape(shp)


def kernel(x, p, ffn1_norm, ffn1_w_gate, ffn1_w_up, ffn1_w_down, mix_norm, w_in, conv_w, conv_b, ssm_A_re, ssm_A_im, ssm_B_re, ssm_B_im, ssm_C_re, ssm_C_im, ssm_D, ssm_log_dt, glu_w, glu_b, conv_out_norm, ssm_out_norm, w_out, ffn2_norm, ffn2_w_gate, ffn2_w_up, ffn2_w_down, ple_norm, ple_w_gate, ple_w_proj, final_norm, loss_target, m_ffn1_norm, m_ffn1_w_gate, m_ffn1_w_up, m_ffn1_w_down, m_mix_norm, m_w_in, m_conv_w, m_conv_b, m_ssm_A_re, m_ssm_A_im, m_ssm_B_re, m_ssm_B_im, m_ssm_C_re, m_ssm_C_im, m_ssm_D, m_ssm_log_dt, m_glu_w, m_glu_b, m_conv_out_norm, m_ssm_out_norm, m_w_out, m_ffn2_norm, m_ffn2_w_gate, m_ffn2_w_up, m_ffn2_w_down, m_ple_norm, m_ple_w_gate, m_ple_w_proj, m_final_norm, v_ffn1_norm, v_ffn1_w_gate, v_ffn1_w_up, v_ffn1_w_down, v_mix_norm, v_w_in, v_conv_w, v_conv_b, v_ssm_A_re, v_ssm_A_im, v_ssm_B_re, v_ssm_B_im, v_ssm_C_re, v_ssm_C_im, v_ssm_D, v_ssm_log_dt, v_glu_w, v_glu_b, v_conv_out_norm, v_ssm_out_norm, v_w_out, v_ffn2_norm, v_ffn2_w_gate, v_ffn2_w_up, v_ffn2_w_down, v_ple_norm, v_ple_w_gate, v_ple_w_proj, v_final_norm):
    given = dict(locals())
    W = {n: given[n] for n in W_NAMES}
    M = {n: given['m_' + n] for n in W_NAMES}
    V = {n: given['v_' + n] for n in W_NAMES}
    Wv, Mv, Vv = [{n: _view(n, d[n]) for n in W_NAMES} for d in (W, M, V)]
    my_dev = _dev_index(_mesh_pos())

    conv_shard = _pad_rows(W['conv_w'].reshape(-1), SUBLANES)
    conv_all = _allgather(conv_shard, ((1, SUBLANES),), "ag_conv_w")[0]
    conv_full = conv_all.reshape(N_DEV, -1)[:, :DEPTH * 3 * (CONV_W // N_DEV)]
    conv_full = conv_full.reshape(N_DEV, DEPTH, 3, CONV_W // N_DEV).transpose(1, 2, 0, 3).reshape(DEPTH, 3, CONV_W)
    first, after = [], conv_all
    for gi, segments in enumerate(FIRST_LAYER_GROUPS):
        first.append(_ag_start(_layer_pack(W, 0, segments), tuple(SEGS[s] for s in segments), after,
                               "ag_start_0%s" % "abc"[gi]))
        after = first[-1][4]
    s5 = _s5_prepare(*[W[n] + after[0, 0] for n in ('ssm_A_re', 'ssm_A_im', 'ssm_log_dt')],
                     *[W[n] for n in ('ssm_B_re', 'ssm_B_im', 'ssm_C_re', 'ssm_C_im')])
    packs = [None] + [_layer_pack(W, l) for l in range(1, DEPTH)]
    prepared = conv_full[0, 0:1, 0:1] + s5[DEPTH - 1][1][0:1, 0:1] + packs[DEPTH - 1][0:1, 0:1].astype(F32)

    smalls, saves, bigs = [], [], []
    h = x[0]

    flight = None

    def gathered(handles, segments, after, name, next_layer=None, gate=None):
        nonlocal flight
        send_sems, recv_sems, pack_thru, lands, _ = handles
        pack_thru, lands = _ag_wait(send_sems, recv_sems, pack_thru, lands, after, "ag_wait_" + name)
        if next_layer is not None:
            flight = _ag_start(packs[next_layer], SEGS, pack_thru, "ag_start_%d" % next_layer)
            gate[0][gate[1]] = gate[0][gate[1]] + flight[4][0:1, 0:1]
        outs = _ag_finish(pack_thru, lands, tuple(SEGS[s] for s in segments))
        return _as_big({SEG_NAMES[s]: a for s, a in zip(segments, outs)})

    for l in range(DEPTH):
        small = {n: W[n][l][None] for n in ('ffn1_norm', 'mix_norm', 'conv_b', 'glu_b', 'conv_out_norm',
                                            'ssm_out_norm', 'ffn2_norm', 'ple_norm')}
        small['conv_w'] = conv_full[l]
        small['dvec'] = W['ssm_D'][l].reshape(1, SSM_W)
        small['disc_in'], small['ltab'], small['ltab_rev'], small['bbmat'], small['ccmat'] = s5[l]
        big = {}
        bigs.append(big)
        if l == 0:
            def arrive(stage, h_now, big=big, small=small):
                big.update(gathered(first[stage], FIRST_LAYER_GROUPS[stage], prepared if stage == 0 else h_now,
                                    "0%s" % "abc"[stage], *((1, (small, 'ffn2_norm')) if stage == 2 else ())))
            h, saved = _layer_fwd(h, p[l, 0], small, big, arrive)
        else:
            nxt = (l + 1, (small, 'ffn1_norm')) if l + 1 < DEPTH else ()
            big.update(gathered(flight, range(len(SEGS)), h, "%d" % l, *nxt))
            h, saved = _layer_fwd(h, p[l, 0], small, big)
        smalls.append(small)
        saves.append(saved)
    loss_tile, dh, d_final = _final_loss(h, W['final_norm'][None], loss_target[0])
    loss = lax.psum(loss_tile[0, 0], ("x", "y", "c"))

    layer_gs = [None] * DEPTH
    shard_grads = None
    sib, ici = None, None

    def finish_sibling(after_sib, after_ici):
        nonlocal sib, ici
        up, (send_sem, recv_sem, fulls_thru, land, _) = sib
        fulls_thru, got = _rs_sibling_wait(send_sem, recv_sem, fulls_thru, land, after_sib, "sib_wait_%d" % up)
        pbf = _pair_sum(fulls_thru, got, SEGS)
        done = finish_chips(after_ici)
        ici = (up, _rs_chips_start(pbf, after_ici if done is None else done, "rs_start_%d" % up), fulls_thru, got)
        sib = None

    def finish_chips(after):
        nonlocal ici, shard_grads
        if ici is None:
            return None
        up, (send_sems, recv_sems, pbf_thru, land, _), fulls_up, got_up = ici
        got3 = _rs_chips_wait(send_sems, recv_sems, pbf_thru, land, after, "rs_wait_%d" % up)
        shard_grads = _chip_sum(fulls_up, got_up, got3, SEGS, up, shard_grads)
        ici = None
        return shard_grads

    layer_names = [n for n in SMALL_NAMES if n != 'final_norm']
    small_flights = [None] * DEPTH
    for l in reversed(range(DEPTH)):
        small = dict(smalls[l])
        if sib is not None:
            small['ple_norm'] = small['ple_norm'] + sib[1][4][0:1, 0:1] + small_flights[l + 1][4][0:1, 0:1]
        dh, top = _layer_bwd_top(dh, p[l, 0], small, bigs[l], saves[l])
        if sib is not None:
            finish_sibling(dh, dh)
            small['glu_b'] = small['glu_b'] + ici[1][4][0:1, 0:1]
        dh, fulls, layer_gs[l] = _layer_bwd_rest(dh, top, small, bigs[l], saves[l])
        sib = (l, _rs_sibling_start(fulls, SEGS, "sib_start_%d" % l))
        last_slot = d_final[0] if l == DEPTH - 1 else jnp.zeros((D_MODEL,), F32)
        flat = jnp.concatenate([layer_gs[l][n].reshape(-1) for n in layer_names + ['conv_w']] + [last_slot])
        small_flights[l] = _small_gather_start(_pad_rows(flat, SUBLANES, D_MODEL), "small_start_%d" % l)
    grad_x = dh[None]
    finish_sibling(small_flights[0][4], small_flights[0][4])

    reduced = []
    for l in range(DEPTH):
        send_sems, recv_sems, flat_thru, land, _ = small_flights[l]
        flat_thru, land = _small_gather_wait(send_sems, recv_sems, flat_thru, land, ici[1][4], "small_wait_%d" % l)
        reduced.append(_sum_devices(land, flat_thru).reshape(-1))
    reduced = jnp.stack(reduced)
    G = {}
    o = 0
    for n in layer_names + ['conv_w']:
        size = (W[n].size if n != 'conv_w' else DEPTH * 3 * CONV_W) // DEPTH
        shape = Wv[n].shape if n != 'conv_w' else (DEPTH, 3, CONV_W)
        G[n] = reduced[:, o:o + size].reshape(shape)
        o += size
    G['final_norm'] = reduced[DEPTH - 1, o:o + D_MODEL]
    G['conv_w'] = lax.dynamic_slice_in_dim(G['conv_w'], my_dev * (CONV_W // N_DEV), CONV_W // N_DEV, axis=2)

    delta, new_m, new_v = {}, {}, {}
    for n in SMALL_NAMES + ['conv_w']:
        two = (lambda t: t.reshape(1, -1) if t.ndim == 1 else t)
        delta[n], new_m[n], new_v[n] = [t.reshape(Wv[n].shape) for t in
                                        _adamw_any(two(Wv[n]), two(G[n]), two(Mv[n]), two(Vv[n]))]

    offs = _seg_offsets(SEGS)
    r = SEGS[0][1]
    packed_rows = {'w_out': offs[3], 'ple_w_gate': offs[4]}
    for a, f in ((0, 'ffn1'), (1, 'ffn2')):
        packed_rows.update({f + '_w_gate': offs[a], f + '_w_up': offs[a] + r, f + '_w_down': offs[a] + 2 * r})

    def relaid(sg):
        nl = sg.shape[0]
        return {'w_in': _tp(sg[:, offs[2]:offs[2] + SEGS[2][1]]),
                'ple_w_proj': _tp(sg[:, offs[5]:offs[5] + SEGS[5][1]].reshape(nl, D_MODEL // N_DEV, PLE_DIM)),
                'glu_w': sg[:, offs[6]:offs[6] + SEGS[6][1]].reshape(nl, SSM_W // N_DEV, SSM_W)}

    groups = {}
    for n in list(packed_rows) + ['w_in', 'ple_w_proj', 'glu_w']:
        groups.setdefault(Wv[n].shape, []).append(n)

    def update(first, nl, prev):
        other = relaid(shard_grads[first:first + nl])
        sets = lambda ns: [(Wv[n], Mv[n], Vv[n], shard_grads, packed_rows[n]) if n in packed_rows
                           else (Wv[n], Mv[n], Vv[n], other[n], None) for n in ns]
        return {shape: _adamw_layers(sets(ns), first, nl, None if prev is None else prev[shape])
                for shape, ns in groups.items()}

    part = update(1, DEPTH - 1, None)
    finish_chips(sum(four[3][1, 0:1, 0:1] for fours in part.values() for four in fours)
                 + sum(new_v[n].reshape(-1, 1)[0:1] for n in SMALL_NAMES + ['conv_w']))
    for shape, fours in update(0, 1, part).items():
        for n, four in zip(groups[shape], fours):
            G[n], delta[n], new_m[n], new_v[n] = four

    outs = [[_view(n, d[n]) for n in W_NAMES] for d in (G, delta, new_m, new_v)]
    return (loss, grad_x, *outs[0], *outs[1], *outs[2], *outs[3])
```

```python
import math

import jax
import jax.numpy as jnp
from jax import lax
from jax.experimental import pallas as pl
from jax.experimental.pallas import tpu as pltpu

F32 = jnp.float32
BF16 = jnp.bfloat16

N_DEV = 8
DEPTH = 4
SEQ = 2048
D_MODEL = 1024
D_FF = 2816
CONV_W = 512
SSM_W = 512
SSM_GROUPS = 32
SSM_GROUP = 16
SSM_STATE = 64
N_STATE = SSM_GROUPS * SSM_STATE
IN_COLS = 2048
PLE_DIM = 256
EPS = 1e-6

ADAM_LR = 0.001
ADAM_B1 = 0.9
ADAM_B2 = 0.999
ADAM_EPS = 1e-08
ADAM_WD = 0.01
ADAM_STEP = 10

FF_BLOCK = 256
N_FF_BLOCKS = D_FF // FF_BLOCK
TOK_TILE_FFN_FWD = 2048
TOK_TILE_FFN_BWD = 1024
TOK_TILE = 512
CHUNK = 256
N_CHUNKS = SEQ // CHUNK
LANE_GROUP = 512
SUBLANES = 8
LANES = 128
MIB = 1024 * 1024

W_NAMES = ['ffn1_norm', 'ffn1_w_gate', 'ffn1_w_up', 'ffn1_w_down', 'mix_norm', 'w_in', 'conv_w', 'conv_b',
           'ssm_A_re', 'ssm_A_im', 'ssm_B_re', 'ssm_B_im', 'ssm_C_re', 'ssm_C_im', 'ssm_D', 'ssm_log_dt',
           'glu_w', 'glu_b', 'conv_out_norm', 'ssm_out_norm', 'w_out', 'ffn2_norm', 'ffn2_w_gate', 'ffn2_w_up',
           'ffn2_w_down', 'ple_norm', 'ple_w_gate', 'ple_w_proj', 'final_norm']
SMALL_NAMES = ['ffn1_norm', 'mix_norm', 'conv_b', 'ssm_A_re', 'ssm_A_im', 'ssm_B_re', 'ssm_B_im', 'ssm_C_re',
               'ssm_C_im', 'ssm_D', 'ssm_log_dt', 'glu_b', 'conv_out_norm', 'ssm_out_norm', 'ffn2_norm',
               'ple_norm', 'final_norm']

SEGS = ((3, 352), (3, 352), (1, 256), (1, 128), (1, 128), (1, 32), (1, 32))
PACK_ROWS = sum(n * r for n, r in SEGS)

MESH = pl.DeviceIdType.MESH
UNREAD = pl.BlockSpec(memory_space=pltpu.HBM)


def _in_hbm(*arrays):
    return [pltpu.with_memory_space_constraint(a, pltpu.HBM) for a in arrays]


def _out_hbm(outs, which):
    if not isinstance(outs, (list, tuple)):
        return pltpu.with_memory_space_constraint(outs, pltpu.HBM) if which else outs
    return [pltpu.with_memory_space_constraint(a, pltpu.HBM) if i in which else a for i, a in enumerate(outs)]


def _cparams(sem=None, vmem_mib=48, **kw):
    return pltpu.CompilerParams(dimension_semantics=sem, vmem_limit_bytes=vmem_mib * MIB, **kw)


def _dot(a, b):
    return jnp.dot(a, b, preferred_element_type=F32)


def _dot_nt(a, b):
    return lax.dot_general(a, b, (((1,), (1,)), ((), ())), preferred_element_type=F32)


def _dot_tn(a, b):
    return lax.dot_general(a, b, (((0,), (0,)), ((), ())), preferred_element_type=F32)


def _rms_stats(x):
    r = lax.rsqrt(jnp.mean(x * x, axis=-1, keepdims=True) + EPS)
    return x * r, r


def _rms_bwd(dy, xh, r, g):
    dxh = dy * g
    dx = r * (dxh - xh * jnp.mean(dxh * xh, axis=-1, keepdims=True))
    dg = jnp.sum(dy * xh, axis=0, keepdims=True)
    return dx, dg


def _sigmoid(x):
    return 0.5 * jnp.tanh(0.5 * x) + 0.5


_GELU_C = math.sqrt(2.0 / math.pi)


def _gelu(x):
    t = jnp.tanh(_GELU_C * (x + 0.044715 * x * x * x))
    return 0.5 * x * (1.0 + t), t


def _gelu_grad(x, t):
    return 0.5 * (1.0 + t) + 0.5 * x * (1.0 - t * t) * _GELU_C * (1.0 + 3.0 * 0.044715 * x * x)


def _accumulate(ref, first, value):
    @pl.when(first)
    def _():
        ref[...] = value

    @pl.when(jnp.logical_not(first))
    def _():
        ref[...] += value


def _ffn_fwd(h, g, w3):
    tm = TOK_TILE_FFN_FWD
    last = N_FF_BLOCKS - 1

    def body(h_ref, g_ref, wgu_ref, wd_ref, wd_last_ref, out_ref, gu_ref, u_ref, a_ref):
        k = pl.program_id(1)

        @pl.when(k == 0)
        def _():
            x = h_ref[...]
            xh, _ = _rms_stats(x)
            u_ref[...] = (xh * g_ref[...]).astype(BF16)
            out_ref[...] = x
            a_ref[1] = jnp.zeros((tm, FF_BLOCK), BF16)

        out_ref[...] += 0.5 * _dot(a_ref[(k + 1) % 2], wd_ref[0])
        gu = _dot_nt(u_ref[...], wgu_ref[...].reshape(2 * FF_BLOCK, D_MODEL))
        gate, up = gu[:, :FF_BLOCK], gu[:, FF_BLOCK:]
        a_ref[k % 2] = (gate * _sigmoid(gate) * up).astype(BF16)
        gu_ref[0] = gate.astype(BF16)
        gu_ref[1] = up.astype(BF16)

        @pl.when(k == last)
        def _():
            out_ref[...] += 0.5 * _dot(a_ref[last % 2], wd_last_ref[0])

    return _out_hbm(pl.pallas_call(
        body, name="ffn_fwd",
        grid=(SEQ // tm, N_FF_BLOCKS),
        in_specs=[pl.BlockSpec((tm, D_MODEL), lambda m, k: (m, 0), pipeline_mode=pl.Buffered(1)),
                  pl.BlockSpec((1, D_MODEL), lambda m, k: (0, 0)),
                  pl.BlockSpec((2, FF_BLOCK, D_MODEL), lambda m, k: (0, k, 0)),
                  pl.BlockSpec((1, FF_BLOCK, D_MODEL), lambda m, k: (2, jnp.maximum(k - 1, 0), 0)),
                  pl.BlockSpec((1, FF_BLOCK, D_MODEL), lambda m, k: (2, last, 0), pipeline_mode=pl.Buffered(1))],
        out_specs=[pl.BlockSpec((tm, D_MODEL), lambda m, k: (m, 0)),
                   pl.BlockSpec((2, tm, FF_BLOCK), lambda m, k: (0, m, k))],
        out_shape=[jax.ShapeDtypeStruct((SEQ, D_MODEL), F32),
                   pltpu.HBM((2, SEQ, D_FF), BF16)],
        scratch_shapes=[pltpu.VMEM((tm, D_MODEL), BF16), pltpu.VMEM((2, tm, FF_BLOCK), BF16)],
        compiler_params=_cparams(("parallel", "arbitrary"), 56),
    )(*_in_hbm(h, g, w3, w3, w3)), (1,))


def _ffn_bwd_act(h, g, dout, gu, w3):
    tm = TOK_TILE_FFN_BWD
    last = N_FF_BLOCKS - 1

    def body(h_ref, g_ref, d_ref, gu_ref, wd_ref, wgu_ref, wgu_last_ref, dh_ref, dga_ref, ud_ref, dg_ref,
             acc_ref, dgu_ref):
        m = pl.program_id(0)
        k = pl.program_id(1)

        @pl.when(k == 0)
        def _():
            xh, _ = _rms_stats(h_ref[...])
            ud_ref[0] = (xh * g_ref[...]).astype(BF16)
            ud_ref[1] = (0.5 * d_ref[...]).astype(BF16)
            acc_ref[...] = jnp.zeros_like(acc_ref)
            dgu_ref[1] = jnp.zeros((tm, 2 * FF_BLOCK), BF16)

        acc_ref[...] += _dot(dgu_ref[(k + 1) % 2], wgu_ref[...].reshape(2 * FF_BLOCK, D_MODEL))
        gate = gu_ref[0].astype(F32)
        up = gu_ref[1].astype(F32)
        sg = _sigmoid(gate)
        silu = gate * sg
        da = _dot_nt(ud_ref[1], wd_ref[0])
        dgate = (da * up * (sg + silu * (1.0 - sg))).astype(BF16)
        dup = (da * silu).astype(BF16)
        dga_ref[0] = dgate
        dga_ref[1] = dup
        dga_ref[2] = (silu * up).astype(BF16)
        dgu_ref[k % 2, :, 0:FF_BLOCK] = dgate
        dgu_ref[k % 2, :, FF_BLOCK:2 * FF_BLOCK] = dup

        @pl.when(k == last)
        def _():
            du = acc_ref[...] + _dot(dgu_ref[last % 2], wgu_last_ref[...].reshape(2 * FF_BLOCK, D_MODEL))
            xh, r = _rms_stats(h_ref[...])
            dx, dg = _rms_bwd(du, xh, r, g_ref[...])
            dh_ref[...] = d_ref[...] + dx
            _accumulate(dg_ref, m == 0, dg)

    return _out_hbm(pl.pallas_call(
        body, name="ffn_bwd_act",
        grid=(SEQ // tm, N_FF_BLOCKS),
        in_specs=[pl.BlockSpec((tm, D_MODEL), lambda m, k: (m, 0), pipeline_mode=pl.Buffered(1)),
                  pl.BlockSpec((1, D_MODEL), lambda m, k: (0, 0)),
                  pl.BlockSpec((tm, D_MODEL), lambda m, k: (m, 0), pipeline_mode=pl.Buffered(1)),
                  pl.BlockSpec((2, tm, FF_BLOCK), lambda m, k: (0, m, k)),
                  pl.BlockSpec((1, FF_BLOCK, D_MODEL), lambda m, k: (2, k, 0)),
                  pl.BlockSpec((2, FF_BLOCK, D_MODEL), lambda m, k: (0, jnp.maximum(k - 1, 0), 0)),
                  pl.BlockSpec((2, FF_BLOCK, D_MODEL), lambda m, k: (0, last, 0), pipeline_mode=pl.Buffered(1))],
        out_specs=[pl.BlockSpec((tm, D_MODEL), lambda m, k: (m, 0)),
                   pl.BlockSpec((3, tm, FF_BLOCK), lambda m, k: (0, m, k)),
                   pl.BlockSpec((2, tm, D_MODEL), lambda m, k: (0, m, 0)),
                   pl.BlockSpec((1, D_MODEL), lambda m, k: (0, 0))],
        out_shape=[jax.ShapeDtypeStruct((SEQ, D_MODEL), F32),
                   pltpu.HBM((3, SEQ, D_FF), BF16),
                   pltpu.HBM((2, SEQ, D_MODEL), BF16),
                   jax.ShapeDtypeStruct((1, D_MODEL), F32)],
        scratch_shapes=[pltpu.VMEM((tm, D_MODEL), F32), pltpu.VMEM((2, tm, 2 * FF_BLOCK), BF16)],
        compiler_params=_cparams(("arbitrary", "arbitrary"), 56),
    )(*_in_hbm(h, g, dout, gu, w3, w3, w3)), (1, 2))


def _matmul_tn(a, b, bm, out_dtype, name, bn=None, to_kernel=True):
    na, t, m = a.shape
    nb, _, n = b.shape
    bn = n if bn is None else bn

    def body(a_ref, b_ref, o_ref):
        o_ref[0] = _dot_tn(a_ref[0], b_ref[0]).astype(out_dtype)

    return _out_hbm(pl.pallas_call(
        body, name=name,
        grid=(na, m // bm, n // bn),
        in_specs=[pl.BlockSpec((1, t, bm), lambda i, k, j: (i, 0, k)),
                  pl.BlockSpec((1, t, bn), lambda i, k, j: (jnp.maximum(i - (na - nb), 0), 0, j))],
        out_specs=pl.BlockSpec((1, bm, bn), lambda i, k, j: (i, k, j)),
        out_shape=pltpu.HBM((na, m, n), out_dtype) if to_kernel else jax.ShapeDtypeStruct((na, m, n), out_dtype),
        compiler_params=_cparams(("arbitrary", "parallel", "parallel")),
    )(*_in_hbm(a, b)), to_kernel)


def _inproj_fwd(h, g, wint):
    tm = TOK_TILE

    def body(h_ref, g_ref, w_ref, z_ref):
        xh, _ = _rms_stats(h_ref[...])
        z_ref[...] = _dot_nt((xh * g_ref[...]).astype(BF16), w_ref[...])

    return pl.pallas_call(
        body, name="inproj_fwd",
        grid=(SEQ // tm,),
        in_specs=[pl.BlockSpec((tm, D_MODEL), lambda m: (m, 0)),
                  pl.BlockSpec((1, D_MODEL), lambda m: (0, 0)),
                  pl.BlockSpec((None, IN_COLS, D_MODEL), lambda m: (0, 0, 0))],
        out_specs=pl.BlockSpec((tm, IN_COLS), lambda m: (m, 0)),
        out_shape=jax.ShapeDtypeStruct((SEQ, IN_COLS), F32),
        compiler_params=_cparams(("parallel",)),
    )(*_in_hbm(h, g, wint))


def _inproj_bwd(h, g, dh, dz, wint):
    tm = TOK_TILE

    def body(h_ref, g_ref, dh_ref, dz_ref, w_ref, o_ref, u_ref, dg_ref):
        xh, r = _rms_stats(h_ref[...])
        u_ref[0] = (xh * g_ref[...]).astype(BF16)
        dx, dg = _rms_bwd(_dot(dz_ref[...], w_ref[...]), xh, r, g_ref[...])
        o_ref[...] = dh_ref[...] + dx
        _accumulate(dg_ref, pl.program_id(0) == 0, dg)

    return _out_hbm(pl.pallas_call(
        body, name="inproj_bwd",
        grid=(SEQ // tm,),
        in_specs=[pl.BlockSpec((tm, D_MODEL), lambda m: (m, 0)),
                  pl.BlockSpec((1, D_MODEL), lambda m: (0, 0)),
                  pl.BlockSpec((tm, D_MODEL), lambda m: (m, 0)),
                  pl.BlockSpec((tm, IN_COLS), lambda m: (m, 0)),
                  pl.BlockSpec((None, IN_COLS, D_MODEL), lambda m: (0, 0, 0))],
        out_specs=[pl.BlockSpec((tm, D_MODEL), lambda m: (m, 0)),
                   pl.BlockSpec((1, tm, D_MODEL), lambda m: (0, m, 0)),
                   pl.BlockSpec((1, D_MODEL), lambda m: (0, 0))],
        out_shape=[jax.ShapeDtypeStruct((SEQ, D_MODEL), F32),
                   pltpu.HBM((1, SEQ, D_MODEL), BF16),
                   jax.ShapeDtypeStruct((1, D_MODEL), F32)],
        compiler_params=_cparams(("arbitrary",)),
    )(*_in_hbm(h, g, dh, dz, wint)), (1,))


def _row_ids(n, w):
    return lax.broadcasted_iota(jnp.int32, (n, w), 0)


def _bcast_row(x, i, n):
    return jnp.broadcast_to(x[i:i + 1, :], (n, x.shape[1]))


def _conv_taps(v, tail):
    n, w = v.shape
    rid = _row_ids(n, w)
    v1 = jnp.where(rid == 0, _bcast_row(tail, 7, n), pltpu.roll(v, 1, 0))
    v2 = jnp.where(rid == 0, _bcast_row(tail, 6, n),
                   jnp.where(rid == 1, _bcast_row(tail, 7, n), pltpu.roll(v, 2, 0)))
    return v1, v2


def _block_tiles():
    half_rows, half_cols = SSM_W // 2, N_STATE // 2
    for half in range(2):
        for part in range(2):
            yield (slice(half * half_rows, (half + 1) * half_rows),
                   slice(part * N_STATE + half * half_cols, part * N_STATE + (half + 1) * half_cols))


def _block_expand(x, mat_ref, out_ref):
    for rows, cols in _block_tiles():
        out_ref[:, cols] = _dot(x[:, rows], mat_ref[rows, cols])


def _block_contract(s, mat_ref):
    halves = {}
    for rows, cols in _block_tiles():
        part = _dot_nt(s[:, cols], mat_ref[rows, cols])
        halves[rows.start] = part if rows.start not in halves else halves[rows.start] + part
    return jnp.concatenate([halves[k] for k in sorted(halves)], axis=1)


def _block_wgrad(a, b, name):
    t = a.shape[1]
    half_rows, half_cols = SSM_W // 2, N_STATE // 2

    def body(a_ref, b_ref, o_ref):
        o_ref[...] = _dot_tn(a_ref[...], b_ref[...])

    return pl.pallas_call(
        body, name=name,
        grid=(2, 2),
        in_specs=[pl.BlockSpec((None, t, half_rows), lambda h, p: (0, 0, h)),
                  pl.BlockSpec((None, t, half_cols), lambda h, p: (0, 0, 2 * p + h))],
        out_specs=pl.BlockSpec((half_rows, half_cols), lambda h, p: (h, 2 * p + h)),
        out_shape=jax.ShapeDtypeStruct((SSM_W, 2 * N_STATE), F32),
        compiler_params=_cparams(("parallel", "parallel")),
    )(*_in_hbm(a, b))


def _scan_chunk(work, ltab, carry, reverse):
    nblk = CHUNK // SUBLANES
    for gi in range(N_STATE // LANE_GROUP):
        cre = pl.ds(gi * LANE_GROUP, LANE_GROUP)
        cim = pl.ds(N_STATE + gi * LANE_GROUP, LANE_GROUP)
        pows = [(ltab[8 * k:8 * k + 8, cre], ltab[8 * k:8 * k + 8, cim]) for k in range(3)]
        pr = ltab[24:32, cre]
        pi = ltab[24:32, cim]

        def blk(i, c, cre=cre, cim=cim, pows=pows, pr=pr, pi=pi):
            cr, ci = c
            b = (nblk - 1 - i) if reverse else i
            r0 = pl.multiple_of(b * SUBLANES, SUBLANES)
            xr = work[pl.ds(r0, SUBLANES), cre]
            xi = work[pl.ds(r0, SUBLANES), cim]
            for k, s in enumerate((1, 2, 4)):
                lr, li = pows[k]
                shift = SUBLANES - s if reverse else s
                sr = pltpu.roll(xr, shift, 0)
                si = pltpu.roll(xi, shift, 0)
                xr, xi = xr + lr * sr - li * si, xi + lr * si + li * sr
            xr, xi = xr + pr * cr - pi * ci, xi + pr * ci + pi * cr
            work[pl.ds(r0, SUBLANES), cre] = xr
            work[pl.ds(r0, SUBLANES), cim] = xi
            edge = 0 if reverse else SUBLANES - 1
            return _bcast_row(xr, edge, SUBLANES), _bcast_row(xi, edge, SUBLANES)

        cr, ci = lax.fori_loop(0, nblk, blk, (carry[:, cre], carry[:, cim]))
        carry[:, cre] = cr
        carry[:, cim] = ci


def _s5conv_fwd(z, convw, convb, bbmat, ccmat, dvec, ltab):
    def body(z_ref, cw_ref, cb_ref, bb_ref, cc_ref, d_ref, lt_ref, ya_ref, ys_ref, hs_ref,
             work, carry, tail):
        c = pl.program_id(0)

        @pl.when(c == 0)
        def _():
            carry[...] = jnp.zeros_like(carry)
            tail[...] = jnp.zeros_like(tail)

        zb = z_ref[:, 0:CONV_W]
        v = z_ref[:, CONV_W:2 * CONV_W] * z_ref[:, 2 * CONV_W:3 * CONV_W]
        us = z_ref[:, 3 * CONV_W:4 * CONV_W]
        v1, v2 = _conv_taps(v, tail[...])
        tail[...] = v[CHUNK - 8:CHUNK, :]
        y = cw_ref[0:1, :] * v2 + cw_ref[1:2, :] * v1 + cw_ref[2:3, :] * v
        ya_ref[...] = zb * (y + cb_ref[...])

        _block_expand(us.astype(BF16), bb_ref, work)
        _scan_chunk(work, lt_ref, carry, reverse=False)
        hs = work[...].astype(BF16)
        hs_ref[...] = hs
        ys_ref[...] = _block_contract(hs, cc_ref) + d_ref[...] * us

    return _out_hbm(pl.pallas_call(
        body, name="s5conv_fwd",
        grid=(N_CHUNKS,),
        in_specs=[pl.BlockSpec((CHUNK, IN_COLS), lambda c: (c, 0)),
                  pl.BlockSpec((3, CONV_W), lambda c: (0, 0)),
                  pl.BlockSpec((1, CONV_W), lambda c: (0, 0)),
                  pl.BlockSpec((SSM_W, 2 * N_STATE), lambda c: (0, 0)),
                  pl.BlockSpec((SSM_W, 2 * N_STATE), lambda c: (0, 0)),
                  pl.BlockSpec((1, SSM_W), lambda c: (0, 0)),
                  pl.BlockSpec((32, 2 * N_STATE), lambda c: (0, 0))],
        out_specs=[pl.BlockSpec((CHUNK, CONV_W), lambda c: (c, 0)),
                   pl.BlockSpec((CHUNK, SSM_W), lambda c: (c, 0)),
                   pl.BlockSpec((CHUNK, 2 * N_STATE), lambda c: (c, 0))],
        out_shape=[pltpu.HBM((SEQ, CONV_W), F32),
                   pltpu.HBM((SEQ, SSM_W), F32),
                   jax.ShapeDtypeStruct((SEQ, 2 * N_STATE), BF16)],
        scratch_shapes=[pltpu.VMEM((CHUNK, 2 * N_STATE), F32),
                        pltpu.VMEM((8, 2 * N_STATE), F32),
                        pltpu.VMEM((8, CONV_W), F32)],
        compiler_params=_cparams(("arbitrary",)),
    )(*_in_hbm(z, convw, convb, bbmat, ccmat, dvec, ltab)), (0, 1))


def _s5conv_bwd(z, hs, dya, dys, convw, convb, bbmat, ccmat, dvec, ltab_rev):
    nc = N_CHUNKS
    hb = 16

    def body(z_ref, zp_ref, hs_ref, hp_ref, dya_ref, dys_ref, cw_ref, cb_ref, bb_ref, cc_ref, d_ref, lt_ref,
             dz_ref, g_ref, us_ref, dyb_ref, dl_ref, dcw_ref, work, carry, head):
        i = pl.program_id(0)
        first_chunk = i == nc - 1

        @pl.when(i == 0)
        def _():
            carry[...] = jnp.zeros_like(carry)
            head[...] = jnp.zeros_like(head)
            dl_ref[...] = jnp.zeros_like(dl_ref)
            dcw_ref[...] = jnp.zeros_like(dcw_ref)

        us = z_ref[:, 3 * CONV_W:4 * CONV_W]
        dy = dys_ref[...]
        dy_bf = dy.astype(BF16)
        us_ref[0] = us.astype(BF16)
        dyb_ref[0] = dy_bf

        _block_expand(dy_bf, cc_ref, work)
        _scan_chunk(work, lt_ref, carry, reverse=True)
        gg = work[...]
        gg_bf = gg.astype(BF16)
        g_ref[0] = gg_bf
        dus = d_ref[...] * dy + _block_contract(gg_bf, bb_ref)

        hcur = hs_ref[...].astype(F32)
        hlast = hp_ref[...].astype(F32)[hb - 1:hb, :]
        hlast = jnp.where(first_chunk, 0.0, hlast)
        rid = _row_ids(CHUNK, 2 * N_STATE)
        hprev = jnp.where(rid == 0, jnp.broadcast_to(hlast, (CHUNK, 2 * N_STATE)), pltpu.roll(hcur, 1, 0))
        gr, gi = gg[:, :N_STATE], gg[:, N_STATE:]
        hr, hi = hprev[:, :N_STATE], hprev[:, N_STATE:]
        dl_ref[:, :N_STATE] += (gr * hr + gi * hi).reshape(CHUNK // 8, 8, N_STATE).sum(axis=0)
        dl_ref[:, N_STATE:] += (gi * hr - gr * hi).reshape(CHUNK // 8, 8, N_STATE).sum(axis=0)

        @pl.when(i == nc - 1)
        def _():
            dl_ref[0:1, :] = jnp.sum(dl_ref[...], axis=0, keepdims=True)

        zb = z_ref[:, 0:CONV_W]
        zc = z_ref[:, CONV_W:2 * CONV_W]
        zv = z_ref[:, 2 * CONV_W:3 * CONV_W]
        v = zc * zv
        vtail = jnp.where(first_chunk, 0.0, zp_ref[:, CONV_W:2 * CONV_W] * zp_ref[:, 2 * CONV_W:3 * CONV_W])
        v1, v2 = _conv_taps(v, vtail)
        w0, w1, w2 = cw_ref[0:1, :], cw_ref[1:2, :], cw_ref[2:3, :]
        y = w0 * v2 + w1 * v1 + w2 * v
        dya_v = dya_ref[...]
        dzb = dya_v * (y + cb_ref[...])
        dyc = dya_v * zb
        hd = head[...]
        rc = _row_ids(CHUNK, CONV_W)
        n1 = jnp.where(rc == CHUNK - 1, _bcast_row(hd, 0, CHUNK), pltpu.roll(dyc, CHUNK - 1, 0))
        n2 = jnp.where(rc == CHUNK - 1, _bcast_row(hd, 1, CHUNK),
                       jnp.where(rc == CHUNK - 2, _bcast_row(hd, 0, CHUNK), pltpu.roll(dyc, CHUNK - 2, 0)))
        head[...] = dyc[0:8, :]
        dv = w2 * dyc + w1 * n1 + w0 * n2
        dz_ref[:, 0:CONV_W] = dzb.astype(BF16)
        dz_ref[:, CONV_W:2 * CONV_W] = (dv * zv).astype(BF16)
        dz_ref[:, 2 * CONV_W:3 * CONV_W] = (dv * zc).astype(BF16)
        dz_ref[:, 3 * CONV_W:4 * CONV_W] = dus.astype(BF16)
        dcw_ref[0:1, :] += jnp.sum(dyc * v2, axis=0, keepdims=True)
        dcw_ref[1:2, :] += jnp.sum(dyc * v1, axis=0, keepdims=True)
        dcw_ref[2:3, :] += jnp.sum(dyc * v, axis=0, keepdims=True)
        dcw_ref[3:4, :] += jnp.sum(dyc, axis=0, keepdims=True)
        dcw_ref[4:5, :] += jnp.sum(dy * us, axis=0, keepdims=True)

    rev = lambda i: nc - 1 - i
    return _out_hbm(pl.pallas_call(
        body, name="s5conv_bwd",
        grid=(nc,),
        in_specs=[pl.BlockSpec((CHUNK, IN_COLS), lambda i: (rev(i), 0)),
                  pl.BlockSpec((8, IN_COLS), lambda i: (jnp.maximum(rev(i) * (CHUNK // 8) - 1, 0), 0)),
                  pl.BlockSpec((CHUNK, 2 * N_STATE), lambda i: (rev(i), 0)),
                  pl.BlockSpec((hb, 2 * N_STATE), lambda i: (jnp.maximum(rev(i) * (CHUNK // hb) - 1, 0), 0)),
                  pl.BlockSpec((CHUNK, CONV_W), lambda i: (rev(i), 0)),
                  pl.BlockSpec((CHUNK, SSM_W), lambda i: (rev(i), 0)),
                  pl.BlockSpec((3, CONV_W), lambda i: (0, 0)),
                  pl.BlockSpec((1, CONV_W), lambda i: (0, 0)),
                  pl.BlockSpec((SSM_W, 2 * N_STATE), lambda i: (0, 0)),
                  pl.BlockSpec((SSM_W, 2 * N_STATE), lambda i: (0, 0)),
                  pl.BlockSpec((1, SSM_W), lambda i: (0, 0)),
                  pl.BlockSpec((32, 2 * N_STATE), lambda i: (0, 0))],
        out_specs=[pl.BlockSpec((CHUNK, IN_COLS), lambda i: (rev(i), 0)),
                   pl.BlockSpec((1, CHUNK, 2 * N_STATE), lambda i: (0, rev(i), 0)),
                   pl.BlockSpec((1, CHUNK, SSM_W), lambda i: (0, rev(i), 0)),
                   pl.BlockSpec((1, CHUNK, SSM_W), lambda i: (0, rev(i), 0)),
                   pl.BlockSpec((8, 2 * N_STATE), lambda i: (0, 0)),
                   pl.BlockSpec((8, CONV_W), lambda i: (0, 0))],
        out_shape=[jax.ShapeDtypeStruct((SEQ, IN_COLS), BF16),
                   pltpu.HBM((1, SEQ, 2 * N_STATE), BF16),
                   pltpu.HBM((1, SEQ, SSM_W), BF16),
                   pltpu.HBM((1, SEQ, SSM_W), BF16),
                   jax.ShapeDtypeStruct((8, 2 * N_STATE), F32),
                   jax.ShapeDtypeStruct((8, CONV_W), F32)],
        scratch_shapes=[pltpu.VMEM((CHUNK, 2 * N_STATE), F32),
                        pltpu.VMEM((8, 2 * N_STATE), F32),
                        pltpu.VMEM((8, CONV_W), F32)],
        compiler_params=_cparams(("arbitrary",)),
    )(*_in_hbm(z, z, hs, hs, dya, dys, convw, convb, bbmat, ccmat, dvec, ltab_rev)), (1, 2, 3))


def _mix_out_fwd(h, ya, ys, gluw, glub, con, son, wout):
    tm = TOK_TILE

    def body(h_ref, ya_ref, ys_ref, gw_ref, gb_ref, con_ref, son_ref, wo_ref, o_ref):
        zg, _ = _gelu(ys_ref[...])
        q = _dot(zg.astype(BF16), gw_ref[...]) + gb_ref[...]
        out_s = zg * _sigmoid(q)
        na, _ = _rms_stats(ya_ref[...])
        ns, _ = _rms_stats(out_s)
        o_ref[...] = (h_ref[...]
                      + _dot((na * con_ref[...]).astype(BF16), wo_ref[0:CONV_W, :])
                      + _dot((ns * son_ref[...]).astype(BF16), wo_ref[CONV_W:2 * CONV_W, :]))

    row = lambda m: (m, 0)
    fixed = lambda m: (0, 0)
    return pl.pallas_call(
        body, name="mix_out_fwd",
        grid=(SEQ // tm,),
        in_specs=[pl.BlockSpec((tm, D_MODEL), row), pl.BlockSpec((tm, CONV_W), row), pl.BlockSpec((tm, SSM_W), row),
                  pl.BlockSpec((SSM_W, SSM_W), fixed), pl.BlockSpec((1, SSM_W), fixed),
                  pl.BlockSpec((1, CONV_W), fixed), pl.BlockSpec((1, SSM_W), fixed),
                  pl.BlockSpec((None, D_MODEL, D_MODEL), lambda m: (0, 0, 0))],
        out_specs=pl.BlockSpec((tm, D_MODEL), row),
        out_shape=jax.ShapeDtypeStruct((SEQ, D_MODEL), F32),
        compiler_params=_cparams(("parallel",)),
    )(*_in_hbm(h, ya, ys, gluw, glub, con, son, wout))


def _mix_out_bwd(dh, ya, ys, gluw, glub, con, son, wout):
    tm = TOK_TILE

    def body(dh_ref, ya_ref, ys_ref, gw_ref, gb_ref, con_ref, son_ref, wo_ref,
             dya_ref, dys_ref, yc_ref, dhb_ref, zg_ref, dq_ref, part_ref):
        ysv = ys_ref[...]
        zg, th = _gelu(ysv)
        zg_bf = zg.astype(BF16)
        s = _sigmoid(_dot(zg_bf, gw_ref[...]) + gb_ref[...])
        out_s = zg * s
        na, ra = _rms_stats(ya_ref[...])
        ns, rs = _rms_stats(out_s)
        dh_bf = dh_ref[...].astype(BF16)
        yc_ref[0, :, 0:CONV_W] = (na * con_ref[...]).astype(BF16)
        yc_ref[0, :, CONV_W:2 * CONV_W] = (ns * son_ref[...]).astype(BF16)
        dhb_ref[0] = dh_bf
        dca = _dot_nt(dh_bf, wo_ref[0:CONV_W, :])
        dcs = _dot_nt(dh_bf, wo_ref[CONV_W:2 * CONV_W, :])
        dya, dcon = _rms_bwd(dca, na, ra, con_ref[...])
        dos, dson = _rms_bwd(dcs, ns, rs, son_ref[...])
        dya_ref[...] = dya
        dq = dos * zg * s * (1.0 - s)
        dq_bf = dq.astype(BF16)
        dzg = dos * s + _dot_nt(dq_bf, gw_ref[...])
        dys_ref[...] = dzg * _gelu_grad(ysv, th)
        zg_ref[0] = zg_bf
        dq_ref[0] = dq_bf
        rid = _row_ids(SUBLANES, SSM_W)
        part = jnp.zeros((SUBLANES, SSM_W), F32)
        for i, rowv in enumerate((dcon, dson, jnp.sum(dq, axis=0, keepdims=True))):
            part = jnp.where(rid == i, jnp.broadcast_to(rowv, (SUBLANES, SSM_W)), part)
        _accumulate(part_ref, pl.program_id(0) == 0, part)

    row = lambda m: (m, 0)
    fixed = lambda m: (0, 0)
    lead = lambda m: (0, m, 0)
    return _out_hbm(pl.pallas_call(
        body, name="mix_out_bwd",
        grid=(SEQ // tm,),
        in_specs=[pl.BlockSpec((tm, D_MODEL), row), pl.BlockSpec((tm, CONV_W), row), pl.BlockSpec((tm, SSM_W), row),
                  pl.BlockSpec((SSM_W, SSM_W), fixed), pl.BlockSpec((1, SSM_W), fixed),
                  pl.BlockSpec((1, CONV_W), fixed), pl.BlockSpec((1, SSM_W), fixed),
                  pl.BlockSpec((None, D_MODEL, D_MODEL), lambda m: (0, 0, 0))],
        out_specs=[pl.BlockSpec((tm, CONV_W), row), pl.BlockSpec((tm, SSM_W), row),
                   pl.BlockSpec((1, tm, D_MODEL), lead), pl.BlockSpec((1, tm, D_MODEL), lead),
                   pl.BlockSpec((1, tm, SSM_W), lead), pl.BlockSpec((1, tm, SSM_W), lead),
                   pl.BlockSpec((8, SSM_W), fixed)],
        out_shape=[pltpu.HBM((SEQ, CONV_W), F32), pltpu.HBM((SEQ, SSM_W), F32),
                   pltpu.HBM((1, SEQ, D_MODEL), BF16), pltpu.HBM((1, SEQ, D_MODEL), BF16),
                   pltpu.HBM((1, SEQ, SSM_W), BF16), pltpu.HBM((1, SEQ, SSM_W), BF16),
                   jax.ShapeDtypeStruct((8, SSM_W), F32)],
        compiler_params=_cparams(("arbitrary",)),
    )(*_in_hbm(dh, ya, ys, gluw, glub, con, son, wout)), (0, 1, 2, 3, 4, 5))


def _ple_fwd(h, g, p, wgate, wprojt):
    tm = TOK_TILE

    def body(h_ref, g_ref, p_ref, wg_ref, wp_ref, o_ref):
        x = h_ref[...]
        xh, _ = _rms_stats(x)
        s = _sigmoid(_dot((xh * g_ref[...]).astype(BF16), wg_ref[...]))
        o_ref[...] = x + _dot_nt(p_ref[...].astype(BF16), wp_ref[...]) * s

    row = lambda m: (m, 0)
    fixed = lambda m: (0, 0)
    return pl.pallas_call(
        body, name="ple_fwd",
        grid=(SEQ // tm,),
        in_specs=[pl.BlockSpec((tm, D_MODEL), row), pl.BlockSpec((1, D_MODEL), fixed), pl.BlockSpec((tm, PLE_DIM), row),
                  pl.BlockSpec((None, D_MODEL, D_MODEL), lambda m: (0, 0, 0)), pl.BlockSpec((D_MODEL, PLE_DIM), fixed)],
        out_specs=pl.BlockSpec((tm, D_MODEL), row),
        out_shape=jax.ShapeDtypeStruct((SEQ, D_MODEL), F32),
        compiler_params=_cparams(("parallel",)),
    )(*_in_hbm(h, g, p, wgate, wprojt))


def _ple_bwd(h, g, p, dh, wgate, wprojt):
    tm = TOK_TILE

    def body(h_ref, g_ref, p_ref, dh_ref, wg_ref, wp_ref, o_ref, u_ref, dq_ref, dpp_ref, pb_ref, dg_ref):
        xh, r = _rms_stats(h_ref[...])
        u = (xh * g_ref[...]).astype(BF16)
        s = _sigmoid(_dot(u, wg_ref[...]))
        p_bf = p_ref[...].astype(BF16)
        pp = _dot_nt(p_bf, wp_ref[...])
        dhv = dh_ref[...]
        dq = (dhv * pp * s * (1.0 - s)).astype(BF16)
        u_ref[0] = u
        dq_ref[0] = dq
        dpp_ref[0] = (dhv * s).astype(BF16)
        pb_ref[0] = p_bf
        dx, dg = _rms_bwd(_dot_nt(dq, wg_ref[...]), xh, r, g_ref[...])
        o_ref[...] = dhv + dx
        _accumulate(dg_ref, pl.program_id(0) == 0, dg)

    row = lambda m: (m, 0)
    fixed = lambda m: (0, 0)
    lead = lambda m: (0, m, 0)
    big = pltpu.HBM((1, SEQ, D_MODEL), BF16)
    return _out_hbm(pl.pallas_call(
        body, name="ple_bwd",
        grid=(SEQ // tm,),
        in_specs=[pl.BlockSpec((tm, D_MODEL), row), pl.BlockSpec((1, D_MODEL), fixed), pl.BlockSpec((tm, PLE_DIM), row),
                  pl.BlockSpec((tm, D_MODEL), row),
                  pl.BlockSpec((None, D_MODEL, D_MODEL), lambda m: (0, 0, 0)), pl.BlockSpec((D_MODEL, PLE_DIM), fixed)],
        out_specs=[pl.BlockSpec((tm, D_MODEL), row),
                   pl.BlockSpec((1, tm, D_MODEL), lead), pl.BlockSpec((1, tm, D_MODEL), lead),
                   pl.BlockSpec((1, tm, D_MODEL), lead), pl.BlockSpec((1, tm, PLE_DIM), lead),
                   pl.BlockSpec((1, D_MODEL), fixed)],
        out_shape=[jax.ShapeDtypeStruct((SEQ, D_MODEL), F32), big, big, big,
                   pltpu.HBM((1, SEQ, PLE_DIM), BF16),
                   jax.ShapeDtypeStruct((1, D_MODEL), F32)],
        compiler_params=_cparams(("arbitrary",)),
    )(*_in_hbm(h, g, p, dh, wgate, wprojt)), (1, 2, 3, 4))


def _final_loss(h, g, target):
    tm = TOK_TILE

    def body(h_ref, g_ref, t_ref, loss_ref, dh_ref, dg_ref):
        first = pl.program_id(0) == 0
        xh, r = _rms_stats(h_ref[...])
        diff = xh * g_ref[...] - t_ref[...]
        part = 0.5 * jnp.sum(jnp.mean(diff * diff, axis=-1, keepdims=True), axis=0, keepdims=True)
        _accumulate(loss_ref, first, jnp.broadcast_to(part, (SUBLANES, LANES)))
        dx, dg = _rms_bwd(diff * (1.0 / D_MODEL), xh, r, g_ref[...])
        dh_ref[...] = dx
        _accumulate(dg_ref, first, dg)

    row = lambda m: (m, 0)
    fixed = lambda m: (0, 0)
    return pl.pallas_call(
        body, name="final_loss",
        grid=(SEQ // tm,),
        in_specs=[pl.BlockSpec((tm, D_MODEL), row), pl.BlockSpec((1, D_MODEL), fixed),
                  pl.BlockSpec((tm, D_MODEL), row)],
        out_specs=[pl.BlockSpec((SUBLANES, LANES), fixed),
                   pl.BlockSpec((tm, D_MODEL), row),
                   pl.BlockSpec((1, D_MODEL), fixed)],
        out_shape=[jax.ShapeDtypeStruct((SUBLANES, LANES), F32),
                   jax.ShapeDtypeStruct((SEQ, D_MODEL), F32),
                   jax.ShapeDtypeStruct((1, D_MODEL), F32)],
        compiler_params=_cparams(("arbitrary",)),
    )(*_in_hbm(h, g, target))


def _disc(ar, ai, ldt):
    dt = jnp.exp(ldt)
    mag = jnp.exp(ar * dt)
    ph = ai * dt
    lr, li = mag * jnp.cos(ph), mag * jnp.sin(ph)
    nr, ni = lr - 1.0, li
    den = ar * ar + ai * ai
    return lr, li, (nr * ar + ni * ai) / den, (ni * ar - nr * ai) / den


def _s5_disc(a_row, ldt_row, a_rep, ldt_rep, bt, ct, tile_e, mask):
    n = N_STATE

    def body(ar_ref, lr_ref, ap_ref, lp_ref, b_ref, c_ref, e_ref, m_ref, lt_ref, ltr_ref, bb_ref, cc_ref):
        lr, li, _, _ = _disc(ar_ref[0], ar_ref[1], lr_ref[...])
        pr, pi = lr, li
        rid = _row_ids(SUBLANES, n)
        for k in range(1, 9):
            for ref, sgn, edge in ((lt_ref, 1.0, 24 + k - 1), (ltr_ref, -1.0, 24 + 8 - k)):
                if k in (1, 2, 4):
                    r0 = {1: 0, 2: 8, 4: 16}[k]
                    keep = (rid >= k) if ref is lt_ref else (rid < SUBLANES - k)
                    ref[r0:r0 + 8, 0:n] = jnp.where(keep, jnp.broadcast_to(pr, (8, n)), 0.0)
                    ref[r0:r0 + 8, n:2 * n] = jnp.where(keep, jnp.broadcast_to(sgn * pi, (8, n)), 0.0)
                ref[edge:edge + 1, 0:n] = pr
                ref[edge:edge + 1, n:2 * n] = sgn * pi
            pr, pi = pr * lr - pi * li, pr * li + pi * lr
        _, _, fr, fi = _disc(ap_ref[0], ap_ref[1], lp_ref[...])
        br, bi = b_ref[0], b_ref[1]
        e = e_ref[...]
        m = m_ref[...].astype(F32)
        bb_ref[:, 0:n] = (_dot((fr * br - fi * bi).astype(BF16), e) * m).astype(BF16)
        bb_ref[:, n:2 * n] = (_dot((fr * bi + fi * br).astype(BF16), e) * m).astype(BF16)
        cc_ref[:, 0:n] = (_dot(c_ref[0].astype(BF16), e) * m).astype(BF16)
        cc_ref[:, n:2 * n] = (-(_dot(c_ref[1].astype(BF16), e) * m)).astype(BF16)

    return pl.pallas_call(
        body, name="s5_disc",
        out_shape=[jax.ShapeDtypeStruct((32, 2 * n), F32), jax.ShapeDtypeStruct((32, 2 * n), F32),
                   jax.ShapeDtypeStruct((SSM_W, 2 * n), BF16), jax.ShapeDtypeStruct((SSM_W, 2 * n), BF16)],
        compiler_params=_cparams(None),
    )(a_row, ldt_row, a_rep, ldt_rep, bt, ct, tile_e, mask)


def _dot_exact(x, sel):
    hi = x.astype(BF16)
    r1 = x - hi.astype(F32)
    mid = r1.astype(BF16)
    lo = (r1 - mid.astype(F32)).astype(BF16)
    return _dot(hi, sel) + _dot(mid, sel) + _dot(lo, sel)


def _s5_disc_bwd(a, ldt, a_rep, ldt_rep, bt, mask, dl, d_bb, d_cc, fold):
    n = N_STATE

    def body(a_ref, l_ref, ap_ref, lp_ref, b_ref, m_ref, dl_ref, dbb_ref, dcc_ref, f_ref,
             da_ref, dldt_ref, db_ref, dc_ref):
        m = m_ref[...].astype(F32)
        fold_m = f_ref[...]
        diag = lambda x: _dot_exact(jnp.where(m > 0.0, x, 0.0), fold_m)
        dr, di = diag(dbb_ref[:, 0:n]), diag(dbb_ref[:, n:2 * n])
        dc_ref[0] = diag(dcc_ref[:, 0:n])
        dc_ref[1] = -diag(dcc_ref[:, n:2 * n])
        _, _, fr, fi = _disc(ap_ref[0], ap_ref[1], lp_ref[...])
        br, bi = b_ref[0], b_ref[1]
        db_ref[0] = fr * dr + fi * di
        db_ref[1] = fr * di - fi * dr
        per_state = lambda x: x.reshape(SSM_GROUPS, SSM_GROUP, SSM_STATE).sum(axis=1)
        dfr = per_state(dr * br + di * bi)
        dfi = per_state(di * br - dr * bi)
        _, vjp = jax.vjp(_disc, a_ref[0], a_ref[1], l_ref[...])
        dar, dai, dldt = vjp((dl_ref[0], dl_ref[1], dfr, dfi))
        da_ref[0] = dar
        da_ref[1] = dai
        dldt_ref[...] = jnp.sum(dldt, axis=1, keepdims=True)

    return pl.pallas_call(
        body, name="s5_disc_bwd",
        out_shape=[jax.ShapeDtypeStruct((2, SSM_GROUPS, SSM_STATE), F32),
                   jax.ShapeDtypeStruct((SSM_GROUPS, 1), F32),
                   jax.ShapeDtypeStruct((2, SSM_W, SSM_STATE), F32),
                   jax.ShapeDtypeStruct((2, SSM_W, SSM_STATE), F32)],
        compiler_params=_cparams(None),
    )(a, ldt, a_rep, ldt_rep, bt, mask, dl, d_bb, d_cc, fold)


def _row_block(rows, cap=512):
    for bm in range(min(cap, rows), 0, -1):
        if rows % bm == 0 and (bm % 8 == 0 or bm == rows):
            return bm
    return rows


SUM_PARTS = 2


def _own_pieces(segs, rtot):
    pr = rtot // SUM_PARTS
    assert pr * SUM_PARTS == rtot and pr % 16 == 0
    offs = _seg_offsets(segs)
    pieces = [[] for _ in range(SUM_PARTS)]
    for a, (n, r) in enumerate(segs):
        for m in range(n):
            lo = offs[a] + m * r
            for h in range(SUM_PARTS):
                clo, chi = max(lo, h * pr), min(lo + r, (h + 1) * pr)
                if chi > clo:
                    pieces[h].append((a, m, clo - lo, clo - h * pr, chi - clo))
    return pieces


def _pair_rows(srcs, got_ref, segs, pieces, h, chip, own_v, got_v, sems):
    pr = own_v.shape[0]
    dev = 2 * chip + lax.axis_index("c")
    for hh in range(SUM_PARTS):
        @pl.when(h == hh)
        def _(hh=hh):
            cps = [pltpu.make_async_copy(got_ref.at[chip, pl.ds(hh * pr, pr), :], got_v, sems.at[0])]
            for i, (a, m, so, do, rows) in enumerate(pieces[hh]):
                start = pl.multiple_of(dev * segs[a][1] + so, 16)
                cps.append(pltpu.make_async_copy(srcs[a].at[m, pl.ds(start, rows), :],
                                                 own_v.at[pl.ds(do, rows), :], sems.at[1 + i]))
            for cp in cps:
                cp.start()
            for cp in cps:
                cp.wait()
    return own_v[...].astype(F32) + got_v[...].astype(F32)


def _pair_sum(fulls, got, segs):
    ns = len(segs)
    _, rtot, c = got.shape
    pieces = _own_pieces(segs, rtot)
    pr = rtot // SUM_PARTS

    def body(*refs):
        srcs = refs[:ns]
        got_ref, pbf_ref, own_v, got_v, sems = refs[ns:]
        x, y, _ = _mesh_pos()
        j = pl.program_id(1)
        chip = jnp.where(j == 0, 2 * (1 - x) + y, jnp.where(j == 1, 2 * x + 1 - y, 2 * (1 - x) + 1 - y))
        pbf_ref[0] = _pair_rows(srcs, got_ref, segs, pieces, pl.program_id(0), chip, own_v, got_v, sems).astype(BF16)

    return pl.pallas_call(
        body, name="pair_sum",
        grid=(SUM_PARTS, 3),
        in_specs=[HBM] * (ns + 1), out_specs=pl.BlockSpec((1, pr, c), lambda h, j: (j, h, 0)),
        out_shape=pltpu.HBM((3, rtot, c), BF16),
        scratch_shapes=[pltpu.VMEM((pr, c), BF16), pltpu.VMEM((pr, c), BF16),
                        pltpu.SemaphoreType.DMA((1 + max(len(p) for p in pieces),))],
        compiler_params=_cparams(("arbitrary", "arbitrary")),
    )(*_in_hbm(*fulls, got))


def _chip_sum(fulls, got, rb, segs, layer, into):
    ns = len(segs)
    _, rtot, c = got.shape
    pieces = _own_pieces(segs, rtot)
    pr = rtot // SUM_PARTS

    def body(*refs):
        srcs = refs[:ns]
        got_ref, r_ref = refs[ns], refs[ns + 1]
        s_ref, own_v, got_v, sems = refs[-4:]
        x, y, _ = _mesh_pos()
        own = _pair_rows(srcs, got_ref, segs, pieces, pl.program_id(0), 2 * x + y, own_v, got_v, sems)
        s_ref[0] = ((own + r_ref[0].astype(F32)) + r_ref[1].astype(F32)) + r_ref[2].astype(F32)

    old = [] if into is None else [into]
    return pl.pallas_call(
        body, name="chip_sum",
        grid=(SUM_PARTS,),
        in_specs=[HBM] * (ns + 1) + [pl.BlockSpec((3, pr, c), lambda h: (0, h, 0))] + [HBM] * len(old),
        out_specs=pl.BlockSpec((1, pr, c), lambda h: (layer, h, 0)),
        out_shape=jax.ShapeDtypeStruct((DEPTH, rtot, c), F32),
        input_output_aliases={ns + 2: 0} if old else {},
        scratch_shapes=[pltpu.VMEM((pr, c), BF16), pltpu.VMEM((pr, c), BF16),
                        pltpu.SemaphoreType.DMA((1 + max(len(p) for p in pieces),))],
        compiler_params=_cparams(("arbitrary",)),
    )(*_in_hbm(*fulls, got, rb), *old)


def _adamw(w, g, m, v):
    r, c = w.shape
    bm = _row_block(r)
    bc1 = 1.0 - ADAM_B1 ** ADAM_STEP
    bc2 = 1.0 - ADAM_B2 ** ADAM_STEP

    def body(w_ref, g_ref, m_ref, v_ref, d_ref, nm_ref, nv_ref):
        gv = g_ref[...]
        nm = ADAM_B1 * m_ref[...] + (1.0 - ADAM_B1) * gv
        nv = ADAM_B2 * v_ref[...] + (1.0 - ADAM_B2) * (gv * gv)
        nm_ref[...] = nm
        nv_ref[...] = nv
        d_ref[...] = -ADAM_LR * ((nm / bc1) / (jnp.sqrt(nv / bc2) + ADAM_EPS) + ADAM_WD * w_ref[...])

    spec = pl.BlockSpec((bm, c), lambda k: (k, 0))
    shp = jax.ShapeDtypeStruct((r, c), F32)
    return pl.pallas_call(
        body, name="adamw",
        grid=(r // bm,),
        in_specs=[spec] * 4, out_specs=[spec] * 3, out_shape=[shp] * 3,
        compiler_params=_cparams(("parallel",)),
    )(*_in_hbm(w, g, m, v))


def _adamw_layers(sets, first, nl, prev):
    ns = len(sets)
    depth, r, c = sets[0][0].shape
    bm = _row_block(r, min(512, max(SUBLANES, (24 * MIB) // (ns * 8 * 2 * c * 4))))
    while any(four[4] is not None and four[4] % bm for four in sets):
        bm //= 2
    assert bm % SUBLANES == 0 and r % bm == 0
    bc1 = 1.0 - ADAM_B1 ** ADAM_STEP
    bc2 = 1.0 - ADAM_B2 ** ADAM_STEP

    def body(*refs):
        outs = refs[len(refs) - 4 * ns:]
        for s in range(ns):
            w_ref, m_ref, v_ref, g_ref = refs[4 * s:4 * s + 4]
            go_ref, d_ref, nm_ref, nv_ref = outs[4 * s:4 * s + 4]
            gv = g_ref[...]
            nm = ADAM_B1 * m_ref[...] + (1.0 - ADAM_B1) * gv
            nv = ADAM_B2 * v_ref[...] + (1.0 - ADAM_B2) * (gv * gv)
            go_ref[...] = gv
            nm_ref[...] = nm
            nv_ref[...] = nv
            d_ref[...] = -ADAM_LR * ((nm / bc1) / (jnp.sqrt(nv / bc2) + ADAM_EPS) + ADAM_WD * w_ref[...])

    at = pl.BlockSpec((1, bm, c), lambda i, k: (first + i, k, 0))

    def grad_spec(g_rows):
        if g_rows is None:
            return pl.BlockSpec((1, bm, c), lambda i, k: (i, k, 0))
        return pl.BlockSpec((1, bm, c), lambda i, k: (first + i, g_rows // bm + k, 0))

    shp = jax.ShapeDtypeStruct((depth, r, c), F32)
    old = [] if prev is None else [a for four in prev for a in four]
    flat = pl.pallas_call(
        body, name="adamw_layers",
        grid=(nl, r // bm),
        in_specs=[spec for four in sets for spec in (at, at, at, grad_spec(four[4]))] + [HBM] * len(old),
        out_specs=[at] * (4 * ns), out_shape=[shp] * (4 * ns),
        input_output_aliases={4 * ns + i: i for i in range(len(old))},
        compiler_params=_cparams(("parallel", "parallel")),
    )(*_in_hbm(*[a for four in sets for a in four[:4]]), *old)
    return [flat[4 * s:4 * s + 4] for s in range(ns)]


def _mesh_pos():
    return lax.axis_index("x"), lax.axis_index("y"), lax.axis_index("c")


def _dev_index(p):
    return 4 * p[0] + 2 * p[1] + p[2]


def _seg_offsets(segs):
    offs, o = [], 0
    for n, r in segs:
        offs.append(o)
        o += n * r
    return offs


def _remote(src, dst, send_sem, recv_sem, to):
    return pltpu.make_async_remote_copy(src_ref=src, dst_ref=dst, send_sem=send_sem, recv_sem=recv_sem,
                                        device_id=to, device_id_type=MESH)


def _allgather(pack, segs, name):
    rtot, c = pack.shape
    ns = len(segs)
    offs = _seg_offsets(segs)
    assert rtot == sum(n * r for n, r in segs)

    def body(pack_ref, *refs):
        outs = refs[:ns]
        send_sems, recv_sems, local_sem = refs[ns:]
        x, y, cc = _mesh_pos()
        me, sib = (x, y, cc), (x, y, 1 - cc)
        chips = [(1 - x, y), (x, 1 - y), (1 - x, 1 - y)]

        def pieces(dev, from_pack):
            res = []
            for a, (n, r) in enumerate(segs):
                for m in range(n):
                    dst = outs[a].at[m, pl.ds(pl.multiple_of(dev * r, r), r), :]
                    src = pack_ref.at[pl.ds(offs[a] + m * r, r), :] if from_pack else dst
                    res.append((src, dst))
            return res

        def push(k, dev, to, from_pack):
            for s, d in pieces(dev, from_pack):
                _remote(s, d, send_sems.at[k], recv_sems.at[k], to).start()

        def whole(k):
            return _remote(pack_ref, pack_ref, send_sems.at[k], recv_sems.at[k], me)

        my_dev = _dev_index(me)
        for s, d in pieces(my_dev, True):
            pltpu.make_async_copy(s, d, local_sem).start()
        push(0, my_dev, sib, True)
        for j, chip in enumerate(chips):
            push(1 + j, my_dev, (*chip, cc), True)
        for j, chip in enumerate(chips):
            whole(1 + j).wait_recv()
            push(4 + j, _dev_index((*chip, cc)), sib, False)
        whole(0).wait_recv()
        for j in range(3):
            whole(4 + j).wait_recv()
        for k in range(7):
            whole(k).wait_send()
        pltpu.make_async_copy(pack_ref, pack_ref, local_sem).wait()

    return pl.pallas_call(
        body, name=name,
        in_specs=[HBM], out_specs=[HBM] * ns,
        out_shape=[jax.ShapeDtypeStruct((n, N_DEV * r, c), pack.dtype) for n, r in segs],
        scratch_shapes=[pltpu.SemaphoreType.DMA((7,)), pltpu.SemaphoreType.DMA((7,)), pltpu.SemaphoreType.DMA],
    )(pack)


HBM = pl.BlockSpec(memory_space=pltpu.HBM)
SEM = pl.BlockSpec(memory_space=pltpu.SEMAPHORE)
VMEM_WHOLE = pl.BlockSpec(memory_space=pltpu.VMEM)
EFFECT = pltpu.SideEffectType.DATAFLOW_SIDE_EFFECTING


def _hbm(a):
    return pltpu.with_memory_space_constraint(a, pltpu.HBM)


def _ag_start(pack, segs, after, name):
    rtot, c = pack.shape
    ns = len(segs)
    offs = _seg_offsets(segs)

    def body(pack_ref, *refs):
        lands = refs[:ns]
        send_sems, recv_sems = refs[ns + 1], refs[ns + 2]
        token = refs[-1]
        x, y, cc = _mesh_pos()
        my_dev = _dev_index((x, y, cc))
        targets = [(x, y, 1 - cc), (1 - x, y, cc), (x, 1 - y, cc), (1 - x, 1 - y, cc)]
        for k, to in enumerate(targets):
            for a, (n, r) in enumerate(segs):
                for m in range(n):
                    _remote(pack_ref.at[pl.ds(offs[a] + m * r, r), :],
                            lands[a].at[m, pl.ds(pl.multiple_of(my_dev * r, r), r), :],
                            send_sems.at[k], recv_sems.at[k], to).start()
        token[...] = jnp.zeros_like(token)

    land_shapes = [(n, N_DEV * r, c) for n, r in segs]
    outs = pl.pallas_call(
        body, name=name,
        in_specs=[HBM] * (1 + ns) + [UNREAD],
        out_specs=[SEM, SEM, HBM] + [HBM] * ns + [VMEM_WHOLE],
        out_shape=[pltpu.SemaphoreType.DMA((4,)), pltpu.SemaphoreType.DMA((4,)), pltpu.HBM(pack.shape, pack.dtype)]
        + [pltpu.HBM(s, pack.dtype) for s in land_shapes] + [jax.ShapeDtypeStruct((SUBLANES, LANES), F32)],
        input_output_aliases={0: 2, **{1 + i: 3 + i for i in range(ns)}},
        compiler_params=pltpu.CompilerParams(has_side_effects=EFFECT),
    )(_hbm(pack), *[_hbm(lax.empty(s, pack.dtype)) for s in land_shapes], _hbm(after))
    return outs[0], outs[1], outs[2], list(outs[3:3 + ns]), outs[-1]


def _ag_wait(send_sems, recv_sems, pack, lands, after, name):
    ns = len(lands)

    def body(pack_ref, *refs):
        send_ref, recv_ref = refs[ns], refs[ns + 1]
        me = _mesh_pos()
        for k in range(4):
            whole = _remote(pack_ref, pack_ref, send_ref.at[k], recv_ref.at[k], me)
            whole.wait_send()
            whole.wait_recv()

    outs = pl.pallas_call(
        body, name=name,
        in_specs=[HBM] * (1 + ns) + [SEM, SEM, UNREAD],
        out_specs=[HBM] * (1 + ns),
        out_shape=[pltpu.HBM(pack.shape, pack.dtype)] + [pltpu.HBM(a.shape, a.dtype) for a in lands],
        input_output_aliases={i: i for i in range(1 + ns)},
        compiler_params=pltpu.CompilerParams(has_side_effects=EFFECT),
    )(pack, *lands, send_sems, recv_sems, _hbm(after))
    return outs[0], list(outs[1:])


def _ag_finish(pack, lands, segs):
    rtot, c = pack.shape
    ns = len(segs)
    offs = _seg_offsets(segs)

    def body(pack_ref, *refs):
        outs = refs[ns:2 * ns]
        stage, send_sems, recv_sems, local_sems = refs[2 * ns:]
        x, y, cc = _mesh_pos()
        me, sib = (x, y, cc), (x, y, 1 - cc)
        chips = [(1 - x, y), (x, 1 - y), (1 - x, 1 - y)]

        def rows(a, m, dev):
            return outs[a].at[m, pl.ds(pl.multiple_of(dev * segs[a][1], segs[a][1]), segs[a][1]), :]

        for j, chip in enumerate(chips):
            dev = _dev_index((*chip, cc))
            for a, (n, r) in enumerate(segs):
                for m in range(n):
                    _remote(rows(a, m, dev), rows(a, m, dev), send_sems.at[j], recv_sems.at[j], sib).start()
        load = pltpu.make_async_copy(pack_ref, stage, local_sems.at[0])
        load.start()
        load.wait()
        my_dev = _dev_index(me)
        for a, (n, r) in enumerate(segs):
            for m in range(n):
                pltpu.make_async_copy(stage.at[pl.ds(offs[a] + m * r, r), :], rows(a, m, my_dev), local_sems.at[1]).start()
        pltpu.make_async_copy(stage, pack_ref, local_sems.at[1]).wait()
        for j in range(3):
            _remote(pack_ref, pack_ref, send_sems.at[j], recv_sems.at[j], me).wait()

    outs = pl.pallas_call(
        body, name="ag_finish",
        in_specs=[HBM] * (1 + ns), out_specs=[HBM] * ns,
        out_shape=[pltpu.HBM(a.shape, a.dtype) if r >= 128 else jax.ShapeDtypeStruct(a.shape, a.dtype)
                   for a, (_, r) in zip(lands, segs)],
        input_output_aliases={1 + i: i for i in range(ns)},
        scratch_shapes=[pltpu.VMEM((rtot, c), pack.dtype), pltpu.SemaphoreType.DMA((3,)),
                        pltpu.SemaphoreType.DMA((3,)), pltpu.SemaphoreType.DMA((2,))],
        compiler_params=_cparams(None, 16),
    )(pack, *lands)
    return list(outs)


def _rs_chips_start(pbf, after, name):
    _, rtot, c = pbf.shape

    def body(pbf_ref, land_ref, after_ref, send_sems, recv_sems, pbf_thru, land_thru, token):
        x, y, cc = _mesh_pos()
        for j, (cx, cy) in enumerate([(1 - x, y), (x, 1 - y), (1 - x, 1 - y)]):
            _remote(pbf_ref.at[j], land_ref.at[j], send_sems.at[j], recv_sems.at[j], (cx, cy, cc)).start()
        token[...] = jnp.zeros_like(token)

    return pl.pallas_call(
        body, name=name,
        in_specs=[HBM, HBM, UNREAD],
        out_specs=[SEM, SEM, HBM, HBM, VMEM_WHOLE],
        out_shape=[pltpu.SemaphoreType.DMA((3,)), pltpu.SemaphoreType.DMA((3,)), pltpu.HBM(pbf.shape, pbf.dtype),
                   pltpu.HBM((3, rtot, c), pbf.dtype), jax.ShapeDtypeStruct((SUBLANES, LANES), F32)],
        input_output_aliases={0: 2, 1: 3},
        compiler_params=pltpu.CompilerParams(has_side_effects=EFFECT),
    )(_hbm(pbf), _hbm(lax.empty((3, rtot, c), pbf.dtype)), _hbm(after))


def _rs_chips_wait(send_sems, recv_sems, pbf, land, after, name):
    def body(pbf_ref, land_ref, send_ref, recv_ref, after_ref, pbf_out, land_out):
        me = _mesh_pos()
        for j in range(3):
            cp = _remote(pbf_ref.at[0], land_ref.at[j], send_ref.at[j], recv_ref.at[j], me)
            cp.wait_send()
            cp.wait_recv()

    return pl.pallas_call(
        body, name=name,
        in_specs=[HBM, HBM, SEM, SEM, UNREAD], out_specs=[HBM, HBM],
        out_shape=[pltpu.HBM(pbf.shape, pbf.dtype), pltpu.HBM(land.shape, land.dtype)],
        input_output_aliases={0: 0, 1: 1},
        compiler_params=pltpu.CompilerParams(has_side_effects=EFFECT),
    )(pbf, land, send_sems, recv_sems, _hbm(after))[1]


def _flips():
    return [(dx, dy, dc) for dx in (0, 1) for dy in (0, 1) for dc in (0, 1) if dx or dy or dc]


def _small_gather_start(flat, name):
    r, c = flat.shape

    def body(flat_ref, land_ref, send_sems, recv_sems, flat_thru, land_thru, token):
        x, y, cc = _mesh_pos()
        mine = land_ref.at[_dev_index((x, y, cc))]
        for k, (dx, dy, dc) in enumerate(_flips()):
            to = (1 - x if dx else x, 1 - y if dy else y, 1 - cc if dc else cc)
            _remote(flat_ref, mine, send_sems.at[k], recv_sems.at[k], to).start()
        token[...] = jnp.zeros_like(token)

    return pl.pallas_call(
        body, name=name,
        in_specs=[HBM, HBM],
        out_specs=[SEM, SEM, HBM, HBM, VMEM_WHOLE],
        out_shape=[pltpu.SemaphoreType.DMA((7,)), pltpu.SemaphoreType.DMA((7,)), pltpu.HBM(flat.shape, flat.dtype),
                   pltpu.HBM((N_DEV, r, c), flat.dtype), jax.ShapeDtypeStruct((SUBLANES, LANES), F32)],
        input_output_aliases={0: 2, 1: 3},
        compiler_params=pltpu.CompilerParams(has_side_effects=EFFECT),
    )(_hbm(flat), _hbm(lax.empty((N_DEV, r, c), flat.dtype)))


def _small_gather_wait(send_sems, recv_sems, flat, land, after, name):
    def body(flat_ref, land_ref, send_ref, recv_ref, after_ref, flat_out, land_out):
        me = _mesh_pos()
        for k in range(N_DEV - 1):
            cp = _remote(flat_ref, land_ref.at[0], send_ref.at[k], recv_ref.at[k], me)
            cp.wait_send()
            cp.wait_recv()

    return pl.pallas_call(
        body, name=name,
        in_specs=[HBM, HBM, SEM, SEM, UNREAD], out_specs=[HBM, HBM],
        out_shape=[pltpu.HBM(flat.shape, flat.dtype), pltpu.HBM(land.shape, land.dtype)],
        input_output_aliases={0: 0, 1: 1},
        compiler_params=pltpu.CompilerParams(has_side_effects=EFFECT),
    )(flat, land, send_sems, recv_sems, _hbm(after))


def _sum_devices(land, own):
    _, r, c = land.shape

    def body(land_ref, own_ref, out_ref):
        me = _dev_index(_mesh_pos())
        total = None
        for d in range(N_DEV):
            other = land_ref[jnp.where(d == me, (d + 1) % N_DEV, d)]
            block = jnp.where(d == me, own_ref[...], other)
            total = block if total is None else total + block
        out_ref[...] = total

    return pl.pallas_call(
        body, name="sum_devices",
        grid=(1,),
        in_specs=[pl.BlockSpec((N_DEV, r, c), lambda i: (0, 0, 0)), pl.BlockSpec((r, c), lambda i: (0, 0))],
        out_specs=pl.BlockSpec((r, c), lambda i: (0, 0)),
        out_shape=jax.ShapeDtypeStruct((r, c), F32),
        compiler_params=_cparams(("arbitrary",)),
    )(land, own)


def _rs_sibling_start(fulls, segs, name):
    ns = len(segs)
    offs = _seg_offsets(segs)
    rtot = sum(n * r for n, r in segs)
    c = fulls[0].shape[-1]
    dt = fulls[0].dtype

    def body(*refs):
        srcs = refs[:ns]
        land_ref, send_sem, recv_sem = refs[ns], refs[ns + 1], refs[ns + 2]
        token = refs[-1]
        x, y, cc = _mesh_pos()
        for k in range(4):
            for a, (n, r) in enumerate(segs):
                for m in range(n):
                    theirs = srcs[a].at[m, pl.ds(pl.multiple_of((2 * k + 1 - cc) * r, r), r), :]
                    _remote(theirs, land_ref.at[k, pl.ds(offs[a] + m * r, r), :], send_sem, recv_sem,
                            (x, y, 1 - cc)).start()
        token[...] = jnp.zeros_like(token)

    outs = pl.pallas_call(
        body, name=name,
        in_specs=[HBM] * (ns + 1),
        out_specs=[SEM, SEM] + [HBM] * (ns + 1) + [VMEM_WHOLE],
        out_shape=[pltpu.SemaphoreType.DMA(()), pltpu.SemaphoreType.DMA(())]
        + [pltpu.HBM(a.shape, a.dtype) for a in fulls] + [pltpu.HBM((4, rtot, c), dt),
                                                           jax.ShapeDtypeStruct((SUBLANES, LANES), F32)],
        input_output_aliases={i: 2 + i for i in range(ns + 1)},
        compiler_params=pltpu.CompilerParams(has_side_effects=EFFECT),
    )(*[_hbm(a) for a in fulls], _hbm(lax.empty((4, rtot, c), dt)))
    return outs[0], outs[1], list(outs[2:2 + ns]), outs[2 + ns], outs[-1]


def _rs_sibling_wait(send_sem, recv_sem, fulls, land, after, name):
    ns = len(fulls)

    def body(*refs):
        land_ref, send_ref, recv_ref = refs[ns], refs[ns + 1], refs[ns + 2]
        whole = _remote(land_ref, land_ref, send_ref, recv_ref, _mesh_pos())
        whole.wait_send()
        whole.wait_recv()

    outs = pl.pallas_call(
        body, name=name,
        in_specs=[HBM] * (ns + 1) + [SEM, SEM, UNREAD], out_specs=[HBM] * (ns + 1),
        out_shape=[pltpu.HBM(a.shape, a.dtype) for a in fulls] + [pltpu.HBM(land.shape, land.dtype)],
        input_output_aliases={i: i for i in range(ns + 1)},
        compiler_params=pltpu.CompilerParams(has_side_effects=EFFECT),
    )(*fulls, land, send_sem, recv_sem, _hbm(after))
    return list(outs[:ns]), outs[ns]


def _tp(w):
    return jnp.swapaxes(w, -1, -2)


def _s5_prepare(a_re, a_im, log_dt, b_re, b_im, c_re, c_im):
    a = jnp.stack([a_re, a_im], axis=1)
    ldt = jnp.broadcast_to(log_dt[:, :, None], (DEPTH, SSM_GROUPS, SSM_STATE))
    a_row = a.reshape(DEPTH, 2, 1, N_STATE)
    ldt_row = ldt.reshape(DEPTH, 1, N_STATE)
    a_rep = jnp.repeat(a, SSM_GROUP, axis=2)
    ldt_rep = jnp.repeat(ldt, SSM_GROUP, axis=1)
    bt = jnp.stack([_tp(b_re), _tp(b_im)], axis=1).reshape(DEPTH, 2, SSM_W, SSM_STATE)
    ct = jnp.stack([c_re, c_im], axis=1).reshape(DEPTH, 2, SSM_W, SSM_STATE)
    tile_e = jnp.tile(jnp.eye(SSM_STATE, dtype=BF16), (1, SSM_GROUPS))
    mask = jnp.repeat(jnp.repeat(jnp.eye(SSM_GROUPS, dtype=BF16), SSM_GROUP, axis=0), SSM_STATE, axis=1)
    out = []
    for l in range(DEPTH):
        tabs = _s5_disc(a_row[l], ldt_row[l], a_rep[l], ldt_rep[l], bt[l], ct[l], tile_e, mask)
        out.append(((a[l], ldt[l], a_rep[l], ldt_rep[l], bt[l], mask), *tabs))
    return out


def _layer_fwd(h, p_l, small, big, arrive=None):
    saved = {'h0': h}
    if arrive is not None:
        arrive(0, h)
    h, saved['gu1'] = _ffn_fwd(h, small['ffn1_norm'], big['ff1'])
    saved['h1'] = h
    if arrive is not None:
        arrive(1, h)
    z = _inproj_fwd(h, small['mix_norm'], big['wint'])
    ya, ys, hs = _s5conv_fwd(z, small['conv_w'], small['conv_b'], small['bbmat'], small['ccmat'], small['dvec'],
                             small['ltab'])
    saved.update(z=z, ya=ya, ys=ys, hs=hs)
    h = _mix_out_fwd(h, ya, ys, big['glu'], small['glu_b'], small['conv_out_norm'], small['ssm_out_norm'], big['wout'])
    saved['h2'] = h
    if arrive is not None:
        arrive(2, h)
    h, saved['gu2'] = _ffn_fwd(h, small['ffn2_norm'], big['ff2'])
    saved['h3'] = h
    h = _ple_fwd(h, small['ple_norm'], p_l, big['plg'], big['plpt'])
    return h, saved


def _ffn_bwd(h_in, g, dh, gu, w3):
    dh_in, dga, ud, dg = _ffn_bwd_act(h_in, g, dh, gu, w3)
    return dh_in, _matmul_tn(dga, ud, FF_BLOCK, BF16, "ffn_wgrad"), dg


def _layer_bwd_top(dh, p_l, small, big, saved):
    gs = {}
    dh, u, dq, dpp, pb, gs['ple_norm'] = _ple_bwd(saved['h3'], small['ple_norm'], p_l, dh, big['plg'], big['plpt'])
    d_plg = _matmul_tn(u, dq, 256, BF16, "ple_gate_wgrad")
    d_plpt = _matmul_tn(dpp, pb, 256, BF16, "ple_proj_wgrad", to_kernel=False)
    dh, d_ff2, gs['ffn2_norm'] = _ffn_bwd(saved['h2'], small['ffn2_norm'], dh, saved['gu2'], big['ff2'])
    return dh, (gs, d_plg, d_plpt, d_ff2)


def _layer_bwd_rest(dh, top, small, big, saved):
    gs, d_plg, d_plpt, d_ff2 = top
    dya, dys, ycat, dhb, zg, dq, part = _mix_out_bwd(dh, saved['ya'], saved['ys'], big['glu'], small['glu_b'],
                                                     small['conv_out_norm'], small['ssm_out_norm'], big['wout'])
    d_wout = _matmul_tn(ycat, dhb, 256, BF16, "w_out_wgrad")
    d_glu = _matmul_tn(zg, dq, 256, BF16, "glu_wgrad", to_kernel=False)
    dz, gadj, us, dyb, dl, dcw = _s5conv_bwd(saved['z'], saved['hs'], dya, dys, small['conv_w'], small['conv_b'],
                                             small['bbmat'], small['ccmat'], small['dvec'], small['ltab_rev'])
    d_bb = _block_wgrad(us, gadj, "s5_b_wgrad")
    d_cc = _block_wgrad(dyb, saved['hs'][None], "s5_c_wgrad")
    dh, u, gs['mix_norm'] = _inproj_bwd(saved['h1'], small['mix_norm'], dh, dz, big['wint'])
    d_wint = _matmul_tn(dz[None], u, 256, BF16, "w_in_wgrad")
    dh, d_ff1, gs['ffn1_norm'] = _ffn_bwd(saved['h0'], small['ffn1_norm'], dh, saved['gu1'], big['ff1'])

    dlb = dl[0].reshape(2, SSM_GROUPS, SSM_STATE)
    fold = jnp.tile(jnp.eye(SSM_STATE, dtype=BF16), (SSM_GROUPS, 1))
    da, dldt, dbt, dct = _s5_disc_bwd(*small['disc_in'], dlb, d_bb, d_cc, fold)
    gs['ssm_A_re'], gs['ssm_A_im'] = da[0], da[1]
    gs['ssm_log_dt'] = dldt[:, 0]
    ghp = (SSM_GROUPS, SSM_GROUP, SSM_STATE)
    gs['ssm_B_re'], gs['ssm_B_im'] = dbt[0].reshape(ghp), dbt[1].reshape(ghp)
    gs['ssm_C_re'], gs['ssm_C_im'] = dct[0].reshape(ghp), dct[1].reshape(ghp)
    gs['conv_w'] = dcw[0:3]
    gs['conv_b'] = dcw[3]
    gs['ssm_D'] = dcw[4].reshape(SSM_GROUPS, SSM_GROUP)
    gs['conv_out_norm'], gs['ssm_out_norm'], gs['glu_b'] = part[0], part[1], part[2]
    for n in ('ple_norm', 'ffn2_norm', 'mix_norm', 'ffn1_norm'):
        gs[n] = gs[n][0]
    fulls = [d_ff1, d_ff2, d_wint, d_wout, d_plg,
             d_plpt.reshape(1, D_MODEL * PLE_DIM // D_MODEL, D_MODEL), d_glu.reshape(1, SSM_W * SSM_W // D_MODEL, D_MODEL)]
    return dh, fulls, gs


VIEW_T = ('ffn1_w_gate', 'ffn1_w_up', 'ffn2_w_gate', 'ffn2_w_up', 'ssm_B_re', 'ssm_B_im')


def _view(name, a):
    return _tp(a) if name in VIEW_T else a


SEG_NAMES = ('ff1', 'ff2', 'wint', 'wout', 'plg', 'plpt', 'glu')
FIRST_LAYER_GROUPS = ((0,), (2, 3, 6), (1, 4, 5))


def _layer_pack(W, l, segments=range(len(SEGS))):
    pieces = {
        0: lambda: [_tp(W['ffn1_w_gate'][l]), _tp(W['ffn1_w_up'][l]), W['ffn1_w_down'][l]],
        1: lambda: [_tp(W['ffn2_w_gate'][l]), _tp(W['ffn2_w_up'][l]), W['ffn2_w_down'][l]],
        2: lambda: [_tp(W['w_in'][l])],
        3: lambda: [W['w_out'][l]],
        4: lambda: [W['ple_w_gate'][l]],
        5: lambda: [_tp(W['ple_w_proj'][l]).reshape(-1, D_MODEL)],
        6: lambda: [W['glu_w'][l].reshape(-1, D_MODEL)],
    }
    return jnp.concatenate([a for s in segments for a in pieces[s]()], axis=0).astype(BF16)


def _as_big(named):
    shape = dict(plpt=(D_MODEL, PLE_DIM), glu=(SSM_W, SSM_W))
    return {n: (a.reshape(shape[n]) if n in shape else a) for n, a in named.items()}


def _pad_rows(flat, mult, width=LANES):
    per = mult * width
    n = flat.shape[0]
    tot = -(-n // per) * per
    return jnp.pad(flat, (0, tot - n)).reshape(tot // width, width)


def _adamw_any(w, g, m, v):
    shp = w.shape
    two = (lambda t: t.reshape(-1, shp[-1]))
    d, nm, nv = _adamw(two(w), two(g), two(m), two(v))
    return d.reshape(shp), nm.reshape(shp), nv.reshape(shp)


def kernel(x, p, ffn1_norm, ffn1_w_gate, ffn1_w_up, ffn1_w_down, mix_norm, w_in, conv_w, conv_b, ssm_A_re, ssm_A_im, ssm_B_re, ssm_B_im, ssm_C_re, ssm_C_im, ssm_D, ssm_log_dt, glu_w, glu_b, conv_out_norm, ssm_out_norm, w_out, ffn2_norm, ffn2_w_gate, ffn2_w_up, ffn2_w_down, ple_norm, ple_w_gate, ple_w_proj, final_norm, loss_target, m_ffn1_norm, m_ffn1_w_gate, m_ffn1_w_up, m_ffn1_w_down, m_mix_norm, m_w_in, m_conv_w, m_conv_b, m_ssm_A_re, m_ssm_A_im, m_ssm_B_re, m_ssm_B_im, m_ssm_C_re, m_ssm_C_im, m_ssm_D, m_ssm_log_dt, m_glu_w, m_glu_b, m_conv_out_norm, m_ssm_out_norm, m_w_out, m_ffn2_norm, m_ffn2_w_gate, m_ffn2_w_up, m_ffn2_w_down, m_ple_norm, m_ple_w_gate, m_ple_w_proj, m_final_norm, v_ffn1_norm, v_ffn1_w_gate, v_ffn1_w_up, v_ffn1_w_down, v_mix_norm, v_w_in, v_conv_w, v_conv_b, v_ssm_A_re, v_ssm_A_im, v_ssm_B_re, v_ssm_B_im, v_ssm_C_re, v_ssm_C_im, v_ssm_D, v_ssm_log_dt, v_glu_w, v_glu_b, v_conv_out_norm, v_ssm_out_norm, v_w_out, v_ffn2_norm, v_ffn2_w_gate, v_ffn2_w_up, v_ffn2_w_down, v_ple_norm, v_ple_w_gate, v_ple_w_proj, v_final_norm):
    given = dict(locals())
    W = {n: given[n] for n in W_NAMES}
    M = {n: given['m_' + n] for n in W_NAMES}
    V = {n: given['v_' + n] for n in W_NAMES}
    Wv, Mv, Vv = [{n: _view(n, d[n]) for n in W_NAMES} for d in (W, M, V)]
    my_dev = _dev_index(_mesh_pos())

    conv_shard = _pad_rows(W['conv_w'].reshape(-1), SUBLANES)
    conv_all = _allgather(conv_shard, ((1, SUBLANES),), "ag_conv_w")[0]
    conv_full = conv_all.reshape(N_DEV, -1)[:, :DEPTH * 3 * (CONV_W // N_DEV)]
    conv_full = conv_full.reshape(N_DEV, DEPTH, 3, CONV_W // N_DEV).transpose(1, 2, 0, 3).reshape(DEPTH, 3, CONV_W)
    first, after = [], conv_all
    for gi, segments in enumerate(FIRST_LAYER_GROUPS):
        first.append(_ag_start(_layer_pack(W, 0, segments), tuple(SEGS[s] for s in segments), after,
                               "ag_start_0%s" % "abc"[gi]))
        after = first[-1][4]
    s5 = _s5_prepare(*[W[n] + after[0, 0] for n in ('ssm_A_re', 'ssm_A_im', 'ssm_log_dt')],
                     *[W[n] for n in ('ssm_B_re', 'ssm_B_im', 'ssm_C_re', 'ssm_C_im')])
    packs = [None] + [_layer_pack(W, l) for l in range(1, DEPTH)]
    prepared = conv_full[0, 0:1, 0:1] + s5[DEPTH - 1][1][0:1, 0:1] + packs[DEPTH - 1][0:1, 0:1].astype(F32)

    smalls, saves, bigs = [], [], []
    h = x[0]

    flight = None

    def gathered(handles, segments, after, name, next_layer=None, gate=None):
        nonlocal flight
        send_sems, recv_sems, pack_thru, lands, _ = handles
        pack_thru, lands = _ag_wait(send_sems, recv_sems, pack_thru, lands, after, "ag_wait_" + name)
        if next_layer is not None:
            flight = _ag_start(packs[next_layer], SEGS, pack_thru, "ag_start_%d" % next_layer)
            gate[0][gate[1]] = gate[0][gate[1]] + flight[4][0:1, 0:1]
        outs = _ag_finish(pack_thru, lands, tuple(SEGS[s] for s in segments))
        return _as_big({SEG_NAMES[s]: a for s, a in zip(segments, outs)})

    for l in range(DEPTH):
        small = {n: W[n][l][None] for n in ('ffn1_norm', 'mix_norm', 'conv_b', 'glu_b', 'conv_out_norm',
                                            'ssm_out_norm', 'ffn2_norm', 'ple_norm')}
        small['conv_w'] = conv_full[l]
        small['dvec'] = W['ssm_D'][l].reshape(1, SSM_W)
        small['disc_in'], small['ltab'], small['ltab_rev'], small['bbmat'], small['ccmat'] = s5[l]
        big = {}
        bigs.append(big)
        if l == 0:
            def arrive(stage, h_now, big=big, small=small):
                big.update(gathered(first[stage], FIRST_LAYER_GROUPS[stage], prepared if stage == 0 else h_now,
                                    "0%s" % "abc"[stage], *((1, (small, 'ffn2_norm')) if stage == 2 else ())))
            h, saved = _layer_fwd(h, p[l, 0], small, big, arrive)
        else:
            nxt = (l + 1, (small, 'ffn1_norm')) if l + 1 < DEPTH else ()
            big.update(gathered(flight, range(len(SEGS)), h, "%d" % l, *nxt))
            h, saved = _layer_fwd(h, p[l, 0], small, big)
        smalls.append(small)
        saves.append(saved)
    loss_tile, dh, d_final = _final_loss(h, W['final_norm'][None], loss_target[0])
    loss = lax.psum(loss_tile[0, 0], ("x", "y", "c"))

    layer_gs = [None] * DEPTH
    shard_grads = None
    sib, ici = None, None

    def finish_sibling(after_sib, after_ici):
        nonlocal sib, ici
        up, (send_sem, recv_sem, fulls_thru, land, _) = sib
        fulls_thru, got = _rs_sibling_wait(send_sem, recv_sem, fulls_thru, land, after_sib, "sib_wait_%d" % up)
        pbf = _pair_sum(fulls_thru, got, SEGS)
        done = finish_chips(after_ici)
        ici = (up, _rs_chips_start(pbf, after_ici if done is None else done, "rs_start_%d" % up), fulls_thru, got)
        sib = None

    def finish_chips(after):
        nonlocal ici, shard_grads
        if ici is None:
            return None
        up, (send_sems, recv_sems, pbf_thru, land, _), fulls_up, got_up = ici
        got3 = _rs_chips_wait(send_sems, recv_sems, pbf_thru, land, after, "rs_wait_%d" % up)
        shard_grads = _chip_sum(fulls_up, got_up, got3, SEGS, up, shard_grads)
        ici = None
        return shard_grads

    layer_names = [n for n in SMALL_NAMES if n != 'final_norm']
    small_flights = [None] * DEPTH
    for l in reversed(range(DEPTH)):
        small = dict(smalls[l])
        if sib is not None:
            small['ple_norm'] = small['ple_norm'] + sib[1][4][0:1, 0:1] + small_flights[l + 1][4][0:1, 0:1]
        dh, top = _layer_bwd_top(dh, p[l, 0], small, bigs[l], saves[l])
        if sib is not None:
            finish_sibling(dh, dh)
            small['glu_b'] = small['glu_b'] + ici[1][4][0:1, 0:1]
        dh, fulls, layer_gs[l] = _layer_bwd_rest(dh, top, small, bigs[l], saves[l])
        sib = (l, _rs_sibling_start(fulls, SEGS, "sib_start_%d" % l))
        last_slot = d_final[0] if l == DEPTH - 1 else jnp.zeros((D_MODEL,), F32)
        flat = jnp.concatenate([layer_gs[l][n].reshape(-1) for n in layer_names + ['conv_w']] + [last_slot])
        small_flights[l] = _small_gather_start(_pad_rows(flat, SUBLANES, D_MODEL), "small_start_%d" % l)
    grad_x = dh[None]
    finish_sibling(small_flights[0][4], small_flights[0][4])

    reduced = []
    for l in range(DEPTH):
        send_sems, recv_sems, flat_thru, land, _ = small_flights[l]
        flat_thru, land = _small_gather_wait(send_sems, recv_sems, flat_thru, land, ici[1][4], "small_wait_%d" % l)
        reduced.append(_sum_devices(land, flat_thru).reshape(-1))
    reduced = jnp.stack(reduced)
    G = {}
    o = 0
    for n in layer_names + ['conv_w']:
        size = (W[n].size if n != 'conv_w' else DEPTH * 3 * CONV_W) // DEPTH
        shape = Wv[n].shape if n != 'conv_w' else (DEPTH, 3, CONV_W)
        G[n] = reduced[:, o:o + size].reshape(shape)
        o += size
    G['final_norm'] = reduced[DEPTH - 1, o:o + D_MODEL]
    G['conv_w'] = lax.dynamic_slice_in_dim(G['conv_w'], my_dev * (CONV_W // N_DEV), CONV_W // N_DEV, axis=2)

    delta, new_m, new_v = {}, {}, {}
    for n in SMALL_NAMES + ['conv_w']:
        two = (lambda t: t.reshape(1, -1) if t.ndim == 1 else t)
        delta[n], new_m[n], new_v[n] = [t.reshape(Wv[n].shape) for t in
                                        _adamw_any(two(Wv[n]), two(G[n]), two(Mv[n]), two(Vv[n]))]

    offs = _seg_offsets(SEGS)
    r = SEGS[0][1]
    packed_rows = {'w_out': offs[3], 'ple_w_gate': offs[4]}
    for a, f in ((0, 'ffn1'), (1, 'ffn2')):
        packed_rows.update({f + '_w_gate': offs[a], f + '_w_up': offs[a] + r, f + '_w_down': offs[a] + 2 * r})

    def relaid(sg):
        nl = sg.shape[0]
        return {'w_in': _tp(sg[:, offs[2]:offs[2] + SEGS[2][1]]),
                'ple_w_proj': _tp(sg[:, offs[5]:offs[5] + SEGS[5][1]].reshape(nl, D_MODEL // N_DEV, PLE_DIM)),
                'glu_w': sg[:, offs[6]:offs[6] + SEGS[6][1]].reshape(nl, SSM_W // N_DEV, SSM_W)}

    groups = {}
    for n in list(packed_rows) + ['w_in', 'ple_w_proj', 'glu_w']:
        groups.setdefault(Wv[n].shape, []).append(n)

    def update(first, nl, prev):
        other = relaid(shard_grads[first:first + nl])
        sets = lambda ns: [(Wv[n], Mv[n], Vv[n], shard_grads, packed_rows[n]) if n in packed_rows
                           else (Wv[n], Mv[n], Vv[n], other[n], None) for n in ns]
        return {shape: _adamw_layers(sets(ns), first, nl, None if prev is None else prev[shape])
                for shape, ns in groups.items()}

    part = update(1, DEPTH - 1, None)
    finish_chips(sum(four[3][1, 0:1, 0:1] for fours in part.values() for four in fours)
                 + sum(new_v[n][(0,) * new_v[n].ndim].reshape(1, 1) for n in SMALL_NAMES + ['conv_w']))
    for shape, fours in update(0, 1, part).items():
        for n, four in zip(groups[shape], fours):
            G[n], delta[n], new_m[n], new_v[n] = four

    outs = [[_view(n, d[n]) for n in W_NAMES] for d in (G, delta, new_m, new_v)]
    return (loss, grad_x, *outs[0], *outs[1], *outs[2], *outs[3])
```

```python
import math

import jax
import jax.numpy as jnp
from jax import lax
from jax.experimental import pallas as pl
from jax.experimental.pallas import tpu as pltpu

F32 = jnp.float32
BF16 = jnp.bfloat16

N_DEV = 8
DEPTH = 4
SEQ = 2048
D_MODEL = 1024
D_FF = 2816
CONV_W = 512
SSM_W = 512
SSM_GROUPS = 32
SSM_GROUP = 16
SSM_STATE = 64
N_STATE = SSM_GROUPS * SSM_STATE
IN_COLS = 2048
PLE_DIM = 256
EPS = 1e-6

ADAM_LR = 0.001
ADAM_B1 = 0.9
ADAM_B2 = 0.999
ADAM_EPS = 1e-08
ADAM_WD = 0.01
ADAM_STEP = 10

FF_BLOCK = 256
N_FF_BLOCKS = D_FF // FF_BLOCK
TOK_TILE_FFN_FWD = 2048
TOK_TILE_FFN_BWD = 1024
TOK_TILE = 512
CHUNK = 256
N_CHUNKS = SEQ // CHUNK
LANE_GROUP = 512
SUBLANES = 8
LANES = 128
MIB = 1024 * 1024

W_NAMES = ['ffn1_norm', 'ffn1_w_gate', 'ffn1_w_up', 'ffn1_w_down', 'mix_norm', 'w_in', 'conv_w', 'conv_b',
           'ssm_A_re', 'ssm_A_im', 'ssm_B_re', 'ssm_B_im', 'ssm_C_re', 'ssm_C_im', 'ssm_D', 'ssm_log_dt',
           'glu_w', 'glu_b', 'conv_out_norm', 'ssm_out_norm', 'w_out', 'ffn2_norm', 'ffn2_w_gate', 'ffn2_w_up',
           'ffn2_w_down', 'ple_norm', 'ple_w_gate', 'ple_w_proj', 'final_norm']
SMALL_NAMES = ['ffn1_norm', 'mix_norm', 'conv_b', 'ssm_A_re', 'ssm_A_im', 'ssm_B_re', 'ssm_B_im', 'ssm_C_re',
               'ssm_C_im', 'ssm_D', 'ssm_log_dt', 'glu_b', 'conv_out_norm', 'ssm_out_norm', 'ffn2_norm',
               'ple_norm', 'final_norm']

SEGS = ((3, 352), (3, 352), (1, 256), (1, 128), (1, 128), (1, 32), (1, 32))
PACK_ROWS = sum(n * r for n, r in SEGS)

MESH = pl.DeviceIdType.MESH
UNREAD = pl.BlockSpec(memory_space=pltpu.HBM)


def _in_hbm(*arrays):
    return [pltpu.with_memory_space_constraint(a, pltpu.HBM) for a in arrays]


def _out_hbm(outs, which):
    if not isinstance(outs, (list, tuple)):
        return pltpu.with_memory_space_constraint(outs, pltpu.HBM) if which else outs
    return [pltpu.with_memory_space_constraint(a, pltpu.HBM) if i in which else a for i, a in enumerate(outs)]


def _cparams(sem=None, vmem_mib=48, **kw):
    return pltpu.CompilerParams(dimension_semantics=sem, vmem_limit_bytes=vmem_mib * MIB, **kw)


def _dot(a, b):
    return jnp.dot(a, b, preferred_element_type=F32)


def _dot_nt(a, b):
    return lax.dot_general(a, b, (((1,), (1,)), ((), ())), preferred_element_type=F32)


def _dot_tn(a, b):
    return lax.dot_general(a, b, (((0,), (0,)), ((), ())), preferred_element_type=F32)


def _rms_stats(x):
    r = lax.rsqrt(jnp.mean(x * x, axis=-1, keepdims=True) + EPS)
    return x * r, r


def _rms_bwd(dy, xh, r, g):
    dxh = dy * g
    dx = r * (dxh - xh * jnp.mean(dxh * xh, axis=-1, keepdims=True))
    dg = jnp.sum(dy * xh, axis=0, keepdims=True)
    return dx, dg


def _sigmoid(x):
    return 0.5 * jnp.tanh(0.5 * x) + 0.5


_GELU_C = math.sqrt(2.0 / math.pi)


def _gelu(x):
    t = jnp.tanh(_GELU_C * (x + 0.044715 * x * x * x))
    return 0.5 * x * (1.0 + t), t


def _gelu_grad(x, t):
    return 0.5 * (1.0 + t) + 0.5 * x * (1.0 - t * t) * _GELU_C * (1.0 + 3.0 * 0.044715 * x * x)


def _accumulate(ref, first, value):
    @pl.when(first)
    def _():
        ref[...] = value

    @pl.when(jnp.logical_not(first))
    def _():
        ref[...] += value


def _ffn_fwd(h, g, w3):
    tm = TOK_TILE_FFN_FWD
    last = N_FF_BLOCKS - 1

    def body(h_ref, g_ref, wgu_ref, wd_ref, wd_last_ref, out_ref, gu_ref, u_ref, a_ref):
        k = pl.program_id(1)

        @pl.when(k == 0)
        def _():
            x = h_ref[...]
            xh, _ = _rms_stats(x)
            u_ref[...] = (xh * g_ref[...]).astype(BF16)
            out_ref[...] = x
            a_ref[1] = jnp.zeros((tm, FF_BLOCK), BF16)

        out_ref[...] += 0.5 * _dot(a_ref[(k + 1) % 2], wd_ref[0])
        gu = _dot_nt(u_ref[...], wgu_ref[...].reshape(2 * FF_BLOCK, D_MODEL))
        gate, up = gu[:, :FF_BLOCK], gu[:, FF_BLOCK:]
        a_ref[k % 2] = (gate * _sigmoid(gate) * up).astype(BF16)
        gu_ref[0] = gate.astype(BF16)
        gu_ref[1] = up.astype(BF16)

        @pl.when(k == last)
        def _():
            out_ref[...] += 0.5 * _dot(a_ref[last % 2], wd_last_ref[0])

    return _out_hbm(pl.pallas_call(
        body, name="ffn_fwd",
        grid=(SEQ // tm, N_FF_BLOCKS),
        in_specs=[pl.BlockSpec((tm, D_MODEL), lambda m, k: (m, 0), pipeline_mode=pl.Buffered(1)),
                  pl.BlockSpec((1, D_MODEL), lambda m, k: (0, 0)),
                  pl.BlockSpec((2, FF_BLOCK, D_MODEL), lambda m, k: (0, k, 0)),
                  pl.BlockSpec((1, FF_BLOCK, D_MODEL), lambda m, k: (2, jnp.maximum(k - 1, 0), 0)),
                  pl.BlockSpec((1, FF_BLOCK, D_MODEL), lambda m, k: (2, last, 0), pipeline_mode=pl.Buffered(1))],
        out_specs=[pl.BlockSpec((tm, D_MODEL), lambda m, k: (m, 0)),
                   pl.BlockSpec((2, tm, FF_BLOCK), lambda m, k: (0, m, k))],
        out_shape=[jax.ShapeDtypeStruct((SEQ, D_MODEL), F32),
                   pltpu.HBM((2, SEQ, D_FF), BF16)],
        scratch_shapes=[pltpu.VMEM((tm, D_MODEL), BF16), pltpu.VMEM((2, tm, FF_BLOCK), BF16)],
        compiler_params=_cparams(("parallel", "arbitrary"), 56),
    )(*_in_hbm(h, g, w3, w3, w3)), (1,))


def _ffn_bwd_act(h, g, dout, gu, w3):
    tm = TOK_TILE_FFN_BWD
    last = N_FF_BLOCKS - 1

    def body(h_ref, g_ref, d_ref, gu_ref, wd_ref, wgu_ref, wgu_last_ref, dh_ref, dga_ref, ud_ref, dg_ref,
             acc_ref, dgu_ref):
        m = pl.program_id(0)
        k = pl.program_id(1)

        @pl.when(k == 0)
        def _():
            xh, _ = _rms_stats(h_ref[...])
            ud_ref[0] = (xh * g_ref[...]).astype(BF16)
            ud_ref[1] = (0.5 * d_ref[...]).astype(BF16)
            acc_ref[...] = jnp.zeros_like(acc_ref)
            dgu_ref[1] = jnp.zeros((tm, 2 * FF_BLOCK), BF16)

        acc_ref[...] += _dot(dgu_ref[(k + 1) % 2], wgu_ref[...].reshape(2 * FF_BLOCK, D_MODEL))
        gate = gu_ref[0].astype(F32)
        up = gu_ref[1].astype(F32)
        sg = _sigmoid(gate)
        silu = gate * sg
        da = _dot_nt(ud_ref[1], wd_ref[0])
        dgate = (da * up * (sg + silu * (1.0 - sg))).astype(BF16)
        dup = (da * silu).astype(BF16)
        dga_ref[0] = dgate
        dga_ref[1] = dup
        dga_ref[2] = (silu * up).astype(BF16)
        dgu_ref[k % 2, :, 0:FF_BLOCK] = dgate
        dgu_ref[k % 2, :, FF_BLOCK:2 * FF_BLOCK] = dup

        @pl.when(k == last)
        def _():
            du = acc_ref[...] + _dot(dgu_ref[last % 2], wgu_last_ref[...].reshape(2 * FF_BLOCK, D_MODEL))
            xh, r = _rms_stats(h_ref[...])
            dx, dg = _rms_bwd(du, xh, r, g_ref[...])
            dh_ref[...] = d_ref[...] + dx
            _accumulate(dg_ref, m == 0, dg)

    return _out_hbm(pl.pallas_call(
        body, name="ffn_bwd_act",
        grid=(SEQ // tm, N_FF_BLOCKS),
        in_specs=[pl.BlockSpec((tm, D_MODEL), lambda m, k: (m, 0), pipeline_mode=pl.Buffered(1)),
                  pl.BlockSpec((1, D_MODEL), lambda m, k: (0, 0)),
                  pl.BlockSpec((tm, D_MODEL), lambda m, k: (m, 0), pipeline_mode=pl.Buffered(1)),
                  pl.BlockSpec((2, tm, FF_BLOCK), lambda m, k: (0, m, k)),
                  pl.BlockSpec((1, FF_BLOCK, D_MODEL), lambda m, k: (2, k, 0)),
                  pl.BlockSpec((2, FF_BLOCK, D_MODEL), lambda m, k: (0, jnp.maximum(k - 1, 0), 0)),
                  pl.BlockSpec((2, FF_BLOCK, D_MODEL), lambda m, k: (0, last, 0), pipeline_mode=pl.Buffered(1))],
        out_specs=[pl.BlockSpec((tm, D_MODEL), lambda m, k: (m, 0)),
                   pl.BlockSpec((3, tm, FF_BLOCK), lambda m, k: (0, m, k)),
                   pl.BlockSpec((2, tm, D_MODEL), lambda m, k: (0, m, 0)),
                   pl.BlockSpec((1, D_MODEL), lambda m, k: (0, 0))],
        out_shape=[jax.ShapeDtypeStruct((SEQ, D_MODEL), F32),
                   pltpu.HBM((3, SEQ, D_FF), BF16),
                   pltpu.HBM((2, SEQ, D_MODEL), BF16),
                   jax.ShapeDtypeStruct((1, D_MODEL), F32)],
        scratch_shapes=[pltpu.VMEM((tm, D_MODEL), F32), pltpu.VMEM((2, tm, 2 * FF_BLOCK), BF16)],
        compiler_params=_cparams(("arbitrary", "arbitrary"), 56),
    )(*_in_hbm(h, g, dout, gu, w3, w3, w3)), (1, 2))


def _matmul_tn(a, b, bm, out_dtype, name, bn=None, to_kernel=True):
    na, t, m = a.shape
    nb, _, n = b.shape
    bn = n if bn is None else bn

    def body(a_ref, b_ref, o_ref):
        o_ref[0] = _dot_tn(a_ref[0], b_ref[0]).astype(out_dtype)

    return _out_hbm(pl.pallas_call(
        body, name=name,
        grid=(na, m // bm, n // bn),
        in_specs=[pl.BlockSpec((1, t, bm), lambda i, k, j: (i, 0, k)),
                  pl.BlockSpec((1, t, bn), lambda i, k, j: (jnp.maximum(i - (na - nb), 0), 0, j))],
        out_specs=pl.BlockSpec((1, bm, bn), lambda i, k, j: (i, k, j)),
        out_shape=pltpu.HBM((na, m, n), out_dtype) if to_kernel else jax.ShapeDtypeStruct((na, m, n), out_dtype),
        compiler_params=_cparams(("arbitrary", "parallel", "parallel")),
    )(*_in_hbm(a, b)), to_kernel)


def _inproj_fwd(h, g, wint):
    tm = TOK_TILE

    def body(h_ref, g_ref, w_ref, z_ref):
        xh, _ = _rms_stats(h_ref[...])
        z_ref[...] = _dot_nt((xh * g_ref[...]).astype(BF16), w_ref[...])

    return pl.pallas_call(
        body, name="inproj_fwd",
        grid=(SEQ // tm,),
        in_specs=[pl.BlockSpec((tm, D_MODEL), lambda m: (m, 0)),
                  pl.BlockSpec((1, D_MODEL), lambda m: (0, 0)),
                  pl.BlockSpec((None, IN_COLS, D_MODEL), lambda m: (0, 0, 0))],
        out_specs=pl.BlockSpec((tm, IN_COLS), lambda m: (m, 0)),
        out_shape=jax.ShapeDtypeStruct((SEQ, IN_COLS), F32),
        compiler_params=_cparams(("parallel",)),
    )(*_in_hbm(h, g, wint))


def _inproj_bwd(h, g, dh, dz, wint):
    tm = TOK_TILE

    def body(h_ref, g_ref, dh_ref, dz_ref, w_ref, o_ref, u_ref, dg_ref):
        xh, r = _rms_stats(h_ref[...])
        u_ref[0] = (xh * g_ref[...]).astype(BF16)
        dx, dg = _rms_bwd(_dot(dz_ref[...], w_ref[...]), xh, r, g_ref[...])
        o_ref[...] = dh_ref[...] + dx
        _accumulate(dg_ref, pl.program_id(0) == 0, dg)

    return _out_hbm(pl.pallas_call(
        body, name="inproj_bwd",
        grid=(SEQ // tm,),
        in_specs=[pl.BlockSpec((tm, D_MODEL), lambda m: (m, 0)),
                  pl.BlockSpec((1, D_MODEL), lambda m: (0, 0)),
                  pl.BlockSpec((tm, D_MODEL), lambda m: (m, 0)),
                  pl.BlockSpec((tm, IN_COLS), lambda m: (m, 0)),
                  pl.BlockSpec((None, IN_COLS, D_MODEL), lambda m: (0, 0, 0))],
        out_specs=[pl.BlockSpec((tm, D_MODEL), lambda m: (m, 0)),
                   pl.BlockSpec((1, tm, D_MODEL), lambda m: (0, m, 0)),
                   pl.BlockSpec((1, D_MODEL), lambda m: (0, 0))],
        out_shape=[jax.ShapeDtypeStruct((SEQ, D_MODEL), F32),
                   pltpu.HBM((1, SEQ, D_MODEL), BF16),
                   jax.ShapeDtypeStruct((1, D_MODEL), F32)],
        compiler_params=_cparams(("arbitrary",)),
    )(*_in_hbm(h, g, dh, dz, wint)), (1,))


def _row_ids(n, w):
    return lax.broadcasted_iota(jnp.int32, (n, w), 0)


def _bcast_row(x, i, n):
    return jnp.broadcast_to(x[i:i + 1, :], (n, x.shape[1]))


def _conv_taps(v, tail):
    n, w = v.shape
    rid = _row_ids(n, w)
    v1 = jnp.where(rid == 0, _bcast_row(tail, 7, n), pltpu.roll(v, 1, 0))
    v2 = jnp.where(rid == 0, _bcast_row(tail, 6, n),
                   jnp.where(rid == 1, _bcast_row(tail, 7, n), pltpu.roll(v, 2, 0)))
    return v1, v2


def _block_tiles():
    half_rows, half_cols = SSM_W // 2, N_STATE // 2
    for half in range(2):
        for part in range(2):
            yield (slice(half * half_rows, (half + 1) * half_rows),
                   slice(part * N_STATE + half * half_cols, part * N_STATE + (half + 1) * half_cols))


def _block_expand(x, mat_ref, out_ref):
    for rows, cols in _block_tiles():
        out_ref[:, cols] = _dot(x[:, rows], mat_ref[rows, cols])


def _block_contract(s, mat_ref):
    halves = {}
    for rows, cols in _block_tiles():
        part = _dot_nt(s[:, cols], mat_ref[rows, cols])
        halves[rows.start] = part if rows.start not in halves else halves[rows.start] + part
    return jnp.concatenate([halves[k] for k in sorted(halves)], axis=1)


def _block_wgrad(a, b, name):
    t = a.shape[1]
    half_rows, half_cols = SSM_W // 2, N_STATE // 2

    def body(a_ref, b_ref, o_ref):
        o_ref[...] = _dot_tn(a_ref[...], b_ref[...])

    return pl.pallas_call(
        body, name=name,
        grid=(2, 2),
        in_specs=[pl.BlockSpec((None, t, half_rows), lambda h, p: (0, 0, h)),
                  pl.BlockSpec((None, t, half_cols), lambda h, p: (0, 0, 2 * p + h))],
        out_specs=pl.BlockSpec((half_rows, half_cols), lambda h, p: (h, 2 * p + h)),
        out_shape=jax.ShapeDtypeStruct((SSM_W, 2 * N_STATE), F32),
        compiler_params=_cparams(("parallel", "parallel")),
    )(*_in_hbm(a, b))


def _scan_chunk(work, ltab, carry, reverse):
    nblk = CHUNK // SUBLANES
    for gi in range(N_STATE // LANE_GROUP):
        cre = pl.ds(gi * LANE_GROUP, LANE_GROUP)
        cim = pl.ds(N_STATE + gi * LANE_GROUP, LANE_GROUP)
        pows = [(ltab[8 * k:8 * k + 8, cre], ltab[8 * k:8 * k + 8, cim]) for k in range(3)]
        pr = ltab[24:32, cre]
        pi = ltab[24:32, cim]

        def blk(i, c, cre=cre, cim=cim, pows=pows, pr=pr, pi=pi):
            cr, ci = c
            b = (nblk - 1 - i) if reverse else i
            r0 = pl.multiple_of(b * SUBLANES, SUBLANES)
            xr = work[pl.ds(r0, SUBLANES), cre]
            xi = work[pl.ds(r0, SUBLANES), cim]
            for k, s in enumerate((1, 2, 4)):
                lr, li = pows[k]
                shift = SUBLANES - s if reverse else s
                sr = pltpu.roll(xr, shift, 0)
                si = pltpu.roll(xi, shift, 0)
                xr, xi = xr + lr * sr - li * si, xi + lr * si + li * sr
            xr, xi = xr + pr * cr - pi * ci, xi + pr * ci + pi * cr
            work[pl.ds(r0, SUBLANES), cre] = xr
            work[pl.ds(r0, SUBLANES), cim] = xi
            edge = 0 if reverse else SUBLANES - 1
            return _bcast_row(xr, edge, SUBLANES), _bcast_row(xi, edge, SUBLANES)

        cr, ci = lax.fori_loop(0, nblk, blk, (carry[:, cre], carry[:, cim]))
        carry[:, cre] = cr
        carry[:, cim] = ci


def _s5conv_fwd(z, convw, convb, bbmat, ccmat, dvec, ltab):
    def body(z_ref, cw_ref, cb_ref, bb_ref, cc_ref, d_ref, lt_ref, ya_ref, ys_ref, hs_ref,
             work, carry, tail):
        c = pl.program_id(0)

        @pl.when(c == 0)
        def _():
            carry[...] = jnp.zeros_like(carry)
            tail[...] = jnp.zeros_like(tail)

        zb = z_ref[:, 0:CONV_W]
        v = z_ref[:, CONV_W:2 * CONV_W] * z_ref[:, 2 * CONV_W:3 * CONV_W]
        us = z_ref[:, 3 * CONV_W:4 * CONV_W]
        v1, v2 = _conv_taps(v, tail[...])
        tail[...] = v[CHUNK - 8:CHUNK, :]
        y = cw_ref[0:1, :] * v2 + cw_ref[1:2, :] * v1 + cw_ref[2:3, :] * v
        ya_ref[...] = zb * (y + cb_ref[...])

        _block_expand(us.astype(BF16), bb_ref, work)
        _scan_chunk(work, lt_ref, carry, reverse=False)
        hs = work[...].astype(BF16)
        hs_ref[...] = hs
        ys_ref[...] = _block_contract(hs, cc_ref) + d_ref[...] * us

    return _out_hbm(pl.pallas_call(
        body, name="s5conv_fwd",
        grid=(N_CHUNKS,),
        in_specs=[pl.BlockSpec((CHUNK, IN_COLS), lambda c: (c, 0)),
                  pl.BlockSpec((3, CONV_W), lambda c: (0, 0)),
                  pl.BlockSpec((1, CONV_W), lambda c: (0, 0)),
                  pl.BlockSpec((SSM_W, 2 * N_STATE), lambda c: (0, 0)),
                  pl.BlockSpec((SSM_W, 2 * N_STATE), lambda c: (0, 0)),
                  pl.BlockSpec((1, SSM_W), lambda c: (0, 0)),
                  pl.BlockSpec((32, 2 * N_STATE), lambda c: (0, 0))],
        out_specs=[pl.BlockSpec((CHUNK, CONV_W), lambda c: (c, 0)),
                   pl.BlockSpec((CHUNK, SSM_W), lambda c: (c, 0)),
                   pl.BlockSpec((CHUNK, 2 * N_STATE), lambda c: (c, 0))],
        out_shape=[pltpu.HBM((SEQ, CONV_W), F32),
                   pltpu.HBM((SEQ, SSM_W), F32),
                   jax.ShapeDtypeStruct((SEQ, 2 * N_STATE), BF16)],
        scratch_shapes=[pltpu.VMEM((CHUNK, 2 * N_STATE), F32),
                        pltpu.VMEM((8, 2 * N_STATE), F32),
                        pltpu.VMEM((8, CONV_W), F32)],
        compiler_params=_cparams(("arbitrary",)),
    )(*_in_hbm(z, convw, convb, bbmat, ccmat, dvec, ltab)), (0, 1))


def _s5conv_bwd(z, hs, dya, dys, convw, convb, bbmat, ccmat, dvec, ltab_rev):
    nc = N_CHUNKS
    hb = 16

    def body(z_ref, zp_ref, hs_ref, hp_ref, dya_ref, dys_ref, cw_ref, cb_ref, bb_ref, cc_ref, d_ref, lt_ref,
             dz_ref, g_ref, us_ref, dyb_ref, dl_ref, dcw_ref, work, carry, head):
        i = pl.program_id(0)
        first_chunk = i == nc - 1

        @pl.when(i == 0)
        def _():
            carry[...] = jnp.zeros_like(carry)
            head[...] = jnp.zeros_like(head)
            dl_ref[...] = jnp.zeros_like(dl_ref)
            dcw_ref[...] = jnp.zeros_like(dcw_ref)

        us = z_ref[:, 3 * CONV_W:4 * CONV_W]
        dy = dys_ref[...]
        dy_bf = dy.astype(BF16)
        us_ref[0] = us.astype(BF16)
        dyb_ref[0] = dy_bf

        _block_expand(dy_bf, cc_ref, work)
        _scan_chunk(work, lt_ref, carry, reverse=True)
        gg = work[...]
        gg_bf = gg.astype(BF16)
        g_ref[0] = gg_bf
        dus = d_ref[...] * dy + _block_contract(gg_bf, bb_ref)

        hcur = hs_ref[...].astype(F32)
        hlast = hp_ref[...].astype(F32)[hb - 1:hb, :]
        hlast = jnp.where(first_chunk, 0.0, hlast)
        rid = _row_ids(CHUNK, 2 * N_STATE)
        hprev = jnp.where(rid == 0, jnp.broadcast_to(hlast, (CHUNK, 2 * N_STATE)), pltpu.roll(hcur, 1, 0))
        gr, gi = gg[:, :N_STATE], gg[:, N_STATE:]
        hr, hi = hprev[:, :N_STATE], hprev[:, N_STATE:]
        dl_ref[:, :N_STATE] += (gr * hr + gi * hi).reshape(CHUNK // 8, 8, N_STATE).sum(axis=0)
        dl_ref[:, N_STATE:] += (gi * hr - gr * hi).reshape(CHUNK // 8, 8, N_STATE).sum(axis=0)

        @pl.when(i == nc - 1)
        def _():
            dl_ref[0:1, :] = jnp.sum(dl_ref[...], axis=0, keepdims=True)

        zb = z_ref[:, 0:CONV_W]
        zc = z_ref[:, CONV_W:2 * CONV_W]
        zv = z_ref[:, 2 * CONV_W:3 * CONV_W]
        v = zc * zv
        vtail = jnp.where(first_chunk, 0.0, zp_ref[:, CONV_W:2 * CONV_W] * zp_ref[:, 2 * CONV_W:3 * CONV_W])
        v1, v2 = _conv_taps(v, vtail)
        w0, w1, w2 = cw_ref[0:1, :], cw_ref[1:2, :], cw_ref[2:3, :]
        y = w0 * v2 + w1 * v1 + w2 * v
        dya_v = dya_ref[...]
        dzb = dya_v * (y + cb_ref[...])
        dyc = dya_v * zb
        hd = head[...]
        rc = _row_ids(CHUNK, CONV_W)
        n1 = jnp.where(rc == CHUNK - 1, _bcast_row(hd, 0, CHUNK), pltpu.roll(dyc, CHUNK - 1, 0))
        n2 = jnp.where(rc == CHUNK - 1, _bcast_row(hd, 1, CHUNK),
                       jnp.where(rc == CHUNK - 2, _bcast_row(hd, 0, CHUNK), pltpu.roll(dyc, CHUNK - 2, 0)))
        head[...] = dyc[0:8, :]
        dv = w2 * dyc + w1 * n1 + w0 * n2
        dz_ref[:, 0:CONV_W] = dzb.astype(BF16)
        dz_ref[:, CONV_W:2 * CONV_W] = (dv * zv).astype(BF16)
        dz_ref[:, 2 * CONV_W:3 * CONV_W] = (dv * zc).astype(BF16)
        dz_ref[:, 3 * CONV_W:4 * CONV_W] = dus.astype(BF16)
        dcw_ref[0:1, :] += jnp.sum(dyc * v2, axis=0, keepdims=True)
        dcw_ref[1:2, :] += jnp.sum(dyc * v1, axis=0, keepdims=True)
        dcw_ref[2:3, :] += jnp.sum(dyc * v, axis=0, keepdims=True)
        dcw_ref[3:4, :] += jnp.sum(dyc, axis=0, keepdims=True)
        dcw_ref[4:5, :] += jnp.sum(dy * us, axis=0, keepdims=True)

    rev = lambda i: nc - 1 - i
    return _out_hbm(pl.pallas_call(
        body, name="s5conv_bwd",
        grid=(nc,),
        in_specs=[pl.BlockSpec((CHUNK, IN_COLS), lambda i: (rev(i), 0)),
                  pl.BlockSpec((8, IN_COLS), lambda i: (jnp.maximum(rev(i) * (CHUNK // 8) - 1, 0), 0)),
                  pl.BlockSpec((CHUNK, 2 * N_STATE), lambda i: (rev(i), 0)),
                  pl.BlockSpec((hb, 2 * N_STATE), lambda i: (jnp.maximum(rev(i) * (CHUNK // hb) - 1, 0), 0)),
                  pl.BlockSpec((CHUNK, CONV_W), lambda i: (rev(i), 0)),
                  pl.BlockSpec((CHUNK, SSM_W), lambda i: (rev(i), 0)),
                  pl.BlockSpec((3, CONV_W), lambda i: (0, 0)),
                  pl.BlockSpec((1, CONV_W), lambda i: (0, 0)),
                  pl.BlockSpec((SSM_W, 2 * N_STATE), lambda i: (0, 0)),
                  pl.BlockSpec((SSM_W, 2 * N_STATE), lambda i: (0, 0)),
                  pl.BlockSpec((1, SSM_W), lambda i: (0, 0)),
                  pl.BlockSpec((32, 2 * N_STATE), lambda i: (0, 0))],
        out_specs=[pl.BlockSpec((CHUNK, IN_COLS), lambda i: (rev(i), 0)),
                   pl.BlockSpec((1, CHUNK, 2 * N_STATE), lambda i: (0, rev(i), 0)),
                   pl.BlockSpec((1, CHUNK, SSM_W), lambda i: (0, rev(i), 0)),
                   pl.BlockSpec((1, CHUNK, SSM_W), lambda i: (0, rev(i), 0)),
                   pl.BlockSpec((8, 2 * N_STATE), lambda i: (0, 0)),
                   pl.BlockSpec((8, CONV_W), lambda i: (0, 0))],
        out_shape=[jax.ShapeDtypeStruct((SEQ, IN_COLS), BF16),
                   pltpu.HBM((1, SEQ, 2 * N_STATE), BF16),
                   pltpu.HBM((1, SEQ, SSM_W), BF16),
                   pltpu.HBM((1, SEQ, SSM_W), BF16),
                   jax.ShapeDtypeStruct((8, 2 * N_STATE), F32),
                   jax.ShapeDtypeStruct((8, CONV_W), F32)],
        scratch_shapes=[pltpu.VMEM((CHUNK, 2 * N_STATE), F32),
                        pltpu.VMEM((8, 2 * N_STATE), F32),
                        pltpu.VMEM((8, CONV_W), F32)],
        compiler_params=_cparams(("arbitrary",)),
    )(*_in_hbm(z, z, hs, hs, dya, dys, convw, convb, bbmat, ccmat, dvec, ltab_rev)), (1, 2, 3))


def _mix_out_fwd(h, ya, ys, gluw, glub, con, son, wout):
    tm = TOK_TILE

    def body(h_ref, ya_ref, ys_ref, gw_ref, gb_ref, con_ref, son_ref, wo_ref, o_ref):
        zg, _ = _gelu(ys_ref[...])
        q = _dot(zg.astype(BF16), gw_ref[...]) + gb_ref[...]
        out_s = zg * _sigmoid(q)
        na, _ = _rms_stats(ya_ref[...])
        ns, _ = _rms_stats(out_s)
        o_ref[...] = (h_ref[...]
                      + _dot((na * con_ref[...]).astype(BF16), wo_ref[0:CONV_W, :])
                      + _dot((ns * son_ref[...]).astype(BF16), wo_ref[CONV_W:2 * CONV_W, :]))

    row = lambda m: (m, 0)
    fixed = lambda m: (0, 0)
    return pl.pallas_call(
        body, name="mix_out_fwd",
        grid=(SEQ // tm,),
        in_specs=[pl.BlockSpec((tm, D_MODEL), row), pl.BlockSpec((tm, CONV_W), row), pl.BlockSpec((tm, SSM_W), row),
                  pl.BlockSpec((SSM_W, SSM_W), fixed), pl.BlockSpec((1, SSM_W), fixed),
                  pl.BlockSpec((1, CONV_W), fixed), pl.BlockSpec((1, SSM_W), fixed),
                  pl.BlockSpec((None, D_MODEL, D_MODEL), lambda m: (0, 0, 0))],
        out_specs=pl.BlockSpec((tm, D_MODEL), row),
        out_shape=jax.ShapeDtypeStruct((SEQ, D_MODEL), F32),
        compiler_params=_cparams(("parallel",)),
    )(*_in_hbm(h, ya, ys, gluw, glub, con, son, wout))


def _mix_out_bwd(dh, ya, ys, gluw, glub, con, son, wout):
    tm = TOK_TILE

    def body(dh_ref, ya_ref, ys_ref, gw_ref, gb_ref, con_ref, son_ref, wo_ref,
             dya_ref, dys_ref, yc_ref, dhb_ref, zg_ref, dq_ref, part_ref):
        ysv = ys_ref[...]
        zg, th = _gelu(ysv)
        zg_bf = zg.astype(BF16)
        s = _sigmoid(_dot(zg_bf, gw_ref[...]) + gb_ref[...])
        out_s = zg * s
        na, ra = _rms_stats(ya_ref[...])
        ns, rs = _rms_stats(out_s)
        dh_bf = dh_ref[...].astype(BF16)
        yc_ref[0, :, 0:CONV_W] = (na * con_ref[...]).astype(BF16)
        yc_ref[0, :, CONV_W:2 * CONV_W] = (ns * son_ref[...]).astype(BF16)
        dhb_ref[0] = dh_bf
        dca = _dot_nt(dh_bf, wo_ref[0:CONV_W, :])
        dcs = _dot_nt(dh_bf, wo_ref[CONV_W:2 * CONV_W, :])
        dya, dcon = _rms_bwd(dca, na, ra, con_ref[...])
        dos, dson = _rms_bwd(dcs, ns, rs, son_ref[...])
        dya_ref[...] = dya
        dq = dos * zg * s * (1.0 - s)
        dq_bf = dq.astype(BF16)
        dzg = dos * s + _dot_nt(dq_bf, gw_ref[...])
        dys_ref[...] = dzg * _gelu_grad(ysv, th)
        zg_ref[0] = zg_bf
        dq_ref[0] = dq_bf
        rid = _row_ids(SUBLANES, SSM_W)
        part = jnp.zeros((SUBLANES, SSM_W), F32)
        for i, rowv in enumerate((dcon, dson, jnp.sum(dq, axis=0, keepdims=True))):
            part = jnp.where(rid == i, jnp.broadcast_to(rowv, (SUBLANES, SSM_W)), part)
        _accumulate(part_ref, pl.program_id(0) == 0, part)

    row = lambda m: (m, 0)
    fixed = lambda m: (0, 0)
    lead = lambda m: (0, m, 0)
    return _out_hbm(pl.pallas_call(
        body, name="mix_out_bwd",
        grid=(SEQ // tm,),
        in_specs=[pl.BlockSpec((tm, D_MODEL), row), pl.BlockSpec((tm, CONV_W), row), pl.BlockSpec((tm, SSM_W), row),
                  pl.BlockSpec((SSM_W, SSM_W), fixed), pl.BlockSpec((1, SSM_W), fixed),
                  pl.BlockSpec((1, CONV_W), fixed), pl.BlockSpec((1, SSM_W), fixed),
                  pl.BlockSpec((None, D_MODEL, D_MODEL), lambda m: (0, 0, 0))],
        out_specs=[pl.BlockSpec((tm, CONV_W), row), pl.BlockSpec((tm, SSM_W), row),
                   pl.BlockSpec((1, tm, D_MODEL), lead), pl.BlockSpec((1, tm, D_MODEL), lead),
                   pl.BlockSpec((1, tm, SSM_W), lead), pl.BlockSpec((1, tm, SSM_W), lead),
                   pl.BlockSpec((8, SSM_W), fixed)],
        out_shape=[pltpu.HBM((SEQ, CONV_W), F32), pltpu.HBM((SEQ, SSM_W), F32),
                   pltpu.HBM((1, SEQ, D_MODEL), BF16), pltpu.HBM((1, SEQ, D_MODEL), BF16),
                   pltpu.HBM((1, SEQ, SSM_W), BF16), pltpu.HBM((1, SEQ, SSM_W), BF16),
                   jax.ShapeDtypeStruct((8, SSM_W), F32)],
        compiler_params=_cparams(("arbitrary",)),
    )(*_in_hbm(dh, ya, ys, gluw, glub, con, son, wout)), (0, 1, 2, 3, 4, 5))


def _ple_fwd(h, g, p, wgate, wprojt):
    tm = TOK_TILE

    def body(h_ref, g_ref, p_ref, wg_ref, wp_ref, o_ref):
        x = h_ref[...]
        xh, _ = _rms_stats(x)
        s = _sigmoid(_dot((xh * g_ref[...]).astype(BF16), wg_ref[...]))
        o_ref[...] = x + _dot_nt(p_ref[...].astype(BF16), wp_ref[...]) * s

    row = lambda m: (m, 0)
    fixed = lambda m: (0, 0)
    return pl.pallas_call(
        body, name="ple_fwd",
        grid=(SEQ // tm,),
        in_specs=[pl.BlockSpec((tm, D_MODEL), row), pl.BlockSpec((1, D_MODEL), fixed), pl.BlockSpec((tm, PLE_DIM), row),
                  pl.BlockSpec((None, D_MODEL, D_MODEL), lambda m: (0, 0, 0)), pl.BlockSpec((D_MODEL, PLE_DIM), fixed)],
        out_specs=pl.BlockSpec((tm, D_MODEL), row),
        out_shape=jax.ShapeDtypeStruct((SEQ, D_MODEL), F32),
        compiler_params=_cparams(("parallel",)),
    )(*_in_hbm(h, g, p, wgate, wprojt))


def _ple_bwd(h, g, p, dh, wgate, wprojt):
    tm = TOK_TILE

    def body(h_ref, g_ref, p_ref, dh_ref, wg_ref, wp_ref, o_ref, u_ref, dq_ref, dpp_ref, pb_ref, dg_ref):
        xh, r = _rms_stats(h_ref[...])
        u = (xh * g_ref[...]).astype(BF16)
        s = _sigmoid(_dot(u, wg_ref[...]))
        p_bf = p_ref[...].astype(BF16)
        pp = _dot_nt(p_bf, wp_ref[...])
        dhv = dh_ref[...]
        dq = (dhv * pp * s * (1.0 - s)).astype(BF16)
        u_ref[0] = u
        dq_ref[0] = dq
        dpp_ref[0] = (dhv * s).astype(BF16)
        pb_ref[0] = p_bf
        dx, dg = _rms_bwd(_dot_nt(dq, wg_ref[...]), xh, r, g_ref[...])
        o_ref[...] = dhv + dx
        _accumulate(dg_ref, pl.program_id(0) == 0, dg)

    row = lambda m: (m, 0)
    fixed = lambda m: (0, 0)
    lead = lambda m: (0, m, 0)
    big = pltpu.HBM((1, SEQ, D_MODEL), BF16)
    return _out_hbm(pl.pallas_call(
        body, name="ple_bwd",
        grid=(SEQ // tm,),
        in_specs=[pl.BlockSpec((tm, D_MODEL), row), pl.BlockSpec((1, D_MODEL), fixed), pl.BlockSpec((tm, PLE_DIM), row),
                  pl.BlockSpec((tm, D_MODEL), row),
                  pl.BlockSpec((None, D_MODEL, D_MODEL), lambda m: (0, 0, 0)), pl.BlockSpec((D_MODEL, PLE_DIM), fixed)],
        out_specs=[pl.BlockSpec((tm, D_MODEL), row),
                   pl.BlockSpec((1, tm, D_MODEL), lead), pl.BlockSpec((1, tm, D_MODEL), lead),
                   pl.BlockSpec((1, tm, D_MODEL), lead), pl.BlockSpec((1, tm, PLE_DIM), lead),
                   pl.BlockSpec((1, D_MODEL), fixed)],
        out_shape=[jax.ShapeDtypeStruct((SEQ, D_MODEL), F32), big, big, big,
                   pltpu.HBM((1, SEQ, PLE_DIM), BF16),
                   jax.ShapeDtypeStruct((1, D_MODEL), F32)],
        compiler_params=_cparams(("arbitrary",)),
    )(*_in_hbm(h, g, p, dh, wgate, wprojt)), (1, 2, 3, 4))


def _final_loss(h, g, target):
    tm = TOK_TILE

    def body(h_ref, g_ref, t_ref, loss_ref, dh_ref, dg_ref):
        first = pl.program_id(0) == 0
        xh, r = _rms_stats(h_ref[...])
        diff = xh * g_ref[...] - t_ref[...]
        part = 0.5 * jnp.sum(jnp.mean(diff * diff, axis=-1, keepdims=True), axis=0, keepdims=True)
        _accumulate(loss_ref, first, jnp.broadcast_to(part, (SUBLANES, LANES)))
        dx, dg = _rms_bwd(diff * (1.0 / D_MODEL), xh, r, g_ref[...])
        dh_ref[...] = dx
        _accumulate(dg_ref, first, dg)

    row = lambda m: (m, 0)
    fixed = lambda m: (0, 0)
    return pl.pallas_call(
        body, name="final_loss",
        grid=(SEQ // tm,),
        in_specs=[pl.BlockSpec((tm, D_MODEL), row), pl.BlockSpec((1, D_MODEL), fixed),
                  pl.BlockSpec((tm, D_MODEL), row)],
        out_specs=[pl.BlockSpec((SUBLANES, LANES), fixed),
                   pl.BlockSpec((tm, D_MODEL), row),
                   pl.BlockSpec((1, D_MODEL), fixed)],
        out_shape=[jax.ShapeDtypeStruct((SUBLANES, LANES), F32),
                   jax.ShapeDtypeStruct((SEQ, D_MODEL), F32),
                   jax.ShapeDtypeStruct((1, D_MODEL), F32)],
        compiler_params=_cparams(("arbitrary",)),
    )(*_in_hbm(h, g, target))


def _disc(ar, ai, ldt):
    dt = jnp.exp(ldt)
    mag = jnp.exp(ar * dt)
    ph = ai * dt
    lr, li = mag * jnp.cos(ph), mag * jnp.sin(ph)
    nr, ni = lr - 1.0, li
    den = ar * ar + ai * ai
    return lr, li, (nr * ar + ni * ai) / den, (ni * ar - nr * ai) / den


def _s5_disc(a_row, ldt_row, a_rep, ldt_rep, bt, ct, tile_e, mask):
    n = N_STATE

    def body(ar_ref, lr_ref, ap_ref, lp_ref, b_ref, c_ref, e_ref, m_ref, lt_ref, ltr_ref, bb_ref, cc_ref):
        lr, li, _, _ = _disc(ar_ref[0], ar_ref[1], lr_ref[...])
        pr, pi = lr, li
        rid = _row_ids(SUBLANES, n)
        for k in range(1, 9):
            for ref, sgn, edge in ((lt_ref, 1.0, 24 + k - 1), (ltr_ref, -1.0, 24 + 8 - k)):
                if k in (1, 2, 4):
                    r0 = {1: 0, 2: 8, 4: 16}[k]
                    keep = (rid >= k) if ref is lt_ref else (rid < SUBLANES - k)
                    ref[r0:r0 + 8, 0:n] = jnp.where(keep, jnp.broadcast_to(pr, (8, n)), 0.0)
                    ref[r0:r0 + 8, n:2 * n] = jnp.where(keep, jnp.broadcast_to(sgn * pi, (8, n)), 0.0)
                ref[edge:edge + 1, 0:n] = pr
                ref[edge:edge + 1, n:2 * n] = sgn * pi
            pr, pi = pr * lr - pi * li, pr * li + pi * lr
        _, _, fr, fi = _disc(ap_ref[0], ap_ref[1], lp_ref[...])
        br, bi = b_ref[0], b_ref[1]
        e = e_ref[...]
        m = m_ref[...].astype(F32)
        bb_ref[:, 0:n] = (_dot((fr * br - fi * bi).astype(BF16), e) * m).astype(BF16)
        bb_ref[:, n:2 * n] = (_dot((fr * bi + fi * br).astype(BF16), e) * m).astype(BF16)
        cc_ref[:, 0:n] = (_dot(c_ref[0].astype(BF16), e) * m).astype(BF16)
        cc_ref[:, n:2 * n] = (-(_dot(c_ref[1].astype(BF16), e) * m)).astype(BF16)

    return pl.pallas_call(
        body, name="s5_disc",
        out_shape=[jax.ShapeDtypeStruct((32, 2 * n), F32), jax.ShapeDtypeStruct((32, 2 * n), F32),
                   jax.ShapeDtypeStruct((SSM_W, 2 * n), BF16), jax.ShapeDtypeStruct((SSM_W, 2 * n), BF16)],
        compiler_params=_cparams(None),
    )(a_row, ldt_row, a_rep, ldt_rep, bt, ct, tile_e, mask)


def _dot_exact(x, sel):
    hi = x.astype(BF16)
    r1 = x - hi.astype(F32)
    mid = r1.astype(BF16)
    lo = (r1 - mid.astype(F32)).astype(BF16)
    return _dot(hi, sel) + _dot(mid, sel) + _dot(lo, sel)


def _s5_disc_bwd(a, ldt, a_rep, ldt_rep, bt, mask, dl, d_bb, d_cc, fold):
    n = N_STATE

    def body(a_ref, l_ref, ap_ref, lp_ref, b_ref, m_ref, dl_ref, dbb_ref, dcc_ref, f_ref,
             da_ref, dldt_ref, db_ref, dc_ref):
        m = m_ref[...].astype(F32)
        fold_m = f_ref[...]
        diag = lambda x: _dot_exact(jnp.where(m > 0.0, x, 0.0), fold_m)
        dr, di = diag(dbb_ref[:, 0:n]), diag(dbb_ref[:, n:2 * n])
        dc_ref[0] = diag(dcc_ref[:, 0:n])
        dc_ref[1] = -diag(dcc_ref[:, n:2 * n])
        _, _, fr, fi = _disc(ap_ref[0], ap_ref[1], lp_ref[...])
        br, bi = b_ref[0], b_ref[1]
        db_ref[0] = fr * dr + fi * di
        db_ref[1] = fr * di - fi * dr
        per_state = lambda x: x.reshape(SSM_GROUPS, SSM_GROUP, SSM_STATE).sum(axis=1)
        dfr = per_state(dr * br + di * bi)
        dfi = per_state(di * br - dr * bi)
        _, vjp = jax.vjp(_disc, a_ref[0], a_ref[1], l_ref[...])
        dar, dai, dldt = vjp((dl_ref[0], dl_ref[1], dfr, dfi))
        da_ref[0] = dar
        da_ref[1] = dai
        dldt_ref[...] = jnp.sum(dldt, axis=1, keepdims=True)

    return pl.pallas_call(
        body, name="s5_disc_bwd",
        out_shape=[jax.ShapeDtypeStruct((2, SSM_GROUPS, SSM_STATE), F32),
                   jax.ShapeDtypeStruct((SSM_GROUPS, 1), F32),
                   jax.ShapeDtypeStruct((2, SSM_W, SSM_STATE), F32),
                   jax.ShapeDtypeStruct((2, SSM_W, SSM_STATE), F32)],
        compiler_params=_cparams(None),
    )(a, ldt, a_rep, ldt_rep, bt, mask, dl, d_bb, d_cc, fold)


def _row_block(rows, cap=512):
    for bm in range(min(cap, rows), 0, -1):
        if rows % bm == 0 and (bm % 8 == 0 or bm == rows):
            return bm
    return rows


SUM_PARTS = 2


def _own_pieces(segs, rtot):
    pr = rtot // SUM_PARTS
    assert pr * SUM_PARTS == rtot and pr % 16 == 0
    offs = _seg_offsets(segs)
    pieces = [[] for _ in range(SUM_PARTS)]
    for a, (n, r) in enumerate(segs):
        for m in range(n):
            lo = offs[a] + m * r
            for h in range(SUM_PARTS):
                clo, chi = max(lo, h * pr), min(lo + r, (h + 1) * pr)
                if chi > clo:
                    pieces[h].append((a, m, clo - lo, clo - h * pr, chi - clo))
    return pieces


def _pair_rows(srcs, got_ref, segs, pieces, h, chip, own_v, got_v, sems):
    pr = own_v.shape[0]
    dev = 2 * chip + lax.axis_index("c")
    for hh in range(SUM_PARTS):
        @pl.when(h == hh)
        def _(hh=hh):
            cps = [pltpu.make_async_copy(got_ref.at[chip, pl.ds(hh * pr, pr), :], got_v, sems.at[0])]
            for i, (a, m, so, do, rows) in enumerate(pieces[hh]):
                start = pl.multiple_of(dev * segs[a][1] + so, 16)
                cps.append(pltpu.make_async_copy(srcs[a].at[m, pl.ds(start, rows), :],
                                                 own_v.at[pl.ds(do, rows), :], sems.at[1 + i]))
            for cp in cps:
                cp.start()
            for cp in cps:
                cp.wait()
    return own_v[...].astype(F32) + got_v[...].astype(F32)


def _pair_sum(fulls, got, segs):
    ns = len(segs)
    _, rtot, c = got.shape
    pieces = _own_pieces(segs, rtot)
    pr = rtot // SUM_PARTS

    def body(*refs):
        srcs = refs[:ns]
        got_ref, pbf_ref, own_v, got_v, sems = refs[ns:]
        x, y, _ = _mesh_pos()
        j = pl.program_id(1)
        chip = jnp.where(j == 0, 2 * (1 - x) + y, jnp.where(j == 1, 2 * x + 1 - y, 2 * (1 - x) + 1 - y))
        pbf_ref[0] = _pair_rows(srcs, got_ref, segs, pieces, pl.program_id(0), chip, own_v, got_v, sems).astype(BF16)

    return pl.pallas_call(
        body, name="pair_sum",
        grid=(SUM_PARTS, 3),
        in_specs=[HBM] * (ns + 1), out_specs=pl.BlockSpec((1, pr, c), lambda h, j: (j, h, 0)),
        out_shape=pltpu.HBM((3, rtot, c), BF16),
        scratch_shapes=[pltpu.VMEM((pr, c), BF16), pltpu.VMEM((pr, c), BF16),
                        pltpu.SemaphoreType.DMA((1 + max(len(p) for p in pieces),))],
        compiler_params=_cparams(("arbitrary", "arbitrary")),
    )(*_in_hbm(*fulls, got))


def _chip_sum(fulls, got, rb, segs, layer, into):
    ns = len(segs)
    _, rtot, c = got.shape
    pieces = _own_pieces(segs, rtot)
    pr = rtot // SUM_PARTS

    def body(*refs):
        srcs = refs[:ns]
        got_ref, r_ref = refs[ns], refs[ns + 1]
        s_ref, own_v, got_v, sems = refs[-4:]
        x, y, _ = _mesh_pos()
        own = _pair_rows(srcs, got_ref, segs, pieces, pl.program_id(0), 2 * x + y, own_v, got_v, sems)
        s_ref[0] = ((own + r_ref[0].astype(F32)) + r_ref[1].astype(F32)) + r_ref[2].astype(F32)

    old = [] if into is None else [into]
    return pl.pallas_call(
        body, name="chip_sum",
        grid=(SUM_PARTS,),
        in_specs=[HBM] * (ns + 1) + [pl.BlockSpec((3, pr, c), lambda h: (0, h, 0))] + [HBM] * len(old),
        out_specs=pl.BlockSpec((1, pr, c), lambda h: (layer, h, 0)),
        out_shape=jax.ShapeDtypeStruct((DEPTH, rtot, c), F32),
        input_output_aliases={ns + 2: 0} if old else {},
        scratch_shapes=[pltpu.VMEM((pr, c), BF16), pltpu.VMEM((pr, c), BF16),
                        pltpu.SemaphoreType.DMA((1 + max(len(p) for p in pieces),))],
        compiler_params=_cparams(("arbitrary",)),
    )(*_in_hbm(*fulls, got, rb), *old)


def _adamw(w, g, m, v):
    r, c = w.shape
    bm = _row_block(r)
    bc1 = 1.0 - ADAM_B1 ** ADAM_STEP
    bc2 = 1.0 - ADAM_B2 ** ADAM_STEP

    def body(w_ref, g_ref, m_ref, v_ref, d_ref, nm_ref, nv_ref):
        gv = g_ref[...]
        nm = ADAM_B1 * m_ref[...] + (1.0 - ADAM_B1) * gv
        nv = ADAM_B2 * v_ref[...] + (1.0 - ADAM_B2) * (gv * gv)
        nm_ref[...] = nm
        nv_ref[...] = nv
        d_ref[...] = -ADAM_LR * ((nm / bc1) / (jnp.sqrt(nv / bc2) + ADAM_EPS) + ADAM_WD * w_ref[...])

    spec = pl.BlockSpec((bm, c), lambda k: (k, 0))
    shp = jax.ShapeDtypeStruct((r, c), F32)
    return pl.pallas_call(
        body, name="adamw",
        grid=(r // bm,),
        in_specs=[spec] * 4, out_specs=[spec] * 3, out_shape=[shp] * 3,
        compiler_params=_cparams(("parallel",)),
    )(*_in_hbm(w, g, m, v))


def _adamw_layers(sets, first, nl, prev):
    ns = len(sets)
    depth, r, c = sets[0][0].shape
    bm = _row_block(r, min(512, max(SUBLANES, (24 * MIB) // (ns * 8 * 2 * c * 4))))
    while any(four[4] is not None and four[4] % bm for four in sets):
        bm //= 2
    assert bm % SUBLANES == 0 and r % bm == 0
    bc1 = 1.0 - ADAM_B1 ** ADAM_STEP
    bc2 = 1.0 - ADAM_B2 ** ADAM_STEP

    def body(*refs):
        outs = refs[len(refs) - 4 * ns:]
        for s in range(ns):
            w_ref, m_ref, v_ref, g_ref = refs[4 * s:4 * s + 4]
            go_ref, d_ref, nm_ref, nv_ref = outs[4 * s:4 * s + 4]
            gv = g_ref[...]
            nm = ADAM_B1 * m_ref[...] + (1.0 - ADAM_B1) * gv
            nv = ADAM_B2 * v_ref[...] + (1.0 - ADAM_B2) * (gv * gv)
            go_ref[...] = gv
            nm_ref[...] = nm
            nv_ref[...] = nv
            d_ref[...] = -ADAM_LR * ((nm / bc1) / (jnp.sqrt(nv / bc2) + ADAM_EPS) + ADAM_WD * w_ref[...])

    at = pl.BlockSpec((1, bm, c), lambda i, k: (first + i, k, 0))

    def grad_spec(g_rows):
        if g_rows is None:
            return pl.BlockSpec((1, bm, c), lambda i, k: (i, k, 0))
        return pl.BlockSpec((1, bm, c), lambda i, k: (first + i, g_rows // bm + k, 0))

    shp = jax.ShapeDtypeStruct((depth, r, c), F32)
    old = [] if prev is None else [a for four in prev for a in four]
    flat = pl.pallas_call(
        body, name="adamw_layers",
        grid=(nl, r // bm),
        in_specs=[spec for four in sets for spec in (at, at, at, grad_spec(four[4]))] + [HBM] * len(old),
        out_specs=[at] * (4 * ns), out_shape=[shp] * (4 * ns),
        input_output_aliases={4 * ns + i: i for i in range(len(old))},
        compiler_params=_cparams(("parallel", "parallel")),
    )(*_in_hbm(*[a for four in sets for a in four[:4]]), *old)
    return [flat[4 * s:4 * s + 4] for s in range(ns)]


def _mesh_pos():
    return lax.axis_index("x"), lax.axis_index("y"), lax.axis_index("c")


def _dev_index(p):
    return 4 * p[0] + 2 * p[1] + p[2]


def _seg_offsets(segs):
    offs, o = [], 0
    for n, r in segs:
        offs.append(o)
        o += n * r
    return offs


def _remote(src, dst, send_sem, recv_sem, to):
    return pltpu.make_async_remote_copy(src_ref=src, dst_ref=dst, send_sem=send_sem, recv_sem=recv_sem,
                                        device_id=to, device_id_type=MESH)


def _allgather(pack, segs, name):
    rtot, c = pack.shape
    ns = len(segs)
    offs = _seg_offsets(segs)
    assert rtot == sum(n * r for n, r in segs)

    def body(pack_ref, *refs):
        outs = refs[:ns]
        send_sems, recv_sems, local_sem = refs[ns:]
        x, y, cc = _mesh_pos()
        me, sib = (x, y, cc), (x, y, 1 - cc)
        chips = [(1 - x, y), (x, 1 - y), (1 - x, 1 - y)]

        def pieces(dev, from_pack):
            res = []
            for a, (n, r) in enumerate(segs):
                for m in range(n):
                    dst = outs[a].at[m, pl.ds(pl.multiple_of(dev * r, r), r), :]
                    src = pack_ref.at[pl.ds(offs[a] + m * r, r), :] if from_pack else dst
                    res.append((src, dst))
            return res

        def push(k, dev, to, from_pack):
            for s, d in pieces(dev, from_pack):
                _remote(s, d, send_sems.at[k], recv_sems.at[k], to).start()

        def whole(k):
            return _remote(pack_ref, pack_ref, send_sems.at[k], recv_sems.at[k], me)

        my_dev = _dev_index(me)
        for s, d in pieces(my_dev, True):
            pltpu.make_async_copy(s, d, local_sem).start()
        push(0, my_dev, sib, True)
        for j, chip in enumerate(chips):
            push(1 + j, my_dev, (*chip, cc), True)
        for j, chip in enumerate(chips):
            whole(1 + j).wait_recv()
            push(4 + j, _dev_index((*chip, cc)), sib, False)
        whole(0).wait_recv()
        for j in range(3):
            whole(4 + j).wait_recv()
        for k in range(7):
            whole(k).wait_send()
        pltpu.make_async_copy(pack_ref, pack_ref, local_sem).wait()

    return pl.pallas_call(
        body, name=name,
        in_specs=[HBM], out_specs=[HBM] * ns,
        out_shape=[jax.ShapeDtypeStruct((n, N_DEV * r, c), pack.dtype) for n, r in segs],
        scratch_shapes=[pltpu.SemaphoreType.DMA((7,)), pltpu.SemaphoreType.DMA((7,)), pltpu.SemaphoreType.DMA],
    )(pack)


HBM = pl.BlockSpec(memory_space=pltpu.HBM)
SEM = pl.BlockSpec(memory_space=pltpu.SEMAPHORE)
VMEM_WHOLE = pl.BlockSpec(memory_space=pltpu.VMEM)
EFFECT = pltpu.SideEffectType.DATAFLOW_SIDE_EFFECTING


def _hbm(a):
    return pltpu.with_memory_space_constraint(a, pltpu.HBM)


def _ag_start(pack, segs, after, name):
    rtot, c = pack.shape
    ns = len(segs)
    offs = _seg_offsets(segs)

    def body(pack_ref, *refs):
        lands = refs[:ns]
        send_sems, recv_sems = refs[ns + 1], refs[ns + 2]
        token = refs[-1]
        x, y, cc = _mesh_pos()
        my_dev = _dev_index((x, y, cc))
        targets = [(x, y, 1 - cc), (1 - x, y, cc), (x, 1 - y, cc), (1 - x, 1 - y, cc)]
        for k, to in enumerate(targets):
            for a, (n, r) in enumerate(segs):
                for m in range(n):
                    _remote(pack_ref.at[pl.ds(offs[a] + m * r, r), :],
                            lands[a].at[m, pl.ds(pl.multiple_of(my_dev * r, r), r), :],
                            send_sems.at[k], recv_sems.at[k], to).start()
        token[...] = jnp.zeros_like(token)

    land_shapes = [(n, N_DEV * r, c) for n, r in segs]
    outs = pl.pallas_call(
        body, name=name,
        in_specs=[HBM] * (1 + ns) + [UNREAD],
        out_specs=[SEM, SEM, HBM] + [HBM] * ns + [VMEM_WHOLE],
        out_shape=[pltpu.SemaphoreType.DMA((4,)), pltpu.SemaphoreType.DMA((4,)), pltpu.HBM(pack.shape, pack.dtype)]
        + [pltpu.HBM(s, pack.dtype) for s in land_shapes] + [jax.ShapeDtypeStruct((SUBLANES, LANES), F32)],
        input_output_aliases={0: 2, **{1 + i: 3 + i for i in range(ns)}},
        compiler_params=pltpu.CompilerParams(has_side_effects=EFFECT),
    )(_hbm(pack), *[_hbm(lax.empty(s, pack.dtype)) for s in land_shapes], _hbm(after))
    return outs[0], outs[1], outs[2], list(outs[3:3 + ns]), outs[-1]


def _ag_wait(send_sems, recv_sems, pack, lands, after, name):
    ns = len(lands)

    def body(pack_ref, *refs):
        send_ref, recv_ref = refs[ns], refs[ns + 1]
        me = _mesh_pos()
        for k in range(4):
            whole = _remote(pack_ref, pack_ref, send_ref.at[k], recv_ref.at[k], me)
            whole.wait_send()
            whole.wait_recv()

    outs = pl.pallas_call(
        body, name=name,
        in_specs=[HBM] * (1 + ns) + [SEM, SEM, UNREAD],
        out_specs=[HBM] * (1 + ns),
        out_shape=[pltpu.HBM(pack.shape, pack.dtype)] + [pltpu.HBM(a.shape, a.dtype) for a in lands],
        input_output_aliases={i: i for i in range(1 + ns)},
        compiler_params=pltpu.CompilerParams(has_side_effects=EFFECT),
    )(pack, *lands, send_sems, recv_sems, _hbm(after))
    return outs[0], list(outs[1:])


def _ag_finish(pack, lands, segs):
    rtot, c = pack.shape
    ns = len(segs)
    offs = _seg_offsets(segs)

    def body(pack_ref, *refs):
        outs = refs[ns:2 * ns]
        stage, send_sems, recv_sems, local_sems = refs[2 * ns:]
        x, y, cc = _mesh_pos()
        me, sib = (x, y, cc), (x, y, 1 - cc)
        chips = [(1 - x, y), (x, 1 - y), (1 - x, 1 - y)]

        def rows(a, m, dev):
            return outs[a].at[m, pl.ds(pl.multiple_of(dev * segs[a][1], segs[a][1]), segs[a][1]), :]

        for j, chip in enumerate(chips):
            dev = _dev_index((*chip, cc))
            for a, (n, r) in enumerate(segs):
                for m in range(n):
                    _remote(rows(a, m, dev), rows(a, m, dev), send_sems.at[j], recv_sems.at[j], sib).start()
        load = pltpu.make_async_copy(pack_ref, stage, local_sems.at[0])
        load.start()
        load.wait()
        my_dev = _dev_index(me)
        for a, (n, r) in enumerate(segs):
            for m in range(n):
                pltpu.make_async_copy(stage.at[pl.ds(offs[a] + m * r, r), :], rows(a, m, my_dev), local_sems.at[1]).start()
        pltpu.make_async_copy(stage, pack_ref, local_sems.at[1]).wait()
        for j in range(3):
            _remote(pack_ref, pack_ref, send_sems.at[j], recv_sems.at[j], me).wait()

    outs = pl.pallas_call(
        body, name="ag_finish",
        in_specs=[HBM] * (1 + ns), out_specs=[HBM] * ns,
        out_shape=[pltpu.HBM(a.shape, a.dtype) if r >= 128 else jax.ShapeDtypeStruct(a.shape, a.dtype)
                   for a, (_, r) in zip(lands, segs)],
        input_output_aliases={1 + i: i for i in range(ns)},
        scratch_shapes=[pltpu.VMEM((rtot, c), pack.dtype), pltpu.SemaphoreType.DMA((3,)),
                        pltpu.SemaphoreType.DMA((3,)), pltpu.SemaphoreType.DMA((2,))],
        compiler_params=_cparams(None, 16),
    )(pack, *lands)
    return list(outs)


def _rs_chips_start(pbf, after, name):
    _, rtot, c = pbf.shape

    def body(pbf_ref, land_ref, after_ref, send_sems, recv_sems, pbf_thru, land_thru, token):
        x, y, cc = _mesh_pos()
        for j, (cx, cy) in enumerate([(1 - x, y), (x, 1 - y), (1 - x, 1 - y)]):
            _remote(pbf_ref.at[j], land_ref.at[j], send_sems.at[j], recv_sems.at[j], (cx, cy, cc)).start()
        token[...] = jnp.zeros_like(token)

    return pl.pallas_call(
        body, name=name,
        in_specs=[HBM, HBM, UNREAD],
        out_specs=[SEM, SEM, HBM, HBM, VMEM_WHOLE],
        out_shape=[pltpu.SemaphoreType.DMA((3,)), pltpu.SemaphoreType.DMA((3,)), pltpu.HBM(pbf.shape, pbf.dtype),
                   pltpu.HBM((3, rtot, c), pbf.dtype), jax.ShapeDtypeStruct((SUBLANES, LANES), F32)],
        input_output_aliases={0: 2, 1: 3},
        compiler_params=pltpu.CompilerParams(has_side_effects=EFFECT),
    )(_hbm(pbf), _hbm(lax.empty((3, rtot, c), pbf.dtype)), _hbm(after))


def _rs_chips_wait(send_sems, recv_sems, pbf, land, after, name):
    def body(pbf_ref, land_ref, send_ref, recv_ref, after_ref, pbf_out, land_out):
        me = _mesh_pos()
        for j in range(3):
            cp = _remote(pbf_ref.at[0], land_ref.at[j], send_ref.at[j], recv_ref.at[j], me)
            cp.wait_send()
            cp.wait_recv()

    return pl.pallas_call(
        body, name=name,
        in_specs=[HBM, HBM, SEM, SEM, UNREAD], out_specs=[HBM, HBM],
        out_shape=[pltpu.HBM(pbf.shape, pbf.dtype), pltpu.HBM(land.shape, land.dtype)],
        input_output_aliases={0: 0, 1: 1},
        compiler_params=pltpu.CompilerParams(has_side_effects=EFFECT),
    )(pbf, land, send_sems, recv_sems, _hbm(after))[1]


def _flips():
    return [(dx, dy, dc) for dx in (0, 1) for dy in (0, 1) for dc in (0, 1) if dx or dy or dc]


def _small_gather_start(flat, name):
    r, c = flat.shape

    def body(flat_ref, land_ref, send_sems, recv_sems, flat_thru, land_thru, token):
        x, y, cc = _mesh_pos()
        mine = land_ref.at[_dev_index((x, y, cc))]
        for k, (dx, dy, dc) in enumerate(_flips()):
            to = (1 - x if dx else x, 1 - y if dy else y, 1 - cc if dc else cc)
            _remote(flat_ref, mine, send_sems.at[k], recv_sems.at[k], to).start()
        token[...] = jnp.zeros_like(token)

    return pl.pallas_call(
        body, name=name,
        in_specs=[HBM, HBM],
        out_specs=[SEM, SEM, HBM, HBM, VMEM_WHOLE],
        out_shape=[pltpu.SemaphoreType.DMA((7,)), pltpu.SemaphoreType.DMA((7,)), pltpu.HBM(flat.shape, flat.dtype),
                   pltpu.HBM((N_DEV, r, c), flat.dtype), jax.ShapeDtypeStruct((SUBLANES, LANES), F32)],
        input_output_aliases={0: 2, 1: 3},
        compiler_params=pltpu.CompilerParams(has_side_effects=EFFECT),
    )(_hbm(flat), _hbm(lax.empty((N_DEV, r, c), flat.dtype)))


def _small_gather_wait(send_sems, recv_sems, flat, land, after, name):
    def body(flat_ref, land_ref, send_ref, recv_ref, after_ref, flat_out, land_out):
        me = _mesh_pos()
        for k in range(N_DEV - 1):
            cp = _remote(flat_ref, land_ref.at[0], send_ref.at[k], recv_ref.at[k], me)
            cp.wait_send()
            cp.wait_recv()

    return pl.pallas_call(
        body, name=name,
        in_specs=[HBM, HBM, SEM, SEM, UNREAD], out_specs=[HBM, HBM],
        out_shape=[pltpu.HBM(flat.shape, flat.dtype), pltpu.HBM(land.shape, land.dtype)],
        input_output_aliases={0: 0, 1: 1},
        compiler_params=pltpu.CompilerParams(has_side_effects=EFFECT),
    )(flat, land, send_sems, recv_sems, _hbm(after))


def _sum_devices(land, own):
    _, r, c = land.shape

    def body(land_ref, own_ref, out_ref):
        me = _dev_index(_mesh_pos())
        total = None
        for d in range(N_DEV):
            other = land_ref[jnp.where(d == me, (d + 1) % N_DEV, d)]
            block = jnp.where(d == me, own_ref[...], other)
            total = block if total is None else total + block
        out_ref[...] = total

    return pl.pallas_call(
        body, name="sum_devices",
        grid=(1,),
        in_specs=[pl.BlockSpec((N_DEV, r, c), lambda i: (0, 0, 0)), pl.BlockSpec((r, c), lambda i: (0, 0))],
        out_specs=pl.BlockSpec((r, c), lambda i: (0, 0)),
        out_shape=jax.ShapeDtypeStruct((r, c), F32),
        compiler_params=_cparams(("arbitrary",)),
    )(land, own)


def _rs_sibling_start(fulls, segs, name):
    ns = len(segs)
    offs = _seg_offsets(segs)
    rtot = sum(n * r for n, r in segs)
    c = fulls[0].shape[-1]
    dt = fulls[0].dtype

    def body(*refs):
        srcs = refs[:ns]
        land_ref, send_sem, recv_sem = refs[ns], refs[ns + 1], refs[ns + 2]
        token = refs[-1]
        x, y, cc = _mesh_pos()
        for k in range(4):
            for a, (n, r) in enumerate(segs):
                for m in range(n):
                    theirs = srcs[a].at[m, pl.ds(pl.multiple_of((2 * k + 1 - cc) * r, r), r), :]
                    _remote(theirs, land_ref.at[k, pl.ds(offs[a] + m * r, r), :], send_sem, recv_sem,
                            (x, y, 1 - cc)).start()
        token[...] = jnp.zeros_like(token)

    outs = pl.pallas_call(
        body, name=name,
        in_specs=[HBM] * (ns + 1),
        out_specs=[SEM, SEM] + [HBM] * (ns + 1) + [VMEM_WHOLE],
        out_shape=[pltpu.SemaphoreType.DMA(()), pltpu.SemaphoreType.DMA(())]
        + [pltpu.HBM(a.shape, a.dtype) for a in fulls] + [pltpu.HBM((4, rtot, c), dt),
                                                           jax.ShapeDtypeStruct((SUBLANES, LANES), F32)],
        input_output_aliases={i: 2 + i for i in range(ns + 1)},
        compiler_params=pltpu.CompilerParams(has_side_effects=EFFECT),
    )(*[_hbm(a) for a in fulls], _hbm(lax.empty((4, rtot, c), dt)))
    return outs[0], outs[1], list(outs[2:2 + ns]), outs[2 + ns], outs[-1]


def _rs_sibling_wait(send_sem, recv_sem, fulls, land, after, name):
    ns = len(fulls)

    def body(*refs):
        land_ref, send_ref, recv_ref = refs[ns], refs[ns + 1], refs[ns + 2]
        whole = _remote(land_ref, land_ref, send_ref, recv_ref, _mesh_pos())
        whole.wait_send()
        whole.wait_recv()

    outs = pl.pallas_call(
        body, name=name,
        in_specs=[HBM] * (ns + 1) + [SEM, SEM, UNREAD], out_specs=[HBM] * (ns + 1),
        out_shape=[pltpu.HBM(a.shape, a.dtype) for a in fulls] + [pltpu.HBM(land.shape, land.dtype)],
        input_output_aliases={i: i for i in range(ns + 1)},
        compiler_params=pltpu.CompilerParams(has_side_effects=EFFECT),
    )(*fulls, land, send_sem, recv_sem, _hbm(after))
    return list(outs[:ns]), outs[ns]


def _tp(w):
    return jnp.swapaxes(w, -1, -2)


def _s5_prepare(a_re, a_im, log_dt, b_re, b_im, c_re, c_im):
    a = jnp.stack([a_re, a_im], axis=1)
    ldt = jnp.broadcast_to(log_dt[:, :, None], (DEPTH, SSM_GROUPS, SSM_STATE))
    a_row = a.reshape(DEPTH, 2, 1, N_STATE)
    ldt_row = ldt.reshape(DEPTH, 1, N_STATE)
    a_rep = jnp.repeat(a, SSM_GROUP, axis=2)
    ldt_rep = jnp.repeat(ldt, SSM_GROUP, axis=1)
    bt = jnp.stack([_tp(b_re), _tp(b_im)], axis=1).reshape(DEPTH, 2, SSM_W, SSM_STATE)
    ct = jnp.stack([c_re, c_im], axis=1).reshape(DEPTH, 2, SSM_W, SSM_STATE)
    tile_e = jnp.tile(jnp.eye(SSM_STATE, dtype=BF16), (1, SSM_GROUPS))
    mask = jnp.repeat(jnp.repeat(jnp.eye(SSM_GROUPS, dtype=BF16), SSM_GROUP, axis=0), SSM_STATE, axis=1)
    out = []
    for l in range(DEPTH):
        tabs = _s5_disc(a_row[l], ldt_row[l], a_rep[l], ldt_rep[l], bt[l], ct[l], tile_e, mask)
        out.append(((a[l], ldt[l], a_rep[l], ldt_rep[l], bt[l], mask), *tabs))
    return out


def _layer_fwd(h, p_l, small, big, arrive=None):
    saved = {'h0': h}
    if arrive is not None:
        arrive(0, h)
    h, saved['gu1'] = _ffn_fwd(h, small['ffn1_norm'], big['ff1'])
    saved['h1'] = h
    if arrive is not None:
        arrive(1, h)
    z = _inproj_fwd(h, small['mix_norm'], big['wint'])
    ya, ys, hs = _s5conv_fwd(z, small['conv_w'], small['conv_b'], small['bbmat'], small['ccmat'], small['dvec'],
                             small['ltab'])
    saved.update(z=z, ya=ya, ys=ys, hs=hs)
    h = _mix_out_fwd(h, ya, ys, big['glu'], small['glu_b'], small['conv_out_norm'], small['ssm_out_norm'], big['wout'])
    saved['h2'] = h
    if arrive is not None:
        arrive(2, h)
    h, saved['gu2'] = _ffn_fwd(h, small['ffn2_norm'], big['ff2'])
    saved['h3'] = h
    h = _ple_fwd(h, small['ple_norm'], p_l, big['plg'], big['plpt'])
    return h, saved


def _ffn_bwd(h_in, g, dh, gu, w3):
    dh_in, dga, ud, dg = _ffn_bwd_act(h_in, g, dh, gu, w3)
    return dh_in, _matmul_tn(dga, ud, FF_BLOCK, BF16, "ffn_wgrad"), dg


def _layer_bwd_top(dh, p_l, small, big, saved):
    gs = {}
    dh, u, dq, dpp, pb, gs['ple_norm'] = _ple_bwd(saved['h3'], small['ple_norm'], p_l, dh, big['plg'], big['plpt'])
    d_plg = _matmul_tn(u, dq, 256, BF16, "ple_gate_wgrad")
    d_plpt = _matmul_tn(dpp, pb, 256, BF16, "ple_proj_wgrad", to_kernel=False)
    dh, d_ff2, gs['ffn2_norm'] = _ffn_bwd(saved['h2'], small['ffn2_norm'], dh, saved['gu2'], big['ff2'])
    return dh, (gs, d_plg, d_plpt, d_ff2)


def _layer_bwd_rest(dh, top, small, big, saved):
    gs, d_plg, d_plpt, d_ff2 = top
    dya, dys, ycat, dhb, zg, dq, part = _mix_out_bwd(dh, saved['ya'], saved['ys'], big['glu'], small['glu_b'],
                                                     small['conv_out_norm'], small['ssm_out_norm'], big['wout'])
    d_wout = _matmul_tn(ycat, dhb, 256, BF16, "w_out_wgrad")
    d_glu = _matmul_tn(zg, dq, 256, BF16, "glu_wgrad", to_kernel=False)
    dz, gadj, us, dyb, dl, dcw = _s5conv_bwd(saved['z'], saved['hs'], dya, dys, small['conv_w'], small['conv_b'],
                                             small['bbmat'], small['ccmat'], small['dvec'], small['ltab_rev'])
    d_bb = _block_wgrad(us, gadj, "s5_b_wgrad")
    d_cc = _block_wgrad(dyb, saved['hs'][None], "s5_c_wgrad")
    dh, u, gs['mix_norm'] = _inproj_bwd(saved['h1'], small['mix_norm'], dh, dz, big['wint'])
    d_wint = _matmul_tn(dz[None], u, 256, BF16, "w_in_wgrad")
    dh, d_ff1, gs['ffn1_norm'] = _ffn_bwd(saved['h0'], small['ffn1_norm'], dh, saved['gu1'], big['ff1'])

    dlb = dl[0].reshape(2, SSM_GROUPS, SSM_STATE)
    fold = jnp.tile(jnp.eye(SSM_STATE, dtype=BF16), (SSM_GROUPS, 1))
    da, dldt, dbt, dct = _s5_disc_bwd(*small['disc_in'], dlb, d_bb, d_cc, fold)
    gs['ssm_A_re'], gs['ssm_A_im'] = da[0], da[1]
    gs['ssm_log_dt'] = dldt[:, 0]
    ghp = (SSM_GROUPS, SSM_GROUP, SSM_STATE)
    gs['ssm_B_re'], gs['ssm_B_im'] = dbt[0].reshape(ghp), dbt[1].reshape(ghp)
    gs['ssm_C_re'], gs['ssm_C_im'] = dct[0].reshape(ghp), dct[1].reshape(ghp)
    gs['conv_w'] = dcw[0:3]
    gs['conv_b'] = dcw[3]
    gs['ssm_D'] = dcw[4].reshape(SSM_GROUPS, SSM_GROUP)
    gs['conv_out_norm'], gs['ssm_out_norm'], gs['glu_b'] = part[0], part[1], part[2]
    for n in ('ple_norm', 'ffn2_norm', 'mix_norm', 'ffn1_norm'):
        gs[n] = gs[n][0]
    fulls = [d_ff1, d_ff2, d_wint, d_wout, d_plg,
             d_plpt.reshape(1, D_MODEL * PLE_DIM // D_MODEL, D_MODEL), d_glu.reshape(1, SSM_W * SSM_W // D_MODEL, D_MODEL)]
    return dh, fulls, gs


VIEW_T = ('ffn1_w_gate', 'ffn1_w_up', 'ffn2_w_gate', 'ffn2_w_up', 'ssm_B_re', 'ssm_B_im')


def _view(name, a):
    return _tp(a) if name in VIEW_T else a


SEG_NAMES = ('ff1', 'ff2', 'wint', 'wout', 'plg', 'plpt', 'glu')
FIRST_LAYER_GROUPS = ((0,), (2, 3, 6), (1, 4, 5))


def _layer_pack(W, l, segments=range(len(SEGS))):
    pieces = {
        0: lambda: [_tp(W['ffn1_w_gate'][l]), _tp(W['ffn1_w_up'][l]), W['ffn1_w_down'][l]],
        1: lambda: [_tp(W['ffn2_w_gate'][l]), _tp(W['ffn2_w_up'][l]), W['ffn2_w_down'][l]],
        2: lambda: [_tp(W['w_in'][l])],
        3: lambda: [W['w_out'][l]],
        4: lambda: [W['ple_w_gate'][l]],
        5: lambda: [_tp(W['ple_w_proj'][l]).reshape(-1, D_MODEL)],
        6: lambda: [W['glu_w'][l].reshape(-1, D_MODEL)],
    }
    return jnp.concatenate([a for s in segments for a in pieces[s]()], axis=0).astype(BF16)


def _as_big(named):
    shape = dict(plpt=(D_MODEL, PLE_DIM), glu=(SSM_W, SSM_W))
    return {n: (a.reshape(shape[n]) if n in shape else a) for n, a in named.items()}


def _pad_rows(flat, mult, width=LANES):
    per = mult * width
    n = flat.shape[0]
    tot = -(-n // per) * per
    return jnp.pad(flat, (0, tot - n)).reshape(tot // width, width)


def _adamw_any(w, g, m, v):
    shp = w.shape
    two = (lambda t: t.reshape(-1, shp[-1]))
    d, nm, nv = _adamw(two(w), two(g), two(m), two(v))
    return d.reshape(shp), nm.reshape(shp), nv.reshape(shp)


def kernel(x, p, ffn1_norm, ffn1_w_gate, ffn1_w_up, ffn1_w_down, mix_norm, w_in, conv_w, conv_b, ssm_A_re, ssm_A_im, ssm_B_re, ssm_B_im, ssm_C_re, ssm_C_im, ssm_D, ssm_log_dt, glu_w, glu_b, conv_out_norm, ssm_out_norm, w_out, ffn2_norm, ffn2_w_gate, ffn2_w_up, ffn2_w_down, ple_norm, ple_w_gate, ple_w_proj, final_norm, loss_target, m_ffn1_norm, m_ffn1_w_gate, m_ffn1_w_up, m_ffn1_w_down, m_mix_norm, m_w_in, m_conv_w, m_conv_b, m_ssm_A_re, m_ssm_A_im, m_ssm_B_re, m_ssm_B_im, m_ssm_C_re, m_ssm_C_im, m_ssm_D, m_ssm_log_dt, m_glu_w, m_glu_b, m_conv_out_norm, m_ssm_out_norm, m_w_out, m_ffn2_norm, m_ffn2_w_gate, m_ffn2_w_up, m_ffn2_w_down, m_ple_norm, m_ple_w_gate, m_ple_w_proj, m_final_norm, v_ffn1_norm, v_ffn1_w_gate, v_ffn1_w_up, v_ffn1_w_down, v_mix_norm, v_w_in, v_conv_w, v_conv_b, v_ssm_A_re, v_ssm_A_im, v_ssm_B_re, v_ssm_B_im, v_ssm_C_re, v_ssm_C_im, v_ssm_D, v_ssm_log_dt, v_glu_w, v_glu_b, v_conv_out_norm, v_ssm_out_norm, v_w_out, v_ffn2_norm, v_ffn2_w_gate, v_ffn2_w_up, v_ffn2_w_down, v_ple_norm, v_ple_w_gate, v_ple_w_proj, v_final_norm):
    given = dict(locals())
    W = {n: given[n] for n in W_NAMES}
    M = {n: given['m_' + n] for n in W_NAMES}
    V = {n: given['v_' + n] for n in W_NAMES}
    Wv, Mv, Vv = [{n: _view(n, d[n]) for n in W_NAMES} for d in (W, M, V)]
    my_dev = _dev_index(_mesh_pos())

    conv_shard = _pad_rows(W['conv_w'].reshape(-1), SUBLANES)
    conv_all = _allgather(conv_shard, ((1, SUBLANES),), "ag_conv_w")[0]
    conv_full = conv_all.reshape(N_DEV, -1)[:, :DEPTH * 3 * (CONV_W // N_DEV)]
    conv_full = conv_full.reshape(N_DEV, DEPTH, 3, CONV_W // N_DEV).transpose(1, 2, 0, 3).reshape(DEPTH, 3, CONV_W)
    first, after = [], conv_all
    for gi, segments in enumerate(FIRST_LAYER_GROUPS):
        first.append(_ag_start(_layer_pack(W, 0, segments), tuple(SEGS[s] for s in segments), after,
                               "ag_start_0%s" % "abc"[gi]))
        after = first[-1][4]
    packs = [None] + [_layer_pack(W, l) for l in range(1, DEPTH)]
    flights = {1: _ag_start(packs[1], SEGS, after, "ag_start_1")}
    after = flights[1][4]
    s5 = _s5_prepare(*[W[n] + after[0, 0] for n in ('ssm_A_re', 'ssm_A_im', 'ssm_log_dt')],
                     *[W[n] for n in ('ssm_B_re', 'ssm_B_im', 'ssm_C_re', 'ssm_C_im')])
    prepared = conv_full[0, 0:1, 0:1] + s5[DEPTH - 1][1][0:1, 0:1] + packs[DEPTH - 1][0:1, 0:1].astype(F32)

    smalls, saves, bigs = [], [], []
    h = x[0]

    def gathered(handles, segments, after, name, next_layer=None, gate=None):
        send_sems, recv_sems, pack_thru, lands, _ = handles
        pack_thru, lands = _ag_wait(send_sems, recv_sems, pack_thru, lands, after, "ag_wait_" + name)
        if next_layer is not None:
            flights[next_layer] = _ag_start(packs[next_layer], SEGS, pack_thru, "ag_start_%d" % next_layer)
            gate[0][gate[1]] = gate[0][gate[1]] + flights[next_layer][4][0:1, 0:1]
        outs = _ag_finish(pack_thru, lands, tuple(SEGS[s] for s in segments))
        return _as_big({SEG_NAMES[s]: a for s, a in zip(segments, outs)})

    for l in range(DEPTH):
        small = {n: W[n][l][None] for n in ('ffn1_norm', 'mix_norm', 'conv_b', 'glu_b', 'conv_out_norm',
                                            'ssm_out_norm', 'ffn2_norm', 'ple_norm')}
        small['conv_w'] = conv_full[l]
        small['dvec'] = W['ssm_D'][l].reshape(1, SSM_W)
        small['disc_in'], small['ltab'], small['ltab_rev'], small['bbmat'], small['ccmat'] = s5[l]
        big = {}
        bigs.append(big)
        if l == 0:
            def arrive(stage, h_now, big=big, small=small):
                big.update(gathered(first[stage], FIRST_LAYER_GROUPS[stage], prepared if stage == 0 else h_now,
                                    "0%s" % "abc"[stage], *((2, (small, 'ffn2_norm')) if stage == 2 else ())))
            h, saved = _layer_fwd(h, p[l, 0], small, big, arrive)
        else:
            nxt = (l + 2, (small, 'ffn1_norm')) if l + 2 < DEPTH else ()
            big.update(gathered(flights[l], range(len(SEGS)), h, "%d" % l, *nxt))
            h, saved = _layer_fwd(h, p[l, 0], small, big)
        smalls.append(small)
        saves.append(saved)
    loss_tile, dh, d_final = _final_loss(h, W['final_norm'][None], loss_target[0])
    loss = lax.psum(loss_tile[0, 0], ("x", "y", "c"))

    layer_gs = [None] * DEPTH
    shard_grads = None
    sib, ici = None, None

    def finish_sibling(after_sib, after_ici):
        nonlocal sib, ici
        up, (send_sem, recv_sem, fulls_thru, land, _) = sib
        fulls_thru, got = _rs_sibling_wait(send_sem, recv_sem, fulls_thru, land, after_sib, "sib_wait_%d" % up)
        pbf = _pair_sum(fulls_thru, got, SEGS)
        done = finish_chips(after_ici)
        ici = (up, _rs_chips_start(pbf, after_ici if done is None else done, "rs_start_%d" % up), fulls_thru, got)
        sib = None

    def finish_chips(after):
        nonlocal ici, shard_grads
        if ici is None:
            return None
        up, (send_sems, recv_sems, pbf_thru, land, _), fulls_up, got_up = ici
        got3 = _rs_chips_wait(send_sems, recv_sems, pbf_thru, land, after, "rs_wait_%d" % up)
        shard_grads = _chip_sum(fulls_up, got_up, got3, SEGS, up, shard_grads)
        ici = None
        return shard_grads

    layer_names = [n for n in SMALL_NAMES if n != 'final_norm']
    small_flights = [None] * DEPTH
    for l in reversed(range(DEPTH)):
        small = dict(smalls[l])
        if sib is not None:
            small['ple_norm'] = small['ple_norm'] + sib[1][4][0:1, 0:1] + small_flights[l + 1][4][0:1, 0:1]
        dh, top = _layer_bwd_top(dh, p[l, 0], small, bigs[l], saves[l])
        if sib is not None:
            finish_sibling(dh, dh)
            small['glu_b'] = small['glu_b'] + ici[1][4][0:1, 0:1]
        dh, fulls, layer_gs[l] = _layer_bwd_rest(dh, top, small, bigs[l], saves[l])
        sib = (l, _rs_sibling_start(fulls, SEGS, "sib_start_%d" % l))
        last_slot = d_final[0] if l == DEPTH - 1 else jnp.zeros((D_MODEL,), F32)
        flat = jnp.concatenate([layer_gs[l][n].reshape(-1) for n in layer_names + ['conv_w']] + [last_slot])
        small_flights[l] = _small_gather_start(_pad_rows(flat, SUBLANES, D_MODEL), "small_start_%d" % l)
    grad_x = dh[None]
    finish_sibling(small_flights[0][4], small_flights[0][4])

    reduced = []
    for l in range(DEPTH):
        send_sems, recv_sems, flat_thru, land, _ = small_flights[l]
        flat_thru, land = _small_gather_wait(send_sems, recv_sems, flat_thru, land, ici[1][4], "small_wait_%d" % l)
        reduced.append(_sum_devices(land, flat_thru).reshape(-1))
    reduced = jnp.stack(reduced)
    G = {}
    o = 0
    for n in layer_names + ['conv_w']:
        size = (W[n].size if n != 'conv_w' else DEPTH * 3 * CONV_W) // DEPTH
        shape = Wv[n].shape if n != 'conv_w' else (DEPTH, 3, CONV_W)
        G[n] = reduced[:, o:o + size].reshape(shape)
        o += size
    G['final_norm'] = reduced[DEPTH - 1, o:o + D_MODEL]
    G['conv_w'] = lax.dynamic_slice_in_dim(G['conv_w'], my_dev * (CONV_W // N_DEV), CONV_W // N_DEV, axis=2)

    delta, new_m, new_v = {}, {}, {}
    for n in SMALL_NAMES + ['conv_w']:
        two = (lambda t: t.reshape(1, -1) if t.ndim == 1 else t)
        delta[n], new_m[n], new_v[n] = [t.reshape(Wv[n].shape) for t in
                                        _adamw_any(two(Wv[n]), two(G[n]), two(Mv[n]), two(Vv[n]))]

    offs = _seg_offsets(SEGS)
    r = SEGS[0][1]
    packed_rows = {'w_out': offs[3], 'ple_w_gate': offs[4]}
    for a, f in ((0, 'ffn1'), (1, 'ffn2')):
        packed_rows.update({f + '_w_gate': offs[a], f + '_w_up': offs[a] + r, f + '_w_down': offs[a] + 2 * r})

    def relaid(sg):
        nl = sg.shape[0]
        return {'w_in': _tp(sg[:, offs[2]:offs[2] + SEGS[2][1]]),
                'ple_w_proj': _tp(sg[:, offs[5]:offs[5] + SEGS[5][1]].reshape(nl, D_MODEL // N_DEV, PLE_DIM)),
                'glu_w': sg[:, offs[6]:offs[6] + SEGS[6][1]].reshape(nl, SSM_W // N_DEV, SSM_W)}

    groups = {}
    for n in list(packed_rows) + ['w_in', 'ple_w_proj', 'glu_w']:
        groups.setdefault(Wv[n].shape, []).append(n)

    def update(first, nl, prev):
        other = relaid(shard_grads[first:first + nl])
        sets = lambda ns: [(Wv[n], Mv[n], Vv[n], shard_grads, packed_rows[n]) if n in packed_rows
                           else (Wv[n], Mv[n], Vv[n], other[n], None) for n in ns]
        return {shape: _adamw_layers(sets(ns), first, nl, None if prev is None else prev[shape])
                for shape, ns in groups.items()}

    part = update(1, DEPTH - 1, None)
    finish_chips(sum(four[3][1, 0:1, 0:1] for fours in part.values() for four in fours)
                 + sum(new_v[n][(0,) * new_v[n].ndim].reshape(1, 1) for n in SMALL_NAMES + ['conv_w']))
    for shape, fours in update(0, 1, part).items():
        for n, four in zip(groups[shape], fours):
            G[n], delta[n], new_m[n], new_v[n] = four

    outs = [[_view(n, d[n]) for n in W_NAMES] for d in (G, delta, new_m, new_v)]
    return (loss, grad_x, *outs[0], *outs[1], *outs[2], *outs[3])
```

```python
import math

import jax
import jax.numpy as jnp
from jax import lax
from jax.experimental import pallas as pl
from jax.experimental.pallas import tpu as pltpu

F32 = jnp.float32
BF16 = jnp.bfloat16

N_DEV = 8
DEPTH = 4
SEQ = 2048
D_MODEL = 1024
D_FF = 2816
CONV_W = 512
SSM_W = 512
SSM_GROUPS = 32
SSM_GROUP = 16
SSM_STATE = 64
N_STATE = SSM_GROUPS * SSM_STATE
IN_COLS = 2048
PLE_DIM = 256
EPS = 1e-6

ADAM_LR = 0.001
ADAM_B1 = 0.9
ADAM_B2 = 0.999
ADAM_EPS = 1e-08
ADAM_WD = 0.01
ADAM_STEP = 10

FF_BLOCK = 256
N_FF_BLOCKS = D_FF // FF_BLOCK
TOK_TILE_FFN_FWD = 2048
TOK_TILE_FFN_BWD = 1024
TOK_TILE = 512
CHUNK = 256
N_CHUNKS = SEQ // CHUNK
LANE_GROUP = 512
SUBLANES = 8
LANES = 128
MIB = 1024 * 1024

W_NAMES = ['ffn1_norm', 'ffn1_w_gate', 'ffn1_w_up', 'ffn1_w_down', 'mix_norm', 'w_in', 'conv_w', 'conv_b',
           'ssm_A_re', 'ssm_A_im', 'ssm_B_re', 'ssm_B_im', 'ssm_C_re', 'ssm_C_im', 'ssm_D', 'ssm_log_dt',
           'glu_w', 'glu_b', 'conv_out_norm', 'ssm_out_norm', 'w_out', 'ffn2_norm', 'ffn2_w_gate', 'ffn2_w_up',
           'ffn2_w_down', 'ple_norm', 'ple_w_gate', 'ple_w_proj', 'final_norm']
SMALL_NAMES = ['ffn1_norm', 'mix_norm', 'conv_b', 'ssm_A_re', 'ssm_A_im', 'ssm_B_re', 'ssm_B_im', 'ssm_C_re',
               'ssm_C_im', 'ssm_D', 'ssm_log_dt', 'glu_b', 'conv_out_norm', 'ssm_out_norm', 'ffn2_norm',
               'ple_norm', 'final_norm']

SEGS = ((3, 352), (3, 352), (1, 256), (1, 128), (1, 128), (1, 32), (1, 32))
PACK_ROWS = sum(n * r for n, r in SEGS)

MESH = pl.DeviceIdType.MESH
UNREAD = pl.BlockSpec(memory_space=pltpu.HBM)


def _in_hbm(*arrays):
    return [pltpu.with_memory_space_constraint(a, pltpu.HBM) for a in arrays]


def _out_hbm(outs, which):
    if not isinstance(outs, (list, tuple)):
        return pltpu.with_memory_space_constraint(outs, pltpu.HBM) if which else outs
    return [pltpu.with_memory_space_constraint(a, pltpu.HBM) if i in which else a for i, a in enumerate(outs)]


def _cparams(sem=None, vmem_mib=48, **kw):
    return pltpu.CompilerParams(dimension_semantics=sem, vmem_limit_bytes=vmem_mib * MIB, **kw)


def _dot(a, b):
    return jnp.dot(a, b, preferred_element_type=F32)


def _dot_nt(a, b):
    return lax.dot_general(a, b, (((1,), (1,)), ((), ())), preferred_element_type=F32)


def _dot_tn(a, b):
    return lax.dot_general(a, b, (((0,), (0,)), ((), ())), preferred_element_type=F32)


def _rms_stats(x):
    r = lax.rsqrt(jnp.mean(x * x, axis=-1, keepdims=True) + EPS)
    return x * r, r


def _rms_bwd(dy, xh, r, g):
    dxh = dy * g
    dx = r * (dxh - xh * jnp.mean(dxh * xh, axis=-1, keepdims=True))
    dg = jnp.sum(dy * xh, axis=0, keepdims=True)
    return dx, dg


def _sigmoid(x):
    return 0.5 * jnp.tanh(0.5 * x) + 0.5


_GELU_C = math.sqrt(2.0 / math.pi)


def _gelu(x):
    t = jnp.tanh(_GELU_C * (x + 0.044715 * x * x * x))
    return 0.5 * x * (1.0 + t), t


def _gelu_grad(x, t):
    return 0.5 * (1.0 + t) + 0.5 * x * (1.0 - t * t) * _GELU_C * (1.0 + 3.0 * 0.044715 * x * x)


def _accumulate(ref, first, value):
    @pl.when(first)
    def _():
        ref[...] = value

    @pl.when(jnp.logical_not(first))
    def _():
        ref[...] += value


def _ffn_fwd(h, g, w3):
    tm = TOK_TILE_FFN_FWD
    last = N_FF_BLOCKS - 1

    def body(h_ref, g_ref, wgu_ref, wd_ref, wd_last_ref, out_ref, gu_ref, u_ref, a_ref):
        k = pl.program_id(1)

        @pl.when(k == 0)
        def _():
            x = h_ref[...]
            xh, _ = _rms_stats(x)
            u_ref[...] = (xh * g_ref[...]).astype(BF16)
            out_ref[...] = x
            a_ref[1] = jnp.zeros((tm, FF_BLOCK), BF16)

        out_ref[...] += 0.5 * _dot(a_ref[(k + 1) % 2], wd_ref[0])
        gu = _dot_nt(u_ref[...], wgu_ref[...].reshape(2 * FF_BLOCK, D_MODEL))
        gate, up = gu[:, :FF_BLOCK], gu[:, FF_BLOCK:]
        a_ref[k % 2] = (gate * _sigmoid(gate) * up).astype(BF16)
        gu_ref[0] = gate.astype(BF16)
        gu_ref[1] = up.astype(BF16)

        @pl.when(k == last)
        def _():
            out_ref[...] += 0.5 * _dot(a_ref[last % 2], wd_last_ref[0])

    return _out_hbm(pl.pallas_call(
        body, name="ffn_fwd",
        grid=(SEQ // tm, N_FF_BLOCKS),
        in_specs=[pl.BlockSpec((tm, D_MODEL), lambda m, k: (m, 0), pipeline_mode=pl.Buffered(1)),
                  pl.BlockSpec((1, D_MODEL), lambda m, k: (0, 0)),
                  pl.BlockSpec((2, FF_BLOCK, D_MODEL), lambda m, k: (0, k, 0)),
                  pl.BlockSpec((1, FF_BLOCK, D_MODEL), lambda m, k: (2, jnp.maximum(k - 1, 0), 0)),
                  pl.BlockSpec((1, FF_BLOCK, D_MODEL), lambda m, k: (2, last, 0), pipeline_mode=pl.Buffered(1))],
        out_specs=[pl.BlockSpec((tm, D_MODEL), lambda m, k: (m, 0)),
                   pl.BlockSpec((2, tm, FF_BLOCK), lambda m, k: (0, m, k))],
        out_shape=[jax.ShapeDtypeStruct((SEQ, D_MODEL), F32),
                   pltpu.HBM((2, SEQ, D_FF), BF16)],
        scratch_shapes=[pltpu.VMEM((tm, D_MODEL), BF16), pltpu.VMEM((2, tm, FF_BLOCK), BF16)],
        compiler_params=_cparams(("parallel", "arbitrary"), 56),
    )(*_in_hbm(h, g, w3, w3, w3)), (1,))


def _ffn_bwd_act(h, g, dout, gu, w3):
    tm = TOK_TILE_FFN_BWD
    last = N_FF_BLOCKS - 1

    def body(h_ref, g_ref, d_ref, gu_ref, wd_ref, wgu_ref, wgu_last_ref, dh_ref, dga_ref, ud_ref, dg_ref,
             acc_ref, dgu_ref):
        m = pl.program_id(0)
        k = pl.program_id(1)

        @pl.when(k == 0)
        def _():
            xh, _ = _rms_stats(h_ref[...])
            ud_ref[0] = (xh * g_ref[...]).astype(BF16)
            ud_ref[1] = (0.5 * d_ref[...]).astype(BF16)
            acc_ref[...] = jnp.zeros_like(acc_ref)
            dgu_ref[1] = jnp.zeros((tm, 2 * FF_BLOCK), BF16)

        acc_ref[...] += _dot(dgu_ref[(k + 1) % 2], wgu_ref[...].reshape(2 * FF_BLOCK, D_MODEL))
        gate = gu_ref[0].astype(F32)
        up = gu_ref[1].astype(F32)
        sg = _sigmoid(gate)
        silu = gate * sg
        da = _dot_nt(ud_ref[1], wd_ref[0])
        dgate = (da * up * (sg + silu * (1.0 - sg))).astype(BF16)
        dup = (da * silu).astype(BF16)
        dga_ref[0] = dgate
        dga_ref[1] = dup
        dga_ref[2] = (silu * up).astype(BF16)
        dgu_ref[k % 2, :, 0:FF_BLOCK] = dgate
        dgu_ref[k % 2, :, FF_BLOCK:2 * FF_BLOCK] = dup

        @pl.when(k == last)
        def _():
            du = acc_ref[...] + _dot(dgu_ref[last % 2], wgu_last_ref[...].reshape(2 * FF_BLOCK, D_MODEL))
            xh, r = _rms_stats(h_ref[...])
            dx, dg = _rms_bwd(du, xh, r, g_ref[...])
            dh_ref[...] = d_ref[...] + dx
            _accumulate(dg_ref, m == 0, dg)

    return _out_hbm(pl.pallas_call(
        body, name="ffn_bwd_act",
        grid=(SEQ // tm, N_FF_BLOCKS),
        in_specs=[pl.BlockSpec((tm, D_MODEL), lambda m, k: (m, 0), pipeline_mode=pl.Buffered(1)),
                  pl.BlockSpec((1, D_MODEL), lambda m, k: (0, 0)),
                  pl.BlockSpec((tm, D_MODEL), lambda m, k: (m, 0), pipeline_mode=pl.Buffered(1)),
                  pl.BlockSpec((2, tm, FF_BLOCK), lambda m, k: (0, m, k)),
                  pl.BlockSpec((1, FF_BLOCK, D_MODEL), lambda m, k: (2, k, 0)),
                  pl.BlockSpec((2, FF_BLOCK, D_MODEL), lambda m, k: (0, jnp.maximum(k - 1, 0), 0)),
                  pl.BlockSpec((2, FF_BLOCK, D_MODEL), lambda m, k: (0, last, 0), pipeline_mode=pl.Buffered(1))],
        out_specs=[pl.BlockSpec((tm, D_MODEL), lambda m, k: (m, 0)),
                   pl.BlockSpec((3, tm, FF_BLOCK), lambda m, k: (0, m, k)),
                   pl.BlockSpec((2, tm, D_MODEL), lambda m, k: (0, m, 0)),
                   pl.BlockSpec((1, D_MODEL), lambda m, k: (0, 0))],
        out_shape=[jax.ShapeDtypeStruct((SEQ, D_MODEL), F32),
                   pltpu.HBM((3, SEQ, D_FF), BF16),
                   pltpu.HBM((2, SEQ, D_MODEL), BF16),
                   jax.ShapeDtypeStruct((1, D_MODEL), F32)],
        scratch_shapes=[pltpu.VMEM((tm, D_MODEL), F32), pltpu.VMEM((2, tm, 2 * FF_BLOCK), BF16)],
        compiler_params=_cparams(("arbitrary", "arbitrary"), 56),
    )(*_in_hbm(h, g, dout, gu, w3, w3, w3)), (1, 2))


def _matmul_tn(a, b, bm, out_dtype, name, bn=None, to_kernel=True):
    na, t, m = a.shape
    nb, _, n = b.shape
    bn = n if bn is None else bn

    def body(a_ref, b_ref, o_ref):
        o_ref[0] = _dot_tn(a_ref[0], b_ref[0]).astype(out_dtype)

    return _out_hbm(pl.pallas_call(
        body, name=name,
        grid=(na, m // bm, n // bn),
        in_specs=[pl.BlockSpec((1, t, bm), lambda i, k, j: (i, 0, k)),
                  pl.BlockSpec((1, t, bn), lambda i, k, j: (jnp.maximum(i - (na - nb), 0), 0, j))],
        out_specs=pl.BlockSpec((1, bm, bn), lambda i, k, j: (i, k, j)),
        out_shape=pltpu.HBM((na, m, n), out_dtype) if to_kernel else jax.ShapeDtypeStruct((na, m, n), out_dtype),
        compiler_params=_cparams(("arbitrary", "parallel", "parallel")),
    )(*_in_hbm(a, b)), to_kernel)


def _inproj_fwd(h, g, wint):
    tm = TOK_TILE

    def body(h_ref, g_ref, w_ref, z_ref):
        xh, _ = _rms_stats(h_ref[...])
        z_ref[...] = _dot_nt((xh * g_ref[...]).astype(BF16), w_ref[...])

    return pl.pallas_call(
        body, name="inproj_fwd",
        grid=(SEQ // tm,),
        in_specs=[pl.BlockSpec((tm, D_MODEL), lambda m: (m, 0)),
                  pl.BlockSpec((1, D_MODEL), lambda m: (0, 0)),
                  pl.BlockSpec((None, IN_COLS, D_MODEL), lambda m: (0, 0, 0))],
        out_specs=pl.BlockSpec((tm, IN_COLS), lambda m: (m, 0)),
        out_shape=jax.ShapeDtypeStruct((SEQ, IN_COLS), F32),
        compiler_params=_cparams(("parallel",)),
    )(*_in_hbm(h, g, wint))


def _inproj_bwd(h, g, dh, dz, wint):
    tm = TOK_TILE

    def body(h_ref, g_ref, dh_ref, dz_ref, w_ref, o_ref, u_ref, dg_ref):
        xh, r = _rms_stats(h_ref[...])
        u_ref[0] = (xh * g_ref[...]).astype(BF16)
        dx, dg = _rms_bwd(_dot(dz_ref[...], w_ref[...]), xh, r, g_ref[...])
        o_ref[...] = dh_ref[...] + dx
        _accumulate(dg_ref, pl.program_id(0) == 0, dg)

    return _out_hbm(pl.pallas_call(
        body, name="inproj_bwd",
        grid=(SEQ // tm,),
        in_specs=[pl.BlockSpec((tm, D_MODEL), lambda m: (m, 0)),
                  pl.BlockSpec((1, D_MODEL), lambda m: (0, 0)),
                  pl.BlockSpec((tm, D_MODEL), lambda m: (m, 0)),
                  pl.BlockSpec((tm, IN_COLS), lambda m: (m, 0)),
                  pl.BlockSpec((None, IN_COLS, D_MODEL), lambda m: (0, 0, 0))],
        out_specs=[pl.BlockSpec((tm, D_MODEL), lambda m: (m, 0)),
                   pl.BlockSpec((1, tm, D_MODEL), lambda m: (0, m, 0)),
                   pl.BlockSpec((1, D_MODEL), lambda m: (0, 0))],
        out_shape=[jax.ShapeDtypeStruct((SEQ, D_MODEL), F32),
                   pltpu.HBM((1, SEQ, D_MODEL), BF16),
                   jax.ShapeDtypeStruct((1, D_MODEL), F32)],
        compiler_params=_cparams(("arbitrary",)),
    )(*_in_hbm(h, g, dh, dz, wint)), (1,))


def _row_ids(n, w):
    return lax.broadcasted_iota(jnp.int32, (n, w), 0)


def _bcast_row(x, i, n):
    return jnp.broadcast_to(x[i:i + 1, :], (n, x.shape[1]))


def _conv_taps(v, tail):
    n, w = v.shape
    rid = _row_ids(n, w)
    v1 = jnp.where(rid == 0, _bcast_row(tail, 7, n), pltpu.roll(v, 1, 0))
    v2 = jnp.where(rid == 0, _bcast_row(tail, 6, n),
                   jnp.where(rid == 1, _bcast_row(tail, 7, n), pltpu.roll(v, 2, 0)))
    return v1, v2


def _block_tiles():
    half_rows, half_cols = SSM_W // 2, N_STATE // 2
    for half in range(2):
        for part in range(2):
            yield (slice(half * half_rows, (half + 1) * half_rows),
                   slice(part * N_STATE + half * half_cols, part * N_STATE + (half + 1) * half_cols))


def _block_expand(x, mat_ref, out_ref):
    for rows, cols in _block_tiles():
        out_ref[:, cols] = _dot(x[:, rows], mat_ref[rows, cols])


def _block_contract(s, mat_ref):
    halves = {}
    for rows, cols in _block_tiles():
        part = _dot_nt(s[:, cols], mat_ref[rows, cols])
        halves[rows.start] = part if rows.start not in halves else halves[rows.start] + part
    return jnp.concatenate([halves[k] for k in sorted(halves)], axis=1)


def _block_wgrad(a, b, name):
    t = a.shape[1]
    half_rows, half_cols = SSM_W // 2, N_STATE // 2

    def body(a_ref, b_ref, o_ref):
        o_ref[...] = _dot_tn(a_ref[...], b_ref[...])

    return pl.pallas_call(
        body, name=name,
        grid=(2, 2),
        in_specs=[pl.BlockSpec((None, t, half_rows), lambda h, p: (0, 0, h)),
                  pl.BlockSpec((None, t, half_cols), lambda h, p: (0, 0, 2 * p + h))],
        out_specs=pl.BlockSpec((half_rows, half_cols), lambda h, p: (h, 2 * p + h)),
        out_shape=jax.ShapeDtypeStruct((SSM_W, 2 * N_STATE), F32),
        compiler_params=_cparams(("parallel", "parallel")),
    )(*_in_hbm(a, b))


def _scan_chunk(work, ltab, carry, reverse):
    nblk = CHUNK // SUBLANES
    for gi in range(N_STATE // LANE_GROUP):
        cre = pl.ds(gi * LANE_GROUP, LANE_GROUP)
        cim = pl.ds(N_STATE + gi * LANE_GROUP, LANE_GROUP)
        pows = [(ltab[8 * k:8 * k + 8, cre], ltab[8 * k:8 * k + 8, cim]) for k in range(3)]
        pr = ltab[24:32, cre]
        pi = ltab[24:32, cim]

        def blk(i, c, cre=cre, cim=cim, pows=pows, pr=pr, pi=pi):
            cr, ci = c
            b = (nblk - 1 - i) if reverse else i
            r0 = pl.multiple_of(b * SUBLANES, SUBLANES)
            xr = work[pl.ds(r0, SUBLANES), cre]
            xi = work[pl.ds(r0, SUBLANES), cim]
            for k, s in enumerate((1, 2, 4)):
                lr, li = pows[k]
                shift = SUBLANES - s if reverse else s
                sr = pltpu.roll(xr, shift, 0)
                si = pltpu.roll(xi, shift, 0)
                xr, xi = xr + lr * sr - li * si, xi + lr * si + li * sr
            xr, xi = xr + pr * cr - pi * ci, xi + pr * ci + pi * cr
            work[pl.ds(r0, SUBLANES), cre] = xr
            work[pl.ds(r0, SUBLANES), cim] = xi
            edge = 0 if reverse else SUBLANES - 1
            return _bcast_row(xr, edge, SUBLANES), _bcast_row(xi, edge, SUBLANES)

        cr, ci = lax.fori_loop(0, nblk, blk, (carry[:, cre], carry[:, cim]))
        carry[:, cre] = cr
        carry[:, cim] = ci


def _s5conv_fwd(z, convw, convb, bbmat, ccmat, dvec, ltab):
    def body(z_ref, cw_ref, cb_ref, bb_ref, cc_ref, d_ref, lt_ref, ya_ref, ys_ref, hs_ref,
             work, carry, tail):
        c = pl.program_id(0)

        @pl.when(c == 0)
        def _():
            carry[...] = jnp.zeros_like(carry)
            tail[...] = jnp.zeros_like(tail)

        zb = z_ref[:, 0:CONV_W]
        v = z_ref[:, CONV_W:2 * CONV_W] * z_ref[:, 2 * CONV_W:3 * CONV_W]
        us = z_ref[:, 3 * CONV_W:4 * CONV_W]
        v1, v2 = _conv_taps(v, tail[...])
        tail[...] = v[CHUNK - 8:CHUNK, :]
        y = cw_ref[0:1, :] * v2 + cw_ref[1:2, :] * v1 + cw_ref[2:3, :] * v
        ya_ref[...] = zb * (y + cb_ref[...])

        _block_expand(us.astype(BF16), bb_ref, work)
        _scan_chunk(work, lt_ref, carry, reverse=False)
        hs = work[...].astype(BF16)
        hs_ref[...] = hs
        ys_ref[...] = _block_contract(hs, cc_ref) + d_ref[...] * us

    return _out_hbm(pl.pallas_call(
        body, name="s5conv_fwd",
        grid=(N_CHUNKS,),
        in_specs=[pl.BlockSpec((CHUNK, IN_COLS), lambda c: (c, 0)),
                  pl.BlockSpec((3, CONV_W), lambda c: (0, 0)),
                  pl.BlockSpec((1, CONV_W), lambda c: (0, 0)),
                  pl.BlockSpec((SSM_W, 2 * N_STATE), lambda c: (0, 0)),
                  pl.BlockSpec((SSM_W, 2 * N_STATE), lambda c: (0, 0)),
                  pl.BlockSpec((1, SSM_W), lambda c: (0, 0)),
                  pl.BlockSpec((32, 2 * N_STATE), lambda c: (0, 0))],
        out_specs=[pl.BlockSpec((CHUNK, CONV_W), lambda c: (c, 0)),
                   pl.BlockSpec((CHUNK, SSM_W), lambda c: (c, 0)),
                   pl.BlockSpec((CHUNK, 2 * N_STATE), lambda c: (c, 0))],
        out_shape=[pltpu.HBM((SEQ, CONV_W), F32),
                   pltpu.HBM((SEQ, SSM_W), F32),
                   jax.ShapeDtypeStruct((SEQ, 2 * N_STATE), BF16)],
        scratch_shapes=[pltpu.VMEM((CHUNK, 2 * N_STATE), F32),
                        pltpu.VMEM((8, 2 * N_STATE), F32),
                        pltpu.VMEM((8, CONV_W), F32)],
        compiler_params=_cparams(("arbitrary",)),
    )(*_in_hbm(z, convw, convb, bbmat, ccmat, dvec, ltab)), (0, 1))


def _s5conv_bwd(z, hs, dya, dys, convw, convb, bbmat, ccmat, dvec, ltab_rev):
    nc = N_CHUNKS
    hb = 16

    def body(z_ref, zp_ref, hs_ref, hp_ref, dya_ref, dys_ref, cw_ref, cb_ref, bb_ref, cc_ref, d_ref, lt_ref,
             dz_ref, g_ref, us_ref, dyb_ref, dl_ref, dcw_ref, work, carry, head):
        i = pl.program_id(0)
        first_chunk = i == nc - 1

        @pl.when(i == 0)
        def _():
            carry[...] = jnp.zeros_like(carry)
            head[...] = jnp.zeros_like(head)
            dl_ref[...] = jnp.zeros_like(dl_ref)
            dcw_ref[...] = jnp.zeros_like(dcw_ref)

        us = z_ref[:, 3 * CONV_W:4 * CONV_W]
        dy = dys_ref[...]
        dy_bf = dy.astype(BF16)
        us_ref[0] = us.astype(BF16)
        dyb_ref[0] = dy_bf

        _block_expand(dy_bf, cc_ref, work)
        _scan_chunk(work, lt_ref, carry, reverse=True)
        gg = work[...]
        gg_bf = gg.astype(BF16)
        g_ref[0] = gg_bf
        dus = d_ref[...] * dy + _block_contract(gg_bf, bb_ref)

        hcur = hs_ref[...].astype(F32)
        hlast = hp_ref[...].astype(F32)[hb - 1:hb, :]
        hlast = jnp.where(first_chunk, 0.0, hlast)
        rid = _row_ids(CHUNK, 2 * N_STATE)
        hprev = jnp.where(rid == 0, jnp.broadcast_to(hlast, (CHUNK, 2 * N_STATE)), pltpu.roll(hcur, 1, 0))
        gr, gi = gg[:, :N_STATE], gg[:, N_STATE:]
        hr, hi = hprev[:, :N_STATE], hprev[:, N_STATE:]
        dl_ref[:, :N_STATE] += (gr * hr + gi * hi).reshape(CHUNK // 8, 8, N_STATE).sum(axis=0)
        dl_ref[:, N_STATE:] += (gi * hr - gr * hi).reshape(CHUNK // 8, 8, N_STATE).sum(axis=0)

        @pl.when(i == nc - 1)
        def _():
            dl_ref[0:1, :] = jnp.sum(dl_ref[...], axis=0, keepdims=True)

        zb = z_ref[:, 0:CONV_W]
        zc = z_ref[:, CONV_W:2 * CONV_W]
        zv = z_ref[:, 2 * CONV_W:3 * CONV_W]
        v = zc * zv
        vtail = jnp.where(first_chunk, 0.0, zp_ref[:, CONV_W:2 * CONV_W] * zp_ref[:, 2 * CONV_W:3 * CONV_W])
        v1, v2 = _conv_taps(v, vtail)
        w0, w1, w2 = cw_ref[0:1, :], cw_ref[1:2, :], cw_ref[2:3, :]
        y = w0 * v2 + w1 * v1 + w2 * v
        dya_v = dya_ref[...]
        dzb = dya_v * (y + cb_ref[...])
        dyc = dya_v * zb
        hd = head[...]
        rc = _row_ids(CHUNK, CONV_W)
        n1 = jnp.where(rc == CHUNK - 1, _bcast_row(hd, 0, CHUNK), pltpu.roll(dyc, CHUNK - 1, 0))
        n2 = jnp.where(rc == CHUNK - 1, _bcast_row(hd, 1, CHUNK),
                       jnp.where(rc == CHUNK - 2, _bcast_row(hd, 0, CHUNK), pltpu.roll(dyc, CHUNK - 2, 0)))
        head[...] = dyc[0:8, :]
        dv = w2 * dyc + w1 * n1 + w0 * n2
        dz_ref[:, 0:CONV_W] = dzb.astype(BF16)
        dz_ref[:, CONV_W:2 * CONV_W] = (dv * zv).astype(BF16)
        dz_ref[:, 2 * CONV_W:3 * CONV_W] = (dv * zc).astype(BF16)
        dz_ref[:, 3 * CONV_W:4 * CONV_W] = dus.astype(BF16)
        dcw_ref[0:1, :] += jnp.sum(dyc * v2, axis=0, keepdims=True)
        dcw_ref[1:2, :] += jnp.sum(dyc * v1, axis=0, keepdims=True)
        dcw_ref[2:3, :] += jnp.sum(dyc * v, axis=0, keepdims=True)
        dcw_ref[3:4, :] += jnp.sum(dyc, axis=0, keepdims=True)
        dcw_ref[4:5, :] += jnp.sum(dy * us, axis=0, keepdims=True)

    rev = lambda i: nc - 1 - i
    return _out_hbm(pl.pallas_call(
        body, name="s5conv_bwd",
        grid=(nc,),
        in_specs=[pl.BlockSpec((CHUNK, IN_COLS), lambda i: (rev(i), 0)),
                  pl.BlockSpec((8, IN_COLS), lambda i: (jnp.maximum(rev(i) * (CHUNK // 8) - 1, 0), 0)),
                  pl.BlockSpec((CHUNK, 2 * N_STATE), lambda i: (rev(i), 0)),
                  pl.BlockSpec((hb, 2 * N_STATE), lambda i: (jnp.maximum(rev(i) * (CHUNK // hb) - 1, 0), 0)),
                  pl.BlockSpec((CHUNK, CONV_W), lambda i: (rev(i), 0)),
                  pl.BlockSpec((CHUNK, SSM_W), lambda i: (rev(i), 0)),
                  pl.BlockSpec((3, CONV_W), lambda i: (0, 0)),
                  pl.BlockSpec((1, CONV_W), lambda i: (0, 0)),
                  pl.BlockSpec((SSM_W, 2 * N_STATE), lambda i: (0, 0)),
                  pl.BlockSpec((SSM_W, 2 * N_STATE), lambda i: (0, 0)),
                  pl.BlockSpec((1, SSM_W), lambda i: (0, 0)),
                  pl.BlockSpec((32, 2 * N_STATE), lambda i: (0, 0))],
        out_specs=[pl.BlockSpec((CHUNK, IN_COLS), lambda i: (rev(i), 0)),
                   pl.BlockSpec((1, CHUNK, 2 * N_STATE), lambda i: (0, rev(i), 0)),
                   pl.BlockSpec((1, CHUNK, SSM_W), lambda i: (0, rev(i), 0)),
                   pl.BlockSpec((1, CHUNK, SSM_W), lambda i: (0, rev(i), 0)),
                   pl.BlockSpec((8, 2 * N_STATE), lambda i: (0, 0)),
                   pl.BlockSpec((8, CONV_W), lambda i: (0, 0))],
        out_shape=[jax.ShapeDtypeStruct((SEQ, IN_COLS), BF16),
                   pltpu.HBM((1, SEQ, 2 * N_STATE), BF16),
                   pltpu.HBM((1, SEQ, SSM_W), BF16),
                   pltpu.HBM((1, SEQ, SSM_W), BF16),
                   jax.ShapeDtypeStruct((8, 2 * N_STATE), F32),
                   jax.ShapeDtypeStruct((8, CONV_W), F32)],
        scratch_shapes=[pltpu.VMEM((CHUNK, 2 * N_STATE), F32),
                        pltpu.VMEM((8, 2 * N_STATE), F32),
                        pltpu.VMEM((8, CONV_W), F32)],
        compiler_params=_cparams(("arbitrary",)),
    )(*_in_hbm(z, z, hs, hs, dya, dys, convw, convb, bbmat, ccmat, dvec, ltab_rev)), (1, 2, 3))


def _mix_out_fwd(h, ya, ys, gluw, glub, con, son, wout):
    tm = TOK_TILE

    def body(h_ref, ya_ref, ys_ref, gw_ref, gb_ref, con_ref, son_ref, wo_ref, o_ref):
        zg, _ = _gelu(ys_ref[...])
        q = _dot(zg.astype(BF16), gw_ref[...]) + gb_ref[...]
        out_s = zg * _sigmoid(q)
        na, _ = _rms_stats(ya_ref[...])
        ns, _ = _rms_stats(out_s)
        o_ref[...] = (h_ref[...]
                      + _dot((na * con_ref[...]).astype(BF16), wo_ref[0:CONV_W, :])
                      + _dot((ns * son_ref[...]).astype(BF16), wo_ref[CONV_W:2 * CONV_W, :]))

    row = lambda m: (m, 0)
    fixed = lambda m: (0, 0)
    return pl.pallas_call(
        body, name="mix_out_fwd",
        grid=(SEQ // tm,),
        in_specs=[pl.BlockSpec((tm, D_MODEL), row), pl.BlockSpec((tm, CONV_W), row), pl.BlockSpec((tm, SSM_W), row),
                  pl.BlockSpec((SSM_W, SSM_W), fixed), pl.BlockSpec((1, SSM_W), fixed),
                  pl.BlockSpec((1, CONV_W), fixed), pl.BlockSpec((1, SSM_W), fixed),
                  pl.BlockSpec((None, D_MODEL, D_MODEL), lambda m: (0, 0, 0))],
        out_specs=pl.BlockSpec((tm, D_MODEL), row),
        out_shape=jax.ShapeDtypeStruct((SEQ, D_MODEL), F32),
        compiler_params=_cparams(("parallel",)),
    )(*_in_hbm(h, ya, ys, gluw, glub, con, son, wout))


def _mix_out_bwd(dh, ya, ys, gluw, glub, con, son, wout):
    tm = TOK_TILE

    def body(dh_ref, ya_ref, ys_ref, gw_ref, gb_ref, con_ref, son_ref, wo_ref,
             dya_ref, dys_ref, yc_ref, dhb_ref, zg_ref, dq_ref, part_ref):
        ysv = ys_ref[...]
        zg, th = _gelu(ysv)
        zg_bf = zg.astype(BF16)
        s = _sigmoid(_dot(zg_bf, gw_ref[...]) + gb_ref[...])
        out_s = zg * s
        na, ra = _rms_stats(ya_ref[...])
        ns, rs = _rms_stats(out_s)
        dh_bf = dh_ref[...].astype(BF16)
        yc_ref[0, :, 0:CONV_W] = (na * con_ref[...]).astype(BF16)
        yc_ref[0, :, CONV_W:2 * CONV_W] = (ns * son_ref[...]).astype(BF16)
        dhb_ref[0] = dh_bf
        dca = _dot_nt(dh_bf, wo_ref[0:CONV_W, :])
        dcs = _dot_nt(dh_bf, wo_ref[CONV_W:2 * CONV_W, :])
        dya, dcon = _rms_bwd(dca, na, ra, con_ref[...])
        dos, dson = _rms_bwd(dcs, ns, rs, son_ref[...])
        dya_ref[...] = dya
        dq = dos * zg * s * (1.0 - s)
        dq_bf = dq.astype(BF16)
        dzg = dos * s + _dot_nt(dq_bf, gw_ref[...])
        dys_ref[...] = dzg * _gelu_grad(ysv, th)
        zg_ref[0] = zg_bf
        dq_ref[0] = dq_bf
        rid = _row_ids(SUBLANES, SSM_W)
        part = jnp.zeros((SUBLANES, SSM_W), F32)
        for i, rowv in enumerate((dcon, dson, jnp.sum(dq, axis=0, keepdims=True))):
            part = jnp.where(rid == i, jnp.broadcast_to(rowv, (SUBLANES, SSM_W)), part)
        _accumulate(part_ref, pl.program_id(0) == 0, part)

    row = lambda m: (m, 0)
    fixed = lambda m: (0, 0)
    lead = lambda m: (0, m, 0)
    return _out_hbm(pl.pallas_call(
        body, name="mix_out_bwd",
        grid=(SEQ // tm,),
        in_specs=[pl.BlockSpec((tm, D_MODEL), row), pl.BlockSpec((tm, CONV_W), row), pl.BlockSpec((tm, SSM_W), row),
                  pl.BlockSpec((SSM_W, SSM_W), fixed), pl.BlockSpec((1, SSM_W), fixed),
                  pl.BlockSpec((1, CONV_W), fixed), pl.BlockSpec((1, SSM_W), fixed),
                  pl.BlockSpec((None, D_MODEL, D_MODEL), lambda m: (0, 0, 0))],
        out_specs=[pl.BlockSpec((tm, CONV_W), row), pl.BlockSpec((tm, SSM_W), row),
                   pl.BlockSpec((1, tm, D_MODEL), lead), pl.BlockSpec((1, tm, D_MODEL), lead),
                   pl.BlockSpec((1, tm, SSM_W), lead), pl.BlockSpec((1, tm, SSM_W), lead),
                   pl.BlockSpec((8, SSM_W), fixed)],
        out_shape=[pltpu.HBM((SEQ, CONV_W), F32), pltpu.HBM((SEQ, SSM_W), F32),
                   pltpu.HBM((1, SEQ, D_MODEL), BF16), pltpu.HBM((1, SEQ, D_MODEL), BF16),
                   pltpu.HBM((1, SEQ, SSM_W), BF16), pltpu.HBM((1, SEQ, SSM_W), BF16),
                   jax.ShapeDtypeStruct((8, SSM_W), F32)],
        compiler_params=_cparams(("arbitrary",)),
    )(*_in_hbm(dh, ya, ys, gluw, glub, con, son, wout)), (0, 1, 2, 3, 4, 5))


def _ple_fwd(h, g, p, wgate, wprojt):
    tm = TOK_TILE

    def body(h_ref, g_ref, p_ref, wg_ref, wp_ref, o_ref):
        x = h_ref[...]
        xh, _ = _rms_stats(x)
        s = _sigmoid(_dot((xh * g_ref[...]).astype(BF16), wg_ref[...]))
        o_ref[...] = x + _dot_nt(p_ref[...].astype(BF16), wp_ref[...]) * s

    row = lambda m: (m, 0)
    fixed = lambda m: (0, 0)
    return pl.pallas_call(
        body, name="ple_fwd",
        grid=(SEQ // tm,),
        in_specs=[pl.BlockSpec((tm, D_MODEL), row), pl.BlockSpec((1, D_MODEL), fixed), pl.BlockSpec((tm, PLE_DIM), row),
                  pl.BlockSpec((None, D_MODEL, D_MODEL), lambda m: (0, 0, 0)), pl.BlockSpec((D_MODEL, PLE_DIM), fixed)],
        out_specs=pl.BlockSpec((tm, D_MODEL), row),
        out_shape=jax.ShapeDtypeStruct((SEQ, D_MODEL), F32),
        compiler_params=_cparams(("parallel",)),
    )(*_in_hbm(h, g, p, wgate, wprojt))


def _ple_bwd(h, g, p, dh, wgate, wprojt):
    tm = TOK_TILE

    def body(h_ref, g_ref, p_ref, dh_ref, wg_ref, wp_ref, o_ref, u_ref, dq_ref, dpp_ref, pb_ref, dg_ref):
        xh, r = _rms_stats(h_ref[...])
        u = (xh * g_ref[...]).astype(BF16)
        s = _sigmoid(_dot(u, wg_ref[...]))
        p_bf = p_ref[...].astype(BF16)
        pp = _dot_nt(p_bf, wp_ref[...])
        dhv = dh_ref[...]
        dq = (dhv * pp * s * (1.0 - s)).astype(BF16)
        u_ref[0] = u
        dq_ref[0] = dq
        dpp_ref[0] = (dhv * s).astype(BF16)
        pb_ref[0] = p_bf
        dx, dg = _rms_bwd(_dot_nt(dq, wg_ref[...]), xh, r, g_ref[...])
        o_ref[...] = dhv + dx
        _accumulate(dg_ref, pl.program_id(0) == 0, dg)

    row = lambda m: (m, 0)
    fixed = lambda m: (0, 0)
    lead = lambda m: (0, m, 0)
    big = pltpu.HBM((1, SEQ, D_MODEL), BF16)
    return _out_hbm(pl.pallas_call(
        body, name="ple_bwd",
        grid=(SEQ // tm,),
        in_specs=[pl.BlockSpec((tm, D_MODEL), row), pl.BlockSpec((1, D_MODEL), fixed), pl.BlockSpec((tm, PLE_DIM), row),
                  pl.BlockSpec((tm, D_MODEL), row),
                  pl.BlockSpec((None, D_MODEL, D_MODEL), lambda m: (0, 0, 0)), pl.BlockSpec((D_MODEL, PLE_DIM), fixed)],
        out_specs=[pl.BlockSpec((tm, D_MODEL), row),
                   pl.BlockSpec((1, tm, D_MODEL), lead), pl.BlockSpec((1, tm, D_MODEL), lead),
                   pl.BlockSpec((1, tm, D_MODEL), lead), pl.BlockSpec((1, tm, PLE_DIM), lead),
                   pl.BlockSpec((1, D_MODEL), fixed)],
        out_shape=[jax.ShapeDtypeStruct((SEQ, D_MODEL), F32), big, big, big,
                   pltpu.HBM((1, SEQ, PLE_DIM), BF16),
                   jax.ShapeDtypeStruct((1, D_MODEL), F32)],
        compiler_params=_cparams(("arbitrary",)),
    )(*_in_hbm(h, g, p, dh, wgate, wprojt)), (1, 2, 3, 4))


def _final_loss(h, g, target):
    tm = TOK_TILE

    def body(h_ref, g_ref, t_ref, loss_ref, dh_ref, dg_ref):
        first = pl.program_id(0) == 0
        xh, r = _rms_stats(h_ref[...])
        diff = xh * g_ref[...] - t_ref[...]
        part = 0.5 * jnp.sum(jnp.mean(diff * diff, axis=-1, keepdims=True), axis=0, keepdims=True)
        _accumulate(loss_ref, first, jnp.broadcast_to(part, (SUBLANES, LANES)))
        dx, dg = _rms_bwd(diff * (1.0 / D_MODEL), xh, r, g_ref[...])
        dh_ref[...] = dx
        _accumulate(dg_ref, first, dg)

    row = lambda m: (m, 0)
    fixed = lambda m: (0, 0)
    return pl.pallas_call(
        body, name="final_loss",
        grid=(SEQ // tm,),
        in_specs=[pl.BlockSpec((tm, D_MODEL), row), pl.BlockSpec((1, D_MODEL), fixed),
                  pl.BlockSpec((tm, D_MODEL), row)],
        out_specs=[pl.BlockSpec((SUBLANES, LANES), fixed),
                   pl.BlockSpec((tm, D_MODEL), row),
                   pl.BlockSpec((1, D_MODEL), fixed)],
        out_shape=[jax.ShapeDtypeStruct((SUBLANES, LANES), F32),
                   jax.ShapeDtypeStruct((SEQ, D_MODEL), F32),
                   jax.ShapeDtypeStruct((1, D_MODEL), F32)],
        compiler_params=_cparams(("arbitrary",)),
    )(*_in_hbm(h, g, target))


def _disc(ar, ai, ldt):
    dt = jnp.exp(ldt)
    mag = jnp.exp(ar * dt)
    ph = ai * dt
    lr, li = mag * jnp.cos(ph), mag * jnp.sin(ph)
    nr, ni = lr - 1.0, li
    den = ar * ar + ai * ai
    return lr, li, (nr * ar + ni * ai) / den, (ni * ar - nr * ai) / den


def _s5_disc(a_row, ldt_row, a_rep, ldt_rep, bt, ct, tile_e, mask):
    n = N_STATE

    def body(ar_ref, lr_ref, ap_ref, lp_ref, b_ref, c_ref, e_ref, m_ref, lt_ref, ltr_ref, bb_ref, cc_ref):
        lr, li, _, _ = _disc(ar_ref[0], ar_ref[1], lr_ref[...])
        pr, pi = lr, li
        rid = _row_ids(SUBLANES, n)
        for k in range(1, 9):
            for ref, sgn, edge in ((lt_ref, 1.0, 24 + k - 1), (ltr_ref, -1.0, 24 + 8 - k)):
                if k in (1, 2, 4):
                    r0 = {1: 0, 2: 8, 4: 16}[k]
                    keep = (rid >= k) if ref is lt_ref else (rid < SUBLANES - k)
                    ref[r0:r0 + 8, 0:n] = jnp.where(keep, jnp.broadcast_to(pr, (8, n)), 0.0)
                    ref[r0:r0 + 8, n:2 * n] = jnp.where(keep, jnp.broadcast_to(sgn * pi, (8, n)), 0.0)
                ref[edge:edge + 1, 0:n] = pr
                ref[edge:edge + 1, n:2 * n] = sgn * pi
            pr, pi = pr * lr - pi * li, pr * li + pi * lr
        _, _, fr, fi = _disc(ap_ref[0], ap_ref[1], lp_ref[...])
        br, bi = b_ref[0], b_ref[1]
        e = e_ref[...]
        m = m_ref[...].astype(F32)
        bb_ref[:, 0:n] = (_dot((fr * br - fi * bi).astype(BF16), e) * m).astype(BF16)
        bb_ref[:, n:2 * n] = (_dot((fr * bi + fi * br).astype(BF16), e) * m).astype(BF16)
        cc_ref[:, 0:n] = (_dot(c_ref[0].astype(BF16), e) * m).astype(BF16)
        cc_ref[:, n:2 * n] = (-(_dot(c_ref[1].astype(BF16), e) * m)).astype(BF16)

    return pl.pallas_call(
        body, name="s5_disc",
        out_shape=[jax.ShapeDtypeStruct((32, 2 * n), F32), jax.ShapeDtypeStruct((32, 2 * n), F32),
                   jax.ShapeDtypeStruct((SSM_W, 2 * n), BF16), jax.ShapeDtypeStruct((SSM_W, 2 * n), BF16)],
        compiler_params=_cparams(None),
    )(a_row, ldt_row, a_rep, ldt_rep, bt, ct, tile_e, mask)


def _dot_exact(x, sel):
    hi = x.astype(BF16)
    r1 = x - hi.astype(F32)
    mid = r1.astype(BF16)
    lo = (r1 - mid.astype(F32)).astype(BF16)
    return _dot(hi, sel) + _dot(mid, sel) + _dot(lo, sel)


def _s5_disc_bwd(a, ldt, a_rep, ldt_rep, bt, mask, dl, d_bb, d_cc, fold):
    n = N_STATE

    def body(a_ref, l_ref, ap_ref, lp_ref, b_ref, m_ref, dl_ref, dbb_ref, dcc_ref, f_ref,
             da_ref, dldt_ref, db_ref, dc_ref):
        m = m_ref[...].astype(F32)
        fold_m = f_ref[...]
        diag = lambda x: _dot_exact(jnp.where(m > 0.0, x, 0.0), fold_m)
        dr, di = diag(dbb_ref[:, 0:n]), diag(dbb_ref[:, n:2 * n])
        dc_ref[0] = diag(dcc_ref[:, 0:n])
        dc_ref[1] = -diag(dcc_ref[:, n:2 * n])
        _, _, fr, fi = _disc(ap_ref[0], ap_ref[1], lp_ref[...])
        br, bi = b_ref[0], b_ref[1]
        db_ref[0] = fr * dr + fi * di
        db_ref[1] = fr * di - fi * dr
        per_state = lambda x: x.reshape(SSM_GROUPS, SSM_GROUP, SSM_STATE).sum(axis=1)
        dfr = per_state(dr * br + di * bi)
        dfi = per_state(di * br - dr * bi)
        _, vjp = jax.vjp(_disc, a_ref[0], a_ref[1], l_ref[...])
        dar, dai, dldt = vjp((dl_ref[0], dl_ref[1], dfr, dfi))
        da_ref[0] = dar
        da_ref[1] = dai
        dldt_ref[...] = jnp.sum(dldt, axis=1, keepdims=True)

    return pl.pallas_call(
        body, name="s5_disc_bwd",
        out_shape=[jax.ShapeDtypeStruct((2, SSM_GROUPS, SSM_STATE), F32),
                   jax.ShapeDtypeStruct((SSM_GROUPS, 1), F32),
                   jax.ShapeDtypeStruct((2, SSM_W, SSM_STATE), F32),
                   jax.ShapeDtypeStruct((2, SSM_W, SSM_STATE), F32)],
        compiler_params=_cparams(None),
    )(a, ldt, a_rep, ldt_rep, bt, mask, dl, d_bb, d_cc, fold)


def _row_block(rows, cap=512):
    for bm in range(min(cap, rows), 0, -1):
        if rows % bm == 0 and (bm % 8 == 0 or bm == rows):
            return bm
    return rows


SUM_PARTS = 2


def _own_pieces(segs, rtot):
    pr = rtot // SUM_PARTS
    assert pr * SUM_PARTS == rtot and pr % 16 == 0
    offs = _seg_offsets(segs)
    pieces = [[] for _ in range(SUM_PARTS)]
    for a, (n, r) in enumerate(segs):
        for m in range(n):
            lo = offs[a] + m * r
            for h in range(SUM_PARTS):
                clo, chi = max(lo, h * pr), min(lo + r, (h + 1) * pr)
                if chi > clo:
                    pieces[h].append((a, m, clo - lo, clo - h * pr, chi - clo))
    return pieces


def _pair_rows(srcs, got_ref, segs, pieces, h, chip, own_v, got_v, sems):
    pr = own_v.shape[0]
    dev = 2 * chip + lax.axis_index("c")
    for hh in range(SUM_PARTS):
        @pl.when(h == hh)
        def _(hh=hh):
            cps = [pltpu.make_async_copy(got_ref.at[chip, pl.ds(hh * pr, pr), :], got_v, sems.at[0])]
            for i, (a, m, so, do, rows) in enumerate(pieces[hh]):
                start = pl.multiple_of(dev * segs[a][1] + so, 16)
                cps.append(pltpu.make_async_copy(srcs[a].at[m, pl.ds(start, rows), :],
                                                 own_v.at[pl.ds(do, rows), :], sems.at[1 + i]))
            for cp in cps:
                cp.start()
            for cp in cps:
                cp.wait()
    return own_v[...].astype(F32) + got_v[...].astype(F32)


def _pair_sum(fulls, got, segs):
    ns = len(segs)
    _, rtot, c = got.shape
    pieces = _own_pieces(segs, rtot)
    pr = rtot // SUM_PARTS

    def body(*refs):
        srcs = refs[:ns]
        got_ref, pbf_ref, own_v, got_v, sems = refs[ns:]
        x, y, _ = _mesh_pos()
        j = pl.program_id(1)
        chip = jnp.where(j == 0, 2 * (1 - x) + y, jnp.where(j == 1, 2 * x + 1 - y, 2 * (1 - x) + 1 - y))
        pbf_ref[0] = _pair_rows(srcs, got_ref, segs, pieces, pl.program_id(0), chip, own_v, got_v, sems).astype(BF16)

    return pl.pallas_call(
        body, name="pair_sum",
        grid=(SUM_PARTS, 3),
        in_specs=[HBM] * (ns + 1), out_specs=pl.BlockSpec((1, pr, c), lambda h, j: (j, h, 0)),
        out_shape=pltpu.HBM((3, rtot, c), BF16),
        scratch_shapes=[pltpu.VMEM((pr, c), BF16), pltpu.VMEM((pr, c), BF16),
                        pltpu.SemaphoreType.DMA((1 + max(len(p) for p in pieces),))],
        compiler_params=_cparams(("arbitrary", "arbitrary")),
    )(*_in_hbm(*fulls, got))


def _chip_sum(fulls, got, rb, segs, rows_of, layer, into):
    ns = len(segs)
    _, rtot, c = got.shape
    pieces = _own_pieces(segs, rtot)
    pr = rtot // SUM_PARTS
    n_sems = 1 + max(len(p) for p in pieces)

    def body(*refs):
        srcs = refs[:ns]
        got_ref, r_ref = refs[ns], refs[ns + 1]
        out_ref, own_v, got_v, sum_v, sems = refs[-5:]
        x, y, _ = _mesh_pos()
        h = pl.program_id(0)
        own = _pair_rows(srcs, got_ref, segs, pieces, h, 2 * x + y, own_v, got_v, sems)
        sum_v[...] = ((own + r_ref[0].astype(F32)) + r_ref[1].astype(F32)) + r_ref[2].astype(F32)
        for hh in range(SUM_PARTS):
            @pl.when(h == hh)
            def _(hh=hh):
                cps = [pltpu.make_async_copy(
                    sum_v.at[pl.ds(do, rows), :],
                    out_ref.at[layer, pl.ds(rows_of[a] + m * segs[a][1] + so, rows), :], sems.at[i])
                    for i, (a, m, so, do, rows) in enumerate(pieces[hh])]
                for cp in cps:
                    cp.start()
                for cp in cps:
                    cp.wait()

    old = [] if into is None else [into]
    return pl.pallas_call(
        body, name="chip_sum",
        grid=(SUM_PARTS,),
        in_specs=[HBM] * (ns + 1) + [pl.BlockSpec((3, pr, c), lambda h: (0, h, 0))] + [HBM] * len(old),
        out_specs=HBM,
        out_shape=jax.ShapeDtypeStruct((DEPTH, PACK_ROWS, c), F32),
        input_output_aliases={ns + 2: 0} if old else {},
        scratch_shapes=[pltpu.VMEM((pr, c), BF16), pltpu.VMEM((pr, c), BF16), pltpu.VMEM((pr, c), F32),
                        pltpu.SemaphoreType.DMA((n_sems,))],
        compiler_params=_cparams(("arbitrary",)),
    )(*_in_hbm(*fulls, got, rb), *old)


def _adamw(w, g, m, v):
    r, c = w.shape
    bm = _row_block(r)
    bc1 = 1.0 - ADAM_B1 ** ADAM_STEP
    bc2 = 1.0 - ADAM_B2 ** ADAM_STEP

    def body(w_ref, g_ref, m_ref, v_ref, d_ref, nm_ref, nv_ref):
        gv = g_ref[...]
        nm = ADAM_B1 * m_ref[...] + (1.0 - ADAM_B1) * gv
        nv = ADAM_B2 * v_ref[...] + (1.0 - ADAM_B2) * (gv * gv)
        nm_ref[...] = nm
        nv_ref[...] = nv
        d_ref[...] = -ADAM_LR * ((nm / bc1) / (jnp.sqrt(nv / bc2) + ADAM_EPS) + ADAM_WD * w_ref[...])

    spec = pl.BlockSpec((bm, c), lambda k: (k, 0))
    shp = jax.ShapeDtypeStruct((r, c), F32)
    return pl.pallas_call(
        body, name="adamw",
        grid=(r // bm,),
        in_specs=[spec] * 4, out_specs=[spec] * 3, out_shape=[shp] * 3,
        compiler_params=_cparams(("parallel",)),
    )(*_in_hbm(w, g, m, v))


def _adamw_layers(sets, first, nl, prev):
    ns = len(sets)
    depth, r, c = sets[0][0].shape
    bm = _row_block(r, min(512, max(SUBLANES, (24 * MIB) // (ns * 8 * 2 * c * 4))))
    while any(four[4] is not None and four[4] % bm for four in sets):
        bm //= 2
    assert bm % SUBLANES == 0 and r % bm == 0
    bc1 = 1.0 - ADAM_B1 ** ADAM_STEP
    bc2 = 1.0 - ADAM_B2 ** ADAM_STEP

    def body(*refs):
        outs = refs[len(refs) - 4 * ns:]
        for s in range(ns):
            w_ref, m_ref, v_ref, g_ref = refs[4 * s:4 * s + 4]
            go_ref, d_ref, nm_ref, nv_ref = outs[4 * s:4 * s + 4]
            gv = g_ref[...]
            nm = ADAM_B1 * m_ref[...] + (1.0 - ADAM_B1) * gv
            nv = ADAM_B2 * v_ref[...] + (1.0 - ADAM_B2) * (gv * gv)
            go_ref[...] = gv
            nm_ref[...] = nm
            nv_ref[...] = nv
            d_ref[...] = -ADAM_LR * ((nm / bc1) / (jnp.sqrt(nv / bc2) + ADAM_EPS) + ADAM_WD * w_ref[...])

    at = pl.BlockSpec((1, bm, c), lambda i, k: (first + i, k, 0))

    def grad_spec(g_rows):
        if g_rows is None:
            return pl.BlockSpec((1, bm, c), lambda i, k: (i, k, 0))
        return pl.BlockSpec((1, bm, c), lambda i, k: (first + i, g_rows // bm + k, 0))

    shp = jax.ShapeDtypeStruct((depth, r, c), F32)
    old = [] if prev is None else [a for four in prev for a in four]
    flat = pl.pallas_call(
        body, name="adamw_layers",
        grid=(nl, r // bm),
        in_specs=[spec for four in sets for spec in (at, at, at, grad_spec(four[4]))] + [HBM] * len(old),
        out_specs=[at] * (4 * ns), out_shape=[shp] * (4 * ns),
        input_output_aliases={4 * ns + i: i for i in range(len(old))},
        compiler_params=_cparams(("parallel", "parallel")),
    )(*_in_hbm(*[a for four in sets for a in four[:4]]), *old)
    return [flat[4 * s:4 * s + 4] for s in range(ns)]


def _mesh_pos():
    return lax.axis_index("x"), lax.axis_index("y"), lax.axis_index("c")


def _dev_index(p):
    return 4 * p[0] + 2 * p[1] + p[2]


def _seg_offsets(segs):
    offs, o = [], 0
    for n, r in segs:
        offs.append(o)
        o += n * r
    return offs


def _remote(src, dst, send_sem, recv_sem, to):
    return pltpu.make_async_remote_copy(src_ref=src, dst_ref=dst, send_sem=send_sem, recv_sem=recv_sem,
                                        device_id=to, device_id_type=MESH)


def _allgather(pack, segs, name):
    rtot, c = pack.shape
    ns = len(segs)
    offs = _seg_offsets(segs)
    assert rtot == sum(n * r for n, r in segs)

    def body(pack_ref, *refs):
        outs = refs[:ns]
        send_sems, recv_sems, local_sem = refs[ns:]
        x, y, cc = _mesh_pos()
        me, sib = (x, y, cc), (x, y, 1 - cc)
        chips = [(1 - x, y), (x, 1 - y), (1 - x, 1 - y)]

        def pieces(dev, from_pack):
            res = []
            for a, (n, r) in enumerate(segs):
                for m in range(n):
                    dst = outs[a].at[m, pl.ds(pl.multiple_of(dev * r, r), r), :]
                    src = pack_ref.at[pl.ds(offs[a] + m * r, r), :] if from_pack else dst
                    res.append((src, dst))
            return res

        def push(k, dev, to, from_pack):
            for s, d in pieces(dev, from_pack):
                _remote(s, d, send_sems.at[k], recv_sems.at[k], to).start()

        def whole(k):
            return _remote(pack_ref, pack_ref, send_sems.at[k], recv_sems.at[k], me)

        my_dev = _dev_index(me)
        for s, d in pieces(my_dev, True):
            pltpu.make_async_copy(s, d, local_sem).start()
        push(0, my_dev, sib, True)
        for j, chip in enumerate(chips):
            push(1 + j, my_dev, (*chip, cc), True)
        for j, chip in enumerate(chips):
            whole(1 + j).wait_recv()
            push(4 + j, _dev_index((*chip, cc)), sib, False)
        whole(0).wait_recv()
        for j in range(3):
            whole(4 + j).wait_recv()
        for k in range(7):
            whole(k).wait_send()
        pltpu.make_async_copy(pack_ref, pack_ref, local_sem).wait()

    return pl.pallas_call(
        body, name=name,
        in_specs=[HBM], out_specs=[HBM] * ns,
        out_shape=[jax.ShapeDtypeStruct((n, N_DEV * r, c), pack.dtype) for n, r in segs],
        scratch_shapes=[pltpu.SemaphoreType.DMA((7,)), pltpu.SemaphoreType.DMA((7,)), pltpu.SemaphoreType.DMA],
    )(pack)


HBM = pl.BlockSpec(memory_space=pltpu.HBM)
SEM = pl.BlockSpec(memory_space=pltpu.SEMAPHORE)
VMEM_WHOLE = pl.BlockSpec(memory_space=pltpu.VMEM)
EFFECT = pltpu.SideEffectType.DATAFLOW_SIDE_EFFECTING


def _hbm(a):
    return pltpu.with_memory_space_constraint(a, pltpu.HBM)


def _ag_start(pack, segs, after, name):
    rtot, c = pack.shape
    ns = len(segs)
    offs = _seg_offsets(segs)

    def body(pack_ref, *refs):
        lands = refs[:ns]
        send_sems, recv_sems = refs[ns + 1], refs[ns + 2]
        token = refs[-1]
        x, y, cc = _mesh_pos()
        my_dev = _dev_index((x, y, cc))
        targets = [(x, y, 1 - cc), (1 - x, y, cc), (x, 1 - y, cc), (1 - x, 1 - y, cc)]
        for k, to in enumerate(targets):
            for a, (n, r) in enumerate(segs):
                for m in range(n):
                    _remote(pack_ref.at[pl.ds(offs[a] + m * r, r), :],
                            lands[a].at[m, pl.ds(pl.multiple_of(my_dev * r, r), r), :],
                            send_sems.at[k], recv_sems.at[k], to).start()
        token[...] = jnp.zeros_like(token)

    land_shapes = [(n, N_DEV * r, c) for n, r in segs]
    outs = pl.pallas_call(
        body, name=name,
        in_specs=[HBM] * (1 + ns) + [UNREAD],
        out_specs=[SEM, SEM, HBM] + [HBM] * ns + [VMEM_WHOLE],
        out_shape=[pltpu.SemaphoreType.DMA((4,)), pltpu.SemaphoreType.DMA((4,)), pltpu.HBM(pack.shape, pack.dtype)]
        + [pltpu.HBM(s, pack.dtype) for s in land_shapes] + [jax.ShapeDtypeStruct((SUBLANES, LANES), F32)],
        input_output_aliases={0: 2, **{1 + i: 3 + i for i in range(ns)}},
        compiler_params=pltpu.CompilerParams(has_side_effects=EFFECT),
    )(_hbm(pack), *[_hbm(lax.empty(s, pack.dtype)) for s in land_shapes], _hbm(after))
    return outs[0], outs[1], outs[2], list(outs[3:3 + ns]), outs[-1]


def _ag_wait(send_sems, recv_sems, pack, lands, after, name):
    ns = len(lands)

    def body(pack_ref, *refs):
        send_ref, recv_ref = refs[ns], refs[ns + 1]
        me = _mesh_pos()
        for k in range(4):
            whole = _remote(pack_ref, pack_ref, send_ref.at[k], recv_ref.at[k], me)
            whole.wait_send()
            whole.wait_recv()

    outs = pl.pallas_call(
        body, name=name,
        in_specs=[HBM] * (1 + ns) + [SEM, SEM, UNREAD],
        out_specs=[HBM] * (1 + ns),
        out_shape=[pltpu.HBM(pack.shape, pack.dtype)] + [pltpu.HBM(a.shape, a.dtype) for a in lands],
        input_output_aliases={i: i for i in range(1 + ns)},
        compiler_params=pltpu.CompilerParams(has_side_effects=EFFECT),
    )(pack, *lands, send_sems, recv_sems, _hbm(after))
    return outs[0], list(outs[1:])


def _ag_finish(pack, lands, segs):
    rtot, c = pack.shape
    ns = len(segs)
    offs = _seg_offsets(segs)

    def body(pack_ref, *refs):
        outs = refs[ns:2 * ns]
        stage, send_sems, recv_sems, local_sems = refs[2 * ns:]
        x, y, cc = _mesh_pos()
        me, sib = (x, y, cc), (x, y, 1 - cc)
        chips = [(1 - x, y), (x, 1 - y), (1 - x, 1 - y)]

        def rows(a, m, dev):
            return outs[a].at[m, pl.ds(pl.multiple_of(dev * segs[a][1], segs[a][1]), segs[a][1]), :]

        for j, chip in enumerate(chips):
            dev = _dev_index((*chip, cc))
            for a, (n, r) in enumerate(segs):
                for m in range(n):
                    _remote(rows(a, m, dev), rows(a, m, dev), send_sems.at[j], recv_sems.at[j], sib).start()
        load = pltpu.make_async_copy(pack_ref, stage, local_sems.at[0])
        load.start()
        load.wait()
        my_dev = _dev_index(me)
        for a, (n, r) in enumerate(segs):
            for m in range(n):
                pltpu.make_async_copy(stage.at[pl.ds(offs[a] + m * r, r), :], rows(a, m, my_dev), local_sems.at[1]).start()
        pltpu.make_async_copy(stage, pack_ref, local_sems.at[1]).wait()
        for j in range(3):
            _remote(pack_ref, pack_ref, send_sems.at[j], recv_sems.at[j], me).wait()

    outs = pl.pallas_call(
        body, name="ag_finish",
        in_specs=[HBM] * (1 + ns), out_specs=[HBM] * ns,
        out_shape=[pltpu.HBM(a.shape, a.dtype) if r >= 128 else jax.ShapeDtypeStruct(a.shape, a.dtype)
                   for a, (_, r) in zip(lands, segs)],
        input_output_aliases={1 + i: i for i in range(ns)},
        scratch_shapes=[pltpu.VMEM((rtot, c), pack.dtype), pltpu.SemaphoreType.DMA((3,)),
                        pltpu.SemaphoreType.DMA((3,)), pltpu.SemaphoreType.DMA((2,))],
        compiler_params=_cparams(None, 16),
    )(pack, *lands)
    return list(outs)


def _rs_chips_start(pbf, after, name):
    _, rtot, c = pbf.shape

    def body(pbf_ref, land_ref, after_ref, send_sems, recv_sems, pbf_thru, land_thru, token):
        x, y, cc = _mesh_pos()
        for j, (cx, cy) in enumerate([(1 - x, y), (x, 1 - y), (1 - x, 1 - y)]):
            _remote(pbf_ref.at[j], land_ref.at[j], send_sems.at[j], recv_sems.at[j], (cx, cy, cc)).start()
        token[...] = jnp.zeros_like(token)

    return pl.pallas_call(
        body, name=name,
        in_specs=[HBM, HBM, UNREAD],
        out_specs=[SEM, SEM, HBM, HBM, VMEM_WHOLE],
        out_shape=[pltpu.SemaphoreType.DMA((3,)), pltpu.SemaphoreType.DMA((3,)), pltpu.HBM(pbf.shape, pbf.dtype),
                   pltpu.HBM((3, rtot, c), pbf.dtype), jax.ShapeDtypeStruct((SUBLANES, LANES), F32)],
        input_output_aliases={0: 2, 1: 3},
        compiler_params=pltpu.CompilerParams(has_side_effects=EFFECT),
    )(_hbm(pbf), _hbm(lax.empty((3, rtot, c), pbf.dtype)), _hbm(after))


def _rs_chips_wait(send_sems, recv_sems, pbf, land, after, name):
    def body(pbf_ref, land_ref, send_ref, recv_ref, after_ref, pbf_out, land_out):
        me = _mesh_pos()
        for j in range(3):
            cp = _remote(pbf_ref.at[0], land_ref.at[j], send_ref.at[j], recv_ref.at[j], me)
            cp.wait_send()
            cp.wait_recv()

    return pl.pallas_call(
        body, name=name,
        in_specs=[HBM, HBM, SEM, SEM, UNREAD], out_specs=[HBM, HBM],
        out_shape=[pltpu.HBM(pbf.shape, pbf.dtype), pltpu.HBM(land.shape, land.dtype)],
        input_output_aliases={0: 0, 1: 1},
        compiler_params=pltpu.CompilerParams(has_side_effects=EFFECT),
    )(pbf, land, send_sems, recv_sems, _hbm(after))[1]


def _flips():
    return [(dx, dy, dc) for dx in (0, 1) for dy in (0, 1) for dc in (0, 1) if dx or dy or dc]


def _small_gather_start(flat, name):
    r, c = flat.shape

    def body(flat_ref, land_ref, send_sems, recv_sems, flat_thru, land_thru, token):
        x, y, cc = _mesh_pos()
        mine = land_ref.at[_dev_index((x, y, cc))]
        for k, (dx, dy, dc) in enumerate(_flips()):
            to = (1 - x if dx else x, 1 - y if dy else y, 1 - cc if dc else cc)
            _remote(flat_ref, mine, send_sems.at[k], recv_sems.at[k], to).start()
        token[...] = jnp.zeros_like(token)

    return pl.pallas_call(
        body, name=name,
        in_specs=[HBM, HBM],
        out_specs=[SEM, SEM, HBM, HBM, VMEM_WHOLE],
        out_shape=[pltpu.SemaphoreType.DMA((7,)), pltpu.SemaphoreType.DMA((7,)), pltpu.HBM(flat.shape, flat.dtype),
                   pltpu.HBM((N_DEV, r, c), flat.dtype), jax.ShapeDtypeStruct((SUBLANES, LANES), F32)],
        input_output_aliases={0: 2, 1: 3},
        compiler_params=pltpu.CompilerParams(has_side_effects=EFFECT),
    )(_hbm(flat), _hbm(lax.empty((N_DEV, r, c), flat.dtype)))


def _small_gather_wait(send_sems, recv_sems, flat, land, after, name):
    def body(flat_ref, land_ref, send_ref, recv_ref, after_ref, flat_out, land_out):
        me = _mesh_pos()
        for k in range(N_DEV - 1):
            cp = _remote(flat_ref, land_ref.at[0], send_ref.at[k], recv_ref.at[k], me)
            cp.wait_send()
            cp.wait_recv()

    return pl.pallas_call(
        body, name=name,
        in_specs=[HBM, HBM, SEM, SEM, UNREAD], out_specs=[HBM, HBM],
        out_shape=[pltpu.HBM(flat.shape, flat.dtype), pltpu.HBM(land.shape, land.dtype)],
        input_output_aliases={0: 0, 1: 1},
        compiler_params=pltpu.CompilerParams(has_side_effects=EFFECT),
    )(flat, land, send_sems, recv_sems, _hbm(after))


def _sum_devices(land, own):
    _, r, c = land.shape

    def body(land_ref, own_ref, out_ref):
        me = _dev_index(_mesh_pos())
        total = None
        for d in range(N_DEV):
            other = land_ref[jnp.where(d == me, (d + 1) % N_DEV, d)]
            block = jnp.where(d == me, own_ref[...], other)
            total = block if total is None else total + block
        out_ref[...] = total

    return pl.pallas_call(
        body, name="sum_devices",
        grid=(1,),
        in_specs=[pl.BlockSpec((N_DEV, r, c), lambda i: (0, 0, 0)), pl.BlockSpec((r, c), lambda i: (0, 0))],
        out_specs=pl.BlockSpec((r, c), lambda i: (0, 0)),
        out_shape=jax.ShapeDtypeStruct((r, c), F32),
        compiler_params=_cparams(("arbitrary",)),
    )(land, own)


def _rs_sibling_start(fulls, segs, name):
    ns = len(segs)
    offs = _seg_offsets(segs)
    rtot = sum(n * r for n, r in segs)
    c = fulls[0].shape[-1]
    dt = fulls[0].dtype

    def body(*refs):
        srcs = refs[:ns]
        land_ref, send_sem, recv_sem = refs[ns], refs[ns + 1], refs[ns + 2]
        token = refs[-1]
        x, y, cc = _mesh_pos()
        for k in range(4):
            for a, (n, r) in enumerate(segs):
                for m in range(n):
                    theirs = srcs[a].at[m, pl.ds(pl.multiple_of((2 * k + 1 - cc) * r, r), r), :]
                    _remote(theirs, land_ref.at[k, pl.ds(offs[a] + m * r, r), :], send_sem, recv_sem,
                            (x, y, 1 - cc)).start()
        token[...] = jnp.zeros_like(token)

    outs = pl.pallas_call(
        body, name=name,
        in_specs=[HBM] * (ns + 1),
        out_specs=[SEM, SEM] + [HBM] * (ns + 1) + [VMEM_WHOLE],
        out_shape=[pltpu.SemaphoreType.DMA(()), pltpu.SemaphoreType.DMA(())]
        + [pltpu.HBM(a.shape, a.dtype) for a in fulls] + [pltpu.HBM((4, rtot, c), dt),
                                                           jax.ShapeDtypeStruct((SUBLANES, LANES), F32)],
        input_output_aliases={i: 2 + i for i in range(ns + 1)},
        compiler_params=pltpu.CompilerParams(has_side_effects=EFFECT),
    )(*[_hbm(a) for a in fulls], _hbm(lax.empty((4, rtot, c), dt)))
    return outs[0], outs[1], list(outs[2:2 + ns]), outs[2 + ns], outs[-1]


def _rs_sibling_wait(send_sem, recv_sem, fulls, land, after, name):
    ns = len(fulls)

    def body(*refs):
        land_ref, send_ref, recv_ref = refs[ns], refs[ns + 1], refs[ns + 2]
        whole = _remote(land_ref, land_ref, send_ref, recv_ref, _mesh_pos())
        whole.wait_send()
        whole.wait_recv()

    outs = pl.pallas_call(
        body, name=name,
        in_specs=[HBM] * (ns + 1) + [SEM, SEM, UNREAD], out_specs=[HBM] * (ns + 1),
        out_shape=[pltpu.HBM(a.shape, a.dtype) for a in fulls] + [pltpu.HBM(land.shape, land.dtype)],
        input_output_aliases={i: i for i in range(ns + 1)},
        compiler_params=pltpu.CompilerParams(has_side_effects=EFFECT),
    )(*fulls, land, send_sem, recv_sem, _hbm(after))
    return list(outs[:ns]), outs[ns]


def _tp(w):
    return jnp.swapaxes(w, -1, -2)


def _s5_prepare(a_re, a_im, log_dt, b_re, b_im, c_re, c_im):
    a = jnp.stack([a_re, a_im], axis=1)
    ldt = jnp.broadcast_to(log_dt[:, :, None], (DEPTH, SSM_GROUPS, SSM_STATE))
    a_row = a.reshape(DEPTH, 2, 1, N_STATE)
    ldt_row = ldt.reshape(DEPTH, 1, N_STATE)
    a_rep = jnp.repeat(a, SSM_GROUP, axis=2)
    ldt_rep = jnp.repeat(ldt, SSM_GROUP, axis=1)
    bt = jnp.stack([_tp(b_re), _tp(b_im)], axis=1).reshape(DEPTH, 2, SSM_W, SSM_STATE)
    ct = jnp.stack([c_re, c_im], axis=1).reshape(DEPTH, 2, SSM_W, SSM_STATE)
    tile_e = jnp.tile(jnp.eye(SSM_STATE, dtype=BF16), (1, SSM_GROUPS))
    mask = jnp.repeat(jnp.repeat(jnp.eye(SSM_GROUPS, dtype=BF16), SSM_GROUP, axis=0), SSM_STATE, axis=1)
    out = []
    for l in range(DEPTH):
        tabs = _s5_disc(a_row[l], ldt_row[l], a_rep[l], ldt_rep[l], bt[l], ct[l], tile_e, mask)
        out.append(((a[l], ldt[l], a_rep[l], ldt_rep[l], bt[l], mask), *tabs))
    return out


def _layer_fwd(h, p_l, small, big, arrive=None):
    saved = {'h0': h}
    if arrive is not None:
        arrive(0, h)
    h, saved['gu1'] = _ffn_fwd(h, small['ffn1_norm'], big['ff1'])
    saved['h1'] = h
    if arrive is not None:
        arrive(1, h)
    z = _inproj_fwd(h, small['mix_norm'], big['wint'])
    ya, ys, hs = _s5conv_fwd(z, small['conv_w'], small['conv_b'], small['bbmat'], small['ccmat'], small['dvec'],
                             small['ltab'])
    saved.update(z=z, ya=ya, ys=ys, hs=hs)
    h = _mix_out_fwd(h, ya, ys, big['glu'], small['glu_b'], small['conv_out_norm'], small['ssm_out_norm'], big['wout'])
    saved['h2'] = h
    if arrive is not None:
        arrive(2, h)
    h, saved['gu2'] = _ffn_fwd(h, small['ffn2_norm'], big['ff2'])
    saved['h3'] = h
    h = _ple_fwd(h, small['ple_norm'], p_l, big['plg'], big['plpt'])
    return h, saved


def _ffn_bwd(h_in, g, dh, gu, w3):
    dh_in, dga, ud, dg = _ffn_bwd_act(h_in, g, dh, gu, w3)
    return dh_in, _matmul_tn(dga, ud, FF_BLOCK, BF16, "ffn_wgrad"), dg


def _layer_bwd_top(dh, p_l, small, big, saved):
    gs = {}
    dh, u, dq, dpp, pb, gs['ple_norm'] = _ple_bwd(saved['h3'], small['ple_norm'], p_l, dh, big['plg'], big['plpt'])
    d_plg = _matmul_tn(u, dq, 256, BF16, "ple_gate_wgrad")
    d_plpt = _matmul_tn(dpp, pb, 256, BF16, "ple_proj_wgrad", to_kernel=False)
    dh, d_ff2, gs['ffn2_norm'] = _ffn_bwd(saved['h2'], small['ffn2_norm'], dh, saved['gu2'], big['ff2'])
    return dh, (gs, d_plg, d_plpt, d_ff2)


def _layer_bwd_rest(dh, top, small, big, saved):
    gs, d_plg, d_plpt, d_ff2 = top
    dya, dys, ycat, dhb, zg, dq, part = _mix_out_bwd(dh, saved['ya'], saved['ys'], big['glu'], small['glu_b'],
                                                     small['conv_out_norm'], small['ssm_out_norm'], big['wout'])
    d_wout = _matmul_tn(ycat, dhb, 256, BF16, "w_out_wgrad")
    d_glu = _matmul_tn(zg, dq, 256, BF16, "glu_wgrad", to_kernel=False)
    dz, gadj, us, dyb, dl, dcw = _s5conv_bwd(saved['z'], saved['hs'], dya, dys, small['conv_w'], small['conv_b'],
                                             small['bbmat'], small['ccmat'], small['dvec'], small['ltab_rev'])
    d_bb = _block_wgrad(us, gadj, "s5_b_wgrad")
    d_cc = _block_wgrad(dyb, saved['hs'][None], "s5_c_wgrad")
    dh, u, gs['mix_norm'] = _inproj_bwd(saved['h1'], small['mix_norm'], dh, dz, big['wint'])
    d_wint = _matmul_tn(dz[None], u, 256, BF16, "w_in_wgrad")
    dh, d_ff1, gs['ffn1_norm'] = _ffn_bwd(saved['h0'], small['ffn1_norm'], dh, saved['gu1'], big['ff1'])

    dlb = dl[0].reshape(2, SSM_GROUPS, SSM_STATE)
    fold = jnp.tile(jnp.eye(SSM_STATE, dtype=BF16), (SSM_GROUPS, 1))
    da, dldt, dbt, dct = _s5_disc_bwd(*small['disc_in'], dlb, d_bb, d_cc, fold)
    gs['ssm_A_re'], gs['ssm_A_im'] = da[0], da[1]
    gs['ssm_log_dt'] = dldt[:, 0]
    ghp = (SSM_GROUPS, SSM_GROUP, SSM_STATE)
    gs['ssm_B_re'], gs['ssm_B_im'] = dbt[0].reshape(ghp), dbt[1].reshape(ghp)
    gs['ssm_C_re'], gs['ssm_C_im'] = dct[0].reshape(ghp), dct[1].reshape(ghp)
    gs['conv_w'] = dcw[0:3]
    gs['conv_b'] = dcw[3]
    gs['ssm_D'] = dcw[4].reshape(SSM_GROUPS, SSM_GROUP)
    gs['conv_out_norm'], gs['ssm_out_norm'], gs['glu_b'] = part[0], part[1], part[2]
    for n in ('ple_norm', 'ffn2_norm', 'mix_norm', 'ffn1_norm'):
        gs[n] = gs[n][0]
    fulls = [d_ff1, d_ff2, d_wint, d_wout, d_plg,
             d_plpt.reshape(1, D_MODEL * PLE_DIM // D_MODEL, D_MODEL), d_glu.reshape(1, SSM_W * SSM_W // D_MODEL, D_MODEL)]
    return dh, fulls, gs


VIEW_T = ('ffn1_w_gate', 'ffn1_w_up', 'ffn2_w_gate', 'ffn2_w_up', 'ssm_B_re', 'ssm_B_im')


def _view(name, a):
    return _tp(a) if name in VIEW_T else a


SEG_NAMES = ('ff1', 'ff2', 'wint', 'wout', 'plg', 'plpt', 'glu')
FIRST_LAYER_GROUPS = ((0,), (2, 3, 6), (1, 4, 5))


def _layer_pack(W, l, segments=range(len(SEGS))):
    pieces = {
        0: lambda: [_tp(W['ffn1_w_gate'][l]), _tp(W['ffn1_w_up'][l]), W['ffn1_w_down'][l]],
        1: lambda: [_tp(W['ffn2_w_gate'][l]), _tp(W['ffn2_w_up'][l]), W['ffn2_w_down'][l]],
        2: lambda: [_tp(W['w_in'][l])],
        3: lambda: [W['w_out'][l]],
        4: lambda: [W['ple_w_gate'][l]],
        5: lambda: [_tp(W['ple_w_proj'][l]).reshape(-1, D_MODEL)],
        6: lambda: [W['glu_w'][l].reshape(-1, D_MODEL)],
    }
    return jnp.concatenate([a for s in segments for a in pieces[s]()], axis=0).astype(BF16)


def _as_big(named):
    shape = dict(plpt=(D_MODEL, PLE_DIM), glu=(SSM_W, SSM_W))
    return {n: (a.reshape(shape[n]) if n in shape else a) for n, a in named.items()}


def _pad_rows(flat, mult, width=LANES):
    per = mult * width
    n = flat.shape[0]
    tot = -(-n // per) * per
    return jnp.pad(flat, (0, tot - n)).reshape(tot // width, width)


def _adamw_any(w, g, m, v):
    shp = w.shape
    two = (lambda t: t.reshape(-1, shp[-1]))
    d, nm, nv = _adamw(two(w), two(g), two(m), two(v))
    return d.reshape(shp), nm.reshape(shp), nv.reshape(shp)


def kernel(x, p, ffn1_norm, ffn1_w_gate, ffn1_w_up, ffn1_w_down, mix_norm, w_in, conv_w, conv_b, ssm_A_re, ssm_A_im, ssm_B_re, ssm_B_im, ssm_C_re, ssm_C_im, ssm_D, ssm_log_dt, glu_w, glu_b, conv_out_norm, ssm_out_norm, w_out, ffn2_norm, ffn2_w_gate, ffn2_w_up, ffn2_w_down, ple_norm, ple_w_gate, ple_w_proj, final_norm, loss_target, m_ffn1_norm, m_ffn1_w_gate, m_ffn1_w_up, m_ffn1_w_down, m_mix_norm, m_w_in, m_conv_w, m_conv_b, m_ssm_A_re, m_ssm_A_im, m_ssm_B_re, m_ssm_B_im, m_ssm_C_re, m_ssm_C_im, m_ssm_D, m_ssm_log_dt, m_glu_w, m_glu_b, m_conv_out_norm, m_ssm_out_norm, m_w_out, m_ffn2_norm, m_ffn2_w_gate, m_ffn2_w_up, m_ffn2_w_down, m_ple_norm, m_ple_w_gate, m_ple_w_proj, m_final_norm, v_ffn1_norm, v_ffn1_w_gate, v_ffn1_w_up, v_ffn1_w_down, v_mix_norm, v_w_in, v_conv_w, v_conv_b, v_ssm_A_re, v_ssm_A_im, v_ssm_B_re, v_ssm_B_im, v_ssm_C_re, v_ssm_C_im, v_ssm_D, v_ssm_log_dt, v_glu_w, v_glu_b, v_conv_out_norm, v_ssm_out_norm, v_w_out, v_ffn2_norm, v_ffn2_w_gate, v_ffn2_w_up, v_ffn2_w_down, v_ple_norm, v_ple_w_gate, v_ple_w_proj, v_final_norm):
    given = dict(locals())
    W = {n: given[n] for n in W_NAMES}
    M = {n: given['m_' + n] for n in W_NAMES}
    V = {n: given['v_' + n] for n in W_NAMES}
    Wv, Mv, Vv = [{n: _view(n, d[n]) for n in W_NAMES} for d in (W, M, V)]
    my_dev = _dev_index(_mesh_pos())

    conv_shard = _pad_rows(W['conv_w'].reshape(-1), SUBLANES)
    conv_all = _allgather(conv_shard, ((1, SUBLANES),), "ag_conv_w")[0]
    conv_full = conv_all.reshape(N_DEV, -1)[:, :DEPTH * 3 * (CONV_W // N_DEV)]
    conv_full = conv_full.reshape(N_DEV, DEPTH, 3, CONV_W // N_DEV).transpose(1, 2, 0, 3).reshape(DEPTH, 3, CONV_W)
    first, after = [], conv_all
    for gi, segments in enumerate(FIRST_LAYER_GROUPS):
        first.append(_ag_start(_layer_pack(W, 0, segments), tuple(SEGS[s] for s in segments), after,
                               "ag_start_0%s" % "abc"[gi]))
        after = first[-1][4]
    packs = [None] + [_layer_pack(W, l) for l in range(1, DEPTH)]
    flights = {1: _ag_start(packs[1], SEGS, after, "ag_start_1")}
    after = flights[1][4]
    s5 = _s5_prepare(*[W[n] + after[0, 0] for n in ('ssm_A_re', 'ssm_A_im', 'ssm_log_dt')],
                     *[W[n] for n in ('ssm_B_re', 'ssm_B_im', 'ssm_C_re', 'ssm_C_im')])
    prepared = conv_full[0, 0:1, 0:1] + s5[DEPTH - 1][1][0:1, 0:1] + packs[DEPTH - 1][0:1, 0:1].astype(F32)

    smalls, saves, bigs = [], [], []
    h = x[0]

    def gathered(handles, segments, after, name, next_layer=None, gate=None):
        send_sems, recv_sems, pack_thru, lands, _ = handles
        pack_thru, lands = _ag_wait(send_sems, recv_sems, pack_thru, lands, after, "ag_wait_" + name)
        if next_layer is not None:
            flights[next_layer] = _ag_start(packs[next_layer], SEGS, pack_thru, "ag_start_%d" % next_layer)
            gate[0][gate[1]] = gate[0][gate[1]] + flights[next_layer][4][0:1, 0:1]
        outs = _ag_finish(pack_thru, lands, tuple(SEGS[s] for s in segments))
        return _as_big({SEG_NAMES[s]: a for s, a in zip(segments, outs)})

    for l in range(DEPTH):
        small = {n: W[n][l][None] for n in ('ffn1_norm', 'mix_norm', 'conv_b', 'glu_b', 'conv_out_norm',
                                            'ssm_out_norm', 'ffn2_norm', 'ple_norm')}
        small['conv_w'] = conv_full[l]
        small['dvec'] = W['ssm_D'][l].reshape(1, SSM_W)
        small['disc_in'], small['ltab'], small['ltab_rev'], small['bbmat'], small['ccmat'] = s5[l]
        big = {}
        bigs.append(big)
        if l == 0:
            def arrive(stage, h_now, big=big, small=small):
                big.update(gathered(first[stage], FIRST_LAYER_GROUPS[stage], prepared if stage == 0 else h_now,
                                    "0%s" % "abc"[stage], *((2, (small, 'ffn2_norm')) if stage == 2 else ())))
            h, saved = _layer_fwd(h, p[l, 0], small, big, arrive)
        else:
            nxt = (l + 2, (small, 'ffn1_norm')) if l + 2 < DEPTH else ()
            big.update(gathered(flights[l], range(len(SEGS)), h, "%d" % l, *nxt))
            h, saved = _layer_fwd(h, p[l, 0], small, big)
        smalls.append(small)
        saves.append(saved)
    loss_tile, dh, d_final = _final_loss(h, W['final_norm'][None], loss_target[0])
    loss = lax.psum(loss_tile[0, 0], ("x", "y", "c"))

    layer_gs = [None] * DEPTH
    shard_grads = None
    sib, ici = None, None
    pack_offs = _seg_offsets(SEGS)

    def start_sibling(name, layer, segments, fulls):
        return (name, layer, segments, _rs_sibling_start(fulls, tuple(SEGS[s] for s in segments), "sib_start_" + name))

    def to_chips(sibling, after_sib, after_ici):
        name, layer, segments, (send_sem, recv_sem, fulls_thru, land, _) = sibling
        fulls_thru, got = _rs_sibling_wait(send_sem, recv_sem, fulls_thru, land, after_sib, "sib_wait_" + name)
        pbf = _pair_sum(fulls_thru, got, tuple(SEGS[s] for s in segments))
        return (name, layer, segments, _rs_chips_start(pbf, after_ici, "rs_start_" + name), fulls_thru, got)

    def finish_chips(flight, after):
        nonlocal shard_grads
        name, layer, segments, (send_sems, recv_sems, pbf_thru, land, _), fulls_up, got_up = flight
        got3 = _rs_chips_wait(send_sems, recv_sems, pbf_thru, land, after, "rs_wait_" + name)
        shard_grads = _chip_sum(fulls_up, got_up, got3, tuple(SEGS[s] for s in segments),
                                {i: pack_offs[s] for i, s in enumerate(segments)}, layer, shard_grads)
        return shard_grads

    def finish_sibling(after_sib, after_ici):
        nonlocal sib, ici
        nxt = to_chips(sib, after_sib, after_ici if ici is None else finish_chips(ici, after_ici))
        sib, ici = None, nxt

    layer_names = [n for n in SMALL_NAMES if n != 'final_norm']
    small_flights = [None] * DEPTH
    all_segments = tuple(range(len(SEGS)))
    early_segments = FIRST_LAYER_GROUPS[2]
    late_segments = tuple(s for s in all_segments if s not in early_segments)
    early = None
    for l in reversed(range(DEPTH)):
        small = dict(smalls[l])
        if sib is not None:
            small['ple_norm'] = small['ple_norm'] + sib[3][4][0:1, 0:1] + small_flights[l + 1][4][0:1, 0:1]
        dh, top = _layer_bwd_top(dh, p[l, 0], small, bigs[l], saves[l])
        if sib is not None:
            finish_sibling(dh, dh)
            small['glu_b'] = small['glu_b'] + ici[3][4][0:1, 0:1]
        if l == 0:
            _, d_plg, d_plpt, d_ff2 = top
            early = to_chips(start_sibling("0e", 0, early_segments, [d_ff2, d_plg, d_plpt.reshape(1, -1, D_MODEL)]),
                             ici[3][4], ici[3][4])
            small['glu_b'] = small['glu_b'] + early[3][4][0:1, 0:1]
        dh, fulls, layer_gs[l] = _layer_bwd_rest(dh, top, small, bigs[l], saves[l])
        segments = late_segments if l == 0 else all_segments
        sib = start_sibling("%d" % l, l, segments, [fulls[s] for s in segments])
        last_slot = d_final[0] if l == DEPTH - 1 else jnp.zeros((D_MODEL,), F32)
        flat = jnp.concatenate([layer_gs[l][n].reshape(-1) for n in layer_names + ['conv_w']] + [last_slot])
        small_flights[l] = _small_gather_start(_pad_rows(flat, SUBLANES, D_MODEL), "small_start_%d" % l)
    grad_x = dh[None]
    finish_sibling(small_flights[0][4], small_flights[0][4])

    reduced = []
    for l in range(DEPTH):
        send_sems, recv_sems, flat_thru, land, _ = small_flights[l]
        flat_thru, land = _small_gather_wait(send_sems, recv_sems, flat_thru, land, ici[3][4], "small_wait_%d" % l)
        reduced.append(_sum_devices(land, flat_thru).reshape(-1))
    reduced = jnp.stack(reduced)
    G = {}
    o = 0
    for n in layer_names + ['conv_w']:
        size = (W[n].size if n != 'conv_w' else DEPTH * 3 * CONV_W) // DEPTH
        shape = Wv[n].shape if n != 'conv_w' else (DEPTH, 3, CONV_W)
        G[n] = reduced[:, o:o + size].reshape(shape)
        o += size
    G['final_norm'] = reduced[DEPTH - 1, o:o + D_MODEL]
    G['conv_w'] = lax.dynamic_slice_in_dim(G['conv_w'], my_dev * (CONV_W // N_DEV), CONV_W // N_DEV, axis=2)

    delta, new_m, new_v = {}, {}, {}
    for n in SMALL_NAMES + ['conv_w']:
        two = (lambda t: t.reshape(1, -1) if t.ndim == 1 else t)
        delta[n], new_m[n], new_v[n] = [t.reshape(Wv[n].shape) for t in
                                        _adamw_any(two(Wv[n]), two(G[n]), two(Mv[n]), two(Vv[n]))]

    offs = _seg_offsets(SEGS)
    r = SEGS[0][1]
    packed_rows = {'w_out': offs[3], 'ple_w_gate': offs[4]}
    for a, f in ((0, 'ffn1'), (1, 'ffn2')):
        packed_rows.update({f + '_w_gate': offs[a], f + '_w_up': offs[a] + r, f + '_w_down': offs[a] + 2 * r})

    def relaid(sg):
        nl = sg.shape[0]
        return {'w_in': _tp(sg[:, offs[2]:offs[2] + SEGS[2][1]]),
                'ple_w_proj': _tp(sg[:, offs[5]:offs[5] + SEGS[5][1]].reshape(nl, D_MODEL // N_DEV, PLE_DIM)),
                'glu_w': sg[:, offs[6]:offs[6] + SEGS[6][1]].reshape(nl, SSM_W // N_DEV, SSM_W)}

    groups = {}
    for n in list(packed_rows) + ['w_in', 'ple_w_proj', 'glu_w']:
        groups.setdefault(Wv[n].shape, []).append(n)

    def update(first, nl, prev):
        other = relaid(shard_grads[first:first + nl])
        sets = lambda ns: [(Wv[n], Mv[n], Vv[n], shard_grads, packed_rows[n]) if n in packed_rows
                           else (Wv[n], Mv[n], Vv[n], other[n], None) for n in ns]
        return {shape: _adamw_layers(sets(ns), first, nl, None if prev is None else prev[shape])
                for shape, ns in groups.items()}

    part = update(1, DEPTH - 1, None)
    behind = (sum(four[3][1, 0:1, 0:1] for fours in part.values() for four in fours)
              + sum(new_v[n][(0,) * new_v[n].ndim].reshape(1, 1) for n in SMALL_NAMES + ['conv_w']))
    finish_chips(early, behind)
    finish_chips(ici, behind)
    for shape, fours in update(0, 1, part).items():
        for n, four in zip(groups[shape], fours):
            G[n], delta[n], new_m[n], new_v[n] = four

    outs = [[_view(n, d[n]) for n in W_NAMES] for d in (G, delta, new_m, new_v)]
    return (loss, grad_x, *outs[0], *outs[1], *outs[2], *outs[3])
```

```python
import math

import jax
import jax.numpy as jnp
from jax import lax
from jax.experimental import pallas as pl
from jax.experimental.pallas import tpu as pltpu

F32 = jnp.float32
BF16 = jnp.bfloat16

N_DEV = 8
DEPTH = 4
SEQ = 2048
D_MODEL = 1024
D_FF = 2816
CONV_W = 512
SSM_W = 512
SSM_GROUPS = 32
SSM_GROUP = 16
SSM_STATE = 64
N_STATE = SSM_GROUPS * SSM_STATE
IN_COLS = 2048
PLE_DIM = 256
EPS = 1e-6

ADAM_LR = 0.001
ADAM_B1 = 0.9
ADAM_B2 = 0.999
ADAM_EPS = 1e-08
ADAM_WD = 0.01
ADAM_STEP = 10

FF_BLOCK = 256
N_FF_BLOCKS = D_FF // FF_BLOCK
TOK_TILE_FFN_FWD = 2048
TOK_TILE_FFN_BWD = 1024
TOK_TILE = 512
CHUNK = 256
N_CHUNKS = SEQ // CHUNK
LANE_GROUP = 512
SUBLANES = 8
LANES = 128
MIB = 1024 * 1024

W_NAMES = ['ffn1_norm', 'ffn1_w_gate', 'ffn1_w_up', 'ffn1_w_down', 'mix_norm', 'w_in', 'conv_w', 'conv_b',
           'ssm_A_re', 'ssm_A_im', 'ssm_B_re', 'ssm_B_im', 'ssm_C_re', 'ssm_C_im', 'ssm_D', 'ssm_log_dt',
           'glu_w', 'glu_b', 'conv_out_norm', 'ssm_out_norm', 'w_out', 'ffn2_norm', 'ffn2_w_gate', 'ffn2_w_up',
           'ffn2_w_down', 'ple_norm', 'ple_w_gate', 'ple_w_proj', 'final_norm']
SMALL_NAMES = ['ffn1_norm', 'mix_norm', 'conv_b', 'ssm_A_re', 'ssm_A_im', 'ssm_B_re', 'ssm_B_im', 'ssm_C_re',
               'ssm_C_im', 'ssm_D', 'ssm_log_dt', 'glu_b', 'conv_out_norm', 'ssm_out_norm', 'ffn2_norm',
               'ple_norm', 'final_norm']

SEGS = ((3, 352), (3, 352), (1, 256), (1, 128), (1, 128), (1, 32), (1, 32))
PACK_ROWS = sum(n * r for n, r in SEGS)

MESH = pl.DeviceIdType.MESH
UNREAD = pl.BlockSpec(memory_space=pltpu.HBM)


def _in_hbm(*arrays):
    return [pltpu.with_memory_space_constraint(a, pltpu.HBM) for a in arrays]


def _out_hbm(outs, which):
    if not isinstance(outs, (list, tuple)):
        return pltpu.with_memory_space_constraint(outs, pltpu.HBM) if which else outs
    return [pltpu.with_memory_space_constraint(a, pltpu.HBM) if i in which else a for i, a in enumerate(outs)]


def _cparams(sem=None, vmem_mib=48, **kw):
    return pltpu.CompilerParams(dimension_semantics=sem, vmem_limit_bytes=vmem_mib * MIB, **kw)


def _dot(a, b):
    return jnp.dot(a, b, preferred_element_type=F32)


def _dot_nt(a, b):
    return lax.dot_general(a, b, (((1,), (1,)), ((), ())), preferred_element_type=F32)


def _dot_tn(a, b):
    return lax.dot_general(a, b, (((0,), (0,)), ((), ())), preferred_element_type=F32)


def _rms_stats(x):
    r = lax.rsqrt(jnp.mean(x * x, axis=-1, keepdims=True) + EPS)
    return x * r, r


def _rms_bwd(dy, xh, r, g):
    dxh = dy * g
    dx = r * (dxh - xh * jnp.mean(dxh * xh, axis=-1, keepdims=True))
    dg = jnp.sum(dy * xh, axis=0, keepdims=True)
    return dx, dg


def _sigmoid(x):
    return 0.5 * jnp.tanh(0.5 * x) + 0.5


_GELU_C = math.sqrt(2.0 / math.pi)


def _gelu(x):
    t = jnp.tanh(_GELU_C * (x + 0.044715 * x * x * x))
    return 0.5 * x * (1.0 + t), t


def _gelu_grad(x, t):
    return 0.5 * (1.0 + t) + 0.5 * x * (1.0 - t * t) * _GELU_C * (1.0 + 3.0 * 0.044715 * x * x)


def _accumulate(ref, first, value):
    @pl.when(first)
    def _():
        ref[...] = value

    @pl.when(jnp.logical_not(first))
    def _():
        ref[...] += value


def _ffn_fwd(h, g, w3):
    tm = TOK_TILE_FFN_FWD
    last = N_FF_BLOCKS - 1

    def body(h_ref, g_ref, wgu_ref, wd_ref, wd_last_ref, out_ref, gu_ref, u_ref, a_ref):
        k = pl.program_id(1)

        @pl.when(k == 0)
        def _():
            x = h_ref[...]
            xh, _ = _rms_stats(x)
            u_ref[...] = (xh * g_ref[...]).astype(BF16)
            out_ref[...] = x
            a_ref[1] = jnp.zeros((tm, FF_BLOCK), BF16)

        out_ref[...] += 0.5 * _dot(a_ref[(k + 1) % 2], wd_ref[0])
        gu = _dot_nt(u_ref[...], wgu_ref[...].reshape(2 * FF_BLOCK, D_MODEL))
        gate, up = gu[:, :FF_BLOCK], gu[:, FF_BLOCK:]
        a_ref[k % 2] = (gate * _sigmoid(gate) * up).astype(BF16)
        gu_ref[0] = gate.astype(BF16)
        gu_ref[1] = up.astype(BF16)

        @pl.when(k == last)
        def _():
            out_ref[...] += 0.5 * _dot(a_ref[last % 2], wd_last_ref[0])

    return _out_hbm(pl.pallas_call(
        body, name="ffn_fwd",
        grid=(SEQ // tm, N_FF_BLOCKS),
        in_specs=[pl.BlockSpec((tm, D_MODEL), lambda m, k: (m, 0), pipeline_mode=pl.Buffered(1)),
                  pl.BlockSpec((1, D_MODEL), lambda m, k: (0, 0)),
                  pl.BlockSpec((2, FF_BLOCK, D_MODEL), lambda m, k: (0, k, 0)),
                  pl.BlockSpec((1, FF_BLOCK, D_MODEL), lambda m, k: (2, jnp.maximum(k - 1, 0), 0)),
                  pl.BlockSpec((1, FF_BLOCK, D_MODEL), lambda m, k: (2, last, 0), pipeline_mode=pl.Buffered(1))],
        out_specs=[pl.BlockSpec((tm, D_MODEL), lambda m, k: (m, 0)),
                   pl.BlockSpec((2, tm, FF_BLOCK), lambda m, k: (0, m, k))],
        out_shape=[jax.ShapeDtypeStruct((SEQ, D_MODEL), F32),
                   pltpu.HBM((2, SEQ, D_FF), BF16)],
        scratch_shapes=[pltpu.VMEM((tm, D_MODEL), BF16), pltpu.VMEM((2, tm, FF_BLOCK), BF16)],
        compiler_params=_cparams(("parallel", "arbitrary"), 56),
    )(*_in_hbm(h, g, w3, w3, w3)), (1,))


def _ffn_bwd_act(h, g, dout, gu, w3):
    tm = TOK_TILE_FFN_BWD
    last = N_FF_BLOCKS - 1

    def body(h_ref, g_ref, d_ref, gu_ref, wd_ref, wgu_ref, wgu_last_ref, dh_ref, dga_ref, ud_ref, dg_ref,
             acc_ref, dgu_ref):
        m = pl.program_id(0)
        k = pl.program_id(1)

        @pl.when(k == 0)
        def _():
            xh, _ = _rms_stats(h_ref[...])
            ud_ref[0] = (xh * g_ref[...]).astype(BF16)
            ud_ref[1] = (0.5 * d_ref[...]).astype(BF16)
            acc_ref[...] = jnp.zeros_like(acc_ref)
            dgu_ref[1] = jnp.zeros((tm, 2 * FF_BLOCK), BF16)

        acc_ref[...] += _dot(dgu_ref[(k + 1) % 2], wgu_ref[...].reshape(2 * FF_BLOCK, D_MODEL))
        gate = gu_ref[0].astype(F32)
        up = gu_ref[1].astype(F32)
        sg = _sigmoid(gate)
        silu = gate * sg
        da = _dot_nt(ud_ref[1], wd_ref[0])
        dgate = (da * up * (sg + silu * (1.0 - sg))).astype(BF16)
        dup = (da * silu).astype(BF16)
        dga_ref[0] = dgate
        dga_ref[1] = dup
        dga_ref[2] = (silu * up).astype(BF16)
        dgu_ref[k % 2, :, 0:FF_BLOCK] = dgate
        dgu_ref[k % 2, :, FF_BLOCK:2 * FF_BLOCK] = dup

        @pl.when(k == last)
        def _():
            du = acc_ref[...] + _dot(dgu_ref[last % 2], wgu_last_ref[...].reshape(2 * FF_BLOCK, D_MODEL))
            xh, r = _rms_stats(h_ref[...])
            dx, dg = _rms_bwd(du, xh, r, g_ref[...])
            dh_ref[...] = d_ref[...] + dx
            _accumulate(dg_ref, m == 0, dg)

    return _out_hbm(pl.pallas_call(
        body, name="ffn_bwd_act",
        grid=(SEQ // tm, N_FF_BLOCKS),
        in_specs=[pl.BlockSpec((tm, D_MODEL), lambda m, k: (m, 0), pipeline_mode=pl.Buffered(1)),
                  pl.BlockSpec((1, D_MODEL), lambda m, k: (0, 0)),
                  pl.BlockSpec((tm, D_MODEL), lambda m, k: (m, 0), pipeline_mode=pl.Buffered(1)),
                  pl.BlockSpec((2, tm, FF_BLOCK), lambda m, k: (0, m, k)),
                  pl.BlockSpec((1, FF_BLOCK, D_MODEL), lambda m, k: (2, k, 0)),
                  pl.BlockSpec((2, FF_BLOCK, D_MODEL), lambda m, k: (0, jnp.maximum(k - 1, 0), 0)),
                  pl.BlockSpec((2, FF_BLOCK, D_MODEL), lambda m, k: (0, last, 0), pipeline_mode=pl.Buffered(1))],
        out_specs=[pl.BlockSpec((tm, D_MODEL), lambda m, k: (m, 0)),
                   pl.BlockSpec((3, tm, FF_BLOCK), lambda m, k: (0, m, k)),
                   pl.BlockSpec((2, tm, D_MODEL), lambda m, k: (0, m, 0)),
                   pl.BlockSpec((1, D_MODEL), lambda m, k: (0, 0))],
        out_shape=[jax.ShapeDtypeStruct((SEQ, D_MODEL), F32),
                   pltpu.HBM((3, SEQ, D_FF), BF16),
                   pltpu.HBM((2, SEQ, D_MODEL), BF16),
                   jax.ShapeDtypeStruct((1, D_MODEL), F32)],
        scratch_shapes=[pltpu.VMEM((tm, D_MODEL), F32), pltpu.VMEM((2, tm, 2 * FF_BLOCK), BF16)],
        compiler_params=_cparams(("arbitrary", "arbitrary"), 56),
    )(*_in_hbm(h, g, dout, gu, w3, w3, w3)), (1, 2))


def _matmul_tn(a, b, bm, out_dtype, name, bn=None, to_kernel=True):
    na, t, m = a.shape
    nb, _, n = b.shape
    bn = n if bn is None else bn

    def body(a_ref, b_ref, o_ref):
        o_ref[0] = _dot_tn(a_ref[0], b_ref[0]).astype(out_dtype)

    return _out_hbm(pl.pallas_call(
        body, name=name,
        grid=(na, m // bm, n // bn),
        in_specs=[pl.BlockSpec((1, t, bm), lambda i, k, j: (i, 0, k)),
                  pl.BlockSpec((1, t, bn), lambda i, k, j: (jnp.maximum(i - (na - nb), 0), 0, j))],
        out_specs=pl.BlockSpec((1, bm, bn), lambda i, k, j: (i, k, j)),
        out_shape=pltpu.HBM((na, m, n), out_dtype) if to_kernel else jax.ShapeDtypeStruct((na, m, n), out_dtype),
        compiler_params=_cparams(("arbitrary", "parallel", "parallel")),
    )(*_in_hbm(a, b)), to_kernel)


def _inproj_fwd(h, g, wint):
    tm = TOK_TILE

    def body(h_ref, g_ref, w_ref, z_ref):
        xh, _ = _rms_stats(h_ref[...])
        z_ref[...] = _dot_nt((xh * g_ref[...]).astype(BF16), w_ref[...])

    return pl.pallas_call(
        body, name="inproj_fwd",
        grid=(SEQ // tm,),
        in_specs=[pl.BlockSpec((tm, D_MODEL), lambda m: (m, 0)),
                  pl.BlockSpec((1, D_MODEL), lambda m: (0, 0)),
                  pl.BlockSpec((None, IN_COLS, D_MODEL), lambda m: (0, 0, 0))],
        out_specs=pl.BlockSpec((tm, IN_COLS), lambda m: (m, 0)),
        out_shape=jax.ShapeDtypeStruct((SEQ, IN_COLS), F32),
        compiler_params=_cparams(("parallel",)),
    )(*_in_hbm(h, g, wint))


def _inproj_bwd(h, g, dh, dz, wint):
    tm = TOK_TILE

    def body(h_ref, g_ref, dh_ref, dz_ref, w_ref, o_ref, u_ref, dg_ref):
        xh, r = _rms_stats(h_ref[...])
        u_ref[0] = (xh * g_ref[...]).astype(BF16)
        dx, dg = _rms_bwd(_dot(dz_ref[...], w_ref[...]), xh, r, g_ref[...])
        o_ref[...] = dh_ref[...] + dx
        _accumulate(dg_ref, pl.program_id(0) == 0, dg)

    return _out_hbm(pl.pallas_call(
        body, name="inproj_bwd",
        grid=(SEQ // tm,),
        in_specs=[pl.BlockSpec((tm, D_MODEL), lambda m: (m, 0)),
                  pl.BlockSpec((1, D_MODEL), lambda m: (0, 0)),
                  pl.BlockSpec((tm, D_MODEL), lambda m: (m, 0)),
                  pl.BlockSpec((tm, IN_COLS), lambda m: (m, 0)),
                  pl.BlockSpec((None, IN_COLS, D_MODEL), lambda m: (0, 0, 0))],
        out_specs=[pl.BlockSpec((tm, D_MODEL), lambda m: (m, 0)),
                   pl.BlockSpec((1, tm, D_MODEL), lambda m: (0, m, 0)),
                   pl.BlockSpec((1, D_MODEL), lambda m: (0, 0))],
        out_shape=[jax.ShapeDtypeStruct((SEQ, D_MODEL), F32),
                   pltpu.HBM((1, SEQ, D_MODEL), BF16),
                   jax.ShapeDtypeStruct((1, D_MODEL), F32)],
        compiler_params=_cparams(("arbitrary",)),
    )(*_in_hbm(h, g, dh, dz, wint)), (1,))


def _row_ids(n, w):
    return lax.broadcasted_iota(jnp.int32, (n, w), 0)


def _bcast_row(x, i, n):
    return jnp.broadcast_to(x[i:i + 1, :], (n, x.shape[1]))


def _conv_taps(v, tail):
    n, w = v.shape
    rid = _row_ids(n, w)
    v1 = jnp.where(rid == 0, _bcast_row(tail, 7, n), pltpu.roll(v, 1, 0))
    v2 = jnp.where(rid == 0, _bcast_row(tail, 6, n),
                   jnp.where(rid == 1, _bcast_row(tail, 7, n), pltpu.roll(v, 2, 0)))
    return v1, v2


def _block_tiles():
    half_rows, half_cols = SSM_W // 2, N_STATE // 2
    for half in range(2):
        for part in range(2):
            yield (slice(half * half_rows, (half + 1) * half_rows),
                   slice(part * N_STATE + half * half_cols, part * N_STATE + (half + 1) * half_cols))


def _block_expand(x, mat_ref, out_ref):
    for rows, cols in _block_tiles():
        out_ref[:, cols] = _dot(x[:, rows], mat_ref[rows, cols])


def _block_contract(s, mat_ref):
    halves = {}
    for rows, cols in _block_tiles():
        part = _dot_nt(s[:, cols], mat_ref[rows, cols])
        halves[rows.start] = part if rows.start not in halves else halves[rows.start] + part
    return jnp.concatenate([halves[k] for k in sorted(halves)], axis=1)


def _block_wgrad(a, b, name):
    t = a.shape[1]
    half_rows, half_cols = SSM_W // 2, N_STATE // 2

    def body(a_ref, b_ref, o_ref):
        o_ref[...] = _dot_tn(a_ref[...], b_ref[...])

    return pl.pallas_call(
        body, name=name,
        grid=(2, 2),
        in_specs=[pl.BlockSpec((None, t, half_rows), lambda h, p: (0, 0, h)),
                  pl.BlockSpec((None, t, half_cols), lambda h, p: (0, 0, 2 * p + h))],
        out_specs=pl.BlockSpec((half_rows, half_cols), lambda h, p: (h, 2 * p + h)),
        out_shape=jax.ShapeDtypeStruct((SSM_W, 2 * N_STATE), F32),
        compiler_params=_cparams(("parallel", "parallel")),
    )(*_in_hbm(a, b))


def _scan_chunk(work, ltab, carry, reverse):
    nblk = CHUNK // SUBLANES
    for gi in range(N_STATE // LANE_GROUP):
        cre = pl.ds(gi * LANE_GROUP, LANE_GROUP)
        cim = pl.ds(N_STATE + gi * LANE_GROUP, LANE_GROUP)
        pows = [(ltab[8 * k:8 * k + 8, cre], ltab[8 * k:8 * k + 8, cim]) for k in range(3)]
        pr = ltab[24:32, cre]
        pi = ltab[24:32, cim]

        def blk(i, c, cre=cre, cim=cim, pows=pows, pr=pr, pi=pi):
            cr, ci = c
            b = (nblk - 1 - i) if reverse else i
            r0 = pl.multiple_of(b * SUBLANES, SUBLANES)
            xr = work[pl.ds(r0, SUBLANES), cre]
            xi = work[pl.ds(r0, SUBLANES), cim]
            for k, s in enumerate((1, 2, 4)):
                lr, li = pows[k]
                shift = SUBLANES - s if reverse else s
                sr = pltpu.roll(xr, shift, 0)
                si = pltpu.roll(xi, shift, 0)
                xr, xi = xr + lr * sr - li * si, xi + lr * si + li * sr
            xr, xi = xr + pr * cr - pi * ci, xi + pr * ci + pi * cr
            work[pl.ds(r0, SUBLANES), cre] = xr
            work[pl.ds(r0, SUBLANES), cim] = xi
            edge = 0 if reverse else SUBLANES - 1
            return _bcast_row(xr, edge, SUBLANES), _bcast_row(xi, edge, SUBLANES)

        cr, ci = lax.fori_loop(0, nblk, blk, (carry[:, cre], carry[:, cim]))
        carry[:, cre] = cr
        carry[:, cim] = ci


def _s5conv_fwd(z, convw, convb, bbmat, ccmat, dvec, ltab):
    def body(z_ref, cw_ref, cb_ref, bb_ref, cc_ref, d_ref, lt_ref, ya_ref, ys_ref, hs_ref,
             work, carry, tail):
        c = pl.program_id(0)

        @pl.when(c == 0)
        def _():
            carry[...] = jnp.zeros_like(carry)
            tail[...] = jnp.zeros_like(tail)

        zb = z_ref[:, 0:CONV_W]
        v = z_ref[:, CONV_W:2 * CONV_W] * z_ref[:, 2 * CONV_W:3 * CONV_W]
        us = z_ref[:, 3 * CONV_W:4 * CONV_W]
        v1, v2 = _conv_taps(v, tail[...])
        tail[...] = v[CHUNK - 8:CHUNK, :]
        y = cw_ref[0:1, :] * v2 + cw_ref[1:2, :] * v1 + cw_ref[2:3, :] * v
        ya_ref[...] = zb * (y + cb_ref[...])

        _block_expand(us.astype(BF16), bb_ref, work)
        _scan_chunk(work, lt_ref, carry, reverse=False)
        hs = work[...].astype(BF16)
        hs_ref[...] = hs
        ys_ref[...] = _block_contract(hs, cc_ref) + d_ref[...] * us

    return _out_hbm(pl.pallas_call(
        body, name="s5conv_fwd",
        grid=(N_CHUNKS,),
        in_specs=[pl.BlockSpec((CHUNK, IN_COLS), lambda c: (c, 0)),
                  pl.BlockSpec((3, CONV_W), lambda c: (0, 0)),
                  pl.BlockSpec((1, CONV_W), lambda c: (0, 0)),
                  pl.BlockSpec((SSM_W, 2 * N_STATE), lambda c: (0, 0)),
                  pl.BlockSpec((SSM_W, 2 * N_STATE), lambda c: (0, 0)),
                  pl.BlockSpec((1, SSM_W), lambda c: (0, 0)),
                  pl.BlockSpec((32, 2 * N_STATE), lambda c: (0, 0))],
        out_specs=[pl.BlockSpec((CHUNK, CONV_W), lambda c: (c, 0)),
                   pl.BlockSpec((CHUNK, SSM_W), lambda c: (c, 0)),
                   pl.BlockSpec((CHUNK, 2 * N_STATE), lambda c: (c, 0))],
        out_shape=[pltpu.HBM((SEQ, CONV_W), F32),
                   pltpu.HBM((SEQ, SSM_W), F32),
                   jax.ShapeDtypeStruct((SEQ, 2 * N_STATE), BF16)],
        scratch_shapes=[pltpu.VMEM((CHUNK, 2 * N_STATE), F32),
                        pltpu.VMEM((8, 2 * N_STATE), F32),
                        pltpu.VMEM((8, CONV_W), F32)],
        compiler_params=_cparams(("arbitrary",)),
    )(*_in_hbm(z, convw, convb, bbmat, ccmat, dvec, ltab)), (0, 1))


def _s5conv_bwd(z, hs, dya, dys, convw, convb, bbmat, ccmat, dvec, ltab_rev):
    nc = N_CHUNKS
    hb = 16

    def body(z_ref, zp_ref, hs_ref, hp_ref, dya_ref, dys_ref, cw_ref, cb_ref, bb_ref, cc_ref, d_ref, lt_ref,
             dz_ref, g_ref, us_ref, dyb_ref, dl_ref, dcw_ref, work, carry, head):
        i = pl.program_id(0)
        first_chunk = i == nc - 1

        @pl.when(i == 0)
        def _():
            carry[...] = jnp.zeros_like(carry)
            head[...] = jnp.zeros_like(head)
            dl_ref[...] = jnp.zeros_like(dl_ref)
            dcw_ref[...] = jnp.zeros_like(dcw_ref)

        us = z_ref[:, 3 * CONV_W:4 * CONV_W]
        dy = dys_ref[...]
        dy_bf = dy.astype(BF16)
        us_ref[0] = us.astype(BF16)
        dyb_ref[0] = dy_bf

        _block_expand(dy_bf, cc_ref, work)
        _scan_chunk(work, lt_ref, carry, reverse=True)
        gg = work[...]
        gg_bf = gg.astype(BF16)
        g_ref[0] = gg_bf
        dus = d_ref[...] * dy + _block_contract(gg_bf, bb_ref)

        hcur = hs_ref[...].astype(F32)
        hlast = hp_ref[...].astype(F32)[hb - 1:hb, :]
        hlast = jnp.where(first_chunk, 0.0, hlast)
        rid = _row_ids(CHUNK, 2 * N_STATE)
        hprev = jnp.where(rid == 0, jnp.broadcast_to(hlast, (CHUNK, 2 * N_STATE)), pltpu.roll(hcur, 1, 0))
        gr, gi = gg[:, :N_STATE], gg[:, N_STATE:]
        hr, hi = hprev[:, :N_STATE], hprev[:, N_STATE:]
        dl_ref[:, :N_STATE] += (gr * hr + gi * hi).reshape(CHUNK // 8, 8, N_STATE).sum(axis=0)
        dl_ref[:, N_STATE:] += (gi * hr - gr * hi).reshape(CHUNK // 8, 8, N_STATE).sum(axis=0)

        @pl.when(i == nc - 1)
        def _():
            dl_ref[0:1, :] = jnp.sum(dl_ref[...], axis=0, keepdims=True)

        zb = z_ref[:, 0:CONV_W]
        zc = z_ref[:, CONV_W:2 * CONV_W]
        zv = z_ref[:, 2 * CONV_W:3 * CONV_W]
        v = zc * zv
        vtail = jnp.where(first_chunk, 0.0, zp_ref[:, CONV_W:2 * CONV_W] * zp_ref[:, 2 * CONV_W:3 * CONV_W])
        v1, v2 = _conv_taps(v, vtail)
        w0, w1, w2 = cw_ref[0:1, :], cw_ref[1:2, :], cw_ref[2:3, :]
        y = w0 * v2 + w1 * v1 + w2 * v
        dya_v = dya_ref[...]
        dzb = dya_v * (y + cb_ref[...])
        dyc = dya_v * zb
        hd = head[...]
        rc = _row_ids(CHUNK, CONV_W)
        n1 = jnp.where(rc == CHUNK - 1, _bcast_row(hd, 0, CHUNK), pltpu.roll(dyc, CHUNK - 1, 0))
        n2 = jnp.where(rc == CHUNK - 1, _bcast_row(hd, 1, CHUNK),
                       jnp.where(rc == CHUNK - 2, _bcast_row(hd, 0, CHUNK), pltpu.roll(dyc, CHUNK - 2, 0)))
        head[...] = dyc[0:8, :]
        dv = w2 * dyc + w1 * n1 + w0 * n2
        dz_ref[:, 0:CONV_W] = dzb.astype(BF16)
        dz_ref[:, CONV_W:2 * CONV_W] = (dv * zv).astype(BF16)
        dz_ref[:, 2 * CONV_W:3 * CONV_W] = (dv * zc).astype(BF16)
        dz_ref[:, 3 * CONV_W:4 * CONV_W] = dus.astype(BF16)
        dcw_ref[0:1, :] += jnp.sum(dyc * v2, axis=0, keepdims=True)
        dcw_ref[1:2, :] += jnp.sum(dyc * v1, axis=0, keepdims=True)
        dcw_ref[2:3, :] += jnp.sum(dyc * v, axis=0, keepdims=True)
        dcw_ref[3:4, :] += jnp.sum(dyc, axis=0, keepdims=True)
        dcw_ref[4:5, :] += jnp.sum(dy * us, axis=0, keepdims=True)

    rev = lambda i: nc - 1 - i
    return _out_hbm(pl.pallas_call(
        body, name="s5conv_bwd",
        grid=(nc,),
        in_specs=[pl.BlockSpec((CHUNK, IN_COLS), lambda i: (rev(i), 0)),
                  pl.BlockSpec((8, IN_COLS), lambda i: (jnp.maximum(rev(i) * (CHUNK // 8) - 1, 0), 0)),
                  pl.BlockSpec((CHUNK, 2 * N_STATE), lambda i: (rev(i), 0)),
                  pl.BlockSpec((hb, 2 * N_STATE), lambda i: (jnp.maximum(rev(i) * (CHUNK // hb) - 1, 0), 0)),
                  pl.BlockSpec((CHUNK, CONV_W), lambda i: (rev(i), 0)),
                  pl.BlockSpec((CHUNK, SSM_W), lambda i: (rev(i), 0)),
                  pl.BlockSpec((3, CONV_W), lambda i: (0, 0)),
                  pl.BlockSpec((1, CONV_W), lambda i: (0, 0)),
                  pl.BlockSpec((SSM_W, 2 * N_STATE), lambda i: (0, 0)),
                  pl.BlockSpec((SSM_W, 2 * N_STATE), lambda i: (0, 0)),
                  pl.BlockSpec((1, SSM_W), lambda i: (0, 0)),
                  pl.BlockSpec((32, 2 * N_STATE), lambda i: (0, 0))],
        out_specs=[pl.BlockSpec((CHUNK, IN_COLS), lambda i: (rev(i), 0)),
                   pl.BlockSpec((1, CHUNK, 2 * N_STATE), lambda i: (0, rev(i), 0)),
                   pl.BlockSpec((1, CHUNK, SSM_W), lambda i: (0, rev(i), 0)),
                   pl.BlockSpec((1, CHUNK, SSM_W), lambda i: (0, rev(i), 0)),
                   pl.BlockSpec((8, 2 * N_STATE), lambda i: (0, 0)),
                   pl.BlockSpec((8, CONV_W), lambda i: (0, 0))],
        out_shape=[jax.ShapeDtypeStruct((SEQ, IN_COLS), BF16),
                   pltpu.HBM((1, SEQ, 2 * N_STATE), BF16),
                   pltpu.HBM((1, SEQ, SSM_W), BF16),
                   pltpu.HBM((1, SEQ, SSM_W), BF16),
                   jax.ShapeDtypeStruct((8, 2 * N_STATE), F32),
                   jax.ShapeDtypeStruct((8, CONV_W), F32)],
        scratch_shapes=[pltpu.VMEM((CHUNK, 2 * N_STATE), F32),
                        pltpu.VMEM((8, 2 * N_STATE), F32),
                        pltpu.VMEM((8, CONV_W), F32)],
        compiler_params=_cparams(("arbitrary",)),
    )(*_in_hbm(z, z, hs, hs, dya, dys, convw, convb, bbmat, ccmat, dvec, ltab_rev)), (1, 2, 3))


def _mix_out_fwd(h, ya, ys, gluw, glub, con, son, wout):
    tm = TOK_TILE

    def body(h_ref, ya_ref, ys_ref, gw_ref, gb_ref, con_ref, son_ref, wo_ref, o_ref):
        zg, _ = _gelu(ys_ref[...])
        q = _dot(zg.astype(BF16), gw_ref[...]) + gb_ref[...]
        out_s = zg * _sigmoid(q)
        na, _ = _rms_stats(ya_ref[...])
        ns, _ = _rms_stats(out_s)
        o_ref[...] = (h_ref[...]
                      + _dot((na * con_ref[...]).astype(BF16), wo_ref[0:CONV_W, :])
                      + _dot((ns * son_ref[...]).astype(BF16), wo_ref[CONV_W:2 * CONV_W, :]))

    row = lambda m: (m, 0)
    fixed = lambda m: (0, 0)
    return pl.pallas_call(
        body, name="mix_out_fwd",
        grid=(SEQ // tm,),
        in_specs=[pl.BlockSpec((tm, D_MODEL), row), pl.BlockSpec((tm, CONV_W), row), pl.BlockSpec((tm, SSM_W), row),
                  pl.BlockSpec((SSM_W, SSM_W), fixed), pl.BlockSpec((1, SSM_W), fixed),
                  pl.BlockSpec((1, CONV_W), fixed), pl.BlockSpec((1, SSM_W), fixed),
                  pl.BlockSpec((None, D_MODEL, D_MODEL), lambda m: (0, 0, 0))],
        out_specs=pl.BlockSpec((tm, D_MODEL), row),
        out_shape=jax.ShapeDtypeStruct((SEQ, D_MODEL), F32),
        compiler_params=_cparams(("parallel",)),
    )(*_in_hbm(h, ya, ys, gluw, glub, con, son, wout))


def _mix_out_bwd(dh, ya, ys, gluw, glub, con, son, wout):
    tm = TOK_TILE

    def body(dh_ref, ya_ref, ys_ref, gw_ref, gb_ref, con_ref, son_ref, wo_ref,
             dya_ref, dys_ref, yc_ref, dhb_ref, zg_ref, dq_ref, part_ref):
        ysv = ys_ref[...]
        zg, th = _gelu(ysv)
        zg_bf = zg.astype(BF16)
        s = _sigmoid(_dot(zg_bf, gw_ref[...]) + gb_ref[...])
        out_s = zg * s
        na, ra = _rms_stats(ya_ref[...])
        ns, rs = _rms_stats(out_s)
        dh_bf = dh_ref[...].astype(BF16)
        yc_ref[0, :, 0:CONV_W] = (na * con_ref[...]).astype(BF16)
        yc_ref[0, :, CONV_W:2 * CONV_W] = (ns * son_ref[...]).astype(BF16)
        dhb_ref[0] = dh_bf
        dca = _dot_nt(dh_bf, wo_ref[0:CONV_W, :])
        dcs = _dot_nt(dh_bf, wo_ref[CONV_W:2 * CONV_W, :])
        dya, dcon = _rms_bwd(dca, na, ra, con_ref[...])
        dos, dson = _rms_bwd(dcs, ns, rs, son_ref[...])
        dya_ref[...] = dya
        dq = dos * zg * s * (1.0 - s)
        dq_bf = dq.astype(BF16)
        dzg = dos * s + _dot_nt(dq_bf, gw_ref[...])
        dys_ref[...] = dzg * _gelu_grad(ysv, th)
        zg_ref[0] = zg_bf
        dq_ref[0] = dq_bf
        rid = _row_ids(SUBLANES, SSM_W)
        part = jnp.zeros((SUBLANES, SSM_W), F32)
        for i, rowv in enumerate((dcon, dson, jnp.sum(dq, axis=0, keepdims=True))):
            part = jnp.where(rid == i, jnp.broadcast_to(rowv, (SUBLANES, SSM_W)), part)
        _accumulate(part_ref, pl.program_id(0) == 0, part)

    row = lambda m: (m, 0)
    fixed = lambda m: (0, 0)
    lead = lambda m: (0, m, 0)
    return _out_hbm(pl.pallas_call(
        body, name="mix_out_bwd",
        grid=(SEQ // tm,),
        in_specs=[pl.BlockSpec((tm, D_MODEL), row), pl.BlockSpec((tm, CONV_W), row), pl.BlockSpec((tm, SSM_W), row),
                  pl.BlockSpec((SSM_W, SSM_W), fixed), pl.BlockSpec((1, SSM_W), fixed),
                  pl.BlockSpec((1, CONV_W), fixed), pl.BlockSpec((1, SSM_W), fixed),
                  pl.BlockSpec((None, D_MODEL, D_MODEL), lambda m: (0, 0, 0))],
        out_specs=[pl.BlockSpec((tm, CONV_W), row), pl.BlockSpec((tm, SSM_W), row),
                   pl.BlockSpec((1, tm, D_MODEL), lead), pl.BlockSpec((1, tm, D_MODEL), lead),
                   pl.BlockSpec((1, tm, SSM_W), lead), pl.BlockSpec((1, tm, SSM_W), lead),
                   pl.BlockSpec((8, SSM_W), fixed)],
        out_shape=[pltpu.HBM((SEQ, CONV_W), F32), pltpu.HBM((SEQ, SSM_W), F32),
                   pltpu.HBM((1, SEQ, D_MODEL), BF16), pltpu.HBM((1, SEQ, D_MODEL), BF16),
                   pltpu.HBM((1, SEQ, SSM_W), BF16), pltpu.HBM((1, SEQ, SSM_W), BF16),
                   jax.ShapeDtypeStruct((8, SSM_W), F32)],
        compiler_params=_cparams(("arbitrary",)),
    )(*_in_hbm(dh, ya, ys, gluw, glub, con, son, wout)), (0, 1, 2, 3, 4, 5))


def _ple_fwd(h, g, p, wgate, wprojt):
    tm = TOK_TILE

    def body(h_ref, g_ref, p_ref, wg_ref, wp_ref, o_ref):
        x = h_ref[...]
        xh, _ = _rms_stats(x)
        s = _sigmoid(_dot((xh * g_ref[...]).astype(BF16), wg_ref[...]))
        o_ref[...] = x + _dot_nt(p_ref[...].astype(BF16), wp_ref[...]) * s

    row = lambda m: (m, 0)
    fixed = lambda m: (0, 0)
    return pl.pallas_call(
        body, name="ple_fwd",
        grid=(SEQ // tm,),
        in_specs=[pl.BlockSpec((tm, D_MODEL), row), pl.BlockSpec((1, D_MODEL), fixed), pl.BlockSpec((tm, PLE_DIM), row),
                  pl.BlockSpec((None, D_MODEL, D_MODEL), lambda m: (0, 0, 0)), pl.BlockSpec((D_MODEL, PLE_DIM), fixed)],
        out_specs=pl.BlockSpec((tm, D_MODEL), row),
        out_shape=jax.ShapeDtypeStruct((SEQ, D_MODEL), F32),
        compiler_params=_cparams(("parallel",)),
    )(*_in_hbm(h, g, p, wgate, wprojt))


def _ple_bwd(h, g, p, dh, wgate, wprojt):
    tm = TOK_TILE

    def body(h_ref, g_ref, p_ref, dh_ref, wg_ref, wp_ref, o_ref, u_ref, dq_ref, dpp_ref, pb_ref, dg_ref):
        xh, r = _rms_stats(h_ref[...])
        u = (xh * g_ref[...]).astype(BF16)
        s = _sigmoid(_dot(u, wg_ref[...]))
        p_bf = p_ref[...].astype(BF16)
        pp = _dot_nt(p_bf, wp_ref[...])
        dhv = dh_ref[...]
        dq = (dhv * pp * s * (1.0 - s)).astype(BF16)
        u_ref[0] = u
        dq_ref[0] = dq
        dpp_ref[0] = (dhv * s).astype(BF16)
        pb_ref[0] = p_bf
        dx, dg = _rms_bwd(_dot_nt(dq, wg_ref[...]), xh, r, g_ref[...])
        o_ref[...] = dhv + dx
        _accumulate(dg_ref, pl.program_id(0) == 0, dg)

    row = lambda m: (m, 0)
    fixed = lambda m: (0, 0)
    lead = lambda m: (0, m, 0)
    big = pltpu.HBM((1, SEQ, D_MODEL), BF16)
    return _out_hbm(pl.pallas_call(
        body, name="ple_bwd",
        grid=(SEQ // tm,),
        in_specs=[pl.BlockSpec((tm, D_MODEL), row), pl.BlockSpec((1, D_MODEL), fixed), pl.BlockSpec((tm, PLE_DIM), row),
                  pl.BlockSpec((tm, D_MODEL), row),
                  pl.BlockSpec((None, D_MODEL, D_MODEL), lambda m: (0, 0, 0)), pl.BlockSpec((D_MODEL, PLE_DIM), fixed)],
        out_specs=[pl.BlockSpec((tm, D_MODEL), row),
                   pl.BlockSpec((1, tm, D_MODEL), lead), pl.BlockSpec((1, tm, D_MODEL), lead),
                   pl.BlockSpec((1, tm, D_MODEL), lead), pl.BlockSpec((1, tm, PLE_DIM), lead),
                   pl.BlockSpec((1, D_MODEL), fixed)],
        out_shape=[jax.ShapeDtypeStruct((SEQ, D_MODEL), F32), big, big, big,
                   pltpu.HBM((1, SEQ, PLE_DIM), BF16),
                   jax.ShapeDtypeStruct((1, D_MODEL), F32)],
        compiler_params=_cparams(("arbitrary",)),
    )(*_in_hbm(h, g, p, dh, wgate, wprojt)), (1, 2, 3, 4))


def _final_loss(h, g, target):
    tm = TOK_TILE

    def body(h_ref, g_ref, t_ref, loss_ref, dh_ref, dg_ref):
        first = pl.program_id(0) == 0
        xh, r = _rms_stats(h_ref[...])
        diff = xh * g_ref[...] - t_ref[...]
        part = 0.5 * jnp.sum(jnp.mean(diff * diff, axis=-1, keepdims=True), axis=0, keepdims=True)
        _accumulate(loss_ref, first, jnp.broadcast_to(part, (SUBLANES, LANES)))
        dx, dg = _rms_bwd(diff * (1.0 / D_MODEL), xh, r, g_ref[...])
        dh_ref[...] = dx
        _accumulate(dg_ref, first, dg)

    row = lambda m: (m, 0)
    fixed = lambda m: (0, 0)
    return pl.pallas_call(
        body, name="final_loss",
        grid=(SEQ // tm,),
        in_specs=[pl.BlockSpec((tm, D_MODEL), row), pl.BlockSpec((1, D_MODEL), fixed),
                  pl.BlockSpec((tm, D_MODEL), row)],
        out_specs=[pl.BlockSpec((SUBLANES, LANES), fixed),
                   pl.BlockSpec((tm, D_MODEL), row),
                   pl.BlockSpec((1, D_MODEL), fixed)],
        out_shape=[jax.ShapeDtypeStruct((SUBLANES, LANES), F32),
                   jax.ShapeDtypeStruct((SEQ, D_MODEL), F32),
                   jax.ShapeDtypeStruct((1, D_MODEL), F32)],
        compiler_params=_cparams(("arbitrary",)),
    )(*_in_hbm(h, g, target))


def _disc(ar, ai, ldt):
    dt = jnp.exp(ldt)
    mag = jnp.exp(ar * dt)
    ph = ai * dt
    lr, li = mag * jnp.cos(ph), mag * jnp.sin(ph)
    nr, ni = lr - 1.0, li
    den = ar * ar + ai * ai
    return lr, li, (nr * ar + ni * ai) / den, (ni * ar - nr * ai) / den


def _s5_disc(a_row, ldt_row, a_rep, ldt_rep, bt, ct, tile_e, mask):
    n = N_STATE

    def body(ar_ref, lr_ref, ap_ref, lp_ref, b_ref, c_ref, e_ref, m_ref, lt_ref, ltr_ref, bb_ref, cc_ref):
        lr, li, _, _ = _disc(ar_ref[0], ar_ref[1], lr_ref[...])
        pr, pi = lr, li
        rid = _row_ids(SUBLANES, n)
        for k in range(1, 9):
            for ref, sgn, edge in ((lt_ref, 1.0, 24 + k - 1), (ltr_ref, -1.0, 24 + 8 - k)):
                if k in (1, 2, 4):
                    r0 = {1: 0, 2: 8, 4: 16}[k]
                    keep = (rid >= k) if ref is lt_ref else (rid < SUBLANES - k)
                    ref[r0:r0 + 8, 0:n] = jnp.where(keep, jnp.broadcast_to(pr, (8, n)), 0.0)
                    ref[r0:r0 + 8, n:2 * n] = jnp.where(keep, jnp.broadcast_to(sgn * pi, (8, n)), 0.0)
                ref[edge:edge + 1, 0:n] = pr
                ref[edge:edge + 1, n:2 * n] = sgn * pi
            pr, pi = pr * lr - pi * li, pr * li + pi * lr
        _, _, fr, fi = _disc(ap_ref[0], ap_ref[1], lp_ref[...])
        br, bi = b_ref[0], b_ref[1]
        e = e_ref[...]
        m = m_ref[...].astype(F32)
        bb_ref[:, 0:n] = (_dot((fr * br - fi * bi).astype(BF16), e) * m).astype(BF16)
        bb_ref[:, n:2 * n] = (_dot((fr * bi + fi * br).astype(BF16), e) * m).astype(BF16)
        cc_ref[:, 0:n] = (_dot(c_ref[0].astype(BF16), e) * m).astype(BF16)
        cc_ref[:, n:2 * n] = (-(_dot(c_ref[1].astype(BF16), e) * m)).astype(BF16)

    return pl.pallas_call(
        body, name="s5_disc",
        out_shape=[jax.ShapeDtypeStruct((32, 2 * n), F32), jax.ShapeDtypeStruct((32, 2 * n), F32),
                   jax.ShapeDtypeStruct((SSM_W, 2 * n), BF16), jax.ShapeDtypeStruct((SSM_W, 2 * n), BF16)],
        compiler_params=_cparams(None),
    )(a_row, ldt_row, a_rep, ldt_rep, bt, ct, tile_e, mask)


def _dot_exact(x, sel):
    hi = x.astype(BF16)
    r1 = x - hi.astype(F32)
    mid = r1.astype(BF16)
    lo = (r1 - mid.astype(F32)).astype(BF16)
    return _dot(hi, sel) + _dot(mid, sel) + _dot(lo, sel)


def _s5_disc_bwd(a, ldt, a_rep, ldt_rep, bt, mask, dl, d_bb, d_cc, fold):
    n = N_STATE

    def body(a_ref, l_ref, ap_ref, lp_ref, b_ref, m_ref, dl_ref, dbb_ref, dcc_ref, f_ref,
             da_ref, dldt_ref, db_ref, dc_ref):
        m = m_ref[...].astype(F32)
        fold_m = f_ref[...]
        diag = lambda x: _dot_exact(jnp.where(m > 0.0, x, 0.0), fold_m)
        dr, di = diag(dbb_ref[:, 0:n]), diag(dbb_ref[:, n:2 * n])
        dc_ref[0] = diag(dcc_ref[:, 0:n])
        dc_ref[1] = -diag(dcc_ref[:, n:2 * n])
        _, _, fr, fi = _disc(ap_ref[0], ap_ref[1], lp_ref[...])
        br, bi = b_ref[0], b_ref[1]
        db_ref[0] = fr * dr + fi * di
        db_ref[1] = fr * di - fi * dr
        per_state = lambda x: x.reshape(SSM_GROUPS, SSM_GROUP, SSM_STATE).sum(axis=1)
        dfr = per_state(dr * br + di * bi)
        dfi = per_state(di * br - dr * bi)
        _, vjp = jax.vjp(_disc, a_ref[0], a_ref[1], l_ref[...])
        dar, dai, dldt = vjp((dl_ref[0], dl_ref[1], dfr, dfi))
        da_ref[0] = dar
        da_ref[1] = dai
        dldt_ref[...] = jnp.sum(dldt, axis=1, keepdims=True)

    return pl.pallas_call(
        body, name="s5_disc_bwd",
        out_shape=[jax.ShapeDtypeStruct((2, SSM_GROUPS, SSM_STATE), F32),
                   jax.ShapeDtypeStruct((SSM_GROUPS, 1), F32),
                   jax.ShapeDtypeStruct((2, SSM_W, SSM_STATE), F32),
                   jax.ShapeDtypeStruct((2, SSM_W, SSM_STATE), F32)],
        compiler_params=_cparams(None),
    )(a, ldt, a_rep, ldt_rep, bt, mask, dl, d_bb, d_cc, fold)


def _row_block(rows, cap=512):
    for bm in range(min(cap, rows), 0, -1):
        if rows % bm == 0 and (bm % 8 == 0 or bm == rows):
            return bm
    return rows


SUM_PARTS = 2


def _own_pieces(segs, rtot):
    pr = rtot // SUM_PARTS
    assert pr * SUM_PARTS == rtot and pr % 16 == 0
    offs = _seg_offsets(segs)
    pieces = [[] for _ in range(SUM_PARTS)]
    for a, (n, r) in enumerate(segs):
        for m in range(n):
            lo = offs[a] + m * r
            for h in range(SUM_PARTS):
                clo, chi = max(lo, h * pr), min(lo + r, (h + 1) * pr)
                if chi > clo:
                    pieces[h].append((a, m, clo - lo, clo - h * pr, chi - clo))
    return pieces


def _pair_rows(srcs, got_ref, segs, pieces, h, chip, own_v, got_v, sems):
    pr = own_v.shape[0]
    dev = 2 * chip + lax.axis_index("c")
    for hh in range(SUM_PARTS):
        @pl.when(h == hh)
        def _(hh=hh):
            cps = [pltpu.make_async_copy(got_ref.at[chip, pl.ds(hh * pr, pr), :], got_v, sems.at[0])]
            for i, (a, m, so, do, rows) in enumerate(pieces[hh]):
                start = pl.multiple_of(dev * segs[a][1] + so, 16)
                cps.append(pltpu.make_async_copy(srcs[a].at[m, pl.ds(start, rows), :],
                                                 own_v.at[pl.ds(do, rows), :], sems.at[1 + i]))
            for cp in cps:
                cp.start()
            for cp in cps:
                cp.wait()
    return own_v[...].astype(F32) + got_v[...].astype(F32)


def _pair_sum(fulls, got, segs):
    ns = len(segs)
    _, rtot, c = got.shape
    pieces = _own_pieces(segs, rtot)
    pr = rtot // SUM_PARTS

    def body(*refs):
        srcs = refs[:ns]
        got_ref, pbf_ref, own_v, got_v, sems = refs[ns:]
        x, y, _ = _mesh_pos()
        j = pl.program_id(1)
        chip = jnp.where(j == 0, 2 * (1 - x) + y, jnp.where(j == 1, 2 * x + 1 - y, 2 * (1 - x) + 1 - y))
        pbf_ref[0] = _pair_rows(srcs, got_ref, segs, pieces, pl.program_id(0), chip, own_v, got_v, sems).astype(BF16)

    return pl.pallas_call(
        body, name="pair_sum",
        grid=(SUM_PARTS, 3),
        in_specs=[HBM] * (ns + 1), out_specs=pl.BlockSpec((1, pr, c), lambda h, j: (j, h, 0)),
        out_shape=pltpu.HBM((3, rtot, c), BF16),
        scratch_shapes=[pltpu.VMEM((pr, c), BF16), pltpu.VMEM((pr, c), BF16),
                        pltpu.SemaphoreType.DMA((1 + max(len(p) for p in pieces),))],
        compiler_params=_cparams(("arbitrary", "arbitrary")),
    )(*_in_hbm(*fulls, got))


def _chip_sum(fulls, got, rb, segs, layer, into):
    ns = len(segs)
    _, rtot, c = got.shape
    pieces = _own_pieces(segs, rtot)
    pr = rtot // SUM_PARTS

    def body(*refs):
        srcs = refs[:ns]
        got_ref, r_ref = refs[ns], refs[ns + 1]
        s_ref, own_v, got_v, sems = refs[-4:]
        x, y, _ = _mesh_pos()
        own = _pair_rows(srcs, got_ref, segs, pieces, pl.program_id(0), 2 * x + y, own_v, got_v, sems)
        s_ref[0] = ((own + r_ref[0].astype(F32)) + r_ref[1].astype(F32)) + r_ref[2].astype(F32)

    old = [] if into is None else [into]
    return pl.pallas_call(
        body, name="chip_sum",
        grid=(SUM_PARTS,),
        in_specs=[HBM] * (ns + 1) + [pl.BlockSpec((3, pr, c), lambda h: (0, h, 0))] + [HBM] * len(old),
        out_specs=pl.BlockSpec((1, pr, c), lambda h: (layer, h, 0)),
        out_shape=jax.ShapeDtypeStruct((DEPTH, rtot, c), F32),
        input_output_aliases={ns + 2: 0} if old else {},
        scratch_shapes=[pltpu.VMEM((pr, c), BF16), pltpu.VMEM((pr, c), BF16),
                        pltpu.SemaphoreType.DMA((1 + max(len(p) for p in pieces),))],
        compiler_params=_cparams(("arbitrary",)),
    )(*_in_hbm(*fulls, got, rb), *old)


def _adamw(sets):
    ns = len(sets)
    r, c = sets[0][0].shape
    bm = _row_block(r)
    bc1 = 1.0 - ADAM_B1 ** ADAM_STEP
    bc2 = 1.0 - ADAM_B2 ** ADAM_STEP

    def body(*refs):
        for s in range(ns):
            w_ref, g_ref, m_ref, v_ref = refs[4 * s:4 * s + 4]
            d_ref, nm_ref, nv_ref = refs[4 * ns + 3 * s:4 * ns + 3 * s + 3]
            gv = g_ref[...]
            nm = ADAM_B1 * m_ref[...] + (1.0 - ADAM_B1) * gv
            nv = ADAM_B2 * v_ref[...] + (1.0 - ADAM_B2) * (gv * gv)
            nm_ref[...] = nm
            nv_ref[...] = nv
            d_ref[...] = -ADAM_LR * ((nm / bc1) / (jnp.sqrt(nv / bc2) + ADAM_EPS) + ADAM_WD * w_ref[...])

    spec = pl.BlockSpec((bm, c), lambda k: (k, 0))
    shp = jax.ShapeDtypeStruct((r, c), F32)
    flat = pl.pallas_call(
        body, name="adamw",
        grid=(r // bm,),
        in_specs=[spec] * (4 * ns), out_specs=[spec] * (3 * ns), out_shape=[shp] * (3 * ns),
        compiler_params=_cparams(("parallel",)),
    )(*_in_hbm(*[a for four in sets for a in four]))
    return [flat[3 * s:3 * s + 3] for s in range(ns)]


def _adamw_layers(sets, first, nl, prev):
    ns = len(sets)
    depth, r, c = sets[0][0].shape
    bm = _row_block(r, min(512, max(SUBLANES, (24 * MIB) // (ns * 8 * 2 * c * 4))))
    while any(four[4] is not None and four[4] % bm for four in sets):
        bm //= 2
    assert bm % SUBLANES == 0 and r % bm == 0
    bc1 = 1.0 - ADAM_B1 ** ADAM_STEP
    bc2 = 1.0 - ADAM_B2 ** ADAM_STEP

    def body(*refs):
        outs = refs[len(refs) - 4 * ns:]
        for s in range(ns):
            w_ref, m_ref, v_ref, g_ref = refs[4 * s:4 * s + 4]
            go_ref, d_ref, nm_ref, nv_ref = outs[4 * s:4 * s + 4]
            gv = g_ref[...]
            nm = ADAM_B1 * m_ref[...] + (1.0 - ADAM_B1) * gv
            nv = ADAM_B2 * v_ref[...] + (1.0 - ADAM_B2) * (gv * gv)
            go_ref[...] = gv
            nm_ref[...] = nm
            nv_ref[...] = nv
            d_ref[...] = -ADAM_LR * ((nm / bc1) / (jnp.sqrt(nv / bc2) + ADAM_EPS) + ADAM_WD * w_ref[...])

    at = pl.BlockSpec((1, bm, c), lambda i, k: (first + i, k, 0))

    def grad_spec(g_rows):
        if g_rows is None:
            return pl.BlockSpec((1, bm, c), lambda i, k: (i, k, 0))
        return pl.BlockSpec((1, bm, c), lambda i, k: (first + i, g_rows // bm + k, 0))

    shp = jax.ShapeDtypeStruct((depth, r, c), F32)
    old = [] if prev is None else [a for four in prev for a in four]
    flat = pl.pallas_call(
        body, name="adamw_layers",
        grid=(nl, r // bm),
        in_specs=[spec for four in sets for spec in (at, at, at, grad_spec(four[4]))] + [HBM] * len(old),
        out_specs=[at] * (4 * ns), out_shape=[shp] * (4 * ns),
        input_output_aliases={4 * ns + i: i for i in range(len(old))},
        compiler_params=_cparams(("parallel", "parallel")),
    )(*_in_hbm(*[a for four in sets for a in four[:4]]), *old)
    return [flat[4 * s:4 * s + 4] for s in range(ns)]


def _mesh_pos():
    return lax.axis_index("x"), lax.axis_index("y"), lax.axis_index("c")


def _dev_index(p):
    return 4 * p[0] + 2 * p[1] + p[2]


def _seg_offsets(segs):
    offs, o = [], 0
    for n, r in segs:
        offs.append(o)
        o += n * r
    return offs


def _remote(src, dst, send_sem, recv_sem, to):
    return pltpu.make_async_remote_copy(src_ref=src, dst_ref=dst, send_sem=send_sem, recv_sem=recv_sem,
                                        device_id=to, device_id_type=MESH)


def _allgather(pack, segs, name):
    rtot, c = pack.shape
    ns = len(segs)
    offs = _seg_offsets(segs)
    assert rtot == sum(n * r for n, r in segs)

    def body(pack_ref, *refs):
        outs = refs[:ns]
        send_sems, recv_sems, local_sem = refs[ns:]
        x, y, cc = _mesh_pos()
        me, sib = (x, y, cc), (x, y, 1 - cc)
        chips = [(1 - x, y), (x, 1 - y), (1 - x, 1 - y)]

        def pieces(dev, from_pack):
            res = []
            for a, (n, r) in enumerate(segs):
                for m in range(n):
                    dst = outs[a].at[m, pl.ds(pl.multiple_of(dev * r, r), r), :]
                    src = pack_ref.at[pl.ds(offs[a] + m * r, r), :] if from_pack else dst
                    res.append((src, dst))
            return res

        def push(k, dev, to, from_pack):
            for s, d in pieces(dev, from_pack):
                _remote(s, d, send_sems.at[k], recv_sems.at[k], to).start()

        def whole(k):
            return _remote(pack_ref, pack_ref, send_sems.at[k], recv_sems.at[k], me)

        my_dev = _dev_index(me)
        for s, d in pieces(my_dev, True):
            pltpu.make_async_copy(s, d, local_sem).start()
        push(0, my_dev, sib, True)
        for j, chip in enumerate(chips):
            push(1 + j, my_dev, (*chip, cc), True)
        for j, chip in enumerate(chips):
            whole(1 + j).wait_recv()
            push(4 + j, _dev_index((*chip, cc)), sib, False)
        whole(0).wait_recv()
        for j in range(3):
            whole(4 + j).wait_recv()
        for k in range(7):
            whole(k).wait_send()
        pltpu.make_async_copy(pack_ref, pack_ref, local_sem).wait()

    return pl.pallas_call(
        body, name=name,
        in_specs=[HBM], out_specs=[HBM] * ns,
        out_shape=[jax.ShapeDtypeStruct((n, N_DEV * r, c), pack.dtype) for n, r in segs],
        scratch_shapes=[pltpu.SemaphoreType.DMA((7,)), pltpu.SemaphoreType.DMA((7,)), pltpu.SemaphoreType.DMA],
    )(pack)


HBM = pl.BlockSpec(memory_space=pltpu.HBM)
SEM = pl.BlockSpec(memory_space=pltpu.SEMAPHORE)
VMEM_WHOLE = pl.BlockSpec(memory_space=pltpu.VMEM)
EFFECT = pltpu.SideEffectType.DATAFLOW_SIDE_EFFECTING


def _hbm(a):
    return pltpu.with_memory_space_constraint(a, pltpu.HBM)


def _ag_start(pack, segs, after, name):
    rtot, c = pack.shape
    ns = len(segs)
    offs = _seg_offsets(segs)

    def body(pack_ref, *refs):
        lands = refs[:ns]
        send_sems, recv_sems = refs[ns + 1], refs[ns + 2]
        token = refs[-1]
        x, y, cc = _mesh_pos()
        my_dev = _dev_index((x, y, cc))
        targets = [(x, y, 1 - cc), (1 - x, y, cc), (x, 1 - y, cc), (1 - x, 1 - y, cc)]
        for k, to in enumerate(targets):
            for a, (n, r) in enumerate(segs):
                for m in range(n):
                    _remote(pack_ref.at[pl.ds(offs[a] + m * r, r), :],
                            lands[a].at[m, pl.ds(pl.multiple_of(my_dev * r, r), r), :],
                            send_sems.at[k], recv_sems.at[k], to).start()
        token[...] = jnp.zeros_like(token)

    land_shapes = [(n, N_DEV * r, c) for n, r in segs]
    outs = pl.pallas_call(
        body, name=name,
        in_specs=[HBM] * (1 + ns) + [UNREAD],
        out_specs=[SEM, SEM, HBM] + [HBM] * ns + [VMEM_WHOLE],
        out_shape=[pltpu.SemaphoreType.DMA((4,)), pltpu.SemaphoreType.DMA((4,)), pltpu.HBM(pack.shape, pack.dtype)]
        + [pltpu.HBM(s, pack.dtype) for s in land_shapes] + [jax.ShapeDtypeStruct((SUBLANES, LANES), F32)],
        input_output_aliases={0: 2, **{1 + i: 3 + i for i in range(ns)}},
        compiler_params=pltpu.CompilerParams(has_side_effects=EFFECT),
    )(_hbm(pack), *[_hbm(lax.empty(s, pack.dtype)) for s in land_shapes], _hbm(after))
    return outs[0], outs[1], outs[2], list(outs[3:3 + ns]), outs[-1]


def _ag_wait(send_sems, recv_sems, pack, lands, after, name):
    ns = len(lands)

    def body(pack_ref, *refs):
        send_ref, recv_ref = refs[ns], refs[ns + 1]
        me = _mesh_pos()
        for k in range(4):
            whole = _remote(pack_ref, pack_ref, send_ref.at[k], recv_ref.at[k], me)
            whole.wait_send()
            whole.wait_recv()

    outs = pl.pallas_call(
        body, name=name,
        in_specs=[HBM] * (1 + ns) + [SEM, SEM, UNREAD],
        out_specs=[HBM] * (1 + ns),
        out_shape=[pltpu.HBM(pack.shape, pack.dtype)] + [pltpu.HBM(a.shape, a.dtype) for a in lands],
        input_output_aliases={i: i for i in range(1 + ns)},
        compiler_params=pltpu.CompilerParams(has_side_effects=EFFECT),
    )(pack, *lands, send_sems, recv_sems, _hbm(after))
    return outs[0], list(outs[1:])


def _ag_finish(pack, lands, segs):
    rtot, c = pack.shape
    ns = len(segs)
    offs = _seg_offsets(segs)

    def body(pack_ref, *refs):
        outs = refs[ns:2 * ns]
        stage, send_sems, recv_sems, local_sems = refs[2 * ns:]
        x, y, cc = _mesh_pos()
        me, sib = (x, y, cc), (x, y, 1 - cc)
        chips = [(1 - x, y), (x, 1 - y), (1 - x, 1 - y)]

        def rows(a, m, dev):
            return outs[a].at[m, pl.ds(pl.multiple_of(dev * segs[a][1], segs[a][1]), segs[a][1]), :]

        for j, chip in enumerate(chips):
            dev = _dev_index((*chip, cc))
            for a, (n, r) in enumerate(segs):
                for m in range(n):
                    _remote(rows(a, m, dev), rows(a, m, dev), send_sems.at[j], recv_sems.at[j], sib).start()
        load = pltpu.make_async_copy(pack_ref, stage, local_sems.at[0])
        load.start()
        load.wait()
        my_dev = _dev_index(me)
        for a, (n, r) in enumerate(segs):
            for m in range(n):
                pltpu.make_async_copy(stage.at[pl.ds(offs[a] + m * r, r), :], rows(a, m, my_dev), local_sems.at[1]).start()
        pltpu.make_async_copy(stage, pack_ref, local_sems.at[1]).wait()
        for j in range(3):
            _remote(pack_ref, pack_ref, send_sems.at[j], recv_sems.at[j], me).wait()

    outs = pl.pallas_call(
        body, name="ag_finish",
        in_specs=[HBM] * (1 + ns), out_specs=[HBM] * ns,
        out_shape=[pltpu.HBM(a.shape, a.dtype) if r >= 128 else jax.ShapeDtypeStruct(a.shape, a.dtype)
                   for a, (_, r) in zip(lands, segs)],
        input_output_aliases={1 + i: i for i in range(ns)},
        scratch_shapes=[pltpu.VMEM((rtot, c), pack.dtype), pltpu.SemaphoreType.DMA((3,)),
                        pltpu.SemaphoreType.DMA((3,)), pltpu.SemaphoreType.DMA((2,))],
        compiler_params=_cparams(None, 16),
    )(pack, *lands)
    return list(outs)


def _rs_chips_start(pbf, after, name):
    _, rtot, c = pbf.shape

    def body(pbf_ref, land_ref, after_ref, send_sems, recv_sems, pbf_thru, land_thru, token):
        x, y, cc = _mesh_pos()
        for j, (cx, cy) in enumerate([(1 - x, y), (x, 1 - y), (1 - x, 1 - y)]):
            _remote(pbf_ref.at[j], land_ref.at[j], send_sems.at[j], recv_sems.at[j], (cx, cy, cc)).start()
        token[...] = jnp.zeros_like(token)

    return pl.pallas_call(
        body, name=name,
        in_specs=[HBM, HBM, UNREAD],
        out_specs=[SEM, SEM, HBM, HBM, VMEM_WHOLE],
        out_shape=[pltpu.SemaphoreType.DMA((3,)), pltpu.SemaphoreType.DMA((3,)), pltpu.HBM(pbf.shape, pbf.dtype),
                   pltpu.HBM((3, rtot, c), pbf.dtype), jax.ShapeDtypeStruct((SUBLANES, LANES), F32)],
        input_output_aliases={0: 2, 1: 3},
        compiler_params=pltpu.CompilerParams(has_side_effects=EFFECT),
    )(_hbm(pbf), _hbm(lax.empty((3, rtot, c), pbf.dtype)), _hbm(after))


def _rs_chips_wait(send_sems, recv_sems, pbf, land, after, name):
    def body(pbf_ref, land_ref, send_ref, recv_ref, after_ref, pbf_out, land_out):
        me = _mesh_pos()
        for j in range(3):
            cp = _remote(pbf_ref.at[0], land_ref.at[j], send_ref.at[j], recv_ref.at[j], me)
            cp.wait_send()
            cp.wait_recv()

    return pl.pallas_call(
        body, name=name,
        in_specs=[HBM, HBM, SEM, SEM, UNREAD], out_specs=[HBM, HBM],
        out_shape=[pltpu.HBM(pbf.shape, pbf.dtype), pltpu.HBM(land.shape, land.dtype)],
        input_output_aliases={0: 0, 1: 1},
        compiler_params=pltpu.CompilerParams(has_side_effects=EFFECT),
    )(pbf, land, send_sems, recv_sems, _hbm(after))[1]


def _flips():
    return [(dx, dy, dc) for dx in (0, 1) for dy in (0, 1) for dc in (0, 1) if dx or dy or dc]


def _small_gather_start(flat, name):
    r, c = flat.shape

    def body(flat_ref, land_ref, send_sems, recv_sems, flat_thru, land_thru, token):
        x, y, cc = _mesh_pos()
        mine = land_ref.at[_dev_index((x, y, cc))]
        for k, (dx, dy, dc) in enumerate(_flips()):
            to = (1 - x if dx else x, 1 - y if dy else y, 1 - cc if dc else cc)
            _remote(flat_ref, mine, send_sems.at[k], recv_sems.at[k], to).start()
        token[...] = jnp.zeros_like(token)

    return pl.pallas_call(
        body, name=name,
        in_specs=[HBM, HBM],
        out_specs=[SEM, SEM, HBM, HBM, VMEM_WHOLE],
        out_shape=[pltpu.SemaphoreType.DMA((7,)), pltpu.SemaphoreType.DMA((7,)), pltpu.HBM(flat.shape, flat.dtype),
                   pltpu.HBM((N_DEV, r, c), flat.dtype), jax.ShapeDtypeStruct((SUBLANES, LANES), F32)],
        input_output_aliases={0: 2, 1: 3},
        compiler_params=pltpu.CompilerParams(has_side_effects=EFFECT),
    )(_hbm(flat), _hbm(lax.empty((N_DEV, r, c), flat.dtype)))


def _small_gather_wait(send_sems, recv_sems, flat, land, after, name):
    def body(flat_ref, land_ref, send_ref, recv_ref, after_ref, flat_out, land_out):
        me = _mesh_pos()
        for k in range(N_DEV - 1):
            cp = _remote(flat_ref, land_ref.at[0], send_ref.at[k], recv_ref.at[k], me)
            cp.wait_send()
            cp.wait_recv()

    return pl.pallas_call(
        body, name=name,
        in_specs=[HBM, HBM, SEM, SEM, UNREAD], out_specs=[HBM, HBM],
        out_shape=[pltpu.HBM(flat.shape, flat.dtype), pltpu.HBM(land.shape, land.dtype)],
        input_output_aliases={0: 0, 1: 1},
        compiler_params=pltpu.CompilerParams(has_side_effects=EFFECT),
    )(flat, land, send_sems, recv_sems, _hbm(after))


def _sum_devices(land, own):
    _, r, c = land.shape

    def body(land_ref, own_ref, out_ref):
        me = _dev_index(_mesh_pos())
        total = None
        for d in range(N_DEV):
            other = land_ref[jnp.where(d == me, (d + 1) % N_DEV, d)]
            block = jnp.where(d == me, own_ref[...], other)
            total = block if total is None else total + block
        out_ref[...] = total

    return pl.pallas_call(
        body, name="sum_devices",
        grid=(1,),
        in_specs=[pl.BlockSpec((N_DEV, r, c), lambda i: (0, 0, 0)), pl.BlockSpec((r, c), lambda i: (0, 0))],
        out_specs=pl.BlockSpec((r, c), lambda i: (0, 0)),
        out_shape=jax.ShapeDtypeStruct((r, c), F32),
        compiler_params=_cparams(("arbitrary",)),
    )(land, own)


def _rs_sibling_start(fulls, segs, name):
    ns = len(segs)
    offs = _seg_offsets(segs)
    rtot = sum(n * r for n, r in segs)
    c = fulls[0].shape[-1]
    dt = fulls[0].dtype

    def body(*refs):
        srcs = refs[:ns]
        land_ref, send_sem, recv_sem = refs[ns], refs[ns + 1], refs[ns + 2]
        token = refs[-1]
        x, y, cc = _mesh_pos()
        for k in range(4):
            for a, (n, r) in enumerate(segs):
                for m in range(n):
                    theirs = srcs[a].at[m, pl.ds(pl.multiple_of((2 * k + 1 - cc) * r, r), r), :]
                    _remote(theirs, land_ref.at[k, pl.ds(offs[a] + m * r, r), :], send_sem, recv_sem,
                            (x, y, 1 - cc)).start()
        token[...] = jnp.zeros_like(token)

    outs = pl.pallas_call(
        body, name=name,
        in_specs=[HBM] * (ns + 1),
        out_specs=[SEM, SEM] + [HBM] * (ns + 1) + [VMEM_WHOLE],
        out_shape=[pltpu.SemaphoreType.DMA(()), pltpu.SemaphoreType.DMA(())]
        + [pltpu.HBM(a.shape, a.dtype) for a in fulls] + [pltpu.HBM((4, rtot, c), dt),
                                                           jax.ShapeDtypeStruct((SUBLANES, LANES), F32)],
        input_output_aliases={i: 2 + i for i in range(ns + 1)},
        compiler_params=pltpu.CompilerParams(has_side_effects=EFFECT),
    )(*[_hbm(a) for a in fulls], _hbm(lax.empty((4, rtot, c), dt)))
    return outs[0], outs[1], list(outs[2:2 + ns]), outs[2 + ns], outs[-1]


def _rs_sibling_wait(send_sem, recv_sem, fulls, land, after, name):
    ns = len(fulls)

    def body(*refs):
        land_ref, send_ref, recv_ref = refs[ns], refs[ns + 1], refs[ns + 2]
        whole = _remote(land_ref, land_ref, send_ref, recv_ref, _mesh_pos())
        whole.wait_send()
        whole.wait_recv()

    outs = pl.pallas_call(
        body, name=name,
        in_specs=[HBM] * (ns + 1) + [SEM, SEM, UNREAD], out_specs=[HBM] * (ns + 1),
        out_shape=[pltpu.HBM(a.shape, a.dtype) for a in fulls] + [pltpu.HBM(land.shape, land.dtype)],
        input_output_aliases={i: i for i in range(ns + 1)},
        compiler_params=pltpu.CompilerParams(has_side_effects=EFFECT),
    )(*fulls, land, send_sem, recv_sem, _hbm(after))
    return list(outs[:ns]), outs[ns]


def _tp(w):
    return jnp.swapaxes(w, -1, -2)


def _s5_prepare(a_re, a_im, log_dt, b_re, b_im, c_re, c_im):
    a = jnp.stack([a_re, a_im], axis=1)
    ldt = jnp.broadcast_to(log_dt[:, :, None], (DEPTH, SSM_GROUPS, SSM_STATE))
    a_row = a.reshape(DEPTH, 2, 1, N_STATE)
    ldt_row = ldt.reshape(DEPTH, 1, N_STATE)
    a_rep = jnp.repeat(a, SSM_GROUP, axis=2)
    ldt_rep = jnp.repeat(ldt, SSM_GROUP, axis=1)
    bt = jnp.stack([_tp(b_re), _tp(b_im)], axis=1).reshape(DEPTH, 2, SSM_W, SSM_STATE)
    ct = jnp.stack([c_re, c_im], axis=1).reshape(DEPTH, 2, SSM_W, SSM_STATE)
    tile_e = jnp.tile(jnp.eye(SSM_STATE, dtype=BF16), (1, SSM_GROUPS))
    mask = jnp.repeat(jnp.repeat(jnp.eye(SSM_GROUPS, dtype=BF16), SSM_GROUP, axis=0), SSM_STATE, axis=1)
    out = []
    for l in range(DEPTH):
        tabs = _s5_disc(a_row[l], ldt_row[l], a_rep[l], ldt_rep[l], bt[l], ct[l], tile_e, mask)
        out.append(((a[l], ldt[l], a_rep[l], ldt_rep[l], bt[l], mask), *tabs))
    return out


def _layer_fwd(h, p_l, small, big, arrive=None):
    saved = {'h0': h}
    if arrive is not None:
        arrive(0, h)
    h, saved['gu1'] = _ffn_fwd(h, small['ffn1_norm'], big['ff1'])
    saved['h1'] = h
    if arrive is not None:
        arrive(1, h)
    z = _inproj_fwd(h, small['mix_norm'], big['wint'])
    ya, ys, hs = _s5conv_fwd(z, small['conv_w'], small['conv_b'], small['bbmat'], small['ccmat'], small['dvec'],
                             small['ltab'])
    saved.update(z=z, ya=ya, ys=ys, hs=hs)
    h = _mix_out_fwd(h, ya, ys, big['glu'], small['glu_b'], small['conv_out_norm'], small['ssm_out_norm'], big['wout'])
    saved['h2'] = h
    if arrive is not None:
        arrive(2, h)
    h, saved['gu2'] = _ffn_fwd(h, small['ffn2_norm'], big['ff2'])
    saved['h3'] = h
    h = _ple_fwd(h, small['ple_norm'], p_l, big['plg'], big['plpt'])
    return h, saved


def _ffn_bwd(h_in, g, dh, gu, w3):
    dh_in, dga, ud, dg = _ffn_bwd_act(h_in, g, dh, gu, w3)
    return dh_in, _matmul_tn(dga, ud, FF_BLOCK, BF16, "ffn_wgrad"), dg


def _layer_bwd_top(dh, p_l, small, big, saved):
    gs = {}
    dh, u, dq, dpp, pb, gs['ple_norm'] = _ple_bwd(saved['h3'], small['ple_norm'], p_l, dh, big['plg'], big['plpt'])
    d_plg = _matmul_tn(u, dq, 256, BF16, "ple_gate_wgrad")
    d_plpt = _matmul_tn(dpp, pb, 256, BF16, "ple_proj_wgrad", to_kernel=False)
    dh, d_ff2, gs['ffn2_norm'] = _ffn_bwd(saved['h2'], small['ffn2_norm'], dh, saved['gu2'], big['ff2'])
    return dh, (gs, d_plg, d_plpt, d_ff2)


def _layer_bwd_rest(dh, top, small, big, saved):
    gs, d_plg, d_plpt, d_ff2 = top
    dya, dys, ycat, dhb, zg, dq, part = _mix_out_bwd(dh, saved['ya'], saved['ys'], big['glu'], small['glu_b'],
                                                     small['conv_out_norm'], small['ssm_out_norm'], big['wout'])
    d_wout = _matmul_tn(ycat, dhb, 256, BF16, "w_out_wgrad")
    d_glu = _matmul_tn(zg, dq, 256, BF16, "glu_wgrad", to_kernel=False)
    dz, gadj, us, dyb, dl, dcw = _s5conv_bwd(saved['z'], saved['hs'], dya, dys, small['conv_w'], small['conv_b'],
                                             small['bbmat'], small['ccmat'], small['dvec'], small['ltab_rev'])
    d_bb = _block_wgrad(us, gadj, "s5_b_wgrad")
    d_cc = _block_wgrad(dyb, saved['hs'][None], "s5_c_wgrad")
    dh, u, gs['mix_norm'] = _inproj_bwd(saved['h1'], small['mix_norm'], dh, dz, big['wint'])
    d_wint = _matmul_tn(dz[None], u, 256, BF16, "w_in_wgrad")
    dh, d_ff1, gs['ffn1_norm'] = _ffn_bwd(saved['h0'], small['ffn1_norm'], dh, saved['gu1'], big['ff1'])

    dlb = dl[0].reshape(2, SSM_GROUPS, SSM_STATE)
    fold = jnp.tile(jnp.eye(SSM_STATE, dtype=BF16), (SSM_GROUPS, 1))
    da, dldt, dbt, dct = _s5_disc_bwd(*small['disc_in'], dlb, d_bb, d_cc, fold)
    gs['ssm_A_re'], gs['ssm_A_im'] = da[0], da[1]
    gs['ssm_log_dt'] = dldt[:, 0]
    ghp = (SSM_GROUPS, SSM_GROUP, SSM_STATE)
    gs['ssm_B_re'], gs['ssm_B_im'] = dbt[0].reshape(ghp), dbt[1].reshape(ghp)
    gs['ssm_C_re'], gs['ssm_C_im'] = dct[0].reshape(ghp), dct[1].reshape(ghp)
    gs['conv_w'] = dcw[0:3]
    gs['conv_b'] = dcw[3]
    gs['ssm_D'] = dcw[4].reshape(SSM_GROUPS, SSM_GROUP)
    gs['conv_out_norm'], gs['ssm_out_norm'], gs['glu_b'] = part[0], part[1], part[2]
    for n in ('ple_norm', 'ffn2_norm', 'mix_norm', 'ffn1_norm'):
        gs[n] = gs[n][0]
    fulls = [d_ff1, d_ff2, d_wint, d_wout, d_plg,
             d_plpt.reshape(1, D_MODEL * PLE_DIM // D_MODEL, D_MODEL), d_glu.reshape(1, SSM_W * SSM_W // D_MODEL, D_MODEL)]
    return dh, fulls, gs


VIEW_T = ('ffn1_w_gate', 'ffn1_w_up', 'ffn2_w_gate', 'ffn2_w_up', 'ssm_B_re', 'ssm_B_im')


def _view(name, a):
    return _tp(a) if name in VIEW_T else a


SEG_NAMES = ('ff1', 'ff2', 'wint', 'wout', 'plg', 'plpt', 'glu')
FIRST_LAYER_GROUPS = ((0,), (2, 3, 6), (1, 4, 5))


def _layer_pack(W, l, segments=range(len(SEGS))):
    pieces = {
        0: lambda: [_tp(W['ffn1_w_gate'][l]), _tp(W['ffn1_w_up'][l]), W['ffn1_w_down'][l]],
        1: lambda: [_tp(W['ffn2_w_gate'][l]), _tp(W['ffn2_w_up'][l]), W['ffn2_w_down'][l]],
        2: lambda: [_tp(W['w_in'][l])],
        3: lambda: [W['w_out'][l]],
        4: lambda: [W['ple_w_gate'][l]],
        5: lambda: [_tp(W['ple_w_proj'][l]).reshape(-1, D_MODEL)],
        6: lambda: [W['glu_w'][l].reshape(-1, D_MODEL)],
    }
    return jnp.concatenate([a for s in segments for a in pieces[s]()], axis=0).astype(BF16)


def _as_big(named):
    shape = dict(plpt=(D_MODEL, PLE_DIM), glu=(SSM_W, SSM_W))
    return {n: (a.reshape(shape[n]) if n in shape else a) for n, a in named.items()}


def _pad_rows(flat, mult, width=LANES):
    per = mult * width
    n = flat.shape[0]
    tot = -(-n // per) * per
    return jnp.pad(flat, (0, tot - n)).reshape(tot // width, width)


def _adamw_any(names, w, g, m, v):
    two = lambda t: t.reshape(-1, t.shape[-1])
    groups = {}
    for n in names:
        groups.setdefault(two(w[n]).shape, []).append(n)
    out = ({}, {}, {})
    for ns in groups.values():
        done = _adamw([(two(w[n]), two(g[n]), two(m[n]), two(v[n])) for n in ns])
        for n, three in zip(ns, done):
            for dst, t in zip(out, three):
                dst[n] = t.reshape(w[n].shape)
    return out


def kernel(x, p, ffn1_norm, ffn1_w_gate, ffn1_w_up, ffn1_w_down, mix_norm, w_in, conv_w, conv_b, ssm_A_re, ssm_A_im, ssm_B_re, ssm_B_im, ssm_C_re, ssm_C_im, ssm_D, ssm_log_dt, glu_w, glu_b, conv_out_norm, ssm_out_norm, w_out, ffn2_norm, ffn2_w_gate, ffn2_w_up, ffn2_w_down, ple_norm, ple_w_gate, ple_w_proj, final_norm, loss_target, m_ffn1_norm, m_ffn1_w_gate, m_ffn1_w_up, m_ffn1_w_down, m_mix_norm, m_w_in, m_conv_w, m_conv_b, m_ssm_A_re, m_ssm_A_im, m_ssm_B_re, m_ssm_B_im, m_ssm_C_re, m_ssm_C_im, m_ssm_D, m_ssm_log_dt, m_glu_w, m_glu_b, m_conv_out_norm, m_ssm_out_norm, m_w_out, m_ffn2_norm, m_ffn2_w_gate, m_ffn2_w_up, m_ffn2_w_down, m_ple_norm, m_ple_w_gate, m_ple_w_proj, m_final_norm, v_ffn1_norm, v_ffn1_w_gate, v_ffn1_w_up, v_ffn1_w_down, v_mix_norm, v_w_in, v_conv_w, v_conv_b, v_ssm_A_re, v_ssm_A_im, v_ssm_B_re, v_ssm_B_im, v_ssm_C_re, v_ssm_C_im, v_ssm_D, v_ssm_log_dt, v_glu_w, v_glu_b, v_conv_out_norm, v_ssm_out_norm, v_w_out, v_ffn2_norm, v_ffn2_w_gate, v_ffn2_w_up, v_ffn2_w_down, v_ple_norm, v_ple_w_gate, v_ple_w_proj, v_final_norm):
    given = dict(locals())
    W = {n: given[n] for n in W_NAMES}
    M = {n: given['m_' + n] for n in W_NAMES}
    V = {n: given['v_' + n] for n in W_NAMES}
    Wv, Mv, Vv = [{n: _view(n, d[n]) for n in W_NAMES} for d in (W, M, V)]
    my_dev = _dev_index(_mesh_pos())

    conv_shard = _pad_rows(W['conv_w'].reshape(-1), SUBLANES)
    conv_all = _allgather(conv_shard, ((1, SUBLANES),), "ag_conv_w")[0]
    conv_full = conv_all.reshape(N_DEV, -1)[:, :DEPTH * 3 * (CONV_W // N_DEV)]
    conv_full = conv_full.reshape(N_DEV, DEPTH, 3, CONV_W // N_DEV).transpose(1, 2, 0, 3).reshape(DEPTH, 3, CONV_W)
    first, after = [], conv_all
    for gi, segments in enumerate(FIRST_LAYER_GROUPS):
        first.append(_ag_start(_layer_pack(W, 0, segments), tuple(SEGS[s] for s in segments), after,
                               "ag_start_0%s" % "abc"[gi]))
        after = first[-1][4]
    packs = [None] + [_layer_pack(W, l) for l in range(1, DEPTH)]
    flights = {1: _ag_start(packs[1], SEGS, after, "ag_start_1")}
    after = flights[1][4]
    s5 = _s5_prepare(*[W[n] + after[0, 0] for n in ('ssm_A_re', 'ssm_A_im', 'ssm_log_dt')],
                     *[W[n] for n in ('ssm_B_re', 'ssm_B_im', 'ssm_C_re', 'ssm_C_im')])
    prepared = conv_full[0, 0:1, 0:1] + s5[DEPTH - 1][1][0:1, 0:1] + packs[DEPTH - 1][0:1, 0:1].astype(F32)

    smalls, saves, bigs = [], [], []
    h = x[0]

    def gathered(handles, segments, after, name, next_layer=None, gate=None):
        send_sems, recv_sems, pack_thru, lands, _ = handles
        pack_thru, lands = _ag_wait(send_sems, recv_sems, pack_thru, lands, after, "ag_wait_" + name)
        if next_layer is not None:
            flights[next_layer] = _ag_start(packs[next_layer], SEGS, pack_thru, "ag_start_%d" % next_layer)
            gate[0][gate[1]] = gate[0][gate[1]] + flights[next_layer][4][0:1, 0:1]
        outs = _ag_finish(pack_thru, lands, tuple(SEGS[s] for s in segments))
        return _as_big({SEG_NAMES[s]: a for s, a in zip(segments, outs)})

    for l in range(DEPTH):
        small = {n: W[n][l][None] for n in ('ffn1_norm', 'mix_norm', 'conv_b', 'glu_b', 'conv_out_norm',
                                            'ssm_out_norm', 'ffn2_norm', 'ple_norm')}
        small['conv_w'] = conv_full[l]
        small['dvec'] = W['ssm_D'][l].reshape(1, SSM_W)
        small['disc_in'], small['ltab'], small['ltab_rev'], small['bbmat'], small['ccmat'] = s5[l]
        big = {}
        bigs.append(big)
        if l == 0:
            def arrive(stage, h_now, big=big, small=small):
                big.update(gathered(first[stage], FIRST_LAYER_GROUPS[stage], prepared if stage == 0 else h_now,
                                    "0%s" % "abc"[stage], *((2, (small, 'ffn2_norm')) if stage == 2 else ())))
            h, saved = _layer_fwd(h, p[l, 0], small, big, arrive)
        else:
            nxt = (l + 2, (small, 'ffn1_norm')) if l + 2 < DEPTH else ()
            big.update(gathered(flights[l], range(len(SEGS)), h, "%d" % l, *nxt))
            h, saved = _layer_fwd(h, p[l, 0], small, big)
        smalls.append(small)
        saves.append(saved)
    loss_tile, dh, d_final = _final_loss(h, W['final_norm'][None], loss_target[0])
    loss = lax.psum(loss_tile[0, 0], ("x", "y", "c"))

    layer_gs = [None] * DEPTH
    shard_grads = None
    sib, ici = None, None

    def finish_sibling(after_sib, after_ici):
        nonlocal sib, ici
        up, (send_sem, recv_sem, fulls_thru, land, _) = sib
        fulls_thru, got = _rs_sibling_wait(send_sem, recv_sem, fulls_thru, land, after_sib, "sib_wait_%d" % up)
        pbf = _pair_sum(fulls_thru, got, SEGS)
        done = finish_chips(after_ici)
        ici = (up, _rs_chips_start(pbf, after_ici if done is None else done, "rs_start_%d" % up), fulls_thru, got)
        sib = None

    def finish_chips(after):
        nonlocal ici, shard_grads
        if ici is None:
            return None
        up, (send_sems, recv_sems, pbf_thru, land, _), fulls_up, got_up = ici
        got3 = _rs_chips_wait(send_sems, recv_sems, pbf_thru, land, after, "rs_wait_%d" % up)
        shard_grads = _chip_sum(fulls_up, got_up, got3, SEGS, up, shard_grads)
        ici = None
        return shard_grads

    layer_names = [n for n in SMALL_NAMES if n != 'final_norm']
    small_flights = [None] * DEPTH
    for l in reversed(range(DEPTH)):
        small = dict(smalls[l])
        if sib is not None:
            small['ple_norm'] = small['ple_norm'] + sib[1][4][0:1, 0:1] + small_flights[l + 1][4][0:1, 0:1]
        dh, top = _layer_bwd_top(dh, p[l, 0], small, bigs[l], saves[l])
        if sib is not None:
            finish_sibling(dh, dh)
            small['glu_b'] = small['glu_b'] + ici[1][4][0:1, 0:1]
        dh, fulls, layer_gs[l] = _layer_bwd_rest(dh, top, small, bigs[l], saves[l])
        sib = (l, _rs_sibling_start(fulls, SEGS, "sib_start_%d" % l))
        last_slot = d_final[0] if l == DEPTH - 1 else jnp.zeros((D_MODEL,), F32)
        flat = jnp.concatenate([layer_gs[l][n].reshape(-1) for n in layer_names + ['conv_w']] + [last_slot])
        small_flights[l] = _small_gather_start(_pad_rows(flat, SUBLANES, D_MODEL), "small_start_%d" % l)
    grad_x = dh[None]
    finish_sibling(small_flights[0][4], small_flights[0][4])

    reduced = []
    for l in range(DEPTH):
        send_sems, recv_sems, flat_thru, land, _ = small_flights[l]
        flat_thru, land = _small_gather_wait(send_sems, recv_sems, flat_thru, land, ici[1][4], "small_wait_%d" % l)
        reduced.append(_sum_devices(land, flat_thru).reshape(-1))
    reduced = jnp.stack(reduced)
    G = {}
    o = 0
    for n in layer_names + ['conv_w']:
        size = (W[n].size if n != 'conv_w' else DEPTH * 3 * CONV_W) // DEPTH
        shape = Wv[n].shape if n != 'conv_w' else (DEPTH, 3, CONV_W)
        G[n] = reduced[:, o:o + size].reshape(shape)
        o += size
    G['final_norm'] = reduced[DEPTH - 1, o:o + D_MODEL]
    G['conv_w'] = lax.dynamic_slice_in_dim(G['conv_w'], my_dev * (CONV_W // N_DEV), CONV_W // N_DEV, axis=2)

    delta, new_m, new_v = _adamw_any(SMALL_NAMES + ['conv_w'], Wv, G, Mv, Vv)

    offs = _seg_offsets(SEGS)
    r = SEGS[0][1]
    packed_rows = {'w_out': offs[3], 'ple_w_gate': offs[4]}
    for a, f in ((0, 'ffn1'), (1, 'ffn2')):
        packed_rows.update({f + '_w_gate': offs[a], f + '_w_up': offs[a] + r, f + '_w_down': offs[a] + 2 * r})

    def relaid(sg):
        nl = sg.shape[0]
        return {'w_in': _tp(sg[:, offs[2]:offs[2] + SEGS[2][1]]),
                'ple_w_proj': _tp(sg[:, offs[5]:offs[5] + SEGS[5][1]].reshape(nl, D_MODEL // N_DEV, PLE_DIM)),
                'glu_w': sg[:, offs[6]:offs[6] + SEGS[6][1]].reshape(nl, SSM_W // N_DEV, SSM_W)}

    groups = {}
    for n in list(packed_rows) + ['w_in', 'ple_w_proj', 'glu_w']:
        groups.setdefault(Wv[n].shape, []).append(n)

    def update(first, nl, prev):
        other = relaid(shard_grads[first:first + nl])
        sets = lambda ns: [(Wv[n], Mv[n], Vv[n], shard_grads, packed_rows[n]) if n in packed_rows
                           else (Wv[n], Mv[n], Vv[n], other[n], None) for n in ns]
        return {shape: _adamw_layers(sets(ns), first, nl, None if prev is None else prev[shape])
                for shape, ns in groups.items()}

    part = update(1, DEPTH - 1, None)
    finish_chips(sum(four[3][1, 0:1, 0:1] for fours in part.values() for four in fours)
                 + sum(new_v[n][(0,) * new_v[n].ndim].reshape(1, 1) for n in SMALL_NAMES + ['conv_w']))
    for shape, fours in update(0, 1, part).items():
        for n, four in zip(groups[shape], fours):
            G[n], delta[n], new_m[n], new_v[n] = four

    outs = [[_view(n, d[n]) for n in W_NAMES] for d in (G, delta, new_m, new_v)]
    return (loss, grad_x, *outs[0], *outs[1], *outs[2], *outs[3])
```

```python
import math

import jax
import jax.numpy as jnp
from jax import lax
from jax.experimental import pallas as pl
from jax.experimental.pallas import tpu as pltpu

F32 = jnp.float32
BF16 = jnp.bfloat16

N_DEV = 8
DEPTH = 4
SEQ = 2048
D_MODEL = 1024
D_FF = 2816
CONV_W = 512
SSM_W = 512
SSM_GROUPS = 32
SSM_GROUP = 16
SSM_STATE = 64
N_STATE = SSM_GROUPS * SSM_STATE
IN_COLS = 2048
PLE_DIM = 256
EPS = 1e-6

ADAM_LR = 0.001
ADAM_B1 = 0.9
ADAM_B2 = 0.999
ADAM_EPS = 1e-08
ADAM_WD = 0.01
ADAM_STEP = 10

FF_BLOCK = 256
N_FF_BLOCKS = D_FF // FF_BLOCK
TOK_TILE_FFN_FWD = 2048
TOK_TILE_FFN_BWD = 1024
TOK_TILE = 512
CHUNK = 256
N_CHUNKS = SEQ // CHUNK
LANE_GROUP = 512
SUBLANES = 8
LANES = 128
MIB = 1024 * 1024

W_NAMES = ['ffn1_norm', 'ffn1_w_gate', 'ffn1_w_up', 'ffn1_w_down', 'mix_norm', 'w_in', 'conv_w', 'conv_b',
           'ssm_A_re', 'ssm_A_im', 'ssm_B_re', 'ssm_B_im', 'ssm_C_re', 'ssm_C_im', 'ssm_D', 'ssm_log_dt',
           'glu_w', 'glu_b', 'conv_out_norm', 'ssm_out_norm', 'w_out', 'ffn2_norm', 'ffn2_w_gate', 'ffn2_w_up',
           'ffn2_w_down', 'ple_norm', 'ple_w_gate', 'ple_w_proj', 'final_norm']
SMALL_NAMES = ['ffn1_norm', 'mix_norm', 'conv_b', 'ssm_A_re', 'ssm_A_im', 'ssm_B_re', 'ssm_B_im', 'ssm_C_re',
               'ssm_C_im', 'ssm_D', 'ssm_log_dt', 'glu_b', 'conv_out_norm', 'ssm_out_norm', 'ffn2_norm',
               'ple_norm', 'final_norm']

SEGS = ((3, 352), (3, 352), (1, 256), (1, 128), (1, 128), (1, 32), (1, 32))
PACK_ROWS = sum(n * r for n, r in SEGS)

MESH = pl.DeviceIdType.MESH
UNREAD = pl.BlockSpec(memory_space=pltpu.HBM)


def _in_hbm(*arrays):
    return [pltpu.with_memory_space_constraint(a, pltpu.HBM) for a in arrays]


def _out_hbm(outs, which):
    if not isinstance(outs, (list, tuple)):
        return pltpu.with_memory_space_constraint(outs, pltpu.HBM) if which else outs
    return [pltpu.with_memory_space_constraint(a, pltpu.HBM) if i in which else a for i, a in enumerate(outs)]


def _cparams(sem=None, vmem_mib=48, **kw):
    return pltpu.CompilerParams(dimension_semantics=sem, vmem_limit_bytes=vmem_mib * MIB, **kw)


def _dot(a, b):
    return jnp.dot(a, b, preferred_element_type=F32)


def _dot_nt(a, b):
    return lax.dot_general(a, b, (((1,), (1,)), ((), ())), preferred_element_type=F32)


def _dot_tn(a, b):
    return lax.dot_general(a, b, (((0,), (0,)), ((), ())), preferred_element_type=F32)


def _rms_stats(x):
    r = lax.rsqrt(jnp.mean(x * x, axis=-1, keepdims=True) + EPS)
    return x * r, r


def _rms_bwd(dy, xh, r, g):
    dxh = dy * g
    dx = r * (dxh - xh * jnp.mean(dxh * xh, axis=-1, keepdims=True))
    dg = jnp.sum(dy * xh, axis=0, keepdims=True)
    return dx, dg


def _sigmoid(x):
    return 0.5 * jnp.tanh(0.5 * x) + 0.5


_GELU_C = math.sqrt(2.0 / math.pi)


def _gelu(x):
    t = jnp.tanh(_GELU_C * (x + 0.044715 * x * x * x))
    return 0.5 * x * (1.0 + t), t


def _gelu_grad(x, t):
    return 0.5 * (1.0 + t) + 0.5 * x * (1.0 - t * t) * _GELU_C * (1.0 + 3.0 * 0.044715 * x * x)


def _accumulate(ref, first, value):
    @pl.when(first)
    def _():
        ref[...] = value

    @pl.when(jnp.logical_not(first))
    def _():
        ref[...] += value


def _ffn_fwd(h, g, w3):
    tm = TOK_TILE_FFN_FWD
    last = N_FF_BLOCKS - 1

    def body(h_ref, g_ref, wgu_ref, wd_ref, wd_last_ref, out_ref, gu_ref, u_ref, a_ref):
        k = pl.program_id(1)

        @pl.when(k == 0)
        def _():
            x = h_ref[...]
            xh, _ = _rms_stats(x)
            u_ref[...] = (xh * g_ref[...]).astype(BF16)
            out_ref[...] = x
            a_ref[1] = jnp.zeros((tm, FF_BLOCK), BF16)

        out_ref[...] += 0.5 * _dot(a_ref[(k + 1) % 2], wd_ref[0])
        gu = _dot_nt(u_ref[...], wgu_ref[...].reshape(2 * FF_BLOCK, D_MODEL))
        gate, up = gu[:, :FF_BLOCK], gu[:, FF_BLOCK:]
        a_ref[k % 2] = (gate * _sigmoid(gate) * up).astype(BF16)
        gu_ref[0] = gate.astype(BF16)
        gu_ref[1] = up.astype(BF16)

        @pl.when(k == last)
        def _():
            out_ref[...] += 0.5 * _dot(a_ref[last % 2], wd_last_ref[0])

    return _out_hbm(pl.pallas_call(
        body, name="ffn_fwd",
        grid=(SEQ // tm, N_FF_BLOCKS),
        in_specs=[pl.BlockSpec((tm, D_MODEL), lambda m, k: (m, 0), pipeline_mode=pl.Buffered(1)),
                  pl.BlockSpec((1, D_MODEL), lambda m, k: (0, 0)),
                  pl.BlockSpec((2, FF_BLOCK, D_MODEL), lambda m, k: (0, k, 0)),
                  pl.BlockSpec((1, FF_BLOCK, D_MODEL), lambda m, k: (2, jnp.maximum(k - 1, 0), 0)),
                  pl.BlockSpec((1, FF_BLOCK, D_MODEL), lambda m, k: (2, last, 0), pipeline_mode=pl.Buffered(1))],
        out_specs=[pl.BlockSpec((tm, D_MODEL), lambda m, k: (m, 0)),
                   pl.BlockSpec((2, tm, FF_BLOCK), lambda m, k: (0, m, k))],
        out_shape=[jax.ShapeDtypeStruct((SEQ, D_MODEL), F32),
                   pltpu.HBM((2, SEQ, D_FF), BF16)],
        scratch_shapes=[pltpu.VMEM((tm, D_MODEL), BF16), pltpu.VMEM((2, tm, FF_BLOCK), BF16)],
        compiler_params=_cparams(("parallel", "arbitrary"), 56),
    )(*_in_hbm(h, g, w3, w3, w3)), (1,))


def _ffn_bwd_act(h, g, dout, gu, w3):
    tm = TOK_TILE_FFN_BWD
    last = N_FF_BLOCKS - 1

    def body(h_ref, g_ref, d_ref, gu_ref, wd_ref, wgu_ref, wgu_last_ref, dh_ref, dga_ref, ud_ref, dg_ref,
             acc_ref, dgu_ref):
        m = pl.program_id(0)
        k = pl.program_id(1)

        @pl.when(k == 0)
        def _():
            xh, _ = _rms_stats(h_ref[...])
            ud_ref[0] = (xh * g_ref[...]).astype(BF16)
            ud_ref[1] = (0.5 * d_ref[...]).astype(BF16)
            acc_ref[...] = jnp.zeros_like(acc_ref)
            dgu_ref[1] = jnp.zeros((tm, 2 * FF_BLOCK), BF16)

        acc_ref[...] += _dot(dgu_ref[(k + 1) % 2], wgu_ref[...].reshape(2 * FF_BLOCK, D_MODEL))
        gate = gu_ref[0].astype(F32)
        up = gu_ref[1].astype(F32)
        sg = _sigmoid(gate)
        silu = gate * sg
        da = _dot_nt(ud_ref[1], wd_ref[0])
        dgate = (da * up * (sg + silu * (1.0 - sg))).astype(BF16)
        dup = (da * silu).astype(BF16)
        dga_ref[0] = dgate
        dga_ref[1] = dup
        dga_ref[2] = (silu * up).astype(BF16)
        dgu_ref[k % 2, :, 0:FF_BLOCK] = dgate
        dgu_ref[k % 2, :, FF_BLOCK:2 * FF_BLOCK] = dup

        @pl.when(k == last)
        def _():
            du = acc_ref[...] + _dot(dgu_ref[last % 2], wgu_last_ref[...].reshape(2 * FF_BLOCK, D_MODEL))
            xh, r = _rms_stats(h_ref[...])
            dx, dg = _rms_bwd(du, xh, r, g_ref[...])
            dh_ref[...] = d_ref[...] + dx
            _accumulate(dg_ref, m == 0, dg)

    return _out_hbm(pl.pallas_call(
        body, name="ffn_bwd_act",
        grid=(SEQ // tm, N_FF_BLOCKS),
        in_specs=[pl.BlockSpec((tm, D_MODEL), lambda m, k: (m, 0), pipeline_mode=pl.Buffered(1)),
                  pl.BlockSpec((1, D_MODEL), lambda m, k: (0, 0)),
                  pl.BlockSpec((tm, D_MODEL), lambda m, k: (m, 0), pipeline_mode=pl.Buffered(1)),
                  pl.BlockSpec((2, tm, FF_BLOCK), lambda m, k: (0, m, k)),
                  pl.BlockSpec((1, FF_BLOCK, D_MODEL), lambda m, k: (2, k, 0)),
                  pl.BlockSpec((2, FF_BLOCK, D_MODEL), lambda m, k: (0, jnp.maximum(k - 1, 0), 0)),
                  pl.BlockSpec((2, FF_BLOCK, D_MODEL), lambda m, k: (0, last, 0), pipeline_mode=pl.Buffered(1))],
        out_specs=[pl.BlockSpec((tm, D_MODEL), lambda m, k: (m, 0)),
                   pl.BlockSpec((3, tm, FF_BLOCK), lambda m, k: (0, m, k)),
                   pl.BlockSpec((2, tm, D_MODEL), lambda m, k: (0, m, 0)),
                   pl.BlockSpec((1, D_MODEL), lambda m, k: (0, 0))],
        out_shape=[jax.ShapeDtypeStruct((SEQ, D_MODEL), F32),
                   pltpu.HBM((3, SEQ, D_FF), BF16),
                   pltpu.HBM((2, SEQ, D_MODEL), BF16),
                   jax.ShapeDtypeStruct((1, D_MODEL), F32)],
        scratch_shapes=[pltpu.VMEM((tm, D_MODEL), F32), pltpu.VMEM((2, tm, 2 * FF_BLOCK), BF16)],
        compiler_params=_cparams(("arbitrary", "arbitrary"), 56),
    )(*_in_hbm(h, g, dout, gu, w3, w3, w3)), (1, 2))


def _matmul_tn(a, b, bm, out_dtype, name, bn=None, to_kernel=True):
    na, t, m = a.shape
    nb, _, n = b.shape
    bn = n if bn is None else bn

    def body(a_ref, b_ref, o_ref):
        o_ref[0] = _dot_tn(a_ref[0], b_ref[0]).astype(out_dtype)

    return _out_hbm(pl.pallas_call(
        body, name=name,
        grid=(na, m // bm, n // bn),
        in_specs=[pl.BlockSpec((1, t, bm), lambda i, k, j: (i, 0, k)),
                  pl.BlockSpec((1, t, bn), lambda i, k, j: (jnp.maximum(i - (na - nb), 0), 0, j))],
        out_specs=pl.BlockSpec((1, bm, bn), lambda i, k, j: (i, k, j)),
        out_shape=pltpu.HBM((na, m, n), out_dtype) if to_kernel else jax.ShapeDtypeStruct((na, m, n), out_dtype),
        compiler_params=_cparams(("arbitrary", "parallel", "parallel")),
    )(*_in_hbm(a, b)), to_kernel)


def _inproj_fwd(h, g, wint):
    tm = TOK_TILE

    def body(h_ref, g_ref, w_ref, z_ref):
        xh, _ = _rms_stats(h_ref[...])
        z_ref[...] = _dot_nt((xh * g_ref[...]).astype(BF16), w_ref[...])

    return pl.pallas_call(
        body, name="inproj_fwd",
        grid=(SEQ // tm,),
        in_specs=[pl.BlockSpec((tm, D_MODEL), lambda m: (m, 0)),
                  pl.BlockSpec((1, D_MODEL), lambda m: (0, 0)),
                  pl.BlockSpec((None, IN_COLS, D_MODEL), lambda m: (0, 0, 0))],
        out_specs=pl.BlockSpec((tm, IN_COLS), lambda m: (m, 0)),
        out_shape=jax.ShapeDtypeStruct((SEQ, IN_COLS), F32),
        compiler_params=_cparams(("parallel",)),
    )(*_in_hbm(h, g, wint))


def _inproj_bwd(h, g, dh, dz, wint):
    tm = TOK_TILE

    def body(h_ref, g_ref, dh_ref, dz_ref, w_ref, o_ref, u_ref, dg_ref):
        xh, r = _rms_stats(h_ref[...])
        u_ref[0] = (xh * g_ref[...]).astype(BF16)
        dx, dg = _rms_bwd(_dot(dz_ref[...], w_ref[...]), xh, r, g_ref[...])
        o_ref[...] = dh_ref[...] + dx
        _accumulate(dg_ref, pl.program_id(0) == 0, dg)

    return _out_hbm(pl.pallas_call(
        body, name="inproj_bwd",
        grid=(SEQ // tm,),
        in_specs=[pl.BlockSpec((tm, D_MODEL), lambda m: (m, 0)),
                  pl.BlockSpec((1, D_MODEL), lambda m: (0, 0)),
                  pl.BlockSpec((tm, D_MODEL), lambda m: (m, 0)),
                  pl.BlockSpec((tm, IN_COLS), lambda m: (m, 0)),
                  pl.BlockSpec((None, IN_COLS, D_MODEL), lambda m: (0, 0, 0))],
        out_specs=[pl.BlockSpec((tm, D_MODEL), lambda m: (m, 0)),
                   pl.BlockSpec((1, tm, D_MODEL), lambda m: (0, m, 0)),
                   pl.BlockSpec((1, D_MODEL), lambda m: (0, 0))],
        out_shape=[jax.ShapeDtypeStruct((SEQ, D_MODEL), F32),
                   pltpu.HBM((1, SEQ, D_MODEL), BF16),
                   jax.ShapeDtypeStruct((1, D_MODEL), F32)],
        compiler_params=_cparams(("arbitrary",)),
    )(*_in_hbm(h, g, dh, dz, wint)), (1,))


def _row_ids(n, w):
    return lax.broadcasted_iota(jnp.int32, (n, w), 0)


def _bcast_row(x, i, n):
    return jnp.broadcast_to(x[i:i + 1, :], (n, x.shape[1]))


def _conv_taps(v, tail):
    n, w = v.shape
    rid = _row_ids(n, w)
    v1 = jnp.where(rid == 0, _bcast_row(tail, 7, n), pltpu.roll(v, 1, 0))
    v2 = jnp.where(rid == 0, _bcast_row(tail, 6, n),
                   jnp.where(rid == 1, _bcast_row(tail, 7, n), pltpu.roll(v, 2, 0)))
    return v1, v2


def _block_tiles():
    half_rows, half_cols = SSM_W // 2, N_STATE // 2
    for half in range(2):
        for part in range(2):
            yield (slice(half * half_rows, (half + 1) * half_rows),
                   slice(part * N_STATE + half * half_cols, part * N_STATE + (half + 1) * half_cols))


def _block_expand(x, mat_ref, out_ref):
    for rows, cols in _block_tiles():
        out_ref[:, cols] = _dot(x[:, rows], mat_ref[rows, cols])


def _block_contract(s, mat_ref):
    halves = {}
    for rows, cols in _block_tiles():
        part = _dot_nt(s[:, cols], mat_ref[rows, cols])
        halves[rows.start] = part if rows.start not in halves else halves[rows.start] + part
    return jnp.concatenate([halves[k] for k in sorted(halves)], axis=1)


def _block_wgrad(a, b, name):
    t = a.shape[1]
    half_rows, half_cols = SSM_W // 2, N_STATE // 2

    def body(a_ref, b_ref, o_ref):
        o_ref[...] = _dot_tn(a_ref[...], b_ref[...])

    return pl.pallas_call(
        body, name=name,
        grid=(2, 2),
        in_specs=[pl.BlockSpec((None, t, half_rows), lambda h, p: (0, 0, h)),
                  pl.BlockSpec((None, t, half_cols), lambda h, p: (0, 0, 2 * p + h))],
        out_specs=pl.BlockSpec((half_rows, half_cols), lambda h, p: (h, 2 * p + h)),
        out_shape=jax.ShapeDtypeStruct((SSM_W, 2 * N_STATE), F32),
        compiler_params=_cparams(("parallel", "parallel")),
    )(*_in_hbm(a, b))


def _scan_chunk(work, ltab, carry, reverse):
    nblk = CHUNK // SUBLANES
    for gi in range(N_STATE // LANE_GROUP):
        cre = pl.ds(gi * LANE_GROUP, LANE_GROUP)
        cim = pl.ds(N_STATE + gi * LANE_GROUP, LANE_GROUP)
        pows = [(ltab[8 * k:8 * k + 8, cre], ltab[8 * k:8 * k + 8, cim]) for k in range(3)]
        pr = ltab[24:32, cre]
        pi = ltab[24:32, cim]

        def blk(i, c, cre=cre, cim=cim, pows=pows, pr=pr, pi=pi):
            cr, ci = c
            b = (nblk - 1 - i) if reverse else i
            r0 = pl.multiple_of(b * SUBLANES, SUBLANES)
            xr = work[pl.ds(r0, SUBLANES), cre]
            xi = work[pl.ds(r0, SUBLANES), cim]
            for k, s in enumerate((1, 2, 4)):
                lr, li = pows[k]
                shift = SUBLANES - s if reverse else s
                sr = pltpu.roll(xr, shift, 0)
                si = pltpu.roll(xi, shift, 0)
                xr, xi = xr + lr * sr - li * si, xi + lr * si + li * sr
            xr, xi = xr + pr * cr - pi * ci, xi + pr * ci + pi * cr
            work[pl.ds(r0, SUBLANES), cre] = xr
            work[pl.ds(r0, SUBLANES), cim] = xi
            edge = 0 if reverse else SUBLANES - 1
            return _bcast_row(xr, edge, SUBLANES), _bcast_row(xi, edge, SUBLANES)

        cr, ci = lax.fori_loop(0, nblk, blk, (carry[:, cre], carry[:, cim]))
        carry[:, cre] = cr
        carry[:, cim] = ci


def _s5conv_fwd(z, convw, convb, bbmat, ccmat, dvec, ltab):
    def body(z_ref, cw_ref, cb_ref, bb_ref, cc_ref, d_ref, lt_ref, ya_ref, ys_ref, hs_ref,
             work, carry, tail):
        c = pl.program_id(0)

        @pl.when(c == 0)
        def _():
            carry[...] = jnp.zeros_like(carry)
            tail[...] = jnp.zeros_like(tail)

        zb = z_ref[:, 0:CONV_W]
        v = z_ref[:, CONV_W:2 * CONV_W] * z_ref[:, 2 * CONV_W:3 * CONV_W]
        us = z_ref[:, 3 * CONV_W:4 * CONV_W]
        v1, v2 = _conv_taps(v, tail[...])
        tail[...] = v[CHUNK - 8:CHUNK, :]
        y = cw_ref[0:1, :] * v2 + cw_ref[1:2, :] * v1 + cw_ref[2:3, :] * v
        ya_ref[...] = zb * (y + cb_ref[...])

        _block_expand(us.astype(BF16), bb_ref, work)
        _scan_chunk(work, lt_ref, carry, reverse=False)
        hs = work[...].astype(BF16)
        hs_ref[...] = hs
        ys_ref[...] = _block_contract(hs, cc_ref) + d_ref[...] * us

    return _out_hbm(pl.pallas_call(
        body, name="s5conv_fwd",
        grid=(N_CHUNKS,),
        in_specs=[pl.BlockSpec((CHUNK, IN_COLS), lambda c: (c, 0)),
                  pl.BlockSpec((3, CONV_W), lambda c: (0, 0)),
                  pl.BlockSpec((1, CONV_W), lambda c: (0, 0)),
                  pl.BlockSpec((SSM_W, 2 * N_STATE), lambda c: (0, 0)),
                  pl.BlockSpec((SSM_W, 2 * N_STATE), lambda c: (0, 0)),
                  pl.BlockSpec((1, SSM_W), lambda c: (0, 0)),
                  pl.BlockSpec((32, 2 * N_STATE), lambda c: (0, 0))],
        out_specs=[pl.BlockSpec((CHUNK, CONV_W), lambda c: (c, 0)),
                   pl.BlockSpec((CHUNK, SSM_W), lambda c: (c, 0)),
                   pl.BlockSpec((CHUNK, 2 * N_STATE), lambda c: (c, 0))],
        out_shape=[pltpu.HBM((SEQ, CONV_W), F32),
                   pltpu.HBM((SEQ, SSM_W), F32),
                   jax.ShapeDtypeStruct((SEQ, 2 * N_STATE), BF16)],
        scratch_shapes=[pltpu.VMEM((CHUNK, 2 * N_STATE), F32),
                        pltpu.VMEM((8, 2 * N_STATE), F32),
                        pltpu.VMEM((8, CONV_W), F32)],
        compiler_params=_cparams(("arbitrary",)),
    )(*_in_hbm(z, convw, convb, bbmat, ccmat, dvec, ltab)), (0, 1))


def _s5conv_bwd(z, hs, dya, dys, convw, convb, bbmat, ccmat, dvec, ltab_rev):
    nc = N_CHUNKS
    hb = 16

    def body(z_ref, zp_ref, hs_ref, hp_ref, dya_ref, dys_ref, cw_ref, cb_ref, bb_ref, cc_ref, d_ref, lt_ref,
             dz_ref, g_ref, us_ref, dyb_ref, dl_ref, dcw_ref, work, carry, head):
        i = pl.program_id(0)
        first_chunk = i == nc - 1

        @pl.when(i == 0)
        def _():
            carry[...] = jnp.zeros_like(carry)
            head[...] = jnp.zeros_like(head)
            dl_ref[...] = jnp.zeros_like(dl_ref)
            dcw_ref[...] = jnp.zeros_like(dcw_ref)

        us = z_ref[:, 3 * CONV_W:4 * CONV_W]
        dy = dys_ref[...]
        dy_bf = dy.astype(BF16)
        us_ref[0] = us.astype(BF16)
        dyb_ref[0] = dy_bf

        _block_expand(dy_bf, cc_ref, work)
        _scan_chunk(work, lt_ref, carry, reverse=True)
        gg = work[...]
        gg_bf = gg.astype(BF16)
        g_ref[0] = gg_bf
        dus = d_ref[...] * dy + _block_contract(gg_bf, bb_ref)

        hcur = hs_ref[...].astype(F32)
        hlast = hp_ref[...].astype(F32)[hb - 1:hb, :]
        hlast = jnp.where(first_chunk, 0.0, hlast)
        rid = _row_ids(CHUNK, 2 * N_STATE)
        hprev = jnp.where(rid == 0, jnp.broadcast_to(hlast, (CHUNK, 2 * N_STATE)), pltpu.roll(hcur, 1, 0))
        gr, gi = gg[:, :N_STATE], gg[:, N_STATE:]
        hr, hi = hprev[:, :N_STATE], hprev[:, N_STATE:]
        dl_ref[:, :N_STATE] += (gr * hr + gi * hi).reshape(CHUNK // 8, 8, N_STATE).sum(axis=0)
        dl_ref[:, N_STATE:] += (gi * hr - gr * hi).reshape(CHUNK // 8, 8, N_STATE).sum(axis=0)

        @pl.when(i == nc - 1)
        def _():
            dl_ref[0:1, :] = jnp.sum(dl_ref[...], axis=0, keepdims=True)

        zb = z_ref[:, 0:CONV_W]
        zc = z_ref[:, CONV_W:2 * CONV_W]
        zv = z_ref[:, 2 * CONV_W:3 * CONV_W]
        v = zc * zv
        vtail = jnp.where(first_chunk, 0.0, zp_ref[:, CONV_W:2 * CONV_W] * zp_ref[:, 2 * CONV_W:3 * CONV_W])
        v1, v2 = _conv_taps(v, vtail)
        w0, w1, w2 = cw_ref[0:1, :], cw_ref[1:2, :], cw_ref[2:3, :]
        y = w0 * v2 + w1 * v1 + w2 * v
        dya_v = dya_ref[...]
        dzb = dya_v * (y + cb_ref[...])
        dyc = dya_v * zb
        hd = head[...]
        rc = _row_ids(CHUNK, CONV_W)
        n1 = jnp.where(rc == CHUNK - 1, _bcast_row(hd, 0, CHUNK), pltpu.roll(dyc, CHUNK - 1, 0))
        n2 = jnp.where(rc == CHUNK - 1, _bcast_row(hd, 1, CHUNK),
                       jnp.where(rc == CHUNK - 2, _bcast_row(hd, 0, CHUNK), pltpu.roll(dyc, CHUNK - 2, 0)))
        head[...] = dyc[0:8, :]
        dv = w2 * dyc + w1 * n1 + w0 * n2
        dz_ref[:, 0:CONV_W] = dzb.astype(BF16)
        dz_ref[:, CONV_W:2 * CONV_W] = (dv * zv).astype(BF16)
        dz_ref[:, 2 * CONV_W:3 * CONV_W] = (dv * zc).astype(BF16)
        dz_ref[:, 3 * CONV_W:4 * CONV_W] = dus.astype(BF16)
        dcw_ref[0:1, :] += jnp.sum(dyc * v2, axis=0, keepdims=True)
        dcw_ref[1:2, :] += jnp.sum(dyc * v1, axis=0, keepdims=True)
        dcw_ref[2:3, :] += jnp.sum(dyc * v, axis=0, keepdims=True)
        dcw_ref[3:4, :] += jnp.sum(dyc, axis=0, keepdims=True)
        dcw_ref[4:5, :] += jnp.sum(dy * us, axis=0, keepdims=True)

    rev = lambda i: nc - 1 - i
    return _out_hbm(pl.pallas_call(
        body, name="s5conv_bwd",
        grid=(nc,),
        in_specs=[pl.BlockSpec((CHUNK, IN_COLS), lambda i: (rev(i), 0)),
                  pl.BlockSpec((8, IN_COLS), lambda i: (jnp.maximum(rev(i) * (CHUNK // 8) - 1, 0), 0)),
                  pl.BlockSpec((CHUNK, 2 * N_STATE), lambda i: (rev(i), 0)),
                  pl.BlockSpec((hb, 2 * N_STATE), lambda i: (jnp.maximum(rev(i) * (CHUNK // hb) - 1, 0), 0)),
                  pl.BlockSpec((CHUNK, CONV_W), lambda i: (rev(i), 0)),
                  pl.BlockSpec((CHUNK, SSM_W), lambda i: (rev(i), 0)),
                  pl.BlockSpec((3, CONV_W), lambda i: (0, 0)),
                  pl.BlockSpec((1, CONV_W), lambda i: (0, 0)),
                  pl.BlockSpec((SSM_W, 2 * N_STATE), lambda i: (0, 0)),
                  pl.BlockSpec((SSM_W, 2 * N_STATE), lambda i: (0, 0)),
                  pl.BlockSpec((1, SSM_W), lambda i: (0, 0)),
                  pl.BlockSpec((32, 2 * N_STATE), lambda i: (0, 0))],
        out_specs=[pl.BlockSpec((CHUNK, IN_COLS), lambda i: (rev(i), 0)),
                   pl.BlockSpec((1, CHUNK, 2 * N_STATE), lambda i: (0, rev(i), 0)),
                   pl.BlockSpec((1, CHUNK, SSM_W), lambda i: (0, rev(i), 0)),
                   pl.BlockSpec((1, CHUNK, SSM_W), lambda i: (0, rev(i), 0)),
                   pl.BlockSpec((8, 2 * N_STATE), lambda i: (0, 0)),
                   pl.BlockSpec((8, CONV_W), lambda i: (0, 0))],
        out_shape=[jax.ShapeDtypeStruct((SEQ, IN_COLS), BF16),
                   pltpu.HBM((1, SEQ, 2 * N_STATE), BF16),
                   pltpu.HBM((1, SEQ, SSM_W), BF16),
                   pltpu.HBM((1, SEQ, SSM_W), BF16),
                   jax.ShapeDtypeStruct((8, 2 * N_STATE), F32),
                   jax.ShapeDtypeStruct((8, CONV_W), F32)],
        scratch_shapes=[pltpu.VMEM((CHUNK, 2 * N_STATE), F32),
                        pltpu.VMEM((8, 2 * N_STATE), F32),
                        pltpu.VMEM((8, CONV_W), F32)],
        compiler_params=_cparams(("arbitrary",)),
    )(*_in_hbm(z, z, hs, hs, dya, dys, convw, convb, bbmat, ccmat, dvec, ltab_rev)), (1, 2, 3))


def _mix_out_fwd(h, ya, ys, gluw, glub, con, son, wout):
    tm = TOK_TILE

    def body(h_ref, ya_ref, ys_ref, gw_ref, gb_ref, con_ref, son_ref, wo_ref, o_ref):
        zg, _ = _gelu(ys_ref[...])
        q = _dot(zg.astype(BF16), gw_ref[...]) + gb_ref[...]
        out_s = zg * _sigmoid(q)
        na, _ = _rms_stats(ya_ref[...])
        ns, _ = _rms_stats(out_s)
        o_ref[...] = (h_ref[...]
                      + _dot((na * con_ref[...]).astype(BF16), wo_ref[0:CONV_W, :])
                      + _dot((ns * son_ref[...]).astype(BF16), wo_ref[CONV_W:2 * CONV_W, :]))

    row = lambda m: (m, 0)
    fixed = lambda m: (0, 0)
    return pl.pallas_call(
        body, name="mix_out_fwd",
        grid=(SEQ // tm,),
        in_specs=[pl.BlockSpec((tm, D_MODEL), row), pl.BlockSpec((tm, CONV_W), row), pl.BlockSpec((tm, SSM_W), row),
                  pl.BlockSpec((SSM_W, SSM_W), fixed), pl.BlockSpec((1, SSM_W), fixed),
                  pl.BlockSpec((1, CONV_W), fixed), pl.BlockSpec((1, SSM_W), fixed),
                  pl.BlockSpec((None, D_MODEL, D_MODEL), lambda m: (0, 0, 0))],
        out_specs=pl.BlockSpec((tm, D_MODEL), row),
        out_shape=jax.ShapeDtypeStruct((SEQ, D_MODEL), F32),
        compiler_params=_cparams(("parallel",)),
    )(*_in_hbm(h, ya, ys, gluw, glub, con, son, wout))


def _mix_out_bwd(dh, ya, ys, gluw, glub, con, son, wout):
    tm = TOK_TILE

    def body(dh_ref, ya_ref, ys_ref, gw_ref, gb_ref, con_ref, son_ref, wo_ref,
             dya_ref, dys_ref, yc_ref, dhb_ref, zg_ref, dq_ref, part_ref):
        ysv = ys_ref[...]
        zg, th = _gelu(ysv)
        zg_bf = zg.astype(BF16)
        s = _sigmoid(_dot(zg_bf, gw_ref[...]) + gb_ref[...])
        out_s = zg * s
        na, ra = _rms_stats(ya_ref[...])
        ns, rs = _rms_stats(out_s)
        dh_bf = dh_ref[...].astype(BF16)
        yc_ref[0, :, 0:CONV_W] = (na * con_ref[...]).astype(BF16)
        yc_ref[0, :, CONV_W:2 * CONV_W] = (ns * son_ref[...]).astype(BF16)
        dhb_ref[0] = dh_bf
        dca = _dot_nt(dh_bf, wo_ref[0:CONV_W, :])
        dcs = _dot_nt(dh_bf, wo_ref[CONV_W:2 * CONV_W, :])
        dya, dcon = _rms_bwd(dca, na, ra, con_ref[...])
        dos, dson = _rms_bwd(dcs, ns, rs, son_ref[...])
        dya_ref[...] = dya
        dq = dos * zg * s * (1.0 - s)
        dq_bf = dq.astype(BF16)
        dzg = dos * s + _dot_nt(dq_bf, gw_ref[...])
        dys_ref[...] = dzg * _gelu_grad(ysv, th)
        zg_ref[0] = zg_bf
        dq_ref[0] = dq_bf
        rid = _row_ids(SUBLANES, SSM_W)
        part = jnp.zeros((SUBLANES, SSM_W), F32)
        for i, rowv in enumerate((dcon, dson, jnp.sum(dq, axis=0, keepdims=True))):
            part = jnp.where(rid == i, jnp.broadcast_to(rowv, (SUBLANES, SSM_W)), part)
        _accumulate(part_ref, pl.program_id(0) == 0, part)

    row = lambda m: (m, 0)
    fixed = lambda m: (0, 0)
    lead = lambda m: (0, m, 0)
    return _out_hbm(pl.pallas_call(
        body, name="mix_out_bwd",
        grid=(SEQ // tm,),
        in_specs=[pl.BlockSpec((tm, D_MODEL), row), pl.BlockSpec((tm, CONV_W), row), pl.BlockSpec((tm, SSM_W), row),
                  pl.BlockSpec((SSM_W, SSM_W), fixed), pl.BlockSpec((1, SSM_W), fixed),
                  pl.BlockSpec((1, CONV_W), fixed), pl.BlockSpec((1, SSM_W), fixed),
                  pl.BlockSpec((None, D_MODEL, D_MODEL), lambda m: (0, 0, 0))],
        out_specs=[pl.BlockSpec((tm, CONV_W), row), pl.BlockSpec((tm, SSM_W), row),
                   pl.BlockSpec((1, tm, D_MODEL), lead), pl.BlockSpec((1, tm, D_MODEL), lead),
                   pl.BlockSpec((1, tm, SSM_W), lead), pl.BlockSpec((1, tm, SSM_W), lead),
                   pl.BlockSpec((8, SSM_W), fixed)],
        out_shape=[pltpu.HBM((SEQ, CONV_W), F32), pltpu.HBM((SEQ, SSM_W), F32),
                   pltpu.HBM((1, SEQ, D_MODEL), BF16), pltpu.HBM((1, SEQ, D_MODEL), BF16),
                   pltpu.HBM((1, SEQ, SSM_W), BF16), pltpu.HBM((1, SEQ, SSM_W), BF16),
                   jax.ShapeDtypeStruct((8, SSM_W), F32)],
        compiler_params=_cparams(("arbitrary",)),
    )(*_in_hbm(dh, ya, ys, gluw, glub, con, son, wout)), (0, 1, 2, 3, 4, 5))


def _ple_fwd(h, g, p, wgate, wprojt):
    tm = TOK_TILE

    def body(h_ref, g_ref, p_ref, wg_ref, wp_ref, o_ref, u_ref, pb_ref, q_ref, pp_ref):
        x = h_ref[...]
        xh, _ = _rms_stats(x)
        u = (xh * g_ref[...]).astype(BF16)
        p_bf = p_ref[...].astype(BF16)
        q = _dot(u, wg_ref[...])
        pp = _dot_nt(p_bf, wp_ref[...])
        o_ref[...] = x + pp * _sigmoid(q)
        u_ref[0] = u
        pb_ref[0] = p_bf
        q_ref[...] = q.astype(BF16)
        pp_ref[...] = pp.astype(BF16)

    row = lambda m: (m, 0)
    fixed = lambda m: (0, 0)
    lead = lambda m: (0, m, 0)
    return pl.pallas_call(
        body, name="ple_fwd",
        grid=(SEQ // tm,),
        in_specs=[pl.BlockSpec((tm, D_MODEL), row), pl.BlockSpec((1, D_MODEL), fixed), pl.BlockSpec((tm, PLE_DIM), row),
                  pl.BlockSpec((None, D_MODEL, D_MODEL), lambda m: (0, 0, 0)), pl.BlockSpec((D_MODEL, PLE_DIM), fixed)],
        out_specs=[pl.BlockSpec((tm, D_MODEL), row), pl.BlockSpec((1, tm, D_MODEL), lead),
                   pl.BlockSpec((1, tm, PLE_DIM), lead), pl.BlockSpec((tm, D_MODEL), row),
                   pl.BlockSpec((tm, D_MODEL), row)],
        out_shape=[jax.ShapeDtypeStruct((SEQ, D_MODEL), F32), pltpu.HBM((1, SEQ, D_MODEL), BF16),
                   pltpu.HBM((1, SEQ, PLE_DIM), BF16), pltpu.HBM((SEQ, D_MODEL), BF16),
                   pltpu.HBM((SEQ, D_MODEL), BF16)],
        compiler_params=_cparams(("parallel",)),
    )(*_in_hbm(h, g, p, wgate, wprojt))


def _ple_bwd(h, g, dh, q, pp, wgate):
    tm = TOK_TILE

    def body(h_ref, g_ref, dh_ref, q_ref, pp_ref, wg_ref, o_ref, dq_ref, dpp_ref, dg_ref):
        xh, r = _rms_stats(h_ref[...])
        s = _sigmoid(q_ref[...].astype(F32))
        dhv = dh_ref[...]
        dq = (dhv * pp_ref[...].astype(F32) * s * (1.0 - s)).astype(BF16)
        dq_ref[0] = dq
        dpp_ref[0] = (dhv * s).astype(BF16)
        dx, dg = _rms_bwd(_dot_nt(dq, wg_ref[...]), xh, r, g_ref[...])
        o_ref[...] = dhv + dx
        _accumulate(dg_ref, pl.program_id(0) == 0, dg)

    row = lambda m: (m, 0)
    fixed = lambda m: (0, 0)
    lead = lambda m: (0, m, 0)
    big = pltpu.HBM((1, SEQ, D_MODEL), BF16)
    return _out_hbm(pl.pallas_call(
        body, name="ple_bwd",
        grid=(SEQ // tm,),
        in_specs=[pl.BlockSpec((tm, D_MODEL), row), pl.BlockSpec((1, D_MODEL), fixed), pl.BlockSpec((tm, D_MODEL), row),
                  pl.BlockSpec((tm, D_MODEL), row), pl.BlockSpec((tm, D_MODEL), row),
                  pl.BlockSpec((None, D_MODEL, D_MODEL), lambda m: (0, 0, 0))],
        out_specs=[pl.BlockSpec((tm, D_MODEL), row),
                   pl.BlockSpec((1, tm, D_MODEL), lead), pl.BlockSpec((1, tm, D_MODEL), lead),
                   pl.BlockSpec((1, D_MODEL), fixed)],
        out_shape=[jax.ShapeDtypeStruct((SEQ, D_MODEL), F32), big, big,
                   jax.ShapeDtypeStruct((1, D_MODEL), F32)],
        compiler_params=_cparams(("arbitrary",)),
    )(*_in_hbm(h, g, dh, q, pp, wgate)), (1, 2))


def _final_loss(h, g, target):
    tm = TOK_TILE

    def body(h_ref, g_ref, t_ref, loss_ref, dh_ref, dg_ref):
        first = pl.program_id(0) == 0
        xh, r = _rms_stats(h_ref[...])
        diff = xh * g_ref[...] - t_ref[...]
        part = 0.5 * jnp.sum(jnp.mean(diff * diff, axis=-1, keepdims=True), axis=0, keepdims=True)
        _accumulate(loss_ref, first, jnp.broadcast_to(part, (SUBLANES, LANES)))
        dx, dg = _rms_bwd(diff * (1.0 / D_MODEL), xh, r, g_ref[...])
        dh_ref[...] = dx
        _accumulate(dg_ref, first, dg)

    row = lambda m: (m, 0)
    fixed = lambda m: (0, 0)
    return pl.pallas_call(
        body, name="final_loss",
        grid=(SEQ // tm,),
        in_specs=[pl.BlockSpec((tm, D_MODEL), row), pl.BlockSpec((1, D_MODEL), fixed),
                  pl.BlockSpec((tm, D_MODEL), row)],
        out_specs=[pl.BlockSpec((SUBLANES, LANES), fixed),
                   pl.BlockSpec((tm, D_MODEL), row),
                   pl.BlockSpec((1, D_MODEL), fixed)],
        out_shape=[jax.ShapeDtypeStruct((SUBLANES, LANES), F32),
                   jax.ShapeDtypeStruct((SEQ, D_MODEL), F32),
                   jax.ShapeDtypeStruct((1, D_MODEL), F32)],
        compiler_params=_cparams(("arbitrary",)),
    )(*_in_hbm(h, g, target))


def _disc(ar, ai, ldt):
    dt = jnp.exp(ldt)
    mag = jnp.exp(ar * dt)
    ph = ai * dt
    lr, li = mag * jnp.cos(ph), mag * jnp.sin(ph)
    nr, ni = lr - 1.0, li
    den = ar * ar + ai * ai
    return lr, li, (nr * ar + ni * ai) / den, (ni * ar - nr * ai) / den


def _s5_disc(a_row, ldt_row, a_rep, ldt_rep, bt, ct, tile_e, mask):
    n = N_STATE

    def body(ar_ref, lr_ref, ap_ref, lp_ref, b_ref, c_ref, e_ref, m_ref, lt_ref, ltr_ref, bb_ref, cc_ref):
        lr, li, _, _ = _disc(ar_ref[0], ar_ref[1], lr_ref[...])
        pr, pi = lr, li
        rid = _row_ids(SUBLANES, n)
        for k in range(1, 9):
            for ref, sgn, edge in ((lt_ref, 1.0, 24 + k - 1), (ltr_ref, -1.0, 24 + 8 - k)):
                if k in (1, 2, 4):
                    r0 = {1: 0, 2: 8, 4: 16}[k]
                    keep = (rid >= k) if ref is lt_ref else (rid < SUBLANES - k)
                    ref[r0:r0 + 8, 0:n] = jnp.where(keep, jnp.broadcast_to(pr, (8, n)), 0.0)
                    ref[r0:r0 + 8, n:2 * n] = jnp.where(keep, jnp.broadcast_to(sgn * pi, (8, n)), 0.0)
                ref[edge:edge + 1, 0:n] = pr
                ref[edge:edge + 1, n:2 * n] = sgn * pi
            pr, pi = pr * lr - pi * li, pr * li + pi * lr
        _, _, fr, fi = _disc(ap_ref[0], ap_ref[1], lp_ref[...])
        br, bi = b_ref[0], b_ref[1]
        e = e_ref[...]
        m = m_ref[...].astype(F32)
        bb_ref[:, 0:n] = (_dot((fr * br - fi * bi).astype(BF16), e) * m).astype(BF16)
        bb_ref[:, n:2 * n] = (_dot((fr * bi + fi * br).astype(BF16), e) * m).astype(BF16)
        cc_ref[:, 0:n] = (_dot(c_ref[0].astype(BF16), e) * m).astype(BF16)
        cc_ref[:, n:2 * n] = (-(_dot(c_ref[1].astype(BF16), e) * m)).astype(BF16)

    return pl.pallas_call(
        body, name="s5_disc",
        out_shape=[jax.ShapeDtypeStruct((32, 2 * n), F32), jax.ShapeDtypeStruct((32, 2 * n), F32),
                   jax.ShapeDtypeStruct((SSM_W, 2 * n), BF16), jax.ShapeDtypeStruct((SSM_W, 2 * n), BF16)],
        compiler_params=_cparams(None),
    )(a_row, ldt_row, a_rep, ldt_rep, bt, ct, tile_e, mask)


def _dot_exact(x, sel):
    hi = x.astype(BF16)
    r1 = x - hi.astype(F32)
    mid = r1.astype(BF16)
    lo = (r1 - mid.astype(F32)).astype(BF16)
    return _dot(hi, sel) + _dot(mid, sel) + _dot(lo, sel)


def _s5_disc_bwd(a, ldt, a_rep, ldt_rep, bt, mask, dl, d_bb, d_cc, fold):
    n = N_STATE

    def body(a_ref, l_ref, ap_ref, lp_ref, b_ref, m_ref, dl_ref, dbb_ref, dcc_ref, f_ref,
             da_ref, dldt_ref, db_ref, dc_ref):
        m = m_ref[...].astype(F32)
        fold_m = f_ref[...]
        diag = lambda x: _dot_exact(jnp.where(m > 0.0, x, 0.0), fold_m)
        dr, di = diag(dbb_ref[:, 0:n]), diag(dbb_ref[:, n:2 * n])
        dc_ref[0] = diag(dcc_ref[:, 0:n])
        dc_ref[1] = -diag(dcc_ref[:, n:2 * n])
        _, _, fr, fi = _disc(ap_ref[0], ap_ref[1], lp_ref[...])
        br, bi = b_ref[0], b_ref[1]
        db_ref[0] = fr * dr + fi * di
        db_ref[1] = fr * di - fi * dr
        per_state = lambda x: x.reshape(SSM_GROUPS, SSM_GROUP, SSM_STATE).sum(axis=1)
        dfr = per_state(dr * br + di * bi)
        dfi = per_state(di * br - dr * bi)
        _, vjp = jax.vjp(_disc, a_ref[0], a_ref[1], l_ref[...])
        dar, dai, dldt = vjp((dl_ref[0], dl_ref[1], dfr, dfi))
        da_ref[0] = dar
        da_ref[1] = dai
        dldt_ref[...] = jnp.sum(dldt, axis=1, keepdims=True)

    return pl.pallas_call(
        body, name="s5_disc_bwd",
        out_shape=[jax.ShapeDtypeStruct((2, SSM_GROUPS, SSM_STATE), F32),
                   jax.ShapeDtypeStruct((SSM_GROUPS, 1), F32),
                   jax.ShapeDtypeStruct((2, SSM_W, SSM_STATE), F32),
                   jax.ShapeDtypeStruct((2, SSM_W, SSM_STATE), F32)],
        compiler_params=_cparams(None),
    )(a, ldt, a_rep, ldt_rep, bt, mask, dl, d_bb, d_cc, fold)


def _row_block(rows, cap=512):
    for bm in range(min(cap, rows), 0, -1):
        if rows % bm == 0 and (bm % 8 == 0 or bm == rows):
            return bm
    return rows


SUM_PARTS = 2


def _own_pieces(segs, rtot):
    pr = rtot // SUM_PARTS
    assert pr * SUM_PARTS == rtot and pr % 16 == 0
    offs = _seg_offsets(segs)
    pieces = [[] for _ in range(SUM_PARTS)]
    for a, (n, r) in enumerate(segs):
        for m in range(n):
            lo = offs[a] + m * r
            for h in range(SUM_PARTS):
                clo, chi = max(lo, h * pr), min(lo + r, (h + 1) * pr)
                if chi > clo:
                    pieces[h].append((a, m, clo - lo, clo - h * pr, chi - clo))
    return pieces


def _pair_rows(srcs, got_ref, segs, pieces, h, chip, own_v, got_v, sems):
    pr = own_v.shape[0]
    dev = 2 * chip + lax.axis_index("c")
    for hh in range(SUM_PARTS):
        @pl.when(h == hh)
        def _(hh=hh):
            cps = [pltpu.make_async_copy(got_ref.at[chip, pl.ds(hh * pr, pr), :], got_v, sems.at[0])]
            for i, (a, m, so, do, rows) in enumerate(pieces[hh]):
                start = pl.multiple_of(dev * segs[a][1] + so, 16)
                cps.append(pltpu.make_async_copy(srcs[a].at[m, pl.ds(start, rows), :],
                                                 own_v.at[pl.ds(do, rows), :], sems.at[1 + i]))
            for cp in cps:
                cp.start()
            for cp in cps:
                cp.wait()
    return own_v[...].astype(F32) + got_v[...].astype(F32)


def _pair_sum(fulls, got, segs):
    ns = len(segs)
    _, rtot, c = got.shape
    pieces = _own_pieces(segs, rtot)
    pr = rtot // SUM_PARTS

    def body(*refs):
        srcs = refs[:ns]
        got_ref, pbf_ref, own_v, got_v, sems = refs[ns:]
        x, y, _ = _mesh_pos()
        j = pl.program_id(1)
        chip = jnp.where(j == 0, 2 * (1 - x) + y, jnp.where(j == 1, 2 * x + 1 - y, 2 * (1 - x) + 1 - y))
        pbf_ref[0] = _pair_rows(srcs, got_ref, segs, pieces, pl.program_id(0), chip, own_v, got_v, sems).astype(BF16)

    return pl.pallas_call(
        body, name="pair_sum",
        grid=(SUM_PARTS, 3),
        in_specs=[HBM] * (ns + 1), out_specs=pl.BlockSpec((1, pr, c), lambda h, j: (j, h, 0)),
        out_shape=pltpu.HBM((3, rtot, c), BF16),
        scratch_shapes=[pltpu.VMEM((pr, c), BF16), pltpu.VMEM((pr, c), BF16),
                        pltpu.SemaphoreType.DMA((1 + max(len(p) for p in pieces),))],
        compiler_params=_cparams(("arbitrary", "arbitrary")),
    )(*_in_hbm(*fulls, got))


def _chip_sum(fulls, got, rb, segs, layer, into):
    ns = len(segs)
    _, rtot, c = got.shape
    pieces = _own_pieces(segs, rtot)
    pr = rtot // SUM_PARTS

    def body(*refs):
        srcs = refs[:ns]
        got_ref, r_ref = refs[ns], refs[ns + 1]
        s_ref, own_v, got_v, sems = refs[-4:]
        x, y, _ = _mesh_pos()
        own = _pair_rows(srcs, got_ref, segs, pieces, pl.program_id(0), 2 * x + y, own_v, got_v, sems)
        s_ref[0] = ((own + r_ref[0].astype(F32)) + r_ref[1].astype(F32)) + r_ref[2].astype(F32)

    old = [] if into is None else [into]
    return pl.pallas_call(
        body, name="chip_sum",
        grid=(SUM_PARTS,),
        in_specs=[HBM] * (ns + 1) + [pl.BlockSpec((3, pr, c), lambda h: (0, h, 0))] + [HBM] * len(old),
        out_specs=pl.BlockSpec((1, pr, c), lambda h: (layer, h, 0)),
        out_shape=jax.ShapeDtypeStruct((DEPTH, rtot, c), F32),
        input_output_aliases={ns + 2: 0} if old else {},
        scratch_shapes=[pltpu.VMEM((pr, c), BF16), pltpu.VMEM((pr, c), BF16),
                        pltpu.SemaphoreType.DMA((1 + max(len(p) for p in pieces),))],
        compiler_params=_cparams(("arbitrary",)),
    )(*_in_hbm(*fulls, got, rb), *old)


def _adamw(sets):
    ns = len(sets)
    r, c = sets[0][0].shape
    bm = _row_block(r)
    bc1 = 1.0 - ADAM_B1 ** ADAM_STEP
    bc2 = 1.0 - ADAM_B2 ** ADAM_STEP

    def body(*refs):
        for s in range(ns):
            w_ref, g_ref, m_ref, v_ref = refs[4 * s:4 * s + 4]
            d_ref, nm_ref, nv_ref = refs[4 * ns + 3 * s:4 * ns + 3 * s + 3]
            gv = g_ref[...]
            nm = ADAM_B1 * m_ref[...] + (1.0 - ADAM_B1) * gv
            nv = ADAM_B2 * v_ref[...] + (1.0 - ADAM_B2) * (gv * gv)
            nm_ref[...] = nm
            nv_ref[...] = nv
            d_ref[...] = -ADAM_LR * ((nm / bc1) / (jnp.sqrt(nv / bc2) + ADAM_EPS) + ADAM_WD * w_ref[...])

    spec = pl.BlockSpec((bm, c), lambda k: (k, 0))
    shp = jax.ShapeDtypeStruct((r, c), F32)
    flat = pl.pallas_call(
        body, name="adamw",
        grid=(r // bm,),
        in_specs=[spec] * (4 * ns), out_specs=[spec] * (3 * ns), out_shape=[shp] * (3 * ns),
        compiler_params=_cparams(("parallel",)),
    )(*_in_hbm(*[a for four in sets for a in four]))
    return [flat[3 * s:3 * s + 3] for s in range(ns)]


def _adamw_layers(sets, first, nl, prev):
    ns = len(sets)
    depth, r, c = sets[0][0].shape
    bm = _row_block(r, min(512, max(SUBLANES, (24 * MIB) // (ns * 8 * 2 * c * 4))))
    while any(four[4] is not None and four[4] % bm for four in sets):
        bm //= 2
    assert bm % SUBLANES == 0 and r % bm == 0
    bc1 = 1.0 - ADAM_B1 ** ADAM_STEP
    bc2 = 1.0 - ADAM_B2 ** ADAM_STEP

    def body(*refs):
        outs = refs[len(refs) - 4 * ns:]
        for s in range(ns):
            w_ref, m_ref, v_ref, g_ref = refs[4 * s:4 * s + 4]
            go_ref, d_ref, nm_ref, nv_ref = outs[4 * s:4 * s + 4]
            gv = g_ref[...]
            nm = ADAM_B1 * m_ref[...] + (1.0 - ADAM_B1) * gv
            nv = ADAM_B2 * v_ref[...] + (1.0 - ADAM_B2) * (gv * gv)
            go_ref[...] = gv
            nm_ref[...] = nm
            nv_ref[...] = nv
            d_ref[...] = -ADAM_LR * ((nm / bc1) / (jnp.sqrt(nv / bc2) + ADAM_EPS) + ADAM_WD * w_ref[...])

    at = pl.BlockSpec((1, bm, c), lambda i, k: (first + i, k, 0))

    def grad_spec(g_rows):
        if g_rows is None:
            return pl.BlockSpec((1, bm, c), lambda i, k: (i, k, 0))
        return pl.BlockSpec((1, bm, c), lambda i, k: (first + i, g_rows // bm + k, 0))

    shp = jax.ShapeDtypeStruct((depth, r, c), F32)
    old = [] if prev is None else [a for four in prev for a in four]
    flat = pl.pallas_call(
        body, name="adamw_layers",
        grid=(nl, r // bm),
        in_specs=[spec for four in sets for spec in (at, at, at, grad_spec(four[4]))] + [HBM] * len(old),
        out_specs=[at] * (4 * ns), out_shape=[shp] * (4 * ns),
        input_output_aliases={4 * ns + i: i for i in range(len(old))},
        compiler_params=_cparams(("parallel", "parallel")),
    )(*_in_hbm(*[a for four in sets for a in four[:4]]), *old)
    return [flat[4 * s:4 * s + 4] for s in range(ns)]


def _mesh_pos():
    return lax.axis_index("x"), lax.axis_index("y"), lax.axis_index("c")


def _dev_index(p):
    return 4 * p[0] + 2 * p[1] + p[2]


def _seg_offsets(segs):
    offs, o = [], 0
    for n, r in segs:
        offs.append(o)
        o += n * r
    return offs


def _remote(src, dst, send_sem, recv_sem, to):
    return pltpu.make_async_remote_copy(src_ref=src, dst_ref=dst, send_sem=send_sem, recv_sem=recv_sem,
                                        device_id=to, device_id_type=MESH)


def _allgather(pack, segs, name):
    rtot, c = pack.shape
    ns = len(segs)
    offs = _seg_offsets(segs)
    assert rtot == sum(n * r for n, r in segs)

    def body(pack_ref, *refs):
        outs = refs[:ns]
        send_sems, recv_sems, local_sem = refs[ns:]
        x, y, cc = _mesh_pos()
        me, sib = (x, y, cc), (x, y, 1 - cc)
        chips = [(1 - x, y), (x, 1 - y), (1 - x, 1 - y)]

        def pieces(dev, from_pack):
            res = []
            for a, (n, r) in enumerate(segs):
                for m in range(n):
                    dst = outs[a].at[m, pl.ds(pl.multiple_of(dev * r, r), r), :]
                    src = pack_ref.at[pl.ds(offs[a] + m * r, r), :] if from_pack else dst
                    res.append((src, dst))
            return res

        def push(k, dev, to, from_pack):
            for s, d in pieces(dev, from_pack):
                _remote(s, d, send_sems.at[k], recv_sems.at[k], to).start()

        def whole(k):
            return _remote(pack_ref, pack_ref, send_sems.at[k], recv_sems.at[k], me)

        my_dev = _dev_index(me)
        for s, d in pieces(my_dev, True):
            pltpu.make_async_copy(s, d, local_sem).start()
        push(0, my_dev, sib, True)
        for j, chip in enumerate(chips):
            push(1 + j, my_dev, (*chip, cc), True)
        for j, chip in enumerate(chips):
            whole(1 + j).wait_recv()
            push(4 + j, _dev_index((*chip, cc)), sib, False)
        whole(0).wait_recv()
        for j in range(3):
            whole(4 + j).wait_recv()
        for k in range(7):
            whole(k).wait_send()
        pltpu.make_async_copy(pack_ref, pack_ref, local_sem).wait()

    return pl.pallas_call(
        body, name=name,
        in_specs=[HBM], out_specs=[HBM] * ns,
        out_shape=[jax.ShapeDtypeStruct((n, N_DEV * r, c), pack.dtype) for n, r in segs],
        scratch_shapes=[pltpu.SemaphoreType.DMA((7,)), pltpu.SemaphoreType.DMA((7,)), pltpu.SemaphoreType.DMA],
    )(pack)


HBM = pl.BlockSpec(memory_space=pltpu.HBM)
SEM = pl.BlockSpec(memory_space=pltpu.SEMAPHORE)
VMEM_WHOLE = pl.BlockSpec(memory_space=pltpu.VMEM)
EFFECT = pltpu.SideEffectType.DATAFLOW_SIDE_EFFECTING


def _hbm(a):
    return pltpu.with_memory_space_constraint(a, pltpu.HBM)


def _ag_start(pack, segs, after, name):
    rtot, c = pack.shape
    ns = len(segs)
    offs = _seg_offsets(segs)

    def body(pack_ref, *refs):
        lands = refs[:ns]
        send_sems, recv_sems = refs[ns + 1], refs[ns + 2]
        token = refs[-1]
        x, y, cc = _mesh_pos()
        my_dev = _dev_index((x, y, cc))
        targets = [(x, y, 1 - cc), (1 - x, y, cc), (x, 1 - y, cc), (1 - x, 1 - y, cc)]
        for k, to in enumerate(targets):
            for a, (n, r) in enumerate(segs):
                for m in range(n):
                    _remote(pack_ref.at[pl.ds(offs[a] + m * r, r), :],
                            lands[a].at[m, pl.ds(pl.multiple_of(my_dev * r, r), r), :],
                            send_sems.at[k], recv_sems.at[k], to).start()
        token[...] = jnp.zeros_like(token)

    land_shapes = [(n, N_DEV * r, c) for n, r in segs]
    outs = pl.pallas_call(
        body, name=name,
        in_specs=[HBM] * (1 + ns) + [UNREAD],
        out_specs=[SEM, SEM, HBM] + [HBM] * ns + [VMEM_WHOLE],
        out_shape=[pltpu.SemaphoreType.DMA((4,)), pltpu.SemaphoreType.DMA((4,)), pltpu.HBM(pack.shape, pack.dtype)]
        + [pltpu.HBM(s, pack.dtype) for s in land_shapes] + [jax.ShapeDtypeStruct((SUBLANES, LANES), F32)],
        input_output_aliases={0: 2, **{1 + i: 3 + i for i in range(ns)}},
        compiler_params=pltpu.CompilerParams(has_side_effects=EFFECT),
    )(_hbm(pack), *[_hbm(lax.empty(s, pack.dtype)) for s in land_shapes], _hbm(after))
    return outs[0], outs[1], outs[2], list(outs[3:3 + ns]), outs[-1]


def _ag_wait(send_sems, recv_sems, pack, lands, after, name):
    ns = len(lands)

    def body(pack_ref, *refs):
        send_ref, recv_ref = refs[ns], refs[ns + 1]
        me = _mesh_pos()
        for k in range(4):
            whole = _remote(pack_ref, pack_ref, send_ref.at[k], recv_ref.at[k], me)
            whole.wait_send()
            whole.wait_recv()

    outs = pl.pallas_call(
        body, name=name,
        in_specs=[HBM] * (1 + ns) + [SEM, SEM, UNREAD],
        out_specs=[HBM] * (1 + ns),
        out_shape=[pltpu.HBM(pack.shape, pack.dtype)] + [pltpu.HBM(a.shape, a.dtype) for a in lands],
        input_output_aliases={i: i for i in range(1 + ns)},
        compiler_params=pltpu.CompilerParams(has_side_effects=EFFECT),
    )(pack, *lands, send_sems, recv_sems, _hbm(after))
    return outs[0], list(outs[1:])


def _ag_finish(pack, lands, segs):
    rtot, c = pack.shape
    ns = len(segs)
    offs = _seg_offsets(segs)

    def body(pack_ref, *refs):
        outs = refs[ns:2 * ns]
        stage, send_sems, recv_sems, local_sems = refs[2 * ns:]
        x, y, cc = _mesh_pos()
        me, sib = (x, y, cc), (x, y, 1 - cc)
        chips = [(1 - x, y), (x, 1 - y), (1 - x, 1 - y)]

        def rows(a, m, dev):
            return outs[a].at[m, pl.ds(pl.multiple_of(dev * segs[a][1], segs[a][1]), segs[a][1]), :]

        for j, chip in enumerate(chips):
            dev = _dev_index((*chip, cc))
            for a, (n, r) in enumerate(segs):
                for m in range(n):
                    _remote(rows(a, m, dev), rows(a, m, dev), send_sems.at[j], recv_sems.at[j], sib).start()
        load = pltpu.make_async_copy(pack_ref, stage, local_sems.at[0])
        load.start()
        load.wait()
        my_dev = _dev_index(me)
        for a, (n, r) in enumerate(segs):
            for m in range(n):
                pltpu.make_async_copy(stage.at[pl.ds(offs[a] + m * r, r), :], rows(a, m, my_dev), local_sems.at[1]).start()
        pltpu.make_async_copy(stage, pack_ref, local_sems.at[1]).wait()
        for j in range(3):
            _remote(pack_ref, pack_ref, send_sems.at[j], recv_sems.at[j], me).wait()

    outs = pl.pallas_call(
        body, name="ag_finish",
        in_specs=[HBM] * (1 + ns), out_specs=[HBM] * ns,
        out_shape=[pltpu.HBM(a.shape, a.dtype) if r >= 128 else jax.ShapeDtypeStruct(a.shape, a.dtype)
                   for a, (_, r) in zip(lands, segs)],
        input_output_aliases={1 + i: i for i in range(ns)},
        scratch_shapes=[pltpu.VMEM((rtot, c), pack.dtype), pltpu.SemaphoreType.DMA((3,)),
                        pltpu.SemaphoreType.DMA((3,)), pltpu.SemaphoreType.DMA((2,))],
        compiler_params=_cparams(None, 16),
    )(pack, *lands)
    return list(outs)


def _rs_chips_start(pbf, after, name):
    _, rtot, c = pbf.shape

    def body(pbf_ref, land_ref, after_ref, send_sems, recv_sems, pbf_thru, land_thru, token):
        x, y, cc = _mesh_pos()
        for j, (cx, cy) in enumerate([(1 - x, y), (x, 1 - y), (1 - x, 1 - y)]):
            _remote(pbf_ref.at[j], land_ref.at[j], send_sems.at[j], recv_sems.at[j], (cx, cy, cc)).start()
        token[...] = jnp.zeros_like(token)

    return pl.pallas_call(
        body, name=name,
        in_specs=[HBM, HBM, UNREAD],
        out_specs=[SEM, SEM, HBM, HBM, VMEM_WHOLE],
        out_shape=[pltpu.SemaphoreType.DMA((3,)), pltpu.SemaphoreType.DMA((3,)), pltpu.HBM(pbf.shape, pbf.dtype),
                   pltpu.HBM((3, rtot, c), pbf.dtype), jax.ShapeDtypeStruct((SUBLANES, LANES), F32)],
        input_output_aliases={0: 2, 1: 3},
        compiler_params=pltpu.CompilerParams(has_side_effects=EFFECT),
    )(_hbm(pbf), _hbm(lax.empty((3, rtot, c), pbf.dtype)), _hbm(after))


def _rs_chips_wait(send_sems, recv_sems, pbf, land, after, name):
    def body(pbf_ref, land_ref, send_ref, recv_ref, after_ref, pbf_out, land_out):
        me = _mesh_pos()
        for j in range(3):
            cp = _remote(pbf_ref.at[0], land_ref.at[j], send_ref.at[j], recv_ref.at[j], me)
            cp.wait_send()
            cp.wait_recv()

    return pl.pallas_call(
        body, name=name,
        in_specs=[HBM, HBM, SEM, SEM, UNREAD], out_specs=[HBM, HBM],
        out_shape=[pltpu.HBM(pbf.shape, pbf.dtype), pltpu.HBM(land.shape, land.dtype)],
        input_output_aliases={0: 0, 1: 1},
        compiler_params=pltpu.CompilerParams(has_side_effects=EFFECT),
    )(pbf, land, send_sems, recv_sems, _hbm(after))[1]


def _flips():
    return [(dx, dy, dc) for dx in (0, 1) for dy in (0, 1) for dc in (0, 1) if dx or dy or dc]


def _small_gather_start(flat, name):
    r, c = flat.shape

    def body(flat_ref, land_ref, send_sems, recv_sems, flat_thru, land_thru, token):
        x, y, cc = _mesh_pos()
        mine = land_ref.at[_dev_index((x, y, cc))]
        for k, (dx, dy, dc) in enumerate(_flips()):
            to = (1 - x if dx else x, 1 - y if dy else y, 1 - cc if dc else cc)
            _remote(flat_ref, mine, send_sems.at[k], recv_sems.at[k], to).start()
        token[...] = jnp.zeros_like(token)

    return pl.pallas_call(
        body, name=name,
        in_specs=[HBM, HBM],
        out_specs=[SEM, SEM, HBM, HBM, VMEM_WHOLE],
        out_shape=[pltpu.SemaphoreType.DMA((7,)), pltpu.SemaphoreType.DMA((7,)), pltpu.HBM(flat.shape, flat.dtype),
                   pltpu.HBM((N_DEV, r, c), flat.dtype), jax.ShapeDtypeStruct((SUBLANES, LANES), F32)],
        input_output_aliases={0: 2, 1: 3},
        compiler_params=pltpu.CompilerParams(has_side_effects=EFFECT),
    )(_hbm(flat), _hbm(lax.empty((N_DEV, r, c), flat.dtype)))


def _small_gather_wait(send_sems, recv_sems, flat, land, after, name):
    def body(flat_ref, land_ref, send_ref, recv_ref, after_ref, flat_out, land_out):
        me = _mesh_pos()
        for k in range(N_DEV - 1):
            cp = _remote(flat_ref, land_ref.at[0], send_ref.at[k], recv_ref.at[k], me)
            cp.wait_send()
            cp.wait_recv()

    return pl.pallas_call(
        body, name=name,
        in_specs=[HBM, HBM, SEM, SEM, UNREAD], out_specs=[HBM, HBM],
        out_shape=[pltpu.HBM(flat.shape, flat.dtype), pltpu.HBM(land.shape, land.dtype)],
        input_output_aliases={0: 0, 1: 1},
        compiler_params=pltpu.CompilerParams(has_side_effects=EFFECT),
    )(flat, land, send_sems, recv_sems, _hbm(after))


def _sum_devices(land, own):
    _, r, c = land.shape

    def body(land_ref, own_ref, out_ref):
        me = _dev_index(_mesh_pos())
        total = None
        for d in range(N_DEV):
            other = land_ref[jnp.where(d == me, (d + 1) % N_DEV, d)]
            block = jnp.where(d == me, own_ref[...], other)
            total = block if total is None else total + block
        out_ref[...] = total

    return pl.pallas_call(
        body, name="sum_devices",
        grid=(1,),
        in_specs=[pl.BlockSpec((N_DEV, r, c), lambda i: (0, 0, 0)), pl.BlockSpec((r, c), lambda i: (0, 0))],
        out_specs=pl.BlockSpec((r, c), lambda i: (0, 0)),
        out_shape=jax.ShapeDtypeStruct((r, c), F32),
        compiler_params=_cparams(("arbitrary",)),
    )(land, own)


def _rs_sibling_start(fulls, segs, name):
    ns = len(segs)
    offs = _seg_offsets(segs)
    rtot = sum(n * r for n, r in segs)
    c = fulls[0].shape[-1]
    dt = fulls[0].dtype

    def body(*refs):
        srcs = refs[:ns]
        land_ref, send_sem, recv_sem = refs[ns], refs[ns + 1], refs[ns + 2]
        token = refs[-1]
        x, y, cc = _mesh_pos()
        for k in range(4):
            for a, (n, r) in enumerate(segs):
                for m in range(n):
                    theirs = srcs[a].at[m, pl.ds(pl.multiple_of((2 * k + 1 - cc) * r, r), r), :]
                    _remote(theirs, land_ref.at[k, pl.ds(offs[a] + m * r, r), :], send_sem, recv_sem,
                            (x, y, 1 - cc)).start()
        token[...] = jnp.zeros_like(token)

    outs = pl.pallas_call(
        body, name=name,
        in_specs=[HBM] * (ns + 1),
        out_specs=[SEM, SEM] + [HBM] * (ns + 1) + [VMEM_WHOLE],
        out_shape=[pltpu.SemaphoreType.DMA(()), pltpu.SemaphoreType.DMA(())]
        + [pltpu.HBM(a.shape, a.dtype) for a in fulls] + [pltpu.HBM((4, rtot, c), dt),
                                                           jax.ShapeDtypeStruct((SUBLANES, LANES), F32)],
        input_output_aliases={i: 2 + i for i in range(ns + 1)},
        compiler_params=pltpu.CompilerParams(has_side_effects=EFFECT),
    )(*[_hbm(a) for a in fulls], _hbm(lax.empty((4, rtot, c), dt)))
    return outs[0], outs[1], list(outs[2:2 + ns]), outs[2 + ns], outs[-1]


def _rs_sibling_wait(send_sem, recv_sem, fulls, land, after, name):
    ns = len(fulls)

    def body(*refs):
        land_ref, send_ref, recv_ref = refs[ns], refs[ns + 1], refs[ns + 2]
        whole = _remote(land_ref, land_ref, send_ref, recv_ref, _mesh_pos())
        whole.wait_send()
        whole.wait_recv()

    outs = pl.pallas_call(
        body, name=name,
        in_specs=[HBM] * (ns + 1) + [SEM, SEM, UNREAD], out_specs=[HBM] * (ns + 1),
        out_shape=[pltpu.HBM(a.shape, a.dtype) for a in fulls] + [pltpu.HBM(land.shape, land.dtype)],
        input_output_aliases={i: i for i in range(ns + 1)},
        compiler_params=pltpu.CompilerParams(has_side_effects=EFFECT),
    )(*fulls, land, send_sem, recv_sem, _hbm(after))
    return list(outs[:ns]), outs[ns]


def _tp(w):
    return jnp.swapaxes(w, -1, -2)


def _s5_prepare(a_re, a_im, log_dt, b_re, b_im, c_re, c_im):
    a = jnp.stack([a_re, a_im], axis=1)
    ldt = jnp.broadcast_to(log_dt[:, :, None], (DEPTH, SSM_GROUPS, SSM_STATE))
    a_row = a.reshape(DEPTH, 2, 1, N_STATE)
    ldt_row = ldt.reshape(DEPTH, 1, N_STATE)
    a_rep = jnp.repeat(a, SSM_GROUP, axis=2)
    ldt_rep = jnp.repeat(ldt, SSM_GROUP, axis=1)
    bt = jnp.stack([_tp(b_re), _tp(b_im)], axis=1).reshape(DEPTH, 2, SSM_W, SSM_STATE)
    ct = jnp.stack([c_re, c_im], axis=1).reshape(DEPTH, 2, SSM_W, SSM_STATE)
    tile_e = jnp.tile(jnp.eye(SSM_STATE, dtype=BF16), (1, SSM_GROUPS))
    mask = jnp.repeat(jnp.repeat(jnp.eye(SSM_GROUPS, dtype=BF16), SSM_GROUP, axis=0), SSM_STATE, axis=1)
    out = []
    for l in range(DEPTH):
        tabs = _s5_disc(a_row[l], ldt_row[l], a_rep[l], ldt_rep[l], bt[l], ct[l], tile_e, mask)
        out.append(((a[l], ldt[l], a_rep[l], ldt_rep[l], bt[l], mask), *tabs))
    return out


def _layer_fwd(h, p_l, small, big, arrive=None):
    saved = {'h0': h}
    if arrive is not None:
        arrive(0, h)
    h, saved['gu1'] = _ffn_fwd(h, small['ffn1_norm'], big['ff1'])
    saved['h1'] = h
    if arrive is not None:
        arrive(1, h)
    z = _inproj_fwd(h, small['mix_norm'], big['wint'])
    ya, ys, hs = _s5conv_fwd(z, small['conv_w'], small['conv_b'], small['bbmat'], small['ccmat'], small['dvec'],
                             small['ltab'])
    saved.update(z=z, ya=ya, ys=ys, hs=hs)
    h = _mix_out_fwd(h, ya, ys, big['glu'], small['glu_b'], small['conv_out_norm'], small['ssm_out_norm'], big['wout'])
    saved['h2'] = h
    if arrive is not None:
        arrive(2, h)
    h, saved['gu2'] = _ffn_fwd(h, small['ffn2_norm'], big['ff2'])
    saved['h3'] = h
    h, *saved['ple'] = _ple_fwd(h, small['ple_norm'], p_l, big['plg'], big['plpt'])
    return h, saved


def _ffn_bwd(h_in, g, dh, gu, w3):
    dh_in, dga, ud, dg = _ffn_bwd_act(h_in, g, dh, gu, w3)
    return dh_in, _matmul_tn(dga, ud, FF_BLOCK, BF16, "ffn_wgrad"), dg


def _layer_bwd_top(dh, small, big, saved):
    gs = {}
    u, pb, q, pp = saved['ple']
    dh, dq, dpp, gs['ple_norm'] = _ple_bwd(saved['h3'], small['ple_norm'], dh, q, pp, big['plg'])
    d_plg = _matmul_tn(u, dq, 256, BF16, "ple_gate_wgrad")
    d_plpt = _matmul_tn(dpp, pb, 256, BF16, "ple_proj_wgrad", to_kernel=False)
    dh, d_ff2, gs['ffn2_norm'] = _ffn_bwd(saved['h2'], small['ffn2_norm'], dh, saved['gu2'], big['ff2'])
    return dh, (gs, d_plg, d_plpt, d_ff2)


def _layer_bwd_rest(dh, top, small, big, saved):
    gs, d_plg, d_plpt, d_ff2 = top
    dya, dys, ycat, dhb, zg, dq, part = _mix_out_bwd(dh, saved['ya'], saved['ys'], big['glu'], small['glu_b'],
                                                     small['conv_out_norm'], small['ssm_out_norm'], big['wout'])
    d_wout = _matmul_tn(ycat, dhb, 256, BF16, "w_out_wgrad")
    d_glu = _matmul_tn(zg, dq, 256, BF16, "glu_wgrad", to_kernel=False)
    dz, gadj, us, dyb, dl, dcw = _s5conv_bwd(saved['z'], saved['hs'], dya, dys, small['conv_w'], small['conv_b'],
                                             small['bbmat'], small['ccmat'], small['dvec'], small['ltab_rev'])
    d_bb = _block_wgrad(us, gadj, "s5_b_wgrad")
    d_cc = _block_wgrad(dyb, saved['hs'][None], "s5_c_wgrad")
    dh, u, gs['mix_norm'] = _inproj_bwd(saved['h1'], small['mix_norm'], dh, dz, big['wint'])
    d_wint = _matmul_tn(dz[None], u, 256, BF16, "w_in_wgrad")
    dh, d_ff1, gs['ffn1_norm'] = _ffn_bwd(saved['h0'], small['ffn1_norm'], dh, saved['gu1'], big['ff1'])

    dlb = dl[0].reshape(2, SSM_GROUPS, SSM_STATE)
    fold = jnp.tile(jnp.eye(SSM_STATE, dtype=BF16), (SSM_GROUPS, 1))
    da, dldt, dbt, dct = _s5_disc_bwd(*small['disc_in'], dlb, d_bb, d_cc, fold)
    gs['ssm_A_re'], gs['ssm_A_im'] = da[0], da[1]
    gs['ssm_log_dt'] = dldt[:, 0]
    ghp = (SSM_GROUPS, SSM_GROUP, SSM_STATE)
    gs['ssm_B_re'], gs['ssm_B_im'] = dbt[0].reshape(ghp), dbt[1].reshape(ghp)
    gs['ssm_C_re'], gs['ssm_C_im'] = dct[0].reshape(ghp), dct[1].reshape(ghp)
    gs['conv_w'] = dcw[0:3]
    gs['conv_b'] = dcw[3]
    gs['ssm_D'] = dcw[4].reshape(SSM_GROUPS, SSM_GROUP)
    gs['conv_out_norm'], gs['ssm_out_norm'], gs['glu_b'] = part[0], part[1], part[2]
    for n in ('ple_norm', 'ffn2_norm', 'mix_norm', 'ffn1_norm'):
        gs[n] = gs[n][0]
    fulls = [d_ff1, d_ff2, d_wint, d_wout, d_plg,
             d_plpt.reshape(1, D_MODEL * PLE_DIM // D_MODEL, D_MODEL), d_glu.reshape(1, SSM_W * SSM_W // D_MODEL, D_MODEL)]
    return dh, fulls, gs


VIEW_T = ('ffn1_w_gate', 'ffn1_w_up', 'ffn2_w_gate', 'ffn2_w_up', 'ssm_B_re', 'ssm_B_im')


def _view(name, a):
    return _tp(a) if name in VIEW_T else a


SEG_NAMES = ('ff1', 'ff2', 'wint', 'wout', 'plg', 'plpt', 'glu')
FIRST_LAYER_GROUPS = ((0,), (2, 3, 6), (1, 4, 5))


def _layer_pack(W, l, segments=range(len(SEGS))):
    pieces = {
        0: lambda: [_tp(W['ffn1_w_gate'][l]), _tp(W['ffn1_w_up'][l]), W['ffn1_w_down'][l]],
        1: lambda: [_tp(W['ffn2_w_gate'][l]), _tp(W['ffn2_w_up'][l]), W['ffn2_w_down'][l]],
        2: lambda: [_tp(W['w_in'][l])],
        3: lambda: [W['w_out'][l]],
        4: lambda: [W['ple_w_gate'][l]],
        5: lambda: [_tp(W['ple_w_proj'][l]).reshape(-1, D_MODEL)],
        6: lambda: [W['glu_w'][l].reshape(-1, D_MODEL)],
    }
    return jnp.concatenate([a for s in segments for a in pieces[s]()], axis=0).astype(BF16)


def _as_big(named):
    shape = dict(plpt=(D_MODEL, PLE_DIM), glu=(SSM_W, SSM_W))
    return {n: (a.reshape(shape[n]) if n in shape else a) for n, a in named.items()}


def _pad_rows(flat, mult, width=LANES):
    per = mult * width
    n = flat.shape[0]
    tot = -(-n // per) * per
    return jnp.pad(flat, (0, tot - n)).reshape(tot // width, width)


def _adamw_any(names, w, g, m, v):
    two = lambda t: t.reshape(-1, t.shape[-1])
    groups = {}
    for n in names:
        groups.setdefault(two(w[n]).shape, []).append(n)
    out = ({}, {}, {})
    for ns in groups.values():
        done = _adamw([(two(w[n]), two(g[n]), two(m[n]), two(v[n])) for n in ns])
        for n, three in zip(ns, done):
            for dst, t in zip(out, three):
                dst[n] = t.reshape(w[n].shape)
    return out


def kernel(x, p, ffn1_norm, ffn1_w_gate, ffn1_w_up, ffn1_w_down, mix_norm, w_in, conv_w, conv_b, ssm_A_re, ssm_A_im, ssm_B_re, ssm_B_im, ssm_C_re, ssm_C_im, ssm_D, ssm_log_dt, glu_w, glu_b, conv_out_norm, ssm_out_norm, w_out, ffn2_norm, ffn2_w_gate, ffn2_w_up, ffn2_w_down, ple_norm, ple_w_gate, ple_w_proj, final_norm, loss_target, m_ffn1_norm, m_ffn1_w_gate, m_ffn1_w_up, m_ffn1_w_down, m_mix_norm, m_w_in, m_conv_w, m_conv_b, m_ssm_A_re, m_ssm_A_im, m_ssm_B_re, m_ssm_B_im, m_ssm_C_re, m_ssm_C_im, m_ssm_D, m_ssm_log_dt, m_glu_w, m_glu_b, m_conv_out_norm, m_ssm_out_norm, m_w_out, m_ffn2_norm, m_ffn2_w_gate, m_ffn2_w_up, m_ffn2_w_down, m_ple_norm, m_ple_w_gate, m_ple_w_proj, m_final_norm, v_ffn1_norm, v_ffn1_w_gate, v_ffn1_w_up, v_ffn1_w_down, v_mix_norm, v_w_in, v_conv_w, v_conv_b, v_ssm_A_re, v_ssm_A_im, v_ssm_B_re, v_ssm_B_im, v_ssm_C_re, v_ssm_C_im, v_ssm_D, v_ssm_log_dt, v_glu_w, v_glu_b, v_conv_out_norm, v_ssm_out_norm, v_w_out, v_ffn2_norm, v_ffn2_w_gate, v_ffn2_w_up, v_ffn2_w_down, v_ple_norm, v_ple_w_gate, v_ple_w_proj, v_final_norm):
    given = dict(locals())
    W = {n: given[n] for n in W_NAMES}
    M = {n: given['m_' + n] for n in W_NAMES}
    V = {n: given['v_' + n] for n in W_NAMES}
    Wv, Mv, Vv = [{n: _view(n, d[n]) for n in W_NAMES} for d in (W, M, V)]
    my_dev = _dev_index(_mesh_pos())

    conv_shard = _pad_rows(W['conv_w'].reshape(-1), SUBLANES)
    conv_all = _allgather(conv_shard, ((1, SUBLANES),), "ag_conv_w")[0]
    conv_full = conv_all.reshape(N_DEV, -1)[:, :DEPTH * 3 * (CONV_W // N_DEV)]
    conv_full = conv_full.reshape(N_DEV, DEPTH, 3, CONV_W // N_DEV).transpose(1, 2, 0, 3).reshape(DEPTH, 3, CONV_W)
    first, after = [], conv_all
    for gi, segments in enumerate(FIRST_LAYER_GROUPS):
        first.append(_ag_start(_layer_pack(W, 0, segments), tuple(SEGS[s] for s in segments), after,
                               "ag_start_0%s" % "abc"[gi]))
        after = first[-1][4]
    packs = [None] + [_layer_pack(W, l) for l in range(1, DEPTH)]
    flights = {1: _ag_start(packs[1], SEGS, after, "ag_start_1")}
    after = flights[1][4]
    s5 = _s5_prepare(*[W[n] + after[0, 0] for n in ('ssm_A_re', 'ssm_A_im', 'ssm_log_dt')],
                     *[W[n] for n in ('ssm_B_re', 'ssm_B_im', 'ssm_C_re', 'ssm_C_im')])
    prepared = conv_full[0, 0:1, 0:1] + s5[DEPTH - 1][1][0:1, 0:1] + packs[DEPTH - 1][0:1, 0:1].astype(F32)

    smalls, saves, bigs = [], [], []
    h = x[0]

    def gathered(handles, segments, after, name, next_layer=None, gate=None):
        send_sems, recv_sems, pack_thru, lands, _ = handles
        pack_thru, lands = _ag_wait(send_sems, recv_sems, pack_thru, lands, after, "ag_wait_" + name)
        if next_layer is not None:
            flights[next_layer] = _ag_start(packs[next_layer], SEGS, pack_thru, "ag_start_%d" % next_layer)
            gate[0][gate[1]] = gate[0][gate[1]] + flights[next_layer][4][0:1, 0:1]
        outs = _ag_finish(pack_thru, lands, tuple(SEGS[s] for s in segments))
        return _as_big({SEG_NAMES[s]: a for s, a in zip(segments, outs)})

    for l in range(DEPTH):
        small = {n: W[n][l][None] for n in ('ffn1_norm', 'mix_norm', 'conv_b', 'glu_b', 'conv_out_norm',
                                            'ssm_out_norm', 'ffn2_norm', 'ple_norm')}
        small['conv_w'] = conv_full[l]
        small['dvec'] = W['ssm_D'][l].reshape(1, SSM_W)
        small['disc_in'], small['ltab'], small['ltab_rev'], small['bbmat'], small['ccmat'] = s5[l]
        big = {}
        bigs.append(big)
        if l == 0:
            def arrive(stage, h_now, big=big, small=small):
                big.update(gathered(first[stage], FIRST_LAYER_GROUPS[stage], prepared if stage == 0 else h_now,
                                    "0%s" % "abc"[stage], *((2, (small, 'ffn2_norm')) if stage == 2 else ())))
            h, saved = _layer_fwd(h, p[l, 0], small, big, arrive)
        else:
            nxt = (l + 2, (small, 'ffn1_norm')) if l + 2 < DEPTH else ()
            big.update(gathered(flights[l], range(len(SEGS)), h, "%d" % l, *nxt))
            h, saved = _layer_fwd(h, p[l, 0], small, big)
        smalls.append(small)
        saves.append(saved)
    loss_tile, dh, d_final = _final_loss(h, W['final_norm'][None], loss_target[0])
    loss = lax.psum(loss_tile[0, 0], ("x", "y", "c"))

    layer_gs = [None] * DEPTH
    shard_grads = None
    sib, ici = None, None

    def finish_sibling(after_sib, after_ici):
        nonlocal sib, ici
        up, (send_sem, recv_sem, fulls_thru, land, _) = sib
        fulls_thru, got = _rs_sibling_wait(send_sem, recv_sem, fulls_thru, land, after_sib, "sib_wait_%d" % up)
        pbf = _pair_sum(fulls_thru, got, SEGS)
        done = finish_chips(after_ici)
        ici = (up, _rs_chips_start(pbf, after_ici if done is None else done, "rs_start_%d" % up), fulls_thru, got)
        sib = None

    def finish_chips(after):
        nonlocal ici, shard_grads
        if ici is None:
            return None
        up, (send_sems, recv_sems, pbf_thru, land, _), fulls_up, got_up = ici
        got3 = _rs_chips_wait(send_sems, recv_sems, pbf_thru, land, after, "rs_wait_%d" % up)
        shard_grads = _chip_sum(fulls_up, got_up, got3, SEGS, up, shard_grads)
        ici = None
        return shard_grads

    layer_names = [n for n in SMALL_NAMES if n != 'final_norm']
    small_flights = [None] * DEPTH
    for l in reversed(range(DEPTH)):
        small = dict(smalls[l])
        if sib is not None:
            small['ple_norm'] = small['ple_norm'] + sib[1][4][0:1, 0:1] + small_flights[l + 1][4][0:1, 0:1]
        dh, top = _layer_bwd_top(dh, small, bigs[l], saves[l])
        if sib is not None:
            finish_sibling(dh, dh)
            small['glu_b'] = small['glu_b'] + ici[1][4][0:1, 0:1]
        dh, fulls, layer_gs[l] = _layer_bwd_rest(dh, top, small, bigs[l], saves[l])
        sib = (l, _rs_sibling_start(fulls, SEGS, "sib_start_%d" % l))
        last_slot = d_final[0] if l == DEPTH - 1 else jnp.zeros((D_MODEL,), F32)
        flat = jnp.concatenate([layer_gs[l][n].reshape(-1) for n in layer_names + ['conv_w']] + [last_slot])
        small_flights[l] = _small_gather_start(_pad_rows(flat, SUBLANES, D_MODEL), "small_start_%d" % l)
    grad_x = dh[None]
    finish_sibling(small_flights[0][4], small_flights[0][4])

    reduced = []
    for l in range(DEPTH):
        send_sems, recv_sems, flat_thru, land, _ = small_flights[l]
        flat_thru, land = _small_gather_wait(send_sems, recv_sems, flat_thru, land, ici[1][4], "small_wait_%d" % l)
        reduced.append(_sum_devices(land, flat_thru).reshape(-1))
    reduced = jnp.stack(reduced)
    G = {}
    o = 0
    for n in layer_names + ['conv_w']:
        size = (W[n].size if n != 'conv_w' else DEPTH * 3 * CONV_W) // DEPTH
        shape = Wv[n].shape if n != 'conv_w' else (DEPTH, 3, CONV_W)
        G[n] = reduced[:, o:o + size].reshape(shape)
        o += size
    G['final_norm'] = reduced[DEPTH - 1, o:o + D_MODEL]
    G['conv_w'] = lax.dynamic_slice_in_dim(G['conv_w'], my_dev * (CONV_W // N_DEV), CONV_W // N_DEV, axis=2)

    delta, new_m, new_v = _adamw_any(SMALL_NAMES + ['conv_w'], Wv, G, Mv, Vv)

    offs = _seg_offsets(SEGS)
    r = SEGS[0][1]
    packed_rows = {'w_out': offs[3], 'ple_w_gate': offs[4]}
    for a, f in ((0, 'ffn1'), (1, 'ffn2')):
        packed_rows.update({f + '_w_gate': offs[a], f + '_w_up': offs[a] + r, f + '_w_down': offs[a] + 2 * r})

    def relaid(sg):
        nl = sg.shape[0]
        return {'w_in': _tp(sg[:, offs[2]:offs[2] + SEGS[2][1]]),
                'ple_w_proj': _tp(sg[:, offs[5]:offs[5] + SEGS[5][1]].reshape(nl, D_MODEL // N_DEV, PLE_DIM)),
                'glu_w': sg[:, offs[6]:offs[6] + SEGS[6][1]].reshape(nl, SSM_W // N_DEV, SSM_W)}

    groups = {}
    for n in list(packed_rows) + ['w_in', 'ple_w_proj', 'glu_w']:
        groups.setdefault(Wv[n].shape, []).append(n)

    def update(first, nl, prev):
        other = relaid(shard_grads[first:first + nl])
        sets = lambda ns: [(Wv[n], Mv[n], Vv[n], shard_grads, packed_rows[n]) if n in packed_rows
                           else (Wv[n], Mv[n], Vv[n], other[n], None) for n in ns]
        return {shape: _adamw_layers(sets(ns), first, nl, None if prev is None else prev[shape])
                for shape, ns in groups.items()}

    part = update(1, DEPTH - 1, None)
    finish_chips(sum(four[3][1, 0:1, 0:1] for fours in part.values() for four in fours)
                 + sum(new_v[n][(0,) * new_v[n].ndim].reshape(1, 1) for n in SMALL_NAMES + ['conv_w']))
    for shape, fours in update(0, 1, part).items():
        for n, four in zip(groups[shape], fours):
            G[n], delta[n], new_m[n], new_v[n] = four

    outs = [[_view(n, d[n]) for n in W_NAMES] for d in (G, delta, new_m, new_v)]
    return (loss, grad_x, *outs[0], *outs[1], *outs[2], *outs[3])
```

```python
import math

import jax
import jax.numpy as jnp
from jax import lax
from jax.experimental import pallas as pl
from jax.experimental.pallas import tpu as pltpu

F32 = jnp.float32
BF16 = jnp.bfloat16

N_DEV = 8
DEPTH = 4
SEQ = 2048
D_MODEL = 1024
D_FF = 2816
CONV_W = 512
SSM_W = 512
SSM_GROUPS = 32
SSM_GROUP = 16
SSM_STATE = 64
N_STATE = SSM_GROUPS * SSM_STATE
IN_COLS = 2048
PLE_DIM = 256
EPS = 1e-6

ADAM_LR = 0.001
ADAM_B1 = 0.9
ADAM_B2 = 0.999
ADAM_EPS = 1e-08
ADAM_WD = 0.01
ADAM_STEP = 10

FF_BLOCK = 256
N_FF_BLOCKS = D_FF // FF_BLOCK
TOK_TILE_FFN_FWD = 2048
TOK_TILE_FFN_BWD = 1024
TOK_TILE = 512
CHUNK = 256
N_CHUNKS = SEQ // CHUNK
LANE_GROUP = 512
SUBLANES = 8
LANES = 128
MIB = 1024 * 1024

W_NAMES = ['ffn1_norm', 'ffn1_w_gate', 'ffn1_w_up', 'ffn1_w_down', 'mix_norm', 'w_in', 'conv_w', 'conv_b',
           'ssm_A_re', 'ssm_A_im', 'ssm_B_re', 'ssm_B_im', 'ssm_C_re', 'ssm_C_im', 'ssm_D', 'ssm_log_dt',
           'glu_w', 'glu_b', 'conv_out_norm', 'ssm_out_norm', 'w_out', 'ffn2_norm', 'ffn2_w_gate', 'ffn2_w_up',
           'ffn2_w_down', 'ple_norm', 'ple_w_gate', 'ple_w_proj', 'final_norm']
SMALL_NAMES = ['ffn1_norm', 'mix_norm', 'conv_b', 'ssm_A_re', 'ssm_A_im', 'ssm_B_re', 'ssm_B_im', 'ssm_C_re',
               'ssm_C_im', 'ssm_D', 'ssm_log_dt', 'glu_b', 'conv_out_norm', 'ssm_out_norm', 'ffn2_norm',
               'ple_norm', 'final_norm']

SEGS = ((3, 352), (3, 352), (1, 256), (1, 128), (1, 128), (1, 32), (1, 32))
PACK_ROWS = sum(n * r for n, r in SEGS)

MESH = pl.DeviceIdType.MESH
UNREAD = pl.BlockSpec(memory_space=pltpu.HBM)


def _in_hbm(*arrays):
    return [pltpu.with_memory_space_constraint(a, pltpu.HBM) for a in arrays]


def _out_hbm(outs, which):
    if not isinstance(outs, (list, tuple)):
        return pltpu.with_memory_space_constraint(outs, pltpu.HBM) if which else outs
    return [pltpu.with_memory_space_constraint(a, pltpu.HBM) if i in which else a for i, a in enumerate(outs)]


def _cparams(sem=None, vmem_mib=48, **kw):
    return pltpu.CompilerParams(dimension_semantics=sem, vmem_limit_bytes=vmem_mib * MIB, **kw)


def _dot(a, b):
    return jnp.dot(a, b, preferred_element_type=F32)


def _dot_nt(a, b):
    return lax.dot_general(a, b, (((1,), (1,)), ((), ())), preferred_element_type=F32)


def _dot_tn(a, b):
    return lax.dot_general(a, b, (((0,), (0,)), ((), ())), preferred_element_type=F32)


def _rms_stats(x):
    r = lax.rsqrt(jnp.mean(x * x, axis=-1, keepdims=True) + EPS)
    return x * r, r


def _rms_bwd(dy, xh, r, g):
    dxh = dy * g
    dx = r * (dxh - xh * jnp.mean(dxh * xh, axis=-1, keepdims=True))
    dg = jnp.sum(dy * xh, axis=0, keepdims=True)
    return dx, dg


def _sigmoid(x):
    return 0.5 * jnp.tanh(0.5 * x) + 0.5


_GELU_C = math.sqrt(2.0 / math.pi)


def _gelu(x):
    t = jnp.tanh(_GELU_C * (x + 0.044715 * x * x * x))
    return 0.5 * x * (1.0 + t), t


def _gelu_grad(x, t):
    return 0.5 * (1.0 + t) + 0.5 * x * (1.0 - t * t) * _GELU_C * (1.0 + 3.0 * 0.044715 * x * x)


def _accumulate(ref, first, value):
    @pl.when(first)
    def _():
        ref[...] = value

    @pl.when(jnp.logical_not(first))
    def _():
        ref[...] += value


def _ffn_fwd(h, g, w3):
    tm = TOK_TILE_FFN_FWD
    last = N_FF_BLOCKS - 1

    def body(h_ref, g_ref, wgu_ref, wd_ref, wd_last_ref, out_ref, gu_ref, u_ref, a_ref):
        k = pl.program_id(1)

        @pl.when(k == 0)
        def _():
            x = h_ref[...]
            xh, _ = _rms_stats(x)
            u_ref[...] = (xh * g_ref[...]).astype(BF16)
            out_ref[...] = x
            a_ref[1] = jnp.zeros((tm, FF_BLOCK), BF16)

        out_ref[...] += 0.5 * _dot(a_ref[(k + 1) % 2], wd_ref[0])
        gu = _dot_nt(u_ref[...], wgu_ref[...].reshape(2 * FF_BLOCK, D_MODEL))
        gate, up = gu[:, :FF_BLOCK], gu[:, FF_BLOCK:]
        a_ref[k % 2] = (gate * _sigmoid(gate) * up).astype(BF16)
        gu_ref[0] = gate.astype(BF16)
        gu_ref[1] = up.astype(BF16)

        @pl.when(k == last)
        def _():
            out_ref[...] += 0.5 * _dot(a_ref[last % 2], wd_last_ref[0])

    return _out_hbm(pl.pallas_call(
        body, name="ffn_fwd",
        grid=(SEQ // tm, N_FF_BLOCKS),
        in_specs=[pl.BlockSpec((tm, D_MODEL), lambda m, k: (m, 0), pipeline_mode=pl.Buffered(1)),
                  pl.BlockSpec((1, D_MODEL), lambda m, k: (0, 0)),
                  pl.BlockSpec((2, FF_BLOCK, D_MODEL), lambda m, k: (0, k, 0)),
                  pl.BlockSpec((1, FF_BLOCK, D_MODEL), lambda m, k: (2, jnp.maximum(k - 1, 0), 0)),
                  pl.BlockSpec((1, FF_BLOCK, D_MODEL), lambda m, k: (2, last, 0), pipeline_mode=pl.Buffered(1))],
        out_specs=[pl.BlockSpec((tm, D_MODEL), lambda m, k: (m, 0)),
                   pl.BlockSpec((2, tm, FF_BLOCK), lambda m, k: (0, m, k))],
        out_shape=[jax.ShapeDtypeStruct((SEQ, D_MODEL), F32),
                   pltpu.HBM((2, SEQ, D_FF), BF16)],
        scratch_shapes=[pltpu.VMEM((tm, D_MODEL), BF16), pltpu.VMEM((2, tm, FF_BLOCK), BF16)],
        compiler_params=_cparams(("parallel", "arbitrary"), 56),
    )(*_in_hbm(h, g, w3, w3, w3)), (1,))


def _ffn_bwd_act(h, g, dout, gu, w3):
    tm = TOK_TILE_FFN_BWD
    last = N_FF_BLOCKS - 1

    def body(h_ref, g_ref, d_ref, gu_ref, wd_ref, wgu_ref, wgu_last_ref, dh_ref, dga_ref, ud_ref, dg_ref,
             acc_ref, dgu_ref):
        m = pl.program_id(0)
        k = pl.program_id(1)

        @pl.when(k == 0)
        def _():
            xh, _ = _rms_stats(h_ref[...])
            ud_ref[0] = (xh * g_ref[...]).astype(BF16)
            ud_ref[1] = (0.5 * d_ref[...]).astype(BF16)
            acc_ref[...] = jnp.zeros_like(acc_ref)
            dgu_ref[1] = jnp.zeros((tm, 2 * FF_BLOCK), BF16)

        acc_ref[...] += _dot(dgu_ref[(k + 1) % 2], wgu_ref[...].reshape(2 * FF_BLOCK, D_MODEL))
        gate = gu_ref[0].astype(F32)
        up = gu_ref[1].astype(F32)
        sg = _sigmoid(gate)
        silu = gate * sg
        da = _dot_nt(ud_ref[1], wd_ref[0])
        dgate = (da * up * (sg + silu * (1.0 - sg))).astype(BF16)
        dup = (da * silu).astype(BF16)
        dga_ref[0] = dgate
        dga_ref[1] = dup
        dga_ref[2] = (silu * up).astype(BF16)
        dgu_ref[k % 2, :, 0:FF_BLOCK] = dgate
        dgu_ref[k % 2, :, FF_BLOCK:2 * FF_BLOCK] = dup

        @pl.when(k == last)
        def _():
            du = acc_ref[...] + _dot(dgu_ref[last % 2], wgu_last_ref[...].reshape(2 * FF_BLOCK, D_MODEL))
            xh, r = _rms_stats(h_ref[...])
            dx, dg = _rms_bwd(du, xh, r, g_ref[...])
            dh_ref[...] = d_ref[...] + dx
            _accumulate(dg_ref, m == 0, dg)

    return _out_hbm(pl.pallas_call(
        body, name="ffn_bwd_act",
        grid=(SEQ // tm, N_FF_BLOCKS),
        in_specs=[pl.BlockSpec((tm, D_MODEL), lambda m, k: (m, 0), pipeline_mode=pl.Buffered(1)),
                  pl.BlockSpec((1, D_MODEL), lambda m, k: (0, 0)),
                  pl.BlockSpec((tm, D_MODEL), lambda m, k: (m, 0), pipeline_mode=pl.Buffered(1)),
                  pl.BlockSpec((2, tm, FF_BLOCK), lambda m, k: (0, m, k)),
                  pl.BlockSpec((1, FF_BLOCK, D_MODEL), lambda m, k: (2, k, 0)),
                  pl.BlockSpec((2, FF_BLOCK, D_MODEL), lambda m, k: (0, jnp.maximum(k - 1, 0), 0)),
                  pl.BlockSpec((2, FF_BLOCK, D_MODEL), lambda m, k: (0, last, 0), pipeline_mode=pl.Buffered(1))],
        out_specs=[pl.BlockSpec((tm, D_MODEL), lambda m, k: (m, 0)),
                   pl.BlockSpec((3, tm, FF_BLOCK), lambda m, k: (0, m, k)),
                   pl.BlockSpec((2, tm, D_MODEL), lambda m, k: (0, m, 0)),
                   pl.BlockSpec((1, D_MODEL), lambda m, k: (0, 0))],
        out_shape=[jax.ShapeDtypeStruct((SEQ, D_MODEL), F32),
                   pltpu.HBM((3, SEQ, D_FF), BF16),
                   pltpu.HBM((2, SEQ, D_MODEL), BF16),
                   jax.ShapeDtypeStruct((1, D_MODEL), F32)],
        scratch_shapes=[pltpu.VMEM((tm, D_MODEL), F32), pltpu.VMEM((2, tm, 2 * FF_BLOCK), BF16)],
        compiler_params=_cparams(("arbitrary", "arbitrary"), 56),
    )(*_in_hbm(h, g, dout, gu, w3, w3, w3)), (1, 2))


def _matmul_tn(a, b, bm, out_dtype, name, bn=None, to_kernel=True):
    na, t, m = a.shape
    nb, _, n = b.shape
    bn = n if bn is None else bn

    def body(a_ref, b_ref, o_ref):
        o_ref[0] = _dot_tn(a_ref[0], b_ref[0]).astype(out_dtype)

    return _out_hbm(pl.pallas_call(
        body, name=name,
        grid=(na, m // bm, n // bn),
        in_specs=[pl.BlockSpec((1, t, bm), lambda i, k, j: (i, 0, k)),
                  pl.BlockSpec((1, t, bn), lambda i, k, j: (jnp.maximum(i - (na - nb), 0), 0, j))],
        out_specs=pl.BlockSpec((1, bm, bn), lambda i, k, j: (i, k, j)),
        out_shape=pltpu.HBM((na, m, n), out_dtype) if to_kernel else jax.ShapeDtypeStruct((na, m, n), out_dtype),
        compiler_params=_cparams(("arbitrary", "parallel", "parallel")),
    )(*_in_hbm(a, b)), to_kernel)


def _inproj_fwd(h, g, wint):
    tm = TOK_TILE

    def body(h_ref, g_ref, w_ref, z_ref):
        xh, _ = _rms_stats(h_ref[...])
        z_ref[...] = _dot_nt((xh * g_ref[...]).astype(BF16), w_ref[...])

    return pl.pallas_call(
        body, name="inproj_fwd",
        grid=(SEQ // tm,),
        in_specs=[pl.BlockSpec((tm, D_MODEL), lambda m: (m, 0)),
                  pl.BlockSpec((1, D_MODEL), lambda m: (0, 0)),
                  pl.BlockSpec((None, IN_COLS, D_MODEL), lambda m: (0, 0, 0))],
        out_specs=pl.BlockSpec((tm, IN_COLS), lambda m: (m, 0)),
        out_shape=jax.ShapeDtypeStruct((SEQ, IN_COLS), F32),
        compiler_params=_cparams(("parallel",)),
    )(*_in_hbm(h, g, wint))


def _inproj_bwd(h, g, dh, dz, wint):
    tm = TOK_TILE

    def body(h_ref, g_ref, dh_ref, dz_ref, w_ref, o_ref, u_ref, dg_ref):
        xh, r = _rms_stats(h_ref[...])
        u_ref[0] = (xh * g_ref[...]).astype(BF16)
        dx, dg = _rms_bwd(_dot(dz_ref[...], w_ref[...]), xh, r, g_ref[...])
        o_ref[...] = dh_ref[...] + dx
        _accumulate(dg_ref, pl.program_id(0) == 0, dg)

    return _out_hbm(pl.pallas_call(
        body, name="inproj_bwd",
        grid=(SEQ // tm,),
        in_specs=[pl.BlockSpec((tm, D_MODEL), lambda m: (m, 0)),
                  pl.BlockSpec((1, D_MODEL), lambda m: (0, 0)),
                  pl.BlockSpec((tm, D_MODEL), lambda m: (m, 0)),
                  pl.BlockSpec((tm, IN_COLS), lambda m: (m, 0)),
                  pl.BlockSpec((None, IN_COLS, D_MODEL), lambda m: (0, 0, 0))],
        out_specs=[pl.BlockSpec((tm, D_MODEL), lambda m: (m, 0)),
                   pl.BlockSpec((1, tm, D_MODEL), lambda m: (0, m, 0)),
                   pl.BlockSpec((1, D_MODEL), lambda m: (0, 0))],
        out_shape=[jax.ShapeDtypeStruct((SEQ, D_MODEL), F32),
                   pltpu.HBM((1, SEQ, D_MODEL), BF16),
                   jax.ShapeDtypeStruct((1, D_MODEL), F32)],
        compiler_params=_cparams(("arbitrary",)),
    )(*_in_hbm(h, g, dh, dz, wint)), (1,))


def _row_ids(n, w):
    return lax.broadcasted_iota(jnp.int32, (n, w), 0)


def _bcast_row(x, i, n):
    return jnp.broadcast_to(x[i:i + 1, :], (n, x.shape[1]))


def _conv_taps(v, tail):
    n, w = v.shape
    rid = _row_ids(n, w)
    v1 = jnp.where(rid == 0, _bcast_row(tail, 7, n), pltpu.roll(v, 1, 0))
    v2 = jnp.where(rid == 0, _bcast_row(tail, 6, n),
                   jnp.where(rid == 1, _bcast_row(tail, 7, n), pltpu.roll(v, 2, 0)))
    return v1, v2


def _block_tiles():
    half_rows, half_cols = SSM_W // 2, N_STATE // 2
    for half in range(2):
        for part in range(2):
            yield (slice(half * half_rows, (half + 1) * half_rows),
                   slice(part * N_STATE + half * half_cols, part * N_STATE + (half + 1) * half_cols))


def _block_expand(x, mat_ref, out_ref):
    for rows, cols in _block_tiles():
        out_ref[:, cols] = _dot(x[:, rows], mat_ref[rows, cols])


def _block_contract(s, mat_ref):
    halves = {}
    for rows, cols in _block_tiles():
        part = _dot_nt(s[:, cols], mat_ref[rows, cols])
        halves[rows.start] = part if rows.start not in halves else halves[rows.start] + part
    return jnp.concatenate([halves[k] for k in sorted(halves)], axis=1)


def _block_wgrad(a, b, name):
    t = a.shape[1]
    half_rows, half_cols = SSM_W // 2, N_STATE // 2

    def body(a_ref, b_ref, o_ref):
        o_ref[...] = _dot_tn(a_ref[...], b_ref[...])

    return pl.pallas_call(
        body, name=name,
        grid=(2, 2),
        in_specs=[pl.BlockSpec((None, t, half_rows), lambda h, p: (0, 0, h)),
                  pl.BlockSpec((None, t, half_cols), lambda h, p: (0, 0, 2 * p + h))],
        out_specs=pl.BlockSpec((half_rows, half_cols), lambda h, p: (h, 2 * p + h)),
        out_shape=jax.ShapeDtypeStruct((SSM_W, 2 * N_STATE), F32),
        compiler_params=_cparams(("parallel", "parallel")),
    )(*_in_hbm(a, b))


def _scan_chunk(work, ltab, carry, reverse):
    nblk = CHUNK // SUBLANES
    for gi in range(N_STATE // LANE_GROUP):
        cre = pl.ds(gi * LANE_GROUP, LANE_GROUP)
        cim = pl.ds(N_STATE + gi * LANE_GROUP, LANE_GROUP)
        pows = [(ltab[8 * k:8 * k + 8, cre], ltab[8 * k:8 * k + 8, cim]) for k in range(3)]
        pr = ltab[24:32, cre]
        pi = ltab[24:32, cim]

        def blk(i, c, cre=cre, cim=cim, pows=pows, pr=pr, pi=pi):
            cr, ci = c
            b = (nblk - 1 - i) if reverse else i
            r0 = pl.multiple_of(b * SUBLANES, SUBLANES)
            xr = work[pl.ds(r0, SUBLANES), cre]
            xi = work[pl.ds(r0, SUBLANES), cim]
            for k, s in enumerate((1, 2, 4)):
                lr, li = pows[k]
                shift = SUBLANES - s if reverse else s
                sr = pltpu.roll(xr, shift, 0)
                si = pltpu.roll(xi, shift, 0)
                xr, xi = xr + lr * sr - li * si, xi + lr * si + li * sr
            xr, xi = xr + pr * cr - pi * ci, xi + pr * ci + pi * cr
            work[pl.ds(r0, SUBLANES), cre] = xr
            work[pl.ds(r0, SUBLANES), cim] = xi
            edge = 0 if reverse else SUBLANES - 1
            return _bcast_row(xr, edge, SUBLANES), _bcast_row(xi, edge, SUBLANES)

        cr, ci = lax.fori_loop(0, nblk, blk, (carry[:, cre], carry[:, cim]))
        carry[:, cre] = cr
        carry[:, cim] = ci


def _s5conv_fwd(z, convw, convb, bbmat, ccmat, dvec, ltab):
    def body(z_ref, cw_ref, cb_ref, bb_ref, cc_ref, d_ref, lt_ref, ya_ref, ys_ref, hs_ref,
             work, carry, tail):
        c = pl.program_id(0)

        @pl.when(c == 0)
        def _():
            carry[...] = jnp.zeros_like(carry)
            tail[...] = jnp.zeros_like(tail)

        zb = z_ref[:, 0:CONV_W]
        v = z_ref[:, CONV_W:2 * CONV_W] * z_ref[:, 2 * CONV_W:3 * CONV_W]
        us = z_ref[:, 3 * CONV_W:4 * CONV_W]
        v1, v2 = _conv_taps(v, tail[...])
        tail[...] = v[CHUNK - 8:CHUNK, :]
        y = cw_ref[0:1, :] * v2 + cw_ref[1:2, :] * v1 + cw_ref[2:3, :] * v
        ya_ref[...] = zb * (y + cb_ref[...])

        _block_expand(us.astype(BF16), bb_ref, work)
        _scan_chunk(work, lt_ref, carry, reverse=False)
        hs = work[...].astype(BF16)
        hs_ref[...] = hs
        ys_ref[...] = _block_contract(hs, cc_ref) + d_ref[...] * us

    return _out_hbm(pl.pallas_call(
        body, name="s5conv_fwd",
        grid=(N_CHUNKS,),
        in_specs=[pl.BlockSpec((CHUNK, IN_COLS), lambda c: (c, 0)),
                  pl.BlockSpec((3, CONV_W), lambda c: (0, 0)),
                  pl.BlockSpec((1, CONV_W), lambda c: (0, 0)),
                  pl.BlockSpec((SSM_W, 2 * N_STATE), lambda c: (0, 0)),
                  pl.BlockSpec((SSM_W, 2 * N_STATE), lambda c: (0, 0)),
                  pl.BlockSpec((1, SSM_W), lambda c: (0, 0)),
                  pl.BlockSpec((32, 2 * N_STATE), lambda c: (0, 0))],
        out_specs=[pl.BlockSpec((CHUNK, CONV_W), lambda c: (c, 0)),
                   pl.BlockSpec((CHUNK, SSM_W), lambda c: (c, 0)),
                   pl.BlockSpec((CHUNK, 2 * N_STATE), lambda c: (c, 0))],
        out_shape=[pltpu.HBM((SEQ, CONV_W), F32),
                   pltpu.HBM((SEQ, SSM_W), F32),
                   jax.ShapeDtypeStruct((SEQ, 2 * N_STATE), BF16)],
        scratch_shapes=[pltpu.VMEM((CHUNK, 2 * N_STATE), F32),
                        pltpu.VMEM((8, 2 * N_STATE), F32),
                        pltpu.VMEM((8, CONV_W), F32)],
        compiler_params=_cparams(("arbitrary",)),
    )(*_in_hbm(z, convw, convb, bbmat, ccmat, dvec, ltab)), (0, 1))


def _s5conv_bwd(z, hs, dya, dys, convw, convb, bbmat, ccmat, dvec, ltab_rev):
    nc = N_CHUNKS
    hb = 16

    def body(z_ref, zp_ref, hs_ref, hp_ref, dya_ref, dys_ref, cw_ref, cb_ref, bb_ref, cc_ref, d_ref, lt_ref,
             dz_ref, g_ref, us_ref, dyb_ref, dl_ref, dcw_ref, work, carry, head):
        i = pl.program_id(0)
        first_chunk = i == nc - 1

        @pl.when(i == 0)
        def _():
            carry[...] = jnp.zeros_like(carry)
            head[...] = jnp.zeros_like(head)
            dl_ref[...] = jnp.zeros_like(dl_ref)
            dcw_ref[...] = jnp.zeros_like(dcw_ref)

        us = z_ref[:, 3 * CONV_W:4 * CONV_W]
        dy = dys_ref[...]
        dy_bf = dy.astype(BF16)
        us_ref[0] = us.astype(BF16)
        dyb_ref[0] = dy_bf

        _block_expand(dy_bf, cc_ref, work)
        _scan_chunk(work, lt_ref, carry, reverse=True)
        gg = work[...]
        gg_bf = gg.astype(BF16)
        g_ref[0] = gg_bf
        dus = d_ref[...] * dy + _block_contract(gg_bf, bb_ref)

        hcur = hs_ref[...].astype(F32)
        hlast = hp_ref[...].astype(F32)[hb - 1:hb, :]
        hlast = jnp.where(first_chunk, 0.0, hlast)
        rid = _row_ids(CHUNK, 2 * N_STATE)
        hprev = jnp.where(rid == 0, jnp.broadcast_to(hlast, (CHUNK, 2 * N_STATE)), pltpu.roll(hcur, 1, 0))
        gr, gi = gg[:, :N_STATE], gg[:, N_STATE:]
        hr, hi = hprev[:, :N_STATE], hprev[:, N_STATE:]
        dl_ref[:, :N_STATE] += (gr * hr + gi * hi).reshape(CHUNK // 8, 8, N_STATE).sum(axis=0)
        dl_ref[:, N_STATE:] += (gi * hr - gr * hi).reshape(CHUNK // 8, 8, N_STATE).sum(axis=0)

        @pl.when(i == nc - 1)
        def _():
            dl_ref[0:1, :] = jnp.sum(dl_ref[...], axis=0, keepdims=True)

        zb = z_ref[:, 0:CONV_W]
        zc = z_ref[:, CONV_W:2 * CONV_W]
        zv = z_ref[:, 2 * CONV_W:3 * CONV_W]
        v = zc * zv
        vtail = jnp.where(first_chunk, 0.0, zp_ref[:, CONV_W:2 * CONV_W] * zp_ref[:, 2 * CONV_W:3 * CONV_W])
        v1, v2 = _conv_taps(v, vtail)
        w0, w1, w2 = cw_ref[0:1, :], cw_ref[1:2, :], cw_ref[2:3, :]
        y = w0 * v2 + w1 * v1 + w2 * v
        dya_v = dya_ref[...]
        dzb = dya_v * (y + cb_ref[...])
        dyc = dya_v * zb
        hd = head[...]
        rc = _row_ids(CHUNK, CONV_W)
        n1 = jnp.where(rc == CHUNK - 1, _bcast_row(hd, 0, CHUNK), pltpu.roll(dyc, CHUNK - 1, 0))
        n2 = jnp.where(rc == CHUNK - 1, _bcast_row(hd, 1, CHUNK),
                       jnp.where(rc == CHUNK - 2, _bcast_row(hd, 0, CHUNK), pltpu.roll(dyc, CHUNK - 2, 0)))
        head[...] = dyc[0:8, :]
        dv = w2 * dyc + w1 * n1 + w0 * n2
        dz_ref[:, 0:CONV_W] = dzb.astype(BF16)
        dz_ref[:, CONV_W:2 * CONV_W] = (dv * zv).astype(BF16)
        dz_ref[:, 2 * CONV_W:3 * CONV_W] = (dv * zc).astype(BF16)
        dz_ref[:, 3 * CONV_W:4 * CONV_W] = dus.astype(BF16)
        dcw_ref[0:1, :] += jnp.sum(dyc * v2, axis=0, keepdims=True)
        dcw_ref[1:2, :] += jnp.sum(dyc * v1, axis=0, keepdims=True)
        dcw_ref[2:3, :] += jnp.sum(dyc * v, axis=0, keepdims=True)
        dcw_ref[3:4, :] += jnp.sum(dyc, axis=0, keepdims=True)
        dcw_ref[4:5, :] += jnp.sum(dy * us, axis=0, keepdims=True)

    rev = lambda i: nc - 1 - i
    return _out_hbm(pl.pallas_call(
        body, name="s5conv_bwd",
        grid=(nc,),
        in_specs=[pl.BlockSpec((CHUNK, IN_COLS), lambda i: (rev(i), 0)),
                  pl.BlockSpec((8, IN_COLS), lambda i: (jnp.maximum(rev(i) * (CHUNK // 8) - 1, 0), 0)),
                  pl.BlockSpec((CHUNK, 2 * N_STATE), lambda i: (rev(i), 0)),
                  pl.BlockSpec((hb, 2 * N_STATE), lambda i: (jnp.maximum(rev(i) * (CHUNK // hb) - 1, 0), 0)),
                  pl.BlockSpec((CHUNK, CONV_W), lambda i: (rev(i), 0)),
                  pl.BlockSpec((CHUNK, SSM_W), lambda i: (rev(i), 0)),
                  pl.BlockSpec((3, CONV_W), lambda i: (0, 0)),
                  pl.BlockSpec((1, CONV_W), lambda i: (0, 0)),
                  pl.BlockSpec((SSM_W, 2 * N_STATE), lambda i: (0, 0)),
                  pl.BlockSpec((SSM_W, 2 * N_STATE), lambda i: (0, 0)),
                  pl.BlockSpec((1, SSM_W), lambda i: (0, 0)),
                  pl.BlockSpec((32, 2 * N_STATE), lambda i: (0, 0))],
        out_specs=[pl.BlockSpec((CHUNK, IN_COLS), lambda i: (rev(i), 0)),
                   pl.BlockSpec((1, CHUNK, 2 * N_STATE), lambda i: (0, rev(i), 0)),
                   pl.BlockSpec((1, CHUNK, SSM_W), lambda i: (0, rev(i), 0)),
                   pl.BlockSpec((1, CHUNK, SSM_W), lambda i: (0, rev(i), 0)),
                   pl.BlockSpec((8, 2 * N_STATE), lambda i: (0, 0)),
                   pl.BlockSpec((8, CONV_W), lambda i: (0, 0))],
        out_shape=[jax.ShapeDtypeStruct((SEQ, IN_COLS), BF16),
                   pltpu.HBM((1, SEQ, 2 * N_STATE), BF16),
                   pltpu.HBM((1, SEQ, SSM_W), BF16),
                   pltpu.HBM((1, SEQ, SSM_W), BF16),
                   jax.ShapeDtypeStruct((8, 2 * N_STATE), F32),
                   jax.ShapeDtypeStruct((8, CONV_W), F32)],
        scratch_shapes=[pltpu.VMEM((CHUNK, 2 * N_STATE), F32),
                        pltpu.VMEM((8, 2 * N_STATE), F32),
                        pltpu.VMEM((8, CONV_W), F32)],
        compiler_params=_cparams(("arbitrary",)),
    )(*_in_hbm(z, z, hs, hs, dya, dys, convw, convb, bbmat, ccmat, dvec, ltab_rev)), (1, 2, 3))


def _mix_out_fwd(h, ya, ys, gluw, glub, con, son, wout):
    tm = TOK_TILE

    def body(h_ref, ya_ref, ys_ref, gw_ref, gb_ref, con_ref, son_ref, wo_ref, o_ref):
        zg, _ = _gelu(ys_ref[...])
        q = _dot(zg.astype(BF16), gw_ref[...]) + gb_ref[...]
        out_s = zg * _sigmoid(q)
        na, _ = _rms_stats(ya_ref[...])
        ns, _ = _rms_stats(out_s)
        o_ref[...] = (h_ref[...]
                      + _dot((na * con_ref[...]).astype(BF16), wo_ref[0:CONV_W, :])
                      + _dot((ns * son_ref[...]).astype(BF16), wo_ref[CONV_W:2 * CONV_W, :]))

    row = lambda m: (m, 0)
    fixed = lambda m: (0, 0)
    return pl.pallas_call(
        body, name="mix_out_fwd",
        grid=(SEQ // tm,),
        in_specs=[pl.BlockSpec((tm, D_MODEL), row), pl.BlockSpec((tm, CONV_W), row), pl.BlockSpec((tm, SSM_W), row),
                  pl.BlockSpec((SSM_W, SSM_W), fixed), pl.BlockSpec((1, SSM_W), fixed),
                  pl.BlockSpec((1, CONV_W), fixed), pl.BlockSpec((1, SSM_W), fixed),
                  pl.BlockSpec((None, D_MODEL, D_MODEL), lambda m: (0, 0, 0))],
        out_specs=pl.BlockSpec((tm, D_MODEL), row),
        out_shape=jax.ShapeDtypeStruct((SEQ, D_MODEL), F32),
        compiler_params=_cparams(("parallel",)),
    )(*_in_hbm(h, ya, ys, gluw, glub, con, son, wout))


def _mix_out_bwd(dh, ya, ys, gluw, glub, con, son, wout, after=()):
    tm = TOK_TILE

    def body(dh_ref, ya_ref, ys_ref, gw_ref, gb_ref, con_ref, son_ref, wo_ref, *rest):
        dya_ref, dys_ref, yc_ref, dhb_ref, zg_ref, dq_ref, part_ref = rest[len(after):]
        ysv = ys_ref[...]
        zg, th = _gelu(ysv)
        zg_bf = zg.astype(BF16)
        s = _sigmoid(_dot(zg_bf, gw_ref[...]) + gb_ref[...])
        out_s = zg * s
        na, ra = _rms_stats(ya_ref[...])
        ns, rs = _rms_stats(out_s)
        dh_bf = dh_ref[...].astype(BF16)
        yc_ref[0, :, 0:CONV_W] = (na * con_ref[...]).astype(BF16)
        yc_ref[0, :, CONV_W:2 * CONV_W] = (ns * son_ref[...]).astype(BF16)
        dhb_ref[0] = dh_bf
        dca = _dot_nt(dh_bf, wo_ref[0:CONV_W, :])
        dcs = _dot_nt(dh_bf, wo_ref[CONV_W:2 * CONV_W, :])
        dya, dcon = _rms_bwd(dca, na, ra, con_ref[...])
        dos, dson = _rms_bwd(dcs, ns, rs, son_ref[...])
        dya_ref[...] = dya
        dq = dos * zg * s * (1.0 - s)
        dq_bf = dq.astype(BF16)
        dzg = dos * s + _dot_nt(dq_bf, gw_ref[...])
        dys_ref[...] = dzg * _gelu_grad(ysv, th)
        zg_ref[0] = zg_bf
        dq_ref[0] = dq_bf
        rid = _row_ids(SUBLANES, SSM_W)
        part = jnp.zeros((SUBLANES, SSM_W), F32)
        for i, rowv in enumerate((dcon, dson, jnp.sum(dq, axis=0, keepdims=True))):
            part = jnp.where(rid == i, jnp.broadcast_to(rowv, (SUBLANES, SSM_W)), part)
        _accumulate(part_ref, pl.program_id(0) == 0, part)

    row = lambda m: (m, 0)
    fixed = lambda m: (0, 0)
    lead = lambda m: (0, m, 0)
    return _out_hbm(pl.pallas_call(
        body, name="mix_out_bwd",
        grid=(SEQ // tm,),
        in_specs=[pl.BlockSpec((tm, D_MODEL), row), pl.BlockSpec((tm, CONV_W), row), pl.BlockSpec((tm, SSM_W), row),
                  pl.BlockSpec((SSM_W, SSM_W), fixed), pl.BlockSpec((1, SSM_W), fixed),
                  pl.BlockSpec((1, CONV_W), fixed), pl.BlockSpec((1, SSM_W), fixed),
                  pl.BlockSpec((None, D_MODEL, D_MODEL), lambda m: (0, 0, 0))] + [UNREAD] * len(after),
        out_specs=[pl.BlockSpec((tm, CONV_W), row), pl.BlockSpec((tm, SSM_W), row),
                   pl.BlockSpec((1, tm, D_MODEL), lead), pl.BlockSpec((1, tm, D_MODEL), lead),
                   pl.BlockSpec((1, tm, SSM_W), lead), pl.BlockSpec((1, tm, SSM_W), lead),
                   pl.BlockSpec((8, SSM_W), fixed)],
        out_shape=[pltpu.HBM((SEQ, CONV_W), F32), pltpu.HBM((SEQ, SSM_W), F32),
                   pltpu.HBM((1, SEQ, D_MODEL), BF16), pltpu.HBM((1, SEQ, D_MODEL), BF16),
                   pltpu.HBM((1, SEQ, SSM_W), BF16), pltpu.HBM((1, SEQ, SSM_W), BF16),
                   jax.ShapeDtypeStruct((8, SSM_W), F32)],
        compiler_params=_cparams(("arbitrary",)),
    )(*_in_hbm(dh, ya, ys, gluw, glub, con, son, wout, *after)), (0, 1, 2, 3, 4, 5))


def _ple_fwd(h, g, p, wgate, wprojt):
    tm = TOK_TILE

    def body(h_ref, g_ref, p_ref, wg_ref, wp_ref, o_ref, u_ref, pb_ref, q_ref, pp_ref):
        x = h_ref[...]
        xh, _ = _rms_stats(x)
        u = (xh * g_ref[...]).astype(BF16)
        p_bf = p_ref[...].astype(BF16)
        q = _dot(u, wg_ref[...])
        pp = _dot_nt(p_bf, wp_ref[...])
        o_ref[...] = x + pp * _sigmoid(q)
        u_ref[0] = u
        pb_ref[0] = p_bf
        q_ref[...] = q.astype(BF16)
        pp_ref[...] = pp.astype(BF16)

    row = lambda m: (m, 0)
    fixed = lambda m: (0, 0)
    lead = lambda m: (0, m, 0)
    return pl.pallas_call(
        body, name="ple_fwd",
        grid=(SEQ // tm,),
        in_specs=[pl.BlockSpec((tm, D_MODEL), row), pl.BlockSpec((1, D_MODEL), fixed), pl.BlockSpec((tm, PLE_DIM), row),
                  pl.BlockSpec((None, D_MODEL, D_MODEL), lambda m: (0, 0, 0)), pl.BlockSpec((D_MODEL, PLE_DIM), fixed)],
        out_specs=[pl.BlockSpec((tm, D_MODEL), row), pl.BlockSpec((1, tm, D_MODEL), lead),
                   pl.BlockSpec((1, tm, PLE_DIM), lead), pl.BlockSpec((tm, D_MODEL), row),
                   pl.BlockSpec((tm, D_MODEL), row)],
        out_shape=[jax.ShapeDtypeStruct((SEQ, D_MODEL), F32), pltpu.HBM((1, SEQ, D_MODEL), BF16),
                   pltpu.HBM((1, SEQ, PLE_DIM), BF16), pltpu.HBM((SEQ, D_MODEL), BF16),
                   pltpu.HBM((SEQ, D_MODEL), BF16)],
        compiler_params=_cparams(("parallel",)),
    )(*_in_hbm(h, g, p, wgate, wprojt))


def _ple_bwd(h, g, dh, q, pp, wgate, after=()):
    tm = TOK_TILE

    def body(h_ref, g_ref, dh_ref, q_ref, pp_ref, wg_ref, *rest):
        o_ref, dq_ref, dpp_ref, dg_ref = rest[len(after):]
        xh, r = _rms_stats(h_ref[...])
        s = _sigmoid(q_ref[...].astype(F32))
        dhv = dh_ref[...]
        dq = (dhv * pp_ref[...].astype(F32) * s * (1.0 - s)).astype(BF16)
        dq_ref[0] = dq
        dpp_ref[0] = (dhv * s).astype(BF16)
        dx, dg = _rms_bwd(_dot_nt(dq, wg_ref[...]), xh, r, g_ref[...])
        o_ref[...] = dhv + dx
        _accumulate(dg_ref, pl.program_id(0) == 0, dg)

    row = lambda m: (m, 0)
    fixed = lambda m: (0, 0)
    lead = lambda m: (0, m, 0)
    big = pltpu.HBM((1, SEQ, D_MODEL), BF16)
    return _out_hbm(pl.pallas_call(
        body, name="ple_bwd",
        grid=(SEQ // tm,),
        in_specs=[pl.BlockSpec((tm, D_MODEL), row), pl.BlockSpec((1, D_MODEL), fixed), pl.BlockSpec((tm, D_MODEL), row),
                  pl.BlockSpec((tm, D_MODEL), row), pl.BlockSpec((tm, D_MODEL), row),
                  pl.BlockSpec((None, D_MODEL, D_MODEL), lambda m: (0, 0, 0))] + [UNREAD] * len(after),
        out_specs=[pl.BlockSpec((tm, D_MODEL), row),
                   pl.BlockSpec((1, tm, D_MODEL), lead), pl.BlockSpec((1, tm, D_MODEL), lead),
                   pl.BlockSpec((1, D_MODEL), fixed)],
        out_shape=[jax.ShapeDtypeStruct((SEQ, D_MODEL), F32), big, big,
                   jax.ShapeDtypeStruct((1, D_MODEL), F32)],
        compiler_params=_cparams(("arbitrary",)),
    )(*_in_hbm(h, g, dh, q, pp, wgate, *after)), (1, 2))


def _final_loss(h, g, target):
    tm = TOK_TILE

    def body(h_ref, g_ref, t_ref, loss_ref, dh_ref, dg_ref):
        first = pl.program_id(0) == 0
        xh, r = _rms_stats(h_ref[...])
        diff = xh * g_ref[...] - t_ref[...]
        part = 0.5 * jnp.sum(jnp.mean(diff * diff, axis=-1, keepdims=True), axis=0, keepdims=True)
        _accumulate(loss_ref, first, jnp.broadcast_to(part, (SUBLANES, LANES)))
        dx, dg = _rms_bwd(diff * (1.0 / D_MODEL), xh, r, g_ref[...])
        dh_ref[...] = dx
        _accumulate(dg_ref, first, dg)

    row = lambda m: (m, 0)
    fixed = lambda m: (0, 0)
    return pl.pallas_call(
        body, name="final_loss",
        grid=(SEQ // tm,),
        in_specs=[pl.BlockSpec((tm, D_MODEL), row), pl.BlockSpec((1, D_MODEL), fixed),
                  pl.BlockSpec((tm, D_MODEL), row)],
        out_specs=[pl.BlockSpec((SUBLANES, LANES), fixed),
                   pl.BlockSpec((tm, D_MODEL), row),
                   pl.BlockSpec((1, D_MODEL), fixed)],
        out_shape=[jax.ShapeDtypeStruct((SUBLANES, LANES), F32),
                   jax.ShapeDtypeStruct((SEQ, D_MODEL), F32),
                   jax.ShapeDtypeStruct((1, D_MODEL), F32)],
        compiler_params=_cparams(("arbitrary",)),
    )(*_in_hbm(h, g, target))


def _disc(ar, ai, ldt):
    dt = jnp.exp(ldt)
    mag = jnp.exp(ar * dt)
    ph = ai * dt
    lr, li = mag * jnp.cos(ph), mag * jnp.sin(ph)
    nr, ni = lr - 1.0, li
    den = ar * ar + ai * ai
    return lr, li, (nr * ar + ni * ai) / den, (ni * ar - nr * ai) / den


def _s5_disc(a_row, ldt_row, a_rep, ldt_rep, bt, ct, tile_e, mask):
    n = N_STATE

    def body(ar_ref, lr_ref, ap_ref, lp_ref, b_ref, c_ref, e_ref, m_ref, lt_ref, ltr_ref, bb_ref, cc_ref):
        lr, li, _, _ = _disc(ar_ref[0], ar_ref[1], lr_ref[...])
        pr, pi = lr, li
        rid = _row_ids(SUBLANES, n)
        for k in range(1, 9):
            for ref, sgn, edge in ((lt_ref, 1.0, 24 + k - 1), (ltr_ref, -1.0, 24 + 8 - k)):
                if k in (1, 2, 4):
                    r0 = {1: 0, 2: 8, 4: 16}[k]
                    keep = (rid >= k) if ref is lt_ref else (rid < SUBLANES - k)
                    ref[r0:r0 + 8, 0:n] = jnp.where(keep, jnp.broadcast_to(pr, (8, n)), 0.0)
                    ref[r0:r0 + 8, n:2 * n] = jnp.where(keep, jnp.broadcast_to(sgn * pi, (8, n)), 0.0)
                ref[edge:edge + 1, 0:n] = pr
                ref[edge:edge + 1, n:2 * n] = sgn * pi
            pr, pi = pr * lr - pi * li, pr * li + pi * lr
        _, _, fr, fi = _disc(ap_ref[0], ap_ref[1], lp_ref[...])
        br, bi = b_ref[0], b_ref[1]
        e = e_ref[...]
        m = m_ref[...].astype(F32)
        bb_ref[:, 0:n] = (_dot((fr * br - fi * bi).astype(BF16), e) * m).astype(BF16)
        bb_ref[:, n:2 * n] = (_dot((fr * bi + fi * br).astype(BF16), e) * m).astype(BF16)
        cc_ref[:, 0:n] = (_dot(c_ref[0].astype(BF16), e) * m).astype(BF16)
        cc_ref[:, n:2 * n] = (-(_dot(c_ref[1].astype(BF16), e) * m)).astype(BF16)

    return pl.pallas_call(
        body, name="s5_disc",
        out_shape=[jax.ShapeDtypeStruct((32, 2 * n), F32), jax.ShapeDtypeStruct((32, 2 * n), F32),
                   jax.ShapeDtypeStruct((SSM_W, 2 * n), BF16), jax.ShapeDtypeStruct((SSM_W, 2 * n), BF16)],
        compiler_params=_cparams(None),
    )(a_row, ldt_row, a_rep, ldt_rep, bt, ct, tile_e, mask)


def _dot_exact(x, sel):
    hi = x.astype(BF16)
    r1 = x - hi.astype(F32)
    mid = r1.astype(BF16)
    lo = (r1 - mid.astype(F32)).astype(BF16)
    return _dot(hi, sel) + _dot(mid, sel) + _dot(lo, sel)


def _s5_disc_bwd(a, ldt, a_rep, ldt_rep, bt, mask, dl, d_bb, d_cc, fold):
    n = N_STATE

    def body(a_ref, l_ref, ap_ref, lp_ref, b_ref, m_ref, dl_ref, dbb_ref, dcc_ref, f_ref,
             da_ref, dldt_ref, db_ref, dc_ref):
        m = m_ref[...].astype(F32)
        fold_m = f_ref[...]
        diag = lambda x: _dot_exact(jnp.where(m > 0.0, x, 0.0), fold_m)
        dr, di = diag(dbb_ref[:, 0:n]), diag(dbb_ref[:, n:2 * n])
        dc_ref[0] = diag(dcc_ref[:, 0:n])
        dc_ref[1] = -diag(dcc_ref[:, n:2 * n])
        _, _, fr, fi = _disc(ap_ref[0], ap_ref[1], lp_ref[...])
        br, bi = b_ref[0], b_ref[1]
        db_ref[0] = fr * dr + fi * di
        db_ref[1] = fr * di - fi * dr
        per_state = lambda x: x.reshape(SSM_GROUPS, SSM_GROUP, SSM_STATE).sum(axis=1)
        dfr = per_state(dr * br + di * bi)
        dfi = per_state(di * br - dr * bi)
        _, vjp = jax.vjp(_disc, a_ref[0], a_ref[1], l_ref[...])
        dar, dai, dldt = vjp((dl_ref[0], dl_ref[1], dfr, dfi))
        da_ref[0] = dar
        da_ref[1] = dai
        dldt_ref[...] = jnp.sum(dldt, axis=1, keepdims=True)

    return pl.pallas_call(
        body, name="s5_disc_bwd",
        out_shape=[jax.ShapeDtypeStruct((2, SSM_GROUPS, SSM_STATE), F32),
                   jax.ShapeDtypeStruct((SSM_GROUPS, 1), F32),
                   jax.ShapeDtypeStruct((2, SSM_W, SSM_STATE), F32),
                   jax.ShapeDtypeStruct((2, SSM_W, SSM_STATE), F32)],
        compiler_params=_cparams(None),
    )(a, ldt, a_rep, ldt_rep, bt, mask, dl, d_bb, d_cc, fold)


def _row_block(rows, cap=512):
    for bm in range(min(cap, rows), 0, -1):
        if rows % bm == 0 and (bm % 8 == 0 or bm == rows):
            return bm
    return rows


SUM_PARTS = 2


def _own_pieces(segs, rtot):
    pr = rtot // SUM_PARTS
    assert pr * SUM_PARTS == rtot and pr % 16 == 0
    offs = _seg_offsets(segs)
    pieces = [[] for _ in range(SUM_PARTS)]
    for a, (n, r) in enumerate(segs):
        for m in range(n):
            lo = offs[a] + m * r
            for h in range(SUM_PARTS):
                clo, chi = max(lo, h * pr), min(lo + r, (h + 1) * pr)
                if chi > clo:
                    pieces[h].append((a, m, clo - lo, clo - h * pr, chi - clo))
    return pieces


def _pair_rows(srcs, got_ref, segs, pieces, h, chip, own_v, got_v, sems):
    pr = own_v.shape[0]
    dev = 2 * chip + lax.axis_index("c")
    for hh in range(SUM_PARTS):
        @pl.when(h == hh)
        def _(hh=hh):
            cps = [pltpu.make_async_copy(got_ref.at[chip, pl.ds(hh * pr, pr), :], got_v, sems.at[0])]
            for i, (a, m, so, do, rows) in enumerate(pieces[hh]):
                start = pl.multiple_of(dev * segs[a][1] + so, 16)
                cps.append(pltpu.make_async_copy(srcs[a].at[m, pl.ds(start, rows), :],
                                                 own_v.at[pl.ds(do, rows), :], sems.at[1 + i]))
            for cp in cps:
                cp.start()
            for cp in cps:
                cp.wait()
    return own_v[...].astype(F32) + got_v[...].astype(F32)


def _pair_sum(fulls, got, segs):
    ns = len(segs)
    _, rtot, c = got.shape
    pieces = _own_pieces(segs, rtot)
    pr = rtot // SUM_PARTS

    def body(*refs):
        srcs = refs[:ns]
        got_ref, pbf_ref, own_v, got_v, sems = refs[ns:]
        x, y, _ = _mesh_pos()
        j = pl.program_id(1)
        chip = jnp.where(j == 0, 2 * (1 - x) + y, jnp.where(j == 1, 2 * x + 1 - y, 2 * (1 - x) + 1 - y))
        pbf_ref[0] = _pair_rows(srcs, got_ref, segs, pieces, pl.program_id(0), chip, own_v, got_v, sems).astype(BF16)

    return pl.pallas_call(
        body, name="pair_sum",
        grid=(SUM_PARTS, 3),
        in_specs=[HBM] * (ns + 1), out_specs=pl.BlockSpec((1, pr, c), lambda h, j: (j, h, 0)),
        out_shape=pltpu.HBM((3, rtot, c), BF16),
        scratch_shapes=[pltpu.VMEM((pr, c), BF16), pltpu.VMEM((pr, c), BF16),
                        pltpu.SemaphoreType.DMA((1 + max(len(p) for p in pieces),))],
        compiler_params=_cparams(("arbitrary", "arbitrary")),
    )(*_in_hbm(*fulls, got))


def _chip_sum(fulls, got, rb, segs, layer, into):
    ns = len(segs)
    _, rtot, c = got.shape
    pieces = _own_pieces(segs, rtot)
    pr = rtot // SUM_PARTS

    def body(*refs):
        srcs = refs[:ns]
        got_ref, r_ref = refs[ns], refs[ns + 1]
        s_ref, own_v, got_v, sems = refs[-4:]
        x, y, _ = _mesh_pos()
        own = _pair_rows(srcs, got_ref, segs, pieces, pl.program_id(0), 2 * x + y, own_v, got_v, sems)
        s_ref[0] = ((own + r_ref[0].astype(F32)) + r_ref[1].astype(F32)) + r_ref[2].astype(F32)

    old = [] if into is None else [into]
    return pl.pallas_call(
        body, name="chip_sum",
        grid=(SUM_PARTS,),
        in_specs=[HBM] * (ns + 1) + [pl.BlockSpec((3, pr, c), lambda h: (0, h, 0))] + [HBM] * len(old),
        out_specs=pl.BlockSpec((1, pr, c), lambda h: (layer, h, 0)),
        out_shape=jax.ShapeDtypeStruct((DEPTH, rtot, c), F32),
        input_output_aliases={ns + 2: 0} if old else {},
        scratch_shapes=[pltpu.VMEM((pr, c), BF16), pltpu.VMEM((pr, c), BF16),
                        pltpu.SemaphoreType.DMA((1 + max(len(p) for p in pieces),))],
        compiler_params=_cparams(("arbitrary",)),
    )(*_in_hbm(*fulls, got, rb), *old)


def _adamw(sets):
    ns = len(sets)
    r, c = sets[0][0].shape
    bm = _row_block(r)
    bc1 = 1.0 - ADAM_B1 ** ADAM_STEP
    bc2 = 1.0 - ADAM_B2 ** ADAM_STEP

    def body(*refs):
        for s in range(ns):
            w_ref, g_ref, m_ref, v_ref = refs[4 * s:4 * s + 4]
            d_ref, nm_ref, nv_ref = refs[4 * ns + 3 * s:4 * ns + 3 * s + 3]
            gv = g_ref[...]
            nm = ADAM_B1 * m_ref[...] + (1.0 - ADAM_B1) * gv
            nv = ADAM_B2 * v_ref[...] + (1.0 - ADAM_B2) * (gv * gv)
            nm_ref[...] = nm
            nv_ref[...] = nv
            d_ref[...] = -ADAM_LR * ((nm / bc1) / (jnp.sqrt(nv / bc2) + ADAM_EPS) + ADAM_WD * w_ref[...])

    spec = pl.BlockSpec((bm, c), lambda k: (k, 0))
    shp = jax.ShapeDtypeStruct((r, c), F32)
    flat = pl.pallas_call(
        body, name="adamw",
        grid=(r // bm,),
        in_specs=[spec] * (4 * ns), out_specs=[spec] * (3 * ns), out_shape=[shp] * (3 * ns),
        compiler_params=_cparams(("parallel",)),
    )(*_in_hbm(*[a for four in sets for a in four]))
    return [flat[3 * s:3 * s + 3] for s in range(ns)]


def _adamw_layers(sets, first, nl, prev):
    ns = len(sets)
    depth, r, c = sets[0][0].shape
    bm = _row_block(r, min(512, max(SUBLANES, (24 * MIB) // (ns * 8 * 2 * c * 4))))
    while any(four[4] is not None and four[4] % bm for four in sets):
        bm //= 2
    assert bm % SUBLANES == 0 and r % bm == 0
    bc1 = 1.0 - ADAM_B1 ** ADAM_STEP
    bc2 = 1.0 - ADAM_B2 ** ADAM_STEP

    def body(*refs):
        outs = refs[len(refs) - 4 * ns:]
        for s in range(ns):
            w_ref, m_ref, v_ref, g_ref = refs[4 * s:4 * s + 4]
            go_ref, d_ref, nm_ref, nv_ref = outs[4 * s:4 * s + 4]
            gv = g_ref[...]
            nm = ADAM_B1 * m_ref[...] + (1.0 - ADAM_B1) * gv
            nv = ADAM_B2 * v_ref[...] + (1.0 - ADAM_B2) * (gv * gv)
            go_ref[...] = gv
            nm_ref[...] = nm
            nv_ref[...] = nv
            d_ref[...] = -ADAM_LR * ((nm / bc1) / (jnp.sqrt(nv / bc2) + ADAM_EPS) + ADAM_WD * w_ref[...])

    at = pl.BlockSpec((1, bm, c), lambda i, k: (first + i, k, 0))

    def grad_spec(g_rows):
        if g_rows is None:
            return pl.BlockSpec((1, bm, c), lambda i, k: (i, k, 0))
        return pl.BlockSpec((1, bm, c), lambda i, k: (first + i, g_rows // bm + k, 0))

    shp = jax.ShapeDtypeStruct((depth, r, c), F32)
    old = [] if prev is None else [a for four in prev for a in four]
    flat = pl.pallas_call(
        body, name="adamw_layers",
        grid=(nl, r // bm),
        in_specs=[spec for four in sets for spec in (at, at, at, grad_spec(four[4]))] + [HBM] * len(old),
        out_specs=[at] * (4 * ns), out_shape=[shp] * (4 * ns),
        input_output_aliases={4 * ns + i: i for i in range(len(old))},
        compiler_params=_cparams(("parallel", "parallel")),
    )(*_in_hbm(*[a for four in sets for a in four[:4]]), *old)
    return [flat[4 * s:4 * s + 4] for s in range(ns)]


def _mesh_pos():
    return lax.axis_index("x"), lax.axis_index("y"), lax.axis_index("c")


def _dev_index(p):
    return 4 * p[0] + 2 * p[1] + p[2]


def _seg_offsets(segs):
    offs, o = [], 0
    for n, r in segs:
        offs.append(o)
        o += n * r
    return offs


def _remote(src, dst, send_sem, recv_sem, to):
    return pltpu.make_async_remote_copy(src_ref=src, dst_ref=dst, send_sem=send_sem, recv_sem=recv_sem,
                                        device_id=to, device_id_type=MESH)


def _allgather(pack, segs, name):
    rtot, c = pack.shape
    ns = len(segs)
    offs = _seg_offsets(segs)
    assert rtot == sum(n * r for n, r in segs)

    def body(pack_ref, *refs):
        outs = refs[:ns]
        send_sems, recv_sems, local_sem = refs[ns:]
        x, y, cc = _mesh_pos()
        me, sib = (x, y, cc), (x, y, 1 - cc)
        chips = [(1 - x, y), (x, 1 - y), (1 - x, 1 - y)]

        def pieces(dev, from_pack):
            res = []
            for a, (n, r) in enumerate(segs):
                for m in range(n):
                    dst = outs[a].at[m, pl.ds(pl.multiple_of(dev * r, r), r), :]
                    src = pack_ref.at[pl.ds(offs[a] + m * r, r), :] if from_pack else dst
                    res.append((src, dst))
            return res

        def push(k, dev, to, from_pack):
            for s, d in pieces(dev, from_pack):
                _remote(s, d, send_sems.at[k], recv_sems.at[k], to).start()

        def whole(k):
            return _remote(pack_ref, pack_ref, send_sems.at[k], recv_sems.at[k], me)

        my_dev = _dev_index(me)
        for s, d in pieces(my_dev, True):
            pltpu.make_async_copy(s, d, local_sem).start()
        push(0, my_dev, sib, True)
        for j, chip in enumerate(chips):
            push(1 + j, my_dev, (*chip, cc), True)
        for j, chip in enumerate(chips):
            whole(1 + j).wait_recv()
            push(4 + j, _dev_index((*chip, cc)), sib, False)
        whole(0).wait_recv()
        for j in range(3):
            whole(4 + j).wait_recv()
        for k in range(7):
            whole(k).wait_send()
        pltpu.make_async_copy(pack_ref, pack_ref, local_sem).wait()

    return pl.pallas_call(
        body, name=name,
        in_specs=[HBM], out_specs=[HBM] * ns,
        out_shape=[jax.ShapeDtypeStruct((n, N_DEV * r, c), pack.dtype) for n, r in segs],
        scratch_shapes=[pltpu.SemaphoreType.DMA((7,)), pltpu.SemaphoreType.DMA((7,)), pltpu.SemaphoreType.DMA],
    )(pack)


HBM = pl.BlockSpec(memory_space=pltpu.HBM)
SEM = pl.BlockSpec(memory_space=pltpu.SEMAPHORE)
VMEM_WHOLE = pl.BlockSpec(memory_space=pltpu.VMEM)
EFFECT = pltpu.SideEffectType.DATAFLOW_SIDE_EFFECTING


def _hbm(a):
    return pltpu.with_memory_space_constraint(a, pltpu.HBM)


def _ag_start(pack, segs, after, name):
    rtot, c = pack.shape
    ns = len(segs)
    offs = _seg_offsets(segs)

    def body(pack_ref, *refs):
        lands = refs[:ns]
        send_sems, recv_sems = refs[ns + 1], refs[ns + 2]
        token = refs[-1]
        x, y, cc = _mesh_pos()
        my_dev = _dev_index((x, y, cc))
        targets = [(x, y, 1 - cc), (1 - x, y, cc), (x, 1 - y, cc), (1 - x, 1 - y, cc)]
        for k, to in enumerate(targets):
            for a, (n, r) in enumerate(segs):
                for m in range(n):
                    _remote(pack_ref.at[pl.ds(offs[a] + m * r, r), :],
                            lands[a].at[m, pl.ds(pl.multiple_of(my_dev * r, r), r), :],
                            send_sems.at[k], recv_sems.at[k], to).start()
        token[...] = jnp.zeros_like(token)

    land_shapes = [(n, N_DEV * r, c) for n, r in segs]
    outs = pl.pallas_call(
        body, name=name,
        in_specs=[HBM] * (1 + ns) + [UNREAD],
        out_specs=[SEM, SEM, HBM] + [HBM] * ns + [VMEM_WHOLE],
        out_shape=[pltpu.SemaphoreType.DMA((4,)), pltpu.SemaphoreType.DMA((4,)), pltpu.HBM(pack.shape, pack.dtype)]
        + [pltpu.HBM(s, pack.dtype) for s in land_shapes] + [jax.ShapeDtypeStruct((SUBLANES, LANES), F32)],
        input_output_aliases={0: 2, **{1 + i: 3 + i for i in range(ns)}},
        compiler_params=pltpu.CompilerParams(has_side_effects=EFFECT),
    )(_hbm(pack), *[_hbm(lax.empty(s, pack.dtype)) for s in land_shapes], _hbm(after))
    return outs[0], outs[1], outs[2], list(outs[3:3 + ns]), outs[-1]


def _ag_wait(send_sems, recv_sems, pack, lands, after, name):
    ns = len(lands)

    def body(pack_ref, *refs):
        send_ref, recv_ref = refs[ns], refs[ns + 1]
        me = _mesh_pos()
        for k in range(4):
            whole = _remote(pack_ref, pack_ref, send_ref.at[k], recv_ref.at[k], me)
            whole.wait_send()
            whole.wait_recv()

    outs = pl.pallas_call(
        body, name=name,
        in_specs=[HBM] * (1 + ns) + [SEM, SEM, UNREAD],
        out_specs=[HBM] * (1 + ns),
        out_shape=[pltpu.HBM(pack.shape, pack.dtype)] + [pltpu.HBM(a.shape, a.dtype) for a in lands],
        input_output_aliases={i: i for i in range(1 + ns)},
        compiler_params=pltpu.CompilerParams(has_side_effects=EFFECT),
    )(pack, *lands, send_sems, recv_sems, _hbm(after))
    return outs[0], list(outs[1:])


def _ag_finish(pack, lands, segs):
    rtot, c = pack.shape
    ns = len(segs)
    offs = _seg_offsets(segs)

    def body(pack_ref, *refs):
        outs = refs[ns:2 * ns]
        stage, send_sems, recv_sems, local_sems = refs[2 * ns:]
        x, y, cc = _mesh_pos()
        me, sib = (x, y, cc), (x, y, 1 - cc)
        chips = [(1 - x, y), (x, 1 - y), (1 - x, 1 - y)]

        def rows(a, m, dev):
            return outs[a].at[m, pl.ds(pl.multiple_of(dev * segs[a][1], segs[a][1]), segs[a][1]), :]

        for j, chip in enumerate(chips):
            dev = _dev_index((*chip, cc))
            for a, (n, r) in enumerate(segs):
                for m in range(n):
                    _remote(rows(a, m, dev), rows(a, m, dev), send_sems.at[j], recv_sems.at[j], sib).start()
        load = pltpu.make_async_copy(pack_ref, stage, local_sems.at[0])
        load.start()
        load.wait()
        my_dev = _dev_index(me)
        for a, (n, r) in enumerate(segs):
            for m in range(n):
                pltpu.make_async_copy(stage.at[pl.ds(offs[a] + m * r, r), :], rows(a, m, my_dev), local_sems.at[1]).start()
        pltpu.make_async_copy(stage, pack_ref, local_sems.at[1]).wait()
        for j in range(3):
            _remote(pack_ref, pack_ref, send_sems.at[j], recv_sems.at[j], me).wait()

    outs = pl.pallas_call(
        body, name="ag_finish",
        in_specs=[HBM] * (1 + ns), out_specs=[HBM] * ns,
        out_shape=[pltpu.HBM(a.shape, a.dtype) if r >= 128 else jax.ShapeDtypeStruct(a.shape, a.dtype)
                   for a, (_, r) in zip(lands, segs)],
        input_output_aliases={1 + i: i for i in range(ns)},
        scratch_shapes=[pltpu.VMEM((rtot, c), pack.dtype), pltpu.SemaphoreType.DMA((3,)),
                        pltpu.SemaphoreType.DMA((3,)), pltpu.SemaphoreType.DMA((2,))],
        compiler_params=_cparams(None, 16),
    )(pack, *lands)
    return list(outs)


def _rs_chips_start(pbf, after, name):
    _, rtot, c = pbf.shape

    def body(pbf_ref, land_ref, after_ref, send_sems, recv_sems, pbf_thru, land_thru, token):
        x, y, cc = _mesh_pos()
        for j, (cx, cy) in enumerate([(1 - x, y), (x, 1 - y), (1 - x, 1 - y)]):
            _remote(pbf_ref.at[j], land_ref.at[j], send_sems.at[j], recv_sems.at[j], (cx, cy, cc)).start()
        token[...] = jnp.zeros_like(token)

    return pl.pallas_call(
        body, name=name,
        in_specs=[HBM, HBM, UNREAD],
        out_specs=[SEM, SEM, HBM, HBM, VMEM_WHOLE],
        out_shape=[pltpu.SemaphoreType.DMA((3,)), pltpu.SemaphoreType.DMA((3,)), pltpu.HBM(pbf.shape, pbf.dtype),
                   pltpu.HBM((3, rtot, c), pbf.dtype), jax.ShapeDtypeStruct((SUBLANES, LANES), F32)],
        input_output_aliases={0: 2, 1: 3},
        compiler_params=pltpu.CompilerParams(has_side_effects=EFFECT),
    )(_hbm(pbf), _hbm(lax.empty((3, rtot, c), pbf.dtype)), _hbm(after))


def _rs_chips_wait(send_sems, recv_sems, pbf, land, after, name):
    def body(pbf_ref, land_ref, send_ref, recv_ref, *rest):
        me = _mesh_pos()
        for j in range(3):
            cp = _remote(pbf_ref.at[0], land_ref.at[j], send_ref.at[j], recv_ref.at[j], me)
            cp.wait_send()
            cp.wait_recv()

    return pl.pallas_call(
        body, name=name,
        in_specs=[HBM, HBM, SEM, SEM] + [UNREAD] * len(after), out_specs=[HBM, HBM],
        out_shape=[pltpu.HBM(pbf.shape, pbf.dtype), pltpu.HBM(land.shape, land.dtype)],
        input_output_aliases={0: 0, 1: 1},
        compiler_params=pltpu.CompilerParams(has_side_effects=EFFECT),
    )(pbf, land, send_sems, recv_sems, *_in_hbm(*after))[1]


def _flips():
    return [(dx, dy, dc) for dx in (0, 1) for dy in (0, 1) for dc in (0, 1) if dx or dy or dc]


def _small_gather_start(flat, name):
    r, c = flat.shape

    def body(flat_ref, land_ref, send_sems, recv_sems, flat_thru, land_thru, token):
        x, y, cc = _mesh_pos()
        mine = land_ref.at[_dev_index((x, y, cc))]
        for k, (dx, dy, dc) in enumerate(_flips()):
            to = (1 - x if dx else x, 1 - y if dy else y, 1 - cc if dc else cc)
            _remote(flat_ref, mine, send_sems.at[k], recv_sems.at[k], to).start()
        token[...] = jnp.zeros_like(token)

    return pl.pallas_call(
        body, name=name,
        in_specs=[HBM, HBM],
        out_specs=[SEM, SEM, HBM, HBM, VMEM_WHOLE],
        out_shape=[pltpu.SemaphoreType.DMA((7,)), pltpu.SemaphoreType.DMA((7,)), pltpu.HBM(flat.shape, flat.dtype),
                   pltpu.HBM((N_DEV, r, c), flat.dtype), jax.ShapeDtypeStruct((SUBLANES, LANES), F32)],
        input_output_aliases={0: 2, 1: 3},
        compiler_params=pltpu.CompilerParams(has_side_effects=EFFECT),
    )(_hbm(flat), _hbm(lax.empty((N_DEV, r, c), flat.dtype)))


def _small_gather_wait(send_sems, recv_sems, flat, land, after, name):
    def body(flat_ref, land_ref, send_ref, recv_ref, after_ref, flat_out, land_out):
        me = _mesh_pos()
        for k in range(N_DEV - 1):
            cp = _remote(flat_ref, land_ref.at[0], send_ref.at[k], recv_ref.at[k], me)
            cp.wait_send()
            cp.wait_recv()

    return pl.pallas_call(
        body, name=name,
        in_specs=[HBM, HBM, SEM, SEM, UNREAD], out_specs=[HBM, HBM],
        out_shape=[pltpu.HBM(flat.shape, flat.dtype), pltpu.HBM(land.shape, land.dtype)],
        input_output_aliases={0: 0, 1: 1},
        compiler_params=pltpu.CompilerParams(has_side_effects=EFFECT),
    )(flat, land, send_sems, recv_sems, _hbm(after))


def _sum_devices(land, own):
    _, r, c = land.shape

    def body(land_ref, own_ref, out_ref):
        me = _dev_index(_mesh_pos())
        total = None
        for d in range(N_DEV):
            other = land_ref[jnp.where(d == me, (d + 1) % N_DEV, d)]
            block = jnp.where(d == me, own_ref[...], other)
            total = block if total is None else total + block
        out_ref[...] = total

    return pl.pallas_call(
        body, name="sum_devices",
        grid=(1,),
        in_specs=[pl.BlockSpec((N_DEV, r, c), lambda i: (0, 0, 0)), pl.BlockSpec((r, c), lambda i: (0, 0))],
        out_specs=pl.BlockSpec((r, c), lambda i: (0, 0)),
        out_shape=jax.ShapeDtypeStruct((r, c), F32),
        compiler_params=_cparams(("arbitrary",)),
    )(land, own)


def _rs_sibling_start(fulls, segs, name):
    ns = len(segs)
    offs = _seg_offsets(segs)
    rtot = sum(n * r for n, r in segs)
    c = fulls[0].shape[-1]
    dt = fulls[0].dtype

    def body(*refs):
        srcs = refs[:ns]
        land_ref, send_sem, recv_sem = refs[ns], refs[ns + 1], refs[ns + 2]
        token = refs[-1]
        x, y, cc = _mesh_pos()
        for k in range(4):
            for a, (n, r) in enumerate(segs):
                for m in range(n):
                    theirs = srcs[a].at[m, pl.ds(pl.multiple_of((2 * k + 1 - cc) * r, r), r), :]
                    _remote(theirs, land_ref.at[k, pl.ds(offs[a] + m * r, r), :], send_sem, recv_sem,
                            (x, y, 1 - cc)).start()
        token[...] = jnp.zeros_like(token)

    outs = pl.pallas_call(
        body, name=name,
        in_specs=[HBM] * (ns + 1),
        out_specs=[SEM, SEM] + [HBM] * (ns + 1) + [VMEM_WHOLE],
        out_shape=[pltpu.SemaphoreType.DMA(()), pltpu.SemaphoreType.DMA(())]
        + [pltpu.HBM(a.shape, a.dtype) for a in fulls] + [pltpu.HBM((4, rtot, c), dt),
                                                           jax.ShapeDtypeStruct((SUBLANES, LANES), F32)],
        input_output_aliases={i: 2 + i for i in range(ns + 1)},
        compiler_params=pltpu.CompilerParams(has_side_effects=EFFECT),
    )(*[_hbm(a) for a in fulls], _hbm(lax.empty((4, rtot, c), dt)))
    return outs[0], outs[1], list(outs[2:2 + ns]), outs[2 + ns], outs[-1]


def _rs_sibling_wait(send_sem, recv_sem, fulls, land, after, name):
    ns = len(fulls)

    def body(*refs):
        land_ref, send_ref, recv_ref = refs[ns], refs[ns + 1], refs[ns + 2]
        whole = _remote(land_ref, land_ref, send_ref, recv_ref, _mesh_pos())
        whole.wait_send()
        whole.wait_recv()

    outs = pl.pallas_call(
        body, name=name,
        in_specs=[HBM] * (ns + 1) + [SEM, SEM, UNREAD], out_specs=[HBM] * (ns + 1),
        out_shape=[pltpu.HBM(a.shape, a.dtype) for a in fulls] + [pltpu.HBM(land.shape, land.dtype)],
        input_output_aliases={i: i for i in range(ns + 1)},
        compiler_params=pltpu.CompilerParams(has_side_effects=EFFECT),
    )(*fulls, land, send_sem, recv_sem, _hbm(after))
    return list(outs[:ns]), outs[ns]


def _tp(w):
    return jnp.swapaxes(w, -1, -2)


def _s5_prepare(a_re, a_im, log_dt, b_re, b_im, c_re, c_im):
    a = jnp.stack([a_re, a_im], axis=1)
    ldt = jnp.broadcast_to(log_dt[:, :, None], (DEPTH, SSM_GROUPS, SSM_STATE))
    a_row = a.reshape(DEPTH, 2, 1, N_STATE)
    ldt_row = ldt.reshape(DEPTH, 1, N_STATE)
    a_rep = jnp.repeat(a, SSM_GROUP, axis=2)
    ldt_rep = jnp.repeat(ldt, SSM_GROUP, axis=1)
    bt = jnp.stack([_tp(b_re), _tp(b_im)], axis=1).reshape(DEPTH, 2, SSM_W, SSM_STATE)
    ct = jnp.stack([c_re, c_im], axis=1).reshape(DEPTH, 2, SSM_W, SSM_STATE)
    tile_e = jnp.tile(jnp.eye(SSM_STATE, dtype=BF16), (1, SSM_GROUPS))
    mask = jnp.repeat(jnp.repeat(jnp.eye(SSM_GROUPS, dtype=BF16), SSM_GROUP, axis=0), SSM_STATE, axis=1)
    out = []
    for l in range(DEPTH):
        tabs = _s5_disc(a_row[l], ldt_row[l], a_rep[l], ldt_rep[l], bt[l], ct[l], tile_e, mask)
        out.append(((a[l], ldt[l], a_rep[l], ldt_rep[l], bt[l], mask), *tabs))
    return out


def _layer_fwd(h, p_l, small, big, arrive=None):
    saved = {'h0': h}
    if arrive is not None:
        arrive(0, h)
    h, saved['gu1'] = _ffn_fwd(h, small['ffn1_norm'], big['ff1'])
    saved['h1'] = h
    if arrive is not None:
        arrive(1, h)
    z = _inproj_fwd(h, small['mix_norm'], big['wint'])
    ya, ys, hs = _s5conv_fwd(z, small['conv_w'], small['conv_b'], small['bbmat'], small['ccmat'], small['dvec'],
                             small['ltab'])
    saved.update(z=z, ya=ya, ys=ys, hs=hs)
    h = _mix_out_fwd(h, ya, ys, big['glu'], small['glu_b'], small['conv_out_norm'], small['ssm_out_norm'], big['wout'])
    saved['h2'] = h
    if arrive is not None:
        arrive(2, h)
    h, saved['gu2'] = _ffn_fwd(h, small['ffn2_norm'], big['ff2'])
    saved['h3'] = h
    h, *saved['ple'] = _ple_fwd(h, small['ple_norm'], p_l, big['plg'], big['plpt'])
    return h, saved


def _ffn_bwd(h_in, g, dh, gu, w3):
    dh_in, dga, ud, dg = _ffn_bwd_act(h_in, g, dh, gu, w3)
    return dh_in, _matmul_tn(dga, ud, FF_BLOCK, BF16, "ffn_wgrad"), dg


def _layer_bwd_top(dh, small, big, saved, after=()):
    gs = {}
    u, pb, q, pp = saved['ple']
    dh, dq, dpp, gs['ple_norm'] = _ple_bwd(saved['h3'], small['ple_norm'], dh, q, pp, big['plg'], after)
    d_plg = _matmul_tn(u, dq, 256, BF16, "ple_gate_wgrad")
    d_plpt = _matmul_tn(dpp, pb, 256, BF16, "ple_proj_wgrad", to_kernel=False)
    dh, d_ff2, gs['ffn2_norm'] = _ffn_bwd(saved['h2'], small['ffn2_norm'], dh, saved['gu2'], big['ff2'])
    return dh, (gs, d_plg, d_plpt, d_ff2)


def _layer_bwd_rest(dh, top, small, big, saved, after=()):
    gs, d_plg, d_plpt, d_ff2 = top
    dya, dys, ycat, dhb, zg, dq, part = _mix_out_bwd(dh, saved['ya'], saved['ys'], big['glu'], small['glu_b'],
                                                     small['conv_out_norm'], small['ssm_out_norm'], big['wout'],
                                                     after)
    d_wout = _matmul_tn(ycat, dhb, 256, BF16, "w_out_wgrad")
    d_glu = _matmul_tn(zg, dq, 256, BF16, "glu_wgrad", to_kernel=False)
    dz, gadj, us, dyb, dl, dcw = _s5conv_bwd(saved['z'], saved['hs'], dya, dys, small['conv_w'], small['conv_b'],
                                             small['bbmat'], small['ccmat'], small['dvec'], small['ltab_rev'])
    d_bb = _block_wgrad(us, gadj, "s5_b_wgrad")
    d_cc = _block_wgrad(dyb, saved['hs'][None], "s5_c_wgrad")
    dh, u, gs['mix_norm'] = _inproj_bwd(saved['h1'], small['mix_norm'], dh, dz, big['wint'])
    d_wint = _matmul_tn(dz[None], u, 256, BF16, "w_in_wgrad")
    dh, d_ff1, gs['ffn1_norm'] = _ffn_bwd(saved['h0'], small['ffn1_norm'], dh, saved['gu1'], big['ff1'])

    dlb = dl[0].reshape(2, SSM_GROUPS, SSM_STATE)
    fold = jnp.tile(jnp.eye(SSM_STATE, dtype=BF16), (SSM_GROUPS, 1))
    da, dldt, dbt, dct = _s5_disc_bwd(*small['disc_in'], dlb, d_bb, d_cc, fold)
    gs['ssm_A_re'], gs['ssm_A_im'] = da[0], da[1]
    gs['ssm_log_dt'] = dldt[:, 0]
    ghp = (SSM_GROUPS, SSM_GROUP, SSM_STATE)
    gs['ssm_B_re'], gs['ssm_B_im'] = dbt[0].reshape(ghp), dbt[1].reshape(ghp)
    gs['ssm_C_re'], gs['ssm_C_im'] = dct[0].reshape(ghp), dct[1].reshape(ghp)
    gs['conv_w'] = dcw[0:3]
    gs['conv_b'] = dcw[3]
    gs['ssm_D'] = dcw[4].reshape(SSM_GROUPS, SSM_GROUP)
    gs['conv_out_norm'], gs['ssm_out_norm'], gs['glu_b'] = part[0], part[1], part[2]
    for n in ('ple_norm', 'ffn2_norm', 'mix_norm', 'ffn1_norm'):
        gs[n] = gs[n][0]
    fulls = [d_ff1, d_ff2, d_wint, d_wout, d_plg,
             d_plpt.reshape(1, D_MODEL * PLE_DIM // D_MODEL, D_MODEL), d_glu.reshape(1, SSM_W * SSM_W // D_MODEL, D_MODEL)]
    return dh, fulls, gs


VIEW_T = ('ffn1_w_gate', 'ffn1_w_up', 'ffn2_w_gate', 'ffn2_w_up', 'ssm_B_re', 'ssm_B_im')


def _view(name, a):
    return _tp(a) if name in VIEW_T else a


SEG_NAMES = ('ff1', 'ff2', 'wint', 'wout', 'plg', 'plpt', 'glu')
FIRST_LAYER_GROUPS = ((0,), (2, 3, 6), (1, 4, 5))


def _layer_pack(W, l, segments=range(len(SEGS))):
    pieces = {
        0: lambda: [_tp(W['ffn1_w_gate'][l]), _tp(W['ffn1_w_up'][l]), W['ffn1_w_down'][l]],
        1: lambda: [_tp(W['ffn2_w_gate'][l]), _tp(W['ffn2_w_up'][l]), W['ffn2_w_down'][l]],
        2: lambda: [_tp(W['w_in'][l])],
        3: lambda: [W['w_out'][l]],
        4: lambda: [W['ple_w_gate'][l]],
        5: lambda: [_tp(W['ple_w_proj'][l]).reshape(-1, D_MODEL)],
        6: lambda: [W['glu_w'][l].reshape(-1, D_MODEL)],
    }
    return jnp.concatenate([a for s in segments for a in pieces[s]()], axis=0).astype(BF16)


def _as_big(named):
    shape = dict(plpt=(D_MODEL, PLE_DIM), glu=(SSM_W, SSM_W))
    return {n: (a.reshape(shape[n]) if n in shape else a) for n, a in named.items()}


def _pad_rows(flat, mult, width=LANES):
    per = mult * width
    n = flat.shape[0]
    tot = -(-n // per) * per
    return jnp.pad(flat, (0, tot - n)).reshape(tot // width, width)


def _adamw_any(names, w, g, m, v):
    two = lambda t: t.reshape(-1, t.shape[-1])
    groups = {}
    for n in names:
        groups.setdefault(two(w[n]).shape, []).append(n)
    out = ({}, {}, {})
    per_call = []
    for ns in groups.values():
        done = _adamw([(two(w[n]), two(g[n]), two(m[n]), two(v[n])) for n in ns])
        per_call.append(done[0][2])
        for n, three in zip(ns, done):
            for dst, t in zip(out, three):
                dst[n] = t.reshape(w[n].shape)
    return (*out, per_call)


def kernel(x, p, ffn1_norm, ffn1_w_gate, ffn1_w_up, ffn1_w_down, mix_norm, w_in, conv_w, conv_b, ssm_A_re, ssm_A_im, ssm_B_re, ssm_B_im, ssm_C_re, ssm_C_im, ssm_D, ssm_log_dt, glu_w, glu_b, conv_out_norm, ssm_out_norm, w_out, ffn2_norm, ffn2_w_gate, ffn2_w_up, ffn2_w_down, ple_norm, ple_w_gate, ple_w_proj, final_norm, loss_target, m_ffn1_norm, m_ffn1_w_gate, m_ffn1_w_up, m_ffn1_w_down, m_mix_norm, m_w_in, m_conv_w, m_conv_b, m_ssm_A_re, m_ssm_A_im, m_ssm_B_re, m_ssm_B_im, m_ssm_C_re, m_ssm_C_im, m_ssm_D, m_ssm_log_dt, m_glu_w, m_glu_b, m_conv_out_norm, m_ssm_out_norm, m_w_out, m_ffn2_norm, m_ffn2_w_gate, m_ffn2_w_up, m_ffn2_w_down, m_ple_norm, m_ple_w_gate, m_ple_w_proj, m_final_norm, v_ffn1_norm, v_ffn1_w_gate, v_ffn1_w_up, v_ffn1_w_down, v_mix_norm, v_w_in, v_conv_w, v_conv_b, v_ssm_A_re, v_ssm_A_im, v_ssm_B_re, v_ssm_B_im, v_ssm_C_re, v_ssm_C_im, v_ssm_D, v_ssm_log_dt, v_glu_w, v_glu_b, v_conv_out_norm, v_ssm_out_norm, v_w_out, v_ffn2_norm, v_ffn2_w_gate, v_ffn2_w_up, v_ffn2_w_down, v_ple_norm, v_ple_w_gate, v_ple_w_proj, v_final_norm):
    given = dict(locals())
    W = {n: given[n] for n in W_NAMES}
    M = {n: given['m_' + n] for n in W_NAMES}
    V = {n: given['v_' + n] for n in W_NAMES}
    Wv, Mv, Vv = [{n: _view(n, d[n]) for n in W_NAMES} for d in (W, M, V)]
    my_dev = _dev_index(_mesh_pos())

    conv_shard = _pad_rows(W['conv_w'].reshape(-1), SUBLANES)
    conv_all = _allgather(conv_shard, ((1, SUBLANES),), "ag_conv_w")[0]
    conv_full = conv_all.reshape(N_DEV, -1)[:, :DEPTH * 3 * (CONV_W // N_DEV)]
    conv_full = conv_full.reshape(N_DEV, DEPTH, 3, CONV_W // N_DEV).transpose(1, 2, 0, 3).reshape(DEPTH, 3, CONV_W)
    first, after = [], conv_all
    for gi, segments in enumerate(FIRST_LAYER_GROUPS):
        first.append(_ag_start(_layer_pack(W, 0, segments), tuple(SEGS[s] for s in segments), after,
                               "ag_start_0%s" % "abc"[gi]))
        after = first[-1][4]
    packs = [None] + [_layer_pack(W, l) for l in range(1, DEPTH)]
    flights = {1: _ag_start(packs[1], SEGS, after, "ag_start_1")}
    after = flights[1][4]
    s5 = _s5_prepare(*[W[n] + after[0, 0] for n in ('ssm_A_re', 'ssm_A_im', 'ssm_log_dt')],
                     *[W[n] for n in ('ssm_B_re', 'ssm_B_im', 'ssm_C_re', 'ssm_C_im')])
    prepared = conv_full[0, 0:1, 0:1] + s5[DEPTH - 1][1][0:1, 0:1] + packs[DEPTH - 1][0:1, 0:1].astype(F32)

    smalls, saves, bigs = [], [], []
    h = x[0]

    def gathered(handles, segments, after, name, next_layer=None, gate=None):
        send_sems, recv_sems, pack_thru, lands, _ = handles
        pack_thru, lands = _ag_wait(send_sems, recv_sems, pack_thru, lands, after, "ag_wait_" + name)
        if next_layer is not None:
            flights[next_layer] = _ag_start(packs[next_layer], SEGS, pack_thru, "ag_start_%d" % next_layer)
            gate[0][gate[1]] = gate[0][gate[1]] + flights[next_layer][4][0:1, 0:1]
        outs = _ag_finish(pack_thru, lands, tuple(SEGS[s] for s in segments))
        return _as_big({SEG_NAMES[s]: a for s, a in zip(segments, outs)})

    for l in range(DEPTH):
        small = {n: W[n][l][None] for n in ('ffn1_norm', 'mix_norm', 'conv_b', 'glu_b', 'conv_out_norm',
                                            'ssm_out_norm', 'ffn2_norm', 'ple_norm')}
        small['conv_w'] = conv_full[l]
        small['dvec'] = W['ssm_D'][l].reshape(1, SSM_W)
        small['disc_in'], small['ltab'], small['ltab_rev'], small['bbmat'], small['ccmat'] = s5[l]
        big = {}
        bigs.append(big)
        if l == 0:
            def arrive(stage, h_now, big=big, small=small):
                big.update(gathered(first[stage], FIRST_LAYER_GROUPS[stage], prepared if stage == 0 else h_now,
                                    "0%s" % "abc"[stage], *((2, (small, 'ffn2_norm')) if stage == 2 else ())))
            h, saved = _layer_fwd(h, p[l, 0], small, big, arrive)
        else:
            nxt = (l + 2, (small, 'ffn1_norm')) if l + 2 < DEPTH else ()
            big.update(gathered(flights[l], range(len(SEGS)), h, "%d" % l, *nxt))
            h, saved = _layer_fwd(h, p[l, 0], small, big)
        smalls.append(small)
        saves.append(saved)
    loss_tile, dh, d_final = _final_loss(h, W['final_norm'][None], loss_target[0])
    loss = lax.psum(loss_tile[0, 0], ("x", "y", "c"))

    layer_gs = [None] * DEPTH
    shard_grads = None
    sib, ici = None, None

    def finish_sibling(after_sib, after_ici):
        nonlocal sib, ici
        up, (send_sem, recv_sem, fulls_thru, land, _) = sib
        fulls_thru, got = _rs_sibling_wait(send_sem, recv_sem, fulls_thru, land, after_sib, "sib_wait_%d" % up)
        pbf = _pair_sum(fulls_thru, got, SEGS)
        done = finish_chips(after_ici)
        ici = (up, _rs_chips_start(pbf, after_ici if done is None else done, "rs_start_%d" % up), fulls_thru, got)
        sib = None

    def finish_chips(*after):
        nonlocal ici, shard_grads
        if ici is None:
            return None
        up, (send_sems, recv_sems, pbf_thru, land, _), fulls_up, got_up = ici
        got3 = _rs_chips_wait(send_sems, recv_sems, pbf_thru, land, after, "rs_wait_%d" % up)
        shard_grads = _chip_sum(fulls_up, got_up, got3, SEGS, up, shard_grads)
        ici = None
        return shard_grads

    layer_names = [n for n in SMALL_NAMES if n != 'final_norm']
    small_flights = [None] * DEPTH
    for l in reversed(range(DEPTH)):
        small = smalls[l]
        started = () if sib is None else (sib[1][4], small_flights[l + 1][4])
        dh, top = _layer_bwd_top(dh, small, bigs[l], saves[l], started)
        started = ()
        if sib is not None:
            finish_sibling(dh, dh)
            started = (ici[1][4],)
        dh, fulls, layer_gs[l] = _layer_bwd_rest(dh, top, small, bigs[l], saves[l], started)
        sib = (l, _rs_sibling_start(fulls, SEGS, "sib_start_%d" % l))
        last_slot = d_final[0] if l == DEPTH - 1 else jnp.zeros((D_MODEL,), F32)
        flat = jnp.concatenate([layer_gs[l][n].reshape(-1) for n in layer_names + ['conv_w']] + [last_slot])
        small_flights[l] = _small_gather_start(_pad_rows(flat, SUBLANES, D_MODEL), "small_start_%d" % l)
    grad_x = dh[None]
    finish_sibling(small_flights[0][4], small_flights[0][4])

    reduced = []
    for l in range(DEPTH):
        send_sems, recv_sems, flat_thru, land, _ = small_flights[l]
        flat_thru, land = _small_gather_wait(send_sems, recv_sems, flat_thru, land, ici[1][4], "small_wait_%d" % l)
        reduced.append(_sum_devices(land, flat_thru).reshape(-1))
    reduced = jnp.stack(reduced)
    G = {}
    o = 0
    for n in layer_names + ['conv_w']:
        size = (W[n].size if n != 'conv_w' else DEPTH * 3 * CONV_W) // DEPTH
        shape = Wv[n].shape if n != 'conv_w' else (DEPTH, 3, CONV_W)
        G[n] = reduced[:, o:o + size].reshape(shape)
        o += size
    G['final_norm'] = reduced[DEPTH - 1, o:o + D_MODEL]
    G['conv_w'] = lax.dynamic_slice_in_dim(G['conv_w'], my_dev * (CONV_W // N_DEV), CONV_W // N_DEV, axis=2)

    delta, new_m, new_v, small_updates = _adamw_any(SMALL_NAMES + ['conv_w'], Wv, G, Mv, Vv)

    offs = _seg_offsets(SEGS)
    r = SEGS[0][1]
    packed_rows = {'w_out': offs[3], 'ple_w_gate': offs[4]}
    for a, f in ((0, 'ffn1'), (1, 'ffn2')):
        packed_rows.update({f + '_w_gate': offs[a], f + '_w_up': offs[a] + r, f + '_w_down': offs[a] + 2 * r})

    def relaid(sg):
        nl = sg.shape[0]
        return {'w_in': _tp(sg[:, offs[2]:offs[2] + SEGS[2][1]]),
                'ple_w_proj': _tp(sg[:, offs[5]:offs[5] + SEGS[5][1]].reshape(nl, D_MODEL // N_DEV, PLE_DIM)),
                'glu_w': sg[:, offs[6]:offs[6] + SEGS[6][1]].reshape(nl, SSM_W // N_DEV, SSM_W)}

    groups = {}
    for n in list(packed_rows) + ['w_in', 'ple_w_proj', 'glu_w']:
        groups.setdefault(Wv[n].shape, []).append(n)

    def update(first, nl, prev):
        other = relaid(shard_grads[first:first + nl])
        sets = lambda ns: [(Wv[n], Mv[n], Vv[n], shard_grads, packed_rows[n]) if n in packed_rows
                           else (Wv[n], Mv[n], Vv[n], other[n], None) for n in ns]
        return {shape: _adamw_layers(sets(ns), first, nl, None if prev is None else prev[shape])
                for shape, ns in groups.items()}

    part = update(1, DEPTH - 1, None)
    finish_chips(sum(fours[0][3][1, 0:1, 0:1] for fours in part.values()), *small_updates)
    for shape, fours in update(0, 1, part).items():
        for n, four in zip(groups[shape], fours):
            G[n], delta[n], new_m[n], new_v[n] = four

    outs = [[_view(n, d[n]) for n in W_NAMES] for d in (G, delta, new_m, new_v)]
    return (loss, grad_x, *outs[0], *outs[1], *outs[2], *outs[3])
```
